```python
import math
import jax
import jax.numpy as jnp
from jax import lax
import numpy as np

D_MODEL = 1024
BATCH = 16
SEQ = 2048
DEPTH = 2

N_EVEN = (DEPTH + 1) // 2
N_ODD = DEPTH // 2
SB_HEADS = 8
SB_HEAD_DIM = 64
SB_WIDTH = SB_HEADS * SB_HEAD_DIM
QUERY_BLOCK = 128
POOL_WINDOWS = (2, 4, 8, 16)
POOL_WIDTH = D_MODEL - SB_WIDTH
POOL_GROUP = POOL_WIDTH // len(POOL_WINDOWS)
AB_IN_WIDTH = 3 * SB_WIDTH + POOL_WIDTH
SSM_WIDTH = D_MODEL
SSM_GROUP = 16
SSM_GROUPS = SSM_WIDTH // SSM_GROUP
SSM_STATE = 64
DT_MIN = 1e-3
DT_MAX = 1e-1
MEM_LEN = 256
XA_HEADS = 4
XA_HEAD_DIM = D_MODEL // XA_HEADS
D_FF = 2816
CONV_WIDTH = 3
EPS = 1e-6

kernel_name = "hybrid_stickbreak_pool_s5_block"


def rmsnorm(x, g):
    xf = x.astype(jnp.float32)
    xf = xf * lax.rsqrt(jnp.mean(xf * xf, axis=-1, keepdims=True) + EPS)
    return (xf * g.astype(jnp.float32)).astype(x.dtype)


def stick_breaking_attention(q, k, v):
    seq = q.shape[1]
    scale = q.shape[-1] ** -0.5
    outs = []
    for t0 in range(0, seq, QUERY_BLOCK):
        t1 = t0 + QUERY_BLOCK
        z = jnp.einsum('bqhd,bkhd->bhqk', q[:, t0:t1], k[:, :t1]).astype(jnp.float32) * scale
        causal = jnp.arange(t1)[None, :] < (t0 + jnp.arange(QUERY_BLOCK))[:, None]
        log_beta = jax.nn.log_sigmoid(z)
        log_keep = jnp.where(causal, log_beta - z, 0.0)
        after = lax.cumsum(log_keep, axis=3, reverse=True) - log_keep
        w = jnp.where(causal, jnp.exp(log_beta + after), 0.0)
        outs.append(jnp.einsum('bhqk,bkhd->bqhd', w.astype(v.dtype), v[:, :t1]))
    return jnp.concatenate(outs, axis=1)


def multiscale_pool(u, w_grp, scale):
    bsz, seq, _ = u.shape
    ug = u.astype(jnp.float32).reshape(bsz, seq, len(POOL_WINDOWS), POOL_GROUP)
    cs = jnp.concatenate([jnp.zeros_like(ug[:, :1]), jnp.cumsum(ug, axis=1)], axis=1)
    t = jnp.arange(seq)
    pooled = []
    for g, win in enumerate(POOL_WINDOWS):
        cs_g = cs[:, :, g]
        lo = jnp.maximum(t + 1 - win, 0)
        cnt = jnp.minimum(t + 1, win).astype(jnp.float32)[None, :, None]
        mean = (cs_g[:, 1:] - cs_g[:, lo]) / cnt
        pooled.append(mean - ug[:, :, g])
    p = jnp.stack(pooled, axis=2)
    y = jnp.einsum('bsgc,gcd->bsgd', p, w_grp.astype(jnp.float32)).reshape(bsz, seq, POOL_WIDTH)
    return (y * scale.astype(jnp.float32)).astype(u.dtype)


def _complex_linear_combine(left, right):
    a1r, a1i, b1r, b1i = left
    a2r, a2i, b2r, b2i = right
    ar = a1r * a2r - a1i * a2i
    ai = a1r * a2i + a1i * a2r
    br = a2r * b1r - a2i * b1i + b2r
    bi = a2r * b1i + a2i * b1r + b2i
    return (ar, ai, br, bi)


def s5_ssm(u, lam_re, lam_im, log_dt, b_re, b_im, c_re, c_im, d_skip):
    bsz, seq, _ = u.shape
    f32 = jnp.float32
    uf = u.astype(f32)
    ug = uf.reshape(bsz, seq, SSM_GROUPS, SSM_GROUP)
    lam_re = lam_re.astype(f32)
    lam_im = lam_im.astype(f32)
    dt = jnp.exp(log_dt.astype(f32))[:, None]
    mag = jnp.exp(lam_re * dt)
    ang = lam_im * dt
    lb_re = mag * jnp.cos(ang)
    lb_im = mag * jnp.sin(ang)
    n_re = lb_re - 1.0
    den = lam_re * lam_re + lam_im * lam_im
    coef_re = (n_re * lam_re + lb_im * lam_im) / den
    coef_im = (lb_im * lam_re - n_re * lam_im) / den
    b_re = b_re.astype(f32)
    b_im = b_im.astype(f32)
    bb_re = coef_re[..., None] * b_re - coef_im[..., None] * b_im
    bb_im = coef_re[..., None] * b_im + coef_im[..., None] * b_re
    bu_re = jnp.einsum('bsgc,gpc->bsgp', ug, bb_re)
    bu_im = jnp.einsum('bsgc,gpc->bsgp', ug, bb_im)
    a_re = jnp.broadcast_to(lb_re, (1, seq) + lb_re.shape)
    a_im = jnp.broadcast_to(lb_im, (1, seq) + lb_im.shape)
    _, _, h_re, h_im = lax.associative_scan(
        _complex_linear_combine, (a_re, a_im, bu_re, bu_im), axis=1)
    y = (jnp.einsum('bsgp,gcp->bsgc', h_re, c_re.astype(f32))
         - jnp.einsum('bsgp,gcp->bsgc', h_im, c_im.astype(f32)))
    return y.reshape(bsz, seq, SSM_WIDTH) + d_skip.astype(f32) * uf


def memory_cross_attention(h, mem_n, w_q, w_kv, w_o):
    bsz, seq, _ = h.shape
    m = mem_n.shape[1]
    q = (h @ w_q).reshape(bsz, seq, XA_HEADS, XA_HEAD_DIM)
    k, v = jnp.split(mem_n @ w_kv, 2, axis=-1)
    k = k.reshape(bsz, m, XA_HEADS, XA_HEAD_DIM)
    v = v.reshape(bsz, m, XA_HEADS, XA_HEAD_DIM)
    scores = jnp.einsum('bshd,bmhd->bhsm', q, k).astype(jnp.float32) * (XA_HEAD_DIM ** -0.5)
    p = jax.nn.softmax(scores, axis=-1).astype(v.dtype)
    o = jnp.einsum('bhsm,bmhd->bshd', p, v).reshape(bsz, seq, D_MODEL)
    return o @ w_o


def conv_gated_mlp(h, w_up, conv_w, conv_b, w_down):
    up = h @ w_up
    seq = up.shape[1]
    padded = jnp.pad(up, ((0, 0), (CONV_WIDTH - 1, 0), (0, 0)))
    conv = conv_b
    for i in range(CONV_WIDTH):
        conv = conv + conv_w[i] * padded[:, i:i + seq]
    val, gate = jnp.split(conv, 2, axis=-1)
    return (jax.nn.silu(gate) * val) @ w_down


def _fwd_setup_inputs(seed: int = 0) -> dict:
    key = jax.random.key(seed)
    ks = iter(jax.random.split(key, 40))

    def nrm(shape, fan_in):
        return jax.random.normal(next(ks), shape, jnp.float32) * (fan_in ** -0.5)

    def gain(shape):
        return 1.0 + 0.02 * jax.random.normal(next(ks), shape, jnp.float32)

    n_arange = jnp.arange(SSM_STATE, dtype=jnp.float32)
    return {
        "x": jax.random.normal(next(ks), (BATCH, SEQ, D_MODEL), jnp.float32),
        "mem": jax.random.normal(next(ks), (BATCH, MEM_LEN, D_MODEL), jnp.float32),
        "norm_mix": gain((DEPTH, D_MODEL)),
        "norm_xattn": gain((DEPTH, D_MODEL)),
        "norm_ffn": gain((DEPTH, D_MODEL)),
        "norm_mem": gain((D_MODEL,)),
        "norm_final": gain((D_MODEL,)),
        "ab_w_in": nrm((N_EVEN, D_MODEL, AB_IN_WIDTH), D_MODEL),
        "pool_w": nrm((N_EVEN, len(POOL_WINDOWS), POOL_GROUP, POOL_GROUP), POOL_GROUP),
        "pool_scale": gain((N_EVEN, POOL_WIDTH)),
        "ab_w_out": nrm((N_EVEN, SB_WIDTH + POOL_WIDTH, D_MODEL), SB_WIDTH + POOL_WIDTH),
        "ssm_w_in": nrm((N_ODD, D_MODEL, SSM_WIDTH), D_MODEL),
        "ssm_lam_re": -0.5 + 0.01 * jax.random.normal(next(ks), (N_ODD, SSM_GROUPS, SSM_STATE), jnp.float32),
        "ssm_lam_im": math.pi * n_arange + 0.01 * jax.random.normal(next(ks), (N_ODD, SSM_GROUPS, SSM_STATE), jnp.float32),
        "ssm_log_dt": jax.random.uniform(next(ks), (N_ODD, SSM_GROUPS), jnp.float32,
                                         math.log(DT_MIN), math.log(DT_MAX)),
        "ssm_b_re": nrm((N_ODD, SSM_GROUPS, SSM_STATE, SSM_GROUP), 2 * SSM_GROUP),
        "ssm_b_im": nrm((N_ODD, SSM_GROUPS, SSM_STATE, SSM_GROUP), 2 * SSM_GROUP),
        "ssm_c_re": nrm((N_ODD, SSM_GROUPS, SSM_GROUP, SSM_STATE), SSM_STATE),
        "ssm_c_im": nrm((N_ODD, SSM_GROUPS, SSM_GROUP, SSM_STATE), SSM_STATE),
        "ssm_d": jax.random.normal(next(ks), (N_ODD, SSM_WIDTH), jnp.float32),
        "ssm_w_glu": nrm((N_ODD, SSM_WIDTH, 2 * D_MODEL), SSM_WIDTH),
        "xa_w_q": nrm((DEPTH, D_MODEL, D_MODEL), D_MODEL),
        "xa_w_kv": nrm((DEPTH, D_MODEL, 2 * D_MODEL), D_MODEL),
        "xa_w_o": nrm((DEPTH, D_MODEL, D_MODEL), D_MODEL),
        "ffn_w_up": nrm((DEPTH, D_MODEL, 2 * D_FF), D_MODEL),
        "ffn_conv_w": nrm((DEPTH, CONV_WIDTH, 2 * D_FF), CONV_WIDTH),
        "ffn_conv_b": 0.01 * jax.random.normal(next(ks), (DEPTH, 2 * D_FF), jnp.float32),
        "ffn_w_down": nrm((DEPTH, D_FF, D_MODEL), D_FF),
    }


def _fwd_reference(x, mem, norm_mix, norm_xattn, norm_ffn, norm_mem, norm_final,
              ab_w_in, pool_w, pool_scale, ab_w_out,
              ssm_w_in, ssm_lam_re, ssm_lam_im, ssm_log_dt, ssm_b_re, ssm_b_im,
              ssm_c_re, ssm_c_im, ssm_d, ssm_w_glu,
              xa_w_q, xa_w_kv, xa_w_o,
              ffn_w_up, ffn_conv_w, ffn_conv_b, ffn_w_down):
    bsz, seq, _ = x.shape
    mem_n = rmsnorm(mem, norm_mem)
    for layer in range(DEPTH):
        h = rmsnorm(x, norm_mix[layer])
        if layer % 2 == 0:
            e = layer // 2
            proj = h @ ab_w_in[e]
            q, k, v, u = jnp.split(proj, [SB_WIDTH, 2 * SB_WIDTH, 3 * SB_WIDTH], axis=-1)
            q = q.reshape(bsz, seq, SB_HEADS, SB_HEAD_DIM)
            k = k.reshape(bsz, seq, SB_HEADS, SB_HEAD_DIM)
            v = v.reshape(bsz, seq, SB_HEADS, SB_HEAD_DIM)
            a_out = stick_breaking_attention(q, k, v).reshape(bsz, seq, SB_WIDTH)
            p_out = multiscale_pool(u, pool_w[e], pool_scale[e])
            mix = jnp.concatenate([a_out, p_out], axis=-1) @ ab_w_out[e]
        else:
            o = layer // 2
            u = h @ ssm_w_in[o]
            y = s5_ssm(u, ssm_lam_re[o], ssm_lam_im[o], ssm_log_dt[o], ssm_b_re[o],
                       ssm_b_im[o], ssm_c_re[o], ssm_c_im[o], ssm_d[o])
            glu = jax.nn.gelu(y).astype(x.dtype) @ ssm_w_glu[o]
            val, gate = jnp.split(glu, 2, axis=-1)
            mix = val * jax.nn.sigmoid(gate)
        x = x + mix
        x = x + memory_cross_attention(rmsnorm(x, norm_xattn[layer]), mem_n,
                                       xa_w_q[layer], xa_w_kv[layer], xa_w_o[layer])
        x = x + conv_gated_mlp(rmsnorm(x, norm_ffn[layer]), ffn_w_up[layer],
                               ffn_conv_w[layer], ffn_conv_b[layer], ffn_w_down[layer])
    return rmsnorm(x, norm_final)


import jax as _jax
import jax.numpy as _jnp

TWIN_FORMAT = 'train_step'
FWD_PARAMS = ['x', 'mem', 'norm_mix', 'norm_xattn', 'norm_ffn', 'norm_mem', 'norm_final', 'ab_w_in', 'pool_w', 'pool_scale', 'ab_w_out', 'ssm_w_in', 'ssm_lam_re', 'ssm_lam_im', 'ssm_log_dt', 'ssm_b_re', 'ssm_b_im', 'ssm_c_re', 'ssm_c_im', 'ssm_d', 'ssm_w_glu', 'xa_w_q', 'xa_w_kv', 'xa_w_o', 'ffn_w_up', 'ffn_conv_w', 'ffn_conv_b', 'ffn_w_down']
TWIN_WEIGHTS = ['norm_mix', 'norm_xattn', 'norm_ffn', 'norm_mem', 'norm_final', 'ab_w_in', 'pool_w', 'pool_scale', 'ab_w_out', 'ssm_w_in', 'ssm_lam_re', 'ssm_lam_im', 'ssm_log_dt', 'ssm_b_re', 'ssm_b_im', 'ssm_c_re', 'ssm_c_im', 'ssm_d', 'ssm_w_glu', 'xa_w_q', 'xa_w_kv', 'xa_w_o', 'ffn_w_up', 'ffn_conv_w', 'ffn_conv_b', 'ffn_w_down']
TWIN_DIFF_INPUT = 'x'
TWIN_INPUTS = ['x', 'mem', 'norm_mix', 'norm_xattn', 'norm_ffn', 'norm_mem', 'norm_final', 'ab_w_in', 'pool_w', 'pool_scale', 'ab_w_out', 'ssm_w_in', 'ssm_lam_re', 'ssm_lam_im', 'ssm_log_dt', 'ssm_b_re', 'ssm_b_im', 'ssm_c_re', 'ssm_c_im', 'ssm_d', 'ssm_w_glu', 'xa_w_q', 'xa_w_kv', 'xa_w_o', 'ffn_w_up', 'ffn_conv_w', 'ffn_conv_b', 'ffn_w_down', 'loss_target', 'm_norm_mix', 'm_norm_xattn', 'm_norm_ffn', 'm_norm_mem', 'm_norm_final', 'm_ab_w_in', 'm_pool_w', 'm_pool_scale', 'm_ab_w_out', 'm_ssm_w_in', 'm_ssm_lam_re', 'm_ssm_lam_im', 'm_ssm_log_dt', 'm_ssm_b_re', 'm_ssm_b_im', 'm_ssm_c_re', 'm_ssm_c_im', 'm_ssm_d', 'm_ssm_w_glu', 'm_xa_w_q', 'm_xa_w_kv', 'm_xa_w_o', 'm_ffn_w_up', 'm_ffn_conv_w', 'm_ffn_conv_b', 'm_ffn_w_down', 'v_norm_mix', 'v_norm_xattn', 'v_norm_ffn', 'v_norm_mem', 'v_norm_final', 'v_ab_w_in', 'v_pool_w', 'v_pool_scale', 'v_ab_w_out', 'v_ssm_w_in', 'v_ssm_lam_re', 'v_ssm_lam_im', 'v_ssm_log_dt', 'v_ssm_b_re', 'v_ssm_b_im', 'v_ssm_c_re', 'v_ssm_c_im', 'v_ssm_d', 'v_ssm_w_glu', 'v_xa_w_q', 'v_xa_w_kv', 'v_xa_w_o', 'v_ffn_w_up', 'v_ffn_conv_w', 'v_ffn_conv_b', 'v_ffn_w_down']
TWIN_OUTPUTS = ['loss', 'grad_x', 'grad_norm_mix', 'grad_norm_xattn', 'grad_norm_ffn', 'grad_norm_mem', 'grad_norm_final', 'grad_ab_w_in', 'grad_pool_w', 'grad_pool_scale', 'grad_ab_w_out', 'grad_ssm_w_in', 'grad_ssm_lam_re', 'grad_ssm_lam_im', 'grad_ssm_log_dt', 'grad_ssm_b_re', 'grad_ssm_b_im', 'grad_ssm_c_re', 'grad_ssm_c_im', 'grad_ssm_d', 'grad_ssm_w_glu', 'grad_xa_w_q', 'grad_xa_w_kv', 'grad_xa_w_o', 'grad_ffn_w_up', 'grad_ffn_conv_w', 'grad_ffn_conv_b', 'grad_ffn_w_down', 'delta_norm_mix', 'delta_norm_xattn', 'delta_norm_ffn', 'delta_norm_mem', 'delta_norm_final', 'delta_ab_w_in', 'delta_pool_w', 'delta_pool_scale', 'delta_ab_w_out', 'delta_ssm_w_in', 'delta_ssm_lam_re', 'delta_ssm_lam_im', 'delta_ssm_log_dt', 'delta_ssm_b_re', 'delta_ssm_b_im', 'delta_ssm_c_re', 'delta_ssm_c_im', 'delta_ssm_d', 'delta_ssm_w_glu', 'delta_xa_w_q', 'delta_xa_w_kv', 'delta_xa_w_o', 'delta_ffn_w_up', 'delta_ffn_conv_w', 'delta_ffn_conv_b', 'delta_ffn_w_down', 'new_m_norm_mix', 'new_m_norm_xattn', 'new_m_norm_ffn', 'new_m_norm_mem', 'new_m_norm_final', 'new_m_ab_w_in', 'new_m_pool_w', 'new_m_pool_scale', 'new_m_ab_w_out', 'new_m_ssm_w_in', 'new_m_ssm_lam_re', 'new_m_ssm_lam_im', 'new_m_ssm_log_dt', 'new_m_ssm_b_re', 'new_m_ssm_b_im', 'new_m_ssm_c_re', 'new_m_ssm_c_im', 'new_m_ssm_d', 'new_m_ssm_w_glu', 'new_m_xa_w_q', 'new_m_xa_w_kv', 'new_m_xa_w_o', 'new_m_ffn_w_up', 'new_m_ffn_conv_w', 'new_m_ffn_conv_b', 'new_m_ffn_w_down', 'new_v_norm_mix', 'new_v_norm_xattn', 'new_v_norm_ffn', 'new_v_norm_mem', 'new_v_norm_final', 'new_v_ab_w_in', 'new_v_pool_w', 'new_v_pool_scale', 'new_v_ab_w_out', 'new_v_ssm_w_in', 'new_v_ssm_lam_re', 'new_v_ssm_lam_im', 'new_v_ssm_log_dt', 'new_v_ssm_b_re', 'new_v_ssm_b_im', 'new_v_ssm_c_re', 'new_v_ssm_c_im', 'new_v_ssm_d', 'new_v_ssm_w_glu', 'new_v_xa_w_q', 'new_v_xa_w_kv', 'new_v_xa_w_o', 'new_v_ffn_w_up', 'new_v_ffn_conv_w', 'new_v_ffn_conv_b', 'new_v_ffn_w_down']
TWIN_LEAF_KINDS = {'loss': 'loss', 'grad_x': 'grad_x', 'grad_norm_mix': 'grad_w', 'grad_norm_xattn': 'grad_w', 'grad_norm_ffn': 'grad_w', 'grad_norm_mem': 'grad_w', 'grad_norm_final': 'grad_w', 'grad_ab_w_in': 'grad_w', 'grad_pool_w': 'grad_w', 'grad_pool_scale': 'grad_w', 'grad_ab_w_out': 'grad_w', 'grad_ssm_w_in': 'grad_w', 'grad_ssm_lam_re': 'grad_w', 'grad_ssm_lam_im': 'grad_w', 'grad_ssm_log_dt': 'grad_w', 'grad_ssm_b_re': 'grad_w', 'grad_ssm_b_im': 'grad_w', 'grad_ssm_c_re': 'grad_w', 'grad_ssm_c_im': 'grad_w', 'grad_ssm_d': 'grad_w', 'grad_ssm_w_glu': 'grad_w', 'grad_xa_w_q': 'grad_w', 'grad_xa_w_kv': 'grad_w', 'grad_xa_w_o': 'grad_w', 'grad_ffn_w_up': 'grad_w', 'grad_ffn_conv_w': 'grad_w', 'grad_ffn_conv_b': 'grad_w', 'grad_ffn_w_down': 'grad_w', 'delta_norm_mix': 'delta_w', 'delta_norm_xattn': 'delta_w', 'delta_norm_ffn': 'delta_w', 'delta_norm_mem': 'delta_w', 'delta_norm_final': 'delta_w', 'delta_ab_w_in': 'delta_w', 'delta_pool_w': 'delta_w', 'delta_pool_scale': 'delta_w', 'delta_ab_w_out': 'delta_w', 'delta_ssm_w_in': 'delta_w', 'delta_ssm_lam_re': 'delta_w', 'delta_ssm_lam_im': 'delta_w', 'delta_ssm_log_dt': 'delta_w', 'delta_ssm_b_re': 'delta_w', 'delta_ssm_b_im': 'delta_w', 'delta_ssm_c_re': 'delta_w', 'delta_ssm_c_im': 'delta_w', 'delta_ssm_d': 'delta_w', 'delta_ssm_w_glu': 'delta_w', 'delta_xa_w_q': 'delta_w', 'delta_xa_w_kv': 'delta_w', 'delta_xa_w_o': 'delta_w', 'delta_ffn_w_up': 'delta_w', 'delta_ffn_conv_w': 'delta_w', 'delta_ffn_conv_b': 'delta_w', 'delta_ffn_w_down': 'delta_w', 'new_m_norm_mix': 'new_m', 'new_m_norm_xattn': 'new_m', 'new_m_norm_ffn': 'new_m', 'new_m_norm_mem': 'new_m', 'new_m_norm_final': 'new_m', 'new_m_ab_w_in': 'new_m', 'new_m_pool_w': 'new_m', 'new_m_pool_scale': 'new_m', 'new_m_ab_w_out': 'new_m', 'new_m_ssm_w_in': 'new_m', 'new_m_ssm_lam_re': 'new_m', 'new_m_ssm_lam_im': 'new_m', 'new_m_ssm_log_dt': 'new_m', 'new_m_ssm_b_re': 'new_m', 'new_m_ssm_b_im': 'new_m', 'new_m_ssm_c_re': 'new_m', 'new_m_ssm_c_im': 'new_m', 'new_m_ssm_d': 'new_m', 'new_m_ssm_w_glu': 'new_m', 'new_m_xa_w_q': 'new_m', 'new_m_xa_w_kv': 'new_m', 'new_m_xa_w_o': 'new_m', 'new_m_ffn_w_up': 'new_m', 'new_m_ffn_conv_w': 'new_m', 'new_m_ffn_conv_b': 'new_m', 'new_m_ffn_w_down': 'new_m', 'new_v_norm_mix': 'new_v', 'new_v_norm_xattn': 'new_v', 'new_v_norm_ffn': 'new_v', 'new_v_norm_mem': 'new_v', 'new_v_norm_final': 'new_v', 'new_v_ab_w_in': 'new_v', 'new_v_pool_w': 'new_v', 'new_v_pool_scale': 'new_v', 'new_v_ab_w_out': 'new_v', 'new_v_ssm_w_in': 'new_v', 'new_v_ssm_lam_re': 'new_v', 'new_v_ssm_lam_im': 'new_v', 'new_v_ssm_log_dt': 'new_v', 'new_v_ssm_b_re': 'new_v', 'new_v_ssm_b_im': 'new_v', 'new_v_ssm_c_re': 'new_v', 'new_v_ssm_c_im': 'new_v', 'new_v_ssm_d': 'new_v', 'new_v_ssm_w_glu': 'new_v', 'new_v_xa_w_q': 'new_v', 'new_v_xa_w_kv': 'new_v', 'new_v_xa_w_o': 'new_v', 'new_v_ffn_w_up': 'new_v', 'new_v_ffn_conv_w': 'new_v', 'new_v_ffn_conv_b': 'new_v', 'new_v_ffn_w_down': 'new_v'}


def _forward(args):
    return _fwd_reference(*[args[k] for k in FWD_PARAMS])


def _output_shape():
    out = _jax.eval_shape(lambda: _forward(_fwd_setup_inputs(0)))
    return out.shape, out.dtype

N_MICROBATCH = 1
ADAM_LR = 0.001
ADAM_B1 = 0.9
ADAM_B2 = 0.999
ADAM_EPS = 1e-08
ADAM_WD = 0.01
ADAM_STEP = 10
PER_EXAMPLE_BATCH_AXIS = {'x': 0, 'mem': 0, 'loss_target': 0}
SHARED_INPUTS = []
_WEIGHT_DTYPES = {'norm_mix': _jnp.float32, 'norm_xattn': _jnp.float32, 'norm_ffn': _jnp.float32, 'norm_mem': _jnp.float32, 'norm_final': _jnp.float32, 'ab_w_in': _jnp.float32, 'pool_w': _jnp.float32, 'pool_scale': _jnp.float32, 'ab_w_out': _jnp.float32, 'ssm_w_in': _jnp.float32, 'ssm_lam_re': _jnp.float32, 'ssm_lam_im': _jnp.float32, 'ssm_log_dt': _jnp.float32, 'ssm_b_re': _jnp.float32, 'ssm_b_im': _jnp.float32, 'ssm_c_re': _jnp.float32, 'ssm_c_im': _jnp.float32, 'ssm_d': _jnp.float32, 'ssm_w_glu': _jnp.float32, 'xa_w_q': _jnp.float32, 'xa_w_kv': _jnp.float32, 'xa_w_o': _jnp.float32, 'ffn_w_up': _jnp.float32, 'ffn_conv_w': _jnp.float32, 'ffn_conv_b': _jnp.float32, 'ffn_w_down': _jnp.float32}
MOMENT_SCALE = {'norm_mix': 1.047043e-01, 'norm_xattn': 1.631145e-02, 'norm_ffn': 1.108837e-01, 'norm_mem': 3.476377e-02, 'norm_final': 3.202829e+01, 'ab_w_in': 9.749091e-02, 'pool_w': 1.482098e-01, 'pool_scale': 1.533344e-01, 'ab_w_out': 1.278624e-01, 'ssm_w_in': 4.922716e-02, 'ssm_lam_re': 4.107979e-03, 'ssm_lam_im': 4.559974e-03, 'ssm_log_dt': 3.068303e+00, 'ssm_b_re': 2.376380e-03, 'ssm_b_im': 2.431254e-03, 'ssm_c_re': 3.482770e-03, 'ssm_c_im': 3.504742e-03, 'ssm_d': 5.348340e-02, 'ssm_w_glu': 3.510460e-02, 'xa_w_q': 1.565830e-02, 'xa_w_kv': 1.585419e-02, 'xa_w_o': 1.614060e-02, 'ffn_w_up': 4.667446e-02, 'ffn_conv_w': 4.699019e-02, 'ffn_conv_b': 4.709293e-02, 'ffn_w_down': 7.602837e-02}


def _to_microbatches(a, axis):
    t = _jnp.moveaxis(a, axis, 0)
    t = t.reshape((N_MICROBATCH, t.shape[0] // N_MICROBATCH) + t.shape[1:])
    return _jnp.moveaxis(t, 1, axis + 1)


def setup_inputs(seed: int = 0) -> dict:
    inp = _fwd_setup_inputs(seed)
    key = _jax.random.fold_in(_jax.random.key(seed), 7919)
    shape, _ = _output_shape()
    out = dict(inp)
    out["loss_target"] = _jax.random.normal(_jax.random.fold_in(key, 0), shape, _jnp.float32)
    for i, name in enumerate(TWIN_WEIGHTS):
        w = inp[name].astype(_jnp.float32)
        if MOMENT_SCALE is None:
            s = _jnp.sqrt(_jnp.mean(_jnp.square(w)) + 1e-30)
        else:
            s = MOMENT_SCALE[name]
        km, kv = _jax.random.split(_jax.random.fold_in(key, i + 1))
        out[name] = w
        out["m_" + name] = s * _jax.random.normal(km, w.shape, _jnp.float32)
        out["v_" + name] = (s * s) * _jax.random.uniform(kv, w.shape, _jnp.float32, 0.5, 1.5)
    if N_MICROBATCH > 1:
        for name, axis in PER_EXAMPLE_BATCH_AXIS.items():
            out[name] = _to_microbatches(out[name], axis)
    return {'x': out['x'], 'mem': out['mem'], 'norm_mix': out['norm_mix'], 'norm_xattn': out['norm_xattn'], 'norm_ffn': out['norm_ffn'], 'norm_mem': out['norm_mem'], 'norm_final': out['norm_final'], 'ab_w_in': out['ab_w_in'], 'pool_w': out['pool_w'], 'pool_scale': out['pool_scale'], 'ab_w_out': out['ab_w_out'], 'ssm_w_in': out['ssm_w_in'], 'ssm_lam_re': out['ssm_lam_re'], 'ssm_lam_im': out['ssm_lam_im'], 'ssm_log_dt': out['ssm_log_dt'], 'ssm_b_re': out['ssm_b_re'], 'ssm_b_im': out['ssm_b_im'], 'ssm_c_re': out['ssm_c_re'], 'ssm_c_im': out['ssm_c_im'], 'ssm_d': out['ssm_d'], 'ssm_w_glu': out['ssm_w_glu'], 'xa_w_q': out['xa_w_q'], 'xa_w_kv': out['xa_w_kv'], 'xa_w_o': out['xa_w_o'], 'ffn_w_up': out['ffn_w_up'], 'ffn_conv_w': out['ffn_conv_w'], 'ffn_conv_b': out['ffn_conv_b'], 'ffn_w_down': out['ffn_w_down'], 'loss_target': out['loss_target'], 'm_norm_mix': out['m_norm_mix'], 'm_norm_xattn': out['m_norm_xattn'], 'm_norm_ffn': out['m_norm_ffn'], 'm_norm_mem': out['m_norm_mem'], 'm_norm_final': out['m_norm_final'], 'm_ab_w_in': out['m_ab_w_in'], 'm_pool_w': out['m_pool_w'], 'm_pool_scale': out['m_pool_scale'], 'm_ab_w_out': out['m_ab_w_out'], 'm_ssm_w_in': out['m_ssm_w_in'], 'm_ssm_lam_re': out['m_ssm_lam_re'], 'm_ssm_lam_im': out['m_ssm_lam_im'], 'm_ssm_log_dt': out['m_ssm_log_dt'], 'm_ssm_b_re': out['m_ssm_b_re'], 'm_ssm_b_im': out['m_ssm_b_im'], 'm_ssm_c_re': out['m_ssm_c_re'], 'm_ssm_c_im': out['m_ssm_c_im'], 'm_ssm_d': out['m_ssm_d'], 'm_ssm_w_glu': out['m_ssm_w_glu'], 'm_xa_w_q': out['m_xa_w_q'], 'm_xa_w_kv': out['m_xa_w_kv'], 'm_xa_w_o': out['m_xa_w_o'], 'm_ffn_w_up': out['m_ffn_w_up'], 'm_ffn_conv_w': out['m_ffn_conv_w'], 'm_ffn_conv_b': out['m_ffn_conv_b'], 'm_ffn_w_down': out['m_ffn_w_down'], 'v_norm_mix': out['v_norm_mix'], 'v_norm_xattn': out['v_norm_xattn'], 'v_norm_ffn': out['v_norm_ffn'], 'v_norm_mem': out['v_norm_mem'], 'v_norm_final': out['v_norm_final'], 'v_ab_w_in': out['v_ab_w_in'], 'v_pool_w': out['v_pool_w'], 'v_pool_scale': out['v_pool_scale'], 'v_ab_w_out': out['v_ab_w_out'], 'v_ssm_w_in': out['v_ssm_w_in'], 'v_ssm_lam_re': out['v_ssm_lam_re'], 'v_ssm_lam_im': out['v_ssm_lam_im'], 'v_ssm_log_dt': out['v_ssm_log_dt'], 'v_ssm_b_re': out['v_ssm_b_re'], 'v_ssm_b_im': out['v_ssm_b_im'], 'v_ssm_c_re': out['v_ssm_c_re'], 'v_ssm_c_im': out['v_ssm_c_im'], 'v_ssm_d': out['v_ssm_d'], 'v_ssm_w_glu': out['v_ssm_w_glu'], 'v_xa_w_q': out['v_xa_w_q'], 'v_xa_w_kv': out['v_xa_w_kv'], 'v_xa_w_o': out['v_xa_w_o'], 'v_ffn_w_up': out['v_ffn_w_up'], 'v_ffn_conv_w': out['v_ffn_conv_w'], 'v_ffn_conv_b': out['v_ffn_conv_b'], 'v_ffn_w_down': out['v_ffn_w_down']}


def _loss(weights, diff, rest, loss_target):
    with _jax.named_scope("forward"):
        args = {**rest, TWIN_DIFF_INPUT: diff, **{k: w.astype(_WEIGHT_DTYPES[k]) for k, w in weights.items()}}
        y = _forward(args)
    with _jax.named_scope("loss_head"):
        err = _jnp.square(y.astype(_jnp.float32) - loss_target)
        return 0.5 * _jnp.sum(_jnp.mean(err, axis=-1)) if err.ndim else 0.5 * err


def _adamw(w, g, m, v):
    m = ADAM_B1 * m + (1.0 - ADAM_B1) * g
    v = ADAM_B2 * v + (1.0 - ADAM_B2) * _jnp.square(g)
    m_hat = m / (1.0 - ADAM_B1 ** ADAM_STEP)
    v_hat = v / (1.0 - ADAM_B2 ** ADAM_STEP)
    delta = -ADAM_LR * (m_hat / (_jnp.sqrt(v_hat) + ADAM_EPS) + ADAM_WD * w)
    return delta, m, v


def reference(x, mem, norm_mix, norm_xattn, norm_ffn, norm_mem, norm_final, ab_w_in, pool_w, pool_scale, ab_w_out, ssm_w_in, ssm_lam_re, ssm_lam_im, ssm_log_dt, ssm_b_re, ssm_b_im, ssm_c_re, ssm_c_im, ssm_d, ssm_w_glu, xa_w_q, xa_w_kv, xa_w_o, ffn_w_up, ffn_conv_w, ffn_conv_b, ffn_w_down, loss_target, m_norm_mix, m_norm_xattn, m_norm_ffn, m_norm_mem, m_norm_final, m_ab_w_in, m_pool_w, m_pool_scale, m_ab_w_out, m_ssm_w_in, m_ssm_lam_re, m_ssm_lam_im, m_ssm_log_dt, m_ssm_b_re, m_ssm_b_im, m_ssm_c_re, m_ssm_c_im, m_ssm_d, m_ssm_w_glu, m_xa_w_q, m_xa_w_kv, m_xa_w_o, m_ffn_w_up, m_ffn_conv_w, m_ffn_conv_b, m_ffn_w_down, v_norm_mix, v_norm_xattn, v_norm_ffn, v_norm_mem, v_norm_final, v_ab_w_in, v_pool_w, v_pool_scale, v_ab_w_out, v_ssm_w_in, v_ssm_lam_re, v_ssm_lam_im, v_ssm_log_dt, v_ssm_b_re, v_ssm_b_im, v_ssm_c_re, v_ssm_c_im, v_ssm_d, v_ssm_w_glu, v_xa_w_q, v_xa_w_kv, v_xa_w_o, v_ffn_w_up, v_ffn_conv_w, v_ffn_conv_b, v_ffn_w_down):
    given = dict(x=x, mem=mem, norm_mix=norm_mix, norm_xattn=norm_xattn, norm_ffn=norm_ffn, norm_mem=norm_mem, norm_final=norm_final, ab_w_in=ab_w_in, pool_w=pool_w, pool_scale=pool_scale, ab_w_out=ab_w_out, ssm_w_in=ssm_w_in, ssm_lam_re=ssm_lam_re, ssm_lam_im=ssm_lam_im, ssm_log_dt=ssm_log_dt, ssm_b_re=ssm_b_re, ssm_b_im=ssm_b_im, ssm_c_re=ssm_c_re, ssm_c_im=ssm_c_im, ssm_d=ssm_d, ssm_w_glu=ssm_w_glu, xa_w_q=xa_w_q, xa_w_kv=xa_w_kv, xa_w_o=xa_w_o, ffn_w_up=ffn_w_up, ffn_conv_w=ffn_conv_w, ffn_conv_b=ffn_conv_b, ffn_w_down=ffn_w_down, loss_target=loss_target, m_norm_mix=m_norm_mix, m_norm_xattn=m_norm_xattn, m_norm_ffn=m_norm_ffn, m_norm_mem=m_norm_mem, m_norm_final=m_norm_final, m_ab_w_in=m_ab_w_in, m_pool_w=m_pool_w, m_pool_scale=m_pool_scale, m_ab_w_out=m_ab_w_out, m_ssm_w_in=m_ssm_w_in, m_ssm_lam_re=m_ssm_lam_re, m_ssm_lam_im=m_ssm_lam_im, m_ssm_log_dt=m_ssm_log_dt, m_ssm_b_re=m_ssm_b_re, m_ssm_b_im=m_ssm_b_im, m_ssm_c_re=m_ssm_c_re, m_ssm_c_im=m_ssm_c_im, m_ssm_d=m_ssm_d, m_ssm_w_glu=m_ssm_w_glu, m_xa_w_q=m_xa_w_q, m_xa_w_kv=m_xa_w_kv, m_xa_w_o=m_xa_w_o, m_ffn_w_up=m_ffn_w_up, m_ffn_conv_w=m_ffn_conv_w, m_ffn_conv_b=m_ffn_conv_b, m_ffn_w_down=m_ffn_w_down, v_norm_mix=v_norm_mix, v_norm_xattn=v_norm_xattn, v_norm_ffn=v_norm_ffn, v_norm_mem=v_norm_mem, v_norm_final=v_norm_final, v_ab_w_in=v_ab_w_in, v_pool_w=v_pool_w, v_pool_scale=v_pool_scale, v_ab_w_out=v_ab_w_out, v_ssm_w_in=v_ssm_w_in, v_ssm_lam_re=v_ssm_lam_re, v_ssm_lam_im=v_ssm_lam_im, v_ssm_log_dt=v_ssm_log_dt, v_ssm_b_re=v_ssm_b_re, v_ssm_b_im=v_ssm_b_im, v_ssm_c_re=v_ssm_c_re, v_ssm_c_im=v_ssm_c_im, v_ssm_d=v_ssm_d, v_ssm_w_glu=v_ssm_w_glu, v_xa_w_q=v_xa_w_q, v_xa_w_kv=v_xa_w_kv, v_xa_w_o=v_xa_w_o, v_ffn_w_up=v_ffn_w_up, v_ffn_conv_w=v_ffn_conv_w, v_ffn_conv_b=v_ffn_conv_b, v_ffn_w_down=v_ffn_w_down)
    weights = {n: given[n] for n in TWIN_WEIGHTS}
    shared = {n: given[n] for n in SHARED_INPUTS}
    per_example = {n: given[n] for n in ['x', 'mem']}
    grad_fn = _jax.value_and_grad(_loss, argnums=(0, 1))

    def one_microbatch(ex, loss_target):
        ex = dict(ex)
        diff = ex.pop(TWIN_DIFF_INPUT)
        return grad_fn(weights, diff, {**shared, **ex}, loss_target)

    if N_MICROBATCH == 1:
        loss, (grad_w, grad_x) = one_microbatch(per_example, given["loss_target"])
    else:
        def body(carry, xs):
            loss_sum, grad_sum = carry
            l_k, (gw_k, gx_k) = one_microbatch(xs[0], xs[1])
            with _jax.named_scope("update"):
                return (loss_sum + l_k, _jax.tree.map(_jnp.add, grad_sum, gw_k)), gx_k

        init = (_jnp.zeros((), _jnp.float32), _jax.tree.map(_jnp.zeros_like, weights))
        (loss, grad_w), grad_x = _jax.lax.scan(body, init, (per_example, given["loss_target"]))
    with _jax.named_scope("update"):
        delta_w, new_m, new_v = {}, {}, {}
        for n in TWIN_WEIGHTS:
            delta_w[n], new_m[n], new_v[n] = _adamw(weights[n], grad_w[n], given["m_" + n], given["v_" + n])
    return (loss, grad_x, *[grad_w[n] for n in TWIN_WEIGHTS], *[delta_w[n] for n in TWIN_WEIGHTS],
            *[new_m[n] for n in TWIN_WEIGHTS], *[new_v[n] for n in TWIN_WEIGHTS])
```

```python
import functools
import math

import jax
import jax.numpy as jnp
from jax import lax
from jax.experimental import pallas as pl
from jax.experimental.pallas import tpu as pltpu

F32 = jnp.float32
BF16 = jnp.bfloat16
MXU_DTYPE = jnp.bfloat16
N_DEV = 8
MESH_AXES = ("x", "y", "c")

D_MODEL = 1024
SB_HEAD_DIM = 64
SB_WIDTH = 512
POOL_WINDOWS = (2, 4, 8, 16)
POOL_GROUP = 128
POOL_HALO = 16
SSM_TILES = 8
SSM_TILE_STATES = 512
SSM_SLAB = 32
MEM_LEN = 256
XA_HEADS = 4
XA_HEAD_DIM = 256
D_FF = 2816
FF_SHARD = 704
EPS = 1e-6
ADAM_LR = 0.001
ADAM_B1 = 0.9
ADAM_B2 = 0.999
ADAM_EPS = 1e-08
ADAM_WD = 0.01
ADAM_STEP = 10
VMEM_LIMIT = 56 * 1024 * 1024

_NN = (((1,), (0,)), ((), ()))
_NT = (((1,), (1,)), ((), ()))
_TN = (((0,), (0,)), ((), ()))


def _params(sem=None):
    if sem is None:
        return pltpu.CompilerParams(vmem_limit_bytes=VMEM_LIMIT)
    return pltpu.CompilerParams(dimension_semantics=sem, vmem_limit_bytes=VMEM_LIMIT)


def _tile(n, pref, mult=8):
    if n <= pref:
        return n
    for t in range(pref, 0, -1):
        if n % t == 0 and t % mult == 0:
            return t
    return n


def _dot(a, b, dims):
    return lax.dot_general(a.astype(MXU_DTYPE), b.astype(MXU_DTYPE), dims, preferred_element_type=F32)


def _dot_exact01(x, m01, dims=_NN):
    x1 = x.astype(BF16)
    r1 = x - x1.astype(F32)
    x2 = r1.astype(BF16)
    x3 = (r1 - x2.astype(F32)).astype(BF16)
    m = m01.astype(BF16)
    out = lax.dot_general(x1, m, dims, preferred_element_type=F32)
    out = out + lax.dot_general(x2, m, dims, preferred_element_type=F32)
    return out + lax.dot_general(x3, m, dims, preferred_element_type=F32)


def _mm(name, a, b, dims, grid, a_spec, b_spec, o_spec, out_shape, out_dtype, acc_shape, res=None, r_spec=None):
    nk = grid[2]
    if out_dtype is None:
        out_dtype = BF16

    def body(*refs):
        if res is None:
            a_ref, b_ref, o_ref, acc = refs
            r_ref = None
        else:
            a_ref, b_ref, r_ref, o_ref, acc = refs
        k = pl.program_id(2)

        @pl.when(k == 0)
        def _():
            acc[...] = jnp.zeros_like(acc)

        acc[...] += _dot(a_ref[...], b_ref[...], dims)

        @pl.when(k == nk - 1)
        def _():
            r = acc[...]
            if r_ref is not None:
                r = r + r_ref[...].astype(F32)
            o_ref[...] = r.astype(out_dtype)

    in_specs = [a_spec, b_spec] + ([] if res is None else [r_spec])
    args = (a, b) + (() if res is None else (res,))
    return pl.pallas_call(
        body, name=name, grid=grid, in_specs=in_specs, out_specs=o_spec,
        out_shape=jax.ShapeDtypeStruct(out_shape, out_dtype),
        scratch_shapes=[pltpu.VMEM(acc_shape, F32)],
        compiler_params=_params(("parallel", "parallel", "arbitrary")),
    )(*args)


def mm_nn(name, a, b, res=None, koff=0, out_dtype=None):
    m, k = a.shape
    n = b.shape[1]
    tm, tn, tk = _tile(m, 512), _tile(n, 512, 128), _tile(k, 1024, 128)
    kb = koff // tk
    spec = pl.BlockSpec((tm, tn), lambda i, j, kk: (i, j))
    return _mm(name, a, b, _NN, (m // tm, n // tn, k // tk),
               pl.BlockSpec((tm, tk), lambda i, j, kk: (i, kk)),
               pl.BlockSpec((tk, tn), lambda i, j, kk: (kk + kb, j)),
               spec, (m, n), out_dtype, (tm, tn), res, spec)


def mm_nn_bs(name, a, bs, stacked_out=False, out_dtype=None):
    m, k = a.shape
    s, _, n = bs.shape
    tm, tk = _tile(m, 512), _tile(k, 1024, 128)
    if stacked_out:
        o_spec, o_shape = pl.BlockSpec((None, tm, n), lambda i, j, kk: (j, i, 0)), (s, m, n)
    else:
        o_spec, o_shape = pl.BlockSpec((tm, n), lambda i, j, kk: (i, j)), (m, s * n)
    return _mm(name, a, bs, _NN, (m // tm, s, k // tk),
               pl.BlockSpec((tm, tk), lambda i, j, kk: (i, kk)),
               pl.BlockSpec((None, tk, n), lambda i, j, kk: (j, kk, 0)),
               o_spec, o_shape, out_dtype, (tm, n))


def mm_as_nn(name, a_st, b3, res, out_dtype=F32):
    s, m, kp = a_st.shape
    n = b3.shape[2]
    tm, tn = _tile(m, 512), _tile(n, 512, 128)
    spec = pl.BlockSpec((tm, tn), lambda i, j, kk: (i, j))
    return _mm(name, a_st, b3, _NN, (m // tm, n // tn, s),
               pl.BlockSpec((None, tm, kp), lambda i, j, kk: (kk, i, 0)),
               pl.BlockSpec((None, kp, tn), lambda i, j, kk: (kk, 0, j)),
               spec, (m, n), out_dtype, (tm, tn), res, spec)


def mm_nt(name, dc, b, out_dtype=None):
    m, n = dc.shape
    k = b.shape[0]
    tm, tko, tnr = _tile(m, 512), _tile(k, 512, 128), _tile(n, 1024, 128)
    return _mm(name, dc, b, _NT, (m // tm, k // tko, n // tnr),
               pl.BlockSpec((tm, tnr), lambda i, j, kk: (i, kk)),
               pl.BlockSpec((tko, tnr), lambda i, j, kk: (j, kk)),
               pl.BlockSpec((tm, tko), lambda i, j, kk: (i, j)), (m, k), out_dtype, (tm, tko))


def mm_nt_bs(name, dc, bs, dc_stacked=False, out_dtype=None):
    s, k, n = bs.shape
    m = dc.shape[1] if dc_stacked else dc.shape[0]
    tm, tko = _tile(m, 512), _tile(k, 512, 128)
    if dc_stacked:
        a_spec = pl.BlockSpec((None, tm, n), lambda i, j, kk: (kk, i, 0))
    else:
        a_spec = pl.BlockSpec((tm, n), lambda i, j, kk: (i, kk))
    return _mm(name, dc, bs, _NT, (m // tm, k // tko, s), a_spec,
               pl.BlockSpec((None, tko, n), lambda i, j, kk: (kk, j, 0)),
               pl.BlockSpec((tm, tko), lambda i, j, kk: (i, j)), (m, k), out_dtype, (tm, tko))


def mm_nt_os(name, dc, b3, out_dtype=None):
    m, n = dc.shape
    s, kp, _ = b3.shape
    tm, tnr = _tile(m, 512), _tile(n, 1024, 128)
    return _mm(name, dc, b3, _NT, (m // tm, s, n // tnr),
               pl.BlockSpec((tm, tnr), lambda i, j, kk: (i, kk)),
               pl.BlockSpec((None, kp, tnr), lambda i, j, kk: (j, 0, kk)),
               pl.BlockSpec((None, tm, kp), lambda i, j, kk: (j, i, 0)), (s, m, kp), out_dtype, (tm, kp))


def mm_tn(name, a, dc, a_stacked=False, dc_cols=None, dc_stacked=False, out_dtype=None):
    if a_stacked:
        s, m, kp = a.shape
        n = dc.shape[1]
        tno, tmr = _tile(n, 512, 128), _tile(m, 1024)
        return _mm(name, a, dc, _TN, (s, n // tno, m // tmr),
                   pl.BlockSpec((None, tmr, kp), lambda i, j, kk: (i, kk, 0)),
                   pl.BlockSpec((tmr, tno), lambda i, j, kk: (kk, j)),
                   pl.BlockSpec((None, kp, tno), lambda i, j, kk: (i, 0, j)), (s, kp, n), out_dtype, (kp, tno))
    m, k = a.shape
    tko, tmr = _tile(k, 512, 128), _tile(m, 1024)
    a_spec = pl.BlockSpec((tmr, tko), lambda i, j, kk: (kk, i))
    if dc_stacked:
        s, _, n = dc.shape
        return _mm(name, a, dc, _TN, (k // tko, s, m // tmr), a_spec,
                   pl.BlockSpec((None, tmr, n), lambda i, j, kk: (j, kk, 0)),
                   pl.BlockSpec((None, tko, n), lambda i, j, kk: (j, i, 0)), (s, k, n), out_dtype, (tko, n))
    if dc_cols is not None:
        n = dc_cols
        s = dc.shape[1] // n
        return _mm(name, a, dc, _TN, (k // tko, s, m // tmr), a_spec,
                   pl.BlockSpec((tmr, n), lambda i, j, kk: (kk, j)),
                   pl.BlockSpec((None, tko, n), lambda i, j, kk: (j, i, 0)), (s, k, n), out_dtype, (tko, n))
    n = dc.shape[1]
    tno = _tile(n, 512, 128)
    return _mm(name, a, dc, _TN, (k // tko, n // tno, m // tmr), a_spec,
               pl.BlockSpec((tmr, tno), lambda i, j, kk: (kk, j)),
               pl.BlockSpec((tko, tno), lambda i, j, kk: (i, j)), (k, n), out_dtype, (tko, tno))


def rms_fwd(name, x, g):
    t, d = x.shape
    tr = _tile(t, 512)

    def body(x_ref, g_ref, o_ref):
        xf = x_ref[...]
        r = lax.rsqrt(jnp.mean(xf * xf, axis=-1, keepdims=True) + EPS)
        o_ref[...] = (xf * r * g_ref[...]).astype(o_ref.dtype)

    return pl.pallas_call(
        body, name=name, grid=(t // tr,),
        in_specs=[pl.BlockSpec((tr, d), lambda i: (i, 0)), pl.BlockSpec((1, d), lambda i: (0, 0))],
        out_specs=pl.BlockSpec((tr, d), lambda i: (i, 0)),
        out_shape=jax.ShapeDtypeStruct((t, d), BF16), compiler_params=_params(("parallel",)),
    )(x, g.reshape(1, d))


def rms_bwd(name, x, g, dh, dres=None, need_dx=True):
    t, d = x.shape
    tr = _tile(t, 512)

    def body(*refs):
        refs = list(refs)
        x_ref, g_ref, dh_ref = refs[:3]
        r_ref = refs[3] if dres is not None else None
        outs = refs[4:] if dres is not None else refs[3:]
        dx_ref, dg_ref = (outs[0], outs[1]) if need_dx else (None, outs[0])
        i = pl.program_id(0)

        @pl.when(i == 0)
        def _():
            dg_ref[...] = jnp.zeros_like(dg_ref)

        xf = x_ref[...]
        dhf = dh_ref[...].astype(F32)
        r = lax.rsqrt(jnp.mean(xf * xf, axis=-1, keepdims=True) + EPS)
        xh = xf * r
        dg_ref[...] += jnp.sum(dhf * xh, axis=0, keepdims=True)
        if need_dx:
            dxh = dhf * g_ref[...]
            dx = r * (dxh - xh * jnp.mean(dxh * xh, axis=-1, keepdims=True))
            if r_ref is not None:
                dx = dx + r_ref[...]
            dx_ref[...] = dx

    row = pl.BlockSpec((tr, d), lambda i: (i, 0))
    vec = pl.BlockSpec((1, d), lambda i: (0, 0))
    in_specs = [row, vec, row] + ([row] if dres is not None else [])
    args = (x, g.reshape(1, d), dh) + ((dres,) if dres is not None else ())
    out_specs = ([row] if need_dx else []) + [vec]
    out_shape = ([jax.ShapeDtypeStruct((t, d), F32)] if need_dx else []) + [jax.ShapeDtypeStruct((1, d), F32)]
    res = pl.pallas_call(
        body, name=name, grid=(t // tr,), in_specs=in_specs, out_specs=out_specs, out_shape=out_shape,
        compiler_params=_params(("arbitrary",)),
    )(*args)
    return res if need_dx else (None, res[0])


def loss_head(x, g, tgt):
    t, d = x.shape
    tr = _tile(t, 512)

    def body(x_ref, g_ref, t_ref, l_ref, dx_ref, dg_ref):
        i = pl.program_id(0)

        @pl.when(i == 0)
        def _():
            l_ref[...] = jnp.zeros_like(l_ref)
            dg_ref[...] = jnp.zeros_like(dg_ref)

        xf = x_ref[...]
        r = lax.rsqrt(jnp.mean(xf * xf, axis=-1, keepdims=True) + EPS)
        xh = xf * r
        diff = xh * g_ref[...] - t_ref[...]
        l_ref[...] += 0.5 * jnp.sum(jnp.mean(diff * diff, axis=-1, keepdims=True))
        dy = diff * (1.0 / d)
        dg_ref[...] += jnp.sum(dy * xh, axis=0, keepdims=True)
        dxh = dy * g_ref[...]
        dx_ref[...] = r * (dxh - xh * jnp.mean(dxh * xh, axis=-1, keepdims=True))

    row = pl.BlockSpec((tr, d), lambda i: (i, 0))
    vec = pl.BlockSpec((1, d), lambda i: (0, 0))
    return pl.pallas_call(
        body, name="loss_head", grid=(t // tr,), in_specs=[row, vec, row],
        out_specs=[pl.BlockSpec((1, 128), lambda i: (0, 0)), row, vec],
        out_shape=[jax.ShapeDtypeStruct((1, 128), F32), jax.ShapeDtypeStruct((t, d), F32),
                   jax.ShapeDtypeStruct((1, d), F32)],
        compiler_params=_params(("arbitrary",)),
    )(x, g.reshape(1, d), tgt)


def glu_fwd(glu, x):
    t, d = x.shape
    tr = _tile(t, 512)

    def body(v_ref, g_ref, x_ref, o_ref):
        o_ref[...] = x_ref[...] + v_ref[...] * jax.nn.sigmoid(g_ref[...])

    return pl.pallas_call(
        body, name="glu_fwd", grid=(t // tr,),
        in_specs=[pl.BlockSpec((tr, d), lambda i: (i, 0)), pl.BlockSpec((tr, d), lambda i: (i, 1)),
                  pl.BlockSpec((tr, d), lambda i: (i, 0))],
        out_specs=pl.BlockSpec((tr, d), lambda i: (i, 0)),
        out_shape=jax.ShapeDtypeStruct((t, d), F32), compiler_params=_params(("parallel",)),
    )(glu, glu, x)


def glu_bwd(glu, dmix):
    t, d = dmix.shape
    tr = _tile(t, 512)

    def body(v_ref, g_ref, d_ref, o_ref):
        sg = jax.nn.sigmoid(g_ref[...])
        dm = d_ref[...]
        o_ref[:, :d] = (dm * sg).astype(o_ref.dtype)
        o_ref[:, d:] = (dm * v_ref[...] * sg * (1.0 - sg)).astype(o_ref.dtype)

    return pl.pallas_call(
        body, name="glu_bwd", grid=(t // tr,),
        in_specs=[pl.BlockSpec((tr, d), lambda i: (i, 0)), pl.BlockSpec((tr, d), lambda i: (i, 1)),
                  pl.BlockSpec((tr, d), lambda i: (i, 0))],
        out_specs=pl.BlockSpec((tr, 2 * d), lambda i: (i, 0)),
        out_shape=jax.ShapeDtypeStruct((t, 2 * d), BF16), compiler_params=_params(("parallel",)),
    )(glu, glu, dmix)


def _log_sigmoid(z):
    return jnp.minimum(z, 0.0) - jnp.log(1.0 + jnp.exp(-jnp.abs(z)))


def _head_masks(shape):
    lane = lax.broadcasted_iota(jnp.int32, shape, 1)
    return lane < SB_HEAD_DIM


def sb_attn_fwd(proj, bsz, seq):
    qb = 128
    nq = seq // qb
    npair = SB_WIDTH // 128
    scale = SB_HEAD_DIM ** -0.5

    def body(q_ref, k_ref, v_ref, o_ref, r_ref):
        qi = pl.program_id(2)
        is_a = _head_masks((qb, 128))
        qf = q_ref[...]
        qs = [jnp.where(is_a, qf, 0.0).astype(MXU_DTYPE), jnp.where(is_a, 0.0, qf).astype(MXU_DTYPE)]
        row = lax.broadcasted_iota(jnp.int32, (qb, qb), 0)
        col = lax.broadcasted_iota(jnp.int32, (qb, qb), 1)
        upper = (row > col).astype(BF16)

        def step(i, carry):
            kb = qi - i
            ks = pl.ds(pl.multiple_of(kb * qb, qb), qb)
            kblk = k_ref[ks, :].astype(MXU_DTYPE)
            vblk = v_ref[ks, :].astype(MXU_DTYPE)
            causal = (col + kb * qb) < (row + qi * qb)
            new = []
            for h in range(2):
                acc, run = carry[2 * h], carry[2 * h + 1]
                z = lax.dot_general(qs[h], kblk, _NT, preferred_element_type=F32) * scale
                lb = _log_sigmoid(z)
                lk = jnp.where(causal, lb - z, 0.0)
                after = run + _dot_exact01(lk, upper)
                w = jnp.where(causal, jnp.exp(lb + after), 0.0)
                acc = acc + lax.dot_general(w.astype(MXU_DTYPE), vblk, _NN, preferred_element_type=F32)
                new += [acc, run + jnp.sum(lk, axis=1, keepdims=True)]
            return tuple(new)

        zero = jnp.zeros((qb, 128), F32)
        zc = jnp.zeros((qb, 1), F32)
        oa, ra, ob, rb = lax.fori_loop(0, qi + 1, step, (zero, zc, zero, zc))
        o_ref[...] = jnp.where(is_a, oa, ob).astype(o_ref.dtype)
        r_ref[...] = jnp.where(is_a, ra, rb)

    return pl.pallas_call(
        body, name="sb_attn_fwd", grid=(bsz, npair, nq),
        in_specs=[pl.BlockSpec((qb, 128), lambda b, p, i: (b * nq + i, p)),
                  pl.BlockSpec((seq, 128), lambda b, p, i: (b, npair + p)),
                  pl.BlockSpec((seq, 128), lambda b, p, i: (b, 2 * npair + p))],
        out_specs=[pl.BlockSpec((qb, 128), lambda b, p, i: (b * nq + i, p)),
                   pl.BlockSpec((qb, 128), lambda b, p, i: (b * nq + i, p))],
        out_shape=[jax.ShapeDtypeStruct((bsz * seq, SB_WIDTH), BF16),
                   jax.ShapeDtypeStruct((bsz * seq, SB_WIDTH), F32)],
        compiler_params=_params(("parallel", "parallel", "arbitrary")),
    )(proj, proj, proj)


def sb_attn_bwd(proj, rsum, dcat, bsz, seq):
    qb = 128
    nq = seq // qb
    npair = SB_WIDTH // 128
    scale = SB_HEAD_DIM ** -0.5

    def body(q_ref, k_ref, v_ref, r_ref, do_ref, dq_ref, dk_ref, dv_ref):
        qi = pl.program_id(2)

        @pl.when(qi == 0)
        def _():
            dk_ref[...] = jnp.zeros_like(dk_ref)
            dv_ref[...] = jnp.zeros_like(dv_ref)

        is_a = _head_masks((qb, 128))
        qf = q_ref[...]
        dof = do_ref[...].astype(F32)
        rf = r_ref[...]
        qs = [jnp.where(is_a, qf, 0.0).astype(MXU_DTYPE), jnp.where(is_a, 0.0, qf).astype(MXU_DTYPE)]
        dos = [jnp.where(is_a, dof, 0.0).astype(MXU_DTYPE), jnp.where(is_a, 0.0, dof).astype(MXU_DTYPE)]
        rtot = [rf[:, 0:1], rf[:, SB_HEAD_DIM:SB_HEAD_DIM + 1]]
        row = lax.broadcasted_iota(jnp.int32, (qb, qb), 0)
        col = lax.broadcasted_iota(jnp.int32, (qb, qb), 1)
        incl = (row <= col).astype(BF16)
        strict = (row < col).astype(BF16)

        def step(kb, carry):
            ks = pl.ds(pl.multiple_of(kb * qb, qb), qb)
            kblk = k_ref[ks, :].astype(MXU_DTYPE)
            vblk = v_ref[ks, :].astype(MXU_DTYPE)
            causal = (col + kb * qb) < (row + qi * qb)
            new = []
            dk_add = jnp.zeros((qb, 128), F32)
            dv_add = jnp.zeros((qb, 128), F32)
            for h in range(2):
                dq, pre, epre = carry[3 * h], carry[3 * h + 1], carry[3 * h + 2]
                z = lax.dot_general(qs[h], kblk, _NT, preferred_element_type=F32) * scale
                lb = _log_sigmoid(z)
                lk = jnp.where(causal, lb - z, 0.0)
                after = rtot[h] - (pre + _dot_exact01(lk, incl))
                w = jnp.where(causal, jnp.exp(lb + after), 0.0)
                dw = lax.dot_general(dos[h], vblk, _NT, preferred_element_type=F32)
                e = dw * w
                ecum = epre + _dot_exact01(e, strict)
                beta = jnp.exp(lb)
                dz = (jnp.where(causal, e * (1.0 - beta) - ecum * beta, 0.0) * scale).astype(MXU_DTYPE)
                dq = dq + lax.dot_general(dz, kblk, _NN, preferred_element_type=F32)
                dk_add = dk_add + lax.dot_general(dz, qs[h], _TN, preferred_element_type=F32)
                dv_add = dv_add + lax.dot_general(w.astype(MXU_DTYPE), dos[h], _TN, preferred_element_type=F32)
                new += [dq, pre + jnp.sum(lk, axis=1, keepdims=True), epre + jnp.sum(e, axis=1, keepdims=True)]
            dk_ref[ks, :] += dk_add
            dv_ref[ks, :] += dv_add
            return tuple(new)

        zero = jnp.zeros((qb, 128), F32)
        zc = jnp.zeros((qb, 1), F32)
        res = lax.fori_loop(0, qi + 1, step, (zero, zc, zc, zero, zc, zc))
        dq_ref[...] = jnp.where(is_a, res[0], res[3])

    full = jax.ShapeDtypeStruct((bsz * seq, SB_WIDTH), F32)
    qspec = pl.BlockSpec((qb, 128), lambda b, p, i: (b * nq + i, p))
    return pl.pallas_call(
        body, name="sb_attn_bwd", grid=(bsz, npair, nq),
        in_specs=[qspec,
                  pl.BlockSpec((seq, 128), lambda b, p, i: (b, npair + p)),
                  pl.BlockSpec((seq, 128), lambda b, p, i: (b, 2 * npair + p)),
                  qspec, qspec],
        out_specs=[qspec, pl.BlockSpec((seq, 128), lambda b, p, i: (b, p)),
                   pl.BlockSpec((seq, 128), lambda b, p, i: (b, p))],
        out_shape=[full, full, full],
        compiler_params=_params(("parallel", "parallel", "arbitrary")),
    )(proj, proj, proj, rsum, dcat)


def _window_sums(x, forward):
    n = x.shape[0]
    out = []
    s = x
    for sh in (1, 2, 4, 8):
        s = s + pltpu.roll(s, (n - sh) if forward else sh, 0)
        out.append(s)
    return out


def _pool_counts(tc, c, w):
    t = lax.broadcasted_iota(jnp.int32, (tc, 1), 0) + c * tc
    return jnp.minimum(t + 1, w).astype(F32)


def pool_fwd(proj, pool_w, pool_scale, bsz, seq):
    tc = _tile(seq, 512)
    nc = seq // tc
    hb = tc // POOL_HALO
    ucol = 3

    def body(u_ref, prev_ref, w_ref, s_ref, o_ref):
        c = pl.program_id(1)
        prev = jnp.where(c > 0, prev_ref[...], 0.0)
        x = jnp.concatenate([prev, u_ref[...]], axis=0)
        sums = _window_sums(x, forward=False)
        for g, win in enumerate(POOL_WINDOWS):
            ls = slice(g * POOL_GROUP, (g + 1) * POOL_GROUP)
            pooled = sums[g][POOL_HALO:, ls] / _pool_counts(tc, c, win) - x[POOL_HALO:, ls]
            y = _dot(pooled, w_ref[g], _NN)
            o_ref[:, ls] = (y * s_ref[:, ls]).astype(o_ref.dtype)

    return pl.pallas_call(
        body, name="pool_fwd", grid=(bsz, nc),
        in_specs=[pl.BlockSpec((tc, SB_WIDTH), lambda b, c: (b * nc + c, ucol)),
                  pl.BlockSpec((POOL_HALO, SB_WIDTH), lambda b, c: (jnp.maximum((b * nc + c) * hb - 1, 0), ucol)),
                  pl.BlockSpec((4, POOL_GROUP, POOL_GROUP), lambda b, c: (0, 0, 0)),
                  pl.BlockSpec((1, SB_WIDTH), lambda b, c: (0, 0))],
        out_specs=pl.BlockSpec((tc, SB_WIDTH), lambda b, c: (b * nc + c, 0)),
        out_shape=jax.ShapeDtypeStruct((bsz * seq, SB_WIDTH), BF16),
        compiler_params=_params(("parallel", "parallel")),
    )(proj, proj, pool_w, pool_scale)


def pool_bwd(proj, pool_w, pool_scale, dcat, bsz, seq):
    tc = _tile(seq, 512)
    nc = seq // tc
    hb = tc // POOL_HALO
    nblk = bsz * seq // POOL_HALO
    ucol = 3

    def body(u_ref, prev_ref, dy_ref, nxt_ref, w_ref, s_ref, du_ref, dw_ref, ds_ref):
        b, c = pl.program_id(0), pl.program_id(1)

        @pl.when((b == 0) & (c == 0))
        def _():
            dw_ref[...] = jnp.zeros_like(dw_ref)
            ds_ref[...] = jnp.zeros_like(ds_ref)

        prev = jnp.where(c > 0, prev_ref[...], 0.0)
        x = jnp.concatenate([prev, u_ref[...]], axis=0)
        sums = _window_sums(x, forward=False)
        nxt = jnp.where(c < nc - 1, nxt_ref[...].astype(F32), 0.0)
        dy = jnp.concatenate([dy_ref[...].astype(F32), nxt], axis=0)
        tq = lax.broadcasted_iota(jnp.int32, (tc + POOL_HALO, 1), 0) + c * tc
        for g, win in enumerate(POOL_WINDOWS):
            ls = slice(g * POOL_GROUP, (g + 1) * POOL_GROUP)
            pooled = sums[g][POOL_HALO:, ls] / _pool_counts(tc, c, win) - x[POOL_HALO:, ls]
            y = _dot(pooled, w_ref[g], _NN)
            ds_ref[:, ls] += jnp.sum(dy[:tc, ls] * y, axis=0, keepdims=True)
            dz = dy[:, ls] * s_ref[:, ls]
            dw_ref[g] += _dot(pooled, dz[:tc], _TN)
            dpool = _dot(dz, w_ref[g], _NT)
            dmean = dpool / jnp.minimum(tq + 1, win).astype(F32)
            fsum = _window_sums(dmean, forward=True)[g]
            du_ref[:, ls] = fsum[:tc] - dpool[:tc]

    return pl.pallas_call(
        body, name="pool_bwd", grid=(bsz, nc),
        in_specs=[pl.BlockSpec((tc, SB_WIDTH), lambda b, c: (b * nc + c, ucol)),
                  pl.BlockSpec((POOL_HALO, SB_WIDTH), lambda b, c: (jnp.maximum((b * nc + c) * hb - 1, 0), ucol)),
                  pl.BlockSpec((tc, SB_WIDTH), lambda b, c: (b * nc + c, 1)),
                  pl.BlockSpec((POOL_HALO, SB_WIDTH), lambda b, c: (jnp.minimum((b * nc + c + 1) * hb, nblk - 1), 1)),
                  pl.BlockSpec((4, POOL_GROUP, POOL_GROUP), lambda b, c: (0, 0, 0)),
                  pl.BlockSpec((1, SB_WIDTH), lambda b, c: (0, 0))],
        out_specs=[pl.BlockSpec((tc, SB_WIDTH), lambda b, c: (b * nc + c, 0)),
                   pl.BlockSpec((4, POOL_GROUP, POOL_GROUP), lambda b, c: (0, 0, 0)),
                   pl.BlockSpec((1, SB_WIDTH), lambda b, c: (0, 0))],
        out_shape=[jax.ShapeDtypeStruct((bsz * seq, SB_WIDTH), F32),
                   jax.ShapeDtypeStruct((4, POOL_GROUP, POOL_GROUP), F32),
                   jax.ShapeDtypeStruct((1, SB_WIDTH), F32)],
        compiler_params=_params(("arbitrary", "arbitrary")),
    )(proj, proj, dcat, dcat, pool_w, pool_scale)


def _lbar(lam_re, lam_im, log_dt):
    dt = jnp.exp(log_dt)
    mag = jnp.exp(lam_re * dt)
    ang = lam_im * dt
    return mag * jnp.cos(ang), mag * jnp.sin(ang)


def _bbar(lam_re, lam_im, log_dt, b_re, b_im):
    lb_re, lb_im = _lbar(lam_re, lam_im, log_dt)
    n_re = lb_re - 1.0
    den = lam_re * lam_re + lam_im * lam_im
    coef_re = (n_re * lam_re + lb_im * lam_im) / den
    coef_im = (lb_im * lam_re - n_re * lam_im) / den
    return coef_re * b_re - coef_im * b_im, coef_re * b_im + coef_im * b_re


def _expand01():
    p = lax.broadcasted_iota(jnp.int32, (64, 1024), 0)
    q = lax.broadcasted_iota(jnp.int32, (64, 1024), 1)
    return (lax.shift_right_logical(q, 4) == p).astype(BF16)


def ssm_prep(lam_re, lam_im, log_dt, b_re2, b_im2):
    def body(lr_ref, li_ref, dt_ref, br_ref, bi_ref, ar_ref, ai_ref, bbr_ref, bbi_ref):
        e = _expand01()
        lr, li, dt = lr_ref[...], li_ref[...], dt_ref[...]
        ar_ref[...], ai_ref[...] = _lbar(lr, li, dt)
        bbr_ref[...], bbi_ref[...] = _bbar(_dot_exact01(lr, e), _dot_exact01(li, e), dt, br_ref[...], bi_ref[...])

    s64 = jax.ShapeDtypeStruct((64, 64), F32)
    s1k = jax.ShapeDtypeStruct((64, 1024), F32)
    return pl.pallas_call(body, name="ssm_prep", out_shape=[s64, s64, s1k, s1k], compiler_params=_params())(
        lam_re, lam_im, log_dt, b_re2, b_im2)


def ssm_prep_bwd(lam_re, lam_im, log_dt, b_re2, b_im2, da_re, da_im, dbb_re, dbb_im):
    def body(lr_ref, li_ref, dt_ref, br_ref, bi_ref, dar_ref, dai_ref, dbr_ref, dbi_ref,
             olr_ref, oli_ref, odt_ref, obr_ref, obi_ref):
        e = _expand01()
        lr, li, dt = lr_ref[...], li_ref[...], dt_ref[...]
        _, vjp_a = jax.vjp(_lbar, lr, li, dt)
        g_lr, g_li, g_dt = vjp_a((dar_ref[...], dai_ref[...]))
        _, vjp_b = jax.vjp(_bbar, _dot_exact01(lr, e), _dot_exact01(li, e), dt, br_ref[...], bi_ref[...])
        x_lr, x_li, x_dt, g_br, g_bi = vjp_b((dbr_ref[...], dbi_ref[...]))
        olr_ref[...] = g_lr + _dot_exact01(x_lr, e, _NT)
        oli_ref[...] = g_li + _dot_exact01(x_li, e, _NT)
        odt_ref[...] = g_dt + x_dt
        obr_ref[...] = g_br
        obi_ref[...] = g_bi

    s64 = jax.ShapeDtypeStruct((64, 64), F32)
    s1k = jax.ShapeDtypeStruct((64, 1024), F32)
    return pl.pallas_call(body, name="ssm_prep_bwd",
                          out_shape=[s64, s64, jax.ShapeDtypeStruct((64, 1), F32), s1k, s1k],
                          compiler_params=_params())(
        lam_re, lam_im, log_dt, b_re2, b_im2, da_re, da_im, dbb_re, dbb_im)


def _gelu(y):
    c = math.sqrt(2.0 / math.pi)
    return 0.5 * y * (1.0 + jnp.tanh(c * (y + 0.044715 * y * y * y)))


def _gelu_grad(y):
    c = math.sqrt(2.0 / math.pi)
    th = jnp.tanh(c * (y + 0.044715 * y * y * y))
    return 0.5 * (1.0 + th) + 0.5 * y * (1.0 - th * th) * c * (1.0 + 3.0 * 0.044715 * y * y)


def _slab(t):
    return pl.ds(pl.multiple_of(t * SSM_SLAB, SSM_SLAB), SSM_SLAB)


def _st_store(ref, i, tc, val):
    for q in range(4):
        ref[pl.ds(4 * i + q, tc, stride=SSM_SLAB), :] = val[:, 128 * q:128 * (q + 1)]


def _st_load(ref, i, tc):
    return jnp.concatenate([ref[pl.ds(4 * i + q, tc, stride=SSM_SLAB), :] for q in range(4)], axis=1)


def ssm_fwd(u, wt, ct, a_re, a_im, dskip, bsz, seq):
    tc = _tile(seq, 256)
    nc = seq // tc
    ns = SSM_TILE_STATES

    def body(u_ref, wt_ref, ct_ref, ar_ref, ai_ref, d_ref, y_ref, gl_ref, hr_ref, hi_ref, sr_ref, si_ref):
        c = pl.program_id(1)

        @pl.when(c == 0)
        def _():
            sr_ref[...] = jnp.zeros_like(sr_ref)
            si_ref[...] = jnp.zeros_like(si_ref)

        uf = u_ref[...]
        for i in range(SSM_TILES):
            bu = _dot(uf[:, i * 128:(i + 1) * 128], wt_ref[i], _NN)
            _st_store(hr_ref, i, tc, bu[:, :ns])
            _st_store(hi_ref, i, tc, bu[:, ns:])
        ar, ai = ar_ref[...], ai_ref[...]

        def step(t, carry):
            sr, si = carry
            nr = ar * sr - ai * si + hr_ref[_slab(t), :]
            ni = ar * si + ai * sr + hi_ref[_slab(t), :]
            hr_ref[_slab(t), :] = nr
            hi_ref[_slab(t), :] = ni
            return nr, ni

        sr, si = lax.fori_loop(0, tc, step, (sr_ref[...], si_ref[...]), unroll=4)
        sr_ref[...] = sr
        si_ref[...] = si
        for i in range(SSM_TILES):
            hcat = jnp.concatenate([_st_load(hr_ref, i, tc), _st_load(hi_ref, i, tc)], axis=1)
            ls = slice(i * 128, (i + 1) * 128)
            y = _dot(hcat, ct_ref[i], _NN) + d_ref[:, ls] * uf[:, ls]
            y_ref[:, ls] = y
            gl_ref[:, ls] = _gelu(y).astype(gl_ref.dtype)

    t = bsz * seq
    row = pl.BlockSpec((tc, D_MODEL), lambda b, c: (b * nc + c, 0))
    st = pl.BlockSpec((tc * SSM_SLAB, 128), lambda b, c: (b * nc + c, 0))
    slab = pl.BlockSpec((SSM_SLAB, 128), lambda b, c: (0, 0))
    return pl.pallas_call(
        body, name="ssm_fwd", grid=(bsz, nc),
        in_specs=[row, pl.BlockSpec((SSM_TILES, 128, 2 * ns), lambda b, c: (0, 0, 0)),
                  pl.BlockSpec((SSM_TILES, 2 * ns, 128), lambda b, c: (0, 0, 0)), slab, slab,
                  pl.BlockSpec((1, D_MODEL), lambda b, c: (0, 0))],
        out_specs=[row, row, st, st],
        out_shape=[jax.ShapeDtypeStruct((t, D_MODEL), F32), jax.ShapeDtypeStruct((t, D_MODEL), BF16),
                   jax.ShapeDtypeStruct((t * SSM_SLAB, 128), F32), jax.ShapeDtypeStruct((t * SSM_SLAB, 128), F32)],
        scratch_shapes=[pltpu.VMEM((SSM_SLAB, 128), F32), pltpu.VMEM((SSM_SLAB, 128), F32)],
        compiler_params=_params(("parallel", "arbitrary")),
    )(u, wt, ct, a_re, a_im, dskip)


def ssm_bwd(dgl, y, u, h_re, h_im, wt, ct, a_re, a_im, dskip, bsz, seq):
    tc = _tile(seq, 256)
    nc = seq // tc
    ns = SSM_TILE_STATES

    def body(dgl_ref, y_ref, u_ref, hr_ref, hi_ref, pr_ref, pi_ref, wt_ref, ct_ref, ar_ref, ai_ref, d_ref,
             du_ref, dwt_ref, dct_ref, dd_ref, dar_ref, dai_ref, gr_ref, gi_ref, sr_ref, si_ref):
        b, c = pl.program_id(0), pl.program_id(1)

        @pl.when((b == 0) & (c == 0))
        def _():
            dwt_ref[...] = jnp.zeros_like(dwt_ref)
            dct_ref[...] = jnp.zeros_like(dct_ref)
            dd_ref[...] = jnp.zeros_like(dd_ref)
            dar_ref[...] = jnp.zeros_like(dar_ref)
            dai_ref[...] = jnp.zeros_like(dai_ref)

        @pl.when(c == 0)
        def _():
            sr_ref[...] = jnp.zeros_like(sr_ref)
            si_ref[...] = jnp.zeros_like(si_ref)

        uf = u_ref[...]
        dy = dgl_ref[...].astype(F32) * _gelu_grad(y_ref[...])
        dd_ref[...] += jnp.sum(dy * uf, axis=0, keepdims=True)
        for i in range(SSM_TILES):
            dyi = dy[:, i * 128:(i + 1) * 128]
            dh = _dot(dyi, ct_ref[i], _NT)
            _st_store(gr_ref, i, tc, dh[:, :ns])
            _st_store(gi_ref, i, tc, dh[:, ns:])
            hcat = jnp.concatenate([_st_load(hr_ref, i, tc), _st_load(hi_ref, i, tc)], axis=1)
            dct_ref[i] += _dot(hcat, dyi, _TN)
        ar, ai = ar_ref[...], ai_ref[...]

        def one(t, gr, gi, hpr, hpi, dar, dai):
            nr = gr_ref[_slab(t), :] + ar * gr + ai * gi
            ni = gi_ref[_slab(t), :] - ai * gr + ar * gi
            gr_ref[_slab(t), :] = nr
            gi_ref[_slab(t), :] = ni
            return nr, ni, dar + nr * hpr + ni * hpi, dai + ni * hpr - nr * hpi

        def step(j, carry):
            t = tc - 1 - j
            gr, gi, dar, dai = carry
            return one(t, gr, gi, hr_ref[_slab(t - 1), :], hi_ref[_slab(t - 1), :], dar, dai)

        carry = lax.fori_loop(0, tc - 1, step, (sr_ref[...], si_ref[...], dar_ref[...], dai_ref[...]), unroll=2)
        first = c == nc - 1
        hpr = jnp.where(first, 0.0, pr_ref[...])
        hpi = jnp.where(first, 0.0, pi_ref[...])
        gr, gi, dar, dai = one(0, *carry[:2], hpr, hpi, *carry[2:])
        sr_ref[...] = gr
        si_ref[...] = gi
        dar_ref[...] = dar
        dai_ref[...] = dai
        for i in range(SSM_TILES):
            ls = slice(i * 128, (i + 1) * 128)
            gcat = jnp.concatenate([_st_load(gr_ref, i, tc), _st_load(gi_ref, i, tc)], axis=1)
            du_ref[:, ls] = (_dot(gcat, wt_ref[i], _NT) + d_ref[:, ls] * dy[:, ls]).astype(du_ref.dtype)
            dwt_ref[i] += _dot(uf[:, ls], gcat, _TN)

    t = bsz * seq
    rev = lambda b, c: (b * nc + (nc - 1 - c), 0)
    row = pl.BlockSpec((tc, D_MODEL), rev)
    st = pl.BlockSpec((tc * SSM_SLAB, 128), rev)
    prev = pl.BlockSpec((SSM_SLAB, 128), lambda b, c: (jnp.maximum((b * nc + (nc - 1 - c)) * tc - 1, 0), 0))
    slab = pl.BlockSpec((SSM_SLAB, 128), lambda b, c: (0, 0))
    wts = pl.BlockSpec((SSM_TILES, 128, 2 * ns), lambda b, c: (0, 0, 0))
    cts = pl.BlockSpec((SSM_TILES, 2 * ns, 128), lambda b, c: (0, 0, 0))
    vec = pl.BlockSpec((1, D_MODEL), lambda b, c: (0, 0))
    return pl.pallas_call(
        body, name="ssm_bwd", grid=(bsz, nc),
        in_specs=[row, row, row, st, st, prev, prev, wts, cts, slab, slab, vec],
        out_specs=[row, wts, cts, vec, slab, slab],
        out_shape=[jax.ShapeDtypeStruct((t, D_MODEL), BF16),
                   jax.ShapeDtypeStruct((SSM_TILES, 128, 2 * ns), F32),
                   jax.ShapeDtypeStruct((SSM_TILES, 2 * ns, 128), F32),
                   jax.ShapeDtypeStruct((1, D_MODEL), F32),
                   jax.ShapeDtypeStruct((SSM_SLAB, 128), F32), jax.ShapeDtypeStruct((SSM_SLAB, 128), F32)],
        scratch_shapes=[pltpu.VMEM((tc * SSM_SLAB, 128), F32), pltpu.VMEM((tc * SSM_SLAB, 128), F32),
                        pltpu.VMEM((SSM_SLAB, 128), F32), pltpu.VMEM((SSM_SLAB, 128), F32)],
        compiler_params=_params(("arbitrary", "arbitrary")),
    )(dgl, y, u, h_re, h_im, h_re, h_im, wt, ct, a_re, a_im, dskip)


def _ssm_in_weights(bb_re2, bb_im2):
    eye = jnp.eye(8, dtype=F32)[None, :, None, :, None]

    def one(bb):
        t = bb.reshape(8, 8, 64, 16).transpose(0, 1, 3, 2)
        return (t[:, :, :, None, :] * eye).reshape(8, 128, 512)

    return jnp.concatenate([one(bb_re2), one(bb_im2)], axis=-1).astype(MXU_DTYPE)


def _ssm_in_weights_bwd(dwt):
    eye = jnp.eye(8, dtype=F32)[None, :, None, :, None]

    def one(d):
        t = (d.reshape(8, 8, 16, 8, 64) * eye).sum(axis=3)
        return t.transpose(0, 1, 3, 2).reshape(64, 1024)

    return one(dwt[..., :512]), one(dwt[..., 512:])


def _ssm_out_weights(c_re, c_im):
    eye = jnp.eye(8, dtype=F32)[None, :, None, :, None]

    def one(cc):
        t = cc.reshape(8, 8, 16, 64).transpose(0, 1, 3, 2)
        return (t[:, :, :, None, :] * eye).reshape(8, 512, 128)

    return jnp.concatenate([one(c_re), -one(c_im)], axis=1).astype(MXU_DTYPE)


def _ssm_out_weights_bwd(dct):
    eye = jnp.eye(8, dtype=F32)[None, :, None, :, None]

    def one(d):
        t = (d.reshape(8, 8, 64, 8, 16) * eye).sum(axis=3)
        return t.transpose(0, 1, 3, 2).reshape(64, 16, 64)

    return one(dct[:, :512]), -one(dct[:, 512:])


def _softmax(s):
    m = jnp.max(s, axis=-1, keepdims=True)
    e = jnp.exp(s - m)
    return e / jnp.sum(e, axis=-1, keepdims=True)


def xattn_fwd(q, kv, bsz, seq):
    tq = _tile(seq, 512)
    nq = seq // tq
    scale = XA_HEAD_DIM ** -0.5

    def body(q_ref, k_ref, v_ref, o_ref):
        s = lax.dot_general(q_ref[...], k_ref[...], _NT, preferred_element_type=F32) * scale
        p = _softmax(s)
        o_ref[...] = _dot(p, v_ref[...], _NN).astype(o_ref.dtype)

    qs = pl.BlockSpec((tq, XA_HEAD_DIM), lambda b, h, i: (b * nq + i, h))
    return pl.pallas_call(
        body, name="xattn_fwd", grid=(bsz, XA_HEADS, nq),
        in_specs=[qs, pl.BlockSpec((MEM_LEN, XA_HEAD_DIM), lambda b, h, i: (b, h)),
                  pl.BlockSpec((MEM_LEN, XA_HEAD_DIM), lambda b, h, i: (b, XA_HEADS + h))],
        out_specs=qs, out_shape=jax.ShapeDtypeStruct((bsz * seq, D_MODEL), BF16),
        compiler_params=_params(("parallel", "parallel", "parallel")),
    )(q, kv, kv)


def xattn_bwd(q, kv, do, bsz, seq):
    tq = _tile(seq, 512)
    nq = seq // tq
    scale = XA_HEAD_DIM ** -0.5

    def body(q_ref, k_ref, v_ref, do_ref, dq_ref, dk_ref, dv_ref):
        @pl.when(pl.program_id(2) == 0)
        def _():
            dk_ref[...] = jnp.zeros_like(dk_ref)
            dv_ref[...] = jnp.zeros_like(dv_ref)

        qv, kk, vv, dov = q_ref[...], k_ref[...], v_ref[...], do_ref[...]
        s = lax.dot_general(qv, kk, _NT, preferred_element_type=F32) * scale
        p = _softmax(s)
        dp = lax.dot_general(dov, vv, _NT, preferred_element_type=F32)
        ds = (p * (dp - jnp.sum(dp * p, axis=-1, keepdims=True)) * scale).astype(MXU_DTYPE)
        dq_ref[...] = lax.dot_general(ds, kk, _NN, preferred_element_type=F32).astype(dq_ref.dtype)
        dk_ref[...] += lax.dot_general(ds, qv, _TN, preferred_element_type=F32)
        dv_ref[...] += lax.dot_general(p.astype(MXU_DTYPE), dov, _TN, preferred_element_type=F32)

    qs = pl.BlockSpec((tq, XA_HEAD_DIM), lambda b, h, i: (b * nq + i, h))
    ks = pl.BlockSpec((MEM_LEN, XA_HEAD_DIM), lambda b, h, i: (b, h))
    vs = pl.BlockSpec((MEM_LEN, XA_HEAD_DIM), lambda b, h, i: (b, XA_HEADS + h))
    dkv = jax.ShapeDtypeStruct((bsz * MEM_LEN, D_MODEL), F32)
    dq, dk, dv = pl.pallas_call(
        body, name="xattn_bwd", grid=(bsz, XA_HEADS, nq),
        in_specs=[qs, ks, vs, qs], out_specs=[qs, ks, ks],
        out_shape=[jax.ShapeDtypeStruct((bsz * seq, D_MODEL), BF16), dkv, dkv],
        compiler_params=_params(("parallel", "parallel", "arbitrary")),
    )(q, kv, kv, do)
    return dq, dk, dv


CONV_HALO = 16


def _shift_down(x, prev, n):
    r = pltpu.roll(x, n, 0)
    row = lax.broadcasted_iota(jnp.int32, x.shape, 0)
    last = prev.shape[0]
    for k in range(n):
        r = jnp.where(row == k, prev[last - n + k:last - n + k + 1, :], r)
    return r


def _shift_up(x, nxt, n):
    rows = x.shape[0]
    r = pltpu.roll(x, rows - n, 0)
    row = lax.broadcasted_iota(jnp.int32, x.shape, 0)
    for k in range(n):
        r = jnp.where(row == rows - n + k, nxt[k:k + 1, :], r)
    return r


def _conv_taps(u, prev, w, b):
    return b + w[2:3] * u + w[1:2] * _shift_down(u, prev, 1) + w[0:1] * _shift_down(u, prev, 2)


def conv_fwd(up, cw, cb, bsz, seq):
    tc = _tile(seq, 512)
    nc = seq // tc
    hb = tc // CONV_HALO
    half = N_DEV // 2

    def body(uv_ref, ug_ref, pv_ref, pg_ref, wv_ref, wg_ref, bv_ref, bg_ref, o_ref):
        c = pl.program_id(2)
        pv = jnp.where(c > 0, pv_ref[...].astype(F32), 0.0)
        pg = jnp.where(c > 0, pg_ref[...].astype(F32), 0.0)
        val = _conv_taps(uv_ref[...].astype(F32), pv, wv_ref[...], bv_ref[...])
        gate = _conv_taps(ug_ref[...].astype(F32), pg, wg_ref[...], bg_ref[...])
        o_ref[...] = (gate * jax.nn.sigmoid(gate) * val).astype(o_ref.dtype)

    def cur(off):
        return pl.BlockSpec((None, tc, FF_SHARD), lambda b, j, c: (j + off, b * nc + c, 0))

    def prv(off):
        return pl.BlockSpec((None, CONV_HALO, FF_SHARD), lambda b, j, c: (j + off, jnp.maximum((b * nc + c) * hb - 1, 0), 0))

    def par(rows, off):
        return pl.BlockSpec((None, rows, FF_SHARD), lambda b, j, c: (j + off, 0, 0))

    return pl.pallas_call(
        body, name="conv_fwd", grid=(bsz, half, nc),
        in_specs=[cur(0), cur(half), prv(0), prv(half), par(3, 0), par(3, half), par(1, 0), par(1, half)],
        out_specs=cur(0), out_shape=jax.ShapeDtypeStruct((half, bsz * seq, FF_SHARD), BF16),
        compiler_params=_params(("parallel", "parallel", "parallel")),
    )(up, up, up, up, cw, cw, cb, cb)


def conv_bwd_taps(up, cw, cb, dact, bsz, seq):
    tc = _tile(seq, 512)
    nc = seq // tc
    hb = tc // CONV_HALO
    half = N_DEV // 2

    def body(uv_ref, ug_ref, pv_ref, pg_ref, wv_ref, wg_ref, bv_ref, bg_ref, da_ref,
             dcv_ref, dcg_ref, dwv_ref, dwg_ref, dbv_ref, dbg_ref):
        b, c = pl.program_id(1), pl.program_id(2)

        @pl.when((b == 0) & (c == 0))
        def _():
            for r in (dwv_ref, dwg_ref, dbv_ref, dbg_ref):
                r[...] = jnp.zeros_like(r)

        pv = jnp.where(c > 0, pv_ref[...].astype(F32), 0.0)
        pg = jnp.where(c > 0, pg_ref[...].astype(F32), 0.0)
        uv, ug = uv_ref[...].astype(F32), ug_ref[...].astype(F32)
        val = _conv_taps(uv, pv, wv_ref[...], bv_ref[...])
        gate = _conv_taps(ug, pg, wg_ref[...], bg_ref[...])
        sg = jax.nn.sigmoid(gate)
        da = da_ref[...].astype(F32)
        dval = da * gate * sg
        dgate = da * val * sg * (1.0 + gate * (1.0 - sg))
        dcv_ref[...] = dval.astype(dcv_ref.dtype)
        dcg_ref[...] = dgate.astype(dcg_ref.dtype)
        for dcv, u, prev, dw_ref, db_ref in ((dval, uv, pv, dwv_ref, dbv_ref), (dgate, ug, pg, dwg_ref, dbg_ref)):
            db_ref[...] += jnp.sum(dcv, axis=0, keepdims=True)
            dw_ref[2:3, :] += jnp.sum(dcv * u, axis=0, keepdims=True)
            dw_ref[1:2, :] += jnp.sum(dcv * _shift_down(u, prev, 1), axis=0, keepdims=True)
            dw_ref[0:1, :] += jnp.sum(dcv * _shift_down(u, prev, 2), axis=0, keepdims=True)

    def cur(off):
        return pl.BlockSpec((None, tc, FF_SHARD), lambda j, b, c: (j + off, b * nc + c, 0))

    def prv(off):
        return pl.BlockSpec((None, CONV_HALO, FF_SHARD), lambda j, b, c: (j + off, jnp.maximum((b * nc + c) * hb - 1, 0), 0))

    def par(rows, off):
        return pl.BlockSpec((None, rows, FF_SHARD), lambda j, b, c: (j + off, 0, 0))

    t = bsz * seq
    hs = jax.ShapeDtypeStruct((half, t, FF_SHARD), BF16)
    ws = jax.ShapeDtypeStruct((half, 3, FF_SHARD), F32)
    bs = jax.ShapeDtypeStruct((half, 1, FF_SHARD), F32)
    dcv, dcg, dwv, dwg, dbv, dbg = pl.pallas_call(
        body, name="conv_bwd_taps", grid=(half, bsz, nc),
        in_specs=[cur(0), cur(half), prv(0), prv(half), par(3, 0), par(3, half), par(1, 0), par(1, half), cur(0)],
        out_specs=[cur(0), cur(0), par(3, 0), par(3, 0), par(1, 0), par(1, 0)],
        out_shape=[hs, hs, ws, ws, bs, bs],
        compiler_params=_params(("parallel", "arbitrary", "arbitrary")),
    )(up, up, up, up, cw, cw, cb, cb, dact)
    return (jnp.concatenate([dcv, dcg], axis=0), jnp.concatenate([dwv, dwg], axis=0),
            jnp.concatenate([dbv, dbg], axis=0))


def conv_bwd_input(dconv, cw, bsz, seq):
    tc = _tile(seq, 512)
    nc = seq // tc
    hb = tc // CONV_HALO
    nblk = bsz * seq // CONV_HALO

    def body(d_ref, n_ref, w_ref, o_ref):
        c = pl.program_id(2)
        nxt = jnp.where(c < nc - 1, n_ref[...].astype(F32), 0.0)
        d = d_ref[...].astype(F32)
        w = w_ref[...]
        o_ref[...] = (w[2:3] * d + w[1:2] * _shift_up(d, nxt, 1) + w[0:1] * _shift_up(d, nxt, 2)).astype(o_ref.dtype)

    cur = pl.BlockSpec((None, tc, FF_SHARD), lambda j, b, c: (j, b * nc + c, 0))
    return pl.pallas_call(
        body, name="conv_bwd_input", grid=(N_DEV, bsz, nc),
        in_specs=[cur, pl.BlockSpec((None, CONV_HALO, FF_SHARD),
                                    lambda j, b, c: (j, jnp.minimum((b * nc + c + 1) * hb, nblk - 1), 0)),
                  pl.BlockSpec((None, 3, FF_SHARD), lambda j, b, c: (j, 0, 0))],
        out_specs=cur, out_shape=jax.ShapeDtypeStruct(dconv.shape, BF16),
        compiler_params=_params(("parallel", "parallel", "parallel")),
    )(dconv, dconv, cw)


def _my_index():
    return 4 * lax.axis_index("x") + 2 * lax.axis_index("y") + lax.axis_index("c")


def _peer(k):
    return (lax.axis_index("x") ^ ((k >> 2) & 1), lax.axis_index("y") ^ ((k >> 1) & 1),
            lax.axis_index("c") ^ (k & 1))


def all_gather(name, a, out_dtype):
    def body(a_ref, o_ref, stage, send_sems, recv_sems, local_sem):
        me = _my_index()
        stage[...] = a_ref[...].astype(out_dtype)
        local = pltpu.make_async_copy(stage, o_ref.at[me], local_sem)
        local.start()
        sends = []
        for k in range(1, N_DEV):
            cp = pltpu.make_async_remote_copy(
                src_ref=stage, dst_ref=o_ref.at[me], send_sem=send_sems.at[k - 1], recv_sem=recv_sems.at[k - 1],
                device_id=_peer(k), device_id_type=pl.DeviceIdType.MESH)
            cp.start()
            sends.append(cp)
        for k in range(1, N_DEV):
            pltpu.make_async_remote_copy(
                src_ref=stage, dst_ref=o_ref.at[me ^ k], send_sem=send_sems.at[k - 1], recv_sem=recv_sems.at[k - 1],
                device_id=_peer(k), device_id_type=pl.DeviceIdType.MESH).wait_recv()
        for cp in sends:
            cp.wait_send()
        local.wait()

    return pl.pallas_call(
        body, name=name, in_specs=[pl.BlockSpec(memory_space=pltpu.VMEM)],
        out_specs=pl.BlockSpec(memory_space=pltpu.HBM),
        out_shape=jax.ShapeDtypeStruct((N_DEV,) + a.shape, out_dtype),
        scratch_shapes=[pltpu.VMEM(a.shape, out_dtype), pltpu.SemaphoreType.DMA((N_DEV - 1,)),
                        pltpu.SemaphoreType.DMA((N_DEV - 1,)), pltpu.SemaphoreType.DMA],
        compiler_params=pltpu.CompilerParams(vmem_limit_bytes=VMEM_LIMIT),
    )(a)


def exchange(name, g):
    def body(g_ref, r_ref, send_sems, recv_sems, local_sem):
        me = _my_index()
        local = pltpu.make_async_copy(g_ref.at[me], r_ref.at[me], local_sem)
        local.start()
        sends = []
        for k in range(1, N_DEV):
            cp = pltpu.make_async_remote_copy(
                src_ref=g_ref.at[me ^ k], dst_ref=r_ref.at[me], send_sem=send_sems.at[k - 1],
                recv_sem=recv_sems.at[k - 1], device_id=_peer(k), device_id_type=pl.DeviceIdType.MESH)
            cp.start()
            sends.append(cp)
        for k in range(1, N_DEV):
            pltpu.make_async_remote_copy(
                src_ref=g_ref.at[me], dst_ref=r_ref.at[me ^ k], send_sem=send_sems.at[k - 1],
                recv_sem=recv_sems.at[k - 1], device_id=_peer(k), device_id_type=pl.DeviceIdType.MESH).wait_recv()
        for cp in sends:
            cp.wait_send()
        local.wait()

    return pl.pallas_call(
        body, name=name, in_specs=[pl.BlockSpec(memory_space=pltpu.HBM)],
        out_specs=pl.BlockSpec(memory_space=pltpu.HBM),
        out_shape=jax.ShapeDtypeStruct(g.shape, g.dtype),
        scratch_shapes=[pltpu.SemaphoreType.DMA((N_DEV - 1,)), pltpu.SemaphoreType.DMA((N_DEV - 1,)),
                        pltpu.SemaphoreType.DMA],
    )(g)


def sum_parts(name, r):
    _, rows, cols = r.shape

    def body(r_ref, o_ref):
        acc = r_ref[0].astype(F32)
        for s in range(1, N_DEV):
            acc = acc + r_ref[s].astype(F32)
        o_ref[...] = acc

    return pl.pallas_call(body, name=name, out_shape=jax.ShapeDtypeStruct((rows, cols), F32),
                          compiler_params=_params())(r)


def adamw(name, w, m, v, parts=None, g=None):
    rows, cols = w.shape
    br = _tile(rows, 256, 16)
    c1 = 1.0 / (1.0 - ADAM_B1 ** ADAM_STEP)
    c2 = 1.0 / (1.0 - ADAM_B2 ** ADAM_STEP)

    def body(g_ref, w_ref, m_ref, v_ref, og_ref, od_ref, om_ref, ov_ref):
        if parts is None:
            gs = g_ref[...]
        else:
            gs = g_ref[0].astype(F32)
            for s in range(1, N_DEV):
                gs = gs + g_ref[s].astype(F32)
        mn = ADAM_B1 * m_ref[...] + (1.0 - ADAM_B1) * gs
        vn = ADAM_B2 * v_ref[...] + (1.0 - ADAM_B2) * (gs * gs)
        og_ref[...] = gs
        om_ref[...] = mn
        ov_ref[...] = vn
        od_ref[...] = -ADAM_LR * ((mn * c1) / (jnp.sqrt(vn * c2) + ADAM_EPS) + ADAM_WD * w_ref[...])

    blk = pl.BlockSpec((br, cols), lambda i: (i, 0))
    gspec = blk if parts is None else pl.BlockSpec((N_DEV, br, cols), lambda i: (0, i, 0))
    shp = jax.ShapeDtypeStruct((rows, cols), F32)
    return pl.pallas_call(
        body, name=name, grid=(rows // br,), in_specs=[gspec, blk, blk, blk], out_specs=[blk] * 4,
        out_shape=[shp] * 4, compiler_params=_params(("parallel",)),
    )(g if parts is None else parts, w, m, v)


SMALL = ("norm_mix", "norm_xattn", "norm_ffn", "norm_mem", "norm_final", "pool_w", "pool_scale",
         "ssm_lam_re", "ssm_lam_im", "ssm_log_dt", "ssm_b_re", "ssm_b_im", "ssm_c_re", "ssm_c_im",
         "ffn_conv_b", "ssm_d", "ffn_conv_w")
SMALL_SHARDED = {"ssm_d": 1, "ffn_conv_w": 2}
BIG = ("ab_w_in", "ab_w_out", "ssm_w_in", "ssm_w_glu", "xa_w_q", "xa_w_kv", "xa_w_o", "ffn_w_up", "ffn_w_down")
WEIGHTS = ("norm_mix", "norm_xattn", "norm_ffn", "norm_mem", "norm_final", "ab_w_in", "pool_w", "pool_scale",
           "ab_w_out", "ssm_w_in", "ssm_lam_re", "ssm_lam_im", "ssm_log_dt", "ssm_b_re", "ssm_b_im", "ssm_c_re",
           "ssm_c_im", "ssm_d", "ssm_w_glu", "xa_w_q", "xa_w_kv", "xa_w_o", "ffn_w_up", "ffn_conv_w", "ffn_conv_b",
           "ffn_w_down")


def _layer_tail(l, x_in, mem_n, w, acts):
    bsz, seq = acts["bsz"], acts["seq"]
    hq = rms_fwd(f"rms_xattn{l}", x_in, w["norm_xattn"][l])
    q = mm_nn(f"xa_q{l}", hq, w["xa_w_q"][l])
    kv = mm_nn_bs(f"xa_kv{l}", mem_n, w["xa_w_kv"][l])
    o = xattn_fwd(q, kv, bsz, seq)
    x_mid = mm_nn(f"xa_o{l}", o, w["xa_w_o"][l], res=x_in, out_dtype=F32)
    hf = rms_fwd(f"rms_ffn{l}", x_mid, w["norm_ffn"][l])
    up = mm_nn_bs(f"ffn_up{l}", hf, w["ffn_w_up"][l], stacked_out=True)
    act = conv_fwd(up, w["ffn_conv_w"][l], w["ffn_conv_b"][l], bsz, seq)
    x_out = mm_as_nn(f"ffn_down{l}", act, w["ffn_w_down"][l], res=x_mid)
    acts[l].update(x_in=x_in, hq=hq, q=q, kv=kv, o=o, x_mid=x_mid, hf=hf, up=up, act=act)
    return x_out


def _layer_tail_bwd(l, dx, mem_n, w, acts, grads):
    a = acts[l]
    bsz, seq = acts["bsz"], acts["seq"]
    dact = mm_nt_os(f"d_act{l}", dx, w["ffn_w_down"][l])
    grads["ffn_w_down"][l] = mm_tn(f"g_ffn_down{l}", a["act"], dx, a_stacked=True)
    dconv, dcw, dcb = conv_bwd_taps(a["up"], w["ffn_conv_w"][l], w["ffn_conv_b"][l], dact, bsz, seq)
    grads["ffn_conv_w"][l] = dcw
    grads["ffn_conv_b"][l] = dcb
    dup = conv_bwd_input(dconv, w["ffn_conv_w"][l], bsz, seq)
    dhf = mm_nt_bs(f"d_hf{l}", dup, w["ffn_w_up"][l], dc_stacked=True)
    grads["ffn_w_up"][l] = mm_tn(f"g_ffn_up{l}", a["hf"], dup, dc_stacked=True)
    dx_mid, grads["norm_ffn"][l] = rms_bwd(f"rms_ffn_bwd{l}", a["x_mid"], w["norm_ffn"][l], dhf, dres=dx)
    do = mm_nt(f"d_o{l}", dx_mid, w["xa_w_o"][l])
    grads["xa_w_o"][l] = mm_tn(f"g_xa_o{l}", a["o"], dx_mid)
    dq, dk, dv = xattn_bwd(a["q"], a["kv"], do, bsz, seq)
    dkv = jnp.concatenate([dk, dv], axis=1).astype(BF16)
    dhq = mm_nt(f"d_hq{l}", dq, w["xa_w_q"][l])
    grads["xa_w_q"][l] = mm_tn(f"g_xa_q{l}", a["hq"], dq)
    dmem_n = mm_nt_bs(f"d_memn{l}", dkv, w["xa_w_kv"][l], out_dtype=F32)
    grads["xa_w_kv"][l] = mm_tn(f"g_xa_kv{l}", mem_n, dkv, dc_cols=2 * D_MODEL // N_DEV)
    dx_in, grads["norm_xattn"][l] = rms_bwd(f"rms_xattn_bwd{l}", a["x_in"], w["norm_xattn"][l], dhq, dres=dx_mid)
    return dx_in, dmem_n


def kernel(x, mem, norm_mix, norm_xattn, norm_ffn, norm_mem, norm_final, ab_w_in, pool_w, pool_scale, ab_w_out, ssm_w_in, ssm_lam_re, ssm_lam_im, ssm_log_dt, ssm_b_re, ssm_b_im, ssm_c_re, ssm_c_im, ssm_d, ssm_w_glu, xa_w_q, xa_w_kv, xa_w_o, ffn_w_up, ffn_conv_w, ffn_conv_b, ffn_w_down, loss_target, m_norm_mix, m_norm_xattn, m_norm_ffn, m_norm_mem, m_norm_final, m_ab_w_in, m_pool_w, m_pool_scale, m_ab_w_out, m_ssm_w_in, m_ssm_lam_re, m_ssm_lam_im, m_ssm_log_dt, m_ssm_b_re, m_ssm_b_im, m_ssm_c_re, m_ssm_c_im, m_ssm_d, m_ssm_w_glu, m_xa_w_q, m_xa_w_kv, m_xa_w_o, m_ffn_w_up, m_ffn_conv_w, m_ffn_conv_b, m_ffn_w_down, v_norm_mix, v_norm_xattn, v_norm_ffn, v_norm_mem, v_norm_final, v_ab_w_in, v_pool_w, v_pool_scale, v_ab_w_out, v_ssm_w_in, v_ssm_lam_re, v_ssm_lam_im, v_ssm_log_dt, v_ssm_b_re, v_ssm_b_im, v_ssm_c_re, v_ssm_c_im, v_ssm_d, v_ssm_w_glu, v_xa_w_q, v_xa_w_kv, v_xa_w_o, v_ffn_w_up, v_ffn_conv_w, v_ffn_conv_b, v_ffn_w_down):
    given = dict(locals())
    master = {n: given[n] for n in WEIGHTS}
    mom1 = {n: given["m_" + n] for n in WEIGHTS}
    mom2 = {n: given["v_" + n] for n in WEIGHTS}
    bsz, seq, d = x.shape
    t = bsz * seq
    me = _my_index()

    full = {n: all_gather("ag_" + n, master[n], MXU_DTYPE) for n in BIG}
    conv_w_st = all_gather("ag_ffn_conv_w", ffn_conv_w, F32)
    dskip = all_gather("ag_ssm_d", ssm_d.reshape(1, 128), F32).reshape(1, D_MODEL)
    w = {
        "norm_mix": norm_mix, "norm_xattn": norm_xattn, "norm_ffn": norm_ffn,
        "ab_w_in": full["ab_w_in"][:, 0],
        "ab_w_out": full["ab_w_out"].reshape(D_MODEL, D_MODEL),
        "ssm_w_in": full["ssm_w_in"].reshape(D_MODEL, D_MODEL),
        "ssm_w_glu": full["ssm_w_glu"][:, 0],
        "xa_w_q": [full["xa_w_q"][:, l].reshape(D_MODEL, D_MODEL) for l in range(2)],
        "xa_w_kv": [full["xa_w_kv"][:, l] for l in range(2)],
        "xa_w_o": [full["xa_w_o"][:, l].reshape(D_MODEL, D_MODEL) for l in range(2)],
        "ffn_w_up": [full["ffn_w_up"][:, l] for l in range(2)],
        "ffn_w_down": [full["ffn_w_down"][:, l].reshape(4, FF_SHARD, D_MODEL) for l in range(2)],
        "ffn_conv_w": [conv_w_st[:, l] for l in range(2)],
        "ffn_conv_b": [ffn_conv_b[l].reshape(N_DEV, 1, FF_SHARD) for l in range(2)],
    }

    acts = {"bsz": bsz, "seq": seq, 0: {}, 1: {}}
    x0 = x.reshape(t, d)
    mem2 = mem.reshape(bsz * MEM_LEN, d)
    mem_n = rms_fwd("rms_mem", mem2, norm_mem)
    pscale = pool_scale.reshape(1, SB_WIDTH)

    h0 = rms_fwd("rms_mix0", x0, norm_mix[0])
    proj = mm_nn_bs("ab_in", h0, w["ab_w_in"], out_dtype=F32)
    a_out, rsum = sb_attn_fwd(proj, bsz, seq)
    p_out = pool_fwd(proj, pool_w[0], pscale, bsz, seq)
    x1 = mm_nn("ab_out_a", a_out, w["ab_w_out"], res=x0, out_dtype=F32)
    x1 = mm_nn("ab_out_p", p_out, w["ab_w_out"], res=x1, koff=SB_WIDTH, out_dtype=F32)
    x3 = _layer_tail(0, x1, mem_n, w, acts)

    b_re2 = ssm_b_re.reshape(64, 1024)
    b_im2 = ssm_b_im.reshape(64, 1024)
    log_dt = ssm_log_dt.reshape(64, 1)
    lb_re, lb_im, bb_re2, bb_im2 = ssm_prep(ssm_lam_re[0], ssm_lam_im[0], log_dt, b_re2, b_im2)
    wt = _ssm_in_weights(bb_re2, bb_im2)
    ct = _ssm_out_weights(ssm_c_re[0], ssm_c_im[0])
    a_re = lb_re.reshape(SSM_SLAB, 128)
    a_im = lb_im.reshape(SSM_SLAB, 128)
    h1 = rms_fwd("rms_mix1", x3, norm_mix[1])
    u = mm_nn("ssm_in", h1, w["ssm_w_in"], out_dtype=F32)
    y, gl, h_re, h_im = ssm_fwd(u, wt, ct, a_re, a_im, dskip, bsz, seq)
    glu = mm_nn_bs("ssm_glu", gl, w["ssm_w_glu"], out_dtype=F32)
    x4 = glu_fwd(glu, x3)
    x6 = _layer_tail(1, x4, mem_n, w, acts)

    loss_row, dx, g_norm_final = loss_head(x6, norm_final, loss_target.reshape(t, d))
    loss = lax.psum(loss_row[0, 0], MESH_AXES)

    grads = {n: [None, None] for n in ("ffn_w_down", "ffn_conv_w", "ffn_conv_b", "ffn_w_up", "norm_ffn", "xa_w_o",
                                       "xa_w_q", "xa_w_kv", "norm_xattn", "norm_mix")}
    dx4, dmem_1 = _layer_tail_bwd(1, dx, mem_n, w, acts, grads)
    dglu = glu_bwd(glu, dx4)
    dgl = mm_nt_bs("d_gl", dglu, w["ssm_w_glu"])
    g_ssm_w_glu = mm_tn("g_ssm_glu", gl, dglu, dc_cols=2 * D_MODEL // N_DEV)
    du, dwt, dct, g_dskip, da_re, da_im = ssm_bwd(dgl, y, u, h_re, h_im, wt, ct, a_re, a_im, dskip, bsz, seq)
    dbb_re, dbb_im = _ssm_in_weights_bwd(dwt)
    g_c_re, g_c_im = _ssm_out_weights_bwd(dct)
    g_lam_re, g_lam_im, g_log_dt, g_b_re, g_b_im = ssm_prep_bwd(
        ssm_lam_re[0], ssm_lam_im[0], log_dt, b_re2, b_im2, da_re.reshape(64, 64), da_im.reshape(64, 64),
        dbb_re, dbb_im)
    dh1 = mm_nt("d_h1", du, w["ssm_w_in"])
    g_ssm_w_in = mm_tn("g_ssm_in", h1, du)
    dx3, grads["norm_mix"][1] = rms_bwd("rms_mix1_bwd", x3, norm_mix[1], dh1, dres=dx4)

    dx1, dmem_0 = _layer_tail_bwd(0, dx3, mem_n, w, acts, grads)
    dcat = mm_nt("d_cat", dx1, w["ab_w_out"])
    g_ab_w_out = jnp.concatenate([mm_tn("g_ab_out_a", a_out, dx1), mm_tn("g_ab_out_p", p_out, dx1)], axis=0)
    dq, dk, dv = sb_attn_bwd(proj, rsum, dcat, bsz, seq)
    dpu, g_pool_w, g_pool_scale = pool_bwd(proj, pool_w[0], pscale, dcat, bsz, seq)
    dproj = jnp.concatenate([dq, dk, dv, dpu], axis=1).astype(BF16)
    dh0 = mm_nt_bs("d_h0", dproj, w["ab_w_in"])
    g_ab_w_in = mm_tn("g_ab_in", h0, dproj, dc_cols=2 * D_MODEL // N_DEV)
    dx0, grads["norm_mix"][0] = rms_bwd("rms_mix0_bwd", x0, norm_mix[0], dh0, dres=dx1)
    _, g_norm_mem = rms_bwd("rms_mem_bwd", mem2, norm_mem, dmem_0 + dmem_1, need_dx=False)

    def per_layer(name):
        return jnp.stack(grads[name], axis=1)

    big_parts = {
        "ab_w_in": g_ab_w_in[:, None], "ab_w_out": g_ab_w_out.reshape(N_DEV, 1, 128, D_MODEL),
        "ssm_w_in": g_ssm_w_in.reshape(N_DEV, 1, 128, D_MODEL), "ssm_w_glu": g_ssm_w_glu[:, None],
        "xa_w_q": jnp.stack([g.reshape(N_DEV, 128, D_MODEL) for g in grads["xa_w_q"]], axis=1),
        "xa_w_kv": per_layer("xa_w_kv"),
        "xa_w_o": jnp.stack([g.reshape(N_DEV, 128, D_MODEL) for g in grads["xa_w_o"]], axis=1),
        "ffn_w_up": per_layer("ffn_w_up"),
        "ffn_w_down": jnp.stack([g.reshape(N_DEV, D_FF // N_DEV, D_MODEL) for g in grads["ffn_w_down"]], axis=1),
    }
    out_g, out_d, out_m, out_v = {}, {}, {}, {}
    for n in BIG:
        part = big_parts[n]
        rows, cols = part.shape[1] * part.shape[2], part.shape[3]
        recv = exchange("xch_" + n, part.reshape(N_DEV, rows, cols))
        res = adamw("adamw_" + n, master[n].reshape(rows, cols), mom1[n].reshape(rows, cols),
                    mom2[n].reshape(rows, cols), parts=recv)
        out_g[n], out_d[n], out_m[n], out_v[n] = (r.reshape(master[n].shape) for r in res)

    small_g = {
        "norm_mix": jnp.stack([g[0] for g in grads["norm_mix"]]),
        "norm_xattn": jnp.stack([g[0] for g in grads["norm_xattn"]]),
        "norm_ffn": jnp.stack([g[0] for g in grads["norm_ffn"]]),
        "norm_mem": g_norm_mem[0], "norm_final": g_norm_final[0],
        "pool_w": g_pool_w[None], "pool_scale": g_pool_scale,
        "ssm_lam_re": g_lam_re[None], "ssm_lam_im": g_lam_im[None], "ssm_log_dt": g_log_dt.reshape(1, 64),
        "ssm_b_re": g_b_re.reshape(1, 64, 64, 16), "ssm_b_im": g_b_im.reshape(1, 64, 64, 16),
        "ssm_c_re": g_c_re[None], "ssm_c_im": g_c_im[None],
        "ffn_conv_b": jnp.stack([g.reshape(2 * D_FF) for g in grads["ffn_conv_b"]]),
        "ssm_d": g_dskip,
        "ffn_conv_w": jnp.stack([g.transpose(1, 0, 2).reshape(3, 2 * D_FF) for g in grads["ffn_conv_w"]]),
    }
    sizes = [int(small_g[n].size) for n in SMALL]
    total = sum(sizes)
    rows8 = -(-total // (N_DEV * 128 * 8)) * 8
    flat = jnp.concatenate([small_g[n].reshape(-1).astype(F32) for n in SMALL]
                           + [jnp.zeros((N_DEV * rows8 * 128 - total,), F32)])
    recv = exchange("xch_small", flat.reshape(N_DEV, rows8, 128))
    summed = all_gather("ag_small", sum_parts("sum_small", recv), F32).reshape(-1)

    def local_part(name, a):
        ax = SMALL_SHARDED.get(name)
        if ax is None:
            return a
        n_loc = a.shape[ax] // N_DEV
        return lax.dynamic_slice_in_dim(a, me * n_loc, n_loc, axis=ax)

    sg, off = {}, 0
    for n, sz in zip(SMALL, sizes):
        sg[n] = local_part(n, summed[off:off + sz].reshape(small_g[n].shape))
        off += sz
    lsizes = [int(sg[n].size) for n in SMALL]
    ltotal = sum(lsizes)
    lrows = -(-ltotal // (128 * 16)) * 16

    def pack(d_):
        return jnp.concatenate([d_[n].reshape(-1) for n in SMALL] + [jnp.zeros((lrows * 128 - ltotal,), F32)]
                               ).reshape(lrows, 128)

    padv = jnp.concatenate([mom2[n].reshape(-1) for n in SMALL] + [jnp.ones((lrows * 128 - ltotal,), F32)]
                           ).reshape(lrows, 128)
    res = adamw("adamw_small", pack(master), pack(mom1), padv, g=pack(sg))
    off = 0
    for n, sz in zip(SMALL, lsizes):
        for dst, r in zip((out_g, out_d, out_m, out_v), res):
            dst[n] = r.reshape(-1)[off:off + sz].reshape(master[n].shape)
        off += sz

    return (loss, dx0.reshape(bsz, seq, d), *[out_g[n] for n in WEIGHTS], *[out_d[n] for n in WEIGHTS],
            *[out_m[n] for n in WEIGHTS], *[out_v[n] for n in WEIGHTS])
```

```python
import functools
import math

import jax
import jax.numpy as jnp
from jax import lax
from jax.experimental import pallas as pl
from jax.experimental.pallas import tpu as pltpu

F32 = jnp.float32
BF16 = jnp.bfloat16
MXU_DTYPE = jnp.bfloat16
N_DEV = 8
MESH_AXES = ("x", "y", "c")

D_MODEL = 1024
SB_HEAD_DIM = 64
SB_WIDTH = 512
SB_BLOCK = 256
POOL_WINDOWS = (2, 4, 8, 16)
POOL_GROUP = 128
POOL_HALO = 16
SSM_TILES = 8
SSM_TILE_STATES = 512
SSM_SLAB = 32
MEM_LEN = 256
XA_HEADS = 4
XA_HEAD_DIM = 256
D_FF = 2816
FF_SHARD = 704
EPS = 1e-6
ADAM_LR = 0.001
ADAM_B1 = 0.9
ADAM_B2 = 0.999
ADAM_EPS = 1e-08
ADAM_WD = 0.01
ADAM_STEP = 10
VMEM_LIMIT = 56 * 1024 * 1024

_NN = (((1,), (0,)), ((), ()))
_NT = (((1,), (1,)), ((), ()))
_TN = (((0,), (0,)), ((), ()))


def _params(sem=None):
    if sem is None:
        return pltpu.CompilerParams(vmem_limit_bytes=VMEM_LIMIT)
    return pltpu.CompilerParams(dimension_semantics=sem, vmem_limit_bytes=VMEM_LIMIT)


def _tile(n, pref, mult=8):
    if n <= pref:
        return n
    for t in range(pref, 0, -1):
        if n % t == 0 and t % mult == 0:
            return t
    return n


def _dot(a, b, dims):
    return lax.dot_general(a.astype(MXU_DTYPE), b.astype(MXU_DTYPE), dims, preferred_element_type=F32)


def _dot_exact01(x, m01, dims=_NN):
    x1 = x.astype(BF16)
    r1 = x - x1.astype(F32)
    x2 = r1.astype(BF16)
    x3 = (r1 - x2.astype(F32)).astype(BF16)
    m = m01.astype(BF16)
    out = lax.dot_general(x1, m, dims, preferred_element_type=F32)
    out = out + lax.dot_general(x2, m, dims, preferred_element_type=F32)
    return out + lax.dot_general(x3, m, dims, preferred_element_type=F32)


def _mm(name, a, b, dims, grid, a_spec, b_spec, o_spec, out_shape, out_dtype, acc_shape, res=None, r_spec=None,
        group=1, n=None, a_sel="full", b_sel="full", o_sel="full"):
    nk = grid[2]
    if out_dtype is None:
        out_dtype = BF16

    def at(sel, s):
        if sel == "lead":
            return (s,)
        if sel == "lanes":
            return (slice(None), slice(s * n, (s + 1) * n))
        return (Ellipsis,)

    def body(*refs):
        a_ref, b_ref = refs[0], refs[1]
        r_ref = refs[2] if res is not None else None
        o_ref = refs[3] if res is not None else refs[2]
        acc = refs[-1] if nk > 1 else None
        k = pl.program_id(2)

        def emit(s, val):
            if nk == 1:
                if r_ref is not None:
                    val = val + r_ref[...].astype(F32)
                o_ref[at(o_sel, s)] = val.astype(out_dtype)
                return

            @pl.when(k == 0)
            def _():
                acc[at(o_sel, s)] = val

            @pl.when(k > 0)
            def _():
                acc[at(o_sel, s)] += val

        total = None
        for s in range(group):
            val = _dot(a_ref[at(a_sel, s)], b_ref[at(b_sel, s)], dims)
            if o_sel == "full":
                total = val if total is None else total + val
            else:
                emit(s, val)
        if o_sel == "full":
            emit(0, total)
        if nk > 1:
            @pl.when(k == nk - 1)
            def _():
                r = acc[...]
                if r_ref is not None:
                    r = r + r_ref[...].astype(F32)
                o_ref[...] = r.astype(out_dtype)

    in_specs = [a_spec, b_spec] + ([] if res is None else [r_spec])
    args = (a, b) + (() if res is None else (res,))
    return pl.pallas_call(
        body, name=name, grid=grid, in_specs=in_specs, out_specs=o_spec,
        out_shape=jax.ShapeDtypeStruct(out_shape, out_dtype),
        scratch_shapes=[pltpu.VMEM(acc_shape, F32)] if nk > 1 else [],
        compiler_params=_params(("parallel", "parallel", "arbitrary")),
    )(*args)


def mm_nn(name, a, b, res=None, koff=0, out_dtype=None):
    m, k = a.shape
    n = b.shape[1]
    tm, tn, tk = _tile(m, 1024), _tile(n, 1024, 128), _tile(k, 1024, 128)
    kb = koff // tk
    spec = pl.BlockSpec((tm, tn), lambda i, j, kk: (i, j))
    return _mm(name, a, b, _NN, (m // tm, n // tn, k // tk),
               pl.BlockSpec((tm, tk), lambda i, j, kk: (i, kk)),
               pl.BlockSpec((tk, tn), lambda i, j, kk: (kk + kb, j)),
               spec, (m, n), out_dtype, (tm, tn), res, spec)


def mm_nn_bs(name, a, bs, stacked_out=False, out_dtype=None):
    m, k = a.shape
    s, _, n = bs.shape
    tm, tk = _tile(m, 1024), _tile(k, 1024, 128)
    a_spec = pl.BlockSpec((tm, tk), lambda i, j, kk: (i, kk))
    if stacked_out:
        return _mm(name, a, bs, _NN, (m // tm, s, k // tk), a_spec,
                   pl.BlockSpec((None, tk, n), lambda i, j, kk: (j, kk, 0)),
                   pl.BlockSpec((None, tm, n), lambda i, j, kk: (j, i, 0)), (s, m, n), out_dtype, (tm, n))
    g = _tile(s, max(1, 1024 // n), 1)
    return _mm(name, a, bs, _NN, (m // tm, s // g, k // tk), a_spec,
               pl.BlockSpec((g, tk, n), lambda i, j, kk: (j, kk, 0)),
               pl.BlockSpec((tm, g * n), lambda i, j, kk: (i, j)), (m, s * n), out_dtype, (tm, g * n),
               group=g, n=n, b_sel="lead", o_sel="lanes")


def mm_as_nn(name, a_st, b3, res, out_dtype=F32):
    s, m, kp = a_st.shape
    n = b3.shape[2]
    tm, tn = _tile(m, 1024), _tile(n, 1024, 128)
    spec = pl.BlockSpec((tm, tn), lambda i, j, kk: (i, j))
    return _mm(name, a_st, b3, _NN, (m // tm, n // tn, s),
               pl.BlockSpec((None, tm, kp), lambda i, j, kk: (kk, i, 0)),
               pl.BlockSpec((None, kp, tn), lambda i, j, kk: (kk, 0, j)),
               spec, (m, n), out_dtype, (tm, tn), res, spec)


def mm_nt(name, dc, b, out_dtype=None):
    m, n = dc.shape
    k = b.shape[0]
    tm, tko, tnr = _tile(m, 1024), _tile(k, 1024, 128), _tile(n, 1024, 128)
    return _mm(name, dc, b, _NT, (m // tm, k // tko, n // tnr),
               pl.BlockSpec((tm, tnr), lambda i, j, kk: (i, kk)),
               pl.BlockSpec((tko, tnr), lambda i, j, kk: (j, kk)),
               pl.BlockSpec((tm, tko), lambda i, j, kk: (i, j)), (m, k), out_dtype, (tm, tko))


def mm_nt_bs(name, dc, bs, dc_stacked=False, out_dtype=None):
    s, k, n = bs.shape
    m = dc.shape[1] if dc_stacked else dc.shape[0]
    tm, tko = _tile(m, 1024), _tile(k, 1024, 128)
    o_spec = pl.BlockSpec((tm, tko), lambda i, j, kk: (i, j))
    if dc_stacked:
        return _mm(name, dc, bs, _NT, (m // tm, k // tko, s),
                   pl.BlockSpec((None, tm, n), lambda i, j, kk: (kk, i, 0)),
                   pl.BlockSpec((None, tko, n), lambda i, j, kk: (kk, j, 0)), o_spec, (m, k), out_dtype, (tm, tko))
    g = _tile(s, max(1, 2048 // n), 1)
    return _mm(name, dc, bs, _NT, (m // tm, k // tko, s // g),
               pl.BlockSpec((tm, g * n), lambda i, j, kk: (i, kk)),
               pl.BlockSpec((g, tko, n), lambda i, j, kk: (kk, j, 0)), o_spec, (m, k), out_dtype, (tm, tko),
               group=g, n=n, a_sel="lanes", b_sel="lead")


def mm_nt_os(name, dc, b3, out_dtype=None):
    m, n = dc.shape
    s, kp, _ = b3.shape
    tm, tnr = _tile(m, 1024), _tile(n, 1024, 128)
    return _mm(name, dc, b3, _NT, (m // tm, s, n // tnr),
               pl.BlockSpec((tm, tnr), lambda i, j, kk: (i, kk)),
               pl.BlockSpec((None, kp, tnr), lambda i, j, kk: (j, 0, kk)),
               pl.BlockSpec((None, tm, kp), lambda i, j, kk: (j, i, 0)), (s, m, kp), out_dtype, (tm, kp))


def mm_tn(name, a, dc, a_stacked=False, dc_cols=None, dc_stacked=False, out_dtype=None):
    if a_stacked:
        s, m, kp = a.shape
        n = dc.shape[1]
        tno, tmr = _tile(n, 1024, 128), _tile(m, 1024)
        return _mm(name, a, dc, _TN, (s, n // tno, m // tmr),
                   pl.BlockSpec((None, tmr, kp), lambda i, j, kk: (i, kk, 0)),
                   pl.BlockSpec((tmr, tno), lambda i, j, kk: (kk, j)),
                   pl.BlockSpec((None, kp, tno), lambda i, j, kk: (i, 0, j)), (s, kp, n), out_dtype, (kp, tno))
    m, k = a.shape
    tko, tmr = _tile(k, 1024, 128), _tile(m, 1024)
    a_spec = pl.BlockSpec((tmr, tko), lambda i, j, kk: (kk, i))
    if dc_stacked:
        s, _, n = dc.shape
        return _mm(name, a, dc, _TN, (k // tko, s, m // tmr), a_spec,
                   pl.BlockSpec((None, tmr, n), lambda i, j, kk: (j, kk, 0)),
                   pl.BlockSpec((None, tko, n), lambda i, j, kk: (j, i, 0)), (s, k, n), out_dtype, (tko, n))
    if dc_cols is not None:
        n = dc_cols
        s = dc.shape[1] // n
        g = _tile(s, max(1, 1024 // n), 1)
        return _mm(name, a, dc, _TN, (k // tko, s // g, m // tmr), a_spec,
                   pl.BlockSpec((tmr, g * n), lambda i, j, kk: (kk, j)),
                   pl.BlockSpec((g, tko, n), lambda i, j, kk: (j, i, 0)), (s, k, n), out_dtype, (g, tko, n),
                   group=g, n=n, b_sel="lanes", o_sel="lead")
    n = dc.shape[1]
    tno = _tile(n, 1024, 128)
    return _mm(name, a, dc, _TN, (k // tko, n // tno, m // tmr), a_spec,
               pl.BlockSpec((tmr, tno), lambda i, j, kk: (kk, j)),
               pl.BlockSpec((tko, tno), lambda i, j, kk: (i, j)), (k, n), out_dtype, (tko, tno))


def rms_fwd(name, x, g):
    t, d = x.shape
    tr = _tile(t, 512)

    def body(x_ref, g_ref, o_ref):
        xf = x_ref[...]
        r = lax.rsqrt(jnp.mean(xf * xf, axis=-1, keepdims=True) + EPS)
        o_ref[...] = (xf * r * g_ref[...]).astype(o_ref.dtype)

    return pl.pallas_call(
        body, name=name, grid=(t // tr,),
        in_specs=[pl.BlockSpec((tr, d), lambda i: (i, 0)), pl.BlockSpec((1, d), lambda i: (0, 0))],
        out_specs=pl.BlockSpec((tr, d), lambda i: (i, 0)),
        out_shape=jax.ShapeDtypeStruct((t, d), BF16), compiler_params=_params(("parallel",)),
    )(x, g.reshape(1, d))


def rms_bwd(name, x, g, dh, dres=None, need_dx=True):
    t, d = x.shape
    tr = _tile(t, 512)

    def body(*refs):
        refs = list(refs)
        x_ref, g_ref, dh_ref = refs[:3]
        r_ref = refs[3] if dres is not None else None
        outs = refs[4:] if dres is not None else refs[3:]
        dx_ref, dg_ref = (outs[0], outs[1]) if need_dx else (None, outs[0])
        i = pl.program_id(0)

        @pl.when(i == 0)
        def _():
            dg_ref[...] = jnp.zeros_like(dg_ref)

        xf = x_ref[...]
        dhf = dh_ref[...].astype(F32)
        r = lax.rsqrt(jnp.mean(xf * xf, axis=-1, keepdims=True) + EPS)
        xh = xf * r
        dg_ref[...] += jnp.sum(dhf * xh, axis=0, keepdims=True)
        if need_dx:
            dxh = dhf * g_ref[...]
            dx = r * (dxh - xh * jnp.mean(dxh * xh, axis=-1, keepdims=True))
            if r_ref is not None:
                dx = dx + r_ref[...]
            dx_ref[...] = dx

    row = pl.BlockSpec((tr, d), lambda i: (i, 0))
    vec = pl.BlockSpec((1, d), lambda i: (0, 0))
    in_specs = [row, vec, row] + ([row] if dres is not None else [])
    args = (x, g.reshape(1, d), dh) + ((dres,) if dres is not None else ())
    out_specs = ([row] if need_dx else []) + [vec]
    out_shape = ([jax.ShapeDtypeStruct((t, d), F32)] if need_dx else []) + [jax.ShapeDtypeStruct((1, d), F32)]
    res = pl.pallas_call(
        body, name=name, grid=(t // tr,), in_specs=in_specs, out_specs=out_specs, out_shape=out_shape,
        compiler_params=_params(("arbitrary",)),
    )(*args)
    return res if need_dx else (None, res[0])


def loss_head(x, g, tgt):
    t, d = x.shape
    tr = _tile(t, 512)

    def body(x_ref, g_ref, t_ref, l_ref, dx_ref, dg_ref):
        i = pl.program_id(0)

        @pl.when(i == 0)
        def _():
            l_ref[...] = jnp.zeros_like(l_ref)
            dg_ref[...] = jnp.zeros_like(dg_ref)

        xf = x_ref[...]
        r = lax.rsqrt(jnp.mean(xf * xf, axis=-1, keepdims=True) + EPS)
        xh = xf * r
        diff = xh * g_ref[...] - t_ref[...]
        l_ref[...] += 0.5 * jnp.sum(jnp.mean(diff * diff, axis=-1, keepdims=True))
        dy = diff * (1.0 / d)
        dg_ref[...] += jnp.sum(dy * xh, axis=0, keepdims=True)
        dxh = dy * g_ref[...]
        dx_ref[...] = r * (dxh - xh * jnp.mean(dxh * xh, axis=-1, keepdims=True))

    row = pl.BlockSpec((tr, d), lambda i: (i, 0))
    vec = pl.BlockSpec((1, d), lambda i: (0, 0))
    return pl.pallas_call(
        body, name="loss_head", grid=(t // tr,), in_specs=[row, vec, row],
        out_specs=[pl.BlockSpec((1, 128), lambda i: (0, 0)), row, vec],
        out_shape=[jax.ShapeDtypeStruct((1, 128), F32), jax.ShapeDtypeStruct((t, d), F32),
                   jax.ShapeDtypeStruct((1, d), F32)],
        compiler_params=_params(("arbitrary",)),
    )(x, g.reshape(1, d), tgt)


def glu_fwd(glu, x):
    t, d = x.shape
    tr = _tile(t, 512)

    def body(v_ref, g_ref, x_ref, o_ref):
        o_ref[...] = x_ref[...] + v_ref[...] * jax.nn.sigmoid(g_ref[...])

    return pl.pallas_call(
        body, name="glu_fwd", grid=(t // tr,),
        in_specs=[pl.BlockSpec((tr, d), lambda i: (i, 0)), pl.BlockSpec((tr, d), lambda i: (i, 1)),
                  pl.BlockSpec((tr, d), lambda i: (i, 0))],
        out_specs=pl.BlockSpec((tr, d), lambda i: (i, 0)),
        out_shape=jax.ShapeDtypeStruct((t, d), F32), compiler_params=_params(("parallel",)),
    )(glu, glu, x)


def glu_bwd(glu, dmix):
    t, d = dmix.shape
    tr = _tile(t, 512)

    def body(v_ref, g_ref, d_ref, o_ref):
        sg = jax.nn.sigmoid(g_ref[...])
        dm = d_ref[...]
        o_ref[:, :d] = (dm * sg).astype(o_ref.dtype)
        o_ref[:, d:] = (dm * v_ref[...] * sg * (1.0 - sg)).astype(o_ref.dtype)

    return pl.pallas_call(
        body, name="glu_bwd", grid=(t // tr,),
        in_specs=[pl.BlockSpec((tr, d), lambda i: (i, 0)), pl.BlockSpec((tr, d), lambda i: (i, 1)),
                  pl.BlockSpec((tr, d), lambda i: (i, 0))],
        out_specs=pl.BlockSpec((tr, 2 * d), lambda i: (i, 0)),
        out_shape=jax.ShapeDtypeStruct((t, 2 * d), BF16), compiler_params=_params(("parallel",)),
    )(glu, glu, dmix)


def _log_sigmoid(z):
    return jnp.minimum(z, 0.0) - jnp.log(1.0 + jnp.exp(-jnp.abs(z)))


def _head_masks(shape):
    lane = lax.broadcasted_iota(jnp.int32, shape, 1)
    return lane < SB_HEAD_DIM


def _stack_heads(xf, is_a):
    return jnp.concatenate([jnp.where(is_a, xf, 0.0), jnp.where(is_a, 0.0, xf)], axis=0).astype(MXU_DTYPE)


def _diag_mask(qb):
    row = lax.broadcasted_iota(jnp.int32, (2 * qb, qb), 0) & (qb - 1)
    col = lax.broadcasted_iota(jnp.int32, (2 * qb, qb), 1)
    return col < row


def _tri01(qb, pred):
    j = lax.broadcasted_iota(jnp.int32, (qb, qb), 0)
    s = lax.broadcasted_iota(jnp.int32, (qb, qb), 1)
    m = pred(j, s).astype(BF16)
    return jnp.concatenate([m, m], axis=0)


def _split_cat(x):
    hi = x.astype(BF16)
    lo = (x - hi.astype(F32)).astype(BF16)
    return jnp.concatenate([hi, lo], axis=1)


def sb_attn_fwd(proj, bsz, seq):
    qb = SB_BLOCK
    nq = seq // qb
    npair = SB_WIDTH // 128
    scale = SB_HEAD_DIM ** -0.5

    def body(q_ref, k_ref, v_ref, o_ref, r_ref):
        qi = pl.program_id(2)
        is_a = _head_masks((qb, 128))
        q2 = _stack_heads(q_ref[...], is_a)
        diag = _diag_mask(qb)
        upper = _tri01(qb, lambda j, s: j > s)

        def block(kbi, acc, run, masked):
            ks = pl.ds(pl.multiple_of(kbi * qb, qb), qb)
            kblk = k_ref[ks, :].astype(MXU_DTYPE)
            vblk = v_ref[ks, :].astype(MXU_DTYPE)
            z = lax.dot_general(q2, kblk, _NT, preferred_element_type=F32) * scale
            lb = _log_sigmoid(z)
            lk = lb - z
            if masked:
                lk = jnp.where(diag, lk, 0.0)
            after = run + lax.dot_general(_split_cat(lk), upper, _NN, preferred_element_type=F32)
            w = jnp.exp(lb + after)
            if masked:
                w = jnp.where(diag, w, 0.0)
            acc = acc + lax.dot_general(w.astype(MXU_DTYPE), vblk, _NN, preferred_element_type=F32)
            return acc, run + jnp.sum(lk, axis=1, keepdims=True)

        carry = block(qi, jnp.zeros((2 * qb, 128), F32), jnp.zeros((2 * qb, 1), F32), True)
        acc, run = lax.fori_loop(0, qi, lambda i, c: block(qi - 1 - i, c[0], c[1], False), carry)
        o_ref[...] = jnp.where(is_a, acc[:qb], acc[qb:]).astype(o_ref.dtype)
        r_ref[...] = jnp.where(is_a, run[:qb], run[qb:])

    return pl.pallas_call(
        body, name="sb_attn_fwd", grid=(bsz, npair, nq),
        in_specs=[pl.BlockSpec((qb, 128), lambda b, p, i: (b * nq + i, p)),
                  pl.BlockSpec((seq, 128), lambda b, p, i: (b, npair + p)),
                  pl.BlockSpec((seq, 128), lambda b, p, i: (b, 2 * npair + p))],
        out_specs=[pl.BlockSpec((qb, 128), lambda b, p, i: (b * nq + i, p)),
                   pl.BlockSpec((qb, 128), lambda b, p, i: (b * nq + i, p))],
        out_shape=[jax.ShapeDtypeStruct((bsz * seq, SB_WIDTH), BF16),
                   jax.ShapeDtypeStruct((bsz * seq, SB_WIDTH), F32)],
        compiler_params=_params(("parallel", "parallel", "arbitrary")),
    )(proj, proj, proj)


def sb_attn_bwd(proj, rsum, dcat, bsz, seq):
    qb = SB_BLOCK
    nq = seq // qb
    npair = SB_WIDTH // 128
    scale = SB_HEAD_DIM ** -0.5

    def body(q_ref, k_ref, v_ref, r_ref, do_ref, dq_ref, dk_ref, dv_ref):
        qi = pl.program_id(2)

        @pl.when(qi == 0)
        def _():
            dk_ref[...] = jnp.zeros_like(dk_ref)
            dv_ref[...] = jnp.zeros_like(dv_ref)

        is_a = _head_masks((qb, 128))
        q2 = _stack_heads(q_ref[...], is_a)
        do2 = _stack_heads(do_ref[...].astype(F32), is_a)
        rf = r_ref[...]
        rtot = jnp.concatenate([rf[:, 0:1], rf[:, SB_HEAD_DIM:SB_HEAD_DIM + 1]], axis=0)
        diag = _diag_mask(qb)
        incl = _tri01(qb, lambda j, s: j <= s)
        strict = _tri01(qb, lambda j, s: j < s)

        def block(kbi, dq, pre, epre, masked):
            ks = pl.ds(pl.multiple_of(kbi * qb, qb), qb)
            kblk = k_ref[ks, :].astype(MXU_DTYPE)
            vblk = v_ref[ks, :].astype(MXU_DTYPE)
            z = lax.dot_general(q2, kblk, _NT, preferred_element_type=F32) * scale
            lb = _log_sigmoid(z)
            lk = lb - z
            if masked:
                lk = jnp.where(diag, lk, 0.0)
            after = rtot - (pre + lax.dot_general(_split_cat(lk), incl, _NN, preferred_element_type=F32))
            w = jnp.exp(lb + after)
            if masked:
                w = jnp.where(diag, w, 0.0)
            dw = lax.dot_general(do2, vblk, _NT, preferred_element_type=F32)
            e = dw * w
            ecum = epre + lax.dot_general(_split_cat(e), strict, _NN, preferred_element_type=F32)
            beta = jnp.exp(lb)
            dz = (e - beta * (e + ecum)) * scale
            if masked:
                dz = jnp.where(diag, dz, 0.0)
            dz = dz.astype(MXU_DTYPE)
            dq = dq + lax.dot_general(dz, kblk, _NN, preferred_element_type=F32)
            dk_ref[ks, :] += lax.dot_general(dz, q2, _TN, preferred_element_type=F32)
            dv_ref[ks, :] += lax.dot_general(w.astype(MXU_DTYPE), do2, _TN, preferred_element_type=F32)
            return dq, pre + jnp.sum(lk, axis=1, keepdims=True), epre + jnp.sum(e, axis=1, keepdims=True)

        zc = jnp.zeros((2 * qb, 1), F32)
        carry = lax.fori_loop(0, qi, lambda kbi, c: block(kbi, c[0], c[1], c[2], False),
                              (jnp.zeros((2 * qb, 128), F32), zc, zc))
        dq = block(qi, carry[0], carry[1], carry[2], True)[0]
        dq_ref[...] = jnp.where(is_a, dq[:qb], dq[qb:])

    full = jax.ShapeDtypeStruct((bsz * seq, SB_WIDTH), F32)
    qspec = pl.BlockSpec((qb, 128), lambda b, p, i: (b * nq + i, p))
    return pl.pallas_call(
        body, name="sb_attn_bwd", grid=(bsz, npair, nq),
        in_specs=[qspec,
                  pl.BlockSpec((seq, 128), lambda b, p, i: (b, npair + p)),
                  pl.BlockSpec((seq, 128), lambda b, p, i: (b, 2 * npair + p)),
                  qspec, qspec],
        out_specs=[qspec, pl.BlockSpec((seq, 128), lambda b, p, i: (b, p)),
                   pl.BlockSpec((seq, 128), lambda b, p, i: (b, p))],
        out_shape=[full, full, full],
        compiler_params=_params(("parallel", "parallel", "arbitrary")),
    )(proj, proj, proj, rsum, dcat)


def _window_sums(x, forward):
    n = x.shape[0]
    out = []
    s = x
    for sh in (1, 2, 4, 8):
        s = s + pltpu.roll(s, (n - sh) if forward else sh, 0)
        out.append(s)
    return out


def _pool_counts(tc, c, w):
    t = lax.broadcasted_iota(jnp.int32, (tc, 1), 0) + c * tc
    return jnp.minimum(t + 1, w).astype(F32)


def pool_fwd(proj, pool_w, pool_scale, bsz, seq):
    tc = _tile(seq, 512)
    nc = seq // tc
    hb = tc // POOL_HALO
    ucol = 3

    def body(u_ref, prev_ref, w_ref, s_ref, o_ref):
        c = pl.program_id(1)
        prev = jnp.where(c > 0, prev_ref[...], 0.0)
        x = jnp.concatenate([prev, u_ref[...]], axis=0)
        sums = _window_sums(x, forward=False)
        for g, win in enumerate(POOL_WINDOWS):
            ls = slice(g * POOL_GROUP, (g + 1) * POOL_GROUP)
            pooled = sums[g][POOL_HALO:, ls] / _pool_counts(tc, c, win) - x[POOL_HALO:, ls]
            y = _dot(pooled, w_ref[g], _NN)
            o_ref[:, ls] = (y * s_ref[:, ls]).astype(o_ref.dtype)

    return pl.pallas_call(
        body, name="pool_fwd", grid=(bsz, nc),
        in_specs=[pl.BlockSpec((tc, SB_WIDTH), lambda b, c: (b * nc + c, ucol)),
                  pl.BlockSpec((POOL_HALO, SB_WIDTH), lambda b, c: (jnp.maximum((b * nc + c) * hb - 1, 0), ucol)),
                  pl.BlockSpec((4, POOL_GROUP, POOL_GROUP), lambda b, c: (0, 0, 0)),
                  pl.BlockSpec((1, SB_WIDTH), lambda b, c: (0, 0))],
        out_specs=pl.BlockSpec((tc, SB_WIDTH), lambda b, c: (b * nc + c, 0)),
        out_shape=jax.ShapeDtypeStruct((bsz * seq, SB_WIDTH), BF16),
        compiler_params=_params(("parallel", "parallel")),
    )(proj, proj, pool_w, pool_scale)


def pool_bwd(proj, pool_w, pool_scale, dcat, bsz, seq):
    tc = _tile(seq, 512)
    nc = seq // tc
    hb = tc // POOL_HALO
    nblk = bsz * seq // POOL_HALO
    ucol = 3

    def body(u_ref, prev_ref, dy_ref, nxt_ref, w_ref, s_ref, du_ref, dw_ref, ds_ref):
        b, c = pl.program_id(0), pl.program_id(1)

        @pl.when((b == 0) & (c == 0))
        def _():
            dw_ref[...] = jnp.zeros_like(dw_ref)
            ds_ref[...] = jnp.zeros_like(ds_ref)

        prev = jnp.where(c > 0, prev_ref[...], 0.0)
        x = jnp.concatenate([prev, u_ref[...]], axis=0)
        sums = _window_sums(x, forward=False)
        nxt = jnp.where(c < nc - 1, nxt_ref[...].astype(F32), 0.0)
        dy = jnp.concatenate([dy_ref[...].astype(F32), nxt], axis=0)
        tq = lax.broadcasted_iota(jnp.int32, (tc + POOL_HALO, 1), 0) + c * tc
        for g, win in enumerate(POOL_WINDOWS):
            ls = slice(g * POOL_GROUP, (g + 1) * POOL_GROUP)
            pooled = sums[g][POOL_HALO:, ls] / _pool_counts(tc, c, win) - x[POOL_HALO:, ls]
            y = _dot(pooled, w_ref[g], _NN)
            ds_ref[:, ls] += jnp.sum(dy[:tc, ls] * y, axis=0, keepdims=True)
            dz = dy[:, ls] * s_ref[:, ls]
            dw_ref[g] += _dot(pooled, dz[:tc], _TN)
            dpool = _dot(dz, w_ref[g], _NT)
            dmean = dpool / jnp.minimum(tq + 1, win).astype(F32)
            fsum = _window_sums(dmean, forward=True)[g]
            du_ref[:, ls] = fsum[:tc] - dpool[:tc]

    return pl.pallas_call(
        body, name="pool_bwd", grid=(bsz, nc),
        in_specs=[pl.BlockSpec((tc, SB_WIDTH), lambda b, c: (b * nc + c, ucol)),
                  pl.BlockSpec((POOL_HALO, SB_WIDTH), lambda b, c: (jnp.maximum((b * nc + c) * hb - 1, 0), ucol)),
                  pl.BlockSpec((tc, SB_WIDTH), lambda b, c: (b * nc + c, 1)),
                  pl.BlockSpec((POOL_HALO, SB_WIDTH), lambda b, c: (jnp.minimum((b * nc + c + 1) * hb, nblk - 1), 1)),
                  pl.BlockSpec((4, POOL_GROUP, POOL_GROUP), lambda b, c: (0, 0, 0)),
                  pl.BlockSpec((1, SB_WIDTH), lambda b, c: (0, 0))],
        out_specs=[pl.BlockSpec((tc, SB_WIDTH), lambda b, c: (b * nc + c, 0)),
                   pl.BlockSpec((4, POOL_GROUP, POOL_GROUP), lambda b, c: (0, 0, 0)),
                   pl.BlockSpec((1, SB_WIDTH), lambda b, c: (0, 0))],
        out_shape=[jax.ShapeDtypeStruct((bsz * seq, SB_WIDTH), F32),
                   jax.ShapeDtypeStruct((4, POOL_GROUP, POOL_GROUP), F32),
                   jax.ShapeDtypeStruct((1, SB_WIDTH), F32)],
        compiler_params=_params(("arbitrary", "arbitrary")),
    )(proj, proj, dcat, dcat, pool_w, pool_scale)


def _lbar(lam_re, lam_im, log_dt):
    dt = jnp.exp(log_dt)
    mag = jnp.exp(lam_re * dt)
    ang = lam_im * dt
    return mag * jnp.cos(ang), mag * jnp.sin(ang)


def _bbar(lam_re, lam_im, log_dt, b_re, b_im):
    lb_re, lb_im = _lbar(lam_re, lam_im, log_dt)
    n_re = lb_re - 1.0
    den = lam_re * lam_re + lam_im * lam_im
    coef_re = (n_re * lam_re + lb_im * lam_im) / den
    coef_im = (lb_im * lam_re - n_re * lam_im) / den
    return coef_re * b_re - coef_im * b_im, coef_re * b_im + coef_im * b_re


def _expand01():
    p = lax.broadcasted_iota(jnp.int32, (64, 1024), 0)
    q = lax.broadcasted_iota(jnp.int32, (64, 1024), 1)
    return (lax.shift_right_logical(q, 4) == p).astype(BF16)


def ssm_prep(lam_re, lam_im, log_dt, b_re2, b_im2):
    def body(lr_ref, li_ref, dt_ref, br_ref, bi_ref, ar_ref, ai_ref, bbr_ref, bbi_ref):
        e = _expand01()
        lr, li, dt = lr_ref[...], li_ref[...], dt_ref[...]
        ar_ref[...], ai_ref[...] = _lbar(lr, li, dt)
        bbr_ref[...], bbi_ref[...] = _bbar(_dot_exact01(lr, e), _dot_exact01(li, e), dt, br_ref[...], bi_ref[...])

    s64 = jax.ShapeDtypeStruct((64, 64), F32)
    s1k = jax.ShapeDtypeStruct((64, 1024), F32)
    return pl.pallas_call(body, name="ssm_prep", out_shape=[s64, s64, s1k, s1k], compiler_params=_params())(
        lam_re, lam_im, log_dt, b_re2, b_im2)


def ssm_prep_bwd(lam_re, lam_im, log_dt, b_re2, b_im2, da_re, da_im, dbb_re, dbb_im):
    def body(lr_ref, li_ref, dt_ref, br_ref, bi_ref, dar_ref, dai_ref, dbr_ref, dbi_ref,
             olr_ref, oli_ref, odt_ref, obr_ref, obi_ref):
        e = _expand01()
        lr, li, dt = lr_ref[...], li_ref[...], dt_ref[...]
        _, vjp_a = jax.vjp(_lbar, lr, li, dt)
        g_lr, g_li, g_dt = vjp_a((dar_ref[...], dai_ref[...]))
        _, vjp_b = jax.vjp(_bbar, _dot_exact01(lr, e), _dot_exact01(li, e), dt, br_ref[...], bi_ref[...])
        x_lr, x_li, x_dt, g_br, g_bi = vjp_b((dbr_ref[...], dbi_ref[...]))
        olr_ref[...] = g_lr + _dot_exact01(x_lr, e, _NT)
        oli_ref[...] = g_li + _dot_exact01(x_li, e, _NT)
        odt_ref[...] = g_dt + x_dt
        obr_ref[...] = g_br
        obi_ref[...] = g_bi

    s64 = jax.ShapeDtypeStruct((64, 64), F32)
    s1k = jax.ShapeDtypeStruct((64, 1024), F32)
    return pl.pallas_call(body, name="ssm_prep_bwd",
                          out_shape=[s64, s64, jax.ShapeDtypeStruct((64, 1), F32), s1k, s1k],
                          compiler_params=_params())(
        lam_re, lam_im, log_dt, b_re2, b_im2, da_re, da_im, dbb_re, dbb_im)


def _gelu(y):
    c = math.sqrt(2.0 / math.pi)
    return 0.5 * y * (1.0 + jnp.tanh(c * (y + 0.044715 * y * y * y)))


def _gelu_grad(y):
    c = math.sqrt(2.0 / math.pi)
    th = jnp.tanh(c * (y + 0.044715 * y * y * y))
    return 0.5 * (1.0 + th) + 0.5 * y * (1.0 - th * th) * c * (1.0 + 3.0 * 0.044715 * y * y)


def _slab(t):
    return pl.ds(pl.multiple_of(t * SSM_SLAB, SSM_SLAB), SSM_SLAB)


def _st_store(ref, i, tc, val):
    for q in range(4):
        ref[pl.ds(4 * i + q, tc, stride=SSM_SLAB), :] = val[:, 128 * q:128 * (q + 1)]


def _st_load(ref, i, tc):
    return jnp.concatenate([ref[pl.ds(4 * i + q, tc, stride=SSM_SLAB), :] for q in range(4)], axis=1)


def ssm_fwd(u, wt, ct, a_re, a_im, dskip, bsz, seq):
    tc = _tile(seq, 256)
    nc = seq // tc
    ns = SSM_TILE_STATES

    def body(u_ref, wt_ref, ct_ref, ar_ref, ai_ref, d_ref, y_ref, gl_ref, hr_ref, hi_ref, sr_ref, si_ref):
        c = pl.program_id(1)

        @pl.when(c == 0)
        def _():
            sr_ref[...] = jnp.zeros_like(sr_ref)
            si_ref[...] = jnp.zeros_like(si_ref)

        uf = u_ref[...]
        for i in range(SSM_TILES):
            bu = _dot(uf[:, i * 128:(i + 1) * 128], wt_ref[i], _NN)
            _st_store(hr_ref, i, tc, bu[:, :ns])
            _st_store(hi_ref, i, tc, bu[:, ns:])
        ar, ai = ar_ref[...], ai_ref[...]

        def step(t, carry):
            sr, si = carry
            nr = ar * sr - ai * si + hr_ref[_slab(t), :]
            ni = ar * si + ai * sr + hi_ref[_slab(t), :]
            hr_ref[_slab(t), :] = nr
            hi_ref[_slab(t), :] = ni
            return nr, ni

        sr, si = lax.fori_loop(0, tc, step, (sr_ref[...], si_ref[...]), unroll=4)
        sr_ref[...] = sr
        si_ref[...] = si
        for i in range(SSM_TILES):
            hcat = jnp.concatenate([_st_load(hr_ref, i, tc), _st_load(hi_ref, i, tc)], axis=1)
            ls = slice(i * 128, (i + 1) * 128)
            y = _dot(hcat, ct_ref[i], _NN) + d_ref[:, ls] * uf[:, ls]
            y_ref[:, ls] = y
            gl_ref[:, ls] = _gelu(y).astype(gl_ref.dtype)

    t = bsz * seq
    row = pl.BlockSpec((tc, D_MODEL), lambda b, c: (b * nc + c, 0))
    st = pl.BlockSpec((tc * SSM_SLAB, 128), lambda b, c: (b * nc + c, 0))
    slab = pl.BlockSpec((SSM_SLAB, 128), lambda b, c: (0, 0))
    return pl.pallas_call(
        body, name="ssm_fwd", grid=(bsz, nc),
        in_specs=[row, pl.BlockSpec((SSM_TILES, 128, 2 * ns), lambda b, c: (0, 0, 0)),
                  pl.BlockSpec((SSM_TILES, 2 * ns, 128), lambda b, c: (0, 0, 0)), slab, slab,
                  pl.BlockSpec((1, D_MODEL), lambda b, c: (0, 0))],
        out_specs=[row, row, st, st],
        out_shape=[jax.ShapeDtypeStruct((t, D_MODEL), F32), jax.ShapeDtypeStruct((t, D_MODEL), BF16),
                   jax.ShapeDtypeStruct((t * SSM_SLAB, 128), F32), jax.ShapeDtypeStruct((t * SSM_SLAB, 128), F32)],
        scratch_shapes=[pltpu.VMEM((SSM_SLAB, 128), F32), pltpu.VMEM((SSM_SLAB, 128), F32)],
        compiler_params=_params(("parallel", "arbitrary")),
    )(u, wt, ct, a_re, a_im, dskip)


def ssm_bwd(dgl, y, u, h_re, h_im, wt, ct, a_re, a_im, dskip, bsz, seq):
    tc = _tile(seq, 256)
    nc = seq // tc
    ns = SSM_TILE_STATES

    def body(dgl_ref, y_ref, u_ref, hr_ref, hi_ref, pr_ref, pi_ref, wt_ref, ct_ref, ar_ref, ai_ref, d_ref,
             du_ref, dwt_ref, dct_ref, dd_ref, dar_ref, dai_ref, gr_ref, gi_ref, sr_ref, si_ref):
        b, c = pl.program_id(0), pl.program_id(1)

        @pl.when((b == 0) & (c == 0))
        def _():
            dwt_ref[...] = jnp.zeros_like(dwt_ref)
            dct_ref[...] = jnp.zeros_like(dct_ref)
            dd_ref[...] = jnp.zeros_like(dd_ref)
            dar_ref[...] = jnp.zeros_like(dar_ref)
            dai_ref[...] = jnp.zeros_like(dai_ref)

        @pl.when(c == 0)
        def _():
            sr_ref[...] = jnp.zeros_like(sr_ref)
            si_ref[...] = jnp.zeros_like(si_ref)

        uf = u_ref[...]
        dy = dgl_ref[...].astype(F32) * _gelu_grad(y_ref[...])
        dd_ref[...] += jnp.sum(dy * uf, axis=0, keepdims=True)
        for i in range(SSM_TILES):
            dyi = dy[:, i * 128:(i + 1) * 128]
            dh = _dot(dyi, ct_ref[i], _NT)
            _st_store(gr_ref, i, tc, dh[:, :ns])
            _st_store(gi_ref, i, tc, dh[:, ns:])
            hcat = jnp.concatenate([_st_load(hr_ref, i, tc), _st_load(hi_ref, i, tc)], axis=1)
            dct_ref[i] += _dot(hcat, dyi, _TN)
        ar, ai = ar_ref[...], ai_ref[...]

        def one(t, gr, gi, hpr, hpi, dar, dai):
            nr = gr_ref[_slab(t), :] + ar * gr + ai * gi
            ni = gi_ref[_slab(t), :] - ai * gr + ar * gi
            gr_ref[_slab(t), :] = nr
            gi_ref[_slab(t), :] = ni
            return nr, ni, dar + nr * hpr + ni * hpi, dai + ni * hpr - nr * hpi

        def step(j, carry):
            t = tc - 1 - j
            gr, gi, dar, dai = carry
            return one(t, gr, gi, hr_ref[_slab(t - 1), :], hi_ref[_slab(t - 1), :], dar, dai)

        carry = lax.fori_loop(0, tc - 1, step, (sr_ref[...], si_ref[...], dar_ref[...], dai_ref[...]), unroll=2)
        first = c == nc - 1
        hpr = jnp.where(first, 0.0, pr_ref[...])
        hpi = jnp.where(first, 0.0, pi_ref[...])
        gr, gi, dar, dai = one(0, *carry[:2], hpr, hpi, *carry[2:])
        sr_ref[...] = gr
        si_ref[...] = gi
        dar_ref[...] = dar
        dai_ref[...] = dai
        for i in range(SSM_TILES):
            ls = slice(i * 128, (i + 1) * 128)
            gcat = jnp.concatenate([_st_load(gr_ref, i, tc), _st_load(gi_ref, i, tc)], axis=1)
            du_ref[:, ls] = (_dot(gcat, wt_ref[i], _NT) + d_ref[:, ls] * dy[:, ls]).astype(du_ref.dtype)
            dwt_ref[i] += _dot(uf[:, ls], gcat, _TN)

    t = bsz * seq
    rev = lambda b, c: (b * nc + (nc - 1 - c), 0)
    row = pl.BlockSpec((tc, D_MODEL), rev)
    st = pl.BlockSpec((tc * SSM_SLAB, 128), rev)
    prev = pl.BlockSpec((SSM_SLAB, 128), lambda b, c: (jnp.maximum((b * nc + (nc - 1 - c)) * tc - 1, 0), 0))
    slab = pl.BlockSpec((SSM_SLAB, 128), lambda b, c: (0, 0))
    wts = pl.BlockSpec((SSM_TILES, 128, 2 * ns), lambda b, c: (0, 0, 0))
    cts = pl.BlockSpec((SSM_TILES, 2 * ns, 128), lambda b, c: (0, 0, 0))
    vec = pl.BlockSpec((1, D_MODEL), lambda b, c: (0, 0))
    return pl.pallas_call(
        body, name="ssm_bwd", grid=(bsz, nc),
        in_specs=[row, row, row, st, st, prev, prev, wts, cts, slab, slab, vec],
        out_specs=[row, wts, cts, vec, slab, slab],
        out_shape=[jax.ShapeDtypeStruct((t, D_MODEL), BF16),
                   jax.ShapeDtypeStruct((SSM_TILES, 128, 2 * ns), F32),
                   jax.ShapeDtypeStruct((SSM_TILES, 2 * ns, 128), F32),
                   jax.ShapeDtypeStruct((1, D_MODEL), F32),
                   jax.ShapeDtypeStruct((SSM_SLAB, 128), F32), jax.ShapeDtypeStruct((SSM_SLAB, 128), F32)],
        scratch_shapes=[pltpu.VMEM((tc * SSM_SLAB, 128), F32), pltpu.VMEM((tc * SSM_SLAB, 128), F32),
                        pltpu.VMEM((SSM_SLAB, 128), F32), pltpu.VMEM((SSM_SLAB, 128), F32)],
        compiler_params=_params(("arbitrary", "arbitrary")),
    )(dgl, y, u, h_re, h_im, h_re, h_im, wt, ct, a_re, a_im, dskip)


def _ssm_in_weights(bb_re2, bb_im2):
    eye = jnp.eye(8, dtype=F32)[None, :, None, :, None]

    def one(bb):
        t = bb.reshape(8, 8, 64, 16).transpose(0, 1, 3, 2)
        return (t[:, :, :, None, :] * eye).reshape(8, 128, 512)

    return jnp.concatenate([one(bb_re2), one(bb_im2)], axis=-1).astype(MXU_DTYPE)


def _ssm_in_weights_bwd(dwt):
    eye = jnp.eye(8, dtype=F32)[None, :, None, :, None]

    def one(d):
        t = (d.reshape(8, 8, 16, 8, 64) * eye).sum(axis=3)
        return t.transpose(0, 1, 3, 2).reshape(64, 1024)

    return one(dwt[..., :512]), one(dwt[..., 512:])


def _ssm_out_weights(c_re, c_im):
    eye = jnp.eye(8, dtype=F32)[None, :, None, :, None]

    def one(cc):
        t = cc.reshape(8, 8, 16, 64).transpose(0, 1, 3, 2)
        return (t[:, :, :, None, :] * eye).reshape(8, 512, 128)

    return jnp.concatenate([one(c_re), -one(c_im)], axis=1).astype(MXU_DTYPE)


def _ssm_out_weights_bwd(dct):
    eye = jnp.eye(8, dtype=F32)[None, :, None, :, None]

    def one(d):
        t = (d.reshape(8, 8, 64, 8, 16) * eye).sum(axis=3)
        return t.transpose(0, 1, 3, 2).reshape(64, 16, 64)

    return one(dct[:, :512]), -one(dct[:, 512:])


def _softmax(s):
    m = jnp.max(s, axis=-1, keepdims=True)
    e = jnp.exp(s - m)
    return e / jnp.sum(e, axis=-1, keepdims=True)


def xattn_fwd(q, kv, bsz, seq):
    tq = _tile(seq, 512)
    nq = seq // tq
    scale = XA_HEAD_DIM ** -0.5

    def body(q_ref, k_ref, v_ref, o_ref):
        s = lax.dot_general(q_ref[...], k_ref[...], _NT, preferred_element_type=F32) * scale
        p = _softmax(s)
        o_ref[...] = _dot(p, v_ref[...], _NN).astype(o_ref.dtype)

    qs = pl.BlockSpec((tq, XA_HEAD_DIM), lambda b, h, i: (b * nq + i, h))
    return pl.pallas_call(
        body, name="xattn_fwd", grid=(bsz, XA_HEADS, nq),
        in_specs=[qs, pl.BlockSpec((MEM_LEN, XA_HEAD_DIM), lambda b, h, i: (b, h)),
                  pl.BlockSpec((MEM_LEN, XA_HEAD_DIM), lambda b, h, i: (b, XA_HEADS + h))],
        out_specs=qs, out_shape=jax.ShapeDtypeStruct((bsz * seq, D_MODEL), BF16),
        compiler_params=_params(("parallel", "parallel", "parallel")),
    )(q, kv, kv)


def xattn_bwd(q, kv, do, bsz, seq):
    tq = _tile(seq, 512)
    nq = seq // tq
    scale = XA_HEAD_DIM ** -0.5

    def body(q_ref, k_ref, v_ref, do_ref, dq_ref, dk_ref, dv_ref):
        @pl.when(pl.program_id(2) == 0)
        def _():
            dk_ref[...] = jnp.zeros_like(dk_ref)
            dv_ref[...] = jnp.zeros_like(dv_ref)

        qv, kk, vv, dov = q_ref[...], k_ref[...], v_ref[...], do_ref[...]
        s = lax.dot_general(qv, kk, _NT, preferred_element_type=F32) * scale
        p = _softmax(s)
        dp = lax.dot_general(dov, vv, _NT, preferred_element_type=F32)
        ds = (p * (dp - jnp.sum(dp * p, axis=-1, keepdims=True)) * scale).astype(MXU_DTYPE)
        dq_ref[...] = lax.dot_general(ds, kk, _NN, preferred_element_type=F32).astype(dq_ref.dtype)
        dk_ref[...] += lax.dot_general(ds, qv, _TN, preferred_element_type=F32)
        dv_ref[...] += lax.dot_general(p.astype(MXU_DTYPE), dov, _TN, preferred_element_type=F32)

    qs = pl.BlockSpec((tq, XA_HEAD_DIM), lambda b, h, i: (b * nq + i, h))
    ks = pl.BlockSpec((MEM_LEN, XA_HEAD_DIM), lambda b, h, i: (b, h))
    vs = pl.BlockSpec((MEM_LEN, XA_HEAD_DIM), lambda b, h, i: (b, XA_HEADS + h))
    dkv = jax.ShapeDtypeStruct((bsz * MEM_LEN, D_MODEL), F32)
    dq, dk, dv = pl.pallas_call(
        body, name="xattn_bwd", grid=(bsz, XA_HEADS, nq),
        in_specs=[qs, ks, vs, qs], out_specs=[qs, ks, ks],
        out_shape=[jax.ShapeDtypeStruct((bsz * seq, D_MODEL), BF16), dkv, dkv],
        compiler_params=_params(("parallel", "parallel", "arbitrary")),
    )(q, kv, kv, do)
    return dq, dk, dv


CONV_HALO = 16


def _shift_down(x, prev, n):
    r = pltpu.roll(x, n, 0)
    row = lax.broadcasted_iota(jnp.int32, x.shape, 0)
    last = prev.shape[0]
    for k in range(n):
        r = jnp.where(row == k, prev[last - n + k:last - n + k + 1, :], r)
    return r


def _shift_up(x, nxt, n):
    rows = x.shape[0]
    r = pltpu.roll(x, rows - n, 0)
    row = lax.broadcasted_iota(jnp.int32, x.shape, 0)
    for k in range(n):
        r = jnp.where(row == rows - n + k, nxt[k:k + 1, :], r)
    return r


def _conv_taps(u, prev, w, b):
    return b + w[2:3] * u + w[1:2] * _shift_down(u, prev, 1) + w[0:1] * _shift_down(u, prev, 2)


def conv_fwd(up, cw, cb, bsz, seq):
    tc = _tile(seq, 512)
    nc = seq // tc
    hb = tc // CONV_HALO
    half = N_DEV // 2

    def body(uv_ref, ug_ref, pv_ref, pg_ref, wv_ref, wg_ref, bv_ref, bg_ref, o_ref):
        c = pl.program_id(2)
        pv = jnp.where(c > 0, pv_ref[...].astype(F32), 0.0)
        pg = jnp.where(c > 0, pg_ref[...].astype(F32), 0.0)
        val = _conv_taps(uv_ref[...].astype(F32), pv, wv_ref[...], bv_ref[...])
        gate = _conv_taps(ug_ref[...].astype(F32), pg, wg_ref[...], bg_ref[...])
        o_ref[...] = (gate * jax.nn.sigmoid(gate) * val).astype(o_ref.dtype)

    def cur(off):
        return pl.BlockSpec((None, tc, FF_SHARD), lambda b, j, c: (j + off, b * nc + c, 0))

    def prv(off):
        return pl.BlockSpec((None, CONV_HALO, FF_SHARD), lambda b, j, c: (j + off, jnp.maximum((b * nc + c) * hb - 1, 0), 0))

    def par(rows, off):
        return pl.BlockSpec((None, rows, FF_SHARD), lambda b, j, c: (j + off, 0, 0))

    return pl.pallas_call(
        body, name="conv_fwd", grid=(bsz, half, nc),
        in_specs=[cur(0), cur(half), prv(0), prv(half), par(3, 0), par(3, half), par(1, 0), par(1, half)],
        out_specs=cur(0), out_shape=jax.ShapeDtypeStruct((half, bsz * seq, FF_SHARD), BF16),
        compiler_params=_params(("parallel", "parallel", "parallel")),
    )(up, up, up, up, cw, cw, cb, cb)


def conv_bwd_taps(up, cw, cb, dact, bsz, seq):
    tc = _tile(seq, 512)
    nc = seq // tc
    hb = tc // CONV_HALO
    half = N_DEV // 2

    def body(uv_ref, ug_ref, pv_ref, pg_ref, wv_ref, wg_ref, bv_ref, bg_ref, da_ref,
             dc_ref, dwv_ref, dwg_ref, dbv_ref, dbg_ref):
        b, c = pl.program_id(1), pl.program_id(2)

        @pl.when((b == 0) & (c == 0))
        def _():
            for r in (dwv_ref, dwg_ref, dbv_ref, dbg_ref):
                r[...] = jnp.zeros_like(r)

        pv = jnp.where(c > 0, pv_ref[...].astype(F32), 0.0)
        pg = jnp.where(c > 0, pg_ref[...].astype(F32), 0.0)
        uv, ug = uv_ref[...].astype(F32), ug_ref[...].astype(F32)
        val = _conv_taps(uv, pv, wv_ref[...], bv_ref[...])
        gate = _conv_taps(ug, pg, wg_ref[...], bg_ref[...])
        sg = jax.nn.sigmoid(gate)
        da = da_ref[...].astype(F32)
        dval = da * gate * sg
        dgate = da * val * sg * (1.0 + gate * (1.0 - sg))
        dc_ref[0] = dval.astype(dc_ref.dtype)
        dc_ref[1] = dgate.astype(dc_ref.dtype)
        for dcv, u, prev, dw_ref, db_ref in ((dval, uv, pv, dwv_ref, dbv_ref), (dgate, ug, pg, dwg_ref, dbg_ref)):
            db_ref[...] += jnp.sum(dcv, axis=0, keepdims=True)
            dw_ref[2:3, :] += jnp.sum(dcv * u, axis=0, keepdims=True)
            dw_ref[1:2, :] += jnp.sum(dcv * _shift_down(u, prev, 1), axis=0, keepdims=True)
            dw_ref[0:1, :] += jnp.sum(dcv * _shift_down(u, prev, 2), axis=0, keepdims=True)

    def cur(off):
        return pl.BlockSpec((None, tc, FF_SHARD), lambda j, b, c: (j + off, b * nc + c, 0))

    def prv(off):
        return pl.BlockSpec((None, CONV_HALO, FF_SHARD), lambda j, b, c: (j + off, jnp.maximum((b * nc + c) * hb - 1, 0), 0))

    def par(rows, off):
        return pl.BlockSpec((None, rows, FF_SHARD), lambda j, b, c: (j + off, 0, 0))

    t = bsz * seq
    hs = jax.ShapeDtypeStruct((2, half, t, FF_SHARD), BF16)
    ws = jax.ShapeDtypeStruct((half, 3, FF_SHARD), F32)
    bs = jax.ShapeDtypeStruct((half, 1, FF_SHARD), F32)
    dc, dwv, dwg, dbv, dbg = pl.pallas_call(
        body, name="conv_bwd_taps", grid=(half, bsz, nc),
        in_specs=[cur(0), cur(half), prv(0), prv(half), par(3, 0), par(3, half), par(1, 0), par(1, half), cur(0)],
        out_specs=[pl.BlockSpec((2, None, tc, FF_SHARD), lambda j, b, c: (0, j, b * nc + c, 0)),
                   par(3, 0), par(3, 0), par(1, 0), par(1, 0)],
        out_shape=[hs, ws, ws, bs, bs],
        compiler_params=_params(("parallel", "arbitrary", "arbitrary")),
    )(up, up, up, up, cw, cw, cb, cb, dact)
    return (dc.reshape(N_DEV, t, FF_SHARD), jnp.concatenate([dwv, dwg], axis=0),
            jnp.concatenate([dbv, dbg], axis=0))


def conv_bwd_input(dconv, cw, bsz, seq):
    tc = _tile(seq, 512)
    nc = seq // tc
    hb = tc // CONV_HALO
    nblk = bsz * seq // CONV_HALO

    def body(d_ref, n_ref, w_ref, o_ref):
        c = pl.program_id(2)
        nxt = jnp.where(c < nc - 1, n_ref[...].astype(F32), 0.0)
        d = d_ref[...].astype(F32)
        w = w_ref[...]
        o_ref[...] = (w[2:3] * d + w[1:2] * _shift_up(d, nxt, 1) + w[0:1] * _shift_up(d, nxt, 2)).astype(o_ref.dtype)

    cur = pl.BlockSpec((None, tc, FF_SHARD), lambda j, b, c: (j, b * nc + c, 0))
    return pl.pallas_call(
        body, name="conv_bwd_input", grid=(N_DEV, bsz, nc),
        in_specs=[cur, pl.BlockSpec((None, CONV_HALO, FF_SHARD),
                                    lambda j, b, c: (j, jnp.minimum((b * nc + c + 1) * hb, nblk - 1), 0)),
                  pl.BlockSpec((None, 3, FF_SHARD), lambda j, b, c: (j, 0, 0))],
        out_specs=cur, out_shape=jax.ShapeDtypeStruct(dconv.shape, BF16),
        compiler_params=_params(("parallel", "parallel", "parallel")),
    )(dconv, dconv, cw)


def _my_index():
    return 4 * lax.axis_index("x") + 2 * lax.axis_index("y") + lax.axis_index("c")


def _peer(k):
    return (lax.axis_index("x") ^ ((k >> 2) & 1), lax.axis_index("y") ^ ((k >> 1) & 1),
            lax.axis_index("c") ^ (k & 1))


def all_gather(name, a, out_dtype):
    def body(a_ref, o_ref, stage, send_sems, recv_sems, local_sem):
        me = _my_index()
        stage[...] = a_ref[...].astype(out_dtype)
        local = pltpu.make_async_copy(stage, o_ref.at[me], local_sem)
        local.start()
        sends = []
        for k in range(1, N_DEV):
            cp = pltpu.make_async_remote_copy(
                src_ref=stage, dst_ref=o_ref.at[me], send_sem=send_sems.at[k - 1], recv_sem=recv_sems.at[k - 1],
                device_id=_peer(k), device_id_type=pl.DeviceIdType.MESH)
            cp.start()
            sends.append(cp)
        for k in range(1, N_DEV):
            pltpu.make_async_remote_copy(
                src_ref=stage, dst_ref=o_ref.at[me ^ k], send_sem=send_sems.at[k - 1], recv_sem=recv_sems.at[k - 1],
                device_id=_peer(k), device_id_type=pl.DeviceIdType.MESH).wait_recv()
        for cp in sends:
            cp.wait_send()
        local.wait()

    return pl.pallas_call(
        body, name=name, in_specs=[pl.BlockSpec(memory_space=pltpu.VMEM)],
        out_specs=pl.BlockSpec(memory_space=pltpu.HBM),
        out_shape=jax.ShapeDtypeStruct((N_DEV,) + a.shape, out_dtype),
        scratch_shapes=[pltpu.VMEM(a.shape, out_dtype), pltpu.SemaphoreType.DMA((N_DEV - 1,)),
                        pltpu.SemaphoreType.DMA((N_DEV - 1,)), pltpu.SemaphoreType.DMA],
        compiler_params=pltpu.CompilerParams(vmem_limit_bytes=VMEM_LIMIT),
    )(a)


def exchange(name, g):
    def body(g_ref, r_ref, send_sems, recv_sems, local_sem):
        me = _my_index()
        local = pltpu.make_async_copy(g_ref.at[me], r_ref.at[me], local_sem)
        local.start()
        sends = []
        for k in range(1, N_DEV):
            cp = pltpu.make_async_remote_copy(
                src_ref=g_ref.at[me ^ k], dst_ref=r_ref.at[me], send_sem=send_sems.at[k - 1],
                recv_sem=recv_sems.at[k - 1], device_id=_peer(k), device_id_type=pl.DeviceIdType.MESH)
            cp.start()
            sends.append(cp)
        for k in range(1, N_DEV):
            pltpu.make_async_remote_copy(
                src_ref=g_ref.at[me], dst_ref=r_ref.at[me ^ k], send_sem=send_sems.at[k - 1],
                recv_sem=recv_sems.at[k - 1], device_id=_peer(k), device_id_type=pl.DeviceIdType.MESH).wait_recv()
        for cp in sends:
            cp.wait_send()
        local.wait()

    return pl.pallas_call(
        body, name=name, in_specs=[pl.BlockSpec(memory_space=pltpu.HBM)],
        out_specs=pl.BlockSpec(memory_space=pltpu.HBM),
        out_shape=jax.ShapeDtypeStruct(g.shape, g.dtype),
        scratch_shapes=[pltpu.SemaphoreType.DMA((N_DEV - 1,)), pltpu.SemaphoreType.DMA((N_DEV - 1,)),
                        pltpu.SemaphoreType.DMA],
    )(g)


def sum_parts(name, r):
    _, rows, cols = r.shape

    def body(r_ref, o_ref):
        acc = r_ref[0].astype(F32)
        for s in range(1, N_DEV):
            acc = acc + r_ref[s].astype(F32)
        o_ref[...] = acc

    return pl.pallas_call(body, name=name, out_shape=jax.ShapeDtypeStruct((rows, cols), F32),
                          compiler_params=_params())(r)


def adamw(name, w, m, v, parts=None, g=None):
    rows, cols = w.shape
    br = _tile(rows, 256, 16)
    c1 = 1.0 / (1.0 - ADAM_B1 ** ADAM_STEP)
    c2 = 1.0 / (1.0 - ADAM_B2 ** ADAM_STEP)

    def body(g_ref, w_ref, m_ref, v_ref, og_ref, od_ref, om_ref, ov_ref):
        if parts is None:
            gs = g_ref[...]
        else:
            gs = g_ref[0].astype(F32)
            for s in range(1, N_DEV):
                gs = gs + g_ref[s].astype(F32)
        mn = ADAM_B1 * m_ref[...] + (1.0 - ADAM_B1) * gs
        vn = ADAM_B2 * v_ref[...] + (1.0 - ADAM_B2) * (gs * gs)
        og_ref[...] = gs
        om_ref[...] = mn
        ov_ref[...] = vn
        od_ref[...] = -ADAM_LR * ((mn * c1) / (jnp.sqrt(vn * c2) + ADAM_EPS) + ADAM_WD * w_ref[...])

    blk = pl.BlockSpec((br, cols), lambda i: (i, 0))
    gspec = blk if parts is None else pl.BlockSpec((N_DEV, br, cols), lambda i: (0, i, 0))
    shp = jax.ShapeDtypeStruct((rows, cols), F32)
    return pl.pallas_call(
        body, name=name, grid=(rows // br,), in_specs=[gspec, blk, blk, blk], out_specs=[blk] * 4,
        out_shape=[shp] * 4, compiler_params=_params(("parallel",)),
    )(g if parts is None else parts, w, m, v)


SMALL = ("norm_mix", "norm_xattn", "norm_ffn", "norm_mem", "norm_final", "pool_w", "pool_scale",
         "ssm_lam_re", "ssm_lam_im", "ssm_log_dt", "ssm_b_re", "ssm_b_im", "ssm_c_re", "ssm_c_im",
         "ffn_conv_b", "ssm_d", "ffn_conv_w")
SMALL_SHARDED = {"ssm_d": 1, "ffn_conv_w": 2}
BIG = ("ab_w_in", "ab_w_out", "ssm_w_in", "ssm_w_glu", "xa_w_q", "xa_w_kv", "xa_w_o", "ffn_w_up", "ffn_w_down")
WEIGHTS = ("norm_mix", "norm_xattn", "norm_ffn", "norm_mem", "norm_final", "ab_w_in", "pool_w", "pool_scale",
           "ab_w_out", "ssm_w_in", "ssm_lam_re", "ssm_lam_im", "ssm_log_dt", "ssm_b_re", "ssm_b_im", "ssm_c_re",
           "ssm_c_im", "ssm_d", "ssm_w_glu", "xa_w_q", "xa_w_kv", "xa_w_o", "ffn_w_up", "ffn_conv_w", "ffn_conv_b",
           "ffn_w_down")


def _layer_tail(l, x_in, mem_n, w, acts):
    bsz, seq = acts["bsz"], acts["seq"]
    hq = rms_fwd(f"rms_xattn{l}", x_in, w["norm_xattn"][l])
    q = mm_nn(f"xa_q{l}", hq, w["xa_w_q"][l])
    kv = mm_nn_bs(f"xa_kv{l}", mem_n, w["xa_w_kv"][l])
    o = xattn_fwd(q, kv, bsz, seq)
    x_mid = mm_nn(f"xa_o{l}", o, w["xa_w_o"][l], res=x_in, out_dtype=F32)
    hf = rms_fwd(f"rms_ffn{l}", x_mid, w["norm_ffn"][l])
    up = mm_nn_bs(f"ffn_up{l}", hf, w["ffn_w_up"][l], stacked_out=True)
    act = conv_fwd(up, w["ffn_conv_w"][l], w["ffn_conv_b"][l], bsz, seq)
    x_out = mm_as_nn(f"ffn_down{l}", act, w["ffn_w_down"][l], res=x_mid)
    acts[l].update(x_in=x_in, hq=hq, q=q, kv=kv, o=o, x_mid=x_mid, hf=hf, up=up, act=act)
    return x_out


def _layer_tail_bwd(l, dx, mem_n, w, acts, grads):
    a = acts[l]
    bsz, seq = acts["bsz"], acts["seq"]
    dact = mm_nt_os(f"d_act{l}", dx, w["ffn_w_down"][l])
    grads["ffn_w_down"][l] = mm_tn(f"g_ffn_down{l}", a["act"], dx, a_stacked=True)
    dconv, dcw, dcb = conv_bwd_taps(a["up"], w["ffn_conv_w"][l], w["ffn_conv_b"][l], dact, bsz, seq)
    grads["ffn_conv_w"][l] = dcw
    grads["ffn_conv_b"][l] = dcb
    dup = conv_bwd_input(dconv, w["ffn_conv_w"][l], bsz, seq)
    dhf = mm_nt_bs(f"d_hf{l}", dup, w["ffn_w_up"][l], dc_stacked=True)
    grads["ffn_w_up"][l] = mm_tn(f"g_ffn_up{l}", a["hf"], dup, dc_stacked=True)
    dx_mid, grads["norm_ffn"][l] = rms_bwd(f"rms_ffn_bwd{l}", a["x_mid"], w["norm_ffn"][l], dhf, dres=dx)
    do = mm_nt(f"d_o{l}", dx_mid, w["xa_w_o"][l])
    grads["xa_w_o"][l] = mm_tn(f"g_xa_o{l}", a["o"], dx_mid)
    dq, dk, dv = xattn_bwd(a["q"], a["kv"], do, bsz, seq)
    dkv = jnp.concatenate([dk, dv], axis=1).astype(BF16)
    dhq = mm_nt(f"d_hq{l}", dq, w["xa_w_q"][l])
    grads["xa_w_q"][l] = mm_tn(f"g_xa_q{l}", a["hq"], dq)
    dmem_n = mm_nt_bs(f"d_memn{l}", dkv, w["xa_w_kv"][l], out_dtype=F32)
    grads["xa_w_kv"][l] = mm_tn(f"g_xa_kv{l}", mem_n, dkv, dc_cols=2 * D_MODEL // N_DEV)
    dx_in, grads["norm_xattn"][l] = rms_bwd(f"rms_xattn_bwd{l}", a["x_in"], w["norm_xattn"][l], dhq, dres=dx_mid)
    return dx_in, dmem_n


def kernel(x, mem, norm_mix, norm_xattn, norm_ffn, norm_mem, norm_final, ab_w_in, pool_w, pool_scale, ab_w_out, ssm_w_in, ssm_lam_re, ssm_lam_im, ssm_log_dt, ssm_b_re, ssm_b_im, ssm_c_re, ssm_c_im, ssm_d, ssm_w_glu, xa_w_q, xa_w_kv, xa_w_o, ffn_w_up, ffn_conv_w, ffn_conv_b, ffn_w_down, loss_target, m_norm_mix, m_norm_xattn, m_norm_ffn, m_norm_mem, m_norm_final, m_ab_w_in, m_pool_w, m_pool_scale, m_ab_w_out, m_ssm_w_in, m_ssm_lam_re, m_ssm_lam_im, m_ssm_log_dt, m_ssm_b_re, m_ssm_b_im, m_ssm_c_re, m_ssm_c_im, m_ssm_d, m_ssm_w_glu, m_xa_w_q, m_xa_w_kv, m_xa_w_o, m_ffn_w_up, m_ffn_conv_w, m_ffn_conv_b, m_ffn_w_down, v_norm_mix, v_norm_xattn, v_norm_ffn, v_norm_mem, v_norm_final, v_ab_w_in, v_pool_w, v_pool_scale, v_ab_w_out, v_ssm_w_in, v_ssm_lam_re, v_ssm_lam_im, v_ssm_log_dt, v_ssm_b_re, v_ssm_b_im, v_ssm_c_re, v_ssm_c_im, v_ssm_d, v_ssm_w_glu, v_xa_w_q, v_xa_w_kv, v_xa_w_o, v_ffn_w_up, v_ffn_conv_w, v_ffn_conv_b, v_ffn_w_down):
    given = dict(locals())
    master = {n: given[n] for n in WEIGHTS}
    mom1 = {n: given["m_" + n] for n in WEIGHTS}
    mom2 = {n: given["v_" + n] for n in WEIGHTS}
    bsz, seq, d = x.shape
    t = bsz * seq
    me = _my_index()

    def gather(name, l):
        return all_gather(f"ag_{name}{l}", master[name][l], MXU_DTYPE)

    def square(a):
        return a.reshape(D_MODEL, D_MODEL)

    conv_w_st = all_gather("ag_ffn_conv_w", ffn_conv_w, F32)
    dskip = all_gather("ag_ssm_d", ssm_d.reshape(1, 128), F32).reshape(1, D_MODEL)
    w = {
        "norm_mix": norm_mix, "norm_xattn": norm_xattn, "norm_ffn": norm_ffn,
        "ab_w_in": gather("ab_w_in", 0),
        "ab_w_out": square(gather("ab_w_out", 0)),
        "ssm_w_in": square(gather("ssm_w_in", 0)),
        "ssm_w_glu": gather("ssm_w_glu", 0),
        "xa_w_q": [square(gather("xa_w_q", l)) for l in range(2)],
        "xa_w_kv": [gather("xa_w_kv", l) for l in range(2)],
        "xa_w_o": [square(gather("xa_w_o", l)) for l in range(2)],
        "ffn_w_up": [gather("ffn_w_up", l) for l in range(2)],
        "ffn_w_down": [gather("ffn_w_down", l).reshape(4, FF_SHARD, D_MODEL) for l in range(2)],
        "ffn_conv_w": [conv_w_st[:, l] for l in range(2)],
        "ffn_conv_b": [ffn_conv_b[l].reshape(N_DEV, 1, FF_SHARD) for l in range(2)],
    }

    acts = {"bsz": bsz, "seq": seq, 0: {}, 1: {}}
    x0 = x.reshape(t, d)
    mem2 = mem.reshape(bsz * MEM_LEN, d)
    mem_n = rms_fwd("rms_mem", mem2, norm_mem)
    pscale = pool_scale.reshape(1, SB_WIDTH)

    h0 = rms_fwd("rms_mix0", x0, norm_mix[0])
    proj = mm_nn_bs("ab_in", h0, w["ab_w_in"], out_dtype=F32)
    a_out, rsum = sb_attn_fwd(proj, bsz, seq)
    p_out = pool_fwd(proj, pool_w[0], pscale, bsz, seq)
    x1 = mm_nn("ab_out_a", a_out, w["ab_w_out"], res=x0, out_dtype=F32)
    x1 = mm_nn("ab_out_p", p_out, w["ab_w_out"], res=x1, koff=SB_WIDTH, out_dtype=F32)
    x3 = _layer_tail(0, x1, mem_n, w, acts)

    b_re2 = ssm_b_re.reshape(64, 1024)
    b_im2 = ssm_b_im.reshape(64, 1024)
    log_dt = ssm_log_dt.reshape(64, 1)
    lb_re, lb_im, bb_re2, bb_im2 = ssm_prep(ssm_lam_re[0], ssm_lam_im[0], log_dt, b_re2, b_im2)
    wt = _ssm_in_weights(bb_re2, bb_im2)
    ct = _ssm_out_weights(ssm_c_re[0], ssm_c_im[0])
    a_re = lb_re.reshape(SSM_SLAB, 128)
    a_im = lb_im.reshape(SSM_SLAB, 128)
    h1 = rms_fwd("rms_mix1", x3, norm_mix[1])
    u = mm_nn("ssm_in", h1, w["ssm_w_in"], out_dtype=F32)
    y, gl, h_re, h_im = ssm_fwd(u, wt, ct, a_re, a_im, dskip, bsz, seq)
    glu = mm_nn_bs("ssm_glu", gl, w["ssm_w_glu"], out_dtype=F32)
    x4 = glu_fwd(glu, x3)
    x6 = _layer_tail(1, x4, mem_n, w, acts)

    loss_row, dx, g_norm_final = loss_head(x6, norm_final, loss_target.reshape(t, d))
    loss = lax.psum(loss_row[0, 0], MESH_AXES)

    grads = {n: [None, None] for n in ("ffn_w_down", "ffn_conv_w", "ffn_conv_b", "ffn_w_up", "norm_ffn", "xa_w_o",
                                       "xa_w_q", "xa_w_kv", "norm_xattn", "norm_mix")}
    dx4, dmem_1 = _layer_tail_bwd(1, dx, mem_n, w, acts, grads)
    dglu = glu_bwd(glu, dx4)
    dgl = mm_nt_bs("d_gl", dglu, w["ssm_w_glu"])
    g_ssm_w_glu = mm_tn("g_ssm_glu", gl, dglu, dc_cols=2 * D_MODEL // N_DEV)
    du, dwt, dct, g_dskip, da_re, da_im = ssm_bwd(dgl, y, u, h_re, h_im, wt, ct, a_re, a_im, dskip, bsz, seq)
    dbb_re, dbb_im = _ssm_in_weights_bwd(dwt)
    g_c_re, g_c_im = _ssm_out_weights_bwd(dct)
    g_lam_re, g_lam_im, g_log_dt, g_b_re, g_b_im = ssm_prep_bwd(
        ssm_lam_re[0], ssm_lam_im[0], log_dt, b_re2, b_im2, da_re.reshape(64, 64), da_im.reshape(64, 64),
        dbb_re, dbb_im)
    dh1 = mm_nt("d_h1", du, w["ssm_w_in"])
    g_ssm_w_in = mm_tn("g_ssm_in", h1, du)
    dx3, grads["norm_mix"][1] = rms_bwd("rms_mix1_bwd", x3, norm_mix[1], dh1, dres=dx4)

    dx1, dmem_0 = _layer_tail_bwd(0, dx3, mem_n, w, acts, grads)
    dcat = mm_nt("d_cat", dx1, w["ab_w_out"])
    g_ab_w_out = jnp.concatenate([mm_tn("g_ab_out_a", a_out, dx1), mm_tn("g_ab_out_p", p_out, dx1)], axis=0)
    dq, dk, dv = sb_attn_bwd(proj, rsum, dcat, bsz, seq)
    dpu, g_pool_w, g_pool_scale = pool_bwd(proj, pool_w[0], pscale, dcat, bsz, seq)
    dproj = jnp.concatenate([dq, dk, dv, dpu], axis=1).astype(BF16)
    dh0 = mm_nt_bs("d_h0", dproj, w["ab_w_in"])
    g_ab_w_in = mm_tn("g_ab_in", h0, dproj, dc_cols=2 * D_MODEL // N_DEV)
    dx0, grads["norm_mix"][0] = rms_bwd("rms_mix0_bwd", x0, norm_mix[0], dh0, dres=dx1)
    _, g_norm_mem = rms_bwd("rms_mem_bwd", mem2, norm_mem, dmem_0 + dmem_1, need_dx=False)

    def rows8(g):
        return g.reshape(N_DEV, g.size // (N_DEV * D_MODEL), D_MODEL)

    big_items = []
    for l in (1, 0):
        big_items += [("ffn_w_down", l, rows8(grads["ffn_w_down"][l])), ("ffn_w_up", l, grads["ffn_w_up"][l]),
                      ("xa_w_o", l, rows8(grads["xa_w_o"][l])), ("xa_w_q", l, rows8(grads["xa_w_q"][l])),
                      ("xa_w_kv", l, grads["xa_w_kv"][l])]
        if l == 1:
            big_items += [("ssm_w_glu", 0, g_ssm_w_glu), ("ssm_w_in", 0, rows8(g_ssm_w_in))]
    big_items += [("ab_w_out", 0, rows8(g_ab_w_out)), ("ab_w_in", 0, g_ab_w_in)]
    per_layer = {n: [[None] * master[n].shape[0] for _ in range(4)] for n in BIG}
    for n, l, part in big_items:
        rows, cols = part.shape[1:]
        recv = exchange(f"xch_{n}{l}", part)
        res = adamw(f"adamw_{n}{l}", master[n][l].reshape(rows, cols), mom1[n][l].reshape(rows, cols),
                    mom2[n][l].reshape(rows, cols), parts=recv)
        for slot, r in zip(per_layer[n], res):
            slot[l] = r.reshape(master[n].shape[1:])
    out_g, out_d, out_m, out_v = ({n: jnp.stack(per_layer[n][k]) for n in BIG} for k in range(4))

    small_g = {
        "norm_mix": jnp.stack([g[0] for g in grads["norm_mix"]]),
        "norm_xattn": jnp.stack([g[0] for g in grads["norm_xattn"]]),
        "norm_ffn": jnp.stack([g[0] for g in grads["norm_ffn"]]),
        "norm_mem": g_norm_mem[0], "norm_final": g_norm_final[0],
        "pool_w": g_pool_w[None], "pool_scale": g_pool_scale,
        "ssm_lam_re": g_lam_re[None], "ssm_lam_im": g_lam_im[None], "ssm_log_dt": g_log_dt.reshape(1, 64),
        "ssm_b_re": g_b_re.reshape(1, 64, 64, 16), "ssm_b_im": g_b_im.reshape(1, 64, 64, 16),
        "ssm_c_re": g_c_re[None], "ssm_c_im": g_c_im[None],
        "ffn_conv_b": jnp.stack([g.reshape(2 * D_FF) for g in grads["ffn_conv_b"]]),
        "ssm_d": g_dskip,
        "ffn_conv_w": jnp.stack([g.transpose(1, 0, 2).reshape(3, 2 * D_FF) for g in grads["ffn_conv_w"]]),
    }
    sizes = [int(small_g[n].size) for n in SMALL]
    total = sum(sizes)
    rows8 = -(-total // (N_DEV * 128 * 8)) * 8
    flat = jnp.concatenate([small_g[n].reshape(-1).astype(F32) for n in SMALL]
                           + [jnp.zeros((N_DEV * rows8 * 128 - total,), F32)])
    recv = exchange("xch_small", flat.reshape(N_DEV, rows8, 128))
    summed = all_gather("ag_small", sum_parts("sum_small", recv), F32).reshape(-1)

    def local_part(name, a):
        ax = SMALL_SHARDED.get(name)
        if ax is None:
            return a
        n_loc = a.shape[ax] // N_DEV
        return lax.dynamic_slice_in_dim(a, me * n_loc, n_loc, axis=ax)

    sg, off = {}, 0
    for n, sz in zip(SMALL, sizes):
        sg[n] = local_part(n, summed[off:off + sz].reshape(small_g[n].shape))
        off += sz
    lsizes = [int(sg[n].size) for n in SMALL]
    ltotal = sum(lsizes)
    lrows = -(-ltotal // (128 * 16)) * 16

    def pack(d_):
        return jnp.concatenate([d_[n].reshape(-1) for n in SMALL] + [jnp.zeros((lrows * 128 - ltotal,), F32)]
                               ).reshape(lrows, 128)

    padv = jnp.concatenate([mom2[n].reshape(-1) for n in SMALL] + [jnp.ones((lrows * 128 - ltotal,), F32)]
                           ).reshape(lrows, 128)
    res = adamw("adamw_small", pack(master), pack(mom1), padv, g=pack(sg))
    off = 0
    for n, sz in zip(SMALL, lsizes):
        for dst, r in zip((out_g, out_d, out_m, out_v), res):
            dst[n] = r.reshape(-1)[off:off + sz].reshape(master[n].shape)
        off += sz

    return (loss, dx0.reshape(bsz, seq, d), *[out_g[n] for n in WEIGHTS], *[out_d[n] for n in WEIGHTS],
            *[out_m[n] for n in WEIGHTS], *[out_v[n] for n in WEIGHTS])
```

```python
import functools
import math

import jax
import jax.numpy as jnp
from jax import lax
from jax.experimental import pallas as pl
from jax.experimental.pallas import tpu as pltpu

F32 = jnp.float32
BF16 = jnp.bfloat16
MXU_DTYPE = jnp.bfloat16
N_DEV = 8
MESH_AXES = ("x", "y", "c")

D_MODEL = 1024
SB_HEAD_DIM = 64
SB_WIDTH = 512
SB_BLOCK = 256
POOL_WINDOWS = (2, 4, 8, 16)
POOL_GROUP = 128
POOL_HALO = 16
SSM_TILES = 8
SSM_TILE_STATES = 512
SSM_SLAB = 32
MEM_LEN = 256
XA_HEADS = 4
XA_HEAD_DIM = 256
D_FF = 2816
FF_SHARD = 704
EPS = 1e-6
ADAM_LR = 0.001
ADAM_B1 = 0.9
ADAM_B2 = 0.999
ADAM_EPS = 1e-08
ADAM_WD = 0.01
ADAM_STEP = 10
VMEM_LIMIT = 56 * 1024 * 1024

_NN = (((1,), (0,)), ((), ()))
_NT = (((1,), (1,)), ((), ()))
_TN = (((0,), (0,)), ((), ()))


def _params(sem=None):
    if sem is None:
        return pltpu.CompilerParams(vmem_limit_bytes=VMEM_LIMIT)
    return pltpu.CompilerParams(dimension_semantics=sem, vmem_limit_bytes=VMEM_LIMIT)


def _tile(n, pref, mult=8):
    if n <= pref:
        return n
    for t in range(pref, 0, -1):
        if n % t == 0 and t % mult == 0:
            return t
    return n


def _dot(a, b, dims):
    return lax.dot_general(a.astype(MXU_DTYPE), b.astype(MXU_DTYPE), dims, preferred_element_type=F32)


def _dot_exact01(x, m01, dims=_NN):
    x1 = x.astype(BF16)
    r1 = x - x1.astype(F32)
    x2 = r1.astype(BF16)
    x3 = (r1 - x2.astype(F32)).astype(BF16)
    m = m01.astype(BF16)
    out = lax.dot_general(x1, m, dims, preferred_element_type=F32)
    out = out + lax.dot_general(x2, m, dims, preferred_element_type=F32)
    return out + lax.dot_general(x3, m, dims, preferred_element_type=F32)


def _mm(name, a, b, dims, grid, a_spec, b_spec, o_spec, out_shape, out_dtype, acc_shape, res=None, r_spec=None,
        group=1, n=None, a_sel="full", b_sel="full", o_sel="full"):
    nk = grid[2]
    if out_dtype is None:
        out_dtype = BF16

    def at(sel, s):
        if sel == "lead":
            return (s,)
        if sel == "lanes":
            return (slice(None), slice(s * n, (s + 1) * n))
        return (Ellipsis,)

    def body(*refs):
        a_ref, b_ref = refs[0], refs[1]
        r_ref = refs[2] if res is not None else None
        o_ref = refs[3] if res is not None else refs[2]
        acc = refs[-1] if nk > 1 else None
        k = pl.program_id(2)

        def emit(s, val):
            if nk == 1:
                if r_ref is not None:
                    val = val + r_ref[...].astype(F32)
                o_ref[at(o_sel, s)] = val.astype(out_dtype)
                return

            @pl.when(k == 0)
            def _():
                acc[at(o_sel, s)] = val

            @pl.when(k > 0)
            def _():
                acc[at(o_sel, s)] += val

        total = None
        for s in range(group):
            val = _dot(a_ref[at(a_sel, s)], b_ref[at(b_sel, s)], dims)
            if o_sel == "full":
                total = val if total is None else total + val
            else:
                emit(s, val)
        if o_sel == "full":
            emit(0, total)
        if nk > 1:
            @pl.when(k == nk - 1)
            def _():
                r = acc[...]
                if r_ref is not None:
                    r = r + r_ref[...].astype(F32)
                o_ref[...] = r.astype(out_dtype)

    in_specs = [a_spec, b_spec] + ([] if res is None else [r_spec])
    args = (a, b) + (() if res is None else (res,))
    return pl.pallas_call(
        body, name=name, grid=grid, in_specs=in_specs, out_specs=o_spec,
        out_shape=jax.ShapeDtypeStruct(out_shape, out_dtype),
        scratch_shapes=[pltpu.VMEM(acc_shape, F32)] if nk > 1 else [],
        compiler_params=_params(("parallel", "parallel", "arbitrary")),
    )(*args)


def mm_nn(name, a, b, res=None, koff=0, out_dtype=None):
    m, k = a.shape
    n = b.shape[1]
    tm, tn, tk = _tile(m, 1024), _tile(n, 1024, 128), _tile(k, 1024, 128)
    kb = koff // tk
    spec = pl.BlockSpec((tm, tn), lambda i, j, kk: (i, j))
    return _mm(name, a, b, _NN, (m // tm, n // tn, k // tk),
               pl.BlockSpec((tm, tk), lambda i, j, kk: (i, kk)),
               pl.BlockSpec((tk, tn), lambda i, j, kk: (kk + kb, j)),
               spec, (m, n), out_dtype, (tm, tn), res, spec)


def mm_nn_bs(name, a, bs, stacked_out=False, out_dtype=None):
    m, k = a.shape
    s, _, n = bs.shape
    tm, tk = _tile(m, 1024), _tile(k, 1024, 128)
    a_spec = pl.BlockSpec((tm, tk), lambda i, j, kk: (i, kk))
    if stacked_out:
        return _mm(name, a, bs, _NN, (m // tm, s, k // tk), a_spec,
                   pl.BlockSpec((None, tk, n), lambda i, j, kk: (j, kk, 0)),
                   pl.BlockSpec((None, tm, n), lambda i, j, kk: (j, i, 0)), (s, m, n), out_dtype, (tm, n))
    g = _tile(s, max(1, 1024 // n), 1)
    return _mm(name, a, bs, _NN, (m // tm, s // g, k // tk), a_spec,
               pl.BlockSpec((g, tk, n), lambda i, j, kk: (j, kk, 0)),
               pl.BlockSpec((tm, g * n), lambda i, j, kk: (i, j)), (m, s * n), out_dtype, (tm, g * n),
               group=g, n=n, b_sel="lead", o_sel="lanes")


def mm_as_nn(name, a_st, b3, res, out_dtype=F32):
    s, m, kp = a_st.shape
    n = b3.shape[2]
    tm, tn = _tile(m, 1024), _tile(n, 1024, 128)
    spec = pl.BlockSpec((tm, tn), lambda i, j, kk: (i, j))
    return _mm(name, a_st, b3, _NN, (m // tm, n // tn, s),
               pl.BlockSpec((None, tm, kp), lambda i, j, kk: (kk, i, 0)),
               pl.BlockSpec((None, kp, tn), lambda i, j, kk: (kk, 0, j)),
               spec, (m, n), out_dtype, (tm, tn), res, spec)


def mm_nt(name, dc, b, out_dtype=None):
    m, n = dc.shape
    k = b.shape[0]
    tm, tko, tnr = _tile(m, 1024), _tile(k, 1024, 128), _tile(n, 1024, 128)
    return _mm(name, dc, b, _NT, (m // tm, k // tko, n // tnr),
               pl.BlockSpec((tm, tnr), lambda i, j, kk: (i, kk)),
               pl.BlockSpec((tko, tnr), lambda i, j, kk: (j, kk)),
               pl.BlockSpec((tm, tko), lambda i, j, kk: (i, j)), (m, k), out_dtype, (tm, tko))


def mm_nt_bs(name, dc, bs, dc_stacked=False, out_dtype=None):
    s, k, n = bs.shape
    m = dc.shape[1] if dc_stacked else dc.shape[0]
    tm, tko = _tile(m, 1024), _tile(k, 1024, 128)
    o_spec = pl.BlockSpec((tm, tko), lambda i, j, kk: (i, j))
    if dc_stacked:
        return _mm(name, dc, bs, _NT, (m // tm, k // tko, s),
                   pl.BlockSpec((None, tm, n), lambda i, j, kk: (kk, i, 0)),
                   pl.BlockSpec((None, tko, n), lambda i, j, kk: (kk, j, 0)), o_spec, (m, k), out_dtype, (tm, tko))
    g = _tile(s, max(1, 2048 // n), 1)
    return _mm(name, dc, bs, _NT, (m // tm, k // tko, s // g),
               pl.BlockSpec((tm, g * n), lambda i, j, kk: (i, kk)),
               pl.BlockSpec((g, tko, n), lambda i, j, kk: (kk, j, 0)), o_spec, (m, k), out_dtype, (tm, tko),
               group=g, n=n, a_sel="lanes", b_sel="lead")


def mm_nt_os(name, dc, b3, out_dtype=None):
    m, n = dc.shape
    s, kp, _ = b3.shape
    tm, tnr = _tile(m, 1024), _tile(n, 1024, 128)
    return _mm(name, dc, b3, _NT, (m // tm, s, n // tnr),
               pl.BlockSpec((tm, tnr), lambda i, j, kk: (i, kk)),
               pl.BlockSpec((None, kp, tnr), lambda i, j, kk: (j, 0, kk)),
               pl.BlockSpec((None, tm, kp), lambda i, j, kk: (j, i, 0)), (s, m, kp), out_dtype, (tm, kp))


def mm_tn(name, a, dc, a_stacked=False, dc_cols=None, dc_stacked=False, out_dtype=None):
    if a_stacked:
        s, m, kp = a.shape
        n = dc.shape[1]
        tno, tmr = _tile(n, 1024, 128), _tile(m, 1024)
        return _mm(name, a, dc, _TN, (s, n // tno, m // tmr),
                   pl.BlockSpec((None, tmr, kp), lambda i, j, kk: (i, kk, 0)),
                   pl.BlockSpec((tmr, tno), lambda i, j, kk: (kk, j)),
                   pl.BlockSpec((None, kp, tno), lambda i, j, kk: (i, 0, j)), (s, kp, n), out_dtype, (kp, tno))
    m, k = a.shape
    tko, tmr = _tile(k, 1024, 128), _tile(m, 1024)
    a_spec = pl.BlockSpec((tmr, tko), lambda i, j, kk: (kk, i))
    if dc_stacked:
        s, _, n = dc.shape
        return _mm(name, a, dc, _TN, (k // tko, s, m // tmr), a_spec,
                   pl.BlockSpec((None, tmr, n), lambda i, j, kk: (j, kk, 0)),
                   pl.BlockSpec((None, tko, n), lambda i, j, kk: (j, i, 0)), (s, k, n), out_dtype, (tko, n))
    if dc_cols is not None:
        n = dc_cols
        s = dc.shape[1] // n
        g = _tile(s, max(1, 1024 // n), 1)
        return _mm(name, a, dc, _TN, (k // tko, s // g, m // tmr), a_spec,
                   pl.BlockSpec((tmr, g * n), lambda i, j, kk: (kk, j)),
                   pl.BlockSpec((g, tko, n), lambda i, j, kk: (j, i, 0)), (s, k, n), out_dtype, (g, tko, n),
                   group=g, n=n, b_sel="lanes", o_sel="lead")
    n = dc.shape[1]
    tno = _tile(n, 1024, 128)
    return _mm(name, a, dc, _TN, (k // tko, n // tno, m // tmr), a_spec,
               pl.BlockSpec((tmr, tno), lambda i, j, kk: (kk, j)),
               pl.BlockSpec((tko, tno), lambda i, j, kk: (i, j)), (k, n), out_dtype, (tko, tno))


def rms_fwd(name, x, g):
    t, d = x.shape
    tr = _tile(t, 512)

    def body(x_ref, g_ref, o_ref):
        xf = x_ref[...]
        r = lax.rsqrt(jnp.mean(xf * xf, axis=-1, keepdims=True) + EPS)
        o_ref[...] = (xf * r * g_ref[...]).astype(o_ref.dtype)

    return pl.pallas_call(
        body, name=name, grid=(t // tr,),
        in_specs=[pl.BlockSpec((tr, d), lambda i: (i, 0)), pl.BlockSpec((1, d), lambda i: (0, 0))],
        out_specs=pl.BlockSpec((tr, d), lambda i: (i, 0)),
        out_shape=jax.ShapeDtypeStruct((t, d), BF16), compiler_params=_params(("parallel",)),
    )(x, g.reshape(1, d))


def rms_bwd(name, x, g, dh, dres=None, need_dx=True):
    t, d = x.shape
    tr = _tile(t, 512)

    def body(*refs):
        refs = list(refs)
        x_ref, g_ref, dh_ref = refs[:3]
        r_ref = refs[3] if dres is not None else None
        outs = refs[4:] if dres is not None else refs[3:]
        dx_ref, dg_ref = (outs[0], outs[1]) if need_dx else (None, outs[0])
        i = pl.program_id(0)

        @pl.when(i == 0)
        def _():
            dg_ref[...] = jnp.zeros_like(dg_ref)

        xf = x_ref[...]
        dhf = dh_ref[...].astype(F32)
        r = lax.rsqrt(jnp.mean(xf * xf, axis=-1, keepdims=True) + EPS)
        xh = xf * r
        dg_ref[...] += jnp.sum(dhf * xh, axis=0, keepdims=True)
        if need_dx:
            dxh = dhf * g_ref[...]
            dx = r * (dxh - xh * jnp.mean(dxh * xh, axis=-1, keepdims=True))
            if r_ref is not None:
                dx = dx + r_ref[...]
            dx_ref[...] = dx

    row = pl.BlockSpec((tr, d), lambda i: (i, 0))
    vec = pl.BlockSpec((1, d), lambda i: (0, 0))
    in_specs = [row, vec, row] + ([row] if dres is not None else [])
    args = (x, g.reshape(1, d), dh) + ((dres,) if dres is not None else ())
    out_specs = ([row] if need_dx else []) + [vec]
    out_shape = ([jax.ShapeDtypeStruct((t, d), F32)] if need_dx else []) + [jax.ShapeDtypeStruct((1, d), F32)]
    res = pl.pallas_call(
        body, name=name, grid=(t // tr,), in_specs=in_specs, out_specs=out_specs, out_shape=out_shape,
        compiler_params=_params(("arbitrary",)),
    )(*args)
    return res if need_dx else (None, res[0])


def loss_head(x, g, tgt):
    t, d = x.shape
    tr = _tile(t, 512)

    def body(x_ref, g_ref, t_ref, l_ref, dx_ref, dg_ref):
        i = pl.program_id(0)

        @pl.when(i == 0)
        def _():
            l_ref[...] = jnp.zeros_like(l_ref)
            dg_ref[...] = jnp.zeros_like(dg_ref)

        xf = x_ref[...]
        r = lax.rsqrt(jnp.mean(xf * xf, axis=-1, keepdims=True) + EPS)
        xh = xf * r
        diff = xh * g_ref[...] - t_ref[...]
        l_ref[...] += 0.5 * jnp.sum(jnp.mean(diff * diff, axis=-1, keepdims=True))
        dy = diff * (1.0 / d)
        dg_ref[...] += jnp.sum(dy * xh, axis=0, keepdims=True)
        dxh = dy * g_ref[...]
        dx_ref[...] = r * (dxh - xh * jnp.mean(dxh * xh, axis=-1, keepdims=True))

    row = pl.BlockSpec((tr, d), lambda i: (i, 0))
    vec = pl.BlockSpec((1, d), lambda i: (0, 0))
    return pl.pallas_call(
        body, name="loss_head", grid=(t // tr,), in_specs=[row, vec, row],
        out_specs=[pl.BlockSpec((1, 128), lambda i: (0, 0)), row, vec],
        out_shape=[jax.ShapeDtypeStruct((1, 128), F32), jax.ShapeDtypeStruct((t, d), F32),
                   jax.ShapeDtypeStruct((1, d), F32)],
        compiler_params=_params(("arbitrary",)),
    )(x, g.reshape(1, d), tgt)


def glu_fwd(glu, x):
    t, d = x.shape
    tr = _tile(t, 512)

    def body(v_ref, g_ref, x_ref, o_ref):
        o_ref[...] = x_ref[...] + v_ref[...] * jax.nn.sigmoid(g_ref[...])

    return pl.pallas_call(
        body, name="glu_fwd", grid=(t // tr,),
        in_specs=[pl.BlockSpec((tr, d), lambda i: (i, 0)), pl.BlockSpec((tr, d), lambda i: (i, 1)),
                  pl.BlockSpec((tr, d), lambda i: (i, 0))],
        out_specs=pl.BlockSpec((tr, d), lambda i: (i, 0)),
        out_shape=jax.ShapeDtypeStruct((t, d), F32), compiler_params=_params(("parallel",)),
    )(glu, glu, x)


def glu_bwd(glu, dmix):
    t, d = dmix.shape
    tr = _tile(t, 512)

    def body(v_ref, g_ref, d_ref, o_ref):
        sg = jax.nn.sigmoid(g_ref[...])
        dm = d_ref[...]
        o_ref[:, :d] = (dm * sg).astype(o_ref.dtype)
        o_ref[:, d:] = (dm * v_ref[...] * sg * (1.0 - sg)).astype(o_ref.dtype)

    return pl.pallas_call(
        body, name="glu_bwd", grid=(t // tr,),
        in_specs=[pl.BlockSpec((tr, d), lambda i: (i, 0)), pl.BlockSpec((tr, d), lambda i: (i, 1)),
                  pl.BlockSpec((tr, d), lambda i: (i, 0))],
        out_specs=pl.BlockSpec((tr, 2 * d), lambda i: (i, 0)),
        out_shape=jax.ShapeDtypeStruct((t, 2 * d), BF16), compiler_params=_params(("parallel",)),
    )(glu, glu, dmix)


def _log_sigmoid(z):
    return jnp.minimum(z, 0.0) - jnp.log(1.0 + jnp.exp(-jnp.abs(z)))


def _head_masks(shape):
    lane = lax.broadcasted_iota(jnp.int32, shape, 1)
    return lane < SB_HEAD_DIM


def _stack_heads(xf, is_a):
    return jnp.concatenate([jnp.where(is_a, xf, 0.0), jnp.where(is_a, 0.0, xf)], axis=0).astype(MXU_DTYPE)


def _diag_mask(qb):
    row = lax.broadcasted_iota(jnp.int32, (2 * qb, qb), 0) & (qb - 1)
    col = lax.broadcasted_iota(jnp.int32, (2 * qb, qb), 1)
    return col < row


def _tri01(qb, pred):
    j = lax.broadcasted_iota(jnp.int32, (qb, qb), 0)
    s = lax.broadcasted_iota(jnp.int32, (qb, qb), 1)
    m = pred(j, s).astype(BF16)
    return jnp.concatenate([m, m], axis=0)


def _split_cat(x):
    hi = x.astype(BF16)
    lo = (x - hi.astype(F32)).astype(BF16)
    return jnp.concatenate([hi, lo], axis=1)


def sb_attn_fwd(proj, bsz, seq):
    qb = SB_BLOCK
    nq = seq // qb
    npair = SB_WIDTH // 128
    scale = SB_HEAD_DIM ** -0.5

    def body(q_ref, k_ref, v_ref, o_ref, r_ref):
        qi = pl.program_id(2)
        is_a = _head_masks((qb, 128))
        q2 = _stack_heads(q_ref[...], is_a)
        diag = _diag_mask(qb)
        upper = _tri01(qb, lambda j, s: j > s)

        def block(kbi, acc, run, masked):
            ks = pl.ds(pl.multiple_of(kbi * qb, qb), qb)
            kblk = k_ref[ks, :].astype(MXU_DTYPE)
            vblk = v_ref[ks, :].astype(MXU_DTYPE)
            z = lax.dot_general(q2, kblk, _NT, preferred_element_type=F32) * scale
            lb = _log_sigmoid(z)
            lk = lb - z
            if masked:
                lk = jnp.where(diag, lk, 0.0)
            after = run + lax.dot_general(_split_cat(lk), upper, _NN, preferred_element_type=F32)
            w = jnp.exp(lb + after)
            if masked:
                w = jnp.where(diag, w, 0.0)
            acc = acc + lax.dot_general(w.astype(MXU_DTYPE), vblk, _NN, preferred_element_type=F32)
            return acc, run + jnp.sum(lk, axis=1, keepdims=True)

        carry = block(qi, jnp.zeros((2 * qb, 128), F32), jnp.zeros((2 * qb, 1), F32), True)
        acc, run = lax.fori_loop(0, qi, lambda i, c: block(qi - 1 - i, c[0], c[1], False), carry)
        o_ref[...] = jnp.where(is_a, acc[:qb], acc[qb:]).astype(o_ref.dtype)
        r_ref[...] = jnp.where(is_a, run[:qb], run[qb:])

    return pl.pallas_call(
        body, name="sb_attn_fwd", grid=(bsz, npair, nq),
        in_specs=[pl.BlockSpec((qb, 128), lambda b, p, i: (b * nq + i, p)),
                  pl.BlockSpec((seq, 128), lambda b, p, i: (b, npair + p)),
                  pl.BlockSpec((seq, 128), lambda b, p, i: (b, 2 * npair + p))],
        out_specs=[pl.BlockSpec((qb, 128), lambda b, p, i: (b * nq + i, p)),
                   pl.BlockSpec((qb, 128), lambda b, p, i: (b * nq + i, p))],
        out_shape=[jax.ShapeDtypeStruct((bsz * seq, SB_WIDTH), BF16),
                   jax.ShapeDtypeStruct((bsz * seq, SB_WIDTH), F32)],
        compiler_params=_params(("parallel", "parallel", "arbitrary")),
    )(proj, proj, proj)


def sb_attn_bwd(proj, rsum, dcat, bsz, seq):
    qb = SB_BLOCK
    nq = seq // qb
    npair = SB_WIDTH // 128
    scale = SB_HEAD_DIM ** -0.5

    def body(q_ref, k_ref, v_ref, r_ref, do_ref, dq_ref, dk_ref, dv_ref):
        qi = pl.program_id(2)

        @pl.when(qi == 0)
        def _():
            dk_ref[...] = jnp.zeros_like(dk_ref)
            dv_ref[...] = jnp.zeros_like(dv_ref)

        is_a = _head_masks((qb, 128))
        q2 = _stack_heads(q_ref[...], is_a)
        do2 = _stack_heads(do_ref[...].astype(F32), is_a)
        rf = r_ref[...]
        rtot = jnp.concatenate([rf[:, 0:1], rf[:, SB_HEAD_DIM:SB_HEAD_DIM + 1]], axis=0)
        diag = _diag_mask(qb)
        incl = _tri01(qb, lambda j, s: j <= s)
        strict = _tri01(qb, lambda j, s: j < s)

        def block(kbi, dq, pre, epre, masked):
            ks = pl.ds(pl.multiple_of(kbi * qb, qb), qb)
            kblk = k_ref[ks, :].astype(MXU_DTYPE)
            vblk = v_ref[ks, :].astype(MXU_DTYPE)
            z = lax.dot_general(q2, kblk, _NT, preferred_element_type=F32) * scale
            lb = _log_sigmoid(z)
            lk = lb - z
            if masked:
                lk = jnp.where(diag, lk, 0.0)
            after = rtot - (pre + lax.dot_general(_split_cat(lk), incl, _NN, preferred_element_type=F32))
            w = jnp.exp(lb + after)
            if masked:
                w = jnp.where(diag, w, 0.0)
            dw = lax.dot_general(do2, vblk, _NT, preferred_element_type=F32)
            e = dw * w
            ecum = epre + lax.dot_general(_split_cat(e), strict, _NN, preferred_element_type=F32)
            beta = jnp.exp(lb)
            dz = (e - beta * (e + ecum)) * scale
            if masked:
                dz = jnp.where(diag, dz, 0.0)
            dz = dz.astype(MXU_DTYPE)
            dq = dq + lax.dot_general(dz, kblk, _NN, preferred_element_type=F32)
            dk_ref[ks, :] += lax.dot_general(dz, q2, _TN, preferred_element_type=F32)
            dv_ref[ks, :] += lax.dot_general(w.astype(MXU_DTYPE), do2, _TN, preferred_element_type=F32)
            return dq, pre + jnp.sum(lk, axis=1, keepdims=True), epre + jnp.sum(e, axis=1, keepdims=True)

        zc = jnp.zeros((2 * qb, 1), F32)
        carry = lax.fori_loop(0, qi, lambda kbi, c: block(kbi, c[0], c[1], c[2], False),
                              (jnp.zeros((2 * qb, 128), F32), zc, zc))
        dq = block(qi, carry[0], carry[1], carry[2], True)[0]
        dq_ref[...] = jnp.where(is_a, dq[:qb], dq[qb:])

    full = jax.ShapeDtypeStruct((bsz * seq, SB_WIDTH), F32)
    qspec = pl.BlockSpec((qb, 128), lambda b, p, i: (b * nq + i, p))
    return pl.pallas_call(
        body, name="sb_attn_bwd", grid=(bsz, npair, nq),
        in_specs=[qspec,
                  pl.BlockSpec((seq, 128), lambda b, p, i: (b, npair + p)),
                  pl.BlockSpec((seq, 128), lambda b, p, i: (b, 2 * npair + p)),
                  qspec, qspec],
        out_specs=[qspec, pl.BlockSpec((seq, 128), lambda b, p, i: (b, p)),
                   pl.BlockSpec((seq, 128), lambda b, p, i: (b, p))],
        out_shape=[full, full, full],
        compiler_params=_params(("parallel", "parallel", "arbitrary")),
    )(proj, proj, proj, rsum, dcat)


def _window_sums(x, forward):
    n = x.shape[0]
    out = []
    s = x
    for sh in (1, 2, 4, 8):
        s = s + pltpu.roll(s, (n - sh) if forward else sh, 0)
        out.append(s)
    return out


def _pool_counts(tc, c, w):
    t = lax.broadcasted_iota(jnp.int32, (tc, 1), 0) + c * tc
    return jnp.minimum(t + 1, w).astype(F32)


def pool_fwd(proj, pool_w, pool_scale, bsz, seq):
    tc = _tile(seq, 512)
    nc = seq // tc
    hb = tc // POOL_HALO
    ucol = 3

    def body(u_ref, prev_ref, w_ref, s_ref, o_ref):
        c = pl.program_id(1)
        prev = jnp.where(c > 0, prev_ref[...], 0.0)
        x = jnp.concatenate([prev, u_ref[...]], axis=0)
        sums = _window_sums(x, forward=False)
        for g, win in enumerate(POOL_WINDOWS):
            ls = slice(g * POOL_GROUP, (g + 1) * POOL_GROUP)
            pooled = sums[g][POOL_HALO:, ls] / _pool_counts(tc, c, win) - x[POOL_HALO:, ls]
            y = _dot(pooled, w_ref[g], _NN)
            o_ref[:, ls] = (y * s_ref[:, ls]).astype(o_ref.dtype)

    return pl.pallas_call(
        body, name="pool_fwd", grid=(bsz, nc),
        in_specs=[pl.BlockSpec((tc, SB_WIDTH), lambda b, c: (b * nc + c, ucol)),
                  pl.BlockSpec((POOL_HALO, SB_WIDTH), lambda b, c: (jnp.maximum((b * nc + c) * hb - 1, 0), ucol)),
                  pl.BlockSpec((4, POOL_GROUP, POOL_GROUP), lambda b, c: (0, 0, 0)),
                  pl.BlockSpec((1, SB_WIDTH), lambda b, c: (0, 0))],
        out_specs=pl.BlockSpec((tc, SB_WIDTH), lambda b, c: (b * nc + c, 0)),
        out_shape=jax.ShapeDtypeStruct((bsz * seq, SB_WIDTH), BF16),
        compiler_params=_params(("parallel", "parallel")),
    )(proj, proj, pool_w, pool_scale)


def pool_bwd(proj, pool_w, pool_scale, dcat, bsz, seq):
    tc = _tile(seq, 512)
    nc = seq // tc
    hb = tc // POOL_HALO
    nblk = bsz * seq // POOL_HALO
    ucol = 3

    def body(u_ref, prev_ref, dy_ref, nxt_ref, w_ref, s_ref, du_ref, dw_ref, ds_ref):
        b, c = pl.program_id(0), pl.program_id(1)

        @pl.when((b == 0) & (c == 0))
        def _():
            dw_ref[...] = jnp.zeros_like(dw_ref)
            ds_ref[...] = jnp.zeros_like(ds_ref)

        prev = jnp.where(c > 0, prev_ref[...], 0.0)
        x = jnp.concatenate([prev, u_ref[...]], axis=0)
        sums = _window_sums(x, forward=False)
        nxt = jnp.where(c < nc - 1, nxt_ref[...].astype(F32), 0.0)
        dy = jnp.concatenate([dy_ref[...].astype(F32), nxt], axis=0)
        tq = lax.broadcasted_iota(jnp.int32, (tc + POOL_HALO, 1), 0) + c * tc
        for g, win in enumerate(POOL_WINDOWS):
            ls = slice(g * POOL_GROUP, (g + 1) * POOL_GROUP)
            pooled = sums[g][POOL_HALO:, ls] / _pool_counts(tc, c, win) - x[POOL_HALO:, ls]
            y = _dot(pooled, w_ref[g], _NN)
            ds_ref[:, ls] += jnp.sum(dy[:tc, ls] * y, axis=0, keepdims=True)
            dz = dy[:, ls] * s_ref[:, ls]
            dw_ref[g] += _dot(pooled, dz[:tc], _TN)
            dpool = _dot(dz, w_ref[g], _NT)
            dmean = dpool / jnp.minimum(tq + 1, win).astype(F32)
            fsum = _window_sums(dmean, forward=True)[g]
            du_ref[:, ls] = fsum[:tc] - dpool[:tc]

    return pl.pallas_call(
        body, name="pool_bwd", grid=(bsz, nc),
        in_specs=[pl.BlockSpec((tc, SB_WIDTH), lambda b, c: (b * nc + c, ucol)),
                  pl.BlockSpec((POOL_HALO, SB_WIDTH), lambda b, c: (jnp.maximum((b * nc + c) * hb - 1, 0), ucol)),
                  pl.BlockSpec((tc, SB_WIDTH), lambda b, c: (b * nc + c, 1)),
                  pl.BlockSpec((POOL_HALO, SB_WIDTH), lambda b, c: (jnp.minimum((b * nc + c + 1) * hb, nblk - 1), 1)),
                  pl.BlockSpec((4, POOL_GROUP, POOL_GROUP), lambda b, c: (0, 0, 0)),
                  pl.BlockSpec((1, SB_WIDTH), lambda b, c: (0, 0))],
        out_specs=[pl.BlockSpec((tc, SB_WIDTH), lambda b, c: (b * nc + c, 0)),
                   pl.BlockSpec((4, POOL_GROUP, POOL_GROUP), lambda b, c: (0, 0, 0)),
                   pl.BlockSpec((1, SB_WIDTH), lambda b, c: (0, 0))],
        out_shape=[jax.ShapeDtypeStruct((bsz * seq, SB_WIDTH), F32),
                   jax.ShapeDtypeStruct((4, POOL_GROUP, POOL_GROUP), F32),
                   jax.ShapeDtypeStruct((1, SB_WIDTH), F32)],
        compiler_params=_params(("arbitrary", "arbitrary")),
    )(proj, proj, dcat, dcat, pool_w, pool_scale)


def _lbar(lam_re, lam_im, log_dt):
    dt = jnp.exp(log_dt)
    mag = jnp.exp(lam_re * dt)
    ang = lam_im * dt
    return mag * jnp.cos(ang), mag * jnp.sin(ang)


def _bbar(lam_re, lam_im, log_dt, b_re, b_im):
    lb_re, lb_im = _lbar(lam_re, lam_im, log_dt)
    n_re = lb_re - 1.0
    den = lam_re * lam_re + lam_im * lam_im
    coef_re = (n_re * lam_re + lb_im * lam_im) / den
    coef_im = (lb_im * lam_re - n_re * lam_im) / den
    return coef_re * b_re - coef_im * b_im, coef_re * b_im + coef_im * b_re


def _expand01():
    p = lax.broadcasted_iota(jnp.int32, (64, 1024), 0)
    q = lax.broadcasted_iota(jnp.int32, (64, 1024), 1)
    return (lax.shift_right_logical(q, 4) == p).astype(BF16)


def ssm_prep(lam_re, lam_im, log_dt, b_re2, b_im2):
    def body(lr_ref, li_ref, dt_ref, br_ref, bi_ref, ar_ref, ai_ref, bbr_ref, bbi_ref):
        e = _expand01()
        lr, li, dt = lr_ref[...], li_ref[...], dt_ref[...]
        ar_ref[...], ai_ref[...] = _lbar(lr, li, dt)
        bbr_ref[...], bbi_ref[...] = _bbar(_dot_exact01(lr, e), _dot_exact01(li, e), dt, br_ref[...], bi_ref[...])

    s64 = jax.ShapeDtypeStruct((64, 64), F32)
    s1k = jax.ShapeDtypeStruct((64, 1024), F32)
    return pl.pallas_call(body, name="ssm_prep", out_shape=[s64, s64, s1k, s1k], compiler_params=_params())(
        lam_re, lam_im, log_dt, b_re2, b_im2)


def ssm_prep_bwd(lam_re, lam_im, log_dt, b_re2, b_im2, da_re, da_im, dbb_re, dbb_im):
    def body(lr_ref, li_ref, dt_ref, br_ref, bi_ref, dar_ref, dai_ref, dbr_ref, dbi_ref,
             olr_ref, oli_ref, odt_ref, obr_ref, obi_ref):
        e = _expand01()
        lr, li, dt = lr_ref[...], li_ref[...], dt_ref[...]
        _, vjp_a = jax.vjp(_lbar, lr, li, dt)
        g_lr, g_li, g_dt = vjp_a((dar_ref[...], dai_ref[...]))
        _, vjp_b = jax.vjp(_bbar, _dot_exact01(lr, e), _dot_exact01(li, e), dt, br_ref[...], bi_ref[...])
        x_lr, x_li, x_dt, g_br, g_bi = vjp_b((dbr_ref[...], dbi_ref[...]))
        olr_ref[...] = g_lr + _dot_exact01(x_lr, e, _NT)
        oli_ref[...] = g_li + _dot_exact01(x_li, e, _NT)
        odt_ref[...] = g_dt + x_dt
        obr_ref[...] = g_br
        obi_ref[...] = g_bi

    s64 = jax.ShapeDtypeStruct((64, 64), F32)
    s1k = jax.ShapeDtypeStruct((64, 1024), F32)
    return pl.pallas_call(body, name="ssm_prep_bwd",
                          out_shape=[s64, s64, jax.ShapeDtypeStruct((64, 1), F32), s1k, s1k],
                          compiler_params=_params())(
        lam_re, lam_im, log_dt, b_re2, b_im2, da_re, da_im, dbb_re, dbb_im)


def _gelu(y):
    c = math.sqrt(2.0 / math.pi)
    return 0.5 * y * (1.0 + jnp.tanh(c * (y + 0.044715 * y * y * y)))


def _gelu_grad(y):
    c = math.sqrt(2.0 / math.pi)
    th = jnp.tanh(c * (y + 0.044715 * y * y * y))
    return 0.5 * (1.0 + th) + 0.5 * y * (1.0 - th * th) * c * (1.0 + 3.0 * 0.044715 * y * y)


def _slab(t):
    return pl.ds(pl.multiple_of(t * SSM_SLAB, SSM_SLAB), SSM_SLAB)


def _st_store(ref, i, tc, val):
    for q in range(4):
        ref[pl.ds(4 * i + q, tc, stride=SSM_SLAB), :] = val[:, 128 * q:128 * (q + 1)]


def _st_load(ref, i, tc):
    return jnp.concatenate([ref[pl.ds(4 * i + q, tc, stride=SSM_SLAB), :] for q in range(4)], axis=1)


def ssm_fwd(u, wt, ct, a_re, a_im, dskip, bsz, seq):
    tc = _tile(seq, 256)
    nc = seq // tc
    ns = SSM_TILE_STATES

    def body(u_ref, wt_ref, ct_ref, ar_ref, ai_ref, d_ref, y_ref, gl_ref, hr_ref, hi_ref, sr_ref, si_ref):
        c = pl.program_id(1)

        @pl.when(c == 0)
        def _():
            sr_ref[...] = jnp.zeros_like(sr_ref)
            si_ref[...] = jnp.zeros_like(si_ref)

        uf = u_ref[...]
        for i in range(SSM_TILES):
            bu = _dot(uf[:, i * 128:(i + 1) * 128], wt_ref[i], _NN)
            _st_store(hr_ref, i, tc, bu[:, :ns])
            _st_store(hi_ref, i, tc, bu[:, ns:])
        ar, ai = ar_ref[...], ai_ref[...]

        def step(t, carry):
            sr, si = carry
            nr = ar * sr - ai * si + hr_ref[_slab(t), :]
            ni = ar * si + ai * sr + hi_ref[_slab(t), :]
            hr_ref[_slab(t), :] = nr
            hi_ref[_slab(t), :] = ni
            return nr, ni

        sr, si = lax.fori_loop(0, tc, step, (sr_ref[...], si_ref[...]), unroll=4)
        sr_ref[...] = sr
        si_ref[...] = si
        for i in range(SSM_TILES):
            hcat = jnp.concatenate([_st_load(hr_ref, i, tc), _st_load(hi_ref, i, tc)], axis=1)
            ls = slice(i * 128, (i + 1) * 128)
            y = _dot(hcat, ct_ref[i], _NN) + d_ref[:, ls] * uf[:, ls]
            y_ref[:, ls] = y
            gl_ref[:, ls] = _gelu(y).astype(gl_ref.dtype)

    t = bsz * seq
    row = pl.BlockSpec((tc, D_MODEL), lambda b, c: (b * nc + c, 0))
    st = pl.BlockSpec((tc * SSM_SLAB, 128), lambda b, c: (b * nc + c, 0))
    slab = pl.BlockSpec((SSM_SLAB, 128), lambda b, c: (0, 0))
    return pl.pallas_call(
        body, name="ssm_fwd", grid=(bsz, nc),
        in_specs=[row, pl.BlockSpec((SSM_TILES, 128, 2 * ns), lambda b, c: (0, 0, 0)),
                  pl.BlockSpec((SSM_TILES, 2 * ns, 128), lambda b, c: (0, 0, 0)), slab, slab,
                  pl.BlockSpec((1, D_MODEL), lambda b, c: (0, 0))],
        out_specs=[row, row, st, st],
        out_shape=[jax.ShapeDtypeStruct((t, D_MODEL), F32), jax.ShapeDtypeStruct((t, D_MODEL), BF16),
                   jax.ShapeDtypeStruct((t * SSM_SLAB, 128), F32), jax.ShapeDtypeStruct((t * SSM_SLAB, 128), F32)],
        scratch_shapes=[pltpu.VMEM((SSM_SLAB, 128), F32), pltpu.VMEM((SSM_SLAB, 128), F32)],
        compiler_params=_params(("parallel", "arbitrary")),
    )(u, wt, ct, a_re, a_im, dskip)


def ssm_bwd(dgl, y, u, h_re, h_im, wt, ct, a_re, a_im, dskip, bsz, seq):
    tc = _tile(seq, 256)
    nc = seq // tc
    ns = SSM_TILE_STATES

    def body(dgl_ref, y_ref, u_ref, hr_ref, hi_ref, pr_ref, pi_ref, wt_ref, ct_ref, ar_ref, ai_ref, d_ref,
             du_ref, dwt_ref, dct_ref, dd_ref, dar_ref, dai_ref, gr_ref, gi_ref, sr_ref, si_ref):
        b, c = pl.program_id(0), pl.program_id(1)

        @pl.when((b == 0) & (c == 0))
        def _():
            dwt_ref[...] = jnp.zeros_like(dwt_ref)
            dct_ref[...] = jnp.zeros_like(dct_ref)
            dd_ref[...] = jnp.zeros_like(dd_ref)
            dar_ref[...] = jnp.zeros_like(dar_ref)
            dai_ref[...] = jnp.zeros_like(dai_ref)

        @pl.when(c == 0)
        def _():
            sr_ref[...] = jnp.zeros_like(sr_ref)
            si_ref[...] = jnp.zeros_like(si_ref)

        uf = u_ref[...]
        dy = dgl_ref[...].astype(F32) * _gelu_grad(y_ref[...])
        dd_ref[...] += jnp.sum(dy * uf, axis=0, keepdims=True)
        for i in range(SSM_TILES):
            dyi = dy[:, i * 128:(i + 1) * 128]
            dh = _dot(dyi, ct_ref[i], _NT)
            _st_store(gr_ref, i, tc, dh[:, :ns])
            _st_store(gi_ref, i, tc, dh[:, ns:])
            hcat = jnp.concatenate([_st_load(hr_ref, i, tc), _st_load(hi_ref, i, tc)], axis=1)
            dct_ref[i] += _dot(hcat, dyi, _TN)
        ar, ai = ar_ref[...], ai_ref[...]

        def one(t, gr, gi, hpr, hpi, dar, dai):
            nr = gr_ref[_slab(t), :] + ar * gr + ai * gi
            ni = gi_ref[_slab(t), :] - ai * gr + ar * gi
            gr_ref[_slab(t), :] = nr
            gi_ref[_slab(t), :] = ni
            return nr, ni, dar + nr * hpr + ni * hpi, dai + ni * hpr - nr * hpi

        def step(j, carry):
            t = tc - 1 - j
            gr, gi, dar, dai = carry
            return one(t, gr, gi, hr_ref[_slab(t - 1), :], hi_ref[_slab(t - 1), :], dar, dai)

        carry = lax.fori_loop(0, tc - 1, step, (sr_ref[...], si_ref[...], dar_ref[...], dai_ref[...]), unroll=2)
        first = c == nc - 1
        hpr = jnp.where(first, 0.0, pr_ref[...])
        hpi = jnp.where(first, 0.0, pi_ref[...])
        gr, gi, dar, dai = one(0, *carry[:2], hpr, hpi, *carry[2:])
        sr_ref[...] = gr
        si_ref[...] = gi
        dar_ref[...] = dar
        dai_ref[...] = dai
        for i in range(SSM_TILES):
            ls = slice(i * 128, (i + 1) * 128)
            gcat = jnp.concatenate([_st_load(gr_ref, i, tc), _st_load(gi_ref, i, tc)], axis=1)
            du_ref[:, ls] = (_dot(gcat, wt_ref[i], _NT) + d_ref[:, ls] * dy[:, ls]).astype(du_ref.dtype)
            dwt_ref[i] += _dot(uf[:, ls], gcat, _TN)

    t = bsz * seq
    rev = lambda b, c: (b * nc + (nc - 1 - c), 0)
    row = pl.BlockSpec((tc, D_MODEL), rev)
    st = pl.BlockSpec((tc * SSM_SLAB, 128), rev)
    prev = pl.BlockSpec((SSM_SLAB, 128), lambda b, c: (jnp.maximum((b * nc + (nc - 1 - c)) * tc - 1, 0), 0))
    slab = pl.BlockSpec((SSM_SLAB, 128), lambda b, c: (0, 0))
    wts = pl.BlockSpec((SSM_TILES, 128, 2 * ns), lambda b, c: (0, 0, 0))
    cts = pl.BlockSpec((SSM_TILES, 2 * ns, 128), lambda b, c: (0, 0, 0))
    vec = pl.BlockSpec((1, D_MODEL), lambda b, c: (0, 0))
    return pl.pallas_call(
        body, name="ssm_bwd", grid=(bsz, nc),
        in_specs=[row, row, row, st, st, prev, prev, wts, cts, slab, slab, vec],
        out_specs=[row, wts, cts, vec, slab, slab],
        out_shape=[jax.ShapeDtypeStruct((t, D_MODEL), BF16),
                   jax.ShapeDtypeStruct((SSM_TILES, 128, 2 * ns), F32),
                   jax.ShapeDtypeStruct((SSM_TILES, 2 * ns, 128), F32),
                   jax.ShapeDtypeStruct((1, D_MODEL), F32),
                   jax.ShapeDtypeStruct((SSM_SLAB, 128), F32), jax.ShapeDtypeStruct((SSM_SLAB, 128), F32)],
        scratch_shapes=[pltpu.VMEM((tc * SSM_SLAB, 128), F32), pltpu.VMEM((tc * SSM_SLAB, 128), F32),
                        pltpu.VMEM((SSM_SLAB, 128), F32), pltpu.VMEM((SSM_SLAB, 128), F32)],
        compiler_params=_params(("arbitrary", "arbitrary")),
    )(dgl, y, u, h_re, h_im, h_re, h_im, wt, ct, a_re, a_im, dskip)


def _ssm_in_weights(bb_re2, bb_im2):
    eye = jnp.eye(8, dtype=F32)[None, :, None, :, None]

    def one(bb):
        t = bb.reshape(8, 8, 64, 16).transpose(0, 1, 3, 2)
        return (t[:, :, :, None, :] * eye).reshape(8, 128, 512)

    return jnp.concatenate([one(bb_re2), one(bb_im2)], axis=-1).astype(MXU_DTYPE)


def _ssm_in_weights_bwd(dwt):
    eye = jnp.eye(8, dtype=F32)[None, :, None, :, None]

    def one(d):
        t = (d.reshape(8, 8, 16, 8, 64) * eye).sum(axis=3)
        return t.transpose(0, 1, 3, 2).reshape(64, 1024)

    return one(dwt[..., :512]), one(dwt[..., 512:])


def _ssm_out_weights(c_re, c_im):
    eye = jnp.eye(8, dtype=F32)[None, :, None, :, None]

    def one(cc):
        t = cc.reshape(8, 8, 16, 64).transpose(0, 1, 3, 2)
        return (t[:, :, :, None, :] * eye).reshape(8, 512, 128)

    return jnp.concatenate([one(c_re), -one(c_im)], axis=1).astype(MXU_DTYPE)


def _ssm_out_weights_bwd(dct):
    eye = jnp.eye(8, dtype=F32)[None, :, None, :, None]

    def one(d):
        t = (d.reshape(8, 8, 64, 8, 16) * eye).sum(axis=3)
        return t.transpose(0, 1, 3, 2).reshape(64, 16, 64)

    return one(dct[:, :512]), -one(dct[:, 512:])


def _softmax(s):
    m = jnp.max(s, axis=-1, keepdims=True)
    e = jnp.exp(s - m)
    return e / jnp.sum(e, axis=-1, keepdims=True)


def xattn_fwd(q, kv, bsz, seq):
    tq = _tile(seq, 512)
    nq = seq // tq
    scale = XA_HEAD_DIM ** -0.5

    def body(q_ref, k_ref, v_ref, o_ref):
        s = lax.dot_general(q_ref[...], k_ref[...], _NT, preferred_element_type=F32) * scale
        p = _softmax(s)
        o_ref[...] = _dot(p, v_ref[...], _NN).astype(o_ref.dtype)

    qs = pl.BlockSpec((tq, XA_HEAD_DIM), lambda b, h, i: (b * nq + i, h))
    return pl.pallas_call(
        body, name="xattn_fwd", grid=(bsz, XA_HEADS, nq),
        in_specs=[qs, pl.BlockSpec((MEM_LEN, XA_HEAD_DIM), lambda b, h, i: (b, h)),
                  pl.BlockSpec((MEM_LEN, XA_HEAD_DIM), lambda b, h, i: (b, XA_HEADS + h))],
        out_specs=qs, out_shape=jax.ShapeDtypeStruct((bsz * seq, D_MODEL), BF16),
        compiler_params=_params(("parallel", "parallel", "parallel")),
    )(q, kv, kv)


def xattn_bwd(q, kv, do, bsz, seq):
    tq = _tile(seq, 512)
    nq = seq // tq
    scale = XA_HEAD_DIM ** -0.5

    def body(q_ref, k_ref, v_ref, do_ref, dq_ref, dk_ref, dv_ref):
        @pl.when(pl.program_id(2) == 0)
        def _():
            dk_ref[...] = jnp.zeros_like(dk_ref)
            dv_ref[...] = jnp.zeros_like(dv_ref)

        qv, kk, vv, dov = q_ref[...], k_ref[...], v_ref[...], do_ref[...]
        s = lax.dot_general(qv, kk, _NT, preferred_element_type=F32) * scale
        p = _softmax(s)
        dp = lax.dot_general(dov, vv, _NT, preferred_element_type=F32)
        ds = (p * (dp - jnp.sum(dp * p, axis=-1, keepdims=True)) * scale).astype(MXU_DTYPE)
        dq_ref[...] = lax.dot_general(ds, kk, _NN, preferred_element_type=F32).astype(dq_ref.dtype)
        dk_ref[...] += lax.dot_general(ds, qv, _TN, preferred_element_type=F32)
        dv_ref[...] += lax.dot_general(p.astype(MXU_DTYPE), dov, _TN, preferred_element_type=F32)

    qs = pl.BlockSpec((tq, XA_HEAD_DIM), lambda b, h, i: (b * nq + i, h))
    ks = pl.BlockSpec((MEM_LEN, XA_HEAD_DIM), lambda b, h, i: (b, h))
    vs = pl.BlockSpec((MEM_LEN, XA_HEAD_DIM), lambda b, h, i: (b, XA_HEADS + h))
    dkv = jax.ShapeDtypeStruct((bsz * MEM_LEN, D_MODEL), F32)
    dq, dk, dv = pl.pallas_call(
        body, name="xattn_bwd", grid=(bsz, XA_HEADS, nq),
        in_specs=[qs, ks, vs, qs], out_specs=[qs, ks, ks],
        out_shape=[jax.ShapeDtypeStruct((bsz * seq, D_MODEL), BF16), dkv, dkv],
        compiler_params=_params(("parallel", "parallel", "arbitrary")),
    )(q, kv, kv, do)
    return dq, dk, dv


CONV_HALO = 16


def _shift_down(x, prev, n):
    r = pltpu.roll(x, n, 0)
    row = lax.broadcasted_iota(jnp.int32, x.shape, 0)
    last = prev.shape[0]
    for k in range(n):
        r = jnp.where(row == k, prev[last - n + k:last - n + k + 1, :], r)
    return r


def _shift_up(x, nxt, n):
    rows = x.shape[0]
    r = pltpu.roll(x, rows - n, 0)
    row = lax.broadcasted_iota(jnp.int32, x.shape, 0)
    for k in range(n):
        r = jnp.where(row == rows - n + k, nxt[k:k + 1, :], r)
    return r


def _conv_taps(u, prev, w, b):
    return b + w[2:3] * u + w[1:2] * _shift_down(u, prev, 1) + w[0:1] * _shift_down(u, prev, 2)


def conv_fwd(up, cw, cb, bsz, seq):
    tc = _tile(seq, 512)
    nc = seq // tc
    hb = tc // CONV_HALO
    half = N_DEV // 2

    def body(uv_ref, ug_ref, pv_ref, pg_ref, wv_ref, wg_ref, bv_ref, bg_ref, o_ref):
        c = pl.program_id(2)
        pv = jnp.where(c > 0, pv_ref[...].astype(F32), 0.0)
        pg = jnp.where(c > 0, pg_ref[...].astype(F32), 0.0)
        val = _conv_taps(uv_ref[...].astype(F32), pv, wv_ref[...], bv_ref[...])
        gate = _conv_taps(ug_ref[...].astype(F32), pg, wg_ref[...], bg_ref[...])
        o_ref[...] = (gate * jax.nn.sigmoid(gate) * val).astype(o_ref.dtype)

    def cur(off):
        return pl.BlockSpec((None, tc, FF_SHARD), lambda b, j, c: (j + off, b * nc + c, 0))

    def prv(off):
        return pl.BlockSpec((None, CONV_HALO, FF_SHARD), lambda b, j, c: (j + off, jnp.maximum((b * nc + c) * hb - 1, 0), 0))

    def par(rows, off):
        return pl.BlockSpec((None, rows, FF_SHARD), lambda b, j, c: (j + off, 0, 0))

    return pl.pallas_call(
        body, name="conv_fwd", grid=(bsz, half, nc),
        in_specs=[cur(0), cur(half), prv(0), prv(half), par(3, 0), par(3, half), par(1, 0), par(1, half)],
        out_specs=cur(0), out_shape=jax.ShapeDtypeStruct((half, bsz * seq, FF_SHARD), BF16),
        compiler_params=_params(("parallel", "parallel", "parallel")),
    )(up, up, up, up, cw, cw, cb, cb)


def conv_bwd_taps(up, cw, cb, dact, bsz, seq):
    tc = _tile(seq, 512)
    nc = seq // tc
    hb = tc // CONV_HALO
    half = N_DEV // 2

    def body(uv_ref, ug_ref, pv_ref, pg_ref, wv_ref, wg_ref, bv_ref, bg_ref, da_ref,
             dc_ref, dwv_ref, dwg_ref, dbv_ref, dbg_ref):
        b, c = pl.program_id(1), pl.program_id(2)

        @pl.when((b == 0) & (c == 0))
        def _():
            for r in (dwv_ref, dwg_ref, dbv_ref, dbg_ref):
                r[...] = jnp.zeros_like(r)

        pv = jnp.where(c > 0, pv_ref[...].astype(F32), 0.0)
        pg = jnp.where(c > 0, pg_ref[...].astype(F32), 0.0)
        uv, ug = uv_ref[...].astype(F32), ug_ref[...].astype(F32)
        val = _conv_taps(uv, pv, wv_ref[...], bv_ref[...])
        gate = _conv_taps(ug, pg, wg_ref[...], bg_ref[...])
        sg = jax.nn.sigmoid(gate)
        da = da_ref[...].astype(F32)
        dval = da * gate * sg
        dgate = da * val * sg * (1.0 + gate * (1.0 - sg))
        dc_ref[0] = dval.astype(dc_ref.dtype)
        dc_ref[1] = dgate.astype(dc_ref.dtype)
        for dcv, u, prev, dw_ref, db_ref in ((dval, uv, pv, dwv_ref, dbv_ref), (dgate, ug, pg, dwg_ref, dbg_ref)):
            db_ref[...] += jnp.sum(dcv, axis=0, keepdims=True)
            dw_ref[2:3, :] += jnp.sum(dcv * u, axis=0, keepdims=True)
            dw_ref[1:2, :] += jnp.sum(dcv * _shift_down(u, prev, 1), axis=0, keepdims=True)
            dw_ref[0:1, :] += jnp.sum(dcv * _shift_down(u, prev, 2), axis=0, keepdims=True)

    def cur(off):
        return pl.BlockSpec((None, tc, FF_SHARD), lambda j, b, c: (j + off, b * nc + c, 0))

    def prv(off):
        return pl.BlockSpec((None, CONV_HALO, FF_SHARD), lambda j, b, c: (j + off, jnp.maximum((b * nc + c) * hb - 1, 0), 0))

    def par(rows, off):
        return pl.BlockSpec((None, rows, FF_SHARD), lambda j, b, c: (j + off, 0, 0))

    t = bsz * seq
    hs = jax.ShapeDtypeStruct((2, half, t, FF_SHARD), BF16)
    ws = jax.ShapeDtypeStruct((half, 3, FF_SHARD), F32)
    bs = jax.ShapeDtypeStruct((half, 1, FF_SHARD), F32)
    dc, dwv, dwg, dbv, dbg = pl.pallas_call(
        body, name="conv_bwd_taps", grid=(half, bsz, nc),
        in_specs=[cur(0), cur(half), prv(0), prv(half), par(3, 0), par(3, half), par(1, 0), par(1, half), cur(0)],
        out_specs=[pl.BlockSpec((2, None, tc, FF_SHARD), lambda j, b, c: (0, j, b * nc + c, 0)),
                   par(3, 0), par(3, 0), par(1, 0), par(1, 0)],
        out_shape=[hs, ws, ws, bs, bs],
        compiler_params=_params(("parallel", "arbitrary", "arbitrary")),
    )(up, up, up, up, cw, cw, cb, cb, dact)
    return (dc.reshape(N_DEV, t, FF_SHARD), jnp.concatenate([dwv, dwg], axis=0),
            jnp.concatenate([dbv, dbg], axis=0))


def conv_bwd_input(dconv, cw, bsz, seq):
    tc = _tile(seq, 512)
    nc = seq // tc
    hb = tc // CONV_HALO
    nblk = bsz * seq // CONV_HALO

    def body(d_ref, n_ref, w_ref, o_ref):
        c = pl.program_id(2)
        nxt = jnp.where(c < nc - 1, n_ref[...].astype(F32), 0.0)
        d = d_ref[...].astype(F32)
        w = w_ref[...]
        o_ref[...] = (w[2:3] * d + w[1:2] * _shift_up(d, nxt, 1) + w[0:1] * _shift_up(d, nxt, 2)).astype(o_ref.dtype)

    cur = pl.BlockSpec((None, tc, FF_SHARD), lambda j, b, c: (j, b * nc + c, 0))
    return pl.pallas_call(
        body, name="conv_bwd_input", grid=(N_DEV, bsz, nc),
        in_specs=[cur, pl.BlockSpec((None, CONV_HALO, FF_SHARD),
                                    lambda j, b, c: (j, jnp.minimum((b * nc + c + 1) * hb, nblk - 1), 0)),
                  pl.BlockSpec((None, 3, FF_SHARD), lambda j, b, c: (j, 0, 0))],
        out_specs=cur, out_shape=jax.ShapeDtypeStruct(dconv.shape, BF16),
        compiler_params=_params(("parallel", "parallel", "parallel")),
    )(dconv, dconv, cw)


def _my_index():
    return 4 * lax.axis_index("x") + 2 * lax.axis_index("y") + lax.axis_index("c")


def _peer(k):
    return (lax.axis_index("x") ^ ((k >> 2) & 1), lax.axis_index("y") ^ ((k >> 1) & 1),
            lax.axis_index("c") ^ (k & 1))


def all_gather(name, a, out_dtype):
    def body(a_ref, o_ref, stage, send_sems, recv_sems, local_sem):
        me = _my_index()
        stage[...] = a_ref[...].astype(out_dtype)
        local = pltpu.make_async_copy(stage, o_ref.at[me], local_sem)
        local.start()
        sends = []
        for k in range(1, N_DEV):
            cp = pltpu.make_async_remote_copy(
                src_ref=stage, dst_ref=o_ref.at[me], send_sem=send_sems.at[k - 1], recv_sem=recv_sems.at[k - 1],
                device_id=_peer(k), device_id_type=pl.DeviceIdType.MESH)
            cp.start()
            sends.append(cp)
        for k in range(1, N_DEV):
            pltpu.make_async_remote_copy(
                src_ref=stage, dst_ref=o_ref.at[me ^ k], send_sem=send_sems.at[k - 1], recv_sem=recv_sems.at[k - 1],
                device_id=_peer(k), device_id_type=pl.DeviceIdType.MESH).wait_recv()
        for cp in sends:
            cp.wait_send()
        local.wait()

    return pl.pallas_call(
        body, name=name, in_specs=[pl.BlockSpec(memory_space=pltpu.VMEM)],
        out_specs=pl.BlockSpec(memory_space=pltpu.HBM),
        out_shape=jax.ShapeDtypeStruct((N_DEV,) + a.shape, out_dtype),
        scratch_shapes=[pltpu.VMEM(a.shape, out_dtype), pltpu.SemaphoreType.DMA((N_DEV - 1,)),
                        pltpu.SemaphoreType.DMA((N_DEV - 1,)), pltpu.SemaphoreType.DMA],
        compiler_params=pltpu.CompilerParams(vmem_limit_bytes=VMEM_LIMIT),
    )(a)


def exchange(name, g):
    def body(g_ref, r_ref, send_sems, recv_sems, local_sem):
        me = _my_index()
        local = pltpu.make_async_copy(g_ref.at[me], r_ref.at[me], local_sem)
        local.start()
        sends = []
        for k in range(1, N_DEV):
            cp = pltpu.make_async_remote_copy(
                src_ref=g_ref.at[me ^ k], dst_ref=r_ref.at[me], send_sem=send_sems.at[k - 1],
                recv_sem=recv_sems.at[k - 1], device_id=_peer(k), device_id_type=pl.DeviceIdType.MESH)
            cp.start()
            sends.append(cp)
        for k in range(1, N_DEV):
            pltpu.make_async_remote_copy(
                src_ref=g_ref.at[me], dst_ref=r_ref.at[me ^ k], send_sem=send_sems.at[k - 1],
                recv_sem=recv_sems.at[k - 1], device_id=_peer(k), device_id_type=pl.DeviceIdType.MESH).wait_recv()
        for cp in sends:
            cp.wait_send()
        local.wait()

    return pl.pallas_call(
        body, name=name, in_specs=[pl.BlockSpec(memory_space=pltpu.HBM)],
        out_specs=pl.BlockSpec(memory_space=pltpu.HBM),
        out_shape=jax.ShapeDtypeStruct(g.shape, g.dtype),
        scratch_shapes=[pltpu.SemaphoreType.DMA((N_DEV - 1,)), pltpu.SemaphoreType.DMA((N_DEV - 1,)),
                        pltpu.SemaphoreType.DMA],
    )(g)


_HBM = pl.BlockSpec(memory_space=pltpu.HBM)
_SEM = pl.BlockSpec(memory_space=pltpu.SEMAPHORE)
_DATAFLOW = pltpu.SideEffectType.DATAFLOW_SIDE_EFFECTING


def _split_copies(gather, src_ref, land_ref, send_sems, recv_sems, local_sem):
    me = _my_index()

    def part(j):
        return src_ref if gather else src_ref.at[j]

    local = pltpu.make_async_copy(part(me), land_ref.at[me], local_sem)
    sends = [pltpu.make_async_remote_copy(
        src_ref=part(me ^ k), dst_ref=land_ref.at[me], send_sem=send_sems.at[k - 1], recv_sem=recv_sems.at[k - 1],
        device_id=_peer(k), device_id_type=pl.DeviceIdType.MESH) for k in range(1, N_DEV)]
    recvs = [pltpu.make_async_remote_copy(
        src_ref=part(me ^ k), dst_ref=land_ref.at[me ^ k], send_sem=send_sems.at[k - 1], recv_sem=recv_sems.at[k - 1],
        device_id=_peer(k), device_id_type=pl.DeviceIdType.MESH) for k in range(1, N_DEV)]
    return local, sends, recvs


def split_start(name, src, gather):
    land_shape = ((N_DEV,) + src.shape) if gather else src.shape

    def body(src_ref, land_ref, send_sems, recv_sems, local_sem, src_thru, land_thru, token):
        local, sends, _ = _split_copies(gather, src_ref, land_ref, send_sems, recv_sems, local_sem)
        local.start()
        for cp in sends:
            cp.start()
        token[...] = jnp.zeros_like(token)

    dma7 = pltpu.SemaphoreType.DMA((N_DEV - 1,))
    out = pl.pallas_call(
        body, name=name,
        out_shape=(dma7, dma7, pltpu.SemaphoreType.DMA(()), pltpu.HBM(src.shape, src.dtype),
                   pltpu.HBM(land_shape, src.dtype), jax.ShapeDtypeStruct((8, 128), F32)),
        in_specs=(_HBM, _HBM), out_specs=(_SEM, _SEM, _SEM, _HBM, _HBM, pl.BlockSpec(memory_space=pltpu.VMEM)),
        input_output_aliases={0: 3, 1: 4},
        compiler_params=pltpu.CompilerParams(has_side_effects=_DATAFLOW),
    )(pltpu.with_memory_space_constraint(src, pltpu.HBM),
      pltpu.with_memory_space_constraint(lax.empty(land_shape, src.dtype), pltpu.HBM))
    return out[:5], out[5][0, 0]


def split_wait(name, handles, after, gather):
    send_sems, recv_sems, local_sem, src_thru, land_thru = handles

    def body(src_ref, land_ref, send_sems, recv_sems, local_sem, after_ref, src_dead, got_ref):
        local, sends, recvs = _split_copies(gather, src_ref, land_ref, send_sems, recv_sems, local_sem)
        local.wait()
        for cp in recvs:
            cp.wait_send()
            cp.wait_recv()

    return pl.pallas_call(
        body, name=name,
        out_shape=(pltpu.HBM(src_thru.shape, src_thru.dtype), pltpu.HBM(land_thru.shape, land_thru.dtype)),
        in_specs=(_HBM, _HBM, _SEM, _SEM, _SEM, pl.BlockSpec(memory_space=pl.ANY)), out_specs=(_HBM, _HBM),
        input_output_aliases={0: 0, 1: 1},
        compiler_params=pltpu.CompilerParams(has_side_effects=_DATAFLOW),
    )(src_thru, land_thru, send_sems, recv_sems, local_sem, after)[1]


def sum_parts(name, r):
    _, rows, cols = r.shape

    def body(r_ref, o_ref):
        acc = r_ref[0].astype(F32)
        for s in range(1, N_DEV):
            acc = acc + r_ref[s].astype(F32)
        o_ref[...] = acc

    return pl.pallas_call(body, name=name, out_shape=jax.ShapeDtypeStruct((rows, cols), F32),
                          compiler_params=_params())(r)


def adamw(name, w, m, v, parts=None, g=None):
    rows, cols = w.shape
    br = _tile(rows, 256, 16)
    c1 = 1.0 / (1.0 - ADAM_B1 ** ADAM_STEP)
    c2 = 1.0 / (1.0 - ADAM_B2 ** ADAM_STEP)

    def body(g_ref, w_ref, m_ref, v_ref, og_ref, od_ref, om_ref, ov_ref):
        if parts is None:
            gs = g_ref[...]
        else:
            gs = g_ref[0].astype(F32)
            for s in range(1, N_DEV):
                gs = gs + g_ref[s].astype(F32)
        mn = ADAM_B1 * m_ref[...] + (1.0 - ADAM_B1) * gs
        vn = ADAM_B2 * v_ref[...] + (1.0 - ADAM_B2) * (gs * gs)
        og_ref[...] = gs
        om_ref[...] = mn
        ov_ref[...] = vn
        od_ref[...] = -ADAM_LR * ((mn * c1) / (jnp.sqrt(vn * c2) + ADAM_EPS) + ADAM_WD * w_ref[...])

    blk = pl.BlockSpec((br, cols), lambda i: (i, 0))
    gspec = blk if parts is None else pl.BlockSpec((N_DEV, br, cols), lambda i: (0, i, 0))
    shp = jax.ShapeDtypeStruct((rows, cols), F32)
    return pl.pallas_call(
        body, name=name, grid=(rows // br,), in_specs=[gspec, blk, blk, blk], out_specs=[blk] * 4,
        out_shape=[shp] * 4, compiler_params=_params(("parallel",)),
    )(g if parts is None else parts, w, m, v)


SMALL = ("norm_mix", "norm_xattn", "norm_ffn", "norm_mem", "norm_final", "pool_w", "pool_scale",
         "ssm_lam_re", "ssm_lam_im", "ssm_log_dt", "ssm_b_re", "ssm_b_im", "ssm_c_re", "ssm_c_im",
         "ffn_conv_b", "ssm_d", "ffn_conv_w")
SMALL_SHARDED = {"ssm_d": 1, "ffn_conv_w": 2}
BIG = ("ab_w_in", "ab_w_out", "ssm_w_in", "ssm_w_glu", "xa_w_q", "xa_w_kv", "xa_w_o", "ffn_w_up", "ffn_w_down")
WEIGHTS = ("norm_mix", "norm_xattn", "norm_ffn", "norm_mem", "norm_final", "ab_w_in", "pool_w", "pool_scale",
           "ab_w_out", "ssm_w_in", "ssm_lam_re", "ssm_lam_im", "ssm_log_dt", "ssm_b_re", "ssm_b_im", "ssm_c_re",
           "ssm_c_im", "ssm_d", "ssm_w_glu", "xa_w_q", "xa_w_kv", "xa_w_o", "ffn_w_up", "ffn_conv_w", "ffn_conv_b",
           "ffn_w_down")


def _rows8(g):
    return g.reshape(N_DEV, g.size // (N_DEV * D_MODEL), D_MODEL)


def _square(a):
    return a.reshape(D_MODEL, D_MODEL)


_LAYOUT = {"ab_w_out": _square, "ssm_w_in": _square, "xa_w_q": _square, "xa_w_o": _square,
           "ffn_w_down": lambda a: a.reshape(N_DEV // 2, FF_SHARD, D_MODEL)}
GATHER_ORDER = (("ab_w_in", 0), ("ab_w_out", 0), ("xa_w_q", 0), ("xa_w_kv", 0), ("xa_w_o", 0), ("ffn_w_up", 0),
                ("ffn_w_down", 0), ("ssm_w_in", 0), ("ssm_w_glu", 0), ("xa_w_q", 1), ("xa_w_kv", 1),
                ("xa_w_o", 1), ("ffn_w_up", 1), ("ffn_w_down", 1))


class _Step:
    def __init__(self, master, small):
        self.master, self.small = master, small
        self.pending, self.gathers, self.weights, self.sent = [], {}, {}, []

    def follow(self, v):
        for z in self.pending:
            v = v + z
        self.pending = []
        return v

    def start_gather(self, n, l):
        self.gathers[(n, l)], z = split_start(f"ags_{n}{l}", self.master[n][l].astype(MXU_DTYPE), gather=True)
        self.pending.append(z)

    def weight(self, n, l, after):
        if (n, l) not in self.weights:
            full = split_wait(f"agw_{n}{l}", self.gathers[(n, l)], after, gather=True)
            self.weights[(n, l)] = _LAYOUT.get(n, lambda a: a)(full)
        return self.weights[(n, l)]

    def send_grad(self, n, l, part):
        h, z = split_start(f"xs_{n}{l}", part, gather=False)
        self.pending.append(z)
        self.sent.append((n, l, h))


def _layer_tail(st, l, x_in, mem_n, acts):
    bsz, seq = acts["bsz"], acts["seq"]
    p = st.small
    hq = rms_fwd(f"rms_xattn{l}", x_in, p["norm_xattn"][l])
    q = mm_nn(f"xa_q{l}", hq, st.weight("xa_w_q", l, hq))
    kv = mm_nn_bs(f"xa_kv{l}", mem_n, st.weight("xa_w_kv", l, hq))
    o = xattn_fwd(q, kv, bsz, seq)
    x_mid = mm_nn(f"xa_o{l}", o, st.weight("xa_w_o", l, o), res=x_in, out_dtype=F32)
    hf = rms_fwd(f"rms_ffn{l}", x_mid, p["norm_ffn"][l])
    up = mm_nn_bs(f"ffn_up{l}", hf, st.weight("ffn_w_up", l, hf), stacked_out=True)
    act = conv_fwd(up, p["ffn_conv_w"][l], p["ffn_conv_b"][l], bsz, seq)
    x_out = mm_as_nn(f"ffn_down{l}", act, st.weight("ffn_w_down", l, act), res=x_mid)
    acts[l].update(x_in=x_in, hq=hq, q=q, kv=kv, o=o, x_mid=x_mid, hf=hf, up=up, act=act)
    return x_out


def _layer_tail_bwd(st, l, dx, mem_n, acts, grads):
    a = acts[l]
    bsz, seq = acts["bsz"], acts["seq"]
    p = st.small
    dact = mm_nt_os(f"d_act{l}", dx, st.weight("ffn_w_down", l, dx))
    st.send_grad("ffn_w_down", l, _rows8(mm_tn(f"g_ffn_down{l}", a["act"], dx, a_stacked=True)))
    dconv, dcw, dcb = conv_bwd_taps(a["up"], p["ffn_conv_w"][l], p["ffn_conv_b"][l], dact, bsz, seq)
    grads["ffn_conv_w"][l] = dcw
    grads["ffn_conv_b"][l] = dcb
    dup = conv_bwd_input(dconv, p["ffn_conv_w"][l], bsz, seq)
    dhf = mm_nt_bs(f"d_hf{l}", dup, st.weight("ffn_w_up", l, dx), dc_stacked=True)
    st.send_grad("ffn_w_up", l, mm_tn(f"g_ffn_up{l}", a["hf"], dup, dc_stacked=True))
    dx_mid, grads["norm_ffn"][l] = rms_bwd(f"rms_ffn_bwd{l}", a["x_mid"], st.follow(p["norm_ffn"][l]), dhf, dres=dx)
    do = mm_nt(f"d_o{l}", dx_mid, st.weight("xa_w_o", l, dx))
    st.send_grad("xa_w_o", l, _rows8(mm_tn(f"g_xa_o{l}", a["o"], dx_mid)))
    dq, dk, dv = xattn_bwd(a["q"], a["kv"], do, bsz, seq)
    dkv = jnp.concatenate([dk, dv], axis=1).astype(BF16)
    dhq = mm_nt(f"d_hq{l}", dq, st.weight("xa_w_q", l, dx))
    st.send_grad("xa_w_q", l, _rows8(mm_tn(f"g_xa_q{l}", a["hq"], dq)))
    dmem_n = mm_nt_bs(f"d_memn{l}", dkv, st.weight("xa_w_kv", l, dx), out_dtype=F32)
    st.send_grad("xa_w_kv", l, mm_tn(f"g_xa_kv{l}", mem_n, dkv, dc_cols=2 * D_MODEL // N_DEV))
    dx_in, grads["norm_xattn"][l] = rms_bwd(f"rms_xattn_bwd{l}", a["x_in"], st.follow(p["norm_xattn"][l]), dhq,
                                            dres=dx_mid)
    return dx_in, dmem_n


def kernel(x, mem, norm_mix, norm_xattn, norm_ffn, norm_mem, norm_final, ab_w_in, pool_w, pool_scale, ab_w_out, ssm_w_in, ssm_lam_re, ssm_lam_im, ssm_log_dt, ssm_b_re, ssm_b_im, ssm_c_re, ssm_c_im, ssm_d, ssm_w_glu, xa_w_q, xa_w_kv, xa_w_o, ffn_w_up, ffn_conv_w, ffn_conv_b, ffn_w_down, loss_target, m_norm_mix, m_norm_xattn, m_norm_ffn, m_norm_mem, m_norm_final, m_ab_w_in, m_pool_w, m_pool_scale, m_ab_w_out, m_ssm_w_in, m_ssm_lam_re, m_ssm_lam_im, m_ssm_log_dt, m_ssm_b_re, m_ssm_b_im, m_ssm_c_re, m_ssm_c_im, m_ssm_d, m_ssm_w_glu, m_xa_w_q, m_xa_w_kv, m_xa_w_o, m_ffn_w_up, m_ffn_conv_w, m_ffn_conv_b, m_ffn_w_down, v_norm_mix, v_norm_xattn, v_norm_ffn, v_norm_mem, v_norm_final, v_ab_w_in, v_pool_w, v_pool_scale, v_ab_w_out, v_ssm_w_in, v_ssm_lam_re, v_ssm_lam_im, v_ssm_log_dt, v_ssm_b_re, v_ssm_b_im, v_ssm_c_re, v_ssm_c_im, v_ssm_d, v_ssm_w_glu, v_xa_w_q, v_xa_w_kv, v_xa_w_o, v_ffn_w_up, v_ffn_conv_w, v_ffn_conv_b, v_ffn_w_down):
    given = dict(locals())
    master = {n: given[n] for n in WEIGHTS}
    mom1 = {n: given["m_" + n] for n in WEIGHTS}
    mom2 = {n: given["v_" + n] for n in WEIGHTS}
    bsz, seq, d = x.shape
    t = bsz * seq
    me = _my_index()

    conv_w_st = all_gather("ag_ffn_conv_w", ffn_conv_w, F32)
    dskip = all_gather("ag_ssm_d", ssm_d.reshape(1, 128), F32).reshape(1, D_MODEL)
    st = _Step(master, {
        "norm_xattn": norm_xattn, "norm_ffn": norm_ffn,
        "ffn_conv_w": [conv_w_st[:, l] for l in range(2)],
        "ffn_conv_b": [ffn_conv_b[l].reshape(N_DEV, 1, FF_SHARD) for l in range(2)],
    })
    for n, l in GATHER_ORDER:
        st.start_gather(n, l)

    acts = {"bsz": bsz, "seq": seq, 0: {}, 1: {}}
    x0 = x.reshape(t, d)
    mem2 = mem.reshape(bsz * MEM_LEN, d)
    mem_n = rms_fwd("rms_mem", mem2, st.follow(norm_mem))
    pscale = pool_scale.reshape(1, SB_WIDTH)

    h0 = rms_fwd("rms_mix0", x0, norm_mix[0])
    proj = mm_nn_bs("ab_in", h0, st.weight("ab_w_in", 0, mem_n), out_dtype=F32)
    a_out, rsum = sb_attn_fwd(proj, bsz, seq)
    p_out = pool_fwd(proj, pool_w[0], pscale, bsz, seq)
    w_out = st.weight("ab_w_out", 0, a_out)
    x1 = mm_nn("ab_out_a", a_out, w_out, res=x0, out_dtype=F32)
    x1 = mm_nn("ab_out_p", p_out, w_out, res=x1, koff=SB_WIDTH, out_dtype=F32)
    x3 = _layer_tail(st, 0, x1, mem_n, acts)

    b_re2 = ssm_b_re.reshape(64, 1024)
    b_im2 = ssm_b_im.reshape(64, 1024)
    log_dt = ssm_log_dt.reshape(64, 1)
    lb_re, lb_im, bb_re2, bb_im2 = ssm_prep(ssm_lam_re[0], ssm_lam_im[0], log_dt, b_re2, b_im2)
    wt = _ssm_in_weights(bb_re2, bb_im2)
    ct = _ssm_out_weights(ssm_c_re[0], ssm_c_im[0])
    a_re = lb_re.reshape(SSM_SLAB, 128)
    a_im = lb_im.reshape(SSM_SLAB, 128)
    h1 = rms_fwd("rms_mix1", x3, norm_mix[1])
    u = mm_nn("ssm_in", h1, st.weight("ssm_w_in", 0, h1), out_dtype=F32)
    y, gl, h_re, h_im = ssm_fwd(u, wt, ct, a_re, a_im, dskip, bsz, seq)
    glu = mm_nn_bs("ssm_glu", gl, st.weight("ssm_w_glu", 0, gl), out_dtype=F32)
    x4 = glu_fwd(glu, x3)
    x6 = _layer_tail(st, 1, x4, mem_n, acts)

    loss_row, dx, g_norm_final = loss_head(x6, norm_final, loss_target.reshape(t, d))
    loss = lax.psum(loss_row[0, 0], MESH_AXES)

    grads = {n: [None, None] for n in ("ffn_conv_w", "ffn_conv_b", "norm_ffn", "norm_xattn", "norm_mix")}
    dx4, dmem_1 = _layer_tail_bwd(st, 1, dx, mem_n, acts, grads)
    dglu = glu_bwd(glu, dx4)
    dgl = mm_nt_bs("d_gl", dglu, st.weight("ssm_w_glu", 0, dx))
    st.send_grad("ssm_w_glu", 0, mm_tn("g_ssm_glu", gl, dglu, dc_cols=2 * D_MODEL // N_DEV))
    du, dwt, dct, g_dskip, da_re, da_im = ssm_bwd(dgl, y, u, h_re, h_im, wt, ct, a_re, a_im, dskip, bsz, seq)
    dbb_re, dbb_im = _ssm_in_weights_bwd(dwt)
    g_c_re, g_c_im = _ssm_out_weights_bwd(dct)
    g_lam_re, g_lam_im, g_log_dt, g_b_re, g_b_im = ssm_prep_bwd(
        ssm_lam_re[0], ssm_lam_im[0], log_dt, b_re2, b_im2, da_re.reshape(64, 64), da_im.reshape(64, 64),
        dbb_re, dbb_im)
    dh1 = mm_nt("d_h1", du, st.weight("ssm_w_in", 0, dx))
    st.send_grad("ssm_w_in", 0, _rows8(mm_tn("g_ssm_in", h1, du)))
    dx3, grads["norm_mix"][1] = rms_bwd("rms_mix1_bwd", x3, st.follow(norm_mix[1]), dh1, dres=dx4)

    dx1, dmem_0 = _layer_tail_bwd(st, 0, dx3, mem_n, acts, grads)
    dcat = mm_nt("d_cat", dx1, st.weight("ab_w_out", 0, dx))
    st.send_grad("ab_w_out", 0, _rows8(jnp.concatenate(
        [mm_tn("g_ab_out_a", a_out, dx1), mm_tn("g_ab_out_p", p_out, dx1)], axis=0)))
    dq, dk, dv = sb_attn_bwd(proj, rsum, dcat, bsz, seq)
    dpu, g_pool_w, g_pool_scale = pool_bwd(proj, pool_w[0], st.follow(pscale), dcat, bsz, seq)
    dproj = jnp.concatenate([dq, dk, dv, dpu], axis=1).astype(BF16)
    dh0 = mm_nt_bs("d_h0", dproj, st.weight("ab_w_in", 0, dx))
    st.send_grad("ab_w_in", 0, mm_tn("g_ab_in", h0, dproj, dc_cols=2 * D_MODEL // N_DEV))
    dx0, grads["norm_mix"][0] = rms_bwd("rms_mix0_bwd", x0, st.follow(norm_mix[0]), dh0, dres=dx1)
    _, g_norm_mem = rms_bwd("rms_mem_bwd", mem2, norm_mem, dmem_0 + dmem_1, need_dx=False)

    per_layer = {n: [[None] * master[n].shape[0] for _ in range(4)] for n in BIG}
    for n, l, handles in st.sent:
        recv = split_wait(f"xw_{n}{l}", handles, dx0, gather=False)
        rows, cols = recv.shape[1:]
        res = adamw(f"adamw_{n}{l}", master[n][l].reshape(rows, cols), mom1[n][l].reshape(rows, cols),
                    mom2[n][l].reshape(rows, cols), parts=recv)
        for slot, r in zip(per_layer[n], res):
            slot[l] = r.reshape(master[n].shape[1:])
    out_g, out_d, out_m, out_v = ({n: jnp.stack(per_layer[n][k]) for n in BIG} for k in range(4))

    small_g = {
        "norm_mix": jnp.stack([g[0] for g in grads["norm_mix"]]),
        "norm_xattn": jnp.stack([g[0] for g in grads["norm_xattn"]]),
        "norm_ffn": jnp.stack([g[0] for g in grads["norm_ffn"]]),
        "norm_mem": g_norm_mem[0], "norm_final": g_norm_final[0],
        "pool_w": g_pool_w[None], "pool_scale": g_pool_scale,
        "ssm_lam_re": g_lam_re[None], "ssm_lam_im": g_lam_im[None], "ssm_log_dt": g_log_dt.reshape(1, 64),
        "ssm_b_re": g_b_re.reshape(1, 64, 64, 16), "ssm_b_im": g_b_im.reshape(1, 64, 64, 16),
        "ssm_c_re": g_c_re[None], "ssm_c_im": g_c_im[None],
        "ffn_conv_b": jnp.stack([g.reshape(2 * D_FF) for g in grads["ffn_conv_b"]]),
        "ssm_d": g_dskip,
        "ffn_conv_w": jnp.stack([g.transpose(1, 0, 2).reshape(3, 2 * D_FF) for g in grads["ffn_conv_w"]]),
    }
    sizes = [int(small_g[n].size) for n in SMALL]
    total = sum(sizes)
    rows8 = -(-total // (N_DEV * 128 * 8)) * 8
    flat = jnp.concatenate([small_g[n].reshape(-1).astype(F32) for n in SMALL]
                           + [jnp.zeros((N_DEV * rows8 * 128 - total,), F32)])
    recv = exchange("xch_small", flat.reshape(N_DEV, rows8, 128))
    summed = all_gather("ag_small", sum_parts("sum_small", recv), F32).reshape(-1)

    def local_part(name, a):
        ax = SMALL_SHARDED.get(name)
        if ax is None:
            return a
        n_loc = a.shape[ax] // N_DEV
        return lax.dynamic_slice_in_dim(a, me * n_loc, n_loc, axis=ax)

    sg, off = {}, 0
    for n, sz in zip(SMALL, sizes):
        sg[n] = local_part(n, summed[off:off + sz].reshape(small_g[n].shape))
        off += sz
    lsizes = [int(sg[n].size) for n in SMALL]
    ltotal = sum(lsizes)
    lrows = -(-ltotal // (128 * 16)) * 16

    def pack(d_):
        return jnp.concatenate([d_[n].reshape(-1) for n in SMALL] + [jnp.zeros((lrows * 128 - ltotal,), F32)]
                               ).reshape(lrows, 128)

    padv = jnp.concatenate([mom2[n].reshape(-1) for n in SMALL] + [jnp.ones((lrows * 128 - ltotal,), F32)]
                           ).reshape(lrows, 128)
    res = adamw("adamw_small", pack(master), pack(mom1), padv, g=pack(sg))
    off = 0
    for n, sz in zip(SMALL, lsizes):
        for dst, r in zip((out_g, out_d, out_m, out_v), res):
            dst[n] = r.reshape(-1)[off:off + sz].reshape(master[n].shape)
        off += sz

    return (loss, dx0.reshape(bsz, seq, d), *[out_g[n] for n in WEIGHTS], *[out_d[n] for n in WEIGHTS],
            *[out_m[n] for n in WEIGHTS], *[out_v[n] for n in WEIGHTS])
```

```python
import functools
import math

import jax
import jax.numpy as jnp
from jax import lax
from jax.experimental import pallas as pl
from jax.experimental.pallas import tpu as pltpu

F32 = jnp.float32
BF16 = jnp.bfloat16
MXU_DTYPE = jnp.bfloat16
N_DEV = 8
MESH_AXES = ("x", "y", "c")

D_MODEL = 1024
SB_HEAD_DIM = 64
SB_WIDTH = 512
SB_BLOCK = 256
POOL_WINDOWS = (2, 4, 8, 16)
POOL_GROUP = 128
POOL_HALO = 16
SSM_TILES = 8
SSM_TILE_STATES = 512
SSM_SLAB = 32
MEM_LEN = 256
XA_HEADS = 4
XA_HEAD_DIM = 256
D_FF = 2816
FF_SHARD = 704
EPS = 1e-6
ADAM_LR = 0.001
ADAM_B1 = 0.9
ADAM_B2 = 0.999
ADAM_EPS = 1e-08
ADAM_WD = 0.01
ADAM_STEP = 10
VMEM_LIMIT = 56 * 1024 * 1024

_NN = (((1,), (0,)), ((), ()))
_NT = (((1,), (1,)), ((), ()))
_TN = (((0,), (0,)), ((), ()))


def _params(sem=None):
    if sem is None:
        return pltpu.CompilerParams(vmem_limit_bytes=VMEM_LIMIT)
    return pltpu.CompilerParams(dimension_semantics=sem, vmem_limit_bytes=VMEM_LIMIT)


def _tile(n, pref, mult=8):
    if n <= pref:
        return n
    for t in range(pref, 0, -1):
        if n % t == 0 and t % mult == 0:
            return t
    return n


def _dot(a, b, dims):
    return lax.dot_general(a.astype(MXU_DTYPE), b.astype(MXU_DTYPE), dims, preferred_element_type=F32)


def _dot_exact01(x, m01, dims=_NN):
    x1 = x.astype(BF16)
    r1 = x - x1.astype(F32)
    x2 = r1.astype(BF16)
    x3 = (r1 - x2.astype(F32)).astype(BF16)
    m = m01.astype(BF16)
    out = lax.dot_general(x1, m, dims, preferred_element_type=F32)
    out = out + lax.dot_general(x2, m, dims, preferred_element_type=F32)
    return out + lax.dot_general(x3, m, dims, preferred_element_type=F32)


def _mm(name, a, b, dims, grid, a_spec, b_spec, o_spec, out_shape, out_dtype, acc_shape, res=None, r_spec=None,
        group=1, n=None, a_sel="full", b_sel="full", o_sel="full"):
    nk = grid[2]
    if out_dtype is None:
        out_dtype = BF16

    def at(sel, s):
        if sel == "lead":
            return (s,)
        if sel == "lanes":
            return (slice(None), slice(s * n, (s + 1) * n))
        return (Ellipsis,)

    def body(*refs):
        a_ref, b_ref = refs[0], refs[1]
        r_ref = refs[2] if res is not None else None
        o_ref = refs[3] if res is not None else refs[2]
        acc = refs[-1] if nk > 1 else None
        k = pl.program_id(2)

        def emit(s, val):
            if nk == 1:
                if r_ref is not None:
                    val = val + r_ref[...].astype(F32)
                o_ref[at(o_sel, s)] = val.astype(out_dtype)
                return

            @pl.when(k == 0)
            def _():
                acc[at(o_sel, s)] = val

            @pl.when(k > 0)
            def _():
                acc[at(o_sel, s)] += val

        total = None
        for s in range(group):
            val = _dot(a_ref[at(a_sel, s)], b_ref[at(b_sel, s)], dims)
            if o_sel == "full":
                total = val if total is None else total + val
            else:
                emit(s, val)
        if o_sel == "full":
            emit(0, total)
        if nk > 1:
            @pl.when(k == nk - 1)
            def _():
                r = acc[...]
                if r_ref is not None:
                    r = r + r_ref[...].astype(F32)
                o_ref[...] = r.astype(out_dtype)

    in_specs = [a_spec, b_spec] + ([] if res is None else [r_spec])
    args = (a, b) + (() if res is None else (res,))
    return pl.pallas_call(
        body, name=name, grid=grid, in_specs=in_specs, out_specs=o_spec,
        out_shape=jax.ShapeDtypeStruct(out_shape, out_dtype),
        scratch_shapes=[pltpu.VMEM(acc_shape, F32)] if nk > 1 else [],
        compiler_params=_params(("parallel", "parallel", "arbitrary")),
    )(*args)


def mm_nn(name, a, b, res=None, koff=0, out_dtype=None):
    m, k = a.shape
    n = b.shape[1]
    tm, tn, tk = _tile(m, 1024), _tile(n, 1024, 128), _tile(k, 1024, 128)
    kb = koff // tk
    spec = pl.BlockSpec((tm, tn), lambda i, j, kk: (i, j))
    return _mm(name, a, b, _NN, (m // tm, n // tn, k // tk),
               pl.BlockSpec((tm, tk), lambda i, j, kk: (i, kk)),
               pl.BlockSpec((tk, tn), lambda i, j, kk: (kk + kb, j)),
               spec, (m, n), out_dtype, (tm, tn), res, spec)


def mm_nn_bs(name, a, bs, stacked_out=False, out_dtype=None):
    m, k = a.shape
    s, _, n = bs.shape
    tm, tk = _tile(m, 1024), _tile(k, 1024, 128)
    a_spec = pl.BlockSpec((tm, tk), lambda i, j, kk: (i, kk))
    if stacked_out:
        return _mm(name, a, bs, _NN, (m // tm, s, k // tk), a_spec,
                   pl.BlockSpec((None, tk, n), lambda i, j, kk: (j, kk, 0)),
                   pl.BlockSpec((None, tm, n), lambda i, j, kk: (j, i, 0)), (s, m, n), out_dtype, (tm, n))
    g = _tile(s, max(1, 1024 // n), 1)
    return _mm(name, a, bs, _NN, (m // tm, s // g, k // tk), a_spec,
               pl.BlockSpec((g, tk, n), lambda i, j, kk: (j, kk, 0)),
               pl.BlockSpec((tm, g * n), lambda i, j, kk: (i, j)), (m, s * n), out_dtype, (tm, g * n),
               group=g, n=n, b_sel="lead", o_sel="lanes")


def mm_as_nn(name, a_st, b3, res, out_dtype=F32):
    s, m, kp = a_st.shape
    n = b3.shape[2]
    tm, tn = _tile(m, 1024), _tile(n, 1024, 128)
    spec = pl.BlockSpec((tm, tn), lambda i, j, kk: (i, j))
    return _mm(name, a_st, b3, _NN, (m // tm, n // tn, s),
               pl.BlockSpec((None, tm, kp), lambda i, j, kk: (kk, i, 0)),
               pl.BlockSpec((None, kp, tn), lambda i, j, kk: (kk, 0, j)),
               spec, (m, n), out_dtype, (tm, tn), res, spec)


def mm_nt(name, dc, b, out_dtype=None):
    m, n = dc.shape
    k = b.shape[0]
    tm, tko, tnr = _tile(m, 1024), _tile(k, 1024, 128), _tile(n, 1024, 128)
    return _mm(name, dc, b, _NT, (m // tm, k // tko, n // tnr),
               pl.BlockSpec((tm, tnr), lambda i, j, kk: (i, kk)),
               pl.BlockSpec((tko, tnr), lambda i, j, kk: (j, kk)),
               pl.BlockSpec((tm, tko), lambda i, j, kk: (i, j)), (m, k), out_dtype, (tm, tko))


def mm_nt_bs(name, dc, bs, dc_stacked=False, out_dtype=None):
    s, k, n = bs.shape
    m = dc.shape[1] if dc_stacked else dc.shape[0]
    tm, tko = _tile(m, 1024), _tile(k, 1024, 128)
    o_spec = pl.BlockSpec((tm, tko), lambda i, j, kk: (i, j))
    if dc_stacked:
        return _mm(name, dc, bs, _NT, (m // tm, k // tko, s),
                   pl.BlockSpec((None, tm, n), lambda i, j, kk: (kk, i, 0)),
                   pl.BlockSpec((None, tko, n), lambda i, j, kk: (kk, j, 0)), o_spec, (m, k), out_dtype, (tm, tko))
    g = _tile(s, max(1, 2048 // n), 1)
    return _mm(name, dc, bs, _NT, (m // tm, k // tko, s // g),
               pl.BlockSpec((tm, g * n), lambda i, j, kk: (i, kk)),
               pl.BlockSpec((g, tko, n), lambda i, j, kk: (kk, j, 0)), o_spec, (m, k), out_dtype, (tm, tko),
               group=g, n=n, a_sel="lanes", b_sel="lead")


def mm_nt_os(name, dc, b3, out_dtype=None):
    m, n = dc.shape
    s, kp, _ = b3.shape
    tm, tnr = _tile(m, 1024), _tile(n, 1024, 128)
    return _mm(name, dc, b3, _NT, (m // tm, s, n // tnr),
               pl.BlockSpec((tm, tnr), lambda i, j, kk: (i, kk)),
               pl.BlockSpec((None, kp, tnr), lambda i, j, kk: (j, 0, kk)),
               pl.BlockSpec((None, tm, kp), lambda i, j, kk: (j, i, 0)), (s, m, kp), out_dtype, (tm, kp))


def mm_tn(name, a, dc, a_stacked=False, dc_cols=None, dc_stacked=False, out_dtype=None):
    if a_stacked:
        s, m, kp = a.shape
        n = dc.shape[1]
        tno, tmr = _tile(n, 1024, 128), _tile(m, 1024)
        return _mm(name, a, dc, _TN, (s, n // tno, m // tmr),
                   pl.BlockSpec((None, tmr, kp), lambda i, j, kk: (i, kk, 0)),
                   pl.BlockSpec((tmr, tno), lambda i, j, kk: (kk, j)),
                   pl.BlockSpec((None, kp, tno), lambda i, j, kk: (i, 0, j)), (s, kp, n), out_dtype, (kp, tno))
    m, k = a.shape
    tko, tmr = _tile(k, 1024, 128), _tile(m, 1024)
    a_spec = pl.BlockSpec((tmr, tko), lambda i, j, kk: (kk, i))
    if dc_stacked:
        s, _, n = dc.shape
        return _mm(name, a, dc, _TN, (k // tko, s, m // tmr), a_spec,
                   pl.BlockSpec((None, tmr, n), lambda i, j, kk: (j, kk, 0)),
                   pl.BlockSpec((None, tko, n), lambda i, j, kk: (j, i, 0)), (s, k, n), out_dtype, (tko, n))
    if dc_cols is not None:
        n = dc_cols
        s = dc.shape[1] // n
        g = _tile(s, max(1, 1024 // n), 1)
        return _mm(name, a, dc, _TN, (k // tko, s // g, m // tmr), a_spec,
                   pl.BlockSpec((tmr, g * n), lambda i, j, kk: (kk, j)),
                   pl.BlockSpec((g, tko, n), lambda i, j, kk: (j, i, 0)), (s, k, n), out_dtype, (g, tko, n),
                   group=g, n=n, b_sel="lanes", o_sel="lead")
    n = dc.shape[1]
    tno = _tile(n, 1024, 128)
    return _mm(name, a, dc, _TN, (k // tko, n // tno, m // tmr), a_spec,
               pl.BlockSpec((tmr, tno), lambda i, j, kk: (kk, j)),
               pl.BlockSpec((tko, tno), lambda i, j, kk: (i, j)), (k, n), out_dtype, (tko, tno))


def rms_fwd(name, x, g):
    t, d = x.shape
    tr = _tile(t, 512)

    def body(x_ref, g_ref, o_ref):
        xf = x_ref[...]
        r = lax.rsqrt(jnp.mean(xf * xf, axis=-1, keepdims=True) + EPS)
        o_ref[...] = (xf * r * g_ref[...]).astype(o_ref.dtype)

    return pl.pallas_call(
        body, name=name, grid=(t // tr,),
        in_specs=[pl.BlockSpec((tr, d), lambda i: (i, 0)), pl.BlockSpec((1, d), lambda i: (0, 0))],
        out_specs=pl.BlockSpec((tr, d), lambda i: (i, 0)),
        out_shape=jax.ShapeDtypeStruct((t, d), BF16), compiler_params=_params(("parallel",)),
    )(x, g.reshape(1, d))


def rms_bwd(name, x, g, dh, dres=None, need_dx=True):
    t, d = x.shape
    tr = _tile(t, 512)

    def body(*refs):
        refs = list(refs)
        x_ref, g_ref, dh_ref = refs[:3]
        r_ref = refs[3] if dres is not None else None
        outs = refs[4:] if dres is not None else refs[3:]
        dx_ref, dg_ref = (outs[0], outs[1]) if need_dx else (None, outs[0])
        i = pl.program_id(0)

        @pl.when(i == 0)
        def _():
            dg_ref[...] = jnp.zeros_like(dg_ref)

        xf = x_ref[...]
        dhf = dh_ref[...].astype(F32)
        r = lax.rsqrt(jnp.mean(xf * xf, axis=-1, keepdims=True) + EPS)
        xh = xf * r
        dg_ref[...] += jnp.sum(dhf * xh, axis=0, keepdims=True)
        if need_dx:
            dxh = dhf * g_ref[...]
            dx = r * (dxh - xh * jnp.mean(dxh * xh, axis=-1, keepdims=True))
            if r_ref is not None:
                dx = dx + r_ref[...]
            dx_ref[...] = dx

    row = pl.BlockSpec((tr, d), lambda i: (i, 0))
    vec = pl.BlockSpec((1, d), lambda i: (0, 0))
    in_specs = [row, vec, row] + ([row] if dres is not None else [])
    args = (x, g.reshape(1, d), dh) + ((dres,) if dres is not None else ())
    out_specs = ([row] if need_dx else []) + [vec]
    out_shape = ([jax.ShapeDtypeStruct((t, d), F32)] if need_dx else []) + [jax.ShapeDtypeStruct((1, d), F32)]
    res = pl.pallas_call(
        body, name=name, grid=(t // tr,), in_specs=in_specs, out_specs=out_specs, out_shape=out_shape,
        compiler_params=_params(("arbitrary",)),
    )(*args)
    return res if need_dx else (None, res[0])


def loss_head(x, g, tgt):
    t, d = x.shape
    tr = _tile(t, 512)

    def body(x_ref, g_ref, t_ref, l_ref, dx_ref, dg_ref):
        i = pl.program_id(0)

        @pl.when(i == 0)
        def _():
            l_ref[...] = jnp.zeros_like(l_ref)
            dg_ref[...] = jnp.zeros_like(dg_ref)

        xf = x_ref[...]
        r = lax.rsqrt(jnp.mean(xf * xf, axis=-1, keepdims=True) + EPS)
        xh = xf * r
        diff = xh * g_ref[...] - t_ref[...]
        l_ref[...] += 0.5 * jnp.sum(jnp.mean(diff * diff, axis=-1, keepdims=True))
        dy = diff * (1.0 / d)
        dg_ref[...] += jnp.sum(dy * xh, axis=0, keepdims=True)
        dxh = dy * g_ref[...]
        dx_ref[...] = r * (dxh - xh * jnp.mean(dxh * xh, axis=-1, keepdims=True))

    row = pl.BlockSpec((tr, d), lambda i: (i, 0))
    vec = pl.BlockSpec((1, d), lambda i: (0, 0))
    return pl.pallas_call(
        body, name="loss_head", grid=(t // tr,), in_specs=[row, vec, row],
        out_specs=[pl.BlockSpec((1, 128), lambda i: (0, 0)), row, vec],
        out_shape=[jax.ShapeDtypeStruct((1, 128), F32), jax.ShapeDtypeStruct((t, d), F32),
                   jax.ShapeDtypeStruct((1, d), F32)],
        compiler_params=_params(("arbitrary",)),
    )(x, g.reshape(1, d), tgt)


def glu_fwd(glu, x):
    t, d = x.shape
    tr = _tile(t, 512)

    def body(v_ref, g_ref, x_ref, o_ref):
        o_ref[...] = x_ref[...] + v_ref[...] * jax.nn.sigmoid(g_ref[...])

    return pl.pallas_call(
        body, name="glu_fwd", grid=(t // tr,),
        in_specs=[pl.BlockSpec((tr, d), lambda i: (i, 0)), pl.BlockSpec((tr, d), lambda i: (i, 1)),
                  pl.BlockSpec((tr, d), lambda i: (i, 0))],
        out_specs=pl.BlockSpec((tr, d), lambda i: (i, 0)),
        out_shape=jax.ShapeDtypeStruct((t, d), F32), compiler_params=_params(("parallel",)),
    )(glu, glu, x)


def glu_bwd(glu, dmix):
    t, d = dmix.shape
    tr = _tile(t, 512)

    def body(v_ref, g_ref, d_ref, o_ref):
        sg = jax.nn.sigmoid(g_ref[...])
        dm = d_ref[...]
        o_ref[:, :d] = (dm * sg).astype(o_ref.dtype)
        o_ref[:, d:] = (dm * v_ref[...] * sg * (1.0 - sg)).astype(o_ref.dtype)

    return pl.pallas_call(
        body, name="glu_bwd", grid=(t // tr,),
        in_specs=[pl.BlockSpec((tr, d), lambda i: (i, 0)), pl.BlockSpec((tr, d), lambda i: (i, 1)),
                  pl.BlockSpec((tr, d), lambda i: (i, 0))],
        out_specs=pl.BlockSpec((tr, 2 * d), lambda i: (i, 0)),
        out_shape=jax.ShapeDtypeStruct((t, 2 * d), BF16), compiler_params=_params(("parallel",)),
    )(glu, glu, dmix)


def _log_sigmoid(z):
    return jnp.minimum(z, 0.0) - jnp.log(1.0 + jnp.exp(-jnp.abs(z)))


def _head_masks(shape):
    lane = lax.broadcasted_iota(jnp.int32, shape, 1)
    return lane < SB_HEAD_DIM


def _stack_heads(xf, is_a):
    return jnp.concatenate([jnp.where(is_a, xf, 0.0), jnp.where(is_a, 0.0, xf)], axis=0).astype(MXU_DTYPE)


def _diag_mask(qb):
    row = lax.broadcasted_iota(jnp.int32, (2 * qb, qb), 0) & (qb - 1)
    col = lax.broadcasted_iota(jnp.int32, (2 * qb, qb), 1)
    return col < row


def _tri01(qb, pred):
    j = lax.broadcasted_iota(jnp.int32, (qb, qb), 0)
    s = lax.broadcasted_iota(jnp.int32, (qb, qb), 1)
    m = pred(j, s).astype(BF16)
    return jnp.concatenate([m, m], axis=0)


def _split_cat(x):
    hi = x.astype(BF16)
    lo = (x - hi.astype(F32)).astype(BF16)
    return jnp.concatenate([hi, lo], axis=1)


def sb_attn_fwd(proj, bsz, seq):
    qb = SB_BLOCK
    nq = seq // qb
    npair = SB_WIDTH // 128
    scale = SB_HEAD_DIM ** -0.5

    def body(q_ref, k_ref, v_ref, o_ref, r_ref):
        qi = pl.program_id(2)
        is_a = _head_masks((qb, 128))
        q2 = _stack_heads(q_ref[...], is_a)
        diag = _diag_mask(qb)
        upper = _tri01(qb, lambda j, s: j > s)

        def block(kbi, acc, run, masked):
            ks = pl.ds(pl.multiple_of(kbi * qb, qb), qb)
            kblk = k_ref[ks, :].astype(MXU_DTYPE)
            vblk = v_ref[ks, :].astype(MXU_DTYPE)
            z = lax.dot_general(q2, kblk, _NT, preferred_element_type=F32) * scale
            lb = _log_sigmoid(z)
            lk = lb - z
            if masked:
                lk = jnp.where(diag, lk, 0.0)
            after = run + lax.dot_general(_split_cat(lk), upper, _NN, preferred_element_type=F32)
            w = jnp.exp(lb + after)
            if masked:
                w = jnp.where(diag, w, 0.0)
            acc = acc + lax.dot_general(w.astype(MXU_DTYPE), vblk, _NN, preferred_element_type=F32)
            return acc, run + jnp.sum(lk, axis=1, keepdims=True)

        carry = block(qi, jnp.zeros((2 * qb, 128), F32), jnp.zeros((2 * qb, 1), F32), True)
        acc, run = lax.fori_loop(0, qi, lambda i, c: block(qi - 1 - i, c[0], c[1], False), carry)
        o_ref[...] = jnp.where(is_a, acc[:qb], acc[qb:]).astype(o_ref.dtype)
        r_ref[...] = jnp.where(is_a, run[:qb], run[qb:])

    return pl.pallas_call(
        body, name="sb_attn_fwd", grid=(bsz, npair, nq),
        in_specs=[pl.BlockSpec((qb, 128), lambda b, p, i: (b * nq + i, p)),
                  pl.BlockSpec((seq, 128), lambda b, p, i: (b, npair + p)),
                  pl.BlockSpec((seq, 128), lambda b, p, i: (b, 2 * npair + p))],
        out_specs=[pl.BlockSpec((qb, 128), lambda b, p, i: (b * nq + i, p)),
                   pl.BlockSpec((qb, 128), lambda b, p, i: (b * nq + i, p))],
        out_shape=[jax.ShapeDtypeStruct((bsz * seq, SB_WIDTH), BF16),
                   jax.ShapeDtypeStruct((bsz * seq, SB_WIDTH), F32)],
        compiler_params=_params(("parallel", "parallel", "arbitrary")),
    )(proj, proj, proj)


def sb_attn_bwd(proj, rsum, dcat, bsz, seq):
    qb = SB_BLOCK
    nq = seq // qb
    npair = SB_WIDTH // 128
    scale = SB_HEAD_DIM ** -0.5

    def body(q_ref, k_ref, v_ref, r_ref, do_ref, dq_ref, dk_ref, dv_ref):
        qi = pl.program_id(2)

        @pl.when(qi == 0)
        def _():
            dk_ref[...] = jnp.zeros_like(dk_ref)
            dv_ref[...] = jnp.zeros_like(dv_ref)

        is_a = _head_masks((qb, 128))
        q2 = _stack_heads(q_ref[...], is_a)
        do2 = _stack_heads(do_ref[...].astype(F32), is_a)
        rf = r_ref[...]
        rtot = jnp.concatenate([rf[:, 0:1], rf[:, SB_HEAD_DIM:SB_HEAD_DIM + 1]], axis=0)
        diag = _diag_mask(qb)
        incl = _tri01(qb, lambda j, s: j <= s)
        strict = _tri01(qb, lambda j, s: j < s)

        def block(kbi, dq, pre, epre, masked):
            ks = pl.ds(pl.multiple_of(kbi * qb, qb), qb)
            kblk = k_ref[ks, :].astype(MXU_DTYPE)
            vblk = v_ref[ks, :].astype(MXU_DTYPE)
            z = lax.dot_general(q2, kblk, _NT, preferred_element_type=F32) * scale
            lb = _log_sigmoid(z)
            lk = lb - z
            if masked:
                lk = jnp.where(diag, lk, 0.0)
            after = rtot - (pre + lax.dot_general(_split_cat(lk), incl, _NN, preferred_element_type=F32))
            w = jnp.exp(lb + after)
            if masked:
                w = jnp.where(diag, w, 0.0)
            dw = lax.dot_general(do2, vblk, _NT, preferred_element_type=F32)
            e = dw * w
            ecum = epre + lax.dot_general(_split_cat(e), strict, _NN, preferred_element_type=F32)
            beta = jnp.exp(lb)
            dz = (e - beta * (e + ecum)) * scale
            if masked:
                dz = jnp.where(diag, dz, 0.0)
            dz = dz.astype(MXU_DTYPE)
            dq = dq + lax.dot_general(dz, kblk, _NN, preferred_element_type=F32)
            dk_ref[ks, :] += lax.dot_general(dz, q2, _TN, preferred_element_type=F32)
            dv_ref[ks, :] += lax.dot_general(w.astype(MXU_DTYPE), do2, _TN, preferred_element_type=F32)
            return dq, pre + jnp.sum(lk, axis=1, keepdims=True), epre + jnp.sum(e, axis=1, keepdims=True)

        zc = jnp.zeros((2 * qb, 1), F32)
        carry = lax.fori_loop(0, qi, lambda kbi, c: block(kbi, c[0], c[1], c[2], False),
                              (jnp.zeros((2 * qb, 128), F32), zc, zc))
        dq = block(qi, carry[0], carry[1], carry[2], True)[0]
        dq_ref[...] = jnp.where(is_a, dq[:qb], dq[qb:])

    full = jax.ShapeDtypeStruct((bsz * seq, SB_WIDTH), F32)
    qspec = pl.BlockSpec((qb, 128), lambda b, p, i: (b * nq + i, p))
    return pl.pallas_call(
        body, name="sb_attn_bwd", grid=(bsz, npair, nq),
        in_specs=[qspec,
                  pl.BlockSpec((seq, 128), lambda b, p, i: (b, npair + p)),
                  pl.BlockSpec((seq, 128), lambda b, p, i: (b, 2 * npair + p)),
                  qspec, qspec],
        out_specs=[qspec, pl.BlockSpec((seq, 128), lambda b, p, i: (b, p)),
                   pl.BlockSpec((seq, 128), lambda b, p, i: (b, p))],
        out_shape=[full, full, full],
        compiler_params=_params(("parallel", "parallel", "arbitrary")),
    )(proj, proj, proj, rsum, dcat)


def _window_sums(x, forward):
    n = x.shape[0]
    out = []
    s = x
    for sh in (1, 2, 4, 8):
        s = s + pltpu.roll(s, (n - sh) if forward else sh, 0)
        out.append(s)
    return out


def _pool_counts(tc, c, w):
    t = lax.broadcasted_iota(jnp.int32, (tc, 1), 0) + c * tc
    return jnp.minimum(t + 1, w).astype(F32)


def pool_fwd(proj, pool_w, pool_scale, bsz, seq):
    tc = _tile(seq, 512)
    nc = seq // tc
    hb = tc // POOL_HALO
    ucol = 3

    def body(u_ref, prev_ref, w_ref, s_ref, o_ref):
        c = pl.program_id(1)
        prev = jnp.where(c > 0, prev_ref[...], 0.0)
        x = jnp.concatenate([prev, u_ref[...]], axis=0)
        sums = _window_sums(x, forward=False)
        for g, win in enumerate(POOL_WINDOWS):
            ls = slice(g * POOL_GROUP, (g + 1) * POOL_GROUP)
            pooled = sums[g][POOL_HALO:, ls] / _pool_counts(tc, c, win) - x[POOL_HALO:, ls]
            y = _dot(pooled, w_ref[g], _NN)
            o_ref[:, ls] = (y * s_ref[:, ls]).astype(o_ref.dtype)

    return pl.pallas_call(
        body, name="pool_fwd", grid=(bsz, nc),
        in_specs=[pl.BlockSpec((tc, SB_WIDTH), lambda b, c: (b * nc + c, ucol)),
                  pl.BlockSpec((POOL_HALO, SB_WIDTH), lambda b, c: (jnp.maximum((b * nc + c) * hb - 1, 0), ucol)),
                  pl.BlockSpec((4, POOL_GROUP, POOL_GROUP), lambda b, c: (0, 0, 0)),
                  pl.BlockSpec((1, SB_WIDTH), lambda b, c: (0, 0))],
        out_specs=pl.BlockSpec((tc, SB_WIDTH), lambda b, c: (b * nc + c, 0)),
        out_shape=jax.ShapeDtypeStruct((bsz * seq, SB_WIDTH), BF16),
        compiler_params=_params(("parallel", "parallel")),
    )(proj, proj, pool_w, pool_scale)


def pool_bwd(proj, pool_w, pool_scale, dcat, bsz, seq):
    tc = _tile(seq, 512)
    nc = seq // tc
    hb = tc // POOL_HALO
    nblk = bsz * seq // POOL_HALO
    ucol = 3

    def body(u_ref, prev_ref, dy_ref, nxt_ref, w_ref, s_ref, du_ref, dw_ref, ds_ref):
        b, c = pl.program_id(0), pl.program_id(1)

        @pl.when((b == 0) & (c == 0))
        def _():
            dw_ref[...] = jnp.zeros_like(dw_ref)
            ds_ref[...] = jnp.zeros_like(ds_ref)

        prev = jnp.where(c > 0, prev_ref[...], 0.0)
        x = jnp.concatenate([prev, u_ref[...]], axis=0)
        sums = _window_sums(x, forward=False)
        nxt = jnp.where(c < nc - 1, nxt_ref[...].astype(F32), 0.0)
        dy = jnp.concatenate([dy_ref[...].astype(F32), nxt], axis=0)
        tq = lax.broadcasted_iota(jnp.int32, (tc + POOL_HALO, 1), 0) + c * tc
        for g, win in enumerate(POOL_WINDOWS):
            ls = slice(g * POOL_GROUP, (g + 1) * POOL_GROUP)
            pooled = sums[g][POOL_HALO:, ls] / _pool_counts(tc, c, win) - x[POOL_HALO:, ls]
            y = _dot(pooled, w_ref[g], _NN)
            ds_ref[:, ls] += jnp.sum(dy[:tc, ls] * y, axis=0, keepdims=True)
            dz = dy[:, ls] * s_ref[:, ls]
            dw_ref[g] += _dot(pooled, dz[:tc], _TN)
            dpool = _dot(dz, w_ref[g], _NT)
            dmean = dpool / jnp.minimum(tq + 1, win).astype(F32)
            fsum = _window_sums(dmean, forward=True)[g]
            du_ref[:, ls] = fsum[:tc] - dpool[:tc]

    return pl.pallas_call(
        body, name="pool_bwd", grid=(bsz, nc),
        in_specs=[pl.BlockSpec((tc, SB_WIDTH), lambda b, c: (b * nc + c, ucol)),
                  pl.BlockSpec((POOL_HALO, SB_WIDTH), lambda b, c: (jnp.maximum((b * nc + c) * hb - 1, 0), ucol)),
                  pl.BlockSpec((tc, SB_WIDTH), lambda b, c: (b * nc + c, 1)),
                  pl.BlockSpec((POOL_HALO, SB_WIDTH), lambda b, c: (jnp.minimum((b * nc + c + 1) * hb, nblk - 1), 1)),
                  pl.BlockSpec((4, POOL_GROUP, POOL_GROUP), lambda b, c: (0, 0, 0)),
                  pl.BlockSpec((1, SB_WIDTH), lambda b, c: (0, 0))],
        out_specs=[pl.BlockSpec((tc, SB_WIDTH), lambda b, c: (b * nc + c, 0)),
                   pl.BlockSpec((4, POOL_GROUP, POOL_GROUP), lambda b, c: (0, 0, 0)),
                   pl.BlockSpec((1, SB_WIDTH), lambda b, c: (0, 0))],
        out_shape=[jax.ShapeDtypeStruct((bsz * seq, SB_WIDTH), F32),
                   jax.ShapeDtypeStruct((4, POOL_GROUP, POOL_GROUP), F32),
                   jax.ShapeDtypeStruct((1, SB_WIDTH), F32)],
        compiler_params=_params(("arbitrary", "arbitrary")),
    )(proj, proj, dcat, dcat, pool_w, pool_scale)


def _lbar(lam_re, lam_im, log_dt):
    dt = jnp.exp(log_dt)
    mag = jnp.exp(lam_re * dt)
    ang = lam_im * dt
    return mag * jnp.cos(ang), mag * jnp.sin(ang)


def _bbar(lam_re, lam_im, log_dt, b_re, b_im):
    lb_re, lb_im = _lbar(lam_re, lam_im, log_dt)
    n_re = lb_re - 1.0
    den = lam_re * lam_re + lam_im * lam_im
    coef_re = (n_re * lam_re + lb_im * lam_im) / den
    coef_im = (lb_im * lam_re - n_re * lam_im) / den
    return coef_re * b_re - coef_im * b_im, coef_re * b_im + coef_im * b_re


def _expand01():
    p = lax.broadcasted_iota(jnp.int32, (64, 1024), 0)
    q = lax.broadcasted_iota(jnp.int32, (64, 1024), 1)
    return (lax.shift_right_logical(q, 4) == p).astype(BF16)


def ssm_prep(lam_re, lam_im, log_dt, b_re2, b_im2):
    def body(lr_ref, li_ref, dt_ref, br_ref, bi_ref, ar_ref, ai_ref, bbr_ref, bbi_ref):
        e = _expand01()
        lr, li, dt = lr_ref[...], li_ref[...], dt_ref[...]
        ar_ref[...], ai_ref[...] = _lbar(lr, li, dt)
        bbr_ref[...], bbi_ref[...] = _bbar(_dot_exact01(lr, e), _dot_exact01(li, e), dt, br_ref[...], bi_ref[...])

    s64 = jax.ShapeDtypeStruct((64, 64), F32)
    s1k = jax.ShapeDtypeStruct((64, 1024), F32)
    return pl.pallas_call(body, name="ssm_prep", out_shape=[s64, s64, s1k, s1k], compiler_params=_params())(
        lam_re, lam_im, log_dt, b_re2, b_im2)


def ssm_prep_bwd(lam_re, lam_im, log_dt, b_re2, b_im2, da_re, da_im, dbb_re, dbb_im):
    def body(lr_ref, li_ref, dt_ref, br_ref, bi_ref, dar_ref, dai_ref, dbr_ref, dbi_ref,
             olr_ref, oli_ref, odt_ref, obr_ref, obi_ref):
        e = _expand01()
        lr, li, dt = lr_ref[...], li_ref[...], dt_ref[...]
        _, vjp_a = jax.vjp(_lbar, lr, li, dt)
        g_lr, g_li, g_dt = vjp_a((dar_ref[...], dai_ref[...]))
        _, vjp_b = jax.vjp(_bbar, _dot_exact01(lr, e), _dot_exact01(li, e), dt, br_ref[...], bi_ref[...])
        x_lr, x_li, x_dt, g_br, g_bi = vjp_b((dbr_ref[...], dbi_ref[...]))
        olr_ref[...] = g_lr + _dot_exact01(x_lr, e, _NT)
        oli_ref[...] = g_li + _dot_exact01(x_li, e, _NT)
        odt_ref[...] = g_dt + x_dt
        obr_ref[...] = g_br
        obi_ref[...] = g_bi

    s64 = jax.ShapeDtypeStruct((64, 64), F32)
    s1k = jax.ShapeDtypeStruct((64, 1024), F32)
    return pl.pallas_call(body, name="ssm_prep_bwd",
                          out_shape=[s64, s64, jax.ShapeDtypeStruct((64, 1), F32), s1k, s1k],
                          compiler_params=_params())(
        lam_re, lam_im, log_dt, b_re2, b_im2, da_re, da_im, dbb_re, dbb_im)


def _gelu(y):
    c = math.sqrt(2.0 / math.pi)
    return 0.5 * y * (1.0 + jnp.tanh(c * (y + 0.044715 * y * y * y)))


def _gelu_grad(y):
    c = math.sqrt(2.0 / math.pi)
    th = jnp.tanh(c * (y + 0.044715 * y * y * y))
    return 0.5 * (1.0 + th) + 0.5 * y * (1.0 - th * th) * c * (1.0 + 3.0 * 0.044715 * y * y)


def _slab(t):
    return pl.ds(pl.multiple_of(t * SSM_SLAB, SSM_SLAB), SSM_SLAB)


def _st_store(ref, i, tc, val):
    for q in range(4):
        ref[pl.ds(4 * i + q, tc, stride=SSM_SLAB), :] = val[:, 128 * q:128 * (q + 1)]


def _st_load(ref, i, tc):
    return jnp.concatenate([ref[pl.ds(4 * i + q, tc, stride=SSM_SLAB), :] for q in range(4)], axis=1)


def ssm_fwd(u, wt, ct, a_re, a_im, dskip, bsz, seq):
    tc = _tile(seq, 256)
    nc = seq // tc
    ns = SSM_TILE_STATES

    def body(u_ref, wt_ref, ct_ref, ar_ref, ai_ref, d_ref, y_ref, gl_ref, hr_ref, hi_ref, sr_ref, si_ref):
        c = pl.program_id(1)

        @pl.when(c == 0)
        def _():
            sr_ref[...] = jnp.zeros_like(sr_ref)
            si_ref[...] = jnp.zeros_like(si_ref)

        uf = u_ref[...]
        for i in range(SSM_TILES):
            bu = _dot(uf[:, i * 128:(i + 1) * 128], wt_ref[i], _NN)
            _st_store(hr_ref, i, tc, bu[:, :ns])
            _st_store(hi_ref, i, tc, bu[:, ns:])
        ar, ai = ar_ref[...], ai_ref[...]

        def step(t, carry):
            sr, si = carry
            nr = ar * sr - ai * si + hr_ref[_slab(t), :]
            ni = ar * si + ai * sr + hi_ref[_slab(t), :]
            hr_ref[_slab(t), :] = nr
            hi_ref[_slab(t), :] = ni
            return nr, ni

        sr, si = lax.fori_loop(0, tc, step, (sr_ref[...], si_ref[...]), unroll=4)
        sr_ref[...] = sr
        si_ref[...] = si
        for i in range(SSM_TILES):
            hcat = jnp.concatenate([_st_load(hr_ref, i, tc), _st_load(hi_ref, i, tc)], axis=1)
            ls = slice(i * 128, (i + 1) * 128)
            y = _dot(hcat, ct_ref[i], _NN) + d_ref[:, ls] * uf[:, ls]
            y_ref[:, ls] = y
            gl_ref[:, ls] = _gelu(y).astype(gl_ref.dtype)

    t = bsz * seq
    row = pl.BlockSpec((tc, D_MODEL), lambda b, c: (b * nc + c, 0))
    st = pl.BlockSpec((tc * SSM_SLAB, 128), lambda b, c: (b * nc + c, 0))
    slab = pl.BlockSpec((SSM_SLAB, 128), lambda b, c: (0, 0))
    return pl.pallas_call(
        body, name="ssm_fwd", grid=(bsz, nc),
        in_specs=[row, pl.BlockSpec((SSM_TILES, 128, 2 * ns), lambda b, c: (0, 0, 0)),
                  pl.BlockSpec((SSM_TILES, 2 * ns, 128), lambda b, c: (0, 0, 0)), slab, slab,
                  pl.BlockSpec((1, D_MODEL), lambda b, c: (0, 0))],
        out_specs=[row, row, st, st],
        out_shape=[jax.ShapeDtypeStruct((t, D_MODEL), F32), jax.ShapeDtypeStruct((t, D_MODEL), BF16),
                   jax.ShapeDtypeStruct((t * SSM_SLAB, 128), F32), jax.ShapeDtypeStruct((t * SSM_SLAB, 128), F32)],
        scratch_shapes=[pltpu.VMEM((SSM_SLAB, 128), F32), pltpu.VMEM((SSM_SLAB, 128), F32)],
        compiler_params=_params(("parallel", "arbitrary")),
    )(u, wt, ct, a_re, a_im, dskip)


def ssm_bwd(dgl, y, u, h_re, h_im, wt, ct, a_re, a_im, dskip, bsz, seq):
    tc = _tile(seq, 256)
    nc = seq // tc
    ns = SSM_TILE_STATES

    def body(dgl_ref, y_ref, u_ref, hr_ref, hi_ref, pr_ref, pi_ref, wt_ref, ct_ref, ar_ref, ai_ref, d_ref,
             du_ref, dwt_ref, dct_ref, dd_ref, dar_ref, dai_ref, gr_ref, gi_ref, sr_ref, si_ref):
        b, c = pl.program_id(0), pl.program_id(1)

        @pl.when((b == 0) & (c == 0))
        def _():
            dwt_ref[...] = jnp.zeros_like(dwt_ref)
            dct_ref[...] = jnp.zeros_like(dct_ref)
            dd_ref[...] = jnp.zeros_like(dd_ref)
            dar_ref[...] = jnp.zeros_like(dar_ref)
            dai_ref[...] = jnp.zeros_like(dai_ref)

        @pl.when(c == 0)
        def _():
            sr_ref[...] = jnp.zeros_like(sr_ref)
            si_ref[...] = jnp.zeros_like(si_ref)

        uf = u_ref[...]
        dy = dgl_ref[...].astype(F32) * _gelu_grad(y_ref[...])
        dd_ref[...] += jnp.sum(dy * uf, axis=0, keepdims=True)
        for i in range(SSM_TILES):
            dyi = dy[:, i * 128:(i + 1) * 128]
            dh = _dot(dyi, ct_ref[i], _NT)
            _st_store(gr_ref, i, tc, dh[:, :ns])
            _st_store(gi_ref, i, tc, dh[:, ns:])
            hcat = jnp.concatenate([_st_load(hr_ref, i, tc), _st_load(hi_ref, i, tc)], axis=1)
            dct_ref[i] += _dot(hcat, dyi, _TN)
        ar, ai = ar_ref[...], ai_ref[...]

        def one(t, gr, gi, hpr, hpi, dar, dai):
            nr = gr_ref[_slab(t), :] + ar * gr + ai * gi
            ni = gi_ref[_slab(t), :] - ai * gr + ar * gi
            gr_ref[_slab(t), :] = nr
            gi_ref[_slab(t), :] = ni
            return nr, ni, dar + nr * hpr + ni * hpi, dai + ni * hpr - nr * hpi

        def step(j, carry):
            t = tc - 1 - j
            gr, gi, dar, dai = carry
            return one(t, gr, gi, hr_ref[_slab(t - 1), :], hi_ref[_slab(t - 1), :], dar, dai)

        carry = lax.fori_loop(0, tc - 1, step, (sr_ref[...], si_ref[...], dar_ref[...], dai_ref[...]), unroll=2)
        first = c == nc - 1
        hpr = jnp.where(first, 0.0, pr_ref[...])
        hpi = jnp.where(first, 0.0, pi_ref[...])
        gr, gi, dar, dai = one(0, *carry[:2], hpr, hpi, *carry[2:])
        sr_ref[...] = gr
        si_ref[...] = gi
        dar_ref[...] = dar
        dai_ref[...] = dai
        for i in range(SSM_TILES):
            ls = slice(i * 128, (i + 1) * 128)
            gcat = jnp.concatenate([_st_load(gr_ref, i, tc), _st_load(gi_ref, i, tc)], axis=1)
            du_ref[:, ls] = (_dot(gcat, wt_ref[i], _NT) + d_ref[:, ls] * dy[:, ls]).astype(du_ref.dtype)
            dwt_ref[i] += _dot(uf[:, ls], gcat, _TN)

    t = bsz * seq
    rev = lambda b, c: (b * nc + (nc - 1 - c), 0)
    row = pl.BlockSpec((tc, D_MODEL), rev)
    st = pl.BlockSpec((tc * SSM_SLAB, 128), rev)
    prev = pl.BlockSpec((SSM_SLAB, 128), lambda b, c: (jnp.maximum((b * nc + (nc - 1 - c)) * tc - 1, 0), 0))
    slab = pl.BlockSpec((SSM_SLAB, 128), lambda b, c: (0, 0))
    wts = pl.BlockSpec((SSM_TILES, 128, 2 * ns), lambda b, c: (0, 0, 0))
    cts = pl.BlockSpec((SSM_TILES, 2 * ns, 128), lambda b, c: (0, 0, 0))
    vec = pl.BlockSpec((1, D_MODEL), lambda b, c: (0, 0))
    return pl.pallas_call(
        body, name="ssm_bwd", grid=(bsz, nc),
        in_specs=[row, row, row, st, st, prev, prev, wts, cts, slab, slab, vec],
        out_specs=[row, wts, cts, vec, slab, slab],
        out_shape=[jax.ShapeDtypeStruct((t, D_MODEL), BF16),
                   jax.ShapeDtypeStruct((SSM_TILES, 128, 2 * ns), F32),
                   jax.ShapeDtypeStruct((SSM_TILES, 2 * ns, 128), F32),
                   jax.ShapeDtypeStruct((1, D_MODEL), F32),
                   jax.ShapeDtypeStruct((SSM_SLAB, 128), F32), jax.ShapeDtypeStruct((SSM_SLAB, 128), F32)],
        scratch_shapes=[pltpu.VMEM((tc * SSM_SLAB, 128), F32), pltpu.VMEM((tc * SSM_SLAB, 128), F32),
                        pltpu.VMEM((SSM_SLAB, 128), F32), pltpu.VMEM((SSM_SLAB, 128), F32)],
        compiler_params=_params(("arbitrary", "arbitrary")),
    )(dgl, y, u, h_re, h_im, h_re, h_im, wt, ct, a_re, a_im, dskip)


def _ssm_in_weights(bb_re2, bb_im2):
    eye = jnp.eye(8, dtype=F32)[None, :, None, :, None]

    def one(bb):
        t = bb.reshape(8, 8, 64, 16).transpose(0, 1, 3, 2)
        return (t[:, :, :, None, :] * eye).reshape(8, 128, 512)

    return jnp.concatenate([one(bb_re2), one(bb_im2)], axis=-1).astype(MXU_DTYPE)


def _ssm_in_weights_bwd(dwt):
    eye = jnp.eye(8, dtype=F32)[None, :, None, :, None]

    def one(d):
        t = (d.reshape(8, 8, 16, 8, 64) * eye).sum(axis=3)
        return t.transpose(0, 1, 3, 2).reshape(64, 1024)

    return one(dwt[..., :512]), one(dwt[..., 512:])


def _ssm_out_weights(c_re, c_im):
    eye = jnp.eye(8, dtype=F32)[None, :, None, :, None]

    def one(cc):
        t = cc.reshape(8, 8, 16, 64).transpose(0, 1, 3, 2)
        return (t[:, :, :, None, :] * eye).reshape(8, 512, 128)

    return jnp.concatenate([one(c_re), -one(c_im)], axis=1).astype(MXU_DTYPE)


def _ssm_out_weights_bwd(dct):
    eye = jnp.eye(8, dtype=F32)[None, :, None, :, None]

    def one(d):
        t = (d.reshape(8, 8, 64, 8, 16) * eye).sum(axis=3)
        return t.transpose(0, 1, 3, 2).reshape(64, 16, 64)

    return one(dct[:, :512]), -one(dct[:, 512:])


def _softmax(s):
    m = jnp.max(s, axis=-1, keepdims=True)
    e = jnp.exp(s - m)
    return e / jnp.sum(e, axis=-1, keepdims=True)


def xattn_fwd(q, kv, bsz, seq):
    tq = _tile(seq, 512)
    nq = seq // tq
    scale = XA_HEAD_DIM ** -0.5

    def body(q_ref, k_ref, v_ref, o_ref):
        s = lax.dot_general(q_ref[...], k_ref[...], _NT, preferred_element_type=F32) * scale
        p = _softmax(s)
        o_ref[...] = _dot(p, v_ref[...], _NN).astype(o_ref.dtype)

    qs = pl.BlockSpec((tq, XA_HEAD_DIM), lambda b, h, i: (b * nq + i, h))
    return pl.pallas_call(
        body, name="xattn_fwd", grid=(bsz, XA_HEADS, nq),
        in_specs=[qs, pl.BlockSpec((MEM_LEN, XA_HEAD_DIM), lambda b, h, i: (b, h)),
                  pl.BlockSpec((MEM_LEN, XA_HEAD_DIM), lambda b, h, i: (b, XA_HEADS + h))],
        out_specs=qs, out_shape=jax.ShapeDtypeStruct((bsz * seq, D_MODEL), BF16),
        compiler_params=_params(("parallel", "parallel", "parallel")),
    )(q, kv, kv)


def xattn_bwd(q, kv, do, bsz, seq):
    tq = _tile(seq, 512)
    nq = seq // tq
    scale = XA_HEAD_DIM ** -0.5

    def body(q_ref, k_ref, v_ref, do_ref, dq_ref, dk_ref, dv_ref):
        @pl.when(pl.program_id(2) == 0)
        def _():
            dk_ref[...] = jnp.zeros_like(dk_ref)
            dv_ref[...] = jnp.zeros_like(dv_ref)

        qv, kk, vv, dov = q_ref[...], k_ref[...], v_ref[...], do_ref[...]
        s = lax.dot_general(qv, kk, _NT, preferred_element_type=F32) * scale
        p = _softmax(s)
        dp = lax.dot_general(dov, vv, _NT, preferred_element_type=F32)
        ds = (p * (dp - jnp.sum(dp * p, axis=-1, keepdims=True)) * scale).astype(MXU_DTYPE)
        dq_ref[...] = lax.dot_general(ds, kk, _NN, preferred_element_type=F32).astype(dq_ref.dtype)
        dk_ref[...] += lax.dot_general(ds, qv, _TN, preferred_element_type=F32)
        dv_ref[...] += lax.dot_general(p.astype(MXU_DTYPE), dov, _TN, preferred_element_type=F32)

    qs = pl.BlockSpec((tq, XA_HEAD_DIM), lambda b, h, i: (b * nq + i, h))
    ks = pl.BlockSpec((MEM_LEN, XA_HEAD_DIM), lambda b, h, i: (b, h))
    vs = pl.BlockSpec((MEM_LEN, XA_HEAD_DIM), lambda b, h, i: (b, XA_HEADS + h))
    dkv = jax.ShapeDtypeStruct((bsz * MEM_LEN, D_MODEL), F32)
    dq, dk, dv = pl.pallas_call(
        body, name="xattn_bwd", grid=(bsz, XA_HEADS, nq),
        in_specs=[qs, ks, vs, qs], out_specs=[qs, ks, ks],
        out_shape=[jax.ShapeDtypeStruct((bsz * seq, D_MODEL), BF16), dkv, dkv],
        compiler_params=_params(("parallel", "parallel", "arbitrary")),
    )(q, kv, kv, do)
    return dq, dk, dv


CONV_HALO = 16


def _shift_down(x, prev, n):
    r = pltpu.roll(x, n, 0)
    row = lax.broadcasted_iota(jnp.int32, x.shape, 0)
    last = prev.shape[0]
    for k in range(n):
        r = jnp.where(row == k, prev[last - n + k:last - n + k + 1, :], r)
    return r


def _shift_up(x, nxt, n):
    rows = x.shape[0]
    r = pltpu.roll(x, rows - n, 0)
    row = lax.broadcasted_iota(jnp.int32, x.shape, 0)
    for k in range(n):
        r = jnp.where(row == rows - n + k, nxt[k:k + 1, :], r)
    return r


def _conv_taps(u, prev, w, b):
    return b + w[2:3] * u + w[1:2] * _shift_down(u, prev, 1) + w[0:1] * _shift_down(u, prev, 2)


def conv_fwd(up, cw, cb, bsz, seq):
    tc = _tile(seq, 512)
    nc = seq // tc
    hb = tc // CONV_HALO
    half = N_DEV // 2

    def body(uv_ref, ug_ref, pv_ref, pg_ref, wv_ref, wg_ref, bv_ref, bg_ref, o_ref):
        c = pl.program_id(2)
        pv = jnp.where(c > 0, pv_ref[...].astype(F32), 0.0)
        pg = jnp.where(c > 0, pg_ref[...].astype(F32), 0.0)
        val = _conv_taps(uv_ref[...].astype(F32), pv, wv_ref[...], bv_ref[...])
        gate = _conv_taps(ug_ref[...].astype(F32), pg, wg_ref[...], bg_ref[...])
        o_ref[...] = (gate * jax.nn.sigmoid(gate) * val).astype(o_ref.dtype)

    def cur(off):
        return pl.BlockSpec((None, tc, FF_SHARD), lambda b, j, c: (j + off, b * nc + c, 0))

    def prv(off):
        return pl.BlockSpec((None, CONV_HALO, FF_SHARD), lambda b, j, c: (j + off, jnp.maximum((b * nc + c) * hb - 1, 0), 0))

    def par(rows, off):
        return pl.BlockSpec((None, rows, FF_SHARD), lambda b, j, c: (j + off, 0, 0))

    return pl.pallas_call(
        body, name="conv_fwd", grid=(bsz, half, nc),
        in_specs=[cur(0), cur(half), prv(0), prv(half), par(3, 0), par(3, half), par(1, 0), par(1, half)],
        out_specs=cur(0), out_shape=jax.ShapeDtypeStruct((half, bsz * seq, FF_SHARD), BF16),
        compiler_params=_params(("parallel", "parallel", "parallel")),
    )(up, up, up, up, cw, cw, cb, cb)


def conv_bwd_taps(up, cw, cb, dact, bsz, seq):
    tc = _tile(seq, 512)
    nc = seq // tc
    hb = tc // CONV_HALO
    half = N_DEV // 2

    def body(uv_ref, ug_ref, pv_ref, pg_ref, wv_ref, wg_ref, bv_ref, bg_ref, da_ref,
             dc_ref, dwv_ref, dwg_ref, dbv_ref, dbg_ref):
        b, c = pl.program_id(1), pl.program_id(2)

        @pl.when((b == 0) & (c == 0))
        def _():
            for r in (dwv_ref, dwg_ref, dbv_ref, dbg_ref):
                r[...] = jnp.zeros_like(r)

        pv = jnp.where(c > 0, pv_ref[...].astype(F32), 0.0)
        pg = jnp.where(c > 0, pg_ref[...].astype(F32), 0.0)
        uv, ug = uv_ref[...].astype(F32), ug_ref[...].astype(F32)
        val = _conv_taps(uv, pv, wv_ref[...], bv_ref[...])
        gate = _conv_taps(ug, pg, wg_ref[...], bg_ref[...])
        sg = jax.nn.sigmoid(gate)
        da = da_ref[...].astype(F32)
        dval = da * gate * sg
        dgate = da * val * sg * (1.0 + gate * (1.0 - sg))
        dc_ref[0] = dval.astype(dc_ref.dtype)
        dc_ref[1] = dgate.astype(dc_ref.dtype)
        for dcv, u, prev, dw_ref, db_ref in ((dval, uv, pv, dwv_ref, dbv_ref), (dgate, ug, pg, dwg_ref, dbg_ref)):
            db_ref[...] += jnp.sum(dcv, axis=0, keepdims=True)
            dw_ref[2:3, :] += jnp.sum(dcv * u, axis=0, keepdims=True)
            dw_ref[1:2, :] += jnp.sum(dcv * _shift_down(u, prev, 1), axis=0, keepdims=True)
            dw_ref[0:1, :] += jnp.sum(dcv * _shift_down(u, prev, 2), axis=0, keepdims=True)

    def cur(off):
        return pl.BlockSpec((None, tc, FF_SHARD), lambda j, b, c: (j + off, b * nc + c, 0))

    def prv(off):
        return pl.BlockSpec((None, CONV_HALO, FF_SHARD), lambda j, b, c: (j + off, jnp.maximum((b * nc + c) * hb - 1, 0), 0))

    def par(rows, off):
        return pl.BlockSpec((None, rows, FF_SHARD), lambda j, b, c: (j + off, 0, 0))

    t = bsz * seq
    hs = jax.ShapeDtypeStruct((2, half, t, FF_SHARD), BF16)
    ws = jax.ShapeDtypeStruct((half, 3, FF_SHARD), F32)
    bs = jax.ShapeDtypeStruct((half, 1, FF_SHARD), F32)
    dc, dwv, dwg, dbv, dbg = pl.pallas_call(
        body, name="conv_bwd_taps", grid=(half, bsz, nc),
        in_specs=[cur(0), cur(half), prv(0), prv(half), par(3, 0), par(3, half), par(1, 0), par(1, half), cur(0)],
        out_specs=[pl.BlockSpec((2, None, tc, FF_SHARD), lambda j, b, c: (0, j, b * nc + c, 0)),
                   par(3, 0), par(3, 0), par(1, 0), par(1, 0)],
        out_shape=[hs, ws, ws, bs, bs],
        compiler_params=_params(("parallel", "arbitrary", "arbitrary")),
    )(up, up, up, up, cw, cw, cb, cb, dact)
    return (dc.reshape(N_DEV, t, FF_SHARD), jnp.concatenate([dwv, dwg], axis=0),
            jnp.concatenate([dbv, dbg], axis=0))


def conv_bwd_input(dconv, cw, bsz, seq):
    tc = _tile(seq, 512)
    nc = seq // tc
    hb = tc // CONV_HALO
    nblk = bsz * seq // CONV_HALO

    def body(d_ref, n_ref, w_ref, o_ref):
        c = pl.program_id(2)
        nxt = jnp.where(c < nc - 1, n_ref[...].astype(F32), 0.0)
        d = d_ref[...].astype(F32)
        w = w_ref[...]
        o_ref[...] = (w[2:3] * d + w[1:2] * _shift_up(d, nxt, 1) + w[0:1] * _shift_up(d, nxt, 2)).astype(o_ref.dtype)

    cur = pl.BlockSpec((None, tc, FF_SHARD), lambda j, b, c: (j, b * nc + c, 0))
    return pl.pallas_call(
        body, name="conv_bwd_input", grid=(N_DEV, bsz, nc),
        in_specs=[cur, pl.BlockSpec((None, CONV_HALO, FF_SHARD),
                                    lambda j, b, c: (j, jnp.minimum((b * nc + c + 1) * hb, nblk - 1), 0)),
                  pl.BlockSpec((None, 3, FF_SHARD), lambda j, b, c: (j, 0, 0))],
        out_specs=cur, out_shape=jax.ShapeDtypeStruct(dconv.shape, BF16),
        compiler_params=_params(("parallel", "parallel", "parallel")),
    )(dconv, dconv, cw)


def _my_index():
    return 4 * lax.axis_index("x") + 2 * lax.axis_index("y") + lax.axis_index("c")


def _peer(k):
    return (lax.axis_index("x") ^ ((k >> 2) & 1), lax.axis_index("y") ^ ((k >> 1) & 1),
            lax.axis_index("c") ^ (k & 1))


def all_gather(name, a, out_dtype):
    def body(a_ref, o_ref, stage, send_sems, recv_sems, local_sem):
        me = _my_index()
        stage[...] = a_ref[...].astype(out_dtype)
        local = pltpu.make_async_copy(stage, o_ref.at[me], local_sem)
        local.start()
        sends = []
        for k in range(1, N_DEV):
            cp = pltpu.make_async_remote_copy(
                src_ref=stage, dst_ref=o_ref.at[me], send_sem=send_sems.at[k - 1], recv_sem=recv_sems.at[k - 1],
                device_id=_peer(k), device_id_type=pl.DeviceIdType.MESH)
            cp.start()
            sends.append(cp)
        for k in range(1, N_DEV):
            pltpu.make_async_remote_copy(
                src_ref=stage, dst_ref=o_ref.at[me ^ k], send_sem=send_sems.at[k - 1], recv_sem=recv_sems.at[k - 1],
                device_id=_peer(k), device_id_type=pl.DeviceIdType.MESH).wait_recv()
        for cp in sends:
            cp.wait_send()
        local.wait()

    return pl.pallas_call(
        body, name=name, in_specs=[pl.BlockSpec(memory_space=pltpu.VMEM)],
        out_specs=pl.BlockSpec(memory_space=pltpu.HBM),
        out_shape=jax.ShapeDtypeStruct((N_DEV,) + a.shape, out_dtype),
        scratch_shapes=[pltpu.VMEM(a.shape, out_dtype), pltpu.SemaphoreType.DMA((N_DEV - 1,)),
                        pltpu.SemaphoreType.DMA((N_DEV - 1,)), pltpu.SemaphoreType.DMA],
        compiler_params=pltpu.CompilerParams(vmem_limit_bytes=VMEM_LIMIT),
    )(a)


def exchange(name, g):
    def body(g_ref, r_ref, send_sems, recv_sems, local_sem):
        me = _my_index()
        local = pltpu.make_async_copy(g_ref.at[me], r_ref.at[me], local_sem)
        local.start()
        sends = []
        for k in range(1, N_DEV):
            cp = pltpu.make_async_remote_copy(
                src_ref=g_ref.at[me ^ k], dst_ref=r_ref.at[me], send_sem=send_sems.at[k - 1],
                recv_sem=recv_sems.at[k - 1], device_id=_peer(k), device_id_type=pl.DeviceIdType.MESH)
            cp.start()
            sends.append(cp)
        for k in range(1, N_DEV):
            pltpu.make_async_remote_copy(
                src_ref=g_ref.at[me], dst_ref=r_ref.at[me ^ k], send_sem=send_sems.at[k - 1],
                recv_sem=recv_sems.at[k - 1], device_id=_peer(k), device_id_type=pl.DeviceIdType.MESH).wait_recv()
        for cp in sends:
            cp.wait_send()
        local.wait()

    return pl.pallas_call(
        body, name=name, in_specs=[pl.BlockSpec(memory_space=pltpu.HBM)],
        out_specs=pl.BlockSpec(memory_space=pltpu.HBM),
        out_shape=jax.ShapeDtypeStruct(g.shape, g.dtype),
        scratch_shapes=[pltpu.SemaphoreType.DMA((N_DEV - 1,)), pltpu.SemaphoreType.DMA((N_DEV - 1,)),
                        pltpu.SemaphoreType.DMA],
    )(g)


_HBM = pl.BlockSpec(memory_space=pltpu.HBM)
_SEM = pl.BlockSpec(memory_space=pltpu.SEMAPHORE)
_DATAFLOW = pltpu.SideEffectType.DATAFLOW_SIDE_EFFECTING


def _split_copies(gather, src_ref, land_ref, send_sems, recv_sems, local_sem):
    me = _my_index()

    def part(j):
        return src_ref if gather else src_ref.at[j]

    local = pltpu.make_async_copy(part(me), land_ref.at[me], local_sem)
    sends = [pltpu.make_async_remote_copy(
        src_ref=part(me ^ k), dst_ref=land_ref.at[me], send_sem=send_sems.at[k - 1], recv_sem=recv_sems.at[k - 1],
        device_id=_peer(k), device_id_type=pl.DeviceIdType.MESH) for k in range(1, N_DEV)]
    recvs = [pltpu.make_async_remote_copy(
        src_ref=part(me ^ k), dst_ref=land_ref.at[me ^ k], send_sem=send_sems.at[k - 1], recv_sem=recv_sems.at[k - 1],
        device_id=_peer(k), device_id_type=pl.DeviceIdType.MESH) for k in range(1, N_DEV)]
    return local, sends, recvs


def split_start(name, src, gather):
    land_shape = ((N_DEV,) + src.shape) if gather else src.shape

    def body(src_ref, land_ref, send_sems, recv_sems, local_sem, src_thru, land_thru, token):
        local, sends, _ = _split_copies(gather, src_ref, land_ref, send_sems, recv_sems, local_sem)
        local.start()
        for cp in sends:
            cp.start()
        token[...] = jnp.zeros_like(token)

    dma7 = pltpu.SemaphoreType.DMA((N_DEV - 1,))
    out = pl.pallas_call(
        body, name=name,
        out_shape=(dma7, dma7, pltpu.SemaphoreType.DMA(()), pltpu.HBM(src.shape, src.dtype),
                   pltpu.HBM(land_shape, src.dtype), jax.ShapeDtypeStruct((8, 128), F32)),
        in_specs=(_HBM, _HBM), out_specs=(_SEM, _SEM, _SEM, _HBM, _HBM, pl.BlockSpec(memory_space=pltpu.VMEM)),
        input_output_aliases={0: 3, 1: 4},
        compiler_params=pltpu.CompilerParams(has_side_effects=_DATAFLOW),
    )(pltpu.with_memory_space_constraint(src, pltpu.HBM),
      pltpu.with_memory_space_constraint(lax.empty(land_shape, src.dtype), pltpu.HBM))
    return out[:5], out[5][0, 0]


def split_wait(name, handles, after, gather):
    send_sems, recv_sems, local_sem, src_thru, land_thru = handles

    def body(src_ref, land_ref, send_sems, recv_sems, local_sem, after_ref, src_dead, got_ref, token):
        local, sends, recvs = _split_copies(gather, src_ref, land_ref, send_sems, recv_sems, local_sem)
        local.wait()
        for cp in recvs:
            cp.wait_send()
            cp.wait_recv()
        token[...] = jnp.zeros_like(token)

    out = pl.pallas_call(
        body, name=name,
        out_shape=(pltpu.HBM(src_thru.shape, src_thru.dtype), pltpu.HBM(land_thru.shape, land_thru.dtype),
                   jax.ShapeDtypeStruct((8, 128), F32)),
        in_specs=(_HBM, _HBM, _SEM, _SEM, _SEM, pl.BlockSpec(memory_space=pl.ANY)),
        out_specs=(_HBM, _HBM, pl.BlockSpec(memory_space=pltpu.VMEM)),
        input_output_aliases={0: 0, 1: 1},
        compiler_params=pltpu.CompilerParams(has_side_effects=_DATAFLOW),
    )(src_thru, land_thru, send_sems, recv_sems, local_sem, after)
    return out[1], out[2][0, 0]


def sum_parts(name, r):
    _, rows, cols = r.shape

    def body(r_ref, o_ref):
        acc = r_ref[0].astype(F32)
        for s in range(1, N_DEV):
            acc = acc + r_ref[s].astype(F32)
        o_ref[...] = acc

    return pl.pallas_call(body, name=name, out_shape=jax.ShapeDtypeStruct((rows, cols), F32),
                          compiler_params=_params())(r)


def adamw(name, w, m, v, parts=None, g=None, layer=0, into=None):
    _, rows, cols = w.shape
    br = _tile(rows, 256, 16)
    c1 = 1.0 / (1.0 - ADAM_B1 ** ADAM_STEP)
    c2 = 1.0 / (1.0 - ADAM_B2 ** ADAM_STEP)

    def body(g_ref, w_ref, m_ref, v_ref, *rest):
        og_ref, od_ref, om_ref, ov_ref = rest[-4:]
        if parts is None:
            gs = g_ref[...]
        else:
            gs = g_ref[0].astype(F32)
            for s in range(1, N_DEV):
                gs = gs + g_ref[s].astype(F32)
        mn = ADAM_B1 * m_ref[...] + (1.0 - ADAM_B1) * gs
        vn = ADAM_B2 * v_ref[...] + (1.0 - ADAM_B2) * (gs * gs)
        og_ref[...] = gs
        om_ref[...] = mn
        ov_ref[...] = vn
        od_ref[...] = -ADAM_LR * ((mn * c1) / (jnp.sqrt(vn * c2) + ADAM_EPS) + ADAM_WD * w_ref[...])

    blk = pl.BlockSpec((None, br, cols), lambda i: (layer, i, 0))
    if parts is None:
        gspec = pl.BlockSpec((br, cols), lambda i: (i, 0))
    else:
        gspec = pl.BlockSpec((N_DEV, br, cols), lambda i: (0, i, 0))
    earlier = [] if into is None else list(into)
    return pl.pallas_call(
        body, name=name, grid=(rows // br,),
        in_specs=[gspec, blk, blk, blk] + [pl.BlockSpec(memory_space=pl.ANY)] * len(earlier),
        out_specs=[blk] * 4, out_shape=[jax.ShapeDtypeStruct(w.shape, F32)] * 4,
        input_output_aliases={4 + k: k for k in range(len(earlier))},
        compiler_params=_params(("parallel",)),
    )(g if parts is None else parts, w, m, v, *earlier)


SMALL = ("norm_mix", "norm_xattn", "norm_ffn", "norm_mem", "norm_final", "pool_w", "pool_scale",
         "ssm_lam_re", "ssm_lam_im", "ssm_log_dt", "ssm_b_re", "ssm_b_im", "ssm_c_re", "ssm_c_im",
         "ffn_conv_b", "ssm_d", "ffn_conv_w")
SMALL_SHARDED = {"ssm_d": 1, "ffn_conv_w": 2}
BIG = ("ab_w_in", "ab_w_out", "ssm_w_in", "ssm_w_glu", "xa_w_q", "xa_w_kv", "xa_w_o", "ffn_w_up", "ffn_w_down")
WEIGHTS = ("norm_mix", "norm_xattn", "norm_ffn", "norm_mem", "norm_final", "ab_w_in", "pool_w", "pool_scale",
           "ab_w_out", "ssm_w_in", "ssm_lam_re", "ssm_lam_im", "ssm_log_dt", "ssm_b_re", "ssm_b_im", "ssm_c_re",
           "ssm_c_im", "ssm_d", "ssm_w_glu", "xa_w_q", "xa_w_kv", "xa_w_o", "ffn_w_up", "ffn_conv_w", "ffn_conv_b",
           "ffn_w_down")


def _rows8(g):
    return g.reshape(N_DEV, g.size // (N_DEV * D_MODEL), D_MODEL)


def _square(a):
    return a.reshape(D_MODEL, D_MODEL)


_LAYOUT = {"ab_w_out": _square, "ssm_w_in": _square, "xa_w_q": _square, "xa_w_o": _square,
           "ffn_w_down": lambda a: a.reshape(N_DEV // 2, FF_SHARD, D_MODEL)}
GATHER_ORDER = (("ab_w_in", 0), ("ab_w_out", 0), ("xa_w_q", 0), ("xa_w_kv", 0), ("xa_w_o", 0), ("ffn_w_up", 0),
                ("ffn_w_down", 0), ("ssm_w_in", 0), ("ssm_w_glu", 0), ("xa_w_q", 1), ("xa_w_kv", 1),
                ("xa_w_o", 1), ("ffn_w_up", 1), ("ffn_w_down", 1))
GATHER_AHEAD = 5


class _Step:
    def __init__(self, master, small):
        self.master, self.small = master, small
        self.pending, self.gathers, self.weights, self.sent = [], {}, {}, []

    def follow(self, v):
        for z in self.pending:
            v = v + z
        self.pending = []
        return v

    def start_gathers(self, upto, zero):
        for n, l in GATHER_ORDER[len(self.gathers):upto]:
            shard = (self.master[n][l] + zero).astype(MXU_DTYPE)
            self.gathers[(n, l)], z = split_start(f"ags_{n}{l}", shard, gather=True)
            self.pending.append(z)

    def weight(self, n, l, after):
        if (n, l) not in self.weights:
            full, z = split_wait(f"agw_{n}{l}", self.gathers[(n, l)], after, gather=True)
            self.weights[(n, l)] = _LAYOUT.get(n, lambda a: a)(full)
            self.start_gathers(GATHER_ORDER.index((n, l)) + 1 + GATHER_AHEAD, z)
        return self.weights[(n, l)]

    def send_grad(self, n, l, part):
        h, z = split_start(f"xs_{n}{l}", part, gather=False)
        self.pending.append(z)
        self.sent.append((n, l, h))


def _layer_tail(st, l, x_in, mem_n, acts):
    bsz, seq = acts["bsz"], acts["seq"]
    p = st.small
    w_q, w_kv = st.weight("xa_w_q", l, x_in), st.weight("xa_w_kv", l, x_in)
    hq = rms_fwd(f"rms_xattn{l}", x_in, st.follow(p["norm_xattn"][l]))
    q = mm_nn(f"xa_q{l}", hq, w_q)
    kv = mm_nn_bs(f"xa_kv{l}", mem_n, w_kv)
    o = xattn_fwd(q, kv, bsz, seq)
    x_mid = mm_nn(f"xa_o{l}", o, st.weight("xa_w_o", l, o), res=x_in, out_dtype=F32)
    w_up = st.weight("ffn_w_up", l, x_mid)
    hf = rms_fwd(f"rms_ffn{l}", x_mid, st.follow(p["norm_ffn"][l]))
    up = mm_nn_bs(f"ffn_up{l}", hf, w_up, stacked_out=True)
    act = conv_fwd(up, p["ffn_conv_w"][l], p["ffn_conv_b"][l], bsz, seq)
    x_out = mm_as_nn(f"ffn_down{l}", act, st.weight("ffn_w_down", l, act), res=x_mid)
    acts[l].update(x_in=x_in, hq=hq, q=q, kv=kv, o=o, x_mid=x_mid, hf=hf, up=up, act=act)
    return x_out


def _layer_tail_bwd(st, l, dx, mem_n, acts, grads):
    a = acts[l]
    bsz, seq = acts["bsz"], acts["seq"]
    p = st.small
    dact = mm_nt_os(f"d_act{l}", dx, st.weight("ffn_w_down", l, dx))
    st.send_grad("ffn_w_down", l, _rows8(mm_tn(f"g_ffn_down{l}", a["act"], dx, a_stacked=True)))
    dconv, dcw, dcb = conv_bwd_taps(a["up"], p["ffn_conv_w"][l], p["ffn_conv_b"][l], dact, bsz, seq)
    grads["ffn_conv_w"][l] = dcw
    grads["ffn_conv_b"][l] = dcb
    dup = conv_bwd_input(dconv, p["ffn_conv_w"][l], bsz, seq)
    dhf = mm_nt_bs(f"d_hf{l}", dup, st.weight("ffn_w_up", l, dx), dc_stacked=True)
    st.send_grad("ffn_w_up", l, mm_tn(f"g_ffn_up{l}", a["hf"], dup, dc_stacked=True))
    dx_mid, grads["norm_ffn"][l] = rms_bwd(f"rms_ffn_bwd{l}", a["x_mid"], st.follow(p["norm_ffn"][l]), dhf, dres=dx)
    do = mm_nt(f"d_o{l}", dx_mid, st.weight("xa_w_o", l, dx))
    st.send_grad("xa_w_o", l, _rows8(mm_tn(f"g_xa_o{l}", a["o"], dx_mid)))
    dq, dk, dv = xattn_bwd(a["q"], a["kv"], do, bsz, seq)
    dkv = jnp.concatenate([dk, dv], axis=1).astype(BF16)
    dhq = mm_nt(f"d_hq{l}", dq, st.weight("xa_w_q", l, dx))
    st.send_grad("xa_w_q", l, _rows8(mm_tn(f"g_xa_q{l}", a["hq"], dq)))
    dmem_n = mm_nt_bs(f"d_memn{l}", dkv, st.weight("xa_w_kv", l, dx), out_dtype=F32)
    st.send_grad("xa_w_kv", l, mm_tn(f"g_xa_kv{l}", mem_n, dkv, dc_cols=2 * D_MODEL // N_DEV))
    dx_in, grads["norm_xattn"][l] = rms_bwd(f"rms_xattn_bwd{l}", a["x_in"], st.follow(p["norm_xattn"][l]), dhq,
                                            dres=dx_mid)
    return dx_in, dmem_n


def kernel(x, mem, norm_mix, norm_xattn, norm_ffn, norm_mem, norm_final, ab_w_in, pool_w, pool_scale, ab_w_out, ssm_w_in, ssm_lam_re, ssm_lam_im, ssm_log_dt, ssm_b_re, ssm_b_im, ssm_c_re, ssm_c_im, ssm_d, ssm_w_glu, xa_w_q, xa_w_kv, xa_w_o, ffn_w_up, ffn_conv_w, ffn_conv_b, ffn_w_down, loss_target, m_norm_mix, m_norm_xattn, m_norm_ffn, m_norm_mem, m_norm_final, m_ab_w_in, m_pool_w, m_pool_scale, m_ab_w_out, m_ssm_w_in, m_ssm_lam_re, m_ssm_lam_im, m_ssm_log_dt, m_ssm_b_re, m_ssm_b_im, m_ssm_c_re, m_ssm_c_im, m_ssm_d, m_ssm_w_glu, m_xa_w_q, m_xa_w_kv, m_xa_w_o, m_ffn_w_up, m_ffn_conv_w, m_ffn_conv_b, m_ffn_w_down, v_norm_mix, v_norm_xattn, v_norm_ffn, v_norm_mem, v_norm_final, v_ab_w_in, v_pool_w, v_pool_scale, v_ab_w_out, v_ssm_w_in, v_ssm_lam_re, v_ssm_lam_im, v_ssm_log_dt, v_ssm_b_re, v_ssm_b_im, v_ssm_c_re, v_ssm_c_im, v_ssm_d, v_ssm_w_glu, v_xa_w_q, v_xa_w_kv, v_xa_w_o, v_ffn_w_up, v_ffn_conv_w, v_ffn_conv_b, v_ffn_w_down):
    given = dict(locals())
    master = {n: given[n] for n in WEIGHTS}
    mom1 = {n: given["m_" + n] for n in WEIGHTS}
    mom2 = {n: given["v_" + n] for n in WEIGHTS}
    bsz, seq, d = x.shape
    t = bsz * seq
    me = _my_index()

    conv_w_st = all_gather("ag_ffn_conv_w", ffn_conv_w, F32)
    dskip = all_gather("ag_ssm_d", ssm_d.reshape(1, 128), F32).reshape(1, D_MODEL)
    st = _Step(master, {
        "norm_xattn": norm_xattn, "norm_ffn": norm_ffn,
        "ffn_conv_w": [conv_w_st[:, l] for l in range(2)],
        "ffn_conv_b": [ffn_conv_b[l].reshape(N_DEV, 1, FF_SHARD) for l in range(2)],
    })
    st.start_gathers(1, 0.0)

    acts = {"bsz": bsz, "seq": seq, 0: {}, 1: {}}
    x0 = x.reshape(t, d)
    mem2 = mem.reshape(bsz * MEM_LEN, d)
    mem_n = rms_fwd("rms_mem", mem2, st.follow(norm_mem))
    pscale = pool_scale.reshape(1, SB_WIDTH)

    w_in = st.weight("ab_w_in", 0, mem_n)
    h0 = rms_fwd("rms_mix0", x0, st.follow(norm_mix[0]))
    proj = mm_nn_bs("ab_in", h0, w_in, out_dtype=F32)
    a_out, rsum = sb_attn_fwd(proj, bsz, seq)
    p_out = pool_fwd(proj, pool_w[0], pscale, bsz, seq)
    w_out = st.weight("ab_w_out", 0, a_out)
    x1 = mm_nn("ab_out_a", a_out, w_out, res=x0, out_dtype=F32)
    x1 = mm_nn("ab_out_p", p_out, w_out, res=x1, koff=SB_WIDTH, out_dtype=F32)
    x3 = _layer_tail(st, 0, x1, mem_n, acts)

    b_re2 = ssm_b_re.reshape(64, 1024)
    b_im2 = ssm_b_im.reshape(64, 1024)
    log_dt = ssm_log_dt.reshape(64, 1)
    lb_re, lb_im, bb_re2, bb_im2 = ssm_prep(ssm_lam_re[0], ssm_lam_im[0], log_dt, b_re2, b_im2)
    wt = _ssm_in_weights(bb_re2, bb_im2)
    ct = _ssm_out_weights(ssm_c_re[0], ssm_c_im[0])
    a_re = lb_re.reshape(SSM_SLAB, 128)
    a_im = lb_im.reshape(SSM_SLAB, 128)
    w_ssm_in = st.weight("ssm_w_in", 0, x3)
    h1 = rms_fwd("rms_mix1", x3, st.follow(norm_mix[1]))
    u = mm_nn("ssm_in", h1, w_ssm_in, out_dtype=F32)
    y, gl, h_re, h_im = ssm_fwd(u, wt, ct, a_re, a_im, dskip, bsz, seq)
    glu = mm_nn_bs("ssm_glu", gl, st.weight("ssm_w_glu", 0, gl), out_dtype=F32)
    x4 = glu_fwd(glu, x3)
    x6 = _layer_tail(st, 1, x4, mem_n, acts)

    loss_row, dx, g_norm_final = loss_head(x6, norm_final, loss_target.reshape(t, d))
    loss = lax.psum(loss_row[0, 0], MESH_AXES)

    grads = {n: [None, None] for n in ("ffn_conv_w", "ffn_conv_b", "norm_ffn", "norm_xattn", "norm_mix")}
    dx4, dmem_1 = _layer_tail_bwd(st, 1, dx, mem_n, acts, grads)
    dglu = glu_bwd(glu, dx4)
    dgl = mm_nt_bs("d_gl", dglu, st.weight("ssm_w_glu", 0, dx))
    st.send_grad("ssm_w_glu", 0, mm_tn("g_ssm_glu", gl, dglu, dc_cols=2 * D_MODEL // N_DEV))
    du, dwt, dct, g_dskip, da_re, da_im = ssm_bwd(dgl, y, u, h_re, h_im, wt, ct, a_re, a_im, dskip, bsz, seq)
    dbb_re, dbb_im = _ssm_in_weights_bwd(dwt)
    g_c_re, g_c_im = _ssm_out_weights_bwd(dct)
    g_lam_re, g_lam_im, g_log_dt, g_b_re, g_b_im = ssm_prep_bwd(
        ssm_lam_re[0], ssm_lam_im[0], log_dt, b_re2, b_im2, da_re.reshape(64, 64), da_im.reshape(64, 64),
        dbb_re, dbb_im)
    dh1 = mm_nt("d_h1", du, st.weight("ssm_w_in", 0, dx))
    st.send_grad("ssm_w_in", 0, _rows8(mm_tn("g_ssm_in", h1, du)))
    dx3, grads["norm_mix"][1] = rms_bwd("rms_mix1_bwd", x3, st.follow(norm_mix[1]), dh1, dres=dx4)

    dx1, dmem_0 = _layer_tail_bwd(st, 0, dx3, mem_n, acts, grads)
    dcat = mm_nt("d_cat", dx1, st.weight("ab_w_out", 0, dx))
    st.send_grad("ab_w_out", 0, _rows8(jnp.concatenate(
        [mm_tn("g_ab_out_a", a_out, dx1), mm_tn("g_ab_out_p", p_out, dx1)], axis=0)))
    dq, dk, dv = sb_attn_bwd(proj, rsum, dcat, bsz, seq)
    dpu, g_pool_w, g_pool_scale = pool_bwd(proj, pool_w[0], st.follow(pscale), dcat, bsz, seq)
    dproj = jnp.concatenate([dq, dk, dv, dpu], axis=1).astype(BF16)
    dh0 = mm_nt_bs("d_h0", dproj, st.weight("ab_w_in", 0, dx))
    st.send_grad("ab_w_in", 0, mm_tn("g_ab_in", h0, dproj, dc_cols=2 * D_MODEL // N_DEV))
    dx0, grads["norm_mix"][0] = rms_bwd("rms_mix0_bwd", x0, st.follow(norm_mix[0]), dh0, dres=dx1)
    _, g_norm_mem = rms_bwd("rms_mem_bwd", mem2, norm_mem, dmem_0 + dmem_1, need_dx=False)

    stepped = {}
    for n, l, handles in st.sent:
        recv, _ = split_wait(f"xw_{n}{l}", handles, dx0, gather=False)
        shape3 = (master[n].shape[0],) + recv.shape[1:]
        stepped[n] = adamw(f"adamw_{n}{l}", master[n].reshape(shape3), mom1[n].reshape(shape3),
                           mom2[n].reshape(shape3), parts=recv, layer=l, into=stepped.get(n))
    out_g, out_d, out_m, out_v = ({n: stepped[n][k].reshape(master[n].shape) for n in BIG} for k in range(4))

    small_g = {
        "norm_mix": jnp.stack([g[0] for g in grads["norm_mix"]]),
        "norm_xattn": jnp.stack([g[0] for g in grads["norm_xattn"]]),
        "norm_ffn": jnp.stack([g[0] for g in grads["norm_ffn"]]),
        "norm_mem": g_norm_mem[0], "norm_final": g_norm_final[0],
        "pool_w": g_pool_w[None], "pool_scale": g_pool_scale,
        "ssm_lam_re": g_lam_re[None], "ssm_lam_im": g_lam_im[None], "ssm_log_dt": g_log_dt.reshape(1, 64),
        "ssm_b_re": g_b_re.reshape(1, 64, 64, 16), "ssm_b_im": g_b_im.reshape(1, 64, 64, 16),
        "ssm_c_re": g_c_re[None], "ssm_c_im": g_c_im[None],
        "ffn_conv_b": jnp.stack([g.reshape(2 * D_FF) for g in grads["ffn_conv_b"]]),
        "ssm_d": g_dskip,
        "ffn_conv_w": jnp.stack([g.transpose(1, 0, 2).reshape(3, 2 * D_FF) for g in grads["ffn_conv_w"]]),
    }
    sizes = [int(small_g[n].size) for n in SMALL]
    total = sum(sizes)
    rows8 = -(-total // (N_DEV * 128 * 8)) * 8
    flat = jnp.concatenate([small_g[n].reshape(-1).astype(F32) for n in SMALL]
                           + [jnp.zeros((N_DEV * rows8 * 128 - total,), F32)])
    recv = exchange("xch_small", flat.reshape(N_DEV, rows8, 128))
    summed = all_gather("ag_small", sum_parts("sum_small", recv), F32).reshape(-1)

    def local_part(name, a):
        ax = SMALL_SHARDED.get(name)
        if ax is None:
            return a
        n_loc = a.shape[ax] // N_DEV
        return lax.dynamic_slice_in_dim(a, me * n_loc, n_loc, axis=ax)

    sg, off = {}, 0
    for n, sz in zip(SMALL, sizes):
        sg[n] = local_part(n, summed[off:off + sz].reshape(small_g[n].shape))
        off += sz
    lsizes = [int(sg[n].size) for n in SMALL]
    ltotal = sum(lsizes)
    lrows = -(-ltotal // (128 * 16)) * 16

    def pack(d_):
        return jnp.concatenate([d_[n].reshape(-1) for n in SMALL] + [jnp.zeros((lrows * 128 - ltotal,), F32)]
                               ).reshape(lrows, 128)

    padv = jnp.concatenate([mom2[n].reshape(-1) for n in SMALL] + [jnp.ones((lrows * 128 - ltotal,), F32)]
                           ).reshape(lrows, 128)
    res = adamw("adamw_small", pack(master)[None], pack(mom1)[None], padv[None], g=pack(sg))
    off = 0
    for n, sz in zip(SMALL, lsizes):
        for dst, r in zip((out_g, out_d, out_m, out_v), res):
            dst[n] = r.reshape(-1)[off:off + sz].reshape(master[n].shape)
        off += sz

    return (loss, dx0.reshape(bsz, seq, d), *[out_g[n] for n in WEIGHTS], *[out_d[n] for n in WEIGHTS],
            *[out_m[n] for n in WEIGHTS], *[out_v[n] for n in WEIGHTS])
```

```python
import functools
import math

import jax
import jax.numpy as jnp
from jax import lax
from jax.experimental import pallas as pl
from jax.experimental.pallas import tpu as pltpu

F32 = jnp.float32
BF16 = jnp.bfloat16
MXU_DTYPE = jnp.bfloat16
N_DEV = 8
MESH_AXES = ("x", "y", "c")

D_MODEL = 1024
SB_HEAD_DIM = 64
SB_WIDTH = 512
SB_BLOCK = 256
POOL_WINDOWS = (2, 4, 8, 16)
POOL_GROUP = 128
POOL_HALO = 16
SSM_TILES = 8
SSM_TILE_STATES = 512
SSM_STATES = 4096
SSM_LANES = 1024
MEM_LEN = 256
XA_HEADS = 4
XA_HEAD_DIM = 256
D_FF = 2816
FF_SHARD = 704
EPS = 1e-6
ADAM_LR = 0.001
ADAM_B1 = 0.9
ADAM_B2 = 0.999
ADAM_EPS = 1e-08
ADAM_WD = 0.01
ADAM_STEP = 10
VMEM_LIMIT = 56 * 1024 * 1024

_NN = (((1,), (0,)), ((), ()))
_NT = (((1,), (1,)), ((), ()))
_TN = (((0,), (0,)), ((), ()))


def _params(sem=None):
    if sem is None:
        return pltpu.CompilerParams(vmem_limit_bytes=VMEM_LIMIT)
    return pltpu.CompilerParams(dimension_semantics=sem, vmem_limit_bytes=VMEM_LIMIT)


def _tile(n, pref, mult=8):
    if n <= pref:
        return n
    for t in range(pref, 0, -1):
        if n % t == 0 and t % mult == 0:
            return t
    return n


def _dot(a, b, dims):
    return lax.dot_general(a.astype(MXU_DTYPE), b.astype(MXU_DTYPE), dims, preferred_element_type=F32)


def _dot_exact01(x, m01, dims=_NN):
    x1 = x.astype(BF16)
    r1 = x - x1.astype(F32)
    x2 = r1.astype(BF16)
    x3 = (r1 - x2.astype(F32)).astype(BF16)
    m = m01.astype(BF16)
    out = lax.dot_general(x1, m, dims, preferred_element_type=F32)
    out = out + lax.dot_general(x2, m, dims, preferred_element_type=F32)
    return out + lax.dot_general(x3, m, dims, preferred_element_type=F32)


def _mm(name, a, b, dims, grid, a_spec, b_spec, o_spec, out_shape, out_dtype, acc_shape, res=None, r_spec=None,
        group=1, n=None, a_sel="full", b_sel="full", o_sel="full"):
    nk = grid[2]
    if out_dtype is None:
        out_dtype = BF16

    def at(sel, s):
        if sel == "lead":
            return (s,)
        if sel == "lanes":
            return (slice(None), slice(s * n, (s + 1) * n))
        return (Ellipsis,)

    def body(*refs):
        a_ref, b_ref = refs[0], refs[1]
        r_ref = refs[2] if res is not None else None
        o_ref = refs[3] if res is not None else refs[2]
        acc = refs[-1] if nk > 1 else None
        k = pl.program_id(2)

        def emit(s, val):
            if nk == 1:
                if r_ref is not None:
                    val = val + r_ref[...].astype(F32)
                o_ref[at(o_sel, s)] = val.astype(out_dtype)
                return

            @pl.when(k == 0)
            def _():
                acc[at(o_sel, s)] = val

            @pl.when(k > 0)
            def _():
                acc[at(o_sel, s)] += val

        total = None
        for s in range(group):
            val = _dot(a_ref[at(a_sel, s)], b_ref[at(b_sel, s)], dims)
            if o_sel == "full":
                total = val if total is None else total + val
            else:
                emit(s, val)
        if o_sel == "full":
            emit(0, total)
        if nk > 1:
            @pl.when(k == nk - 1)
            def _():
                r = acc[...]
                if r_ref is not None:
                    r = r + r_ref[...].astype(F32)
                o_ref[...] = r.astype(out_dtype)

    in_specs = [a_spec, b_spec] + ([] if res is None else [r_spec])
    args = (a, b) + (() if res is None else (res,))
    return pl.pallas_call(
        body, name=name, grid=grid, in_specs=in_specs, out_specs=o_spec,
        out_shape=jax.ShapeDtypeStruct(out_shape, out_dtype),
        scratch_shapes=[pltpu.VMEM(acc_shape, F32)] if nk > 1 else [],
        compiler_params=_params(("parallel", "parallel", "arbitrary")),
    )(*args)


def mm_nn(name, a, b, res=None, koff=0, out_dtype=None):
    m, k = a.shape
    n = b.shape[1]
    tm, tn, tk = _tile(m, 1024), _tile(n, 1024, 128), _tile(k, 1024, 128)
    kb = koff // tk
    spec = pl.BlockSpec((tm, tn), lambda i, j, kk: (i, j))
    return _mm(name, a, b, _NN, (m // tm, n // tn, k // tk),
               pl.BlockSpec((tm, tk), lambda i, j, kk: (i, kk)),
               pl.BlockSpec((tk, tn), lambda i, j, kk: (kk + kb, j)),
               spec, (m, n), out_dtype, (tm, tn), res, spec)


def mm_nn_bs(name, a, bs, stacked_out=False, out_dtype=None):
    m, k = a.shape
    s, _, n = bs.shape
    tm, tk = _tile(m, 1024), _tile(k, 1024, 128)
    a_spec = pl.BlockSpec((tm, tk), lambda i, j, kk: (i, kk))
    if stacked_out:
        return _mm(name, a, bs, _NN, (m // tm, s, k // tk), a_spec,
                   pl.BlockSpec((None, tk, n), lambda i, j, kk: (j, kk, 0)),
                   pl.BlockSpec((None, tm, n), lambda i, j, kk: (j, i, 0)), (s, m, n), out_dtype, (tm, n))
    g = _tile(s, max(1, 1024 // n), 1)
    return _mm(name, a, bs, _NN, (m // tm, s // g, k // tk), a_spec,
               pl.BlockSpec((g, tk, n), lambda i, j, kk: (j, kk, 0)),
               pl.BlockSpec((tm, g * n), lambda i, j, kk: (i, j)), (m, s * n), out_dtype, (tm, g * n),
               group=g, n=n, b_sel="lead", o_sel="lanes")


def mm_as_nn(name, a_st, b3, res, out_dtype=F32):
    s, m, kp = a_st.shape
    n = b3.shape[2]
    tm, tn = _tile(m, 1024), _tile(n, 1024, 128)
    spec = pl.BlockSpec((tm, tn), lambda i, j, kk: (i, j))
    return _mm(name, a_st, b3, _NN, (m // tm, n // tn, s),
               pl.BlockSpec((None, tm, kp), lambda i, j, kk: (kk, i, 0)),
               pl.BlockSpec((None, kp, tn), lambda i, j, kk: (kk, 0, j)),
               spec, (m, n), out_dtype, (tm, tn), res, spec)


def mm_nt(name, dc, b, out_dtype=None):
    m, n = dc.shape
    k = b.shape[0]
    tm, tko, tnr = _tile(m, 1024), _tile(k, 1024, 128), _tile(n, 1024, 128)
    return _mm(name, dc, b, _NT, (m // tm, k // tko, n // tnr),
               pl.BlockSpec((tm, tnr), lambda i, j, kk: (i, kk)),
               pl.BlockSpec((tko, tnr), lambda i, j, kk: (j, kk)),
               pl.BlockSpec((tm, tko), lambda i, j, kk: (i, j)), (m, k), out_dtype, (tm, tko))


def mm_nt_bs(name, dc, bs, dc_stacked=False, out_dtype=None):
    s, k, n = bs.shape
    m = dc.shape[1] if dc_stacked else dc.shape[0]
    tm, tko = _tile(m, 1024), _tile(k, 1024, 128)
    o_spec = pl.BlockSpec((tm, tko), lambda i, j, kk: (i, j))
    if dc_stacked:
        return _mm(name, dc, bs, _NT, (m // tm, k // tko, s),
                   pl.BlockSpec((None, tm, n), lambda i, j, kk: (kk, i, 0)),
                   pl.BlockSpec((None, tko, n), lambda i, j, kk: (kk, j, 0)), o_spec, (m, k), out_dtype, (tm, tko))
    g = _tile(s, max(1, 2048 // n), 1)
    return _mm(name, dc, bs, _NT, (m // tm, k // tko, s // g),
               pl.BlockSpec((tm, g * n), lambda i, j, kk: (i, kk)),
               pl.BlockSpec((g, tko, n), lambda i, j, kk: (kk, j, 0)), o_spec, (m, k), out_dtype, (tm, tko),
               group=g, n=n, a_sel="lanes", b_sel="lead")


def mm_nt_os(name, dc, b3, out_dtype=None):
    m, n = dc.shape
    s, kp, _ = b3.shape
    tm, tnr = _tile(m, 1024), _tile(n, 1024, 128)
    return _mm(name, dc, b3, _NT, (m // tm, s, n // tnr),
               pl.BlockSpec((tm, tnr), lambda i, j, kk: (i, kk)),
               pl.BlockSpec((None, kp, tnr), lambda i, j, kk: (j, 0, kk)),
               pl.BlockSpec((None, tm, kp), lambda i, j, kk: (j, i, 0)), (s, m, kp), out_dtype, (tm, kp))


def mm_tn(name, a, dc, a_stacked=False, dc_cols=None, dc_stacked=False, out_dtype=None):
    if a_stacked:
        s, m, kp = a.shape
        n = dc.shape[1]
        tno, tmr = _tile(n, 1024, 128), _tile(m, 1024)
        return _mm(name, a, dc, _TN, (s, n // tno, m // tmr),
                   pl.BlockSpec((None, tmr, kp), lambda i, j, kk: (i, kk, 0)),
                   pl.BlockSpec((tmr, tno), lambda i, j, kk: (kk, j)),
                   pl.BlockSpec((None, kp, tno), lambda i, j, kk: (i, 0, j)), (s, kp, n), out_dtype, (kp, tno))
    m, k = a.shape
    tko, tmr = _tile(k, 1024, 128), _tile(m, 1024)
    a_spec = pl.BlockSpec((tmr, tko), lambda i, j, kk: (kk, i))
    if dc_stacked:
        s, _, n = dc.shape
        return _mm(name, a, dc, _TN, (k // tko, s, m // tmr), a_spec,
                   pl.BlockSpec((None, tmr, n), lambda i, j, kk: (j, kk, 0)),
                   pl.BlockSpec((None, tko, n), lambda i, j, kk: (j, i, 0)), (s, k, n), out_dtype, (tko, n))
    if dc_cols is not None:
        n = dc_cols
        s = dc.shape[1] // n
        g = _tile(s, max(1, 1024 // n), 1)
        return _mm(name, a, dc, _TN, (k // tko, s // g, m // tmr), a_spec,
                   pl.BlockSpec((tmr, g * n), lambda i, j, kk: (kk, j)),
                   pl.BlockSpec((g, tko, n), lambda i, j, kk: (j, i, 0)), (s, k, n), out_dtype, (g, tko, n),
                   group=g, n=n, b_sel="lanes", o_sel="lead")
    n = dc.shape[1]
    tno = _tile(n, 1024, 128)
    return _mm(name, a, dc, _TN, (k // tko, n // tno, m // tmr), a_spec,
               pl.BlockSpec((tmr, tno), lambda i, j, kk: (kk, j)),
               pl.BlockSpec((tko, tno), lambda i, j, kk: (i, j)), (k, n), out_dtype, (tko, tno))


def rms_fwd(name, x, g):
    t, d = x.shape
    tr = _tile(t, 512)

    def body(x_ref, g_ref, o_ref):
        xf = x_ref[...]
        r = lax.rsqrt(jnp.mean(xf * xf, axis=-1, keepdims=True) + EPS)
        o_ref[...] = (xf * r * g_ref[...]).astype(o_ref.dtype)

    return pl.pallas_call(
        body, name=name, grid=(t // tr,),
        in_specs=[pl.BlockSpec((tr, d), lambda i: (i, 0)), pl.BlockSpec((1, d), lambda i: (0, 0))],
        out_specs=pl.BlockSpec((tr, d), lambda i: (i, 0)),
        out_shape=jax.ShapeDtypeStruct((t, d), BF16), compiler_params=_params(("parallel",)),
    )(x, g.reshape(1, d))


def rms_bwd(name, x, g, dh, dres=None, need_dx=True):
    t, d = x.shape
    tr = _tile(t, 512)

    def body(*refs):
        refs = list(refs)
        x_ref, g_ref, dh_ref = refs[:3]
        r_ref = refs[3] if dres is not None else None
        outs = refs[4:] if dres is not None else refs[3:]
        dx_ref, dg_ref = (outs[0], outs[1]) if need_dx else (None, outs[0])
        i = pl.program_id(0)

        @pl.when(i == 0)
        def _():
            dg_ref[...] = jnp.zeros_like(dg_ref)

        xf = x_ref[...]
        dhf = dh_ref[...].astype(F32)
        r = lax.rsqrt(jnp.mean(xf * xf, axis=-1, keepdims=True) + EPS)
        xh = xf * r
        dg_ref[...] += jnp.sum(dhf * xh, axis=0, keepdims=True)
        if need_dx:
            dxh = dhf * g_ref[...]
            dx = r * (dxh - xh * jnp.mean(dxh * xh, axis=-1, keepdims=True))
            if r_ref is not None:
                dx = dx + r_ref[...]
            dx_ref[...] = dx

    row = pl.BlockSpec((tr, d), lambda i: (i, 0))
    vec = pl.BlockSpec((1, d), lambda i: (0, 0))
    in_specs = [row, vec, row] + ([row] if dres is not None else [])
    args = (x, g.reshape(1, d), dh) + ((dres,) if dres is not None else ())
    out_specs = ([row] if need_dx else []) + [vec]
    out_shape = ([jax.ShapeDtypeStruct((t, d), F32)] if need_dx else []) + [jax.ShapeDtypeStruct((1, d), F32)]
    res = pl.pallas_call(
        body, name=name, grid=(t // tr,), in_specs=in_specs, out_specs=out_specs, out_shape=out_shape,
        compiler_params=_params(("arbitrary",)),
    )(*args)
    return res if need_dx else (None, res[0])


def loss_head(x, g, tgt):
    t, d = x.shape
    tr = _tile(t, 512)

    def body(x_ref, g_ref, t_ref, l_ref, dx_ref, dg_ref):
        i = pl.program_id(0)

        @pl.when(i == 0)
        def _():
            l_ref[...] = jnp.zeros_like(l_ref)
            dg_ref[...] = jnp.zeros_like(dg_ref)

        xf = x_ref[...]
        r = lax.rsqrt(jnp.mean(xf * xf, axis=-1, keepdims=True) + EPS)
        xh = xf * r
        diff = xh * g_ref[...] - t_ref[...]
        l_ref[...] += 0.5 * jnp.sum(jnp.mean(diff * diff, axis=-1, keepdims=True))
        dy = diff * (1.0 / d)
        dg_ref[...] += jnp.sum(dy * xh, axis=0, keepdims=True)
        dxh = dy * g_ref[...]
        dx_ref[...] = r * (dxh - xh * jnp.mean(dxh * xh, axis=-1, keepdims=True))

    row = pl.BlockSpec((tr, d), lambda i: (i, 0))
    vec = pl.BlockSpec((1, d), lambda i: (0, 0))
    return pl.pallas_call(
        body, name="loss_head", grid=(t // tr,), in_specs=[row, vec, row],
        out_specs=[pl.BlockSpec((1, 128), lambda i: (0, 0)), row, vec],
        out_shape=[jax.ShapeDtypeStruct((1, 128), F32), jax.ShapeDtypeStruct((t, d), F32),
                   jax.ShapeDtypeStruct((1, d), F32)],
        compiler_params=_params(("arbitrary",)),
    )(x, g.reshape(1, d), tgt)


def glu_fwd(glu, x):
    t, d = x.shape
    tr = _tile(t, 512)

    def body(v_ref, g_ref, x_ref, o_ref):
        o_ref[...] = x_ref[...] + v_ref[...] * jax.nn.sigmoid(g_ref[...])

    return pl.pallas_call(
        body, name="glu_fwd", grid=(t // tr,),
        in_specs=[pl.BlockSpec((tr, d), lambda i: (i, 0)), pl.BlockSpec((tr, d), lambda i: (i, 1)),
                  pl.BlockSpec((tr, d), lambda i: (i, 0))],
        out_specs=pl.BlockSpec((tr, d), lambda i: (i, 0)),
        out_shape=jax.ShapeDtypeStruct((t, d), F32), compiler_params=_params(("parallel",)),
    )(glu, glu, x)


def glu_bwd(glu, dmix):
    t, d = dmix.shape
    tr = _tile(t, 512)

    def body(v_ref, g_ref, d_ref, o_ref):
        sg = jax.nn.sigmoid(g_ref[...])
        dm = d_ref[...]
        o_ref[:, :d] = (dm * sg).astype(o_ref.dtype)
        o_ref[:, d:] = (dm * v_ref[...] * sg * (1.0 - sg)).astype(o_ref.dtype)

    return pl.pallas_call(
        body, name="glu_bwd", grid=(t // tr,),
        in_specs=[pl.BlockSpec((tr, d), lambda i: (i, 0)), pl.BlockSpec((tr, d), lambda i: (i, 1)),
                  pl.BlockSpec((tr, d), lambda i: (i, 0))],
        out_specs=pl.BlockSpec((tr, 2 * d), lambda i: (i, 0)),
        out_shape=jax.ShapeDtypeStruct((t, 2 * d), BF16), compiler_params=_params(("parallel",)),
    )(glu, glu, dmix)


def _head_masks(shape):
    lane = lax.broadcasted_iota(jnp.int32, shape, 1)
    return lane < SB_HEAD_DIM


def _stack_heads(xf, is_a):
    return jnp.concatenate([jnp.where(is_a, xf, 0.0), jnp.where(is_a, 0.0, xf)], axis=0).astype(MXU_DTYPE)


def _diag_mask(qb, row0, rows):
    row = (lax.broadcasted_iota(jnp.int32, (rows, qb), 0) + row0) & (qb - 1)
    col = lax.broadcasted_iota(jnp.int32, (rows, qb), 1)
    return col < row


def _tri01(qb, pred):
    j = lax.broadcasted_iota(jnp.int32, (qb, qb), 0)
    s = lax.broadcasted_iota(jnp.int32, (qb, qb), 1)
    m = pred(j, s).astype(BF16)
    return jnp.concatenate([m, m], axis=0)


def _split_cat(x):
    hi = x.astype(BF16)
    lo = (x - hi.astype(F32)).astype(BF16)
    return jnp.concatenate([hi, lo], axis=1)


def sb_attn_fwd(proj, bsz, seq):
    qb = SB_BLOCK
    nq = seq // qb
    npair = SB_WIDTH // 128
    scale = SB_HEAD_DIM ** -0.5

    def body(q_ref, k_ref, v_ref, o_ref, r_ref):
        qi = pl.program_id(2)
        is_a = _head_masks((qb, 128))
        q2 = _stack_heads(q_ref[...] * scale, is_a)
        diag = _diag_mask(qb, 0, 2 * qb)
        upper = _tri01(qb, lambda j, s: j > s)

        def block(kbi, acc, run, masked):
            ks = pl.ds(pl.multiple_of(kbi * qb, qb), qb)
            kblk = k_ref[ks, :].astype(MXU_DTYPE)
            vblk = v_ref[ks, :].astype(MXU_DTYPE)
            z = lax.dot_general(q2, kblk, _NT, preferred_element_type=F32)
            lk = -jnp.maximum(z, 0.0) - jnp.log(1.0 + jnp.exp(-jnp.abs(z)))
            lb = lk + z
            if masked:
                lk = jnp.where(diag, lk, 0.0)
            after = run + lax.dot_general(_split_cat(lk), upper, _NN, preferred_element_type=F32)
            w = jnp.exp(lb + after)
            if masked:
                w = jnp.where(diag, w, 0.0)
            acc = acc + lax.dot_general(w.astype(MXU_DTYPE), vblk, _NN, preferred_element_type=F32)
            return acc, run + jnp.sum(lk, axis=1, keepdims=True)

        carry = block(qi, jnp.zeros((2 * qb, 128), F32), jnp.zeros((2 * qb, 1), F32), True)
        acc, run = lax.fori_loop(0, qi, lambda i, c: block(qi - 1 - i, c[0], c[1], False), carry)
        o_ref[...] = jnp.where(is_a, acc[:qb], acc[qb:]).astype(o_ref.dtype)
        r_ref[...] = jnp.where(is_a, run[:qb], run[qb:])

    return pl.pallas_call(
        body, name="sb_attn_fwd", grid=(bsz, npair, nq),
        in_specs=[pl.BlockSpec((qb, 128), lambda b, p, i: (b * nq + i, p)),
                  pl.BlockSpec((seq, 128), lambda b, p, i: (b, npair + p)),
                  pl.BlockSpec((seq, 128), lambda b, p, i: (b, 2 * npair + p))],
        out_specs=[pl.BlockSpec((qb, 128), lambda b, p, i: (b * nq + i, p)),
                   pl.BlockSpec((qb, 128), lambda b, p, i: (b * nq + i, p))],
        out_shape=[jax.ShapeDtypeStruct((bsz * seq, SB_WIDTH), BF16),
                   jax.ShapeDtypeStruct((bsz * seq, SB_WIDTH), F32)],
        compiler_params=_params(("parallel", "parallel", "arbitrary")),
    )(proj, proj, proj)


def sb_attn_bwd(proj, rsum, dcat, bsz, seq):
    qb = SB_BLOCK
    nq = seq // qb
    npair = SB_WIDTH // 128
    scale = SB_HEAD_DIM ** -0.5

    def body(q_ref, k_ref, v_ref, r_ref, do_ref, dq_ref, dk_ref, dv_ref):
        qi = pl.program_id(2)

        @pl.when(qi == 0)
        def _():
            dk_ref[...] = jnp.zeros_like(dk_ref)
            dv_ref[...] = jnp.zeros_like(dv_ref)

        is_a = _head_masks((qb, 128))
        q2 = _stack_heads(q_ref[...] * scale, is_a)
        do2 = _stack_heads(do_ref[...].astype(F32), is_a)
        rf = r_ref[...]
        rtot = jnp.concatenate([rf[:, 0:1], rf[:, SB_HEAD_DIM:SB_HEAD_DIM + 1]], axis=0)
        diag = _diag_mask(qb, 0, 2 * qb)
        incl = _tri01(qb, lambda j, s: j <= s)
        strict = _tri01(qb, lambda j, s: j < s)

        def block(kbi, dq, pre, epre, masked):
            ks = pl.ds(pl.multiple_of(kbi * qb, qb), qb)
            kblk = k_ref[ks, :].astype(MXU_DTYPE)
            vblk = v_ref[ks, :].astype(MXU_DTYPE)
            z = lax.dot_general(q2, kblk, _NT, preferred_element_type=F32)
            lk = -jnp.maximum(z, 0.0) - jnp.log(1.0 + jnp.exp(-jnp.abs(z)))
            lb = lk + z
            if masked:
                lk = jnp.where(diag, lk, 0.0)
            after = rtot - (pre + lax.dot_general(_split_cat(lk), incl, _NN, preferred_element_type=F32))
            w = jnp.exp(lb + after)
            if masked:
                w = jnp.where(diag, w, 0.0)
            e = lax.dot_general(do2, vblk, _NT, preferred_element_type=F32) * w
            ecum = epre + lax.dot_general(_split_cat(e), strict, _NN, preferred_element_type=F32)
            dz = e - jnp.exp(lb) * (e + ecum)
            if masked:
                dz = jnp.where(diag, dz, 0.0)
            dz = dz.astype(MXU_DTYPE)
            dq = dq + lax.dot_general(dz, kblk, _NN, preferred_element_type=F32)
            dk_ref[ks, :] += lax.dot_general(dz, q2, _TN, preferred_element_type=F32)
            dv_ref[ks, :] += lax.dot_general(w.astype(MXU_DTYPE), do2, _TN, preferred_element_type=F32)
            return dq, pre + jnp.sum(lk, axis=1, keepdims=True), epre + jnp.sum(e, axis=1, keepdims=True)

        zc = jnp.zeros((2 * qb, 1), F32)
        carry = lax.fori_loop(0, qi, lambda kbi, c: block(kbi, c[0], c[1], c[2], False),
                              (jnp.zeros((2 * qb, 128), F32), zc, zc))
        dq = block(qi, carry[0], carry[1], carry[2], True)[0]
        dq_ref[...] = jnp.where(is_a, dq[:qb], dq[qb:]) * scale

    full = jax.ShapeDtypeStruct((bsz * seq, SB_WIDTH), F32)
    qspec = pl.BlockSpec((qb, 128), lambda b, p, i: (b * nq + i, p))
    return pl.pallas_call(
        body, name="sb_attn_bwd", grid=(bsz, npair, nq),
        in_specs=[qspec,
                  pl.BlockSpec((seq, 128), lambda b, p, i: (b, npair + p)),
                  pl.BlockSpec((seq, 128), lambda b, p, i: (b, 2 * npair + p)),
                  qspec, qspec],
        out_specs=[qspec, pl.BlockSpec((seq, 128), lambda b, p, i: (b, p)),
                   pl.BlockSpec((seq, 128), lambda b, p, i: (b, p))],
        out_shape=[full, full, full],
        compiler_params=_params(("parallel", "parallel", "arbitrary")),
    )(proj, proj, proj, rsum, dcat)


def _window_sums(x, forward):
    n = x.shape[0]
    out = []
    s = x
    for sh in (1, 2, 4, 8):
        s = s + pltpu.roll(s, (n - sh) if forward else sh, 0)
        out.append(s)
    return out


def _pool_counts(tc, c, w):
    t = lax.broadcasted_iota(jnp.int32, (tc, 1), 0) + c * tc
    return jnp.minimum(t + 1, w).astype(F32)


def pool_fwd(proj, pool_w, pool_scale, bsz, seq):
    tc = _tile(seq, 512)
    nc = seq // tc
    hb = tc // POOL_HALO
    ucol = 3

    def body(u_ref, prev_ref, w_ref, s_ref, o_ref):
        c = pl.program_id(1)
        prev = jnp.where(c > 0, prev_ref[...], 0.0)
        x = jnp.concatenate([prev, u_ref[...]], axis=0)
        sums = _window_sums(x, forward=False)
        for g, win in enumerate(POOL_WINDOWS):
            ls = slice(g * POOL_GROUP, (g + 1) * POOL_GROUP)
            pooled = sums[g][POOL_HALO:, ls] / _pool_counts(tc, c, win) - x[POOL_HALO:, ls]
            y = _dot(pooled, w_ref[g], _NN)
            o_ref[:, ls] = (y * s_ref[:, ls]).astype(o_ref.dtype)

    return pl.pallas_call(
        body, name="pool_fwd", grid=(bsz, nc),
        in_specs=[pl.BlockSpec((tc, SB_WIDTH), lambda b, c: (b * nc + c, ucol)),
                  pl.BlockSpec((POOL_HALO, SB_WIDTH), lambda b, c: (jnp.maximum((b * nc + c) * hb - 1, 0), ucol)),
                  pl.BlockSpec((4, POOL_GROUP, POOL_GROUP), lambda b, c: (0, 0, 0)),
                  pl.BlockSpec((1, SB_WIDTH), lambda b, c: (0, 0))],
        out_specs=pl.BlockSpec((tc, SB_WIDTH), lambda b, c: (b * nc + c, 0)),
        out_shape=jax.ShapeDtypeStruct((bsz * seq, SB_WIDTH), BF16),
        compiler_params=_params(("parallel", "parallel")),
    )(proj, proj, pool_w, pool_scale)


def pool_bwd(proj, pool_w, pool_scale, dcat, bsz, seq):
    tc = _tile(seq, 512)
    nc = seq // tc
    hb = tc // POOL_HALO
    nblk = bsz * seq // POOL_HALO
    ucol = 3

    def body(u_ref, prev_ref, dy_ref, nxt_ref, w_ref, s_ref, du_ref, dw_ref, ds_ref):
        b, c = pl.program_id(0), pl.program_id(1)

        @pl.when((b == 0) & (c == 0))
        def _():
            dw_ref[...] = jnp.zeros_like(dw_ref)
            ds_ref[...] = jnp.zeros_like(ds_ref)

        prev = jnp.where(c > 0, prev_ref[...], 0.0)
        x = jnp.concatenate([prev, u_ref[...]], axis=0)
        sums = _window_sums(x, forward=False)
        nxt = jnp.where(c < nc - 1, nxt_ref[...].astype(F32), 0.0)
        dy = jnp.concatenate([dy_ref[...].astype(F32), nxt], axis=0)
        tq = lax.broadcasted_iota(jnp.int32, (tc + POOL_HALO, 1), 0) + c * tc
        for g, win in enumerate(POOL_WINDOWS):
            ls = slice(g * POOL_GROUP, (g + 1) * POOL_GROUP)
            pooled = sums[g][POOL_HALO:, ls] / _pool_counts(tc, c, win) - x[POOL_HALO:, ls]
            y = _dot(pooled, w_ref[g], _NN)
            ds_ref[:, ls] += jnp.sum(dy[:tc, ls] * y, axis=0, keepdims=True)
            dz = dy[:, ls] * s_ref[:, ls]
            dw_ref[g] += _dot(pooled, dz[:tc], _TN)
            dpool = _dot(dz, w_ref[g], _NT)
            dmean = dpool / jnp.minimum(tq + 1, win).astype(F32)
            fsum = _window_sums(dmean, forward=True)[g]
            du_ref[:, ls] = fsum[:tc] - dpool[:tc]

    return pl.pallas_call(
        body, name="pool_bwd", grid=(bsz, nc),
        in_specs=[pl.BlockSpec((tc, SB_WIDTH), lambda b, c: (b * nc + c, ucol)),
                  pl.BlockSpec((POOL_HALO, SB_WIDTH), lambda b, c: (jnp.maximum((b * nc + c) * hb - 1, 0), ucol)),
                  pl.BlockSpec((tc, SB_WIDTH), lambda b, c: (b * nc + c, 1)),
                  pl.BlockSpec((POOL_HALO, SB_WIDTH), lambda b, c: (jnp.minimum((b * nc + c + 1) * hb, nblk - 1), 1)),
                  pl.BlockSpec((4, POOL_GROUP, POOL_GROUP), lambda b, c: (0, 0, 0)),
                  pl.BlockSpec((1, SB_WIDTH), lambda b, c: (0, 0))],
        out_specs=[pl.BlockSpec((tc, SB_WIDTH), lambda b, c: (b * nc + c, 0)),
                   pl.BlockSpec((4, POOL_GROUP, POOL_GROUP), lambda b, c: (0, 0, 0)),
                   pl.BlockSpec((1, SB_WIDTH), lambda b, c: (0, 0))],
        out_shape=[jax.ShapeDtypeStruct((bsz * seq, SB_WIDTH), F32),
                   jax.ShapeDtypeStruct((4, POOL_GROUP, POOL_GROUP), F32),
                   jax.ShapeDtypeStruct((1, SB_WIDTH), F32)],
        compiler_params=_params(("arbitrary", "arbitrary")),
    )(proj, proj, dcat, dcat, pool_w, pool_scale)


def _lbar(lam_re, lam_im, log_dt):
    dt = jnp.exp(log_dt)
    mag = jnp.exp(lam_re * dt)
    ang = lam_im * dt
    return mag * jnp.cos(ang), mag * jnp.sin(ang)


def _bbar(lam_re, lam_im, log_dt, b_re, b_im):
    lb_re, lb_im = _lbar(lam_re, lam_im, log_dt)
    n_re = lb_re - 1.0
    den = lam_re * lam_re + lam_im * lam_im
    coef_re = (n_re * lam_re + lb_im * lam_im) / den
    coef_im = (lb_im * lam_re - n_re * lam_im) / den
    return coef_re * b_re - coef_im * b_im, coef_re * b_im + coef_im * b_re


def _expand01():
    p = lax.broadcasted_iota(jnp.int32, (64, 1024), 0)
    q = lax.broadcasted_iota(jnp.int32, (64, 1024), 1)
    return (lax.shift_right_logical(q, 4) == p).astype(BF16)


def ssm_prep(lam_re, lam_im, log_dt, b_re2, b_im2):
    def body(lr_ref, li_ref, dt_ref, br_ref, bi_ref, ar_ref, ai_ref, bbr_ref, bbi_ref):
        e = _expand01()
        lr, li, dt = lr_ref[...], li_ref[...], dt_ref[...]
        ar_ref[...], ai_ref[...] = _lbar(lr, li, dt)
        bbr_ref[...], bbi_ref[...] = _bbar(_dot_exact01(lr, e), _dot_exact01(li, e), dt, br_ref[...], bi_ref[...])

    s64 = jax.ShapeDtypeStruct((64, 64), F32)
    s1k = jax.ShapeDtypeStruct((64, 1024), F32)
    return pl.pallas_call(body, name="ssm_prep", out_shape=[s64, s64, s1k, s1k], compiler_params=_params())(
        lam_re, lam_im, log_dt, b_re2, b_im2)


def ssm_prep_bwd(lam_re, lam_im, log_dt, b_re2, b_im2, da_re, da_im, dbb_re, dbb_im):
    def body(lr_ref, li_ref, dt_ref, br_ref, bi_ref, dar_ref, dai_ref, dbr_ref, dbi_ref,
             olr_ref, oli_ref, odt_ref, obr_ref, obi_ref):
        e = _expand01()
        lr, li, dt = lr_ref[...], li_ref[...], dt_ref[...]
        _, vjp_a = jax.vjp(_lbar, lr, li, dt)
        g_lr, g_li, g_dt = vjp_a((dar_ref[...], dai_ref[...]))
        _, vjp_b = jax.vjp(_bbar, _dot_exact01(lr, e), _dot_exact01(li, e), dt, br_ref[...], bi_ref[...])
        x_lr, x_li, x_dt, g_br, g_bi = vjp_b((dbr_ref[...], dbi_ref[...]))
        olr_ref[...] = g_lr + _dot_exact01(x_lr, e, _NT)
        oli_ref[...] = g_li + _dot_exact01(x_li, e, _NT)
        odt_ref[...] = g_dt + x_dt
        obr_ref[...] = g_br
        obi_ref[...] = g_bi

    s64 = jax.ShapeDtypeStruct((64, 64), F32)
    s1k = jax.ShapeDtypeStruct((64, 1024), F32)
    return pl.pallas_call(body, name="ssm_prep_bwd",
                          out_shape=[s64, s64, jax.ShapeDtypeStruct((64, 1), F32), s1k, s1k],
                          compiler_params=_params())(
        lam_re, lam_im, log_dt, b_re2, b_im2, da_re, da_im, dbb_re, dbb_im)


def _gelu(y):
    c = math.sqrt(2.0 / math.pi)
    return 0.5 * y * (1.0 + jnp.tanh(c * (y + 0.044715 * y * y * y)))


def _gelu_grad(y):
    c = math.sqrt(2.0 / math.pi)
    th = jnp.tanh(c * (y + 0.044715 * y * y * y))
    return 0.5 * (1.0 + th) + 0.5 * y * (1.0 - th * th) * c * (1.0 + 3.0 * 0.044715 * y * y)


def _cmul(ar, ai, br, bi):
    return ar * br - ai * bi, ar * bi + ai * br


def _scan_tables(ar, ai, reverse, tabs):
    row = lax.broadcasted_iota(jnp.int32, (8, SSM_STATES), 0)
    a1 = (ar, ai)
    a2 = _cmul(*a1, *a1)
    a4 = _cmul(*a2, *a2)
    powers = [a1, a2, _cmul(*a2, *a1), a4]
    powers += [_cmul(*a4, *p) for p in powers]
    for k, (val, sh) in enumerate(((a1, 1), (a2, 2), (a4, 4))):
        keep = (row < 8 - sh) if reverse else (row >= sh)
        tabs[2 * k][...] = jnp.where(keep, val[0], 0.0)
        tabs[2 * k + 1][...] = jnp.where(keep, val[1], 0.0)
    pr = jnp.zeros((8, SSM_STATES), F32)
    pi = jnp.zeros((8, SSM_STATES), F32)
    for r in range(8):
        val = powers[7 - r] if reverse else powers[r]
        pr = jnp.where(row == r, val[0], pr)
        pi = jnp.where(row == r, val[1], pi)
    tabs[6][...] = pr
    tabs[7][...] = pi


def _scan8(xr, xi, tabs, ls, cr, ci, reverse):
    for k, sh in enumerate((1, 2, 4)):
        amt = (8 - sh) if reverse else sh
        sr, si = pltpu.roll(xr, amt, 0), pltpu.roll(xi, amt, 0)
        lr, li = tabs[2 * k][:, ls], tabs[2 * k + 1][:, ls]
        xr, xi = xr + lr * sr - li * si, xi + lr * si + li * sr
    pr, pi = tabs[6][:, ls], tabs[7][:, ls]
    return xr + pr * cr - pi * ci, xi + pr * ci + pi * cr


def _block8(b):
    return pl.ds(pl.multiple_of(b * 8, 8), 8)


def ssm_fwd(u, wt, ct, a_re, a_im, dskip, bsz, seq):
    tc = _tile(seq, 256)
    nc = seq // tc
    ns = SSM_TILE_STATES
    nl = SSM_STATES // SSM_LANES

    def body(u_ref, wt_ref, ct_ref, ar_ref, ai_ref, d_ref, y_ref, gl_ref, hr_ref, hi_ref, sr_ref, si_ref, *tabs):
        b, c = pl.program_id(0), pl.program_id(1)

        @pl.when((b == 0) & (c == 0))
        def _():
            _scan_tables(ar_ref[...], ai_ref[...], False, tabs)

        @pl.when(c == 0)
        def _():
            sr_ref[...] = jnp.zeros_like(sr_ref)
            si_ref[...] = jnp.zeros_like(si_ref)

        uf = u_ref[...]
        for i in range(SSM_TILES):
            bu = _dot(uf[:, i * 128:(i + 1) * 128], wt_ref[i], _NN)
            hr_ref[:, i * ns:(i + 1) * ns] = bu[:, :ns]
            hi_ref[:, i * ns:(i + 1) * ns] = bu[:, ns:]

        def step(blk, carry):
            rows = _block8(blk)
            new = []
            for j in range(nl):
                ls = slice(j * SSM_LANES, (j + 1) * SSM_LANES)
                xr, xi = _scan8(hr_ref[rows, ls], hi_ref[rows, ls], tabs, ls, carry[2 * j], carry[2 * j + 1], False)
                hr_ref[rows, ls] = xr
                hi_ref[rows, ls] = xi
                new += [xr[7:8], xi[7:8]]
            return tuple(new)

        init = []
        for j in range(nl):
            ls = slice(j * SSM_LANES, (j + 1) * SSM_LANES)
            init += [sr_ref[:, ls], si_ref[:, ls]]
        last = lax.fori_loop(0, tc // 8, step, tuple(init), unroll=2)
        for j in range(nl):
            ls = slice(j * SSM_LANES, (j + 1) * SSM_LANES)
            sr_ref[:, ls] = last[2 * j]
            si_ref[:, ls] = last[2 * j + 1]
        for i in range(SSM_TILES):
            hcat = jnp.concatenate([hr_ref[:, i * ns:(i + 1) * ns], hi_ref[:, i * ns:(i + 1) * ns]], axis=1)
            ls = slice(i * 128, (i + 1) * 128)
            y = _dot(hcat, ct_ref[i], _NN) + d_ref[:, ls] * uf[:, ls]
            y_ref[:, ls] = y
            gl_ref[:, ls] = _gelu(y).astype(gl_ref.dtype)

    t = bsz * seq
    row = pl.BlockSpec((tc, D_MODEL), lambda b, c: (b * nc + c, 0))
    st = pl.BlockSpec((tc, SSM_STATES), lambda b, c: (b * nc + c, 0))
    diag = pl.BlockSpec((1, SSM_STATES), lambda b, c: (0, 0))
    return pl.pallas_call(
        body, name="ssm_fwd", grid=(bsz, nc),
        in_specs=[row, pl.BlockSpec((SSM_TILES, 128, 2 * ns), lambda b, c: (0, 0, 0)),
                  pl.BlockSpec((SSM_TILES, 2 * ns, 128), lambda b, c: (0, 0, 0)), diag, diag,
                  pl.BlockSpec((1, D_MODEL), lambda b, c: (0, 0))],
        out_specs=[row, row, st, st],
        out_shape=[jax.ShapeDtypeStruct((t, D_MODEL), F32), jax.ShapeDtypeStruct((t, D_MODEL), BF16),
                   jax.ShapeDtypeStruct((t, SSM_STATES), F32), jax.ShapeDtypeStruct((t, SSM_STATES), F32)],
        scratch_shapes=[pltpu.VMEM((1, SSM_STATES), F32)] * 2 + [pltpu.VMEM((8, SSM_STATES), F32)] * 8,
        compiler_params=_params(("arbitrary", "arbitrary")),
    )(u, wt, ct, a_re, a_im, dskip)


def ssm_bwd(dgl, y, u, h_re, h_im, wt, ct, a_re, a_im, dskip, bsz, seq):
    tc = _tile(seq, 256)
    nc = seq // tc
    nb = tc // 8
    ns = SSM_TILE_STATES
    nl = SSM_STATES // SSM_LANES

    def body(dgl_ref, y_ref, u_ref, hr_ref, hi_ref, pr_ref, pi_ref, wt_ref, ct_ref, ar_ref, ai_ref, d_ref,
             du_ref, dwt_ref, dct_ref, dd_ref, dar_ref, dai_ref, gr_ref, gi_ref, sr_ref, si_ref, ar8_ref, ai8_ref,
             *tabs):
        b, c = pl.program_id(0), pl.program_id(1)

        @pl.when((b == 0) & (c == 0))
        def _():
            for r in (dwt_ref, dct_ref, dd_ref, ar8_ref, ai8_ref):
                r[...] = jnp.zeros_like(r)
            _scan_tables(ar_ref[...], -ai_ref[...], True, tabs)

        @pl.when(c == 0)
        def _():
            sr_ref[...] = jnp.zeros_like(sr_ref)
            si_ref[...] = jnp.zeros_like(si_ref)

        uf = u_ref[...]
        dy = dgl_ref[...].astype(F32) * _gelu_grad(y_ref[...])
        dd_ref[...] += jnp.sum(dy * uf, axis=0, keepdims=True)
        for i in range(SSM_TILES):
            dyi = dy[:, i * 128:(i + 1) * 128]
            dh = _dot(dyi, ct_ref[i], _NT)
            gr_ref[:, i * ns:(i + 1) * ns] = dh[:, :ns]
            gi_ref[:, i * ns:(i + 1) * ns] = dh[:, ns:]
            hcat = jnp.concatenate([hr_ref[:, i * ns:(i + 1) * ns], hi_ref[:, i * ns:(i + 1) * ns]], axis=1)
            dct_ref[i] += _dot(hcat, dyi, _TN)
        row0 = lax.broadcasted_iota(jnp.int32, (8, SSM_LANES), 0) == 0

        def block(blk, carry, before):
            rows = _block8(blk)
            new = []
            for j in range(nl):
                ls = slice(j * SSM_LANES, (j + 1) * SSM_LANES)
                gr, gi = _scan8(gr_ref[rows, ls], gi_ref[rows, ls], tabs, ls, carry[2 * j], carry[2 * j + 1], True)
                gr_ref[rows, ls] = gr
                gi_ref[rows, ls] = gi
                bpr, bpi = before(j)
                hpr = jnp.where(row0, bpr, pltpu.roll(hr_ref[rows, ls], 1, 0))
                hpi = jnp.where(row0, bpi, pltpu.roll(hi_ref[rows, ls], 1, 0))
                ar8_ref[:, ls] += gr * hpr + gi * hpi
                ai8_ref[:, ls] += gi * hpr - gr * hpi
                new += [gr[0:1], gi[0:1]]
            return tuple(new)

        def step(jj, carry):
            blk = nb - 1 - jj
            prev_rows = _block8(blk - 1)

            def before(j):
                ls = slice(j * SSM_LANES, (j + 1) * SSM_LANES)
                return hr_ref[prev_rows, ls][7:8], hi_ref[prev_rows, ls][7:8]

            return block(blk, carry, before)

        init = []
        for j in range(nl):
            ls = slice(j * SSM_LANES, (j + 1) * SSM_LANES)
            init += [sr_ref[:, ls], si_ref[:, ls]]
        carry = lax.fori_loop(0, nb - 1, step, tuple(init))
        first = c == nc - 1

        def before_chunk(j):
            ls = slice(j * SSM_LANES, (j + 1) * SSM_LANES)
            return (jnp.where(first, 0.0, pr_ref[:, ls][7:8]), jnp.where(first, 0.0, pi_ref[:, ls][7:8]))

        last = block(0, carry, before_chunk)
        for j in range(nl):
            ls = slice(j * SSM_LANES, (j + 1) * SSM_LANES)
            sr_ref[:, ls] = last[2 * j]
            si_ref[:, ls] = last[2 * j + 1]
        for i in range(SSM_TILES):
            ls = slice(i * 128, (i + 1) * 128)
            gcat = jnp.concatenate([gr_ref[:, i * ns:(i + 1) * ns], gi_ref[:, i * ns:(i + 1) * ns]], axis=1)
            du_ref[:, ls] = (_dot(gcat, wt_ref[i], _NT) + d_ref[:, ls] * dy[:, ls]).astype(du_ref.dtype)
            dwt_ref[i] += _dot(uf[:, ls], gcat, _TN)

        @pl.when((b == bsz - 1) & (c == nc - 1))
        def _():
            dar_ref[...] = jnp.sum(ar8_ref[...], axis=0, keepdims=True)
            dai_ref[...] = jnp.sum(ai8_ref[...], axis=0, keepdims=True)

    t = bsz * seq
    rev = lambda b, c: (b * nc + (nc - 1 - c), 0)
    row = pl.BlockSpec((tc, D_MODEL), rev)
    st = pl.BlockSpec((tc, SSM_STATES), rev)
    prev = pl.BlockSpec((8, SSM_STATES), lambda b, c: (jnp.maximum((b * nc + (nc - 1 - c)) * nb - 1, 0), 0))
    diag = pl.BlockSpec((1, SSM_STATES), lambda b, c: (0, 0))
    wts = pl.BlockSpec((SSM_TILES, 128, 2 * ns), lambda b, c: (0, 0, 0))
    cts = pl.BlockSpec((SSM_TILES, 2 * ns, 128), lambda b, c: (0, 0, 0))
    vec = pl.BlockSpec((1, D_MODEL), lambda b, c: (0, 0))
    return pl.pallas_call(
        body, name="ssm_bwd", grid=(bsz, nc),
        in_specs=[row, row, row, st, st, prev, prev, wts, cts, diag, diag, vec],
        out_specs=[row, wts, cts, vec, diag, diag],
        out_shape=[jax.ShapeDtypeStruct((t, D_MODEL), BF16),
                   jax.ShapeDtypeStruct((SSM_TILES, 128, 2 * ns), F32),
                   jax.ShapeDtypeStruct((SSM_TILES, 2 * ns, 128), F32),
                   jax.ShapeDtypeStruct((1, D_MODEL), F32),
                   jax.ShapeDtypeStruct((1, SSM_STATES), F32), jax.ShapeDtypeStruct((1, SSM_STATES), F32)],
        scratch_shapes=[pltpu.VMEM((tc, SSM_STATES), F32)] * 2 + [pltpu.VMEM((1, SSM_STATES), F32)] * 2
                       + [pltpu.VMEM((8, SSM_STATES), F32)] * 10,
        compiler_params=_params(("arbitrary", "arbitrary")),
    )(dgl, y, u, h_re, h_im, h_re, h_im, wt, ct, a_re, a_im, dskip)


def _ssm_in_weights(bb_re2, bb_im2):
    eye = jnp.eye(8, dtype=F32)[None, :, None, :, None]

    def one(bb):
        t = bb.reshape(8, 8, 64, 16).transpose(0, 1, 3, 2)
        return (t[:, :, :, None, :] * eye).reshape(8, 128, 512)

    return jnp.concatenate([one(bb_re2), one(bb_im2)], axis=-1).astype(MXU_DTYPE)


def _ssm_in_weights_bwd(dwt):
    eye = jnp.eye(8, dtype=F32)[None, :, None, :, None]

    def one(d):
        t = (d.reshape(8, 8, 16, 8, 64) * eye).sum(axis=3)
        return t.transpose(0, 1, 3, 2).reshape(64, 1024)

    return one(dwt[..., :512]), one(dwt[..., 512:])


def _ssm_out_weights(c_re, c_im):
    eye = jnp.eye(8, dtype=F32)[None, :, None, :, None]

    def one(cc):
        t = cc.reshape(8, 8, 16, 64).transpose(0, 1, 3, 2)
        return (t[:, :, :, None, :] * eye).reshape(8, 512, 128)

    return jnp.concatenate([one(c_re), -one(c_im)], axis=1).astype(MXU_DTYPE)


def _ssm_out_weights_bwd(dct):
    eye = jnp.eye(8, dtype=F32)[None, :, None, :, None]

    def one(d):
        t = (d.reshape(8, 8, 64, 8, 16) * eye).sum(axis=3)
        return t.transpose(0, 1, 3, 2).reshape(64, 16, 64)

    return one(dct[:, :512]), -one(dct[:, 512:])


def _softmax(s):
    m = jnp.max(s, axis=-1, keepdims=True)
    e = jnp.exp(s - m)
    return e / jnp.sum(e, axis=-1, keepdims=True)


def xattn_fwd(q, kv, bsz, seq):
    tq = _tile(seq, 512)
    nq = seq // tq
    scale = XA_HEAD_DIM ** -0.5

    def body(q_ref, k_ref, v_ref, o_ref):
        s = lax.dot_general(q_ref[...], k_ref[...], _NT, preferred_element_type=F32) * scale
        p = _softmax(s)
        o_ref[...] = _dot(p, v_ref[...], _NN).astype(o_ref.dtype)

    qs = pl.BlockSpec((tq, XA_HEAD_DIM), lambda b, h, i: (b * nq + i, h))
    return pl.pallas_call(
        body, name="xattn_fwd", grid=(bsz, XA_HEADS, nq),
        in_specs=[qs, pl.BlockSpec((MEM_LEN, XA_HEAD_DIM), lambda b, h, i: (b, h)),
                  pl.BlockSpec((MEM_LEN, XA_HEAD_DIM), lambda b, h, i: (b, XA_HEADS + h))],
        out_specs=qs, out_shape=jax.ShapeDtypeStruct((bsz * seq, D_MODEL), BF16),
        compiler_params=_params(("parallel", "parallel", "parallel")),
    )(q, kv, kv)


def xattn_bwd(q, kv, do, bsz, seq):
    tq = _tile(seq, 512)
    nq = seq // tq
    scale = XA_HEAD_DIM ** -0.5

    def body(q_ref, k_ref, v_ref, do_ref, dq_ref, dk_ref, dv_ref):
        @pl.when(pl.program_id(2) == 0)
        def _():
            dk_ref[...] = jnp.zeros_like(dk_ref)
            dv_ref[...] = jnp.zeros_like(dv_ref)

        qv, kk, vv, dov = q_ref[...], k_ref[...], v_ref[...], do_ref[...]
        s = lax.dot_general(qv, kk, _NT, preferred_element_type=F32) * scale
        p = _softmax(s)
        dp = lax.dot_general(dov, vv, _NT, preferred_element_type=F32)
        ds = (p * (dp - jnp.sum(dp * p, axis=-1, keepdims=True)) * scale).astype(MXU_DTYPE)
        dq_ref[...] = lax.dot_general(ds, kk, _NN, preferred_element_type=F32).astype(dq_ref.dtype)
        dk_ref[...] += lax.dot_general(ds, qv, _TN, preferred_element_type=F32)
        dv_ref[...] += lax.dot_general(p.astype(MXU_DTYPE), dov, _TN, preferred_element_type=F32)

    qs = pl.BlockSpec((tq, XA_HEAD_DIM), lambda b, h, i: (b * nq + i, h))
    ks = pl.BlockSpec((MEM_LEN, XA_HEAD_DIM), lambda b, h, i: (b, h))
    vs = pl.BlockSpec((MEM_LEN, XA_HEAD_DIM), lambda b, h, i: (b, XA_HEADS + h))
    dkv = jax.ShapeDtypeStruct((bsz * MEM_LEN, D_MODEL), F32)
    dq, dk, dv = pl.pallas_call(
        body, name="xattn_bwd", grid=(bsz, XA_HEADS, nq),
        in_specs=[qs, ks, vs, qs], out_specs=[qs, ks, ks],
        out_shape=[jax.ShapeDtypeStruct((bsz * seq, D_MODEL), BF16), dkv, dkv],
        compiler_params=_params(("parallel", "parallel", "arbitrary")),
    )(q, kv, kv, do)
    return dq, dk, dv


CONV_HALO = 16


def _shift_down(x, prev, n):
    r = pltpu.roll(x, n, 0)
    row = lax.broadcasted_iota(jnp.int32, x.shape, 0)
    last = prev.shape[0]
    for k in range(n):
        r = jnp.where(row == k, prev[last - n + k:last - n + k + 1, :], r)
    return r


def _shift_up(x, nxt, n):
    rows = x.shape[0]
    r = pltpu.roll(x, rows - n, 0)
    row = lax.broadcasted_iota(jnp.int32, x.shape, 0)
    for k in range(n):
        r = jnp.where(row == rows - n + k, nxt[k:k + 1, :], r)
    return r


def _conv_taps(u, prev, w, b):
    return b + w[2:3] * u + w[1:2] * _shift_down(u, prev, 1) + w[0:1] * _shift_down(u, prev, 2)


def conv_fwd(up, cw, cb, bsz, seq):
    tc = _tile(seq, 512)
    nc = seq // tc
    hb = tc // CONV_HALO
    half = N_DEV // 2

    def body(uv_ref, ug_ref, pv_ref, pg_ref, wv_ref, wg_ref, bv_ref, bg_ref, o_ref):
        c = pl.program_id(2)
        pv = jnp.where(c > 0, pv_ref[...].astype(F32), 0.0)
        pg = jnp.where(c > 0, pg_ref[...].astype(F32), 0.0)
        val = _conv_taps(uv_ref[...].astype(F32), pv, wv_ref[...], bv_ref[...])
        gate = _conv_taps(ug_ref[...].astype(F32), pg, wg_ref[...], bg_ref[...])
        o_ref[...] = (gate * jax.nn.sigmoid(gate) * val).astype(o_ref.dtype)

    def cur(off):
        return pl.BlockSpec((None, tc, FF_SHARD), lambda b, j, c: (j + off, b * nc + c, 0))

    def prv(off):
        return pl.BlockSpec((None, CONV_HALO, FF_SHARD), lambda b, j, c: (j + off, jnp.maximum((b * nc + c) * hb - 1, 0), 0))

    def par(rows, off):
        return pl.BlockSpec((None, rows, FF_SHARD), lambda b, j, c: (j + off, 0, 0))

    return pl.pallas_call(
        body, name="conv_fwd", grid=(bsz, half, nc),
        in_specs=[cur(0), cur(half), prv(0), prv(half), par(3, 0), par(3, half), par(1, 0), par(1, half)],
        out_specs=cur(0), out_shape=jax.ShapeDtypeStruct((half, bsz * seq, FF_SHARD), BF16),
        compiler_params=_params(("parallel", "parallel", "parallel")),
    )(up, up, up, up, cw, cw, cb, cb)


def conv_bwd_taps(up, cw, cb, dact, bsz, seq):
    tc = _tile(seq, 512)
    nc = seq // tc
    hb = tc // CONV_HALO
    half = N_DEV // 2

    def body(uv_ref, ug_ref, pv_ref, pg_ref, wv_ref, wg_ref, bv_ref, bg_ref, da_ref,
             dc_ref, dwv_ref, dwg_ref, dbv_ref, dbg_ref):
        b, c = pl.program_id(1), pl.program_id(2)

        @pl.when((b == 0) & (c == 0))
        def _():
            for r in (dwv_ref, dwg_ref, dbv_ref, dbg_ref):
                r[...] = jnp.zeros_like(r)

        pv = jnp.where(c > 0, pv_ref[...].astype(F32), 0.0)
        pg = jnp.where(c > 0, pg_ref[...].astype(F32), 0.0)
        uv, ug = uv_ref[...].astype(F32), ug_ref[...].astype(F32)
        val = _conv_taps(uv, pv, wv_ref[...], bv_ref[...])
        gate = _conv_taps(ug, pg, wg_ref[...], bg_ref[...])
        sg = jax.nn.sigmoid(gate)
        da = da_ref[...].astype(F32)
        dval = da * gate * sg
        dgate = da * val * sg * (1.0 + gate * (1.0 - sg))
        dc_ref[0] = dval.astype(dc_ref.dtype)
        dc_ref[1] = dgate.astype(dc_ref.dtype)
        for dcv, u, prev, dw_ref, db_ref in ((dval, uv, pv, dwv_ref, dbv_ref), (dgate, ug, pg, dwg_ref, dbg_ref)):
            db_ref[...] += jnp.sum(dcv, axis=0, keepdims=True)
            dw_ref[2:3, :] += jnp.sum(dcv * u, axis=0, keepdims=True)
            dw_ref[1:2, :] += jnp.sum(dcv * _shift_down(u, prev, 1), axis=0, keepdims=True)
            dw_ref[0:1, :] += jnp.sum(dcv * _shift_down(u, prev, 2), axis=0, keepdims=True)

    def cur(off):
        return pl.BlockSpec((None, tc, FF_SHARD), lambda j, b, c: (j + off, b * nc + c, 0))

    def prv(off):
        return pl.BlockSpec((None, CONV_HALO, FF_SHARD), lambda j, b, c: (j + off, jnp.maximum((b * nc + c) * hb - 1, 0), 0))

    def par(rows, off):
        return pl.BlockSpec((None, rows, FF_SHARD), lambda j, b, c: (j + off, 0, 0))

    t = bsz * seq
    hs = jax.ShapeDtypeStruct((2, half, t, FF_SHARD), BF16)
    ws = jax.ShapeDtypeStruct((half, 3, FF_SHARD), F32)
    bs = jax.ShapeDtypeStruct((half, 1, FF_SHARD), F32)
    dc, dwv, dwg, dbv, dbg = pl.pallas_call(
        body, name="conv_bwd_taps", grid=(half, bsz, nc),
        in_specs=[cur(0), cur(half), prv(0), prv(half), par(3, 0), par(3, half), par(1, 0), par(1, half), cur(0)],
        out_specs=[pl.BlockSpec((2, None, tc, FF_SHARD), lambda j, b, c: (0, j, b * nc + c, 0)),
                   par(3, 0), par(3, 0), par(1, 0), par(1, 0)],
        out_shape=[hs, ws, ws, bs, bs],
        compiler_params=_params(("parallel", "arbitrary", "arbitrary")),
    )(up, up, up, up, cw, cw, cb, cb, dact)
    return (dc.reshape(N_DEV, t, FF_SHARD), jnp.concatenate([dwv, dwg], axis=0),
            jnp.concatenate([dbv, dbg], axis=0))


def conv_bwd_input(dconv, cw, bsz, seq):
    tc = _tile(seq, 512)
    nc = seq // tc
    hb = tc // CONV_HALO
    nblk = bsz * seq // CONV_HALO

    def body(d_ref, n_ref, w_ref, o_ref):
        c = pl.program_id(2)
        nxt = jnp.where(c < nc - 1, n_ref[...].astype(F32), 0.0)
        d = d_ref[...].astype(F32)
        w = w_ref[...]
        o_ref[...] = (w[2:3] * d + w[1:2] * _shift_up(d, nxt, 1) + w[0:1] * _shift_up(d, nxt, 2)).astype(o_ref.dtype)

    cur = pl.BlockSpec((None, tc, FF_SHARD), lambda j, b, c: (j, b * nc + c, 0))
    return pl.pallas_call(
        body, name="conv_bwd_input", grid=(N_DEV, bsz, nc),
        in_specs=[cur, pl.BlockSpec((None, CONV_HALO, FF_SHARD),
                                    lambda j, b, c: (j, jnp.minimum((b * nc + c + 1) * hb, nblk - 1), 0)),
                  pl.BlockSpec((None, 3, FF_SHARD), lambda j, b, c: (j, 0, 0))],
        out_specs=cur, out_shape=jax.ShapeDtypeStruct(dconv.shape, BF16),
        compiler_params=_params(("parallel", "parallel", "parallel")),
    )(dconv, dconv, cw)


def _my_index():
    return 4 * lax.axis_index("x") + 2 * lax.axis_index("y") + lax.axis_index("c")


def _peer(k):
    return (lax.axis_index("x") ^ ((k >> 2) & 1), lax.axis_index("y") ^ ((k >> 1) & 1),
            lax.axis_index("c") ^ (k & 1))


def all_gather(name, a, out_dtype):
    def body(a_ref, o_ref, stage, send_sems, recv_sems, local_sem):
        me = _my_index()
        stage[...] = a_ref[...].astype(out_dtype)
        local = pltpu.make_async_copy(stage, o_ref.at[me], local_sem)
        local.start()
        sends = []
        for k in range(1, N_DEV):
            cp = pltpu.make_async_remote_copy(
                src_ref=stage, dst_ref=o_ref.at[me], send_sem=send_sems.at[k - 1], recv_sem=recv_sems.at[k - 1],
                device_id=_peer(k), device_id_type=pl.DeviceIdType.MESH)
            cp.start()
            sends.append(cp)
        for k in range(1, N_DEV):
            pltpu.make_async_remote_copy(
                src_ref=stage, dst_ref=o_ref.at[me ^ k], send_sem=send_sems.at[k - 1], recv_sem=recv_sems.at[k - 1],
                device_id=_peer(k), device_id_type=pl.DeviceIdType.MESH).wait_recv()
        for cp in sends:
            cp.wait_send()
        local.wait()

    return pl.pallas_call(
        body, name=name, in_specs=[pl.BlockSpec(memory_space=pltpu.VMEM)],
        out_specs=pl.BlockSpec(memory_space=pltpu.HBM),
        out_shape=jax.ShapeDtypeStruct((N_DEV,) + a.shape, out_dtype),
        scratch_shapes=[pltpu.VMEM(a.shape, out_dtype), pltpu.SemaphoreType.DMA((N_DEV - 1,)),
                        pltpu.SemaphoreType.DMA((N_DEV - 1,)), pltpu.SemaphoreType.DMA],
        compiler_params=pltpu.CompilerParams(vmem_limit_bytes=VMEM_LIMIT),
    )(a)


def exchange(name, g):
    def body(g_ref, r_ref, send_sems, recv_sems, local_sem):
        me = _my_index()
        local = pltpu.make_async_copy(g_ref.at[me], r_ref.at[me], local_sem)
        local.start()
        sends = []
        for k in range(1, N_DEV):
            cp = pltpu.make_async_remote_copy(
                src_ref=g_ref.at[me ^ k], dst_ref=r_ref.at[me], send_sem=send_sems.at[k - 1],
                recv_sem=recv_sems.at[k - 1], device_id=_peer(k), device_id_type=pl.DeviceIdType.MESH)
            cp.start()
            sends.append(cp)
        for k in range(1, N_DEV):
            pltpu.make_async_remote_copy(
                src_ref=g_ref.at[me], dst_ref=r_ref.at[me ^ k], send_sem=send_sems.at[k - 1],
                recv_sem=recv_sems.at[k - 1], device_id=_peer(k), device_id_type=pl.DeviceIdType.MESH).wait_recv()
        for cp in sends:
            cp.wait_send()
        local.wait()

    return pl.pallas_call(
        body, name=name, in_specs=[pl.BlockSpec(memory_space=pltpu.HBM)],
        out_specs=pl.BlockSpec(memory_space=pltpu.HBM),
        out_shape=jax.ShapeDtypeStruct(g.shape, g.dtype),
        scratch_shapes=[pltpu.SemaphoreType.DMA((N_DEV - 1,)), pltpu.SemaphoreType.DMA((N_DEV - 1,)),
                        pltpu.SemaphoreType.DMA],
    )(g)


_HBM = pl.BlockSpec(memory_space=pltpu.HBM)
_SEM = pl.BlockSpec(memory_space=pltpu.SEMAPHORE)
_DATAFLOW = pltpu.SideEffectType.DATAFLOW_SIDE_EFFECTING


def _split_copies(gather, src_ref, land_ref, send_sems, recv_sems, local_sem):
    me = _my_index()

    def part(j):
        return src_ref if gather else src_ref.at[j]

    local = pltpu.make_async_copy(part(me), land_ref.at[me], local_sem)
    sends = [pltpu.make_async_remote_copy(
        src_ref=part(me ^ k), dst_ref=land_ref.at[me], send_sem=send_sems.at[k - 1], recv_sem=recv_sems.at[k - 1],
        device_id=_peer(k), device_id_type=pl.DeviceIdType.MESH) for k in range(1, N_DEV)]
    recvs = [pltpu.make_async_remote_copy(
        src_ref=part(me ^ k), dst_ref=land_ref.at[me ^ k], send_sem=send_sems.at[k - 1], recv_sem=recv_sems.at[k - 1],
        device_id=_peer(k), device_id_type=pl.DeviceIdType.MESH) for k in range(1, N_DEV)]
    return local, sends, recvs


def split_start(name, src, gather):
    land_shape = ((N_DEV,) + src.shape) if gather else src.shape

    def body(src_ref, land_ref, send_sems, recv_sems, local_sem, src_thru, land_thru, token):
        local, sends, _ = _split_copies(gather, src_ref, land_ref, send_sems, recv_sems, local_sem)
        local.start()
        for cp in sends:
            cp.start()
        token[...] = jnp.zeros_like(token)

    dma7 = pltpu.SemaphoreType.DMA((N_DEV - 1,))
    out = pl.pallas_call(
        body, name=name,
        out_shape=(dma7, dma7, pltpu.SemaphoreType.DMA(()), pltpu.HBM(src.shape, src.dtype),
                   pltpu.HBM(land_shape, src.dtype), jax.ShapeDtypeStruct((8, 128), F32)),
        in_specs=(_HBM, _HBM), out_specs=(_SEM, _SEM, _SEM, _HBM, _HBM, pl.BlockSpec(memory_space=pltpu.VMEM)),
        input_output_aliases={0: 3, 1: 4},
        compiler_params=pltpu.CompilerParams(has_side_effects=_DATAFLOW),
    )(pltpu.with_memory_space_constraint(src, pltpu.HBM),
      pltpu.with_memory_space_constraint(lax.empty(land_shape, src.dtype), pltpu.HBM))
    return out[:5], out[5][0, 0]


def split_wait(name, handles, after, gather):
    send_sems, recv_sems, local_sem, src_thru, land_thru = handles

    def body(src_ref, land_ref, send_sems, recv_sems, local_sem, after_ref, src_dead, got_ref, token):
        local, sends, recvs = _split_copies(gather, src_ref, land_ref, send_sems, recv_sems, local_sem)
        local.wait()
        for cp in recvs:
            cp.wait_send()
            cp.wait_recv()
        token[...] = jnp.zeros_like(token)

    out = pl.pallas_call(
        body, name=name,
        out_shape=(pltpu.HBM(src_thru.shape, src_thru.dtype), pltpu.HBM(land_thru.shape, land_thru.dtype),
                   jax.ShapeDtypeStruct((8, 128), F32)),
        in_specs=(_HBM, _HBM, _SEM, _SEM, _SEM, pl.BlockSpec(memory_space=pl.ANY)),
        out_specs=(_HBM, _HBM, pl.BlockSpec(memory_space=pltpu.VMEM)),
        input_output_aliases={0: 0, 1: 1},
        compiler_params=pltpu.CompilerParams(has_side_effects=_DATAFLOW),
    )(src_thru, land_thru, send_sems, recv_sems, local_sem, after)
    return out[1], out[2][0, 0]


def sum_parts(name, r):
    _, rows, cols = r.shape

    def body(r_ref, o_ref):
        acc = r_ref[0].astype(F32)
        for s in range(1, N_DEV):
            acc = acc + r_ref[s].astype(F32)
        o_ref[...] = acc

    return pl.pallas_call(body, name=name, out_shape=jax.ShapeDtypeStruct((rows, cols), F32),
                          compiler_params=_params())(r)


def adamw(name, w, m, v, parts=None, g=None, layer=0, into=None):
    _, rows, cols = w.shape
    br = _tile(rows, 256, 16)
    c1 = 1.0 / (1.0 - ADAM_B1 ** ADAM_STEP)
    c2 = 1.0 / (1.0 - ADAM_B2 ** ADAM_STEP)

    def body(g_ref, w_ref, m_ref, v_ref, *rest):
        og_ref, od_ref, om_ref, ov_ref = rest[-4:]
        if parts is None:
            gs = g_ref[...]
        else:
            gs = g_ref[0].astype(F32)
            for s in range(1, N_DEV):
                gs = gs + g_ref[s].astype(F32)
        mn = ADAM_B1 * m_ref[...] + (1.0 - ADAM_B1) * gs
        vn = ADAM_B2 * v_ref[...] + (1.0 - ADAM_B2) * (gs * gs)
        og_ref[...] = gs
        om_ref[...] = mn
        ov_ref[...] = vn
        od_ref[...] = -ADAM_LR * ((mn * c1) / (jnp.sqrt(vn * c2) + ADAM_EPS) + ADAM_WD * w_ref[...])

    blk = pl.BlockSpec((None, br, cols), lambda i: (layer, i, 0))
    if parts is None:
        gspec = pl.BlockSpec((br, cols), lambda i: (i, 0))
    else:
        gspec = pl.BlockSpec((N_DEV, br, cols), lambda i: (0, i, 0))
    earlier = [] if into is None else list(into)
    return pl.pallas_call(
        body, name=name, grid=(rows // br,),
        in_specs=[gspec, blk, blk, blk] + [pl.BlockSpec(memory_space=pl.ANY)] * len(earlier),
        out_specs=[blk] * 4, out_shape=[jax.ShapeDtypeStruct(w.shape, F32)] * 4,
        input_output_aliases={4 + k: k for k in range(len(earlier))},
        compiler_params=_params(("parallel",)),
    )(g if parts is None else parts, w, m, v, *earlier)


SMALL = ("norm_mix", "norm_xattn", "norm_ffn", "norm_mem", "norm_final", "pool_w", "pool_scale",
         "ssm_lam_re", "ssm_lam_im", "ssm_log_dt", "ssm_b_re", "ssm_b_im", "ssm_c_re", "ssm_c_im",
         "ffn_conv_b", "ssm_d", "ffn_conv_w")
SMALL_SHARDED = {"ssm_d": 1, "ffn_conv_w": 2}
BIG = ("ab_w_in", "ab_w_out", "ssm_w_in", "ssm_w_glu", "xa_w_q", "xa_w_kv", "xa_w_o", "ffn_w_up", "ffn_w_down")
WEIGHTS = ("norm_mix", "norm_xattn", "norm_ffn", "norm_mem", "norm_final", "ab_w_in", "pool_w", "pool_scale",
           "ab_w_out", "ssm_w_in", "ssm_lam_re", "ssm_lam_im", "ssm_log_dt", "ssm_b_re", "ssm_b_im", "ssm_c_re",
           "ssm_c_im", "ssm_d", "ssm_w_glu", "xa_w_q", "xa_w_kv", "xa_w_o", "ffn_w_up", "ffn_conv_w", "ffn_conv_b",
           "ffn_w_down")


def _rows8(g):
    return g.reshape(N_DEV, g.size // (N_DEV * D_MODEL), D_MODEL)


def _square(a):
    return a.reshape(D_MODEL, D_MODEL)


_LAYOUT = {"ab_w_out": _square, "ssm_w_in": _square, "xa_w_q": _square, "xa_w_o": _square,
           "ffn_w_down": lambda a: a.reshape(N_DEV // 2, FF_SHARD, D_MODEL)}
GATHER_ORDER = (("ab_w_in", 0), ("ab_w_out", 0), ("xa_w_q", 0), ("xa_w_kv", 0), ("xa_w_o", 0), ("ffn_w_up", 0),
                ("ffn_w_down", 0), ("ssm_w_in", 0), ("ssm_w_glu", 0), ("xa_w_q", 1), ("xa_w_kv", 1),
                ("xa_w_o", 1), ("ffn_w_up", 1), ("ffn_w_down", 1))
GATHER_AHEAD = 5


class _Step:
    def __init__(self, master, small):
        self.master, self.small = master, small
        self.pending, self.gathers, self.weights, self.sent = [], {}, {}, []

    def follow(self, v):
        for z in self.pending:
            v = v + z
        self.pending = []
        return v

    def start_gathers(self, upto, zero):
        for n, l in GATHER_ORDER[len(self.gathers):upto]:
            shard = (self.master[n][l] + zero).astype(MXU_DTYPE)
            self.gathers[(n, l)], z = split_start(f"ags_{n}{l}", shard, gather=True)
            self.pending.append(z)

    def weight(self, n, l, after):
        if (n, l) not in self.weights:
            full, z = split_wait(f"agw_{n}{l}", self.gathers[(n, l)], after, gather=True)
            self.weights[(n, l)] = _LAYOUT.get(n, lambda a: a)(full)
            self.start_gathers(GATHER_ORDER.index((n, l)) + 1 + GATHER_AHEAD, z)
        return self.weights[(n, l)]

    def send_grad(self, n, l, part):
        h, z = split_start(f"xs_{n}{l}", part, gather=False)
        self.pending.append(z)
        self.sent.append((n, l, h))


def _layer_tail(st, l, x_in, mem_n, acts):
    bsz, seq = acts["bsz"], acts["seq"]
    p = st.small
    w_q, w_kv = st.weight("xa_w_q", l, x_in), st.weight("xa_w_kv", l, x_in)
    hq = rms_fwd(f"rms_xattn{l}", x_in, st.follow(p["norm_xattn"][l]))
    q = mm_nn(f"xa_q{l}", hq, w_q)
    kv = mm_nn_bs(f"xa_kv{l}", mem_n, w_kv)
    o = xattn_fwd(q, kv, bsz, seq)
    x_mid = mm_nn(f"xa_o{l}", o, st.weight("xa_w_o", l, o), res=x_in, out_dtype=F32)
    w_up = st.weight("ffn_w_up", l, x_mid)
    hf = rms_fwd(f"rms_ffn{l}", x_mid, st.follow(p["norm_ffn"][l]))
    up = mm_nn_bs(f"ffn_up{l}", hf, w_up, stacked_out=True)
    act = conv_fwd(up, p["ffn_conv_w"][l], p["ffn_conv_b"][l], bsz, seq)
    x_out = mm_as_nn(f"ffn_down{l}", act, st.weight("ffn_w_down", l, act), res=x_mid)
    acts[l].update(x_in=x_in, hq=hq, q=q, kv=kv, o=o, x_mid=x_mid, hf=hf, up=up, act=act)
    return x_out


def _layer_tail_bwd(st, l, dx, mem_n, acts, grads):
    a = acts[l]
    bsz, seq = acts["bsz"], acts["seq"]
    p = st.small
    dact = mm_nt_os(f"d_act{l}", dx, st.weight("ffn_w_down", l, dx))
    st.send_grad("ffn_w_down", l, _rows8(mm_tn(f"g_ffn_down{l}", a["act"], dx, a_stacked=True)))
    dconv, dcw, dcb = conv_bwd_taps(a["up"], p["ffn_conv_w"][l], p["ffn_conv_b"][l], dact, bsz, seq)
    grads["ffn_conv_w"][l] = dcw
    grads["ffn_conv_b"][l] = dcb
    dup = conv_bwd_input(dconv, p["ffn_conv_w"][l], bsz, seq)
    dhf = mm_nt_bs(f"d_hf{l}", dup, st.weight("ffn_w_up", l, dx), dc_stacked=True)
    st.send_grad("ffn_w_up", l, mm_tn(f"g_ffn_up{l}", a["hf"], dup, dc_stacked=True))
    dx_mid, grads["norm_ffn"][l] = rms_bwd(f"rms_ffn_bwd{l}", a["x_mid"], st.follow(p["norm_ffn"][l]), dhf, dres=dx)
    do = mm_nt(f"d_o{l}", dx_mid, st.weight("xa_w_o", l, dx))
    st.send_grad("xa_w_o", l, _rows8(mm_tn(f"g_xa_o{l}", a["o"], dx_mid)))
    dq, dk, dv = xattn_bwd(a["q"], a["kv"], do, bsz, seq)
    dkv = jnp.concatenate([dk, dv], axis=1).astype(BF16)
    dhq = mm_nt(f"d_hq{l}", dq, st.weight("xa_w_q", l, dx))
    st.send_grad("xa_w_q", l, _rows8(mm_tn(f"g_xa_q{l}", a["hq"], dq)))
    dmem_n = mm_nt_bs(f"d_memn{l}", dkv, st.weight("xa_w_kv", l, dx), out_dtype=F32)
    st.send_grad("xa_w_kv", l, mm_tn(f"g_xa_kv{l}", mem_n, dkv, dc_cols=2 * D_MODEL // N_DEV))
    dx_in, grads["norm_xattn"][l] = rms_bwd(f"rms_xattn_bwd{l}", a["x_in"], st.follow(p["norm_xattn"][l]), dhq,
                                            dres=dx_mid)
    return dx_in, dmem_n


def kernel(x, mem, norm_mix, norm_xattn, norm_ffn, norm_mem, norm_final, ab_w_in, pool_w, pool_scale, ab_w_out, ssm_w_in, ssm_lam_re, ssm_lam_im, ssm_log_dt, ssm_b_re, ssm_b_im, ssm_c_re, ssm_c_im, ssm_d, ssm_w_glu, xa_w_q, xa_w_kv, xa_w_o, ffn_w_up, ffn_conv_w, ffn_conv_b, ffn_w_down, loss_target, m_norm_mix, m_norm_xattn, m_norm_ffn, m_norm_mem, m_norm_final, m_ab_w_in, m_pool_w, m_pool_scale, m_ab_w_out, m_ssm_w_in, m_ssm_lam_re, m_ssm_lam_im, m_ssm_log_dt, m_ssm_b_re, m_ssm_b_im, m_ssm_c_re, m_ssm_c_im, m_ssm_d, m_ssm_w_glu, m_xa_w_q, m_xa_w_kv, m_xa_w_o, m_ffn_w_up, m_ffn_conv_w, m_ffn_conv_b, m_ffn_w_down, v_norm_mix, v_norm_xattn, v_norm_ffn, v_norm_mem, v_norm_final, v_ab_w_in, v_pool_w, v_pool_scale, v_ab_w_out, v_ssm_w_in, v_ssm_lam_re, v_ssm_lam_im, v_ssm_log_dt, v_ssm_b_re, v_ssm_b_im, v_ssm_c_re, v_ssm_c_im, v_ssm_d, v_ssm_w_glu, v_xa_w_q, v_xa_w_kv, v_xa_w_o, v_ffn_w_up, v_ffn_conv_w, v_ffn_conv_b, v_ffn_w_down):
    given = dict(locals())
    master = {n: given[n] for n in WEIGHTS}
    mom1 = {n: given["m_" + n] for n in WEIGHTS}
    mom2 = {n: given["v_" + n] for n in WEIGHTS}
    bsz, seq, d = x.shape
    t = bsz * seq
    me = _my_index()

    conv_w_st = all_gather("ag_ffn_conv_w", ffn_conv_w, F32)
    dskip = all_gather("ag_ssm_d", ssm_d.reshape(1, 128), F32).reshape(1, D_MODEL)
    st = _Step(master, {
        "norm_xattn": norm_xattn, "norm_ffn": norm_ffn,
        "ffn_conv_w": [conv_w_st[:, l] for l in range(2)],
        "ffn_conv_b": [ffn_conv_b[l].reshape(N_DEV, 1, FF_SHARD) for l in range(2)],
    })
    st.start_gathers(1, 0.0)

    acts = {"bsz": bsz, "seq": seq, 0: {}, 1: {}}
    x0 = x.reshape(t, d)
    mem2 = mem.reshape(bsz * MEM_LEN, d)
    mem_n = rms_fwd("rms_mem", mem2, st.follow(norm_mem))
    pscale = pool_scale.reshape(1, SB_WIDTH)

    w_in = st.weight("ab_w_in", 0, mem_n)
    h0 = rms_fwd("rms_mix0", x0, st.follow(norm_mix[0]))
    proj = mm_nn_bs("ab_in", h0, w_in, out_dtype=F32)
    a_out, rsum = sb_attn_fwd(proj, bsz, seq)
    p_out = pool_fwd(proj, pool_w[0], pscale, bsz, seq)
    w_out = st.weight("ab_w_out", 0, a_out)
    x1 = mm_nn("ab_out_a", a_out, w_out, res=x0, out_dtype=F32)
    x1 = mm_nn("ab_out_p", p_out, w_out, res=x1, koff=SB_WIDTH, out_dtype=F32)
    x3 = _layer_tail(st, 0, x1, mem_n, acts)

    b_re2 = ssm_b_re.reshape(64, 1024)
    b_im2 = ssm_b_im.reshape(64, 1024)
    log_dt = ssm_log_dt.reshape(64, 1)
    lb_re, lb_im, bb_re2, bb_im2 = ssm_prep(ssm_lam_re[0], ssm_lam_im[0], log_dt, b_re2, b_im2)
    wt = _ssm_in_weights(bb_re2, bb_im2)
    ct = _ssm_out_weights(ssm_c_re[0], ssm_c_im[0])
    a_re = lb_re.reshape(1, SSM_STATES)
    a_im = lb_im.reshape(1, SSM_STATES)
    w_ssm_in = st.weight("ssm_w_in", 0, x3)
    h1 = rms_fwd("rms_mix1", x3, st.follow(norm_mix[1]))
    u = mm_nn("ssm_in", h1, w_ssm_in, out_dtype=F32)
    y, gl, h_re, h_im = ssm_fwd(u, wt, ct, a_re, a_im, dskip, bsz, seq)
    glu = mm_nn_bs("ssm_glu", gl, st.weight("ssm_w_glu", 0, gl), out_dtype=F32)
    x4 = glu_fwd(glu, x3)
    x6 = _layer_tail(st, 1, x4, mem_n, acts)

    loss_row, dx, g_norm_final = loss_head(x6, norm_final, loss_target.reshape(t, d))
    loss = lax.psum(loss_row[0, 0], MESH_AXES)

    grads = {n: [None, None] for n in ("ffn_conv_w", "ffn_conv_b", "norm_ffn", "norm_xattn", "norm_mix")}
    dx4, dmem_1 = _layer_tail_bwd(st, 1, dx, mem_n, acts, grads)
    dglu = glu_bwd(glu, dx4)
    dgl = mm_nt_bs("d_gl", dglu, st.weight("ssm_w_glu", 0, dx))
    st.send_grad("ssm_w_glu", 0, mm_tn("g_ssm_glu", gl, dglu, dc_cols=2 * D_MODEL // N_DEV))
    du, dwt, dct, g_dskip, da_re, da_im = ssm_bwd(dgl, y, u, h_re, h_im, wt, ct, a_re, a_im, dskip, bsz, seq)
    dbb_re, dbb_im = _ssm_in_weights_bwd(dwt)
    g_c_re, g_c_im = _ssm_out_weights_bwd(dct)
    g_lam_re, g_lam_im, g_log_dt, g_b_re, g_b_im = ssm_prep_bwd(
        ssm_lam_re[0], ssm_lam_im[0], log_dt, b_re2, b_im2, da_re.reshape(64, 64), da_im.reshape(64, 64),
        dbb_re, dbb_im)
    dh1 = mm_nt("d_h1", du, st.weight("ssm_w_in", 0, dx))
    st.send_grad("ssm_w_in", 0, _rows8(mm_tn("g_ssm_in", h1, du)))
    dx3, grads["norm_mix"][1] = rms_bwd("rms_mix1_bwd", x3, st.follow(norm_mix[1]), dh1, dres=dx4)

    dx1, dmem_0 = _layer_tail_bwd(st, 0, dx3, mem_n, acts, grads)
    dcat = mm_nt("d_cat", dx1, st.weight("ab_w_out", 0, dx))
    st.send_grad("ab_w_out", 0, _rows8(jnp.concatenate(
        [mm_tn("g_ab_out_a", a_out, dx1), mm_tn("g_ab_out_p", p_out, dx1)], axis=0)))
    dq, dk, dv = sb_attn_bwd(proj, rsum, dcat, bsz, seq)
    dpu, g_pool_w, g_pool_scale = pool_bwd(proj, pool_w[0], st.follow(pscale), dcat, bsz, seq)
    dproj = jnp.concatenate([dq, dk, dv, dpu], axis=1).astype(BF16)
    dh0 = mm_nt_bs("d_h0", dproj, st.weight("ab_w_in", 0, dx))
    st.send_grad("ab_w_in", 0, mm_tn("g_ab_in", h0, dproj, dc_cols=2 * D_MODEL // N_DEV))
    dx0, grads["norm_mix"][0] = rms_bwd("rms_mix0_bwd", x0, st.follow(norm_mix[0]), dh0, dres=dx1)
    _, g_norm_mem = rms_bwd("rms_mem_bwd", mem2, norm_mem, dmem_0 + dmem_1, need_dx=False)

    stepped = {}
    for n, l, handles in st.sent:
        recv, _ = split_wait(f"xw_{n}{l}", handles, dx0, gather=False)
        shape3 = (master[n].shape[0],) + recv.shape[1:]
        stepped[n] = adamw(f"adamw_{n}{l}", master[n].reshape(shape3), mom1[n].reshape(shape3),
                           mom2[n].reshape(shape3), parts=recv, layer=l, into=stepped.get(n))
    out_g, out_d, out_m, out_v = ({n: stepped[n][k].reshape(master[n].shape) for n in BIG} for k in range(4))

    small_g = {
        "norm_mix": jnp.stack([g[0] for g in grads["norm_mix"]]),
        "norm_xattn": jnp.stack([g[0] for g in grads["norm_xattn"]]),
        "norm_ffn": jnp.stack([g[0] for g in grads["norm_ffn"]]),
        "norm_mem": g_norm_mem[0], "norm_final": g_norm_final[0],
        "pool_w": g_pool_w[None], "pool_scale": g_pool_scale,
        "ssm_lam_re": g_lam_re[None], "ssm_lam_im": g_lam_im[None], "ssm_log_dt": g_log_dt.reshape(1, 64),
        "ssm_b_re": g_b_re.reshape(1, 64, 64, 16), "ssm_b_im": g_b_im.reshape(1, 64, 64, 16),
        "ssm_c_re": g_c_re[None], "ssm_c_im": g_c_im[None],
        "ffn_conv_b": jnp.stack([g.reshape(2 * D_FF) for g in grads["ffn_conv_b"]]),
        "ssm_d": g_dskip,
        "ffn_conv_w": jnp.stack([g.transpose(1, 0, 2).reshape(3, 2 * D_FF) for g in grads["ffn_conv_w"]]),
    }
    sizes = [int(small_g[n].size) for n in SMALL]
    total = sum(sizes)
    rows8 = -(-total // (N_DEV * 128 * 8)) * 8
    flat = jnp.concatenate([small_g[n].reshape(-1).astype(F32) for n in SMALL]
                           + [jnp.zeros((N_DEV * rows8 * 128 - total,), F32)])
    recv = exchange("xch_small", flat.reshape(N_DEV, rows8, 128))
    summed = all_gather("ag_small", sum_parts("sum_small", recv), F32).reshape(-1)

    def local_part(name, a):
        ax = SMALL_SHARDED.get(name)
        if ax is None:
            return a
        n_loc = a.shape[ax] // N_DEV
        return lax.dynamic_slice_in_dim(a, me * n_loc, n_loc, axis=ax)

    sg, off = {}, 0
    for n, sz in zip(SMALL, sizes):
        sg[n] = local_part(n, summed[off:off + sz].reshape(small_g[n].shape))
        off += sz
    lsizes = [int(sg[n].size) for n in SMALL]
    ltotal = sum(lsizes)
    lrows = -(-ltotal // (128 * 16)) * 16

    def pack(d_):
        return jnp.concatenate([d_[n].reshape(-1) for n in SMALL] + [jnp.zeros((lrows * 128 - ltotal,), F32)]
                               ).reshape(lrows, 128)

    padv = jnp.concatenate([mom2[n].reshape(-1) for n in SMALL] + [jnp.ones((lrows * 128 - ltotal,), F32)]
                           ).reshape(lrows, 128)
    res = adamw("adamw_small", pack(master)[None], pack(mom1)[None], padv[None], g=pack(sg))
    off = 0
    for n, sz in zip(SMALL, lsizes):
        for dst, r in zip((out_g, out_d, out_m, out_v), res):
            dst[n] = r.reshape(-1)[off:off + sz].reshape(master[n].shape)
        off += sz

    return (loss, dx0.reshape(bsz, seq, d), *[out_g[n] for n in WEIGHTS], *[out_d[n] for n in WEIGHTS],
            *[out_m[n] for n in WEIGHTS], *[out_v[n] for n in WEIGHTS])
```

```python
import functools
import math

import jax
import jax.numpy as jnp
from jax import lax
from jax.experimental import pallas as pl
from jax.experimental.pallas import tpu as pltpu

F32 = jnp.float32
BF16 = jnp.bfloat16
MXU_DTYPE = jnp.bfloat16
N_DEV = 8
MESH_AXES = ("x", "y", "c")

D_MODEL = 1024
SB_HEAD_DIM = 64
SB_WIDTH = 512
SB_BLOCK = 256
POOL_WINDOWS = (2, 4, 8, 16)
POOL_GROUP = 128
POOL_HALO = 16
SSM_TILES = 8
SSM_TILE_STATES = 512
SSM_STATES = 4096
SSM_LANES = 1024
MEM_LEN = 256
XA_HEADS = 4
XA_HEAD_DIM = 256
D_FF = 2816
FF_SHARD = 704
EPS = 1e-6
ADAM_LR = 0.001
ADAM_B1 = 0.9
ADAM_B2 = 0.999
ADAM_EPS = 1e-08
ADAM_WD = 0.01
ADAM_STEP = 10
VMEM_LIMIT = 56 * 1024 * 1024

_NN = (((1,), (0,)), ((), ()))
_NT = (((1,), (1,)), ((), ()))
_TN = (((0,), (0,)), ((), ()))


def _params(sem=None):
    if sem is None:
        return pltpu.CompilerParams(vmem_limit_bytes=VMEM_LIMIT)
    return pltpu.CompilerParams(dimension_semantics=sem, vmem_limit_bytes=VMEM_LIMIT)


def _tile(n, pref, mult=8):
    if n <= pref:
        return n
    for t in range(pref, 0, -1):
        if n % t == 0 and t % mult == 0:
            return t
    return n


def _dot(a, b, dims):
    return lax.dot_general(a.astype(MXU_DTYPE), b.astype(MXU_DTYPE), dims, preferred_element_type=F32)


def _dot_exact01(x, m01, dims=_NN):
    x1 = x.astype(BF16)
    r1 = x - x1.astype(F32)
    x2 = r1.astype(BF16)
    x3 = (r1 - x2.astype(F32)).astype(BF16)
    m = m01.astype(BF16)
    out = lax.dot_general(x1, m, dims, preferred_element_type=F32)
    out = out + lax.dot_general(x2, m, dims, preferred_element_type=F32)
    return out + lax.dot_general(x3, m, dims, preferred_element_type=F32)


def _mm(name, a, b, dims, grid, a_spec, b_spec, o_spec, out_shape, out_dtype, acc_shape, res=None, r_spec=None,
        group=1, n=None, a_sel="full", b_sel="full", o_sel="full", norm_gain=None, rms=None):
    nk = grid[2]
    if out_dtype is None:
        out_dtype = BF16
    n_out = out_shape[-1]
    vec = pl.BlockSpec((1, n_out), lambda i, j, kk: (0, 0))

    def at(sel, s):
        if sel == "lead":
            return (s,)
        if sel == "lanes":
            return (slice(None), slice(s * n, (s + 1) * n))
        return (Ellipsis,)

    extra = [] if res is None else [(res, r_spec)]
    if norm_gain is not None:
        extra.append((norm_gain.reshape(1, n_out), vec))
    if rms is not None:
        extra += [(rms[0], o_spec), (rms[1].reshape(1, n_out), vec), (rms[2], o_spec)]
    n_in = 2 + len(extra)
    if rms is not None:
        out_specs = [o_spec, vec]
        out_shapes = [jax.ShapeDtypeStruct(out_shape, F32), jax.ShapeDtypeStruct((1, n_out), F32)]
    elif norm_gain is not None:
        out_specs = [o_spec, o_spec]
        out_shapes = [jax.ShapeDtypeStruct(out_shape, out_dtype), jax.ShapeDtypeStruct(out_shape, BF16)]
    else:
        out_specs, out_shapes = o_spec, jax.ShapeDtypeStruct(out_shape, out_dtype)

    def body(*refs):
        a_ref, b_ref = refs[0], refs[1]
        ins = list(refs[2:n_in])
        r_ref = ins.pop(0) if res is not None else None
        outs = refs[n_in:]
        o_ref = outs[0]
        acc = refs[-1] if nk > 1 else None
        k = pl.program_id(2)

        def finish(val):
            if r_ref is not None:
                val = val + r_ref[...].astype(F32)
            if rms is not None:
                x_ref, g_ref, d_ref = ins
                xf = x_ref[...]
                r = lax.rsqrt(jnp.mean(xf * xf, axis=-1, keepdims=True) + EPS)
                xh = xf * r
                part = jnp.sum(val * xh, axis=0, keepdims=True)
                first = pl.program_id(0) == 0

                @pl.when(first)
                def _():
                    outs[1][...] = part

                @pl.when(jnp.logical_not(first))
                def _():
                    outs[1][...] += part

                dxh = val * g_ref[...]
                o_ref[...] = d_ref[...] + r * (dxh - xh * jnp.mean(dxh * xh, axis=-1, keepdims=True))
                return
            o_ref[...] = val.astype(out_dtype)
            if norm_gain is not None:
                r = lax.rsqrt(jnp.mean(val * val, axis=-1, keepdims=True) + EPS)
                outs[1][...] = (val * r * ins[0][...]).astype(BF16)

        def emit(s, val):
            if nk == 1:
                if o_sel == "full":
                    finish(val)
                else:
                    o_ref[at(o_sel, s)] = val.astype(out_dtype)
                return

            @pl.when(k == 0)
            def _():
                acc[at(o_sel, s)] = val

            @pl.when(k > 0)
            def _():
                acc[at(o_sel, s)] += val

        total = None
        for s in range(group):
            val = _dot(a_ref[at(a_sel, s)], b_ref[at(b_sel, s)], dims)
            if o_sel == "full":
                total = val if total is None else total + val
            else:
                emit(s, val)
        if o_sel == "full":
            emit(0, total)
        if nk > 1:
            @pl.when(k == nk - 1)
            def _():
                if o_sel == "full":
                    finish(acc[...])
                else:
                    o_ref[...] = acc[...].astype(out_dtype)

    rows_sem = "arbitrary" if rms is not None else "parallel"
    return pl.pallas_call(
        body, name=name, grid=grid, in_specs=[a_spec, b_spec] + [s for _, s in extra], out_specs=out_specs,
        out_shape=out_shapes, scratch_shapes=[pltpu.VMEM(acc_shape, F32)] if nk > 1 else [],
        compiler_params=_params((rows_sem, rows_sem, "arbitrary")),
    )(a, b, *[x for x, _ in extra])


def _row_tile(m, epi):
    return _tile(m, 512 if epi.get("rms") is not None else 1024)


def mm_nn(name, a, b, res=None, koff=0, out_dtype=None, **epi):
    m, k = a.shape
    n = b.shape[1]
    tm, tn, tk = _row_tile(m, epi), _tile(n, 1024, 128), _tile(k, 1024, 128)
    kb = koff // tk
    spec = pl.BlockSpec((tm, tn), lambda i, j, kk: (i, j))
    return _mm(name, a, b, _NN, (m // tm, n // tn, k // tk),
               pl.BlockSpec((tm, tk), lambda i, j, kk: (i, kk)),
               pl.BlockSpec((tk, tn), lambda i, j, kk: (kk + kb, j)),
               spec, (m, n), out_dtype, (tm, tn), res, spec, **epi)


def mm_nn_bs(name, a, bs, stacked_out=False, out_dtype=None):
    m, k = a.shape
    s, _, n = bs.shape
    tm, tk = _tile(m, 1024), _tile(k, 1024, 128)
    a_spec = pl.BlockSpec((tm, tk), lambda i, j, kk: (i, kk))
    if stacked_out:
        return _mm(name, a, bs, _NN, (m // tm, s, k // tk), a_spec,
                   pl.BlockSpec((None, tk, n), lambda i, j, kk: (j, kk, 0)),
                   pl.BlockSpec((None, tm, n), lambda i, j, kk: (j, i, 0)), (s, m, n), out_dtype, (tm, n))
    g = _tile(s, max(1, 1024 // n), 1)
    return _mm(name, a, bs, _NN, (m // tm, s // g, k // tk), a_spec,
               pl.BlockSpec((g, tk, n), lambda i, j, kk: (j, kk, 0)),
               pl.BlockSpec((tm, g * n), lambda i, j, kk: (i, j)), (m, s * n), out_dtype, (tm, g * n),
               group=g, n=n, b_sel="lead", o_sel="lanes")


def mm_as_nn(name, a_st, b3, res, out_dtype=F32, **epi):
    s, m, kp = a_st.shape
    n = b3.shape[2]
    tm, tn = _row_tile(m, epi), _tile(n, 1024, 128)
    spec = pl.BlockSpec((tm, tn), lambda i, j, kk: (i, j))
    return _mm(name, a_st, b3, _NN, (m // tm, n // tn, s),
               pl.BlockSpec((None, tm, kp), lambda i, j, kk: (kk, i, 0)),
               pl.BlockSpec((None, kp, tn), lambda i, j, kk: (kk, 0, j)),
               spec, (m, n), out_dtype, (tm, tn), res, spec, **epi)


def mm_nt(name, dc, b, out_dtype=None, **epi):
    m, n = dc.shape
    k = b.shape[0]
    tm, tko, tnr = _row_tile(m, epi), _tile(k, 1024, 128), _tile(n, 1024, 128)
    return _mm(name, dc, b, _NT, (m // tm, k // tko, n // tnr),
               pl.BlockSpec((tm, tnr), lambda i, j, kk: (i, kk)),
               pl.BlockSpec((tko, tnr), lambda i, j, kk: (j, kk)),
               pl.BlockSpec((tm, tko), lambda i, j, kk: (i, j)), (m, k), out_dtype, (tm, tko), **epi)


def mm_nt_bs(name, dc, bs, dc_stacked=False, out_dtype=None, **epi):
    s, k, n = bs.shape
    m = dc.shape[1] if dc_stacked else dc.shape[0]
    tm, tko = _row_tile(m, epi), _tile(k, 1024, 128)
    o_spec = pl.BlockSpec((tm, tko), lambda i, j, kk: (i, j))
    if dc_stacked:
        return _mm(name, dc, bs, _NT, (m // tm, k // tko, s),
                   pl.BlockSpec((None, tm, n), lambda i, j, kk: (kk, i, 0)),
                   pl.BlockSpec((None, tko, n), lambda i, j, kk: (kk, j, 0)), o_spec, (m, k), out_dtype, (tm, tko),
                   **epi)
    g = _tile(s, max(1, 2048 // n), 1)
    return _mm(name, dc, bs, _NT, (m // tm, k // tko, s // g),
               pl.BlockSpec((tm, g * n), lambda i, j, kk: (i, kk)),
               pl.BlockSpec((g, tko, n), lambda i, j, kk: (kk, j, 0)), o_spec, (m, k), out_dtype, (tm, tko),
               group=g, n=n, a_sel="lanes", b_sel="lead", **epi)


def mm_nt_os(name, dc, b3, out_dtype=None):
    m, n = dc.shape
    s, kp, _ = b3.shape
    tm, tnr = _tile(m, 1024), _tile(n, 1024, 128)
    return _mm(name, dc, b3, _NT, (m // tm, s, n // tnr),
               pl.BlockSpec((tm, tnr), lambda i, j, kk: (i, kk)),
               pl.BlockSpec((None, kp, tnr), lambda i, j, kk: (j, 0, kk)),
               pl.BlockSpec((None, tm, kp), lambda i, j, kk: (j, i, 0)), (s, m, kp), out_dtype, (tm, kp))


def mm_tn(name, a, dc, a_stacked=False, dc_cols=None, dc_stacked=False, out_dtype=None):
    if a_stacked:
        s, m, kp = a.shape
        n = dc.shape[1]
        tno, tmr = _tile(n, 1024, 128), _tile(m, 1024)
        return _mm(name, a, dc, _TN, (s, n // tno, m // tmr),
                   pl.BlockSpec((None, tmr, kp), lambda i, j, kk: (i, kk, 0)),
                   pl.BlockSpec((tmr, tno), lambda i, j, kk: (kk, j)),
                   pl.BlockSpec((None, kp, tno), lambda i, j, kk: (i, 0, j)), (s, kp, n), out_dtype, (kp, tno))
    m, k = a.shape
    tko, tmr = _tile(k, 1024, 128), _tile(m, 1024)
    a_spec = pl.BlockSpec((tmr, tko), lambda i, j, kk: (kk, i))
    if dc_stacked:
        s, _, n = dc.shape
        return _mm(name, a, dc, _TN, (k // tko, s, m // tmr), a_spec,
                   pl.BlockSpec((None, tmr, n), lambda i, j, kk: (j, kk, 0)),
                   pl.BlockSpec((None, tko, n), lambda i, j, kk: (j, i, 0)), (s, k, n), out_dtype, (tko, n))
    if dc_cols is not None:
        n = dc_cols
        s = dc.shape[1] // n
        g = _tile(s, max(1, 1024 // n), 1)
        return _mm(name, a, dc, _TN, (k // tko, s // g, m // tmr), a_spec,
                   pl.BlockSpec((tmr, g * n), lambda i, j, kk: (kk, j)),
                   pl.BlockSpec((g, tko, n), lambda i, j, kk: (j, i, 0)), (s, k, n), out_dtype, (g, tko, n),
                   group=g, n=n, b_sel="lanes", o_sel="lead")
    n = dc.shape[1]
    tno = _tile(n, 1024, 128)
    return _mm(name, a, dc, _TN, (k // tko, n // tno, m // tmr), a_spec,
               pl.BlockSpec((tmr, tno), lambda i, j, kk: (kk, j)),
               pl.BlockSpec((tko, tno), lambda i, j, kk: (i, j)), (k, n), out_dtype, (tko, tno))


def rms_fwd(name, x, g):
    t, d = x.shape
    tr = _tile(t, 512)

    def body(x_ref, g_ref, o_ref):
        xf = x_ref[...]
        r = lax.rsqrt(jnp.mean(xf * xf, axis=-1, keepdims=True) + EPS)
        o_ref[...] = (xf * r * g_ref[...]).astype(o_ref.dtype)

    return pl.pallas_call(
        body, name=name, grid=(t // tr,),
        in_specs=[pl.BlockSpec((tr, d), lambda i: (i, 0)), pl.BlockSpec((1, d), lambda i: (0, 0))],
        out_specs=pl.BlockSpec((tr, d), lambda i: (i, 0)),
        out_shape=jax.ShapeDtypeStruct((t, d), BF16), compiler_params=_params(("parallel",)),
    )(x, g.reshape(1, d))


def rms_bwd(name, x, g, dh, dres=None, need_dx=True):
    t, d = x.shape
    tr = _tile(t, 512)

    def body(*refs):
        refs = list(refs)
        x_ref, g_ref, dh_ref = refs[:3]
        r_ref = refs[3] if dres is not None else None
        outs = refs[4:] if dres is not None else refs[3:]
        dx_ref, dg_ref = (outs[0], outs[1]) if need_dx else (None, outs[0])
        i = pl.program_id(0)

        @pl.when(i == 0)
        def _():
            dg_ref[...] = jnp.zeros_like(dg_ref)

        xf = x_ref[...]
        dhf = dh_ref[...].astype(F32)
        r = lax.rsqrt(jnp.mean(xf * xf, axis=-1, keepdims=True) + EPS)
        xh = xf * r
        dg_ref[...] += jnp.sum(dhf * xh, axis=0, keepdims=True)
        if need_dx:
            dxh = dhf * g_ref[...]
            dx = r * (dxh - xh * jnp.mean(dxh * xh, axis=-1, keepdims=True))
            if r_ref is not None:
                dx = dx + r_ref[...]
            dx_ref[...] = dx

    row = pl.BlockSpec((tr, d), lambda i: (i, 0))
    vec = pl.BlockSpec((1, d), lambda i: (0, 0))
    in_specs = [row, vec, row] + ([row] if dres is not None else [])
    args = (x, g.reshape(1, d), dh) + ((dres,) if dres is not None else ())
    out_specs = ([row] if need_dx else []) + [vec]
    out_shape = ([jax.ShapeDtypeStruct((t, d), F32)] if need_dx else []) + [jax.ShapeDtypeStruct((1, d), F32)]
    res = pl.pallas_call(
        body, name=name, grid=(t // tr,), in_specs=in_specs, out_specs=out_specs, out_shape=out_shape,
        compiler_params=_params(("arbitrary",)),
    )(*args)
    return res if need_dx else (None, res[0])


def loss_head(x, g, tgt):
    t, d = x.shape
    tr = _tile(t, 512)

    def body(x_ref, g_ref, t_ref, l_ref, dx_ref, dg_ref):
        i = pl.program_id(0)

        @pl.when(i == 0)
        def _():
            l_ref[...] = jnp.zeros_like(l_ref)
            dg_ref[...] = jnp.zeros_like(dg_ref)

        xf = x_ref[...]
        r = lax.rsqrt(jnp.mean(xf * xf, axis=-1, keepdims=True) + EPS)
        xh = xf * r
        diff = xh * g_ref[...] - t_ref[...]
        l_ref[...] += 0.5 * jnp.sum(jnp.mean(diff * diff, axis=-1, keepdims=True))
        dy = diff * (1.0 / d)
        dg_ref[...] += jnp.sum(dy * xh, axis=0, keepdims=True)
        dxh = dy * g_ref[...]
        dx_ref[...] = r * (dxh - xh * jnp.mean(dxh * xh, axis=-1, keepdims=True))

    row = pl.BlockSpec((tr, d), lambda i: (i, 0))
    vec = pl.BlockSpec((1, d), lambda i: (0, 0))
    return pl.pallas_call(
        body, name="loss_head", grid=(t // tr,), in_specs=[row, vec, row],
        out_specs=[pl.BlockSpec((1, 128), lambda i: (0, 0)), row, vec],
        out_shape=[jax.ShapeDtypeStruct((1, 128), F32), jax.ShapeDtypeStruct((t, d), F32),
                   jax.ShapeDtypeStruct((1, d), F32)],
        compiler_params=_params(("arbitrary",)),
    )(x, g.reshape(1, d), tgt)


def glu_fwd(glu, x, gain):
    t, d = x.shape
    tr = _tile(t, 512)

    def body(v_ref, g_ref, x_ref, n_ref, o_ref, h_ref):
        y = x_ref[...] + v_ref[...] * jax.nn.sigmoid(g_ref[...])
        o_ref[...] = y
        r = lax.rsqrt(jnp.mean(y * y, axis=-1, keepdims=True) + EPS)
        h_ref[...] = (y * r * n_ref[...]).astype(h_ref.dtype)

    row = pl.BlockSpec((tr, d), lambda i: (i, 0))
    return pl.pallas_call(
        body, name="glu_fwd", grid=(t // tr,),
        in_specs=[row, pl.BlockSpec((tr, d), lambda i: (i, 1)), row, pl.BlockSpec((1, d), lambda i: (0, 0))],
        out_specs=[row, row],
        out_shape=[jax.ShapeDtypeStruct((t, d), F32), jax.ShapeDtypeStruct((t, d), BF16)],
        compiler_params=_params(("parallel",)),
    )(glu, glu, x, gain.reshape(1, d))


def glu_bwd(glu, dmix):
    t, d = dmix.shape
    tr = _tile(t, 512)

    def body(v_ref, g_ref, d_ref, o_ref):
        sg = jax.nn.sigmoid(g_ref[...])
        dm = d_ref[...]
        o_ref[:, :d] = (dm * sg).astype(o_ref.dtype)
        o_ref[:, d:] = (dm * v_ref[...] * sg * (1.0 - sg)).astype(o_ref.dtype)

    return pl.pallas_call(
        body, name="glu_bwd", grid=(t // tr,),
        in_specs=[pl.BlockSpec((tr, d), lambda i: (i, 0)), pl.BlockSpec((tr, d), lambda i: (i, 1)),
                  pl.BlockSpec((tr, d), lambda i: (i, 0))],
        out_specs=pl.BlockSpec((tr, 2 * d), lambda i: (i, 0)),
        out_shape=jax.ShapeDtypeStruct((t, 2 * d), BF16), compiler_params=_params(("parallel",)),
    )(glu, glu, dmix)


def _head_masks(shape):
    lane = lax.broadcasted_iota(jnp.int32, shape, 1)
    return lane < SB_HEAD_DIM


def _stack_heads(xf, is_a):
    return jnp.concatenate([jnp.where(is_a, xf, 0.0), jnp.where(is_a, 0.0, xf)], axis=0).astype(MXU_DTYPE)


def _diag_mask(qb, row0, rows):
    row = (lax.broadcasted_iota(jnp.int32, (rows, qb), 0) + row0) & (qb - 1)
    col = lax.broadcasted_iota(jnp.int32, (rows, qb), 1)
    return col < row


def _tri01(qb, pred):
    j = lax.broadcasted_iota(jnp.int32, (qb, qb), 0)
    s = lax.broadcasted_iota(jnp.int32, (qb, qb), 1)
    m = pred(j, s).astype(BF16)
    return jnp.concatenate([m, m], axis=0)


def _split_cat(x):
    hi = x.astype(BF16)
    lo = (x - hi.astype(F32)).astype(BF16)
    return jnp.concatenate([hi, lo], axis=1)


def sb_attn_fwd(proj, order, bsz, seq):
    qb = SB_BLOCK
    nq = seq // qb
    npair = SB_WIDTH // 128
    scale = SB_HEAD_DIM ** -0.5

    def body(q_ref, k_ref, v_ref, order_ref, o_ref, r_ref):
        qi = pl.program_id(2)
        is_a = _head_masks((qb, 128))
        q2 = _stack_heads(q_ref[...] * scale, is_a)
        diag = _diag_mask(qb, 0, 2 * qb)
        upper = _tri01(qb, lambda j, s: j > s)

        def block(kbi, acc, run, masked):
            ks = pl.ds(pl.multiple_of(kbi * qb, qb), qb)
            kblk = k_ref[ks, :].astype(MXU_DTYPE)
            vblk = v_ref[ks, :].astype(MXU_DTYPE)
            z = lax.dot_general(q2, kblk, _NT, preferred_element_type=F32)
            lk = -jnp.maximum(z, 0.0) - jnp.log(1.0 + jnp.exp(-jnp.abs(z)))
            lb = lk + z
            if masked:
                lk = jnp.where(diag, lk, 0.0)
            after = run + lax.dot_general(_split_cat(lk), upper, _NN, preferred_element_type=F32)
            w = jnp.exp(lb + after)
            if masked:
                w = jnp.where(diag, w, 0.0)
            acc = acc + lax.dot_general(w.astype(MXU_DTYPE), vblk, _NN, preferred_element_type=F32)
            return acc, run + jnp.sum(lk, axis=1, keepdims=True)

        carry = block(qi, jnp.zeros((2 * qb, 128), F32), jnp.zeros((2 * qb, 1), F32), True)
        acc, run = lax.fori_loop(0, qi, lambda i, c: block(qi - 1 - i, c[0], c[1], False), carry)
        o_ref[...] = jnp.where(is_a, acc[:qb], acc[qb:]).astype(o_ref.dtype)
        r_ref[...] = jnp.where(is_a, run[:qb], run[qb:])

    return pl.pallas_call(
        body, name="sb_attn_fwd", grid=(bsz, npair, nq),
        in_specs=[pl.BlockSpec((qb, 128), lambda b, p, i: (b * nq + i, p)),
                  pl.BlockSpec((seq, 128), lambda b, p, i: (b, npair + p)),
                  pl.BlockSpec((seq, 128), lambda b, p, i: (b, 2 * npair + p)),
                  pl.BlockSpec((1, 128), lambda b, p, i: (0, 0))],
        out_specs=[pl.BlockSpec((qb, 128), lambda b, p, i: (b * nq + i, p)),
                   pl.BlockSpec((qb, 128), lambda b, p, i: (b * nq + i, p))],
        out_shape=[jax.ShapeDtypeStruct((bsz * seq, SB_WIDTH), BF16),
                   jax.ShapeDtypeStruct((bsz * seq, SB_WIDTH), F32)],
        compiler_params=_params(("parallel", "parallel", "arbitrary")),
    )(proj, proj, proj, order)


def sb_attn_bwd(proj, rsum, dcat, bsz, seq):
    qb = SB_BLOCK
    nq = seq // qb
    npair = SB_WIDTH // 128
    scale = SB_HEAD_DIM ** -0.5

    def body(q_ref, k_ref, v_ref, r_ref, do_ref, dq_ref, dk_ref, dv_ref):
        qi = pl.program_id(2)

        @pl.when(qi == 0)
        def _():
            dk_ref[...] = jnp.zeros_like(dk_ref)
            dv_ref[...] = jnp.zeros_like(dv_ref)

        is_a = _head_masks((qb, 128))
        q2 = _stack_heads(q_ref[...] * scale, is_a)
        do2 = _stack_heads(do_ref[...].astype(F32), is_a)
        rf = r_ref[...]
        rtot = jnp.concatenate([rf[:, 0:1], rf[:, SB_HEAD_DIM:SB_HEAD_DIM + 1]], axis=0)
        diag = _diag_mask(qb, 0, 2 * qb)
        incl = _tri01(qb, lambda j, s: j <= s)
        strict = _tri01(qb, lambda j, s: j < s)

        def block(kbi, dq, pre, epre, masked):
            ks = pl.ds(pl.multiple_of(kbi * qb, qb), qb)
            kblk = k_ref[ks, :].astype(MXU_DTYPE)
            vblk = v_ref[ks, :].astype(MXU_DTYPE)
            z = lax.dot_general(q2, kblk, _NT, preferred_element_type=F32)
            lk = -jnp.maximum(z, 0.0) - jnp.log(1.0 + jnp.exp(-jnp.abs(z)))
            lb = lk + z
            if masked:
                lk = jnp.where(diag, lk, 0.0)
            after = rtot - (pre + lax.dot_general(_split_cat(lk), incl, _NN, preferred_element_type=F32))
            w = jnp.exp(lb + after)
            if masked:
                w = jnp.where(diag, w, 0.0)
            e = lax.dot_general(do2, vblk, _NT, preferred_element_type=F32) * w
            ecum = epre + lax.dot_general(_split_cat(e), strict, _NN, preferred_element_type=F32)
            dz = e - jnp.exp(lb) * (e + ecum)
            if masked:
                dz = jnp.where(diag, dz, 0.0)
            dz = dz.astype(MXU_DTYPE)
            dq = dq + lax.dot_general(dz, kblk, _NN, preferred_element_type=F32)
            dk_ref[ks, :] += lax.dot_general(dz, q2, _TN, preferred_element_type=F32)
            dv_ref[ks, :] += lax.dot_general(w.astype(MXU_DTYPE), do2, _TN, preferred_element_type=F32)
            return dq, pre + jnp.sum(lk, axis=1, keepdims=True), epre + jnp.sum(e, axis=1, keepdims=True)

        zc = jnp.zeros((2 * qb, 1), F32)
        carry = lax.fori_loop(0, qi, lambda kbi, c: block(kbi, c[0], c[1], c[2], False),
                              (jnp.zeros((2 * qb, 128), F32), zc, zc))
        dq = block(qi, carry[0], carry[1], carry[2], True)[0]
        dq_ref[...] = jnp.where(is_a, dq[:qb], dq[qb:]) * scale

    full = jax.ShapeDtypeStruct((bsz * seq, SB_WIDTH), F32)
    qspec = pl.BlockSpec((qb, 128), lambda b, p, i: (b * nq + i, p))
    return pl.pallas_call(
        body, name="sb_attn_bwd", grid=(bsz, npair, nq),
        in_specs=[qspec,
                  pl.BlockSpec((seq, 128), lambda b, p, i: (b, npair + p)),
                  pl.BlockSpec((seq, 128), lambda b, p, i: (b, 2 * npair + p)),
                  qspec, qspec],
        out_specs=[qspec, pl.BlockSpec((seq, 128), lambda b, p, i: (b, p)),
                   pl.BlockSpec((seq, 128), lambda b, p, i: (b, p))],
        out_shape=[full, full, full],
        compiler_params=_params(("parallel", "parallel", "arbitrary")),
    )(proj, proj, proj, rsum, dcat)


def _window_sums(x, forward):
    n = x.shape[0]
    out = []
    s = x
    for sh in (1, 2, 4, 8):
        s = s + pltpu.roll(s, (n - sh) if forward else sh, 0)
        out.append(s)
    return out


def _pool_counts(tc, c, w):
    t = lax.broadcasted_iota(jnp.int32, (tc, 1), 0) + c * tc
    return jnp.minimum(t + 1, w).astype(F32)


def pool_fwd(proj, pool_w, pool_scale, bsz, seq):
    tc = _tile(seq, 512)
    nc = seq // tc
    hb = tc // POOL_HALO
    ucol = 3

    def body(u_ref, prev_ref, w_ref, s_ref, o_ref):
        c = pl.program_id(1)
        prev = jnp.where(c > 0, prev_ref[...], 0.0)
        x = jnp.concatenate([prev, u_ref[...]], axis=0)
        sums = _window_sums(x, forward=False)
        for g, win in enumerate(POOL_WINDOWS):
            ls = slice(g * POOL_GROUP, (g + 1) * POOL_GROUP)
            pooled = sums[g][POOL_HALO:, ls] / _pool_counts(tc, c, win) - x[POOL_HALO:, ls]
            y = _dot(pooled, w_ref[g], _NN)
            o_ref[:, ls] = (y * s_ref[:, ls]).astype(o_ref.dtype)

    return pl.pallas_call(
        body, name="pool_fwd", grid=(bsz, nc),
        in_specs=[pl.BlockSpec((tc, SB_WIDTH), lambda b, c: (b * nc + c, ucol)),
                  pl.BlockSpec((POOL_HALO, SB_WIDTH), lambda b, c: (jnp.maximum((b * nc + c) * hb - 1, 0), ucol)),
                  pl.BlockSpec((4, POOL_GROUP, POOL_GROUP), lambda b, c: (0, 0, 0)),
                  pl.BlockSpec((1, SB_WIDTH), lambda b, c: (0, 0))],
        out_specs=pl.BlockSpec((tc, SB_WIDTH), lambda b, c: (b * nc + c, 0)),
        out_shape=jax.ShapeDtypeStruct((bsz * seq, SB_WIDTH), BF16),
        compiler_params=_params(("parallel", "parallel")),
    )(proj, proj, pool_w, pool_scale)


def pool_bwd(proj, pool_w, pool_scale, dcat, bsz, seq):
    tc = _tile(seq, 512)
    nc = seq // tc
    hb = tc // POOL_HALO
    nblk = bsz * seq // POOL_HALO
    ucol = 3

    def body(u_ref, prev_ref, dy_ref, nxt_ref, w_ref, s_ref, du_ref, dw_ref, ds_ref):
        b, c = pl.program_id(0), pl.program_id(1)

        @pl.when((b == 0) & (c == 0))
        def _():
            dw_ref[...] = jnp.zeros_like(dw_ref)
            ds_ref[...] = jnp.zeros_like(ds_ref)

        prev = jnp.where(c > 0, prev_ref[...], 0.0)
        x = jnp.concatenate([prev, u_ref[...]], axis=0)
        sums = _window_sums(x, forward=False)
        nxt = jnp.where(c < nc - 1, nxt_ref[...].astype(F32), 0.0)
        dy = jnp.concatenate([dy_ref[...].astype(F32), nxt], axis=0)
        tq = lax.broadcasted_iota(jnp.int32, (tc + POOL_HALO, 1), 0) + c * tc
        for g, win in enumerate(POOL_WINDOWS):
            ls = slice(g * POOL_GROUP, (g + 1) * POOL_GROUP)
            pooled = sums[g][POOL_HALO:, ls] / _pool_counts(tc, c, win) - x[POOL_HALO:, ls]
            y = _dot(pooled, w_ref[g], _NN)
            ds_ref[:, ls] += jnp.sum(dy[:tc, ls] * y, axis=0, keepdims=True)
            dz = dy[:, ls] * s_ref[:, ls]
            dw_ref[g] += _dot(pooled, dz[:tc], _TN)
            dpool = _dot(dz, w_ref[g], _NT)
            dmean = dpool / jnp.minimum(tq + 1, win).astype(F32)
            fsum = _window_sums(dmean, forward=True)[g]
            du_ref[:, ls] = fsum[:tc] - dpool[:tc]

    return pl.pallas_call(
        body, name="pool_bwd", grid=(bsz, nc),
        in_specs=[pl.BlockSpec((tc, SB_WIDTH), lambda b, c: (b * nc + c, ucol)),
                  pl.BlockSpec((POOL_HALO, SB_WIDTH), lambda b, c: (jnp.maximum((b * nc + c) * hb - 1, 0), ucol)),
                  pl.BlockSpec((tc, SB_WIDTH), lambda b, c: (b * nc + c, 1)),
                  pl.BlockSpec((POOL_HALO, SB_WIDTH), lambda b, c: (jnp.minimum((b * nc + c + 1) * hb, nblk - 1), 1)),
                  pl.BlockSpec((4, POOL_GROUP, POOL_GROUP), lambda b, c: (0, 0, 0)),
                  pl.BlockSpec((1, SB_WIDTH), lambda b, c: (0, 0))],
        out_specs=[pl.BlockSpec((tc, SB_WIDTH), lambda b, c: (b * nc + c, 0)),
                   pl.BlockSpec((4, POOL_GROUP, POOL_GROUP), lambda b, c: (0, 0, 0)),
                   pl.BlockSpec((1, SB_WIDTH), lambda b, c: (0, 0))],
        out_shape=[jax.ShapeDtypeStruct((bsz * seq, SB_WIDTH), F32),
                   jax.ShapeDtypeStruct((4, POOL_GROUP, POOL_GROUP), F32),
                   jax.ShapeDtypeStruct((1, SB_WIDTH), F32)],
        compiler_params=_params(("arbitrary", "arbitrary")),
    )(proj, proj, dcat, dcat, pool_w, pool_scale)


def _lbar(lam_re, lam_im, log_dt):
    dt = jnp.exp(log_dt)
    mag = jnp.exp(lam_re * dt)
    ang = lam_im * dt
    return mag * jnp.cos(ang), mag * jnp.sin(ang)


def _bbar(lam_re, lam_im, log_dt, b_re, b_im):
    lb_re, lb_im = _lbar(lam_re, lam_im, log_dt)
    n_re = lb_re - 1.0
    den = lam_re * lam_re + lam_im * lam_im
    coef_re = (n_re * lam_re + lb_im * lam_im) / den
    coef_im = (lb_im * lam_re - n_re * lam_im) / den
    return coef_re * b_re - coef_im * b_im, coef_re * b_im + coef_im * b_re


def _expand01():
    p = lax.broadcasted_iota(jnp.int32, (64, 1024), 0)
    q = lax.broadcasted_iota(jnp.int32, (64, 1024), 1)
    return (lax.shift_right_logical(q, 4) == p).astype(BF16)


def ssm_prep(lam_re, lam_im, log_dt, b_re2, b_im2):
    def body(lr_ref, li_ref, dt_ref, br_ref, bi_ref, ar_ref, ai_ref, bbr_ref, bbi_ref):
        e = _expand01()
        lr, li, dt = lr_ref[...], li_ref[...], dt_ref[...]
        ar_ref[...], ai_ref[...] = _lbar(lr, li, dt)
        bbr_ref[...], bbi_ref[...] = _bbar(_dot_exact01(lr, e), _dot_exact01(li, e), dt, br_ref[...], bi_ref[...])

    s64 = jax.ShapeDtypeStruct((64, 64), F32)
    s1k = jax.ShapeDtypeStruct((64, 1024), F32)
    return pl.pallas_call(body, name="ssm_prep", out_shape=[s64, s64, s1k, s1k], compiler_params=_params())(
        lam_re, lam_im, log_dt, b_re2, b_im2)


def ssm_prep_bwd(lam_re, lam_im, log_dt, b_re2, b_im2, da_re, da_im, dbb_re, dbb_im):
    def body(lr_ref, li_ref, dt_ref, br_ref, bi_ref, dar_ref, dai_ref, dbr_ref, dbi_ref,
             olr_ref, oli_ref, odt_ref, obr_ref, obi_ref):
        e = _expand01()
        lr, li, dt = lr_ref[...], li_ref[...], dt_ref[...]
        _, vjp_a = jax.vjp(_lbar, lr, li, dt)
        g_lr, g_li, g_dt = vjp_a((dar_ref[...], dai_ref[...]))
        _, vjp_b = jax.vjp(_bbar, _dot_exact01(lr, e), _dot_exact01(li, e), dt, br_ref[...], bi_ref[...])
        x_lr, x_li, x_dt, g_br, g_bi = vjp_b((dbr_ref[...], dbi_ref[...]))
        olr_ref[...] = g_lr + _dot_exact01(x_lr, e, _NT)
        oli_ref[...] = g_li + _dot_exact01(x_li, e, _NT)
        odt_ref[...] = g_dt + x_dt
        obr_ref[...] = g_br
        obi_ref[...] = g_bi

    s64 = jax.ShapeDtypeStruct((64, 64), F32)
    s1k = jax.ShapeDtypeStruct((64, 1024), F32)
    return pl.pallas_call(body, name="ssm_prep_bwd",
                          out_shape=[s64, s64, jax.ShapeDtypeStruct((64, 1), F32), s1k, s1k],
                          compiler_params=_params())(
        lam_re, lam_im, log_dt, b_re2, b_im2, da_re, da_im, dbb_re, dbb_im)


def _gelu(y):
    c = math.sqrt(2.0 / math.pi)
    return 0.5 * y * (1.0 + jnp.tanh(c * (y + 0.044715 * y * y * y)))


def _gelu_grad(y):
    c = math.sqrt(2.0 / math.pi)
    th = jnp.tanh(c * (y + 0.044715 * y * y * y))
    return 0.5 * (1.0 + th) + 0.5 * y * (1.0 - th * th) * c * (1.0 + 3.0 * 0.044715 * y * y)


def _cmul(ar, ai, br, bi):
    return ar * br - ai * bi, ar * bi + ai * br


def _scan_tables(ar, ai, reverse, tabs):
    row = lax.broadcasted_iota(jnp.int32, (8, SSM_STATES), 0)
    a1 = (ar, ai)
    a2 = _cmul(*a1, *a1)
    a4 = _cmul(*a2, *a2)
    powers = [a1, a2, _cmul(*a2, *a1), a4]
    powers += [_cmul(*a4, *p) for p in powers]
    for k, (val, sh) in enumerate(((a1, 1), (a2, 2), (a4, 4))):
        keep = (row < 8 - sh) if reverse else (row >= sh)
        tabs[2 * k][...] = jnp.where(keep, val[0], 0.0)
        tabs[2 * k + 1][...] = jnp.where(keep, val[1], 0.0)
    pr = jnp.zeros((8, SSM_STATES), F32)
    pi = jnp.zeros((8, SSM_STATES), F32)
    for r in range(8):
        val = powers[7 - r] if reverse else powers[r]
        pr = jnp.where(row == r, val[0], pr)
        pi = jnp.where(row == r, val[1], pi)
    tabs[6][...] = pr
    tabs[7][...] = pi


def _scan8(xr, xi, tabs, ls, cr, ci, reverse):
    for k, sh in enumerate((1, 2, 4)):
        amt = (8 - sh) if reverse else sh
        sr, si = pltpu.roll(xr, amt, 0), pltpu.roll(xi, amt, 0)
        lr, li = tabs[2 * k][:, ls], tabs[2 * k + 1][:, ls]
        xr, xi = xr + lr * sr - li * si, xi + lr * si + li * sr
    pr, pi = tabs[6][:, ls], tabs[7][:, ls]
    return xr + pr * cr - pi * ci, xi + pr * ci + pi * cr


def _block8(b):
    return pl.ds(pl.multiple_of(b * 8, 8), 8)


def ssm_fwd(u, wt, ct, a_re, a_im, dskip, bsz, seq):
    tc = _tile(seq, 256)
    nc = seq // tc
    ns = SSM_TILE_STATES
    nl = SSM_STATES // SSM_LANES

    def body(u_ref, wt_ref, ct_ref, ar_ref, ai_ref, d_ref, y_ref, gl_ref, hr_ref, hi_ref, sr_ref, si_ref, *tabs):
        b, c = pl.program_id(0), pl.program_id(1)

        @pl.when((b == 0) & (c == 0))
        def _():
            _scan_tables(ar_ref[...], ai_ref[...], False, tabs)

        @pl.when(c == 0)
        def _():
            sr_ref[...] = jnp.zeros_like(sr_ref)
            si_ref[...] = jnp.zeros_like(si_ref)

        uf = u_ref[...]
        for i in range(SSM_TILES):
            bu = _dot(uf[:, i * 128:(i + 1) * 128], wt_ref[i], _NN)
            hr_ref[:, i * ns:(i + 1) * ns] = bu[:, :ns]
            hi_ref[:, i * ns:(i + 1) * ns] = bu[:, ns:]

        def step(blk, carry):
            rows = _block8(blk)
            new = []
            for j in range(nl):
                ls = slice(j * SSM_LANES, (j + 1) * SSM_LANES)
                xr, xi = _scan8(hr_ref[rows, ls], hi_ref[rows, ls], tabs, ls, carry[2 * j], carry[2 * j + 1], False)
                hr_ref[rows, ls] = xr
                hi_ref[rows, ls] = xi
                new += [xr[7:8], xi[7:8]]
            return tuple(new)

        init = []
        for j in range(nl):
            ls = slice(j * SSM_LANES, (j + 1) * SSM_LANES)
            init += [sr_ref[:, ls], si_ref[:, ls]]
        last = lax.fori_loop(0, tc // 8, step, tuple(init), unroll=2)
        for j in range(nl):
            ls = slice(j * SSM_LANES, (j + 1) * SSM_LANES)
            sr_ref[:, ls] = last[2 * j]
            si_ref[:, ls] = last[2 * j + 1]
        for i in range(SSM_TILES):
            hcat = jnp.concatenate([hr_ref[:, i * ns:(i + 1) * ns], hi_ref[:, i * ns:(i + 1) * ns]], axis=1)
            ls = slice(i * 128, (i + 1) * 128)
            y = _dot(hcat, ct_ref[i], _NN) + d_ref[:, ls] * uf[:, ls]
            y_ref[:, ls] = y
            gl_ref[:, ls] = _gelu(y).astype(gl_ref.dtype)

    t = bsz * seq
    row = pl.BlockSpec((tc, D_MODEL), lambda b, c: (b * nc + c, 0))
    st = pl.BlockSpec((tc, SSM_STATES), lambda b, c: (b * nc + c, 0))
    diag = pl.BlockSpec((1, SSM_STATES), lambda b, c: (0, 0))
    return pl.pallas_call(
        body, name="ssm_fwd", grid=(bsz, nc),
        in_specs=[row, pl.BlockSpec((SSM_TILES, 128, 2 * ns), lambda b, c: (0, 0, 0)),
                  pl.BlockSpec((SSM_TILES, 2 * ns, 128), lambda b, c: (0, 0, 0)), diag, diag,
                  pl.BlockSpec((1, D_MODEL), lambda b, c: (0, 0))],
        out_specs=[row, row, st, st],
        out_shape=[jax.ShapeDtypeStruct((t, D_MODEL), F32), jax.ShapeDtypeStruct((t, D_MODEL), BF16),
                   jax.ShapeDtypeStruct((t, SSM_STATES), F32), jax.ShapeDtypeStruct((t, SSM_STATES), F32)],
        scratch_shapes=[pltpu.VMEM((1, SSM_STATES), F32)] * 2 + [pltpu.VMEM((8, SSM_STATES), F32)] * 8,
        compiler_params=_params(("arbitrary", "arbitrary")),
    )(u, wt, ct, a_re, a_im, dskip)


def ssm_bwd(dgl, y, u, h_re, h_im, wt, ct, a_re, a_im, dskip, bsz, seq):
    tc = _tile(seq, 256)
    nc = seq // tc
    nb = tc // 8
    ns = SSM_TILE_STATES
    nl = SSM_STATES // SSM_LANES

    def body(dgl_ref, y_ref, u_ref, hr_ref, hi_ref, pr_ref, pi_ref, wt_ref, ct_ref, ar_ref, ai_ref, d_ref,
             du_ref, dwt_ref, dct_ref, dd_ref, dar_ref, dai_ref, gr_ref, gi_ref, sr_ref, si_ref, ar8_ref, ai8_ref,
             *tabs):
        b, c = pl.program_id(0), pl.program_id(1)

        @pl.when((b == 0) & (c == 0))
        def _():
            for r in (dwt_ref, dct_ref, dd_ref, ar8_ref, ai8_ref):
                r[...] = jnp.zeros_like(r)
            _scan_tables(ar_ref[...], -ai_ref[...], True, tabs)

        @pl.when(c == 0)
        def _():
            sr_ref[...] = jnp.zeros_like(sr_ref)
            si_ref[...] = jnp.zeros_like(si_ref)

        uf = u_ref[...]
        dy = dgl_ref[...].astype(F32) * _gelu_grad(y_ref[...])
        dd_ref[...] += jnp.sum(dy * uf, axis=0, keepdims=True)
        for i in range(SSM_TILES):
            dyi = dy[:, i * 128:(i + 1) * 128]
            dh = _dot(dyi, ct_ref[i], _NT)
            gr_ref[:, i * ns:(i + 1) * ns] = dh[:, :ns]
            gi_ref[:, i * ns:(i + 1) * ns] = dh[:, ns:]
            hcat = jnp.concatenate([hr_ref[:, i * ns:(i + 1) * ns], hi_ref[:, i * ns:(i + 1) * ns]], axis=1)
            dct_ref[i] += _dot(hcat, dyi, _TN)
        row0 = lax.broadcasted_iota(jnp.int32, (8, SSM_LANES), 0) == 0

        def block(blk, carry, before):
            rows = _block8(blk)
            new = []
            for j in range(nl):
                ls = slice(j * SSM_LANES, (j + 1) * SSM_LANES)
                gr, gi = _scan8(gr_ref[rows, ls], gi_ref[rows, ls], tabs, ls, carry[2 * j], carry[2 * j + 1], True)
                gr_ref[rows, ls] = gr
                gi_ref[rows, ls] = gi
                bpr, bpi = before(j)
                hpr = jnp.where(row0, bpr, pltpu.roll(hr_ref[rows, ls], 1, 0))
                hpi = jnp.where(row0, bpi, pltpu.roll(hi_ref[rows, ls], 1, 0))
                ar8_ref[:, ls] += gr * hpr + gi * hpi
                ai8_ref[:, ls] += gi * hpr - gr * hpi
                new += [gr[0:1], gi[0:1]]
            return tuple(new)

        def step(jj, carry):
            blk = nb - 1 - jj
            prev_rows = _block8(blk - 1)

            def before(j):
                ls = slice(j * SSM_LANES, (j + 1) * SSM_LANES)
                return hr_ref[prev_rows, ls][7:8], hi_ref[prev_rows, ls][7:8]

            return block(blk, carry, before)

        init = []
        for j in range(nl):
            ls = slice(j * SSM_LANES, (j + 1) * SSM_LANES)
            init += [sr_ref[:, ls], si_ref[:, ls]]
        carry = lax.fori_loop(0, nb - 1, step, tuple(init))
        first = c == nc - 1

        def before_chunk(j):
            ls = slice(j * SSM_LANES, (j + 1) * SSM_LANES)
            return (jnp.where(first, 0.0, pr_ref[:, ls][7:8]), jnp.where(first, 0.0, pi_ref[:, ls][7:8]))

        last = block(0, carry, before_chunk)
        for j in range(nl):
            ls = slice(j * SSM_LANES, (j + 1) * SSM_LANES)
            sr_ref[:, ls] = last[2 * j]
            si_ref[:, ls] = last[2 * j + 1]
        for i in range(SSM_TILES):
            ls = slice(i * 128, (i + 1) * 128)
            gcat = jnp.concatenate([gr_ref[:, i * ns:(i + 1) * ns], gi_ref[:, i * ns:(i + 1) * ns]], axis=1)
            du_ref[:, ls] = (_dot(gcat, wt_ref[i], _NT) + d_ref[:, ls] * dy[:, ls]).astype(du_ref.dtype)
            dwt_ref[i] += _dot(uf[:, ls], gcat, _TN)

        @pl.when((b == bsz - 1) & (c == nc - 1))
        def _():
            dar_ref[...] = jnp.sum(ar8_ref[...], axis=0, keepdims=True)
            dai_ref[...] = jnp.sum(ai8_ref[...], axis=0, keepdims=True)

    t = bsz * seq
    rev = lambda b, c: (b * nc + (nc - 1 - c), 0)
    row = pl.BlockSpec((tc, D_MODEL), rev)
    st = pl.BlockSpec((tc, SSM_STATES), rev)
    prev = pl.BlockSpec((8, SSM_STATES), lambda b, c: (jnp.maximum((b * nc + (nc - 1 - c)) * nb - 1, 0), 0))
    diag = pl.BlockSpec((1, SSM_STATES), lambda b, c: (0, 0))
    wts = pl.BlockSpec((SSM_TILES, 128, 2 * ns), lambda b, c: (0, 0, 0))
    cts = pl.BlockSpec((SSM_TILES, 2 * ns, 128), lambda b, c: (0, 0, 0))
    vec = pl.BlockSpec((1, D_MODEL), lambda b, c: (0, 0))
    return pl.pallas_call(
        body, name="ssm_bwd", grid=(bsz, nc),
        in_specs=[row, row, row, st, st, prev, prev, wts, cts, diag, diag, vec],
        out_specs=[row, wts, cts, vec, diag, diag],
        out_shape=[jax.ShapeDtypeStruct((t, D_MODEL), BF16),
                   jax.ShapeDtypeStruct((SSM_TILES, 128, 2 * ns), F32),
                   jax.ShapeDtypeStruct((SSM_TILES, 2 * ns, 128), F32),
                   jax.ShapeDtypeStruct((1, D_MODEL), F32),
                   jax.ShapeDtypeStruct((1, SSM_STATES), F32), jax.ShapeDtypeStruct((1, SSM_STATES), F32)],
        scratch_shapes=[pltpu.VMEM((tc, SSM_STATES), F32)] * 2 + [pltpu.VMEM((1, SSM_STATES), F32)] * 2
                       + [pltpu.VMEM((8, SSM_STATES), F32)] * 10,
        compiler_params=_params(("arbitrary", "arbitrary")),
    )(dgl, y, u, h_re, h_im, h_re, h_im, wt, ct, a_re, a_im, dskip)


def _ssm_in_weights(bb_re2, bb_im2):
    eye = jnp.eye(8, dtype=F32)[None, :, None, :, None]

    def one(bb):
        t = bb.reshape(8, 8, 64, 16).transpose(0, 1, 3, 2)
        return (t[:, :, :, None, :] * eye).reshape(8, 128, 512)

    return jnp.concatenate([one(bb_re2), one(bb_im2)], axis=-1).astype(MXU_DTYPE)


def _ssm_in_weights_bwd(dwt):
    eye = jnp.eye(8, dtype=F32)[None, :, None, :, None]

    def one(d):
        t = (d.reshape(8, 8, 16, 8, 64) * eye).sum(axis=3)
        return t.transpose(0, 1, 3, 2).reshape(64, 1024)

    return one(dwt[..., :512]), one(dwt[..., 512:])


def _ssm_out_weights(c_re, c_im):
    eye = jnp.eye(8, dtype=F32)[None, :, None, :, None]

    def one(cc):
        t = cc.reshape(8, 8, 16, 64).transpose(0, 1, 3, 2)
        return (t[:, :, :, None, :] * eye).reshape(8, 512, 128)

    return jnp.concatenate([one(c_re), -one(c_im)], axis=1).astype(MXU_DTYPE)


def _ssm_out_weights_bwd(dct):
    eye = jnp.eye(8, dtype=F32)[None, :, None, :, None]

    def one(d):
        t = (d.reshape(8, 8, 64, 8, 16) * eye).sum(axis=3)
        return t.transpose(0, 1, 3, 2).reshape(64, 16, 64)

    return one(dct[:, :512]), -one(dct[:, 512:])


def _softmax(s):
    m = jnp.max(s, axis=-1, keepdims=True)
    e = jnp.exp(s - m)
    return e / jnp.sum(e, axis=-1, keepdims=True)


def xattn_fwd(q, kv, bsz, seq):
    tq = _tile(seq, 512)
    nq = seq // tq
    scale = XA_HEAD_DIM ** -0.5

    def body(q_ref, k_ref, v_ref, o_ref):
        s = lax.dot_general(q_ref[...], k_ref[...], _NT, preferred_element_type=F32) * scale
        p = _softmax(s)
        o_ref[...] = _dot(p, v_ref[...], _NN).astype(o_ref.dtype)

    qs = pl.BlockSpec((tq, XA_HEAD_DIM), lambda b, h, i: (b * nq + i, h))
    return pl.pallas_call(
        body, name="xattn_fwd", grid=(bsz, XA_HEADS, nq),
        in_specs=[qs, pl.BlockSpec((MEM_LEN, XA_HEAD_DIM), lambda b, h, i: (b, h)),
                  pl.BlockSpec((MEM_LEN, XA_HEAD_DIM), lambda b, h, i: (b, XA_HEADS + h))],
        out_specs=qs, out_shape=jax.ShapeDtypeStruct((bsz * seq, D_MODEL), BF16),
        compiler_params=_params(("parallel", "parallel", "parallel")),
    )(q, kv, kv)


def xattn_bwd(q, kv, do, bsz, seq):
    tq = _tile(seq, 512)
    nq = seq // tq
    scale = XA_HEAD_DIM ** -0.5

    def body(q_ref, k_ref, v_ref, do_ref, dq_ref, dk_ref, dv_ref):
        @pl.when(pl.program_id(2) == 0)
        def _():
            dk_ref[...] = jnp.zeros_like(dk_ref)
            dv_ref[...] = jnp.zeros_like(dv_ref)

        qv, kk, vv, dov = q_ref[...], k_ref[...], v_ref[...], do_ref[...]
        s = lax.dot_general(qv, kk, _NT, preferred_element_type=F32) * scale
        p = _softmax(s)
        dp = lax.dot_general(dov, vv, _NT, preferred_element_type=F32)
        ds = (p * (dp - jnp.sum(dp * p, axis=-1, keepdims=True)) * scale).astype(MXU_DTYPE)
        dq_ref[...] = lax.dot_general(ds, kk, _NN, preferred_element_type=F32).astype(dq_ref.dtype)
        dk_ref[...] += lax.dot_general(ds, qv, _TN, preferred_element_type=F32)
        dv_ref[...] += lax.dot_general(p.astype(MXU_DTYPE), dov, _TN, preferred_element_type=F32)

    qs = pl.BlockSpec((tq, XA_HEAD_DIM), lambda b, h, i: (b * nq + i, h))
    ks = pl.BlockSpec((MEM_LEN, XA_HEAD_DIM), lambda b, h, i: (b, h))
    vs = pl.BlockSpec((MEM_LEN, XA_HEAD_DIM), lambda b, h, i: (b, XA_HEADS + h))
    dkv = jax.ShapeDtypeStruct((bsz * MEM_LEN, D_MODEL), F32)
    dq, dk, dv = pl.pallas_call(
        body, name="xattn_bwd", grid=(bsz, XA_HEADS, nq),
        in_specs=[qs, ks, vs, qs], out_specs=[qs, ks, ks],
        out_shape=[jax.ShapeDtypeStruct((bsz * seq, D_MODEL), BF16), dkv, dkv],
        compiler_params=_params(("parallel", "parallel", "arbitrary")),
    )(q, kv, kv, do)
    return dq, dk, dv


CONV_HALO = 16


def _shifts_down(x, prev):
    h = prev.shape[0]
    ext = jnp.concatenate([prev, x], axis=0)
    return pltpu.roll(ext, 1, 0)[h:], pltpu.roll(ext, 2, 0)[h:]


def _shifts_up(x, nxt):
    rows = x.shape[0]
    n = rows + nxt.shape[0]
    ext = jnp.concatenate([x, nxt], axis=0)
    return pltpu.roll(ext, n - 1, 0)[:rows], pltpu.roll(ext, n - 2, 0)[:rows]


def _conv_taps(u, u1, u2, w, b):
    return b + w[2:3] * u + w[1:2] * u1 + w[0:1] * u2


def conv_fwd(up, cw, cb, bsz, seq):
    tc = _tile(seq, 512)
    nc = seq // tc
    hb = tc // CONV_HALO
    half = N_DEV // 2

    def body(uv_ref, ug_ref, pv_ref, pg_ref, wv_ref, wg_ref, bv_ref, bg_ref, o_ref):
        c = pl.program_id(2)
        pv = jnp.where(c > 0, pv_ref[...].astype(F32), 0.0)
        pg = jnp.where(c > 0, pg_ref[...].astype(F32), 0.0)
        uv, ug = uv_ref[...].astype(F32), ug_ref[...].astype(F32)
        val = _conv_taps(uv, *_shifts_down(uv, pv), wv_ref[...], bv_ref[...])
        gate = _conv_taps(ug, *_shifts_down(ug, pg), wg_ref[...], bg_ref[...])
        o_ref[...] = (gate * jax.nn.sigmoid(gate) * val).astype(o_ref.dtype)

    def cur(off):
        return pl.BlockSpec((None, tc, FF_SHARD), lambda b, j, c: (j + off, b * nc + c, 0))

    def prv(off):
        return pl.BlockSpec((None, CONV_HALO, FF_SHARD), lambda b, j, c: (j + off, jnp.maximum((b * nc + c) * hb - 1, 0), 0))

    def par(rows, off):
        return pl.BlockSpec((None, rows, FF_SHARD), lambda b, j, c: (j + off, 0, 0))

    return pl.pallas_call(
        body, name="conv_fwd", grid=(bsz, half, nc),
        in_specs=[cur(0), cur(half), prv(0), prv(half), par(3, 0), par(3, half), par(1, 0), par(1, half)],
        out_specs=cur(0), out_shape=jax.ShapeDtypeStruct((half, bsz * seq, FF_SHARD), BF16),
        compiler_params=_params(("parallel", "parallel", "parallel")),
    )(up, up, up, up, cw, cw, cb, cb)


def conv_bwd_taps(up, cw, cb, dact, bsz, seq):
    tc = _tile(seq, 512)
    nc = seq // tc
    hb = tc // CONV_HALO
    half = N_DEV // 2

    def body(uv_ref, ug_ref, pv_ref, pg_ref, wv_ref, wg_ref, bv_ref, bg_ref, da_ref,
             dc_ref, dwv_ref, dwg_ref, dbv_ref, dbg_ref):
        b, c = pl.program_id(1), pl.program_id(2)

        @pl.when((b == 0) & (c == 0))
        def _():
            for r in (dwv_ref, dwg_ref, dbv_ref, dbg_ref):
                r[...] = jnp.zeros_like(r)

        pv = jnp.where(c > 0, pv_ref[...].astype(F32), 0.0)
        pg = jnp.where(c > 0, pg_ref[...].astype(F32), 0.0)
        uv, ug = uv_ref[...].astype(F32), ug_ref[...].astype(F32)
        uv1, uv2 = _shifts_down(uv, pv)
        ug1, ug2 = _shifts_down(ug, pg)
        val = _conv_taps(uv, uv1, uv2, wv_ref[...], bv_ref[...])
        gate = _conv_taps(ug, ug1, ug2, wg_ref[...], bg_ref[...])
        sg = jax.nn.sigmoid(gate)
        da = da_ref[...].astype(F32)
        dsilu = da * sg
        dval = dsilu * gate
        dgate = dsilu * val * (1.0 + gate * (1.0 - sg))
        dc_ref[0] = dval.astype(dc_ref.dtype)
        dc_ref[1] = dgate.astype(dc_ref.dtype)
        for dcv, taps, dw_ref, db_ref in ((dval, (uv2, uv1, uv), dwv_ref, dbv_ref),
                                          (dgate, (ug2, ug1, ug), dwg_ref, dbg_ref)):
            db_ref[...] += jnp.sum(dcv, axis=0, keepdims=True)
            for k, u_k in enumerate(taps):
                dw_ref[k:k + 1, :] += jnp.sum(dcv * u_k, axis=0, keepdims=True)

    def cur(off):
        return pl.BlockSpec((None, tc, FF_SHARD), lambda j, b, c: (j + off, b * nc + c, 0))

    def prv(off):
        return pl.BlockSpec((None, CONV_HALO, FF_SHARD), lambda j, b, c: (j + off, jnp.maximum((b * nc + c) * hb - 1, 0), 0))

    def par(rows, off):
        return pl.BlockSpec((None, rows, FF_SHARD), lambda j, b, c: (j + off, 0, 0))

    t = bsz * seq
    hs = jax.ShapeDtypeStruct((2, half, t, FF_SHARD), BF16)
    ws = jax.ShapeDtypeStruct((half, 3, FF_SHARD), F32)
    bs = jax.ShapeDtypeStruct((half, 1, FF_SHARD), F32)
    dc, dwv, dwg, dbv, dbg = pl.pallas_call(
        body, name="conv_bwd_taps", grid=(half, bsz, nc),
        in_specs=[cur(0), cur(half), prv(0), prv(half), par(3, 0), par(3, half), par(1, 0), par(1, half), cur(0)],
        out_specs=[pl.BlockSpec((2, None, tc, FF_SHARD), lambda j, b, c: (0, j, b * nc + c, 0)),
                   par(3, 0), par(3, 0), par(1, 0), par(1, 0)],
        out_shape=[hs, ws, ws, bs, bs],
        compiler_params=_params(("parallel", "arbitrary", "arbitrary")),
    )(up, up, up, up, cw, cw, cb, cb, dact)
    return (dc.reshape(N_DEV, t, FF_SHARD), jnp.concatenate([dwv, dwg], axis=0),
            jnp.concatenate([dbv, dbg], axis=0))


def conv_bwd_input(dconv, cw, bsz, seq):
    tc = _tile(seq, 1024)
    nc = seq // tc
    hb = tc // CONV_HALO
    nblk = bsz * seq // CONV_HALO

    def body(d_ref, n_ref, w_ref, o_ref):
        c = pl.program_id(2)
        nxt = jnp.where(c < nc - 1, n_ref[...].astype(F32), 0.0)
        d = d_ref[...].astype(F32)
        d1, d2 = _shifts_up(d, nxt)
        w = w_ref[...]
        o_ref[...] = (w[2:3] * d + w[1:2] * d1 + w[0:1] * d2).astype(o_ref.dtype)

    cur = pl.BlockSpec((None, tc, FF_SHARD), lambda j, b, c: (j, b * nc + c, 0))
    return pl.pallas_call(
        body, name="conv_bwd_input", grid=(N_DEV, bsz, nc),
        in_specs=[cur, pl.BlockSpec((None, CONV_HALO, FF_SHARD),
                                    lambda j, b, c: (j, jnp.minimum((b * nc + c + 1) * hb, nblk - 1), 0)),
                  pl.BlockSpec((None, 3, FF_SHARD), lambda j, b, c: (j, 0, 0))],
        out_specs=cur, out_shape=jax.ShapeDtypeStruct(dconv.shape, BF16),
        compiler_params=_params(("parallel", "parallel", "parallel")),
    )(dconv, dconv, cw)


def _my_index():
    return 4 * lax.axis_index("x") + 2 * lax.axis_index("y") + lax.axis_index("c")


def _peer(k):
    return (lax.axis_index("x") ^ ((k >> 2) & 1), lax.axis_index("y") ^ ((k >> 1) & 1),
            lax.axis_index("c") ^ (k & 1))


def all_gather(name, a, out_dtype):
    def body(a_ref, o_ref, stage, send_sems, recv_sems, local_sem):
        me = _my_index()
        stage[...] = a_ref[...].astype(out_dtype)
        local = pltpu.make_async_copy(stage, o_ref.at[me], local_sem)
        local.start()
        sends = []
        for k in range(1, N_DEV):
            cp = pltpu.make_async_remote_copy(
                src_ref=stage, dst_ref=o_ref.at[me], send_sem=send_sems.at[k - 1], recv_sem=recv_sems.at[k - 1],
                device_id=_peer(k), device_id_type=pl.DeviceIdType.MESH)
            cp.start()
            sends.append(cp)
        for k in range(1, N_DEV):
            pltpu.make_async_remote_copy(
                src_ref=stage, dst_ref=o_ref.at[me ^ k], send_sem=send_sems.at[k - 1], recv_sem=recv_sems.at[k - 1],
                device_id=_peer(k), device_id_type=pl.DeviceIdType.MESH).wait_recv()
        for cp in sends:
            cp.wait_send()
        local.wait()

    return pl.pallas_call(
        body, name=name, in_specs=[pl.BlockSpec(memory_space=pltpu.VMEM)],
        out_specs=pl.BlockSpec(memory_space=pltpu.HBM),
        out_shape=jax.ShapeDtypeStruct((N_DEV,) + a.shape, out_dtype),
        scratch_shapes=[pltpu.VMEM(a.shape, out_dtype), pltpu.SemaphoreType.DMA((N_DEV - 1,)),
                        pltpu.SemaphoreType.DMA((N_DEV - 1,)), pltpu.SemaphoreType.DMA],
        compiler_params=pltpu.CompilerParams(vmem_limit_bytes=VMEM_LIMIT),
    )(a)


def exchange(name, g):
    def body(g_ref, r_ref, send_sems, recv_sems, local_sem):
        me = _my_index()
        local = pltpu.make_async_copy(g_ref.at[me], r_ref.at[me], local_sem)
        local.start()
        sends = []
        for k in range(1, N_DEV):
            cp = pltpu.make_async_remote_copy(
                src_ref=g_ref.at[me ^ k], dst_ref=r_ref.at[me], send_sem=send_sems.at[k - 1],
                recv_sem=recv_sems.at[k - 1], device_id=_peer(k), device_id_type=pl.DeviceIdType.MESH)
            cp.start()
            sends.append(cp)
        for k in range(1, N_DEV):
            pltpu.make_async_remote_copy(
                src_ref=g_ref.at[me], dst_ref=r_ref.at[me ^ k], send_sem=send_sems.at[k - 1],
                recv_sem=recv_sems.at[k - 1], device_id=_peer(k), device_id_type=pl.DeviceIdType.MESH).wait_recv()
        for cp in sends:
            cp.wait_send()
        local.wait()

    return pl.pallas_call(
        body, name=name, in_specs=[pl.BlockSpec(memory_space=pltpu.HBM)],
        out_specs=pl.BlockSpec(memory_space=pltpu.HBM),
        out_shape=jax.ShapeDtypeStruct(g.shape, g.dtype),
        scratch_shapes=[pltpu.SemaphoreType.DMA((N_DEV - 1,)), pltpu.SemaphoreType.DMA((N_DEV - 1,)),
                        pltpu.SemaphoreType.DMA],
    )(g)


_HBM = pl.BlockSpec(memory_space=pltpu.HBM)
_SEM = pl.BlockSpec(memory_space=pltpu.SEMAPHORE)
_DATAFLOW = pltpu.SideEffectType.DATAFLOW_SIDE_EFFECTING


def _split_copies(gather, src_ref, land_ref, send_sems, recv_sems, local_sem):
    me = _my_index()

    def part(j):
        return src_ref if gather else src_ref.at[j]

    local = pltpu.make_async_copy(part(me), land_ref.at[me], local_sem)
    sends = [pltpu.make_async_remote_copy(
        src_ref=part(me ^ k), dst_ref=land_ref.at[me], send_sem=send_sems.at[k - 1], recv_sem=recv_sems.at[k - 1],
        device_id=_peer(k), device_id_type=pl.DeviceIdType.MESH) for k in range(1, N_DEV)]
    recvs = [pltpu.make_async_remote_copy(
        src_ref=part(me ^ k), dst_ref=land_ref.at[me ^ k], send_sem=send_sems.at[k - 1], recv_sem=recv_sems.at[k - 1],
        device_id=_peer(k), device_id_type=pl.DeviceIdType.MESH) for k in range(1, N_DEV)]
    return local, sends, recvs


def split_start(name, src, gather):
    land_shape = ((N_DEV,) + src.shape) if gather else src.shape

    def body(src_ref, land_ref, send_sems, recv_sems, local_sem, src_thru, land_thru, token):
        local, sends, _ = _split_copies(gather, src_ref, land_ref, send_sems, recv_sems, local_sem)
        local.start()
        for cp in sends:
            cp.start()
        token[...] = jnp.zeros_like(token)

    dma7 = pltpu.SemaphoreType.DMA((N_DEV - 1,))
    out = pl.pallas_call(
        body, name=name,
        out_shape=(dma7, dma7, pltpu.SemaphoreType.DMA(()), pltpu.HBM(src.shape, src.dtype),
                   pltpu.HBM(land_shape, src.dtype), jax.ShapeDtypeStruct((8, 128), F32)),
        in_specs=(_HBM, _HBM), out_specs=(_SEM, _SEM, _SEM, _HBM, _HBM, pl.BlockSpec(memory_space=pltpu.VMEM)),
        input_output_aliases={0: 3, 1: 4},
        compiler_params=pltpu.CompilerParams(has_side_effects=_DATAFLOW),
    )(pltpu.with_memory_space_constraint(src, pltpu.HBM),
      pltpu.with_memory_space_constraint(lax.empty(land_shape, src.dtype), pltpu.HBM))
    return out[:5], out[5][0, 0]


def split_wait(name, handles, after, gather):
    send_sems, recv_sems, local_sem, src_thru, land_thru = handles

    def body(src_ref, land_ref, send_sems, recv_sems, local_sem, after_ref, src_dead, got_ref, token):
        local, sends, recvs = _split_copies(gather, src_ref, land_ref, send_sems, recv_sems, local_sem)
        local.wait()
        for cp in recvs:
            cp.wait_send()
            cp.wait_recv()
        token[...] = jnp.zeros_like(token)

    out = pl.pallas_call(
        body, name=name,
        out_shape=(pltpu.HBM(src_thru.shape, src_thru.dtype), pltpu.HBM(land_thru.shape, land_thru.dtype),
                   jax.ShapeDtypeStruct((8, 128), F32)),
        in_specs=(_HBM, _HBM, _SEM, _SEM, _SEM, pl.BlockSpec(memory_space=pl.ANY)),
        out_specs=(_HBM, _HBM, pl.BlockSpec(memory_space=pltpu.VMEM)),
        input_output_aliases={0: 0, 1: 1},
        compiler_params=pltpu.CompilerParams(has_side_effects=_DATAFLOW),
    )(src_thru, land_thru, send_sems, recv_sems, local_sem, after)
    return out[1], out[2][0, 0]


def sum_parts(name, r):
    _, rows, cols = r.shape

    def body(r_ref, o_ref):
        acc = r_ref[0].astype(F32)
        for s in range(1, N_DEV):
            acc = acc + r_ref[s].astype(F32)
        o_ref[...] = acc

    return pl.pallas_call(body, name=name, out_shape=jax.ShapeDtypeStruct((rows, cols), F32),
                          compiler_params=_params())(r)


def adamw(name, w, m, v, parts=None, g=None, layer=0, into=None):
    _, rows, cols = w.shape
    br = _tile(rows, 256, 16)
    c1 = 1.0 / (1.0 - ADAM_B1 ** ADAM_STEP)
    c2 = 1.0 / (1.0 - ADAM_B2 ** ADAM_STEP)

    def body(g_ref, w_ref, m_ref, v_ref, *rest):
        og_ref, od_ref, om_ref, ov_ref = rest[-4:]
        if parts is None:
            gs = g_ref[...]
        else:
            gs = g_ref[0].astype(F32)
            for s in range(1, N_DEV):
                gs = gs + g_ref[s].astype(F32)
        mn = ADAM_B1 * m_ref[...] + (1.0 - ADAM_B1) * gs
        vn = ADAM_B2 * v_ref[...] + (1.0 - ADAM_B2) * (gs * gs)
        og_ref[...] = gs
        om_ref[...] = mn
        ov_ref[...] = vn
        od_ref[...] = -ADAM_LR * ((mn * c1) / (jnp.sqrt(vn * c2) + ADAM_EPS) + ADAM_WD * w_ref[...])

    blk = pl.BlockSpec((None, br, cols), lambda i: (layer, i, 0))
    if parts is None:
        gspec = pl.BlockSpec((br, cols), lambda i: (i, 0))
    else:
        gspec = pl.BlockSpec((N_DEV, br, cols), lambda i: (0, i, 0))
    earlier = [] if into is None else list(into)
    return pl.pallas_call(
        body, name=name, grid=(rows // br,),
        in_specs=[gspec, blk, blk, blk] + [pl.BlockSpec(memory_space=pl.ANY)] * len(earlier),
        out_specs=[blk] * 4, out_shape=[jax.ShapeDtypeStruct(w.shape, F32)] * 4,
        input_output_aliases={4 + k: k for k in range(len(earlier))},
        compiler_params=_params(("parallel",)),
    )(g if parts is None else parts, w, m, v, *earlier)


SMALL = ("norm_mix", "norm_xattn", "norm_ffn", "norm_mem", "norm_final", "pool_w", "pool_scale",
         "ssm_lam_re", "ssm_lam_im", "ssm_log_dt", "ssm_b_re", "ssm_b_im", "ssm_c_re", "ssm_c_im",
         "ffn_conv_b", "ssm_d", "ffn_conv_w")
SMALL_SHARDED = {"ssm_d": 1, "ffn_conv_w": 2}
BIG = ("ab_w_in", "ab_w_out", "ssm_w_in", "ssm_w_glu", "xa_w_q", "xa_w_kv", "xa_w_o", "ffn_w_up", "ffn_w_down")
WEIGHTS = ("norm_mix", "norm_xattn", "norm_ffn", "norm_mem", "norm_final", "ab_w_in", "pool_w", "pool_scale",
           "ab_w_out", "ssm_w_in", "ssm_lam_re", "ssm_lam_im", "ssm_log_dt", "ssm_b_re", "ssm_b_im", "ssm_c_re",
           "ssm_c_im", "ssm_d", "ssm_w_glu", "xa_w_q", "xa_w_kv", "xa_w_o", "ffn_w_up", "ffn_conv_w", "ffn_conv_b",
           "ffn_w_down")


def _rows8(g):
    return g.reshape(N_DEV, g.size // (N_DEV * D_MODEL), D_MODEL)


def _square(a):
    return a.reshape(D_MODEL, D_MODEL)


_LAYOUT = {"ab_w_out": _square, "ssm_w_in": _square, "xa_w_q": _square, "xa_w_o": _square,
           "ffn_w_down": lambda a: a.reshape(N_DEV // 2, FF_SHARD, D_MODEL)}
GATHER_ORDER = (("ab_w_in", 0), ("ab_w_out", 0), ("xa_w_q", 0), ("xa_w_kv", 0), ("xa_w_o", 0), ("ffn_w_up", 0),
                ("ffn_w_down", 0), ("ssm_w_in", 0), ("ssm_w_glu", 0), ("xa_w_q", 1), ("xa_w_kv", 1),
                ("xa_w_o", 1), ("ffn_w_up", 1), ("ffn_w_down", 1))
GATHER_AHEAD = 5


class _Step:
    def __init__(self, master, small):
        self.master, self.small = master, small
        self.pending, self.gathers, self.weights, self.sent = [], {}, {}, []

    def follow(self, v):
        for z in self.pending:
            v = v + z
        self.pending = []
        return v

    def start_gathers(self, upto, zero):
        for n, l in GATHER_ORDER[len(self.gathers):upto]:
            shard = (self.master[n][l] + zero).astype(MXU_DTYPE)
            self.gathers[(n, l)], z = split_start(f"ags_{n}{l}", shard, gather=True)
            self.pending.append(z)

    def weight(self, n, l, after):
        if (n, l) not in self.weights:
            full, z = split_wait(f"agw_{n}{l}", self.gathers[(n, l)], after, gather=True)
            self.weights[(n, l)] = _LAYOUT.get(n, lambda a: a)(full)
            self.start_gathers(GATHER_ORDER.index((n, l)) + 1 + GATHER_AHEAD, z)
        return self.weights[(n, l)]

    def send_grad(self, n, l, part):
        h, z = split_start(f"xs_{n}{l}", part, gather=False)
        self.pending.append(z)
        self.sent.append((n, l, h))


def _layer_tail(st, l, x_in, hq, mem_n, acts, next_gain=None):
    bsz, seq = acts["bsz"], acts["seq"]
    p = st.small
    q = mm_nn(f"xa_q{l}", hq, st.weight("xa_w_q", l, x_in))
    kv = mm_nn_bs(f"xa_kv{l}", mem_n, st.weight("xa_w_kv", l, x_in))
    o = xattn_fwd(q, kv, bsz, seq)
    x_mid, hf = mm_nn(f"xa_o{l}", o, st.weight("xa_w_o", l, o), res=x_in, out_dtype=F32,
                      norm_gain=st.follow(p["norm_ffn"][l]))
    up = mm_nn_bs(f"ffn_up{l}", hf, st.weight("ffn_w_up", l, x_mid), stacked_out=True)
    act = conv_fwd(up, p["ffn_conv_w"][l], p["ffn_conv_b"][l], bsz, seq)
    w_down = st.weight("ffn_w_down", l, act)
    if next_gain is None:
        x_out, h_next = mm_as_nn(f"ffn_down{l}", act, w_down, res=x_mid), None
    else:
        x_out, h_next = mm_as_nn(f"ffn_down{l}", act, w_down, res=x_mid, norm_gain=st.follow(next_gain))
    acts[l].update(x_in=x_in, hq=hq, q=q, kv=kv, o=o, x_mid=x_mid, hf=hf, up=up, act=act)
    return x_out, h_next


def _layer_tail_bwd(st, l, dx, mem_n, acts, grads):
    a = acts[l]
    bsz, seq = acts["bsz"], acts["seq"]
    p = st.small
    dact = mm_nt_os(f"d_act{l}", dx, st.weight("ffn_w_down", l, dx))
    st.send_grad("ffn_w_down", l, _rows8(mm_tn(f"g_ffn_down{l}", a["act"], dx, a_stacked=True)))
    dconv, dcw, dcb = conv_bwd_taps(a["up"], p["ffn_conv_w"][l], p["ffn_conv_b"][l], dact, bsz, seq)
    grads["ffn_conv_w"][l] = dcw
    grads["ffn_conv_b"][l] = dcb
    dup = conv_bwd_input(dconv, p["ffn_conv_w"][l], bsz, seq)
    dx_mid, grads["norm_ffn"][l] = mm_nt_bs(f"d_hf{l}", dup, st.weight("ffn_w_up", l, dx), dc_stacked=True,
                                            rms=(a["x_mid"], st.follow(p["norm_ffn"][l]), dx))
    st.send_grad("ffn_w_up", l, mm_tn(f"g_ffn_up{l}", a["hf"], dup, dc_stacked=True))
    do = mm_nt(f"d_o{l}", dx_mid, st.weight("xa_w_o", l, dx))
    st.send_grad("xa_w_o", l, _rows8(mm_tn(f"g_xa_o{l}", a["o"], dx_mid)))
    dq, dk, dv = xattn_bwd(a["q"], a["kv"], do, bsz, seq)
    dkv = jnp.concatenate([dk, dv], axis=1).astype(BF16)
    dx_in, grads["norm_xattn"][l] = mm_nt(f"d_hq{l}", dq, st.weight("xa_w_q", l, dx),
                                          rms=(a["x_in"], st.follow(p["norm_xattn"][l]), dx_mid))
    st.send_grad("xa_w_q", l, _rows8(mm_tn(f"g_xa_q{l}", a["hq"], dq)))
    dmem_n = mm_nt_bs(f"d_memn{l}", dkv, st.weight("xa_w_kv", l, dx), out_dtype=F32)
    st.send_grad("xa_w_kv", l, mm_tn(f"g_xa_kv{l}", mem_n, dkv, dc_cols=2 * D_MODEL // N_DEV))
    return dx_in, dmem_n


def kernel(x, mem, norm_mix, norm_xattn, norm_ffn, norm_mem, norm_final, ab_w_in, pool_w, pool_scale, ab_w_out, ssm_w_in, ssm_lam_re, ssm_lam_im, ssm_log_dt, ssm_b_re, ssm_b_im, ssm_c_re, ssm_c_im, ssm_d, ssm_w_glu, xa_w_q, xa_w_kv, xa_w_o, ffn_w_up, ffn_conv_w, ffn_conv_b, ffn_w_down, loss_target, m_norm_mix, m_norm_xattn, m_norm_ffn, m_norm_mem, m_norm_final, m_ab_w_in, m_pool_w, m_pool_scale, m_ab_w_out, m_ssm_w_in, m_ssm_lam_re, m_ssm_lam_im, m_ssm_log_dt, m_ssm_b_re, m_ssm_b_im, m_ssm_c_re, m_ssm_c_im, m_ssm_d, m_ssm_w_glu, m_xa_w_q, m_xa_w_kv, m_xa_w_o, m_ffn_w_up, m_ffn_conv_w, m_ffn_conv_b, m_ffn_w_down, v_norm_mix, v_norm_xattn, v_norm_ffn, v_norm_mem, v_norm_final, v_ab_w_in, v_pool_w, v_pool_scale, v_ab_w_out, v_ssm_w_in, v_ssm_lam_re, v_ssm_lam_im, v_ssm_log_dt, v_ssm_b_re, v_ssm_b_im, v_ssm_c_re, v_ssm_c_im, v_ssm_d, v_ssm_w_glu, v_xa_w_q, v_xa_w_kv, v_xa_w_o, v_ffn_w_up, v_ffn_conv_w, v_ffn_conv_b, v_ffn_w_down):
    given = dict(locals())
    master = {n: given[n] for n in WEIGHTS}
    mom1 = {n: given["m_" + n] for n in WEIGHTS}
    mom2 = {n: given["v_" + n] for n in WEIGHTS}
    bsz, seq, d = x.shape
    t = bsz * seq
    me = _my_index()

    st = _Step(master, {"norm_xattn": norm_xattn, "norm_ffn": norm_ffn,
                        "ffn_conv_b": [ffn_conv_b[l].reshape(N_DEV, 1, FF_SHARD) for l in range(2)]})
    st.start_gathers(1, 0.0)
    zero = st.follow(jnp.zeros((), F32))
    conv_w_st = all_gather("ag_ffn_conv_w", ffn_conv_w + zero, F32)
    st.small["ffn_conv_w"] = [conv_w_st[:, l] for l in range(2)]
    dskip = all_gather("ag_ssm_d", ssm_d.reshape(1, 128) + zero, F32).reshape(1, D_MODEL)

    acts = {"bsz": bsz, "seq": seq, 0: {}, 1: {}}
    x0 = x.reshape(t, d)
    mem2 = mem.reshape(bsz * MEM_LEN, d)
    mem_n = rms_fwd("rms_mem", mem2, norm_mem + zero)
    pscale = pool_scale.reshape(1, SB_WIDTH)

    h0 = rms_fwd("rms_mix0", x0, norm_mix[0] + zero)
    w_in = st.weight("ab_w_in", 0, h0)
    proj = mm_nn_bs("ab_in", h0, w_in, out_dtype=F32)
    a_out, rsum = sb_attn_fwd(proj, st.follow(jnp.zeros((1, 128), F32)), bsz, seq)
    p_out = pool_fwd(proj, pool_w[0], pscale, bsz, seq)
    w_out = st.weight("ab_w_out", 0, a_out)
    x1 = mm_nn("ab_out_a", a_out, w_out, res=x0, out_dtype=F32)
    x1, hq0 = mm_nn("ab_out_p", p_out, w_out, res=x1, koff=SB_WIDTH, out_dtype=F32,
                    norm_gain=st.follow(norm_xattn[0]))
    x3, h1 = _layer_tail(st, 0, x1, hq0, mem_n, acts, next_gain=norm_mix[1])

    b_re2 = ssm_b_re.reshape(64, 1024)
    b_im2 = ssm_b_im.reshape(64, 1024)
    log_dt = ssm_log_dt.reshape(64, 1)
    lb_re, lb_im, bb_re2, bb_im2 = ssm_prep(ssm_lam_re[0], ssm_lam_im[0], log_dt, b_re2, b_im2)
    wt = _ssm_in_weights(bb_re2, bb_im2)
    ct = _ssm_out_weights(ssm_c_re[0], ssm_c_im[0])
    a_re = lb_re.reshape(1, SSM_STATES)
    a_im = lb_im.reshape(1, SSM_STATES)
    u = mm_nn("ssm_in", h1, st.weight("ssm_w_in", 0, x3), out_dtype=F32)
    y, gl, h_re, h_im = ssm_fwd(u, wt, ct, a_re, a_im, dskip, bsz, seq)
    glu = mm_nn_bs("ssm_glu", gl, st.weight("ssm_w_glu", 0, gl), out_dtype=F32)
    x4, hq1 = glu_fwd(glu, x3, st.follow(norm_xattn[1]))
    x6, _ = _layer_tail(st, 1, x4, hq1, mem_n, acts)

    loss_row, dx, g_norm_final = loss_head(x6, norm_final, loss_target.reshape(t, d))
    loss = lax.psum(loss_row[0, 0], MESH_AXES)

    grads = {n: [None, None] for n in ("ffn_conv_w", "ffn_conv_b", "norm_ffn", "norm_xattn", "norm_mix")}
    dx4, dmem_1 = _layer_tail_bwd(st, 1, dx, mem_n, acts, grads)
    dglu = glu_bwd(glu, dx4)
    dgl = mm_nt_bs("d_gl", dglu, st.weight("ssm_w_glu", 0, dx))
    st.send_grad("ssm_w_glu", 0, mm_tn("g_ssm_glu", gl, dglu, dc_cols=2 * D_MODEL // N_DEV))
    du, dwt, dct, g_dskip, da_re, da_im = ssm_bwd(dgl, y, u, h_re, h_im, wt, ct, a_re, a_im, dskip, bsz, seq)
    dbb_re, dbb_im = _ssm_in_weights_bwd(dwt)
    g_c_re, g_c_im = _ssm_out_weights_bwd(dct)
    g_lam_re, g_lam_im, g_log_dt, g_b_re, g_b_im = ssm_prep_bwd(
        ssm_lam_re[0], ssm_lam_im[0], log_dt, b_re2, b_im2, da_re.reshape(64, 64), da_im.reshape(64, 64),
        dbb_re, dbb_im)
    dx3, grads["norm_mix"][1] = mm_nt("d_h1", du, st.weight("ssm_w_in", 0, dx),
                                      rms=(x3, st.follow(norm_mix[1]), dx4))
    st.send_grad("ssm_w_in", 0, _rows8(mm_tn("g_ssm_in", h1, du)))

    dx1, dmem_0 = _layer_tail_bwd(st, 0, dx3, mem_n, acts, grads)
    dcat = mm_nt("d_cat", dx1, st.weight("ab_w_out", 0, dx))
    st.send_grad("ab_w_out", 0, _rows8(jnp.concatenate(
        [mm_tn("g_ab_out_a", a_out, dx1), mm_tn("g_ab_out_p", p_out, dx1)], axis=0)))
    dq, dk, dv = sb_attn_bwd(proj, rsum, dcat, bsz, seq)
    dpu, g_pool_w, g_pool_scale = pool_bwd(proj, pool_w[0], st.follow(pscale), dcat, bsz, seq)
    dproj = jnp.concatenate([dq, dk, dv, dpu], axis=1).astype(BF16)
    st.send_grad("ab_w_in", 0, mm_tn("g_ab_in", h0, dproj, dc_cols=2 * D_MODEL // N_DEV))
    dx0, grads["norm_mix"][0] = mm_nt_bs("d_h0", dproj, st.weight("ab_w_in", 0, dx),
                                         rms=(x0, st.follow(norm_mix[0]), dx1))
    _, g_norm_mem = rms_bwd("rms_mem_bwd", mem2, norm_mem, dmem_0 + dmem_1, need_dx=False)

    stepped = {}
    for n, l, handles in st.sent:
        recv, _ = split_wait(f"xw_{n}{l}", handles, dx0, gather=False)
        shape3 = (master[n].shape[0],) + recv.shape[1:]
        stepped[n] = adamw(f"adamw_{n}{l}", master[n].reshape(shape3), mom1[n].reshape(shape3),
                           mom2[n].reshape(shape3), parts=recv, layer=l, into=stepped.get(n))
    out_g, out_d, out_m, out_v = ({n: stepped[n][k].reshape(master[n].shape) for n in BIG} for k in range(4))

    small_g = {
        "norm_mix": jnp.stack([g[0] for g in grads["norm_mix"]]),
        "norm_xattn": jnp.stack([g[0] for g in grads["norm_xattn"]]),
        "norm_ffn": jnp.stack([g[0] for g in grads["norm_ffn"]]),
        "norm_mem": g_norm_mem[0], "norm_final": g_norm_final[0],
        "pool_w": g_pool_w[None], "pool_scale": g_pool_scale,
        "ssm_lam_re": g_lam_re[None], "ssm_lam_im": g_lam_im[None], "ssm_log_dt": g_log_dt.reshape(1, 64),
        "ssm_b_re": g_b_re.reshape(1, 64, 64, 16), "ssm_b_im": g_b_im.reshape(1, 64, 64, 16),
        "ssm_c_re": g_c_re[None], "ssm_c_im": g_c_im[None],
        "ffn_conv_b": jnp.stack([g.reshape(2 * D_FF) for g in grads["ffn_conv_b"]]),
        "ssm_d": g_dskip,
        "ffn_conv_w": jnp.stack([g.transpose(1, 0, 2).reshape(3, 2 * D_FF) for g in grads["ffn_conv_w"]]),
    }
    sizes = [int(small_g[n].size) for n in SMALL]
    total = sum(sizes)
    rows8 = -(-total // (N_DEV * 128 * 8)) * 8
    flat = jnp.concatenate([small_g[n].reshape(-1).astype(F32) for n in SMALL]
                           + [jnp.zeros((N_DEV * rows8 * 128 - total,), F32)])
    recv = exchange("xch_small", flat.reshape(N_DEV, rows8, 128))
    summed = all_gather("ag_small", sum_parts("sum_small", recv), F32).reshape(-1)

    def local_part(name, a):
        ax = SMALL_SHARDED.get(name)
        if ax is None:
            return a
        n_loc = a.shape[ax] // N_DEV
        return lax.dynamic_slice_in_dim(a, me * n_loc, n_loc, axis=ax)

    sg, off = {}, 0
    for n, sz in zip(SMALL, sizes):
        sg[n] = local_part(n, summed[off:off + sz].reshape(small_g[n].shape))
        off += sz
    lsizes = [int(sg[n].size) for n in SMALL]
    ltotal = sum(lsizes)
    lrows = -(-ltotal // (128 * 16)) * 16

    def pack(d_):
        return jnp.concatenate([d_[n].reshape(-1) for n in SMALL] + [jnp.zeros((lrows * 128 - ltotal,), F32)]
                               ).reshape(lrows, 128)

    padv = jnp.concatenate([mom2[n].reshape(-1) for n in SMALL] + [jnp.ones((lrows * 128 - ltotal,), F32)]
                           ).reshape(lrows, 128)
    res = adamw("adamw_small", pack(master)[None], pack(mom1)[None], padv[None], g=pack(sg))
    off = 0
    for n, sz in zip(SMALL, lsizes):
        for dst, r in zip((out_g, out_d, out_m, out_v), res):
            dst[n] = r.reshape(-1)[off:off + sz].reshape(master[n].shape)
        off += sz

    return (loss, dx0.reshape(bsz, seq, d), *[out_g[n] for n in WEIGHTS], *[out_d[n] for n in WEIGHTS],
            *[out_m[n] for n in WEIGHTS], *[out_v[n] for n in WEIGHTS])
```

```python
import functools
import math

import jax
import jax.numpy as jnp
from jax import lax
from jax.experimental import pallas as pl
from jax.experimental.pallas import tpu as pltpu

F32 = jnp.float32
BF16 = jnp.bfloat16
MXU_DTYPE = jnp.bfloat16
N_DEV = 8
MESH_AXES = ("x", "y", "c")

D_MODEL = 1024
SB_HEAD_DIM = 64
SB_WIDTH = 512
SB_BLOCK = 256
POOL_WINDOWS = (2, 4, 8, 16)
POOL_GROUP = 128
POOL_HALO = 16
SSM_TILES = 8
SSM_TILE_STATES = 512
SSM_STATES = 4096
SSM_LANES = 1024
MEM_LEN = 256
XA_HEADS = 4
XA_HEAD_DIM = 256
D_FF = 2816
FF_SHARD = 704
EPS = 1e-6
ADAM_LR = 0.001
ADAM_B1 = 0.9
ADAM_B2 = 0.999
ADAM_EPS = 1e-08
ADAM_WD = 0.01
ADAM_STEP = 10
VMEM_LIMIT = 56 * 1024 * 1024

_NN = (((1,), (0,)), ((), ()))
_NT = (((1,), (1,)), ((), ()))
_TN = (((0,), (0,)), ((), ()))


def _params(sem=None):
    if sem is None:
        return pltpu.CompilerParams(vmem_limit_bytes=VMEM_LIMIT)
    return pltpu.CompilerParams(dimension_semantics=sem, vmem_limit_bytes=VMEM_LIMIT)


def _tile(n, pref, mult=8):
    if n <= pref:
        return n
    for t in range(pref, 0, -1):
        if n % t == 0 and t % mult == 0:
            return t
    return n


def _dot(a, b, dims):
    return lax.dot_general(a.astype(MXU_DTYPE), b.astype(MXU_DTYPE), dims, preferred_element_type=F32)


def _dot_exact01(x, m01, dims=_NN):
    x1 = x.astype(BF16)
    r1 = x - x1.astype(F32)
    x2 = r1.astype(BF16)
    x3 = (r1 - x2.astype(F32)).astype(BF16)
    m = m01.astype(BF16)
    out = lax.dot_general(x1, m, dims, preferred_element_type=F32)
    out = out + lax.dot_general(x2, m, dims, preferred_element_type=F32)
    return out + lax.dot_general(x3, m, dims, preferred_element_type=F32)


def _mm(name, a, b, dims, grid, a_spec, b_spec, o_spec, out_shape, out_dtype, acc_shape, res=None, r_spec=None,
        group=1, n=None, a_sel="full", b_sel="full", o_sel="full", norm_gain=None, rms=None):
    nk = grid[2]
    if out_dtype is None:
        out_dtype = BF16
    n_out = out_shape[-1]
    vec = pl.BlockSpec((1, n_out), lambda i, j, kk: (0, 0))

    def at(sel, s):
        if sel == "lead":
            return (s,)
        if sel == "lanes":
            return (slice(None), slice(s * n, (s + 1) * n))
        return (Ellipsis,)

    extra = [] if res is None else [(res, r_spec)]
    if norm_gain is not None:
        extra.append((norm_gain.reshape(1, n_out), vec))
    if rms is not None:
        extra += [(rms[0], o_spec), (rms[1].reshape(1, n_out), vec), (rms[2], o_spec)]
    n_in = 2 + len(extra)
    if rms is not None:
        out_specs = [o_spec, vec]
        out_shapes = [jax.ShapeDtypeStruct(out_shape, F32), jax.ShapeDtypeStruct((1, n_out), F32)]
    elif norm_gain is not None:
        out_specs = [o_spec, o_spec]
        out_shapes = [jax.ShapeDtypeStruct(out_shape, out_dtype), jax.ShapeDtypeStruct(out_shape, BF16)]
    else:
        out_specs, out_shapes = o_spec, jax.ShapeDtypeStruct(out_shape, out_dtype)

    def body(*refs):
        a_ref, b_ref = refs[0], refs[1]
        ins = list(refs[2:n_in])
        r_ref = ins.pop(0) if res is not None else None
        outs = refs[n_in:]
        o_ref = outs[0]
        acc = refs[-1] if nk > 1 else None
        k = pl.program_id(2)

        def finish(val):
            if r_ref is not None:
                val = val + r_ref[...].astype(F32)
            if rms is not None:
                x_ref, g_ref, d_ref = ins
                xf = x_ref[...]
                r = lax.rsqrt(jnp.mean(xf * xf, axis=-1, keepdims=True) + EPS)
                xh = xf * r
                part = jnp.sum(val * xh, axis=0, keepdims=True)
                first = pl.program_id(0) == 0

                @pl.when(first)
                def _():
                    outs[1][...] = part

                @pl.when(jnp.logical_not(first))
                def _():
                    outs[1][...] += part

                dxh = val * g_ref[...]
                o_ref[...] = d_ref[...] + r * (dxh - xh * jnp.mean(dxh * xh, axis=-1, keepdims=True))
                return
            o_ref[...] = val.astype(out_dtype)
            if norm_gain is not None:
                r = lax.rsqrt(jnp.mean(val * val, axis=-1, keepdims=True) + EPS)
                outs[1][...] = (val * r * ins[0][...]).astype(BF16)

        def emit(s, val):
            if nk == 1:
                if o_sel == "full":
                    finish(val)
                else:
                    o_ref[at(o_sel, s)] = val.astype(out_dtype)
                return

            @pl.when(k == 0)
            def _():
                acc[at(o_sel, s)] = val

            @pl.when(k > 0)
            def _():
                acc[at(o_sel, s)] += val

        total = None
        for s in range(group):
            val = _dot(a_ref[at(a_sel, s)], b_ref[at(b_sel, s)], dims)
            if o_sel == "full":
                total = val if total is None else total + val
            else:
                emit(s, val)
        if o_sel == "full":
            emit(0, total)
        if nk > 1:
            @pl.when(k == nk - 1)
            def _():
                if o_sel == "full":
                    finish(acc[...])
                else:
                    o_ref[...] = acc[...].astype(out_dtype)

    rows_sem = "arbitrary" if rms is not None else "parallel"
    return pl.pallas_call(
        body, name=name, grid=grid, in_specs=[a_spec, b_spec] + [s for _, s in extra], out_specs=out_specs,
        out_shape=out_shapes, scratch_shapes=[pltpu.VMEM(acc_shape, F32)] if nk > 1 else [],
        compiler_params=_params((rows_sem, rows_sem, "arbitrary")),
    )(a, b, *[x for x, _ in extra])


def _row_tile(m, epi):
    return _tile(m, 512 if epi.get("rms") is not None else 1024)


def mm_nn(name, a, b, res=None, koff=0, out_dtype=None, **epi):
    m, k = a.shape
    n = b.shape[1]
    tm, tn, tk = _row_tile(m, epi), _tile(n, 1024, 128), _tile(k, 1024, 128)
    kb = koff // tk
    spec = pl.BlockSpec((tm, tn), lambda i, j, kk: (i, j))
    return _mm(name, a, b, _NN, (m // tm, n // tn, k // tk),
               pl.BlockSpec((tm, tk), lambda i, j, kk: (i, kk)),
               pl.BlockSpec((tk, tn), lambda i, j, kk: (kk + kb, j)),
               spec, (m, n), out_dtype, (tm, tn), res, spec, **epi)


def mm_nn_bs(name, a, bs, stacked_out=False, out_dtype=None):
    m, k = a.shape
    s, _, n = bs.shape
    tm, tk = _tile(m, 1024), _tile(k, 1024, 128)
    a_spec = pl.BlockSpec((tm, tk), lambda i, j, kk: (i, kk))
    if stacked_out:
        return _mm(name, a, bs, _NN, (m // tm, s, k // tk), a_spec,
                   pl.BlockSpec((None, tk, n), lambda i, j, kk: (j, kk, 0)),
                   pl.BlockSpec((None, tm, n), lambda i, j, kk: (j, i, 0)), (s, m, n), out_dtype, (tm, n))
    g = _tile(s, max(1, 1024 // n), 1)
    return _mm(name, a, bs, _NN, (m // tm, s // g, k // tk), a_spec,
               pl.BlockSpec((g, tk, n), lambda i, j, kk: (j, kk, 0)),
               pl.BlockSpec((tm, g * n), lambda i, j, kk: (i, j)), (m, s * n), out_dtype, (tm, g * n),
               group=g, n=n, b_sel="lead", o_sel="lanes")


def mm_as_nn(name, a_st, b3, res, out_dtype=F32, **epi):
    s, m, kp = a_st.shape
    n = b3.shape[2]
    tm, tn = _row_tile(m, epi), _tile(n, 1024, 128)
    spec = pl.BlockSpec((tm, tn), lambda i, j, kk: (i, j))
    return _mm(name, a_st, b3, _NN, (m // tm, n // tn, s),
               pl.BlockSpec((None, tm, kp), lambda i, j, kk: (kk, i, 0)),
               pl.BlockSpec((None, kp, tn), lambda i, j, kk: (kk, 0, j)),
               spec, (m, n), out_dtype, (tm, tn), res, spec, **epi)


def mm_nt(name, dc, b, out_dtype=None, **epi):
    m, n = dc.shape
    k = b.shape[0]
    tm, tko, tnr = _row_tile(m, epi), _tile(k, 1024, 128), _tile(n, 1024, 128)
    return _mm(name, dc, b, _NT, (m // tm, k // tko, n // tnr),
               pl.BlockSpec((tm, tnr), lambda i, j, kk: (i, kk)),
               pl.BlockSpec((tko, tnr), lambda i, j, kk: (j, kk)),
               pl.BlockSpec((tm, tko), lambda i, j, kk: (i, j)), (m, k), out_dtype, (tm, tko), **epi)


def mm_nt_bs(name, dc, bs, dc_stacked=False, out_dtype=None, **epi):
    s, k, n = bs.shape
    m = dc.shape[1] if dc_stacked else dc.shape[0]
    tm, tko = (_tile(m, 1024) if dc_stacked else _row_tile(m, epi)), _tile(k, 1024, 128)
    o_spec = pl.BlockSpec((tm, tko), lambda i, j, kk: (i, j))
    if dc_stacked:
        return _mm(name, dc, bs, _NT, (m // tm, k // tko, s),
                   pl.BlockSpec((None, tm, n), lambda i, j, kk: (kk, i, 0)),
                   pl.BlockSpec((None, tko, n), lambda i, j, kk: (kk, j, 0)), o_spec, (m, k), out_dtype, (tm, tko),
                   **epi)
    g = _tile(s, max(1, 2048 // n), 1)
    return _mm(name, dc, bs, _NT, (m // tm, k // tko, s // g),
               pl.BlockSpec((tm, g * n), lambda i, j, kk: (i, kk)),
               pl.BlockSpec((g, tko, n), lambda i, j, kk: (kk, j, 0)), o_spec, (m, k), out_dtype, (tm, tko),
               group=g, n=n, a_sel="lanes", b_sel="lead", **epi)


def mm_nt_os(name, dc, b3, out_dtype=None):
    m, n = dc.shape
    s, kp, _ = b3.shape
    tm, tnr = _tile(m, 1024), _tile(n, 1024, 128)
    return _mm(name, dc, b3, _NT, (m // tm, s, n // tnr),
               pl.BlockSpec((tm, tnr), lambda i, j, kk: (i, kk)),
               pl.BlockSpec((None, kp, tnr), lambda i, j, kk: (j, 0, kk)),
               pl.BlockSpec((None, tm, kp), lambda i, j, kk: (j, i, 0)), (s, m, kp), out_dtype, (tm, kp))


def mm_tn(name, a, dc, a_stacked=False, dc_cols=None, dc_stacked=False, out_dtype=None):
    if a_stacked:
        s, m, kp = a.shape
        n = dc.shape[1]
        tno, tmr = _tile(n, 1024, 128), _tile(m, 1024)
        return _mm(name, a, dc, _TN, (s, n // tno, m // tmr),
                   pl.BlockSpec((None, tmr, kp), lambda i, j, kk: (i, kk, 0)),
                   pl.BlockSpec((tmr, tno), lambda i, j, kk: (kk, j)),
                   pl.BlockSpec((None, kp, tno), lambda i, j, kk: (i, 0, j)), (s, kp, n), out_dtype, (kp, tno))
    m, k = a.shape
    tko, tmr = _tile(k, 1024, 128), _tile(m, 1024)
    a_spec = pl.BlockSpec((tmr, tko), lambda i, j, kk: (kk, i))
    if dc_stacked:
        s, _, n = dc.shape
        return _mm(name, a, dc, _TN, (k // tko, s, m // tmr), a_spec,
                   pl.BlockSpec((None, tmr, n), lambda i, j, kk: (j, kk, 0)),
                   pl.BlockSpec((None, tko, n), lambda i, j, kk: (j, i, 0)), (s, k, n), out_dtype, (tko, n))
    if dc_cols is not None:
        n = dc_cols
        s = dc.shape[1] // n
        g = _tile(s, max(1, 1024 // n), 1)
        return _mm(name, a, dc, _TN, (k // tko, s // g, m // tmr), a_spec,
                   pl.BlockSpec((tmr, g * n), lambda i, j, kk: (kk, j)),
                   pl.BlockSpec((g, tko, n), lambda i, j, kk: (j, i, 0)), (s, k, n), out_dtype, (g, tko, n),
                   group=g, n=n, b_sel="lanes", o_sel="lead")
    n = dc.shape[1]
    tno = _tile(n, 1024, 128)
    return _mm(name, a, dc, _TN, (k // tko, n // tno, m // tmr), a_spec,
               pl.BlockSpec((tmr, tno), lambda i, j, kk: (kk, j)),
               pl.BlockSpec((tko, tno), lambda i, j, kk: (i, j)), (k, n), out_dtype, (tko, tno))


def rms_fwd(name, x, g):
    t, d = x.shape
    tr = _tile(t, 512)

    def body(x_ref, g_ref, o_ref):
        xf = x_ref[...]
        r = lax.rsqrt(jnp.mean(xf * xf, axis=-1, keepdims=True) + EPS)
        o_ref[...] = (xf * r * g_ref[...]).astype(o_ref.dtype)

    return pl.pallas_call(
        body, name=name, grid=(t // tr,),
        in_specs=[pl.BlockSpec((tr, d), lambda i: (i, 0)), pl.BlockSpec((1, d), lambda i: (0, 0))],
        out_specs=pl.BlockSpec((tr, d), lambda i: (i, 0)),
        out_shape=jax.ShapeDtypeStruct((t, d), BF16), compiler_params=_params(("parallel",)),
    )(x, g.reshape(1, d))


def rms_bwd(name, x, g, dh, dres=None, need_dx=True):
    t, d = x.shape
    tr = _tile(t, 512)

    def body(*refs):
        refs = list(refs)
        x_ref, g_ref, dh_ref = refs[:3]
        r_ref = refs[3] if dres is not None else None
        outs = refs[4:] if dres is not None else refs[3:]
        dx_ref, dg_ref = (outs[0], outs[1]) if need_dx else (None, outs[0])
        i = pl.program_id(0)

        @pl.when(i == 0)
        def _():
            dg_ref[...] = jnp.zeros_like(dg_ref)

        xf = x_ref[...]
        dhf = dh_ref[...].astype(F32)
        r = lax.rsqrt(jnp.mean(xf * xf, axis=-1, keepdims=True) + EPS)
        xh = xf * r
        dg_ref[...] += jnp.sum(dhf * xh, axis=0, keepdims=True)
        if need_dx:
            dxh = dhf * g_ref[...]
            dx = r * (dxh - xh * jnp.mean(dxh * xh, axis=-1, keepdims=True))
            if r_ref is not None:
                dx = dx + r_ref[...]
            dx_ref[...] = dx

    row = pl.BlockSpec((tr, d), lambda i: (i, 0))
    vec = pl.BlockSpec((1, d), lambda i: (0, 0))
    in_specs = [row, vec, row] + ([row] if dres is not None else [])
    args = (x, g.reshape(1, d), dh) + ((dres,) if dres is not None else ())
    out_specs = ([row] if need_dx else []) + [vec]
    out_shape = ([jax.ShapeDtypeStruct((t, d), F32)] if need_dx else []) + [jax.ShapeDtypeStruct((1, d), F32)]
    res = pl.pallas_call(
        body, name=name, grid=(t // tr,), in_specs=in_specs, out_specs=out_specs, out_shape=out_shape,
        compiler_params=_params(("arbitrary",)),
    )(*args)
    return res if need_dx else (None, res[0])


def loss_head(x, g, tgt):
    t, d = x.shape
    tr = _tile(t, 512)

    def body(x_ref, g_ref, t_ref, l_ref, dx_ref, dg_ref):
        i = pl.program_id(0)

        @pl.when(i == 0)
        def _():
            l_ref[...] = jnp.zeros_like(l_ref)
            dg_ref[...] = jnp.zeros_like(dg_ref)

        xf = x_ref[...]
        r = lax.rsqrt(jnp.mean(xf * xf, axis=-1, keepdims=True) + EPS)
        xh = xf * r
        diff = xh * g_ref[...] - t_ref[...]
        l_ref[...] += 0.5 * jnp.sum(jnp.mean(diff * diff, axis=-1, keepdims=True))
        dy = diff * (1.0 / d)
        dg_ref[...] += jnp.sum(dy * xh, axis=0, keepdims=True)
        dxh = dy * g_ref[...]
        dx_ref[...] = r * (dxh - xh * jnp.mean(dxh * xh, axis=-1, keepdims=True))

    row = pl.BlockSpec((tr, d), lambda i: (i, 0))
    vec = pl.BlockSpec((1, d), lambda i: (0, 0))
    return pl.pallas_call(
        body, name="loss_head", grid=(t // tr,), in_specs=[row, vec, row],
        out_specs=[pl.BlockSpec((1, 128), lambda i: (0, 0)), row, vec],
        out_shape=[jax.ShapeDtypeStruct((1, 128), F32), jax.ShapeDtypeStruct((t, d), F32),
                   jax.ShapeDtypeStruct((1, d), F32)],
        compiler_params=_params(("arbitrary",)),
    )(x, g.reshape(1, d), tgt)


def glu_fwd(glu, x, gain):
    t, d = x.shape
    tr = _tile(t, 512)

    def body(v_ref, g_ref, x_ref, n_ref, o_ref, h_ref):
        y = x_ref[...] + v_ref[...] * jax.nn.sigmoid(g_ref[...])
        o_ref[...] = y
        r = lax.rsqrt(jnp.mean(y * y, axis=-1, keepdims=True) + EPS)
        h_ref[...] = (y * r * n_ref[...]).astype(h_ref.dtype)

    row = pl.BlockSpec((tr, d), lambda i: (i, 0))
    return pl.pallas_call(
        body, name="glu_fwd", grid=(t // tr,),
        in_specs=[row, pl.BlockSpec((tr, d), lambda i: (i, 1)), row, pl.BlockSpec((1, d), lambda i: (0, 0))],
        out_specs=[row, row],
        out_shape=[jax.ShapeDtypeStruct((t, d), F32), jax.ShapeDtypeStruct((t, d), BF16)],
        compiler_params=_params(("parallel",)),
    )(glu, glu, x, gain.reshape(1, d))


def glu_bwd(glu, dmix):
    t, d = dmix.shape
    tr = _tile(t, 512)

    def body(v_ref, g_ref, d_ref, o_ref):
        sg = jax.nn.sigmoid(g_ref[...])
        dm = d_ref[...]
        o_ref[:, :d] = (dm * sg).astype(o_ref.dtype)
        o_ref[:, d:] = (dm * v_ref[...] * sg * (1.0 - sg)).astype(o_ref.dtype)

    return pl.pallas_call(
        body, name="glu_bwd", grid=(t // tr,),
        in_specs=[pl.BlockSpec((tr, d), lambda i: (i, 0)), pl.BlockSpec((tr, d), lambda i: (i, 1)),
                  pl.BlockSpec((tr, d), lambda i: (i, 0))],
        out_specs=pl.BlockSpec((tr, 2 * d), lambda i: (i, 0)),
        out_shape=jax.ShapeDtypeStruct((t, 2 * d), BF16), compiler_params=_params(("parallel",)),
    )(glu, glu, dmix)


def _head_masks(shape):
    lane = lax.broadcasted_iota(jnp.int32, shape, 1)
    return lane < SB_HEAD_DIM


def _stack_heads(xf, is_a):
    return jnp.concatenate([jnp.where(is_a, xf, 0.0), jnp.where(is_a, 0.0, xf)], axis=0).astype(MXU_DTYPE)


def _diag_mask(qb, row0, rows):
    row = (lax.broadcasted_iota(jnp.int32, (rows, qb), 0) + row0) & (qb - 1)
    col = lax.broadcasted_iota(jnp.int32, (rows, qb), 1)
    return col < row


def _tri01(qb, pred):
    j = lax.broadcasted_iota(jnp.int32, (qb, qb), 0)
    s = lax.broadcasted_iota(jnp.int32, (qb, qb), 1)
    m = pred(j, s).astype(BF16)
    return jnp.concatenate([m, m], axis=0)


def _split_cat(x):
    hi = x.astype(BF16)
    lo = (x - hi.astype(F32)).astype(BF16)
    return jnp.concatenate([hi, lo], axis=1)


def sb_attn_fwd(proj, order, bsz, seq):
    qb = SB_BLOCK
    nq = seq // qb
    npair = SB_WIDTH // 128
    scale = SB_HEAD_DIM ** -0.5

    def body(q_ref, k_ref, v_ref, order_ref, o_ref, r_ref):
        qi = pl.program_id(2)
        is_a = _head_masks((qb, 128))
        q2 = _stack_heads(q_ref[...] * scale, is_a)
        diag = _diag_mask(qb, 0, 2 * qb)
        upper = _tri01(qb, lambda j, s: j > s)

        def block(kbi, acc, run, masked):
            ks = pl.ds(pl.multiple_of(kbi * qb, qb), qb)
            kblk = k_ref[ks, :].astype(MXU_DTYPE)
            vblk = v_ref[ks, :].astype(MXU_DTYPE)
            z = lax.dot_general(q2, kblk, _NT, preferred_element_type=F32)
            lk = -jnp.maximum(z, 0.0) - jnp.log(1.0 + jnp.exp(-jnp.abs(z)))
            lb = lk + z
            if masked:
                lk = jnp.where(diag, lk, 0.0)
            after = run + lax.dot_general(_split_cat(lk), upper, _NN, preferred_element_type=F32)
            w = jnp.exp(lb + after)
            if masked:
                w = jnp.where(diag, w, 0.0)
            acc = acc + lax.dot_general(w.astype(MXU_DTYPE), vblk, _NN, preferred_element_type=F32)
            return acc, run + jnp.sum(lk, axis=1, keepdims=True)

        carry = block(qi, jnp.zeros((2 * qb, 128), F32), jnp.zeros((2 * qb, 1), F32), True)
        acc, run = lax.fori_loop(0, qi, lambda i, c: block(qi - 1 - i, c[0], c[1], False), carry)
        o_ref[...] = jnp.where(is_a, acc[:qb], acc[qb:]).astype(o_ref.dtype)
        r_ref[...] = jnp.where(is_a, run[:qb], run[qb:])

    return pl.pallas_call(
        body, name="sb_attn_fwd", grid=(bsz, npair, nq),
        in_specs=[pl.BlockSpec((qb, 128), lambda b, p, i: (b * nq + i, p)),
                  pl.BlockSpec((seq, 128), lambda b, p, i: (b, npair + p)),
                  pl.BlockSpec((seq, 128), lambda b, p, i: (b, 2 * npair + p)),
                  pl.BlockSpec((1, 128), lambda b, p, i: (0, 0))],
        out_specs=[pl.BlockSpec((qb, 128), lambda b, p, i: (b * nq + i, p)),
                   pl.BlockSpec((qb, 128), lambda b, p, i: (b * nq + i, p))],
        out_shape=[jax.ShapeDtypeStruct((bsz * seq, SB_WIDTH), BF16),
                   jax.ShapeDtypeStruct((bsz * seq, SB_WIDTH), F32)],
        compiler_params=_params(("parallel", "parallel", "arbitrary")),
    )(proj, proj, proj, order)


def sb_attn_bwd(proj, rsum, dcat, bsz, seq):
    qb = SB_BLOCK
    nq = seq // qb
    npair = SB_WIDTH // 128
    scale = SB_HEAD_DIM ** -0.5

    def body(q_ref, k_ref, v_ref, r_ref, do_ref, dq_ref, dk_ref, dv_ref):
        qi = pl.program_id(2)

        @pl.when(qi == 0)
        def _():
            dk_ref[...] = jnp.zeros_like(dk_ref)
            dv_ref[...] = jnp.zeros_like(dv_ref)

        is_a = _head_masks((qb, 128))
        q2 = _stack_heads(q_ref[...] * scale, is_a)
        do2 = _stack_heads(do_ref[...].astype(F32), is_a)
        rf = r_ref[...]
        rtot = jnp.concatenate([rf[:, 0:1], rf[:, SB_HEAD_DIM:SB_HEAD_DIM + 1]], axis=0)
        diag = _diag_mask(qb, 0, 2 * qb)
        incl = _tri01(qb, lambda j, s: j <= s)
        strict = _tri01(qb, lambda j, s: j < s)

        def block(kbi, dq, pre, epre, masked):
            ks = pl.ds(pl.multiple_of(kbi * qb, qb), qb)
            kblk = k_ref[ks, :].astype(MXU_DTYPE)
            vblk = v_ref[ks, :].astype(MXU_DTYPE)
            z = lax.dot_general(q2, kblk, _NT, preferred_element_type=F32)
            lk = -jnp.maximum(z, 0.0) - jnp.log(1.0 + jnp.exp(-jnp.abs(z)))
            lb = lk + z
            if masked:
                lk = jnp.where(diag, lk, 0.0)
            after = rtot - (pre + lax.dot_general(_split_cat(lk), incl, _NN, preferred_element_type=F32))
            w = jnp.exp(lb + after)
            if masked:
                w = jnp.where(diag, w, 0.0)
            e = lax.dot_general(do2, vblk, _NT, preferred_element_type=F32) * w
            ecum = epre + lax.dot_general(_split_cat(e), strict, _NN, preferred_element_type=F32)
            dz = e - jnp.exp(lb) * (e + ecum)
            if masked:
                dz = jnp.where(diag, dz, 0.0)
            dz = dz.astype(MXU_DTYPE)
            dq = dq + lax.dot_general(dz, kblk, _NN, preferred_element_type=F32)
            dk_ref[ks, :] += lax.dot_general(dz, q2, _TN, preferred_element_type=F32)
            dv_ref[ks, :] += lax.dot_general(w.astype(MXU_DTYPE), do2, _TN, preferred_element_type=F32)
            return dq, pre + jnp.sum(lk, axis=1, keepdims=True), epre + jnp.sum(e, axis=1, keepdims=True)

        zc = jnp.zeros((2 * qb, 1), F32)
        carry = lax.fori_loop(0, qi, lambda kbi, c: block(kbi, c[0], c[1], c[2], False),
                              (jnp.zeros((2 * qb, 128), F32), zc, zc))
        dq = block(qi, carry[0], carry[1], carry[2], True)[0]
        dq_ref[...] = jnp.where(is_a, dq[:qb], dq[qb:]) * scale

    full = jax.ShapeDtypeStruct((bsz * seq, SB_WIDTH), F32)
    qspec = pl.BlockSpec((qb, 128), lambda b, p, i: (b * nq + i, p))
    return pl.pallas_call(
        body, name="sb_attn_bwd", grid=(bsz, npair, nq),
        in_specs=[qspec,
                  pl.BlockSpec((seq, 128), lambda b, p, i: (b, npair + p)),
                  pl.BlockSpec((seq, 128), lambda b, p, i: (b, 2 * npair + p)),
                  qspec, qspec],
        out_specs=[qspec, pl.BlockSpec((seq, 128), lambda b, p, i: (b, p)),
                   pl.BlockSpec((seq, 128), lambda b, p, i: (b, p))],
        out_shape=[full, full, full],
        compiler_params=_params(("parallel", "parallel", "arbitrary")),
    )(proj, proj, proj, rsum, dcat)


def _window_sums(x, forward):
    n = x.shape[0]
    out = []
    s = x
    for sh in (1, 2, 4, 8):
        s = s + pltpu.roll(s, (n - sh) if forward else sh, 0)
        out.append(s)
    return out


def _pool_counts(tc, c, w):
    t = lax.broadcasted_iota(jnp.int32, (tc, 1), 0) + c * tc
    return jnp.minimum(t + 1, w).astype(F32)


def pool_fwd(proj, pool_w, pool_scale, bsz, seq):
    tc = _tile(seq, 512)
    nc = seq // tc
    hb = tc // POOL_HALO
    ucol = 3

    def body(u_ref, prev_ref, w_ref, s_ref, o_ref):
        c = pl.program_id(1)
        prev = jnp.where(c > 0, prev_ref[...], 0.0)
        x = jnp.concatenate([prev, u_ref[...]], axis=0)
        sums = _window_sums(x, forward=False)
        for g, win in enumerate(POOL_WINDOWS):
            ls = slice(g * POOL_GROUP, (g + 1) * POOL_GROUP)
            pooled = sums[g][POOL_HALO:, ls] / _pool_counts(tc, c, win) - x[POOL_HALO:, ls]
            y = _dot(pooled, w_ref[g], _NN)
            o_ref[:, ls] = (y * s_ref[:, ls]).astype(o_ref.dtype)

    return pl.pallas_call(
        body, name="pool_fwd", grid=(bsz, nc),
        in_specs=[pl.BlockSpec((tc, SB_WIDTH), lambda b, c: (b * nc + c, ucol)),
                  pl.BlockSpec((POOL_HALO, SB_WIDTH), lambda b, c: (jnp.maximum((b * nc + c) * hb - 1, 0), ucol)),
                  pl.BlockSpec((4, POOL_GROUP, POOL_GROUP), lambda b, c: (0, 0, 0)),
                  pl.BlockSpec((1, SB_WIDTH), lambda b, c: (0, 0))],
        out_specs=pl.BlockSpec((tc, SB_WIDTH), lambda b, c: (b * nc + c, 0)),
        out_shape=jax.ShapeDtypeStruct((bsz * seq, SB_WIDTH), BF16),
        compiler_params=_params(("parallel", "parallel")),
    )(proj, proj, pool_w, pool_scale)


def pool_bwd(proj, pool_w, pool_scale, dcat, bsz, seq):
    tc = _tile(seq, 512)
    nc = seq // tc
    hb = tc // POOL_HALO
    nblk = bsz * seq // POOL_HALO
    ucol = 3

    def body(u_ref, prev_ref, dy_ref, nxt_ref, w_ref, s_ref, du_ref, dw_ref, ds_ref):
        b, c = pl.program_id(0), pl.program_id(1)

        @pl.when((b == 0) & (c == 0))
        def _():
            dw_ref[...] = jnp.zeros_like(dw_ref)
            ds_ref[...] = jnp.zeros_like(ds_ref)

        prev = jnp.where(c > 0, prev_ref[...], 0.0)
        x = jnp.concatenate([prev, u_ref[...]], axis=0)
        sums = _window_sums(x, forward=False)
        nxt = jnp.where(c < nc - 1, nxt_ref[...].astype(F32), 0.0)
        dy = jnp.concatenate([dy_ref[...].astype(F32), nxt], axis=0)
        tq = lax.broadcasted_iota(jnp.int32, (tc + POOL_HALO, 1), 0) + c * tc
        for g, win in enumerate(POOL_WINDOWS):
            ls = slice(g * POOL_GROUP, (g + 1) * POOL_GROUP)
            pooled = sums[g][POOL_HALO:, ls] / _pool_counts(tc, c, win) - x[POOL_HALO:, ls]
            y = _dot(pooled, w_ref[g], _NN)
            ds_ref[:, ls] += jnp.sum(dy[:tc, ls] * y, axis=0, keepdims=True)
            dz = dy[:, ls] * s_ref[:, ls]
            dw_ref[g] += _dot(pooled, dz[:tc], _TN)
            dpool = _dot(dz, w_ref[g], _NT)
            dmean = dpool / jnp.minimum(tq + 1, win).astype(F32)
            fsum = _window_sums(dmean, forward=True)[g]
            du_ref[:, ls] = fsum[:tc] - dpool[:tc]

    return pl.pallas_call(
        body, name="pool_bwd", grid=(bsz, nc),
        in_specs=[pl.BlockSpec((tc, SB_WIDTH), lambda b, c: (b * nc + c, ucol)),
                  pl.BlockSpec((POOL_HALO, SB_WIDTH), lambda b, c: (jnp.maximum((b * nc + c) * hb - 1, 0), ucol)),
                  pl.BlockSpec((tc, SB_WIDTH), lambda b, c: (b * nc + c, 1)),
                  pl.BlockSpec((POOL_HALO, SB_WIDTH), lambda b, c: (jnp.minimum((b * nc + c + 1) * hb, nblk - 1), 1)),
                  pl.BlockSpec((4, POOL_GROUP, POOL_GROUP), lambda b, c: (0, 0, 0)),
                  pl.BlockSpec((1, SB_WIDTH), lambda b, c: (0, 0))],
        out_specs=[pl.BlockSpec((tc, SB_WIDTH), lambda b, c: (b * nc + c, 0)),
                   pl.BlockSpec((4, POOL_GROUP, POOL_GROUP), lambda b, c: (0, 0, 0)),
                   pl.BlockSpec((1, SB_WIDTH), lambda b, c: (0, 0))],
        out_shape=[jax.ShapeDtypeStruct((bsz * seq, SB_WIDTH), F32),
                   jax.ShapeDtypeStruct((4, POOL_GROUP, POOL_GROUP), F32),
                   jax.ShapeDtypeStruct((1, SB_WIDTH), F32)],
        compiler_params=_params(("arbitrary", "arbitrary")),
    )(proj, proj, dcat, dcat, pool_w, pool_scale)


def _lbar(lam_re, lam_im, log_dt):
    dt = jnp.exp(log_dt)
    mag = jnp.exp(lam_re * dt)
    ang = lam_im * dt
    return mag * jnp.cos(ang), mag * jnp.sin(ang)


def _bbar(lam_re, lam_im, log_dt, b_re, b_im):
    lb_re, lb_im = _lbar(lam_re, lam_im, log_dt)
    n_re = lb_re - 1.0
    den = lam_re * lam_re + lam_im * lam_im
    coef_re = (n_re * lam_re + lb_im * lam_im) / den
    coef_im = (lb_im * lam_re - n_re * lam_im) / den
    return coef_re * b_re - coef_im * b_im, coef_re * b_im + coef_im * b_re


def _expand01():
    p = lax.broadcasted_iota(jnp.int32, (64, 1024), 0)
    q = lax.broadcasted_iota(jnp.int32, (64, 1024), 1)
    return (lax.shift_right_logical(q, 4) == p).astype(BF16)


def ssm_prep(lam_re, lam_im, log_dt, b_re2, b_im2):
    def body(lr_ref, li_ref, dt_ref, br_ref, bi_ref, ar_ref, ai_ref, bbr_ref, bbi_ref):
        e = _expand01()
        lr, li, dt = lr_ref[...], li_ref[...], dt_ref[...]
        ar_ref[...], ai_ref[...] = _lbar(lr, li, dt)
        bbr_ref[...], bbi_ref[...] = _bbar(_dot_exact01(lr, e), _dot_exact01(li, e), dt, br_ref[...], bi_ref[...])

    s64 = jax.ShapeDtypeStruct((64, 64), F32)
    s1k = jax.ShapeDtypeStruct((64, 1024), F32)
    return pl.pallas_call(body, name="ssm_prep", out_shape=[s64, s64, s1k, s1k], compiler_params=_params())(
        lam_re, lam_im, log_dt, b_re2, b_im2)


def ssm_prep_bwd(lam_re, lam_im, log_dt, b_re2, b_im2, da_re, da_im, dbb_re, dbb_im):
    def body(lr_ref, li_ref, dt_ref, br_ref, bi_ref, dar_ref, dai_ref, dbr_ref, dbi_ref,
             olr_ref, oli_ref, odt_ref, obr_ref, obi_ref):
        e = _expand01()
        lr, li, dt = lr_ref[...], li_ref[...], dt_ref[...]
        _, vjp_a = jax.vjp(_lbar, lr, li, dt)
        g_lr, g_li, g_dt = vjp_a((dar_ref[...], dai_ref[...]))
        _, vjp_b = jax.vjp(_bbar, _dot_exact01(lr, e), _dot_exact01(li, e), dt, br_ref[...], bi_ref[...])
        x_lr, x_li, x_dt, g_br, g_bi = vjp_b((dbr_ref[...], dbi_ref[...]))
        olr_ref[...] = g_lr + _dot_exact01(x_lr, e, _NT)
        oli_ref[...] = g_li + _dot_exact01(x_li, e, _NT)
        odt_ref[...] = g_dt + x_dt
        obr_ref[...] = g_br
        obi_ref[...] = g_bi

    s64 = jax.ShapeDtypeStruct((64, 64), F32)
    s1k = jax.ShapeDtypeStruct((64, 1024), F32)
    return pl.pallas_call(body, name="ssm_prep_bwd",
                          out_shape=[s64, s64, jax.ShapeDtypeStruct((64, 1), F32), s1k, s1k],
                          compiler_params=_params())(
        lam_re, lam_im, log_dt, b_re2, b_im2, da_re, da_im, dbb_re, dbb_im)


def _gelu(y):
    c = math.sqrt(2.0 / math.pi)
    return 0.5 * y * (1.0 + jnp.tanh(c * (y + 0.044715 * y * y * y)))


def _gelu_grad(y):
    c = math.sqrt(2.0 / math.pi)
    th = jnp.tanh(c * (y + 0.044715 * y * y * y))
    return 0.5 * (1.0 + th) + 0.5 * y * (1.0 - th * th) * c * (1.0 + 3.0 * 0.044715 * y * y)


def _cmul(ar, ai, br, bi):
    return ar * br - ai * bi, ar * bi + ai * br


def _scan_tables(ar, ai, reverse, tabs):
    row = lax.broadcasted_iota(jnp.int32, (8, SSM_STATES), 0)
    a1 = (ar, ai)
    a2 = _cmul(*a1, *a1)
    a4 = _cmul(*a2, *a2)
    powers = [a1, a2, _cmul(*a2, *a1), a4]
    powers += [_cmul(*a4, *p) for p in powers]
    for k, (val, sh) in enumerate(((a1, 1), (a2, 2), (a4, 4))):
        keep = (row < 8 - sh) if reverse else (row >= sh)
        tabs[2 * k][...] = jnp.where(keep, val[0], 0.0)
        tabs[2 * k + 1][...] = jnp.where(keep, val[1], 0.0)
    pr = jnp.zeros((8, SSM_STATES), F32)
    pi = jnp.zeros((8, SSM_STATES), F32)
    for r in range(8):
        val = powers[7 - r] if reverse else powers[r]
        pr = jnp.where(row == r, val[0], pr)
        pi = jnp.where(row == r, val[1], pi)
    tabs[6][...] = pr
    tabs[7][...] = pi


def _scan8(xr, xi, tabs, ls, cr, ci, reverse):
    for k, sh in enumerate((1, 2, 4)):
        amt = (8 - sh) if reverse else sh
        sr, si = pltpu.roll(xr, amt, 0), pltpu.roll(xi, amt, 0)
        lr, li = tabs[2 * k][:, ls], tabs[2 * k + 1][:, ls]
        xr, xi = xr + lr * sr - li * si, xi + lr * si + li * sr
    pr, pi = tabs[6][:, ls], tabs[7][:, ls]
    return xr + pr * cr - pi * ci, xi + pr * ci + pi * cr


def _block8(b):
    return pl.ds(pl.multiple_of(b * 8, 8), 8)


def ssm_fwd(u, wt, ct, a_re, a_im, dskip, bsz, seq):
    tc = _tile(seq, 256)
    nc = seq // tc
    ns = SSM_TILE_STATES
    nl = SSM_STATES // SSM_LANES

    def body(u_ref, wt_ref, ct_ref, ar_ref, ai_ref, d_ref, y_ref, gl_ref, hr_ref, hi_ref, sr_ref, si_ref, *tabs):
        b, c = pl.program_id(0), pl.program_id(1)

        @pl.when((b == 0) & (c == 0))
        def _():
            _scan_tables(ar_ref[...], ai_ref[...], False, tabs)

        @pl.when(c == 0)
        def _():
            sr_ref[...] = jnp.zeros_like(sr_ref)
            si_ref[...] = jnp.zeros_like(si_ref)

        uf = u_ref[...]
        for i in range(SSM_TILES):
            bu = _dot(uf[:, i * 128:(i + 1) * 128], wt_ref[i], _NN)
            hr_ref[:, i * ns:(i + 1) * ns] = bu[:, :ns]
            hi_ref[:, i * ns:(i + 1) * ns] = bu[:, ns:]

        def step(blk, carry):
            rows = _block8(blk)
            new = []
            for j in range(nl):
                ls = slice(j * SSM_LANES, (j + 1) * SSM_LANES)
                xr, xi = _scan8(hr_ref[rows, ls], hi_ref[rows, ls], tabs, ls, carry[2 * j], carry[2 * j + 1], False)
                hr_ref[rows, ls] = xr
                hi_ref[rows, ls] = xi
                new += [xr[7:8], xi[7:8]]
            return tuple(new)

        init = []
        for j in range(nl):
            ls = slice(j * SSM_LANES, (j + 1) * SSM_LANES)
            init += [sr_ref[:, ls], si_ref[:, ls]]
        last = lax.fori_loop(0, tc // 8, step, tuple(init), unroll=2)
        for j in range(nl):
            ls = slice(j * SSM_LANES, (j + 1) * SSM_LANES)
            sr_ref[:, ls] = last[2 * j]
            si_ref[:, ls] = last[2 * j + 1]
        for i in range(SSM_TILES):
            hcat = jnp.concatenate([hr_ref[:, i * ns:(i + 1) * ns], hi_ref[:, i * ns:(i + 1) * ns]], axis=1)
            ls = slice(i * 128, (i + 1) * 128)
            y = _dot(hcat, ct_ref[i], _NN) + d_ref[:, ls] * uf[:, ls]
            y_ref[:, ls] = y
            gl_ref[:, ls] = _gelu(y).astype(gl_ref.dtype)

    t = bsz * seq
    row = pl.BlockSpec((tc, D_MODEL), lambda b, c: (b * nc + c, 0))
    st = pl.BlockSpec((tc, SSM_STATES), lambda b, c: (b * nc + c, 0))
    diag = pl.BlockSpec((1, SSM_STATES), lambda b, c: (0, 0))
    return pl.pallas_call(
        body, name="ssm_fwd", grid=(bsz, nc),
        in_specs=[row, pl.BlockSpec((SSM_TILES, 128, 2 * ns), lambda b, c: (0, 0, 0)),
                  pl.BlockSpec((SSM_TILES, 2 * ns, 128), lambda b, c: (0, 0, 0)), diag, diag,
                  pl.BlockSpec((1, D_MODEL), lambda b, c: (0, 0))],
        out_specs=[row, row, st, st],
        out_shape=[jax.ShapeDtypeStruct((t, D_MODEL), F32), jax.ShapeDtypeStruct((t, D_MODEL), BF16),
                   jax.ShapeDtypeStruct((t, SSM_STATES), F32), jax.ShapeDtypeStruct((t, SSM_STATES), F32)],
        scratch_shapes=[pltpu.VMEM((1, SSM_STATES), F32)] * 2 + [pltpu.VMEM((8, SSM_STATES), F32)] * 8,
        compiler_params=_params(("arbitrary", "arbitrary")),
    )(u, wt, ct, a_re, a_im, dskip)


def ssm_bwd(dgl, y, u, h_re, h_im, wt, ct, a_re, a_im, dskip, bsz, seq):
    tc = _tile(seq, 256)
    nc = seq // tc
    nb = tc // 8
    ns = SSM_TILE_STATES
    nl = SSM_STATES // SSM_LANES

    def body(dgl_ref, y_ref, u_ref, hr_ref, hi_ref, pr_ref, pi_ref, wt_ref, ct_ref, ar_ref, ai_ref, d_ref,
             du_ref, dwt_ref, dct_ref, dd_ref, dar_ref, dai_ref, gr_ref, gi_ref, sr_ref, si_ref, ar8_ref, ai8_ref,
             *tabs):
        b, c = pl.program_id(0), pl.program_id(1)

        @pl.when((b == 0) & (c == 0))
        def _():
            for r in (dwt_ref, dct_ref, dd_ref, ar8_ref, ai8_ref):
                r[...] = jnp.zeros_like(r)
            _scan_tables(ar_ref[...], -ai_ref[...], True, tabs)

        @pl.when(c == 0)
        def _():
            sr_ref[...] = jnp.zeros_like(sr_ref)
            si_ref[...] = jnp.zeros_like(si_ref)

        uf = u_ref[...]
        dy = dgl_ref[...].astype(F32) * _gelu_grad(y_ref[...])
        dd_ref[...] += jnp.sum(dy * uf, axis=0, keepdims=True)
        for i in range(SSM_TILES):
            dyi = dy[:, i * 128:(i + 1) * 128]
            dh = _dot(dyi, ct_ref[i], _NT)
            gr_ref[:, i * ns:(i + 1) * ns] = dh[:, :ns]
            gi_ref[:, i * ns:(i + 1) * ns] = dh[:, ns:]
            hcat = jnp.concatenate([hr_ref[:, i * ns:(i + 1) * ns], hi_ref[:, i * ns:(i + 1) * ns]], axis=1)
            dct_ref[i] += _dot(hcat, dyi, _TN)
        row0 = lax.broadcasted_iota(jnp.int32, (8, SSM_LANES), 0) == 0

        def block(blk, carry, before):
            rows = _block8(blk)
            new = []
            for j in range(nl):
                ls = slice(j * SSM_LANES, (j + 1) * SSM_LANES)
                gr, gi = _scan8(gr_ref[rows, ls], gi_ref[rows, ls], tabs, ls, carry[2 * j], carry[2 * j + 1], True)
                gr_ref[rows, ls] = gr
                gi_ref[rows, ls] = gi
                bpr, bpi = before(j)
                hpr = jnp.where(row0, bpr, pltpu.roll(hr_ref[rows, ls], 1, 0))
                hpi = jnp.where(row0, bpi, pltpu.roll(hi_ref[rows, ls], 1, 0))
                ar8_ref[:, ls] += gr * hpr + gi * hpi
                ai8_ref[:, ls] += gi * hpr - gr * hpi
                new += [gr[0:1], gi[0:1]]
            return tuple(new)

        def step(jj, carry):
            blk = nb - 1 - jj
            prev_rows = _block8(blk - 1)

            def before(j):
                ls = slice(j * SSM_LANES, (j + 1) * SSM_LANES)
                return hr_ref[prev_rows, ls][7:8], hi_ref[prev_rows, ls][7:8]

            return block(blk, carry, before)

        init = []
        for j in range(nl):
            ls = slice(j * SSM_LANES, (j + 1) * SSM_LANES)
            init += [sr_ref[:, ls], si_ref[:, ls]]
        carry = lax.fori_loop(0, nb - 1, step, tuple(init))
        first = c == nc - 1

        def before_chunk(j):
            ls = slice(j * SSM_LANES, (j + 1) * SSM_LANES)
            return (jnp.where(first, 0.0, pr_ref[:, ls][7:8]), jnp.where(first, 0.0, pi_ref[:, ls][7:8]))

        last = block(0, carry, before_chunk)
        for j in range(nl):
            ls = slice(j * SSM_LANES, (j + 1) * SSM_LANES)
            sr_ref[:, ls] = last[2 * j]
            si_ref[:, ls] = last[2 * j + 1]
        for i in range(SSM_TILES):
            ls = slice(i * 128, (i + 1) * 128)
            gcat = jnp.concatenate([gr_ref[:, i * ns:(i + 1) * ns], gi_ref[:, i * ns:(i + 1) * ns]], axis=1)
            du_ref[:, ls] = (_dot(gcat, wt_ref[i], _NT) + d_ref[:, ls] * dy[:, ls]).astype(du_ref.dtype)
            dwt_ref[i] += _dot(uf[:, ls], gcat, _TN)

        @pl.when((b == bsz - 1) & (c == nc - 1))
        def _():
            dar_ref[...] = jnp.sum(ar8_ref[...], axis=0, keepdims=True)
            dai_ref[...] = jnp.sum(ai8_ref[...], axis=0, keepdims=True)

    t = bsz * seq
    rev = lambda b, c: (b * nc + (nc - 1 - c), 0)
    row = pl.BlockSpec((tc, D_MODEL), rev)
    st = pl.BlockSpec((tc, SSM_STATES), rev)
    prev = pl.BlockSpec((8, SSM_STATES), lambda b, c: (jnp.maximum((b * nc + (nc - 1 - c)) * nb - 1, 0), 0))
    diag = pl.BlockSpec((1, SSM_STATES), lambda b, c: (0, 0))
    wts = pl.BlockSpec((SSM_TILES, 128, 2 * ns), lambda b, c: (0, 0, 0))
    cts = pl.BlockSpec((SSM_TILES, 2 * ns, 128), lambda b, c: (0, 0, 0))
    vec = pl.BlockSpec((1, D_MODEL), lambda b, c: (0, 0))
    return pl.pallas_call(
        body, name="ssm_bwd", grid=(bsz, nc),
        in_specs=[row, row, row, st, st, prev, prev, wts, cts, diag, diag, vec],
        out_specs=[row, wts, cts, vec, diag, diag],
        out_shape=[jax.ShapeDtypeStruct((t, D_MODEL), BF16),
                   jax.ShapeDtypeStruct((SSM_TILES, 128, 2 * ns), F32),
                   jax.ShapeDtypeStruct((SSM_TILES, 2 * ns, 128), F32),
                   jax.ShapeDtypeStruct((1, D_MODEL), F32),
                   jax.ShapeDtypeStruct((1, SSM_STATES), F32), jax.ShapeDtypeStruct((1, SSM_STATES), F32)],
        scratch_shapes=[pltpu.VMEM((tc, SSM_STATES), F32)] * 2 + [pltpu.VMEM((1, SSM_STATES), F32)] * 2
                       + [pltpu.VMEM((8, SSM_STATES), F32)] * 10,
        compiler_params=_params(("arbitrary", "arbitrary")),
    )(dgl, y, u, h_re, h_im, h_re, h_im, wt, ct, a_re, a_im, dskip)


def _ssm_in_weights(bb_re2, bb_im2):
    eye = jnp.eye(8, dtype=F32)[None, :, None, :, None]

    def one(bb):
        t = bb.reshape(8, 8, 64, 16).transpose(0, 1, 3, 2)
        return (t[:, :, :, None, :] * eye).reshape(8, 128, 512)

    return jnp.concatenate([one(bb_re2), one(bb_im2)], axis=-1).astype(MXU_DTYPE)


def _ssm_in_weights_bwd(dwt):
    eye = jnp.eye(8, dtype=F32)[None, :, None, :, None]

    def one(d):
        t = (d.reshape(8, 8, 16, 8, 64) * eye).sum(axis=3)
        return t.transpose(0, 1, 3, 2).reshape(64, 1024)

    return one(dwt[..., :512]), one(dwt[..., 512:])


def _ssm_out_weights(c_re, c_im):
    eye = jnp.eye(8, dtype=F32)[None, :, None, :, None]

    def one(cc):
        t = cc.reshape(8, 8, 16, 64).transpose(0, 1, 3, 2)
        return (t[:, :, :, None, :] * eye).reshape(8, 512, 128)

    return jnp.concatenate([one(c_re), -one(c_im)], axis=1).astype(MXU_DTYPE)


def _ssm_out_weights_bwd(dct):
    eye = jnp.eye(8, dtype=F32)[None, :, None, :, None]

    def one(d):
        t = (d.reshape(8, 8, 64, 8, 16) * eye).sum(axis=3)
        return t.transpose(0, 1, 3, 2).reshape(64, 16, 64)

    return one(dct[:, :512]), -one(dct[:, 512:])


def _softmax(s):
    m = jnp.max(s, axis=-1, keepdims=True)
    e = jnp.exp(s - m)
    return e / jnp.sum(e, axis=-1, keepdims=True)


def xattn_fwd(q, kv, bsz, seq):
    tq = _tile(seq, 512)
    nq = seq // tq
    scale = XA_HEAD_DIM ** -0.5

    def body(q_ref, k_ref, v_ref, o_ref):
        s = lax.dot_general(q_ref[...], k_ref[...], _NT, preferred_element_type=F32) * scale
        p = _softmax(s)
        o_ref[...] = _dot(p, v_ref[...], _NN).astype(o_ref.dtype)

    qs = pl.BlockSpec((tq, XA_HEAD_DIM), lambda b, h, i: (b * nq + i, h))
    return pl.pallas_call(
        body, name="xattn_fwd", grid=(bsz, XA_HEADS, nq),
        in_specs=[qs, pl.BlockSpec((MEM_LEN, XA_HEAD_DIM), lambda b, h, i: (b, h)),
                  pl.BlockSpec((MEM_LEN, XA_HEAD_DIM), lambda b, h, i: (b, XA_HEADS + h))],
        out_specs=qs, out_shape=jax.ShapeDtypeStruct((bsz * seq, D_MODEL), BF16),
        compiler_params=_params(("parallel", "parallel", "parallel")),
    )(q, kv, kv)


def xattn_bwd(q, kv, do, bsz, seq):
    tq = _tile(seq, 512)
    nq = seq // tq
    scale = XA_HEAD_DIM ** -0.5

    def body(q_ref, k_ref, v_ref, do_ref, dq_ref, dk_ref, dv_ref):
        @pl.when(pl.program_id(2) == 0)
        def _():
            dk_ref[...] = jnp.zeros_like(dk_ref)
            dv_ref[...] = jnp.zeros_like(dv_ref)

        qv, kk, vv, dov = q_ref[...], k_ref[...], v_ref[...], do_ref[...]
        s = lax.dot_general(qv, kk, _NT, preferred_element_type=F32) * scale
        p = _softmax(s)
        dp = lax.dot_general(dov, vv, _NT, preferred_element_type=F32)
        ds = (p * (dp - jnp.sum(dp * p, axis=-1, keepdims=True)) * scale).astype(MXU_DTYPE)
        dq_ref[...] = lax.dot_general(ds, kk, _NN, preferred_element_type=F32).astype(dq_ref.dtype)
        dk_ref[...] += lax.dot_general(ds, qv, _TN, preferred_element_type=F32)
        dv_ref[...] += lax.dot_general(p.astype(MXU_DTYPE), dov, _TN, preferred_element_type=F32)

    qs = pl.BlockSpec((tq, XA_HEAD_DIM), lambda b, h, i: (b * nq + i, h))
    ks = pl.BlockSpec((MEM_LEN, XA_HEAD_DIM), lambda b, h, i: (b, h))
    vs = pl.BlockSpec((MEM_LEN, XA_HEAD_DIM), lambda b, h, i: (b, XA_HEADS + h))
    dkv = jax.ShapeDtypeStruct((bsz * MEM_LEN, D_MODEL), F32)
    dq, dk, dv = pl.pallas_call(
        body, name="xattn_bwd", grid=(bsz, XA_HEADS, nq),
        in_specs=[qs, ks, vs, qs], out_specs=[qs, ks, ks],
        out_shape=[jax.ShapeDtypeStruct((bsz * seq, D_MODEL), BF16), dkv, dkv],
        compiler_params=_params(("parallel", "parallel", "arbitrary")),
    )(q, kv, kv, do)
    return dq, dk, dv


CONV_HALO = 16


def _shifts_down(x, prev):
    h = prev.shape[0]
    ext = jnp.concatenate([prev, x], axis=0)
    return pltpu.roll(ext, 1, 0)[h:], pltpu.roll(ext, 2, 0)[h:]


def _shifts_up(x, nxt):
    rows = x.shape[0]
    n = rows + nxt.shape[0]
    ext = jnp.concatenate([x, nxt], axis=0)
    return pltpu.roll(ext, n - 1, 0)[:rows], pltpu.roll(ext, n - 2, 0)[:rows]


def _conv_taps(u, u1, u2, w, b):
    return b + w[2:3] * u + w[1:2] * u1 + w[0:1] * u2


def conv_fwd(up, cw, cb, bsz, seq):
    tc = _tile(seq, 512)
    nc = seq // tc
    hb = tc // CONV_HALO
    half = N_DEV // 2

    def body(uv_ref, ug_ref, pv_ref, pg_ref, wv_ref, wg_ref, bv_ref, bg_ref, o_ref):
        c = pl.program_id(2)
        pv = jnp.where(c > 0, pv_ref[...].astype(F32), 0.0)
        pg = jnp.where(c > 0, pg_ref[...].astype(F32), 0.0)
        uv, ug = uv_ref[...].astype(F32), ug_ref[...].astype(F32)
        val = _conv_taps(uv, *_shifts_down(uv, pv), wv_ref[...], bv_ref[...])
        gate = _conv_taps(ug, *_shifts_down(ug, pg), wg_ref[...], bg_ref[...])
        o_ref[...] = (gate * jax.nn.sigmoid(gate) * val).astype(o_ref.dtype)

    def cur(off):
        return pl.BlockSpec((None, tc, FF_SHARD), lambda b, j, c: (j + off, b * nc + c, 0))

    def prv(off):
        return pl.BlockSpec((None, CONV_HALO, FF_SHARD), lambda b, j, c: (j + off, jnp.maximum((b * nc + c) * hb - 1, 0), 0))

    def par(rows, off):
        return pl.BlockSpec((None, rows, FF_SHARD), lambda b, j, c: (j + off, 0, 0))

    return pl.pallas_call(
        body, name="conv_fwd", grid=(bsz, half, nc),
        in_specs=[cur(0), cur(half), prv(0), prv(half), par(3, 0), par(3, half), par(1, 0), par(1, half)],
        out_specs=cur(0), out_shape=jax.ShapeDtypeStruct((half, bsz * seq, FF_SHARD), BF16),
        compiler_params=_params(("parallel", "parallel", "parallel")),
    )(up, up, up, up, cw, cw, cb, cb)


def conv_bwd_taps(up, cw, cb, dact, bsz, seq):
    tc = _tile(seq, 512)
    nc = seq // tc
    hb = tc // CONV_HALO
    half = N_DEV // 2

    def body(uv_ref, ug_ref, pv_ref, pg_ref, wv_ref, wg_ref, bv_ref, bg_ref, da_ref,
             dc_ref, dwv_ref, dwg_ref, dbv_ref, dbg_ref):
        b, c = pl.program_id(1), pl.program_id(2)

        @pl.when((b == 0) & (c == 0))
        def _():
            for r in (dwv_ref, dwg_ref, dbv_ref, dbg_ref):
                r[...] = jnp.zeros_like(r)

        pv = jnp.where(c > 0, pv_ref[...].astype(F32), 0.0)
        pg = jnp.where(c > 0, pg_ref[...].astype(F32), 0.0)
        uv, ug = uv_ref[...].astype(F32), ug_ref[...].astype(F32)
        uv1, uv2 = _shifts_down(uv, pv)
        ug1, ug2 = _shifts_down(ug, pg)
        val = _conv_taps(uv, uv1, uv2, wv_ref[...], bv_ref[...])
        gate = _conv_taps(ug, ug1, ug2, wg_ref[...], bg_ref[...])
        sg = jax.nn.sigmoid(gate)
        da = da_ref[...].astype(F32)
        dsilu = da * sg
        dval = dsilu * gate
        dgate = dsilu * val * (1.0 + gate * (1.0 - sg))
        dc_ref[0] = dval.astype(dc_ref.dtype)
        dc_ref[1] = dgate.astype(dc_ref.dtype)
        for dcv, taps, dw_ref, db_ref in ((dval, (uv2, uv1, uv), dwv_ref, dbv_ref),
                                          (dgate, (ug2, ug1, ug), dwg_ref, dbg_ref)):
            db_ref[...] += jnp.sum(dcv, axis=0, keepdims=True)
            for k, u_k in enumerate(taps):
                dw_ref[k:k + 1, :] += jnp.sum(dcv * u_k, axis=0, keepdims=True)

    def cur(off):
        return pl.BlockSpec((None, tc, FF_SHARD), lambda j, b, c: (j + off, b * nc + c, 0))

    def prv(off):
        return pl.BlockSpec((None, CONV_HALO, FF_SHARD), lambda j, b, c: (j + off, jnp.maximum((b * nc + c) * hb - 1, 0), 0))

    def par(rows, off):
        return pl.BlockSpec((None, rows, FF_SHARD), lambda j, b, c: (j + off, 0, 0))

    t = bsz * seq
    hs = jax.ShapeDtypeStruct((2, half, t, FF_SHARD), BF16)
    ws = jax.ShapeDtypeStruct((half, 3, FF_SHARD), F32)
    bs = jax.ShapeDtypeStruct((half, 1, FF_SHARD), F32)
    dc, dwv, dwg, dbv, dbg = pl.pallas_call(
        body, name="conv_bwd_taps", grid=(half, bsz, nc),
        in_specs=[cur(0), cur(half), prv(0), prv(half), par(3, 0), par(3, half), par(1, 0), par(1, half), cur(0)],
        out_specs=[pl.BlockSpec((2, None, tc, FF_SHARD), lambda j, b, c: (0, j, b * nc + c, 0)),
                   par(3, 0), par(3, 0), par(1, 0), par(1, 0)],
        out_shape=[hs, ws, ws, bs, bs],
        compiler_params=_params(("parallel", "arbitrary", "arbitrary")),
    )(up, up, up, up, cw, cw, cb, cb, dact)
    return (dc.reshape(N_DEV, t, FF_SHARD), jnp.concatenate([dwv, dwg], axis=0),
            jnp.concatenate([dbv, dbg], axis=0))


def conv_bwd_input(dconv, cw, bsz, seq):
    tc = _tile(seq, 1024)
    nc = seq // tc
    hb = tc // CONV_HALO
    nblk = bsz * seq // CONV_HALO

    def body(d_ref, n_ref, w_ref, o_ref):
        c = pl.program_id(2)
        nxt = jnp.where(c < nc - 1, n_ref[...].astype(F32), 0.0)
        d = d_ref[...].astype(F32)
        d1, d2 = _shifts_up(d, nxt)
        w = w_ref[...]
        o_ref[...] = (w[2:3] * d + w[1:2] * d1 + w[0:1] * d2).astype(o_ref.dtype)

    cur = pl.BlockSpec((None, tc, FF_SHARD), lambda j, b, c: (j, b * nc + c, 0))
    return pl.pallas_call(
        body, name="conv_bwd_input", grid=(N_DEV, bsz, nc),
        in_specs=[cur, pl.BlockSpec((None, CONV_HALO, FF_SHARD),
                                    lambda j, b, c: (j, jnp.minimum((b * nc + c + 1) * hb, nblk - 1), 0)),
                  pl.BlockSpec((None, 3, FF_SHARD), lambda j, b, c: (j, 0, 0))],
        out_specs=cur, out_shape=jax.ShapeDtypeStruct(dconv.shape, BF16),
        compiler_params=_params(("parallel", "parallel", "parallel")),
    )(dconv, dconv, cw)


def _my_index():
    return 4 * lax.axis_index("x") + 2 * lax.axis_index("y") + lax.axis_index("c")


def _peer(k):
    return (lax.axis_index("x") ^ ((k >> 2) & 1), lax.axis_index("y") ^ ((k >> 1) & 1),
            lax.axis_index("c") ^ (k & 1))


def all_gather(name, a, out_dtype):
    def body(a_ref, o_ref, stage, send_sems, recv_sems, local_sem):
        me = _my_index()
        stage[...] = a_ref[...].astype(out_dtype)
        local = pltpu.make_async_copy(stage, o_ref.at[me], local_sem)
        local.start()
        sends = []
        for k in range(1, N_DEV):
            cp = pltpu.make_async_remote_copy(
                src_ref=stage, dst_ref=o_ref.at[me], send_sem=send_sems.at[k - 1], recv_sem=recv_sems.at[k - 1],
                device_id=_peer(k), device_id_type=pl.DeviceIdType.MESH)
            cp.start()
            sends.append(cp)
        for k in range(1, N_DEV):
            pltpu.make_async_remote_copy(
                src_ref=stage, dst_ref=o_ref.at[me ^ k], send_sem=send_sems.at[k - 1], recv_sem=recv_sems.at[k - 1],
                device_id=_peer(k), device_id_type=pl.DeviceIdType.MESH).wait_recv()
        for cp in sends:
            cp.wait_send()
        local.wait()

    return pl.pallas_call(
        body, name=name, in_specs=[pl.BlockSpec(memory_space=pltpu.VMEM)],
        out_specs=pl.BlockSpec(memory_space=pltpu.HBM),
        out_shape=jax.ShapeDtypeStruct((N_DEV,) + a.shape, out_dtype),
        scratch_shapes=[pltpu.VMEM(a.shape, out_dtype), pltpu.SemaphoreType.DMA((N_DEV - 1,)),
                        pltpu.SemaphoreType.DMA((N_DEV - 1,)), pltpu.SemaphoreType.DMA],
        compiler_params=pltpu.CompilerParams(vmem_limit_bytes=VMEM_LIMIT),
    )(a)


def exchange(name, g):
    def body(g_ref, r_ref, send_sems, recv_sems, local_sem):
        me = _my_index()
        local = pltpu.make_async_copy(g_ref.at[me], r_ref.at[me], local_sem)
        local.start()
        sends = []
        for k in range(1, N_DEV):
            cp = pltpu.make_async_remote_copy(
                src_ref=g_ref.at[me ^ k], dst_ref=r_ref.at[me], send_sem=send_sems.at[k - 1],
                recv_sem=recv_sems.at[k - 1], device_id=_peer(k), device_id_type=pl.DeviceIdType.MESH)
            cp.start()
            sends.append(cp)
        for k in range(1, N_DEV):
            pltpu.make_async_remote_copy(
                src_ref=g_ref.at[me], dst_ref=r_ref.at[me ^ k], send_sem=send_sems.at[k - 1],
                recv_sem=recv_sems.at[k - 1], device_id=_peer(k), device_id_type=pl.DeviceIdType.MESH).wait_recv()
        for cp in sends:
            cp.wait_send()
        local.wait()

    return pl.pallas_call(
        body, name=name, in_specs=[pl.BlockSpec(memory_space=pltpu.HBM)],
        out_specs=pl.BlockSpec(memory_space=pltpu.HBM),
        out_shape=jax.ShapeDtypeStruct(g.shape, g.dtype),
        scratch_shapes=[pltpu.SemaphoreType.DMA((N_DEV - 1,)), pltpu.SemaphoreType.DMA((N_DEV - 1,)),
                        pltpu.SemaphoreType.DMA],
    )(g)


_HBM = pl.BlockSpec(memory_space=pltpu.HBM)
_SEM = pl.BlockSpec(memory_space=pltpu.SEMAPHORE)
_DATAFLOW = pltpu.SideEffectType.DATAFLOW_SIDE_EFFECTING


def _split_copies(gather, src_ref, land_ref, send_sems, recv_sems, local_sem):
    me = _my_index()

    def part(j):
        return src_ref if gather else src_ref.at[j]

    local = pltpu.make_async_copy(part(me), land_ref.at[me], local_sem)
    sends = [pltpu.make_async_remote_copy(
        src_ref=part(me ^ k), dst_ref=land_ref.at[me], send_sem=send_sems.at[k - 1], recv_sem=recv_sems.at[k - 1],
        device_id=_peer(k), device_id_type=pl.DeviceIdType.MESH) for k in range(1, N_DEV)]
    recvs = [pltpu.make_async_remote_copy(
        src_ref=part(me ^ k), dst_ref=land_ref.at[me ^ k], send_sem=send_sems.at[k - 1], recv_sem=recv_sems.at[k - 1],
        device_id=_peer(k), device_id_type=pl.DeviceIdType.MESH) for k in range(1, N_DEV)]
    return local, sends, recvs


def split_start(name, src, gather):
    land_shape = ((N_DEV,) + src.shape) if gather else src.shape

    def body(src_ref, land_ref, send_sems, recv_sems, local_sem, src_thru, land_thru, token):
        local, sends, _ = _split_copies(gather, src_ref, land_ref, send_sems, recv_sems, local_sem)
        local.start()
        for cp in sends:
            cp.start()
        token[...] = jnp.zeros_like(token)

    dma7 = pltpu.SemaphoreType.DMA((N_DEV - 1,))
    out = pl.pallas_call(
        body, name=name,
        out_shape=(dma7, dma7, pltpu.SemaphoreType.DMA(()), pltpu.HBM(src.shape, src.dtype),
                   pltpu.HBM(land_shape, src.dtype), jax.ShapeDtypeStruct((8, 128), F32)),
        in_specs=(_HBM, _HBM), out_specs=(_SEM, _SEM, _SEM, _HBM, _HBM, pl.BlockSpec(memory_space=pltpu.VMEM)),
        input_output_aliases={0: 3, 1: 4},
        compiler_params=pltpu.CompilerParams(has_side_effects=_DATAFLOW),
    )(pltpu.with_memory_space_constraint(src, pltpu.HBM),
      pltpu.with_memory_space_constraint(lax.empty(land_shape, src.dtype), pltpu.HBM))
    return out[:5], out[5][0, 0]


def split_wait(name, handles, after, gather):
    send_sems, recv_sems, local_sem, src_thru, land_thru = handles

    def body(src_ref, land_ref, send_sems, recv_sems, local_sem, after_ref, src_dead, got_ref, token):
        local, sends, recvs = _split_copies(gather, src_ref, land_ref, send_sems, recv_sems, local_sem)
        local.wait()
        for cp in recvs:
            cp.wait_send()
            cp.wait_recv()
        token[...] = jnp.zeros_like(token)

    out = pl.pallas_call(
        body, name=name,
        out_shape=(pltpu.HBM(src_thru.shape, src_thru.dtype), pltpu.HBM(land_thru.shape, land_thru.dtype),
                   jax.ShapeDtypeStruct((8, 128), F32)),
        in_specs=(_HBM, _HBM, _SEM, _SEM, _SEM, pl.BlockSpec(memory_space=pl.ANY)),
        out_specs=(_HBM, _HBM, pl.BlockSpec(memory_space=pltpu.VMEM)),
        input_output_aliases={0: 0, 1: 1},
        compiler_params=pltpu.CompilerParams(has_side_effects=_DATAFLOW),
    )(src_thru, land_thru, send_sems, recv_sems, local_sem, after)
    return out[1], out[2][0, 0]


def sum_parts(name, r):
    _, rows, cols = r.shape

    def body(r_ref, o_ref):
        acc = r_ref[0].astype(F32)
        for s in range(1, N_DEV):
            acc = acc + r_ref[s].astype(F32)
        o_ref[...] = acc

    return pl.pallas_call(body, name=name, out_shape=jax.ShapeDtypeStruct((rows, cols), F32),
                          compiler_params=_params())(r)


def adamw(name, w, m, v, parts=None, g=None, layer=0, into=None):
    _, rows, cols = w.shape
    br = _tile(rows, 256, 16)
    c1 = 1.0 / (1.0 - ADAM_B1 ** ADAM_STEP)
    c2 = 1.0 / (1.0 - ADAM_B2 ** ADAM_STEP)

    def body(g_ref, w_ref, m_ref, v_ref, *rest):
        og_ref, od_ref, om_ref, ov_ref = rest[-4:]
        if parts is None:
            gs = g_ref[...]
        else:
            gs = g_ref[0].astype(F32)
            for s in range(1, N_DEV):
                gs = gs + g_ref[s].astype(F32)
        mn = ADAM_B1 * m_ref[...] + (1.0 - ADAM_B1) * gs
        vn = ADAM_B2 * v_ref[...] + (1.0 - ADAM_B2) * (gs * gs)
        og_ref[...] = gs
        om_ref[...] = mn
        ov_ref[...] = vn
        od_ref[...] = -ADAM_LR * ((mn * c1) / (jnp.sqrt(vn * c2) + ADAM_EPS) + ADAM_WD * w_ref[...])

    blk = pl.BlockSpec((None, br, cols), lambda i: (layer, i, 0))
    if parts is None:
        gspec = pl.BlockSpec((br, cols), lambda i: (i, 0))
    else:
        gspec = pl.BlockSpec((N_DEV, br, cols), lambda i: (0, i, 0))
    earlier = [] if into is None else list(into)
    return pl.pallas_call(
        body, name=name, grid=(rows // br,),
        in_specs=[gspec, blk, blk, blk] + [pl.BlockSpec(memory_space=pl.ANY)] * len(earlier),
        out_specs=[blk] * 4, out_shape=[jax.ShapeDtypeStruct(w.shape, F32)] * 4,
        input_output_aliases={4 + k: k for k in range(len(earlier))},
        compiler_params=_params(("parallel",)),
    )(g if parts is None else parts, w, m, v, *earlier)


SMALL = ("norm_mix", "norm_xattn", "norm_ffn", "norm_mem", "norm_final", "pool_w", "pool_scale",
         "ssm_lam_re", "ssm_lam_im", "ssm_log_dt", "ssm_b_re", "ssm_b_im", "ssm_c_re", "ssm_c_im",
         "ffn_conv_b", "ssm_d", "ffn_conv_w")
SMALL_SHARDED = {"ssm_d": 1, "ffn_conv_w": 2}
BIG = ("ab_w_in", "ab_w_out", "ssm_w_in", "ssm_w_glu", "xa_w_q", "xa_w_kv", "xa_w_o", "ffn_w_up", "ffn_w_down")
WEIGHTS = ("norm_mix", "norm_xattn", "norm_ffn", "norm_mem", "norm_final", "ab_w_in", "pool_w", "pool_scale",
           "ab_w_out", "ssm_w_in", "ssm_lam_re", "ssm_lam_im", "ssm_log_dt", "ssm_b_re", "ssm_b_im", "ssm_c_re",
           "ssm_c_im", "ssm_d", "ssm_w_glu", "xa_w_q", "xa_w_kv", "xa_w_o", "ffn_w_up", "ffn_conv_w", "ffn_conv_b",
           "ffn_w_down")


def _rows8(g):
    return g.reshape(N_DEV, g.size // (N_DEV * D_MODEL), D_MODEL)


def _square(a):
    return a.reshape(D_MODEL, D_MODEL)


_LAYOUT = {"ab_w_out": _square, "ssm_w_in": _square, "xa_w_q": _square, "xa_w_o": _square,
           "ffn_w_down": lambda a: a.reshape(N_DEV // 2, FF_SHARD, D_MODEL)}
GATHER_ORDER = (("ab_w_in", 0), ("ab_w_out", 0), ("xa_w_q", 0), ("xa_w_kv", 0), ("xa_w_o", 0), ("ffn_w_up", 0),
                ("ffn_w_down", 0), ("ffn_w_up", 1), ("ffn_w_down", 1), ("ssm_w_in", 0), ("ssm_w_glu", 0),
                ("xa_w_q", 1), ("xa_w_kv", 1), ("xa_w_o", 1))
GATHER_AHEAD = 5


class _Step:
    def __init__(self, master, small):
        self.master, self.small = master, small
        self.pending, self.gathers, self.weights, self.sent = [], {}, {}, []

    def follow(self, v):
        for z in self.pending:
            v = v + z
        self.pending = []
        return v

    def start_gathers(self, upto, zero):
        for n, l in GATHER_ORDER[len(self.gathers):upto]:
            shard = (self.master[n][l] + zero).astype(MXU_DTYPE)
            self.gathers[(n, l)], z = split_start(f"ags_{n}{l}", shard, gather=True)
            self.pending.append(z)

    def weight(self, n, l, after):
        if (n, l) not in self.weights:
            full, z = split_wait(f"agw_{n}{l}", self.gathers[(n, l)], after, gather=True)
            self.weights[(n, l)] = _LAYOUT.get(n, lambda a: a)(full)
            self.start_gathers(GATHER_ORDER.index((n, l)) + 1 + GATHER_AHEAD, z)
        return self.weights[(n, l)]

    def send_grad(self, n, l, part):
        h, z = split_start(f"xs_{n}{l}", part, gather=False)
        self.pending.append(z)
        self.sent.append((n, l, h))


def _layer_tail(st, l, x_in, hq, mem_n, acts, next_gain=None):
    bsz, seq = acts["bsz"], acts["seq"]
    p = st.small
    q = mm_nn(f"xa_q{l}", hq, st.weight("xa_w_q", l, x_in))
    kv = mm_nn_bs(f"xa_kv{l}", mem_n, st.weight("xa_w_kv", l, x_in))
    o = xattn_fwd(q, kv, bsz, seq)
    x_mid, hf = mm_nn(f"xa_o{l}", o, st.weight("xa_w_o", l, o), res=x_in, out_dtype=F32,
                      norm_gain=st.follow(p["norm_ffn"][l]))
    up = mm_nn_bs(f"ffn_up{l}", hf, st.weight("ffn_w_up", l, x_mid), stacked_out=True)
    act = conv_fwd(up, p["ffn_conv_w"][l], p["ffn_conv_b"][l], bsz, seq)
    w_down = st.weight("ffn_w_down", l, act)
    if next_gain is None:
        x_out, h_next = mm_as_nn(f"ffn_down{l}", act, w_down, res=x_mid), None
    else:
        x_out, h_next = mm_as_nn(f"ffn_down{l}", act, w_down, res=x_mid, norm_gain=st.follow(next_gain))
    acts[l].update(x_in=x_in, hq=hq, q=q, kv=kv, o=o, x_mid=x_mid, hf=hf, up=up, act=act)
    return x_out, h_next


def _layer_tail_bwd(st, l, dx, mem_n, acts, grads):
    a = acts[l]
    bsz, seq = acts["bsz"], acts["seq"]
    p = st.small
    dact = mm_nt_os(f"d_act{l}", dx, st.weight("ffn_w_down", l, dx))
    st.send_grad("ffn_w_down", l, _rows8(mm_tn(f"g_ffn_down{l}", a["act"], dx, a_stacked=True)))
    dconv, dcw, dcb = conv_bwd_taps(a["up"], p["ffn_conv_w"][l], p["ffn_conv_b"][l], dact, bsz, seq)
    grads["ffn_conv_w"][l] = dcw
    grads["ffn_conv_b"][l] = dcb
    dup = conv_bwd_input(dconv, p["ffn_conv_w"][l], bsz, seq)
    dx_mid, grads["norm_ffn"][l] = mm_nt_bs(f"d_hf{l}", dup, st.weight("ffn_w_up", l, dx), dc_stacked=True,
                                            rms=(a["x_mid"], st.follow(p["norm_ffn"][l]), dx))
    st.send_grad("ffn_w_up", l, mm_tn(f"g_ffn_up{l}", a["hf"], dup, dc_stacked=True))
    do = mm_nt(f"d_o{l}", dx_mid, st.weight("xa_w_o", l, dx))
    st.send_grad("xa_w_o", l, _rows8(mm_tn(f"g_xa_o{l}", a["o"], dx_mid)))
    dq, dk, dv = xattn_bwd(a["q"], a["kv"], do, bsz, seq)
    dkv = jnp.concatenate([dk, dv], axis=1).astype(BF16)
    dx_in, grads["norm_xattn"][l] = mm_nt(f"d_hq{l}", dq, st.weight("xa_w_q", l, dx),
                                          rms=(a["x_in"], st.follow(p["norm_xattn"][l]), dx_mid))
    st.send_grad("xa_w_q", l, _rows8(mm_tn(f"g_xa_q{l}", a["hq"], dq)))
    dmem_n = mm_nt_bs(f"d_memn{l}", dkv, st.weight("xa_w_kv", l, dx), out_dtype=F32)
    st.send_grad("xa_w_kv", l, mm_tn(f"g_xa_kv{l}", mem_n, dkv, dc_cols=2 * D_MODEL // N_DEV))
    return dx_in, dmem_n


def kernel(x, mem, norm_mix, norm_xattn, norm_ffn, norm_mem, norm_final, ab_w_in, pool_w, pool_scale, ab_w_out, ssm_w_in, ssm_lam_re, ssm_lam_im, ssm_log_dt, ssm_b_re, ssm_b_im, ssm_c_re, ssm_c_im, ssm_d, ssm_w_glu, xa_w_q, xa_w_kv, xa_w_o, ffn_w_up, ffn_conv_w, ffn_conv_b, ffn_w_down, loss_target, m_norm_mix, m_norm_xattn, m_norm_ffn, m_norm_mem, m_norm_final, m_ab_w_in, m_pool_w, m_pool_scale, m_ab_w_out, m_ssm_w_in, m_ssm_lam_re, m_ssm_lam_im, m_ssm_log_dt, m_ssm_b_re, m_ssm_b_im, m_ssm_c_re, m_ssm_c_im, m_ssm_d, m_ssm_w_glu, m_xa_w_q, m_xa_w_kv, m_xa_w_o, m_ffn_w_up, m_ffn_conv_w, m_ffn_conv_b, m_ffn_w_down, v_norm_mix, v_norm_xattn, v_norm_ffn, v_norm_mem, v_norm_final, v_ab_w_in, v_pool_w, v_pool_scale, v_ab_w_out, v_ssm_w_in, v_ssm_lam_re, v_ssm_lam_im, v_ssm_log_dt, v_ssm_b_re, v_ssm_b_im, v_ssm_c_re, v_ssm_c_im, v_ssm_d, v_ssm_w_glu, v_xa_w_q, v_xa_w_kv, v_xa_w_o, v_ffn_w_up, v_ffn_conv_w, v_ffn_conv_b, v_ffn_w_down):
    given = dict(locals())
    master = {n: given[n] for n in WEIGHTS}
    mom1 = {n: given["m_" + n] for n in WEIGHTS}
    mom2 = {n: given["v_" + n] for n in WEIGHTS}
    bsz, seq, d = x.shape
    t = bsz * seq
    me = _my_index()

    st = _Step(master, {"norm_xattn": norm_xattn, "norm_ffn": norm_ffn,
                        "ffn_conv_b": [ffn_conv_b[l].reshape(N_DEV, 1, FF_SHARD) for l in range(2)]})
    st.start_gathers(1, 0.0)
    zero = st.follow(jnp.zeros((), F32))
    conv_w_st = all_gather("ag_ffn_conv_w", ffn_conv_w + zero, F32)
    st.small["ffn_conv_w"] = [conv_w_st[:, l] for l in range(2)]
    dskip = all_gather("ag_ssm_d", ssm_d.reshape(1, 128) + zero, F32).reshape(1, D_MODEL)

    acts = {"bsz": bsz, "seq": seq, 0: {}, 1: {}}
    x0 = x.reshape(t, d)
    mem2 = mem.reshape(bsz * MEM_LEN, d)
    mem_n = rms_fwd("rms_mem", mem2, norm_mem + zero)
    pscale = pool_scale.reshape(1, SB_WIDTH)

    h0 = rms_fwd("rms_mix0", x0, norm_mix[0] + zero)
    w_in = st.weight("ab_w_in", 0, h0)
    proj = mm_nn_bs("ab_in", h0, w_in, out_dtype=F32)
    a_out, rsum = sb_attn_fwd(proj, st.follow(jnp.zeros((1, 128), F32)), bsz, seq)
    p_out = pool_fwd(proj, pool_w[0], pscale, bsz, seq)
    w_out = st.weight("ab_w_out", 0, a_out)
    x1 = mm_nn("ab_out_a", a_out, w_out, res=x0, out_dtype=F32)
    x1, hq0 = mm_nn("ab_out_p", p_out, w_out, res=x1, koff=SB_WIDTH, out_dtype=F32,
                    norm_gain=st.follow(norm_xattn[0]))
    x3, h1 = _layer_tail(st, 0, x1, hq0, mem_n, acts, next_gain=norm_mix[1])

    b_re2 = ssm_b_re.reshape(64, 1024)
    b_im2 = ssm_b_im.reshape(64, 1024)
    log_dt = ssm_log_dt.reshape(64, 1)
    lb_re, lb_im, bb_re2, bb_im2 = ssm_prep(ssm_lam_re[0], ssm_lam_im[0], log_dt, b_re2, b_im2)
    wt = _ssm_in_weights(bb_re2, bb_im2)
    ct = _ssm_out_weights(ssm_c_re[0], ssm_c_im[0])
    a_re = lb_re.reshape(1, SSM_STATES)
    a_im = lb_im.reshape(1, SSM_STATES)
    u = mm_nn("ssm_in", h1, st.weight("ssm_w_in", 0, x3), out_dtype=F32)
    y, gl, h_re, h_im = ssm_fwd(u, wt, ct, a_re, a_im, dskip, bsz, seq)
    glu = mm_nn_bs("ssm_glu", gl, st.weight("ssm_w_glu", 0, gl), out_dtype=F32)
    x4, hq1 = glu_fwd(glu, x3, st.follow(norm_xattn[1]))
    x6, _ = _layer_tail(st, 1, x4, hq1, mem_n, acts)

    loss_row, dx, g_norm_final = loss_head(x6, norm_final, loss_target.reshape(t, d))
    loss = lax.psum(loss_row[0, 0], MESH_AXES)

    grads = {n: [None, None] for n in ("ffn_conv_w", "ffn_conv_b", "norm_ffn", "norm_xattn", "norm_mix")}
    dx4, dmem_1 = _layer_tail_bwd(st, 1, dx, mem_n, acts, grads)
    dglu = glu_bwd(glu, dx4)
    dgl = mm_nt_bs("d_gl", dglu, st.weight("ssm_w_glu", 0, dx))
    st.send_grad("ssm_w_glu", 0, mm_tn("g_ssm_glu", gl, dglu, dc_cols=2 * D_MODEL // N_DEV))
    du, dwt, dct, g_dskip, da_re, da_im = ssm_bwd(dgl, y, u, h_re, h_im, wt, ct, a_re, a_im, dskip, bsz, seq)
    dbb_re, dbb_im = _ssm_in_weights_bwd(dwt)
    g_c_re, g_c_im = _ssm_out_weights_bwd(dct)
    g_lam_re, g_lam_im, g_log_dt, g_b_re, g_b_im = ssm_prep_bwd(
        ssm_lam_re[0], ssm_lam_im[0], log_dt, b_re2, b_im2, da_re.reshape(64, 64), da_im.reshape(64, 64),
        dbb_re, dbb_im)
    dx3, grads["norm_mix"][1] = mm_nt("d_h1", du, st.weight("ssm_w_in", 0, dx),
                                      rms=(x3, st.follow(norm_mix[1]), dx4))
    st.send_grad("ssm_w_in", 0, _rows8(mm_tn("g_ssm_in", h1, du)))

    dx1, dmem_0 = _layer_tail_bwd(st, 0, dx3, mem_n, acts, grads)
    dcat = mm_nt("d_cat", dx1, st.weight("ab_w_out", 0, dx))
    st.send_grad("ab_w_out", 0, _rows8(jnp.concatenate(
        [mm_tn("g_ab_out_a", a_out, dx1), mm_tn("g_ab_out_p", p_out, dx1)], axis=0)))
    dq, dk, dv = sb_attn_bwd(proj, rsum, dcat, bsz, seq)
    dpu, g_pool_w, g_pool_scale = pool_bwd(proj, pool_w[0], st.follow(pscale), dcat, bsz, seq)
    dproj = jnp.concatenate([dq, dk, dv, dpu], axis=1).astype(BF16)
    st.send_grad("ab_w_in", 0, mm_tn("g_ab_in", h0, dproj, dc_cols=2 * D_MODEL // N_DEV))
    dx0, grads["norm_mix"][0] = mm_nt_bs("d_h0", dproj, st.weight("ab_w_in", 0, dx),
                                         rms=(x0, st.follow(norm_mix[0]), dx1))
    _, g_norm_mem = rms_bwd("rms_mem_bwd", mem2, norm_mem, dmem_0 + dmem_1, need_dx=False)

    stepped = {}
    for n, l, handles in st.sent:
        recv, _ = split_wait(f"xw_{n}{l}", handles, dx0, gather=False)
        shape3 = (master[n].shape[0],) + recv.shape[1:]
        stepped[n] = adamw(f"adamw_{n}{l}", master[n].reshape(shape3), mom1[n].reshape(shape3),
                           mom2[n].reshape(shape3), parts=recv, layer=l, into=stepped.get(n))
    out_g, out_d, out_m, out_v = ({n: stepped[n][k].reshape(master[n].shape) for n in BIG} for k in range(4))

    small_g = {
        "norm_mix": jnp.stack([g[0] for g in grads["norm_mix"]]),
        "norm_xattn": jnp.stack([g[0] for g in grads["norm_xattn"]]),
        "norm_ffn": jnp.stack([g[0] for g in grads["norm_ffn"]]),
        "norm_mem": g_norm_mem[0], "norm_final": g_norm_final[0],
        "pool_w": g_pool_w[None], "pool_scale": g_pool_scale,
        "ssm_lam_re": g_lam_re[None], "ssm_lam_im": g_lam_im[None], "ssm_log_dt": g_log_dt.reshape(1, 64),
        "ssm_b_re": g_b_re.reshape(1, 64, 64, 16), "ssm_b_im": g_b_im.reshape(1, 64, 64, 16),
        "ssm_c_re": g_c_re[None], "ssm_c_im": g_c_im[None],
        "ffn_conv_b": jnp.stack([g.reshape(2 * D_FF) for g in grads["ffn_conv_b"]]),
        "ssm_d": g_dskip,
        "ffn_conv_w": jnp.stack([g.transpose(1, 0, 2).reshape(3, 2 * D_FF) for g in grads["ffn_conv_w"]]),
    }
    sizes = [int(small_g[n].size) for n in SMALL]
    total = sum(sizes)
    rows8 = -(-total // (N_DEV * 128 * 8)) * 8
    flat = jnp.concatenate([small_g[n].reshape(-1).astype(F32) for n in SMALL]
                           + [jnp.zeros((N_DEV * rows8 * 128 - total,), F32)])
    recv = exchange("xch_small", flat.reshape(N_DEV, rows8, 128))
    summed = all_gather("ag_small", sum_parts("sum_small", recv), F32).reshape(-1)

    def local_part(name, a):
        ax = SMALL_SHARDED.get(name)
        if ax is None:
            return a
        n_loc = a.shape[ax] // N_DEV
        return lax.dynamic_slice_in_dim(a, me * n_loc, n_loc, axis=ax)

    off = 0
    for n, sz in zip(SMALL, sizes):
        g_n = local_part(n, summed[off:off + sz].reshape(small_g[n].shape))
        off += sz
        shape3 = (1, g_n.size // g_n.shape[-1], g_n.shape[-1])
        res = adamw("adamw_" + n, master[n].reshape(shape3), mom1[n].reshape(shape3), mom2[n].reshape(shape3),
                    g=g_n.reshape(shape3[1:]))
        for dst, r in zip((out_g, out_d, out_m, out_v), res):
            dst[n] = r.reshape(master[n].shape)

    return (loss, dx0.reshape(bsz, seq, d), *[out_g[n] for n in WEIGHTS], *[out_d[n] for n in WEIGHTS],
            *[out_m[n] for n in WEIGHTS], *[out_v[n] for n in WEIGHTS])
```

```python
import functools
import math

import jax
import jax.numpy as jnp
from jax import lax
from jax.experimental import pallas as pl
from jax.experimental.pallas import tpu as pltpu

F32 = jnp.float32
BF16 = jnp.bfloat16
MXU_DTYPE = jnp.bfloat16
N_DEV = 8
MESH_AXES = ("x", "y", "c")

D_MODEL = 1024
SB_HEAD_DIM = 64
SB_WIDTH = 512
SB_BLOCK = 256
POOL_WINDOWS = (2, 4, 8, 16)
POOL_GROUP = 128
POOL_HALO = 16
SSM_TILES = 8
SSM_TILE_STATES = 512
SSM_STATES = 4096
SSM_LANES = 1024
MEM_LEN = 256
XA_HEADS = 4
XA_HEAD_DIM = 256
D_FF = 2816
FF_SHARD = 704
EPS = 1e-6
ADAM_LR = 0.001
ADAM_B1 = 0.9
ADAM_B2 = 0.999
ADAM_EPS = 1e-08
ADAM_WD = 0.01
ADAM_STEP = 10
VMEM_LIMIT = 56 * 1024 * 1024

_NN = (((1,), (0,)), ((), ()))
_NT = (((1,), (1,)), ((), ()))
_TN = (((0,), (0,)), ((), ()))


def _params(sem=None):
    if sem is None:
        return pltpu.CompilerParams(vmem_limit_bytes=VMEM_LIMIT)
    return pltpu.CompilerParams(dimension_semantics=sem, vmem_limit_bytes=VMEM_LIMIT)


def _tile(n, pref, mult=8):
    if n <= pref:
        return n
    for t in range(pref, 0, -1):
        if n % t == 0 and t % mult == 0:
            return t
    return n


def _dot(a, b, dims):
    return lax.dot_general(a.astype(MXU_DTYPE), b.astype(MXU_DTYPE), dims, preferred_element_type=F32)


def _dot_exact01(x, m01, dims=_NN):
    x1 = x.astype(BF16)
    r1 = x - x1.astype(F32)
    x2 = r1.astype(BF16)
    x3 = (r1 - x2.astype(F32)).astype(BF16)
    m = m01.astype(BF16)
    out = lax.dot_general(x1, m, dims, preferred_element_type=F32)
    out = out + lax.dot_general(x2, m, dims, preferred_element_type=F32)
    return out + lax.dot_general(x3, m, dims, preferred_element_type=F32)


def _mm(name, a, b, dims, grid, a_spec, b_spec, o_spec, out_shape, out_dtype, acc_shape, res=None, r_spec=None,
        group=1, n=None, a_sel="full", b_sel="full", o_sel="full", norm_gain=None, rms=None):
    nk = grid[2]
    if out_dtype is None:
        out_dtype = BF16
    n_out = out_shape[-1]
    vec = pl.BlockSpec((1, n_out), lambda i, j, kk: (0, 0))

    def at(sel, s):
        if sel == "lead":
            return (s,)
        if sel == "lanes":
            return (slice(None), slice(s * n, (s + 1) * n))
        return (Ellipsis,)

    extra = [] if res is None else [(res, r_spec)]
    if norm_gain is not None:
        extra.append((norm_gain.reshape(1, n_out), vec))
    if rms is not None:
        extra += [(rms[0], o_spec), (rms[1].reshape(1, n_out), vec), (rms[2], o_spec)]
    n_in = 2 + len(extra)
    if rms is not None:
        out_specs = [o_spec, vec]
        out_shapes = [jax.ShapeDtypeStruct(out_shape, F32), jax.ShapeDtypeStruct((1, n_out), F32)]
    elif norm_gain is not None:
        out_specs = [o_spec, o_spec]
        out_shapes = [jax.ShapeDtypeStruct(out_shape, out_dtype), jax.ShapeDtypeStruct(out_shape, BF16)]
    else:
        out_specs, out_shapes = o_spec, jax.ShapeDtypeStruct(out_shape, out_dtype)

    def body(*refs):
        a_ref, b_ref = refs[0], refs[1]
        ins = list(refs[2:n_in])
        r_ref = ins.pop(0) if res is not None else None
        outs = refs[n_in:]
        o_ref = outs[0]
        acc = refs[-1] if nk > 1 else None
        k = pl.program_id(2)

        def finish(val):
            if r_ref is not None:
                val = val + r_ref[...].astype(F32)
            if rms is not None:
                x_ref, g_ref, d_ref = ins
                xf = x_ref[...]
                r = lax.rsqrt(jnp.mean(xf * xf, axis=-1, keepdims=True) + EPS)
                xh = xf * r
                part = jnp.sum(val * xh, axis=0, keepdims=True)
                first = pl.program_id(0) == 0

                @pl.when(first)
                def _():
                    outs[1][...] = part

                @pl.when(jnp.logical_not(first))
                def _():
                    outs[1][...] += part

                dxh = val * g_ref[...]
                o_ref[...] = d_ref[...] + r * (dxh - xh * jnp.mean(dxh * xh, axis=-1, keepdims=True))
                return
            o_ref[...] = val.astype(out_dtype)
            if norm_gain is not None:
                r = lax.rsqrt(jnp.mean(val * val, axis=-1, keepdims=True) + EPS)
                outs[1][...] = (val * r * ins[0][...]).astype(BF16)

        def emit(s, val):
            if nk == 1:
                if o_sel == "full":
                    finish(val)
                else:
                    o_ref[at(o_sel, s)] = val.astype(out_dtype)
                return

            @pl.when(k == 0)
            def _():
                acc[at(o_sel, s)] = val

            @pl.when(k > 0)
            def _():
                acc[at(o_sel, s)] += val

        total = None
        for s in range(group):
            val = _dot(a_ref[at(a_sel, s)], b_ref[at(b_sel, s)], dims)
            if o_sel == "full":
                total = val if total is None else total + val
            else:
                emit(s, val)
        if o_sel == "full":
            emit(0, total)
        if nk > 1:
            @pl.when(k == nk - 1)
            def _():
                if o_sel == "full":
                    finish(acc[...])
                else:
                    o_ref[...] = acc[...].astype(out_dtype)

    rows_sem = "arbitrary" if rms is not None else "parallel"
    return pl.pallas_call(
        body, name=name, grid=grid, in_specs=[a_spec, b_spec] + [s for _, s in extra], out_specs=out_specs,
        out_shape=out_shapes, scratch_shapes=[pltpu.VMEM(acc_shape, F32)] if nk > 1 else [],
        compiler_params=_params((rows_sem, rows_sem, "arbitrary")),
    )(a, b, *[x for x, _ in extra])


def _row_tile(m, epi):
    return _tile(m, 512 if epi.get("rms") is not None else 1024)


def mm_nn(name, a, b, res=None, koff=0, out_dtype=None, **epi):
    m, k = a.shape
    n = b.shape[1]
    tm, tn, tk = _row_tile(m, epi), _tile(n, 1024, 128), _tile(k, 1024, 128)
    kb = koff // tk
    spec = pl.BlockSpec((tm, tn), lambda i, j, kk: (i, j))
    return _mm(name, a, b, _NN, (m // tm, n // tn, k // tk),
               pl.BlockSpec((tm, tk), lambda i, j, kk: (i, kk)),
               pl.BlockSpec((tk, tn), lambda i, j, kk: (kk + kb, j)),
               spec, (m, n), out_dtype, (tm, tn), res, spec, **epi)


def mm_nn_bs(name, a, bs, stacked_out=False, out_dtype=None):
    m, k = a.shape
    s, _, n = bs.shape
    tm, tk = _tile(m, 1024), _tile(k, 1024, 128)
    a_spec = pl.BlockSpec((tm, tk), lambda i, j, kk: (i, kk))
    if stacked_out:
        return _mm(name, a, bs, _NN, (m // tm, s, k // tk), a_spec,
                   pl.BlockSpec((None, tk, n), lambda i, j, kk: (j, kk, 0)),
                   pl.BlockSpec((None, tm, n), lambda i, j, kk: (j, i, 0)), (s, m, n), out_dtype, (tm, n))
    g = _tile(s, max(1, 1024 // n), 1)
    return _mm(name, a, bs, _NN, (m // tm, s // g, k // tk), a_spec,
               pl.BlockSpec((g, tk, n), lambda i, j, kk: (j, kk, 0)),
               pl.BlockSpec((tm, g * n), lambda i, j, kk: (i, j)), (m, s * n), out_dtype, (tm, g * n),
               group=g, n=n, b_sel="lead", o_sel="lanes")


def mm_as_nn(name, a_st, b3, res, out_dtype=F32, **epi):
    s, m, kp = a_st.shape
    n = b3.shape[2]
    tm, tn = _row_tile(m, epi), _tile(n, 1024, 128)
    spec = pl.BlockSpec((tm, tn), lambda i, j, kk: (i, j))
    g = _tile(s, 2, 1)
    return _mm(name, a_st, b3, _NN, (m // tm, n // tn, s // g),
               pl.BlockSpec((g, tm, kp), lambda i, j, kk: (kk, i, 0)),
               pl.BlockSpec((g, kp, tn), lambda i, j, kk: (kk, 0, j)),
               spec, (m, n), out_dtype, (tm, tn), res, spec, group=g, a_sel="lead", b_sel="lead", **epi)


def mm_nt(name, dc, b, out_dtype=None, **epi):
    m, n = dc.shape
    k = b.shape[0]
    tm, tko, tnr = _row_tile(m, epi), _tile(k, 1024, 128), _tile(n, 1024, 128)
    return _mm(name, dc, b, _NT, (m // tm, k // tko, n // tnr),
               pl.BlockSpec((tm, tnr), lambda i, j, kk: (i, kk)),
               pl.BlockSpec((tko, tnr), lambda i, j, kk: (j, kk)),
               pl.BlockSpec((tm, tko), lambda i, j, kk: (i, j)), (m, k), out_dtype, (tm, tko), **epi)


def mm_nt_bs(name, dc, bs, dc_stacked=False, out_dtype=None, **epi):
    s, k, n = bs.shape
    m = dc.shape[1] if dc_stacked else dc.shape[0]
    tm, tko = (_tile(m, 1024) if dc_stacked else _row_tile(m, epi)), _tile(k, 1024, 128)
    o_spec = pl.BlockSpec((tm, tko), lambda i, j, kk: (i, j))
    if dc_stacked:
        g = _tile(s, 2, 1)
        return _mm(name, dc, bs, _NT, (m // tm, k // tko, s // g),
                   pl.BlockSpec((g, tm, n), lambda i, j, kk: (kk, i, 0)),
                   pl.BlockSpec((g, tko, n), lambda i, j, kk: (kk, j, 0)), o_spec, (m, k), out_dtype, (tm, tko),
                   group=g, a_sel="lead", b_sel="lead", **epi)
    g = _tile(s, max(1, 2048 // n), 1)
    return _mm(name, dc, bs, _NT, (m // tm, k // tko, s // g),
               pl.BlockSpec((tm, g * n), lambda i, j, kk: (i, kk)),
               pl.BlockSpec((g, tko, n), lambda i, j, kk: (kk, j, 0)), o_spec, (m, k), out_dtype, (tm, tko),
               group=g, n=n, a_sel="lanes", b_sel="lead", **epi)


def mm_nt_os(name, dc, b3, out_dtype=None):
    m, n = dc.shape
    s, kp, _ = b3.shape
    tm, tnr = _tile(m, 1024), _tile(n, 1024, 128)
    return _mm(name, dc, b3, _NT, (m // tm, s, n // tnr),
               pl.BlockSpec((tm, tnr), lambda i, j, kk: (i, kk)),
               pl.BlockSpec((None, kp, tnr), lambda i, j, kk: (j, 0, kk)),
               pl.BlockSpec((None, tm, kp), lambda i, j, kk: (j, i, 0)), (s, m, kp), out_dtype, (tm, kp))


def mm_tn(name, a, dc, a_stacked=False, dc_cols=None, dc_stacked=False, out_dtype=None):
    if a_stacked:
        s, m, kp = a.shape
        n = dc.shape[1]
        tno, tmr = _tile(n, 1024, 128), _tile(m, 2048)
        return _mm(name, a, dc, _TN, (s, n // tno, m // tmr),
                   pl.BlockSpec((None, tmr, kp), lambda i, j, kk: (i, kk, 0)),
                   pl.BlockSpec((tmr, tno), lambda i, j, kk: (kk, j)),
                   pl.BlockSpec((None, kp, tno), lambda i, j, kk: (i, 0, j)), (s, kp, n), out_dtype, (kp, tno))
    m, k = a.shape
    tko, tmr = _tile(k, 1024, 128), _tile(m, 2048)
    a_spec = pl.BlockSpec((tmr, tko), lambda i, j, kk: (kk, i))
    if dc_stacked:
        s, _, n = dc.shape
        return _mm(name, a, dc, _TN, (k // tko, s, m // tmr), a_spec,
                   pl.BlockSpec((None, tmr, n), lambda i, j, kk: (j, kk, 0)),
                   pl.BlockSpec((None, tko, n), lambda i, j, kk: (j, i, 0)), (s, k, n), out_dtype, (tko, n))
    if dc_cols is not None:
        n = dc_cols
        s = dc.shape[1] // n
        g = _tile(s, max(1, 1024 // n), 1)
        return _mm(name, a, dc, _TN, (k // tko, s // g, m // tmr), a_spec,
                   pl.BlockSpec((tmr, g * n), lambda i, j, kk: (kk, j)),
                   pl.BlockSpec((g, tko, n), lambda i, j, kk: (j, i, 0)), (s, k, n), out_dtype, (g, tko, n),
                   group=g, n=n, b_sel="lanes", o_sel="lead")
    n = dc.shape[1]
    tno = _tile(n, 1024, 128)
    return _mm(name, a, dc, _TN, (k // tko, n // tno, m // tmr), a_spec,
               pl.BlockSpec((tmr, tno), lambda i, j, kk: (kk, j)),
               pl.BlockSpec((tko, tno), lambda i, j, kk: (i, j)), (k, n), out_dtype, (tko, tno))


def rms_fwd(name, x, g):
    t, d = x.shape
    tr = _tile(t, 512)

    def body(x_ref, g_ref, o_ref):
        xf = x_ref[...]
        r = lax.rsqrt(jnp.mean(xf * xf, axis=-1, keepdims=True) + EPS)
        o_ref[...] = (xf * r * g_ref[...]).astype(o_ref.dtype)

    return pl.pallas_call(
        body, name=name, grid=(t // tr,),
        in_specs=[pl.BlockSpec((tr, d), lambda i: (i, 0)), pl.BlockSpec((1, d), lambda i: (0, 0))],
        out_specs=pl.BlockSpec((tr, d), lambda i: (i, 0)),
        out_shape=jax.ShapeDtypeStruct((t, d), BF16), compiler_params=_params(("parallel",)),
    )(x, g.reshape(1, d))


def rms_bwd(name, x, g, dh, dres=None, need_dx=True):
    t, d = x.shape
    tr = _tile(t, 512)

    def body(*refs):
        refs = list(refs)
        x_ref, g_ref, dh_ref = refs[:3]
        r_ref = refs[3] if dres is not None else None
        outs = refs[4:] if dres is not None else refs[3:]
        dx_ref, dg_ref = (outs[0], outs[1]) if need_dx else (None, outs[0])
        i = pl.program_id(0)

        @pl.when(i == 0)
        def _():
            dg_ref[...] = jnp.zeros_like(dg_ref)

        xf = x_ref[...]
        dhf = dh_ref[...].astype(F32)
        r = lax.rsqrt(jnp.mean(xf * xf, axis=-1, keepdims=True) + EPS)
        xh = xf * r
        dg_ref[...] += jnp.sum(dhf * xh, axis=0, keepdims=True)
        if need_dx:
            dxh = dhf * g_ref[...]
            dx = r * (dxh - xh * jnp.mean(dxh * xh, axis=-1, keepdims=True))
            if r_ref is not None:
                dx = dx + r_ref[...]
            dx_ref[...] = dx

    row = pl.BlockSpec((tr, d), lambda i: (i, 0))
    vec = pl.BlockSpec((1, d), lambda i: (0, 0))
    in_specs = [row, vec, row] + ([row] if dres is not None else [])
    args = (x, g.reshape(1, d), dh) + ((dres,) if dres is not None else ())
    out_specs = ([row] if need_dx else []) + [vec]
    out_shape = ([jax.ShapeDtypeStruct((t, d), F32)] if need_dx else []) + [jax.ShapeDtypeStruct((1, d), F32)]
    res = pl.pallas_call(
        body, name=name, grid=(t // tr,), in_specs=in_specs, out_specs=out_specs, out_shape=out_shape,
        compiler_params=_params(("arbitrary",)),
    )(*args)
    return res if need_dx else (None, res[0])


def loss_head(x, g, tgt):
    t, d = x.shape
    tr = _tile(t, 512)

    def body(x_ref, g_ref, t_ref, l_ref, dx_ref, dg_ref):
        i = pl.program_id(0)

        @pl.when(i == 0)
        def _():
            l_ref[...] = jnp.zeros_like(l_ref)
            dg_ref[...] = jnp.zeros_like(dg_ref)

        xf = x_ref[...]
        r = lax.rsqrt(jnp.mean(xf * xf, axis=-1, keepdims=True) + EPS)
        xh = xf * r
        diff = xh * g_ref[...] - t_ref[...]
        l_ref[...] += 0.5 * jnp.sum(jnp.mean(diff * diff, axis=-1, keepdims=True))
        dy = diff * (1.0 / d)
        dg_ref[...] += jnp.sum(dy * xh, axis=0, keepdims=True)
        dxh = dy * g_ref[...]
        dx_ref[...] = r * (dxh - xh * jnp.mean(dxh * xh, axis=-1, keepdims=True))

    row = pl.BlockSpec((tr, d), lambda i: (i, 0))
    vec = pl.BlockSpec((1, d), lambda i: (0, 0))
    return pl.pallas_call(
        body, name="loss_head", grid=(t // tr,), in_specs=[row, vec, row],
        out_specs=[pl.BlockSpec((1, 128), lambda i: (0, 0)), row, vec],
        out_shape=[jax.ShapeDtypeStruct((1, 128), F32), jax.ShapeDtypeStruct((t, d), F32),
                   jax.ShapeDtypeStruct((1, d), F32)],
        compiler_params=_params(("arbitrary",)),
    )(x, g.reshape(1, d), tgt)


def glu_fwd(glu, x, gain):
    t, d = x.shape
    tr = _tile(t, 512)

    def body(v_ref, g_ref, x_ref, n_ref, o_ref, h_ref):
        y = x_ref[...] + v_ref[...] * jax.nn.sigmoid(g_ref[...])
        o_ref[...] = y
        r = lax.rsqrt(jnp.mean(y * y, axis=-1, keepdims=True) + EPS)
        h_ref[...] = (y * r * n_ref[...]).astype(h_ref.dtype)

    row = pl.BlockSpec((tr, d), lambda i: (i, 0))
    return pl.pallas_call(
        body, name="glu_fwd", grid=(t // tr,),
        in_specs=[row, pl.BlockSpec((tr, d), lambda i: (i, 1)), row, pl.BlockSpec((1, d), lambda i: (0, 0))],
        out_specs=[row, row],
        out_shape=[jax.ShapeDtypeStruct((t, d), F32), jax.ShapeDtypeStruct((t, d), BF16)],
        compiler_params=_params(("parallel",)),
    )(glu, glu, x, gain.reshape(1, d))


def glu_bwd(glu, dmix):
    t, d = dmix.shape
    tr = _tile(t, 512)

    def body(v_ref, g_ref, d_ref, o_ref):
        sg = jax.nn.sigmoid(g_ref[...])
        dm = d_ref[...]
        o_ref[:, :d] = (dm * sg).astype(o_ref.dtype)
        o_ref[:, d:] = (dm * v_ref[...] * sg * (1.0 - sg)).astype(o_ref.dtype)

    return pl.pallas_call(
        body, name="glu_bwd", grid=(t // tr,),
        in_specs=[pl.BlockSpec((tr, d), lambda i: (i, 0)), pl.BlockSpec((tr, d), lambda i: (i, 1)),
                  pl.BlockSpec((tr, d), lambda i: (i, 0))],
        out_specs=pl.BlockSpec((tr, 2 * d), lambda i: (i, 0)),
        out_shape=jax.ShapeDtypeStruct((t, 2 * d), BF16), compiler_params=_params(("parallel",)),
    )(glu, glu, dmix)


def _head_masks(shape):
    lane = lax.broadcasted_iota(jnp.int32, shape, 1)
    return lane < SB_HEAD_DIM


def _stack_heads(xf, is_a):
    return jnp.concatenate([jnp.where(is_a, xf, 0.0), jnp.where(is_a, 0.0, xf)], axis=0).astype(MXU_DTYPE)


def _diag_mask(qb, row0, rows):
    row = (lax.broadcasted_iota(jnp.int32, (rows, qb), 0) + row0) & (qb - 1)
    col = lax.broadcasted_iota(jnp.int32, (rows, qb), 1)
    return col < row


def _tri01(qb, pred):
    j = lax.broadcasted_iota(jnp.int32, (qb, qb), 0)
    s = lax.broadcasted_iota(jnp.int32, (qb, qb), 1)
    m = pred(j, s).astype(BF16)
    return jnp.concatenate([m, m], axis=0)


def _split_cat(x):
    hi = x.astype(BF16)
    lo = (x - hi.astype(F32)).astype(BF16)
    return jnp.concatenate([hi, lo], axis=1)


def sb_attn_fwd(proj, order, bsz, seq):
    qb = SB_BLOCK
    nq = seq // qb
    npair = SB_WIDTH // 128
    scale = SB_HEAD_DIM ** -0.5

    def body(q_ref, k_ref, v_ref, order_ref, o_ref, r_ref):
        qi = pl.program_id(2)
        is_a = _head_masks((qb, 128))
        q2 = _stack_heads(q_ref[...] * scale, is_a)
        diag = _diag_mask(qb, 0, 2 * qb)
        upper = _tri01(qb, lambda j, s: j > s)

        def block(kbi, acc, run, masked):
            ks = pl.ds(pl.multiple_of(kbi * qb, qb), qb)
            kblk = k_ref[ks, :].astype(MXU_DTYPE)
            vblk = v_ref[ks, :].astype(MXU_DTYPE)
            z = lax.dot_general(q2, kblk, _NT, preferred_element_type=F32)
            lk = -jnp.maximum(z, 0.0) - jnp.log(1.0 + jnp.exp(-jnp.abs(z)))
            lb = lk + z
            if masked:
                lk = jnp.where(diag, lk, 0.0)
            after = run + lax.dot_general(_split_cat(lk), upper, _NN, preferred_element_type=F32)
            w = jnp.exp(lb + after)
            if masked:
                w = jnp.where(diag, w, 0.0)
            acc = acc + lax.dot_general(w.astype(MXU_DTYPE), vblk, _NN, preferred_element_type=F32)
            return acc, run + jnp.sum(lk, axis=1, keepdims=True)

        carry = block(qi, jnp.zeros((2 * qb, 128), F32), jnp.zeros((2 * qb, 1), F32), True)
        acc, run = lax.fori_loop(0, qi, lambda i, c: block(qi - 1 - i, c[0], c[1], False), carry)
        o_ref[...] = jnp.where(is_a, acc[:qb], acc[qb:]).astype(o_ref.dtype)
        r_ref[...] = jnp.where(is_a, run[:qb], run[qb:])

    return pl.pallas_call(
        body, name="sb_attn_fwd", grid=(bsz, npair, nq),
        in_specs=[pl.BlockSpec((qb, 128), lambda b, p, i: (b * nq + i, p)),
                  pl.BlockSpec((seq, 128), lambda b, p, i: (b, npair + p)),
                  pl.BlockSpec((seq, 128), lambda b, p, i: (b, 2 * npair + p)),
                  pl.BlockSpec((1, 128), lambda b, p, i: (0, 0))],
        out_specs=[pl.BlockSpec((qb, 128), lambda b, p, i: (b * nq + i, p)),
                   pl.BlockSpec((qb, 128), lambda b, p, i: (b * nq + i, p))],
        out_shape=[jax.ShapeDtypeStruct((bsz * seq, SB_WIDTH), BF16),
                   jax.ShapeDtypeStruct((bsz * seq, SB_WIDTH), F32)],
        compiler_params=_params(("parallel", "parallel", "arbitrary")),
    )(proj, proj, proj, order)


def sb_attn_bwd(proj, rsum, dcat, bsz, seq):
    qb = SB_BLOCK
    nq = seq // qb
    npair = SB_WIDTH // 128
    scale = SB_HEAD_DIM ** -0.5

    def body(q_ref, k_ref, v_ref, r_ref, do_ref, dq_ref, dk_ref, dv_ref):
        qi = pl.program_id(2)

        @pl.when(qi == 0)
        def _():
            dk_ref[...] = jnp.zeros_like(dk_ref)
            dv_ref[...] = jnp.zeros_like(dv_ref)

        is_a = _head_masks((qb, 128))
        q2 = _stack_heads(q_ref[...] * scale, is_a)
        do2 = _stack_heads(do_ref[...].astype(F32), is_a)
        rf = r_ref[...]
        rtot = jnp.concatenate([rf[:, 0:1], rf[:, SB_HEAD_DIM:SB_HEAD_DIM + 1]], axis=0)
        diag = _diag_mask(qb, 0, 2 * qb)
        incl = _tri01(qb, lambda j, s: j <= s)
        strict = _tri01(qb, lambda j, s: j < s)

        def block(kbi, dq, pre, epre, masked):
            ks = pl.ds(pl.multiple_of(kbi * qb, qb), qb)
            kblk = k_ref[ks, :].astype(MXU_DTYPE)
            vblk = v_ref[ks, :].astype(MXU_DTYPE)
            z = lax.dot_general(q2, kblk, _NT, preferred_element_type=F32)
            lk = -jnp.maximum(z, 0.0) - jnp.log(1.0 + jnp.exp(-jnp.abs(z)))
            lb = lk + z
            if masked:
                lk = jnp.where(diag, lk, 0.0)
            after = rtot - (pre + lax.dot_general(_split_cat(lk), incl, _NN, preferred_element_type=F32))
            w = jnp.exp(lb + after)
            if masked:
                w = jnp.where(diag, w, 0.0)
            e = lax.dot_general(do2, vblk, _NT, preferred_element_type=F32) * w
            ecum = epre + lax.dot_general(_split_cat(e), strict, _NN, preferred_element_type=F32)
            dz = e - jnp.exp(lb) * (e + ecum)
            if masked:
                dz = jnp.where(diag, dz, 0.0)
            dz = dz.astype(MXU_DTYPE)
            dq = dq + lax.dot_general(dz, kblk, _NN, preferred_element_type=F32)
            dk_ref[ks, :] += lax.dot_general(dz, q2, _TN, preferred_element_type=F32)
            dv_ref[ks, :] += lax.dot_general(w.astype(MXU_DTYPE), do2, _TN, preferred_element_type=F32)
            return dq, pre + jnp.sum(lk, axis=1, keepdims=True), epre + jnp.sum(e, axis=1, keepdims=True)

        zc = jnp.zeros((2 * qb, 1), F32)
        carry = lax.fori_loop(0, qi, lambda kbi, c: block(kbi, c[0], c[1], c[2], False),
                              (jnp.zeros((2 * qb, 128), F32), zc, zc))
        dq = block(qi, carry[0], carry[1], carry[2], True)[0]
        dq_ref[...] = jnp.where(is_a, dq[:qb], dq[qb:]) * scale

    full = jax.ShapeDtypeStruct((bsz * seq, SB_WIDTH), F32)
    qspec = pl.BlockSpec((qb, 128), lambda b, p, i: (b * nq + i, p))
    return pl.pallas_call(
        body, name="sb_attn_bwd", grid=(bsz, npair, nq),
        in_specs=[qspec,
                  pl.BlockSpec((seq, 128), lambda b, p, i: (b, npair + p)),
                  pl.BlockSpec((seq, 128), lambda b, p, i: (b, 2 * npair + p)),
                  qspec, qspec],
        out_specs=[qspec, pl.BlockSpec((seq, 128), lambda b, p, i: (b, p)),
                   pl.BlockSpec((seq, 128), lambda b, p, i: (b, p))],
        out_shape=[full, full, full],
        compiler_params=_params(("parallel", "parallel", "arbitrary")),
    )(proj, proj, proj, rsum, dcat)


def _window_sums(x, forward):
    n = x.shape[0]
    out = []
    s = x
    for sh in (1, 2, 4, 8):
        s = s + pltpu.roll(s, (n - sh) if forward else sh, 0)
        out.append(s)
    return out


def _pool_counts(tc, c, w):
    t = lax.broadcasted_iota(jnp.int32, (tc, 1), 0) + c * tc
    return jnp.minimum(t + 1, w).astype(F32)


def pool_fwd(proj, pool_w, pool_scale, bsz, seq):
    tc = _tile(seq, 512)
    nc = seq // tc
    hb = tc // POOL_HALO
    ucol = 3

    def body(u_ref, prev_ref, w_ref, s_ref, o_ref):
        c = pl.program_id(1)
        prev = jnp.where(c > 0, prev_ref[...], 0.0)
        x = jnp.concatenate([prev, u_ref[...]], axis=0)
        sums = _window_sums(x, forward=False)
        for g, win in enumerate(POOL_WINDOWS):
            ls = slice(g * POOL_GROUP, (g + 1) * POOL_GROUP)
            pooled = sums[g][POOL_HALO:, ls] / _pool_counts(tc, c, win) - x[POOL_HALO:, ls]
            y = _dot(pooled, w_ref[g], _NN)
            o_ref[:, ls] = (y * s_ref[:, ls]).astype(o_ref.dtype)

    return pl.pallas_call(
        body, name="pool_fwd", grid=(bsz, nc),
        in_specs=[pl.BlockSpec((tc, SB_WIDTH), lambda b, c: (b * nc + c, ucol)),
                  pl.BlockSpec((POOL_HALO, SB_WIDTH), lambda b, c: (jnp.maximum((b * nc + c) * hb - 1, 0), ucol)),
                  pl.BlockSpec((4, POOL_GROUP, POOL_GROUP), lambda b, c: (0, 0, 0)),
                  pl.BlockSpec((1, SB_WIDTH), lambda b, c: (0, 0))],
        out_specs=pl.BlockSpec((tc, SB_WIDTH), lambda b, c: (b * nc + c, 0)),
        out_shape=jax.ShapeDtypeStruct((bsz * seq, SB_WIDTH), BF16),
        compiler_params=_params(("parallel", "parallel")),
    )(proj, proj, pool_w, pool_scale)


def pool_bwd(proj, pool_w, pool_scale, dcat, bsz, seq):
    tc = _tile(seq, 512)
    nc = seq // tc
    hb = tc // POOL_HALO
    nblk = bsz * seq // POOL_HALO
    ucol = 3

    def body(u_ref, prev_ref, dy_ref, nxt_ref, w_ref, s_ref, du_ref, dw_ref, ds_ref):
        b, c = pl.program_id(0), pl.program_id(1)

        @pl.when((b == 0) & (c == 0))
        def _():
            dw_ref[...] = jnp.zeros_like(dw_ref)
            ds_ref[...] = jnp.zeros_like(ds_ref)

        prev = jnp.where(c > 0, prev_ref[...], 0.0)
        x = jnp.concatenate([prev, u_ref[...]], axis=0)
        sums = _window_sums(x, forward=False)
        nxt = jnp.where(c < nc - 1, nxt_ref[...].astype(F32), 0.0)
        dy = jnp.concatenate([dy_ref[...].astype(F32), nxt], axis=0)
        tq = lax.broadcasted_iota(jnp.int32, (tc + POOL_HALO, 1), 0) + c * tc
        for g, win in enumerate(POOL_WINDOWS):
            ls = slice(g * POOL_GROUP, (g + 1) * POOL_GROUP)
            pooled = sums[g][POOL_HALO:, ls] / _pool_counts(tc, c, win) - x[POOL_HALO:, ls]
            y = _dot(pooled, w_ref[g], _NN)
            ds_ref[:, ls] += jnp.sum(dy[:tc, ls] * y, axis=0, keepdims=True)
            dz = dy[:, ls] * s_ref[:, ls]
            dw_ref[g] += _dot(pooled, dz[:tc], _TN)
            dpool = _dot(dz, w_ref[g], _NT)
            dmean = dpool / jnp.minimum(tq + 1, win).astype(F32)
            fsum = _window_sums(dmean, forward=True)[g]
            du_ref[:, ls] = fsum[:tc] - dpool[:tc]

    return pl.pallas_call(
        body, name="pool_bwd", grid=(bsz, nc),
        in_specs=[pl.BlockSpec((tc, SB_WIDTH), lambda b, c: (b * nc + c, ucol)),
                  pl.BlockSpec((POOL_HALO, SB_WIDTH), lambda b, c: (jnp.maximum((b * nc + c) * hb - 1, 0), ucol)),
                  pl.BlockSpec((tc, SB_WIDTH), lambda b, c: (b * nc + c, 1)),
                  pl.BlockSpec((POOL_HALO, SB_WIDTH), lambda b, c: (jnp.minimum((b * nc + c + 1) * hb, nblk - 1), 1)),
                  pl.BlockSpec((4, POOL_GROUP, POOL_GROUP), lambda b, c: (0, 0, 0)),
                  pl.BlockSpec((1, SB_WIDTH), lambda b, c: (0, 0))],
        out_specs=[pl.BlockSpec((tc, SB_WIDTH), lambda b, c: (b * nc + c, 0)),
                   pl.BlockSpec((4, POOL_GROUP, POOL_GROUP), lambda b, c: (0, 0, 0)),
                   pl.BlockSpec((1, SB_WIDTH), lambda b, c: (0, 0))],
        out_shape=[jax.ShapeDtypeStruct((bsz * seq, SB_WIDTH), F32),
                   jax.ShapeDtypeStruct((4, POOL_GROUP, POOL_GROUP), F32),
                   jax.ShapeDtypeStruct((1, SB_WIDTH), F32)],
        compiler_params=_params(("arbitrary", "arbitrary")),
    )(proj, proj, dcat, dcat, pool_w, pool_scale)


def _lbar(lam_re, lam_im, log_dt):
    dt = jnp.exp(log_dt)
    mag = jnp.exp(lam_re * dt)
    ang = lam_im * dt
    return mag * jnp.cos(ang), mag * jnp.sin(ang)


def _bbar(lam_re, lam_im, log_dt, b_re, b_im):
    lb_re, lb_im = _lbar(lam_re, lam_im, log_dt)
    n_re = lb_re - 1.0
    den = lam_re * lam_re + lam_im * lam_im
    coef_re = (n_re * lam_re + lb_im * lam_im) / den
    coef_im = (lb_im * lam_re - n_re * lam_im) / den
    return coef_re * b_re - coef_im * b_im, coef_re * b_im + coef_im * b_re


def _expand01():
    p = lax.broadcasted_iota(jnp.int32, (64, 1024), 0)
    q = lax.broadcasted_iota(jnp.int32, (64, 1024), 1)
    return (lax.shift_right_logical(q, 4) == p).astype(BF16)


def ssm_prep(lam_re, lam_im, log_dt, b_re2, b_im2):
    def body(lr_ref, li_ref, dt_ref, br_ref, bi_ref, ar_ref, ai_ref, bbr_ref, bbi_ref):
        e = _expand01()
        lr, li, dt = lr_ref[...], li_ref[...], dt_ref[...]
        ar_ref[...], ai_ref[...] = _lbar(lr, li, dt)
        bbr_ref[...], bbi_ref[...] = _bbar(_dot_exact01(lr, e), _dot_exact01(li, e), dt, br_ref[...], bi_ref[...])

    s64 = jax.ShapeDtypeStruct((64, 64), F32)
    s1k = jax.ShapeDtypeStruct((64, 1024), F32)
    return pl.pallas_call(body, name="ssm_prep", out_shape=[s64, s64, s1k, s1k], compiler_params=_params())(
        lam_re, lam_im, log_dt, b_re2, b_im2)


def ssm_prep_bwd(lam_re, lam_im, log_dt, b_re2, b_im2, da_re, da_im, dbb_re, dbb_im):
    def body(lr_ref, li_ref, dt_ref, br_ref, bi_ref, dar_ref, dai_ref, dbr_ref, dbi_ref,
             olr_ref, oli_ref, odt_ref, obr_ref, obi_ref):
        e = _expand01()
        lr, li, dt = lr_ref[...], li_ref[...], dt_ref[...]
        _, vjp_a = jax.vjp(_lbar, lr, li, dt)
        g_lr, g_li, g_dt = vjp_a((dar_ref[...], dai_ref[...]))
        _, vjp_b = jax.vjp(_bbar, _dot_exact01(lr, e), _dot_exact01(li, e), dt, br_ref[...], bi_ref[...])
        x_lr, x_li, x_dt, g_br, g_bi = vjp_b((dbr_ref[...], dbi_ref[...]))
        olr_ref[...] = g_lr + _dot_exact01(x_lr, e, _NT)
        oli_ref[...] = g_li + _dot_exact01(x_li, e, _NT)
        odt_ref[...] = g_dt + x_dt
        obr_ref[...] = g_br
        obi_ref[...] = g_bi

    s64 = jax.ShapeDtypeStruct((64, 64), F32)
    s1k = jax.ShapeDtypeStruct((64, 1024), F32)
    return pl.pallas_call(body, name="ssm_prep_bwd",
                          out_shape=[s64, s64, jax.ShapeDtypeStruct((64, 1), F32), s1k, s1k],
                          compiler_params=_params())(
        lam_re, lam_im, log_dt, b_re2, b_im2, da_re, da_im, dbb_re, dbb_im)


def _gelu(y):
    c = math.sqrt(2.0 / math.pi)
    return 0.5 * y * (1.0 + jnp.tanh(c * (y + 0.044715 * y * y * y)))


def _gelu_grad(y):
    c = math.sqrt(2.0 / math.pi)
    th = jnp.tanh(c * (y + 0.044715 * y * y * y))
    return 0.5 * (1.0 + th) + 0.5 * y * (1.0 - th * th) * c * (1.0 + 3.0 * 0.044715 * y * y)


def _cmul(ar, ai, br, bi):
    return ar * br - ai * bi, ar * bi + ai * br


def _scan_tables(ar, ai, reverse, tabs):
    row = lax.broadcasted_iota(jnp.int32, (8, SSM_STATES), 0)
    a1 = (ar, ai)
    a2 = _cmul(*a1, *a1)
    a4 = _cmul(*a2, *a2)
    powers = [a1, a2, _cmul(*a2, *a1), a4]
    powers += [_cmul(*a4, *p) for p in powers]
    for k, (val, sh) in enumerate(((a1, 1), (a2, 2), (a4, 4))):
        keep = (row < 8 - sh) if reverse else (row >= sh)
        tabs[2 * k][...] = jnp.where(keep, val[0], 0.0)
        tabs[2 * k + 1][...] = jnp.where(keep, val[1], 0.0)
    pr = jnp.zeros((8, SSM_STATES), F32)
    pi = jnp.zeros((8, SSM_STATES), F32)
    for r in range(8):
        val = powers[7 - r] if reverse else powers[r]
        pr = jnp.where(row == r, val[0], pr)
        pi = jnp.where(row == r, val[1], pi)
    tabs[6][...] = pr
    tabs[7][...] = pi


def _scan8(xr, xi, tabs, ls, cr, ci, reverse):
    for k, sh in enumerate((1, 2, 4)):
        amt = (8 - sh) if reverse else sh
        sr, si = pltpu.roll(xr, amt, 0), pltpu.roll(xi, amt, 0)
        lr, li = tabs[2 * k][:, ls], tabs[2 * k + 1][:, ls]
        xr, xi = xr + lr * sr - li * si, xi + lr * si + li * sr
    pr, pi = tabs[6][:, ls], tabs[7][:, ls]
    return xr + pr * cr - pi * ci, xi + pr * ci + pi * cr


def _block8(b):
    return pl.ds(pl.multiple_of(b * 8, 8), 8)


def ssm_fwd(u, wt, ct, a_re, a_im, dskip, bsz, seq):
    tc = _tile(seq, 256)
    nc = seq // tc
    ns = SSM_TILE_STATES
    nl = SSM_STATES // SSM_LANES

    def body(u_ref, wt_ref, ct_ref, ar_ref, ai_ref, d_ref, y_ref, gl_ref, hr_ref, hi_ref, sr_ref, si_ref, *tabs):
        b, c = pl.program_id(0), pl.program_id(1)

        @pl.when((b == 0) & (c == 0))
        def _():
            _scan_tables(ar_ref[...], ai_ref[...], False, tabs)

        @pl.when(c == 0)
        def _():
            sr_ref[...] = jnp.zeros_like(sr_ref)
            si_ref[...] = jnp.zeros_like(si_ref)

        uf = u_ref[...]
        for i in range(SSM_TILES):
            bu = _dot(uf[:, i * 128:(i + 1) * 128], wt_ref[i], _NN)
            hr_ref[:, i * ns:(i + 1) * ns] = bu[:, :ns]
            hi_ref[:, i * ns:(i + 1) * ns] = bu[:, ns:]

        def step(blk, carry):
            rows = _block8(blk)
            new = []
            for j in range(nl):
                ls = slice(j * SSM_LANES, (j + 1) * SSM_LANES)
                xr, xi = _scan8(hr_ref[rows, ls], hi_ref[rows, ls], tabs, ls, carry[2 * j], carry[2 * j + 1], False)
                hr_ref[rows, ls] = xr
                hi_ref[rows, ls] = xi
                new += [xr[7:8], xi[7:8]]
            return tuple(new)

        init = []
        for j in range(nl):
            ls = slice(j * SSM_LANES, (j + 1) * SSM_LANES)
            init += [sr_ref[:, ls], si_ref[:, ls]]
        last = lax.fori_loop(0, tc // 8, step, tuple(init), unroll=2)
        for j in range(nl):
            ls = slice(j * SSM_LANES, (j + 1) * SSM_LANES)
            sr_ref[:, ls] = last[2 * j]
            si_ref[:, ls] = last[2 * j + 1]
        for i in range(SSM_TILES):
            hcat = jnp.concatenate([hr_ref[:, i * ns:(i + 1) * ns], hi_ref[:, i * ns:(i + 1) * ns]], axis=1)
            ls = slice(i * 128, (i + 1) * 128)
            y = _dot(hcat, ct_ref[i], _NN) + d_ref[:, ls] * uf[:, ls]
            y_ref[:, ls] = y
            gl_ref[:, ls] = _gelu(y).astype(gl_ref.dtype)

    t = bsz * seq
    row = pl.BlockSpec((tc, D_MODEL), lambda b, c: (b * nc + c, 0))
    st = pl.BlockSpec((tc, SSM_STATES), lambda b, c: (b * nc + c, 0))
    diag = pl.BlockSpec((1, SSM_STATES), lambda b, c: (0, 0))
    return pl.pallas_call(
        body, name="ssm_fwd", grid=(bsz, nc),
        in_specs=[row, pl.BlockSpec((SSM_TILES, 128, 2 * ns), lambda b, c: (0, 0, 0)),
                  pl.BlockSpec((SSM_TILES, 2 * ns, 128), lambda b, c: (0, 0, 0)), diag, diag,
                  pl.BlockSpec((1, D_MODEL), lambda b, c: (0, 0))],
        out_specs=[row, row, st, st],
        out_shape=[jax.ShapeDtypeStruct((t, D_MODEL), F32), jax.ShapeDtypeStruct((t, D_MODEL), BF16),
                   jax.ShapeDtypeStruct((t, SSM_STATES), F32), jax.ShapeDtypeStruct((t, SSM_STATES), F32)],
        scratch_shapes=[pltpu.VMEM((1, SSM_STATES), F32)] * 2 + [pltpu.VMEM((8, SSM_STATES), F32)] * 8,
        compiler_params=_params(("arbitrary", "arbitrary")),
    )(u, wt, ct, a_re, a_im, dskip)


def ssm_bwd(dgl, y, u, h_re, h_im, wt, ct, a_re, a_im, dskip, bsz, seq):
    tc = _tile(seq, 256)
    nc = seq // tc
    nb = tc // 8
    ns = SSM_TILE_STATES
    nl = SSM_STATES // SSM_LANES

    def body(dgl_ref, y_ref, u_ref, hr_ref, hi_ref, pr_ref, pi_ref, wt_ref, ct_ref, ar_ref, ai_ref, d_ref,
             du_ref, dwt_ref, dct_ref, dd_ref, dar_ref, dai_ref, gr_ref, gi_ref, sr_ref, si_ref, ar8_ref, ai8_ref,
             *tabs):
        b, c = pl.program_id(0), pl.program_id(1)

        @pl.when((b == 0) & (c == 0))
        def _():
            for r in (dwt_ref, dct_ref, dd_ref, ar8_ref, ai8_ref):
                r[...] = jnp.zeros_like(r)
            _scan_tables(ar_ref[...], -ai_ref[...], True, tabs)

        @pl.when(c == 0)
        def _():
            sr_ref[...] = jnp.zeros_like(sr_ref)
            si_ref[...] = jnp.zeros_like(si_ref)

        uf = u_ref[...]
        dy = dgl_ref[...].astype(F32) * _gelu_grad(y_ref[...])
        dd_ref[...] += jnp.sum(dy * uf, axis=0, keepdims=True)
        for i in range(SSM_TILES):
            dyi = dy[:, i * 128:(i + 1) * 128]
            dh = _dot(dyi, ct_ref[i], _NT)
            gr_ref[:, i * ns:(i + 1) * ns] = dh[:, :ns]
            gi_ref[:, i * ns:(i + 1) * ns] = dh[:, ns:]
            hcat = jnp.concatenate([hr_ref[:, i * ns:(i + 1) * ns], hi_ref[:, i * ns:(i + 1) * ns]], axis=1)
            dct_ref[i] += _dot(hcat, dyi, _TN)
        row0 = lax.broadcasted_iota(jnp.int32, (8, SSM_LANES), 0) == 0

        def block(blk, carry, before):
            rows = _block8(blk)
            new = []
            for j in range(nl):
                ls = slice(j * SSM_LANES, (j + 1) * SSM_LANES)
                gr, gi = _scan8(gr_ref[rows, ls], gi_ref[rows, ls], tabs, ls, carry[2 * j], carry[2 * j + 1], True)
                gr_ref[rows, ls] = gr
                gi_ref[rows, ls] = gi
                bpr, bpi = before(j)
                hpr = jnp.where(row0, bpr, pltpu.roll(hr_ref[rows, ls], 1, 0))
                hpi = jnp.where(row0, bpi, pltpu.roll(hi_ref[rows, ls], 1, 0))
                ar8_ref[:, ls] += gr * hpr + gi * hpi
                ai8_ref[:, ls] += gi * hpr - gr * hpi
                new += [gr[0:1], gi[0:1]]
            return tuple(new)

        def step(jj, carry):
            blk = nb - 1 - jj
            prev_rows = _block8(blk - 1)

            def before(j):
                ls = slice(j * SSM_LANES, (j + 1) * SSM_LANES)
                return hr_ref[prev_rows, ls][7:8], hi_ref[prev_rows, ls][7:8]

            return block(blk, carry, before)

        init = []
        for j in range(nl):
            ls = slice(j * SSM_LANES, (j + 1) * SSM_LANES)
            init += [sr_ref[:, ls], si_ref[:, ls]]
        carry = lax.fori_loop(0, nb - 1, step, tuple(init))
        first = c == nc - 1

        def before_chunk(j):
            ls = slice(j * SSM_LANES, (j + 1) * SSM_LANES)
            return (jnp.where(first, 0.0, pr_ref[:, ls][7:8]), jnp.where(first, 0.0, pi_ref[:, ls][7:8]))

        last = block(0, carry, before_chunk)
        for j in range(nl):
            ls = slice(j * SSM_LANES, (j + 1) * SSM_LANES)
            sr_ref[:, ls] = last[2 * j]
            si_ref[:, ls] = last[2 * j + 1]
        for i in range(SSM_TILES):
            ls = slice(i * 128, (i + 1) * 128)
            gcat = jnp.concatenate([gr_ref[:, i * ns:(i + 1) * ns], gi_ref[:, i * ns:(i + 1) * ns]], axis=1)
            du_ref[:, ls] = (_dot(gcat, wt_ref[i], _NT) + d_ref[:, ls] * dy[:, ls]).astype(du_ref.dtype)
            dwt_ref[i] += _dot(uf[:, ls], gcat, _TN)

        @pl.when((b == bsz - 1) & (c == nc - 1))
        def _():
            dar_ref[...] = jnp.sum(ar8_ref[...], axis=0, keepdims=True)
            dai_ref[...] = jnp.sum(ai8_ref[...], axis=0, keepdims=True)

    t = bsz * seq
    rev = lambda b, c: (b * nc + (nc - 1 - c), 0)
    row = pl.BlockSpec((tc, D_MODEL), rev)
    st = pl.BlockSpec((tc, SSM_STATES), rev)
    prev = pl.BlockSpec((8, SSM_STATES), lambda b, c: (jnp.maximum((b * nc + (nc - 1 - c)) * nb - 1, 0), 0))
    diag = pl.BlockSpec((1, SSM_STATES), lambda b, c: (0, 0))
    wts = pl.BlockSpec((SSM_TILES, 128, 2 * ns), lambda b, c: (0, 0, 0))
    cts = pl.BlockSpec((SSM_TILES, 2 * ns, 128), lambda b, c: (0, 0, 0))
    vec = pl.BlockSpec((1, D_MODEL), lambda b, c: (0, 0))
    return pl.pallas_call(
        body, name="ssm_bwd", grid=(bsz, nc),
        in_specs=[row, row, row, st, st, prev, prev, wts, cts, diag, diag, vec],
        out_specs=[row, wts, cts, vec, diag, diag],
        out_shape=[jax.ShapeDtypeStruct((t, D_MODEL), BF16),
                   jax.ShapeDtypeStruct((SSM_TILES, 128, 2 * ns), F32),
                   jax.ShapeDtypeStruct((SSM_TILES, 2 * ns, 128), F32),
                   jax.ShapeDtypeStruct((1, D_MODEL), F32),
                   jax.ShapeDtypeStruct((1, SSM_STATES), F32), jax.ShapeDtypeStruct((1, SSM_STATES), F32)],
        scratch_shapes=[pltpu.VMEM((tc, SSM_STATES), F32)] * 2 + [pltpu.VMEM((1, SSM_STATES), F32)] * 2
                       + [pltpu.VMEM((8, SSM_STATES), F32)] * 10,
        compiler_params=_params(("arbitrary", "arbitrary")),
    )(dgl, y, u, h_re, h_im, h_re, h_im, wt, ct, a_re, a_im, dskip)


def _ssm_in_weights(bb_re2, bb_im2):
    eye = jnp.eye(8, dtype=F32)[None, :, None, :, None]

    def one(bb):
        t = bb.reshape(8, 8, 64, 16).transpose(0, 1, 3, 2)
        return (t[:, :, :, None, :] * eye).reshape(8, 128, 512)

    return jnp.concatenate([one(bb_re2), one(bb_im2)], axis=-1).astype(MXU_DTYPE)


def _ssm_in_weights_bwd(dwt):
    eye = jnp.eye(8, dtype=F32)[None, :, None, :, None]

    def one(d):
        t = (d.reshape(8, 8, 16, 8, 64) * eye).sum(axis=3)
        return t.transpose(0, 1, 3, 2).reshape(64, 1024)

    return one(dwt[..., :512]), one(dwt[..., 512:])


def _ssm_out_weights(c_re, c_im):
    eye = jnp.eye(8, dtype=F32)[None, :, None, :, None]

    def one(cc):
        t = cc.reshape(8, 8, 16, 64).transpose(0, 1, 3, 2)
        return (t[:, :, :, None, :] * eye).reshape(8, 512, 128)

    return jnp.concatenate([one(c_re), -one(c_im)], axis=1).astype(MXU_DTYPE)


def _ssm_out_weights_bwd(dct):
    eye = jnp.eye(8, dtype=F32)[None, :, None, :, None]

    def one(d):
        t = (d.reshape(8, 8, 64, 8, 16) * eye).sum(axis=3)
        return t.transpose(0, 1, 3, 2).reshape(64, 16, 64)

    return one(dct[:, :512]), -one(dct[:, 512:])


def _softmax(s):
    m = jnp.max(s, axis=-1, keepdims=True)
    e = jnp.exp(s - m)
    return e / jnp.sum(e, axis=-1, keepdims=True)


def xattn_fwd(q, kv, bsz, seq):
    tq = _tile(seq, 512)
    nq = seq // tq
    scale = XA_HEAD_DIM ** -0.5

    def body(q_ref, k_ref, v_ref, o_ref):
        s = lax.dot_general(q_ref[...], k_ref[...], _NT, preferred_element_type=F32) * scale
        p = _softmax(s)
        o_ref[...] = _dot(p, v_ref[...], _NN).astype(o_ref.dtype)

    qs = pl.BlockSpec((tq, XA_HEAD_DIM), lambda b, h, i: (b * nq + i, h))
    return pl.pallas_call(
        body, name="xattn_fwd", grid=(bsz, XA_HEADS, nq),
        in_specs=[qs, pl.BlockSpec((MEM_LEN, XA_HEAD_DIM), lambda b, h, i: (b, h)),
                  pl.BlockSpec((MEM_LEN, XA_HEAD_DIM), lambda b, h, i: (b, XA_HEADS + h))],
        out_specs=qs, out_shape=jax.ShapeDtypeStruct((bsz * seq, D_MODEL), BF16),
        compiler_params=_params(("parallel", "parallel", "parallel")),
    )(q, kv, kv)


def xattn_bwd(q, kv, do, bsz, seq):
    tq = _tile(seq, 512)
    nq = seq // tq
    scale = XA_HEAD_DIM ** -0.5

    def body(q_ref, k_ref, v_ref, do_ref, dq_ref, dk_ref, dv_ref):
        @pl.when(pl.program_id(2) == 0)
        def _():
            dk_ref[...] = jnp.zeros_like(dk_ref)
            dv_ref[...] = jnp.zeros_like(dv_ref)

        qv, kk, vv, dov = q_ref[...], k_ref[...], v_ref[...], do_ref[...]
        s = lax.dot_general(qv, kk, _NT, preferred_element_type=F32) * scale
        p = _softmax(s)
        dp = lax.dot_general(dov, vv, _NT, preferred_element_type=F32)
        ds = (p * (dp - jnp.sum(dp * p, axis=-1, keepdims=True)) * scale).astype(MXU_DTYPE)
        dq_ref[...] = lax.dot_general(ds, kk, _NN, preferred_element_type=F32).astype(dq_ref.dtype)
        dk_ref[...] += lax.dot_general(ds, qv, _TN, preferred_element_type=F32)
        dv_ref[...] += lax.dot_general(p.astype(MXU_DTYPE), dov, _TN, preferred_element_type=F32)

    qs = pl.BlockSpec((tq, XA_HEAD_DIM), lambda b, h, i: (b * nq + i, h))
    ks = pl.BlockSpec((MEM_LEN, XA_HEAD_DIM), lambda b, h, i: (b, h))
    vs = pl.BlockSpec((MEM_LEN, XA_HEAD_DIM), lambda b, h, i: (b, XA_HEADS + h))
    dkv = jax.ShapeDtypeStruct((bsz * MEM_LEN, D_MODEL), F32)
    dq, dk, dv = pl.pallas_call(
        body, name="xattn_bwd", grid=(bsz, XA_HEADS, nq),
        in_specs=[qs, ks, vs, qs], out_specs=[qs, ks, ks],
        out_shape=[jax.ShapeDtypeStruct((bsz * seq, D_MODEL), BF16), dkv, dkv],
        compiler_params=_params(("parallel", "parallel", "arbitrary")),
    )(q, kv, kv, do)
    return dq, dk, dv


CONV_HALO = 16


def _shifts_down(x, prev):
    h = prev.shape[0]
    ext = jnp.concatenate([prev, x], axis=0)
    return pltpu.roll(ext, 1, 0)[h:], pltpu.roll(ext, 2, 0)[h:]


def _shifts_up(x, nxt):
    rows = x.shape[0]
    n = rows + nxt.shape[0]
    ext = jnp.concatenate([x, nxt], axis=0)
    return pltpu.roll(ext, n - 1, 0)[:rows], pltpu.roll(ext, n - 2, 0)[:rows]


def _conv_taps(u, u1, u2, w, b):
    return b + w[2:3] * u + w[1:2] * u1 + w[0:1] * u2


def conv_fwd(up, cw, cb, bsz, seq):
    tc = _tile(seq, 512)
    nc = seq // tc
    hb = tc // CONV_HALO
    half = N_DEV // 2

    def body(uv_ref, ug_ref, pv_ref, pg_ref, wv_ref, wg_ref, bv_ref, bg_ref, o_ref):
        c = pl.program_id(2)
        pv = jnp.where(c > 0, pv_ref[...].astype(F32), 0.0)
        pg = jnp.where(c > 0, pg_ref[...].astype(F32), 0.0)
        uv, ug = uv_ref[...].astype(F32), ug_ref[...].astype(F32)
        val = _conv_taps(uv, *_shifts_down(uv, pv), wv_ref[...], bv_ref[...])
        gate = _conv_taps(ug, *_shifts_down(ug, pg), wg_ref[...], bg_ref[...])
        o_ref[...] = (gate * jax.nn.sigmoid(gate) * val).astype(o_ref.dtype)

    def cur(off):
        return pl.BlockSpec((None, tc, FF_SHARD), lambda b, j, c: (j + off, b * nc + c, 0))

    def prv(off):
        return pl.BlockSpec((None, CONV_HALO, FF_SHARD), lambda b, j, c: (j + off, jnp.maximum((b * nc + c) * hb - 1, 0), 0))

    def par(rows, off):
        return pl.BlockSpec((None, rows, FF_SHARD), lambda b, j, c: (j + off, 0, 0))

    return pl.pallas_call(
        body, name="conv_fwd", grid=(bsz, half, nc),
        in_specs=[cur(0), cur(half), prv(0), prv(half), par(3, 0), par(3, half), par(1, 0), par(1, half)],
        out_specs=cur(0), out_shape=jax.ShapeDtypeStruct((half, bsz * seq, FF_SHARD), BF16),
        compiler_params=_params(("parallel", "parallel", "parallel")),
    )(up, up, up, up, cw, cw, cb, cb)


def conv_bwd_taps(up, cw, cb, dact, bsz, seq):
    tc = _tile(seq, 512)
    nc = seq // tc
    hb = tc // CONV_HALO
    half = N_DEV // 2

    def body(uv_ref, ug_ref, pv_ref, pg_ref, wv_ref, wg_ref, bv_ref, bg_ref, da_ref,
             dc_ref, dwv_ref, dwg_ref, dbv_ref, dbg_ref):
        b, c = pl.program_id(1), pl.program_id(2)

        @pl.when((b == 0) & (c == 0))
        def _():
            for r in (dwv_ref, dwg_ref, dbv_ref, dbg_ref):
                r[...] = jnp.zeros_like(r)

        pv = jnp.where(c > 0, pv_ref[...].astype(F32), 0.0)
        pg = jnp.where(c > 0, pg_ref[...].astype(F32), 0.0)
        uv, ug = uv_ref[...].astype(F32), ug_ref[...].astype(F32)
        uv1, uv2 = _shifts_down(uv, pv)
        ug1, ug2 = _shifts_down(ug, pg)
        val = _conv_taps(uv, uv1, uv2, wv_ref[...], bv_ref[...])
        gate = _conv_taps(ug, ug1, ug2, wg_ref[...], bg_ref[...])
        sg = jax.nn.sigmoid(gate)
        da = da_ref[...].astype(F32)
        dsilu = da * sg
        dval = dsilu * gate
        dgate = dsilu * val * (1.0 + gate * (1.0 - sg))
        dc_ref[0] = dval.astype(dc_ref.dtype)
        dc_ref[1] = dgate.astype(dc_ref.dtype)
        for dcv, taps, dw_ref, db_ref in ((dval, (uv2, uv1, uv), dwv_ref, dbv_ref),
                                          (dgate, (ug2, ug1, ug), dwg_ref, dbg_ref)):
            db_ref[...] += jnp.sum(dcv, axis=0, keepdims=True)
            for k, u_k in enumerate(taps):
                dw_ref[k:k + 1, :] += jnp.sum(dcv * u_k, axis=0, keepdims=True)

    def cur(off):
        return pl.BlockSpec((None, tc, FF_SHARD), lambda j, b, c: (j + off, b * nc + c, 0))

    def prv(off):
        return pl.BlockSpec((None, CONV_HALO, FF_SHARD), lambda j, b, c: (j + off, jnp.maximum((b * nc + c) * hb - 1, 0), 0))

    def par(rows, off):
        return pl.BlockSpec((None, rows, FF_SHARD), lambda j, b, c: (j + off, 0, 0))

    t = bsz * seq
    hs = jax.ShapeDtypeStruct((2, half, t, FF_SHARD), BF16)
    ws = jax.ShapeDtypeStruct((half, 3, FF_SHARD), F32)
    bs = jax.ShapeDtypeStruct((half, 1, FF_SHARD), F32)
    dc, dwv, dwg, dbv, dbg = pl.pallas_call(
        body, name="conv_bwd_taps", grid=(half, bsz, nc),
        in_specs=[cur(0), cur(half), prv(0), prv(half), par(3, 0), par(3, half), par(1, 0), par(1, half), cur(0)],
        out_specs=[pl.BlockSpec((2, None, tc, FF_SHARD), lambda j, b, c: (0, j, b * nc + c, 0)),
                   par(3, 0), par(3, 0), par(1, 0), par(1, 0)],
        out_shape=[hs, ws, ws, bs, bs],
        compiler_params=_params(("parallel", "arbitrary", "arbitrary")),
    )(up, up, up, up, cw, cw, cb, cb, dact)
    return (dc.reshape(N_DEV, t, FF_SHARD), jnp.concatenate([dwv, dwg], axis=0),
            jnp.concatenate([dbv, dbg], axis=0))


def conv_bwd_input(dconv, cw, bsz, seq):
    tc = _tile(seq, 1024)
    nc = seq // tc
    hb = tc // CONV_HALO
    nblk = bsz * seq // CONV_HALO

    def body(d_ref, n_ref, w_ref, o_ref):
        c = pl.program_id(2)
        nxt = jnp.where(c < nc - 1, n_ref[...].astype(F32), 0.0)
        d = d_ref[...].astype(F32)
        d1, d2 = _shifts_up(d, nxt)
        w = w_ref[...]
        o_ref[...] = (w[2:3] * d + w[1:2] * d1 + w[0:1] * d2).astype(o_ref.dtype)

    cur = pl.BlockSpec((None, tc, FF_SHARD), lambda j, b, c: (j, b * nc + c, 0))
    return pl.pallas_call(
        body, name="conv_bwd_input", grid=(N_DEV, bsz, nc),
        in_specs=[cur, pl.BlockSpec((None, CONV_HALO, FF_SHARD),
                                    lambda j, b, c: (j, jnp.minimum((b * nc + c + 1) * hb, nblk - 1), 0)),
                  pl.BlockSpec((None, 3, FF_SHARD), lambda j, b, c: (j, 0, 0))],
        out_specs=cur, out_shape=jax.ShapeDtypeStruct(dconv.shape, BF16),
        compiler_params=_params(("parallel", "parallel", "parallel")),
    )(dconv, dconv, cw)


def _my_index():
    return 4 * lax.axis_index("x") + 2 * lax.axis_index("y") + lax.axis_index("c")


def _peer(k):
    return (lax.axis_index("x") ^ ((k >> 2) & 1), lax.axis_index("y") ^ ((k >> 1) & 1),
            lax.axis_index("c") ^ (k & 1))


def all_gather(name, a, out_dtype):
    def body(a_ref, o_ref, stage, send_sems, recv_sems, local_sem):
        me = _my_index()
        stage[...] = a_ref[...].astype(out_dtype)
        local = pltpu.make_async_copy(stage, o_ref.at[me], local_sem)
        local.start()
        sends = []
        for k in range(1, N_DEV):
            cp = pltpu.make_async_remote_copy(
                src_ref=stage, dst_ref=o_ref.at[me], send_sem=send_sems.at[k - 1], recv_sem=recv_sems.at[k - 1],
                device_id=_peer(k), device_id_type=pl.DeviceIdType.MESH)
            cp.start()
            sends.append(cp)
        for k in range(1, N_DEV):
            pltpu.make_async_remote_copy(
                src_ref=stage, dst_ref=o_ref.at[me ^ k], send_sem=send_sems.at[k - 1], recv_sem=recv_sems.at[k - 1],
                device_id=_peer(k), device_id_type=pl.DeviceIdType.MESH).wait_recv()
        for cp in sends:
            cp.wait_send()
        local.wait()

    return pl.pallas_call(
        body, name=name, in_specs=[pl.BlockSpec(memory_space=pltpu.VMEM)],
        out_specs=pl.BlockSpec(memory_space=pltpu.HBM),
        out_shape=jax.ShapeDtypeStruct((N_DEV,) + a.shape, out_dtype),
        scratch_shapes=[pltpu.VMEM(a.shape, out_dtype), pltpu.SemaphoreType.DMA((N_DEV - 1,)),
                        pltpu.SemaphoreType.DMA((N_DEV - 1,)), pltpu.SemaphoreType.DMA],
        compiler_params=pltpu.CompilerParams(vmem_limit_bytes=VMEM_LIMIT),
    )(a)


def exchange(name, g):
    def body(g_ref, r_ref, send_sems, recv_sems, local_sem):
        me = _my_index()
        local = pltpu.make_async_copy(g_ref.at[me], r_ref.at[me], local_sem)
        local.start()
        sends = []
        for k in range(1, N_DEV):
            cp = pltpu.make_async_remote_copy(
                src_ref=g_ref.at[me ^ k], dst_ref=r_ref.at[me], send_sem=send_sems.at[k - 1],
                recv_sem=recv_sems.at[k - 1], device_id=_peer(k), device_id_type=pl.DeviceIdType.MESH)
            cp.start()
            sends.append(cp)
        for k in range(1, N_DEV):
            pltpu.make_async_remote_copy(
                src_ref=g_ref.at[me], dst_ref=r_ref.at[me ^ k], send_sem=send_sems.at[k - 1],
                recv_sem=recv_sems.at[k - 1], device_id=_peer(k), device_id_type=pl.DeviceIdType.MESH).wait_recv()
        for cp in sends:
            cp.wait_send()
        local.wait()

    return pl.pallas_call(
        body, name=name, in_specs=[pl.BlockSpec(memory_space=pltpu.HBM)],
        out_specs=pl.BlockSpec(memory_space=pltpu.HBM),
        out_shape=jax.ShapeDtypeStruct(g.shape, g.dtype),
        scratch_shapes=[pltpu.SemaphoreType.DMA((N_DEV - 1,)), pltpu.SemaphoreType.DMA((N_DEV - 1,)),
                        pltpu.SemaphoreType.DMA],
    )(g)


_HBM = pl.BlockSpec(memory_space=pltpu.HBM)
_SEM = pl.BlockSpec(memory_space=pltpu.SEMAPHORE)
_DATAFLOW = pltpu.SideEffectType.DATAFLOW_SIDE_EFFECTING


def _split_copies(gather, src_ref, land_ref, send_sems, recv_sems, local_sem):
    me = _my_index()

    def part(j):
        return src_ref if gather else src_ref.at[j]

    local = pltpu.make_async_copy(part(me), land_ref.at[me], local_sem)
    sends = [pltpu.make_async_remote_copy(
        src_ref=part(me ^ k), dst_ref=land_ref.at[me], send_sem=send_sems.at[k - 1], recv_sem=recv_sems.at[k - 1],
        device_id=_peer(k), device_id_type=pl.DeviceIdType.MESH) for k in range(1, N_DEV)]
    recvs = [pltpu.make_async_remote_copy(
        src_ref=part(me ^ k), dst_ref=land_ref.at[me ^ k], send_sem=send_sems.at[k - 1], recv_sem=recv_sems.at[k - 1],
        device_id=_peer(k), device_id_type=pl.DeviceIdType.MESH) for k in range(1, N_DEV)]
    return local, sends, recvs


def split_start(name, src, gather):
    land_shape = ((N_DEV,) + src.shape) if gather else src.shape

    def body(src_ref, land_ref, send_sems, recv_sems, local_sem, src_thru, land_thru, token):
        local, sends, _ = _split_copies(gather, src_ref, land_ref, send_sems, recv_sems, local_sem)
        local.start()
        for cp in sends:
            cp.start()
        token[...] = jnp.zeros_like(token)

    dma7 = pltpu.SemaphoreType.DMA((N_DEV - 1,))
    out = pl.pallas_call(
        body, name=name,
        out_shape=(dma7, dma7, pltpu.SemaphoreType.DMA(()), pltpu.HBM(src.shape, src.dtype),
                   pltpu.HBM(land_shape, src.dtype), jax.ShapeDtypeStruct((8, 128), F32)),
        in_specs=(_HBM, _HBM), out_specs=(_SEM, _SEM, _SEM, _HBM, _HBM, pl.BlockSpec(memory_space=pltpu.VMEM)),
        input_output_aliases={0: 3, 1: 4},
        compiler_params=pltpu.CompilerParams(has_side_effects=_DATAFLOW),
    )(pltpu.with_memory_space_constraint(src, pltpu.HBM),
      pltpu.with_memory_space_constraint(lax.empty(land_shape, src.dtype), pltpu.HBM))
    return out[:5], out[5][0, 0]


def split_wait(name, handles, after, gather):
    send_sems, recv_sems, local_sem, src_thru, land_thru = handles

    def body(src_ref, land_ref, send_sems, recv_sems, local_sem, after_ref, src_dead, got_ref, token):
        local, sends, recvs = _split_copies(gather, src_ref, land_ref, send_sems, recv_sems, local_sem)
        local.wait()
        for cp in recvs:
            cp.wait_send()
            cp.wait_recv()
        token[...] = jnp.zeros_like(token)

    out = pl.pallas_call(
        body, name=name,
        out_shape=(pltpu.HBM(src_thru.shape, src_thru.dtype), pltpu.HBM(land_thru.shape, land_thru.dtype),
                   jax.ShapeDtypeStruct((8, 128), F32)),
        in_specs=(_HBM, _HBM, _SEM, _SEM, _SEM, pl.BlockSpec(memory_space=pl.ANY)),
        out_specs=(_HBM, _HBM, pl.BlockSpec(memory_space=pltpu.VMEM)),
        input_output_aliases={0: 0, 1: 1},
        compiler_params=pltpu.CompilerParams(has_side_effects=_DATAFLOW),
    )(src_thru, land_thru, send_sems, recv_sems, local_sem, after)
    return out[1], out[2][0, 0]


def sum_parts(name, r):
    _, rows, cols = r.shape

    def body(r_ref, o_ref):
        acc = r_ref[0].astype(F32)
        for s in range(1, N_DEV):
            acc = acc + r_ref[s].astype(F32)
        o_ref[...] = acc

    return pl.pallas_call(body, name=name, out_shape=jax.ShapeDtypeStruct((rows, cols), F32),
                          compiler_params=_params())(r)


def adamw(name, w, m, v, parts=None, g=None, layer=0, into=None):
    _, rows, cols = w.shape
    br = _tile(rows, 256, 16)
    c1 = 1.0 / (1.0 - ADAM_B1 ** ADAM_STEP)
    c2 = 1.0 / (1.0 - ADAM_B2 ** ADAM_STEP)

    def body(g_ref, w_ref, m_ref, v_ref, *rest):
        og_ref, od_ref, om_ref, ov_ref = rest[-4:]
        if parts is None:
            gs = g_ref[...]
        else:
            gs = g_ref[0].astype(F32)
            for s in range(1, N_DEV):
                gs = gs + g_ref[s].astype(F32)
        mn = ADAM_B1 * m_ref[...] + (1.0 - ADAM_B1) * gs
        vn = ADAM_B2 * v_ref[...] + (1.0 - ADAM_B2) * (gs * gs)
        og_ref[...] = gs
        om_ref[...] = mn
        ov_ref[...] = vn
        od_ref[...] = -ADAM_LR * ((mn * c1) / (jnp.sqrt(vn * c2) + ADAM_EPS) + ADAM_WD * w_ref[...])

    blk = pl.BlockSpec((None, br, cols), lambda i: (layer, i, 0))
    if parts is None:
        gspec = pl.BlockSpec((br, cols), lambda i: (i, 0))
    else:
        gspec = pl.BlockSpec((N_DEV, br, cols), lambda i: (0, i, 0))
    earlier = [] if into is None else list(into)
    return pl.pallas_call(
        body, name=name, grid=(rows // br,),
        in_specs=[gspec, blk, blk, blk] + [pl.BlockSpec(memory_space=pl.ANY)] * len(earlier),
        out_specs=[blk] * 4, out_shape=[jax.ShapeDtypeStruct(w.shape, F32)] * 4,
        input_output_aliases={4 + k: k for k in range(len(earlier))},
        compiler_params=_params(("parallel",)),
    )(g if parts is None else parts, w, m, v, *earlier)


SMALL = ("norm_mix", "norm_xattn", "norm_ffn", "norm_mem", "norm_final", "pool_w", "pool_scale",
         "ssm_lam_re", "ssm_lam_im", "ssm_log_dt", "ssm_b_re", "ssm_b_im", "ssm_c_re", "ssm_c_im",
         "ffn_conv_b", "ssm_d", "ffn_conv_w")
SMALL_SHARDED = {"ssm_d": 1, "ffn_conv_w": 2}
BIG = ("ab_w_in", "ab_w_out", "ssm_w_in", "ssm_w_glu", "xa_w_q", "xa_w_kv", "xa_w_o", "ffn_w_up", "ffn_w_down")
WEIGHTS = ("norm_mix", "norm_xattn", "norm_ffn", "norm_mem", "norm_final", "ab_w_in", "pool_w", "pool_scale",
           "ab_w_out", "ssm_w_in", "ssm_lam_re", "ssm_lam_im", "ssm_log_dt", "ssm_b_re", "ssm_b_im", "ssm_c_re",
           "ssm_c_im", "ssm_d", "ssm_w_glu", "xa_w_q", "xa_w_kv", "xa_w_o", "ffn_w_up", "ffn_conv_w", "ffn_conv_b",
           "ffn_w_down")


def _rows8(g):
    return g.reshape(N_DEV, g.size // (N_DEV * D_MODEL), D_MODEL)


def _square(a):
    return a.reshape(D_MODEL, D_MODEL)


_LAYOUT = {"ab_w_out": _square, "ssm_w_in": _square, "xa_w_q": _square, "xa_w_o": _square,
           "ffn_w_down": lambda a: a.reshape(N_DEV // 2, FF_SHARD, D_MODEL)}
GATHER_ORDER = (("ab_w_in", 0), ("ab_w_out", 0), ("xa_w_q", 0), ("xa_w_kv", 0), ("xa_w_o", 0), ("ffn_conv_w", None),
                ("ffn_w_up", 0), ("ffn_w_down", 0), ("ssm_d", None), ("ffn_w_up", 1), ("ffn_w_down", 1),
                ("ssm_w_in", 0), ("ssm_w_glu", 0), ("xa_w_q", 1), ("xa_w_kv", 1), ("xa_w_o", 1))
GATHER_AHEAD = 5


class _Step:
    def __init__(self, master, small):
        self.master, self.small = master, small
        self.pending, self.gathers, self.weights, self.sent = [], {}, {}, []

    def follow(self, v):
        for z in self.pending:
            v = v + z
        self.pending = []
        return v

    def start_gathers(self, upto, zero):
        for n, l in GATHER_ORDER[len(self.gathers):upto]:
            if l is None:
                shard = self.master[n] + zero
            else:
                shard = (self.master[n][l] + zero).astype(MXU_DTYPE)
            self.gathers[(n, l)], z = split_start(f"ags_{n}{'' if l is None else l}", shard, gather=True)
            self.pending.append(z)

    def weight(self, n, l, after):
        if (n, l) not in self.weights:
            full, z = split_wait(f"agw_{n}{'' if l is None else l}", self.gathers[(n, l)], after, gather=True)
            self.weights[(n, l)] = _LAYOUT.get(n, lambda a: a)(full)
            self.start_gathers(GATHER_ORDER.index((n, l)) + 1 + GATHER_AHEAD, z)
        return self.weights[(n, l)]

    def send_grad(self, n, l, part):
        h, z = split_start(f"xs_{n}{l}", part, gather=False)
        self.pending.append(z)
        self.sent.append((n, l, h))


def _layer_tail(st, l, x_in, hq, mem_n, acts, next_gain=None):
    bsz, seq = acts["bsz"], acts["seq"]
    p = st.small
    q = mm_nn(f"xa_q{l}", hq, st.weight("xa_w_q", l, x_in))
    kv = mm_nn_bs(f"xa_kv{l}", mem_n, st.weight("xa_w_kv", l, x_in))
    o = xattn_fwd(q, kv, bsz, seq)
    x_mid, hf = mm_nn(f"xa_o{l}", o, st.weight("xa_w_o", l, o), res=x_in, out_dtype=F32,
                      norm_gain=st.follow(p["norm_ffn"][l]))
    up = mm_nn_bs(f"ffn_up{l}", hf, st.weight("ffn_w_up", l, x_mid), stacked_out=True)
    conv_w = st.weight("ffn_conv_w", None, x_mid)[:, l]
    act = conv_fwd(up, conv_w, p["ffn_conv_b"][l], bsz, seq)
    w_down = st.weight("ffn_w_down", l, act)
    if next_gain is None:
        x_out, h_next = mm_as_nn(f"ffn_down{l}", act, w_down, res=x_mid), None
    else:
        x_out, h_next = mm_as_nn(f"ffn_down{l}", act, w_down, res=x_mid, norm_gain=st.follow(next_gain))
    acts[l].update(x_in=x_in, hq=hq, q=q, kv=kv, o=o, x_mid=x_mid, hf=hf, up=up, act=act)
    return x_out, h_next


def _layer_tail_bwd(st, l, dx, mem_n, acts, grads):
    a = acts[l]
    bsz, seq = acts["bsz"], acts["seq"]
    p = st.small
    dact = mm_nt_os(f"d_act{l}", dx, st.weight("ffn_w_down", l, dx))
    st.send_grad("ffn_w_down", l, _rows8(mm_tn(f"g_ffn_down{l}", a["act"], dx, a_stacked=True)))
    conv_w = st.weight("ffn_conv_w", None, dx)[:, l]
    dconv, dcw, dcb = conv_bwd_taps(a["up"], conv_w, p["ffn_conv_b"][l], dact, bsz, seq)
    grads["ffn_conv_w"][l] = dcw
    grads["ffn_conv_b"][l] = dcb
    dup = conv_bwd_input(dconv, conv_w, bsz, seq)
    dx_mid, grads["norm_ffn"][l] = mm_nt_bs(f"d_hf{l}", dup, st.weight("ffn_w_up", l, dx), dc_stacked=True,
                                            rms=(a["x_mid"], st.follow(p["norm_ffn"][l]), dx))
    st.send_grad("ffn_w_up", l, mm_tn(f"g_ffn_up{l}", a["hf"], dup, dc_stacked=True))
    do = mm_nt(f"d_o{l}", dx_mid, st.weight("xa_w_o", l, dx))
    st.send_grad("xa_w_o", l, _rows8(mm_tn(f"g_xa_o{l}", a["o"], dx_mid)))
    dq, dk, dv = xattn_bwd(a["q"], a["kv"], do, bsz, seq)
    dkv = jnp.concatenate([dk, dv], axis=1).astype(BF16)
    dx_in, grads["norm_xattn"][l] = mm_nt(f"d_hq{l}", dq, st.weight("xa_w_q", l, dx),
                                          rms=(a["x_in"], st.follow(p["norm_xattn"][l]), dx_mid))
    st.send_grad("xa_w_q", l, _rows8(mm_tn(f"g_xa_q{l}", a["hq"], dq)))
    dmem_n = mm_nt_bs(f"d_memn{l}", dkv, st.weight("xa_w_kv", l, dx), out_dtype=F32)
    st.send_grad("xa_w_kv", l, mm_tn(f"g_xa_kv{l}", mem_n, dkv, dc_cols=2 * D_MODEL // N_DEV))
    return dx_in, dmem_n


def kernel(x, mem, norm_mix, norm_xattn, norm_ffn, norm_mem, norm_final, ab_w_in, pool_w, pool_scale, ab_w_out, ssm_w_in, ssm_lam_re, ssm_lam_im, ssm_log_dt, ssm_b_re, ssm_b_im, ssm_c_re, ssm_c_im, ssm_d, ssm_w_glu, xa_w_q, xa_w_kv, xa_w_o, ffn_w_up, ffn_conv_w, ffn_conv_b, ffn_w_down, loss_target, m_norm_mix, m_norm_xattn, m_norm_ffn, m_norm_mem, m_norm_final, m_ab_w_in, m_pool_w, m_pool_scale, m_ab_w_out, m_ssm_w_in, m_ssm_lam_re, m_ssm_lam_im, m_ssm_log_dt, m_ssm_b_re, m_ssm_b_im, m_ssm_c_re, m_ssm_c_im, m_ssm_d, m_ssm_w_glu, m_xa_w_q, m_xa_w_kv, m_xa_w_o, m_ffn_w_up, m_ffn_conv_w, m_ffn_conv_b, m_ffn_w_down, v_norm_mix, v_norm_xattn, v_norm_ffn, v_norm_mem, v_norm_final, v_ab_w_in, v_pool_w, v_pool_scale, v_ab_w_out, v_ssm_w_in, v_ssm_lam_re, v_ssm_lam_im, v_ssm_log_dt, v_ssm_b_re, v_ssm_b_im, v_ssm_c_re, v_ssm_c_im, v_ssm_d, v_ssm_w_glu, v_xa_w_q, v_xa_w_kv, v_xa_w_o, v_ffn_w_up, v_ffn_conv_w, v_ffn_conv_b, v_ffn_w_down):
    given = dict(locals())
    master = {n: given[n] for n in WEIGHTS}
    mom1 = {n: given["m_" + n] for n in WEIGHTS}
    mom2 = {n: given["v_" + n] for n in WEIGHTS}
    bsz, seq, d = x.shape
    t = bsz * seq
    me = _my_index()

    st = _Step(master, {"norm_xattn": norm_xattn, "norm_ffn": norm_ffn,
                        "ffn_conv_b": [ffn_conv_b[l].reshape(N_DEV, 1, FF_SHARD) for l in range(2)]})
    st.start_gathers(1, 0.0)
    zero = st.follow(jnp.zeros((), F32))

    acts = {"bsz": bsz, "seq": seq, 0: {}, 1: {}}
    x0 = x.reshape(t, d)
    mem2 = mem.reshape(bsz * MEM_LEN, d)
    mem_n = rms_fwd("rms_mem", mem2, norm_mem + zero)
    pscale = pool_scale.reshape(1, SB_WIDTH)

    h0 = rms_fwd("rms_mix0", x0, norm_mix[0] + zero)
    w_in = st.weight("ab_w_in", 0, h0)
    proj = mm_nn_bs("ab_in", h0, w_in, out_dtype=F32)
    a_out, rsum = sb_attn_fwd(proj, st.follow(jnp.zeros((1, 128), F32)), bsz, seq)
    p_out = pool_fwd(proj, pool_w[0], pscale, bsz, seq)
    w_out = st.weight("ab_w_out", 0, a_out)
    x1 = mm_nn("ab_out_a", a_out, w_out, res=x0, out_dtype=F32)
    x1, hq0 = mm_nn("ab_out_p", p_out, w_out, res=x1, koff=SB_WIDTH, out_dtype=F32,
                    norm_gain=st.follow(norm_xattn[0]))
    x3, h1 = _layer_tail(st, 0, x1, hq0, mem_n, acts, next_gain=norm_mix[1])

    b_re2 = ssm_b_re.reshape(64, 1024)
    b_im2 = ssm_b_im.reshape(64, 1024)
    log_dt = ssm_log_dt.reshape(64, 1)
    lb_re, lb_im, bb_re2, bb_im2 = ssm_prep(ssm_lam_re[0], ssm_lam_im[0], log_dt, b_re2, b_im2)
    wt = _ssm_in_weights(bb_re2, bb_im2)
    ct = _ssm_out_weights(ssm_c_re[0], ssm_c_im[0])
    a_re = lb_re.reshape(1, SSM_STATES)
    a_im = lb_im.reshape(1, SSM_STATES)
    u = mm_nn("ssm_in", h1, st.weight("ssm_w_in", 0, x3), out_dtype=F32)
    dskip = st.weight("ssm_d", None, x3).reshape(1, D_MODEL)
    y, gl, h_re, h_im = ssm_fwd(u, wt, ct, a_re, a_im, dskip, bsz, seq)
    glu = mm_nn_bs("ssm_glu", gl, st.weight("ssm_w_glu", 0, gl), out_dtype=F32)
    x4, hq1 = glu_fwd(glu, x3, st.follow(norm_xattn[1]))
    x6, _ = _layer_tail(st, 1, x4, hq1, mem_n, acts)

    loss_row, dx, g_norm_final = loss_head(x6, norm_final, loss_target.reshape(t, d))
    loss = lax.psum(loss_row[0, 0], MESH_AXES)

    grads = {n: [None, None] for n in ("ffn_conv_w", "ffn_conv_b", "norm_ffn", "norm_xattn", "norm_mix")}
    dx4, dmem_1 = _layer_tail_bwd(st, 1, dx, mem_n, acts, grads)
    dglu = glu_bwd(glu, dx4)
    dgl = mm_nt_bs("d_gl", dglu, st.weight("ssm_w_glu", 0, dx))
    st.send_grad("ssm_w_glu", 0, mm_tn("g_ssm_glu", gl, dglu, dc_cols=2 * D_MODEL // N_DEV))
    du, dwt, dct, g_dskip, da_re, da_im = ssm_bwd(dgl, y, u, h_re, h_im, wt, ct, a_re, a_im, dskip, bsz, seq)
    dbb_re, dbb_im = _ssm_in_weights_bwd(dwt)
    g_c_re, g_c_im = _ssm_out_weights_bwd(dct)
    g_lam_re, g_lam_im, g_log_dt, g_b_re, g_b_im = ssm_prep_bwd(
        ssm_lam_re[0], ssm_lam_im[0], log_dt, b_re2, b_im2, da_re.reshape(64, 64), da_im.reshape(64, 64),
        dbb_re, dbb_im)
    dx3, grads["norm_mix"][1] = mm_nt("d_h1", du, st.weight("ssm_w_in", 0, dx),
                                      rms=(x3, st.follow(norm_mix[1]), dx4))
    st.send_grad("ssm_w_in", 0, _rows8(mm_tn("g_ssm_in", h1, du)))

    dx1, dmem_0 = _layer_tail_bwd(st, 0, dx3, mem_n, acts, grads)
    dcat = mm_nt("d_cat", dx1, st.weight("ab_w_out", 0, dx))
    st.send_grad("ab_w_out", 0, _rows8(jnp.concatenate(
        [mm_tn("g_ab_out_a", a_out, dx1), mm_tn("g_ab_out_p", p_out, dx1)], axis=0)))
    dq, dk, dv = sb_attn_bwd(proj, rsum, dcat, bsz, seq)
    dpu, g_pool_w, g_pool_scale = pool_bwd(proj, pool_w[0], st.follow(pscale), dcat, bsz, seq)
    dproj = jnp.concatenate([dq, dk, dv, dpu], axis=1).astype(BF16)
    st.send_grad("ab_w_in", 0, mm_tn("g_ab_in", h0, dproj, dc_cols=2 * D_MODEL // N_DEV))
    dx0, grads["norm_mix"][0] = mm_nt_bs("d_h0", dproj, st.weight("ab_w_in", 0, dx),
                                         rms=(x0, st.follow(norm_mix[0]), dx1))
    _, g_norm_mem = rms_bwd("rms_mem_bwd", mem2, norm_mem, dmem_0 + dmem_1, need_dx=False)

    stepped = {}
    for n, l, handles in st.sent:
        recv, _ = split_wait(f"xw_{n}{l}", handles, dx0, gather=False)
        shape3 = (master[n].shape[0],) + recv.shape[1:]
        stepped[n] = adamw(f"adamw_{n}{l}", master[n].reshape(shape3), mom1[n].reshape(shape3),
                           mom2[n].reshape(shape3), parts=recv, layer=l, into=stepped.get(n))
    out_g, out_d, out_m, out_v = ({n: stepped[n][k].reshape(master[n].shape) for n in BIG} for k in range(4))

    small_g = {
        "norm_mix": jnp.stack([g[0] for g in grads["norm_mix"]]),
        "norm_xattn": jnp.stack([g[0] for g in grads["norm_xattn"]]),
        "norm_ffn": jnp.stack([g[0] for g in grads["norm_ffn"]]),
        "norm_mem": g_norm_mem[0], "norm_final": g_norm_final[0],
        "pool_w": g_pool_w[None], "pool_scale": g_pool_scale,
        "ssm_lam_re": g_lam_re[None], "ssm_lam_im": g_lam_im[None], "ssm_log_dt": g_log_dt.reshape(1, 64),
        "ssm_b_re": g_b_re.reshape(1, 64, 64, 16), "ssm_b_im": g_b_im.reshape(1, 64, 64, 16),
        "ssm_c_re": g_c_re[None], "ssm_c_im": g_c_im[None],
        "ffn_conv_b": jnp.stack([g.reshape(2 * D_FF) for g in grads["ffn_conv_b"]]),
        "ssm_d": g_dskip,
        "ffn_conv_w": jnp.stack([g.transpose(1, 0, 2).reshape(3, 2 * D_FF) for g in grads["ffn_conv_w"]]),
    }
    sizes = [int(small_g[n].size) for n in SMALL]
    total = sum(sizes)
    rows8 = -(-total // (N_DEV * 128 * 8)) * 8
    flat = jnp.concatenate([small_g[n].reshape(-1).astype(F32) for n in SMALL]
                           + [jnp.zeros((N_DEV * rows8 * 128 - total,), F32)])
    recv = exchange("xch_small", flat.reshape(N_DEV, rows8, 128))
    summed = all_gather("ag_small", sum_parts("sum_small", recv), F32).reshape(-1)

    def local_part(name, a):
        ax = SMALL_SHARDED.get(name)
        if ax is None:
            return a
        n_loc = a.shape[ax] // N_DEV
        return lax.dynamic_slice_in_dim(a, me * n_loc, n_loc, axis=ax)

    off = 0
    for n, sz in zip(SMALL, sizes):
        g_n = local_part(n, summed[off:off + sz].reshape(small_g[n].shape))
        off += sz
        cols = g_n.shape[-1] if g_n.shape[-1] >= 128 or g_n.ndim < 3 else g_n.shape[-1] * g_n.shape[-2]
        shape3 = (1, g_n.size // cols, cols)
        res = adamw("adamw_" + n, master[n].reshape(shape3), mom1[n].reshape(shape3), mom2[n].reshape(shape3),
                    g=g_n.reshape(shape3[1:]))
        for dst, r in zip((out_g, out_d, out_m, out_v), res):
            dst[n] = r.reshape(master[n].shape)

    return (loss, dx0.reshape(bsz, seq, d), *[out_g[n] for n in WEIGHTS], *[out_d[n] for n in WEIGHTS],
            *[out_m[n] for n in WEIGHTS], *[out_v[n] for n in WEIGHTS])
```

```python
import functools
import math

import jax
import jax.numpy as jnp
from jax import lax
from jax.experimental import pallas as pl
from jax.experimental.pallas import tpu as pltpu

F32 = jnp.float32
BF16 = jnp.bfloat16
MXU_DTYPE = jnp.bfloat16
N_DEV = 8
MESH_AXES = ("x", "y", "c")

D_MODEL = 1024
SB_HEAD_DIM = 64
SB_WIDTH = 512
SB_BLOCK = 256
POOL_WINDOWS = (2, 4, 8, 16)
POOL_GROUP = 128
POOL_HALO = 16
SSM_TILES = 8
SSM_TILE_STATES = 512
SSM_STATES = 4096
SSM_LANES = 1024
MEM_LEN = 256
XA_HEADS = 4
XA_HEAD_DIM = 256
D_FF = 2816
FF_SHARD = 704
EPS = 1e-6
ADAM_LR = 0.001
ADAM_B1 = 0.9
ADAM_B2 = 0.999
ADAM_EPS = 1e-08
ADAM_WD = 0.01
ADAM_STEP = 10
VMEM_LIMIT = 56 * 1024 * 1024

_NN = (((1,), (0,)), ((), ()))
_NT = (((1,), (1,)), ((), ()))
_TN = (((0,), (0,)), ((), ()))


def _params(sem=None):
    if sem is None:
        return pltpu.CompilerParams(vmem_limit_bytes=VMEM_LIMIT)
    return pltpu.CompilerParams(dimension_semantics=sem, vmem_limit_bytes=VMEM_LIMIT)


def _tile(n, pref, mult=8):
    if n <= pref:
        return n
    for t in range(pref, 0, -1):
        if n % t == 0 and t % mult == 0:
            return t
    return n


def _dot(a, b, dims):
    return lax.dot_general(a.astype(MXU_DTYPE), b.astype(MXU_DTYPE), dims, preferred_element_type=F32)


def _dot_exact01(x, m01, dims=_NN):
    x1 = x.astype(BF16)
    r1 = x - x1.astype(F32)
    x2 = r1.astype(BF16)
    x3 = (r1 - x2.astype(F32)).astype(BF16)
    m = m01.astype(BF16)
    out = lax.dot_general(x1, m, dims, preferred_element_type=F32)
    out = out + lax.dot_general(x2, m, dims, preferred_element_type=F32)
    return out + lax.dot_general(x3, m, dims, preferred_element_type=F32)


def _mm(name, a, b, dims, grid, a_spec, b_spec, o_spec, out_shape, out_dtype, acc_shape, res=None, r_spec=None,
        group=1, n=None, a_sel="full", b_sel="full", o_sel="full", norm_gain=None, rms=None):
    nk = grid[2]
    if out_dtype is None:
        out_dtype = BF16
    n_out = out_shape[-1]
    vec = pl.BlockSpec((1, n_out), lambda i, j, kk: (0, 0))

    def at(sel, s):
        if sel == "lead":
            return (s,)
        if sel == "lanes":
            return (slice(None), slice(s * n, (s + 1) * n))
        return (Ellipsis,)

    extra = [] if res is None else [(res, r_spec)]
    if norm_gain is not None:
        extra.append((norm_gain.reshape(1, n_out), vec))
    if rms is not None:
        extra += [(rms[0], o_spec), (rms[1].reshape(1, n_out), vec), (rms[2], o_spec)]
    n_in = 2 + len(extra)
    if rms is not None:
        out_specs = [o_spec, vec]
        out_shapes = [jax.ShapeDtypeStruct(out_shape, F32), jax.ShapeDtypeStruct((1, n_out), F32)]
    elif norm_gain is not None:
        out_specs = [o_spec, o_spec]
        out_shapes = [jax.ShapeDtypeStruct(out_shape, out_dtype), jax.ShapeDtypeStruct(out_shape, BF16)]
    else:
        out_specs, out_shapes = o_spec, jax.ShapeDtypeStruct(out_shape, out_dtype)

    def body(*refs):
        a_ref, b_ref = refs[0], refs[1]
        ins = list(refs[2:n_in])
        r_ref = ins.pop(0) if res is not None else None
        outs = refs[n_in:]
        o_ref = outs[0]
        acc = refs[-1] if nk > 1 else None
        k = pl.program_id(2)

        def finish(val):
            if r_ref is not None:
                val = val + r_ref[...].astype(F32)
            if rms is not None:
                x_ref, g_ref, d_ref = ins
                xf = x_ref[...]
                r = lax.rsqrt(jnp.mean(xf * xf, axis=-1, keepdims=True) + EPS)
                xh = xf * r
                part = jnp.sum(val * xh, axis=0, keepdims=True)
                first = pl.program_id(0) == 0

                @pl.when(first)
                def _():
                    outs[1][...] = part

                @pl.when(jnp.logical_not(first))
                def _():
                    outs[1][...] += part

                dxh = val * g_ref[...]
                o_ref[...] = d_ref[...] + r * (dxh - xh * jnp.mean(dxh * xh, axis=-1, keepdims=True))
                return
            o_ref[...] = val.astype(out_dtype)
            if norm_gain is not None:
                r = lax.rsqrt(jnp.mean(val * val, axis=-1, keepdims=True) + EPS)
                outs[1][...] = (val * r * ins[0][...]).astype(BF16)

        def emit(s, val):
            if nk == 1:
                if o_sel == "full":
                    finish(val)
                else:
                    o_ref[at(o_sel, s)] = val.astype(out_dtype)
                return

            @pl.when(k == 0)
            def _():
                acc[at(o_sel, s)] = val

            @pl.when(k > 0)
            def _():
                acc[at(o_sel, s)] += val

        total = None
        for s in range(group):
            val = _dot(a_ref[at(a_sel, s)], b_ref[at(b_sel, s)], dims)
            if o_sel == "full":
                total = val if total is None else total + val
            else:
                emit(s, val)
        if o_sel == "full":
            emit(0, total)
        if nk > 1:
            @pl.when(k == nk - 1)
            def _():
                if o_sel == "full":
                    finish(acc[...])
                else:
                    o_ref[...] = acc[...].astype(out_dtype)

    rows_sem = "arbitrary" if rms is not None else "parallel"
    return pl.pallas_call(
        body, name=name, grid=grid, in_specs=[a_spec, b_spec] + [s for _, s in extra], out_specs=out_specs,
        out_shape=out_shapes, scratch_shapes=[pltpu.VMEM(acc_shape, F32)] if nk > 1 else [],
        compiler_params=_params((rows_sem, rows_sem, "arbitrary")),
    )(a, b, *[x for x, _ in extra])


def _row_tile(m, epi):
    return _tile(m, 512 if epi.get("rms") is not None else 1024)


def mm_nn(name, a, b, res=None, koff=0, out_dtype=None, **epi):
    m, k = a.shape
    n = b.shape[1]
    tm, tn, tk = _row_tile(m, epi), _tile(n, 1024, 128), _tile(k, 1024, 128)
    kb = koff // tk
    spec = pl.BlockSpec((tm, tn), lambda i, j, kk: (i, j))
    return _mm(name, a, b, _NN, (m // tm, n // tn, k // tk),
               pl.BlockSpec((tm, tk), lambda i, j, kk: (i, kk)),
               pl.BlockSpec((tk, tn), lambda i, j, kk: (kk + kb, j)),
               spec, (m, n), out_dtype, (tm, tn), res, spec, **epi)


def mm_nn_bs(name, a, bs, stacked_out=False, out_dtype=None):
    m, k = a.shape
    s, _, n = bs.shape
    tm, tk = _tile(m, 1024), _tile(k, 1024, 128)
    a_spec = pl.BlockSpec((tm, tk), lambda i, j, kk: (i, kk))
    if stacked_out:
        return _mm(name, a, bs, _NN, (m // tm, s, k // tk), a_spec,
                   pl.BlockSpec((None, tk, n), lambda i, j, kk: (j, kk, 0)),
                   pl.BlockSpec((None, tm, n), lambda i, j, kk: (j, i, 0)), (s, m, n), out_dtype, (tm, n))
    g = _tile(s, max(1, 1024 // n), 1)
    return _mm(name, a, bs, _NN, (m // tm, s // g, k // tk), a_spec,
               pl.BlockSpec((g, tk, n), lambda i, j, kk: (j, kk, 0)),
               pl.BlockSpec((tm, g * n), lambda i, j, kk: (i, j)), (m, s * n), out_dtype, (tm, g * n),
               group=g, n=n, b_sel="lead", o_sel="lanes")


def mm_as_nn(name, a_st, b3, res, out_dtype=F32, **epi):
    s, m, kp = a_st.shape
    n = b3.shape[2]
    tm, tn = _row_tile(m, epi), _tile(n, 1024, 128)
    spec = pl.BlockSpec((tm, tn), lambda i, j, kk: (i, j))
    g = _tile(s, 2, 1)
    return _mm(name, a_st, b3, _NN, (m // tm, n // tn, s // g),
               pl.BlockSpec((g, tm, kp), lambda i, j, kk: (kk, i, 0)),
               pl.BlockSpec((g, kp, tn), lambda i, j, kk: (kk, 0, j)),
               spec, (m, n), out_dtype, (tm, tn), res, spec, group=g, a_sel="lead", b_sel="lead", **epi)


def mm_nt(name, dc, b, out_dtype=None, **epi):
    m, n = dc.shape
    k = b.shape[0]
    tm, tko, tnr = _row_tile(m, epi), _tile(k, 1024, 128), _tile(n, 1024, 128)
    return _mm(name, dc, b, _NT, (m // tm, k // tko, n // tnr),
               pl.BlockSpec((tm, tnr), lambda i, j, kk: (i, kk)),
               pl.BlockSpec((tko, tnr), lambda i, j, kk: (j, kk)),
               pl.BlockSpec((tm, tko), lambda i, j, kk: (i, j)), (m, k), out_dtype, (tm, tko), **epi)


def mm_nt_bs(name, dc, bs, dc_stacked=False, out_dtype=None, **epi):
    s, k, n = bs.shape
    m = dc.shape[1] if dc_stacked else dc.shape[0]
    tm, tko = (_tile(m, 1024) if dc_stacked else _row_tile(m, epi)), _tile(k, 1024, 128)
    o_spec = pl.BlockSpec((tm, tko), lambda i, j, kk: (i, j))
    if dc_stacked:
        g = _tile(s, 2, 1)
        return _mm(name, dc, bs, _NT, (m // tm, k // tko, s // g),
                   pl.BlockSpec((g, tm, n), lambda i, j, kk: (kk, i, 0)),
                   pl.BlockSpec((g, tko, n), lambda i, j, kk: (kk, j, 0)), o_spec, (m, k), out_dtype, (tm, tko),
                   group=g, a_sel="lead", b_sel="lead", **epi)
    g = _tile(s, max(1, 2048 // n), 1)
    return _mm(name, dc, bs, _NT, (m // tm, k // tko, s // g),
               pl.BlockSpec((tm, g * n), lambda i, j, kk: (i, kk)),
               pl.BlockSpec((g, tko, n), lambda i, j, kk: (kk, j, 0)), o_spec, (m, k), out_dtype, (tm, tko),
               group=g, n=n, a_sel="lanes", b_sel="lead", **epi)


def mm_nt_os(name, dc, b3, out_dtype=None):
    m, n = dc.shape
    s, kp, _ = b3.shape
    tm, tnr = _tile(m, 1024), _tile(n, 1024, 128)
    return _mm(name, dc, b3, _NT, (m // tm, s, n // tnr),
               pl.BlockSpec((tm, tnr), lambda i, j, kk: (i, kk)),
               pl.BlockSpec((None, kp, tnr), lambda i, j, kk: (j, 0, kk)),
               pl.BlockSpec((None, tm, kp), lambda i, j, kk: (j, i, 0)), (s, m, kp), out_dtype, (tm, kp))


def mm_tn(name, a, dc, a_stacked=False, dc_cols=None, dc_stacked=False, out_dtype=None):
    if a_stacked:
        s, m, kp = a.shape
        n = dc.shape[1]
        tno, tmr = _tile(n, 1024, 128), _tile(m, 2048)
        return _mm(name, a, dc, _TN, (s, n // tno, m // tmr),
                   pl.BlockSpec((None, tmr, kp), lambda i, j, kk: (i, kk, 0)),
                   pl.BlockSpec((tmr, tno), lambda i, j, kk: (kk, j)),
                   pl.BlockSpec((None, kp, tno), lambda i, j, kk: (i, 0, j)), (s, kp, n), out_dtype, (kp, tno))
    m, k = a.shape
    tko, tmr = _tile(k, 1024, 128), _tile(m, 2048)
    a_spec = pl.BlockSpec((tmr, tko), lambda i, j, kk: (kk, i))
    if dc_stacked:
        s, _, n = dc.shape
        return _mm(name, a, dc, _TN, (k // tko, s, m // tmr), a_spec,
                   pl.BlockSpec((None, tmr, n), lambda i, j, kk: (j, kk, 0)),
                   pl.BlockSpec((None, tko, n), lambda i, j, kk: (j, i, 0)), (s, k, n), out_dtype, (tko, n))
    if dc_cols is not None:
        n = dc_cols
        s = dc.shape[1] // n
        g = _tile(s, max(1, 1024 // n), 1)
        return _mm(name, a, dc, _TN, (k // tko, s // g, m // tmr), a_spec,
                   pl.BlockSpec((tmr, g * n), lambda i, j, kk: (kk, j)),
                   pl.BlockSpec((g, tko, n), lambda i, j, kk: (j, i, 0)), (s, k, n), out_dtype, (g, tko, n),
                   group=g, n=n, b_sel="lanes", o_sel="lead")
    n = dc.shape[1]
    tno = _tile(n, 1024, 128)
    return _mm(name, a, dc, _TN, (k // tko, n // tno, m // tmr), a_spec,
               pl.BlockSpec((tmr, tno), lambda i, j, kk: (kk, j)),
               pl.BlockSpec((tko, tno), lambda i, j, kk: (i, j)), (k, n), out_dtype, (tko, tno))


def rms_fwd(name, x, g):
    t, d = x.shape
    tr = _tile(t, 512)

    def body(x_ref, g_ref, o_ref):
        xf = x_ref[...]
        r = lax.rsqrt(jnp.mean(xf * xf, axis=-1, keepdims=True) + EPS)
        o_ref[...] = (xf * r * g_ref[...]).astype(o_ref.dtype)

    return pl.pallas_call(
        body, name=name, grid=(t // tr,),
        in_specs=[pl.BlockSpec((tr, d), lambda i: (i, 0)), pl.BlockSpec((1, d), lambda i: (0, 0))],
        out_specs=pl.BlockSpec((tr, d), lambda i: (i, 0)),
        out_shape=jax.ShapeDtypeStruct((t, d), BF16), compiler_params=_params(("parallel",)),
    )(x, g.reshape(1, d))


def rms_bwd(name, x, g, dh, dres=None, need_dx=True):
    t, d = x.shape
    tr = _tile(t, 512)

    def body(*refs):
        refs = list(refs)
        x_ref, g_ref, dh_ref = refs[:3]
        r_ref = refs[3] if dres is not None else None
        outs = refs[4:] if dres is not None else refs[3:]
        dx_ref, dg_ref = (outs[0], outs[1]) if need_dx else (None, outs[0])
        i = pl.program_id(0)

        @pl.when(i == 0)
        def _():
            dg_ref[...] = jnp.zeros_like(dg_ref)

        xf = x_ref[...]
        dhf = dh_ref[...].astype(F32)
        r = lax.rsqrt(jnp.mean(xf * xf, axis=-1, keepdims=True) + EPS)
        xh = xf * r
        dg_ref[...] += jnp.sum(dhf * xh, axis=0, keepdims=True)
        if need_dx:
            dxh = dhf * g_ref[...]
            dx = r * (dxh - xh * jnp.mean(dxh * xh, axis=-1, keepdims=True))
            if r_ref is not None:
                dx = dx + r_ref[...]
            dx_ref[...] = dx

    row = pl.BlockSpec((tr, d), lambda i: (i, 0))
    vec = pl.BlockSpec((1, d), lambda i: (0, 0))
    in_specs = [row, vec, row] + ([row] if dres is not None else [])
    args = (x, g.reshape(1, d), dh) + ((dres,) if dres is not None else ())
    out_specs = ([row] if need_dx else []) + [vec]
    out_shape = ([jax.ShapeDtypeStruct((t, d), F32)] if need_dx else []) + [jax.ShapeDtypeStruct((1, d), F32)]
    res = pl.pallas_call(
        body, name=name, grid=(t // tr,), in_specs=in_specs, out_specs=out_specs, out_shape=out_shape,
        compiler_params=_params(("arbitrary",)),
    )(*args)
    return res if need_dx else (None, res[0])


def loss_head(x, g, tgt):
    t, d = x.shape
    tr = _tile(t, 512)

    def body(x_ref, g_ref, t_ref, l_ref, dx_ref, dg_ref):
        i = pl.program_id(0)

        @pl.when(i == 0)
        def _():
            l_ref[...] = jnp.zeros_like(l_ref)
            dg_ref[...] = jnp.zeros_like(dg_ref)

        xf = x_ref[...]
        r = lax.rsqrt(jnp.mean(xf * xf, axis=-1, keepdims=True) + EPS)
        xh = xf * r
        diff = xh * g_ref[...] - t_ref[...]
        l_ref[...] += 0.5 * jnp.sum(jnp.mean(diff * diff, axis=-1, keepdims=True))
        dy = diff * (1.0 / d)
        dg_ref[...] += jnp.sum(dy * xh, axis=0, keepdims=True)
        dxh = dy * g_ref[...]
        dx_ref[...] = r * (dxh - xh * jnp.mean(dxh * xh, axis=-1, keepdims=True))

    row = pl.BlockSpec((tr, d), lambda i: (i, 0))
    vec = pl.BlockSpec((1, d), lambda i: (0, 0))
    return pl.pallas_call(
        body, name="loss_head", grid=(t // tr,), in_specs=[row, vec, row],
        out_specs=[pl.BlockSpec((1, 128), lambda i: (0, 0)), row, vec],
        out_shape=[jax.ShapeDtypeStruct((1, 128), F32), jax.ShapeDtypeStruct((t, d), F32),
                   jax.ShapeDtypeStruct((1, d), F32)],
        compiler_params=_params(("arbitrary",)),
    )(x, g.reshape(1, d), tgt)


def glu_fwd(glu, x, gain):
    t, d = x.shape
    tr = _tile(t, 512)

    def body(v_ref, g_ref, x_ref, n_ref, o_ref, h_ref):
        y = x_ref[...] + v_ref[...] * jax.nn.sigmoid(g_ref[...])
        o_ref[...] = y
        r = lax.rsqrt(jnp.mean(y * y, axis=-1, keepdims=True) + EPS)
        h_ref[...] = (y * r * n_ref[...]).astype(h_ref.dtype)

    row = pl.BlockSpec((tr, d), lambda i: (i, 0))
    return pl.pallas_call(
        body, name="glu_fwd", grid=(t // tr,),
        in_specs=[row, pl.BlockSpec((tr, d), lambda i: (i, 1)), row, pl.BlockSpec((1, d), lambda i: (0, 0))],
        out_specs=[row, row],
        out_shape=[jax.ShapeDtypeStruct((t, d), F32), jax.ShapeDtypeStruct((t, d), BF16)],
        compiler_params=_params(("parallel",)),
    )(glu, glu, x, gain.reshape(1, d))


def glu_bwd(glu, dmix):
    t, d = dmix.shape
    tr = _tile(t, 512)

    def body(v_ref, g_ref, d_ref, o_ref):
        sg = jax.nn.sigmoid(g_ref[...])
        dm = d_ref[...]
        o_ref[:, :d] = (dm * sg).astype(o_ref.dtype)
        o_ref[:, d:] = (dm * v_ref[...] * sg * (1.0 - sg)).astype(o_ref.dtype)

    return pl.pallas_call(
        body, name="glu_bwd", grid=(t // tr,),
        in_specs=[pl.BlockSpec((tr, d), lambda i: (i, 0)), pl.BlockSpec((tr, d), lambda i: (i, 1)),
                  pl.BlockSpec((tr, d), lambda i: (i, 0))],
        out_specs=pl.BlockSpec((tr, 2 * d), lambda i: (i, 0)),
        out_shape=jax.ShapeDtypeStruct((t, 2 * d), BF16), compiler_params=_params(("parallel",)),
    )(glu, glu, dmix)


def _head_masks(shape):
    lane = lax.broadcasted_iota(jnp.int32, shape, 1)
    return lane < SB_HEAD_DIM


def _stack_heads(xf, is_a):
    return jnp.concatenate([jnp.where(is_a, xf, 0.0), jnp.where(is_a, 0.0, xf)], axis=0).astype(MXU_DTYPE)


def _diag_mask(qb, row0, rows):
    row = (lax.broadcasted_iota(jnp.int32, (rows, qb), 0) + row0) & (qb - 1)
    col = lax.broadcasted_iota(jnp.int32, (rows, qb), 1)
    return col < row


def _tri01(qb, pred):
    j = lax.broadcasted_iota(jnp.int32, (qb, qb), 0)
    s = lax.broadcasted_iota(jnp.int32, (qb, qb), 1)
    m = pred(j, s).astype(BF16)
    return jnp.concatenate([m, m], axis=0)


def _split_cat(x):
    hi = x.astype(BF16)
    lo = (x - hi.astype(F32)).astype(BF16)
    return jnp.concatenate([hi, lo], axis=1)


def sb_attn_fwd(proj, order, bsz, seq):
    qb = SB_BLOCK
    nq = seq // qb
    npair = SB_WIDTH // 128
    scale = SB_HEAD_DIM ** -0.5

    def body(q_ref, k_ref, v_ref, order_ref, o_ref, r_ref):
        qi = pl.program_id(2)
        is_a = _head_masks((qb, 128))
        q2 = _stack_heads(q_ref[...] * scale, is_a)
        diag = _diag_mask(qb, 0, 2 * qb)
        upper = _tri01(qb, lambda j, s: j > s)

        def logits(kbi):
            ks = pl.ds(pl.multiple_of(kbi * qb, qb), qb)
            return lax.dot_general(q2, k_ref[ks, :].astype(MXU_DTYPE), _NT, preferred_element_type=F32)

        def block(kbi, z, acc, run, masked):
            z_next = logits(jnp.maximum(kbi - 1, 0))
            vblk = v_ref[pl.ds(pl.multiple_of(kbi * qb, qb), qb), :].astype(MXU_DTYPE)
            lk = -jnp.maximum(z, 0.0) - jnp.log(1.0 + jnp.exp(-jnp.abs(z)))
            lb = lk + z
            if masked:
                lk = jnp.where(diag, lk, 0.0)
            after = run + lax.dot_general(_split_cat(lk), upper, _NN, preferred_element_type=F32)
            w = jnp.exp(lb + after)
            if masked:
                w = jnp.where(diag, w, 0.0)
            acc = acc + lax.dot_general(w.astype(MXU_DTYPE), vblk, _NN, preferred_element_type=F32)
            return z_next, acc, run + jnp.sum(lk, axis=1, keepdims=True)

        carry = block(qi, logits(qi), jnp.zeros((2 * qb, 128), F32), jnp.zeros((2 * qb, 1), F32), True)
        _, acc, run = lax.fori_loop(0, qi, lambda i, c: block(qi - 1 - i, c[0], c[1], c[2], False), carry)
        o_ref[...] = jnp.where(is_a, acc[:qb], acc[qb:]).astype(o_ref.dtype)
        r_ref[...] = jnp.where(is_a, run[:qb], run[qb:])

    return pl.pallas_call(
        body, name="sb_attn_fwd", grid=(bsz, npair, nq),
        in_specs=[pl.BlockSpec((qb, 128), lambda b, p, i: (b * nq + i, p)),
                  pl.BlockSpec((seq, 128), lambda b, p, i: (b, npair + p)),
                  pl.BlockSpec((seq, 128), lambda b, p, i: (b, 2 * npair + p)),
                  pl.BlockSpec((1, 128), lambda b, p, i: (0, 0))],
        out_specs=[pl.BlockSpec((qb, 128), lambda b, p, i: (b * nq + i, p)),
                   pl.BlockSpec((qb, 128), lambda b, p, i: (b * nq + i, p))],
        out_shape=[jax.ShapeDtypeStruct((bsz * seq, SB_WIDTH), BF16),
                   jax.ShapeDtypeStruct((bsz * seq, SB_WIDTH), F32)],
        compiler_params=_params(("parallel", "parallel", "arbitrary")),
    )(proj, proj, proj, order)


def sb_attn_bwd(proj, rsum, dcat, bsz, seq):
    qb = SB_BLOCK
    nq = seq // qb
    npair = SB_WIDTH // 128
    scale = SB_HEAD_DIM ** -0.5

    def body(q_ref, k_ref, v_ref, r_ref, do_ref, dq_ref, dk_ref, dv_ref):
        qi = pl.program_id(2)

        @pl.when(qi == 0)
        def _():
            dk_ref[...] = jnp.zeros_like(dk_ref)
            dv_ref[...] = jnp.zeros_like(dv_ref)

        is_a = _head_masks((qb, 128))
        q2 = _stack_heads(q_ref[...] * scale, is_a)
        do2 = _stack_heads(do_ref[...].astype(F32), is_a)
        rf = r_ref[...]
        rtot = jnp.concatenate([rf[:, 0:1], rf[:, SB_HEAD_DIM:SB_HEAD_DIM + 1]], axis=0)
        diag = _diag_mask(qb, 0, 2 * qb)
        incl = _tri01(qb, lambda j, s: j <= s)
        strict = _tri01(qb, lambda j, s: j < s)

        def logits(kbi):
            ks = pl.ds(pl.multiple_of(kbi * qb, qb), qb)
            return lax.dot_general(q2, k_ref[ks, :].astype(MXU_DTYPE), _NT, preferred_element_type=F32)

        def block(kbi, z, dq, pre, epre, masked):
            ks = pl.ds(pl.multiple_of(kbi * qb, qb), qb)
            kblk = k_ref[ks, :].astype(MXU_DTYPE)
            vblk = v_ref[ks, :].astype(MXU_DTYPE)
            z_next = logits(jnp.minimum(kbi + 1, qi))
            dw = lax.dot_general(do2, vblk, _NT, preferred_element_type=F32)
            lk = -jnp.maximum(z, 0.0) - jnp.log(1.0 + jnp.exp(-jnp.abs(z)))
            lb = lk + z
            if masked:
                lk = jnp.where(diag, lk, 0.0)
            after = rtot - (pre + lax.dot_general(_split_cat(lk), incl, _NN, preferred_element_type=F32))
            w = jnp.exp(lb + after)
            if masked:
                w = jnp.where(diag, w, 0.0)
            e = dw * w
            ecum = epre + lax.dot_general(_split_cat(e), strict, _NN, preferred_element_type=F32)
            dz = e - jnp.exp(lb) * (e + ecum)
            if masked:
                dz = jnp.where(diag, dz, 0.0)
            dz = dz.astype(MXU_DTYPE)
            dq = dq + lax.dot_general(dz, kblk, _NN, preferred_element_type=F32)
            dk_ref[ks, :] += lax.dot_general(dz, q2, _TN, preferred_element_type=F32)
            dv_ref[ks, :] += lax.dot_general(w.astype(MXU_DTYPE), do2, _TN, preferred_element_type=F32)
            return (z_next, dq, pre + jnp.sum(lk, axis=1, keepdims=True),
                    epre + jnp.sum(e, axis=1, keepdims=True))

        zc = jnp.zeros((2 * qb, 1), F32)
        carry = lax.fori_loop(0, qi, lambda kbi, c: block(kbi, c[0], c[1], c[2], c[3], False),
                              (logits(0), jnp.zeros((2 * qb, 128), F32), zc, zc))
        dq = block(qi, carry[0], carry[1], carry[2], carry[3], True)[1]
        dq_ref[...] = jnp.where(is_a, dq[:qb], dq[qb:]) * scale

    full = jax.ShapeDtypeStruct((bsz * seq, SB_WIDTH), F32)
    qspec = pl.BlockSpec((qb, 128), lambda b, p, i: (b * nq + i, p))
    return pl.pallas_call(
        body, name="sb_attn_bwd", grid=(bsz, npair, nq),
        in_specs=[qspec,
                  pl.BlockSpec((seq, 128), lambda b, p, i: (b, npair + p)),
                  pl.BlockSpec((seq, 128), lambda b, p, i: (b, 2 * npair + p)),
                  qspec, qspec],
        out_specs=[qspec, pl.BlockSpec((seq, 128), lambda b, p, i: (b, p)),
                   pl.BlockSpec((seq, 128), lambda b, p, i: (b, p))],
        out_shape=[full, full, full],
        compiler_params=_params(("parallel", "parallel", "arbitrary")),
    )(proj, proj, proj, rsum, dcat)


def _window_sums(x, forward):
    n = x.shape[0]
    out = []
    s = x
    for sh in (1, 2, 4, 8):
        s = s + pltpu.roll(s, (n - sh) if forward else sh, 0)
        out.append(s)
    return out


def _pool_counts(tc, c, w):
    t = lax.broadcasted_iota(jnp.int32, (tc, 1), 0) + c * tc
    return jnp.minimum(t + 1, w).astype(F32)


def pool_fwd(proj, pool_w, pool_scale, bsz, seq):
    tc = _tile(seq, 512)
    nc = seq // tc
    hb = tc // POOL_HALO
    ucol = 3

    def body(u_ref, prev_ref, w_ref, s_ref, o_ref):
        c = pl.program_id(1)
        prev = jnp.where(c > 0, prev_ref[...], 0.0)
        x = jnp.concatenate([prev, u_ref[...]], axis=0)
        sums = _window_sums(x, forward=False)
        for g, win in enumerate(POOL_WINDOWS):
            ls = slice(g * POOL_GROUP, (g + 1) * POOL_GROUP)
            pooled = sums[g][POOL_HALO:, ls] / _pool_counts(tc, c, win) - x[POOL_HALO:, ls]
            y = _dot(pooled, w_ref[g], _NN)
            o_ref[:, ls] = (y * s_ref[:, ls]).astype(o_ref.dtype)

    return pl.pallas_call(
        body, name="pool_fwd", grid=(bsz, nc),
        in_specs=[pl.BlockSpec((tc, SB_WIDTH), lambda b, c: (b * nc + c, ucol)),
                  pl.BlockSpec((POOL_HALO, SB_WIDTH), lambda b, c: (jnp.maximum((b * nc + c) * hb - 1, 0), ucol)),
                  pl.BlockSpec((4, POOL_GROUP, POOL_GROUP), lambda b, c: (0, 0, 0)),
                  pl.BlockSpec((1, SB_WIDTH), lambda b, c: (0, 0))],
        out_specs=pl.BlockSpec((tc, SB_WIDTH), lambda b, c: (b * nc + c, 0)),
        out_shape=jax.ShapeDtypeStruct((bsz * seq, SB_WIDTH), BF16),
        compiler_params=_params(("parallel", "parallel")),
    )(proj, proj, pool_w, pool_scale)


def pool_bwd(proj, pool_w, pool_scale, dcat, bsz, seq):
    tc = _tile(seq, 512)
    nc = seq // tc
    hb = tc // POOL_HALO
    nblk = bsz * seq // POOL_HALO
    ucol = 3

    def body(u_ref, prev_ref, dy_ref, nxt_ref, w_ref, s_ref, du_ref, dw_ref, ds_ref):
        b, c = pl.program_id(0), pl.program_id(1)

        @pl.when((b == 0) & (c == 0))
        def _():
            dw_ref[...] = jnp.zeros_like(dw_ref)
            ds_ref[...] = jnp.zeros_like(ds_ref)

        prev = jnp.where(c > 0, prev_ref[...], 0.0)
        x = jnp.concatenate([prev, u_ref[...]], axis=0)
        sums = _window_sums(x, forward=False)
        nxt = jnp.where(c < nc - 1, nxt_ref[...].astype(F32), 0.0)
        dy = jnp.concatenate([dy_ref[...].astype(F32), nxt], axis=0)
        tq = lax.broadcasted_iota(jnp.int32, (tc + POOL_HALO, 1), 0) + c * tc
        for g, win in enumerate(POOL_WINDOWS):
            ls = slice(g * POOL_GROUP, (g + 1) * POOL_GROUP)
            pooled = sums[g][POOL_HALO:, ls] / _pool_counts(tc, c, win) - x[POOL_HALO:, ls]
            y = _dot(pooled, w_ref[g], _NN)
            ds_ref[:, ls] += jnp.sum(dy[:tc, ls] * y, axis=0, keepdims=True)
            dz = dy[:, ls] * s_ref[:, ls]
            dw_ref[g] += _dot(pooled, dz[:tc], _TN)
            dpool = _dot(dz, w_ref[g], _NT)
            dmean = dpool / jnp.minimum(tq + 1, win).astype(F32)
            fsum = _window_sums(dmean, forward=True)[g]
            du_ref[:, ls] = fsum[:tc] - dpool[:tc]

    return pl.pallas_call(
        body, name="pool_bwd", grid=(bsz, nc),
        in_specs=[pl.BlockSpec((tc, SB_WIDTH), lambda b, c: (b * nc + c, ucol)),
                  pl.BlockSpec((POOL_HALO, SB_WIDTH), lambda b, c: (jnp.maximum((b * nc + c) * hb - 1, 0), ucol)),
                  pl.BlockSpec((tc, SB_WIDTH), lambda b, c: (b * nc + c, 1)),
                  pl.BlockSpec((POOL_HALO, SB_WIDTH), lambda b, c: (jnp.minimum((b * nc + c + 1) * hb, nblk - 1), 1)),
                  pl.BlockSpec((4, POOL_GROUP, POOL_GROUP), lambda b, c: (0, 0, 0)),
                  pl.BlockSpec((1, SB_WIDTH), lambda b, c: (0, 0))],
        out_specs=[pl.BlockSpec((tc, SB_WIDTH), lambda b, c: (b * nc + c, 0)),
                   pl.BlockSpec((4, POOL_GROUP, POOL_GROUP), lambda b, c: (0, 0, 0)),
                   pl.BlockSpec((1, SB_WIDTH), lambda b, c: (0, 0))],
        out_shape=[jax.ShapeDtypeStruct((bsz * seq, SB_WIDTH), F32),
                   jax.ShapeDtypeStruct((4, POOL_GROUP, POOL_GROUP), F32),
                   jax.ShapeDtypeStruct((1, SB_WIDTH), F32)],
        compiler_params=_params(("arbitrary", "arbitrary")),
    )(proj, proj, dcat, dcat, pool_w, pool_scale)


def _lbar(lam_re, lam_im, log_dt):
    dt = jnp.exp(log_dt)
    mag = jnp.exp(lam_re * dt)
    ang = lam_im * dt
    return mag * jnp.cos(ang), mag * jnp.sin(ang)


def _bbar(lam_re, lam_im, log_dt, b_re, b_im):
    lb_re, lb_im = _lbar(lam_re, lam_im, log_dt)
    n_re = lb_re - 1.0
    den = lam_re * lam_re + lam_im * lam_im
    coef_re = (n_re * lam_re + lb_im * lam_im) / den
    coef_im = (lb_im * lam_re - n_re * lam_im) / den
    return coef_re * b_re - coef_im * b_im, coef_re * b_im + coef_im * b_re


def _expand01():
    p = lax.broadcasted_iota(jnp.int32, (64, 1024), 0)
    q = lax.broadcasted_iota(jnp.int32, (64, 1024), 1)
    return (lax.shift_right_logical(q, 4) == p).astype(BF16)


def ssm_prep(lam_re, lam_im, log_dt, b_re2, b_im2):
    def body(lr_ref, li_ref, dt_ref, br_ref, bi_ref, ar_ref, ai_ref, bbr_ref, bbi_ref):
        e = _expand01()
        lr, li, dt = lr_ref[...], li_ref[...], dt_ref[...]
        ar_ref[...], ai_ref[...] = _lbar(lr, li, dt)
        bbr_ref[...], bbi_ref[...] = _bbar(_dot_exact01(lr, e), _dot_exact01(li, e), dt, br_ref[...], bi_ref[...])

    s64 = jax.ShapeDtypeStruct((64, 64), F32)
    s1k = jax.ShapeDtypeStruct((64, 1024), F32)
    return pl.pallas_call(body, name="ssm_prep", out_shape=[s64, s64, s1k, s1k], compiler_params=_params())(
        lam_re, lam_im, log_dt, b_re2, b_im2)


def ssm_prep_bwd(lam_re, lam_im, log_dt, b_re2, b_im2, da_re, da_im, dbb_re, dbb_im):
    def body(lr_ref, li_ref, dt_ref, br_ref, bi_ref, dar_ref, dai_ref, dbr_ref, dbi_ref,
             olr_ref, oli_ref, odt_ref, obr_ref, obi_ref):
        e = _expand01()
        lr, li, dt = lr_ref[...], li_ref[...], dt_ref[...]
        _, vjp_a = jax.vjp(_lbar, lr, li, dt)
        g_lr, g_li, g_dt = vjp_a((dar_ref[...], dai_ref[...]))
        _, vjp_b = jax.vjp(_bbar, _dot_exact01(lr, e), _dot_exact01(li, e), dt, br_ref[...], bi_ref[...])
        x_lr, x_li, x_dt, g_br, g_bi = vjp_b((dbr_ref[...], dbi_ref[...]))
        olr_ref[...] = g_lr + _dot_exact01(x_lr, e, _NT)
        oli_ref[...] = g_li + _dot_exact01(x_li, e, _NT)
        odt_ref[...] = g_dt + x_dt
        obr_ref[...] = g_br
        obi_ref[...] = g_bi

    s64 = jax.ShapeDtypeStruct((64, 64), F32)
    s1k = jax.ShapeDtypeStruct((64, 1024), F32)
    return pl.pallas_call(body, name="ssm_prep_bwd",
                          out_shape=[s64, s64, jax.ShapeDtypeStruct((64, 1), F32), s1k, s1k],
                          compiler_params=_params())(
        lam_re, lam_im, log_dt, b_re2, b_im2, da_re, da_im, dbb_re, dbb_im)


def _gelu(y):
    c = math.sqrt(2.0 / math.pi)
    return 0.5 * y * (1.0 + jnp.tanh(c * (y + 0.044715 * y * y * y)))


def _gelu_grad(y):
    c = math.sqrt(2.0 / math.pi)
    th = jnp.tanh(c * (y + 0.044715 * y * y * y))
    return 0.5 * (1.0 + th) + 0.5 * y * (1.0 - th * th) * c * (1.0 + 3.0 * 0.044715 * y * y)


def _cmul(ar, ai, br, bi):
    return ar * br - ai * bi, ar * bi + ai * br


def _scan_tables(ar, ai, reverse, tabs):
    row = lax.broadcasted_iota(jnp.int32, (8, SSM_STATES), 0)
    a1 = (ar, ai)
    a2 = _cmul(*a1, *a1)
    a4 = _cmul(*a2, *a2)
    powers = [a1, a2, _cmul(*a2, *a1), a4]
    powers += [_cmul(*a4, *p) for p in powers]
    for k, (val, sh) in enumerate(((a1, 1), (a2, 2), (a4, 4))):
        keep = (row < 8 - sh) if reverse else (row >= sh)
        tabs[2 * k][...] = jnp.where(keep, val[0], 0.0)
        tabs[2 * k + 1][...] = jnp.where(keep, val[1], 0.0)
    pr = jnp.zeros((8, SSM_STATES), F32)
    pi = jnp.zeros((8, SSM_STATES), F32)
    for r in range(8):
        val = powers[7 - r] if reverse else powers[r]
        pr = jnp.where(row == r, val[0], pr)
        pi = jnp.where(row == r, val[1], pi)
    tabs[6][...] = pr
    tabs[7][...] = pi


def _scan8(xr, xi, tabs, ls, cr, ci, reverse):
    for k, sh in enumerate((1, 2, 4)):
        amt = (8 - sh) if reverse else sh
        sr, si = pltpu.roll(xr, amt, 0), pltpu.roll(xi, amt, 0)
        lr, li = tabs[2 * k][:, ls], tabs[2 * k + 1][:, ls]
        xr, xi = xr + lr * sr - li * si, xi + lr * si + li * sr
    pr, pi = tabs[6][:, ls], tabs[7][:, ls]
    return xr + pr * cr - pi * ci, xi + pr * ci + pi * cr


def _block8(b):
    return pl.ds(pl.multiple_of(b * 8, 8), 8)


def ssm_fwd(u, wt, ct, a_re, a_im, dskip, bsz, seq):
    tc = _tile(seq, 256)
    nc = seq // tc
    ns = SSM_TILE_STATES
    nl = SSM_STATES // SSM_LANES

    def body(u_ref, wt_ref, ct_ref, ar_ref, ai_ref, d_ref, y_ref, gl_ref, hr_ref, hi_ref, sr_ref, si_ref, *tabs):
        b, c = pl.program_id(0), pl.program_id(1)

        @pl.when((b == 0) & (c == 0))
        def _():
            _scan_tables(ar_ref[...], ai_ref[...], False, tabs)

        @pl.when(c == 0)
        def _():
            sr_ref[...] = jnp.zeros_like(sr_ref)
            si_ref[...] = jnp.zeros_like(si_ref)

        uf = u_ref[...]
        for i in range(SSM_TILES):
            bu = _dot(uf[:, i * 128:(i + 1) * 128], wt_ref[i], _NN)
            hr_ref[:, i * ns:(i + 1) * ns] = bu[:, :ns]
            hi_ref[:, i * ns:(i + 1) * ns] = bu[:, ns:]

        def step(blk, carry):
            rows = _block8(blk)
            new = []
            for j in range(nl):
                ls = slice(j * SSM_LANES, (j + 1) * SSM_LANES)
                xr, xi = _scan8(hr_ref[rows, ls], hi_ref[rows, ls], tabs, ls, carry[2 * j], carry[2 * j + 1], False)
                hr_ref[rows, ls] = xr
                hi_ref[rows, ls] = xi
                new += [xr[7:8], xi[7:8]]
            return tuple(new)

        init = []
        for j in range(nl):
            ls = slice(j * SSM_LANES, (j + 1) * SSM_LANES)
            init += [sr_ref[:, ls], si_ref[:, ls]]
        last = lax.fori_loop(0, tc // 8, step, tuple(init), unroll=2)
        for j in range(nl):
            ls = slice(j * SSM_LANES, (j + 1) * SSM_LANES)
            sr_ref[:, ls] = last[2 * j]
            si_ref[:, ls] = last[2 * j + 1]
        for i in range(SSM_TILES):
            hcat = jnp.concatenate([hr_ref[:, i * ns:(i + 1) * ns], hi_ref[:, i * ns:(i + 1) * ns]], axis=1)
            ls = slice(i * 128, (i + 1) * 128)
            y = _dot(hcat, ct_ref[i], _NN) + d_ref[:, ls] * uf[:, ls]
            y_ref[:, ls] = y
            gl_ref[:, ls] = _gelu(y).astype(gl_ref.dtype)

    t = bsz * seq
    row = pl.BlockSpec((tc, D_MODEL), lambda b, c: (b * nc + c, 0))
    st = pl.BlockSpec((tc, SSM_STATES), lambda b, c: (b * nc + c, 0))
    diag = pl.BlockSpec((1, SSM_STATES), lambda b, c: (0, 0))
    return pl.pallas_call(
        body, name="ssm_fwd", grid=(bsz, nc),
        in_specs=[row, pl.BlockSpec((SSM_TILES, 128, 2 * ns), lambda b, c: (0, 0, 0)),
                  pl.BlockSpec((SSM_TILES, 2 * ns, 128), lambda b, c: (0, 0, 0)), diag, diag,
                  pl.BlockSpec((1, D_MODEL), lambda b, c: (0, 0))],
        out_specs=[row, row, st, st],
        out_shape=[jax.ShapeDtypeStruct((t, D_MODEL), F32), jax.ShapeDtypeStruct((t, D_MODEL), BF16),
                   jax.ShapeDtypeStruct((t, SSM_STATES), F32), jax.ShapeDtypeStruct((t, SSM_STATES), F32)],
        scratch_shapes=[pltpu.VMEM((1, SSM_STATES), F32)] * 2 + [pltpu.VMEM((8, SSM_STATES), F32)] * 8,
        compiler_params=_params(("arbitrary", "arbitrary")),
    )(u, wt, ct, a_re, a_im, dskip)


def ssm_bwd(dgl, y, u, h_re, h_im, wt, ct, a_re, a_im, dskip, bsz, seq):
    tc = _tile(seq, 256)
    nc = seq // tc
    nb = tc // 8
    ns = SSM_TILE_STATES
    nl = SSM_STATES // SSM_LANES

    def body(dgl_ref, y_ref, u_ref, hr_ref, hi_ref, pr_ref, pi_ref, wt_ref, ct_ref, ar_ref, ai_ref, d_ref,
             du_ref, dwt_ref, dct_ref, dd_ref, dar_ref, dai_ref, gr_ref, gi_ref, sr_ref, si_ref, ar8_ref, ai8_ref,
             *tabs):
        b, c = pl.program_id(0), pl.program_id(1)

        @pl.when((b == 0) & (c == 0))
        def _():
            for r in (dwt_ref, dct_ref, dd_ref, ar8_ref, ai8_ref):
                r[...] = jnp.zeros_like(r)
            _scan_tables(ar_ref[...], -ai_ref[...], True, tabs)

        @pl.when(c == 0)
        def _():
            sr_ref[...] = jnp.zeros_like(sr_ref)
            si_ref[...] = jnp.zeros_like(si_ref)

        uf = u_ref[...]
        dy = dgl_ref[...].astype(F32) * _gelu_grad(y_ref[...])
        dd_ref[...] += jnp.sum(dy * uf, axis=0, keepdims=True)
        for i in range(SSM_TILES):
            dyi = dy[:, i * 128:(i + 1) * 128]
            dh = _dot(dyi, ct_ref[i], _NT)
            gr_ref[:, i * ns:(i + 1) * ns] = dh[:, :ns]
            gi_ref[:, i * ns:(i + 1) * ns] = dh[:, ns:]
            hcat = jnp.concatenate([hr_ref[:, i * ns:(i + 1) * ns], hi_ref[:, i * ns:(i + 1) * ns]], axis=1)
            dct_ref[i] += _dot(hcat, dyi, _TN)
        row0 = lax.broadcasted_iota(jnp.int32, (8, SSM_LANES), 0) == 0

        def block(blk, carry, before):
            rows = _block8(blk)
            new = []
            for j in range(nl):
                ls = slice(j * SSM_LANES, (j + 1) * SSM_LANES)
                gr, gi = _scan8(gr_ref[rows, ls], gi_ref[rows, ls], tabs, ls, carry[2 * j], carry[2 * j + 1], True)
                gr_ref[rows, ls] = gr
                gi_ref[rows, ls] = gi
                bpr, bpi = before(j)
                hpr = jnp.where(row0, bpr, pltpu.roll(hr_ref[rows, ls], 1, 0))
                hpi = jnp.where(row0, bpi, pltpu.roll(hi_ref[rows, ls], 1, 0))
                ar8_ref[:, ls] += gr * hpr + gi * hpi
                ai8_ref[:, ls] += gi * hpr - gr * hpi
                new += [gr[0:1], gi[0:1]]
            return tuple(new)

        def step(jj, carry):
            blk = nb - 1 - jj
            prev_rows = _block8(blk - 1)

            def before(j):
                ls = slice(j * SSM_LANES, (j + 1) * SSM_LANES)
                return hr_ref[prev_rows, ls][7:8], hi_ref[prev_rows, ls][7:8]

            return block(blk, carry, before)

        init = []
        for j in range(nl):
            ls = slice(j * SSM_LANES, (j + 1) * SSM_LANES)
            init += [sr_ref[:, ls], si_ref[:, ls]]
        carry = lax.fori_loop(0, nb - 1, step, tuple(init))
        first = c == nc - 1

        def before_chunk(j):
            ls = slice(j * SSM_LANES, (j + 1) * SSM_LANES)
            return (jnp.where(first, 0.0, pr_ref[:, ls][7:8]), jnp.where(first, 0.0, pi_ref[:, ls][7:8]))

        last = block(0, carry, before_chunk)
        for j in range(nl):
            ls = slice(j * SSM_LANES, (j + 1) * SSM_LANES)
            sr_ref[:, ls] = last[2 * j]
            si_ref[:, ls] = last[2 * j + 1]
        for i in range(SSM_TILES):
            ls = slice(i * 128, (i + 1) * 128)
            gcat = jnp.concatenate([gr_ref[:, i * ns:(i + 1) * ns], gi_ref[:, i * ns:(i + 1) * ns]], axis=1)
            du_ref[:, ls] = (_dot(gcat, wt_ref[i], _NT) + d_ref[:, ls] * dy[:, ls]).astype(du_ref.dtype)
            dwt_ref[i] += _dot(uf[:, ls], gcat, _TN)

        @pl.when((b == bsz - 1) & (c == nc - 1))
        def _():
            dar_ref[...] = jnp.sum(ar8_ref[...], axis=0, keepdims=True)
            dai_ref[...] = jnp.sum(ai8_ref[...], axis=0, keepdims=True)

    t = bsz * seq
    rev = lambda b, c: (b * nc + (nc - 1 - c), 0)
    row = pl.BlockSpec((tc, D_MODEL), rev)
    st = pl.BlockSpec((tc, SSM_STATES), rev)
    prev = pl.BlockSpec((8, SSM_STATES), lambda b, c: (jnp.maximum((b * nc + (nc - 1 - c)) * nb - 1, 0), 0))
    diag = pl.BlockSpec((1, SSM_STATES), lambda b, c: (0, 0))
    wts = pl.BlockSpec((SSM_TILES, 128, 2 * ns), lambda b, c: (0, 0, 0))
    cts = pl.BlockSpec((SSM_TILES, 2 * ns, 128), lambda b, c: (0, 0, 0))
    vec = pl.BlockSpec((1, D_MODEL), lambda b, c: (0, 0))
    return pl.pallas_call(
        body, name="ssm_bwd", grid=(bsz, nc),
        in_specs=[row, row, row, st, st, prev, prev, wts, cts, diag, diag, vec],
        out_specs=[row, wts, cts, vec, diag, diag],
        out_shape=[jax.ShapeDtypeStruct((t, D_MODEL), BF16),
                   jax.ShapeDtypeStruct((SSM_TILES, 128, 2 * ns), F32),
                   jax.ShapeDtypeStruct((SSM_TILES, 2 * ns, 128), F32),
                   jax.ShapeDtypeStruct((1, D_MODEL), F32),
                   jax.ShapeDtypeStruct((1, SSM_STATES), F32), jax.ShapeDtypeStruct((1, SSM_STATES), F32)],
        scratch_shapes=[pltpu.VMEM((tc, SSM_STATES), F32)] * 2 + [pltpu.VMEM((1, SSM_STATES), F32)] * 2
                       + [pltpu.VMEM((8, SSM_STATES), F32)] * 10,
        compiler_params=_params(("arbitrary", "arbitrary")),
    )(dgl, y, u, h_re, h_im, h_re, h_im, wt, ct, a_re, a_im, dskip)


def _ssm_in_weights(bb_re2, bb_im2):
    eye = jnp.eye(8, dtype=F32)[None, :, None, :, None]

    def one(bb):
        t = bb.reshape(8, 8, 64, 16).transpose(0, 1, 3, 2)
        return (t[:, :, :, None, :] * eye).reshape(8, 128, 512)

    return jnp.concatenate([one(bb_re2), one(bb_im2)], axis=-1).astype(MXU_DTYPE)


def _ssm_in_weights_bwd(dwt):
    eye = jnp.eye(8, dtype=F32)[None, :, None, :, None]

    def one(d):
        t = (d.reshape(8, 8, 16, 8, 64) * eye).sum(axis=3)
        return t.transpose(0, 1, 3, 2).reshape(64, 1024)

    return one(dwt[..., :512]), one(dwt[..., 512:])


def _ssm_out_weights(c_re, c_im):
    eye = jnp.eye(8, dtype=F32)[None, :, None, :, None]

    def one(cc):
        t = cc.reshape(8, 8, 16, 64).transpose(0, 1, 3, 2)
        return (t[:, :, :, None, :] * eye).reshape(8, 512, 128)

    return jnp.concatenate([one(c_re), -one(c_im)], axis=1).astype(MXU_DTYPE)


def _ssm_out_weights_bwd(dct):
    eye = jnp.eye(8, dtype=F32)[None, :, None, :, None]

    def one(d):
        t = (d.reshape(8, 8, 64, 8, 16) * eye).sum(axis=3)
        return t.transpose(0, 1, 3, 2).reshape(64, 16, 64)

    return one(dct[:, :512]), -one(dct[:, 512:])


def _softmax(s):
    m = jnp.max(s, axis=-1, keepdims=True)
    e = jnp.exp(s - m)
    return e / jnp.sum(e, axis=-1, keepdims=True)


def xattn_fwd(q, kv, bsz, seq):
    tq = _tile(seq, 512)
    nq = seq // tq
    scale = XA_HEAD_DIM ** -0.5

    def body(q_ref, k_ref, v_ref, o_ref):
        s = lax.dot_general(q_ref[...], k_ref[...], _NT, preferred_element_type=F32) * scale
        p = _softmax(s)
        o_ref[...] = _dot(p, v_ref[...], _NN).astype(o_ref.dtype)

    qs = pl.BlockSpec((tq, XA_HEAD_DIM), lambda b, h, i: (b * nq + i, h))
    return pl.pallas_call(
        body, name="xattn_fwd", grid=(bsz, XA_HEADS, nq),
        in_specs=[qs, pl.BlockSpec((MEM_LEN, XA_HEAD_DIM), lambda b, h, i: (b, h)),
                  pl.BlockSpec((MEM_LEN, XA_HEAD_DIM), lambda b, h, i: (b, XA_HEADS + h))],
        out_specs=qs, out_shape=jax.ShapeDtypeStruct((bsz * seq, D_MODEL), BF16),
        compiler_params=_params(("parallel", "parallel", "parallel")),
    )(q, kv, kv)


def xattn_bwd(q, kv, do, bsz, seq):
    tq = _tile(seq, 512)
    nq = seq // tq
    scale = XA_HEAD_DIM ** -0.5

    def body(q_ref, k_ref, v_ref, do_ref, dq_ref, dk_ref, dv_ref):
        @pl.when(pl.program_id(2) == 0)
        def _():
            dk_ref[...] = jnp.zeros_like(dk_ref)
            dv_ref[...] = jnp.zeros_like(dv_ref)

        qv, kk, vv, dov = q_ref[...], k_ref[...], v_ref[...], do_ref[...]
        s = lax.dot_general(qv, kk, _NT, preferred_element_type=F32) * scale
        p = _softmax(s)
        dp = lax.dot_general(dov, vv, _NT, preferred_element_type=F32)
        ds = (p * (dp - jnp.sum(dp * p, axis=-1, keepdims=True)) * scale).astype(MXU_DTYPE)
        dq_ref[...] = lax.dot_general(ds, kk, _NN, preferred_element_type=F32).astype(dq_ref.dtype)
        dk_ref[...] += lax.dot_general(ds, qv, _TN, preferred_element_type=F32)
        dv_ref[...] += lax.dot_general(p.astype(MXU_DTYPE), dov, _TN, preferred_element_type=F32)

    qs = pl.BlockSpec((tq, XA_HEAD_DIM), lambda b, h, i: (b * nq + i, h))
    ks = pl.BlockSpec((MEM_LEN, XA_HEAD_DIM), lambda b, h, i: (b, h))
    vs = pl.BlockSpec((MEM_LEN, XA_HEAD_DIM), lambda b, h, i: (b, XA_HEADS + h))
    dkv = jax.ShapeDtypeStruct((bsz * MEM_LEN, D_MODEL), F32)
    dq, dk, dv = pl.pallas_call(
        body, name="xattn_bwd", grid=(bsz, XA_HEADS, nq),
        in_specs=[qs, ks, vs, qs], out_specs=[qs, ks, ks],
        out_shape=[jax.ShapeDtypeStruct((bsz * seq, D_MODEL), BF16), dkv, dkv],
        compiler_params=_params(("parallel", "parallel", "arbitrary")),
    )(q, kv, kv, do)
    return dq, dk, dv


CONV_HALO = 16


def _shifts_down(x, prev):
    h = prev.shape[0]
    ext = jnp.concatenate([prev, x], axis=0)
    return pltpu.roll(ext, 1, 0)[h:], pltpu.roll(ext, 2, 0)[h:]


def _shifts_up(x, nxt):
    rows = x.shape[0]
    n = rows + nxt.shape[0]
    ext = jnp.concatenate([x, nxt], axis=0)
    return pltpu.roll(ext, n - 1, 0)[:rows], pltpu.roll(ext, n - 2, 0)[:rows]


def _conv_taps(u, u1, u2, w, b):
    return b + w[2:3] * u + w[1:2] * u1 + w[0:1] * u2


def conv_fwd(up, cw, cb, bsz, seq):
    tc = _tile(seq, 512)
    nc = seq // tc
    hb = tc // CONV_HALO
    half = N_DEV // 2

    def body(uv_ref, ug_ref, pv_ref, pg_ref, wv_ref, wg_ref, bv_ref, bg_ref, o_ref):
        c = pl.program_id(2)
        pv = jnp.where(c > 0, pv_ref[...].astype(F32), 0.0)
        pg = jnp.where(c > 0, pg_ref[...].astype(F32), 0.0)
        uv, ug = uv_ref[...].astype(F32), ug_ref[...].astype(F32)
        val = _conv_taps(uv, *_shifts_down(uv, pv), wv_ref[...], bv_ref[...])
        gate = _conv_taps(ug, *_shifts_down(ug, pg), wg_ref[...], bg_ref[...])
        o_ref[...] = (gate * jax.nn.sigmoid(gate) * val).astype(o_ref.dtype)

    def cur(off):
        return pl.BlockSpec((None, tc, FF_SHARD), lambda b, j, c: (j + off, b * nc + c, 0))

    def prv(off):
        return pl.BlockSpec((None, CONV_HALO, FF_SHARD), lambda b, j, c: (j + off, jnp.maximum((b * nc + c) * hb - 1, 0), 0))

    def par(rows, off):
        return pl.BlockSpec((None, rows, FF_SHARD), lambda b, j, c: (j + off, 0, 0))

    return pl.pallas_call(
        body, name="conv_fwd", grid=(bsz, half, nc),
        in_specs=[cur(0), cur(half), prv(0), prv(half), par(3, 0), par(3, half), par(1, 0), par(1, half)],
        out_specs=cur(0), out_shape=jax.ShapeDtypeStruct((half, bsz * seq, FF_SHARD), BF16),
        compiler_params=_params(("parallel", "parallel", "parallel")),
    )(up, up, up, up, cw, cw, cb, cb)


def conv_bwd_taps(up, cw, cb, dact, bsz, seq):
    tc = _tile(seq, 512)
    nc = seq // tc
    hb = tc // CONV_HALO
    half = N_DEV // 2

    def body(uv_ref, ug_ref, pv_ref, pg_ref, wv_ref, wg_ref, bv_ref, bg_ref, da_ref,
             dc_ref, dwv_ref, dwg_ref, dbv_ref, dbg_ref):
        b, c = pl.program_id(1), pl.program_id(2)

        @pl.when((b == 0) & (c == 0))
        def _():
            for r in (dwv_ref, dwg_ref, dbv_ref, dbg_ref):
                r[...] = jnp.zeros_like(r)

        pv = jnp.where(c > 0, pv_ref[...].astype(F32), 0.0)
        pg = jnp.where(c > 0, pg_ref[...].astype(F32), 0.0)
        uv, ug = uv_ref[...].astype(F32), ug_ref[...].astype(F32)
        uv1, uv2 = _shifts_down(uv, pv)
        ug1, ug2 = _shifts_down(ug, pg)
        val = _conv_taps(uv, uv1, uv2, wv_ref[...], bv_ref[...])
        gate = _conv_taps(ug, ug1, ug2, wg_ref[...], bg_ref[...])
        sg = jax.nn.sigmoid(gate)
        da = da_ref[...].astype(F32)
        dsilu = da * sg
        dval = dsilu * gate
        dgate = dsilu * val * (1.0 + gate * (1.0 - sg))
        dc_ref[0] = dval.astype(dc_ref.dtype)
        dc_ref[1] = dgate.astype(dc_ref.dtype)
        for dcv, taps, dw_ref, db_ref in ((dval, (uv2, uv1, uv), dwv_ref, dbv_ref),
                                          (dgate, (ug2, ug1, ug), dwg_ref, dbg_ref)):
            db_ref[...] += jnp.sum(dcv, axis=0, keepdims=True)
            for k, u_k in enumerate(taps):
                dw_ref[k:k + 1, :] += jnp.sum(dcv * u_k, axis=0, keepdims=True)

    def cur(off):
        return pl.BlockSpec((None, tc, FF_SHARD), lambda j, b, c: (j + off, b * nc + c, 0))

    def prv(off):
        return pl.BlockSpec((None, CONV_HALO, FF_SHARD), lambda j, b, c: (j + off, jnp.maximum((b * nc + c) * hb - 1, 0), 0))

    def par(rows, off):
        return pl.BlockSpec((None, rows, FF_SHARD), lambda j, b, c: (j + off, 0, 0))

    t = bsz * seq
    hs = jax.ShapeDtypeStruct((2, half, t, FF_SHARD), BF16)
    ws = jax.ShapeDtypeStruct((half, 3, FF_SHARD), F32)
    bs = jax.ShapeDtypeStruct((half, 1, FF_SHARD), F32)
    dc, dwv, dwg, dbv, dbg = pl.pallas_call(
        body, name="conv_bwd_taps", grid=(half, bsz, nc),
        in_specs=[cur(0), cur(half), prv(0), prv(half), par(3, 0), par(3, half), par(1, 0), par(1, half), cur(0)],
        out_specs=[pl.BlockSpec((2, None, tc, FF_SHARD), lambda j, b, c: (0, j, b * nc + c, 0)),
                   par(3, 0), par(3, 0), par(1, 0), par(1, 0)],
        out_shape=[hs, ws, ws, bs, bs],
        compiler_params=_params(("parallel", "arbitrary", "arbitrary")),
    )(up, up, up, up, cw, cw, cb, cb, dact)
    return (dc.reshape(N_DEV, t, FF_SHARD), jnp.concatenate([dwv, dwg], axis=0),
            jnp.concatenate([dbv, dbg], axis=0))


def conv_bwd_input(dconv, cw, bsz, seq):
    tc = _tile(seq, 1024)
    nc = seq // tc
    hb = tc // CONV_HALO
    nblk = bsz * seq // CONV_HALO

    def body(d_ref, n_ref, w_ref, o_ref):
        c = pl.program_id(2)
        nxt = jnp.where(c < nc - 1, n_ref[...].astype(F32), 0.0)
        d = d_ref[...].astype(F32)
        d1, d2 = _shifts_up(d, nxt)
        w = w_ref[...]
        o_ref[...] = (w[2:3] * d + w[1:2] * d1 + w[0:1] * d2).astype(o_ref.dtype)

    cur = pl.BlockSpec((None, tc, FF_SHARD), lambda j, b, c: (j, b * nc + c, 0))
    return pl.pallas_call(
        body, name="conv_bwd_input", grid=(N_DEV, bsz, nc),
        in_specs=[cur, pl.BlockSpec((None, CONV_HALO, FF_SHARD),
                                    lambda j, b, c: (j, jnp.minimum((b * nc + c + 1) * hb, nblk - 1), 0)),
                  pl.BlockSpec((None, 3, FF_SHARD), lambda j, b, c: (j, 0, 0))],
        out_specs=cur, out_shape=jax.ShapeDtypeStruct(dconv.shape, BF16),
        compiler_params=_params(("parallel", "parallel", "parallel")),
    )(dconv, dconv, cw)


def _my_index():
    return 4 * lax.axis_index("x") + 2 * lax.axis_index("y") + lax.axis_index("c")


def _peer(k):
    return (lax.axis_index("x") ^ ((k >> 2) & 1), lax.axis_index("y") ^ ((k >> 1) & 1),
            lax.axis_index("c") ^ (k & 1))


def all_gather(name, a, out_dtype):
    def body(a_ref, o_ref, stage, send_sems, recv_sems, local_sem):
        me = _my_index()
        stage[...] = a_ref[...].astype(out_dtype)
        local = pltpu.make_async_copy(stage, o_ref.at[me], local_sem)
        local.start()
        sends = []
        for k in range(1, N_DEV):
            cp = pltpu.make_async_remote_copy(
                src_ref=stage, dst_ref=o_ref.at[me], send_sem=send_sems.at[k - 1], recv_sem=recv_sems.at[k - 1],
                device_id=_peer(k), device_id_type=pl.DeviceIdType.MESH)
            cp.start()
            sends.append(cp)
        for k in range(1, N_DEV):
            pltpu.make_async_remote_copy(
                src_ref=stage, dst_ref=o_ref.at[me ^ k], send_sem=send_sems.at[k - 1], recv_sem=recv_sems.at[k - 1],
                device_id=_peer(k), device_id_type=pl.DeviceIdType.MESH).wait_recv()
        for cp in sends:
            cp.wait_send()
        local.wait()

    return pl.pallas_call(
        body, name=name, in_specs=[pl.BlockSpec(memory_space=pltpu.VMEM)],
        out_specs=pl.BlockSpec(memory_space=pltpu.HBM),
        out_shape=jax.ShapeDtypeStruct((N_DEV,) + a.shape, out_dtype),
        scratch_shapes=[pltpu.VMEM(a.shape, out_dtype), pltpu.SemaphoreType.DMA((N_DEV - 1,)),
                        pltpu.SemaphoreType.DMA((N_DEV - 1,)), pltpu.SemaphoreType.DMA],
        compiler_params=pltpu.CompilerParams(vmem_limit_bytes=VMEM_LIMIT),
    )(a)


def exchange(name, g):
    def body(g_ref, r_ref, send_sems, recv_sems, local_sem):
        me = _my_index()
        local = pltpu.make_async_copy(g_ref.at[me], r_ref.at[me], local_sem)
        local.start()
        sends = []
        for k in range(1, N_DEV):
            cp = pltpu.make_async_remote_copy(
                src_ref=g_ref.at[me ^ k], dst_ref=r_ref.at[me], send_sem=send_sems.at[k - 1],
                recv_sem=recv_sems.at[k - 1], device_id=_peer(k), device_id_type=pl.DeviceIdType.MESH)
            cp.start()
            sends.append(cp)
        for k in range(1, N_DEV):
            pltpu.make_async_remote_copy(
                src_ref=g_ref.at[me], dst_ref=r_ref.at[me ^ k], send_sem=send_sems.at[k - 1],
                recv_sem=recv_sems.at[k - 1], device_id=_peer(k), device_id_type=pl.DeviceIdType.MESH).wait_recv()
        for cp in sends:
            cp.wait_send()
        local.wait()

    return pl.pallas_call(
        body, name=name, in_specs=[pl.BlockSpec(memory_space=pltpu.HBM)],
        out_specs=pl.BlockSpec(memory_space=pltpu.HBM),
        out_shape=jax.ShapeDtypeStruct(g.shape, g.dtype),
        scratch_shapes=[pltpu.SemaphoreType.DMA((N_DEV - 1,)), pltpu.SemaphoreType.DMA((N_DEV - 1,)),
                        pltpu.SemaphoreType.DMA],
    )(g)


_HBM = pl.BlockSpec(memory_space=pltpu.HBM)
_SEM = pl.BlockSpec(memory_space=pltpu.SEMAPHORE)
_DATAFLOW = pltpu.SideEffectType.DATAFLOW_SIDE_EFFECTING


def _split_copies(gather, src_ref, land_ref, send_sems, recv_sems, local_sem):
    me = _my_index()

    def part(j):
        return src_ref if gather else src_ref.at[j]

    local = pltpu.make_async_copy(part(me), land_ref.at[me], local_sem)
    sends = [pltpu.make_async_remote_copy(
        src_ref=part(me ^ k), dst_ref=land_ref.at[me], send_sem=send_sems.at[k - 1], recv_sem=recv_sems.at[k - 1],
        device_id=_peer(k), device_id_type=pl.DeviceIdType.MESH) for k in range(1, N_DEV)]
    recvs = [pltpu.make_async_remote_copy(
        src_ref=part(me ^ k), dst_ref=land_ref.at[me ^ k], send_sem=send_sems.at[k - 1], recv_sem=recv_sems.at[k - 1],
        device_id=_peer(k), device_id_type=pl.DeviceIdType.MESH) for k in range(1, N_DEV)]
    return local, sends, recvs


def split_start(name, src, gather):
    land_shape = ((N_DEV,) + src.shape) if gather else src.shape

    def body(src_ref, land_ref, send_sems, recv_sems, local_sem, src_thru, land_thru, token):
        local, sends, _ = _split_copies(gather, src_ref, land_ref, send_sems, recv_sems, local_sem)
        local.start()
        for cp in sends:
            cp.start()
        token[...] = jnp.zeros_like(token)

    dma7 = pltpu.SemaphoreType.DMA((N_DEV - 1,))
    out = pl.pallas_call(
        body, name=name,
        out_shape=(dma7, dma7, pltpu.SemaphoreType.DMA(()), pltpu.HBM(src.shape, src.dtype),
                   pltpu.HBM(land_shape, src.dtype), jax.ShapeDtypeStruct((8, 128), F32)),
        in_specs=(_HBM, _HBM), out_specs=(_SEM, _SEM, _SEM, _HBM, _HBM, pl.BlockSpec(memory_space=pltpu.VMEM)),
        input_output_aliases={0: 3, 1: 4},
        compiler_params=pltpu.CompilerParams(has_side_effects=_DATAFLOW),
    )(pltpu.with_memory_space_constraint(src, pltpu.HBM),
      pltpu.with_memory_space_constraint(lax.empty(land_shape, src.dtype), pltpu.HBM))
    return out[:5], out[5][0, 0]


def split_wait(name, handles, after, gather):
    send_sems, recv_sems, local_sem, src_thru, land_thru = handles

    def body(src_ref, land_ref, send_sems, recv_sems, local_sem, after_ref, src_dead, got_ref, token):
        local, sends, recvs = _split_copies(gather, src_ref, land_ref, send_sems, recv_sems, local_sem)
        local.wait()
        for cp in recvs:
            cp.wait_send()
            cp.wait_recv()
        token[...] = jnp.zeros_like(token)

    out = pl.pallas_call(
        body, name=name,
        out_shape=(pltpu.HBM(src_thru.shape, src_thru.dtype), pltpu.HBM(land_thru.shape, land_thru.dtype),
                   jax.ShapeDtypeStruct((8, 128), F32)),
        in_specs=(_HBM, _HBM, _SEM, _SEM, _SEM, pl.BlockSpec(memory_space=pl.ANY)),
        out_specs=(_HBM, _HBM, pl.BlockSpec(memory_space=pltpu.VMEM)),
        input_output_aliases={0: 0, 1: 1},
        compiler_params=pltpu.CompilerParams(has_side_effects=_DATAFLOW),
    )(src_thru, land_thru, send_sems, recv_sems, local_sem, after)
    return out[1], out[2][0, 0]


def sum_parts(name, r):
    _, rows, cols = r.shape

    def body(r_ref, o_ref):
        acc = r_ref[0].astype(F32)
        for s in range(1, N_DEV):
            acc = acc + r_ref[s].astype(F32)
        o_ref[...] = acc

    return pl.pallas_call(body, name=name, out_shape=jax.ShapeDtypeStruct((rows, cols), F32),
                          compiler_params=_params())(r)


def adamw(name, w, m, v, parts=None, g=None, layer=0, into=None):
    _, rows, cols = w.shape
    br = _tile(rows, 256, 16)
    c1 = 1.0 / (1.0 - ADAM_B1 ** ADAM_STEP)
    c2 = 1.0 / (1.0 - ADAM_B2 ** ADAM_STEP)

    def body(g_ref, w_ref, m_ref, v_ref, *rest):
        og_ref, od_ref, om_ref, ov_ref = rest[-4:]
        if parts is None:
            gs = g_ref[...]
        else:
            gs = g_ref[0].astype(F32)
            for s in range(1, N_DEV):
                gs = gs + g_ref[s].astype(F32)
        mn = ADAM_B1 * m_ref[...] + (1.0 - ADAM_B1) * gs
        vn = ADAM_B2 * v_ref[...] + (1.0 - ADAM_B2) * (gs * gs)
        og_ref[...] = gs
        om_ref[...] = mn
        ov_ref[...] = vn
        od_ref[...] = -ADAM_LR * ((mn * c1) / (jnp.sqrt(vn * c2) + ADAM_EPS) + ADAM_WD * w_ref[...])

    blk = pl.BlockSpec((None, br, cols), lambda i: (layer, i, 0))
    if parts is None:
        gspec = pl.BlockSpec((br, cols), lambda i: (i, 0))
    else:
        gspec = pl.BlockSpec((N_DEV, br, cols), lambda i: (0, i, 0))
    earlier = [] if into is None else list(into)
    return pl.pallas_call(
        body, name=name, grid=(rows // br,),
        in_specs=[gspec, blk, blk, blk] + [pl.BlockSpec(memory_space=pl.ANY)] * len(earlier),
        out_specs=[blk] * 4, out_shape=[jax.ShapeDtypeStruct(w.shape, F32)] * 4,
        input_output_aliases={4 + k: k for k in range(len(earlier))},
        compiler_params=_params(("parallel",)),
    )(g if parts is None else parts, w, m, v, *earlier)


SMALL = ("norm_mix", "norm_xattn", "norm_ffn", "norm_mem", "norm_final", "pool_w", "pool_scale",
         "ssm_lam_re", "ssm_lam_im", "ssm_log_dt", "ssm_b_re", "ssm_b_im", "ssm_c_re", "ssm_c_im",
         "ffn_conv_b", "ssm_d", "ffn_conv_w")
SMALL_SHARDED = {"ssm_d": 1, "ffn_conv_w": 2}
BIG = ("ab_w_in", "ab_w_out", "ssm_w_in", "ssm_w_glu", "xa_w_q", "xa_w_kv", "xa_w_o", "ffn_w_up", "ffn_w_down")
WEIGHTS = ("norm_mix", "norm_xattn", "norm_ffn", "norm_mem", "norm_final", "ab_w_in", "pool_w", "pool_scale",
           "ab_w_out", "ssm_w_in", "ssm_lam_re", "ssm_lam_im", "ssm_log_dt", "ssm_b_re", "ssm_b_im", "ssm_c_re",
           "ssm_c_im", "ssm_d", "ssm_w_glu", "xa_w_q", "xa_w_kv", "xa_w_o", "ffn_w_up", "ffn_conv_w", "ffn_conv_b",
           "ffn_w_down")


def _rows8(g):
    return g.reshape(N_DEV, g.size // (N_DEV * D_MODEL), D_MODEL)


def _square(a):
    return a.reshape(D_MODEL, D_MODEL)


_LAYOUT = {"ab_w_out": _square, "ssm_w_in": _square, "xa_w_q": _square, "xa_w_o": _square,
           "ffn_w_down": lambda a: a.reshape(N_DEV // 2, FF_SHARD, D_MODEL)}
GATHER_ORDER = (("ab_w_in", 0), ("ffn_conv_w", None), ("ssm_d", None), ("ab_w_out", 0), ("xa_w_q", 0),
                ("xa_w_kv", 0), ("xa_w_o", 0), ("ffn_w_up", 0), ("ffn_w_down", 0), ("ffn_w_up", 1),
                ("ffn_w_down", 1), ("ssm_w_in", 0), ("ssm_w_glu", 0), ("xa_w_q", 1), ("xa_w_kv", 1), ("xa_w_o", 1))
GATHER_FIRST = 3
GATHER_AHEAD = 7


class _Step:
    def __init__(self, master, small):
        self.master, self.small = master, small
        self.pending, self.gathers, self.weights, self.sent = [], {}, {}, []

    def follow(self, v):
        for z in self.pending:
            v = v + z
        self.pending = []
        return v

    def start_gathers(self, upto, zero):
        for n, l in GATHER_ORDER[len(self.gathers):upto]:
            if l is None:
                shard = self.master[n] + zero
            else:
                shard = (self.master[n][l] + zero).astype(MXU_DTYPE)
            self.gathers[(n, l)], z = split_start(f"ags_{n}{'' if l is None else l}", shard, gather=True)
            self.pending.append(z)

    def weight(self, n, l, after):
        if (n, l) not in self.weights:
            full, z = split_wait(f"agw_{n}{'' if l is None else l}", self.gathers[(n, l)], after, gather=True)
            self.weights[(n, l)] = _LAYOUT.get(n, lambda a: a)(full)
            self.start_gathers(GATHER_ORDER.index((n, l)) + 1 + GATHER_AHEAD, z)
        return self.weights[(n, l)]

    def send_grad(self, n, l, part):
        h, z = split_start(f"xs_{n}{l}", part, gather=False)
        self.pending.append(z)
        self.sent.append((n, l, h))


def _layer_tail(st, l, x_in, hq, mem_n, acts, next_gain=None):
    bsz, seq = acts["bsz"], acts["seq"]
    p = st.small
    q = mm_nn(f"xa_q{l}", hq, st.weight("xa_w_q", l, x_in))
    kv = mm_nn_bs(f"xa_kv{l}", mem_n, st.weight("xa_w_kv", l, x_in))
    o = xattn_fwd(q, kv, bsz, seq)
    x_mid, hf = mm_nn(f"xa_o{l}", o, st.weight("xa_w_o", l, o), res=x_in, out_dtype=F32,
                      norm_gain=st.follow(p["norm_ffn"][l]))
    up = mm_nn_bs(f"ffn_up{l}", hf, st.weight("ffn_w_up", l, x_mid), stacked_out=True)
    conv_w = st.weight("ffn_conv_w", None, x_mid)[:, l]
    act = conv_fwd(up, conv_w, p["ffn_conv_b"][l], bsz, seq)
    w_down = st.weight("ffn_w_down", l, act)
    if next_gain is None:
        x_out, h_next = mm_as_nn(f"ffn_down{l}", act, w_down, res=x_mid), None
    else:
        x_out, h_next = mm_as_nn(f"ffn_down{l}", act, w_down, res=x_mid, norm_gain=st.follow(next_gain))
    acts[l].update(x_in=x_in, hq=hq, q=q, kv=kv, o=o, x_mid=x_mid, hf=hf, up=up, act=act)
    return x_out, h_next


def _layer_tail_bwd(st, l, dx, mem_n, acts, grads):
    a = acts[l]
    bsz, seq = acts["bsz"], acts["seq"]
    p = st.small
    dact = mm_nt_os(f"d_act{l}", dx, st.weight("ffn_w_down", l, dx))
    st.send_grad("ffn_w_down", l, _rows8(mm_tn(f"g_ffn_down{l}", a["act"], dx, a_stacked=True)))
    conv_w = st.weight("ffn_conv_w", None, dx)[:, l]
    dconv, dcw, dcb = conv_bwd_taps(a["up"], conv_w, p["ffn_conv_b"][l], dact, bsz, seq)
    grads["ffn_conv_w"][l] = dcw
    grads["ffn_conv_b"][l] = dcb
    dup = conv_bwd_input(dconv, conv_w, bsz, seq)
    dx_mid, grads["norm_ffn"][l] = mm_nt_bs(f"d_hf{l}", dup, st.weight("ffn_w_up", l, dx), dc_stacked=True,
                                            rms=(a["x_mid"], st.follow(p["norm_ffn"][l]), dx))
    st.send_grad("ffn_w_up", l, mm_tn(f"g_ffn_up{l}", a["hf"], dup, dc_stacked=True))
    do = mm_nt(f"d_o{l}", dx_mid, st.weight("xa_w_o", l, dx))
    st.send_grad("xa_w_o", l, _rows8(mm_tn(f"g_xa_o{l}", a["o"], dx_mid)))
    dq, dk, dv = xattn_bwd(a["q"], a["kv"], do, bsz, seq)
    dkv = jnp.concatenate([dk, dv], axis=1).astype(BF16)
    dx_in, grads["norm_xattn"][l] = mm_nt(f"d_hq{l}", dq, st.weight("xa_w_q", l, dx),
                                          rms=(a["x_in"], st.follow(p["norm_xattn"][l]), dx_mid))
    st.send_grad("xa_w_q", l, _rows8(mm_tn(f"g_xa_q{l}", a["hq"], dq)))
    dmem_n = mm_nt_bs(f"d_memn{l}", dkv, st.weight("xa_w_kv", l, dx), out_dtype=F32)
    st.send_grad("xa_w_kv", l, mm_tn(f"g_xa_kv{l}", mem_n, dkv, dc_cols=2 * D_MODEL // N_DEV))
    return dx_in, dmem_n


def kernel(x, mem, norm_mix, norm_xattn, norm_ffn, norm_mem, norm_final, ab_w_in, pool_w, pool_scale, ab_w_out, ssm_w_in, ssm_lam_re, ssm_lam_im, ssm_log_dt, ssm_b_re, ssm_b_im, ssm_c_re, ssm_c_im, ssm_d, ssm_w_glu, xa_w_q, xa_w_kv, xa_w_o, ffn_w_up, ffn_conv_w, ffn_conv_b, ffn_w_down, loss_target, m_norm_mix, m_norm_xattn, m_norm_ffn, m_norm_mem, m_norm_final, m_ab_w_in, m_pool_w, m_pool_scale, m_ab_w_out, m_ssm_w_in, m_ssm_lam_re, m_ssm_lam_im, m_ssm_log_dt, m_ssm_b_re, m_ssm_b_im, m_ssm_c_re, m_ssm_c_im, m_ssm_d, m_ssm_w_glu, m_xa_w_q, m_xa_w_kv, m_xa_w_o, m_ffn_w_up, m_ffn_conv_w, m_ffn_conv_b, m_ffn_w_down, v_norm_mix, v_norm_xattn, v_norm_ffn, v_norm_mem, v_norm_final, v_ab_w_in, v_pool_w, v_pool_scale, v_ab_w_out, v_ssm_w_in, v_ssm_lam_re, v_ssm_lam_im, v_ssm_log_dt, v_ssm_b_re, v_ssm_b_im, v_ssm_c_re, v_ssm_c_im, v_ssm_d, v_ssm_w_glu, v_xa_w_q, v_xa_w_kv, v_xa_w_o, v_ffn_w_up, v_ffn_conv_w, v_ffn_conv_b, v_ffn_w_down):
    given = dict(locals())
    master = {n: given[n] for n in WEIGHTS}
    mom1 = {n: given["m_" + n] for n in WEIGHTS}
    mom2 = {n: given["v_" + n] for n in WEIGHTS}
    bsz, seq, d = x.shape
    t = bsz * seq
    me = _my_index()

    st = _Step(master, {"norm_xattn": norm_xattn, "norm_ffn": norm_ffn,
                        "ffn_conv_b": [ffn_conv_b[l].reshape(N_DEV, 1, FF_SHARD) for l in range(2)]})
    st.start_gathers(GATHER_FIRST, 0.0)
    zero = st.follow(jnp.zeros((), F32))

    acts = {"bsz": bsz, "seq": seq, 0: {}, 1: {}}
    x0 = x.reshape(t, d)
    mem2 = mem.reshape(bsz * MEM_LEN, d)
    mem_n = rms_fwd("rms_mem", mem2, norm_mem + zero)
    pscale = pool_scale.reshape(1, SB_WIDTH)

    h0 = rms_fwd("rms_mix0", x0, norm_mix[0] + zero)
    w_in = st.weight("ab_w_in", 0, h0)
    proj = mm_nn_bs("ab_in", h0, w_in, out_dtype=F32)
    a_out, rsum = sb_attn_fwd(proj, st.follow(jnp.zeros((1, 128), F32)), bsz, seq)
    p_out = pool_fwd(proj, pool_w[0], pscale, bsz, seq)
    w_out = st.weight("ab_w_out", 0, a_out)
    x1 = mm_nn("ab_out_a", a_out, w_out, res=x0, out_dtype=F32)
    x1, hq0 = mm_nn("ab_out_p", p_out, w_out, res=x1, koff=SB_WIDTH, out_dtype=F32,
                    norm_gain=st.follow(norm_xattn[0]))
    x3, h1 = _layer_tail(st, 0, x1, hq0, mem_n, acts, next_gain=norm_mix[1])

    b_re2 = ssm_b_re.reshape(64, 1024)
    b_im2 = ssm_b_im.reshape(64, 1024)
    log_dt = ssm_log_dt.reshape(64, 1)
    lb_re, lb_im, bb_re2, bb_im2 = ssm_prep(ssm_lam_re[0], ssm_lam_im[0], log_dt, b_re2, b_im2)
    wt = _ssm_in_weights(bb_re2, bb_im2)
    ct = _ssm_out_weights(ssm_c_re[0], ssm_c_im[0])
    a_re = lb_re.reshape(1, SSM_STATES)
    a_im = lb_im.reshape(1, SSM_STATES)
    u = mm_nn("ssm_in", h1, st.weight("ssm_w_in", 0, x3), out_dtype=F32)
    dskip = st.weight("ssm_d", None, x3).reshape(1, D_MODEL)
    y, gl, h_re, h_im = ssm_fwd(u, wt, ct, a_re, a_im, dskip, bsz, seq)
    glu = mm_nn_bs("ssm_glu", gl, st.weight("ssm_w_glu", 0, gl), out_dtype=F32)
    x4, hq1 = glu_fwd(glu, x3, st.follow(norm_xattn[1]))
    x6, _ = _layer_tail(st, 1, x4, hq1, mem_n, acts)

    loss_row, dx, g_norm_final = loss_head(x6, norm_final, loss_target.reshape(t, d))
    loss = lax.psum(loss_row[0, 0], MESH_AXES)

    grads = {n: [None, None] for n in ("ffn_conv_w", "ffn_conv_b", "norm_ffn", "norm_xattn", "norm_mix")}
    dx4, dmem_1 = _layer_tail_bwd(st, 1, dx, mem_n, acts, grads)
    dglu = glu_bwd(glu, dx4)
    dgl = mm_nt_bs("d_gl", dglu, st.weight("ssm_w_glu", 0, dx))
    st.send_grad("ssm_w_glu", 0, mm_tn("g_ssm_glu", gl, dglu, dc_cols=2 * D_MODEL // N_DEV))
    du, dwt, dct, g_dskip, da_re, da_im = ssm_bwd(dgl, y, u, h_re, h_im, wt, ct, a_re, a_im, dskip, bsz, seq)
    dbb_re, dbb_im = _ssm_in_weights_bwd(dwt)
    g_c_re, g_c_im = _ssm_out_weights_bwd(dct)
    g_lam_re, g_lam_im, g_log_dt, g_b_re, g_b_im = ssm_prep_bwd(
        ssm_lam_re[0], ssm_lam_im[0], log_dt, b_re2, b_im2, da_re.reshape(64, 64), da_im.reshape(64, 64),
        dbb_re, dbb_im)
    dx3, grads["norm_mix"][1] = mm_nt("d_h1", du, st.weight("ssm_w_in", 0, dx),
                                      rms=(x3, st.follow(norm_mix[1]), dx4))
    st.send_grad("ssm_w_in", 0, _rows8(mm_tn("g_ssm_in", h1, du)))

    dx1, dmem_0 = _layer_tail_bwd(st, 0, dx3, mem_n, acts, grads)
    dcat = mm_nt("d_cat", dx1, st.weight("ab_w_out", 0, dx))
    st.send_grad("ab_w_out", 0, _rows8(jnp.concatenate(
        [mm_tn("g_ab_out_a", a_out, dx1), mm_tn("g_ab_out_p", p_out, dx1)], axis=0)))
    dq, dk, dv = sb_attn_bwd(proj, rsum, dcat, bsz, seq)
    dpu, g_pool_w, g_pool_scale = pool_bwd(proj, pool_w[0], st.follow(pscale), dcat, bsz, seq)
    dproj = jnp.concatenate([dq, dk, dv, dpu], axis=1).astype(BF16)
    st.send_grad("ab_w_in", 0, mm_tn("g_ab_in", h0, dproj, dc_cols=2 * D_MODEL // N_DEV))
    dx0, grads["norm_mix"][0] = mm_nt_bs("d_h0", dproj, st.weight("ab_w_in", 0, dx),
                                         rms=(x0, st.follow(norm_mix[0]), dx1))
    _, g_norm_mem = rms_bwd("rms_mem_bwd", mem2, norm_mem, dmem_0 + dmem_1, need_dx=False)

    stepped = {}
    for n, l, handles in st.sent:
        recv, _ = split_wait(f"xw_{n}{l}", handles, dx0, gather=False)
        shape3 = (master[n].shape[0],) + recv.shape[1:]
        stepped[n] = adamw(f"adamw_{n}{l}", master[n].reshape(shape3), mom1[n].reshape(shape3),
                           mom2[n].reshape(shape3), parts=recv, layer=l, into=stepped.get(n))
    out_g, out_d, out_m, out_v = ({n: stepped[n][k].reshape(master[n].shape) for n in BIG} for k in range(4))

    small_g = {
        "norm_mix": jnp.stack([g[0] for g in grads["norm_mix"]]),
        "norm_xattn": jnp.stack([g[0] for g in grads["norm_xattn"]]),
        "norm_ffn": jnp.stack([g[0] for g in grads["norm_ffn"]]),
        "norm_mem": g_norm_mem[0], "norm_final": g_norm_final[0],
        "pool_w": g_pool_w[None], "pool_scale": g_pool_scale,
        "ssm_lam_re": g_lam_re[None], "ssm_lam_im": g_lam_im[None], "ssm_log_dt": g_log_dt.reshape(1, 64),
        "ssm_b_re": g_b_re.reshape(1, 64, 64, 16), "ssm_b_im": g_b_im.reshape(1, 64, 64, 16),
        "ssm_c_re": g_c_re[None], "ssm_c_im": g_c_im[None],
        "ffn_conv_b": jnp.stack([g.reshape(2 * D_FF) for g in grads["ffn_conv_b"]]),
        "ssm_d": g_dskip,
        "ffn_conv_w": jnp.stack([g.transpose(1, 0, 2).reshape(3, 2 * D_FF) for g in grads["ffn_conv_w"]]),
    }
    sizes = [int(small_g[n].size) for n in SMALL]
    total = sum(sizes)
    rows8 = -(-total // (N_DEV * 128 * 8)) * 8
    flat = jnp.concatenate([small_g[n].reshape(-1).astype(F32) for n in SMALL]
                           + [jnp.zeros((N_DEV * rows8 * 128 - total,), F32)])
    recv = exchange("xch_small", flat.reshape(N_DEV, rows8, 128))
    summed = all_gather("ag_small", sum_parts("sum_small", recv), F32).reshape(-1)

    def local_part(name, a):
        ax = SMALL_SHARDED.get(name)
        if ax is None:
            return a
        n_loc = a.shape[ax] // N_DEV
        return lax.dynamic_slice_in_dim(a, me * n_loc, n_loc, axis=ax)

    off = 0
    for n, sz in zip(SMALL, sizes):
        g_n = local_part(n, summed[off:off + sz].reshape(small_g[n].shape))
        off += sz
        cols = g_n.shape[-1] if g_n.shape[-1] >= 128 or g_n.ndim < 3 else g_n.shape[-1] * g_n.shape[-2]
        shape3 = (1, g_n.size // cols, cols)
        res = adamw("adamw_" + n, master[n].reshape(shape3), mom1[n].reshape(shape3), mom2[n].reshape(shape3),
                    g=g_n.reshape(shape3[1:]))
        for dst, r in zip((out_g, out_d, out_m, out_v), res):
            dst[n] = r.reshape(master[n].shape)

    return (loss, dx0.reshape(bsz, seq, d), *[out_g[n] for n in WEIGHTS], *[out_d[n] for n in WEIGHTS],
            *[out_m[n] for n in WEIGHTS], *[out_v[n] for n in WEIGHTS])
```

```python
import math

import jax
import jax.numpy as jnp
from jax import lax
from jax.experimental import pallas as pl
from jax.experimental.pallas import tpu as pltpu

F32 = jnp.float32
BF16 = jnp.bfloat16
MXU_DTYPE = jnp.bfloat16
N_DEV = 8
MESH_AXES = ("x", "y", "c")

D_MODEL = 1024
SB_HEAD_DIM = 64
SB_WIDTH = 512
SB_BLOCK = 256
POOL_WINDOWS = (2, 4, 8, 16)
POOL_GROUP = 128
POOL_HALO = 16
SSM_TILES = 8
SSM_TILE_STATES = 512
SSM_STATES = 4096
SSM_LANES = 1024
MEM_LEN = 256
XA_HEADS = 4
XA_HEAD_DIM = 256
D_FF = 2816
FF_SHARD = 704
EPS = 1e-6
ADAM_LR = 0.001
ADAM_B1 = 0.9
ADAM_B2 = 0.999
ADAM_EPS = 1e-08
ADAM_WD = 0.01
ADAM_STEP = 10
VMEM_LIMIT = 56 * 1024 * 1024

_NN = (((1,), (0,)), ((), ()))
_NT = (((1,), (1,)), ((), ()))
_TN = (((0,), (0,)), ((), ()))


def _params(sem=None):
    if sem is None:
        return pltpu.CompilerParams(vmem_limit_bytes=VMEM_LIMIT)
    return pltpu.CompilerParams(dimension_semantics=sem, vmem_limit_bytes=VMEM_LIMIT)


def _tile(n, pref, mult=8):
    if n <= pref:
        return n
    for t in range(pref, 0, -1):
        if n % t == 0 and t % mult == 0:
            return t
    return n


def _dot(a, b, dims):
    return lax.dot_general(a.astype(MXU_DTYPE), b.astype(MXU_DTYPE), dims, preferred_element_type=F32)


def _dot_exact01(x, m01, dims=_NN):
    x1 = x.astype(BF16)
    r1 = x - x1.astype(F32)
    x2 = r1.astype(BF16)
    x3 = (r1 - x2.astype(F32)).astype(BF16)
    m = m01.astype(BF16)
    out = lax.dot_general(x1, m, dims, preferred_element_type=F32)
    out = out + lax.dot_general(x2, m, dims, preferred_element_type=F32)
    return out + lax.dot_general(x3, m, dims, preferred_element_type=F32)


def _mm(name, a, b, dims, grid, a_spec, b_spec, o_spec, out_shape, out_dtype, acc_shape, res=None, r_spec=None,
        group=1, n=None, a_sel="full", b_sel="full", o_sel="full", norm_gain=None, rms=None):
    nk = grid[2]
    if out_dtype is None:
        out_dtype = BF16
    n_out = out_shape[-1]
    vec = pl.BlockSpec((1, n_out), lambda i, j, kk: (0, 0))

    def at(sel, s):
        if sel == "lead":
            return (s,)
        if sel == "lanes":
            return (slice(None), slice(s * n, (s + 1) * n))
        return (Ellipsis,)

    extra = [] if res is None else [(res, r_spec)]
    if norm_gain is not None:
        extra.append((norm_gain.reshape(1, n_out), vec))
    if rms is not None:
        extra += [(rms[0], o_spec), (rms[1].reshape(1, n_out), vec), (rms[2], o_spec)]
    n_in = 2 + len(extra)
    if rms is not None:
        out_specs = [o_spec, vec]
        out_shapes = [jax.ShapeDtypeStruct(out_shape, F32), jax.ShapeDtypeStruct((1, n_out), F32)]
    elif norm_gain is not None:
        out_specs = [o_spec, o_spec]
        out_shapes = [jax.ShapeDtypeStruct(out_shape, out_dtype), jax.ShapeDtypeStruct(out_shape, BF16)]
    else:
        out_specs, out_shapes = o_spec, jax.ShapeDtypeStruct(out_shape, out_dtype)

    def body(*refs):
        a_ref, b_ref = refs[0], refs[1]
        ins = list(refs[2:n_in])
        r_ref = ins.pop(0) if res is not None else None
        outs = refs[n_in:]
        o_ref = outs[0]
        acc = refs[-1] if nk > 1 else None
        k = pl.program_id(2)

        def finish(val):
            if r_ref is not None:
                val = val + r_ref[...].astype(F32)
            if rms is not None:
                x_ref, g_ref, d_ref = ins
                xf = x_ref[...]
                r = lax.rsqrt(jnp.mean(xf * xf, axis=-1, keepdims=True) + EPS)
                xh = xf * r
                part = jnp.sum(val * xh, axis=0, keepdims=True)
                first = pl.program_id(0) == 0

                @pl.when(first)
                def _():
                    outs[1][...] = part

                @pl.when(jnp.logical_not(first))
                def _():
                    outs[1][...] += part

                dxh = val * g_ref[...]
                o_ref[...] = d_ref[...] + r * (dxh - xh * jnp.mean(dxh * xh, axis=-1, keepdims=True))
                return
            o_ref[...] = val.astype(out_dtype)
            if norm_gain is not None:
                r = lax.rsqrt(jnp.mean(val * val, axis=-1, keepdims=True) + EPS)
                outs[1][...] = (val * r * ins[0][...]).astype(BF16)

        def emit(s, val):
            if nk == 1:
                if o_sel == "full":
                    finish(val)
                else:
                    o_ref[at(o_sel, s)] = val.astype(out_dtype)
                return

            @pl.when(k == 0)
            def _():
                acc[at(o_sel, s)] = val

            @pl.when(k > 0)
            def _():
                acc[at(o_sel, s)] += val

        total = None
        for s in range(group):
            val = _dot(a_ref[at(a_sel, s)], b_ref[at(b_sel, s)], dims)
            if o_sel == "full":
                total = val if total is None else total + val
            else:
                emit(s, val)
        if o_sel == "full":
            emit(0, total)
        if nk > 1:
            @pl.when(k == nk - 1)
            def _():
                if o_sel == "full":
                    finish(acc[...])
                else:
                    o_ref[...] = acc[...].astype(out_dtype)

    rows_sem = "arbitrary" if rms is not None else "parallel"
    return pl.pallas_call(
        body, name=name, grid=grid, in_specs=[a_spec, b_spec] + [s for _, s in extra], out_specs=out_specs,
        out_shape=out_shapes, scratch_shapes=[pltpu.VMEM(acc_shape, F32)] if nk > 1 else [],
        compiler_params=_params((rows_sem, rows_sem, "arbitrary")),
    )(a, b, *[x for x, _ in extra])


def _row_tile(m, epi):
    return _tile(m, 512 if epi.get("rms") is not None else 1024)


def mm_nn(name, a, b, res=None, koff=0, out_dtype=None, **epi):
    m, k = a.shape
    n = b.shape[1]
    tm, tn, tk = _row_tile(m, epi), _tile(n, 1024, 128), _tile(k, 1024, 128)
    kb = koff // tk
    spec = pl.BlockSpec((tm, tn), lambda i, j, kk: (i, j))
    return _mm(name, a, b, _NN, (m // tm, n // tn, k // tk),
               pl.BlockSpec((tm, tk), lambda i, j, kk: (i, kk)),
               pl.BlockSpec((tk, tn), lambda i, j, kk: (kk + kb, j)),
               spec, (m, n), out_dtype, (tm, tn), res, spec, **epi)


def mm_nn_bs(name, a, bs, stacked_out=False, out_dtype=None):
    m, k = a.shape
    s, _, n = bs.shape
    tm, tk = _tile(m, 1024), _tile(k, 1024, 128)
    a_spec = pl.BlockSpec((tm, tk), lambda i, j, kk: (i, kk))
    if stacked_out:
        return _mm(name, a, bs, _NN, (m // tm, s, k // tk), a_spec,
                   pl.BlockSpec((None, tk, n), lambda i, j, kk: (j, kk, 0)),
                   pl.BlockSpec((None, tm, n), lambda i, j, kk: (j, i, 0)), (s, m, n), out_dtype, (tm, n))
    g = _tile(s, max(1, 1024 // n), 1)
    return _mm(name, a, bs, _NN, (m // tm, s // g, k // tk), a_spec,
               pl.BlockSpec((g, tk, n), lambda i, j, kk: (j, kk, 0)),
               pl.BlockSpec((tm, g * n), lambda i, j, kk: (i, j)), (m, s * n), out_dtype, (tm, g * n),
               group=g, n=n, b_sel="lead", o_sel="lanes")


def mm_as_nn(name, a_st, b3, res, out_dtype=F32, **epi):
    s, m, kp = a_st.shape
    n = b3.shape[2]
    tm, tn = _row_tile(m, epi), _tile(n, 1024, 128)
    spec = pl.BlockSpec((tm, tn), lambda i, j, kk: (i, j))
    g = _tile(s, 2, 1)
    return _mm(name, a_st, b3, _NN, (m // tm, n // tn, s // g),
               pl.BlockSpec((g, tm, kp), lambda i, j, kk: (kk, i, 0)),
               pl.BlockSpec((g, kp, tn), lambda i, j, kk: (kk, 0, j)),
               spec, (m, n), out_dtype, (tm, tn), res, spec, group=g, a_sel="lead", b_sel="lead", **epi)


def mm_nt(name, dc, b, out_dtype=None, **epi):
    m, n = dc.shape
    k = b.shape[0]
    tm, tko, tnr = _row_tile(m, epi), _tile(k, 1024, 128), _tile(n, 1024, 128)
    return _mm(name, dc, b, _NT, (m // tm, k // tko, n // tnr),
               pl.BlockSpec((tm, tnr), lambda i, j, kk: (i, kk)),
               pl.BlockSpec((tko, tnr), lambda i, j, kk: (j, kk)),
               pl.BlockSpec((tm, tko), lambda i, j, kk: (i, j)), (m, k), out_dtype, (tm, tko), **epi)


def mm_nt_bs(name, dc, bs, dc_stacked=False, out_dtype=None, **epi):
    s, k, n = bs.shape
    m = dc.shape[1] if dc_stacked else dc.shape[0]
    tm, tko = (_tile(m, 1024) if dc_stacked else _row_tile(m, epi)), _tile(k, 1024, 128)
    o_spec = pl.BlockSpec((tm, tko), lambda i, j, kk: (i, j))
    if dc_stacked:
        g = _tile(s, 2, 1)
        return _mm(name, dc, bs, _NT, (m // tm, k // tko, s // g),
                   pl.BlockSpec((g, tm, n), lambda i, j, kk: (kk, i, 0)),
                   pl.BlockSpec((g, tko, n), lambda i, j, kk: (kk, j, 0)), o_spec, (m, k), out_dtype, (tm, tko),
                   group=g, a_sel="lead", b_sel="lead", **epi)
    g = _tile(s, max(1, 2048 // n), 1)
    return _mm(name, dc, bs, _NT, (m // tm, k // tko, s // g),
               pl.BlockSpec((tm, g * n), lambda i, j, kk: (i, kk)),
               pl.BlockSpec((g, tko, n), lambda i, j, kk: (kk, j, 0)), o_spec, (m, k), out_dtype, (tm, tko),
               group=g, n=n, a_sel="lanes", b_sel="lead", **epi)


def mm_nt_os(name, dc, b3, out_dtype=None):
    m, n = dc.shape
    s, kp, _ = b3.shape
    tm, tnr = _tile(m, 1024), _tile(n, 1024, 128)
    return _mm(name, dc, b3, _NT, (m // tm, s, n // tnr),
               pl.BlockSpec((tm, tnr), lambda i, j, kk: (i, kk)),
               pl.BlockSpec((None, kp, tnr), lambda i, j, kk: (j, 0, kk)),
               pl.BlockSpec((None, tm, kp), lambda i, j, kk: (j, i, 0)), (s, m, kp), out_dtype, (tm, kp))


def mm_tn(name, a, dc, a_stacked=False, dc_cols=None, dc_stacked=False, out_dtype=None):
    if a_stacked:
        s, m, kp = a.shape
        n = dc.shape[1]
        tno, tmr = _tile(n, 1024, 128), _tile(m, 2048)
        return _mm(name, a, dc, _TN, (s, n // tno, m // tmr),
                   pl.BlockSpec((None, tmr, kp), lambda i, j, kk: (i, kk, 0)),
                   pl.BlockSpec((tmr, tno), lambda i, j, kk: (kk, j)),
                   pl.BlockSpec((None, kp, tno), lambda i, j, kk: (i, 0, j)), (s, kp, n), out_dtype, (kp, tno))
    m, k = a.shape
    tko, tmr = _tile(k, 1024, 128), _tile(m, 2048)
    a_spec = pl.BlockSpec((tmr, tko), lambda i, j, kk: (kk, i))
    if dc_stacked:
        s, _, n = dc.shape
        return _mm(name, a, dc, _TN, (k // tko, s, m // tmr), a_spec,
                   pl.BlockSpec((None, tmr, n), lambda i, j, kk: (j, kk, 0)),
                   pl.BlockSpec((None, tko, n), lambda i, j, kk: (j, i, 0)), (s, k, n), out_dtype, (tko, n))
    if dc_cols is not None:
        n = dc_cols
        s = dc.shape[1] // n
        g = _tile(s, max(1, 1024 // n), 1)
        return _mm(name, a, dc, _TN, (k // tko, s // g, m // tmr), a_spec,
                   pl.BlockSpec((tmr, g * n), lambda i, j, kk: (kk, j)),
                   pl.BlockSpec((g, tko, n), lambda i, j, kk: (j, i, 0)), (s, k, n), out_dtype, (g, tko, n),
                   group=g, n=n, b_sel="lanes", o_sel="lead")
    n = dc.shape[1]
    tno = _tile(n, 1024, 128)
    return _mm(name, a, dc, _TN, (k // tko, n // tno, m // tmr), a_spec,
               pl.BlockSpec((tmr, tno), lambda i, j, kk: (kk, j)),
               pl.BlockSpec((tko, tno), lambda i, j, kk: (i, j)), (k, n), out_dtype, (tko, tno))


def rms_fwd(name, x, g):
    t, d = x.shape
    tr = _tile(t, 512)

    def body(x_ref, g_ref, o_ref):
        xf = x_ref[...]
        r = lax.rsqrt(jnp.mean(xf * xf, axis=-1, keepdims=True) + EPS)
        o_ref[...] = (xf * r * g_ref[...]).astype(o_ref.dtype)

    return pl.pallas_call(
        body, name=name, grid=(t // tr,),
        in_specs=[pl.BlockSpec((tr, d), lambda i: (i, 0)), pl.BlockSpec((1, d), lambda i: (0, 0))],
        out_specs=pl.BlockSpec((tr, d), lambda i: (i, 0)),
        out_shape=jax.ShapeDtypeStruct((t, d), BF16), compiler_params=_params(("parallel",)),
    )(x, g.reshape(1, d))


def rms_bwd(name, x, g, dh, dres=None, need_dx=True):
    t, d = x.shape
    tr = _tile(t, 512)

    def body(*refs):
        refs = list(refs)
        x_ref, g_ref, dh_ref = refs[:3]
        r_ref = refs[3] if dres is not None else None
        outs = refs[4:] if dres is not None else refs[3:]
        dx_ref, dg_ref = (outs[0], outs[1]) if need_dx else (None, outs[0])
        i = pl.program_id(0)

        @pl.when(i == 0)
        def _():
            dg_ref[...] = jnp.zeros_like(dg_ref)

        xf = x_ref[...]
        dhf = dh_ref[...].astype(F32)
        r = lax.rsqrt(jnp.mean(xf * xf, axis=-1, keepdims=True) + EPS)
        xh = xf * r
        dg_ref[...] += jnp.sum(dhf * xh, axis=0, keepdims=True)
        if need_dx:
            dxh = dhf * g_ref[...]
            dx = r * (dxh - xh * jnp.mean(dxh * xh, axis=-1, keepdims=True))
            if r_ref is not None:
                dx = dx + r_ref[...]
            dx_ref[...] = dx

    row = pl.BlockSpec((tr, d), lambda i: (i, 0))
    vec = pl.BlockSpec((1, d), lambda i: (0, 0))
    in_specs = [row, vec, row] + ([row] if dres is not None else [])
    args = (x, g.reshape(1, d), dh) + ((dres,) if dres is not None else ())
    out_specs = ([row] if need_dx else []) + [vec]
    out_shape = ([jax.ShapeDtypeStruct((t, d), F32)] if need_dx else []) + [jax.ShapeDtypeStruct((1, d), F32)]
    res = pl.pallas_call(
        body, name=name, grid=(t // tr,), in_specs=in_specs, out_specs=out_specs, out_shape=out_shape,
        compiler_params=_params(("arbitrary",)),
    )(*args)
    return res if need_dx else (None, res[0])


def loss_head(x, g, tgt):
    t, d = x.shape
    tr = _tile(t, 512)

    def body(x_ref, g_ref, t_ref, l_ref, dx_ref, dg_ref):
        i = pl.program_id(0)

        @pl.when(i == 0)
        def _():
            l_ref[...] = jnp.zeros_like(l_ref)
            dg_ref[...] = jnp.zeros_like(dg_ref)

        xf = x_ref[...]
        r = lax.rsqrt(jnp.mean(xf * xf, axis=-1, keepdims=True) + EPS)
        xh = xf * r
        diff = xh * g_ref[...] - t_ref[...]
        l_ref[...] += 0.5 * jnp.sum(jnp.mean(diff * diff, axis=-1, keepdims=True))
        dy = diff * (1.0 / d)
        dg_ref[...] += jnp.sum(dy * xh, axis=0, keepdims=True)
        dxh = dy * g_ref[...]
        dx_ref[...] = r * (dxh - xh * jnp.mean(dxh * xh, axis=-1, keepdims=True))

    row = pl.BlockSpec((tr, d), lambda i: (i, 0))
    vec = pl.BlockSpec((1, d), lambda i: (0, 0))
    return pl.pallas_call(
        body, name="loss_head", grid=(t // tr,), in_specs=[row, vec, row],
        out_specs=[pl.BlockSpec((1, 128), lambda i: (0, 0)), row, vec],
        out_shape=[jax.ShapeDtypeStruct((1, 128), F32), jax.ShapeDtypeStruct((t, d), F32),
                   jax.ShapeDtypeStruct((1, d), F32)],
        compiler_params=_params(("arbitrary",)),
    )(x, g.reshape(1, d), tgt)


def glu_fwd(glu, x, gain):
    t, d = x.shape
    tr = _tile(t, 512)

    def body(v_ref, g_ref, x_ref, n_ref, o_ref, h_ref):
        y = x_ref[...] + v_ref[...] * jax.nn.sigmoid(g_ref[...])
        o_ref[...] = y
        r = lax.rsqrt(jnp.mean(y * y, axis=-1, keepdims=True) + EPS)
        h_ref[...] = (y * r * n_ref[...]).astype(h_ref.dtype)

    row = pl.BlockSpec((tr, d), lambda i: (i, 0))
    return pl.pallas_call(
        body, name="glu_fwd", grid=(t // tr,),
        in_specs=[row, pl.BlockSpec((tr, d), lambda i: (i, 1)), row, pl.BlockSpec((1, d), lambda i: (0, 0))],
        out_specs=[row, row],
        out_shape=[jax.ShapeDtypeStruct((t, d), F32), jax.ShapeDtypeStruct((t, d), BF16)],
        compiler_params=_params(("parallel",)),
    )(glu, glu, x, gain.reshape(1, d))


def glu_bwd(glu, dmix):
    t, d = dmix.shape
    tr = _tile(t, 512)

    def body(v_ref, g_ref, d_ref, o_ref):
        sg = jax.nn.sigmoid(g_ref[...])
        dm = d_ref[...]
        o_ref[:, :d] = (dm * sg).astype(o_ref.dtype)
        o_ref[:, d:] = (dm * v_ref[...] * sg * (1.0 - sg)).astype(o_ref.dtype)

    return pl.pallas_call(
        body, name="glu_bwd", grid=(t // tr,),
        in_specs=[pl.BlockSpec((tr, d), lambda i: (i, 0)), pl.BlockSpec((tr, d), lambda i: (i, 1)),
                  pl.BlockSpec((tr, d), lambda i: (i, 0))],
        out_specs=pl.BlockSpec((tr, 2 * d), lambda i: (i, 0)),
        out_shape=jax.ShapeDtypeStruct((t, 2 * d), BF16), compiler_params=_params(("parallel",)),
    )(glu, glu, dmix)


def _head_masks(shape):
    lane = lax.broadcasted_iota(jnp.int32, shape, 1)
    return lane < SB_HEAD_DIM


def _stack_heads(xf, is_a):
    return jnp.concatenate([jnp.where(is_a, xf, 0.0), jnp.where(is_a, 0.0, xf)], axis=0).astype(MXU_DTYPE)


def _diag_mask(qb, row0, rows):
    row = (lax.broadcasted_iota(jnp.int32, (rows, qb), 0) + row0) & (qb - 1)
    col = lax.broadcasted_iota(jnp.int32, (rows, qb), 1)
    return col < row


def _tri01(qb, pred):
    j = lax.broadcasted_iota(jnp.int32, (qb, qb), 0)
    s = lax.broadcasted_iota(jnp.int32, (qb, qb), 1)
    m = pred(j, s).astype(BF16)
    return jnp.concatenate([m, m], axis=0)


def _split_cat(x):
    hi = x.astype(BF16)
    lo = (x - hi.astype(F32)).astype(BF16)
    return jnp.concatenate([hi, lo], axis=1)


def sb_attn_fwd(proj, order, bsz, seq):
    qb = SB_BLOCK
    nq = seq // qb
    npair = SB_WIDTH // 128
    scale = SB_HEAD_DIM ** -0.5

    def body(q_ref, k_ref, v_ref, order_ref, o_ref, r_ref):
        qi = pl.program_id(2)
        is_a = _head_masks((qb, 128))
        q2 = _stack_heads(q_ref[...] * scale, is_a)
        diag = _diag_mask(qb, 0, 2 * qb)
        upper = _tri01(qb, lambda j, s: j > s)

        def logits(kbi):
            ks = pl.ds(pl.multiple_of(kbi * qb, qb), qb)
            return lax.dot_general(q2, k_ref[ks, :].astype(MXU_DTYPE), _NT, preferred_element_type=F32)

        def block(kbi, z, acc, run, masked):
            z_next = logits(jnp.maximum(kbi - 1, 0))
            vblk = v_ref[pl.ds(pl.multiple_of(kbi * qb, qb), qb), :].astype(MXU_DTYPE)
            lk = -jnp.maximum(z, 0.0) - jnp.log(1.0 + jnp.exp(-jnp.abs(z)))
            lb = lk + z
            if masked:
                lk = jnp.where(diag, lk, 0.0)
            after = run + lax.dot_general(_split_cat(lk), upper, _NN, preferred_element_type=F32)
            w = jnp.exp(lb + after)
            if masked:
                w = jnp.where(diag, w, 0.0)
            acc = acc + lax.dot_general(w.astype(MXU_DTYPE), vblk, _NN, preferred_element_type=F32)
            return z_next, acc, run + jnp.sum(lk, axis=1, keepdims=True)

        carry = block(qi, logits(qi), jnp.zeros((2 * qb, 128), F32), jnp.zeros((2 * qb, 1), F32), True)
        _, acc, run = lax.fori_loop(0, qi, lambda i, c: block(qi - 1 - i, c[0], c[1], c[2], False), carry)
        o_ref[...] = jnp.where(is_a, acc[:qb], acc[qb:]).astype(o_ref.dtype)
        r_ref[...] = jnp.where(is_a, run[:qb], run[qb:])

    return pl.pallas_call(
        body, name="sb_attn_fwd", grid=(bsz, npair, nq),
        in_specs=[pl.BlockSpec((qb, 128), lambda b, p, i: (b * nq + i, p)),
                  pl.BlockSpec((seq, 128), lambda b, p, i: (b, npair + p)),
                  pl.BlockSpec((seq, 128), lambda b, p, i: (b, 2 * npair + p)),
                  pl.BlockSpec((1, 128), lambda b, p, i: (0, 0))],
        out_specs=[pl.BlockSpec((qb, 128), lambda b, p, i: (b * nq + i, p)),
                   pl.BlockSpec((qb, 128), lambda b, p, i: (b * nq + i, p))],
        out_shape=[jax.ShapeDtypeStruct((bsz * seq, SB_WIDTH), BF16),
                   jax.ShapeDtypeStruct((bsz * seq, SB_WIDTH), F32)],
        compiler_params=_params(("parallel", "parallel", "arbitrary")),
    )(proj, proj, proj, order)


def sb_attn_bwd(proj, rsum, dcat, bsz, seq):
    qb = SB_BLOCK
    nq = seq // qb
    npair = SB_WIDTH // 128
    scale = SB_HEAD_DIM ** -0.5

    def body(q_ref, k_ref, v_ref, r_ref, do_ref, dq_ref, dk_ref, dv_ref):
        qi = pl.program_id(2)

        @pl.when(qi == 0)
        def _():
            dk_ref[...] = jnp.zeros_like(dk_ref)
            dv_ref[...] = jnp.zeros_like(dv_ref)

        is_a = _head_masks((qb, 128))
        q2 = _stack_heads(q_ref[...] * scale, is_a)
        do2 = _stack_heads(do_ref[...].astype(F32), is_a)
        rf = r_ref[...]
        rtot = jnp.concatenate([rf[:, 0:1], rf[:, SB_HEAD_DIM:SB_HEAD_DIM + 1]], axis=0)
        diag = _diag_mask(qb, 0, 2 * qb)
        incl = _tri01(qb, lambda j, s: j <= s)
        strict = _tri01(qb, lambda j, s: j < s)

        def logits(kbi):
            ks = pl.ds(pl.multiple_of(kbi * qb, qb), qb)
            return lax.dot_general(q2, k_ref[ks, :].astype(MXU_DTYPE), _NT, preferred_element_type=F32)

        def block(kbi, z, dq, pre, epre, masked):
            ks = pl.ds(pl.multiple_of(kbi * qb, qb), qb)
            kblk = k_ref[ks, :].astype(MXU_DTYPE)
            vblk = v_ref[ks, :].astype(MXU_DTYPE)
            z_next = logits(jnp.minimum(kbi + 1, qi))
            dw = lax.dot_general(do2, vblk, _NT, preferred_element_type=F32)
            lk = -jnp.maximum(z, 0.0) - jnp.log(1.0 + jnp.exp(-jnp.abs(z)))
            lb = lk + z
            if masked:
                lk = jnp.where(diag, lk, 0.0)
            after = rtot - (pre + lax.dot_general(_split_cat(lk), incl, _NN, preferred_element_type=F32))
            w = jnp.exp(lb + after)
            if masked:
                w = jnp.where(diag, w, 0.0)
            e = dw * w
            ecum = epre + lax.dot_general(_split_cat(e), strict, _NN, preferred_element_type=F32)
            dz = e - jnp.exp(lb) * (e + ecum)
            if masked:
                dz = jnp.where(diag, dz, 0.0)
            dz = dz.astype(MXU_DTYPE)
            dq = dq + lax.dot_general(dz, kblk, _NN, preferred_element_type=F32)
            dk_ref[ks, :] += lax.dot_general(dz, q2, _TN, preferred_element_type=F32)
            dv_ref[ks, :] += lax.dot_general(w.astype(MXU_DTYPE), do2, _TN, preferred_element_type=F32)
            return (z_next, dq, pre + jnp.sum(lk, axis=1, keepdims=True),
                    epre + jnp.sum(e, axis=1, keepdims=True))

        zc = jnp.zeros((2 * qb, 1), F32)
        carry = lax.fori_loop(0, qi, lambda kbi, c: block(kbi, c[0], c[1], c[2], c[3], False),
                              (logits(0), jnp.zeros((2 * qb, 128), F32), zc, zc))
        dq = block(qi, carry[0], carry[1], carry[2], carry[3], True)[1]
        dq_ref[...] = jnp.where(is_a, dq[:qb], dq[qb:]) * scale

    full = jax.ShapeDtypeStruct((bsz * seq, SB_WIDTH), F32)
    qspec = pl.BlockSpec((qb, 128), lambda b, p, i: (b * nq + i, p))
    return pl.pallas_call(
        body, name="sb_attn_bwd", grid=(bsz, npair, nq),
        in_specs=[qspec,
                  pl.BlockSpec((seq, 128), lambda b, p, i: (b, npair + p)),
                  pl.BlockSpec((seq, 128), lambda b, p, i: (b, 2 * npair + p)),
                  qspec, qspec],
        out_specs=[qspec, pl.BlockSpec((seq, 128), lambda b, p, i: (b, p)),
                   pl.BlockSpec((seq, 128), lambda b, p, i: (b, p))],
        out_shape=[full, full, full],
        compiler_params=_params(("parallel", "parallel", "arbitrary")),
    )(proj, proj, proj, rsum, dcat)


def _window_sums(x, forward):
    n = x.shape[0]
    out = []
    s = x
    for sh in (1, 2, 4, 8):
        s = s + pltpu.roll(s, (n - sh) if forward else sh, 0)
        out.append(s)
    return out


def _pool_counts(tc, c, w):
    t = lax.broadcasted_iota(jnp.int32, (tc, 1), 0) + c * tc
    return jnp.minimum(t + 1, w).astype(F32)


def pool_fwd(proj, pool_w, pool_scale, bsz, seq):
    tc = _tile(seq, 512)
    nc = seq // tc
    hb = tc // POOL_HALO
    ucol = 3

    def body(u_ref, prev_ref, w_ref, s_ref, o_ref):
        c = pl.program_id(1)
        prev = jnp.where(c > 0, prev_ref[...], 0.0)
        x = jnp.concatenate([prev, u_ref[...]], axis=0)
        sums = _window_sums(x, forward=False)
        for g, win in enumerate(POOL_WINDOWS):
            ls = slice(g * POOL_GROUP, (g + 1) * POOL_GROUP)
            pooled = sums[g][POOL_HALO:, ls] / _pool_counts(tc, c, win) - x[POOL_HALO:, ls]
            y = _dot(pooled, w_ref[g], _NN)
            o_ref[:, ls] = (y * s_ref[:, ls]).astype(o_ref.dtype)

    return pl.pallas_call(
        body, name="pool_fwd", grid=(bsz, nc),
        in_specs=[pl.BlockSpec((tc, SB_WIDTH), lambda b, c: (b * nc + c, ucol)),
                  pl.BlockSpec((POOL_HALO, SB_WIDTH), lambda b, c: (jnp.maximum((b * nc + c) * hb - 1, 0), ucol)),
                  pl.BlockSpec((4, POOL_GROUP, POOL_GROUP), lambda b, c: (0, 0, 0)),
                  pl.BlockSpec((1, SB_WIDTH), lambda b, c: (0, 0))],
        out_specs=pl.BlockSpec((tc, SB_WIDTH), lambda b, c: (b * nc + c, 0)),
        out_shape=jax.ShapeDtypeStruct((bsz * seq, SB_WIDTH), BF16),
        compiler_params=_params(("parallel", "parallel")),
    )(proj, proj, pool_w, pool_scale)


def pool_bwd(proj, pool_w, pool_scale, dcat, bsz, seq):
    tc = _tile(seq, 512)
    nc = seq // tc
    hb = tc // POOL_HALO
    nblk = bsz * seq // POOL_HALO
    ucol = 3

    def body(u_ref, prev_ref, dy_ref, nxt_ref, w_ref, s_ref, du_ref, dw_ref, ds_ref):
        b, c = pl.program_id(0), pl.program_id(1)

        @pl.when((b == 0) & (c == 0))
        def _():
            dw_ref[...] = jnp.zeros_like(dw_ref)
            ds_ref[...] = jnp.zeros_like(ds_ref)

        prev = jnp.where(c > 0, prev_ref[...], 0.0)
        x = jnp.concatenate([prev, u_ref[...]], axis=0)
        sums = _window_sums(x, forward=False)
        nxt = jnp.where(c < nc - 1, nxt_ref[...].astype(F32), 0.0)
        dy = jnp.concatenate([dy_ref[...].astype(F32), nxt], axis=0)
        tq = lax.broadcasted_iota(jnp.int32, (tc + POOL_HALO, 1), 0) + c * tc
        for g, win in enumerate(POOL_WINDOWS):
            ls = slice(g * POOL_GROUP, (g + 1) * POOL_GROUP)
            pooled = sums[g][POOL_HALO:, ls] / _pool_counts(tc, c, win) - x[POOL_HALO:, ls]
            y = _dot(pooled, w_ref[g], _NN)
            ds_ref[:, ls] += jnp.sum(dy[:tc, ls] * y, axis=0, keepdims=True)
            dz = dy[:, ls] * s_ref[:, ls]
            dw_ref[g] += _dot(pooled, dz[:tc], _TN)
            dpool = _dot(dz, w_ref[g], _NT)
            dmean = dpool / jnp.minimum(tq + 1, win).astype(F32)
            fsum = _window_sums(dmean, forward=True)[g]
            du_ref[:, ls] = fsum[:tc] - dpool[:tc]

    return pl.pallas_call(
        body, name="pool_bwd", grid=(bsz, nc),
        in_specs=[pl.BlockSpec((tc, SB_WIDTH), lambda b, c: (b * nc + c, ucol)),
                  pl.BlockSpec((POOL_HALO, SB_WIDTH), lambda b, c: (jnp.maximum((b * nc + c) * hb - 1, 0), ucol)),
                  pl.BlockSpec((tc, SB_WIDTH), lambda b, c: (b * nc + c, 1)),
                  pl.BlockSpec((POOL_HALO, SB_WIDTH), lambda b, c: (jnp.minimum((b * nc + c + 1) * hb, nblk - 1), 1)),
                  pl.BlockSpec((4, POOL_GROUP, POOL_GROUP), lambda b, c: (0, 0, 0)),
                  pl.BlockSpec((1, SB_WIDTH), lambda b, c: (0, 0))],
        out_specs=[pl.BlockSpec((tc, SB_WIDTH), lambda b, c: (b * nc + c, 0)),
                   pl.BlockSpec((4, POOL_GROUP, POOL_GROUP), lambda b, c: (0, 0, 0)),
                   pl.BlockSpec((1, SB_WIDTH), lambda b, c: (0, 0))],
        out_shape=[jax.ShapeDtypeStruct((bsz * seq, SB_WIDTH), F32),
                   jax.ShapeDtypeStruct((4, POOL_GROUP, POOL_GROUP), F32),
                   jax.ShapeDtypeStruct((1, SB_WIDTH), F32)],
        compiler_params=_params(("arbitrary", "arbitrary")),
    )(proj, proj, dcat, dcat, pool_w, pool_scale)


def _lbar(lam_re, lam_im, log_dt):
    dt = jnp.exp(log_dt)
    mag = jnp.exp(lam_re * dt)
    ang = lam_im * dt
    return mag * jnp.cos(ang), mag * jnp.sin(ang)


def _bbar(lam_re, lam_im, log_dt, b_re, b_im):
    lb_re, lb_im = _lbar(lam_re, lam_im, log_dt)
    n_re = lb_re - 1.0
    den = lam_re * lam_re + lam_im * lam_im
    coef_re = (n_re * lam_re + lb_im * lam_im) / den
    coef_im = (lb_im * lam_re - n_re * lam_im) / den
    return coef_re * b_re - coef_im * b_im, coef_re * b_im + coef_im * b_re


def _expand01():
    p = lax.broadcasted_iota(jnp.int32, (64, 1024), 0)
    q = lax.broadcasted_iota(jnp.int32, (64, 1024), 1)
    return (lax.shift_right_logical(q, 4) == p).astype(BF16)


def ssm_prep(lam_re, lam_im, log_dt, b_re2, b_im2):
    def body(lr_ref, li_ref, dt_ref, br_ref, bi_ref, ar_ref, ai_ref, bbr_ref, bbi_ref):
        e = _expand01()
        lr, li, dt = lr_ref[...], li_ref[...], dt_ref[...]
        ar_ref[...], ai_ref[...] = _lbar(lr, li, dt)
        bbr_ref[...], bbi_ref[...] = _bbar(_dot_exact01(lr, e), _dot_exact01(li, e), dt, br_ref[...], bi_ref[...])

    s64 = jax.ShapeDtypeStruct((64, 64), F32)
    s1k = jax.ShapeDtypeStruct((64, 1024), F32)
    return pl.pallas_call(body, name="ssm_prep", out_shape=[s64, s64, s1k, s1k], compiler_params=_params())(
        lam_re, lam_im, log_dt, b_re2, b_im2)


def ssm_prep_bwd(lam_re, lam_im, log_dt, b_re2, b_im2, da_re, da_im, dbb_re, dbb_im):
    def body(lr_ref, li_ref, dt_ref, br_ref, bi_ref, dar_ref, dai_ref, dbr_ref, dbi_ref,
             olr_ref, oli_ref, odt_ref, obr_ref, obi_ref):
        e = _expand01()
        lr, li, dt = lr_ref[...], li_ref[...], dt_ref[...]
        _, vjp_a = jax.vjp(_lbar, lr, li, dt)
        g_lr, g_li, g_dt = vjp_a((dar_ref[...], dai_ref[...]))
        _, vjp_b = jax.vjp(_bbar, _dot_exact01(lr, e), _dot_exact01(li, e), dt, br_ref[...], bi_ref[...])
        x_lr, x_li, x_dt, g_br, g_bi = vjp_b((dbr_ref[...], dbi_ref[...]))
        olr_ref[...] = g_lr + _dot_exact01(x_lr, e, _NT)
        oli_ref[...] = g_li + _dot_exact01(x_li, e, _NT)
        odt_ref[...] = g_dt + x_dt
        obr_ref[...] = g_br
        obi_ref[...] = g_bi

    s64 = jax.ShapeDtypeStruct((64, 64), F32)
    s1k = jax.ShapeDtypeStruct((64, 1024), F32)
    return pl.pallas_call(body, name="ssm_prep_bwd",
                          out_shape=[s64, s64, jax.ShapeDtypeStruct((64, 1), F32), s1k, s1k],
                          compiler_params=_params())(
        lam_re, lam_im, log_dt, b_re2, b_im2, da_re, da_im, dbb_re, dbb_im)


def _gelu(y):
    c = math.sqrt(2.0 / math.pi)
    return 0.5 * y * (1.0 + jnp.tanh(c * (y + 0.044715 * y * y * y)))


def _gelu_grad(y):
    c = math.sqrt(2.0 / math.pi)
    th = jnp.tanh(c * (y + 0.044715 * y * y * y))
    return 0.5 * (1.0 + th) + 0.5 * y * (1.0 - th * th) * c * (1.0 + 3.0 * 0.044715 * y * y)


def _cmul(ar, ai, br, bi):
    return ar * br - ai * bi, ar * bi + ai * br


def _scan_tables(ar, ai, reverse, tabs):
    row = lax.broadcasted_iota(jnp.int32, (8, SSM_STATES), 0)
    a1 = (ar, ai)
    a2 = _cmul(*a1, *a1)
    a4 = _cmul(*a2, *a2)
    powers = [a1, a2, _cmul(*a2, *a1), a4]
    powers += [_cmul(*a4, *p) for p in powers]
    for k, (val, sh) in enumerate(((a1, 1), (a2, 2), (a4, 4))):
        keep = (row < 8 - sh) if reverse else (row >= sh)
        tabs[2 * k][...] = jnp.where(keep, val[0], 0.0)
        tabs[2 * k + 1][...] = jnp.where(keep, val[1], 0.0)
    pr = jnp.zeros((8, SSM_STATES), F32)
    pi = jnp.zeros((8, SSM_STATES), F32)
    for r in range(8):
        val = powers[7 - r] if reverse else powers[r]
        pr = jnp.where(row == r, val[0], pr)
        pi = jnp.where(row == r, val[1], pi)
    tabs[6][...] = pr
    tabs[7][...] = pi


def _scan8(xr, xi, tabs, ls, cr, ci, reverse):
    for k, sh in enumerate((1, 2, 4)):
        amt = (8 - sh) if reverse else sh
        sr, si = pltpu.roll(xr, amt, 0), pltpu.roll(xi, amt, 0)
        lr, li = tabs[2 * k][:, ls], tabs[2 * k + 1][:, ls]
        xr, xi = xr + lr * sr - li * si, xi + lr * si + li * sr
    pr, pi = tabs[6][:, ls], tabs[7][:, ls]
    return xr + pr * cr - pi * ci, xi + pr * ci + pi * cr


def _block8(b):
    return pl.ds(pl.multiple_of(b * 8, 8), 8)


def ssm_fwd(u, wt, ct, a_re, a_im, dskip, bsz, seq):
    tc = _tile(seq, 256)
    nc = seq // tc
    ns = SSM_TILE_STATES
    nl = SSM_STATES // SSM_LANES

    def body(u_ref, wt_ref, ct_ref, ar_ref, ai_ref, d_ref, y_ref, gl_ref, hr_ref, hi_ref, sr_ref, si_ref, *tabs):
        b, c = pl.program_id(0), pl.program_id(1)

        @pl.when((b == 0) & (c == 0))
        def _():
            _scan_tables(ar_ref[...], ai_ref[...], False, tabs)

        @pl.when(c == 0)
        def _():
            sr_ref[...] = jnp.zeros_like(sr_ref)
            si_ref[...] = jnp.zeros_like(si_ref)

        uf = u_ref[...]
        for i in range(SSM_TILES):
            bu = _dot(uf[:, i * 128:(i + 1) * 128], wt_ref[i], _NN)
            hr_ref[:, i * ns:(i + 1) * ns] = bu[:, :ns]
            hi_ref[:, i * ns:(i + 1) * ns] = bu[:, ns:]

        def step(blk, carry):
            rows = _block8(blk)
            new = []
            for j in range(nl):
                ls = slice(j * SSM_LANES, (j + 1) * SSM_LANES)
                xr, xi = _scan8(hr_ref[rows, ls], hi_ref[rows, ls], tabs, ls, carry[2 * j], carry[2 * j + 1], False)
                hr_ref[rows, ls] = xr
                hi_ref[rows, ls] = xi
                new += [xr[7:8], xi[7:8]]
            return tuple(new)

        init = []
        for j in range(nl):
            ls = slice(j * SSM_LANES, (j + 1) * SSM_LANES)
            init += [sr_ref[:, ls], si_ref[:, ls]]
        last = lax.fori_loop(0, tc // 8, step, tuple(init), unroll=2)
        for j in range(nl):
            ls = slice(j * SSM_LANES, (j + 1) * SSM_LANES)
            sr_ref[:, ls] = last[2 * j]
            si_ref[:, ls] = last[2 * j + 1]
        for i in range(SSM_TILES):
            hcat = jnp.concatenate([hr_ref[:, i * ns:(i + 1) * ns], hi_ref[:, i * ns:(i + 1) * ns]], axis=1)
            ls = slice(i * 128, (i + 1) * 128)
            y = _dot(hcat, ct_ref[i], _NN) + d_ref[:, ls] * uf[:, ls]
            y_ref[:, ls] = y
            gl_ref[:, ls] = _gelu(y).astype(gl_ref.dtype)

    t = bsz * seq
    row = pl.BlockSpec((tc, D_MODEL), lambda b, c: (b * nc + c, 0))
    st = pl.BlockSpec((tc, SSM_STATES), lambda b, c: (b * nc + c, 0))
    diag = pl.BlockSpec((1, SSM_STATES), lambda b, c: (0, 0))
    return pl.pallas_call(
        body, name="ssm_fwd", grid=(bsz, nc),
        in_specs=[row, pl.BlockSpec((SSM_TILES, 128, 2 * ns), lambda b, c: (0, 0, 0)),
                  pl.BlockSpec((SSM_TILES, 2 * ns, 128), lambda b, c: (0, 0, 0)), diag, diag,
                  pl.BlockSpec((1, D_MODEL), lambda b, c: (0, 0))],
        out_specs=[row, row, st, st],
        out_shape=[jax.ShapeDtypeStruct((t, D_MODEL), F32), jax.ShapeDtypeStruct((t, D_MODEL), BF16),
                   jax.ShapeDtypeStruct((t, SSM_STATES), F32), jax.ShapeDtypeStruct((t, SSM_STATES), F32)],
        scratch_shapes=[pltpu.VMEM((1, SSM_STATES), F32)] * 2 + [pltpu.VMEM((8, SSM_STATES), F32)] * 8,
        compiler_params=_params(("arbitrary", "arbitrary")),
    )(u, wt, ct, a_re, a_im, dskip)


def ssm_bwd(dgl, y, u, h_re, h_im, wt, ct, a_re, a_im, dskip, bsz, seq):
    tc = _tile(seq, 256)
    nc = seq // tc
    nb = tc // 8
    ns = SSM_TILE_STATES
    nl = SSM_STATES // SSM_LANES

    def body(dgl_ref, y_ref, u_ref, hr_ref, hi_ref, pr_ref, pi_ref, wt_ref, ct_ref, ar_ref, ai_ref, d_ref,
             du_ref, dwt_ref, dct_ref, dd_ref, dar_ref, dai_ref, gr_ref, gi_ref, sr_ref, si_ref, ar8_ref, ai8_ref,
             *tabs):
        b, c = pl.program_id(0), pl.program_id(1)

        @pl.when((b == 0) & (c == 0))
        def _():
            for r in (dwt_ref, dct_ref, dd_ref, ar8_ref, ai8_ref):
                r[...] = jnp.zeros_like(r)
            _scan_tables(ar_ref[...], -ai_ref[...], True, tabs)

        @pl.when(c == 0)
        def _():
            sr_ref[...] = jnp.zeros_like(sr_ref)
            si_ref[...] = jnp.zeros_like(si_ref)

        uf = u_ref[...]
        dy = dgl_ref[...].astype(F32) * _gelu_grad(y_ref[...])
        dd_ref[...] += jnp.sum(dy * uf, axis=0, keepdims=True)
        for i in range(SSM_TILES):
            dyi = dy[:, i * 128:(i + 1) * 128]
            dh = _dot(dyi, ct_ref[i], _NT)
            gr_ref[:, i * ns:(i + 1) * ns] = dh[:, :ns]
            gi_ref[:, i * ns:(i + 1) * ns] = dh[:, ns:]
            hcat = jnp.concatenate([hr_ref[:, i * ns:(i + 1) * ns], hi_ref[:, i * ns:(i + 1) * ns]], axis=1)
            dct_ref[i] += _dot(hcat, dyi, _TN)
        row0 = lax.broadcasted_iota(jnp.int32, (8, SSM_LANES), 0) == 0

        def block(blk, carry, before):
            rows = _block8(blk)
            new = []
            for j in range(nl):
                ls = slice(j * SSM_LANES, (j + 1) * SSM_LANES)
                gr, gi = _scan8(gr_ref[rows, ls], gi_ref[rows, ls], tabs, ls, carry[2 * j], carry[2 * j + 1], True)
                gr_ref[rows, ls] = gr
                gi_ref[rows, ls] = gi
                bpr, bpi = before(j)
                hpr = jnp.where(row0, bpr, pltpu.roll(hr_ref[rows, ls], 1, 0))
                hpi = jnp.where(row0, bpi, pltpu.roll(hi_ref[rows, ls], 1, 0))
                ar8_ref[:, ls] += gr * hpr + gi * hpi
                ai8_ref[:, ls] += gi * hpr - gr * hpi
                new += [gr[0:1], gi[0:1]]
            return tuple(new)

        def step(jj, carry):
            blk = nb - 1 - jj
            prev_rows = _block8(blk - 1)

            def before(j):
                ls = slice(j * SSM_LANES, (j + 1) * SSM_LANES)
                return hr_ref[prev_rows, ls][7:8], hi_ref[prev_rows, ls][7:8]

            return block(blk, carry, before)

        init = []
        for j in range(nl):
            ls = slice(j * SSM_LANES, (j + 1) * SSM_LANES)
            init += [sr_ref[:, ls], si_ref[:, ls]]
        carry = lax.fori_loop(0, nb - 1, step, tuple(init))
        first = c == nc - 1

        def before_chunk(j):
            ls = slice(j * SSM_LANES, (j + 1) * SSM_LANES)
            return (jnp.where(first, 0.0, pr_ref[:, ls][7:8]), jnp.where(first, 0.0, pi_ref[:, ls][7:8]))

        last = block(0, carry, before_chunk)
        for j in range(nl):
            ls = slice(j * SSM_LANES, (j + 1) * SSM_LANES)
            sr_ref[:, ls] = last[2 * j]
            si_ref[:, ls] = last[2 * j + 1]
        for i in range(SSM_TILES):
            ls = slice(i * 128, (i + 1) * 128)
            gcat = jnp.concatenate([gr_ref[:, i * ns:(i + 1) * ns], gi_ref[:, i * ns:(i + 1) * ns]], axis=1)
            du_ref[:, ls] = (_dot(gcat, wt_ref[i], _NT) + d_ref[:, ls] * dy[:, ls]).astype(du_ref.dtype)
            dwt_ref[i] += _dot(uf[:, ls], gcat, _TN)

        @pl.when((b == bsz - 1) & (c == nc - 1))
        def _():
            dar_ref[...] = jnp.sum(ar8_ref[...], axis=0, keepdims=True)
            dai_ref[...] = jnp.sum(ai8_ref[...], axis=0, keepdims=True)

    t = bsz * seq
    rev = lambda b, c: (b * nc + (nc - 1 - c), 0)
    row = pl.BlockSpec((tc, D_MODEL), rev)
    st = pl.BlockSpec((tc, SSM_STATES), rev)
    prev = pl.BlockSpec((8, SSM_STATES), lambda b, c: (jnp.maximum((b * nc + (nc - 1 - c)) * nb - 1, 0), 0))
    diag = pl.BlockSpec((1, SSM_STATES), lambda b, c: (0, 0))
    wts = pl.BlockSpec((SSM_TILES, 128, 2 * ns), lambda b, c: (0, 0, 0))
    cts = pl.BlockSpec((SSM_TILES, 2 * ns, 128), lambda b, c: (0, 0, 0))
    vec = pl.BlockSpec((1, D_MODEL), lambda b, c: (0, 0))
    return pl.pallas_call(
        body, name="ssm_bwd", grid=(bsz, nc),
        in_specs=[row, row, row, st, st, prev, prev, wts, cts, diag, diag, vec],
        out_specs=[row, wts, cts, vec, diag, diag],
        out_shape=[jax.ShapeDtypeStruct((t, D_MODEL), BF16),
                   jax.ShapeDtypeStruct((SSM_TILES, 128, 2 * ns), F32),
                   jax.ShapeDtypeStruct((SSM_TILES, 2 * ns, 128), F32),
                   jax.ShapeDtypeStruct((1, D_MODEL), F32),
                   jax.ShapeDtypeStruct((1, SSM_STATES), F32), jax.ShapeDtypeStruct((1, SSM_STATES), F32)],
        scratch_shapes=[pltpu.VMEM((tc, SSM_STATES), F32)] * 2 + [pltpu.VMEM((1, SSM_STATES), F32)] * 2
                       + [pltpu.VMEM((8, SSM_STATES), F32)] * 10,
        compiler_params=_params(("arbitrary", "arbitrary")),
    )(dgl, y, u, h_re, h_im, h_re, h_im, wt, ct, a_re, a_im, dskip)


def _ssm_in_weights(bb_re2, bb_im2):
    eye = jnp.eye(8, dtype=F32)[None, :, None, :, None]

    def one(bb):
        t = bb.reshape(8, 8, 64, 16).transpose(0, 1, 3, 2)
        return (t[:, :, :, None, :] * eye).reshape(8, 128, 512)

    return jnp.concatenate([one(bb_re2), one(bb_im2)], axis=-1).astype(MXU_DTYPE)


def _ssm_in_weights_bwd(dwt):
    eye = jnp.eye(8, dtype=F32)[None, :, None, :, None]

    def one(d):
        t = (d.reshape(8, 8, 16, 8, 64) * eye).sum(axis=3)
        return t.transpose(0, 1, 3, 2).reshape(64, 1024)

    return one(dwt[..., :512]), one(dwt[..., 512:])


def _ssm_out_weights(c_re, c_im):
    eye = jnp.eye(8, dtype=F32)[None, :, None, :, None]

    def one(cc):
        t = cc.reshape(8, 8, 16, 64).transpose(0, 1, 3, 2)
        return (t[:, :, :, None, :] * eye).reshape(8, 512, 128)

    return jnp.concatenate([one(c_re), -one(c_im)], axis=1).astype(MXU_DTYPE)


def _ssm_out_weights_bwd(dct):
    eye = jnp.eye(8, dtype=F32)[None, :, None, :, None]

    def one(d):
        t = (d.reshape(8, 8, 64, 8, 16) * eye).sum(axis=3)
        return t.transpose(0, 1, 3, 2).reshape(64, 16, 64)

    return one(dct[:, :512]), -one(dct[:, 512:])


def _softmax(s):
    m = jnp.max(s, axis=-1, keepdims=True)
    e = jnp.exp(s - m)
    return e / jnp.sum(e, axis=-1, keepdims=True)


def xattn_fwd(q, kv, bsz, seq):
    tq = _tile(seq, 512)
    nq = seq // tq
    scale = XA_HEAD_DIM ** -0.5

    def body(q_ref, k_ref, v_ref, o_ref):
        s = lax.dot_general(q_ref[...], k_ref[...], _NT, preferred_element_type=F32) * scale
        p = _softmax(s)
        o_ref[...] = _dot(p, v_ref[...], _NN).astype(o_ref.dtype)

    qs = pl.BlockSpec((tq, XA_HEAD_DIM), lambda b, h, i: (b * nq + i, h))
    return pl.pallas_call(
        body, name="xattn_fwd", grid=(bsz, XA_HEADS, nq),
        in_specs=[qs, pl.BlockSpec((MEM_LEN, XA_HEAD_DIM), lambda b, h, i: (b, h)),
                  pl.BlockSpec((MEM_LEN, XA_HEAD_DIM), lambda b, h, i: (b, XA_HEADS + h))],
        out_specs=qs, out_shape=jax.ShapeDtypeStruct((bsz * seq, D_MODEL), BF16),
        compiler_params=_params(("parallel", "parallel", "parallel")),
    )(q, kv, kv)


def xattn_bwd(q, kv, do, bsz, seq):
    tq = _tile(seq, 512)
    nq = seq // tq
    scale = XA_HEAD_DIM ** -0.5

    def body(q_ref, k_ref, v_ref, do_ref, dq_ref, dk_ref, dv_ref):
        @pl.when(pl.program_id(2) == 0)
        def _():
            dk_ref[...] = jnp.zeros_like(dk_ref)
            dv_ref[...] = jnp.zeros_like(dv_ref)

        qv, kk, vv, dov = q_ref[...], k_ref[...], v_ref[...], do_ref[...]
        s = lax.dot_general(qv, kk, _NT, preferred_element_type=F32) * scale
        p = _softmax(s)
        dp = lax.dot_general(dov, vv, _NT, preferred_element_type=F32)
        ds = (p * (dp - jnp.sum(dp * p, axis=-1, keepdims=True)) * scale).astype(MXU_DTYPE)
        dq_ref[...] = lax.dot_general(ds, kk, _NN, preferred_element_type=F32).astype(dq_ref.dtype)
        dk_ref[...] += lax.dot_general(ds, qv, _TN, preferred_element_type=F32)
        dv_ref[...] += lax.dot_general(p.astype(MXU_DTYPE), dov, _TN, preferred_element_type=F32)

    qs = pl.BlockSpec((tq, XA_HEAD_DIM), lambda b, h, i: (b * nq + i, h))
    ks = pl.BlockSpec((MEM_LEN, XA_HEAD_DIM), lambda b, h, i: (b, h))
    vs = pl.BlockSpec((MEM_LEN, XA_HEAD_DIM), lambda b, h, i: (b, XA_HEADS + h))
    dkv = jax.ShapeDtypeStruct((bsz * MEM_LEN, D_MODEL), F32)
    dq, dk, dv = pl.pallas_call(
        body, name="xattn_bwd", grid=(bsz, XA_HEADS, nq),
        in_specs=[qs, ks, vs, qs], out_specs=[qs, ks, ks],
        out_shape=[jax.ShapeDtypeStruct((bsz * seq, D_MODEL), BF16), dkv, dkv],
        compiler_params=_params(("parallel", "parallel", "arbitrary")),
    )(q, kv, kv, do)
    return dq, dk, dv


CONV_HALO = 16


def _shifts_down(x, prev):
    h = prev.shape[0]
    ext = jnp.concatenate([prev, x], axis=0)
    return pltpu.roll(ext, 1, 0)[h:], pltpu.roll(ext, 2, 0)[h:]


def _shifts_up(x, nxt):
    rows = x.shape[0]
    n = rows + nxt.shape[0]
    ext = jnp.concatenate([x, nxt], axis=0)
    return pltpu.roll(ext, n - 1, 0)[:rows], pltpu.roll(ext, n - 2, 0)[:rows]


def _conv_taps(u, u1, u2, w, b):
    return b + w[2:3] * u + w[1:2] * u1 + w[0:1] * u2


def conv_fwd(up, cw, cb, bsz, seq):
    tc = _tile(seq, 512)
    nc = seq // tc
    hb = tc // CONV_HALO
    half = N_DEV // 2

    def body(uv_ref, ug_ref, pv_ref, pg_ref, wv_ref, wg_ref, bv_ref, bg_ref, o_ref):
        c = pl.program_id(2)
        pv = jnp.where(c > 0, pv_ref[...].astype(F32), 0.0)
        pg = jnp.where(c > 0, pg_ref[...].astype(F32), 0.0)
        uv, ug = uv_ref[...].astype(F32), ug_ref[...].astype(F32)
        val = _conv_taps(uv, *_shifts_down(uv, pv), wv_ref[...], bv_ref[...])
        gate = _conv_taps(ug, *_shifts_down(ug, pg), wg_ref[...], bg_ref[...])
        o_ref[...] = (gate * jax.nn.sigmoid(gate) * val).astype(o_ref.dtype)

    def cur(off):
        return pl.BlockSpec((None, tc, FF_SHARD), lambda b, j, c: (j + off, b * nc + c, 0))

    def prv(off):
        return pl.BlockSpec((None, CONV_HALO, FF_SHARD), lambda b, j, c: (j + off, jnp.maximum((b * nc + c) * hb - 1, 0), 0))

    def par(rows, off):
        return pl.BlockSpec((None, rows, FF_SHARD), lambda b, j, c: (j + off, 0, 0))

    return pl.pallas_call(
        body, name="conv_fwd", grid=(bsz, half, nc),
        in_specs=[cur(0), cur(half), prv(0), prv(half), par(3, 0), par(3, half), par(1, 0), par(1, half)],
        out_specs=cur(0), out_shape=jax.ShapeDtypeStruct((half, bsz * seq, FF_SHARD), BF16),
        compiler_params=_params(("parallel", "parallel", "parallel")),
    )(up, up, up, up, cw, cw, cb, cb)


def conv_bwd_taps(up, cw, cb, dact, bsz, seq):
    tc = _tile(seq, 512)
    nc = seq // tc
    hb = tc // CONV_HALO
    half = N_DEV // 2

    def body(uv_ref, ug_ref, pv_ref, pg_ref, wv_ref, wg_ref, bv_ref, bg_ref, da_ref,
             dc_ref, dwv_ref, dwg_ref, dbv_ref, dbg_ref):
        b, c = pl.program_id(1), pl.program_id(2)

        @pl.when((b == 0) & (c == 0))
        def _():
            for r in (dwv_ref, dwg_ref, dbv_ref, dbg_ref):
                r[...] = jnp.zeros_like(r)

        pv = jnp.where(c > 0, pv_ref[...].astype(F32), 0.0)
        pg = jnp.where(c > 0, pg_ref[...].astype(F32), 0.0)
        uv, ug = uv_ref[...].astype(F32), ug_ref[...].astype(F32)
        uv1, uv2 = _shifts_down(uv, pv)
        ug1, ug2 = _shifts_down(ug, pg)
        val = _conv_taps(uv, uv1, uv2, wv_ref[...], bv_ref[...])
        gate = _conv_taps(ug, ug1, ug2, wg_ref[...], bg_ref[...])
        sg = jax.nn.sigmoid(gate)
        da = da_ref[...].astype(F32)
        dsilu = da * sg
        dval = dsilu * gate
        dgate = dsilu * val * (1.0 + gate * (1.0 - sg))
        dc_ref[0] = dval.astype(dc_ref.dtype)
        dc_ref[1] = dgate.astype(dc_ref.dtype)
        for dcv, taps, dw_ref, db_ref in ((dval, (uv2, uv1, uv), dwv_ref, dbv_ref),
                                          (dgate, (ug2, ug1, ug), dwg_ref, dbg_ref)):
            db_ref[...] += jnp.sum(dcv, axis=0, keepdims=True)
            for k, u_k in enumerate(taps):
                dw_ref[k:k + 1, :] += jnp.sum(dcv * u_k, axis=0, keepdims=True)

    def cur(off):
        return pl.BlockSpec((None, tc, FF_SHARD), lambda j, b, c: (j + off, b * nc + c, 0))

    def prv(off):
        return pl.BlockSpec((None, CONV_HALO, FF_SHARD), lambda j, b, c: (j + off, jnp.maximum((b * nc + c) * hb - 1, 0), 0))

    def par(rows, off):
        return pl.BlockSpec((None, rows, FF_SHARD), lambda j, b, c: (j + off, 0, 0))

    t = bsz * seq
    hs = jax.ShapeDtypeStruct((2, half, t, FF_SHARD), BF16)
    ws = jax.ShapeDtypeStruct((half, 3, FF_SHARD), F32)
    bs = jax.ShapeDtypeStruct((half, 1, FF_SHARD), F32)
    dc, dwv, dwg, dbv, dbg = pl.pallas_call(
        body, name="conv_bwd_taps", grid=(half, bsz, nc),
        in_specs=[cur(0), cur(half), prv(0), prv(half), par(3, 0), par(3, half), par(1, 0), par(1, half), cur(0)],
        out_specs=[pl.BlockSpec((2, None, tc, FF_SHARD), lambda j, b, c: (0, j, b * nc + c, 0)),
                   par(3, 0), par(3, 0), par(1, 0), par(1, 0)],
        out_shape=[hs, ws, ws, bs, bs],
        compiler_params=_params(("parallel", "arbitrary", "arbitrary")),
    )(up, up, up, up, cw, cw, cb, cb, dact)
    return (dc.reshape(N_DEV, t, FF_SHARD), jnp.concatenate([dwv, dwg], axis=0),
            jnp.concatenate([dbv, dbg], axis=0))


def conv_bwd_input(dconv, cw, bsz, seq):
    tc = _tile(seq, 1024)
    nc = seq // tc
    hb = tc // CONV_HALO
    nblk = bsz * seq // CONV_HALO

    def body(d_ref, n_ref, w_ref, o_ref):
        c = pl.program_id(2)
        nxt = jnp.where(c < nc - 1, n_ref[...].astype(F32), 0.0)
        d = d_ref[...].astype(F32)
        d1, d2 = _shifts_up(d, nxt)
        w = w_ref[...]
        o_ref[...] = (w[2:3] * d + w[1:2] * d1 + w[0:1] * d2).astype(o_ref.dtype)

    cur = pl.BlockSpec((None, tc, FF_SHARD), lambda j, b, c: (j, b * nc + c, 0))
    return pl.pallas_call(
        body, name="conv_bwd_input", grid=(N_DEV, bsz, nc),
        in_specs=[cur, pl.BlockSpec((None, CONV_HALO, FF_SHARD),
                                    lambda j, b, c: (j, jnp.minimum((b * nc + c + 1) * hb, nblk - 1), 0)),
                  pl.BlockSpec((None, 3, FF_SHARD), lambda j, b, c: (j, 0, 0))],
        out_specs=cur, out_shape=jax.ShapeDtypeStruct(dconv.shape, BF16),
        compiler_params=_params(("parallel", "parallel", "parallel")),
    )(dconv, dconv, cw)


def _my_index():
    return 4 * lax.axis_index("x") + 2 * lax.axis_index("y") + lax.axis_index("c")


def _peer(k):
    return (lax.axis_index("x") ^ ((k >> 2) & 1), lax.axis_index("y") ^ ((k >> 1) & 1),
            lax.axis_index("c") ^ (k & 1))


_HBM = pl.BlockSpec(memory_space=pltpu.HBM)
_SEM = pl.BlockSpec(memory_space=pltpu.SEMAPHORE)
_DATAFLOW = pltpu.SideEffectType.DATAFLOW_SIDE_EFFECTING


def _split_copies(gather, src_ref, land_ref, send_sems, recv_sems, local_sem):
    me = _my_index()

    def part(j):
        return src_ref if gather else src_ref.at[j]

    local = pltpu.make_async_copy(part(me), land_ref.at[me], local_sem)
    sends = [pltpu.make_async_remote_copy(
        src_ref=part(me ^ k), dst_ref=land_ref.at[me], send_sem=send_sems.at[k - 1], recv_sem=recv_sems.at[k - 1],
        device_id=_peer(k), device_id_type=pl.DeviceIdType.MESH) for k in range(1, N_DEV)]
    recvs = [pltpu.make_async_remote_copy(
        src_ref=part(me ^ k), dst_ref=land_ref.at[me ^ k], send_sem=send_sems.at[k - 1], recv_sem=recv_sems.at[k - 1],
        device_id=_peer(k), device_id_type=pl.DeviceIdType.MESH) for k in range(1, N_DEV)]
    return local, sends, recvs


def split_start(name, src, gather):
    land_shape = ((N_DEV,) + src.shape) if gather else src.shape

    def body(src_ref, land_ref, send_sems, recv_sems, local_sem, src_thru, land_thru, token):
        local, sends, _ = _split_copies(gather, src_ref, land_ref, send_sems, recv_sems, local_sem)
        local.start()
        for cp in sends:
            cp.start()
        token[...] = jnp.zeros_like(token)

    dma7 = pltpu.SemaphoreType.DMA((N_DEV - 1,))
    out = pl.pallas_call(
        body, name=name,
        out_shape=(dma7, dma7, pltpu.SemaphoreType.DMA(()), pltpu.HBM(src.shape, src.dtype),
                   pltpu.HBM(land_shape, src.dtype), jax.ShapeDtypeStruct((8, 128), F32)),
        in_specs=(_HBM, _HBM), out_specs=(_SEM, _SEM, _SEM, _HBM, _HBM, pl.BlockSpec(memory_space=pltpu.VMEM)),
        input_output_aliases={0: 3, 1: 4},
        compiler_params=pltpu.CompilerParams(has_side_effects=_DATAFLOW),
    )(pltpu.with_memory_space_constraint(src, pltpu.HBM),
      pltpu.with_memory_space_constraint(lax.empty(land_shape, src.dtype), pltpu.HBM))
    return out[:5], out[5][0, 0]


def split_wait(name, handles, after, gather):
    send_sems, recv_sems, local_sem, src_thru, land_thru = handles

    def body(src_ref, land_ref, send_sems, recv_sems, local_sem, after_ref, src_dead, got_ref, token):
        local, sends, recvs = _split_copies(gather, src_ref, land_ref, send_sems, recv_sems, local_sem)
        local.wait()
        for cp in recvs:
            cp.wait_send()
            cp.wait_recv()
        token[...] = jnp.zeros_like(token)

    out = pl.pallas_call(
        body, name=name,
        out_shape=(pltpu.HBM(src_thru.shape, src_thru.dtype), pltpu.HBM(land_thru.shape, land_thru.dtype),
                   jax.ShapeDtypeStruct((8, 128), F32)),
        in_specs=(_HBM, _HBM, _SEM, _SEM, _SEM, pl.BlockSpec(memory_space=pl.ANY)),
        out_specs=(_HBM, _HBM, pl.BlockSpec(memory_space=pltpu.VMEM)),
        input_output_aliases={0: 0, 1: 1},
        compiler_params=pltpu.CompilerParams(has_side_effects=_DATAFLOW),
    )(src_thru, land_thru, send_sems, recv_sems, local_sem, after)
    return out[1], out[2][0, 0]


def sum_parts(name, r):
    _, rows, cols = r.shape

    def body(r_ref, o_ref):
        acc = r_ref[0].astype(F32)
        for s in range(1, N_DEV):
            acc = acc + r_ref[s].astype(F32)
        o_ref[...] = acc

    return pl.pallas_call(body, name=name, out_shape=jax.ShapeDtypeStruct((rows, cols), F32),
                          compiler_params=_params())(r)


def adamw(name, w, m, v, parts=None, g=None, layer=0, into=None, order=None):
    _, rows, cols = w.shape
    br = _tile(rows, 256, 16)
    c1 = 1.0 / (1.0 - ADAM_B1 ** ADAM_STEP)
    c2 = 1.0 / (1.0 - ADAM_B2 ** ADAM_STEP)

    def body(g_ref, w_ref, m_ref, v_ref, *rest):
        og_ref, od_ref, om_ref, ov_ref = rest[-4:]
        if parts is None:
            gs = g_ref[...]
        else:
            gs = g_ref[0].astype(F32)
            for s in range(1, N_DEV):
                gs = gs + g_ref[s].astype(F32)
        mn = ADAM_B1 * m_ref[...] + (1.0 - ADAM_B1) * gs
        vn = ADAM_B2 * v_ref[...] + (1.0 - ADAM_B2) * (gs * gs)
        og_ref[...] = gs
        om_ref[...] = mn
        ov_ref[...] = vn
        od_ref[...] = -ADAM_LR * ((mn * c1) / (jnp.sqrt(vn * c2) + ADAM_EPS) + ADAM_WD * w_ref[...])

    blk = pl.BlockSpec((None, br, cols), lambda i: (layer, i, 0))
    if parts is None:
        gspec = pl.BlockSpec((br, cols), lambda i: (i, 0))
    else:
        gspec = pl.BlockSpec((N_DEV, br, cols), lambda i: (0, i, 0))
    earlier = [] if into is None else list(into)
    behind = [] if order is None else [order]
    return pl.pallas_call(
        body, name=name, grid=(rows // br,),
        in_specs=[gspec, blk, blk, blk] + [pl.BlockSpec(memory_space=pl.ANY)] * len(earlier)
                 + [pl.BlockSpec((1, 128), lambda i: (0, 0))] * len(behind),
        out_specs=[blk] * 4, out_shape=[jax.ShapeDtypeStruct(w.shape, F32)] * 4,
        input_output_aliases={4 + k: k for k in range(len(earlier))},
        compiler_params=_params(("parallel",)),
    )(g if parts is None else parts, w, m, v, *earlier, *behind)


SMALL = ("norm_mix", "norm_xattn", "norm_ffn", "norm_mem", "norm_final", "pool_w", "pool_scale",
         "ssm_lam_re", "ssm_lam_im", "ssm_log_dt", "ssm_b_re", "ssm_b_im", "ssm_c_re", "ssm_c_im",
         "ffn_conv_b", "ssm_d", "ffn_conv_w")
SMALL_SHARDED = {"ssm_d": 1, "ffn_conv_w": 2}
BIG = ("ab_w_in", "ab_w_out", "ssm_w_in", "ssm_w_glu", "xa_w_q", "xa_w_kv", "xa_w_o", "ffn_w_up", "ffn_w_down")
WEIGHTS = ("norm_mix", "norm_xattn", "norm_ffn", "norm_mem", "norm_final", "ab_w_in", "pool_w", "pool_scale",
           "ab_w_out", "ssm_w_in", "ssm_lam_re", "ssm_lam_im", "ssm_log_dt", "ssm_b_re", "ssm_b_im", "ssm_c_re",
           "ssm_c_im", "ssm_d", "ssm_w_glu", "xa_w_q", "xa_w_kv", "xa_w_o", "ffn_w_up", "ffn_conv_w", "ffn_conv_b",
           "ffn_w_down")


def _rows8(g):
    return g.reshape(N_DEV, g.size // (N_DEV * D_MODEL), D_MODEL)


def _square(a):
    return a.reshape(D_MODEL, D_MODEL)


_LAYOUT = {"ab_w_out": _square, "ssm_w_in": _square, "xa_w_q": _square, "xa_w_o": _square,
           "ffn_w_down": lambda a: a.reshape(N_DEV // 2, FF_SHARD, D_MODEL)}
GATHER_ORDER = (("ab_w_in", 0), ("ffn_conv_w", None), ("ssm_d", None), ("ab_w_out", 0), ("xa_w_q", 0),
                ("xa_w_kv", 0), ("xa_w_o", 0), ("ffn_w_up", 0), ("ffn_w_down", 0), ("ffn_w_up", 1),
                ("ffn_w_down", 1), ("ssm_w_in", 0), ("ssm_w_glu", 0), ("xa_w_q", 1), ("xa_w_kv", 1), ("xa_w_o", 1))
GATHER_FIRST = 3
GATHER_AHEAD = 7


class _Step:
    def __init__(self, master, small):
        self.master, self.small = master, small
        self.pending, self.gathers, self.weights, self.sent = [], {}, {}, []

    def follow(self, v):
        for z in self.pending:
            v = v + z
        self.pending = []
        return v

    def start_gathers(self, upto, zero):
        for n, l in GATHER_ORDER[len(self.gathers):upto]:
            if l is None:
                shard = self.master[n] + zero
            else:
                shard = (self.master[n][l] + zero).astype(MXU_DTYPE)
            self.gathers[(n, l)], z = split_start(f"ags_{n}{'' if l is None else l}", shard, gather=True)
            self.pending.append(z)

    def weight(self, n, l, after):
        if (n, l) not in self.weights:
            full, z = split_wait(f"agw_{n}{'' if l is None else l}", self.gathers[(n, l)], after, gather=True)
            self.weights[(n, l)] = _LAYOUT.get(n, lambda a: a)(full)
            self.start_gathers(GATHER_ORDER.index((n, l)) + 1 + GATHER_AHEAD, z)
        return self.weights[(n, l)]

    def send_grad(self, n, l, part):
        h, z = split_start(f"xs_{n}{l}", part, gather=False)
        self.pending.append(z)
        self.sent.append((n, l, h))


def _layer_tail(st, l, x_in, hq, mem_n, acts, next_gain=None):
    bsz, seq = acts["bsz"], acts["seq"]
    p = st.small
    q = mm_nn(f"xa_q{l}", hq, st.weight("xa_w_q", l, x_in))
    kv = mm_nn_bs(f"xa_kv{l}", mem_n, st.weight("xa_w_kv", l, x_in))
    o = xattn_fwd(q, kv, bsz, seq)
    x_mid, hf = mm_nn(f"xa_o{l}", o, st.weight("xa_w_o", l, o), res=x_in, out_dtype=F32,
                      norm_gain=st.follow(p["norm_ffn"][l]))
    up = mm_nn_bs(f"ffn_up{l}", hf, st.weight("ffn_w_up", l, x_mid), stacked_out=True)
    conv_w = st.weight("ffn_conv_w", None, x_mid)[:, l]
    act = conv_fwd(up, conv_w, p["ffn_conv_b"][l], bsz, seq)
    w_down = st.weight("ffn_w_down", l, act)
    if next_gain is None:
        x_out, h_next = mm_as_nn(f"ffn_down{l}", act, w_down, res=x_mid), None
    else:
        x_out, h_next = mm_as_nn(f"ffn_down{l}", act, w_down, res=x_mid, norm_gain=st.follow(next_gain))
    acts[l].update(x_in=x_in, hq=hq, q=q, kv=kv, o=o, x_mid=x_mid, hf=hf, up=up, act=act)
    return x_out, h_next


def _layer_tail_bwd(st, l, dx, mem_n, acts, grads):
    a = acts[l]
    bsz, seq = acts["bsz"], acts["seq"]
    p = st.small
    dact = mm_nt_os(f"d_act{l}", dx, st.weight("ffn_w_down", l, dx))
    st.send_grad("ffn_w_down", l, _rows8(mm_tn(f"g_ffn_down{l}", a["act"], dx, a_stacked=True)))
    conv_w = st.weight("ffn_conv_w", None, dx)[:, l]
    dconv, dcw, dcb = conv_bwd_taps(a["up"], conv_w, p["ffn_conv_b"][l], dact, bsz, seq)
    grads["ffn_conv_w"][l] = dcw
    grads["ffn_conv_b"][l] = dcb
    dup = conv_bwd_input(dconv, conv_w, bsz, seq)
    dx_mid, grads["norm_ffn"][l] = mm_nt_bs(f"d_hf{l}", dup, st.weight("ffn_w_up", l, dx), dc_stacked=True,
                                            rms=(a["x_mid"], st.follow(p["norm_ffn"][l]), dx))
    st.send_grad("ffn_w_up", l, mm_tn(f"g_ffn_up{l}", a["hf"], dup, dc_stacked=True))
    do = mm_nt(f"d_o{l}", dx_mid, st.weight("xa_w_o", l, dx))
    st.send_grad("xa_w_o", l, _rows8(mm_tn(f"g_xa_o{l}", a["o"], dx_mid)))
    dq, dk, dv = xattn_bwd(a["q"], a["kv"], do, bsz, seq)
    dkv = jnp.concatenate([dk, dv], axis=1).astype(BF16)
    dx_in, grads["norm_xattn"][l] = mm_nt(f"d_hq{l}", dq, st.weight("xa_w_q", l, dx),
                                          rms=(a["x_in"], st.follow(p["norm_xattn"][l]), dx_mid))
    st.send_grad("xa_w_q", l, _rows8(mm_tn(f"g_xa_q{l}", a["hq"], dq)))
    dmem_n = mm_nt_bs(f"d_memn{l}", dkv, st.weight("xa_w_kv", l, dx), out_dtype=F32)
    st.send_grad("xa_w_kv", l, mm_tn(f"g_xa_kv{l}", mem_n, dkv, dc_cols=2 * D_MODEL // N_DEV))
    return dx_in, dmem_n


def kernel(x, mem, norm_mix, norm_xattn, norm_ffn, norm_mem, norm_final, ab_w_in, pool_w, pool_scale, ab_w_out, ssm_w_in, ssm_lam_re, ssm_lam_im, ssm_log_dt, ssm_b_re, ssm_b_im, ssm_c_re, ssm_c_im, ssm_d, ssm_w_glu, xa_w_q, xa_w_kv, xa_w_o, ffn_w_up, ffn_conv_w, ffn_conv_b, ffn_w_down, loss_target, m_norm_mix, m_norm_xattn, m_norm_ffn, m_norm_mem, m_norm_final, m_ab_w_in, m_pool_w, m_pool_scale, m_ab_w_out, m_ssm_w_in, m_ssm_lam_re, m_ssm_lam_im, m_ssm_log_dt, m_ssm_b_re, m_ssm_b_im, m_ssm_c_re, m_ssm_c_im, m_ssm_d, m_ssm_w_glu, m_xa_w_q, m_xa_w_kv, m_xa_w_o, m_ffn_w_up, m_ffn_conv_w, m_ffn_conv_b, m_ffn_w_down, v_norm_mix, v_norm_xattn, v_norm_ffn, v_norm_mem, v_norm_final, v_ab_w_in, v_pool_w, v_pool_scale, v_ab_w_out, v_ssm_w_in, v_ssm_lam_re, v_ssm_lam_im, v_ssm_log_dt, v_ssm_b_re, v_ssm_b_im, v_ssm_c_re, v_ssm_c_im, v_ssm_d, v_ssm_w_glu, v_xa_w_q, v_xa_w_kv, v_xa_w_o, v_ffn_w_up, v_ffn_conv_w, v_ffn_conv_b, v_ffn_w_down):
    given = dict(locals())
    master = {n: given[n] for n in WEIGHTS}
    mom1 = {n: given["m_" + n] for n in WEIGHTS}
    mom2 = {n: given["v_" + n] for n in WEIGHTS}
    bsz, seq, d = x.shape
    t = bsz * seq
    me = _my_index()

    st = _Step(master, {"norm_xattn": norm_xattn, "norm_ffn": norm_ffn,
                        "ffn_conv_b": [ffn_conv_b[l].reshape(N_DEV, 1, FF_SHARD) for l in range(2)]})
    st.start_gathers(GATHER_FIRST, 0.0)
    zero = st.follow(jnp.zeros((), F32))

    acts = {"bsz": bsz, "seq": seq, 0: {}, 1: {}}
    x0 = x.reshape(t, d)
    mem2 = mem.reshape(bsz * MEM_LEN, d)
    mem_n = rms_fwd("rms_mem", mem2, norm_mem + zero)
    pscale = pool_scale.reshape(1, SB_WIDTH)

    h0 = rms_fwd("rms_mix0", x0, norm_mix[0] + zero)
    w_in = st.weight("ab_w_in", 0, h0)
    proj = mm_nn_bs("ab_in", h0, w_in, out_dtype=F32)
    a_out, rsum = sb_attn_fwd(proj, st.follow(jnp.zeros((1, 128), F32)), bsz, seq)
    p_out = pool_fwd(proj, pool_w[0], pscale, bsz, seq)
    w_out = st.weight("ab_w_out", 0, a_out)
    x1 = mm_nn("ab_out_a", a_out, w_out, res=x0, out_dtype=F32)
    x1, hq0 = mm_nn("ab_out_p", p_out, w_out, res=x1, koff=SB_WIDTH, out_dtype=F32,
                    norm_gain=st.follow(norm_xattn[0]))
    x3, h1 = _layer_tail(st, 0, x1, hq0, mem_n, acts, next_gain=norm_mix[1])

    b_re2 = ssm_b_re.reshape(64, 1024)
    b_im2 = ssm_b_im.reshape(64, 1024)
    log_dt = ssm_log_dt.reshape(64, 1)
    lb_re, lb_im, bb_re2, bb_im2 = ssm_prep(ssm_lam_re[0], ssm_lam_im[0], log_dt, b_re2, b_im2)
    wt = _ssm_in_weights(bb_re2, bb_im2)
    ct = _ssm_out_weights(ssm_c_re[0], ssm_c_im[0])
    a_re = lb_re.reshape(1, SSM_STATES)
    a_im = lb_im.reshape(1, SSM_STATES)
    u = mm_nn("ssm_in", h1, st.weight("ssm_w_in", 0, x3), out_dtype=F32)
    dskip = st.weight("ssm_d", None, x3).reshape(1, D_MODEL)
    y, gl, h_re, h_im = ssm_fwd(u, wt, ct, a_re, a_im, dskip, bsz, seq)
    glu = mm_nn_bs("ssm_glu", gl, st.weight("ssm_w_glu", 0, gl), out_dtype=F32)
    x4, hq1 = glu_fwd(glu, x3, st.follow(norm_xattn[1]))
    x6, _ = _layer_tail(st, 1, x4, hq1, mem_n, acts)

    loss_row, dx, g_norm_final = loss_head(x6, norm_final, loss_target.reshape(t, d))
    loss = lax.psum(loss_row[0, 0], MESH_AXES)

    grads = {n: [None, None] for n in ("ffn_conv_w", "ffn_conv_b", "norm_ffn", "norm_xattn", "norm_mix")}
    dx4, dmem_1 = _layer_tail_bwd(st, 1, dx, mem_n, acts, grads)
    dglu = glu_bwd(glu, dx4)
    dgl = mm_nt_bs("d_gl", dglu, st.weight("ssm_w_glu", 0, dx))
    st.send_grad("ssm_w_glu", 0, mm_tn("g_ssm_glu", gl, dglu, dc_cols=2 * D_MODEL // N_DEV))
    du, dwt, dct, g_dskip, da_re, da_im = ssm_bwd(dgl, y, u, h_re, h_im, wt, ct, a_re, a_im, dskip, bsz, seq)
    dbb_re, dbb_im = _ssm_in_weights_bwd(dwt)
    g_c_re, g_c_im = _ssm_out_weights_bwd(dct)
    g_lam_re, g_lam_im, g_log_dt, g_b_re, g_b_im = ssm_prep_bwd(
        ssm_lam_re[0], ssm_lam_im[0], log_dt, b_re2, b_im2, da_re.reshape(64, 64), da_im.reshape(64, 64),
        dbb_re, dbb_im)
    dx3, grads["norm_mix"][1] = mm_nt("d_h1", du, st.weight("ssm_w_in", 0, dx),
                                      rms=(x3, st.follow(norm_mix[1]), dx4))
    st.send_grad("ssm_w_in", 0, _rows8(mm_tn("g_ssm_in", h1, du)))

    dx1, dmem_0 = _layer_tail_bwd(st, 0, dx3, mem_n, acts, grads)
    dcat = mm_nt("d_cat", dx1, st.weight("ab_w_out", 0, dx))
    st.send_grad("ab_w_out", 0, _rows8(jnp.concatenate(
        [mm_tn("g_ab_out_a", a_out, dx1), mm_tn("g_ab_out_p", p_out, dx1)], axis=0)))
    dq, dk, dv = sb_attn_bwd(proj, rsum, dcat, bsz, seq)
    dpu, g_pool_w, g_pool_scale = pool_bwd(proj, pool_w[0], st.follow(pscale), dcat, bsz, seq)
    dproj = jnp.concatenate([dq, dk, dv, dpu], axis=1).astype(BF16)
    st.send_grad("ab_w_in", 0, mm_tn("g_ab_in", h0, dproj, dc_cols=2 * D_MODEL // N_DEV))
    dx0, grads["norm_mix"][0] = mm_nt_bs("d_h0", dproj, st.weight("ab_w_in", 0, dx),
                                         rms=(x0, st.follow(norm_mix[0]), dx1))
    _, g_norm_mem = rms_bwd("rms_mem_bwd", mem2, norm_mem, dmem_0 + dmem_1, need_dx=False)

    small_g = {
        "norm_mix": jnp.stack([g[0] for g in grads["norm_mix"]]),
        "norm_xattn": jnp.stack([g[0] for g in grads["norm_xattn"]]),
        "norm_ffn": jnp.stack([g[0] for g in grads["norm_ffn"]]),
        "norm_mem": g_norm_mem[0], "norm_final": g_norm_final[0],
        "pool_w": g_pool_w[None], "pool_scale": g_pool_scale,
        "ssm_lam_re": g_lam_re[None], "ssm_lam_im": g_lam_im[None], "ssm_log_dt": g_log_dt.reshape(1, 64),
        "ssm_b_re": g_b_re.reshape(1, 64, 64, 16), "ssm_b_im": g_b_im.reshape(1, 64, 64, 16),
        "ssm_c_re": g_c_re[None], "ssm_c_im": g_c_im[None],
        "ffn_conv_b": jnp.stack([g.reshape(2 * D_FF) for g in grads["ffn_conv_b"]]),
        "ssm_d": g_dskip,
        "ffn_conv_w": jnp.stack([g.transpose(1, 0, 2).reshape(3, 2 * D_FF) for g in grads["ffn_conv_w"]]),
    }
    sizes = [int(small_g[n].size) for n in SMALL]
    total = sum(sizes)
    rows8 = -(-total // (N_DEV * 128 * 8)) * 8
    flat = jnp.concatenate([small_g[n].reshape(-1).astype(F32) for n in SMALL]
                           + [jnp.zeros((N_DEV * rows8 * 128 - total,), F32)])
    in_flight, z = split_start("xs_small", flat.reshape(N_DEV, rows8, 128), gather=False)
    st.pending.append(z)
    stepped, last = {}, dx0
    for i, (n, l, handles) in enumerate(st.sent):
        if i == len(st.sent) // 2:
            recv, _ = split_wait("xw_small", in_flight, last, gather=False)
            in_flight, z = split_start("ags_small", sum_parts("sum_small", recv), gather=True)
            st.pending.append(z)
        recv, _ = split_wait(f"xw_{n}{l}", handles, dx0, gather=False)
        shape3 = (master[n].shape[0],) + recv.shape[1:]
        stepped[n] = adamw(f"adamw_{n}{l}", master[n].reshape(shape3), mom1[n].reshape(shape3),
                           mom2[n].reshape(shape3), parts=recv, layer=l, into=stepped.get(n),
                           order=st.follow(jnp.zeros((1, 128), F32)))
        last = stepped[n][0]
    out_g, out_d, out_m, out_v = ({n: stepped[n][k].reshape(master[n].shape) for n in BIG} for k in range(4))
    summed = split_wait("agw_small", in_flight, last, gather=True)[0].reshape(-1)

    def local_part(name, a):
        ax = SMALL_SHARDED.get(name)
        if ax is None:
            return a
        n_loc = a.shape[ax] // N_DEV
        return lax.dynamic_slice_in_dim(a, me * n_loc, n_loc, axis=ax)

    off = 0
    for n, sz in zip(SMALL, sizes):
        g_n = local_part(n, summed[off:off + sz].reshape(small_g[n].shape))
        off += sz
        cols = g_n.shape[-1] if g_n.shape[-1] >= 128 or g_n.ndim < 3 else g_n.shape[-1] * g_n.shape[-2]
        shape3 = (1, g_n.size // cols, cols)
        res = adamw("adamw_" + n, master[n].reshape(shape3), mom1[n].reshape(shape3), mom2[n].reshape(shape3),
                    g=g_n.reshape(shape3[1:]))
        for dst, r in zip((out_g, out_d, out_m, out_v), res):
            dst[n] = r.reshape(master[n].shape)

    return (loss, dx0.reshape(bsz, seq, d), *[out_g[n] for n in WEIGHTS], *[out_d[n] for n in WEIGHTS],
            *[out_m[n] for n in WEIGHTS], *[out_v[n] for n in WEIGHTS])
```

```python
import math

import jax
import jax.numpy as jnp
from jax import lax
from jax.experimental import pallas as pl
from jax.experimental.pallas import tpu as pltpu

F32 = jnp.float32
BF16 = jnp.bfloat16
MXU_DTYPE = jnp.bfloat16
N_DEV = 8
MESH_AXES = ("x", "y", "c")

D_MODEL = 1024
SB_HEAD_DIM = 64
SB_WIDTH = 512
SB_BLOCK = 256
POOL_WINDOWS = (2, 4, 8, 16)
POOL_GROUP = 128
POOL_HALO = 16
SSM_TILES = 8
SSM_TILE_STATES = 512
SSM_STATES = 4096
SSM_LANES = 1024
MEM_LEN = 256
XA_HEADS = 4
XA_HEAD_DIM = 256
D_FF = 2816
FF_SHARD = 704
EPS = 1e-6
ADAM_LR = 0.001
ADAM_B1 = 0.9
ADAM_B2 = 0.999
ADAM_EPS = 1e-08
ADAM_WD = 0.01
ADAM_STEP = 10
VMEM_LIMIT = 56 * 1024 * 1024

_NN = (((1,), (0,)), ((), ()))
_NT = (((1,), (1,)), ((), ()))
_TN = (((0,), (0,)), ((), ()))


def _params(sem=None):
    if sem is None:
        return pltpu.CompilerParams(vmem_limit_bytes=VMEM_LIMIT)
    return pltpu.CompilerParams(dimension_semantics=sem, vmem_limit_bytes=VMEM_LIMIT)


def _tile(n, pref, mult=8):
    if n <= pref:
        return n
    for t in range(pref, 0, -1):
        if n % t == 0 and t % mult == 0:
            return t
    return n


def _dot(a, b, dims):
    return lax.dot_general(a.astype(MXU_DTYPE), b.astype(MXU_DTYPE), dims, preferred_element_type=F32)


def _dot_exact01(x, m01, dims=_NN):
    x1 = x.astype(BF16)
    r1 = x - x1.astype(F32)
    x2 = r1.astype(BF16)
    x3 = (r1 - x2.astype(F32)).astype(BF16)
    m = m01.astype(BF16)
    out = lax.dot_general(x1, m, dims, preferred_element_type=F32)
    out = out + lax.dot_general(x2, m, dims, preferred_element_type=F32)
    return out + lax.dot_general(x3, m, dims, preferred_element_type=F32)


def _mm(name, a, b, dims, grid, a_spec, b_spec, o_spec, out_shape, out_dtype, acc_shape, res=None, r_spec=None,
        group=1, n=None, a_sel="full", b_sel="full", o_sel="full", norm_gain=None, rms=None):
    nk = grid[2]
    if out_dtype is None:
        out_dtype = BF16
    n_out = out_shape[-1]
    vec = pl.BlockSpec((1, n_out), lambda i, j, kk: (0, 0))

    def at(sel, s):
        if sel == "lead":
            return (s,)
        if sel == "lanes":
            return (slice(None), slice(s * n, (s + 1) * n))
        return (Ellipsis,)

    extra = [] if res is None else [(res, r_spec)]
    if norm_gain is not None:
        extra.append((norm_gain.reshape(1, n_out), vec))
    if rms is not None:
        extra += [(rms[0], o_spec), (rms[1].reshape(1, n_out), vec), (rms[2], o_spec)]
    n_in = 2 + len(extra)
    if rms is not None:
        out_specs = [o_spec, vec]
        out_shapes = [jax.ShapeDtypeStruct(out_shape, F32), jax.ShapeDtypeStruct((1, n_out), F32)]
    elif norm_gain is not None:
        out_specs = [o_spec, o_spec]
        out_shapes = [jax.ShapeDtypeStruct(out_shape, out_dtype), jax.ShapeDtypeStruct(out_shape, BF16)]
    else:
        out_specs, out_shapes = o_spec, jax.ShapeDtypeStruct(out_shape, out_dtype)

    def body(*refs):
        a_ref, b_ref = refs[0], refs[1]
        ins = list(refs[2:n_in])
        r_ref = ins.pop(0) if res is not None else None
        outs = refs[n_in:]
        o_ref = outs[0]
        acc = refs[-1] if nk > 1 else None
        k = pl.program_id(2)

        def finish(val):
            if r_ref is not None:
                val = val + r_ref[...].astype(F32)
            if rms is not None:
                x_ref, g_ref, d_ref = ins
                xf = x_ref[...]
                r = lax.rsqrt(jnp.mean(xf * xf, axis=-1, keepdims=True) + EPS)
                xh = xf * r
                part = jnp.sum(val * xh, axis=0, keepdims=True)
                first = pl.program_id(0) == 0

                @pl.when(first)
                def _():
                    outs[1][...] = part

                @pl.when(jnp.logical_not(first))
                def _():
                    outs[1][...] += part

                dxh = val * g_ref[...]
                o_ref[...] = d_ref[...] + r * (dxh - xh * jnp.mean(dxh * xh, axis=-1, keepdims=True))
                return
            o_ref[...] = val.astype(out_dtype)
            if norm_gain is not None:
                r = lax.rsqrt(jnp.mean(val * val, axis=-1, keepdims=True) + EPS)
                outs[1][...] = (val * r * ins[0][...]).astype(BF16)

        def emit(s, val):
            if nk == 1:
                if o_sel == "full":
                    finish(val)
                else:
                    o_ref[at(o_sel, s)] = val.astype(out_dtype)
                return

            @pl.when(k == 0)
            def _():
                acc[at(o_sel, s)] = val

            @pl.when(k > 0)
            def _():
                acc[at(o_sel, s)] += val

        total = None
        for s in range(group):
            val = _dot(a_ref[at(a_sel, s)], b_ref[at(b_sel, s)], dims)
            if o_sel == "full":
                total = val if total is None else total + val
            else:
                emit(s, val)
        if o_sel == "full":
            emit(0, total)
        if nk > 1:
            @pl.when(k == nk - 1)
            def _():
                if o_sel == "full":
                    finish(acc[...])
                else:
                    o_ref[...] = acc[...].astype(out_dtype)

    rows_sem = "arbitrary" if rms is not None else "parallel"
    return pl.pallas_call(
        body, name=name, grid=grid, in_specs=[a_spec, b_spec] + [s for _, s in extra], out_specs=out_specs,
        out_shape=out_shapes, scratch_shapes=[pltpu.VMEM(acc_shape, F32)] if nk > 1 else [],
        compiler_params=_params((rows_sem, rows_sem, "arbitrary")),
    )(a, b, *[x for x, _ in extra])


def _row_tile(m, epi):
    return _tile(m, 512 if epi.get("rms") is not None else 1024)


def mm_nn(name, a, b, res=None, koff=0, out_dtype=None, **epi):
    m, k = a.shape
    n = b.shape[1]
    tm, tn, tk = _row_tile(m, epi), _tile(n, 1024, 128), _tile(k, 1024, 128)
    kb = koff // tk
    spec = pl.BlockSpec((tm, tn), lambda i, j, kk: (i, j))
    return _mm(name, a, b, _NN, (m // tm, n // tn, k // tk),
               pl.BlockSpec((tm, tk), lambda i, j, kk: (i, kk)),
               pl.BlockSpec((tk, tn), lambda i, j, kk: (kk + kb, j)),
               spec, (m, n), out_dtype, (tm, tn), res, spec, **epi)


def mm_nn_bs(name, a, bs, stacked_out=False, out_dtype=None):
    m, k = a.shape
    s, _, n = bs.shape
    tm, tk = _tile(m, 1024), _tile(k, 1024, 128)
    a_spec = pl.BlockSpec((tm, tk), lambda i, j, kk: (i, kk))
    if stacked_out:
        return _mm(name, a, bs, _NN, (m // tm, s, k // tk), a_spec,
                   pl.BlockSpec((None, tk, n), lambda i, j, kk: (j, kk, 0)),
                   pl.BlockSpec((None, tm, n), lambda i, j, kk: (j, i, 0)), (s, m, n), out_dtype, (tm, n))
    g = _tile(s, max(1, 1024 // n), 1)
    return _mm(name, a, bs, _NN, (m // tm, s // g, k // tk), a_spec,
               pl.BlockSpec((g, tk, n), lambda i, j, kk: (j, kk, 0)),
               pl.BlockSpec((tm, g * n), lambda i, j, kk: (i, j)), (m, s * n), out_dtype, (tm, g * n),
               group=g, n=n, b_sel="lead", o_sel="lanes")


def mm_as_nn(name, a_st, b3, res, out_dtype=F32, **epi):
    s, m, kp = a_st.shape
    n = b3.shape[2]
    tm, tn = _row_tile(m, epi), _tile(n, 1024, 128)
    spec = pl.BlockSpec((tm, tn), lambda i, j, kk: (i, j))
    g = _tile(s, 2, 1)
    return _mm(name, a_st, b3, _NN, (m // tm, n // tn, s // g),
               pl.BlockSpec((g, tm, kp), lambda i, j, kk: (kk, i, 0)),
               pl.BlockSpec((g, kp, tn), lambda i, j, kk: (kk, 0, j)),
               spec, (m, n), out_dtype, (tm, tn), res, spec, group=g, a_sel="lead", b_sel="lead", **epi)


def mm_nt(name, dc, b, out_dtype=None, **epi):
    m, n = dc.shape
    k = b.shape[0]
    tm, tko, tnr = _row_tile(m, epi), _tile(k, 1024, 128), _tile(n, 1024, 128)
    return _mm(name, dc, b, _NT, (m // tm, k // tko, n // tnr),
               pl.BlockSpec((tm, tnr), lambda i, j, kk: (i, kk)),
               pl.BlockSpec((tko, tnr), lambda i, j, kk: (j, kk)),
               pl.BlockSpec((tm, tko), lambda i, j, kk: (i, j)), (m, k), out_dtype, (tm, tko), **epi)


def mm_nt_bs(name, dc, bs, dc_stacked=False, out_dtype=None, **epi):
    s, k, n = bs.shape
    m = dc.shape[1] if dc_stacked else dc.shape[0]
    tm, tko = (_tile(m, 1024) if dc_stacked else _row_tile(m, epi)), _tile(k, 1024, 128)
    o_spec = pl.BlockSpec((tm, tko), lambda i, j, kk: (i, j))
    if dc_stacked:
        g = _tile(s, 2, 1)
        return _mm(name, dc, bs, _NT, (m // tm, k // tko, s // g),
                   pl.BlockSpec((g, tm, n), lambda i, j, kk: (kk, i, 0)),
                   pl.BlockSpec((g, tko, n), lambda i, j, kk: (kk, j, 0)), o_spec, (m, k), out_dtype, (tm, tko),
                   group=g, a_sel="lead", b_sel="lead", **epi)
    g = _tile(s, max(1, 2048 // n), 1)
    return _mm(name, dc, bs, _NT, (m // tm, k // tko, s // g),
               pl.BlockSpec((tm, g * n), lambda i, j, kk: (i, kk)),
               pl.BlockSpec((g, tko, n), lambda i, j, kk: (kk, j, 0)), o_spec, (m, k), out_dtype, (tm, tko),
               group=g, n=n, a_sel="lanes", b_sel="lead", **epi)


def mm_nt_os(name, dc, b3, out_dtype=None):
    m, n = dc.shape
    s, kp, _ = b3.shape
    tm, tnr = _tile(m, 1024), _tile(n, 1024, 128)
    return _mm(name, dc, b3, _NT, (m // tm, s, n // tnr),
               pl.BlockSpec((tm, tnr), lambda i, j, kk: (i, kk)),
               pl.BlockSpec((None, kp, tnr), lambda i, j, kk: (j, 0, kk)),
               pl.BlockSpec((None, tm, kp), lambda i, j, kk: (j, i, 0)), (s, m, kp), out_dtype, (tm, kp))


def mm_tn(name, a, dc, a_stacked=False, dc_cols=None, dc_stacked=False, out_dtype=None):
    if a_stacked:
        s, m, kp = a.shape
        n = dc.shape[1]
        tno, tmr = _tile(n, 1024, 128), _tile(m, 2048)
        return _mm(name, a, dc, _TN, (s, n // tno, m // tmr),
                   pl.BlockSpec((None, tmr, kp), lambda i, j, kk: (i, kk, 0)),
                   pl.BlockSpec((tmr, tno), lambda i, j, kk: (kk, j)),
                   pl.BlockSpec((None, kp, tno), lambda i, j, kk: (i, 0, j)), (s, kp, n), out_dtype, (kp, tno))
    m, k = a.shape
    tko, tmr = _tile(k, 1024, 128), _tile(m, 2048)
    a_spec = pl.BlockSpec((tmr, tko), lambda i, j, kk: (kk, i))
    if dc_stacked:
        s, _, n = dc.shape
        return _mm(name, a, dc, _TN, (k // tko, s, m // tmr), a_spec,
                   pl.BlockSpec((None, tmr, n), lambda i, j, kk: (j, kk, 0)),
                   pl.BlockSpec((None, tko, n), lambda i, j, kk: (j, i, 0)), (s, k, n), out_dtype, (tko, n))
    if dc_cols is not None:
        n = dc_cols
        s = dc.shape[1] // n
        g = _tile(s, max(1, 1024 // n), 1)
        return _mm(name, a, dc, _TN, (k // tko, s // g, m // tmr), a_spec,
                   pl.BlockSpec((tmr, g * n), lambda i, j, kk: (kk, j)),
                   pl.BlockSpec((g, tko, n), lambda i, j, kk: (j, i, 0)), (s, k, n), out_dtype, (g, tko, n),
                   group=g, n=n, b_sel="lanes", o_sel="lead")
    n = dc.shape[1]
    tno = _tile(n, 1024, 128)
    return _mm(name, a, dc, _TN, (k // tko, n // tno, m // tmr), a_spec,
               pl.BlockSpec((tmr, tno), lambda i, j, kk: (kk, j)),
               pl.BlockSpec((tko, tno), lambda i, j, kk: (i, j)), (k, n), out_dtype, (tko, tno))


def rms_fwd(name, x, g):
    t, d = x.shape
    tr = _tile(t, 512)

    def body(x_ref, g_ref, o_ref):
        xf = x_ref[...]
        r = lax.rsqrt(jnp.mean(xf * xf, axis=-1, keepdims=True) + EPS)
        o_ref[...] = (xf * r * g_ref[...]).astype(o_ref.dtype)

    return pl.pallas_call(
        body, name=name, grid=(t // tr,),
        in_specs=[pl.BlockSpec((tr, d), lambda i: (i, 0)), pl.BlockSpec((1, d), lambda i: (0, 0))],
        out_specs=pl.BlockSpec((tr, d), lambda i: (i, 0)),
        out_shape=jax.ShapeDtypeStruct((t, d), BF16), compiler_params=_params(("parallel",)),
    )(x, g.reshape(1, d))


def rms_bwd(name, x, g, dh, dres=None, need_dx=True):
    t, d = x.shape
    tr = _tile(t, 512)

    def body(*refs):
        refs = list(refs)
        x_ref, g_ref, dh_ref = refs[:3]
        r_ref = refs[3] if dres is not None else None
        outs = refs[4:] if dres is not None else refs[3:]
        dx_ref, dg_ref = (outs[0], outs[1]) if need_dx else (None, outs[0])
        i = pl.program_id(0)

        @pl.when(i == 0)
        def _():
            dg_ref[...] = jnp.zeros_like(dg_ref)

        xf = x_ref[...]
        dhf = dh_ref[...].astype(F32)
        r = lax.rsqrt(jnp.mean(xf * xf, axis=-1, keepdims=True) + EPS)
        xh = xf * r
        dg_ref[...] += jnp.sum(dhf * xh, axis=0, keepdims=True)
        if need_dx:
            dxh = dhf * g_ref[...]
            dx = r * (dxh - xh * jnp.mean(dxh * xh, axis=-1, keepdims=True))
            if r_ref is not None:
                dx = dx + r_ref[...]
            dx_ref[...] = dx

    row = pl.BlockSpec((tr, d), lambda i: (i, 0))
    vec = pl.BlockSpec((1, d), lambda i: (0, 0))
    in_specs = [row, vec, row] + ([row] if dres is not None else [])
    args = (x, g.reshape(1, d), dh) + ((dres,) if dres is not None else ())
    out_specs = ([row] if need_dx else []) + [vec]
    out_shape = ([jax.ShapeDtypeStruct((t, d), F32)] if need_dx else []) + [jax.ShapeDtypeStruct((1, d), F32)]
    res = pl.pallas_call(
        body, name=name, grid=(t // tr,), in_specs=in_specs, out_specs=out_specs, out_shape=out_shape,
        compiler_params=_params(("arbitrary",)),
    )(*args)
    return res if need_dx else (None, res[0])


def loss_head(x, g, tgt):
    t, d = x.shape
    tr = _tile(t, 512)

    def body(x_ref, g_ref, t_ref, l_ref, dx_ref, dg_ref):
        i = pl.program_id(0)

        @pl.when(i == 0)
        def _():
            l_ref[...] = jnp.zeros_like(l_ref)
            dg_ref[...] = jnp.zeros_like(dg_ref)

        xf = x_ref[...]
        r = lax.rsqrt(jnp.mean(xf * xf, axis=-1, keepdims=True) + EPS)
        xh = xf * r
        diff = xh * g_ref[...] - t_ref[...]
        l_ref[...] += 0.5 * jnp.sum(jnp.mean(diff * diff, axis=-1, keepdims=True))
        dy = diff * (1.0 / d)
        dg_ref[...] += jnp.sum(dy * xh, axis=0, keepdims=True)
        dxh = dy * g_ref[...]
        dx_ref[...] = r * (dxh - xh * jnp.mean(dxh * xh, axis=-1, keepdims=True))

    row = pl.BlockSpec((tr, d), lambda i: (i, 0))
    vec = pl.BlockSpec((1, d), lambda i: (0, 0))
    return pl.pallas_call(
        body, name="loss_head", grid=(t // tr,), in_specs=[row, vec, row],
        out_specs=[pl.BlockSpec((1, 128), lambda i: (0, 0)), row, vec],
        out_shape=[jax.ShapeDtypeStruct((1, 128), F32), jax.ShapeDtypeStruct((t, d), F32),
                   jax.ShapeDtypeStruct((1, d), F32)],
        compiler_params=_params(("arbitrary",)),
    )(x, g.reshape(1, d), tgt)


def glu_fwd(glu, x, gain):
    t, d = x.shape
    tr = _tile(t, 512)

    def body(v_ref, g_ref, x_ref, n_ref, o_ref, h_ref):
        y = x_ref[...] + v_ref[...] * jax.nn.sigmoid(g_ref[...])
        o_ref[...] = y
        r = lax.rsqrt(jnp.mean(y * y, axis=-1, keepdims=True) + EPS)
        h_ref[...] = (y * r * n_ref[...]).astype(h_ref.dtype)

    row = pl.BlockSpec((tr, d), lambda i: (i, 0))
    return pl.pallas_call(
        body, name="glu_fwd", grid=(t // tr,),
        in_specs=[row, pl.BlockSpec((tr, d), lambda i: (i, 1)), row, pl.BlockSpec((1, d), lambda i: (0, 0))],
        out_specs=[row, row],
        out_shape=[jax.ShapeDtypeStruct((t, d), F32), jax.ShapeDtypeStruct((t, d), BF16)],
        compiler_params=_params(("parallel",)),
    )(glu, glu, x, gain.reshape(1, d))


def glu_bwd(glu, dmix):
    t, d = dmix.shape
    tr = _tile(t, 512)

    def body(v_ref, g_ref, d_ref, o_ref):
        sg = jax.nn.sigmoid(g_ref[...])
        dm = d_ref[...]
        o_ref[:, :d] = (dm * sg).astype(o_ref.dtype)
        o_ref[:, d:] = (dm * v_ref[...] * sg * (1.0 - sg)).astype(o_ref.dtype)

    return pl.pallas_call(
        body, name="glu_bwd", grid=(t // tr,),
        in_specs=[pl.BlockSpec((tr, d), lambda i: (i, 0)), pl.BlockSpec((tr, d), lambda i: (i, 1)),
                  pl.BlockSpec((tr, d), lambda i: (i, 0))],
        out_specs=pl.BlockSpec((tr, 2 * d), lambda i: (i, 0)),
        out_shape=jax.ShapeDtypeStruct((t, 2 * d), BF16), compiler_params=_params(("parallel",)),
    )(glu, glu, dmix)


def _head_masks(shape):
    lane = lax.broadcasted_iota(jnp.int32, shape, 1)
    return lane < SB_HEAD_DIM


def _stack_heads(xf, is_a):
    return jnp.concatenate([jnp.where(is_a, xf, 0.0), jnp.where(is_a, 0.0, xf)], axis=0).astype(MXU_DTYPE)


def _diag_mask(qb, row0, rows):
    row = (lax.broadcasted_iota(jnp.int32, (rows, qb), 0) + row0) & (qb - 1)
    col = lax.broadcasted_iota(jnp.int32, (rows, qb), 1)
    return col < row


def _tri01(qb, pred):
    j = lax.broadcasted_iota(jnp.int32, (qb, qb), 0)
    s = lax.broadcasted_iota(jnp.int32, (qb, qb), 1)
    m = pred(j, s).astype(BF16)
    return jnp.concatenate([m, m], axis=0)


def _split_cat(x):
    hi = x.astype(BF16)
    lo = (x - hi.astype(F32)).astype(BF16)
    return jnp.concatenate([hi, lo], axis=1)


def sb_attn_fwd(proj, order, bsz, seq):
    qb = SB_BLOCK
    nq = seq // qb
    npair = SB_WIDTH // 128
    scale = SB_HEAD_DIM ** -0.5

    def body(q_ref, k_ref, v_ref, order_ref, o_ref, r_ref):
        qi = pl.program_id(2)
        is_a = _head_masks((qb, 128))
        q2 = _stack_heads(q_ref[...] * scale, is_a)
        diag = _diag_mask(qb, 0, 2 * qb)
        upper = _tri01(qb, lambda j, s: j > s)

        def blocks(kbs, acc, run, masked):
            sl = [pl.ds(pl.multiple_of(kb * qb, qb), qb) for kb in kbs]
            zs = [lax.dot_general(q2, k_ref[s, :].astype(MXU_DTYPE), _NT, preferred_element_type=F32) for s in sl]
            lks = [-jnp.maximum(z, 0.0) - jnp.log(1.0 + jnp.exp(-jnp.abs(z))) for z in zs]
            lbs = [lk + z for lk, z in zip(lks, zs)]
            if masked:
                lks = [jnp.where(diag, lk, 0.0) for lk in lks]
            cs = [lax.dot_general(_split_cat(lk), upper, _NN, preferred_element_type=F32) for lk in lks]
            for lk, lb, c, s in zip(lks, lbs, cs, sl):
                w = jnp.exp(lb + (run + c))
                if masked:
                    w = jnp.where(diag, w, 0.0)
                acc = acc + lax.dot_general(w.astype(MXU_DTYPE), v_ref[s, :].astype(MXU_DTYPE), _NN,
                                            preferred_element_type=F32)
                run = run + jnp.sum(lk, axis=1, keepdims=True)
            return acc, run

        carry = blocks([qi], jnp.zeros((2 * qb, 128), F32), jnp.zeros((2 * qb, 1), F32), True)
        carry = lax.cond(qi % 2 == 1, lambda c: blocks([qi - 1], c[0], c[1], False), lambda c: c, carry)
        top = qi - qi % 2
        acc, run = lax.fori_loop(
            0, qi // 2, lambda i, c: blocks([top - 1 - 2 * i, top - 2 - 2 * i], c[0], c[1], False), carry)
        o_ref[...] = jnp.where(is_a, acc[:qb], acc[qb:]).astype(o_ref.dtype)
        r_ref[...] = jnp.where(is_a, run[:qb], run[qb:])

    return pl.pallas_call(
        body, name="sb_attn_fwd", grid=(bsz, npair, nq),
        in_specs=[pl.BlockSpec((qb, 128), lambda b, p, i: (b * nq + i, p)),
                  pl.BlockSpec((seq, 128), lambda b, p, i: (b, npair + p)),
                  pl.BlockSpec((seq, 128), lambda b, p, i: (b, 2 * npair + p)),
                  pl.BlockSpec((1, 128), lambda b, p, i: (0, 0))],
        out_specs=[pl.BlockSpec((qb, 128), lambda b, p, i: (b * nq + i, p)),
                   pl.BlockSpec((qb, 128), lambda b, p, i: (b * nq + i, p))],
        out_shape=[jax.ShapeDtypeStruct((bsz * seq, SB_WIDTH), BF16),
                   jax.ShapeDtypeStruct((bsz * seq, SB_WIDTH), F32)],
        compiler_params=_params(("parallel", "parallel", "arbitrary")),
    )(proj, proj, proj, order)


def sb_attn_bwd(proj, rsum, dcat, bsz, seq):
    qb = SB_BLOCK
    nq = seq // qb
    npair = SB_WIDTH // 128
    scale = SB_HEAD_DIM ** -0.5

    def body(q_ref, k_ref, v_ref, r_ref, do_ref, dq_ref, dk_ref, dv_ref):
        qi = pl.program_id(2)

        @pl.when(qi == 0)
        def _():
            dk_ref[...] = jnp.zeros_like(dk_ref)
            dv_ref[...] = jnp.zeros_like(dv_ref)

        is_a = _head_masks((qb, 128))
        q2 = _stack_heads(q_ref[...] * scale, is_a)
        do2 = _stack_heads(do_ref[...].astype(F32), is_a)
        rf = r_ref[...]
        rtot = jnp.concatenate([rf[:, 0:1], rf[:, SB_HEAD_DIM:SB_HEAD_DIM + 1]], axis=0)
        diag = _diag_mask(qb, 0, 2 * qb)
        incl = _tri01(qb, lambda j, s: j <= s)
        strict = _tri01(qb, lambda j, s: j < s)

        def blocks(kbs, dq, pre, epre, masked):
            sl = [pl.ds(pl.multiple_of(kb * qb, qb), qb) for kb in kbs]
            ks = [k_ref[s, :].astype(MXU_DTYPE) for s in sl]
            vs = [v_ref[s, :].astype(MXU_DTYPE) for s in sl]
            zs = [lax.dot_general(q2, kblk, _NT, preferred_element_type=F32) for kblk in ks]
            dws = [lax.dot_general(do2, vblk, _NT, preferred_element_type=F32) for vblk in vs]
            lks = [-jnp.maximum(z, 0.0) - jnp.log(1.0 + jnp.exp(-jnp.abs(z))) for z in zs]
            lbs = [lk + z for lk, z in zip(lks, zs)]
            if masked:
                lks = [jnp.where(diag, lk, 0.0) for lk in lks]
            ps = [lax.dot_general(_split_cat(lk), incl, _NN, preferred_element_type=F32) for lk in lks]
            ws, es = [], []
            for lk, lb, p, dw in zip(lks, lbs, ps, dws):
                w = jnp.exp(lb + (rtot - (pre + p)))
                if masked:
                    w = jnp.where(diag, w, 0.0)
                ws.append(w)
                es.append(dw * w)
                pre = pre + jnp.sum(lk, axis=1, keepdims=True)
            cs = [lax.dot_general(_split_cat(e), strict, _NN, preferred_element_type=F32) for e in es]
            for e, lb, c, w, kblk, s in zip(es, lbs, cs, ws, ks, sl):
                dz = e - jnp.exp(lb) * (e + (epre + c))
                if masked:
                    dz = jnp.where(diag, dz, 0.0)
                dz = dz.astype(MXU_DTYPE)
                dq = dq + lax.dot_general(dz, kblk, _NN, preferred_element_type=F32)
                dk_ref[s, :] += lax.dot_general(dz, q2, _TN, preferred_element_type=F32)
                dv_ref[s, :] += lax.dot_general(w.astype(MXU_DTYPE), do2, _TN, preferred_element_type=F32)
                epre = epre + jnp.sum(e, axis=1, keepdims=True)
            return dq, pre, epre

        zc = jnp.zeros((2 * qb, 1), F32)
        carry = lax.fori_loop(0, qi // 2, lambda i, c: blocks([2 * i, 2 * i + 1], c[0], c[1], c[2], False),
                              (jnp.zeros((2 * qb, 128), F32), zc, zc))
        carry = lax.cond(qi % 2 == 1, lambda c: blocks([qi - 1], c[0], c[1], c[2], False), lambda c: c, carry)
        dq = blocks([qi], carry[0], carry[1], carry[2], True)[0]
        dq_ref[...] = jnp.where(is_a, dq[:qb], dq[qb:]) * scale

    full = jax.ShapeDtypeStruct((bsz * seq, SB_WIDTH), F32)
    qspec = pl.BlockSpec((qb, 128), lambda b, p, i: (b * nq + i, p))
    return pl.pallas_call(
        body, name="sb_attn_bwd", grid=(bsz, npair, nq),
        in_specs=[qspec,
                  pl.BlockSpec((seq, 128), lambda b, p, i: (b, npair + p)),
                  pl.BlockSpec((seq, 128), lambda b, p, i: (b, 2 * npair + p)),
                  qspec, qspec],
        out_specs=[qspec, pl.BlockSpec((seq, 128), lambda b, p, i: (b, p)),
                   pl.BlockSpec((seq, 128), lambda b, p, i: (b, p))],
        out_shape=[full, full, full],
        compiler_params=_params(("parallel", "parallel", "arbitrary")),
    )(proj, proj, proj, rsum, dcat)


def _window_sums(x, forward):
    n = x.shape[0]
    out = []
    s = x
    for sh in (1, 2, 4, 8):
        s = s + pltpu.roll(s, (n - sh) if forward else sh, 0)
        out.append(s)
    return out


def _pool_counts(tc, c, w):
    t = lax.broadcasted_iota(jnp.int32, (tc, 1), 0) + c * tc
    return jnp.minimum(t + 1, w).astype(F32)


def pool_fwd(proj, pool_w, pool_scale, bsz, seq):
    tc = _tile(seq, 512)
    nc = seq // tc
    hb = tc // POOL_HALO
    ucol = 3

    def body(u_ref, prev_ref, w_ref, s_ref, o_ref):
        c = pl.program_id(1)
        prev = jnp.where(c > 0, prev_ref[...], 0.0)
        x = jnp.concatenate([prev, u_ref[...]], axis=0)
        sums = _window_sums(x, forward=False)
        for g, win in enumerate(POOL_WINDOWS):
            ls = slice(g * POOL_GROUP, (g + 1) * POOL_GROUP)
            pooled = sums[g][POOL_HALO:, ls] / _pool_counts(tc, c, win) - x[POOL_HALO:, ls]
            y = _dot(pooled, w_ref[g], _NN)
            o_ref[:, ls] = (y * s_ref[:, ls]).astype(o_ref.dtype)

    return pl.pallas_call(
        body, name="pool_fwd", grid=(bsz, nc),
        in_specs=[pl.BlockSpec((tc, SB_WIDTH), lambda b, c: (b * nc + c, ucol)),
                  pl.BlockSpec((POOL_HALO, SB_WIDTH), lambda b, c: (jnp.maximum((b * nc + c) * hb - 1, 0), ucol)),
                  pl.BlockSpec((4, POOL_GROUP, POOL_GROUP), lambda b, c: (0, 0, 0)),
                  pl.BlockSpec((1, SB_WIDTH), lambda b, c: (0, 0))],
        out_specs=pl.BlockSpec((tc, SB_WIDTH), lambda b, c: (b * nc + c, 0)),
        out_shape=jax.ShapeDtypeStruct((bsz * seq, SB_WIDTH), BF16),
        compiler_params=_params(("parallel", "parallel")),
    )(proj, proj, pool_w, pool_scale)


def pool_bwd(proj, pool_w, pool_scale, dcat, bsz, seq):
    tc = _tile(seq, 512)
    nc = seq // tc
    hb = tc // POOL_HALO
    nblk = bsz * seq // POOL_HALO
    ucol = 3

    def body(u_ref, prev_ref, dy_ref, nxt_ref, w_ref, s_ref, du_ref, dw_ref, ds_ref):
        b, c = pl.program_id(0), pl.program_id(1)

        @pl.when((b == 0) & (c == 0))
        def _():
            dw_ref[...] = jnp.zeros_like(dw_ref)
            ds_ref[...] = jnp.zeros_like(ds_ref)

        prev = jnp.where(c > 0, prev_ref[...], 0.0)
        x = jnp.concatenate([prev, u_ref[...]], axis=0)
        sums = _window_sums(x, forward=False)
        nxt = jnp.where(c < nc - 1, nxt_ref[...].astype(F32), 0.0)
        dy = jnp.concatenate([dy_ref[...].astype(F32), nxt], axis=0)
        tq = lax.broadcasted_iota(jnp.int32, (tc + POOL_HALO, 1), 0) + c * tc
        for g, win in enumerate(POOL_WINDOWS):
            ls = slice(g * POOL_GROUP, (g + 1) * POOL_GROUP)
            pooled = sums[g][POOL_HALO:, ls] / _pool_counts(tc, c, win) - x[POOL_HALO:, ls]
            y = _dot(pooled, w_ref[g], _NN)
            ds_ref[:, ls] += jnp.sum(dy[:tc, ls] * y, axis=0, keepdims=True)
            dz = dy[:, ls] * s_ref[:, ls]
            dw_ref[g] += _dot(pooled, dz[:tc], _TN)
            dpool = _dot(dz, w_ref[g], _NT)
            dmean = dpool / jnp.minimum(tq + 1, win).astype(F32)
            fsum = _window_sums(dmean, forward=True)[g]
            du_ref[:, ls] = fsum[:tc] - dpool[:tc]

    return pl.pallas_call(
        body, name="pool_bwd", grid=(bsz, nc),
        in_specs=[pl.BlockSpec((tc, SB_WIDTH), lambda b, c: (b * nc + c, ucol)),
                  pl.BlockSpec((POOL_HALO, SB_WIDTH), lambda b, c: (jnp.maximum((b * nc + c) * hb - 1, 0), ucol)),
                  pl.BlockSpec((tc, SB_WIDTH), lambda b, c: (b * nc + c, 1)),
                  pl.BlockSpec((POOL_HALO, SB_WIDTH), lambda b, c: (jnp.minimum((b * nc + c + 1) * hb, nblk - 1), 1)),
                  pl.BlockSpec((4, POOL_GROUP, POOL_GROUP), lambda b, c: (0, 0, 0)),
                  pl.BlockSpec((1, SB_WIDTH), lambda b, c: (0, 0))],
        out_specs=[pl.BlockSpec((tc, SB_WIDTH), lambda b, c: (b * nc + c, 0)),
                   pl.BlockSpec((4, POOL_GROUP, POOL_GROUP), lambda b, c: (0, 0, 0)),
                   pl.BlockSpec((1, SB_WIDTH), lambda b, c: (0, 0))],
        out_shape=[jax.ShapeDtypeStruct((bsz * seq, SB_WIDTH), F32),
                   jax.ShapeDtypeStruct((4, POOL_GROUP, POOL_GROUP), F32),
                   jax.ShapeDtypeStruct((1, SB_WIDTH), F32)],
        compiler_params=_params(("arbitrary", "arbitrary")),
    )(proj, proj, dcat, dcat, pool_w, pool_scale)


def _lbar(lam_re, lam_im, log_dt):
    dt = jnp.exp(log_dt)
    mag = jnp.exp(lam_re * dt)
    ang = lam_im * dt
    return mag * jnp.cos(ang), mag * jnp.sin(ang)


def _bbar(lam_re, lam_im, log_dt, b_re, b_im):
    lb_re, lb_im = _lbar(lam_re, lam_im, log_dt)
    n_re = lb_re - 1.0
    den = lam_re * lam_re + lam_im * lam_im
    coef_re = (n_re * lam_re + lb_im * lam_im) / den
    coef_im = (lb_im * lam_re - n_re * lam_im) / den
    return coef_re * b_re - coef_im * b_im, coef_re * b_im + coef_im * b_re


def _expand01():
    p = lax.broadcasted_iota(jnp.int32, (64, 1024), 0)
    q = lax.broadcasted_iota(jnp.int32, (64, 1024), 1)
    return (lax.shift_right_logical(q, 4) == p).astype(BF16)


def ssm_prep(lam_re, lam_im, log_dt, b_re2, b_im2):
    def body(lr_ref, li_ref, dt_ref, br_ref, bi_ref, ar_ref, ai_ref, bbr_ref, bbi_ref):
        e = _expand01()
        lr, li, dt = lr_ref[...], li_ref[...], dt_ref[...]
        ar_ref[...], ai_ref[...] = _lbar(lr, li, dt)
        bbr_ref[...], bbi_ref[...] = _bbar(_dot_exact01(lr, e), _dot_exact01(li, e), dt, br_ref[...], bi_ref[...])

    s64 = jax.ShapeDtypeStruct((64, 64), F32)
    s1k = jax.ShapeDtypeStruct((64, 1024), F32)
    return pl.pallas_call(body, name="ssm_prep", out_shape=[s64, s64, s1k, s1k], compiler_params=_params())(
        lam_re, lam_im, log_dt, b_re2, b_im2)


def ssm_prep_bwd(lam_re, lam_im, log_dt, b_re2, b_im2, da_re, da_im, dbb_re, dbb_im):
    def body(lr_ref, li_ref, dt_ref, br_ref, bi_ref, dar_ref, dai_ref, dbr_ref, dbi_ref,
             olr_ref, oli_ref, odt_ref, obr_ref, obi_ref):
        e = _expand01()
        lr, li, dt = lr_ref[...], li_ref[...], dt_ref[...]
        _, vjp_a = jax.vjp(_lbar, lr, li, dt)
        g_lr, g_li, g_dt = vjp_a((dar_ref[...], dai_ref[...]))
        _, vjp_b = jax.vjp(_bbar, _dot_exact01(lr, e), _dot_exact01(li, e), dt, br_ref[...], bi_ref[...])
        x_lr, x_li, x_dt, g_br, g_bi = vjp_b((dbr_ref[...], dbi_ref[...]))
        olr_ref[...] = g_lr + _dot_exact01(x_lr, e, _NT)
        oli_ref[...] = g_li + _dot_exact01(x_li, e, _NT)
        odt_ref[...] = g_dt + x_dt
        obr_ref[...] = g_br
        obi_ref[...] = g_bi

    s64 = jax.ShapeDtypeStruct((64, 64), F32)
    s1k = jax.ShapeDtypeStruct((64, 1024), F32)
    return pl.pallas_call(body, name="ssm_prep_bwd",
                          out_shape=[s64, s64, jax.ShapeDtypeStruct((64, 1), F32), s1k, s1k],
                          compiler_params=_params())(
        lam_re, lam_im, log_dt, b_re2, b_im2, da_re, da_im, dbb_re, dbb_im)


def _gelu(y):
    c = math.sqrt(2.0 / math.pi)
    return 0.5 * y * (1.0 + jnp.tanh(c * (y + 0.044715 * y * y * y)))


def _gelu_grad(y):
    c = math.sqrt(2.0 / math.pi)
    th = jnp.tanh(c * (y + 0.044715 * y * y * y))
    return 0.5 * (1.0 + th) + 0.5 * y * (1.0 - th * th) * c * (1.0 + 3.0 * 0.044715 * y * y)


def _cmul(ar, ai, br, bi):
    return ar * br - ai * bi, ar * bi + ai * br


def _scan_tables(ar, ai, reverse, tabs):
    row = lax.broadcasted_iota(jnp.int32, (8, SSM_STATES), 0)
    a1 = (ar, ai)
    a2 = _cmul(*a1, *a1)
    a4 = _cmul(*a2, *a2)
    powers = [a1, a2, _cmul(*a2, *a1), a4]
    powers += [_cmul(*a4, *p) for p in powers]
    for k, (val, sh) in enumerate(((a1, 1), (a2, 2), (a4, 4))):
        keep = (row < 8 - sh) if reverse else (row >= sh)
        tabs[2 * k][...] = jnp.where(keep, val[0], 0.0)
        tabs[2 * k + 1][...] = jnp.where(keep, val[1], 0.0)
    pr = jnp.zeros((8, SSM_STATES), F32)
    pi = jnp.zeros((8, SSM_STATES), F32)
    for r in range(8):
        val = powers[7 - r] if reverse else powers[r]
        pr = jnp.where(row == r, val[0], pr)
        pi = jnp.where(row == r, val[1], pi)
    tabs[6][...] = pr
    tabs[7][...] = pi


def _scan8(xr, xi, tabs, ls, cr, ci, reverse):
    for k, sh in enumerate((1, 2, 4)):
        amt = (8 - sh) if reverse else sh
        sr, si = pltpu.roll(xr, amt, 0), pltpu.roll(xi, amt, 0)
        lr, li = tabs[2 * k][:, ls], tabs[2 * k + 1][:, ls]
        xr, xi = xr + lr * sr - li * si, xi + lr * si + li * sr
    pr, pi = tabs[6][:, ls], tabs[7][:, ls]
    return xr + pr * cr - pi * ci, xi + pr * ci + pi * cr


def _block8(b):
    return pl.ds(pl.multiple_of(b * 8, 8), 8)


def ssm_fwd(u, wt, ct, a_re, a_im, dskip, bsz, seq):
    tc = _tile(seq, 256)
    nc = seq // tc
    ns = SSM_TILE_STATES
    nl = SSM_STATES // SSM_LANES

    def body(u_ref, wt_ref, ct_ref, ar_ref, ai_ref, d_ref, y_ref, gl_ref, hr_ref, hi_ref, sr_ref, si_ref, *tabs):
        b, c = pl.program_id(0), pl.program_id(1)

        @pl.when((b == 0) & (c == 0))
        def _():
            _scan_tables(ar_ref[...], ai_ref[...], False, tabs)

        @pl.when(c == 0)
        def _():
            sr_ref[...] = jnp.zeros_like(sr_ref)
            si_ref[...] = jnp.zeros_like(si_ref)

        uf = u_ref[...]
        for i in range(SSM_TILES):
            bu = _dot(uf[:, i * 128:(i + 1) * 128], wt_ref[i], _NN)
            hr_ref[:, i * ns:(i + 1) * ns] = bu[:, :ns]
            hi_ref[:, i * ns:(i + 1) * ns] = bu[:, ns:]

        def step(blk, carry):
            rows = _block8(blk)
            new = []
            for j in range(nl):
                ls = slice(j * SSM_LANES, (j + 1) * SSM_LANES)
                xr, xi = _scan8(hr_ref[rows, ls], hi_ref[rows, ls], tabs, ls, carry[2 * j], carry[2 * j + 1], False)
                hr_ref[rows, ls] = xr
                hi_ref[rows, ls] = xi
                new += [xr[7:8], xi[7:8]]
            return tuple(new)

        init = []
        for j in range(nl):
            ls = slice(j * SSM_LANES, (j + 1) * SSM_LANES)
            init += [sr_ref[:, ls], si_ref[:, ls]]
        last = lax.fori_loop(0, tc // 8, step, tuple(init), unroll=2)
        for j in range(nl):
            ls = slice(j * SSM_LANES, (j + 1) * SSM_LANES)
            sr_ref[:, ls] = last[2 * j]
            si_ref[:, ls] = last[2 * j + 1]
        for i in range(SSM_TILES):
            hcat = jnp.concatenate([hr_ref[:, i * ns:(i + 1) * ns], hi_ref[:, i * ns:(i + 1) * ns]], axis=1)
            ls = slice(i * 128, (i + 1) * 128)
            y = _dot(hcat, ct_ref[i], _NN) + d_ref[:, ls] * uf[:, ls]
            y_ref[:, ls] = y
            gl_ref[:, ls] = _gelu(y).astype(gl_ref.dtype)

    t = bsz * seq
    row = pl.BlockSpec((tc, D_MODEL), lambda b, c: (b * nc + c, 0))
    st = pl.BlockSpec((tc, SSM_STATES), lambda b, c: (b * nc + c, 0))
    diag = pl.BlockSpec((1, SSM_STATES), lambda b, c: (0, 0))
    return pl.pallas_call(
        body, name="ssm_fwd", grid=(bsz, nc),
        in_specs=[row, pl.BlockSpec((SSM_TILES, 128, 2 * ns), lambda b, c: (0, 0, 0)),
                  pl.BlockSpec((SSM_TILES, 2 * ns, 128), lambda b, c: (0, 0, 0)), diag, diag,
                  pl.BlockSpec((1, D_MODEL), lambda b, c: (0, 0))],
        out_specs=[row, row, st, st],
        out_shape=[jax.ShapeDtypeStruct((t, D_MODEL), F32), jax.ShapeDtypeStruct((t, D_MODEL), BF16),
                   jax.ShapeDtypeStruct((t, SSM_STATES), F32), jax.ShapeDtypeStruct((t, SSM_STATES), F32)],
        scratch_shapes=[pltpu.VMEM((1, SSM_STATES), F32)] * 2 + [pltpu.VMEM((8, SSM_STATES), F32)] * 8,
        compiler_params=_params(("arbitrary", "arbitrary")),
    )(u, wt, ct, a_re, a_im, dskip)


def ssm_bwd(dgl, y, u, h_re, h_im, wt, ct, a_re, a_im, dskip, bsz, seq):
    tc = _tile(seq, 256)
    nc = seq // tc
    nb = tc // 8
    ns = SSM_TILE_STATES
    nl = SSM_STATES // SSM_LANES

    def body(dgl_ref, y_ref, u_ref, hr_ref, hi_ref, pr_ref, pi_ref, wt_ref, ct_ref, ar_ref, ai_ref, d_ref,
             du_ref, dwt_ref, dct_ref, dd_ref, dar_ref, dai_ref, gr_ref, gi_ref, sr_ref, si_ref, ar8_ref, ai8_ref,
             *tabs):
        b, c = pl.program_id(0), pl.program_id(1)

        @pl.when((b == 0) & (c == 0))
        def _():
            for r in (dwt_ref, dct_ref, dd_ref, ar8_ref, ai8_ref):
                r[...] = jnp.zeros_like(r)
            _scan_tables(ar_ref[...], -ai_ref[...], True, tabs)

        @pl.when(c == 0)
        def _():
            sr_ref[...] = jnp.zeros_like(sr_ref)
            si_ref[...] = jnp.zeros_like(si_ref)

        uf = u_ref[...]
        dy = dgl_ref[...].astype(F32) * _gelu_grad(y_ref[...])
        dd_ref[...] += jnp.sum(dy * uf, axis=0, keepdims=True)
        for i in range(SSM_TILES):
            dyi = dy[:, i * 128:(i + 1) * 128]
            dh = _dot(dyi, ct_ref[i], _NT)
            gr_ref[:, i * ns:(i + 1) * ns] = dh[:, :ns]
            gi_ref[:, i * ns:(i + 1) * ns] = dh[:, ns:]
            hcat = jnp.concatenate([hr_ref[:, i * ns:(i + 1) * ns], hi_ref[:, i * ns:(i + 1) * ns]], axis=1)
            dct_ref[i] += _dot(hcat, dyi, _TN)
        row0 = lax.broadcasted_iota(jnp.int32, (8, SSM_LANES), 0) == 0

        def block(blk, carry, before):
            rows = _block8(blk)
            new = []
            for j in range(nl):
                ls = slice(j * SSM_LANES, (j + 1) * SSM_LANES)
                gr, gi = _scan8(gr_ref[rows, ls], gi_ref[rows, ls], tabs, ls, carry[2 * j], carry[2 * j + 1], True)
                gr_ref[rows, ls] = gr
                gi_ref[rows, ls] = gi
                bpr, bpi = before(j)
                hpr = jnp.where(row0, bpr, pltpu.roll(hr_ref[rows, ls], 1, 0))
                hpi = jnp.where(row0, bpi, pltpu.roll(hi_ref[rows, ls], 1, 0))
                ar8_ref[:, ls] += gr * hpr + gi * hpi
                ai8_ref[:, ls] += gi * hpr - gr * hpi
                new += [gr[0:1], gi[0:1]]
            return tuple(new)

        def step(jj, carry):
            blk = nb - 1 - jj
            prev_rows = _block8(blk - 1)

            def before(j):
                ls = slice(j * SSM_LANES, (j + 1) * SSM_LANES)
                return hr_ref[prev_rows, ls][7:8], hi_ref[prev_rows, ls][7:8]

            return block(blk, carry, before)

        init = []
        for j in range(nl):
            ls = slice(j * SSM_LANES, (j + 1) * SSM_LANES)
            init += [sr_ref[:, ls], si_ref[:, ls]]
        carry = lax.fori_loop(0, nb - 1, step, tuple(init))
        first = c == nc - 1

        def before_chunk(j):
            ls = slice(j * SSM_LANES, (j + 1) * SSM_LANES)
            return (jnp.where(first, 0.0, pr_ref[:, ls][7:8]), jnp.where(first, 0.0, pi_ref[:, ls][7:8]))

        last = block(0, carry, before_chunk)
        for j in range(nl):
            ls = slice(j * SSM_LANES, (j + 1) * SSM_LANES)
            sr_ref[:, ls] = last[2 * j]
            si_ref[:, ls] = last[2 * j + 1]
        for i in range(SSM_TILES):
            ls = slice(i * 128, (i + 1) * 128)
            gcat = jnp.concatenate([gr_ref[:, i * ns:(i + 1) * ns], gi_ref[:, i * ns:(i + 1) * ns]], axis=1)
            du_ref[:, ls] = (_dot(gcat, wt_ref[i], _NT) + d_ref[:, ls] * dy[:, ls]).astype(du_ref.dtype)
            dwt_ref[i] += _dot(uf[:, ls], gcat, _TN)

        @pl.when((b == bsz - 1) & (c == nc - 1))
        def _():
            dar_ref[...] = jnp.sum(ar8_ref[...], axis=0, keepdims=True)
            dai_ref[...] = jnp.sum(ai8_ref[...], axis=0, keepdims=True)

    t = bsz * seq
    rev = lambda b, c: (b * nc + (nc - 1 - c), 0)
    row = pl.BlockSpec((tc, D_MODEL), rev)
    st = pl.BlockSpec((tc, SSM_STATES), rev)
    prev = pl.BlockSpec((8, SSM_STATES), lambda b, c: (jnp.maximum((b * nc + (nc - 1 - c)) * nb - 1, 0), 0))
    diag = pl.BlockSpec((1, SSM_STATES), lambda b, c: (0, 0))
    wts = pl.BlockSpec((SSM_TILES, 128, 2 * ns), lambda b, c: (0, 0, 0))
    cts = pl.BlockSpec((SSM_TILES, 2 * ns, 128), lambda b, c: (0, 0, 0))
    vec = pl.BlockSpec((1, D_MODEL), lambda b, c: (0, 0))
    return pl.pallas_call(
        body, name="ssm_bwd", grid=(bsz, nc),
        in_specs=[row, row, row, st, st, prev, prev, wts, cts, diag, diag, vec],
        out_specs=[row, wts, cts, vec, diag, diag],
        out_shape=[jax.ShapeDtypeStruct((t, D_MODEL), BF16),
                   jax.ShapeDtypeStruct((SSM_TILES, 128, 2 * ns), F32),
                   jax.ShapeDtypeStruct((SSM_TILES, 2 * ns, 128), F32),
                   jax.ShapeDtypeStruct((1, D_MODEL), F32),
                   jax.ShapeDtypeStruct((1, SSM_STATES), F32), jax.ShapeDtypeStruct((1, SSM_STATES), F32)],
        scratch_shapes=[pltpu.VMEM((tc, SSM_STATES), F32)] * 2 + [pltpu.VMEM((1, SSM_STATES), F32)] * 2
                       + [pltpu.VMEM((8, SSM_STATES), F32)] * 10,
        compiler_params=_params(("arbitrary", "arbitrary")),
    )(dgl, y, u, h_re, h_im, h_re, h_im, wt, ct, a_re, a_im, dskip)


def _ssm_in_weights(bb_re2, bb_im2):
    eye = jnp.eye(8, dtype=F32)[None, :, None, :, None]

    def one(bb):
        t = bb.reshape(8, 8, 64, 16).transpose(0, 1, 3, 2)
        return (t[:, :, :, None, :] * eye).reshape(8, 128, 512)

    return jnp.concatenate([one(bb_re2), one(bb_im2)], axis=-1).astype(MXU_DTYPE)


def _ssm_in_weights_bwd(dwt):
    eye = jnp.eye(8, dtype=F32)[None, :, None, :, None]

    def one(d):
        t = (d.reshape(8, 8, 16, 8, 64) * eye).sum(axis=3)
        return t.transpose(0, 1, 3, 2).reshape(64, 1024)

    return one(dwt[..., :512]), one(dwt[..., 512:])


def _ssm_out_weights(c_re, c_im):
    eye = jnp.eye(8, dtype=F32)[None, :, None, :, None]

    def one(cc):
        t = cc.reshape(8, 8, 16, 64).transpose(0, 1, 3, 2)
        return (t[:, :, :, None, :] * eye).reshape(8, 512, 128)

    return jnp.concatenate([one(c_re), -one(c_im)], axis=1).astype(MXU_DTYPE)


def _ssm_out_weights_bwd(dct):
    eye = jnp.eye(8, dtype=F32)[None, :, None, :, None]

    def one(d):
        t = (d.reshape(8, 8, 64, 8, 16) * eye).sum(axis=3)
        return t.transpose(0, 1, 3, 2).reshape(64, 16, 64)

    return one(dct[:, :512]), -one(dct[:, 512:])


def _softmax(s):
    m = jnp.max(s, axis=-1, keepdims=True)
    e = jnp.exp(s - m)
    return e / jnp.sum(e, axis=-1, keepdims=True)


def xattn_fwd(q, kv, bsz, seq):
    tq = _tile(seq, 512)
    nq = seq // tq
    scale = XA_HEAD_DIM ** -0.5

    def body(q_ref, k_ref, v_ref, o_ref):
        s = lax.dot_general(q_ref[...], k_ref[...], _NT, preferred_element_type=F32) * scale
        p = _softmax(s)
        o_ref[...] = _dot(p, v_ref[...], _NN).astype(o_ref.dtype)

    qs = pl.BlockSpec((tq, XA_HEAD_DIM), lambda b, h, i: (b * nq + i, h))
    return pl.pallas_call(
        body, name="xattn_fwd", grid=(bsz, XA_HEADS, nq),
        in_specs=[qs, pl.BlockSpec((MEM_LEN, XA_HEAD_DIM), lambda b, h, i: (b, h)),
                  pl.BlockSpec((MEM_LEN, XA_HEAD_DIM), lambda b, h, i: (b, XA_HEADS + h))],
        out_specs=qs, out_shape=jax.ShapeDtypeStruct((bsz * seq, D_MODEL), BF16),
        compiler_params=_params(("parallel", "parallel", "parallel")),
    )(q, kv, kv)


def xattn_bwd(q, kv, do, bsz, seq):
    tq = _tile(seq, 512)
    nq = seq // tq
    scale = XA_HEAD_DIM ** -0.5

    def body(q_ref, k_ref, v_ref, do_ref, dq_ref, dk_ref, dv_ref):
        @pl.when(pl.program_id(2) == 0)
        def _():
            dk_ref[...] = jnp.zeros_like(dk_ref)
            dv_ref[...] = jnp.zeros_like(dv_ref)

        qv, kk, vv, dov = q_ref[...], k_ref[...], v_ref[...], do_ref[...]
        s = lax.dot_general(qv, kk, _NT, preferred_element_type=F32) * scale
        p = _softmax(s)
        dp = lax.dot_general(dov, vv, _NT, preferred_element_type=F32)
        ds = (p * (dp - jnp.sum(dp * p, axis=-1, keepdims=True)) * scale).astype(MXU_DTYPE)
        dq_ref[...] = lax.dot_general(ds, kk, _NN, preferred_element_type=F32).astype(dq_ref.dtype)
        dk_ref[...] += lax.dot_general(ds, qv, _TN, preferred_element_type=F32)
        dv_ref[...] += lax.dot_general(p.astype(MXU_DTYPE), dov, _TN, preferred_element_type=F32)

    qs = pl.BlockSpec((tq, XA_HEAD_DIM), lambda b, h, i: (b * nq + i, h))
    ks = pl.BlockSpec((MEM_LEN, XA_HEAD_DIM), lambda b, h, i: (b, h))
    vs = pl.BlockSpec((MEM_LEN, XA_HEAD_DIM), lambda b, h, i: (b, XA_HEADS + h))
    dkv = jax.ShapeDtypeStruct((bsz * MEM_LEN, D_MODEL), F32)
    dq, dk, dv = pl.pallas_call(
        body, name="xattn_bwd", grid=(bsz, XA_HEADS, nq),
        in_specs=[qs, ks, vs, qs], out_specs=[qs, ks, ks],
        out_shape=[jax.ShapeDtypeStruct((bsz * seq, D_MODEL), BF16), dkv, dkv],
        compiler_params=_params(("parallel", "parallel", "arbitrary")),
    )(q, kv, kv, do)
    return dq, dk, dv


CONV_HALO = 16


def _shifts_down(x, prev):
    h = prev.shape[0]
    ext = jnp.concatenate([prev, x], axis=0)
    return pltpu.roll(ext, 1, 0)[h:], pltpu.roll(ext, 2, 0)[h:]


def _shifts_up(x, nxt):
    rows = x.shape[0]
    n = rows + nxt.shape[0]
    ext = jnp.concatenate([x, nxt], axis=0)
    return pltpu.roll(ext, n - 1, 0)[:rows], pltpu.roll(ext, n - 2, 0)[:rows]


def _conv_taps(u, u1, u2, w, b):
    return b + w[2:3] * u + w[1:2] * u1 + w[0:1] * u2


def conv_fwd(up, cw, cb, bsz, seq):
    tc = _tile(seq, 512)
    nc = seq // tc
    hb = tc // CONV_HALO
    half = N_DEV // 2

    def body(uv_ref, ug_ref, pv_ref, pg_ref, wv_ref, wg_ref, bv_ref, bg_ref, o_ref):
        c = pl.program_id(2)
        pv = jnp.where(c > 0, pv_ref[...].astype(F32), 0.0)
        pg = jnp.where(c > 0, pg_ref[...].astype(F32), 0.0)
        uv, ug = uv_ref[...].astype(F32), ug_ref[...].astype(F32)
        val = _conv_taps(uv, *_shifts_down(uv, pv), wv_ref[...], bv_ref[...])
        gate = _conv_taps(ug, *_shifts_down(ug, pg), wg_ref[...], bg_ref[...])
        o_ref[...] = (gate * jax.nn.sigmoid(gate) * val).astype(o_ref.dtype)

    def cur(off):
        return pl.BlockSpec((None, tc, FF_SHARD), lambda b, j, c: (j + off, b * nc + c, 0))

    def prv(off):
        return pl.BlockSpec((None, CONV_HALO, FF_SHARD), lambda b, j, c: (j + off, jnp.maximum((b * nc + c) * hb - 1, 0), 0))

    def par(rows, off):
        return pl.BlockSpec((None, rows, FF_SHARD), lambda b, j, c: (j + off, 0, 0))

    return pl.pallas_call(
        body, name="conv_fwd", grid=(bsz, half, nc),
        in_specs=[cur(0), cur(half), prv(0), prv(half), par(3, 0), par(3, half), par(1, 0), par(1, half)],
        out_specs=cur(0), out_shape=jax.ShapeDtypeStruct((half, bsz * seq, FF_SHARD), BF16),
        compiler_params=_params(("parallel", "parallel", "parallel")),
    )(up, up, up, up, cw, cw, cb, cb)


def conv_bwd_taps(up, cw, cb, dact, bsz, seq):
    tc = _tile(seq, 512)
    nc = seq // tc
    hb = tc // CONV_HALO
    half = N_DEV // 2

    def body(uv_ref, ug_ref, pv_ref, pg_ref, wv_ref, wg_ref, bv_ref, bg_ref, da_ref,
             dc_ref, dwv_ref, dwg_ref, dbv_ref, dbg_ref):
        b, c = pl.program_id(1), pl.program_id(2)

        @pl.when((b == 0) & (c == 0))
        def _():
            for r in (dwv_ref, dwg_ref, dbv_ref, dbg_ref):
                r[...] = jnp.zeros_like(r)

        pv = jnp.where(c > 0, pv_ref[...].astype(F32), 0.0)
        pg = jnp.where(c > 0, pg_ref[...].astype(F32), 0.0)
        uv, ug = uv_ref[...].astype(F32), ug_ref[...].astype(F32)
        uv1, uv2 = _shifts_down(uv, pv)
        ug1, ug2 = _shifts_down(ug, pg)
        val = _conv_taps(uv, uv1, uv2, wv_ref[...], bv_ref[...])
        gate = _conv_taps(ug, ug1, ug2, wg_ref[...], bg_ref[...])
        sg = jax.nn.sigmoid(gate)
        da = da_ref[...].astype(F32)
        dsilu = da * sg
        dval = dsilu * gate
        dgate = dsilu * val * (1.0 + gate * (1.0 - sg))
        dc_ref[0] = dval.astype(dc_ref.dtype)
        dc_ref[1] = dgate.astype(dc_ref.dtype)
        for dcv, taps, dw_ref, db_ref in ((dval, (uv2, uv1, uv), dwv_ref, dbv_ref),
                                          (dgate, (ug2, ug1, ug), dwg_ref, dbg_ref)):
            db_ref[...] += jnp.sum(dcv, axis=0, keepdims=True)
            for k, u_k in enumerate(taps):
                dw_ref[k:k + 1, :] += jnp.sum(dcv * u_k, axis=0, keepdims=True)

    def cur(off):
        return pl.BlockSpec((None, tc, FF_SHARD), lambda j, b, c: (j + off, b * nc + c, 0))

    def prv(off):
        return pl.BlockSpec((None, CONV_HALO, FF_SHARD), lambda j, b, c: (j + off, jnp.maximum((b * nc + c) * hb - 1, 0), 0))

    def par(rows, off):
        return pl.BlockSpec((None, rows, FF_SHARD), lambda j, b, c: (j + off, 0, 0))

    t = bsz * seq
    hs = jax.ShapeDtypeStruct((2, half, t, FF_SHARD), BF16)
    ws = jax.ShapeDtypeStruct((half, 3, FF_SHARD), F32)
    bs = jax.ShapeDtypeStruct((half, 1, FF_SHARD), F32)
    dc, dwv, dwg, dbv, dbg = pl.pallas_call(
        body, name="conv_bwd_taps", grid=(half, bsz, nc),
        in_specs=[cur(0), cur(half), prv(0), prv(half), par(3, 0), par(3, half), par(1, 0), par(1, half), cur(0)],
        out_specs=[pl.BlockSpec((2, None, tc, FF_SHARD), lambda j, b, c: (0, j, b * nc + c, 0)),
                   par(3, 0), par(3, 0), par(1, 0), par(1, 0)],
        out_shape=[hs, ws, ws, bs, bs],
        compiler_params=_params(("parallel", "arbitrary", "arbitrary")),
    )(up, up, up, up, cw, cw, cb, cb, dact)
    return (dc.reshape(N_DEV, t, FF_SHARD), jnp.concatenate([dwv, dwg], axis=0),
            jnp.concatenate([dbv, dbg], axis=0))


def conv_bwd_input(dconv, cw, bsz, seq):
    tc = _tile(seq, 1024)
    nc = seq // tc
    hb = tc // CONV_HALO
    nblk = bsz * seq // CONV_HALO

    def body(d_ref, n_ref, w_ref, o_ref):
        c = pl.program_id(2)
        nxt = jnp.where(c < nc - 1, n_ref[...].astype(F32), 0.0)
        d = d_ref[...].astype(F32)
        d1, d2 = _shifts_up(d, nxt)
        w = w_ref[...]
        o_ref[...] = (w[2:3] * d + w[1:2] * d1 + w[0:1] * d2).astype(o_ref.dtype)

    cur = pl.BlockSpec((None, tc, FF_SHARD), lambda j, b, c: (j, b * nc + c, 0))
    return pl.pallas_call(
        body, name="conv_bwd_input", grid=(N_DEV, bsz, nc),
        in_specs=[cur, pl.BlockSpec((None, CONV_HALO, FF_SHARD),
                                    lambda j, b, c: (j, jnp.minimum((b * nc + c + 1) * hb, nblk - 1), 0)),
                  pl.BlockSpec((None, 3, FF_SHARD), lambda j, b, c: (j, 0, 0))],
        out_specs=cur, out_shape=jax.ShapeDtypeStruct(dconv.shape, BF16),
        compiler_params=_params(("parallel", "parallel", "parallel")),
    )(dconv, dconv, cw)


def _my_index():
    return 4 * lax.axis_index("x") + 2 * lax.axis_index("y") + lax.axis_index("c")


def _peer(k):
    return (lax.axis_index("x") ^ ((k >> 2) & 1), lax.axis_index("y") ^ ((k >> 1) & 1),
            lax.axis_index("c") ^ (k & 1))


_HBM = pl.BlockSpec(memory_space=pltpu.HBM)
_SEM = pl.BlockSpec(memory_space=pltpu.SEMAPHORE)
_DATAFLOW = pltpu.SideEffectType.DATAFLOW_SIDE_EFFECTING


def _split_copies(gather, src_ref, land_ref, send_sems, recv_sems, local_sem):
    me = _my_index()

    def part(j):
        return src_ref if gather else src_ref.at[j]

    local = pltpu.make_async_copy(part(me), land_ref.at[me], local_sem)
    sends = [pltpu.make_async_remote_copy(
        src_ref=part(me ^ k), dst_ref=land_ref.at[me], send_sem=send_sems.at[k - 1], recv_sem=recv_sems.at[k - 1],
        device_id=_peer(k), device_id_type=pl.DeviceIdType.MESH) for k in range(1, N_DEV)]
    recvs = [pltpu.make_async_remote_copy(
        src_ref=part(me ^ k), dst_ref=land_ref.at[me ^ k], send_sem=send_sems.at[k - 1], recv_sem=recv_sems.at[k - 1],
        device_id=_peer(k), device_id_type=pl.DeviceIdType.MESH) for k in range(1, N_DEV)]
    return local, sends, recvs


def split_start(name, src, gather):
    land_shape = ((N_DEV,) + src.shape) if gather else src.shape

    def body(src_ref, land_ref, send_sems, recv_sems, local_sem, src_thru, land_thru, token):
        local, sends, _ = _split_copies(gather, src_ref, land_ref, send_sems, recv_sems, local_sem)
        local.start()
        for cp in sends:
            cp.start()
        token[...] = jnp.zeros_like(token)

    dma7 = pltpu.SemaphoreType.DMA((N_DEV - 1,))
    out = pl.pallas_call(
        body, name=name,
        out_shape=(dma7, dma7, pltpu.SemaphoreType.DMA(()), pltpu.HBM(src.shape, src.dtype),
                   pltpu.HBM(land_shape, src.dtype), jax.ShapeDtypeStruct((8, 128), F32)),
        in_specs=(_HBM, _HBM), out_specs=(_SEM, _SEM, _SEM, _HBM, _HBM, pl.BlockSpec(memory_space=pltpu.VMEM)),
        input_output_aliases={0: 3, 1: 4},
        compiler_params=pltpu.CompilerParams(has_side_effects=_DATAFLOW),
    )(pltpu.with_memory_space_constraint(src, pltpu.HBM),
      pltpu.with_memory_space_constraint(lax.empty(land_shape, src.dtype), pltpu.HBM))
    return out[:5], out[5][0, 0]


def split_wait(name, handles, after, gather):
    send_sems, recv_sems, local_sem, src_thru, land_thru = handles

    def body(src_ref, land_ref, send_sems, recv_sems, local_sem, after_ref, src_dead, got_ref, token):
        local, sends, recvs = _split_copies(gather, src_ref, land_ref, send_sems, recv_sems, local_sem)
        local.wait()
        for cp in recvs:
            cp.wait_send()
            cp.wait_recv()
        token[...] = jnp.zeros_like(token)

    out = pl.pallas_call(
        body, name=name,
        out_shape=(pltpu.HBM(src_thru.shape, src_thru.dtype), pltpu.HBM(land_thru.shape, land_thru.dtype),
                   jax.ShapeDtypeStruct((8, 128), F32)),
        in_specs=(_HBM, _HBM, _SEM, _SEM, _SEM, pl.BlockSpec(memory_space=pl.ANY)),
        out_specs=(_HBM, _HBM, pl.BlockSpec(memory_space=pltpu.VMEM)),
        input_output_aliases={0: 0, 1: 1},
        compiler_params=pltpu.CompilerParams(has_side_effects=_DATAFLOW),
    )(src_thru, land_thru, send_sems, recv_sems, local_sem, after)
    return out[1], out[2][0, 0]


def sum_parts(name, r):
    _, rows, cols = r.shape

    def body(r_ref, o_ref):
        acc = r_ref[0].astype(F32)
        for s in range(1, N_DEV):
            acc = acc + r_ref[s].astype(F32)
        o_ref[...] = acc

    return pl.pallas_call(body, name=name, out_shape=jax.ShapeDtypeStruct((rows, cols), F32),
                          compiler_params=_params())(r)


def adamw(name, w, m, v, parts=None, g=None, layer=0, into=None, order=None):
    _, rows, cols = w.shape
    br = _tile(rows, 256, 16)
    c1 = 1.0 / (1.0 - ADAM_B1 ** ADAM_STEP)
    c2 = 1.0 / (1.0 - ADAM_B2 ** ADAM_STEP)

    def body(g_ref, w_ref, m_ref, v_ref, *rest):
        og_ref, od_ref, om_ref, ov_ref = rest[-4:]
        if parts is None:
            gs = g_ref[...]
        else:
            gs = g_ref[0].astype(F32)
            for s in range(1, N_DEV):
                gs = gs + g_ref[s].astype(F32)
        mn = ADAM_B1 * m_ref[...] + (1.0 - ADAM_B1) * gs
        vn = ADAM_B2 * v_ref[...] + (1.0 - ADAM_B2) * (gs * gs)
        og_ref[...] = gs
        om_ref[...] = mn
        ov_ref[...] = vn
        od_ref[...] = -ADAM_LR * ((mn * c1) / (jnp.sqrt(vn * c2) + ADAM_EPS) + ADAM_WD * w_ref[...])

    blk = pl.BlockSpec((None, br, cols), lambda i: (layer, i, 0))
    if parts is None:
        gspec = pl.BlockSpec((br, cols), lambda i: (i, 0))
    else:
        gspec = pl.BlockSpec((N_DEV, br, cols), lambda i: (0, i, 0))
    earlier = [] if into is None else list(into)
    behind = [] if order is None else [order]
    return pl.pallas_call(
        body, name=name, grid=(rows // br,),
        in_specs=[gspec, blk, blk, blk] + [pl.BlockSpec(memory_space=pl.ANY)] * len(earlier)
                 + [pl.BlockSpec((1, 128), lambda i: (0, 0))] * len(behind),
        out_specs=[blk] * 4, out_shape=[jax.ShapeDtypeStruct(w.shape, F32)] * 4,
        input_output_aliases={4 + k: k for k in range(len(earlier))},
        compiler_params=_params(("parallel",)),
    )(g if parts is None else parts, w, m, v, *earlier, *behind)


SMALL = ("norm_mix", "norm_xattn", "norm_ffn", "norm_mem", "norm_final", "pool_w", "pool_scale",
         "ssm_lam_re", "ssm_lam_im", "ssm_log_dt", "ssm_b_re", "ssm_b_im", "ssm_c_re", "ssm_c_im",
         "ffn_conv_b", "ssm_d", "ffn_conv_w")
SMALL_SHARDED = {"ssm_d": 1, "ffn_conv_w": 2}
BIG = ("ab_w_in", "ab_w_out", "ssm_w_in", "ssm_w_glu", "xa_w_q", "xa_w_kv", "xa_w_o", "ffn_w_up", "ffn_w_down")
WEIGHTS = ("norm_mix", "norm_xattn", "norm_ffn", "norm_mem", "norm_final", "ab_w_in", "pool_w", "pool_scale",
           "ab_w_out", "ssm_w_in", "ssm_lam_re", "ssm_lam_im", "ssm_log_dt", "ssm_b_re", "ssm_b_im", "ssm_c_re",
           "ssm_c_im", "ssm_d", "ssm_w_glu", "xa_w_q", "xa_w_kv", "xa_w_o", "ffn_w_up", "ffn_conv_w", "ffn_conv_b",
           "ffn_w_down")


def _rows8(g):
    return g.reshape(N_DEV, g.size // (N_DEV * D_MODEL), D_MODEL)


def _square(a):
    return a.reshape(D_MODEL, D_MODEL)


_LAYOUT = {"ab_w_out": _square, "ssm_w_in": _square, "xa_w_q": _square, "xa_w_o": _square,
           "ffn_w_down": lambda a: a.reshape(N_DEV // 2, FF_SHARD, D_MODEL)}
GATHER_ORDER = (("ab_w_in", 0), ("ffn_conv_w", None), ("ssm_d", None), ("ab_w_out", 0), ("xa_w_q", 0),
                ("xa_w_kv", 0), ("xa_w_o", 0), ("ffn_w_up", 0), ("ffn_w_down", 0), ("ffn_w_up", 1),
                ("ffn_w_down", 1), ("ssm_w_in", 0), ("ssm_w_glu", 0), ("xa_w_q", 1), ("xa_w_kv", 1), ("xa_w_o", 1))
GATHER_FIRST = 3
GATHER_AHEAD = 7


class _Step:
    def __init__(self, master, small):
        self.master, self.small = master, small
        self.pending, self.gathers, self.weights, self.sent = [], {}, {}, []

    def follow(self, v):
        for z in self.pending:
            v = v + z
        self.pending = []
        return v

    def start_gathers(self, upto, zero):
        for n, l in GATHER_ORDER[len(self.gathers):upto]:
            if l is None:
                shard = self.master[n] + zero
            else:
                shard = (self.master[n][l] + zero).astype(MXU_DTYPE)
            self.gathers[(n, l)], z = split_start(f"ags_{n}{'' if l is None else l}", shard, gather=True)
            self.pending.append(z)

    def weight(self, n, l, after):
        if (n, l) not in self.weights:
            full, z = split_wait(f"agw_{n}{'' if l is None else l}", self.gathers[(n, l)], after, gather=True)
            self.weights[(n, l)] = _LAYOUT.get(n, lambda a: a)(full)
            self.start_gathers(GATHER_ORDER.index((n, l)) + 1 + GATHER_AHEAD, z)
        return self.weights[(n, l)]

    def send_grad(self, n, l, part):
        h, z = split_start(f"xs_{n}{l}", part, gather=False)
        self.pending.append(z)
        self.sent.append((n, l, h))


def _layer_tail(st, l, x_in, hq, mem_n, acts, next_gain=None):
    bsz, seq = acts["bsz"], acts["seq"]
    p = st.small
    q = mm_nn(f"xa_q{l}", hq, st.weight("xa_w_q", l, x_in))
    kv = mm_nn_bs(f"xa_kv{l}", mem_n, st.weight("xa_w_kv", l, x_in))
    o = xattn_fwd(q, kv, bsz, seq)
    x_mid, hf = mm_nn(f"xa_o{l}", o, st.weight("xa_w_o", l, o), res=x_in, out_dtype=F32,
                      norm_gain=st.follow(p["norm_ffn"][l]))
    up = mm_nn_bs(f"ffn_up{l}", hf, st.weight("ffn_w_up", l, x_mid), stacked_out=True)
    conv_w = st.weight("ffn_conv_w", None, x_mid)[:, l]
    act = conv_fwd(up, conv_w, p["ffn_conv_b"][l], bsz, seq)
    w_down = st.weight("ffn_w_down", l, act)
    if next_gain is None:
        x_out, h_next = mm_as_nn(f"ffn_down{l}", act, w_down, res=x_mid), None
    else:
        x_out, h_next = mm_as_nn(f"ffn_down{l}", act, w_down, res=x_mid, norm_gain=st.follow(next_gain))
    acts[l].update(x_in=x_in, hq=hq, q=q, kv=kv, o=o, x_mid=x_mid, hf=hf, up=up, act=act)
    return x_out, h_next


def _layer_tail_bwd(st, l, dx, mem_n, acts, grads):
    a = acts[l]
    bsz, seq = acts["bsz"], acts["seq"]
    p = st.small
    dact = mm_nt_os(f"d_act{l}", dx, st.weight("ffn_w_down", l, dx))
    st.send_grad("ffn_w_down", l, _rows8(mm_tn(f"g_ffn_down{l}", a["act"], dx, a_stacked=True)))
    conv_w = st.weight("ffn_conv_w", None, dx)[:, l]
    dconv, dcw, dcb = conv_bwd_taps(a["up"], conv_w, p["ffn_conv_b"][l], dact, bsz, seq)
    grads["ffn_conv_w"][l] = dcw
    grads["ffn_conv_b"][l] = dcb
    dup = conv_bwd_input(dconv, conv_w, bsz, seq)
    dx_mid, grads["norm_ffn"][l] = mm_nt_bs(f"d_hf{l}", dup, st.weight("ffn_w_up", l, dx), dc_stacked=True,
                                            rms=(a["x_mid"], st.follow(p["norm_ffn"][l]), dx))
    st.send_grad("ffn_w_up", l, mm_tn(f"g_ffn_up{l}", a["hf"], dup, dc_stacked=True))
    do = mm_nt(f"d_o{l}", dx_mid, st.weight("xa_w_o", l, dx))
    st.send_grad("xa_w_o", l, _rows8(mm_tn(f"g_xa_o{l}", a["o"], dx_mid)))
    dq, dk, dv = xattn_bwd(a["q"], a["kv"], do, bsz, seq)
    dkv = jnp.concatenate([dk, dv], axis=1).astype(BF16)
    dx_in, grads["norm_xattn"][l] = mm_nt(f"d_hq{l}", dq, st.weight("xa_w_q", l, dx),
                                          rms=(a["x_in"], st.follow(p["norm_xattn"][l]), dx_mid))
    st.send_grad("xa_w_q", l, _rows8(mm_tn(f"g_xa_q{l}", a["hq"], dq)))
    dmem_n = mm_nt_bs(f"d_memn{l}", dkv, st.weight("xa_w_kv", l, dx), out_dtype=F32)
    st.send_grad("xa_w_kv", l, mm_tn(f"g_xa_kv{l}", mem_n, dkv, dc_cols=2 * D_MODEL // N_DEV))
    return dx_in, dmem_n


def kernel(x, mem, norm_mix, norm_xattn, norm_ffn, norm_mem, norm_final, ab_w_in, pool_w, pool_scale, ab_w_out, ssm_w_in, ssm_lam_re, ssm_lam_im, ssm_log_dt, ssm_b_re, ssm_b_im, ssm_c_re, ssm_c_im, ssm_d, ssm_w_glu, xa_w_q, xa_w_kv, xa_w_o, ffn_w_up, ffn_conv_w, ffn_conv_b, ffn_w_down, loss_target, m_norm_mix, m_norm_xattn, m_norm_ffn, m_norm_mem, m_norm_final, m_ab_w_in, m_pool_w, m_pool_scale, m_ab_w_out, m_ssm_w_in, m_ssm_lam_re, m_ssm_lam_im, m_ssm_log_dt, m_ssm_b_re, m_ssm_b_im, m_ssm_c_re, m_ssm_c_im, m_ssm_d, m_ssm_w_glu, m_xa_w_q, m_xa_w_kv, m_xa_w_o, m_ffn_w_up, m_ffn_conv_w, m_ffn_conv_b, m_ffn_w_down, v_norm_mix, v_norm_xattn, v_norm_ffn, v_norm_mem, v_norm_final, v_ab_w_in, v_pool_w, v_pool_scale, v_ab_w_out, v_ssm_w_in, v_ssm_lam_re, v_ssm_lam_im, v_ssm_log_dt, v_ssm_b_re, v_ssm_b_im, v_ssm_c_re, v_ssm_c_im, v_ssm_d, v_ssm_w_glu, v_xa_w_q, v_xa_w_kv, v_xa_w_o, v_ffn_w_up, v_ffn_conv_w, v_ffn_conv_b, v_ffn_w_down):
    given = dict(locals())
    master = {n: given[n] for n in WEIGHTS}
    mom1 = {n: given["m_" + n] for n in WEIGHTS}
    mom2 = {n: given["v_" + n] for n in WEIGHTS}
    bsz, seq, d = x.shape
    t = bsz * seq
    me = _my_index()

    st = _Step(master, {"norm_xattn": norm_xattn, "norm_ffn": norm_ffn,
                        "ffn_conv_b": [ffn_conv_b[l].reshape(N_DEV, 1, FF_SHARD) for l in range(2)]})
    st.start_gathers(GATHER_FIRST, 0.0)
    zero = st.follow(jnp.zeros((), F32))

    acts = {"bsz": bsz, "seq": seq, 0: {}, 1: {}}
    x0 = x.reshape(t, d)
    mem2 = mem.reshape(bsz * MEM_LEN, d)
    mem_n = rms_fwd("rms_mem", mem2, norm_mem + zero)
    pscale = pool_scale.reshape(1, SB_WIDTH)

    h0 = rms_fwd("rms_mix0", x0, norm_mix[0] + zero)
    w_in = st.weight("ab_w_in", 0, h0)
    proj = mm_nn_bs("ab_in", h0, w_in, out_dtype=F32)
    a_out, rsum = sb_attn_fwd(proj, st.follow(jnp.zeros((1, 128), F32)), bsz, seq)
    p_out = pool_fwd(proj, pool_w[0], pscale, bsz, seq)
    w_out = st.weight("ab_w_out", 0, a_out)
    x1 = mm_nn("ab_out_a", a_out, w_out, res=x0, out_dtype=F32)
    x1, hq0 = mm_nn("ab_out_p", p_out, w_out, res=x1, koff=SB_WIDTH, out_dtype=F32,
                    norm_gain=st.follow(norm_xattn[0]))
    x3, h1 = _layer_tail(st, 0, x1, hq0, mem_n, acts, next_gain=norm_mix[1])

    b_re2 = ssm_b_re.reshape(64, 1024)
    b_im2 = ssm_b_im.reshape(64, 1024)
    log_dt = ssm_log_dt.reshape(64, 1)
    lb_re, lb_im, bb_re2, bb_im2 = ssm_prep(ssm_lam_re[0], ssm_lam_im[0], log_dt, b_re2, b_im2)
    wt = _ssm_in_weights(bb_re2, bb_im2)
    ct = _ssm_out_weights(ssm_c_re[0], ssm_c_im[0])
    a_re = lb_re.reshape(1, SSM_STATES)
    a_im = lb_im.reshape(1, SSM_STATES)
    u = mm_nn("ssm_in", h1, st.weight("ssm_w_in", 0, x3), out_dtype=F32)
    dskip = st.weight("ssm_d", None, x3).reshape(1, D_MODEL)
    y, gl, h_re, h_im = ssm_fwd(u, wt, ct, a_re, a_im, dskip, bsz, seq)
    glu = mm_nn_bs("ssm_glu", gl, st.weight("ssm_w_glu", 0, gl), out_dtype=F32)
    x4, hq1 = glu_fwd(glu, x3, st.follow(norm_xattn[1]))
    x6, _ = _layer_tail(st, 1, x4, hq1, mem_n, acts)

    loss_row, dx, g_norm_final = loss_head(x6, norm_final, loss_target.reshape(t, d))
    loss = lax.psum(loss_row[0, 0], MESH_AXES)

    grads = {n: [None, None] for n in ("ffn_conv_w", "ffn_conv_b", "norm_ffn", "norm_xattn", "norm_mix")}
    dx4, dmem_1 = _layer_tail_bwd(st, 1, dx, mem_n, acts, grads)
    dglu = glu_bwd(glu, dx4)
    dgl = mm_nt_bs("d_gl", dglu, st.weight("ssm_w_glu", 0, dx))
    st.send_grad("ssm_w_glu", 0, mm_tn("g_ssm_glu", gl, dglu, dc_cols=2 * D_MODEL // N_DEV))
    du, dwt, dct, g_dskip, da_re, da_im = ssm_bwd(dgl, y, u, h_re, h_im, wt, ct, a_re, a_im, dskip, bsz, seq)
    dbb_re, dbb_im = _ssm_in_weights_bwd(dwt)
    g_c_re, g_c_im = _ssm_out_weights_bwd(dct)
    g_lam_re, g_lam_im, g_log_dt, g_b_re, g_b_im = ssm_prep_bwd(
        ssm_lam_re[0], ssm_lam_im[0], log_dt, b_re2, b_im2, da_re.reshape(64, 64), da_im.reshape(64, 64),
        dbb_re, dbb_im)
    dx3, grads["norm_mix"][1] = mm_nt("d_h1", du, st.weight("ssm_w_in", 0, dx),
                                      rms=(x3, st.follow(norm_mix[1]), dx4))
    st.send_grad("ssm_w_in", 0, _rows8(mm_tn("g_ssm_in", h1, du)))

    dx1, dmem_0 = _layer_tail_bwd(st, 0, dx3, mem_n, acts, grads)
    dcat = mm_nt("d_cat", dx1, st.weight("ab_w_out", 0, dx))
    st.send_grad("ab_w_out", 0, _rows8(jnp.concatenate(
        [mm_tn("g_ab_out_a", a_out, dx1), mm_tn("g_ab_out_p", p_out, dx1)], axis=0)))
    dq, dk, dv = sb_attn_bwd(proj, rsum, dcat, bsz, seq)
    dpu, g_pool_w, g_pool_scale = pool_bwd(proj, pool_w[0], st.follow(pscale), dcat, bsz, seq)
    dproj = jnp.concatenate([dq, dk, dv, dpu], axis=1).astype(BF16)
    st.send_grad("ab_w_in", 0, mm_tn("g_ab_in", h0, dproj, dc_cols=2 * D_MODEL // N_DEV))
    dx0, grads["norm_mix"][0] = mm_nt_bs("d_h0", dproj, st.weight("ab_w_in", 0, dx),
                                         rms=(x0, st.follow(norm_mix[0]), dx1))
    _, g_norm_mem = rms_bwd("rms_mem_bwd", mem2, norm_mem, dmem_0 + dmem_1, need_dx=False)

    small_g = {
        "norm_mix": jnp.stack([g[0] for g in grads["norm_mix"]]),
        "norm_xattn": jnp.stack([g[0] for g in grads["norm_xattn"]]),
        "norm_ffn": jnp.stack([g[0] for g in grads["norm_ffn"]]),
        "norm_mem": g_norm_mem[0], "norm_final": g_norm_final[0],
        "pool_w": g_pool_w[None], "pool_scale": g_pool_scale,
        "ssm_lam_re": g_lam_re[None], "ssm_lam_im": g_lam_im[None], "ssm_log_dt": g_log_dt.reshape(1, 64),
        "ssm_b_re": g_b_re.reshape(1, 64, 64, 16), "ssm_b_im": g_b_im.reshape(1, 64, 64, 16),
        "ssm_c_re": g_c_re[None], "ssm_c_im": g_c_im[None],
        "ffn_conv_b": jnp.stack([g.reshape(2 * D_FF) for g in grads["ffn_conv_b"]]),
        "ssm_d": g_dskip,
        "ffn_conv_w": jnp.stack([g.transpose(1, 0, 2).reshape(3, 2 * D_FF) for g in grads["ffn_conv_w"]]),
    }
    sizes = [int(small_g[n].size) for n in SMALL]
    total = sum(sizes)
    rows8 = -(-total // (N_DEV * 128 * 8)) * 8
    flat = jnp.concatenate([small_g[n].reshape(-1).astype(F32) for n in SMALL]
                           + [jnp.zeros((N_DEV * rows8 * 128 - total,), F32)])
    in_flight, z = split_start("xs_small", flat.reshape(N_DEV, rows8, 128), gather=False)
    st.pending.append(z)
    stepped, last = {}, dx0
    for i, (n, l, handles) in enumerate(st.sent):
        if i == len(st.sent) // 2:
            recv, _ = split_wait("xw_small", in_flight, last, gather=False)
            in_flight, z = split_start("ags_small", sum_parts("sum_small", recv), gather=True)
            st.pending.append(z)
        recv, _ = split_wait(f"xw_{n}{l}", handles, dx0, gather=False)
        shape3 = (master[n].shape[0],) + recv.shape[1:]
        stepped[n] = adamw(f"adamw_{n}{l}", master[n].reshape(shape3), mom1[n].reshape(shape3),
                           mom2[n].reshape(shape3), parts=recv, layer=l, into=stepped.get(n),
                           order=st.follow(jnp.zeros((1, 128), F32)))
        last = stepped[n][0]
    out_g, out_d, out_m, out_v = ({n: stepped[n][k].reshape(master[n].shape) for n in BIG} for k in range(4))
    summed = split_wait("agw_small", in_flight, last, gather=True)[0].reshape(-1)

    def local_part(name, a):
        ax = SMALL_SHARDED.get(name)
        if ax is None:
            return a
        n_loc = a.shape[ax] // N_DEV
        return lax.dynamic_slice_in_dim(a, me * n_loc, n_loc, axis=ax)

    off = 0
    for n, sz in zip(SMALL, sizes):
        g_n = local_part(n, summed[off:off + sz].reshape(small_g[n].shape))
        off += sz
        cols = g_n.shape[-1] if g_n.shape[-1] >= 128 or g_n.ndim < 3 else g_n.shape[-1] * g_n.shape[-2]
        shape3 = (1, g_n.size // cols, cols)
        res = adamw("adamw_" + n, master[n].reshape(shape3), mom1[n].reshape(shape3), mom2[n].reshape(shape3),
                    g=g_n.reshape(shape3[1:]))
        for dst, r in zip((out_g, out_d, out_m, out_v), res):
            dst[n] = r.reshape(master[n].shape)

    return (loss, dx0.reshape(bsz, seq, d), *[out_g[n] for n in WEIGHTS], *[out_d[n] for n in WEIGHTS],
            *[out_m[n] for n in WEIGHTS], *[out_v[n] for n in WEIGHTS])
```

```python
import math

import jax
import jax.numpy as jnp
from jax import lax
from jax.experimental import pallas as pl
from jax.experimental.pallas import tpu as pltpu

F32 = jnp.float32
BF16 = jnp.bfloat16
MXU_DTYPE = jnp.bfloat16
N_DEV = 8
MESH_AXES = ("x", "y", "c")

D_MODEL = 1024
SB_HEAD_DIM = 64
SB_WIDTH = 512
SB_BLOCK = 256
POOL_WINDOWS = (2, 4, 8, 16)
POOL_GROUP = 128
POOL_HALO = 16
SSM_TILES = 8
SSM_TILE_STATES = 512
SSM_STATES = 4096
SSM_LANES = 1024
MEM_LEN = 256
XA_HEADS = 4
XA_HEAD_DIM = 256
D_FF = 2816
FF_SHARD = 704
EPS = 1e-6
ADAM_LR = 0.001
ADAM_B1 = 0.9
ADAM_B2 = 0.999
ADAM_EPS = 1e-08
ADAM_WD = 0.01
ADAM_STEP = 10
VMEM_LIMIT = 56 * 1024 * 1024

_NN = (((1,), (0,)), ((), ()))
_NT = (((1,), (1,)), ((), ()))
_TN = (((0,), (0,)), ((), ()))


def _params(sem=None):
    if sem is None:
        return pltpu.CompilerParams(vmem_limit_bytes=VMEM_LIMIT)
    return pltpu.CompilerParams(dimension_semantics=sem, vmem_limit_bytes=VMEM_LIMIT)


def _tile(n, pref, mult=8):
    if n <= pref:
        return n
    for t in range(pref, 0, -1):
        if n % t == 0 and t % mult == 0:
            return t
    return n


def _dot(a, b, dims):
    return lax.dot_general(a.astype(MXU_DTYPE), b.astype(MXU_DTYPE), dims, preferred_element_type=F32)


def _dot_exact01(x, m01, dims=_NN):
    x1 = x.astype(BF16)
    r1 = x - x1.astype(F32)
    x2 = r1.astype(BF16)
    x3 = (r1 - x2.astype(F32)).astype(BF16)
    m = m01.astype(BF16)
    out = lax.dot_general(x1, m, dims, preferred_element_type=F32)
    out = out + lax.dot_general(x2, m, dims, preferred_element_type=F32)
    return out + lax.dot_general(x3, m, dims, preferred_element_type=F32)


def _mm(name, a, b, dims, grid, a_spec, b_spec, o_spec, out_shape, out_dtype, acc_shape, res=None, r_spec=None,
        group=1, n=None, a_sel="full", b_sel="full", o_sel="full", norm_gain=None, rms=None):
    nk = grid[2]
    if out_dtype is None:
        out_dtype = BF16
    n_out = out_shape[-1]
    vec = pl.BlockSpec((1, n_out), lambda i, j, kk: (0, 0))

    def at(sel, s):
        if sel == "lead":
            return (s,)
        if sel == "lanes":
            return (slice(None), slice(s * n, (s + 1) * n))
        return (Ellipsis,)

    extra = [] if res is None else [(res, r_spec)]
    if norm_gain is not None:
        extra.append((norm_gain.reshape(1, n_out), vec))
    if rms is not None:
        extra += [(rms[0], o_spec), (rms[1].reshape(1, n_out), vec), (rms[2], o_spec)]
    n_in = 2 + len(extra)
    if rms is not None:
        out_specs = [o_spec, vec]
        out_shapes = [jax.ShapeDtypeStruct(out_shape, F32), jax.ShapeDtypeStruct((1, n_out), F32)]
    elif norm_gain is not None:
        out_specs = [o_spec, o_spec]
        out_shapes = [jax.ShapeDtypeStruct(out_shape, out_dtype), jax.ShapeDtypeStruct(out_shape, BF16)]
    else:
        out_specs, out_shapes = o_spec, jax.ShapeDtypeStruct(out_shape, out_dtype)

    def body(*refs):
        a_ref, b_ref = refs[0], refs[1]
        ins = list(refs[2:n_in])
        r_ref = ins.pop(0) if res is not None else None
        outs = refs[n_in:]
        o_ref = outs[0]
        acc = refs[-1] if nk > 1 else None
        k = pl.program_id(2)

        def finish(val):
            if r_ref is not None:
                val = val + r_ref[...].astype(F32)
            if rms is not None:
                x_ref, g_ref, d_ref = ins
                xf = x_ref[...]
                r = lax.rsqrt(jnp.mean(xf * xf, axis=-1, keepdims=True) + EPS)
                xh = xf * r
                part = jnp.sum(val * xh, axis=0, keepdims=True)
                first = pl.program_id(0) == 0

                @pl.when(first)
                def _():
                    outs[1][...] = part

                @pl.when(jnp.logical_not(first))
                def _():
                    outs[1][...] += part

                dxh = val * g_ref[...]
                o_ref[...] = d_ref[...] + r * (dxh - xh * jnp.mean(dxh * xh, axis=-1, keepdims=True))
                return
            o_ref[...] = val.astype(out_dtype)
            if norm_gain is not None:
                r = lax.rsqrt(jnp.mean(val * val, axis=-1, keepdims=True) + EPS)
                outs[1][...] = (val * r * ins[0][...]).astype(BF16)

        def emit(s, val):
            if nk == 1:
                if o_sel == "full":
                    finish(val)
                else:
                    o_ref[at(o_sel, s)] = val.astype(out_dtype)
                return

            @pl.when(k == 0)
            def _():
                acc[at(o_sel, s)] = val

            @pl.when(k > 0)
            def _():
                acc[at(o_sel, s)] += val

        total = None
        if a_sel == "full" and b_sel == "lanes":
            wide = _dot(a_ref[...], b_ref[...], dims)
            for s in range(group):
                emit(s, wide[:, s * n:(s + 1) * n])
        else:
            for s in range(group):
                val = _dot(a_ref[at(a_sel, s)], b_ref[at(b_sel, s)], dims)
                if o_sel == "full":
                    total = val if total is None else total + val
                else:
                    emit(s, val)
        if o_sel == "full":
            emit(0, total)
        if nk > 1:
            @pl.when(k == nk - 1)
            def _():
                if o_sel == "full":
                    finish(acc[...])
                else:
                    o_ref[...] = acc[...].astype(out_dtype)

    rows_sem = "arbitrary" if rms is not None else "parallel"
    return pl.pallas_call(
        body, name=name, grid=grid, in_specs=[a_spec, b_spec] + [s for _, s in extra], out_specs=out_specs,
        out_shape=out_shapes, scratch_shapes=[pltpu.VMEM(acc_shape, F32)] if nk > 1 else [],
        compiler_params=_params((rows_sem, rows_sem, "arbitrary")),
    )(a, b, *[x for x, _ in extra])


def _row_tile(m, epi):
    return _tile(m, 512 if epi.get("rms") is not None else 1024)


def mm_nn(name, a, b, res=None, koff=0, out_dtype=None, **epi):
    m, k = a.shape
    n = b.shape[1]
    tm, tn, tk = _row_tile(m, epi), _tile(n, 1024, 128), _tile(k, 1024, 128)
    kb = koff // tk
    spec = pl.BlockSpec((tm, tn), lambda i, j, kk: (i, j))
    return _mm(name, a, b, _NN, (m // tm, n // tn, k // tk),
               pl.BlockSpec((tm, tk), lambda i, j, kk: (i, kk)),
               pl.BlockSpec((tk, tn), lambda i, j, kk: (kk + kb, j)),
               spec, (m, n), out_dtype, (tm, tn), res, spec, **epi)


def mm_nn_bs(name, a, bs, stacked_out=False, out_dtype=None):
    m, k = a.shape
    s, _, n = bs.shape
    tm, tk = _tile(m, 1024), _tile(k, 1024, 128)
    a_spec = pl.BlockSpec((tm, tk), lambda i, j, kk: (i, kk))
    if stacked_out:
        return _mm(name, a, bs, _NN, (m // tm, s, k // tk), a_spec,
                   pl.BlockSpec((None, tk, n), lambda i, j, kk: (j, kk, 0)),
                   pl.BlockSpec((None, tm, n), lambda i, j, kk: (j, i, 0)), (s, m, n), out_dtype, (tm, n))
    g = _tile(s, max(1, 1024 // n), 1)
    return _mm(name, a, bs, _NN, (m // tm, s // g, k // tk), a_spec,
               pl.BlockSpec((g, tk, n), lambda i, j, kk: (j, kk, 0)),
               pl.BlockSpec((tm, g * n), lambda i, j, kk: (i, j)), (m, s * n), out_dtype, (tm, g * n),
               group=g, n=n, b_sel="lead", o_sel="lanes")


def mm_as_nn(name, a_st, b3, res, out_dtype=F32, **epi):
    s, m, kp = a_st.shape
    n = b3.shape[2]
    tm, tn = _row_tile(m, epi), _tile(n, 1024, 128)
    spec = pl.BlockSpec((tm, tn), lambda i, j, kk: (i, j))
    g = _tile(s, 2, 1)
    return _mm(name, a_st, b3, _NN, (m // tm, n // tn, s // g),
               pl.BlockSpec((g, tm, kp), lambda i, j, kk: (kk, i, 0)),
               pl.BlockSpec((g, kp, tn), lambda i, j, kk: (kk, 0, j)),
               spec, (m, n), out_dtype, (tm, tn), res, spec, group=g, a_sel="lead", b_sel="lead", **epi)


def mm_nt(name, dc, b, out_dtype=None, **epi):
    m, n = dc.shape
    k = b.shape[0]
    tm, tko, tnr = _row_tile(m, epi), _tile(k, 1024, 128), _tile(n, 1024, 128)
    return _mm(name, dc, b, _NT, (m // tm, k // tko, n // tnr),
               pl.BlockSpec((tm, tnr), lambda i, j, kk: (i, kk)),
               pl.BlockSpec((tko, tnr), lambda i, j, kk: (j, kk)),
               pl.BlockSpec((tm, tko), lambda i, j, kk: (i, j)), (m, k), out_dtype, (tm, tko), **epi)


def mm_nt_bs(name, dc, bs, dc_stacked=False, out_dtype=None, **epi):
    s, k, n = bs.shape
    m = dc.shape[1] if dc_stacked else dc.shape[0]
    tm, tko = (_tile(m, 1024) if dc_stacked else _row_tile(m, epi)), _tile(k, 1024, 128)
    o_spec = pl.BlockSpec((tm, tko), lambda i, j, kk: (i, j))
    if dc_stacked:
        g = _tile(s, 2, 1)
        return _mm(name, dc, bs, _NT, (m // tm, k // tko, s // g),
                   pl.BlockSpec((g, tm, n), lambda i, j, kk: (kk, i, 0)),
                   pl.BlockSpec((g, tko, n), lambda i, j, kk: (kk, j, 0)), o_spec, (m, k), out_dtype, (tm, tko),
                   group=g, a_sel="lead", b_sel="lead", **epi)
    g = _tile(s, max(1, 2048 // n), 1)
    return _mm(name, dc, bs, _NT, (m // tm, k // tko, s // g),
               pl.BlockSpec((tm, g * n), lambda i, j, kk: (i, kk)),
               pl.BlockSpec((g, tko, n), lambda i, j, kk: (kk, j, 0)), o_spec, (m, k), out_dtype, (tm, tko),
               group=g, n=n, a_sel="lanes", b_sel="lead", **epi)


def mm_nt_os(name, dc, b3, out_dtype=None):
    m, n = dc.shape
    s, kp, _ = b3.shape
    tm, tnr = _tile(m, 1024), _tile(n, 1024, 128)
    return _mm(name, dc, b3, _NT, (m // tm, s, n // tnr),
               pl.BlockSpec((tm, tnr), lambda i, j, kk: (i, kk)),
               pl.BlockSpec((None, kp, tnr), lambda i, j, kk: (j, 0, kk)),
               pl.BlockSpec((None, tm, kp), lambda i, j, kk: (j, i, 0)), (s, m, kp), out_dtype, (tm, kp))


def mm_tn(name, a, dc, a_stacked=False, dc_cols=None, dc_stacked=False, out_dtype=None):
    if a_stacked:
        s, m, kp = a.shape
        n = dc.shape[1]
        tno, tmr = _tile(n, 1024, 128), _tile(m, 2048)
        return _mm(name, a, dc, _TN, (s, n // tno, m // tmr),
                   pl.BlockSpec((None, tmr, kp), lambda i, j, kk: (i, kk, 0)),
                   pl.BlockSpec((tmr, tno), lambda i, j, kk: (kk, j)),
                   pl.BlockSpec((None, kp, tno), lambda i, j, kk: (i, 0, j)), (s, kp, n), out_dtype, (kp, tno))
    m, k = a.shape
    tko, tmr = _tile(k, 1024, 128), _tile(m, 2048)
    a_spec = pl.BlockSpec((tmr, tko), lambda i, j, kk: (kk, i))
    if dc_stacked:
        s, _, n = dc.shape
        return _mm(name, a, dc, _TN, (k // tko, s, m // tmr), a_spec,
                   pl.BlockSpec((None, tmr, n), lambda i, j, kk: (j, kk, 0)),
                   pl.BlockSpec((None, tko, n), lambda i, j, kk: (j, i, 0)), (s, k, n), out_dtype, (tko, n))
    if dc_cols is not None:
        n = dc_cols
        s = dc.shape[1] // n
        g = _tile(s, max(1, 1024 // n), 1)
        return _mm(name, a, dc, _TN, (k // tko, s // g, m // tmr), a_spec,
                   pl.BlockSpec((tmr, g * n), lambda i, j, kk: (kk, j)),
                   pl.BlockSpec((g, tko, n), lambda i, j, kk: (j, i, 0)), (s, k, n), out_dtype, (g, tko, n),
                   group=g, n=n, b_sel="lanes", o_sel="lead")
    n = dc.shape[1]
    tno = _tile(n, 1024, 128)
    return _mm(name, a, dc, _TN, (k // tko, n // tno, m // tmr), a_spec,
               pl.BlockSpec((tmr, tno), lambda i, j, kk: (kk, j)),
               pl.BlockSpec((tko, tno), lambda i, j, kk: (i, j)), (k, n), out_dtype, (tko, tno))


def rms_fwd(name, x, g):
    t, d = x.shape
    tr = _tile(t, 512)

    def body(x_ref, g_ref, o_ref):
        xf = x_ref[...]
        r = lax.rsqrt(jnp.mean(xf * xf, axis=-1, keepdims=True) + EPS)
        o_ref[...] = (xf * r * g_ref[...]).astype(o_ref.dtype)

    return pl.pallas_call(
        body, name=name, grid=(t // tr,),
        in_specs=[pl.BlockSpec((tr, d), lambda i: (i, 0)), pl.BlockSpec((1, d), lambda i: (0, 0))],
        out_specs=pl.BlockSpec((tr, d), lambda i: (i, 0)),
        out_shape=jax.ShapeDtypeStruct((t, d), BF16), compiler_params=_params(("parallel",)),
    )(x, g.reshape(1, d))


def rms_bwd(name, x, g, dh, dres=None, need_dx=True):
    t, d = x.shape
    tr = _tile(t, 512)

    def body(*refs):
        refs = list(refs)
        x_ref, g_ref, dh_ref = refs[:3]
        r_ref = refs[3] if dres is not None else None
        outs = refs[4:] if dres is not None else refs[3:]
        dx_ref, dg_ref = (outs[0], outs[1]) if need_dx else (None, outs[0])
        i = pl.program_id(0)

        @pl.when(i == 0)
        def _():
            dg_ref[...] = jnp.zeros_like(dg_ref)

        xf = x_ref[...]
        dhf = dh_ref[...].astype(F32)
        r = lax.rsqrt(jnp.mean(xf * xf, axis=-1, keepdims=True) + EPS)
        xh = xf * r
        dg_ref[...] += jnp.sum(dhf * xh, axis=0, keepdims=True)
        if need_dx:
            dxh = dhf * g_ref[...]
            dx = r * (dxh - xh * jnp.mean(dxh * xh, axis=-1, keepdims=True))
            if r_ref is not None:
                dx = dx + r_ref[...]
            dx_ref[...] = dx

    row = pl.BlockSpec((tr, d), lambda i: (i, 0))
    vec = pl.BlockSpec((1, d), lambda i: (0, 0))
    in_specs = [row, vec, row] + ([row] if dres is not None else [])
    args = (x, g.reshape(1, d), dh) + ((dres,) if dres is not None else ())
    out_specs = ([row] if need_dx else []) + [vec]
    out_shape = ([jax.ShapeDtypeStruct((t, d), F32)] if need_dx else []) + [jax.ShapeDtypeStruct((1, d), F32)]
    res = pl.pallas_call(
        body, name=name, grid=(t // tr,), in_specs=in_specs, out_specs=out_specs, out_shape=out_shape,
        compiler_params=_params(("arbitrary",)),
    )(*args)
    return res if need_dx else (None, res[0])


def loss_head(x, g, tgt):
    t, d = x.shape
    tr = _tile(t, 512)

    def body(x_ref, g_ref, t_ref, l_ref, dx_ref, dg_ref):
        i = pl.program_id(0)

        @pl.when(i == 0)
        def _():
            l_ref[...] = jnp.zeros_like(l_ref)
            dg_ref[...] = jnp.zeros_like(dg_ref)

        xf = x_ref[...]
        r = lax.rsqrt(jnp.mean(xf * xf, axis=-1, keepdims=True) + EPS)
        xh = xf * r
        diff = xh * g_ref[...] - t_ref[...]
        l_ref[...] += 0.5 * jnp.sum(jnp.mean(diff * diff, axis=-1, keepdims=True))
        dy = diff * (1.0 / d)
        dg_ref[...] += jnp.sum(dy * xh, axis=0, keepdims=True)
        dxh = dy * g_ref[...]
        dx_ref[...] = r * (dxh - xh * jnp.mean(dxh * xh, axis=-1, keepdims=True))

    row = pl.BlockSpec((tr, d), lambda i: (i, 0))
    vec = pl.BlockSpec((1, d), lambda i: (0, 0))
    return pl.pallas_call(
        body, name="loss_head", grid=(t // tr,), in_specs=[row, vec, row],
        out_specs=[pl.BlockSpec((1, 128), lambda i: (0, 0)), row, vec],
        out_shape=[jax.ShapeDtypeStruct((1, 128), F32), jax.ShapeDtypeStruct((t, d), F32),
                   jax.ShapeDtypeStruct((1, d), F32)],
        compiler_params=_params(("arbitrary",)),
    )(x, g.reshape(1, d), tgt)


def glu_fwd(glu, x, gain):
    t, d = x.shape
    tr = _tile(t, 512)

    def body(v_ref, g_ref, x_ref, n_ref, o_ref, h_ref):
        y = x_ref[...] + v_ref[...] * jax.nn.sigmoid(g_ref[...])
        o_ref[...] = y
        r = lax.rsqrt(jnp.mean(y * y, axis=-1, keepdims=True) + EPS)
        h_ref[...] = (y * r * n_ref[...]).astype(h_ref.dtype)

    row = pl.BlockSpec((tr, d), lambda i: (i, 0))
    return pl.pallas_call(
        body, name="glu_fwd", grid=(t // tr,),
        in_specs=[row, pl.BlockSpec((tr, d), lambda i: (i, 1)), row, pl.BlockSpec((1, d), lambda i: (0, 0))],
        out_specs=[row, row],
        out_shape=[jax.ShapeDtypeStruct((t, d), F32), jax.ShapeDtypeStruct((t, d), BF16)],
        compiler_params=_params(("parallel",)),
    )(glu, glu, x, gain.reshape(1, d))


def glu_bwd(glu, dmix):
    t, d = dmix.shape
    tr = _tile(t, 512)

    def body(v_ref, g_ref, d_ref, o_ref):
        sg = jax.nn.sigmoid(g_ref[...])
        dm = d_ref[...]
        o_ref[:, :d] = (dm * sg).astype(o_ref.dtype)
        o_ref[:, d:] = (dm * v_ref[...] * sg * (1.0 - sg)).astype(o_ref.dtype)

    return pl.pallas_call(
        body, name="glu_bwd", grid=(t // tr,),
        in_specs=[pl.BlockSpec((tr, d), lambda i: (i, 0)), pl.BlockSpec((tr, d), lambda i: (i, 1)),
                  pl.BlockSpec((tr, d), lambda i: (i, 0))],
        out_specs=pl.BlockSpec((tr, 2 * d), lambda i: (i, 0)),
        out_shape=jax.ShapeDtypeStruct((t, 2 * d), BF16), compiler_params=_params(("parallel",)),
    )(glu, glu, dmix)


def _head_masks(shape):
    lane = lax.broadcasted_iota(jnp.int32, shape, 1)
    return lane < SB_HEAD_DIM


def _stack_heads(xf, is_a):
    return jnp.concatenate([jnp.where(is_a, xf, 0.0), jnp.where(is_a, 0.0, xf)], axis=0).astype(MXU_DTYPE)


def _diag_mask(qb, row0, rows):
    row = (lax.broadcasted_iota(jnp.int32, (rows, qb), 0) + row0) & (qb - 1)
    col = lax.broadcasted_iota(jnp.int32, (rows, qb), 1)
    return col < row


def _tri01(qb, pred):
    j = lax.broadcasted_iota(jnp.int32, (qb, qb), 0)
    s = lax.broadcasted_iota(jnp.int32, (qb, qb), 1)
    m = pred(j, s).astype(BF16)
    return jnp.concatenate([m, m], axis=0)


def _split_cat(x):
    hi = x.astype(BF16)
    lo = (x - hi.astype(F32)).astype(BF16)
    return jnp.concatenate([hi, lo], axis=1)


def sb_attn_fwd(proj, order, bsz, seq):
    qb = SB_BLOCK
    nq = seq // qb
    npair = SB_WIDTH // 128
    scale = SB_HEAD_DIM ** -0.5

    def body(q_ref, k_ref, v_ref, order_ref, o_ref, r_ref):
        qi = pl.program_id(2)
        is_a = _head_masks((qb, 128))
        q2 = _stack_heads(q_ref[...] * scale, is_a)
        diag = _diag_mask(qb, 0, 2 * qb)
        upper = _tri01(qb, lambda j, s: j > s)

        def blocks(kbs, acc, run, masked):
            sl = [pl.ds(pl.multiple_of(kb * qb, qb), qb) for kb in kbs]
            zs = [lax.dot_general(q2, k_ref[s, :].astype(MXU_DTYPE), _NT, preferred_element_type=F32) for s in sl]
            lks = [-jnp.maximum(z, 0.0) - jnp.log(1.0 + jnp.exp(-jnp.abs(z))) for z in zs]
            lbs = [lk + z for lk, z in zip(lks, zs)]
            if masked:
                lks = [jnp.where(diag, lk, 0.0) for lk in lks]
            cs = [lax.dot_general(_split_cat(lk), upper, _NN, preferred_element_type=F32) for lk in lks]
            for lk, lb, c, s in zip(lks, lbs, cs, sl):
                w = jnp.exp(lb + (run + c))
                if masked:
                    w = jnp.where(diag, w, 0.0)
                acc = acc + lax.dot_general(w.astype(MXU_DTYPE), v_ref[s, :].astype(MXU_DTYPE), _NN,
                                            preferred_element_type=F32)
                run = run + jnp.sum(lk, axis=1, keepdims=True)
            return acc, run

        carry = blocks([qi], jnp.zeros((2 * qb, 128), F32), jnp.zeros((2 * qb, 1), F32), True)
        carry = lax.cond(qi % 2 == 1, lambda c: blocks([qi - 1], c[0], c[1], False), lambda c: c, carry)
        top = qi - qi % 2
        acc, run = lax.fori_loop(
            0, qi // 2, lambda i, c: blocks([top - 1 - 2 * i, top - 2 - 2 * i], c[0], c[1], False), carry)
        o_ref[...] = jnp.where(is_a, acc[:qb], acc[qb:]).astype(o_ref.dtype)
        r_ref[...] = jnp.where(is_a, run[:qb], run[qb:])

    return pl.pallas_call(
        body, name="sb_attn_fwd", grid=(bsz, npair, nq),
        in_specs=[pl.BlockSpec((qb, 128), lambda b, p, i: (b * nq + i, p)),
                  pl.BlockSpec((seq, 128), lambda b, p, i: (b, npair + p)),
                  pl.BlockSpec((seq, 128), lambda b, p, i: (b, 2 * npair + p)),
                  pl.BlockSpec((1, 128), lambda b, p, i: (0, 0))],
        out_specs=[pl.BlockSpec((qb, 128), lambda b, p, i: (b * nq + i, p)),
                   pl.BlockSpec((qb, 128), lambda b, p, i: (b * nq + i, p))],
        out_shape=[jax.ShapeDtypeStruct((bsz * seq, SB_WIDTH), BF16),
                   jax.ShapeDtypeStruct((bsz * seq, SB_WIDTH), F32)],
        compiler_params=_params(("parallel", "parallel", "arbitrary")),
    )(proj, proj, proj, order)


def sb_attn_bwd(proj, rsum, dcat, bsz, seq):
    qb = SB_BLOCK
    nq = seq // qb
    npair = SB_WIDTH // 128
    scale = SB_HEAD_DIM ** -0.5

    def body(q_ref, k_ref, v_ref, r_ref, do_ref, dq_ref, dk_ref, dv_ref):
        qi = pl.program_id(2)

        @pl.when(qi == 0)
        def _():
            dk_ref[...] = jnp.zeros_like(dk_ref)
            dv_ref[...] = jnp.zeros_like(dv_ref)

        is_a = _head_masks((qb, 128))
        q2 = _stack_heads(q_ref[...] * scale, is_a)
        do2 = _stack_heads(do_ref[...].astype(F32), is_a)
        rf = r_ref[...]
        rtot = jnp.concatenate([rf[:, 0:1], rf[:, SB_HEAD_DIM:SB_HEAD_DIM + 1]], axis=0)
        diag = _diag_mask(qb, 0, 2 * qb)
        incl = _tri01(qb, lambda j, s: j <= s)
        strict = _tri01(qb, lambda j, s: j < s)

        def blocks(kbs, dq, pre, epre, masked):
            sl = [pl.ds(pl.multiple_of(kb * qb, qb), qb) for kb in kbs]
            ks = [k_ref[s, :].astype(MXU_DTYPE) for s in sl]
            vs = [v_ref[s, :].astype(MXU_DTYPE) for s in sl]
            zs = [lax.dot_general(q2, kblk, _NT, preferred_element_type=F32) for kblk in ks]
            dws = [lax.dot_general(do2, vblk, _NT, preferred_element_type=F32) for vblk in vs]
            lks = [-jnp.maximum(z, 0.0) - jnp.log(1.0 + jnp.exp(-jnp.abs(z))) for z in zs]
            lbs = [lk + z for lk, z in zip(lks, zs)]
            if masked:
                lks = [jnp.where(diag, lk, 0.0) for lk in lks]
            ps = [lax.dot_general(_split_cat(lk), incl, _NN, preferred_element_type=F32) for lk in lks]
            ws, es = [], []
            for lk, lb, p, dw in zip(lks, lbs, ps, dws):
                w = jnp.exp(lb + (rtot - (pre + p)))
                if masked:
                    w = jnp.where(diag, w, 0.0)
                ws.append(w)
                es.append(dw * w)
                pre = pre + jnp.sum(lk, axis=1, keepdims=True)
            cs = [lax.dot_general(_split_cat(e), strict, _NN, preferred_element_type=F32) for e in es]
            for e, lb, c, w, kblk, s in zip(es, lbs, cs, ws, ks, sl):
                dz = e - jnp.exp(lb) * (e + (epre + c))
                if masked:
                    dz = jnp.where(diag, dz, 0.0)
                dz = dz.astype(MXU_DTYPE)
                dq = dq + lax.dot_general(dz, kblk, _NN, preferred_element_type=F32)
                dk_ref[s, :] += lax.dot_general(dz, q2, _TN, preferred_element_type=F32)
                dv_ref[s, :] += lax.dot_general(w.astype(MXU_DTYPE), do2, _TN, preferred_element_type=F32)
                epre = epre + jnp.sum(e, axis=1, keepdims=True)
            return dq, pre, epre

        zc = jnp.zeros((2 * qb, 1), F32)
        carry = lax.fori_loop(0, qi // 2, lambda i, c: blocks([2 * i, 2 * i + 1], c[0], c[1], c[2], False),
                              (jnp.zeros((2 * qb, 128), F32), zc, zc))
        carry = lax.cond(qi % 2 == 1, lambda c: blocks([qi - 1], c[0], c[1], c[2], False), lambda c: c, carry)
        dq = blocks([qi], carry[0], carry[1], carry[2], True)[0]
        dq_ref[...] = jnp.where(is_a, dq[:qb], dq[qb:]) * scale

    full = jax.ShapeDtypeStruct((bsz * seq, SB_WIDTH), F32)
    qspec = pl.BlockSpec((qb, 128), lambda b, p, i: (b * nq + i, p))
    return pl.pallas_call(
        body, name="sb_attn_bwd", grid=(bsz, npair, nq),
        in_specs=[qspec,
                  pl.BlockSpec((seq, 128), lambda b, p, i: (b, npair + p)),
                  pl.BlockSpec((seq, 128), lambda b, p, i: (b, 2 * npair + p)),
                  qspec, qspec],
        out_specs=[qspec, pl.BlockSpec((seq, 128), lambda b, p, i: (b, p)),
                   pl.BlockSpec((seq, 128), lambda b, p, i: (b, p))],
        out_shape=[full, full, full],
        compiler_params=_params(("parallel", "parallel", "arbitrary")),
    )(proj, proj, proj, rsum, dcat)


def _window_sums(x, forward):
    n = x.shape[0]
    out = []
    s = x
    for sh in (1, 2, 4, 8):
        s = s + pltpu.roll(s, (n - sh) if forward else sh, 0)
        out.append(s)
    return out


def _pool_counts(tc, c, w):
    t = lax.broadcasted_iota(jnp.int32, (tc, 1), 0) + c * tc
    return jnp.minimum(t + 1, w).astype(F32)


def pool_fwd(proj, pool_w, pool_scale, bsz, seq):
    tc = _tile(seq, 512)
    nc = seq // tc
    hb = tc // POOL_HALO
    ucol = 3

    def body(u_ref, prev_ref, w_ref, s_ref, o_ref):
        c = pl.program_id(1)
        prev = jnp.where(c > 0, prev_ref[...], 0.0)
        x = jnp.concatenate([prev, u_ref[...]], axis=0)
        sums = _window_sums(x, forward=False)
        for g, win in enumerate(POOL_WINDOWS):
            ls = slice(g * POOL_GROUP, (g + 1) * POOL_GROUP)
            pooled = sums[g][POOL_HALO:, ls] / _pool_counts(tc, c, win) - x[POOL_HALO:, ls]
            y = _dot(pooled, w_ref[g], _NN)
            o_ref[:, ls] = (y * s_ref[:, ls]).astype(o_ref.dtype)

    return pl.pallas_call(
        body, name="pool_fwd", grid=(bsz, nc),
        in_specs=[pl.BlockSpec((tc, SB_WIDTH), lambda b, c: (b * nc + c, ucol)),
                  pl.BlockSpec((POOL_HALO, SB_WIDTH), lambda b, c: (jnp.maximum((b * nc + c) * hb - 1, 0), ucol)),
                  pl.BlockSpec((4, POOL_GROUP, POOL_GROUP), lambda b, c: (0, 0, 0)),
                  pl.BlockSpec((1, SB_WIDTH), lambda b, c: (0, 0))],
        out_specs=pl.BlockSpec((tc, SB_WIDTH), lambda b, c: (b * nc + c, 0)),
        out_shape=jax.ShapeDtypeStruct((bsz * seq, SB_WIDTH), BF16),
        compiler_params=_params(("parallel", "parallel")),
    )(proj, proj, pool_w, pool_scale)


def pool_bwd(proj, pool_w, pool_scale, dcat, bsz, seq):
    tc = _tile(seq, 512)
    nc = seq // tc
    hb = tc // POOL_HALO
    nblk = bsz * seq // POOL_HALO
    ucol = 3

    def body(u_ref, prev_ref, dy_ref, nxt_ref, w_ref, s_ref, du_ref, dw_ref, ds_ref):
        b, c = pl.program_id(0), pl.program_id(1)

        @pl.when((b == 0) & (c == 0))
        def _():
            dw_ref[...] = jnp.zeros_like(dw_ref)
            ds_ref[...] = jnp.zeros_like(ds_ref)

        prev = jnp.where(c > 0, prev_ref[...], 0.0)
        x = jnp.concatenate([prev, u_ref[...]], axis=0)
        sums = _window_sums(x, forward=False)
        nxt = jnp.where(c < nc - 1, nxt_ref[...].astype(F32), 0.0)
        dy = jnp.concatenate([dy_ref[...].astype(F32), nxt], axis=0)
        tq = lax.broadcasted_iota(jnp.int32, (tc + POOL_HALO, 1), 0) + c * tc
        for g, win in enumerate(POOL_WINDOWS):
            ls = slice(g * POOL_GROUP, (g + 1) * POOL_GROUP)
            pooled = sums[g][POOL_HALO:, ls] / _pool_counts(tc, c, win) - x[POOL_HALO:, ls]
            y = _dot(pooled, w_ref[g], _NN)
            ds_ref[:, ls] += jnp.sum(dy[:tc, ls] * y, axis=0, keepdims=True)
            dz = dy[:, ls] * s_ref[:, ls]
            dw_ref[g] += _dot(pooled, dz[:tc], _TN)
            dpool = _dot(dz, w_ref[g], _NT)
            dmean = dpool / jnp.minimum(tq + 1, win).astype(F32)
            fsum = _window_sums(dmean, forward=True)[g]
            du_ref[:, ls] = fsum[:tc] - dpool[:tc]

    return pl.pallas_call(
        body, name="pool_bwd", grid=(bsz, nc),
        in_specs=[pl.BlockSpec((tc, SB_WIDTH), lambda b, c: (b * nc + c, ucol)),
                  pl.BlockSpec((POOL_HALO, SB_WIDTH), lambda b, c: (jnp.maximum((b * nc + c) * hb - 1, 0), ucol)),
                  pl.BlockSpec((tc, SB_WIDTH), lambda b, c: (b * nc + c, 1)),
                  pl.BlockSpec((POOL_HALO, SB_WIDTH), lambda b, c: (jnp.minimum((b * nc + c + 1) * hb, nblk - 1), 1)),
                  pl.BlockSpec((4, POOL_GROUP, POOL_GROUP), lambda b, c: (0, 0, 0)),
                  pl.BlockSpec((1, SB_WIDTH), lambda b, c: (0, 0))],
        out_specs=[pl.BlockSpec((tc, SB_WIDTH), lambda b, c: (b * nc + c, 0)),
                   pl.BlockSpec((4, POOL_GROUP, POOL_GROUP), lambda b, c: (0, 0, 0)),
                   pl.BlockSpec((1, SB_WIDTH), lambda b, c: (0, 0))],
        out_shape=[jax.ShapeDtypeStruct((bsz * seq, SB_WIDTH), F32),
                   jax.ShapeDtypeStruct((4, POOL_GROUP, POOL_GROUP), F32),
                   jax.ShapeDtypeStruct((1, SB_WIDTH), F32)],
        compiler_params=_params(("arbitrary", "arbitrary")),
    )(proj, proj, dcat, dcat, pool_w, pool_scale)


def _lbar(lam_re, lam_im, log_dt):
    dt = jnp.exp(log_dt)
    mag = jnp.exp(lam_re * dt)
    ang = lam_im * dt
    return mag * jnp.cos(ang), mag * jnp.sin(ang)


def _bbar(lam_re, lam_im, log_dt, b_re, b_im):
    lb_re, lb_im = _lbar(lam_re, lam_im, log_dt)
    n_re = lb_re - 1.0
    den = lam_re * lam_re + lam_im * lam_im
    coef_re = (n_re * lam_re + lb_im * lam_im) / den
    coef_im = (lb_im * lam_re - n_re * lam_im) / den
    return coef_re * b_re - coef_im * b_im, coef_re * b_im + coef_im * b_re


def _expand01():
    p = lax.broadcasted_iota(jnp.int32, (64, 1024), 0)
    q = lax.broadcasted_iota(jnp.int32, (64, 1024), 1)
    return (lax.shift_right_logical(q, 4) == p).astype(BF16)


def ssm_prep(lam_re, lam_im, log_dt, b_re2, b_im2):
    def body(lr_ref, li_ref, dt_ref, br_ref, bi_ref, ar_ref, ai_ref, bbr_ref, bbi_ref):
        e = _expand01()
        lr, li, dt = lr_ref[...], li_ref[...], dt_ref[...]
        ar_ref[...], ai_ref[...] = _lbar(lr, li, dt)
        bbr_ref[...], bbi_ref[...] = _bbar(_dot_exact01(lr, e), _dot_exact01(li, e), dt, br_ref[...], bi_ref[...])

    s64 = jax.ShapeDtypeStruct((64, 64), F32)
    s1k = jax.ShapeDtypeStruct((64, 1024), F32)
    return pl.pallas_call(body, name="ssm_prep", out_shape=[s64, s64, s1k, s1k], compiler_params=_params())(
        lam_re, lam_im, log_dt, b_re2, b_im2)


def ssm_prep_bwd(lam_re, lam_im, log_dt, b_re2, b_im2, da_re, da_im, dbb_re, dbb_im):
    def body(lr_ref, li_ref, dt_ref, br_ref, bi_ref, dar_ref, dai_ref, dbr_ref, dbi_ref,
             olr_ref, oli_ref, odt_ref, obr_ref, obi_ref):
        e = _expand01()
        lr, li, dt = lr_ref[...], li_ref[...], dt_ref[...]
        _, vjp_a = jax.vjp(_lbar, lr, li, dt)
        g_lr, g_li, g_dt = vjp_a((dar_ref[...], dai_ref[...]))
        _, vjp_b = jax.vjp(_bbar, _dot_exact01(lr, e), _dot_exact01(li, e), dt, br_ref[...], bi_ref[...])
        x_lr, x_li, x_dt, g_br, g_bi = vjp_b((dbr_ref[...], dbi_ref[...]))
        olr_ref[...] = g_lr + _dot_exact01(x_lr, e, _NT)
        oli_ref[...] = g_li + _dot_exact01(x_li, e, _NT)
        odt_ref[...] = g_dt + x_dt
        obr_ref[...] = g_br
        obi_ref[...] = g_bi

    s64 = jax.ShapeDtypeStruct((64, 64), F32)
    s1k = jax.ShapeDtypeStruct((64, 1024), F32)
    return pl.pallas_call(body, name="ssm_prep_bwd",
                          out_shape=[s64, s64, jax.ShapeDtypeStruct((64, 1), F32), s1k, s1k],
                          compiler_params=_params())(
        lam_re, lam_im, log_dt, b_re2, b_im2, da_re, da_im, dbb_re, dbb_im)


def _gelu(y):
    c = math.sqrt(2.0 / math.pi)
    return 0.5 * y * (1.0 + jnp.tanh(c * (y + 0.044715 * y * y * y)))


def _gelu_grad(y):
    c = math.sqrt(2.0 / math.pi)
    th = jnp.tanh(c * (y + 0.044715 * y * y * y))
    return 0.5 * (1.0 + th) + 0.5 * y * (1.0 - th * th) * c * (1.0 + 3.0 * 0.044715 * y * y)


def _cmul(ar, ai, br, bi):
    return ar * br - ai * bi, ar * bi + ai * br


def _scan_tables(ar, ai, reverse, tabs):
    row = lax.broadcasted_iota(jnp.int32, (8, SSM_STATES), 0)
    a1 = (ar, ai)
    a2 = _cmul(*a1, *a1)
    a4 = _cmul(*a2, *a2)
    powers = [a1, a2, _cmul(*a2, *a1), a4]
    powers += [_cmul(*a4, *p) for p in powers]
    for k, (val, sh) in enumerate(((a1, 1), (a2, 2), (a4, 4))):
        keep = (row < 8 - sh) if reverse else (row >= sh)
        tabs[2 * k][...] = jnp.where(keep, val[0], 0.0)
        tabs[2 * k + 1][...] = jnp.where(keep, val[1], 0.0)
    pr = jnp.zeros((8, SSM_STATES), F32)
    pi = jnp.zeros((8, SSM_STATES), F32)
    for r in range(8):
        val = powers[7 - r] if reverse else powers[r]
        pr = jnp.where(row == r, val[0], pr)
        pi = jnp.where(row == r, val[1], pi)
    tabs[6][...] = pr
    tabs[7][...] = pi


def _scan8(xr, xi, tabs, ls, cr, ci, reverse):
    for k, sh in enumerate((1, 2, 4)):
        amt = (8 - sh) if reverse else sh
        sr, si = pltpu.roll(xr, amt, 0), pltpu.roll(xi, amt, 0)
        lr, li = tabs[2 * k][:, ls], tabs[2 * k + 1][:, ls]
        xr, xi = xr + lr * sr - li * si, xi + lr * si + li * sr
    pr, pi = tabs[6][:, ls], tabs[7][:, ls]
    return xr + pr * cr - pi * ci, xi + pr * ci + pi * cr


def _block8(b):
    return pl.ds(pl.multiple_of(b * 8, 8), 8)


def ssm_fwd(u, wt, ct, a_re, a_im, dskip, bsz, seq):
    tc = _tile(seq, 256)
    nc = seq // tc
    ns = SSM_TILE_STATES
    nl = SSM_STATES // SSM_LANES

    def body(u_ref, wt_ref, ct_ref, ar_ref, ai_ref, d_ref, y_ref, gl_ref, hr_ref, hi_ref, sr_ref, si_ref, *tabs):
        b, c = pl.program_id(0), pl.program_id(1)

        @pl.when((b == 0) & (c == 0))
        def _():
            _scan_tables(ar_ref[...], ai_ref[...], False, tabs)

        @pl.when(c == 0)
        def _():
            sr_ref[...] = jnp.zeros_like(sr_ref)
            si_ref[...] = jnp.zeros_like(si_ref)

        uf = u_ref[...]
        for i in range(SSM_TILES):
            bu = _dot(uf[:, i * 128:(i + 1) * 128], wt_ref[i], _NN)
            hr_ref[:, i * ns:(i + 1) * ns] = bu[:, :ns]
            hi_ref[:, i * ns:(i + 1) * ns] = bu[:, ns:]

        def step(blk, carry):
            rows = _block8(blk)
            new = []
            for j in range(nl):
                ls = slice(j * SSM_LANES, (j + 1) * SSM_LANES)
                xr, xi = _scan8(hr_ref[rows, ls], hi_ref[rows, ls], tabs, ls, carry[2 * j], carry[2 * j + 1], False)
                hr_ref[rows, ls] = xr
                hi_ref[rows, ls] = xi
                new += [xr[7:8], xi[7:8]]
            return tuple(new)

        init = []
        for j in range(nl):
            ls = slice(j * SSM_LANES, (j + 1) * SSM_LANES)
            init += [sr_ref[:, ls], si_ref[:, ls]]
        last = lax.fori_loop(0, tc // 8, step, tuple(init), unroll=2)
        for j in range(nl):
            ls = slice(j * SSM_LANES, (j + 1) * SSM_LANES)
            sr_ref[:, ls] = last[2 * j]
            si_ref[:, ls] = last[2 * j + 1]
        for i in range(SSM_TILES):
            hcat = jnp.concatenate([hr_ref[:, i * ns:(i + 1) * ns], hi_ref[:, i * ns:(i + 1) * ns]], axis=1)
            ls = slice(i * 128, (i + 1) * 128)
            y = _dot(hcat, ct_ref[i], _NN) + d_ref[:, ls] * uf[:, ls]
            y_ref[:, ls] = y
            gl_ref[:, ls] = _gelu(y).astype(gl_ref.dtype)

    t = bsz * seq
    row = pl.BlockSpec((tc, D_MODEL), lambda b, c: (b * nc + c, 0))
    st = pl.BlockSpec((tc, SSM_STATES), lambda b, c: (b * nc + c, 0))
    diag = pl.BlockSpec((1, SSM_STATES), lambda b, c: (0, 0))
    return pl.pallas_call(
        body, name="ssm_fwd", grid=(bsz, nc),
        in_specs=[row, pl.BlockSpec((SSM_TILES, 128, 2 * ns), lambda b, c: (0, 0, 0)),
                  pl.BlockSpec((SSM_TILES, 2 * ns, 128), lambda b, c: (0, 0, 0)), diag, diag,
                  pl.BlockSpec((1, D_MODEL), lambda b, c: (0, 0))],
        out_specs=[row, row, st, st],
        out_shape=[jax.ShapeDtypeStruct((t, D_MODEL), F32), jax.ShapeDtypeStruct((t, D_MODEL), BF16),
                   jax.ShapeDtypeStruct((t, SSM_STATES), F32), jax.ShapeDtypeStruct((t, SSM_STATES), F32)],
        scratch_shapes=[pltpu.VMEM((1, SSM_STATES), F32)] * 2 + [pltpu.VMEM((8, SSM_STATES), F32)] * 8,
        compiler_params=_params(("arbitrary", "arbitrary")),
    )(u, wt, ct, a_re, a_im, dskip)


def ssm_bwd(dgl, y, u, h_re, h_im, wt, ct, a_re, a_im, dskip, bsz, seq):
    tc = _tile(seq, 256)
    nc = seq // tc
    nb = tc // 8
    ns = SSM_TILE_STATES
    nl = SSM_STATES // SSM_LANES

    def body(dgl_ref, y_ref, u_ref, hr_ref, hi_ref, pr_ref, pi_ref, wt_ref, ct_ref, ar_ref, ai_ref, d_ref,
             du_ref, dwt_ref, dct_ref, dd_ref, dar_ref, dai_ref, gr_ref, gi_ref, sr_ref, si_ref, ar8_ref, ai8_ref,
             *tabs):
        b, c = pl.program_id(0), pl.program_id(1)

        @pl.when((b == 0) & (c == 0))
        def _():
            for r in (dwt_ref, dct_ref, dd_ref, ar8_ref, ai8_ref):
                r[...] = jnp.zeros_like(r)
            _scan_tables(ar_ref[...], -ai_ref[...], True, tabs)

        @pl.when(c == 0)
        def _():
            sr_ref[...] = jnp.zeros_like(sr_ref)
            si_ref[...] = jnp.zeros_like(si_ref)

        uf = u_ref[...]
        dy = dgl_ref[...].astype(F32) * _gelu_grad(y_ref[...])
        dd_ref[...] += jnp.sum(dy * uf, axis=0, keepdims=True)
        for i in range(SSM_TILES):
            dyi = dy[:, i * 128:(i + 1) * 128]
            dh = _dot(dyi, ct_ref[i], _NT)
            gr_ref[:, i * ns:(i + 1) * ns] = dh[:, :ns]
            gi_ref[:, i * ns:(i + 1) * ns] = dh[:, ns:]
            hcat = jnp.concatenate([hr_ref[:, i * ns:(i + 1) * ns], hi_ref[:, i * ns:(i + 1) * ns]], axis=1)
            dct_ref[i] += _dot(hcat, dyi, _TN)
        row0 = lax.broadcasted_iota(jnp.int32, (8, SSM_LANES), 0) == 0

        def block(blk, carry, before):
            rows = _block8(blk)
            new = []
            for j in range(nl):
                ls = slice(j * SSM_LANES, (j + 1) * SSM_LANES)
                gr, gi = _scan8(gr_ref[rows, ls], gi_ref[rows, ls], tabs, ls, carry[2 * j], carry[2 * j + 1], True)
                gr_ref[rows, ls] = gr
                gi_ref[rows, ls] = gi
                bpr, bpi = before(j)
                hpr = jnp.where(row0, bpr, pltpu.roll(hr_ref[rows, ls], 1, 0))
                hpi = jnp.where(row0, bpi, pltpu.roll(hi_ref[rows, ls], 1, 0))
                ar8_ref[:, ls] += gr * hpr + gi * hpi
                ai8_ref[:, ls] += gi * hpr - gr * hpi
                new += [gr[0:1], gi[0:1]]
            return tuple(new)

        def step(jj, carry):
            blk = nb - 1 - jj
            prev_rows = _block8(blk - 1)

            def before(j):
                ls = slice(j * SSM_LANES, (j + 1) * SSM_LANES)
                return hr_ref[prev_rows, ls][7:8], hi_ref[prev_rows, ls][7:8]

            return block(blk, carry, before)

        init = []
        for j in range(nl):
            ls = slice(j * SSM_LANES, (j + 1) * SSM_LANES)
            init += [sr_ref[:, ls], si_ref[:, ls]]
        carry = lax.fori_loop(0, nb - 1, step, tuple(init))
        first = c == nc - 1

        def before_chunk(j):
            ls = slice(j * SSM_LANES, (j + 1) * SSM_LANES)
            return (jnp.where(first, 0.0, pr_ref[:, ls][7:8]), jnp.where(first, 0.0, pi_ref[:, ls][7:8]))

        last = block(0, carry, before_chunk)
        for j in range(nl):
            ls = slice(j * SSM_LANES, (j + 1) * SSM_LANES)
            sr_ref[:, ls] = last[2 * j]
            si_ref[:, ls] = last[2 * j + 1]
        for i in range(SSM_TILES):
            ls = slice(i * 128, (i + 1) * 128)
            gcat = jnp.concatenate([gr_ref[:, i * ns:(i + 1) * ns], gi_ref[:, i * ns:(i + 1) * ns]], axis=1)
            du_ref[:, ls] = (_dot(gcat, wt_ref[i], _NT) + d_ref[:, ls] * dy[:, ls]).astype(du_ref.dtype)
            dwt_ref[i] += _dot(uf[:, ls], gcat, _TN)

        @pl.when((b == bsz - 1) & (c == nc - 1))
        def _():
            dar_ref[...] = jnp.sum(ar8_ref[...], axis=0, keepdims=True)
            dai_ref[...] = jnp.sum(ai8_ref[...], axis=0, keepdims=True)

    t = bsz * seq
    rev = lambda b, c: (b * nc + (nc - 1 - c), 0)
    row = pl.BlockSpec((tc, D_MODEL), rev)
    st = pl.BlockSpec((tc, SSM_STATES), rev)
    prev = pl.BlockSpec((8, SSM_STATES), lambda b, c: (jnp.maximum((b * nc + (nc - 1 - c)) * nb - 1, 0), 0))
    diag = pl.BlockSpec((1, SSM_STATES), lambda b, c: (0, 0))
    wts = pl.BlockSpec((SSM_TILES, 128, 2 * ns), lambda b, c: (0, 0, 0))
    cts = pl.BlockSpec((SSM_TILES, 2 * ns, 128), lambda b, c: (0, 0, 0))
    vec = pl.BlockSpec((1, D_MODEL), lambda b, c: (0, 0))
    return pl.pallas_call(
        body, name="ssm_bwd", grid=(bsz, nc),
        in_specs=[row, row, row, st, st, prev, prev, wts, cts, diag, diag, vec],
        out_specs=[row, wts, cts, vec, diag, diag],
        out_shape=[jax.ShapeDtypeStruct((t, D_MODEL), BF16),
                   jax.ShapeDtypeStruct((SSM_TILES, 128, 2 * ns), F32),
                   jax.ShapeDtypeStruct((SSM_TILES, 2 * ns, 128), F32),
                   jax.ShapeDtypeStruct((1, D_MODEL), F32),
                   jax.ShapeDtypeStruct((1, SSM_STATES), F32), jax.ShapeDtypeStruct((1, SSM_STATES), F32)],
        scratch_shapes=[pltpu.VMEM((tc, SSM_STATES), F32)] * 2 + [pltpu.VMEM((1, SSM_STATES), F32)] * 2
                       + [pltpu.VMEM((8, SSM_STATES), F32)] * 10,
        compiler_params=_params(("arbitrary", "arbitrary")),
    )(dgl, y, u, h_re, h_im, h_re, h_im, wt, ct, a_re, a_im, dskip)


def _ssm_in_weights(bb_re2, bb_im2):
    eye = jnp.eye(8, dtype=F32)[None, :, None, :, None]

    def one(bb):
        t = bb.reshape(8, 8, 64, 16).transpose(0, 1, 3, 2)
        return (t[:, :, :, None, :] * eye).reshape(8, 128, 512)

    return jnp.concatenate([one(bb_re2), one(bb_im2)], axis=-1).astype(MXU_DTYPE)


def _ssm_in_weights_bwd(dwt):
    eye = jnp.eye(8, dtype=F32)[None, :, None, :, None]

    def one(d):
        t = (d.reshape(8, 8, 16, 8, 64) * eye).sum(axis=3)
        return t.transpose(0, 1, 3, 2).reshape(64, 1024)

    return one(dwt[..., :512]), one(dwt[..., 512:])


def _ssm_out_weights(c_re, c_im):
    eye = jnp.eye(8, dtype=F32)[None, :, None, :, None]

    def one(cc):
        t = cc.reshape(8, 8, 16, 64).transpose(0, 1, 3, 2)
        return (t[:, :, :, None, :] * eye).reshape(8, 512, 128)

    return jnp.concatenate([one(c_re), -one(c_im)], axis=1).astype(MXU_DTYPE)


def _ssm_out_weights_bwd(dct):
    eye = jnp.eye(8, dtype=F32)[None, :, None, :, None]

    def one(d):
        t = (d.reshape(8, 8, 64, 8, 16) * eye).sum(axis=3)
        return t.transpose(0, 1, 3, 2).reshape(64, 16, 64)

    return one(dct[:, :512]), -one(dct[:, 512:])


def _softmax(s):
    m = jnp.max(s, axis=-1, keepdims=True)
    e = jnp.exp(s - m)
    return e / jnp.sum(e, axis=-1, keepdims=True)


def xattn_fwd(q, kv, bsz, seq):
    tq = _tile(seq, 512)
    nq = seq // tq
    scale = XA_HEAD_DIM ** -0.5

    def body(q_ref, k_ref, v_ref, o_ref):
        s = lax.dot_general(q_ref[...], k_ref[...], _NT, preferred_element_type=F32) * scale
        p = _softmax(s)
        o_ref[...] = _dot(p, v_ref[...], _NN).astype(o_ref.dtype)

    qs = pl.BlockSpec((tq, XA_HEAD_DIM), lambda b, h, i: (b * nq + i, h))
    return pl.pallas_call(
        body, name="xattn_fwd", grid=(bsz, XA_HEADS, nq),
        in_specs=[qs, pl.BlockSpec((MEM_LEN, XA_HEAD_DIM), lambda b, h, i: (b, h)),
                  pl.BlockSpec((MEM_LEN, XA_HEAD_DIM), lambda b, h, i: (b, XA_HEADS + h))],
        out_specs=qs, out_shape=jax.ShapeDtypeStruct((bsz * seq, D_MODEL), BF16),
        compiler_params=_params(("parallel", "parallel", "parallel")),
    )(q, kv, kv)


def xattn_bwd(q, kv, do, bsz, seq):
    tq = _tile(seq, 512)
    nq = seq // tq
    scale = XA_HEAD_DIM ** -0.5

    def body(q_ref, k_ref, v_ref, do_ref, dq_ref, dk_ref, dv_ref):
        @pl.when(pl.program_id(2) == 0)
        def _():
            dk_ref[...] = jnp.zeros_like(dk_ref)
            dv_ref[...] = jnp.zeros_like(dv_ref)

        qv, kk, vv, dov = q_ref[...], k_ref[...], v_ref[...], do_ref[...]
        s = lax.dot_general(qv, kk, _NT, preferred_element_type=F32) * scale
        p = _softmax(s)
        dp = lax.dot_general(dov, vv, _NT, preferred_element_type=F32)
        ds = (p * (dp - jnp.sum(dp * p, axis=-1, keepdims=True)) * scale).astype(MXU_DTYPE)
        dq_ref[...] = lax.dot_general(ds, kk, _NN, preferred_element_type=F32).astype(dq_ref.dtype)
        dk_ref[...] += lax.dot_general(ds, qv, _TN, preferred_element_type=F32)
        dv_ref[...] += lax.dot_general(p.astype(MXU_DTYPE), dov, _TN, preferred_element_type=F32)

    qs = pl.BlockSpec((tq, XA_HEAD_DIM), lambda b, h, i: (b * nq + i, h))
    ks = pl.BlockSpec((MEM_LEN, XA_HEAD_DIM), lambda b, h, i: (b, h))
    vs = pl.BlockSpec((MEM_LEN, XA_HEAD_DIM), lambda b, h, i: (b, XA_HEADS + h))
    dkv = jax.ShapeDtypeStruct((bsz * MEM_LEN, D_MODEL), F32)
    dq, dk, dv = pl.pallas_call(
        body, name="xattn_bwd", grid=(bsz, XA_HEADS, nq),
        in_specs=[qs, ks, vs, qs], out_specs=[qs, ks, ks],
        out_shape=[jax.ShapeDtypeStruct((bsz * seq, D_MODEL), BF16), dkv, dkv],
        compiler_params=_params(("parallel", "parallel", "arbitrary")),
    )(q, kv, kv, do)
    return dq, dk, dv


CONV_HALO = 16


def _shifts_down(x, prev):
    h = prev.shape[0]
    ext = jnp.concatenate([prev, x], axis=0)
    return pltpu.roll(ext, 1, 0)[h:], pltpu.roll(ext, 2, 0)[h:]


def _shifts_up(x, nxt):
    rows = x.shape[0]
    n = rows + nxt.shape[0]
    ext = jnp.concatenate([x, nxt], axis=0)
    return pltpu.roll(ext, n - 1, 0)[:rows], pltpu.roll(ext, n - 2, 0)[:rows]


def _conv_taps(u, u1, u2, w, b):
    return b + w[2:3] * u + w[1:2] * u1 + w[0:1] * u2


def conv_fwd(up, cw, cb, bsz, seq):
    tc = _tile(seq, 512)
    nc = seq // tc
    hb = tc // CONV_HALO
    half = N_DEV // 2

    def body(uv_ref, ug_ref, pv_ref, pg_ref, wv_ref, wg_ref, bv_ref, bg_ref, o_ref):
        c = pl.program_id(2)
        pv = jnp.where(c > 0, pv_ref[...].astype(F32), 0.0)
        pg = jnp.where(c > 0, pg_ref[...].astype(F32), 0.0)
        uv, ug = uv_ref[...].astype(F32), ug_ref[...].astype(F32)
        val = _conv_taps(uv, *_shifts_down(uv, pv), wv_ref[...], bv_ref[...])
        gate = _conv_taps(ug, *_shifts_down(ug, pg), wg_ref[...], bg_ref[...])
        o_ref[...] = (gate * jax.nn.sigmoid(gate) * val).astype(o_ref.dtype)

    def cur(off):
        return pl.BlockSpec((None, tc, FF_SHARD), lambda b, j, c: (j + off, b * nc + c, 0))

    def prv(off):
        return pl.BlockSpec((None, CONV_HALO, FF_SHARD), lambda b, j, c: (j + off, jnp.maximum((b * nc + c) * hb - 1, 0), 0))

    def par(rows, off):
        return pl.BlockSpec((None, rows, FF_SHARD), lambda b, j, c: (j + off, 0, 0))

    return pl.pallas_call(
        body, name="conv_fwd", grid=(bsz, half, nc),
        in_specs=[cur(0), cur(half), prv(0), prv(half), par(3, 0), par(3, half), par(1, 0), par(1, half)],
        out_specs=cur(0), out_shape=jax.ShapeDtypeStruct((half, bsz * seq, FF_SHARD), BF16),
        compiler_params=_params(("parallel", "parallel", "parallel")),
    )(up, up, up, up, cw, cw, cb, cb)


def conv_bwd_taps(up, cw, cb, dact, bsz, seq):
    tc = _tile(seq, 512)
    nc = seq // tc
    hb = tc // CONV_HALO
    half = N_DEV // 2

    def body(uv_ref, ug_ref, pv_ref, pg_ref, wv_ref, wg_ref, bv_ref, bg_ref, da_ref,
             dc_ref, dwv_ref, dwg_ref, dbv_ref, dbg_ref):
        b, c = pl.program_id(1), pl.program_id(2)

        @pl.when((b == 0) & (c == 0))
        def _():
            for r in (dwv_ref, dwg_ref, dbv_ref, dbg_ref):
                r[...] = jnp.zeros_like(r)

        pv = jnp.where(c > 0, pv_ref[...].astype(F32), 0.0)
        pg = jnp.where(c > 0, pg_ref[...].astype(F32), 0.0)
        uv, ug = uv_ref[...].astype(F32), ug_ref[...].astype(F32)
        uv1, uv2 = _shifts_down(uv, pv)
        ug1, ug2 = _shifts_down(ug, pg)
        val = _conv_taps(uv, uv1, uv2, wv_ref[...], bv_ref[...])
        gate = _conv_taps(ug, ug1, ug2, wg_ref[...], bg_ref[...])
        sg = jax.nn.sigmoid(gate)
        da = da_ref[...].astype(F32)
        dsilu = da * sg
        dval = dsilu * gate
        dgate = dsilu * val * (1.0 + gate * (1.0 - sg))
        dc_ref[0] = dval.astype(dc_ref.dtype)
        dc_ref[1] = dgate.astype(dc_ref.dtype)
        for dcv, taps, dw_ref, db_ref in ((dval, (uv2, uv1, uv), dwv_ref, dbv_ref),
                                          (dgate, (ug2, ug1, ug), dwg_ref, dbg_ref)):
            db_ref[...] += jnp.sum(dcv, axis=0, keepdims=True)
            for k, u_k in enumerate(taps):
                dw_ref[k:k + 1, :] += jnp.sum(dcv * u_k, axis=0, keepdims=True)

    def cur(off):
        return pl.BlockSpec((None, tc, FF_SHARD), lambda j, b, c: (j + off, b * nc + c, 0))

    def prv(off):
        return pl.BlockSpec((None, CONV_HALO, FF_SHARD), lambda j, b, c: (j + off, jnp.maximum((b * nc + c) * hb - 1, 0), 0))

    def par(rows, off):
        return pl.BlockSpec((None, rows, FF_SHARD), lambda j, b, c: (j + off, 0, 0))

    t = bsz * seq
    hs = jax.ShapeDtypeStruct((2, half, t, FF_SHARD), BF16)
    ws = jax.ShapeDtypeStruct((half, 3, FF_SHARD), F32)
    bs = jax.ShapeDtypeStruct((half, 1, FF_SHARD), F32)
    dc, dwv, dwg, dbv, dbg = pl.pallas_call(
        body, name="conv_bwd_taps", grid=(half, bsz, nc),
        in_specs=[cur(0), cur(half), prv(0), prv(half), par(3, 0), par(3, half), par(1, 0), par(1, half), cur(0)],
        out_specs=[pl.BlockSpec((2, None, tc, FF_SHARD), lambda j, b, c: (0, j, b * nc + c, 0)),
                   par(3, 0), par(3, 0), par(1, 0), par(1, 0)],
        out_shape=[hs, ws, ws, bs, bs],
        compiler_params=_params(("parallel", "arbitrary", "arbitrary")),
    )(up, up, up, up, cw, cw, cb, cb, dact)
    return (dc.reshape(N_DEV, t, FF_SHARD), jnp.concatenate([dwv, dwg], axis=0),
            jnp.concatenate([dbv, dbg], axis=0))


def conv_bwd_input(dconv, cw, bsz, seq):
    tc = _tile(seq, 1024)
    nc = seq // tc
    hb = tc // CONV_HALO
    nblk = bsz * seq // CONV_HALO

    def body(d_ref, n_ref, w_ref, o_ref):
        c = pl.program_id(2)
        nxt = jnp.where(c < nc - 1, n_ref[...].astype(F32), 0.0)
        d = d_ref[...].astype(F32)
        d1, d2 = _shifts_up(d, nxt)
        w = w_ref[...]
        o_ref[...] = (w[2:3] * d + w[1:2] * d1 + w[0:1] * d2).astype(o_ref.dtype)

    cur = pl.BlockSpec((None, tc, FF_SHARD), lambda j, b, c: (j, b * nc + c, 0))
    return pl.pallas_call(
        body, name="conv_bwd_input", grid=(N_DEV, bsz, nc),
        in_specs=[cur, pl.BlockSpec((None, CONV_HALO, FF_SHARD),
                                    lambda j, b, c: (j, jnp.minimum((b * nc + c + 1) * hb, nblk - 1), 0)),
                  pl.BlockSpec((None, 3, FF_SHARD), lambda j, b, c: (j, 0, 0))],
        out_specs=cur, out_shape=jax.ShapeDtypeStruct(dconv.shape, BF16),
        compiler_params=_params(("parallel", "parallel", "parallel")),
    )(dconv, dconv, cw)


def _my_index():
    return 4 * lax.axis_index("x") + 2 * lax.axis_index("y") + lax.axis_index("c")


def _peer(k):
    return (lax.axis_index("x") ^ ((k >> 2) & 1), lax.axis_index("y") ^ ((k >> 1) & 1),
            lax.axis_index("c") ^ (k & 1))


_HBM = pl.BlockSpec(memory_space=pltpu.HBM)
_SEM = pl.BlockSpec(memory_space=pltpu.SEMAPHORE)
_DATAFLOW = pltpu.SideEffectType.DATAFLOW_SIDE_EFFECTING


def _split_copies(gather, src_ref, land_ref, send_sems, recv_sems, local_sem):
    me = _my_index()

    def part(j):
        return src_ref if gather else src_ref.at[j]

    local = pltpu.make_async_copy(part(me), land_ref.at[me], local_sem)
    sends = [pltpu.make_async_remote_copy(
        src_ref=part(me ^ k), dst_ref=land_ref.at[me], send_sem=send_sems.at[k - 1], recv_sem=recv_sems.at[k - 1],
        device_id=_peer(k), device_id_type=pl.DeviceIdType.MESH) for k in range(1, N_DEV)]
    recvs = [pltpu.make_async_remote_copy(
        src_ref=part(me ^ k), dst_ref=land_ref.at[me ^ k], send_sem=send_sems.at[k - 1], recv_sem=recv_sems.at[k - 1],
        device_id=_peer(k), device_id_type=pl.DeviceIdType.MESH) for k in range(1, N_DEV)]
    return local, sends, recvs


def split_start(name, src, gather):
    land_shape = ((N_DEV,) + src.shape) if gather else src.shape

    def body(src_ref, land_ref, send_sems, recv_sems, local_sem, src_thru, land_thru, token):
        local, sends, _ = _split_copies(gather, src_ref, land_ref, send_sems, recv_sems, local_sem)
        local.start()
        for cp in sends:
            cp.start()
        token[...] = jnp.zeros_like(token)

    dma7 = pltpu.SemaphoreType.DMA((N_DEV - 1,))
    out = pl.pallas_call(
        body, name=name,
        out_shape=(dma7, dma7, pltpu.SemaphoreType.DMA(()), pltpu.HBM(src.shape, src.dtype),
                   pltpu.HBM(land_shape, src.dtype), jax.ShapeDtypeStruct((8, 128), F32)),
        in_specs=(_HBM, _HBM), out_specs=(_SEM, _SEM, _SEM, _HBM, _HBM, pl.BlockSpec(memory_space=pltpu.VMEM)),
        input_output_aliases={0: 3, 1: 4},
        compiler_params=pltpu.CompilerParams(has_side_effects=_DATAFLOW),
    )(pltpu.with_memory_space_constraint(src, pltpu.HBM),
      pltpu.with_memory_space_constraint(lax.empty(land_shape, src.dtype), pltpu.HBM))
    return out[:5], out[5][0, 0]


def split_wait(name, handles, after, gather):
    send_sems, recv_sems, local_sem, src_thru, land_thru = handles

    def body(src_ref, land_ref, send_sems, recv_sems, local_sem, after_ref, src_dead, got_ref, token):
        local, sends, recvs = _split_copies(gather, src_ref, land_ref, send_sems, recv_sems, local_sem)
        local.wait()
        for cp in recvs:
            cp.wait_send()
            cp.wait_recv()
        token[...] = jnp.zeros_like(token)

    out = pl.pallas_call(
        body, name=name,
        out_shape=(pltpu.HBM(src_thru.shape, src_thru.dtype), pltpu.HBM(land_thru.shape, land_thru.dtype),
                   jax.ShapeDtypeStruct((8, 128), F32)),
        in_specs=(_HBM, _HBM, _SEM, _SEM, _SEM, pl.BlockSpec(memory_space=pl.ANY)),
        out_specs=(_HBM, _HBM, pl.BlockSpec(memory_space=pltpu.VMEM)),
        input_output_aliases={0: 0, 1: 1},
        compiler_params=pltpu.CompilerParams(has_side_effects=_DATAFLOW),
    )(src_thru, land_thru, send_sems, recv_sems, local_sem, after)
    return out[1], out[2][0, 0]


def sum_parts(name, r):
    _, rows, cols = r.shape

    def body(r_ref, o_ref):
        acc = r_ref[0].astype(F32)
        for s in range(1, N_DEV):
            acc = acc + r_ref[s].astype(F32)
        o_ref[...] = acc

    return pl.pallas_call(body, name=name, out_shape=jax.ShapeDtypeStruct((rows, cols), F32),
                          compiler_params=_params())(r)


def adamw(name, w, m, v, parts=None, g=None, layer=0, into=None, order=None):
    _, rows, cols = w.shape
    br = _tile(rows, 256, 16)
    c1 = 1.0 / (1.0 - ADAM_B1 ** ADAM_STEP)
    c2 = 1.0 / (1.0 - ADAM_B2 ** ADAM_STEP)

    def body(g_ref, w_ref, m_ref, v_ref, *rest):
        og_ref, od_ref, om_ref, ov_ref = rest[-4:]
        if parts is None:
            gs = g_ref[...]
        else:
            gs = g_ref[0].astype(F32)
            for s in range(1, N_DEV):
                gs = gs + g_ref[s].astype(F32)
        mn = ADAM_B1 * m_ref[...] + (1.0 - ADAM_B1) * gs
        vn = ADAM_B2 * v_ref[...] + (1.0 - ADAM_B2) * (gs * gs)
        og_ref[...] = gs
        om_ref[...] = mn
        ov_ref[...] = vn
        od_ref[...] = -ADAM_LR * ((mn * c1) / (jnp.sqrt(vn * c2) + ADAM_EPS) + ADAM_WD * w_ref[...])

    blk = pl.BlockSpec((None, br, cols), lambda i: (layer, i, 0))
    if parts is None:
        gspec = pl.BlockSpec((br, cols), lambda i: (i, 0))
    else:
        gspec = pl.BlockSpec((N_DEV, br, cols), lambda i: (0, i, 0))
    earlier = [] if into is None else list(into)
    behind = [] if order is None else [order]
    return pl.pallas_call(
        body, name=name, grid=(rows // br,),
        in_specs=[gspec, blk, blk, blk] + [pl.BlockSpec(memory_space=pl.ANY)] * len(earlier)
                 + [pl.BlockSpec((1, 128), lambda i: (0, 0))] * len(behind),
        out_specs=[blk] * 4, out_shape=[jax.ShapeDtypeStruct(w.shape, F32)] * 4,
        input_output_aliases={4 + k: k for k in range(len(earlier))},
        compiler_params=_params(("parallel",)),
    )(g if parts is None else parts, w, m, v, *earlier, *behind)


SMALL = ("norm_mix", "norm_xattn", "norm_ffn", "norm_mem", "norm_final", "pool_w", "pool_scale",
         "ssm_lam_re", "ssm_lam_im", "ssm_log_dt", "ssm_b_re", "ssm_b_im", "ssm_c_re", "ssm_c_im",
         "ffn_conv_b", "ssm_d", "ffn_conv_w")
SMALL_SHARDED = {"ssm_d": 1, "ffn_conv_w": 2}
BIG = ("ab_w_in", "ab_w_out", "ssm_w_in", "ssm_w_glu", "xa_w_q", "xa_w_kv", "xa_w_o", "ffn_w_up", "ffn_w_down")
WEIGHTS = ("norm_mix", "norm_xattn", "norm_ffn", "norm_mem", "norm_final", "ab_w_in", "pool_w", "pool_scale",
           "ab_w_out", "ssm_w_in", "ssm_lam_re", "ssm_lam_im", "ssm_log_dt", "ssm_b_re", "ssm_b_im", "ssm_c_re",
           "ssm_c_im", "ssm_d", "ssm_w_glu", "xa_w_q", "xa_w_kv", "xa_w_o", "ffn_w_up", "ffn_conv_w", "ffn_conv_b",
           "ffn_w_down")


def _rows8(g):
    return g.reshape(N_DEV, g.size // (N_DEV * D_MODEL), D_MODEL)


def _square(a):
    return a.reshape(D_MODEL, D_MODEL)


_LAYOUT = {"ab_w_out": _square, "ssm_w_in": _square, "xa_w_q": _square, "xa_w_o": _square,
           "ffn_w_down": lambda a: a.reshape(N_DEV // 2, FF_SHARD, D_MODEL)}
GATHER_ORDER = (("ab_w_in", 0), ("ffn_conv_w", None), ("ssm_d", None), ("ab_w_out", 0), ("xa_w_q", 0),
                ("xa_w_kv", 0), ("xa_w_o", 0), ("ffn_w_up", 0), ("ffn_w_down", 0), ("ffn_w_up", 1),
                ("ffn_w_down", 1), ("ssm_w_in", 0), ("ssm_w_glu", 0), ("xa_w_q", 1), ("xa_w_kv", 1), ("xa_w_o", 1))
GATHER_FIRST = 3
GATHER_AHEAD = 7


class _Step:
    def __init__(self, master, small):
        self.master, self.small = master, small
        self.pending, self.gathers, self.weights, self.sent = [], {}, {}, []

    def follow(self, v):
        for z in self.pending:
            v = v + z
        self.pending = []
        return v

    def start_gathers(self, upto, zero):
        for n, l in GATHER_ORDER[len(self.gathers):upto]:
            if l is None:
                shard = self.master[n] + zero
            else:
                shard = (self.master[n][l] + zero).astype(MXU_DTYPE)
            self.gathers[(n, l)], z = split_start(f"ags_{n}{'' if l is None else l}", shard, gather=True)
            self.pending.append(z)

    def weight(self, n, l, after):
        if (n, l) not in self.weights:
            full, z = split_wait(f"agw_{n}{'' if l is None else l}", self.gathers[(n, l)], after, gather=True)
            self.weights[(n, l)] = _LAYOUT.get(n, lambda a: a)(full)
            self.start_gathers(GATHER_ORDER.index((n, l)) + 1 + GATHER_AHEAD, z)
        return self.weights[(n, l)]

    def send_grad(self, n, l, part):
        h, z = split_start(f"xs_{n}{l}", part, gather=False)
        self.pending.append(z)
        self.sent.append((n, l, h))


def _layer_tail(st, l, x_in, hq, mem_n, acts, next_gain=None):
    bsz, seq = acts["bsz"], acts["seq"]
    p = st.small
    q = mm_nn(f"xa_q{l}", hq, st.weight("xa_w_q", l, x_in))
    kv = mm_nn_bs(f"xa_kv{l}", mem_n, st.weight("xa_w_kv", l, x_in))
    o = xattn_fwd(q, kv, bsz, seq)
    x_mid, hf = mm_nn(f"xa_o{l}", o, st.weight("xa_w_o", l, o), res=x_in, out_dtype=F32,
                      norm_gain=st.follow(p["norm_ffn"][l]))
    up = mm_nn_bs(f"ffn_up{l}", hf, st.weight("ffn_w_up", l, x_mid), stacked_out=True)
    conv_w = st.weight("ffn_conv_w", None, x_mid)[:, l]
    act = conv_fwd(up, conv_w, p["ffn_conv_b"][l], bsz, seq)
    w_down = st.weight("ffn_w_down", l, act)
    if next_gain is None:
        x_out, h_next = mm_as_nn(f"ffn_down{l}", act, w_down, res=x_mid), None
    else:
        x_out, h_next = mm_as_nn(f"ffn_down{l}", act, w_down, res=x_mid, norm_gain=st.follow(next_gain))
    acts[l].update(x_in=x_in, hq=hq, q=q, kv=kv, o=o, x_mid=x_mid, hf=hf, up=up, act=act)
    return x_out, h_next


def _layer_tail_bwd(st, l, dx, mem_n, acts, grads):
    a = acts[l]
    bsz, seq = acts["bsz"], acts["seq"]
    p = st.small
    dact = mm_nt_os(f"d_act{l}", dx, st.weight("ffn_w_down", l, dx))
    st.send_grad("ffn_w_down", l, _rows8(mm_tn(f"g_ffn_down{l}", a["act"], dx, a_stacked=True)))
    conv_w = st.weight("ffn_conv_w", None, dx)[:, l]
    dconv, dcw, dcb = conv_bwd_taps(a["up"], conv_w, p["ffn_conv_b"][l], dact, bsz, seq)
    grads["ffn_conv_w"][l] = dcw
    grads["ffn_conv_b"][l] = dcb
    dup = conv_bwd_input(dconv, conv_w, bsz, seq)
    dx_mid, grads["norm_ffn"][l] = mm_nt_bs(f"d_hf{l}", dup, st.weight("ffn_w_up", l, dx), dc_stacked=True,
                                            rms=(a["x_mid"], st.follow(p["norm_ffn"][l]), dx))
    st.send_grad("ffn_w_up", l, mm_tn(f"g_ffn_up{l}", a["hf"], dup, dc_stacked=True))
    do = mm_nt(f"d_o{l}", dx_mid, st.weight("xa_w_o", l, dx))
    st.send_grad("xa_w_o", l, _rows8(mm_tn(f"g_xa_o{l}", a["o"], dx_mid)))
    dq, dk, dv = xattn_bwd(a["q"], a["kv"], do, bsz, seq)
    dkv = jnp.concatenate([dk, dv], axis=1).astype(BF16)
    dx_in, grads["norm_xattn"][l] = mm_nt(f"d_hq{l}", dq, st.weight("xa_w_q", l, dx),
                                          rms=(a["x_in"], st.follow(p["norm_xattn"][l]), dx_mid))
    st.send_grad("xa_w_q", l, _rows8(mm_tn(f"g_xa_q{l}", a["hq"], dq)))
    dmem_n = mm_nt_bs(f"d_memn{l}", dkv, st.weight("xa_w_kv", l, dx), out_dtype=F32)
    st.send_grad("xa_w_kv", l, mm_tn(f"g_xa_kv{l}", mem_n, dkv, dc_cols=2 * D_MODEL // N_DEV))
    return dx_in, dmem_n


def kernel(x, mem, norm_mix, norm_xattn, norm_ffn, norm_mem, norm_final, ab_w_in, pool_w, pool_scale, ab_w_out, ssm_w_in, ssm_lam_re, ssm_lam_im, ssm_log_dt, ssm_b_re, ssm_b_im, ssm_c_re, ssm_c_im, ssm_d, ssm_w_glu, xa_w_q, xa_w_kv, xa_w_o, ffn_w_up, ffn_conv_w, ffn_conv_b, ffn_w_down, loss_target, m_norm_mix, m_norm_xattn, m_norm_ffn, m_norm_mem, m_norm_final, m_ab_w_in, m_pool_w, m_pool_scale, m_ab_w_out, m_ssm_w_in, m_ssm_lam_re, m_ssm_lam_im, m_ssm_log_dt, m_ssm_b_re, m_ssm_b_im, m_ssm_c_re, m_ssm_c_im, m_ssm_d, m_ssm_w_glu, m_xa_w_q, m_xa_w_kv, m_xa_w_o, m_ffn_w_up, m_ffn_conv_w, m_ffn_conv_b, m_ffn_w_down, v_norm_mix, v_norm_xattn, v_norm_ffn, v_norm_mem, v_norm_final, v_ab_w_in, v_pool_w, v_pool_scale, v_ab_w_out, v_ssm_w_in, v_ssm_lam_re, v_ssm_lam_im, v_ssm_log_dt, v_ssm_b_re, v_ssm_b_im, v_ssm_c_re, v_ssm_c_im, v_ssm_d, v_ssm_w_glu, v_xa_w_q, v_xa_w_kv, v_xa_w_o, v_ffn_w_up, v_ffn_conv_w, v_ffn_conv_b, v_ffn_w_down):
    given = dict(locals())
    master = {n: given[n] for n in WEIGHTS}
    mom1 = {n: given["m_" + n] for n in WEIGHTS}
    mom2 = {n: given["v_" + n] for n in WEIGHTS}
    bsz, seq, d = x.shape
    t = bsz * seq
    me = _my_index()

    st = _Step(master, {"norm_xattn": norm_xattn, "norm_ffn": norm_ffn,
                        "ffn_conv_b": [ffn_conv_b[l].reshape(N_DEV, 1, FF_SHARD) for l in range(2)]})
    st.start_gathers(GATHER_FIRST, 0.0)
    zero = st.follow(jnp.zeros((), F32))

    acts = {"bsz": bsz, "seq": seq, 0: {}, 1: {}}
    x0 = x.reshape(t, d)
    mem2 = mem.reshape(bsz * MEM_LEN, d)
    mem_n = rms_fwd("rms_mem", mem2, norm_mem + zero)
    pscale = pool_scale.reshape(1, SB_WIDTH)

    h0 = rms_fwd("rms_mix0", x0, norm_mix[0] + zero)
    w_in = st.weight("ab_w_in", 0, h0)
    proj = mm_nn_bs("ab_in", h0, w_in, out_dtype=F32)
    a_out, rsum = sb_attn_fwd(proj, st.follow(jnp.zeros((1, 128), F32)), bsz, seq)
    p_out = pool_fwd(proj, pool_w[0], pscale, bsz, seq)
    w_out = st.weight("ab_w_out", 0, a_out)
    x1 = mm_nn("ab_out_a", a_out, w_out, res=x0, out_dtype=F32)
    x1, hq0 = mm_nn("ab_out_p", p_out, w_out, res=x1, koff=SB_WIDTH, out_dtype=F32,
                    norm_gain=st.follow(norm_xattn[0]))
    x3, h1 = _layer_tail(st, 0, x1, hq0, mem_n, acts, next_gain=norm_mix[1])

    b_re2 = ssm_b_re.reshape(64, 1024)
    b_im2 = ssm_b_im.reshape(64, 1024)
    log_dt = ssm_log_dt.reshape(64, 1)
    lb_re, lb_im, bb_re2, bb_im2 = ssm_prep(ssm_lam_re[0], ssm_lam_im[0], log_dt, b_re2, b_im2)
    wt = _ssm_in_weights(bb_re2, bb_im2)
    ct = _ssm_out_weights(ssm_c_re[0], ssm_c_im[0])
    a_re = lb_re.reshape(1, SSM_STATES)
    a_im = lb_im.reshape(1, SSM_STATES)
    u = mm_nn("ssm_in", h1, st.weight("ssm_w_in", 0, x3), out_dtype=F32)
    dskip = st.weight("ssm_d", None, x3).reshape(1, D_MODEL)
    y, gl, h_re, h_im = ssm_fwd(u, wt, ct, a_re, a_im, dskip, bsz, seq)
    glu = mm_nn_bs("ssm_glu", gl, st.weight("ssm_w_glu", 0, gl), out_dtype=F32)
    x4, hq1 = glu_fwd(glu, x3, st.follow(norm_xattn[1]))
    x6, _ = _layer_tail(st, 1, x4, hq1, mem_n, acts)

    loss_row, dx, g_norm_final = loss_head(x6, norm_final, loss_target.reshape(t, d))
    loss = lax.psum(loss_row[0, 0], MESH_AXES)

    grads = {n: [None, None] for n in ("ffn_conv_w", "ffn_conv_b", "norm_ffn", "norm_xattn", "norm_mix")}
    dx4, dmem_1 = _layer_tail_bwd(st, 1, dx, mem_n, acts, grads)
    dglu = glu_bwd(glu, dx4)
    dgl = mm_nt_bs("d_gl", dglu, st.weight("ssm_w_glu", 0, dx))
    st.send_grad("ssm_w_glu", 0, mm_tn("g_ssm_glu", gl, dglu, dc_cols=2 * D_MODEL // N_DEV))
    du, dwt, dct, g_dskip, da_re, da_im = ssm_bwd(dgl, y, u, h_re, h_im, wt, ct, a_re, a_im, dskip, bsz, seq)
    dbb_re, dbb_im = _ssm_in_weights_bwd(dwt)
    g_c_re, g_c_im = _ssm_out_weights_bwd(dct)
    g_lam_re, g_lam_im, g_log_dt, g_b_re, g_b_im = ssm_prep_bwd(
        ssm_lam_re[0], ssm_lam_im[0], log_dt, b_re2, b_im2, da_re.reshape(64, 64), da_im.reshape(64, 64),
        dbb_re, dbb_im)
    dx3, grads["norm_mix"][1] = mm_nt("d_h1", du, st.weight("ssm_w_in", 0, dx),
                                      rms=(x3, st.follow(norm_mix[1]), dx4))
    st.send_grad("ssm_w_in", 0, _rows8(mm_tn("g_ssm_in", h1, du)))

    dx1, dmem_0 = _layer_tail_bwd(st, 0, dx3, mem_n, acts, grads)
    dcat = mm_nt("d_cat", dx1, st.weight("ab_w_out", 0, dx))
    st.send_grad("ab_w_out", 0, _rows8(jnp.concatenate(
        [mm_tn("g_ab_out_a", a_out, dx1), mm_tn("g_ab_out_p", p_out, dx1)], axis=0)))
    dq, dk, dv = sb_attn_bwd(proj, rsum, dcat, bsz, seq)
    dpu, g_pool_w, g_pool_scale = pool_bwd(proj, pool_w[0], st.follow(pscale), dcat, bsz, seq)
    dproj = jnp.concatenate([dq, dk, dv, dpu], axis=1).astype(BF16)
    st.send_grad("ab_w_in", 0, mm_tn("g_ab_in", h0, dproj, dc_cols=2 * D_MODEL // N_DEV))
    dx0, grads["norm_mix"][0] = mm_nt_bs("d_h0", dproj, st.weight("ab_w_in", 0, dx),
                                         rms=(x0, st.follow(norm_mix[0]), dx1))
    _, g_norm_mem = rms_bwd("rms_mem_bwd", mem2, norm_mem, dmem_0 + dmem_1, need_dx=False)

    small_g = {
        "norm_mix": jnp.stack([g[0] for g in grads["norm_mix"]]),
        "norm_xattn": jnp.stack([g[0] for g in grads["norm_xattn"]]),
        "norm_ffn": jnp.stack([g[0] for g in grads["norm_ffn"]]),
        "norm_mem": g_norm_mem[0], "norm_final": g_norm_final[0],
        "pool_w": g_pool_w[None], "pool_scale": g_pool_scale,
        "ssm_lam_re": g_lam_re[None], "ssm_lam_im": g_lam_im[None], "ssm_log_dt": g_log_dt.reshape(1, 64),
        "ssm_b_re": g_b_re.reshape(1, 64, 64, 16), "ssm_b_im": g_b_im.reshape(1, 64, 64, 16),
        "ssm_c_re": g_c_re[None], "ssm_c_im": g_c_im[None],
        "ffn_conv_b": jnp.stack([g.reshape(2 * D_FF) for g in grads["ffn_conv_b"]]),
        "ssm_d": g_dskip,
        "ffn_conv_w": jnp.stack([g.transpose(1, 0, 2).reshape(3, 2 * D_FF) for g in grads["ffn_conv_w"]]),
    }
    sizes = [int(small_g[n].size) for n in SMALL]
    total = sum(sizes)
    rows8 = -(-total // (N_DEV * 128 * 8)) * 8
    flat = jnp.concatenate([small_g[n].reshape(-1).astype(F32) for n in SMALL]
                           + [jnp.zeros((N_DEV * rows8 * 128 - total,), F32)])
    in_flight, z = split_start("xs_small", flat.reshape(N_DEV, rows8, 128), gather=False)
    st.pending.append(z)
    stepped, last = {}, dx0
    for i, (n, l, handles) in enumerate(st.sent):
        if i == len(st.sent) // 2:
            recv, _ = split_wait("xw_small", in_flight, last, gather=False)
            in_flight, z = split_start("ags_small", sum_parts("sum_small", recv), gather=True)
            st.pending.append(z)
        recv, _ = split_wait(f"xw_{n}{l}", handles, dx0, gather=False)
        shape3 = (master[n].shape[0],) + recv.shape[1:]
        stepped[n] = adamw(f"adamw_{n}{l}", master[n].reshape(shape3), mom1[n].reshape(shape3),
                           mom2[n].reshape(shape3), parts=recv, layer=l, into=stepped.get(n),
                           order=st.follow(jnp.zeros((1, 128), F32)))
        last = stepped[n][0]
    out_g, out_d, out_m, out_v = ({n: stepped[n][k].reshape(master[n].shape) for n in BIG} for k in range(4))
    summed = split_wait("agw_small", in_flight, last, gather=True)[0].reshape(-1)

    def local_part(name, a):
        ax = SMALL_SHARDED.get(name)
        if ax is None:
            return a
        n_loc = a.shape[ax] // N_DEV
        return lax.dynamic_slice_in_dim(a, me * n_loc, n_loc, axis=ax)

    off = 0
    for n, sz in zip(SMALL, sizes):
        g_n = local_part(n, summed[off:off + sz].reshape(small_g[n].shape))
        off += sz
        cols = g_n.shape[-1] if g_n.shape[-1] >= 128 or g_n.ndim < 3 else g_n.shape[-1] * g_n.shape[-2]
        shape3 = (1, g_n.size // cols, cols)
        res = adamw("adamw_" + n, master[n].reshape(shape3), mom1[n].reshape(shape3), mom2[n].reshape(shape3),
                    g=g_n.reshape(shape3[1:]))
        for dst, r in zip((out_g, out_d, out_m, out_v), res):
            dst[n] = r.reshape(master[n].shape)

    return (loss, dx0.reshape(bsz, seq, d), *[out_g[n] for n in WEIGHTS], *[out_d[n] for n in WEIGHTS],
            *[out_m[n] for n in WEIGHTS], *[out_v[n] for n in WEIGHTS])
```

```python
import math

import jax
import jax.numpy as jnp
from jax import lax
from jax.experimental import pallas as pl
from jax.experimental.pallas import tpu as pltpu

F32 = jnp.float32
BF16 = jnp.bfloat16
MXU_DTYPE = jnp.bfloat16
N_DEV = 8
MESH_AXES = ("x", "y", "c")

D_MODEL = 1024
SB_HEAD_DIM = 64
SB_WIDTH = 512
SB_BLOCK = 256
POOL_WINDOWS = (2, 4, 8, 16)
POOL_GROUP = 128
POOL_HALO = 16
SSM_TILES = 8
SSM_TILE_STATES = 512
SSM_STATES = 4096
SSM_LANES = 1024
MEM_LEN = 256
XA_HEADS = 4
XA_HEAD_DIM = 256
D_FF = 2816
FF_SHARD = 704
EPS = 1e-6
ADAM_LR = 0.001
ADAM_B1 = 0.9
ADAM_B2 = 0.999
ADAM_EPS = 1e-08
ADAM_WD = 0.01
ADAM_STEP = 10
VMEM_LIMIT = 56 * 1024 * 1024

_NN = (((1,), (0,)), ((), ()))
_NT = (((1,), (1,)), ((), ()))
_TN = (((0,), (0,)), ((), ()))


def _params(sem=None):
    if sem is None:
        return pltpu.CompilerParams(vmem_limit_bytes=VMEM_LIMIT)
    return pltpu.CompilerParams(dimension_semantics=sem, vmem_limit_bytes=VMEM_LIMIT)


def _tile(n, pref, mult=8):
    if n <= pref:
        return n
    for t in range(pref, 0, -1):
        if n % t == 0 and t % mult == 0:
            return t
    return n


def _dot(a, b, dims):
    return lax.dot_general(a.astype(MXU_DTYPE), b.astype(MXU_DTYPE), dims, preferred_element_type=F32)


def _dot_exact01(x, m01, dims=_NN):
    x1 = x.astype(BF16)
    r1 = x - x1.astype(F32)
    x2 = r1.astype(BF16)
    x3 = (r1 - x2.astype(F32)).astype(BF16)
    m = m01.astype(BF16)
    out = lax.dot_general(x1, m, dims, preferred_element_type=F32)
    out = out + lax.dot_general(x2, m, dims, preferred_element_type=F32)
    return out + lax.dot_general(x3, m, dims, preferred_element_type=F32)


def _mm(name, a, b, dims, grid, a_spec, b_spec, o_spec, out_shape, out_dtype, acc_shape, res=None, r_spec=None,
        group=1, n=None, a_sel="full", b_sel="full", o_sel="full", norm_gain=None, rms=None):
    nk = grid[2]
    if out_dtype is None:
        out_dtype = BF16
    n_out = out_shape[-1]
    vec = pl.BlockSpec((1, n_out), lambda i, j, kk: (0, 0))

    def at(sel, s):
        if sel == "lead":
            return (s,)
        if sel == "lanes":
            return (slice(None), slice(s * n, (s + 1) * n))
        return (Ellipsis,)

    extra = [] if res is None else [(res, r_spec)]
    if norm_gain is not None:
        extra.append((norm_gain.reshape(1, n_out), vec))
    if rms is not None:
        extra += [(rms[0], o_spec), (rms[1].reshape(1, n_out), vec), (rms[2], o_spec)]
    n_in = 2 + len(extra)
    if rms is not None:
        out_specs = [o_spec, vec]
        out_shapes = [jax.ShapeDtypeStruct(out_shape, F32), jax.ShapeDtypeStruct((1, n_out), F32)]
    elif norm_gain is not None:
        out_specs = [o_spec, o_spec]
        out_shapes = [jax.ShapeDtypeStruct(out_shape, out_dtype), jax.ShapeDtypeStruct(out_shape, BF16)]
    else:
        out_specs, out_shapes = o_spec, jax.ShapeDtypeStruct(out_shape, out_dtype)

    def body(*refs):
        a_ref, b_ref = refs[0], refs[1]
        ins = list(refs[2:n_in])
        r_ref = ins.pop(0) if res is not None else None
        outs = refs[n_in:]
        o_ref = outs[0]
        acc = refs[-1] if nk > 1 else None
        k = pl.program_id(2)

        def finish(val):
            if r_ref is not None:
                val = val + r_ref[...].astype(F32)
            if rms is not None:
                x_ref, g_ref, d_ref = ins
                xf = x_ref[...]
                r = lax.rsqrt(jnp.mean(xf * xf, axis=-1, keepdims=True) + EPS)
                xh = xf * r
                part = jnp.sum(val * xh, axis=0, keepdims=True)
                first = pl.program_id(0) == 0

                @pl.when(first)
                def _():
                    outs[1][...] = part

                @pl.when(jnp.logical_not(first))
                def _():
                    outs[1][...] += part

                dxh = val * g_ref[...]
                o_ref[...] = d_ref[...] + r * (dxh - xh * jnp.mean(dxh * xh, axis=-1, keepdims=True))
                return
            o_ref[...] = val.astype(out_dtype)
            if norm_gain is not None:
                r = lax.rsqrt(jnp.mean(val * val, axis=-1, keepdims=True) + EPS)
                outs[1][...] = (val * r * ins[0][...]).astype(BF16)

        def emit(s, val):
            if nk == 1:
                if o_sel == "full":
                    finish(val)
                else:
                    o_ref[at(o_sel, s)] = val.astype(out_dtype)
                return

            @pl.when(k == 0)
            def _():
                acc[at(o_sel, s)] = val

            @pl.when(k > 0)
            def _():
                acc[at(o_sel, s)] += val

        total = None
        if a_sel == "full" and b_sel == "lanes":
            wide = _dot(a_ref[...], b_ref[...], dims)
            for s in range(group):
                emit(s, wide[:, s * n:(s + 1) * n])
        else:
            for s in range(group):
                val = _dot(a_ref[at(a_sel, s)], b_ref[at(b_sel, s)], dims)
                if o_sel == "full":
                    total = val if total is None else total + val
                else:
                    emit(s, val)
        if o_sel == "full":
            emit(0, total)
        if nk > 1:
            @pl.when(k == nk - 1)
            def _():
                if o_sel == "full":
                    finish(acc[...])
                else:
                    o_ref[...] = acc[...].astype(out_dtype)

    rows_sem = "arbitrary" if rms is not None else "parallel"
    return pl.pallas_call(
        body, name=name, grid=grid, in_specs=[a_spec, b_spec] + [s for _, s in extra], out_specs=out_specs,
        out_shape=out_shapes, scratch_shapes=[pltpu.VMEM(acc_shape, F32)] if nk > 1 else [],
        compiler_params=_params((rows_sem, rows_sem, "arbitrary")),
    )(a, b, *[x for x, _ in extra])


def _row_tile(m, epi):
    return _tile(m, 512 if epi.get("rms") is not None else 1024)


def mm_nn(name, a, b, res=None, koff=0, out_dtype=None, **epi):
    m, k = a.shape
    n = b.shape[1]
    tm, tn, tk = _row_tile(m, epi), _tile(n, 1024, 128), _tile(k, 1024, 128)
    kb = koff // tk
    spec = pl.BlockSpec((tm, tn), lambda i, j, kk: (i, j))
    return _mm(name, a, b, _NN, (m // tm, n // tn, k // tk),
               pl.BlockSpec((tm, tk), lambda i, j, kk: (i, kk)),
               pl.BlockSpec((tk, tn), lambda i, j, kk: (kk + kb, j)),
               spec, (m, n), out_dtype, (tm, tn), res, spec, **epi)


def mm_nn_bs(name, a, bs, stacked_out=False, out_dtype=None):
    m, k = a.shape
    s, _, n = bs.shape
    tm, tk = _tile(m, 1024), _tile(k, 1024, 128)
    a_spec = pl.BlockSpec((tm, tk), lambda i, j, kk: (i, kk))
    if stacked_out:
        return _mm(name, a, bs, _NN, (m // tm, s, k // tk), a_spec,
                   pl.BlockSpec((None, tk, n), lambda i, j, kk: (j, kk, 0)),
                   pl.BlockSpec((None, tm, n), lambda i, j, kk: (j, i, 0)), (s, m, n), out_dtype, (tm, n))
    g = _tile(s, max(1, 1024 // n), 1)
    return _mm(name, a, bs, _NN, (m // tm, s // g, k // tk), a_spec,
               pl.BlockSpec((g, tk, n), lambda i, j, kk: (j, kk, 0)),
               pl.BlockSpec((tm, g * n), lambda i, j, kk: (i, j)), (m, s * n), out_dtype, (tm, g * n),
               group=g, n=n, b_sel="lead", o_sel="lanes")


def mm_as_nn(name, a_st, b3, res, out_dtype=F32, **epi):
    s, m, kp = a_st.shape
    n = b3.shape[2]
    tm, tn = _row_tile(m, epi), _tile(n, 1024, 128)
    spec = pl.BlockSpec((tm, tn), lambda i, j, kk: (i, j))
    g = _tile(s, 2, 1)
    return _mm(name, a_st, b3, _NN, (m // tm, n // tn, s // g),
               pl.BlockSpec((g, tm, kp), lambda i, j, kk: (kk, i, 0)),
               pl.BlockSpec((g, kp, tn), lambda i, j, kk: (kk, 0, j)),
               spec, (m, n), out_dtype, (tm, tn), res, spec, group=g, a_sel="lead", b_sel="lead", **epi)


def mm_nt(name, dc, b, out_dtype=None, **epi):
    m, n = dc.shape
    k = b.shape[0]
    tm, tko, tnr = _row_tile(m, epi), _tile(k, 1024, 128), _tile(n, 1024, 128)
    return _mm(name, dc, b, _NT, (m // tm, k // tko, n // tnr),
               pl.BlockSpec((tm, tnr), lambda i, j, kk: (i, kk)),
               pl.BlockSpec((tko, tnr), lambda i, j, kk: (j, kk)),
               pl.BlockSpec((tm, tko), lambda i, j, kk: (i, j)), (m, k), out_dtype, (tm, tko), **epi)


def mm_nt_bs(name, dc, bs, dc_stacked=False, out_dtype=None, **epi):
    s, k, n = bs.shape
    m = dc.shape[1] if dc_stacked else dc.shape[0]
    tm, tko = (_tile(m, 1024) if dc_stacked else _row_tile(m, epi)), _tile(k, 1024, 128)
    o_spec = pl.BlockSpec((tm, tko), lambda i, j, kk: (i, j))
    if dc_stacked:
        g = _tile(s, 2, 1)
        return _mm(name, dc, bs, _NT, (m // tm, k // tko, s // g),
                   pl.BlockSpec((g, tm, n), lambda i, j, kk: (kk, i, 0)),
                   pl.BlockSpec((g, tko, n), lambda i, j, kk: (kk, j, 0)), o_spec, (m, k), out_dtype, (tm, tko),
                   group=g, a_sel="lead", b_sel="lead", **epi)
    g = _tile(s, max(1, 2048 // n), 1)
    return _mm(name, dc, bs, _NT, (m // tm, k // tko, s // g),
               pl.BlockSpec((tm, g * n), lambda i, j, kk: (i, kk)),
               pl.BlockSpec((g, tko, n), lambda i, j, kk: (kk, j, 0)), o_spec, (m, k), out_dtype, (tm, tko),
               group=g, n=n, a_sel="lanes", b_sel="lead", **epi)


def mm_nt_os(name, dc, b3, out_dtype=None):
    m, n = dc.shape
    s, kp, _ = b3.shape
    tm, tnr = _tile(m, 1024), _tile(n, 1024, 128)
    return _mm(name, dc, b3, _NT, (m // tm, s, n // tnr),
               pl.BlockSpec((tm, tnr), lambda i, j, kk: (i, kk)),
               pl.BlockSpec((None, kp, tnr), lambda i, j, kk: (j, 0, kk)),
               pl.BlockSpec((None, tm, kp), lambda i, j, kk: (j, i, 0)), (s, m, kp), out_dtype, (tm, kp))


def mm_tn(name, a, dc, a_stacked=False, dc_cols=None, dc_stacked=False, out_dtype=None):
    if a_stacked:
        s, m, kp = a.shape
        n = dc.shape[1]
        tno, tmr = _tile(n, 1024, 128), _tile(m, 2048)
        return _mm(name, a, dc, _TN, (s, n // tno, m // tmr),
                   pl.BlockSpec((None, tmr, kp), lambda i, j, kk: (i, kk, 0)),
                   pl.BlockSpec((tmr, tno), lambda i, j, kk: (kk, j)),
                   pl.BlockSpec((None, kp, tno), lambda i, j, kk: (i, 0, j)), (s, kp, n), out_dtype, (kp, tno))
    m, k = a.shape
    tko, tmr = _tile(k, 1024, 128), _tile(m, 2048)
    a_spec = pl.BlockSpec((tmr, tko), lambda i, j, kk: (kk, i))
    if dc_stacked:
        s, _, n = dc.shape
        return _mm(name, a, dc, _TN, (k // tko, s, m // tmr), a_spec,
                   pl.BlockSpec((None, tmr, n), lambda i, j, kk: (j, kk, 0)),
                   pl.BlockSpec((None, tko, n), lambda i, j, kk: (j, i, 0)), (s, k, n), out_dtype, (tko, n))
    if dc_cols is not None:
        n = dc_cols
        s = dc.shape[1] // n
        g = _tile(s, max(1, 1024 // n), 1)
        return _mm(name, a, dc, _TN, (k // tko, s // g, m // tmr), a_spec,
                   pl.BlockSpec((tmr, g * n), lambda i, j, kk: (kk, j)),
                   pl.BlockSpec((g, tko, n), lambda i, j, kk: (j, i, 0)), (s, k, n), out_dtype, (g, tko, n),
                   group=g, n=n, b_sel="lanes", o_sel="lead")
    n = dc.shape[1]
    tno = _tile(n, 1024, 128)
    return _mm(name, a, dc, _TN, (k // tko, n // tno, m // tmr), a_spec,
               pl.BlockSpec((tmr, tno), lambda i, j, kk: (kk, j)),
               pl.BlockSpec((tko, tno), lambda i, j, kk: (i, j)), (k, n), out_dtype, (tko, tno))


def rms_fwd(name, x, g):
    t, d = x.shape
    tr = _tile(t, 512)

    def body(x_ref, g_ref, o_ref):
        xf = x_ref[...]
        r = lax.rsqrt(jnp.mean(xf * xf, axis=-1, keepdims=True) + EPS)
        o_ref[...] = (xf * r * g_ref[...]).astype(o_ref.dtype)

    return pl.pallas_call(
        body, name=name, grid=(t // tr,),
        in_specs=[pl.BlockSpec((tr, d), lambda i: (i, 0)), pl.BlockSpec((1, d), lambda i: (0, 0))],
        out_specs=pl.BlockSpec((tr, d), lambda i: (i, 0)),
        out_shape=jax.ShapeDtypeStruct((t, d), BF16), compiler_params=_params(("parallel",)),
    )(x, g.reshape(1, d))


def rms_bwd(name, x, g, dh, dres=None, need_dx=True):
    t, d = x.shape
    tr = _tile(t, 512)

    def body(*refs):
        refs = list(refs)
        x_ref, g_ref, dh_ref = refs[:3]
        r_ref = refs[3] if dres is not None else None
        outs = refs[4:] if dres is not None else refs[3:]
        dx_ref, dg_ref = (outs[0], outs[1]) if need_dx else (None, outs[0])
        i = pl.program_id(0)

        @pl.when(i == 0)
        def _():
            dg_ref[...] = jnp.zeros_like(dg_ref)

        xf = x_ref[...]
        dhf = dh_ref[...].astype(F32)
        r = lax.rsqrt(jnp.mean(xf * xf, axis=-1, keepdims=True) + EPS)
        xh = xf * r
        dg_ref[...] += jnp.sum(dhf * xh, axis=0, keepdims=True)
        if need_dx:
            dxh = dhf * g_ref[...]
            dx = r * (dxh - xh * jnp.mean(dxh * xh, axis=-1, keepdims=True))
            if r_ref is not None:
                dx = dx + r_ref[...]
            dx_ref[...] = dx

    row = pl.BlockSpec((tr, d), lambda i: (i, 0))
    vec = pl.BlockSpec((1, d), lambda i: (0, 0))
    in_specs = [row, vec, row] + ([row] if dres is not None else [])
    args = (x, g.reshape(1, d), dh) + ((dres,) if dres is not None else ())
    out_specs = ([row] if need_dx else []) + [vec]
    out_shape = ([jax.ShapeDtypeStruct((t, d), F32)] if need_dx else []) + [jax.ShapeDtypeStruct((1, d), F32)]
    res = pl.pallas_call(
        body, name=name, grid=(t // tr,), in_specs=in_specs, out_specs=out_specs, out_shape=out_shape,
        compiler_params=_params(("arbitrary",)),
    )(*args)
    return res if need_dx else (None, res[0])


def loss_head(x, g, tgt):
    t, d = x.shape
    tr = _tile(t, 512)

    def body(x_ref, g_ref, t_ref, l_ref, dx_ref, dg_ref):
        i = pl.program_id(0)

        @pl.when(i == 0)
        def _():
            l_ref[...] = jnp.zeros_like(l_ref)
            dg_ref[...] = jnp.zeros_like(dg_ref)

        xf = x_ref[...]
        r = lax.rsqrt(jnp.mean(xf * xf, axis=-1, keepdims=True) + EPS)
        xh = xf * r
        diff = xh * g_ref[...] - t_ref[...]
        l_ref[...] += 0.5 * jnp.sum(jnp.mean(diff * diff, axis=-1, keepdims=True))
        dy = diff * (1.0 / d)
        dg_ref[...] += jnp.sum(dy * xh, axis=0, keepdims=True)
        dxh = dy * g_ref[...]
        dx_ref[...] = r * (dxh - xh * jnp.mean(dxh * xh, axis=-1, keepdims=True))

    row = pl.BlockSpec((tr, d), lambda i: (i, 0))
    vec = pl.BlockSpec((1, d), lambda i: (0, 0))
    return pl.pallas_call(
        body, name="loss_head", grid=(t // tr,), in_specs=[row, vec, row],
        out_specs=[pl.BlockSpec((1, 128), lambda i: (0, 0)), row, vec],
        out_shape=[jax.ShapeDtypeStruct((1, 128), F32), jax.ShapeDtypeStruct((t, d), F32),
                   jax.ShapeDtypeStruct((1, d), F32)],
        compiler_params=_params(("arbitrary",)),
    )(x, g.reshape(1, d), tgt)


def glu_fwd(glu, x, gain):
    t, d = x.shape
    tr = _tile(t, 512)

    def body(v_ref, g_ref, x_ref, n_ref, o_ref, h_ref):
        y = x_ref[...] + v_ref[...] * jax.nn.sigmoid(g_ref[...])
        o_ref[...] = y
        r = lax.rsqrt(jnp.mean(y * y, axis=-1, keepdims=True) + EPS)
        h_ref[...] = (y * r * n_ref[...]).astype(h_ref.dtype)

    row = pl.BlockSpec((tr, d), lambda i: (i, 0))
    return pl.pallas_call(
        body, name="glu_fwd", grid=(t // tr,),
        in_specs=[row, pl.BlockSpec((tr, d), lambda i: (i, 1)), row, pl.BlockSpec((1, d), lambda i: (0, 0))],
        out_specs=[row, row],
        out_shape=[jax.ShapeDtypeStruct((t, d), F32), jax.ShapeDtypeStruct((t, d), BF16)],
        compiler_params=_params(("parallel",)),
    )(glu, glu, x, gain.reshape(1, d))


def glu_bwd(glu, dmix):
    t, d = dmix.shape
    tr = _tile(t, 512)

    def body(v_ref, g_ref, d_ref, o_ref):
        sg = jax.nn.sigmoid(g_ref[...])
        dm = d_ref[...]
        o_ref[:, :d] = (dm * sg).astype(o_ref.dtype)
        o_ref[:, d:] = (dm * v_ref[...] * sg * (1.0 - sg)).astype(o_ref.dtype)

    return pl.pallas_call(
        body, name="glu_bwd", grid=(t // tr,),
        in_specs=[pl.BlockSpec((tr, d), lambda i: (i, 0)), pl.BlockSpec((tr, d), lambda i: (i, 1)),
                  pl.BlockSpec((tr, d), lambda i: (i, 0))],
        out_specs=pl.BlockSpec((tr, 2 * d), lambda i: (i, 0)),
        out_shape=jax.ShapeDtypeStruct((t, 2 * d), BF16), compiler_params=_params(("parallel",)),
    )(glu, glu, dmix)


def _head_masks(shape):
    lane = lax.broadcasted_iota(jnp.int32, shape, 1)
    return lane < SB_HEAD_DIM


def _stack_heads(xf, is_a):
    return jnp.concatenate([jnp.where(is_a, xf, 0.0), jnp.where(is_a, 0.0, xf)], axis=0).astype(MXU_DTYPE)


def _diag_mask(qb, row0, rows):
    row = (lax.broadcasted_iota(jnp.int32, (rows, qb), 0) + row0) & (qb - 1)
    col = lax.broadcasted_iota(jnp.int32, (rows, qb), 1)
    return col < row


def _tri01(qb, pred):
    j = lax.broadcasted_iota(jnp.int32, (qb, qb), 0)
    s = lax.broadcasted_iota(jnp.int32, (qb, qb), 1)
    m = pred(j, s).astype(BF16)
    return jnp.concatenate([m, m], axis=0)


def _split_cat(x):
    hi = x.astype(BF16)
    lo = (x - hi.astype(F32)).astype(BF16)
    return jnp.concatenate([hi, lo], axis=1)


def sb_attn_fwd(proj, order, bsz, seq):
    qb = SB_BLOCK
    nq = seq // qb
    npair = SB_WIDTH // 128
    scale = SB_HEAD_DIM ** -0.5

    def body(q_ref, k_ref, v_ref, order_ref, o_ref, r_ref):
        qi = pl.program_id(2)
        is_a = _head_masks((qb, 128))
        q2 = _stack_heads(q_ref[...] * scale, is_a)
        diag = _diag_mask(qb, 0, 2 * qb)
        upper = _tri01(qb, lambda j, s: j > s)

        def blocks(kbs, acc, run, masked):
            sl = [pl.ds(pl.multiple_of(kb * qb, qb), qb) for kb in kbs]
            zs = [lax.dot_general(q2, k_ref[s, :].astype(MXU_DTYPE), _NT, preferred_element_type=F32) for s in sl]
            lks = [-jnp.maximum(z, 0.0) - jnp.log(1.0 + jnp.exp(-jnp.abs(z))) for z in zs]
            lbs = [lk + z for lk, z in zip(lks, zs)]
            if masked:
                lks = [jnp.where(diag, lk, 0.0) for lk in lks]
            cs = [lax.dot_general(_split_cat(lk), upper, _NN, preferred_element_type=F32) for lk in lks]
            for lk, lb, c, s in zip(lks, lbs, cs, sl):
                w = jnp.exp(lb + (run + c))
                if masked:
                    w = jnp.where(diag, w, 0.0)
                acc = acc + lax.dot_general(w.astype(MXU_DTYPE), v_ref[s, :].astype(MXU_DTYPE), _NN,
                                            preferred_element_type=F32)
                run = run + jnp.sum(lk, axis=1, keepdims=True)
            return acc, run

        carry = blocks([qi], jnp.zeros((2 * qb, 128), F32), jnp.zeros((2 * qb, 1), F32), True)
        carry = lax.cond(qi % 2 == 1, lambda c: blocks([qi - 1], c[0], c[1], False), lambda c: c, carry)
        top = qi - qi % 2
        acc, run = lax.fori_loop(
            0, qi // 2, lambda i, c: blocks([top - 1 - 2 * i, top - 2 - 2 * i], c[0], c[1], False), carry)
        o_ref[...] = jnp.where(is_a, acc[:qb], acc[qb:]).astype(o_ref.dtype)
        r_ref[...] = jnp.where(is_a, run[:qb], run[qb:])

    return pl.pallas_call(
        body, name="sb_attn_fwd", grid=(bsz, npair, nq),
        in_specs=[pl.BlockSpec((qb, 128), lambda b, p, i: (b * nq + i, p)),
                  pl.BlockSpec((seq, 128), lambda b, p, i: (b, npair + p)),
                  pl.BlockSpec((seq, 128), lambda b, p, i: (b, 2 * npair + p)),
                  pl.BlockSpec((1, 128), lambda b, p, i: (0, 0))],
        out_specs=[pl.BlockSpec((qb, 128), lambda b, p, i: (b * nq + i, p)),
                   pl.BlockSpec((qb, 128), lambda b, p, i: (b * nq + i, p))],
        out_shape=[jax.ShapeDtypeStruct((bsz * seq, SB_WIDTH), BF16),
                   jax.ShapeDtypeStruct((bsz * seq, SB_WIDTH), F32)],
        compiler_params=_params(("parallel", "parallel", "arbitrary")),
    )(proj, proj, proj, order)


def sb_attn_bwd(proj, rsum, dcat, bsz, seq):
    qb = SB_BLOCK
    nq = seq // qb
    npair = SB_WIDTH // 128
    scale = SB_HEAD_DIM ** -0.5

    def body(q_ref, k_ref, v_ref, r_ref, do_ref, dq_ref, dk_ref, dv_ref):
        qi = pl.program_id(2)

        @pl.when(qi == 0)
        def _():
            dk_ref[...] = jnp.zeros_like(dk_ref)
            dv_ref[...] = jnp.zeros_like(dv_ref)

        is_a = _head_masks((qb, 128))
        q2 = _stack_heads(q_ref[...] * scale, is_a)
        do2 = _stack_heads(do_ref[...].astype(F32), is_a)
        rf = r_ref[...]
        rtot = jnp.concatenate([rf[:, 0:1], rf[:, SB_HEAD_DIM:SB_HEAD_DIM + 1]], axis=0)
        diag = _diag_mask(qb, 0, 2 * qb)
        incl = _tri01(qb, lambda j, s: j <= s)
        strict = _tri01(qb, lambda j, s: j < s)

        def blocks(kbs, dq, pre, epre, masked):
            sl = [pl.ds(pl.multiple_of(kb * qb, qb), qb) for kb in kbs]
            ks = [k_ref[s, :].astype(MXU_DTYPE) for s in sl]
            vs = [v_ref[s, :].astype(MXU_DTYPE) for s in sl]
            zs = [lax.dot_general(q2, kblk, _NT, preferred_element_type=F32) for kblk in ks]
            dws = [lax.dot_general(do2, vblk, _NT, preferred_element_type=F32) for vblk in vs]
            lks = [-jnp.maximum(z, 0.0) - jnp.log(1.0 + jnp.exp(-jnp.abs(z))) for z in zs]
            lbs = [lk + z for lk, z in zip(lks, zs)]
            if masked:
                lks = [jnp.where(diag, lk, 0.0) for lk in lks]
            ps = [lax.dot_general(_split_cat(lk), incl, _NN, preferred_element_type=F32) for lk in lks]
            ws, es = [], []
            for lk, lb, p, dw in zip(lks, lbs, ps, dws):
                w = jnp.exp(lb + (rtot - (pre + p)))
                if masked:
                    w = jnp.where(diag, w, 0.0)
                ws.append(w)
                es.append(dw * w)
                pre = pre + jnp.sum(lk, axis=1, keepdims=True)
            cs = [lax.dot_general(_split_cat(e), strict, _NN, preferred_element_type=F32) for e in es]
            for e, lb, c, w, kblk, s in zip(es, lbs, cs, ws, ks, sl):
                dz = e - jnp.exp(lb) * (e + (epre + c))
                if masked:
                    dz = jnp.where(diag, dz, 0.0)
                dz = dz.astype(MXU_DTYPE)
                dq = dq + lax.dot_general(dz, kblk, _NN, preferred_element_type=F32)
                dk_ref[s, :] += lax.dot_general(dz, q2, _TN, preferred_element_type=F32)
                dv_ref[s, :] += lax.dot_general(w.astype(MXU_DTYPE), do2, _TN, preferred_element_type=F32)
                epre = epre + jnp.sum(e, axis=1, keepdims=True)
            return dq, pre, epre

        zc = jnp.zeros((2 * qb, 1), F32)
        carry = lax.fori_loop(0, qi // 2, lambda i, c: blocks([2 * i, 2 * i + 1], c[0], c[1], c[2], False),
                              (jnp.zeros((2 * qb, 128), F32), zc, zc))
        carry = lax.cond(qi % 2 == 1, lambda c: blocks([qi - 1], c[0], c[1], c[2], False), lambda c: c, carry)
        dq = blocks([qi], carry[0], carry[1], carry[2], True)[0]
        dq_ref[...] = jnp.where(is_a, dq[:qb], dq[qb:]) * scale

    full = jax.ShapeDtypeStruct((bsz * seq, SB_WIDTH), F32)
    qspec = pl.BlockSpec((qb, 128), lambda b, p, i: (b * nq + i, p))
    return pl.pallas_call(
        body, name="sb_attn_bwd", grid=(bsz, npair, nq),
        in_specs=[qspec,
                  pl.BlockSpec((seq, 128), lambda b, p, i: (b, npair + p)),
                  pl.BlockSpec((seq, 128), lambda b, p, i: (b, 2 * npair + p)),
                  qspec, qspec],
        out_specs=[qspec, pl.BlockSpec((seq, 128), lambda b, p, i: (b, p)),
                   pl.BlockSpec((seq, 128), lambda b, p, i: (b, p))],
        out_shape=[full, full, full],
        compiler_params=_params(("parallel", "parallel", "arbitrary")),
    )(proj, proj, proj, rsum, dcat)


def _window_sums(x, forward):
    n = x.shape[0]
    out = []
    s = x
    for sh in (1, 2, 4, 8):
        s = s + pltpu.roll(s, (n - sh) if forward else sh, 0)
        out.append(s)
    return out


def _pool_counts(tc, c, w):
    t = lax.broadcasted_iota(jnp.int32, (tc, 1), 0) + c * tc
    return jnp.minimum(t + 1, w).astype(F32)


def pool_fwd(proj, pool_w, pool_scale, bsz, seq):
    tc = _tile(seq, 512)
    nc = seq // tc
    hb = tc // POOL_HALO
    ucol = 3

    def body(u_ref, prev_ref, w_ref, s_ref, o_ref):
        c = pl.program_id(1)
        prev = jnp.where(c > 0, prev_ref[...], 0.0)
        x = jnp.concatenate([prev, u_ref[...]], axis=0)
        sums = _window_sums(x, forward=False)
        for g, win in enumerate(POOL_WINDOWS):
            ls = slice(g * POOL_GROUP, (g + 1) * POOL_GROUP)
            pooled = sums[g][POOL_HALO:, ls] / _pool_counts(tc, c, win) - x[POOL_HALO:, ls]
            y = _dot(pooled, w_ref[g], _NN)
            o_ref[:, ls] = (y * s_ref[:, ls]).astype(o_ref.dtype)

    return pl.pallas_call(
        body, name="pool_fwd", grid=(bsz, nc),
        in_specs=[pl.BlockSpec((tc, SB_WIDTH), lambda b, c: (b * nc + c, ucol)),
                  pl.BlockSpec((POOL_HALO, SB_WIDTH), lambda b, c: (jnp.maximum((b * nc + c) * hb - 1, 0), ucol)),
                  pl.BlockSpec((4, POOL_GROUP, POOL_GROUP), lambda b, c: (0, 0, 0)),
                  pl.BlockSpec((1, SB_WIDTH), lambda b, c: (0, 0))],
        out_specs=pl.BlockSpec((tc, SB_WIDTH), lambda b, c: (b * nc + c, 0)),
        out_shape=jax.ShapeDtypeStruct((bsz * seq, SB_WIDTH), BF16),
        compiler_params=_params(("parallel", "parallel")),
    )(proj, proj, pool_w, pool_scale)


def pool_bwd(proj, pool_w, pool_scale, dcat, bsz, seq):
    tc = _tile(seq, 512)
    nc = seq // tc
    hb = tc // POOL_HALO
    nblk = bsz * seq // POOL_HALO
    ucol = 3

    def body(u_ref, prev_ref, dy_ref, nxt_ref, w_ref, s_ref, du_ref, dw_ref, ds_ref):
        b, c = pl.program_id(0), pl.program_id(1)

        @pl.when((b == 0) & (c == 0))
        def _():
            dw_ref[...] = jnp.zeros_like(dw_ref)
            ds_ref[...] = jnp.zeros_like(ds_ref)

        prev = jnp.where(c > 0, prev_ref[...], 0.0)
        x = jnp.concatenate([prev, u_ref[...]], axis=0)
        sums = _window_sums(x, forward=False)
        nxt = jnp.where(c < nc - 1, nxt_ref[...].astype(F32), 0.0)
        dy = jnp.concatenate([dy_ref[...].astype(F32), nxt], axis=0)
        tq = lax.broadcasted_iota(jnp.int32, (tc + POOL_HALO, 1), 0) + c * tc
        for g, win in enumerate(POOL_WINDOWS):
            ls = slice(g * POOL_GROUP, (g + 1) * POOL_GROUP)
            pooled = sums[g][POOL_HALO:, ls] / _pool_counts(tc, c, win) - x[POOL_HALO:, ls]
            y = _dot(pooled, w_ref[g], _NN)
            ds_ref[:, ls] += jnp.sum(dy[:tc, ls] * y, axis=0, keepdims=True)
            dz = dy[:, ls] * s_ref[:, ls]
            dw_ref[g] += _dot(pooled, dz[:tc], _TN)
            dpool = _dot(dz, w_ref[g], _NT)
            dmean = dpool / jnp.minimum(tq + 1, win).astype(F32)
            fsum = _window_sums(dmean, forward=True)[g]
            du_ref[:, ls] = fsum[:tc] - dpool[:tc]

    return pl.pallas_call(
        body, name="pool_bwd", grid=(bsz, nc),
        in_specs=[pl.BlockSpec((tc, SB_WIDTH), lambda b, c: (b * nc + c, ucol)),
                  pl.BlockSpec((POOL_HALO, SB_WIDTH), lambda b, c: (jnp.maximum((b * nc + c) * hb - 1, 0), ucol)),
                  pl.BlockSpec((tc, SB_WIDTH), lambda b, c: (b * nc + c, 1)),
                  pl.BlockSpec((POOL_HALO, SB_WIDTH), lambda b, c: (jnp.minimum((b * nc + c + 1) * hb, nblk - 1), 1)),
                  pl.BlockSpec((4, POOL_GROUP, POOL_GROUP), lambda b, c: (0, 0, 0)),
                  pl.BlockSpec((1, SB_WIDTH), lambda b, c: (0, 0))],
        out_specs=[pl.BlockSpec((tc, SB_WIDTH), lambda b, c: (b * nc + c, 0)),
                   pl.BlockSpec((4, POOL_GROUP, POOL_GROUP), lambda b, c: (0, 0, 0)),
                   pl.BlockSpec((1, SB_WIDTH), lambda b, c: (0, 0))],
        out_shape=[jax.ShapeDtypeStruct((bsz * seq, SB_WIDTH), F32),
                   jax.ShapeDtypeStruct((4, POOL_GROUP, POOL_GROUP), F32),
                   jax.ShapeDtypeStruct((1, SB_WIDTH), F32)],
        compiler_params=_params(("arbitrary", "arbitrary")),
    )(proj, proj, dcat, dcat, pool_w, pool_scale)


def _lbar(lam_re, lam_im, log_dt):
    dt = jnp.exp(log_dt)
    mag = jnp.exp(lam_re * dt)
    ang = lam_im * dt
    return mag * jnp.cos(ang), mag * jnp.sin(ang)


def _bbar(lam_re, lam_im, log_dt, b_re, b_im):
    lb_re, lb_im = _lbar(lam_re, lam_im, log_dt)
    n_re = lb_re - 1.0
    den = lam_re * lam_re + lam_im * lam_im
    coef_re = (n_re * lam_re + lb_im * lam_im) / den
    coef_im = (lb_im * lam_re - n_re * lam_im) / den
    return coef_re * b_re - coef_im * b_im, coef_re * b_im + coef_im * b_re


def _expand01():
    p = lax.broadcasted_iota(jnp.int32, (64, 1024), 0)
    q = lax.broadcasted_iota(jnp.int32, (64, 1024), 1)
    return (lax.shift_right_logical(q, 4) == p).astype(BF16)


def ssm_prep(lam_re, lam_im, log_dt, b_re2, b_im2):
    def body(lr_ref, li_ref, dt_ref, br_ref, bi_ref, ar_ref, ai_ref, bbr_ref, bbi_ref):
        e = _expand01()
        lr, li, dt = lr_ref[...], li_ref[...], dt_ref[...]
        ar_ref[...], ai_ref[...] = _lbar(lr, li, dt)
        bbr_ref[...], bbi_ref[...] = _bbar(_dot_exact01(lr, e), _dot_exact01(li, e), dt, br_ref[...], bi_ref[...])

    s64 = jax.ShapeDtypeStruct((64, 64), F32)
    s1k = jax.ShapeDtypeStruct((64, 1024), F32)
    return pl.pallas_call(body, name="ssm_prep", out_shape=[s64, s64, s1k, s1k], compiler_params=_params())(
        lam_re, lam_im, log_dt, b_re2, b_im2)


def ssm_prep_bwd(lam_re, lam_im, log_dt, b_re2, b_im2, da_re, da_im, dbb_re, dbb_im):
    def body(lr_ref, li_ref, dt_ref, br_ref, bi_ref, dar_ref, dai_ref, dbr_ref, dbi_ref,
             olr_ref, oli_ref, odt_ref, obr_ref, obi_ref):
        e = _expand01()
        lr, li, dt = lr_ref[...], li_ref[...], dt_ref[...]
        _, vjp_a = jax.vjp(_lbar, lr, li, dt)
        g_lr, g_li, g_dt = vjp_a((dar_ref[...], dai_ref[...]))
        _, vjp_b = jax.vjp(_bbar, _dot_exact01(lr, e), _dot_exact01(li, e), dt, br_ref[...], bi_ref[...])
        x_lr, x_li, x_dt, g_br, g_bi = vjp_b((dbr_ref[...], dbi_ref[...]))
        olr_ref[...] = g_lr + _dot_exact01(x_lr, e, _NT)
        oli_ref[...] = g_li + _dot_exact01(x_li, e, _NT)
        odt_ref[...] = g_dt + x_dt
        obr_ref[...] = g_br
        obi_ref[...] = g_bi

    s64 = jax.ShapeDtypeStruct((64, 64), F32)
    s1k = jax.ShapeDtypeStruct((64, 1024), F32)
    return pl.pallas_call(body, name="ssm_prep_bwd",
                          out_shape=[s64, s64, jax.ShapeDtypeStruct((64, 1), F32), s1k, s1k],
                          compiler_params=_params())(
        lam_re, lam_im, log_dt, b_re2, b_im2, da_re, da_im, dbb_re, dbb_im)


def _gelu(y):
    c = math.sqrt(2.0 / math.pi)
    return 0.5 * y * (1.0 + jnp.tanh(c * (y + 0.044715 * y * y * y)))


def _gelu_grad(y):
    c = math.sqrt(2.0 / math.pi)
    th = jnp.tanh(c * (y + 0.044715 * y * y * y))
    return 0.5 * (1.0 + th) + 0.5 * y * (1.0 - th * th) * c * (1.0 + 3.0 * 0.044715 * y * y)


def _cmul(ar, ai, br, bi):
    return ar * br - ai * bi, ar * bi + ai * br


def _scan_tables(ar, ai, reverse, tabs):
    row = lax.broadcasted_iota(jnp.int32, (8, SSM_STATES), 0)
    a1 = (ar, ai)
    a2 = _cmul(*a1, *a1)
    a4 = _cmul(*a2, *a2)
    powers = [a1, a2, _cmul(*a2, *a1), a4]
    powers += [_cmul(*a4, *p) for p in powers]
    for k, (val, sh) in enumerate(((a1, 1), (a2, 2), (a4, 4))):
        keep = (row < 8 - sh) if reverse else (row >= sh)
        tabs[2 * k][...] = jnp.where(keep, val[0], 0.0)
        tabs[2 * k + 1][...] = jnp.where(keep, val[1], 0.0)
    pr = jnp.zeros((8, SSM_STATES), F32)
    pi = jnp.zeros((8, SSM_STATES), F32)
    for r in range(8):
        val = powers[7 - r] if reverse else powers[r]
        pr = jnp.where(row == r, val[0], pr)
        pi = jnp.where(row == r, val[1], pi)
    tabs[6][...] = pr
    tabs[7][...] = pi


def _scan8(xr, xi, tabs, ls, cr, ci, reverse):
    for k, sh in enumerate((1, 2, 4)):
        amt = (8 - sh) if reverse else sh
        sr, si = pltpu.roll(xr, amt, 0), pltpu.roll(xi, amt, 0)
        lr, li = tabs[2 * k][:, ls], tabs[2 * k + 1][:, ls]
        xr, xi = xr + lr * sr - li * si, xi + lr * si + li * sr
    pr, pi = tabs[6][:, ls], tabs[7][:, ls]
    return xr + pr * cr - pi * ci, xi + pr * ci + pi * cr


def _block8(b):
    return pl.ds(pl.multiple_of(b * 8, 8), 8)


def ssm_fwd(u, wt, ct, a_re, a_im, dskip, bsz, seq):
    tc = _tile(seq, 256)
    nc = seq // tc
    ns = SSM_TILE_STATES
    nl = SSM_STATES // SSM_LANES

    def body(u_ref, wt_ref, ct_ref, ar_ref, ai_ref, d_ref, y_ref, gl_ref, hr_ref, hi_ref, sr_ref, si_ref, *tabs):
        b, c = pl.program_id(0), pl.program_id(1)

        @pl.when((b == 0) & (c == 0))
        def _():
            _scan_tables(ar_ref[...], ai_ref[...], False, tabs)

        @pl.when(c == 0)
        def _():
            sr_ref[...] = jnp.zeros_like(sr_ref)
            si_ref[...] = jnp.zeros_like(si_ref)

        uf = u_ref[...]
        for i in range(SSM_TILES):
            bu = _dot(uf[:, i * 128:(i + 1) * 128], wt_ref[i], _NN)
            hr_ref[:, i * ns:(i + 1) * ns] = bu[:, :ns]
            hi_ref[:, i * ns:(i + 1) * ns] = bu[:, ns:]

        def step(blk, carry):
            rows = _block8(blk)
            new = []
            for j in range(nl):
                ls = slice(j * SSM_LANES, (j + 1) * SSM_LANES)
                xr, xi = _scan8(hr_ref[rows, ls], hi_ref[rows, ls], tabs, ls, carry[2 * j], carry[2 * j + 1], False)
                hr_ref[rows, ls] = xr
                hi_ref[rows, ls] = xi
                new += [xr[7:8], xi[7:8]]
            return tuple(new)

        init = []
        for j in range(nl):
            ls = slice(j * SSM_LANES, (j + 1) * SSM_LANES)
            init += [sr_ref[:, ls], si_ref[:, ls]]
        last = lax.fori_loop(0, tc // 8, step, tuple(init), unroll=2)
        for j in range(nl):
            ls = slice(j * SSM_LANES, (j + 1) * SSM_LANES)
            sr_ref[:, ls] = last[2 * j]
            si_ref[:, ls] = last[2 * j + 1]
        for i in range(SSM_TILES):
            hcat = jnp.concatenate([hr_ref[:, i * ns:(i + 1) * ns], hi_ref[:, i * ns:(i + 1) * ns]], axis=1)
            ls = slice(i * 128, (i + 1) * 128)
            y = _dot(hcat, ct_ref[i], _NN) + d_ref[:, ls] * uf[:, ls]
            y_ref[:, ls] = y
            gl_ref[:, ls] = _gelu(y).astype(gl_ref.dtype)

    t = bsz * seq
    row = pl.BlockSpec((tc, D_MODEL), lambda b, c: (b * nc + c, 0))
    st = pl.BlockSpec((tc, SSM_STATES), lambda b, c: (b * nc + c, 0))
    diag = pl.BlockSpec((1, SSM_STATES), lambda b, c: (0, 0))
    return pl.pallas_call(
        body, name="ssm_fwd", grid=(bsz, nc),
        in_specs=[row, pl.BlockSpec((SSM_TILES, 128, 2 * ns), lambda b, c: (0, 0, 0)),
                  pl.BlockSpec((SSM_TILES, 2 * ns, 128), lambda b, c: (0, 0, 0)), diag, diag,
                  pl.BlockSpec((1, D_MODEL), lambda b, c: (0, 0))],
        out_specs=[row, row, st, st],
        out_shape=[jax.ShapeDtypeStruct((t, D_MODEL), F32), jax.ShapeDtypeStruct((t, D_MODEL), BF16),
                   jax.ShapeDtypeStruct((t, SSM_STATES), F32), jax.ShapeDtypeStruct((t, SSM_STATES), F32)],
        scratch_shapes=[pltpu.VMEM((1, SSM_STATES), F32)] * 2 + [pltpu.VMEM((8, SSM_STATES), F32)] * 8,
        compiler_params=_params(("arbitrary", "arbitrary")),
    )(u, wt, ct, a_re, a_im, dskip)


def ssm_bwd(dgl, y, u, h_re, h_im, wt, ct, a_re, a_im, dskip, bsz, seq):
    tc = _tile(seq, 256)
    nc = seq // tc
    nb = tc // 8
    ns = SSM_TILE_STATES
    nl = SSM_STATES // SSM_LANES

    def body(dgl_ref, y_ref, u_ref, hr_ref, hi_ref, pr_ref, pi_ref, wt_ref, ct_ref, ar_ref, ai_ref, d_ref,
             du_ref, dwt_ref, dct_ref, dd_ref, dar_ref, dai_ref, gr_ref, gi_ref, sr_ref, si_ref, ar8_ref, ai8_ref,
             *tabs):
        b, c = pl.program_id(0), pl.program_id(1)

        @pl.when((b == 0) & (c == 0))
        def _():
            for r in (dwt_ref, dct_ref, dd_ref, ar8_ref, ai8_ref):
                r[...] = jnp.zeros_like(r)
            _scan_tables(ar_ref[...], -ai_ref[...], True, tabs)

        @pl.when(c == 0)
        def _():
            sr_ref[...] = jnp.zeros_like(sr_ref)
            si_ref[...] = jnp.zeros_like(si_ref)

        uf = u_ref[...]
        dy = dgl_ref[...].astype(F32) * _gelu_grad(y_ref[...])
        dd_ref[...] += jnp.sum(dy * uf, axis=0, keepdims=True)
        for i in range(SSM_TILES):
            dyi = dy[:, i * 128:(i + 1) * 128]
            dh = _dot(dyi, ct_ref[i], _NT)
            gr_ref[:, i * ns:(i + 1) * ns] = dh[:, :ns]
            gi_ref[:, i * ns:(i + 1) * ns] = dh[:, ns:]
            hcat = jnp.concatenate([hr_ref[:, i * ns:(i + 1) * ns], hi_ref[:, i * ns:(i + 1) * ns]], axis=1)
            dct_ref[i] += _dot(hcat, dyi, _TN)
        row0 = lax.broadcasted_iota(jnp.int32, (8, SSM_LANES), 0) == 0

        def block(blk, carry, before):
            rows = _block8(blk)
            new = []
            for j in range(nl):
                ls = slice(j * SSM_LANES, (j + 1) * SSM_LANES)
                gr, gi = _scan8(gr_ref[rows, ls], gi_ref[rows, ls], tabs, ls, carry[2 * j], carry[2 * j + 1], True)
                gr_ref[rows, ls] = gr
                gi_ref[rows, ls] = gi
                bpr, bpi = before(j)
                hpr = jnp.where(row0, bpr, pltpu.roll(hr_ref[rows, ls], 1, 0))
                hpi = jnp.where(row0, bpi, pltpu.roll(hi_ref[rows, ls], 1, 0))
                ar8_ref[:, ls] += gr * hpr + gi * hpi
                ai8_ref[:, ls] += gi * hpr - gr * hpi
                new += [gr[0:1], gi[0:1]]
            return tuple(new)

        def step(jj, carry):
            blk = nb - 1 - jj
            prev_rows = _block8(blk - 1)

            def before(j):
                ls = slice(j * SSM_LANES, (j + 1) * SSM_LANES)
                return hr_ref[prev_rows, ls][7:8], hi_ref[prev_rows, ls][7:8]

            return block(blk, carry, before)

        init = []
        for j in range(nl):
            ls = slice(j * SSM_LANES, (j + 1) * SSM_LANES)
            init += [sr_ref[:, ls], si_ref[:, ls]]
        carry = lax.fori_loop(0, nb - 1, step, tuple(init))
        first = c == nc - 1

        def before_chunk(j):
            ls = slice(j * SSM_LANES, (j + 1) * SSM_LANES)
            return (jnp.where(first, 0.0, pr_ref[:, ls][7:8]), jnp.where(first, 0.0, pi_ref[:, ls][7:8]))

        last = block(0, carry, before_chunk)
        for j in range(nl):
            ls = slice(j * SSM_LANES, (j + 1) * SSM_LANES)
            sr_ref[:, ls] = last[2 * j]
            si_ref[:, ls] = last[2 * j + 1]
        for i in range(SSM_TILES):
            ls = slice(i * 128, (i + 1) * 128)
            gcat = jnp.concatenate([gr_ref[:, i * ns:(i + 1) * ns], gi_ref[:, i * ns:(i + 1) * ns]], axis=1)
            du_ref[:, ls] = (_dot(gcat, wt_ref[i], _NT) + d_ref[:, ls] * dy[:, ls]).astype(du_ref.dtype)
            dwt_ref[i] += _dot(uf[:, ls], gcat, _TN)

        @pl.when((b == bsz - 1) & (c == nc - 1))
        def _():
            dar_ref[...] = jnp.sum(ar8_ref[...], axis=0, keepdims=True)
            dai_ref[...] = jnp.sum(ai8_ref[...], axis=0, keepdims=True)

    t = bsz * seq
    rev = lambda b, c: (b * nc + (nc - 1 - c), 0)
    row = pl.BlockSpec((tc, D_MODEL), rev)
    st = pl.BlockSpec((tc, SSM_STATES), rev)
    prev = pl.BlockSpec((8, SSM_STATES), lambda b, c: (jnp.maximum((b * nc + (nc - 1 - c)) * nb - 1, 0), 0))
    diag = pl.BlockSpec((1, SSM_STATES), lambda b, c: (0, 0))
    wts = pl.BlockSpec((SSM_TILES, 128, 2 * ns), lambda b, c: (0, 0, 0))
    cts = pl.BlockSpec((SSM_TILES, 2 * ns, 128), lambda b, c: (0, 0, 0))
    vec = pl.BlockSpec((1, D_MODEL), lambda b, c: (0, 0))
    return pl.pallas_call(
        body, name="ssm_bwd", grid=(bsz, nc),
        in_specs=[row, row, row, st, st, prev, prev, wts, cts, diag, diag, vec],
        out_specs=[row, wts, cts, vec, diag, diag],
        out_shape=[jax.ShapeDtypeStruct((t, D_MODEL), BF16),
                   jax.ShapeDtypeStruct((SSM_TILES, 128, 2 * ns), F32),
                   jax.ShapeDtypeStruct((SSM_TILES, 2 * ns, 128), F32),
                   jax.ShapeDtypeStruct((1, D_MODEL), F32),
                   jax.ShapeDtypeStruct((1, SSM_STATES), F32), jax.ShapeDtypeStruct((1, SSM_STATES), F32)],
        scratch_shapes=[pltpu.VMEM((tc, SSM_STATES), F32)] * 2 + [pltpu.VMEM((1, SSM_STATES), F32)] * 2
                       + [pltpu.VMEM((8, SSM_STATES), F32)] * 10,
        compiler_params=_params(("arbitrary", "arbitrary")),
    )(dgl, y, u, h_re, h_im, h_re, h_im, wt, ct, a_re, a_im, dskip)


def _ssm_in_weights(bb_re2, bb_im2):
    eye = jnp.eye(8, dtype=F32)[None, :, None, :, None]

    def one(bb):
        t = bb.reshape(8, 8, 64, 16).transpose(0, 1, 3, 2)
        return (t[:, :, :, None, :] * eye).reshape(8, 128, 512)

    return jnp.concatenate([one(bb_re2), one(bb_im2)], axis=-1).astype(MXU_DTYPE)


def _ssm_in_weights_bwd(dwt):
    eye = jnp.eye(8, dtype=F32)[None, :, None, :, None]

    def one(d):
        t = (d.reshape(8, 8, 16, 8, 64) * eye).sum(axis=3)
        return t.transpose(0, 1, 3, 2).reshape(64, 1024)

    return one(dwt[..., :512]), one(dwt[..., 512:])


def _ssm_out_weights(c_re, c_im):
    eye = jnp.eye(8, dtype=F32)[None, :, None, :, None]

    def one(cc):
        t = cc.reshape(8, 8, 16, 64).transpose(0, 1, 3, 2)
        return (t[:, :, :, None, :] * eye).reshape(8, 512, 128)

    return jnp.concatenate([one(c_re), -one(c_im)], axis=1).astype(MXU_DTYPE)


def _ssm_out_weights_bwd(dct):
    eye = jnp.eye(8, dtype=F32)[None, :, None, :, None]

    def one(d):
        t = (d.reshape(8, 8, 64, 8, 16) * eye).sum(axis=3)
        return t.transpose(0, 1, 3, 2).reshape(64, 16, 64)

    return one(dct[:, :512]), -one(dct[:, 512:])


def _softmax(s):
    m = jnp.max(s, axis=-1, keepdims=True)
    e = jnp.exp(s - m)
    return e / jnp.sum(e, axis=-1, keepdims=True)


def xattn_fwd(q, kv, bsz, seq):
    tq = _tile(seq, 512)
    nq = seq // tq
    scale = XA_HEAD_DIM ** -0.5

    def body(q_ref, k_ref, v_ref, o_ref):
        s = lax.dot_general(q_ref[...], k_ref[...], _NT, preferred_element_type=F32) * scale
        p = _softmax(s)
        o_ref[...] = _dot(p, v_ref[...], _NN).astype(o_ref.dtype)

    qs = pl.BlockSpec((tq, XA_HEAD_DIM), lambda b, h, i: (b * nq + i, h))
    return pl.pallas_call(
        body, name="xattn_fwd", grid=(bsz, XA_HEADS, nq),
        in_specs=[qs, pl.BlockSpec((MEM_LEN, XA_HEAD_DIM), lambda b, h, i: (b, h)),
                  pl.BlockSpec((MEM_LEN, XA_HEAD_DIM), lambda b, h, i: (b, XA_HEADS + h))],
        out_specs=qs, out_shape=jax.ShapeDtypeStruct((bsz * seq, D_MODEL), BF16),
        compiler_params=_params(("parallel", "parallel", "parallel")),
    )(q, kv, kv)


def xattn_bwd(q, kv, do, bsz, seq):
    tq = _tile(seq, 512)
    nq = seq // tq
    scale = XA_HEAD_DIM ** -0.5

    def body(q_ref, k_ref, v_ref, do_ref, dq_ref, dk_ref, dv_ref):
        @pl.when(pl.program_id(2) == 0)
        def _():
            dk_ref[...] = jnp.zeros_like(dk_ref)
            dv_ref[...] = jnp.zeros_like(dv_ref)

        qv, kk, vv, dov = q_ref[...], k_ref[...], v_ref[...], do_ref[...]
        s = lax.dot_general(qv, kk, _NT, preferred_element_type=F32) * scale
        p = _softmax(s)
        dp = lax.dot_general(dov, vv, _NT, preferred_element_type=F32)
        ds = (p * (dp - jnp.sum(dp * p, axis=-1, keepdims=True)) * scale).astype(MXU_DTYPE)
        dq_ref[...] = lax.dot_general(ds, kk, _NN, preferred_element_type=F32).astype(dq_ref.dtype)
        dk_ref[...] += lax.dot_general(ds, qv, _TN, preferred_element_type=F32)
        dv_ref[...] += lax.dot_general(p.astype(MXU_DTYPE), dov, _TN, preferred_element_type=F32)

    qs = pl.BlockSpec((tq, XA_HEAD_DIM), lambda b, h, i: (b * nq + i, h))
    ks = pl.BlockSpec((MEM_LEN, XA_HEAD_DIM), lambda b, h, i: (b, h))
    vs = pl.BlockSpec((MEM_LEN, XA_HEAD_DIM), lambda b, h, i: (b, XA_HEADS + h))
    dkv = jax.ShapeDtypeStruct((bsz * MEM_LEN, D_MODEL), F32)
    dq, dk, dv = pl.pallas_call(
        body, name="xattn_bwd", grid=(bsz, XA_HEADS, nq),
        in_specs=[qs, ks, vs, qs], out_specs=[qs, ks, ks],
        out_shape=[jax.ShapeDtypeStruct((bsz * seq, D_MODEL), BF16), dkv, dkv],
        compiler_params=_params(("parallel", "parallel", "arbitrary")),
    )(q, kv, kv, do)
    return dq, dk, dv


CONV_HALO = 16


def _shifts_down(x, prev):
    h = prev.shape[0]
    ext = jnp.concatenate([prev, x], axis=0)
    return pltpu.roll(ext, 1, 0)[h:], pltpu.roll(ext, 2, 0)[h:]


def _shifts_up(x, nxt):
    rows = x.shape[0]
    n = rows + nxt.shape[0]
    ext = jnp.concatenate([x, nxt], axis=0)
    return pltpu.roll(ext, n - 1, 0)[:rows], pltpu.roll(ext, n - 2, 0)[:rows]


def _conv_taps(u, u1, u2, w, b):
    return b + w[2:3] * u + w[1:2] * u1 + w[0:1] * u2


def conv_fwd(up, cw, cb, bsz, seq):
    tc = _tile(seq, 512)
    nc = seq // tc
    hb = tc // CONV_HALO
    half = N_DEV // 2

    def body(uv_ref, ug_ref, pv_ref, pg_ref, wv_ref, wg_ref, bv_ref, bg_ref, o_ref):
        c = pl.program_id(2)
        pv = jnp.where(c > 0, pv_ref[...].astype(F32), 0.0)
        pg = jnp.where(c > 0, pg_ref[...].astype(F32), 0.0)
        uv, ug = uv_ref[...].astype(F32), ug_ref[...].astype(F32)
        val = _conv_taps(uv, *_shifts_down(uv, pv), wv_ref[...], bv_ref[...])
        gate = _conv_taps(ug, *_shifts_down(ug, pg), wg_ref[...], bg_ref[...])
        o_ref[...] = (gate * jax.nn.sigmoid(gate) * val).astype(o_ref.dtype)

    def cur(off):
        return pl.BlockSpec((None, tc, FF_SHARD), lambda b, j, c: (j + off, b * nc + c, 0))

    def prv(off):
        return pl.BlockSpec((None, CONV_HALO, FF_SHARD), lambda b, j, c: (j + off, jnp.maximum((b * nc + c) * hb - 1, 0), 0))

    def par(rows, off):
        return pl.BlockSpec((None, rows, FF_SHARD), lambda b, j, c: (j + off, 0, 0))

    return pl.pallas_call(
        body, name="conv_fwd", grid=(bsz, half, nc),
        in_specs=[cur(0), cur(half), prv(0), prv(half), par(3, 0), par(3, half), par(1, 0), par(1, half)],
        out_specs=cur(0), out_shape=jax.ShapeDtypeStruct((half, bsz * seq, FF_SHARD), BF16),
        compiler_params=_params(("parallel", "parallel", "parallel")),
    )(up, up, up, up, cw, cw, cb, cb)


def conv_bwd_taps(up, cw, cb, dact, bsz, seq):
    tc = _tile(seq, 512)
    nc = seq // tc
    hb = tc // CONV_HALO
    half = N_DEV // 2

    def body(uv_ref, ug_ref, pv_ref, pg_ref, wv_ref, wg_ref, bv_ref, bg_ref, da_ref,
             dc_ref, dwv_ref, dwg_ref, dbv_ref, dbg_ref):
        b, c = pl.program_id(1), pl.program_id(2)

        @pl.when((b == 0) & (c == 0))
        def _():
            for r in (dwv_ref, dwg_ref, dbv_ref, dbg_ref):
                r[...] = jnp.zeros_like(r)

        pv = jnp.where(c > 0, pv_ref[...].astype(F32), 0.0)
        pg = jnp.where(c > 0, pg_ref[...].astype(F32), 0.0)
        uv, ug = uv_ref[...].astype(F32), ug_ref[...].astype(F32)
        uv1, uv2 = _shifts_down(uv, pv)
        ug1, ug2 = _shifts_down(ug, pg)
        val = _conv_taps(uv, uv1, uv2, wv_ref[...], bv_ref[...])
        gate = _conv_taps(ug, ug1, ug2, wg_ref[...], bg_ref[...])
        sg = jax.nn.sigmoid(gate)
        da = da_ref[...].astype(F32)
        dsilu = da * sg
        dval = dsilu * gate
        dgate = dsilu * val * (1.0 + gate * (1.0 - sg))
        dc_ref[0] = dval.astype(dc_ref.dtype)
        dc_ref[1] = dgate.astype(dc_ref.dtype)
        for dcv, taps, dw_ref, db_ref in ((dval, (uv2, uv1, uv), dwv_ref, dbv_ref),
                                          (dgate, (ug2, ug1, ug), dwg_ref, dbg_ref)):
            db_ref[...] += jnp.sum(dcv, axis=0, keepdims=True)
            for k, u_k in enumerate(taps):
                dw_ref[k:k + 1, :] += jnp.sum(dcv * u_k, axis=0, keepdims=True)

    def cur(off):
        return pl.BlockSpec((None, tc, FF_SHARD), lambda j, b, c: (j + off, b * nc + c, 0))

    def prv(off):
        return pl.BlockSpec((None, CONV_HALO, FF_SHARD), lambda j, b, c: (j + off, jnp.maximum((b * nc + c) * hb - 1, 0), 0))

    def par(rows, off):
        return pl.BlockSpec((None, rows, FF_SHARD), lambda j, b, c: (j + off, 0, 0))

    t = bsz * seq
    hs = jax.ShapeDtypeStruct((2, half, t, FF_SHARD), BF16)
    ws = jax.ShapeDtypeStruct((half, 3, FF_SHARD), F32)
    bs = jax.ShapeDtypeStruct((half, 1, FF_SHARD), F32)
    dc, dwv, dwg, dbv, dbg = pl.pallas_call(
        body, name="conv_bwd_taps", grid=(half, bsz, nc),
        in_specs=[cur(0), cur(half), prv(0), prv(half), par(3, 0), par(3, half), par(1, 0), par(1, half), cur(0)],
        out_specs=[pl.BlockSpec((2, None, tc, FF_SHARD), lambda j, b, c: (0, j, b * nc + c, 0)),
                   par(3, 0), par(3, 0), par(1, 0), par(1, 0)],
        out_shape=[hs, ws, ws, bs, bs],
        compiler_params=_params(("parallel", "arbitrary", "arbitrary")),
    )(up, up, up, up, cw, cw, cb, cb, dact)
    return (dc.reshape(N_DEV, t, FF_SHARD), jnp.concatenate([dwv, dwg], axis=0),
            jnp.concatenate([dbv, dbg], axis=0))


def conv_bwd_input(dconv, cw, bsz, seq):
    tc = _tile(seq, 1024)
    nc = seq // tc
    hb = tc // CONV_HALO
    nblk = bsz * seq // CONV_HALO

    def body(d_ref, n_ref, w_ref, o_ref):
        c = pl.program_id(2)
        nxt = jnp.where(c < nc - 1, n_ref[...].astype(F32), 0.0)
        d = d_ref[...].astype(F32)
        d1, d2 = _shifts_up(d, nxt)
        w = w_ref[...]
        o_ref[...] = (w[2:3] * d + w[1:2] * d1 + w[0:1] * d2).astype(o_ref.dtype)

    cur = pl.BlockSpec((None, tc, FF_SHARD), lambda j, b, c: (j, b * nc + c, 0))
    return pl.pallas_call(
        body, name="conv_bwd_input", grid=(N_DEV, bsz, nc),
        in_specs=[cur, pl.BlockSpec((None, CONV_HALO, FF_SHARD),
                                    lambda j, b, c: (j, jnp.minimum((b * nc + c + 1) * hb, nblk - 1), 0)),
                  pl.BlockSpec((None, 3, FF_SHARD), lambda j, b, c: (j, 0, 0))],
        out_specs=cur, out_shape=jax.ShapeDtypeStruct(dconv.shape, BF16),
        compiler_params=_params(("parallel", "parallel", "parallel")),
    )(dconv, dconv, cw)


def _my_index():
    return 4 * lax.axis_index("x") + 2 * lax.axis_index("y") + lax.axis_index("c")


def _peer(k):
    return (lax.axis_index("x") ^ ((k >> 2) & 1), lax.axis_index("y") ^ ((k >> 1) & 1),
            lax.axis_index("c") ^ (k & 1))


_HBM = pl.BlockSpec(memory_space=pltpu.HBM)
_SEM = pl.BlockSpec(memory_space=pltpu.SEMAPHORE)
_DATAFLOW = pltpu.SideEffectType.DATAFLOW_SIDE_EFFECTING


def _split_copies(gather, src_ref, land_ref, send_sems, recv_sems, local_sem):
    me = _my_index()

    def part(j):
        return src_ref if gather else src_ref.at[j]

    local = pltpu.make_async_copy(part(me), land_ref.at[me], local_sem)
    sends = [pltpu.make_async_remote_copy(
        src_ref=part(me ^ k), dst_ref=land_ref.at[me], send_sem=send_sems.at[k - 1], recv_sem=recv_sems.at[k - 1],
        device_id=_peer(k), device_id_type=pl.DeviceIdType.MESH) for k in range(1, N_DEV)]
    recvs = [pltpu.make_async_remote_copy(
        src_ref=part(me ^ k), dst_ref=land_ref.at[me ^ k], send_sem=send_sems.at[k - 1], recv_sem=recv_sems.at[k - 1],
        device_id=_peer(k), device_id_type=pl.DeviceIdType.MESH) for k in range(1, N_DEV)]
    return local, sends, recvs


def split_start(name, srcs, gather):
    n = len(srcs)
    lands = [((N_DEV,) + s.shape) if gather else s.shape for s in srcs]

    def body(*refs):
        ins, outs = refs[:2 * n], refs[2 * n:]
        for i in range(n):
            local, sends, _ = _split_copies(gather, ins[i], ins[n + i], *outs[3 * i:3 * i + 3])
            local.start()
            for cp in sends:
                cp.start()
        outs[-1][...] = jnp.zeros_like(outs[-1])

    dma7 = pltpu.SemaphoreType.DMA((N_DEV - 1,))
    out = pl.pallas_call(
        body, name=name,
        out_shape=(dma7, dma7, pltpu.SemaphoreType.DMA(())) * n
                  + tuple(pltpu.HBM(s.shape, s.dtype) for s in srcs)
                  + tuple(pltpu.HBM(shape, s.dtype) for shape, s in zip(lands, srcs))
                  + (jax.ShapeDtypeStruct((8, 128), F32),),
        in_specs=(_HBM,) * (2 * n),
        out_specs=(_SEM,) * (3 * n) + (_HBM,) * (2 * n) + (pl.BlockSpec(memory_space=pltpu.VMEM),),
        input_output_aliases={i: 3 * n + i for i in range(2 * n)},
        compiler_params=pltpu.CompilerParams(has_side_effects=_DATAFLOW),
    )(*[pltpu.with_memory_space_constraint(s, pltpu.HBM) for s in srcs],
      *[pltpu.with_memory_space_constraint(lax.empty(shape, s.dtype), pltpu.HBM) for shape, s in zip(lands, srcs)])
    handles = [tuple(out[3 * i:3 * i + 3]) + (out[3 * n + i], out[4 * n + i]) for i in range(n)]
    return handles, out[-1][0, 0]


def split_wait(name, handles, after, gather):
    send_sems, recv_sems, local_sem, src_thru, land_thru = handles

    def body(src_ref, land_ref, send_sems, recv_sems, local_sem, after_ref, src_dead, got_ref, token):
        local, sends, recvs = _split_copies(gather, src_ref, land_ref, send_sems, recv_sems, local_sem)
        local.wait()
        for cp in recvs:
            cp.wait_send()
            cp.wait_recv()
        token[...] = jnp.zeros_like(token)

    out = pl.pallas_call(
        body, name=name,
        out_shape=(pltpu.HBM(src_thru.shape, src_thru.dtype), pltpu.HBM(land_thru.shape, land_thru.dtype),
                   jax.ShapeDtypeStruct((8, 128), F32)),
        in_specs=(_HBM, _HBM, _SEM, _SEM, _SEM, pl.BlockSpec(memory_space=pl.ANY)),
        out_specs=(_HBM, _HBM, pl.BlockSpec(memory_space=pltpu.VMEM)),
        input_output_aliases={0: 0, 1: 1},
        compiler_params=pltpu.CompilerParams(has_side_effects=_DATAFLOW),
    )(src_thru, land_thru, send_sems, recv_sems, local_sem, after)
    return out[1], out[2][0, 0]


def sum_parts(name, r):
    _, rows, cols = r.shape

    def body(r_ref, o_ref):
        acc = r_ref[0].astype(F32)
        for s in range(1, N_DEV):
            acc = acc + r_ref[s].astype(F32)
        o_ref[...] = acc

    return pl.pallas_call(body, name=name, out_shape=jax.ShapeDtypeStruct((rows, cols), F32),
                          compiler_params=_params())(r)


def adamw(name, w, m, v, parts=None, g=None, layer=0, into=None, order=None):
    _, rows, cols = w.shape
    br = _tile(rows, 256, 16)
    c1 = 1.0 / (1.0 - ADAM_B1 ** ADAM_STEP)
    c2 = 1.0 / (1.0 - ADAM_B2 ** ADAM_STEP)

    def body(g_ref, w_ref, m_ref, v_ref, *rest):
        og_ref, od_ref, om_ref, ov_ref = rest[-4:]
        if parts is None:
            gs = g_ref[...]
        else:
            gs = g_ref[0].astype(F32)
            for s in range(1, N_DEV):
                gs = gs + g_ref[s].astype(F32)
        mn = ADAM_B1 * m_ref[...] + (1.0 - ADAM_B1) * gs
        vn = ADAM_B2 * v_ref[...] + (1.0 - ADAM_B2) * (gs * gs)
        og_ref[...] = gs
        om_ref[...] = mn
        ov_ref[...] = vn
        od_ref[...] = -ADAM_LR * ((mn * c1) / (jnp.sqrt(vn * c2) + ADAM_EPS) + ADAM_WD * w_ref[...])

    blk = pl.BlockSpec((None, br, cols), lambda i: (layer, i, 0))
    if parts is None:
        gspec = pl.BlockSpec((br, cols), lambda i: (i, 0))
    else:
        gspec = pl.BlockSpec((N_DEV, br, cols), lambda i: (0, i, 0))
    earlier = [] if into is None else list(into)
    behind = [] if order is None else [order]
    return pl.pallas_call(
        body, name=name, grid=(rows // br,),
        in_specs=[gspec, blk, blk, blk] + [pl.BlockSpec(memory_space=pl.ANY)] * len(earlier)
                 + [pl.BlockSpec((1, 128), lambda i: (0, 0))] * len(behind),
        out_specs=[blk] * 4, out_shape=[jax.ShapeDtypeStruct(w.shape, F32)] * 4,
        input_output_aliases={4 + k: k for k in range(len(earlier))},
        compiler_params=_params(("parallel",)),
    )(g if parts is None else parts, w, m, v, *earlier, *behind)


SMALL = ("norm_mix", "norm_xattn", "norm_ffn", "norm_mem", "norm_final", "pool_w", "pool_scale",
         "ssm_lam_re", "ssm_lam_im", "ssm_log_dt", "ssm_b_re", "ssm_b_im", "ssm_c_re", "ssm_c_im",
         "ffn_conv_b", "ssm_d", "ffn_conv_w")
SMALL_SHARDED = {"ssm_d": 1, "ffn_conv_w": 2}
BIG = ("ab_w_in", "ab_w_out", "ssm_w_in", "ssm_w_glu", "xa_w_q", "xa_w_kv", "xa_w_o", "ffn_w_up", "ffn_w_down")
WEIGHTS = ("norm_mix", "norm_xattn", "norm_ffn", "norm_mem", "norm_final", "ab_w_in", "pool_w", "pool_scale",
           "ab_w_out", "ssm_w_in", "ssm_lam_re", "ssm_lam_im", "ssm_log_dt", "ssm_b_re", "ssm_b_im", "ssm_c_re",
           "ssm_c_im", "ssm_d", "ssm_w_glu", "xa_w_q", "xa_w_kv", "xa_w_o", "ffn_w_up", "ffn_conv_w", "ffn_conv_b",
           "ffn_w_down")


def _rows8(g):
    return g.reshape(N_DEV, g.size // (N_DEV * D_MODEL), D_MODEL)


def _square(a):
    return a.reshape(D_MODEL, D_MODEL)


_LAYOUT = {"ab_w_out": _square, "ssm_w_in": _square, "xa_w_q": _square, "xa_w_o": _square,
           "ffn_w_down": lambda a: a.reshape(N_DEV // 2, FF_SHARD, D_MODEL)}
GATHER_ORDER = (("ab_w_in", 0), ("ffn_conv_w", None), ("ssm_d", None), ("ab_w_out", 0), ("xa_w_q", 0),
                ("xa_w_kv", 0), ("xa_w_o", 0), ("ffn_w_up", 0), ("ffn_w_down", 0), ("ffn_w_up", 1),
                ("ffn_w_down", 1), ("ssm_w_in", 0), ("ssm_w_glu", 0), ("xa_w_q", 1), ("xa_w_kv", 1), ("xa_w_o", 1))
GATHER_FIRST = 3
GATHER_AHEAD = 7


class _Step:
    def __init__(self, master, small):
        self.master, self.small = master, small
        self.pending, self.gathers, self.weights, self.sent, self.queued = [], {}, {}, [], []

    def follow(self, v):
        for z in self.pending:
            v = v + z
        self.pending = []
        return v

    def start_gathers(self, upto, zero):
        todo = GATHER_ORDER[len(self.gathers):upto]
        if not todo:
            return
        shards = []
        for n, l in todo:
            if l is None:
                shards.append(self.master[n] + zero)
            else:
                shards.append((self.master[n][l] + zero).astype(MXU_DTYPE))
        handles, z = split_start(f"ags_{len(self.gathers)}", shards, gather=True)
        self.gathers.update(zip(todo, handles))
        self.pending.append(z)

    def weight(self, n, l, after):
        if (n, l) not in self.weights:
            full, z = split_wait(f"agw_{n}{'' if l is None else l}", self.gathers[(n, l)], after, gather=True)
            self.weights[(n, l)] = _LAYOUT.get(n, lambda a: a)(full)
            self.start_gathers(GATHER_ORDER.index((n, l)) + 1 + GATHER_AHEAD, z)
        return self.weights[(n, l)]

    def send_grad(self, n, l, part, flush=True):
        self.queued.append((n, l, part))
        if flush:
            handles, z = split_start(f"xs_{n}{l}", [p for _, _, p in self.queued], gather=False)
            self.sent += [(qn, ql, h) for (qn, ql, _), h in zip(self.queued, handles)]
            self.queued = []
            self.pending.append(z)


def _layer_tail(st, l, x_in, hq, mem_n, acts, next_gain=None):
    bsz, seq = acts["bsz"], acts["seq"]
    p = st.small
    q = mm_nn(f"xa_q{l}", hq, st.weight("xa_w_q", l, x_in))
    kv = mm_nn_bs(f"xa_kv{l}", mem_n, st.weight("xa_w_kv", l, x_in))
    o = xattn_fwd(q, kv, bsz, seq)
    x_mid, hf = mm_nn(f"xa_o{l}", o, st.weight("xa_w_o", l, o), res=x_in, out_dtype=F32,
                      norm_gain=st.follow(p["norm_ffn"][l]))
    up = mm_nn_bs(f"ffn_up{l}", hf, st.weight("ffn_w_up", l, x_mid), stacked_out=True)
    conv_w = st.weight("ffn_conv_w", None, x_mid)[:, l]
    act = conv_fwd(up, conv_w, p["ffn_conv_b"][l], bsz, seq)
    w_down = st.weight("ffn_w_down", l, act)
    if next_gain is None:
        x_out, h_next = mm_as_nn(f"ffn_down{l}", act, w_down, res=x_mid), None
    else:
        x_out, h_next = mm_as_nn(f"ffn_down{l}", act, w_down, res=x_mid, norm_gain=st.follow(next_gain))
    acts[l].update(x_in=x_in, hq=hq, q=q, kv=kv, o=o, x_mid=x_mid, hf=hf, up=up, act=act)
    return x_out, h_next


def _layer_tail_bwd(st, l, dx, mem_n, acts, grads):
    a = acts[l]
    bsz, seq = acts["bsz"], acts["seq"]
    p = st.small
    dact = mm_nt_os(f"d_act{l}", dx, st.weight("ffn_w_down", l, dx))
    st.send_grad("ffn_w_down", l, _rows8(mm_tn(f"g_ffn_down{l}", a["act"], dx, a_stacked=True)), flush=False)
    conv_w = st.weight("ffn_conv_w", None, dx)[:, l]
    dconv, dcw, dcb = conv_bwd_taps(a["up"], conv_w, p["ffn_conv_b"][l], dact, bsz, seq)
    grads["ffn_conv_w"][l] = dcw
    grads["ffn_conv_b"][l] = dcb
    dup = conv_bwd_input(dconv, conv_w, bsz, seq)
    dx_mid, grads["norm_ffn"][l] = mm_nt_bs(f"d_hf{l}", dup, st.weight("ffn_w_up", l, dx), dc_stacked=True,
                                            rms=(a["x_mid"], st.follow(p["norm_ffn"][l]), dx))
    st.send_grad("ffn_w_up", l, mm_tn(f"g_ffn_up{l}", a["hf"], dup, dc_stacked=True))
    do = mm_nt(f"d_o{l}", dx_mid, st.weight("xa_w_o", l, dx))
    st.send_grad("xa_w_o", l, _rows8(mm_tn(f"g_xa_o{l}", a["o"], dx_mid)), flush=False)
    dq, dk, dv = xattn_bwd(a["q"], a["kv"], do, bsz, seq)
    dkv = jnp.concatenate([dk, dv], axis=1).astype(BF16)
    dx_in, grads["norm_xattn"][l] = mm_nt(f"d_hq{l}", dq, st.weight("xa_w_q", l, dx),
                                          rms=(a["x_in"], st.follow(p["norm_xattn"][l]), dx_mid))
    st.send_grad("xa_w_q", l, _rows8(mm_tn(f"g_xa_q{l}", a["hq"], dq)), flush=False)
    dmem_n = mm_nt_bs(f"d_memn{l}", dkv, st.weight("xa_w_kv", l, dx), out_dtype=F32)
    st.send_grad("xa_w_kv", l, mm_tn(f"g_xa_kv{l}", mem_n, dkv, dc_cols=2 * D_MODEL // N_DEV))
    return dx_in, dmem_n


def kernel(x, mem, norm_mix, norm_xattn, norm_ffn, norm_mem, norm_final, ab_w_in, pool_w, pool_scale, ab_w_out, ssm_w_in, ssm_lam_re, ssm_lam_im, ssm_log_dt, ssm_b_re, ssm_b_im, ssm_c_re, ssm_c_im, ssm_d, ssm_w_glu, xa_w_q, xa_w_kv, xa_w_o, ffn_w_up, ffn_conv_w, ffn_conv_b, ffn_w_down, loss_target, m_norm_mix, m_norm_xattn, m_norm_ffn, m_norm_mem, m_norm_final, m_ab_w_in, m_pool_w, m_pool_scale, m_ab_w_out, m_ssm_w_in, m_ssm_lam_re, m_ssm_lam_im, m_ssm_log_dt, m_ssm_b_re, m_ssm_b_im, m_ssm_c_re, m_ssm_c_im, m_ssm_d, m_ssm_w_glu, m_xa_w_q, m_xa_w_kv, m_xa_w_o, m_ffn_w_up, m_ffn_conv_w, m_ffn_conv_b, m_ffn_w_down, v_norm_mix, v_norm_xattn, v_norm_ffn, v_norm_mem, v_norm_final, v_ab_w_in, v_pool_w, v_pool_scale, v_ab_w_out, v_ssm_w_in, v_ssm_lam_re, v_ssm_lam_im, v_ssm_log_dt, v_ssm_b_re, v_ssm_b_im, v_ssm_c_re, v_ssm_c_im, v_ssm_d, v_ssm_w_glu, v_xa_w_q, v_xa_w_kv, v_xa_w_o, v_ffn_w_up, v_ffn_conv_w, v_ffn_conv_b, v_ffn_w_down):
    given = dict(locals())
    master = {n: given[n] for n in WEIGHTS}
    mom1 = {n: given["m_" + n] for n in WEIGHTS}
    mom2 = {n: given["v_" + n] for n in WEIGHTS}
    bsz, seq, d = x.shape
    t = bsz * seq
    me = _my_index()

    st = _Step(master, {"norm_xattn": norm_xattn, "norm_ffn": norm_ffn,
                        "ffn_conv_b": [ffn_conv_b[l].reshape(N_DEV, 1, FF_SHARD) for l in range(2)]})
    st.start_gathers(GATHER_FIRST, 0.0)
    zero = st.follow(jnp.zeros((), F32))

    acts = {"bsz": bsz, "seq": seq, 0: {}, 1: {}}
    x0 = x.reshape(t, d)
    mem2 = mem.reshape(bsz * MEM_LEN, d)
    mem_n = rms_fwd("rms_mem", mem2, norm_mem + zero)
    pscale = pool_scale.reshape(1, SB_WIDTH)

    h0 = rms_fwd("rms_mix0", x0, norm_mix[0] + zero)
    w_in = st.weight("ab_w_in", 0, h0)
    proj = mm_nn_bs("ab_in", h0, w_in, out_dtype=F32)
    a_out, rsum = sb_attn_fwd(proj, st.follow(jnp.zeros((1, 128), F32)), bsz, seq)
    p_out = pool_fwd(proj, pool_w[0], pscale, bsz, seq)
    w_out = st.weight("ab_w_out", 0, a_out)
    x1 = mm_nn("ab_out_a", a_out, w_out, res=x0, out_dtype=F32)
    x1, hq0 = mm_nn("ab_out_p", p_out, w_out, res=x1, koff=SB_WIDTH, out_dtype=F32,
                    norm_gain=st.follow(norm_xattn[0]))
    x3, h1 = _layer_tail(st, 0, x1, hq0, mem_n, acts, next_gain=norm_mix[1])

    b_re2 = ssm_b_re.reshape(64, 1024)
    b_im2 = ssm_b_im.reshape(64, 1024)
    log_dt = ssm_log_dt.reshape(64, 1)
    lb_re, lb_im, bb_re2, bb_im2 = ssm_prep(ssm_lam_re[0], ssm_lam_im[0], log_dt, b_re2, b_im2)
    wt = _ssm_in_weights(bb_re2, bb_im2)
    ct = _ssm_out_weights(ssm_c_re[0], ssm_c_im[0])
    a_re = lb_re.reshape(1, SSM_STATES)
    a_im = lb_im.reshape(1, SSM_STATES)
    u = mm_nn("ssm_in", h1, st.weight("ssm_w_in", 0, x3), out_dtype=F32)
    dskip = st.weight("ssm_d", None, x3).reshape(1, D_MODEL)
    y, gl, h_re, h_im = ssm_fwd(u, wt, ct, a_re, a_im, dskip, bsz, seq)
    glu = mm_nn_bs("ssm_glu", gl, st.weight("ssm_w_glu", 0, gl), out_dtype=F32)
    x4, hq1 = glu_fwd(glu, x3, st.follow(norm_xattn[1]))
    x6, _ = _layer_tail(st, 1, x4, hq1, mem_n, acts)

    loss_row, dx, g_norm_final = loss_head(x6, norm_final, loss_target.reshape(t, d))
    loss = lax.psum(loss_row[0, 0], MESH_AXES)

    grads = {n: [None, None] for n in ("ffn_conv_w", "ffn_conv_b", "norm_ffn", "norm_xattn", "norm_mix")}
    dx4, dmem_1 = _layer_tail_bwd(st, 1, dx, mem_n, acts, grads)
    dglu = glu_bwd(glu, dx4)
    dgl = mm_nt_bs("d_gl", dglu, st.weight("ssm_w_glu", 0, dx))
    st.send_grad("ssm_w_glu", 0, mm_tn("g_ssm_glu", gl, dglu, dc_cols=2 * D_MODEL // N_DEV), flush=False)
    du, dwt, dct, g_dskip, da_re, da_im = ssm_bwd(dgl, y, u, h_re, h_im, wt, ct, a_re, a_im, dskip, bsz, seq)
    dbb_re, dbb_im = _ssm_in_weights_bwd(dwt)
    g_c_re, g_c_im = _ssm_out_weights_bwd(dct)
    g_lam_re, g_lam_im, g_log_dt, g_b_re, g_b_im = ssm_prep_bwd(
        ssm_lam_re[0], ssm_lam_im[0], log_dt, b_re2, b_im2, da_re.reshape(64, 64), da_im.reshape(64, 64),
        dbb_re, dbb_im)
    dx3, grads["norm_mix"][1] = mm_nt("d_h1", du, st.weight("ssm_w_in", 0, dx),
                                      rms=(x3, st.follow(norm_mix[1]), dx4))
    st.send_grad("ssm_w_in", 0, _rows8(mm_tn("g_ssm_in", h1, du)))

    dx1, dmem_0 = _layer_tail_bwd(st, 0, dx3, mem_n, acts, grads)
    dcat = mm_nt("d_cat", dx1, st.weight("ab_w_out", 0, dx))
    st.send_grad("ab_w_out", 0, _rows8(jnp.concatenate(
        [mm_tn("g_ab_out_a", a_out, dx1), mm_tn("g_ab_out_p", p_out, dx1)], axis=0)), flush=False)
    dq, dk, dv = sb_attn_bwd(proj, rsum, dcat, bsz, seq)
    dpu, g_pool_w, g_pool_scale = pool_bwd(proj, pool_w[0], st.follow(pscale), dcat, bsz, seq)
    dproj = jnp.concatenate([dq, dk, dv, dpu], axis=1).astype(BF16)
    st.send_grad("ab_w_in", 0, mm_tn("g_ab_in", h0, dproj, dc_cols=2 * D_MODEL // N_DEV))
    dx0, grads["norm_mix"][0] = mm_nt_bs("d_h0", dproj, st.weight("ab_w_in", 0, dx),
                                         rms=(x0, st.follow(norm_mix[0]), dx1))
    _, g_norm_mem = rms_bwd("rms_mem_bwd", mem2, norm_mem, dmem_0 + dmem_1, need_dx=False)

    small_g = {
        "norm_mix": jnp.stack([g[0] for g in grads["norm_mix"]]),
        "norm_xattn": jnp.stack([g[0] for g in grads["norm_xattn"]]),
        "norm_ffn": jnp.stack([g[0] for g in grads["norm_ffn"]]),
        "norm_mem": g_norm_mem[0], "norm_final": g_norm_final[0],
        "pool_w": g_pool_w[None], "pool_scale": g_pool_scale,
        "ssm_lam_re": g_lam_re[None], "ssm_lam_im": g_lam_im[None], "ssm_log_dt": g_log_dt.reshape(1, 64),
        "ssm_b_re": g_b_re.reshape(1, 64, 64, 16), "ssm_b_im": g_b_im.reshape(1, 64, 64, 16),
        "ssm_c_re": g_c_re[None], "ssm_c_im": g_c_im[None],
        "ffn_conv_b": jnp.stack([g.reshape(2 * D_FF) for g in grads["ffn_conv_b"]]),
        "ssm_d": g_dskip,
        "ffn_conv_w": jnp.stack([g.transpose(1, 0, 2).reshape(3, 2 * D_FF) for g in grads["ffn_conv_w"]]),
    }
    sizes = [int(small_g[n].size) for n in SMALL]
    total = sum(sizes)
    rows8 = -(-total // (N_DEV * 128 * 8)) * 8
    flat = jnp.concatenate([small_g[n].reshape(-1).astype(F32) for n in SMALL]
                           + [jnp.zeros((N_DEV * rows8 * 128 - total,), F32)])
    (in_flight,), z = split_start("xs_small", [flat.reshape(N_DEV, rows8, 128)], gather=False)
    st.pending.append(z)
    stepped, last = {}, dx0
    for i, (n, l, handles) in enumerate(st.sent):
        if i == len(st.sent) // 2:
            recv, _ = split_wait("xw_small", in_flight, last, gather=False)
            (in_flight,), z = split_start("ags_small", [sum_parts("sum_small", recv)], gather=True)
            st.pending.append(z)
        recv, _ = split_wait(f"xw_{n}{l}", handles, dx0, gather=False)
        shape3 = (master[n].shape[0],) + recv.shape[1:]
        stepped[n] = adamw(f"adamw_{n}{l}", master[n].reshape(shape3), mom1[n].reshape(shape3),
                           mom2[n].reshape(shape3), parts=recv, layer=l, into=stepped.get(n),
                           order=st.follow(jnp.zeros((1, 128), F32)))
        last = stepped[n][0]
    out_g, out_d, out_m, out_v = ({n: stepped[n][k].reshape(master[n].shape) for n in BIG} for k in range(4))
    summed = split_wait("agw_small", in_flight, last, gather=True)[0].reshape(-1)

    def local_part(name, a):
        ax = SMALL_SHARDED.get(name)
        if ax is None:
            return a
        n_loc = a.shape[ax] // N_DEV
        return lax.dynamic_slice_in_dim(a, me * n_loc, n_loc, axis=ax)

    off = 0
    for n, sz in zip(SMALL, sizes):
        g_n = local_part(n, summed[off:off + sz].reshape(small_g[n].shape))
        off += sz
        cols = g_n.shape[-1] if g_n.shape[-1] >= 128 or g_n.ndim < 3 else g_n.shape[-1] * g_n.shape[-2]
        shape3 = (1, g_n.size // cols, cols)
        res = adamw("adamw_" + n, master[n].reshape(shape3), mom1[n].reshape(shape3), mom2[n].reshape(shape3),
                    g=g_n.reshape(shape3[1:]))
        for dst, r in zip((out_g, out_d, out_m, out_v), res):
            dst[n] = r.reshape(master[n].shape)

    return (loss, dx0.reshape(bsz, seq, d), *[out_g[n] for n in WEIGHTS], *[out_d[n] for n in WEIGHTS],
            *[out_m[n] for n in WEIGHTS], *[out_v[n] for n in WEIGHTS])
```

```python
import math

import jax
import jax.numpy as jnp
from jax import lax
from jax.experimental import pallas as pl
from jax.experimental.pallas import tpu as pltpu

F32 = jnp.float32
BF16 = jnp.bfloat16
MXU_DTYPE = jnp.bfloat16
N_DEV = 8
MESH_AXES = ("x", "y", "c")

D_MODEL = 1024
SB_HEAD_DIM = 64
SB_WIDTH = 512
SB_BLOCK = 256
POOL_WINDOWS = (2, 4, 8, 16)
POOL_GROUP = 128
POOL_HALO = 16
SSM_TILES = 8
SSM_TILE_STATES = 512
SSM_STATES = 4096
SSM_LANES = 1024
MEM_LEN = 256
XA_HEADS = 4
XA_HEAD_DIM = 256
D_FF = 2816
FF_SHARD = 704
EPS = 1e-6
ADAM_LR = 0.001
ADAM_B1 = 0.9
ADAM_B2 = 0.999
ADAM_EPS = 1e-08
ADAM_WD = 0.01
ADAM_STEP = 10
VMEM_LIMIT = 56 * 1024 * 1024

_NN = (((1,), (0,)), ((), ()))
_NT = (((1,), (1,)), ((), ()))
_TN = (((0,), (0,)), ((), ()))


def _params(sem=None):
    if sem is None:
        return pltpu.CompilerParams(vmem_limit_bytes=VMEM_LIMIT)
    return pltpu.CompilerParams(dimension_semantics=sem, vmem_limit_bytes=VMEM_LIMIT)


def _tile(n, pref, mult=8):
    if n <= pref:
        return n
    for t in range(pref, 0, -1):
        if n % t == 0 and t % mult == 0:
            return t
    return n


def _dot(a, b, dims):
    return lax.dot_general(a.astype(MXU_DTYPE), b.astype(MXU_DTYPE), dims, preferred_element_type=F32)


def _dot_exact01(x, m01, dims=_NN):
    x1 = x.astype(BF16)
    r1 = x - x1.astype(F32)
    x2 = r1.astype(BF16)
    x3 = (r1 - x2.astype(F32)).astype(BF16)
    m = m01.astype(BF16)
    out = lax.dot_general(x1, m, dims, preferred_element_type=F32)
    out = out + lax.dot_general(x2, m, dims, preferred_element_type=F32)
    return out + lax.dot_general(x3, m, dims, preferred_element_type=F32)


def _mm(name, a, b, dims, grid, a_spec, b_spec, o_spec, out_shape, out_dtype, acc_shape, res=None, r_spec=None,
        group=1, n=None, a_sel="full", b_sel="full", o_sel="full", norm_gain=None, rms=None):
    nk = grid[2]
    if out_dtype is None:
        out_dtype = BF16
    n_out = out_shape[-1]
    vec = pl.BlockSpec((1, n_out), lambda i, j, kk: (0, 0))

    def at(sel, s):
        if sel == "lead":
            return (s,)
        if sel == "lanes":
            return (slice(None), slice(s * n, (s + 1) * n))
        return (Ellipsis,)

    extra = [] if res is None else [(res, r_spec)]
    if norm_gain is not None:
        extra.append((norm_gain.reshape(1, n_out), vec))
    if rms is not None:
        extra += [(rms[0], o_spec), (rms[1].reshape(1, n_out), vec), (rms[2], o_spec)]
    n_in = 2 + len(extra)
    if rms is not None:
        out_specs = [o_spec, vec]
        out_shapes = [jax.ShapeDtypeStruct(out_shape, F32), jax.ShapeDtypeStruct((1, n_out), F32)]
    elif norm_gain is not None:
        out_specs = [o_spec, o_spec]
        out_shapes = [jax.ShapeDtypeStruct(out_shape, out_dtype), jax.ShapeDtypeStruct(out_shape, BF16)]
    else:
        out_specs, out_shapes = o_spec, jax.ShapeDtypeStruct(out_shape, out_dtype)

    def body(*refs):
        a_ref, b_ref = refs[0], refs[1]
        ins = list(refs[2:n_in])
        r_ref = ins.pop(0) if res is not None else None
        outs = refs[n_in:]
        o_ref = outs[0]
        acc = refs[-1] if nk > 1 else None
        k = pl.program_id(2)

        def finish(val):
            if r_ref is not None:
                val = val + r_ref[...].astype(F32)
            if rms is not None:
                x_ref, g_ref, d_ref = ins
                xf = x_ref[...]
                r = lax.rsqrt(jnp.mean(xf * xf, axis=-1, keepdims=True) + EPS)
                xh = xf * r
                part = jnp.sum(val * xh, axis=0, keepdims=True)
                first = pl.program_id(0) == 0

                @pl.when(first)
                def _():
                    outs[1][...] = part

                @pl.when(jnp.logical_not(first))
                def _():
                    outs[1][...] += part

                dxh = val * g_ref[...]
                o_ref[...] = d_ref[...] + r * (dxh - xh * jnp.mean(dxh * xh, axis=-1, keepdims=True))
                return
            o_ref[...] = val.astype(out_dtype)
            if norm_gain is not None:
                r = lax.rsqrt(jnp.mean(val * val, axis=-1, keepdims=True) + EPS)
                outs[1][...] = (val * r * ins[0][...]).astype(BF16)

        def emit(s, val):
            if nk == 1:
                if o_sel == "full":
                    finish(val)
                else:
                    o_ref[at(o_sel, s)] = val.astype(out_dtype)
                return

            @pl.when(k == 0)
            def _():
                acc[at(o_sel, s)] = val

            @pl.when(k > 0)
            def _():
                acc[at(o_sel, s)] += val

        total = None
        if a_sel == "full" and b_sel == "lanes":
            wide = _dot(a_ref[...], b_ref[...], dims)
            for s in range(group):
                emit(s, wide[:, s * n:(s + 1) * n])
        else:
            for s in range(group):
                val = _dot(a_ref[at(a_sel, s)], b_ref[at(b_sel, s)], dims)
                if o_sel == "full":
                    total = val if total is None else total + val
                else:
                    emit(s, val)
        if o_sel == "full":
            emit(0, total)
        if nk > 1:
            @pl.when(k == nk - 1)
            def _():
                if o_sel == "full":
                    finish(acc[...])
                else:
                    o_ref[...] = acc[...].astype(out_dtype)

    rows_sem = "arbitrary" if rms is not None else "parallel"
    return pl.pallas_call(
        body, name=name, grid=grid, in_specs=[a_spec, b_spec] + [s for _, s in extra], out_specs=out_specs,
        out_shape=out_shapes, scratch_shapes=[pltpu.VMEM(acc_shape, F32)] if nk > 1 else [],
        compiler_params=_params((rows_sem, rows_sem, "arbitrary")),
    )(a, b, *[x for x, _ in extra])


def _row_tile(m, epi):
    return _tile(m, 512 if epi.get("rms") is not None else 1024)


def mm_nn(name, a, b, res=None, koff=0, out_dtype=None, **epi):
    m, k = a.shape
    n = b.shape[1]
    tm, tn, tk = _row_tile(m, epi), _tile(n, 1024, 128), _tile(k, 1024, 128)
    kb = koff // tk
    spec = pl.BlockSpec((tm, tn), lambda i, j, kk: (i, j))
    return _mm(name, a, b, _NN, (m // tm, n // tn, k // tk),
               pl.BlockSpec((tm, tk), lambda i, j, kk: (i, kk)),
               pl.BlockSpec((tk, tn), lambda i, j, kk: (kk + kb, j)),
               spec, (m, n), out_dtype, (tm, tn), res, spec, **epi)


def mm_nn_bs(name, a, bs, stacked_out=False, out_dtype=None):
    m, k = a.shape
    s, _, n = bs.shape
    tm, tk = _tile(m, 1024), _tile(k, 1024, 128)
    a_spec = pl.BlockSpec((tm, tk), lambda i, j, kk: (i, kk))
    if stacked_out:
        return _mm(name, a, bs, _NN, (m // tm, s, k // tk), a_spec,
                   pl.BlockSpec((None, tk, n), lambda i, j, kk: (j, kk, 0)),
                   pl.BlockSpec((None, tm, n), lambda i, j, kk: (j, i, 0)), (s, m, n), out_dtype, (tm, n))
    g = _tile(s, max(1, 1024 // n), 1)
    return _mm(name, a, bs, _NN, (m // tm, s // g, k // tk), a_spec,
               pl.BlockSpec((g, tk, n), lambda i, j, kk: (j, kk, 0)),
               pl.BlockSpec((tm, g * n), lambda i, j, kk: (i, j)), (m, s * n), out_dtype, (tm, g * n),
               group=g, n=n, b_sel="lead", o_sel="lanes")


def mm_as_nn(name, a_st, b3, res, out_dtype=F32, **epi):
    s, m, kp = a_st.shape
    n = b3.shape[2]
    tm, tn = _row_tile(m, epi), _tile(n, 1024, 128)
    spec = pl.BlockSpec((tm, tn), lambda i, j, kk: (i, j))
    g = _tile(s, 2, 1)
    return _mm(name, a_st, b3, _NN, (m // tm, n // tn, s // g),
               pl.BlockSpec((g, tm, kp), lambda i, j, kk: (kk, i, 0)),
               pl.BlockSpec((g, kp, tn), lambda i, j, kk: (kk, 0, j)),
               spec, (m, n), out_dtype, (tm, tn), res, spec, group=g, a_sel="lead", b_sel="lead", **epi)


def mm_nt(name, dc, b, out_dtype=None, **epi):
    m, n = dc.shape
    k = b.shape[0]
    tm, tko, tnr = _row_tile(m, epi), _tile(k, 1024, 128), _tile(n, 1024, 128)
    return _mm(name, dc, b, _NT, (m // tm, k // tko, n // tnr),
               pl.BlockSpec((tm, tnr), lambda i, j, kk: (i, kk)),
               pl.BlockSpec((tko, tnr), lambda i, j, kk: (j, kk)),
               pl.BlockSpec((tm, tko), lambda i, j, kk: (i, j)), (m, k), out_dtype, (tm, tko), **epi)


def mm_nt_bs(name, dc, bs, dc_stacked=False, out_dtype=None, **epi):
    s, k, n = bs.shape
    m = dc.shape[1] if dc_stacked else dc.shape[0]
    tm, tko = (_tile(m, 1024) if dc_stacked else _row_tile(m, epi)), _tile(k, 1024, 128)
    o_spec = pl.BlockSpec((tm, tko), lambda i, j, kk: (i, j))
    if dc_stacked:
        g = _tile(s, 2, 1)
        return _mm(name, dc, bs, _NT, (m // tm, k // tko, s // g),
                   pl.BlockSpec((g, tm, n), lambda i, j, kk: (kk, i, 0)),
                   pl.BlockSpec((g, tko, n), lambda i, j, kk: (kk, j, 0)), o_spec, (m, k), out_dtype, (tm, tko),
                   group=g, a_sel="lead", b_sel="lead", **epi)
    g = _tile(s, max(1, 2048 // n), 1)
    return _mm(name, dc, bs, _NT, (m // tm, k // tko, s // g),
               pl.BlockSpec((tm, g * n), lambda i, j, kk: (i, kk)),
               pl.BlockSpec((g, tko, n), lambda i, j, kk: (kk, j, 0)), o_spec, (m, k), out_dtype, (tm, tko),
               group=g, n=n, a_sel="lanes", b_sel="lead", **epi)


def mm_nt_os(name, dc, b3, out_dtype=None):
    m, n = dc.shape
    s, kp, _ = b3.shape
    tm, tnr = _tile(m, 1024), _tile(n, 1024, 128)
    return _mm(name, dc, b3, _NT, (m // tm, s, n // tnr),
               pl.BlockSpec((tm, tnr), lambda i, j, kk: (i, kk)),
               pl.BlockSpec((None, kp, tnr), lambda i, j, kk: (j, 0, kk)),
               pl.BlockSpec((None, tm, kp), lambda i, j, kk: (j, i, 0)), (s, m, kp), out_dtype, (tm, kp))


def mm_tn(name, a, dc, a_stacked=False, dc_cols=None, dc_stacked=False, out_dtype=None):
    if a_stacked:
        s, m, kp = a.shape
        n = dc.shape[1]
        tno, tmr = _tile(n, 1024, 128), _tile(m, 2048)
        return _mm(name, a, dc, _TN, (s, n // tno, m // tmr),
                   pl.BlockSpec((None, tmr, kp), lambda i, j, kk: (i, kk, 0)),
                   pl.BlockSpec((tmr, tno), lambda i, j, kk: (kk, j)),
                   pl.BlockSpec((None, kp, tno), lambda i, j, kk: (i, 0, j)), (s, kp, n), out_dtype, (kp, tno))
    m, k = a.shape
    tko, tmr = _tile(k, 1024, 128), _tile(m, 2048)
    a_spec = pl.BlockSpec((tmr, tko), lambda i, j, kk: (kk, i))
    if dc_stacked:
        s, _, n = dc.shape
        return _mm(name, a, dc, _TN, (k // tko, s, m // tmr), a_spec,
                   pl.BlockSpec((None, tmr, n), lambda i, j, kk: (j, kk, 0)),
                   pl.BlockSpec((None, tko, n), lambda i, j, kk: (j, i, 0)), (s, k, n), out_dtype, (tko, n))
    if dc_cols is not None:
        n = dc_cols
        s = dc.shape[1] // n
        g = _tile(s, max(1, 1024 // n), 1)
        return _mm(name, a, dc, _TN, (k // tko, s // g, m // tmr), a_spec,
                   pl.BlockSpec((tmr, g * n), lambda i, j, kk: (kk, j)),
                   pl.BlockSpec((g, tko, n), lambda i, j, kk: (j, i, 0)), (s, k, n), out_dtype, (g, tko, n),
                   group=g, n=n, b_sel="lanes", o_sel="lead")
    n = dc.shape[1]
    tno = _tile(n, 1024, 128)
    return _mm(name, a, dc, _TN, (k // tko, n // tno, m // tmr), a_spec,
               pl.BlockSpec((tmr, tno), lambda i, j, kk: (kk, j)),
               pl.BlockSpec((tko, tno), lambda i, j, kk: (i, j)), (k, n), out_dtype, (tko, tno))


def rms_fwd(name, x, g):
    t, d = x.shape
    tr = _tile(t, 512)

    def body(x_ref, g_ref, o_ref):
        xf = x_ref[...]
        r = lax.rsqrt(jnp.mean(xf * xf, axis=-1, keepdims=True) + EPS)
        o_ref[...] = (xf * r * g_ref[...]).astype(o_ref.dtype)

    return pl.pallas_call(
        body, name=name, grid=(t // tr,),
        in_specs=[pl.BlockSpec((tr, d), lambda i: (i, 0)), pl.BlockSpec((1, d), lambda i: (0, 0))],
        out_specs=pl.BlockSpec((tr, d), lambda i: (i, 0)),
        out_shape=jax.ShapeDtypeStruct((t, d), BF16), compiler_params=_params(("parallel",)),
    )(x, g.reshape(1, d))


def rms_bwd(name, x, g, dh, dres=None, need_dx=True):
    t, d = x.shape
    tr = _tile(t, 512)

    def body(*refs):
        refs = list(refs)
        x_ref, g_ref, dh_ref = refs[:3]
        r_ref = refs[3] if dres is not None else None
        outs = refs[4:] if dres is not None else refs[3:]
        dx_ref, dg_ref = (outs[0], outs[1]) if need_dx else (None, outs[0])
        i = pl.program_id(0)

        @pl.when(i == 0)
        def _():
            dg_ref[...] = jnp.zeros_like(dg_ref)

        xf = x_ref[...]
        dhf = dh_ref[...].astype(F32)
        r = lax.rsqrt(jnp.mean(xf * xf, axis=-1, keepdims=True) + EPS)
        xh = xf * r
        dg_ref[...] += jnp.sum(dhf * xh, axis=0, keepdims=True)
        if need_dx:
            dxh = dhf * g_ref[...]
            dx = r * (dxh - xh * jnp.mean(dxh * xh, axis=-1, keepdims=True))
            if r_ref is not None:
                dx = dx + r_ref[...]
            dx_ref[...] = dx

    row = pl.BlockSpec((tr, d), lambda i: (i, 0))
    vec = pl.BlockSpec((1, d), lambda i: (0, 0))
    in_specs = [row, vec, row] + ([row] if dres is not None else [])
    args = (x, g.reshape(1, d), dh) + ((dres,) if dres is not None else ())
    out_specs = ([row] if need_dx else []) + [vec]
    out_shape = ([jax.ShapeDtypeStruct((t, d), F32)] if need_dx else []) + [jax.ShapeDtypeStruct((1, d), F32)]
    res = pl.pallas_call(
        body, name=name, grid=(t // tr,), in_specs=in_specs, out_specs=out_specs, out_shape=out_shape,
        compiler_params=_params(("arbitrary",)),
    )(*args)
    return res if need_dx else (None, res[0])


def loss_head(x, g, tgt):
    t, d = x.shape
    tr = _tile(t, 512)

    def body(x_ref, g_ref, t_ref, l_ref, dx_ref, dg_ref):
        i = pl.program_id(0)

        @pl.when(i == 0)
        def _():
            l_ref[...] = jnp.zeros_like(l_ref)
            dg_ref[...] = jnp.zeros_like(dg_ref)

        xf = x_ref[...]
        r = lax.rsqrt(jnp.mean(xf * xf, axis=-1, keepdims=True) + EPS)
        xh = xf * r
        diff = xh * g_ref[...] - t_ref[...]
        l_ref[...] += 0.5 * jnp.sum(jnp.mean(diff * diff, axis=-1, keepdims=True))
        dy = diff * (1.0 / d)
        dg_ref[...] += jnp.sum(dy * xh, axis=0, keepdims=True)
        dxh = dy * g_ref[...]
        dx_ref[...] = r * (dxh - xh * jnp.mean(dxh * xh, axis=-1, keepdims=True))

    row = pl.BlockSpec((tr, d), lambda i: (i, 0))
    vec = pl.BlockSpec((1, d), lambda i: (0, 0))
    return pl.pallas_call(
        body, name="loss_head", grid=(t // tr,), in_specs=[row, vec, row],
        out_specs=[pl.BlockSpec((1, 128), lambda i: (0, 0)), row, vec],
        out_shape=[jax.ShapeDtypeStruct((1, 128), F32), jax.ShapeDtypeStruct((t, d), F32),
                   jax.ShapeDtypeStruct((1, d), F32)],
        compiler_params=_params(("arbitrary",)),
    )(x, g.reshape(1, d), tgt)


def glu_fwd(glu, x, gain):
    t, d = x.shape
    tr = _tile(t, 512)

    def body(v_ref, g_ref, x_ref, n_ref, o_ref, h_ref):
        y = x_ref[...] + v_ref[...] * jax.nn.sigmoid(g_ref[...])
        o_ref[...] = y
        r = lax.rsqrt(jnp.mean(y * y, axis=-1, keepdims=True) + EPS)
        h_ref[...] = (y * r * n_ref[...]).astype(h_ref.dtype)

    row = pl.BlockSpec((tr, d), lambda i: (i, 0))
    return pl.pallas_call(
        body, name="glu_fwd", grid=(t // tr,),
        in_specs=[row, pl.BlockSpec((tr, d), lambda i: (i, 1)), row, pl.BlockSpec((1, d), lambda i: (0, 0))],
        out_specs=[row, row],
        out_shape=[jax.ShapeDtypeStruct((t, d), F32), jax.ShapeDtypeStruct((t, d), BF16)],
        compiler_params=_params(("parallel",)),
    )(glu, glu, x, gain.reshape(1, d))


def glu_bwd(glu, dmix):
    t, d = dmix.shape
    tr = _tile(t, 512)

    def body(v_ref, g_ref, d_ref, o_ref):
        sg = jax.nn.sigmoid(g_ref[...])
        dm = d_ref[...]
        o_ref[:, :d] = (dm * sg).astype(o_ref.dtype)
        o_ref[:, d:] = (dm * v_ref[...] * sg * (1.0 - sg)).astype(o_ref.dtype)

    return pl.pallas_call(
        body, name="glu_bwd", grid=(t // tr,),
        in_specs=[pl.BlockSpec((tr, d), lambda i: (i, 0)), pl.BlockSpec((tr, d), lambda i: (i, 1)),
                  pl.BlockSpec((tr, d), lambda i: (i, 0))],
        out_specs=pl.BlockSpec((tr, 2 * d), lambda i: (i, 0)),
        out_shape=jax.ShapeDtypeStruct((t, 2 * d), BF16), compiler_params=_params(("parallel",)),
    )(glu, glu, dmix)


def _head_masks(shape):
    lane = lax.broadcasted_iota(jnp.int32, shape, 1)
    return lane < SB_HEAD_DIM


def _stack_heads(xf, is_a):
    return jnp.concatenate([jnp.where(is_a, xf, 0.0), jnp.where(is_a, 0.0, xf)], axis=0).astype(MXU_DTYPE)


def _diag_mask(qb, row0, rows):
    row = (lax.broadcasted_iota(jnp.int32, (rows, qb), 0) + row0) & (qb - 1)
    col = lax.broadcasted_iota(jnp.int32, (rows, qb), 1)
    return col < row


def _tri01(qb, pred):
    j = lax.broadcasted_iota(jnp.int32, (qb, qb), 0)
    s = lax.broadcasted_iota(jnp.int32, (qb, qb), 1)
    m = pred(j, s).astype(BF16)
    return jnp.concatenate([m, m], axis=0)


def _split_cat(x):
    hi = x.astype(BF16)
    lo = (x - hi.astype(F32)).astype(BF16)
    return jnp.concatenate([hi, lo], axis=1)


def sb_attn_fwd(proj, order, bsz, seq):
    qb = SB_BLOCK
    nq = seq // qb
    npair = SB_WIDTH // 128
    scale = SB_HEAD_DIM ** -0.5

    def body(q_ref, k_ref, v_ref, order_ref, o_ref, r_ref):
        qi = pl.program_id(2)
        is_a = _head_masks((qb, 128))
        q2 = _stack_heads(q_ref[...] * scale, is_a)
        diag = _diag_mask(qb, 0, 2 * qb)
        upper = _tri01(qb, lambda j, s: j > s)

        def blocks(kbs, acc, run, masked):
            sl = [pl.ds(pl.multiple_of(kb * qb, qb), qb) for kb in kbs]
            zs = [lax.dot_general(q2, k_ref[s, :].astype(MXU_DTYPE), _NT, preferred_element_type=F32) for s in sl]
            lks = [-jnp.maximum(z, 0.0) - jnp.log(1.0 + jnp.exp(-jnp.abs(z))) for z in zs]
            lbs = [lk + z for lk, z in zip(lks, zs)]
            if masked:
                lks = [jnp.where(diag, lk, 0.0) for lk in lks]
            cs = [lax.dot_general(_split_cat(lk), upper, _NN, preferred_element_type=F32) for lk in lks]
            for lk, lb, c, s in zip(lks, lbs, cs, sl):
                w = jnp.exp(lb + (run + c))
                if masked:
                    w = jnp.where(diag, w, 0.0)
                acc = acc + lax.dot_general(w.astype(MXU_DTYPE), v_ref[s, :].astype(MXU_DTYPE), _NN,
                                            preferred_element_type=F32)
                run = run + jnp.sum(lk, axis=1, keepdims=True)
            return acc, run

        carry = blocks([qi], jnp.zeros((2 * qb, 128), F32), jnp.zeros((2 * qb, 1), F32), True)
        carry = lax.cond(qi % 2 == 1, lambda c: blocks([qi - 1], c[0], c[1], False), lambda c: c, carry)
        top = qi - qi % 2
        acc, run = lax.fori_loop(
            0, qi // 2, lambda i, c: blocks([top - 1 - 2 * i, top - 2 - 2 * i], c[0], c[1], False), carry)
        o_ref[...] = jnp.where(is_a, acc[:qb], acc[qb:]).astype(o_ref.dtype)
        r_ref[...] = jnp.where(is_a, run[:qb], run[qb:])

    return pl.pallas_call(
        body, name="sb_attn_fwd", grid=(bsz, npair, nq),
        in_specs=[pl.BlockSpec((qb, 128), lambda b, p, i: (b * nq + i, p)),
                  pl.BlockSpec((seq, 128), lambda b, p, i: (b, npair + p)),
                  pl.BlockSpec((seq, 128), lambda b, p, i: (b, 2 * npair + p)),
                  pl.BlockSpec((1, 128), lambda b, p, i: (0, 0))],
        out_specs=[pl.BlockSpec((qb, 128), lambda b, p, i: (b * nq + i, p)),
                   pl.BlockSpec((qb, 128), lambda b, p, i: (b * nq + i, p))],
        out_shape=[jax.ShapeDtypeStruct((bsz * seq, SB_WIDTH), BF16),
                   jax.ShapeDtypeStruct((bsz * seq, SB_WIDTH), F32)],
        compiler_params=_params(("parallel", "parallel", "arbitrary")),
    )(proj, proj, proj, order)


def sb_attn_bwd(proj, rsum, dcat, bsz, seq):
    qb = SB_BLOCK
    nq = seq // qb
    npair = SB_WIDTH // 128
    scale = SB_HEAD_DIM ** -0.5

    def body(q_ref, k_ref, v_ref, r_ref, do_ref, dq_ref, dk_ref, dv_ref):
        qi = pl.program_id(2)

        @pl.when(qi == 0)
        def _():
            dk_ref[...] = jnp.zeros_like(dk_ref)
            dv_ref[...] = jnp.zeros_like(dv_ref)

        is_a = _head_masks((qb, 128))
        q2 = _stack_heads(q_ref[...] * scale, is_a)
        do2 = _stack_heads(do_ref[...].astype(F32), is_a)
        rf = r_ref[...]
        rtot = jnp.concatenate([rf[:, 0:1], rf[:, SB_HEAD_DIM:SB_HEAD_DIM + 1]], axis=0)
        diag = _diag_mask(qb, 0, 2 * qb)
        incl = _tri01(qb, lambda j, s: j <= s)
        strict = _tri01(qb, lambda j, s: j < s)

        def blocks(kbs, dq, pre, epre, masked):
            sl = [pl.ds(pl.multiple_of(kb * qb, qb), qb) for kb in kbs]
            ks = [k_ref[s, :].astype(MXU_DTYPE) for s in sl]
            vs = [v_ref[s, :].astype(MXU_DTYPE) for s in sl]
            zs = [lax.dot_general(q2, kblk, _NT, preferred_element_type=F32) for kblk in ks]
            dws = [lax.dot_general(do2, vblk, _NT, preferred_element_type=F32) for vblk in vs]
            lks = [-jnp.maximum(z, 0.0) - jnp.log(1.0 + jnp.exp(-jnp.abs(z))) for z in zs]
            lbs = [lk + z for lk, z in zip(lks, zs)]
            if masked:
                lks = [jnp.where(diag, lk, 0.0) for lk in lks]
            ps = [lax.dot_general(_split_cat(lk), incl, _NN, preferred_element_type=F32) for lk in lks]
            ws, es = [], []
            for lk, lb, p, dw in zip(lks, lbs, ps, dws):
                w = jnp.exp(lb + (rtot - (pre + p)))
                if masked:
                    w = jnp.where(diag, w, 0.0)
                ws.append(w)
                es.append(dw * w)
                pre = pre + jnp.sum(lk, axis=1, keepdims=True)
            cs = [lax.dot_general(_split_cat(e), strict, _NN, preferred_element_type=F32) for e in es]
            for e, lb, c, w, kblk, s in zip(es, lbs, cs, ws, ks, sl):
                dz = e - jnp.exp(lb) * (e + (epre + c))
                if masked:
                    dz = jnp.where(diag, dz, 0.0)
                dz = dz.astype(MXU_DTYPE)
                dq = dq + lax.dot_general(dz, kblk, _NN, preferred_element_type=F32)
                dk_ref[s, :] += lax.dot_general(dz, q2, _TN, preferred_element_type=F32)
                dv_ref[s, :] += lax.dot_general(w.astype(MXU_DTYPE), do2, _TN, preferred_element_type=F32)
                epre = epre + jnp.sum(e, axis=1, keepdims=True)
            return dq, pre, epre

        zc = jnp.zeros((2 * qb, 1), F32)
        carry = lax.fori_loop(0, qi // 2, lambda i, c: blocks([2 * i, 2 * i + 1], c[0], c[1], c[2], False),
                              (jnp.zeros((2 * qb, 128), F32), zc, zc))
        carry = lax.cond(qi % 2 == 1, lambda c: blocks([qi - 1], c[0], c[1], c[2], False), lambda c: c, carry)
        dq = blocks([qi], carry[0], carry[1], carry[2], True)[0]
        dq_ref[...] = jnp.where(is_a, dq[:qb], dq[qb:]) * scale

    full = jax.ShapeDtypeStruct((bsz * seq, SB_WIDTH), F32)
    qspec = pl.BlockSpec((qb, 128), lambda b, p, i: (b * nq + i, p))
    return pl.pallas_call(
        body, name="sb_attn_bwd", grid=(bsz, npair, nq),
        in_specs=[qspec,
                  pl.BlockSpec((seq, 128), lambda b, p, i: (b, npair + p)),
                  pl.BlockSpec((seq, 128), lambda b, p, i: (b, 2 * npair + p)),
                  qspec, qspec],
        out_specs=[qspec, pl.BlockSpec((seq, 128), lambda b, p, i: (b, p)),
                   pl.BlockSpec((seq, 128), lambda b, p, i: (b, p))],
        out_shape=[full, full, full],
        compiler_params=_params(("parallel", "parallel", "arbitrary")),
    )(proj, proj, proj, rsum, dcat)


def _window_sums(x, forward):
    n = x.shape[0]
    out = []
    s = x
    for sh in (1, 2, 4, 8):
        s = s + pltpu.roll(s, (n - sh) if forward else sh, 0)
        out.append(s)
    return out


def _pool_counts(tc, c, w):
    t = lax.broadcasted_iota(jnp.int32, (tc, 1), 0) + c * tc
    return jnp.minimum(t + 1, w).astype(F32)


def pool_fwd(proj, pool_w, pool_scale, bsz, seq):
    tc = _tile(seq, 512)
    nc = seq // tc
    hb = tc // POOL_HALO
    ucol = 3

    def body(u_ref, prev_ref, w_ref, s_ref, o_ref):
        c = pl.program_id(1)
        prev = jnp.where(c > 0, prev_ref[...], 0.0)
        x = jnp.concatenate([prev, u_ref[...]], axis=0)
        sums = _window_sums(x, forward=False)
        for g, win in enumerate(POOL_WINDOWS):
            ls = slice(g * POOL_GROUP, (g + 1) * POOL_GROUP)
            pooled = sums[g][POOL_HALO:, ls] / _pool_counts(tc, c, win) - x[POOL_HALO:, ls]
            y = _dot(pooled, w_ref[g], _NN)
            o_ref[:, ls] = (y * s_ref[:, ls]).astype(o_ref.dtype)

    return pl.pallas_call(
        body, name="pool_fwd", grid=(bsz, nc),
        in_specs=[pl.BlockSpec((tc, SB_WIDTH), lambda b, c: (b * nc + c, ucol)),
                  pl.BlockSpec((POOL_HALO, SB_WIDTH), lambda b, c: (jnp.maximum((b * nc + c) * hb - 1, 0), ucol)),
                  pl.BlockSpec((4, POOL_GROUP, POOL_GROUP), lambda b, c: (0, 0, 0)),
                  pl.BlockSpec((1, SB_WIDTH), lambda b, c: (0, 0))],
        out_specs=pl.BlockSpec((tc, SB_WIDTH), lambda b, c: (b * nc + c, 0)),
        out_shape=jax.ShapeDtypeStruct((bsz * seq, SB_WIDTH), BF16),
        compiler_params=_params(("parallel", "parallel")),
    )(proj, proj, pool_w, pool_scale)


def pool_bwd(proj, pool_w, pool_scale, dcat, bsz, seq):
    tc = _tile(seq, 512)
    nc = seq // tc
    hb = tc // POOL_HALO
    nblk = bsz * seq // POOL_HALO
    ucol = 3

    def body(u_ref, prev_ref, dy_ref, nxt_ref, w_ref, s_ref, du_ref, dw_ref, ds_ref):
        b, c = pl.program_id(0), pl.program_id(1)

        @pl.when((b == 0) & (c == 0))
        def _():
            dw_ref[...] = jnp.zeros_like(dw_ref)
            ds_ref[...] = jnp.zeros_like(ds_ref)

        prev = jnp.where(c > 0, prev_ref[...], 0.0)
        x = jnp.concatenate([prev, u_ref[...]], axis=0)
        sums = _window_sums(x, forward=False)
        nxt = jnp.where(c < nc - 1, nxt_ref[...].astype(F32), 0.0)
        dy = jnp.concatenate([dy_ref[...].astype(F32), nxt], axis=0)
        tq = lax.broadcasted_iota(jnp.int32, (tc + POOL_HALO, 1), 0) + c * tc
        for g, win in enumerate(POOL_WINDOWS):
            ls = slice(g * POOL_GROUP, (g + 1) * POOL_GROUP)
            pooled = sums[g][POOL_HALO:, ls] / _pool_counts(tc, c, win) - x[POOL_HALO:, ls]
            y = _dot(pooled, w_ref[g], _NN)
            ds_ref[:, ls] += jnp.sum(dy[:tc, ls] * y, axis=0, keepdims=True)
            dz = dy[:, ls] * s_ref[:, ls]
            dw_ref[g] += _dot(pooled, dz[:tc], _TN)
            dpool = _dot(dz, w_ref[g], _NT)
            dmean = dpool / jnp.minimum(tq + 1, win).astype(F32)
            fsum = _window_sums(dmean, forward=True)[g]
            du_ref[:, ls] = fsum[:tc] - dpool[:tc]

    return pl.pallas_call(
        body, name="pool_bwd", grid=(bsz, nc),
        in_specs=[pl.BlockSpec((tc, SB_WIDTH), lambda b, c: (b * nc + c, ucol)),
                  pl.BlockSpec((POOL_HALO, SB_WIDTH), lambda b, c: (jnp.maximum((b * nc + c) * hb - 1, 0), ucol)),
                  pl.BlockSpec((tc, SB_WIDTH), lambda b, c: (b * nc + c, 1)),
                  pl.BlockSpec((POOL_HALO, SB_WIDTH), lambda b, c: (jnp.minimum((b * nc + c + 1) * hb, nblk - 1), 1)),
                  pl.BlockSpec((4, POOL_GROUP, POOL_GROUP), lambda b, c: (0, 0, 0)),
                  pl.BlockSpec((1, SB_WIDTH), lambda b, c: (0, 0))],
        out_specs=[pl.BlockSpec((tc, SB_WIDTH), lambda b, c: (b * nc + c, 0)),
                   pl.BlockSpec((4, POOL_GROUP, POOL_GROUP), lambda b, c: (0, 0, 0)),
                   pl.BlockSpec((1, SB_WIDTH), lambda b, c: (0, 0))],
        out_shape=[jax.ShapeDtypeStruct((bsz * seq, SB_WIDTH), F32),
                   jax.ShapeDtypeStruct((4, POOL_GROUP, POOL_GROUP), F32),
                   jax.ShapeDtypeStruct((1, SB_WIDTH), F32)],
        compiler_params=_params(("arbitrary", "arbitrary")),
    )(proj, proj, dcat, dcat, pool_w, pool_scale)


def _lbar(lam_re, lam_im, log_dt):
    dt = jnp.exp(log_dt)
    mag = jnp.exp(lam_re * dt)
    ang = lam_im * dt
    return mag * jnp.cos(ang), mag * jnp.sin(ang)


def _bbar(lam_re, lam_im, log_dt, b_re, b_im):
    lb_re, lb_im = _lbar(lam_re, lam_im, log_dt)
    n_re = lb_re - 1.0
    den = lam_re * lam_re + lam_im * lam_im
    coef_re = (n_re * lam_re + lb_im * lam_im) / den
    coef_im = (lb_im * lam_re - n_re * lam_im) / den
    return coef_re * b_re - coef_im * b_im, coef_re * b_im + coef_im * b_re


def _expand01():
    p = lax.broadcasted_iota(jnp.int32, (64, 1024), 0)
    q = lax.broadcasted_iota(jnp.int32, (64, 1024), 1)
    return (lax.shift_right_logical(q, 4) == p).astype(BF16)


def ssm_prep(lam_re, lam_im, log_dt, b_re2, b_im2):
    def body(lr_ref, li_ref, dt_ref, br_ref, bi_ref, ar_ref, ai_ref, bbr_ref, bbi_ref):
        e = _expand01()
        lr, li, dt = lr_ref[...], li_ref[...], dt_ref[...]
        ar_ref[...], ai_ref[...] = _lbar(lr, li, dt)
        bbr_ref[...], bbi_ref[...] = _bbar(_dot_exact01(lr, e), _dot_exact01(li, e), dt, br_ref[...], bi_ref[...])

    s64 = jax.ShapeDtypeStruct((64, 64), F32)
    s1k = jax.ShapeDtypeStruct((64, 1024), F32)
    return pl.pallas_call(body, name="ssm_prep", out_shape=[s64, s64, s1k, s1k], compiler_params=_params())(
        lam_re, lam_im, log_dt, b_re2, b_im2)


def ssm_prep_bwd(lam_re, lam_im, log_dt, b_re2, b_im2, da_re, da_im, dbb_re, dbb_im):
    def body(lr_ref, li_ref, dt_ref, br_ref, bi_ref, dar_ref, dai_ref, dbr_ref, dbi_ref,
             olr_ref, oli_ref, odt_ref, obr_ref, obi_ref):
        e = _expand01()
        lr, li, dt = lr_ref[...], li_ref[...], dt_ref[...]
        _, vjp_a = jax.vjp(_lbar, lr, li, dt)
        g_lr, g_li, g_dt = vjp_a((dar_ref[...], dai_ref[...]))
        _, vjp_b = jax.vjp(_bbar, _dot_exact01(lr, e), _dot_exact01(li, e), dt, br_ref[...], bi_ref[...])
        x_lr, x_li, x_dt, g_br, g_bi = vjp_b((dbr_ref[...], dbi_ref[...]))
        olr_ref[...] = g_lr + _dot_exact01(x_lr, e, _NT)
        oli_ref[...] = g_li + _dot_exact01(x_li, e, _NT)
        odt_ref[...] = g_dt + x_dt
        obr_ref[...] = g_br
        obi_ref[...] = g_bi

    s64 = jax.ShapeDtypeStruct((64, 64), F32)
    s1k = jax.ShapeDtypeStruct((64, 1024), F32)
    return pl.pallas_call(body, name="ssm_prep_bwd",
                          out_shape=[s64, s64, jax.ShapeDtypeStruct((64, 1), F32), s1k, s1k],
                          compiler_params=_params())(
        lam_re, lam_im, log_dt, b_re2, b_im2, da_re, da_im, dbb_re, dbb_im)


def _gelu(y):
    c = math.sqrt(2.0 / math.pi)
    return 0.5 * y * (1.0 + jnp.tanh(c * (y + 0.044715 * y * y * y)))


def _gelu_grad(y):
    c = math.sqrt(2.0 / math.pi)
    th = jnp.tanh(c * (y + 0.044715 * y * y * y))
    return 0.5 * (1.0 + th) + 0.5 * y * (1.0 - th * th) * c * (1.0 + 3.0 * 0.044715 * y * y)


def _cmul(ar, ai, br, bi):
    return ar * br - ai * bi, ar * bi + ai * br


def _scan_tables(ar, ai, reverse, tabs):
    row = lax.broadcasted_iota(jnp.int32, (8, SSM_STATES), 0)
    a1 = (ar, ai)
    a2 = _cmul(*a1, *a1)
    a4 = _cmul(*a2, *a2)
    powers = [a1, a2, _cmul(*a2, *a1), a4]
    powers += [_cmul(*a4, *p) for p in powers]
    for k, (val, sh) in enumerate(((a1, 1), (a2, 2), (a4, 4))):
        keep = (row < 8 - sh) if reverse else (row >= sh)
        tabs[2 * k][...] = jnp.where(keep, val[0], 0.0)
        tabs[2 * k + 1][...] = jnp.where(keep, val[1], 0.0)
    pr = jnp.zeros((8, SSM_STATES), F32)
    pi = jnp.zeros((8, SSM_STATES), F32)
    for r in range(8):
        val = powers[7 - r] if reverse else powers[r]
        pr = jnp.where(row == r, val[0], pr)
        pi = jnp.where(row == r, val[1], pi)
    tabs[6][...] = pr
    tabs[7][...] = pi


def _scan8(xr, xi, tabs, ls, cr, ci, reverse):
    for k, sh in enumerate((1, 2, 4)):
        amt = (8 - sh) if reverse else sh
        sr, si = pltpu.roll(xr, amt, 0), pltpu.roll(xi, amt, 0)
        lr, li = tabs[2 * k][:, ls], tabs[2 * k + 1][:, ls]
        xr, xi = xr + lr * sr - li * si, xi + lr * si + li * sr
    pr, pi = tabs[6][:, ls], tabs[7][:, ls]
    return xr + pr * cr - pi * ci, xi + pr * ci + pi * cr


def _block8(b):
    return pl.ds(pl.multiple_of(b * 8, 8), 8)


def ssm_fwd(u, wt, ct, a_re, a_im, dskip, bsz, seq):
    tc = _tile(seq, 256)
    nc = seq // tc
    ns = SSM_TILE_STATES
    nl = SSM_STATES // SSM_LANES

    def body(u_ref, wt_ref, ct_ref, ar_ref, ai_ref, d_ref, y_ref, gl_ref, hr_ref, hi_ref, sr_ref, si_ref, *tabs):
        b, c = pl.program_id(0), pl.program_id(1)

        @pl.when((b == 0) & (c == 0))
        def _():
            _scan_tables(ar_ref[...], ai_ref[...], False, tabs)

        @pl.when(c == 0)
        def _():
            sr_ref[...] = jnp.zeros_like(sr_ref)
            si_ref[...] = jnp.zeros_like(si_ref)

        uf = u_ref[...]
        for i in range(SSM_TILES):
            bu = _dot(uf[:, i * 128:(i + 1) * 128], wt_ref[i], _NN)
            hr_ref[:, i * ns:(i + 1) * ns] = bu[:, :ns]
            hi_ref[:, i * ns:(i + 1) * ns] = bu[:, ns:]

        def step(blk, carry):
            rows = _block8(blk)
            new = []
            for j in range(nl):
                ls = slice(j * SSM_LANES, (j + 1) * SSM_LANES)
                xr, xi = _scan8(hr_ref[rows, ls], hi_ref[rows, ls], tabs, ls, carry[2 * j], carry[2 * j + 1], False)
                hr_ref[rows, ls] = xr
                hi_ref[rows, ls] = xi
                new += [xr[7:8], xi[7:8]]
            return tuple(new)

        init = []
        for j in range(nl):
            ls = slice(j * SSM_LANES, (j + 1) * SSM_LANES)
            init += [sr_ref[:, ls], si_ref[:, ls]]
        last = lax.fori_loop(0, tc // 8, step, tuple(init), unroll=2)
        for j in range(nl):
            ls = slice(j * SSM_LANES, (j + 1) * SSM_LANES)
            sr_ref[:, ls] = last[2 * j]
            si_ref[:, ls] = last[2 * j + 1]
        for i in range(SSM_TILES):
            hcat = jnp.concatenate([hr_ref[:, i * ns:(i + 1) * ns], hi_ref[:, i * ns:(i + 1) * ns]], axis=1)
            ls = slice(i * 128, (i + 1) * 128)
            y = _dot(hcat, ct_ref[i], _NN) + d_ref[:, ls] * uf[:, ls]
            y_ref[:, ls] = y
            gl_ref[:, ls] = _gelu(y).astype(gl_ref.dtype)

    t = bsz * seq
    row = pl.BlockSpec((tc, D_MODEL), lambda b, c: (b * nc + c, 0))
    st = pl.BlockSpec((tc, SSM_STATES), lambda b, c: (b * nc + c, 0))
    diag = pl.BlockSpec((1, SSM_STATES), lambda b, c: (0, 0))
    return pl.pallas_call(
        body, name="ssm_fwd", grid=(bsz, nc),
        in_specs=[row, pl.BlockSpec((SSM_TILES, 128, 2 * ns), lambda b, c: (0, 0, 0)),
                  pl.BlockSpec((SSM_TILES, 2 * ns, 128), lambda b, c: (0, 0, 0)), diag, diag,
                  pl.BlockSpec((1, D_MODEL), lambda b, c: (0, 0))],
        out_specs=[row, row, st, st],
        out_shape=[jax.ShapeDtypeStruct((t, D_MODEL), F32), jax.ShapeDtypeStruct((t, D_MODEL), BF16),
                   jax.ShapeDtypeStruct((t, SSM_STATES), F32), jax.ShapeDtypeStruct((t, SSM_STATES), F32)],
        scratch_shapes=[pltpu.VMEM((1, SSM_STATES), F32)] * 2 + [pltpu.VMEM((8, SSM_STATES), F32)] * 8,
        compiler_params=_params(("arbitrary", "arbitrary")),
    )(u, wt, ct, a_re, a_im, dskip)


def ssm_bwd(dgl, y, u, h_re, h_im, wt, ct, a_re, a_im, dskip, bsz, seq):
    tc = _tile(seq, 256)
    nc = seq // tc
    nb = tc // 8
    ns = SSM_TILE_STATES
    nl = SSM_STATES // SSM_LANES

    def body(dgl_ref, y_ref, u_ref, hr_ref, hi_ref, pr_ref, pi_ref, wt_ref, ct_ref, ar_ref, ai_ref, d_ref,
             du_ref, dwt_ref, dct_ref, dd_ref, dar_ref, dai_ref, gr_ref, gi_ref, sr_ref, si_ref, ar8_ref, ai8_ref,
             *tabs):
        b, c = pl.program_id(0), pl.program_id(1)

        @pl.when((b == 0) & (c == 0))
        def _():
            for r in (dwt_ref, dct_ref, dd_ref, ar8_ref, ai8_ref):
                r[...] = jnp.zeros_like(r)
            _scan_tables(ar_ref[...], -ai_ref[...], True, tabs)

        @pl.when(c == 0)
        def _():
            sr_ref[...] = jnp.zeros_like(sr_ref)
            si_ref[...] = jnp.zeros_like(si_ref)

        uf = u_ref[...]
        dy = dgl_ref[...].astype(F32) * _gelu_grad(y_ref[...])
        dd_ref[...] += jnp.sum(dy * uf, axis=0, keepdims=True)
        for i in range(SSM_TILES):
            dyi = dy[:, i * 128:(i + 1) * 128]
            dh = _dot(dyi, ct_ref[i], _NT)
            gr_ref[:, i * ns:(i + 1) * ns] = dh[:, :ns]
            gi_ref[:, i * ns:(i + 1) * ns] = dh[:, ns:]
            hcat = jnp.concatenate([hr_ref[:, i * ns:(i + 1) * ns], hi_ref[:, i * ns:(i + 1) * ns]], axis=1)
            dct_ref[i] += _dot(hcat, dyi, _TN)
        row0 = lax.broadcasted_iota(jnp.int32, (8, SSM_LANES), 0) == 0

        def block(blk, carry, before):
            rows = _block8(blk)
            new = []
            for j in range(nl):
                ls = slice(j * SSM_LANES, (j + 1) * SSM_LANES)
                gr, gi = _scan8(gr_ref[rows, ls], gi_ref[rows, ls], tabs, ls, carry[2 * j], carry[2 * j + 1], True)
                gr_ref[rows, ls] = gr
                gi_ref[rows, ls] = gi
                bpr, bpi = before(j)
                hpr = jnp.where(row0, bpr, pltpu.roll(hr_ref[rows, ls], 1, 0))
                hpi = jnp.where(row0, bpi, pltpu.roll(hi_ref[rows, ls], 1, 0))
                ar8_ref[:, ls] += gr * hpr + gi * hpi
                ai8_ref[:, ls] += gi * hpr - gr * hpi
                new += [gr[0:1], gi[0:1]]
            return tuple(new)

        def step(jj, carry):
            blk = nb - 1 - jj
            prev_rows = _block8(blk - 1)

            def before(j):
                ls = slice(j * SSM_LANES, (j + 1) * SSM_LANES)
                return hr_ref[prev_rows, ls][7:8], hi_ref[prev_rows, ls][7:8]

            return block(blk, carry, before)

        init = []
        for j in range(nl):
            ls = slice(j * SSM_LANES, (j + 1) * SSM_LANES)
            init += [sr_ref[:, ls], si_ref[:, ls]]
        carry = lax.fori_loop(0, nb - 1, step, tuple(init))
        first = c == nc - 1

        def before_chunk(j):
            ls = slice(j * SSM_LANES, (j + 1) * SSM_LANES)
            return (jnp.where(first, 0.0, pr_ref[:, ls][7:8]), jnp.where(first, 0.0, pi_ref[:, ls][7:8]))

        last = block(0, carry, before_chunk)
        for j in range(nl):
            ls = slice(j * SSM_LANES, (j + 1) * SSM_LANES)
            sr_ref[:, ls] = last[2 * j]
            si_ref[:, ls] = last[2 * j + 1]
        for i in range(SSM_TILES):
            ls = slice(i * 128, (i + 1) * 128)
            gcat = jnp.concatenate([gr_ref[:, i * ns:(i + 1) * ns], gi_ref[:, i * ns:(i + 1) * ns]], axis=1)
            du_ref[:, ls] = (_dot(gcat, wt_ref[i], _NT) + d_ref[:, ls] * dy[:, ls]).astype(du_ref.dtype)
            dwt_ref[i] += _dot(uf[:, ls], gcat, _TN)

        @pl.when((b == bsz - 1) & (c == nc - 1))
        def _():
            dar_ref[...] = jnp.sum(ar8_ref[...], axis=0, keepdims=True)
            dai_ref[...] = jnp.sum(ai8_ref[...], axis=0, keepdims=True)

    t = bsz * seq
    rev = lambda b, c: (b * nc + (nc - 1 - c), 0)
    row = pl.BlockSpec((tc, D_MODEL), rev)
    st = pl.BlockSpec((tc, SSM_STATES), rev)
    prev = pl.BlockSpec((8, SSM_STATES), lambda b, c: (jnp.maximum((b * nc + (nc - 1 - c)) * nb - 1, 0), 0))
    diag = pl.BlockSpec((1, SSM_STATES), lambda b, c: (0, 0))
    wts = pl.BlockSpec((SSM_TILES, 128, 2 * ns), lambda b, c: (0, 0, 0))
    cts = pl.BlockSpec((SSM_TILES, 2 * ns, 128), lambda b, c: (0, 0, 0))
    vec = pl.BlockSpec((1, D_MODEL), lambda b, c: (0, 0))
    return pl.pallas_call(
        body, name="ssm_bwd", grid=(bsz, nc),
        in_specs=[row, row, row, st, st, prev, prev, wts, cts, diag, diag, vec],
        out_specs=[row, wts, cts, vec, diag, diag],
        out_shape=[jax.ShapeDtypeStruct((t, D_MODEL), BF16),
                   jax.ShapeDtypeStruct((SSM_TILES, 128, 2 * ns), F32),
                   jax.ShapeDtypeStruct((SSM_TILES, 2 * ns, 128), F32),
                   jax.ShapeDtypeStruct((1, D_MODEL), F32),
                   jax.ShapeDtypeStruct((1, SSM_STATES), F32), jax.ShapeDtypeStruct((1, SSM_STATES), F32)],
        scratch_shapes=[pltpu.VMEM((tc, SSM_STATES), F32)] * 2 + [pltpu.VMEM((1, SSM_STATES), F32)] * 2
                       + [pltpu.VMEM((8, SSM_STATES), F32)] * 10,
        compiler_params=_params(("arbitrary", "arbitrary")),
    )(dgl, y, u, h_re, h_im, h_re, h_im, wt, ct, a_re, a_im, dskip)


def _ssm_in_weights(bb_re2, bb_im2):
    eye = jnp.eye(8, dtype=F32)[None, :, None, :, None]

    def one(bb):
        t = bb.reshape(8, 8, 64, 16).transpose(0, 1, 3, 2)
        return (t[:, :, :, None, :] * eye).reshape(8, 128, 512)

    return jnp.concatenate([one(bb_re2), one(bb_im2)], axis=-1).astype(MXU_DTYPE)


def _ssm_in_weights_bwd(dwt):
    eye = jnp.eye(8, dtype=F32)[None, :, None, :, None]

    def one(d):
        t = (d.reshape(8, 8, 16, 8, 64) * eye).sum(axis=3)
        return t.transpose(0, 1, 3, 2).reshape(64, 1024)

    return one(dwt[..., :512]), one(dwt[..., 512:])


def _ssm_out_weights(c_re, c_im):
    eye = jnp.eye(8, dtype=F32)[None, :, None, :, None]

    def one(cc):
        t = cc.reshape(8, 8, 16, 64).transpose(0, 1, 3, 2)
        return (t[:, :, :, None, :] * eye).reshape(8, 512, 128)

    return jnp.concatenate([one(c_re), -one(c_im)], axis=1).astype(MXU_DTYPE)


def _ssm_out_weights_bwd(dct):
    eye = jnp.eye(8, dtype=F32)[None, :, None, :, None]

    def one(d):
        t = (d.reshape(8, 8, 64, 8, 16) * eye).sum(axis=3)
        return t.transpose(0, 1, 3, 2).reshape(64, 16, 64)

    return one(dct[:, :512]), -one(dct[:, 512:])


def _softmax(s):
    m = jnp.max(s, axis=-1, keepdims=True)
    e = jnp.exp(s - m)
    return e / jnp.sum(e, axis=-1, keepdims=True)


def xattn_fwd(q, kv, bsz, seq):
    tq = _tile(seq, 512)
    nq = seq // tq
    scale = XA_HEAD_DIM ** -0.5

    def body(q_ref, k_ref, v_ref, o_ref):
        s = lax.dot_general(q_ref[...], k_ref[...], _NT, preferred_element_type=F32) * scale
        p = _softmax(s)
        o_ref[...] = _dot(p, v_ref[...], _NN).astype(o_ref.dtype)

    qs = pl.BlockSpec((tq, XA_HEAD_DIM), lambda b, h, i: (b * nq + i, h))
    return pl.pallas_call(
        body, name="xattn_fwd", grid=(bsz, XA_HEADS, nq),
        in_specs=[qs, pl.BlockSpec((MEM_LEN, XA_HEAD_DIM), lambda b, h, i: (b, h)),
                  pl.BlockSpec((MEM_LEN, XA_HEAD_DIM), lambda b, h, i: (b, XA_HEADS + h))],
        out_specs=qs, out_shape=jax.ShapeDtypeStruct((bsz * seq, D_MODEL), BF16),
        compiler_params=_params(("parallel", "parallel", "parallel")),
    )(q, kv, kv)


def xattn_bwd(q, kv, do, bsz, seq):
    tq = _tile(seq, 512)
    nq = seq // tq
    scale = XA_HEAD_DIM ** -0.5

    def body(q_ref, k_ref, v_ref, do_ref, dq_ref, dk_ref, dv_ref):
        @pl.when(pl.program_id(2) == 0)
        def _():
            dk_ref[...] = jnp.zeros_like(dk_ref)
            dv_ref[...] = jnp.zeros_like(dv_ref)

        qv, kk, vv, dov = q_ref[...], k_ref[...], v_ref[...], do_ref[...]
        s = lax.dot_general(qv, kk, _NT, preferred_element_type=F32) * scale
        p = _softmax(s)
        dp = lax.dot_general(dov, vv, _NT, preferred_element_type=F32)
        ds = (p * (dp - jnp.sum(dp * p, axis=-1, keepdims=True)) * scale).astype(MXU_DTYPE)
        dq_ref[...] = lax.dot_general(ds, kk, _NN, preferred_element_type=F32).astype(dq_ref.dtype)
        dk_ref[...] += lax.dot_general(ds, qv, _TN, preferred_element_type=F32)
        dv_ref[...] += lax.dot_general(p.astype(MXU_DTYPE), dov, _TN, preferred_element_type=F32)

    qs = pl.BlockSpec((tq, XA_HEAD_DIM), lambda b, h, i: (b * nq + i, h))
    ks = pl.BlockSpec((MEM_LEN, XA_HEAD_DIM), lambda b, h, i: (b, h))
    vs = pl.BlockSpec((MEM_LEN, XA_HEAD_DIM), lambda b, h, i: (b, XA_HEADS + h))
    dkv = jax.ShapeDtypeStruct((bsz * MEM_LEN, D_MODEL), F32)
    dq, dk, dv = pl.pallas_call(
        body, name="xattn_bwd", grid=(bsz, XA_HEADS, nq),
        in_specs=[qs, ks, vs, qs], out_specs=[qs, ks, ks],
        out_shape=[jax.ShapeDtypeStruct((bsz * seq, D_MODEL), BF16), dkv, dkv],
        compiler_params=_params(("parallel", "parallel", "arbitrary")),
    )(q, kv, kv, do)
    return dq, dk, dv


CONV_HALO = 16


def _shifts_down(x, prev):
    h = prev.shape[0]
    ext = jnp.concatenate([prev, x], axis=0)
    return pltpu.roll(ext, 1, 0)[h:], pltpu.roll(ext, 2, 0)[h:]


def _shifts_up(x, nxt):
    rows = x.shape[0]
    n = rows + nxt.shape[0]
    ext = jnp.concatenate([x, nxt], axis=0)
    return pltpu.roll(ext, n - 1, 0)[:rows], pltpu.roll(ext, n - 2, 0)[:rows]


def _conv_taps(u, u1, u2, w, b):
    return b + w[2:3] * u + w[1:2] * u1 + w[0:1] * u2


def conv_fwd(up, cw, cb, bsz, seq):
    tc = _tile(seq, 512)
    nc = seq // tc
    hb = tc // CONV_HALO
    half = N_DEV // 2

    def body(uv_ref, ug_ref, pv_ref, pg_ref, wv_ref, wg_ref, bv_ref, bg_ref, o_ref):
        c = pl.program_id(2)
        pv = jnp.where(c > 0, pv_ref[...].astype(F32), 0.0)
        pg = jnp.where(c > 0, pg_ref[...].astype(F32), 0.0)
        uv, ug = uv_ref[...].astype(F32), ug_ref[...].astype(F32)
        val = _conv_taps(uv, *_shifts_down(uv, pv), wv_ref[...], bv_ref[...])
        gate = _conv_taps(ug, *_shifts_down(ug, pg), wg_ref[...], bg_ref[...])
        o_ref[...] = (gate * jax.nn.sigmoid(gate) * val).astype(o_ref.dtype)

    def cur(off):
        return pl.BlockSpec((None, tc, FF_SHARD), lambda b, j, c: (j + off, b * nc + c, 0))

    def prv(off):
        return pl.BlockSpec((None, CONV_HALO, FF_SHARD), lambda b, j, c: (j + off, jnp.maximum((b * nc + c) * hb - 1, 0), 0))

    def par(rows, off):
        return pl.BlockSpec((None, rows, FF_SHARD), lambda b, j, c: (j + off, 0, 0))

    return pl.pallas_call(
        body, name="conv_fwd", grid=(bsz, half, nc),
        in_specs=[cur(0), cur(half), prv(0), prv(half), par(3, 0), par(3, half), par(1, 0), par(1, half)],
        out_specs=cur(0), out_shape=jax.ShapeDtypeStruct((half, bsz * seq, FF_SHARD), BF16),
        compiler_params=_params(("parallel", "parallel", "parallel")),
    )(up, up, up, up, cw, cw, cb, cb)


def conv_bwd_taps(up, cw, cb, dact, bsz, seq):
    tc = _tile(seq, 512)
    nc = seq // tc
    hb = tc // CONV_HALO
    half = N_DEV // 2

    def body(uv_ref, ug_ref, pv_ref, pg_ref, wv_ref, wg_ref, bv_ref, bg_ref, da_ref,
             dc_ref, dwv_ref, dwg_ref, dbv_ref, dbg_ref):
        b, c = pl.program_id(1), pl.program_id(2)

        @pl.when((b == 0) & (c == 0))
        def _():
            for r in (dwv_ref, dwg_ref, dbv_ref, dbg_ref):
                r[...] = jnp.zeros_like(r)

        pv = jnp.where(c > 0, pv_ref[...].astype(F32), 0.0)
        pg = jnp.where(c > 0, pg_ref[...].astype(F32), 0.0)
        uv, ug = uv_ref[...].astype(F32), ug_ref[...].astype(F32)
        uv1, uv2 = _shifts_down(uv, pv)
        ug1, ug2 = _shifts_down(ug, pg)
        val = _conv_taps(uv, uv1, uv2, wv_ref[...], bv_ref[...])
        gate = _conv_taps(ug, ug1, ug2, wg_ref[...], bg_ref[...])
        sg = jax.nn.sigmoid(gate)
        da = da_ref[...].astype(F32)
        dsilu = da * sg
        dval = dsilu * gate
        dgate = dsilu * val * (1.0 + gate * (1.0 - sg))
        dc_ref[0] = dval.astype(dc_ref.dtype)
        dc_ref[1] = dgate.astype(dc_ref.dtype)
        for dcv, taps, dw_ref, db_ref in ((dval, (uv2, uv1, uv), dwv_ref, dbv_ref),
                                          (dgate, (ug2, ug1, ug), dwg_ref, dbg_ref)):
            db_ref[...] += jnp.sum(dcv, axis=0, keepdims=True)
            for k, u_k in enumerate(taps):
                dw_ref[k:k + 1, :] += jnp.sum(dcv * u_k, axis=0, keepdims=True)

    def cur(off):
        return pl.BlockSpec((None, tc, FF_SHARD), lambda j, b, c: (j + off, b * nc + c, 0))

    def prv(off):
        return pl.BlockSpec((None, CONV_HALO, FF_SHARD), lambda j, b, c: (j + off, jnp.maximum((b * nc + c) * hb - 1, 0), 0))

    def par(rows, off):
        return pl.BlockSpec((None, rows, FF_SHARD), lambda j, b, c: (j + off, 0, 0))

    t = bsz * seq
    hs = jax.ShapeDtypeStruct((2, half, t, FF_SHARD), BF16)
    ws = jax.ShapeDtypeStruct((half, 3, FF_SHARD), F32)
    bs = jax.ShapeDtypeStruct((half, 1, FF_SHARD), F32)
    dc, dwv, dwg, dbv, dbg = pl.pallas_call(
        body, name="conv_bwd_taps", grid=(half, bsz, nc),
        in_specs=[cur(0), cur(half), prv(0), prv(half), par(3, 0), par(3, half), par(1, 0), par(1, half), cur(0)],
        out_specs=[pl.BlockSpec((2, None, tc, FF_SHARD), lambda j, b, c: (0, j, b * nc + c, 0)),
                   par(3, 0), par(3, 0), par(1, 0), par(1, 0)],
        out_shape=[hs, ws, ws, bs, bs],
        compiler_params=_params(("parallel", "arbitrary", "arbitrary")),
    )(up, up, up, up, cw, cw, cb, cb, dact)
    return (dc.reshape(N_DEV, t, FF_SHARD), jnp.concatenate([dwv, dwg], axis=0),
            jnp.concatenate([dbv, dbg], axis=0))


def conv_bwd_input(dconv, cw, bsz, seq):
    tc = _tile(seq, 1024)
    nc = seq // tc
    hb = tc // CONV_HALO
    nblk = bsz * seq // CONV_HALO

    def body(d_ref, n_ref, w_ref, o_ref):
        c = pl.program_id(2)
        nxt = jnp.where(c < nc - 1, n_ref[...].astype(F32), 0.0)
        d = d_ref[...].astype(F32)
        d1, d2 = _shifts_up(d, nxt)
        w = w_ref[...]
        o_ref[...] = (w[2:3] * d + w[1:2] * d1 + w[0:1] * d2).astype(o_ref.dtype)

    cur = pl.BlockSpec((None, tc, FF_SHARD), lambda j, b, c: (j, b * nc + c, 0))
    return pl.pallas_call(
        body, name="conv_bwd_input", grid=(N_DEV, bsz, nc),
        in_specs=[cur, pl.BlockSpec((None, CONV_HALO, FF_SHARD),
                                    lambda j, b, c: (j, jnp.minimum((b * nc + c + 1) * hb, nblk - 1), 0)),
                  pl.BlockSpec((None, 3, FF_SHARD), lambda j, b, c: (j, 0, 0))],
        out_specs=cur, out_shape=jax.ShapeDtypeStruct(dconv.shape, BF16),
        compiler_params=_params(("parallel", "parallel", "parallel")),
    )(dconv, dconv, cw)


def _my_index():
    return 4 * lax.axis_index("x") + 2 * lax.axis_index("y") + lax.axis_index("c")


def _peer(k):
    return (lax.axis_index("x") ^ ((k >> 2) & 1), lax.axis_index("y") ^ ((k >> 1) & 1),
            lax.axis_index("c") ^ (k & 1))


_HBM = pl.BlockSpec(memory_space=pltpu.HBM)
_SEM = pl.BlockSpec(memory_space=pltpu.SEMAPHORE)
_DATAFLOW = pltpu.SideEffectType.DATAFLOW_SIDE_EFFECTING


def _split_copies(gather, src_ref, land_ref, send_sems, recv_sems, local_sem):
    me = _my_index()

    def part(j):
        return src_ref if gather else src_ref.at[j]

    local = pltpu.make_async_copy(part(me), land_ref.at[me], local_sem)
    sends = [pltpu.make_async_remote_copy(
        src_ref=part(me ^ k), dst_ref=land_ref.at[me], send_sem=send_sems.at[k - 1], recv_sem=recv_sems.at[k - 1],
        device_id=_peer(k), device_id_type=pl.DeviceIdType.MESH) for k in range(1, N_DEV)]
    recvs = [pltpu.make_async_remote_copy(
        src_ref=part(me ^ k), dst_ref=land_ref.at[me ^ k], send_sem=send_sems.at[k - 1], recv_sem=recv_sems.at[k - 1],
        device_id=_peer(k), device_id_type=pl.DeviceIdType.MESH) for k in range(1, N_DEV)]
    return local, sends, recvs


def split_start(name, srcs, gather):
    n = len(srcs)
    lands = [((N_DEV,) + s.shape) if gather else s.shape for s in srcs]

    def body(*refs):
        ins, outs = refs[:2 * n], refs[2 * n:]
        for i in range(n):
            local, sends, _ = _split_copies(gather, ins[i], ins[n + i], *outs[3 * i:3 * i + 3])
            local.start()
            for cp in sends:
                cp.start()
        outs[-1][...] = jnp.zeros_like(outs[-1])

    dma7 = pltpu.SemaphoreType.DMA((N_DEV - 1,))
    out = pl.pallas_call(
        body, name=name,
        out_shape=(dma7, dma7, pltpu.SemaphoreType.DMA(())) * n
                  + tuple(pltpu.HBM(s.shape, s.dtype) for s in srcs)
                  + tuple(pltpu.HBM(shape, s.dtype) for shape, s in zip(lands, srcs))
                  + (jax.ShapeDtypeStruct((8, 128), F32),),
        in_specs=(_HBM,) * (2 * n),
        out_specs=(_SEM,) * (3 * n) + (_HBM,) * (2 * n) + (pl.BlockSpec(memory_space=pltpu.VMEM),),
        input_output_aliases={i: 3 * n + i for i in range(2 * n)},
        compiler_params=pltpu.CompilerParams(has_side_effects=_DATAFLOW),
    )(*[pltpu.with_memory_space_constraint(s, pltpu.HBM) for s in srcs],
      *[pltpu.with_memory_space_constraint(lax.empty(shape, s.dtype), pltpu.HBM) for shape, s in zip(lands, srcs)])
    handles = [tuple(out[3 * i:3 * i + 3]) + (out[3 * n + i], out[4 * n + i]) for i in range(n)]
    return handles, out[-1][0, 0]


def split_wait(name, handles, after, gather):
    send_sems, recv_sems, local_sem, src_thru, land_thru = handles

    def body(src_ref, land_ref, send_sems, recv_sems, local_sem, after_ref, src_dead, got_ref, token):
        local, sends, recvs = _split_copies(gather, src_ref, land_ref, send_sems, recv_sems, local_sem)
        local.wait()
        for cp in recvs:
            cp.wait_send()
            cp.wait_recv()
        token[...] = jnp.zeros_like(token)

    out = pl.pallas_call(
        body, name=name,
        out_shape=(pltpu.HBM(src_thru.shape, src_thru.dtype), pltpu.HBM(land_thru.shape, land_thru.dtype),
                   jax.ShapeDtypeStruct((8, 128), F32)),
        in_specs=(_HBM, _HBM, _SEM, _SEM, _SEM, pl.BlockSpec(memory_space=pl.ANY)),
        out_specs=(_HBM, _HBM, pl.BlockSpec(memory_space=pltpu.VMEM)),
        input_output_aliases={0: 0, 1: 1},
        compiler_params=pltpu.CompilerParams(has_side_effects=_DATAFLOW),
    )(src_thru, land_thru, send_sems, recv_sems, local_sem, after)
    return out[1], out[2][0, 0]


def sum_parts(name, r):
    _, rows, cols = r.shape

    def body(r_ref, o_ref):
        acc = r_ref[0].astype(F32)
        for s in range(1, N_DEV):
            acc = acc + r_ref[s].astype(F32)
        o_ref[...] = acc

    return pl.pallas_call(body, name=name, out_shape=jax.ShapeDtypeStruct((rows, cols), F32),
                          compiler_params=_params())(r)


def adamw(name, w, m, v, parts=None, g=None, layer=0, into=None, order=None):
    _, rows, cols = w.shape
    br = _tile(rows, 256, 16)
    c1 = 1.0 / (1.0 - ADAM_B1 ** ADAM_STEP)
    c2 = 1.0 / (1.0 - ADAM_B2 ** ADAM_STEP)

    def body(g_ref, w_ref, m_ref, v_ref, *rest):
        og_ref, od_ref, om_ref, ov_ref = rest[-4:]
        if parts is None:
            gs = g_ref[...]
        else:
            gs = g_ref[0].astype(F32)
            for s in range(1, N_DEV):
                gs = gs + g_ref[s].astype(F32)
        mn = ADAM_B1 * m_ref[...] + (1.0 - ADAM_B1) * gs
        vn = ADAM_B2 * v_ref[...] + (1.0 - ADAM_B2) * (gs * gs)
        og_ref[...] = gs
        om_ref[...] = mn
        ov_ref[...] = vn
        od_ref[...] = -ADAM_LR * ((mn * c1) / (jnp.sqrt(vn * c2) + ADAM_EPS) + ADAM_WD * w_ref[...])

    blk = pl.BlockSpec((None, br, cols), lambda i: (layer, i, 0))
    if parts is None:
        gspec = pl.BlockSpec((br, cols), lambda i: (i, 0))
    else:
        gspec = pl.BlockSpec((N_DEV, br, cols), lambda i: (0, i, 0))
    earlier = [] if into is None else list(into)
    behind = [] if order is None else [order]
    return pl.pallas_call(
        body, name=name, grid=(rows // br,),
        in_specs=[gspec, blk, blk, blk] + [pl.BlockSpec(memory_space=pl.ANY)] * len(earlier)
                 + [pl.BlockSpec((1, 128), lambda i: (0, 0))] * len(behind),
        out_specs=[blk] * 4, out_shape=[jax.ShapeDtypeStruct(w.shape, F32)] * 4,
        input_output_aliases={4 + k: k for k in range(len(earlier))},
        compiler_params=_params(("parallel",)),
    )(g if parts is None else parts, w, m, v, *earlier, *behind)


SMALL = ("norm_mix", "norm_xattn", "norm_ffn", "norm_mem", "norm_final", "pool_w", "pool_scale",
         "ssm_lam_re", "ssm_lam_im", "ssm_log_dt", "ssm_b_re", "ssm_b_im", "ssm_c_re", "ssm_c_im",
         "ffn_conv_b", "ssm_d", "ffn_conv_w")
SMALL_SHARDED = {"ssm_d": 1, "ffn_conv_w": 2}
BIG = ("ab_w_in", "ab_w_out", "ssm_w_in", "ssm_w_glu", "xa_w_q", "xa_w_kv", "xa_w_o", "ffn_w_up", "ffn_w_down")
WEIGHTS = ("norm_mix", "norm_xattn", "norm_ffn", "norm_mem", "norm_final", "ab_w_in", "pool_w", "pool_scale",
           "ab_w_out", "ssm_w_in", "ssm_lam_re", "ssm_lam_im", "ssm_log_dt", "ssm_b_re", "ssm_b_im", "ssm_c_re",
           "ssm_c_im", "ssm_d", "ssm_w_glu", "xa_w_q", "xa_w_kv", "xa_w_o", "ffn_w_up", "ffn_conv_w", "ffn_conv_b",
           "ffn_w_down")


def _rows8(g):
    return g.reshape(N_DEV, g.size // (N_DEV * D_MODEL), D_MODEL)


def _square(a):
    return a.reshape(D_MODEL, D_MODEL)


_LAYOUT = {"ab_w_out": _square, "ssm_w_in": _square, "xa_w_q": _square, "xa_w_o": _square,
           "ffn_w_down": lambda a: a.reshape(N_DEV // 2, FF_SHARD, D_MODEL)}
GATHER_ORDER = (("ab_w_in", 0), ("ffn_conv_w", None), ("ssm_d", None), ("ab_w_out", 0), ("xa_w_q", 0),
                ("xa_w_kv", 0), ("xa_w_o", 0), ("ffn_w_up", 0), ("ffn_w_down", 0), ("ffn_w_up", 1),
                ("ffn_w_down", 1), ("ssm_w_in", 0), ("ssm_w_glu", 0), ("xa_w_q", 1), ("xa_w_kv", 1), ("xa_w_o", 1))
GATHER_FIRST = 3
GATHER_AHEAD = 7


class _Step:
    def __init__(self, master, small):
        self.master, self.small = master, small
        self.pending, self.gathers, self.weights, self.sent, self.queued = [], {}, {}, [], []

    def follow(self, v):
        for z in self.pending:
            v = v + z
        self.pending = []
        return v

    def start_gathers(self, upto, zero):
        todo = GATHER_ORDER[len(self.gathers):upto]
        if not todo:
            return
        shards = []
        for n, l in todo:
            if l is None:
                shards.append(self.master[n] + zero)
            else:
                shards.append((self.master[n][l] + zero).astype(MXU_DTYPE))
        handles, z = split_start(f"ags_{len(self.gathers)}", shards, gather=True)
        self.gathers.update(zip(todo, handles))
        self.pending.append(z)

    def weight(self, n, l, after):
        if (n, l) not in self.weights:
            full, z = split_wait(f"agw_{n}{'' if l is None else l}", self.gathers[(n, l)], after, gather=True)
            self.weights[(n, l)] = _LAYOUT.get(n, lambda a: a)(full)
            self.start_gathers(GATHER_ORDER.index((n, l)) + 1 + GATHER_AHEAD, z)
        return self.weights[(n, l)]

    def send_grad(self, n, l, part, flush=True):
        self.queued.append((n, l, part))
        if flush:
            handles, z = split_start(f"xs_{n}{l}", [p for _, _, p in self.queued], gather=False)
            self.sent += [(qn, ql, h) for (qn, ql, _), h in zip(self.queued, handles)]
            self.queued = []
            self.pending.append(z)


def _layer_tail(st, l, x_in, hq, mem_n, acts, next_gain=None):
    bsz, seq = acts["bsz"], acts["seq"]
    p = st.small
    q = mm_nn(f"xa_q{l}", hq, st.weight("xa_w_q", l, x_in))
    kv = mm_nn_bs(f"xa_kv{l}", mem_n, st.weight("xa_w_kv", l, x_in))
    o = xattn_fwd(q, kv, bsz, seq)
    x_mid, hf = mm_nn(f"xa_o{l}", o, st.weight("xa_w_o", l, o), res=x_in, out_dtype=F32,
                      norm_gain=st.follow(p["norm_ffn"][l]))
    up = mm_nn_bs(f"ffn_up{l}", hf, st.weight("ffn_w_up", l, x_mid), stacked_out=True)
    conv_w = st.weight("ffn_conv_w", None, x_mid)[:, l]
    act = conv_fwd(up, conv_w, p["ffn_conv_b"][l], bsz, seq)
    w_down = st.weight("ffn_w_down", l, act)
    if next_gain is None:
        x_out, h_next = mm_as_nn(f"ffn_down{l}", act, w_down, res=x_mid), None
    else:
        x_out, h_next = mm_as_nn(f"ffn_down{l}", act, w_down, res=x_mid, norm_gain=st.follow(next_gain))
    acts[l].update(x_in=x_in, hq=hq, q=q, kv=kv, o=o, x_mid=x_mid, hf=hf, up=up, act=act)
    return x_out, h_next


def _layer_tail_bwd(st, l, dx, mem_n, acts, grads):
    a = acts[l]
    bsz, seq = acts["bsz"], acts["seq"]
    p = st.small
    dact = mm_nt_os(f"d_act{l}", dx, st.weight("ffn_w_down", l, dx))
    st.send_grad("ffn_w_down", l, _rows8(mm_tn(f"g_ffn_down{l}", a["act"], dx, a_stacked=True)), flush=False)
    conv_w = st.weight("ffn_conv_w", None, dx)[:, l]
    dconv, dcw, dcb = conv_bwd_taps(a["up"], conv_w, p["ffn_conv_b"][l], dact, bsz, seq)
    grads["ffn_conv_w"][l] = dcw
    grads["ffn_conv_b"][l] = dcb
    dup = conv_bwd_input(dconv, conv_w, bsz, seq)
    dx_mid, grads["norm_ffn"][l] = mm_nt_bs(f"d_hf{l}", dup, st.weight("ffn_w_up", l, dx), dc_stacked=True,
                                            rms=(a["x_mid"], st.follow(p["norm_ffn"][l]), dx))
    st.send_grad("ffn_w_up", l, mm_tn(f"g_ffn_up{l}", a["hf"], dup, dc_stacked=True), flush=False)
    do = mm_nt(f"d_o{l}", dx_mid, st.weight("xa_w_o", l, dx))
    st.send_grad("xa_w_o", l, _rows8(mm_tn(f"g_xa_o{l}", a["o"], dx_mid)), flush=False)
    dq, dk, dv = xattn_bwd(a["q"], a["kv"], do, bsz, seq)
    dkv = jnp.concatenate([dk, dv], axis=1).astype(BF16)
    dx_in, grads["norm_xattn"][l] = mm_nt(f"d_hq{l}", dq, st.weight("xa_w_q", l, dx),
                                          rms=(a["x_in"], st.follow(p["norm_xattn"][l]), dx_mid))
    st.send_grad("xa_w_q", l, _rows8(mm_tn(f"g_xa_q{l}", a["hq"], dq)), flush=False)
    dmem_n = mm_nt_bs(f"d_memn{l}", dkv, st.weight("xa_w_kv", l, dx), out_dtype=F32)
    st.send_grad("xa_w_kv", l, mm_tn(f"g_xa_kv{l}", mem_n, dkv, dc_cols=2 * D_MODEL // N_DEV))
    return dx_in, dmem_n


def kernel(x, mem, norm_mix, norm_xattn, norm_ffn, norm_mem, norm_final, ab_w_in, pool_w, pool_scale, ab_w_out, ssm_w_in, ssm_lam_re, ssm_lam_im, ssm_log_dt, ssm_b_re, ssm_b_im, ssm_c_re, ssm_c_im, ssm_d, ssm_w_glu, xa_w_q, xa_w_kv, xa_w_o, ffn_w_up, ffn_conv_w, ffn_conv_b, ffn_w_down, loss_target, m_norm_mix, m_norm_xattn, m_norm_ffn, m_norm_mem, m_norm_final, m_ab_w_in, m_pool_w, m_pool_scale, m_ab_w_out, m_ssm_w_in, m_ssm_lam_re, m_ssm_lam_im, m_ssm_log_dt, m_ssm_b_re, m_ssm_b_im, m_ssm_c_re, m_ssm_c_im, m_ssm_d, m_ssm_w_glu, m_xa_w_q, m_xa_w_kv, m_xa_w_o, m_ffn_w_up, m_ffn_conv_w, m_ffn_conv_b, m_ffn_w_down, v_norm_mix, v_norm_xattn, v_norm_ffn, v_norm_mem, v_norm_final, v_ab_w_in, v_pool_w, v_pool_scale, v_ab_w_out, v_ssm_w_in, v_ssm_lam_re, v_ssm_lam_im, v_ssm_log_dt, v_ssm_b_re, v_ssm_b_im, v_ssm_c_re, v_ssm_c_im, v_ssm_d, v_ssm_w_glu, v_xa_w_q, v_xa_w_kv, v_xa_w_o, v_ffn_w_up, v_ffn_conv_w, v_ffn_conv_b, v_ffn_w_down):
    given = dict(locals())
    master = {n: given[n] for n in WEIGHTS}
    mom1 = {n: given["m_" + n] for n in WEIGHTS}
    mom2 = {n: given["v_" + n] for n in WEIGHTS}
    bsz, seq, d = x.shape
    t = bsz * seq
    me = _my_index()

    st = _Step(master, {"norm_xattn": norm_xattn, "norm_ffn": norm_ffn,
                        "ffn_conv_b": [ffn_conv_b[l].reshape(N_DEV, 1, FF_SHARD) for l in range(2)]})
    st.start_gathers(GATHER_FIRST, 0.0)
    zero = st.follow(jnp.zeros((), F32))

    acts = {"bsz": bsz, "seq": seq, 0: {}, 1: {}}
    x0 = x.reshape(t, d)
    mem2 = mem.reshape(bsz * MEM_LEN, d)
    mem_n = rms_fwd("rms_mem", mem2, norm_mem + zero)
    pscale = pool_scale.reshape(1, SB_WIDTH)

    h0 = rms_fwd("rms_mix0", x0, norm_mix[0] + zero)
    w_in = st.weight("ab_w_in", 0, h0)
    proj = mm_nn_bs("ab_in", h0, w_in, out_dtype=F32)
    a_out, rsum = sb_attn_fwd(proj, st.follow(jnp.zeros((1, 128), F32)), bsz, seq)
    p_out = pool_fwd(proj, pool_w[0], pscale, bsz, seq)
    w_out = st.weight("ab_w_out", 0, a_out)
    x1 = mm_nn("ab_out_a", a_out, w_out, res=x0, out_dtype=F32)
    x1, hq0 = mm_nn("ab_out_p", p_out, w_out, res=x1, koff=SB_WIDTH, out_dtype=F32,
                    norm_gain=st.follow(norm_xattn[0]))
    x3, h1 = _layer_tail(st, 0, x1, hq0, mem_n, acts, next_gain=norm_mix[1])

    b_re2 = ssm_b_re.reshape(64, 1024)
    b_im2 = ssm_b_im.reshape(64, 1024)
    log_dt = ssm_log_dt.reshape(64, 1)
    lb_re, lb_im, bb_re2, bb_im2 = ssm_prep(ssm_lam_re[0], ssm_lam_im[0], log_dt, b_re2, b_im2)
    wt = _ssm_in_weights(bb_re2, bb_im2)
    ct = _ssm_out_weights(ssm_c_re[0], ssm_c_im[0])
    a_re = lb_re.reshape(1, SSM_STATES)
    a_im = lb_im.reshape(1, SSM_STATES)
    u = mm_nn("ssm_in", h1, st.weight("ssm_w_in", 0, x3), out_dtype=F32)
    dskip = st.weight("ssm_d", None, x3).reshape(1, D_MODEL)
    y, gl, h_re, h_im = ssm_fwd(u, wt, ct, a_re, a_im, dskip, bsz, seq)
    glu = mm_nn_bs("ssm_glu", gl, st.weight("ssm_w_glu", 0, gl), out_dtype=F32)
    x4, hq1 = glu_fwd(glu, x3, st.follow(norm_xattn[1]))
    x6, _ = _layer_tail(st, 1, x4, hq1, mem_n, acts)

    loss_row, dx, g_norm_final = loss_head(x6, norm_final, loss_target.reshape(t, d))
    loss = lax.psum(loss_row[0, 0], MESH_AXES)

    grads = {n: [None, None] for n in ("ffn_conv_w", "ffn_conv_b", "norm_ffn", "norm_xattn", "norm_mix")}
    dx4, dmem_1 = _layer_tail_bwd(st, 1, dx, mem_n, acts, grads)
    dglu = glu_bwd(glu, dx4)
    dgl = mm_nt_bs("d_gl", dglu, st.weight("ssm_w_glu", 0, dx))
    st.send_grad("ssm_w_glu", 0, mm_tn("g_ssm_glu", gl, dglu, dc_cols=2 * D_MODEL // N_DEV), flush=False)
    du, dwt, dct, g_dskip, da_re, da_im = ssm_bwd(dgl, y, u, h_re, h_im, wt, ct, a_re, a_im, dskip, bsz, seq)
    dbb_re, dbb_im = _ssm_in_weights_bwd(dwt)
    g_c_re, g_c_im = _ssm_out_weights_bwd(dct)
    g_lam_re, g_lam_im, g_log_dt, g_b_re, g_b_im = ssm_prep_bwd(
        ssm_lam_re[0], ssm_lam_im[0], log_dt, b_re2, b_im2, da_re.reshape(64, 64), da_im.reshape(64, 64),
        dbb_re, dbb_im)
    dx3, grads["norm_mix"][1] = mm_nt("d_h1", du, st.weight("ssm_w_in", 0, dx),
                                      rms=(x3, st.follow(norm_mix[1]), dx4))
    st.send_grad("ssm_w_in", 0, _rows8(mm_tn("g_ssm_in", h1, du)), flush=False)

    dx1, dmem_0 = _layer_tail_bwd(st, 0, dx3, mem_n, acts, grads)
    dcat = mm_nt("d_cat", dx1, st.weight("ab_w_out", 0, dx))
    st.send_grad("ab_w_out", 0, _rows8(jnp.concatenate(
        [mm_tn("g_ab_out_a", a_out, dx1), mm_tn("g_ab_out_p", p_out, dx1)], axis=0)), flush=False)
    dq, dk, dv = sb_attn_bwd(proj, rsum, dcat, bsz, seq)
    dpu, g_pool_w, g_pool_scale = pool_bwd(proj, pool_w[0], st.follow(pscale), dcat, bsz, seq)
    dproj = jnp.concatenate([dq, dk, dv, dpu], axis=1).astype(BF16)
    st.send_grad("ab_w_in", 0, mm_tn("g_ab_in", h0, dproj, dc_cols=2 * D_MODEL // N_DEV))
    dx0, grads["norm_mix"][0] = mm_nt_bs("d_h0", dproj, st.weight("ab_w_in", 0, dx),
                                         rms=(x0, st.follow(norm_mix[0]), dx1))
    _, g_norm_mem = rms_bwd("rms_mem_bwd", mem2, norm_mem, dmem_0 + dmem_1, need_dx=False)

    small_g = {
        "norm_mix": jnp.stack([g[0] for g in grads["norm_mix"]]),
        "norm_xattn": jnp.stack([g[0] for g in grads["norm_xattn"]]),
        "norm_ffn": jnp.stack([g[0] for g in grads["norm_ffn"]]),
        "norm_mem": g_norm_mem[0], "norm_final": g_norm_final[0],
        "pool_w": g_pool_w[None], "pool_scale": g_pool_scale,
        "ssm_lam_re": g_lam_re[None], "ssm_lam_im": g_lam_im[None], "ssm_log_dt": g_log_dt.reshape(1, 64),
        "ssm_b_re": g_b_re.reshape(1, 64, 64, 16), "ssm_b_im": g_b_im.reshape(1, 64, 64, 16),
        "ssm_c_re": g_c_re[None], "ssm_c_im": g_c_im[None],
        "ffn_conv_b": jnp.stack([g.reshape(2 * D_FF) for g in grads["ffn_conv_b"]]),
        "ssm_d": g_dskip,
        "ffn_conv_w": jnp.stack([g.transpose(1, 0, 2).reshape(3, 2 * D_FF) for g in grads["ffn_conv_w"]]),
    }
    sizes = [int(small_g[n].size) for n in SMALL]
    total = sum(sizes)
    rows8 = -(-total // (N_DEV * 128 * 8)) * 8
    flat = jnp.concatenate([small_g[n].reshape(-1).astype(F32) for n in SMALL]
                           + [jnp.zeros((N_DEV * rows8 * 128 - total,), F32)])
    (in_flight,), z = split_start("xs_small", [flat.reshape(N_DEV, rows8, 128)], gather=False)
    st.pending.append(z)
    stepped, last = {}, dx0
    for i, (n, l, handles) in enumerate(st.sent):
        if i == len(st.sent) // 2:
            recv, _ = split_wait("xw_small", in_flight, last, gather=False)
            (in_flight,), z = split_start("ags_small", [sum_parts("sum_small", recv)], gather=True)
            st.pending.append(z)
        recv, _ = split_wait(f"xw_{n}{l}", handles, dx0, gather=False)
        shape3 = (master[n].shape[0],) + recv.shape[1:]
        stepped[n] = adamw(f"adamw_{n}{l}", master[n].reshape(shape3), mom1[n].reshape(shape3),
                           mom2[n].reshape(shape3), parts=recv, layer=l, into=stepped.get(n),
                           order=st.follow(jnp.zeros((1, 128), F32)))
        last = stepped[n][0]
    out_g, out_d, out_m, out_v = ({n: stepped[n][k].reshape(master[n].shape) for n in BIG} for k in range(4))
    summed = split_wait("agw_small", in_flight, last, gather=True)[0].reshape(-1)

    def local_part(name, a):
        ax = SMALL_SHARDED.get(name)
        if ax is None:
            return a
        n_loc = a.shape[ax] // N_DEV
        return lax.dynamic_slice_in_dim(a, me * n_loc, n_loc, axis=ax)

    off = 0
    for n, sz in zip(SMALL, sizes):
        g_n = local_part(n, summed[off:off + sz].reshape(small_g[n].shape))
        off += sz
        cols = g_n.shape[-1] if g_n.shape[-1] >= 128 or g_n.ndim < 3 else g_n.shape[-1] * g_n.shape[-2]
        shape3 = (1, g_n.size // cols, cols)
        res = adamw("adamw_" + n, master[n].reshape(shape3), mom1[n].reshape(shape3), mom2[n].reshape(shape3),
                    g=g_n.reshape(shape3[1:]))
        for dst, r in zip((out_g, out_d, out_m, out_v), res):
            dst[n] = r.reshape(master[n].shape)

    return (loss, dx0.reshape(bsz, seq, d), *[out_g[n] for n in WEIGHTS], *[out_d[n] for n in WEIGHTS],
            *[out_m[n] for n in WEIGHTS], *[out_v[n] for n in WEIGHTS])
```

```python
import math

import jax
import jax.numpy as jnp
from jax import lax
from jax.experimental import pallas as pl
from jax.experimental.pallas import tpu as pltpu

F32 = jnp.float32
BF16 = jnp.bfloat16
MXU_DTYPE = jnp.bfloat16
N_DEV = 8
MESH_AXES = ("x", "y", "c")

D_MODEL = 1024
SB_HEAD_DIM = 64
SB_WIDTH = 512
SB_BLOCK = 256
POOL_WINDOWS = (2, 4, 8, 16)
POOL_GROUP = 128
POOL_HALO = 16
SSM_TILES = 8
SSM_TILE_STATES = 512
SSM_STATES = 4096
SSM_LANES = 1024
MEM_LEN = 256
XA_HEADS = 4
XA_HEAD_DIM = 256
D_FF = 2816
FF_SHARD = 704
EPS = 1e-6
ADAM_LR = 0.001
ADAM_B1 = 0.9
ADAM_B2 = 0.999
ADAM_EPS = 1e-08
ADAM_WD = 0.01
ADAM_STEP = 10
VMEM_LIMIT = 56 * 1024 * 1024

_NN = (((1,), (0,)), ((), ()))
_NT = (((1,), (1,)), ((), ()))
_TN = (((0,), (0,)), ((), ()))


def _params(sem=None):
    if sem is None:
        return pltpu.CompilerParams(vmem_limit_bytes=VMEM_LIMIT)
    return pltpu.CompilerParams(dimension_semantics=sem, vmem_limit_bytes=VMEM_LIMIT)


def _tile(n, pref, mult=8):
    if n <= pref:
        return n
    for t in range(pref, 0, -1):
        if n % t == 0 and t % mult == 0:
            return t
    return n


def _dot(a, b, dims):
    return lax.dot_general(a.astype(MXU_DTYPE), b.astype(MXU_DTYPE), dims, preferred_element_type=F32)


def _dot_exact01(x, m01, dims=_NN):
    x1 = x.astype(BF16)
    r1 = x - x1.astype(F32)
    x2 = r1.astype(BF16)
    x3 = (r1 - x2.astype(F32)).astype(BF16)
    m = m01.astype(BF16)
    out = lax.dot_general(x1, m, dims, preferred_element_type=F32)
    out = out + lax.dot_general(x2, m, dims, preferred_element_type=F32)
    return out + lax.dot_general(x3, m, dims, preferred_element_type=F32)


def _mm(name, a, b, dims, grid, a_spec, b_spec, o_spec, out_shape, out_dtype, acc_shape, res=None, r_spec=None,
        group=1, n=None, a_sel="full", b_sel="full", o_sel="full", norm_gain=None, rms=None):
    nk = grid[2]
    if out_dtype is None:
        out_dtype = BF16
    n_out = out_shape[-1]
    vec = pl.BlockSpec((1, n_out), lambda i, j, kk: (0, 0))

    def at(sel, s):
        if sel == "lead":
            return (s,)
        if sel == "lanes":
            return (slice(None), slice(s * n, (s + 1) * n))
        return (Ellipsis,)

    extra = [] if res is None else [(res, r_spec)]
    if norm_gain is not None:
        extra.append((norm_gain.reshape(1, n_out), vec))
    if rms is not None:
        extra += [(rms[0], o_spec), (rms[1].reshape(1, n_out), vec), (rms[2], o_spec)]
    n_in = 2 + len(extra)
    if rms is not None:
        out_specs = [o_spec, vec]
        out_shapes = [jax.ShapeDtypeStruct(out_shape, F32), jax.ShapeDtypeStruct((1, n_out), F32)]
    elif norm_gain is not None:
        out_specs = [o_spec, o_spec]
        out_shapes = [jax.ShapeDtypeStruct(out_shape, out_dtype), jax.ShapeDtypeStruct(out_shape, BF16)]
    else:
        out_specs, out_shapes = o_spec, jax.ShapeDtypeStruct(out_shape, out_dtype)

    def body(*refs):
        a_ref, b_ref = refs[0], refs[1]
        ins = list(refs[2:n_in])
        r_ref = ins.pop(0) if res is not None else None
        outs = refs[n_in:]
        o_ref = outs[0]
        acc = refs[-1] if nk > 1 else None
        k = pl.program_id(2)

        def finish(val):
            if r_ref is not None:
                val = val + r_ref[...].astype(F32)
            if rms is not None:
                x_ref, g_ref, d_ref = ins
                xf = x_ref[...]
                r = lax.rsqrt(jnp.mean(xf * xf, axis=-1, keepdims=True) + EPS)
                xh = xf * r
                part = jnp.sum(val * xh, axis=0, keepdims=True)
                first = pl.program_id(0) == 0

                @pl.when(first)
                def _():
                    outs[1][...] = part

                @pl.when(jnp.logical_not(first))
                def _():
                    outs[1][...] += part

                dxh = val * g_ref[...]
                o_ref[...] = d_ref[...] + r * (dxh - xh * jnp.mean(dxh * xh, axis=-1, keepdims=True))
                return
            o_ref[...] = val.astype(out_dtype)
            if norm_gain is not None:
                r = lax.rsqrt(jnp.mean(val * val, axis=-1, keepdims=True) + EPS)
                outs[1][...] = (val * r * ins[0][...]).astype(BF16)

        def emit(s, val):
            if nk == 1:
                if o_sel == "full":
                    finish(val)
                else:
                    o_ref[at(o_sel, s)] = val.astype(out_dtype)
                return

            @pl.when(k == 0)
            def _():
                acc[at(o_sel, s)] = val

            @pl.when(k > 0)
            def _():
                acc[at(o_sel, s)] += val

        total = None
        if a_sel == "full" and b_sel == "lanes":
            wide = _dot(a_ref[...], b_ref[...], dims)
            for s in range(group):
                emit(s, wide[:, s * n:(s + 1) * n])
        else:
            for s in range(group):
                val = _dot(a_ref[at(a_sel, s)], b_ref[at(b_sel, s)], dims)
                if o_sel == "full":
                    total = val if total is None else total + val
                else:
                    emit(s, val)
        if o_sel == "full":
            emit(0, total)
        if nk > 1:
            @pl.when(k == nk - 1)
            def _():
                if o_sel == "full":
                    finish(acc[...])
                else:
                    o_ref[...] = acc[...].astype(out_dtype)

    rows_sem = "arbitrary" if rms is not None else "parallel"
    return pl.pallas_call(
        body, name=name, grid=grid, in_specs=[a_spec, b_spec] + [s for _, s in extra], out_specs=out_specs,
        out_shape=out_shapes, scratch_shapes=[pltpu.VMEM(acc_shape, F32)] if nk > 1 else [],
        compiler_params=_params((rows_sem, rows_sem, "arbitrary")),
    )(a, b, *[x for x, _ in extra])


def _row_tile(m, epi):
    return _tile(m, 512 if epi.get("rms") is not None else 1024)


def mm_nn(name, a, b, res=None, koff=0, out_dtype=None, **epi):
    m, k = a.shape
    n = b.shape[1]
    tm, tn, tk = _row_tile(m, epi), _tile(n, 1024, 128), _tile(k, 1024, 128)
    kb = koff // tk
    spec = pl.BlockSpec((tm, tn), lambda i, j, kk: (i, j))
    return _mm(name, a, b, _NN, (m // tm, n // tn, k // tk),
               pl.BlockSpec((tm, tk), lambda i, j, kk: (i, kk)),
               pl.BlockSpec((tk, tn), lambda i, j, kk: (kk + kb, j)),
               spec, (m, n), out_dtype, (tm, tn), res, spec, **epi)


def mm_nn_bs(name, a, bs, stacked_out=False, out_dtype=None):
    m, k = a.shape
    s, _, n = bs.shape
    tm, tk = _tile(m, 1024), _tile(k, 1024, 128)
    a_spec = pl.BlockSpec((tm, tk), lambda i, j, kk: (i, kk))
    if stacked_out:
        return _mm(name, a, bs, _NN, (m // tm, s, k // tk), a_spec,
                   pl.BlockSpec((None, tk, n), lambda i, j, kk: (j, kk, 0)),
                   pl.BlockSpec((None, tm, n), lambda i, j, kk: (j, i, 0)), (s, m, n), out_dtype, (tm, n))
    g = _tile(s, max(1, 1024 // n), 1)
    return _mm(name, a, bs, _NN, (m // tm, s // g, k // tk), a_spec,
               pl.BlockSpec((g, tk, n), lambda i, j, kk: (j, kk, 0)),
               pl.BlockSpec((tm, g * n), lambda i, j, kk: (i, j)), (m, s * n), out_dtype, (tm, g * n),
               group=g, n=n, b_sel="lead", o_sel="lanes")


def mm_as_nn(name, a_st, b3, res, out_dtype=F32, **epi):
    s, m, kp = a_st.shape
    n = b3.shape[2]
    tm, tn = _row_tile(m, epi), _tile(n, 1024, 128)
    spec = pl.BlockSpec((tm, tn), lambda i, j, kk: (i, j))
    g = _tile(s, 2, 1)
    return _mm(name, a_st, b3, _NN, (m // tm, n // tn, s // g),
               pl.BlockSpec((g, tm, kp), lambda i, j, kk: (kk, i, 0)),
               pl.BlockSpec((g, kp, tn), lambda i, j, kk: (kk, 0, j)),
               spec, (m, n), out_dtype, (tm, tn), res, spec, group=g, a_sel="lead", b_sel="lead", **epi)


def mm_nt(name, dc, b, out_dtype=None, **epi):
    m, n = dc.shape
    k = b.shape[0]
    tm, tko, tnr = _row_tile(m, epi), _tile(k, 1024, 128), _tile(n, 1024, 128)
    return _mm(name, dc, b, _NT, (m // tm, k // tko, n // tnr),
               pl.BlockSpec((tm, tnr), lambda i, j, kk: (i, kk)),
               pl.BlockSpec((tko, tnr), lambda i, j, kk: (j, kk)),
               pl.BlockSpec((tm, tko), lambda i, j, kk: (i, j)), (m, k), out_dtype, (tm, tko), **epi)


def mm_nt_bs(name, dc, bs, dc_stacked=False, out_dtype=None, **epi):
    s, k, n = bs.shape
    m = dc.shape[1] if dc_stacked else dc.shape[0]
    tm, tko = (_tile(m, 1024) if dc_stacked else _row_tile(m, epi)), _tile(k, 1024, 128)
    o_spec = pl.BlockSpec((tm, tko), lambda i, j, kk: (i, j))
    if dc_stacked:
        g = _tile(s, 2, 1)
        return _mm(name, dc, bs, _NT, (m // tm, k // tko, s // g),
                   pl.BlockSpec((g, tm, n), lambda i, j, kk: (kk, i, 0)),
                   pl.BlockSpec((g, tko, n), lambda i, j, kk: (kk, j, 0)), o_spec, (m, k), out_dtype, (tm, tko),
                   group=g, a_sel="lead", b_sel="lead", **epi)
    g = _tile(s, max(1, 2048 // n), 1)
    return _mm(name, dc, bs, _NT, (m // tm, k // tko, s // g),
               pl.BlockSpec((tm, g * n), lambda i, j, kk: (i, kk)),
               pl.BlockSpec((g, tko, n), lambda i, j, kk: (kk, j, 0)), o_spec, (m, k), out_dtype, (tm, tko),
               group=g, n=n, a_sel="lanes", b_sel="lead", **epi)


def mm_nt_os(name, dc, b3, out_dtype=None):
    m, n = dc.shape
    s, kp, _ = b3.shape
    tm, tnr = _tile(m, 1024), _tile(n, 1024, 128)
    return _mm(name, dc, b3, _NT, (m // tm, s, n // tnr),
               pl.BlockSpec((tm, tnr), lambda i, j, kk: (i, kk)),
               pl.BlockSpec((None, kp, tnr), lambda i, j, kk: (j, 0, kk)),
               pl.BlockSpec((None, tm, kp), lambda i, j, kk: (j, i, 0)), (s, m, kp), out_dtype, (tm, kp))


def mm_tn(name, a, dc, a_stacked=False, dc_cols=None, dc_stacked=False, out_dtype=None):
    if a_stacked:
        s, m, kp = a.shape
        n = dc.shape[1]
        tno, tmr = _tile(n, 1024, 128), _tile(m, 2048)
        return _mm(name, a, dc, _TN, (s, n // tno, m // tmr),
                   pl.BlockSpec((None, tmr, kp), lambda i, j, kk: (i, kk, 0)),
                   pl.BlockSpec((tmr, tno), lambda i, j, kk: (kk, j)),
                   pl.BlockSpec((None, kp, tno), lambda i, j, kk: (i, 0, j)), (s, kp, n), out_dtype, (kp, tno))
    m, k = a.shape
    tko, tmr = _tile(k, 1024, 128), _tile(m, 2048)
    a_spec = pl.BlockSpec((tmr, tko), lambda i, j, kk: (kk, i))
    if dc_stacked:
        s, _, n = dc.shape
        return _mm(name, a, dc, _TN, (k // tko, s, m // tmr), a_spec,
                   pl.BlockSpec((None, tmr, n), lambda i, j, kk: (j, kk, 0)),
                   pl.BlockSpec((None, tko, n), lambda i, j, kk: (j, i, 0)), (s, k, n), out_dtype, (tko, n))
    if dc_cols is not None:
        n = dc_cols
        s = dc.shape[1] // n
        g = _tile(s, max(1, 1024 // n), 1)
        return _mm(name, a, dc, _TN, (k // tko, s // g, m // tmr), a_spec,
                   pl.BlockSpec((tmr, g * n), lambda i, j, kk: (kk, j)),
                   pl.BlockSpec((g, tko, n), lambda i, j, kk: (j, i, 0)), (s, k, n), out_dtype, (g, tko, n),
                   group=g, n=n, b_sel="lanes", o_sel="lead")
    n = dc.shape[1]
    tno = _tile(n, 1024, 128)
    return _mm(name, a, dc, _TN, (k // tko, n // tno, m // tmr), a_spec,
               pl.BlockSpec((tmr, tno), lambda i, j, kk: (kk, j)),
               pl.BlockSpec((tko, tno), lambda i, j, kk: (i, j)), (k, n), out_dtype, (tko, tno))


def rms_fwd(name, x, g):
    t, d = x.shape
    tr = _tile(t, 512)

    def body(x_ref, g_ref, o_ref):
        xf = x_ref[...]
        r = lax.rsqrt(jnp.mean(xf * xf, axis=-1, keepdims=True) + EPS)
        o_ref[...] = (xf * r * g_ref[...]).astype(o_ref.dtype)

    return pl.pallas_call(
        body, name=name, grid=(t // tr,),
        in_specs=[pl.BlockSpec((tr, d), lambda i: (i, 0)), pl.BlockSpec((1, d), lambda i: (0, 0))],
        out_specs=pl.BlockSpec((tr, d), lambda i: (i, 0)),
        out_shape=jax.ShapeDtypeStruct((t, d), BF16), compiler_params=_params(("parallel",)),
    )(x, g.reshape(1, d))


def rms_bwd(name, x, g, dh, dres=None, need_dx=True):
    t, d = x.shape
    tr = _tile(t, 512)

    def body(*refs):
        refs = list(refs)
        x_ref, g_ref, dh_ref = refs[:3]
        r_ref = refs[3] if dres is not None else None
        outs = refs[4:] if dres is not None else refs[3:]
        dx_ref, dg_ref = (outs[0], outs[1]) if need_dx else (None, outs[0])
        i = pl.program_id(0)

        @pl.when(i == 0)
        def _():
            dg_ref[...] = jnp.zeros_like(dg_ref)

        xf = x_ref[...]
        dhf = dh_ref[...].astype(F32)
        r = lax.rsqrt(jnp.mean(xf * xf, axis=-1, keepdims=True) + EPS)
        xh = xf * r
        dg_ref[...] += jnp.sum(dhf * xh, axis=0, keepdims=True)
        if need_dx:
            dxh = dhf * g_ref[...]
            dx = r * (dxh - xh * jnp.mean(dxh * xh, axis=-1, keepdims=True))
            if r_ref is not None:
                dx = dx + r_ref[...]
            dx_ref[...] = dx

    row = pl.BlockSpec((tr, d), lambda i: (i, 0))
    vec = pl.BlockSpec((1, d), lambda i: (0, 0))
    in_specs = [row, vec, row] + ([row] if dres is not None else [])
    args = (x, g.reshape(1, d), dh) + ((dres,) if dres is not None else ())
    out_specs = ([row] if need_dx else []) + [vec]
    out_shape = ([jax.ShapeDtypeStruct((t, d), F32)] if need_dx else []) + [jax.ShapeDtypeStruct((1, d), F32)]
    res = pl.pallas_call(
        body, name=name, grid=(t // tr,), in_specs=in_specs, out_specs=out_specs, out_shape=out_shape,
        compiler_params=_params(("arbitrary",)),
    )(*args)
    return res if need_dx else (None, res[0])


def loss_head(x, g, tgt):
    t, d = x.shape
    tr = _tile(t, 512)

    def body(x_ref, g_ref, t_ref, l_ref, dx_ref, dg_ref):
        i = pl.program_id(0)

        @pl.when(i == 0)
        def _():
            l_ref[...] = jnp.zeros_like(l_ref)
            dg_ref[...] = jnp.zeros_like(dg_ref)

        xf = x_ref[...]
        r = lax.rsqrt(jnp.mean(xf * xf, axis=-1, keepdims=True) + EPS)
        xh = xf * r
        diff = xh * g_ref[...] - t_ref[...]
        l_ref[...] += 0.5 * jnp.sum(jnp.mean(diff * diff, axis=-1, keepdims=True))
        dy = diff * (1.0 / d)
        dg_ref[...] += jnp.sum(dy * xh, axis=0, keepdims=True)
        dxh = dy * g_ref[...]
        dx_ref[...] = r * (dxh - xh * jnp.mean(dxh * xh, axis=-1, keepdims=True))

    row = pl.BlockSpec((tr, d), lambda i: (i, 0))
    vec = pl.BlockSpec((1, d), lambda i: (0, 0))
    return pl.pallas_call(
        body, name="loss_head", grid=(t // tr,), in_specs=[row, vec, row],
        out_specs=[pl.BlockSpec((1, 128), lambda i: (0, 0)), row, vec],
        out_shape=[jax.ShapeDtypeStruct((1, 128), F32), jax.ShapeDtypeStruct((t, d), F32),
                   jax.ShapeDtypeStruct((1, d), F32)],
        compiler_params=_params(("arbitrary",)),
    )(x, g.reshape(1, d), tgt)


def glu_fwd(glu, x, gain):
    t, d = x.shape
    tr = _tile(t, 512)

    def body(v_ref, g_ref, x_ref, n_ref, o_ref, h_ref):
        y = x_ref[...] + v_ref[...] * jax.nn.sigmoid(g_ref[...])
        o_ref[...] = y
        r = lax.rsqrt(jnp.mean(y * y, axis=-1, keepdims=True) + EPS)
        h_ref[...] = (y * r * n_ref[...]).astype(h_ref.dtype)

    row = pl.BlockSpec((tr, d), lambda i: (i, 0))
    return pl.pallas_call(
        body, name="glu_fwd", grid=(t // tr,),
        in_specs=[row, pl.BlockSpec((tr, d), lambda i: (i, 1)), row, pl.BlockSpec((1, d), lambda i: (0, 0))],
        out_specs=[row, row],
        out_shape=[jax.ShapeDtypeStruct((t, d), F32), jax.ShapeDtypeStruct((t, d), BF16)],
        compiler_params=_params(("parallel",)),
    )(glu, glu, x, gain.reshape(1, d))


def glu_bwd(glu, dmix):
    t, d = dmix.shape
    tr = _tile(t, 512)

    def body(v_ref, g_ref, d_ref, o_ref):
        sg = jax.nn.sigmoid(g_ref[...])
        dm = d_ref[...]
        o_ref[:, :d] = (dm * sg).astype(o_ref.dtype)
        o_ref[:, d:] = (dm * v_ref[...] * sg * (1.0 - sg)).astype(o_ref.dtype)

    return pl.pallas_call(
        body, name="glu_bwd", grid=(t // tr,),
        in_specs=[pl.BlockSpec((tr, d), lambda i: (i, 0)), pl.BlockSpec((tr, d), lambda i: (i, 1)),
                  pl.BlockSpec((tr, d), lambda i: (i, 0))],
        out_specs=pl.BlockSpec((tr, 2 * d), lambda i: (i, 0)),
        out_shape=jax.ShapeDtypeStruct((t, 2 * d), BF16), compiler_params=_params(("parallel",)),
    )(glu, glu, dmix)


def _head_masks(shape):
    lane = lax.broadcasted_iota(jnp.int32, shape, 1)
    return lane < SB_HEAD_DIM


def _stack_heads(xf, is_a):
    return jnp.concatenate([jnp.where(is_a, xf, 0.0), jnp.where(is_a, 0.0, xf)], axis=0).astype(MXU_DTYPE)


def _diag_mask(qb, row0, rows):
    row = (lax.broadcasted_iota(jnp.int32, (rows, qb), 0) + row0) & (qb - 1)
    col = lax.broadcasted_iota(jnp.int32, (rows, qb), 1)
    return col < row


def _tri01(qb, pred):
    j = lax.broadcasted_iota(jnp.int32, (qb, qb), 0)
    s = lax.broadcasted_iota(jnp.int32, (qb, qb), 1)
    m = pred(j, s).astype(BF16)
    return jnp.concatenate([m, m], axis=0)


def _split_cat(x):
    hi = x.astype(BF16)
    lo = (x - hi.astype(F32)).astype(BF16)
    return jnp.concatenate([hi, lo], axis=1)


def sb_attn_fwd(proj, order, bsz, seq):
    qb = SB_BLOCK
    nq = seq // qb
    npair = SB_WIDTH // 128
    scale = SB_HEAD_DIM ** -0.5

    def body(q_ref, k_ref, v_ref, order_ref, o_ref, r_ref):
        qi = pl.program_id(2)
        is_a = _head_masks((qb, 128))
        q2 = _stack_heads(q_ref[...] * scale, is_a)
        diag = _diag_mask(qb, 0, 2 * qb)
        upper = _tri01(qb, lambda j, s: j > s)

        def blocks(kbs, acc, run, masked):
            sl = [pl.ds(pl.multiple_of(kb * qb, qb), qb) for kb in kbs]
            zs = [lax.dot_general(q2, k_ref[s, :].astype(MXU_DTYPE), _NT, preferred_element_type=F32) for s in sl]
            lks = [-jnp.maximum(z, 0.0) - jnp.log(1.0 + jnp.exp(-jnp.abs(z))) for z in zs]
            lbs = [lk + z for lk, z in zip(lks, zs)]
            if masked:
                lks = [jnp.where(diag, lk, 0.0) for lk in lks]
            cs = [lax.dot_general(_split_cat(lk), upper, _NN, preferred_element_type=F32) for lk in lks]
            for lk, lb, c, s in zip(lks, lbs, cs, sl):
                w = jnp.exp(lb + (run + c))
                if masked:
                    w = jnp.where(diag, w, 0.0)
                acc = acc + lax.dot_general(w.astype(MXU_DTYPE), v_ref[s, :].astype(MXU_DTYPE), _NN,
                                            preferred_element_type=F32)
                run = run + jnp.sum(lk, axis=1, keepdims=True)
            return acc, run

        carry = blocks([qi], jnp.zeros((2 * qb, 128), F32), jnp.zeros((2 * qb, 1), F32), True)
        carry = lax.cond(qi % 2 == 1, lambda c: blocks([qi - 1], c[0], c[1], False), lambda c: c, carry)
        top = qi - qi % 2
        acc, run = lax.fori_loop(
            0, qi // 2, lambda i, c: blocks([top - 1 - 2 * i, top - 2 - 2 * i], c[0], c[1], False), carry)
        o_ref[...] = jnp.where(is_a, acc[:qb], acc[qb:]).astype(o_ref.dtype)
        r_ref[...] = jnp.where(is_a, run[:qb], run[qb:])

    return pl.pallas_call(
        body, name="sb_attn_fwd", grid=(bsz, npair, nq),
        in_specs=[pl.BlockSpec((qb, 128), lambda b, p, i: (b * nq + i, p)),
                  pl.BlockSpec((seq, 128), lambda b, p, i: (b, npair + p)),
                  pl.BlockSpec((seq, 128), lambda b, p, i: (b, 2 * npair + p)),
                  pl.BlockSpec((1, 128), lambda b, p, i: (0, 0))],
        out_specs=[pl.BlockSpec((qb, 128), lambda b, p, i: (b * nq + i, p)),
                   pl.BlockSpec((qb, 128), lambda b, p, i: (b * nq + i, p))],
        out_shape=[jax.ShapeDtypeStruct((bsz * seq, SB_WIDTH), BF16),
                   jax.ShapeDtypeStruct((bsz * seq, SB_WIDTH), F32)],
        compiler_params=_params(("parallel", "parallel", "arbitrary")),
    )(proj, proj, proj, order)


def sb_attn_bwd(proj, rsum, dcat, bsz, seq):
    qb = SB_BLOCK
    nq = seq // qb
    npair = SB_WIDTH // 128
    scale = SB_HEAD_DIM ** -0.5

    def body(q_ref, k_ref, v_ref, r_ref, do_ref, dq_ref, dk_ref, dv_ref):
        qi = pl.program_id(2)

        @pl.when(qi == 0)
        def _():
            dk_ref[...] = jnp.zeros_like(dk_ref)
            dv_ref[...] = jnp.zeros_like(dv_ref)

        is_a = _head_masks((qb, 128))
        q2 = _stack_heads(q_ref[...] * scale, is_a)
        do2 = _stack_heads(do_ref[...].astype(F32), is_a)
        rf = r_ref[...]
        rtot = jnp.concatenate([rf[:, 0:1], rf[:, SB_HEAD_DIM:SB_HEAD_DIM + 1]], axis=0)
        diag = _diag_mask(qb, 0, 2 * qb)
        incl = _tri01(qb, lambda j, s: j <= s)
        strict = _tri01(qb, lambda j, s: j < s)

        def blocks(kbs, dq, pre, epre, masked):
            sl = [pl.ds(pl.multiple_of(kb * qb, qb), qb) for kb in kbs]
            ks = [k_ref[s, :].astype(MXU_DTYPE) for s in sl]
            vs = [v_ref[s, :].astype(MXU_DTYPE) for s in sl]
            zs = [lax.dot_general(q2, kblk, _NT, preferred_element_type=F32) for kblk in ks]
            dws = [lax.dot_general(do2, vblk, _NT, preferred_element_type=F32) for vblk in vs]
            lks = [-jnp.maximum(z, 0.0) - jnp.log(1.0 + jnp.exp(-jnp.abs(z))) for z in zs]
            lbs = [lk + z for lk, z in zip(lks, zs)]
            if masked:
                lks = [jnp.where(diag, lk, 0.0) for lk in lks]
            ps = [lax.dot_general(_split_cat(lk), incl, _NN, preferred_element_type=F32) for lk in lks]
            ws, es = [], []
            for lk, lb, p, dw in zip(lks, lbs, ps, dws):
                w = jnp.exp(lb + (rtot - (pre + p)))
                if masked:
                    w = jnp.where(diag, w, 0.0)
                ws.append(w)
                es.append(dw * w)
                pre = pre + jnp.sum(lk, axis=1, keepdims=True)
            cs = [lax.dot_general(_split_cat(e), strict, _NN, preferred_element_type=F32) for e in es]
            for e, lb, c, w, kblk, s in zip(es, lbs, cs, ws, ks, sl):
                dz = e - jnp.exp(lb) * (e + (epre + c))
                if masked:
                    dz = jnp.where(diag, dz, 0.0)
                dz = dz.astype(MXU_DTYPE)
                dq = dq + lax.dot_general(dz, kblk, _NN, preferred_element_type=F32)
                dk_ref[s, :] += lax.dot_general(dz, q2, _TN, preferred_element_type=F32)
                dv_ref[s, :] += lax.dot_general(w.astype(MXU_DTYPE), do2, _TN, preferred_element_type=F32)
                epre = epre + jnp.sum(e, axis=1, keepdims=True)
            return dq, pre, epre

        zc = jnp.zeros((2 * qb, 1), F32)
        carry = lax.fori_loop(0, qi // 2, lambda i, c: blocks([2 * i, 2 * i + 1], c[0], c[1], c[2], False),
                              (jnp.zeros((2 * qb, 128), F32), zc, zc))
        carry = lax.cond(qi % 2 == 1, lambda c: blocks([qi - 1], c[0], c[1], c[2], False), lambda c: c, carry)
        dq = blocks([qi], carry[0], carry[1], carry[2], True)[0]
        dq_ref[...] = jnp.where(is_a, dq[:qb], dq[qb:]) * scale

    full = jax.ShapeDtypeStruct((bsz * seq, SB_WIDTH), F32)
    qspec = pl.BlockSpec((qb, 128), lambda b, p, i: (b * nq + i, p))
    return pl.pallas_call(
        body, name="sb_attn_bwd", grid=(bsz, npair, nq),
        in_specs=[qspec,
                  pl.BlockSpec((seq, 128), lambda b, p, i: (b, npair + p)),
                  pl.BlockSpec((seq, 128), lambda b, p, i: (b, 2 * npair + p)),
                  qspec, qspec],
        out_specs=[qspec, pl.BlockSpec((seq, 128), lambda b, p, i: (b, p)),
                   pl.BlockSpec((seq, 128), lambda b, p, i: (b, p))],
        out_shape=[full, full, full],
        compiler_params=_params(("parallel", "parallel", "arbitrary")),
    )(proj, proj, proj, rsum, dcat)


def _window_sums(x, forward):
    n = x.shape[0]
    out = []
    s = x
    for sh in (1, 2, 4, 8):
        s = s + pltpu.roll(s, (n - sh) if forward else sh, 0)
        out.append(s)
    return out


def _pool_counts(tc, c, w):
    t = lax.broadcasted_iota(jnp.int32, (tc, 1), 0) + c * tc
    return jnp.minimum(t + 1, w).astype(F32)


def pool_fwd(proj, pool_w, pool_scale, bsz, seq):
    tc = _tile(seq, 512)
    nc = seq // tc
    hb = tc // POOL_HALO
    ucol = 3

    def body(u_ref, prev_ref, w_ref, s_ref, o_ref):
        c = pl.program_id(1)
        prev = jnp.where(c > 0, prev_ref[...], 0.0)
        x = jnp.concatenate([prev, u_ref[...]], axis=0)
        sums = _window_sums(x, forward=False)
        for g, win in enumerate(POOL_WINDOWS):
            ls = slice(g * POOL_GROUP, (g + 1) * POOL_GROUP)
            pooled = sums[g][POOL_HALO:, ls] / _pool_counts(tc, c, win) - x[POOL_HALO:, ls]
            y = _dot(pooled, w_ref[g], _NN)
            o_ref[:, ls] = (y * s_ref[:, ls]).astype(o_ref.dtype)

    return pl.pallas_call(
        body, name="pool_fwd", grid=(bsz, nc),
        in_specs=[pl.BlockSpec((tc, SB_WIDTH), lambda b, c: (b * nc + c, ucol)),
                  pl.BlockSpec((POOL_HALO, SB_WIDTH), lambda b, c: (jnp.maximum((b * nc + c) * hb - 1, 0), ucol)),
                  pl.BlockSpec((4, POOL_GROUP, POOL_GROUP), lambda b, c: (0, 0, 0)),
                  pl.BlockSpec((1, SB_WIDTH), lambda b, c: (0, 0))],
        out_specs=pl.BlockSpec((tc, SB_WIDTH), lambda b, c: (b * nc + c, 0)),
        out_shape=jax.ShapeDtypeStruct((bsz * seq, SB_WIDTH), BF16),
        compiler_params=_params(("parallel", "parallel")),
    )(proj, proj, pool_w, pool_scale)


def pool_bwd(proj, pool_w, pool_scale, dcat, bsz, seq):
    tc = _tile(seq, 512)
    nc = seq // tc
    hb = tc // POOL_HALO
    nblk = bsz * seq // POOL_HALO
    ucol = 3

    def body(u_ref, prev_ref, dy_ref, nxt_ref, w_ref, s_ref, du_ref, dw_ref, ds_ref):
        b, c = pl.program_id(0), pl.program_id(1)

        @pl.when((b == 0) & (c == 0))
        def _():
            dw_ref[...] = jnp.zeros_like(dw_ref)
            ds_ref[...] = jnp.zeros_like(ds_ref)

        prev = jnp.where(c > 0, prev_ref[...], 0.0)
        x = jnp.concatenate([prev, u_ref[...]], axis=0)
        sums = _window_sums(x, forward=False)
        nxt = jnp.where(c < nc - 1, nxt_ref[...].astype(F32), 0.0)
        dy = jnp.concatenate([dy_ref[...].astype(F32), nxt], axis=0)
        tq = lax.broadcasted_iota(jnp.int32, (tc + POOL_HALO, 1), 0) + c * tc
        for g, win in enumerate(POOL_WINDOWS):
            ls = slice(g * POOL_GROUP, (g + 1) * POOL_GROUP)
            pooled = sums[g][POOL_HALO:, ls] / _pool_counts(tc, c, win) - x[POOL_HALO:, ls]
            y = _dot(pooled, w_ref[g], _NN)
            ds_ref[:, ls] += jnp.sum(dy[:tc, ls] * y, axis=0, keepdims=True)
            dz = dy[:, ls] * s_ref[:, ls]
            dw_ref[g] += _dot(pooled, dz[:tc], _TN)
            dpool = _dot(dz, w_ref[g], _NT)
            dmean = dpool / jnp.minimum(tq + 1, win).astype(F32)
            fsum = _window_sums(dmean, forward=True)[g]
            du_ref[:, ls] = fsum[:tc] - dpool[:tc]

    return pl.pallas_call(
        body, name="pool_bwd", grid=(bsz, nc),
        in_specs=[pl.BlockSpec((tc, SB_WIDTH), lambda b, c: (b * nc + c, ucol)),
                  pl.BlockSpec((POOL_HALO, SB_WIDTH), lambda b, c: (jnp.maximum((b * nc + c) * hb - 1, 0), ucol)),
                  pl.BlockSpec((tc, SB_WIDTH), lambda b, c: (b * nc + c, 1)),
                  pl.BlockSpec((POOL_HALO, SB_WIDTH), lambda b, c: (jnp.minimum((b * nc + c + 1) * hb, nblk - 1), 1)),
                  pl.BlockSpec((4, POOL_GROUP, POOL_GROUP), lambda b, c: (0, 0, 0)),
                  pl.BlockSpec((1, SB_WIDTH), lambda b, c: (0, 0))],
        out_specs=[pl.BlockSpec((tc, SB_WIDTH), lambda b, c: (b * nc + c, 0)),
                   pl.BlockSpec((4, POOL_GROUP, POOL_GROUP), lambda b, c: (0, 0, 0)),
                   pl.BlockSpec((1, SB_WIDTH), lambda b, c: (0, 0))],
        out_shape=[jax.ShapeDtypeStruct((bsz * seq, SB_WIDTH), F32),
                   jax.ShapeDtypeStruct((4, POOL_GROUP, POOL_GROUP), F32),
                   jax.ShapeDtypeStruct((1, SB_WIDTH), F32)],
        compiler_params=_params(("arbitrary", "arbitrary")),
    )(proj, proj, dcat, dcat, pool_w, pool_scale)


def _lbar(lam_re, lam_im, log_dt):
    dt = jnp.exp(log_dt)
    mag = jnp.exp(lam_re * dt)
    ang = lam_im * dt
    return mag * jnp.cos(ang), mag * jnp.sin(ang)


def _bbar(lam_re, lam_im, log_dt, b_re, b_im):
    lb_re, lb_im = _lbar(lam_re, lam_im, log_dt)
    n_re = lb_re - 1.0
    den = lam_re * lam_re + lam_im * lam_im
    coef_re = (n_re * lam_re + lb_im * lam_im) / den
    coef_im = (lb_im * lam_re - n_re * lam_im) / den
    return coef_re * b_re - coef_im * b_im, coef_re * b_im + coef_im * b_re


def _expand01():
    p = lax.broadcasted_iota(jnp.int32, (64, 1024), 0)
    q = lax.broadcasted_iota(jnp.int32, (64, 1024), 1)
    return (lax.shift_right_logical(q, 4) == p).astype(BF16)


def ssm_prep(lam_re, lam_im, log_dt, b_re2, b_im2):
    def body(lr_ref, li_ref, dt_ref, br_ref, bi_ref, ar_ref, ai_ref, bbr_ref, bbi_ref):
        e = _expand01()
        lr, li, dt = lr_ref[...], li_ref[...], dt_ref[...]
        ar_ref[...], ai_ref[...] = _lbar(lr, li, dt)
        bbr_ref[...], bbi_ref[...] = _bbar(_dot_exact01(lr, e), _dot_exact01(li, e), dt, br_ref[...], bi_ref[...])

    s64 = jax.ShapeDtypeStruct((64, 64), F32)
    s1k = jax.ShapeDtypeStruct((64, 1024), F32)
    return pl.pallas_call(body, name="ssm_prep", out_shape=[s64, s64, s1k, s1k], compiler_params=_params())(
        lam_re, lam_im, log_dt, b_re2, b_im2)


def ssm_prep_bwd(lam_re, lam_im, log_dt, b_re2, b_im2, da_re, da_im, dbb_re, dbb_im):
    def body(lr_ref, li_ref, dt_ref, br_ref, bi_ref, dar_ref, dai_ref, dbr_ref, dbi_ref,
             olr_ref, oli_ref, odt_ref, obr_ref, obi_ref):
        e = _expand01()
        lr, li, dt = lr_ref[...], li_ref[...], dt_ref[...]
        _, vjp_a = jax.vjp(_lbar, lr, li, dt)
        g_lr, g_li, g_dt = vjp_a((dar_ref[...], dai_ref[...]))
        _, vjp_b = jax.vjp(_bbar, _dot_exact01(lr, e), _dot_exact01(li, e), dt, br_ref[...], bi_ref[...])
        x_lr, x_li, x_dt, g_br, g_bi = vjp_b((dbr_ref[...], dbi_ref[...]))
        olr_ref[...] = g_lr + _dot_exact01(x_lr, e, _NT)
        oli_ref[...] = g_li + _dot_exact01(x_li, e, _NT)
        odt_ref[...] = g_dt + x_dt
        obr_ref[...] = g_br
        obi_ref[...] = g_bi

    s64 = jax.ShapeDtypeStruct((64, 64), F32)
    s1k = jax.ShapeDtypeStruct((64, 1024), F32)
    return pl.pallas_call(body, name="ssm_prep_bwd",
                          out_shape=[s64, s64, jax.ShapeDtypeStruct((64, 1), F32), s1k, s1k],
                          compiler_params=_params())(
        lam_re, lam_im, log_dt, b_re2, b_im2, da_re, da_im, dbb_re, dbb_im)


def _gelu(y):
    c = math.sqrt(2.0 / math.pi)
    return 0.5 * y * (1.0 + jnp.tanh(c * (y + 0.044715 * y * y * y)))


def _gelu_grad(y):
    c = math.sqrt(2.0 / math.pi)
    th = jnp.tanh(c * (y + 0.044715 * y * y * y))
    return 0.5 * (1.0 + th) + 0.5 * y * (1.0 - th * th) * c * (1.0 + 3.0 * 0.044715 * y * y)


def _cmul(ar, ai, br, bi):
    return ar * br - ai * bi, ar * bi + ai * br


def _scan_tables(ar, ai, reverse, tabs):
    row = lax.broadcasted_iota(jnp.int32, (8, SSM_STATES), 0)
    a1 = (ar, ai)
    a2 = _cmul(*a1, *a1)
    a4 = _cmul(*a2, *a2)
    powers = [a1, a2, _cmul(*a2, *a1), a4]
    powers += [_cmul(*a4, *p) for p in powers]
    for k, (val, sh) in enumerate(((a1, 1), (a2, 2), (a4, 4))):
        keep = (row < 8 - sh) if reverse else (row >= sh)
        tabs[2 * k][...] = jnp.where(keep, val[0], 0.0)
        tabs[2 * k + 1][...] = jnp.where(keep, val[1], 0.0)
    pr = jnp.zeros((8, SSM_STATES), F32)
    pi = jnp.zeros((8, SSM_STATES), F32)
    for r in range(8):
        val = powers[7 - r] if reverse else powers[r]
        pr = jnp.where(row == r, val[0], pr)
        pi = jnp.where(row == r, val[1], pi)
    tabs[6][...] = pr
    tabs[7][...] = pi


def _scan8(xr, xi, tabs, ls, cr, ci, reverse):
    for k, sh in enumerate((1, 2, 4)):
        amt = (8 - sh) if reverse else sh
        sr, si = pltpu.roll(xr, amt, 0), pltpu.roll(xi, amt, 0)
        lr, li = tabs[2 * k][:, ls], tabs[2 * k + 1][:, ls]
        xr, xi = xr + lr * sr - li * si, xi + lr * si + li * sr
    pr, pi = tabs[6][:, ls], tabs[7][:, ls]
    return xr + pr * cr - pi * ci, xi + pr * ci + pi * cr


def _block8(b):
    return pl.ds(pl.multiple_of(b * 8, 8), 8)


def ssm_fwd(u, wt, ct, a_re, a_im, dskip, bsz, seq):
    tc = _tile(seq, 256)
    nc = seq // tc
    ns = SSM_TILE_STATES
    nl = SSM_STATES // SSM_LANES

    def body(u_ref, wt_ref, ct_ref, ar_ref, ai_ref, d_ref, y_ref, gl_ref, hr_ref, hi_ref, sr_ref, si_ref, *tabs):
        b, c = pl.program_id(0), pl.program_id(1)

        @pl.when((b == 0) & (c == 0))
        def _():
            _scan_tables(ar_ref[...], ai_ref[...], False, tabs)

        @pl.when(c == 0)
        def _():
            sr_ref[...] = jnp.zeros_like(sr_ref)
            si_ref[...] = jnp.zeros_like(si_ref)

        uf = u_ref[...]
        for i in range(SSM_TILES):
            bu = _dot(uf[:, i * 128:(i + 1) * 128], wt_ref[i], _NN)
            hr_ref[:, i * ns:(i + 1) * ns] = bu[:, :ns]
            hi_ref[:, i * ns:(i + 1) * ns] = bu[:, ns:]

        def step(blk, carry):
            rows = _block8(blk)
            new = []
            for j in range(nl):
                ls = slice(j * SSM_LANES, (j + 1) * SSM_LANES)
                xr, xi = _scan8(hr_ref[rows, ls], hi_ref[rows, ls], tabs, ls, carry[2 * j], carry[2 * j + 1], False)
                hr_ref[rows, ls] = xr
                hi_ref[rows, ls] = xi
                new += [xr[7:8], xi[7:8]]
            return tuple(new)

        init = []
        for j in range(nl):
            ls = slice(j * SSM_LANES, (j + 1) * SSM_LANES)
            init += [sr_ref[:, ls], si_ref[:, ls]]
        last = lax.fori_loop(0, tc // 8, step, tuple(init), unroll=2)
        for j in range(nl):
            ls = slice(j * SSM_LANES, (j + 1) * SSM_LANES)
            sr_ref[:, ls] = last[2 * j]
            si_ref[:, ls] = last[2 * j + 1]
        for i in range(SSM_TILES):
            hcat = jnp.concatenate([hr_ref[:, i * ns:(i + 1) * ns], hi_ref[:, i * ns:(i + 1) * ns]], axis=1)
            ls = slice(i * 128, (i + 1) * 128)
            y = _dot(hcat, ct_ref[i], _NN) + d_ref[:, ls] * uf[:, ls]
            y_ref[:, ls] = y
            gl_ref[:, ls] = _gelu(y).astype(gl_ref.dtype)

    t = bsz * seq
    row = pl.BlockSpec((tc, D_MODEL), lambda b, c: (b * nc + c, 0))
    st = pl.BlockSpec((tc, SSM_STATES), lambda b, c: (b * nc + c, 0))
    diag = pl.BlockSpec((1, SSM_STATES), lambda b, c: (0, 0))
    return pl.pallas_call(
        body, name="ssm_fwd", grid=(bsz, nc),
        in_specs=[row, pl.BlockSpec((SSM_TILES, 128, 2 * ns), lambda b, c: (0, 0, 0)),
                  pl.BlockSpec((SSM_TILES, 2 * ns, 128), lambda b, c: (0, 0, 0)), diag, diag,
                  pl.BlockSpec((1, D_MODEL), lambda b, c: (0, 0))],
        out_specs=[row, row, st, st],
        out_shape=[jax.ShapeDtypeStruct((t, D_MODEL), F32), jax.ShapeDtypeStruct((t, D_MODEL), BF16),
                   jax.ShapeDtypeStruct((t, SSM_STATES), F32), jax.ShapeDtypeStruct((t, SSM_STATES), F32)],
        scratch_shapes=[pltpu.VMEM((1, SSM_STATES), F32)] * 2 + [pltpu.VMEM((8, SSM_STATES), F32)] * 8,
        compiler_params=_params(("arbitrary", "arbitrary")),
    )(u, wt, ct, a_re, a_im, dskip)


def ssm_bwd(dgl, y, u, h_re, h_im, wt, ct, a_re, a_im, dskip, bsz, seq):
    tc = _tile(seq, 256)
    nc = seq // tc
    nb = tc // 8
    ns = SSM_TILE_STATES
    nl = SSM_STATES // SSM_LANES

    def body(dgl_ref, y_ref, u_ref, hr_ref, hi_ref, pr_ref, pi_ref, wt_ref, ct_ref, ar_ref, ai_ref, d_ref,
             du_ref, dwt_ref, dct_ref, dd_ref, dar_ref, dai_ref, gr_ref, gi_ref, sr_ref, si_ref, ar8_ref, ai8_ref,
             *tabs):
        b, c = pl.program_id(0), pl.program_id(1)

        @pl.when((b == 0) & (c == 0))
        def _():
            for r in (dwt_ref, dct_ref, dd_ref, ar8_ref, ai8_ref):
                r[...] = jnp.zeros_like(r)
            _scan_tables(ar_ref[...], -ai_ref[...], True, tabs)

        @pl.when(c == 0)
        def _():
            sr_ref[...] = jnp.zeros_like(sr_ref)
            si_ref[...] = jnp.zeros_like(si_ref)

        uf = u_ref[...]
        dy = dgl_ref[...].astype(F32) * _gelu_grad(y_ref[...])
        dd_ref[...] += jnp.sum(dy * uf, axis=0, keepdims=True)
        for i in range(SSM_TILES):
            dyi = dy[:, i * 128:(i + 1) * 128]
            dh = _dot(dyi, ct_ref[i], _NT)
            gr_ref[:, i * ns:(i + 1) * ns] = dh[:, :ns]
            gi_ref[:, i * ns:(i + 1) * ns] = dh[:, ns:]
            hcat = jnp.concatenate([hr_ref[:, i * ns:(i + 1) * ns], hi_ref[:, i * ns:(i + 1) * ns]], axis=1)
            dct_ref[i] += _dot(hcat, dyi, _TN)
        row0 = lax.broadcasted_iota(jnp.int32, (8, SSM_LANES), 0) == 0

        def block(blk, carry, before):
            rows = _block8(blk)
            new = []
            for j in range(nl):
                ls = slice(j * SSM_LANES, (j + 1) * SSM_LANES)
                gr, gi = _scan8(gr_ref[rows, ls], gi_ref[rows, ls], tabs, ls, carry[2 * j], carry[2 * j + 1], True)
                gr_ref[rows, ls] = gr
                gi_ref[rows, ls] = gi
                bpr, bpi = before(j)
                hpr = jnp.where(row0, bpr, pltpu.roll(hr_ref[rows, ls], 1, 0))
                hpi = jnp.where(row0, bpi, pltpu.roll(hi_ref[rows, ls], 1, 0))
                ar8_ref[:, ls] += gr * hpr + gi * hpi
                ai8_ref[:, ls] += gi * hpr - gr * hpi
                new += [gr[0:1], gi[0:1]]
            return tuple(new)

        def step(jj, carry):
            blk = nb - 1 - jj
            prev_rows = _block8(blk - 1)

            def before(j):
                ls = slice(j * SSM_LANES, (j + 1) * SSM_LANES)
                return hr_ref[prev_rows, ls][7:8], hi_ref[prev_rows, ls][7:8]

            return block(blk, carry, before)

        init = []
        for j in range(nl):
            ls = slice(j * SSM_LANES, (j + 1) * SSM_LANES)
            init += [sr_ref[:, ls], si_ref[:, ls]]
        carry = lax.fori_loop(0, nb - 1, step, tuple(init))
        first = c == nc - 1

        def before_chunk(j):
            ls = slice(j * SSM_LANES, (j + 1) * SSM_LANES)
            return (jnp.where(first, 0.0, pr_ref[:, ls][7:8]), jnp.where(first, 0.0, pi_ref[:, ls][7:8]))

        last = block(0, carry, before_chunk)
        for j in range(nl):
            ls = slice(j * SSM_LANES, (j + 1) * SSM_LANES)
            sr_ref[:, ls] = last[2 * j]
            si_ref[:, ls] = last[2 * j + 1]
        for i in range(SSM_TILES):
            ls = slice(i * 128, (i + 1) * 128)
            gcat = jnp.concatenate([gr_ref[:, i * ns:(i + 1) * ns], gi_ref[:, i * ns:(i + 1) * ns]], axis=1)
            du_ref[:, ls] = (_dot(gcat, wt_ref[i], _NT) + d_ref[:, ls] * dy[:, ls]).astype(du_ref.dtype)
            dwt_ref[i] += _dot(uf[:, ls], gcat, _TN)

        @pl.when((b == bsz - 1) & (c == nc - 1))
        def _():
            dar_ref[...] = jnp.sum(ar8_ref[...], axis=0, keepdims=True)
            dai_ref[...] = jnp.sum(ai8_ref[...], axis=0, keepdims=True)

    t = bsz * seq
    rev = lambda b, c: (b * nc + (nc - 1 - c), 0)
    row = pl.BlockSpec((tc, D_MODEL), rev)
    st = pl.BlockSpec((tc, SSM_STATES), rev)
    prev = pl.BlockSpec((8, SSM_STATES), lambda b, c: (jnp.maximum((b * nc + (nc - 1 - c)) * nb - 1, 0), 0))
    diag = pl.BlockSpec((1, SSM_STATES), lambda b, c: (0, 0))
    wts = pl.BlockSpec((SSM_TILES, 128, 2 * ns), lambda b, c: (0, 0, 0))
    cts = pl.BlockSpec((SSM_TILES, 2 * ns, 128), lambda b, c: (0, 0, 0))
    vec = pl.BlockSpec((1, D_MODEL), lambda b, c: (0, 0))
    return pl.pallas_call(
        body, name="ssm_bwd", grid=(bsz, nc),
        in_specs=[row, row, row, st, st, prev, prev, wts, cts, diag, diag, vec],
        out_specs=[row, wts, cts, vec, diag, diag],
        out_shape=[jax.ShapeDtypeStruct((t, D_MODEL), BF16),
                   jax.ShapeDtypeStruct((SSM_TILES, 128, 2 * ns), F32),
                   jax.ShapeDtypeStruct((SSM_TILES, 2 * ns, 128), F32),
                   jax.ShapeDtypeStruct((1, D_MODEL), F32),
                   jax.ShapeDtypeStruct((1, SSM_STATES), F32), jax.ShapeDtypeStruct((1, SSM_STATES), F32)],
        scratch_shapes=[pltpu.VMEM((tc, SSM_STATES), F32)] * 2 + [pltpu.VMEM((1, SSM_STATES), F32)] * 2
                       + [pltpu.VMEM((8, SSM_STATES), F32)] * 10,
        compiler_params=_params(("arbitrary", "arbitrary")),
    )(dgl, y, u, h_re, h_im, h_re, h_im, wt, ct, a_re, a_im, dskip)


def _ssm_in_weights(bb_re2, bb_im2):
    eye = jnp.eye(8, dtype=F32)[None, :, None, :, None]

    def one(bb):
        t = bb.reshape(8, 8, 64, 16).transpose(0, 1, 3, 2)
        return (t[:, :, :, None, :] * eye).reshape(8, 128, 512)

    return jnp.concatenate([one(bb_re2), one(bb_im2)], axis=-1).astype(MXU_DTYPE)


def _ssm_in_weights_bwd(dwt):
    eye = jnp.eye(8, dtype=F32)[None, :, None, :, None]

    def one(d):
        t = (d.reshape(8, 8, 16, 8, 64) * eye).sum(axis=3)
        return t.transpose(0, 1, 3, 2).reshape(64, 1024)

    return one(dwt[..., :512]), one(dwt[..., 512:])


def _ssm_out_weights(c_re, c_im):
    eye = jnp.eye(8, dtype=F32)[None, :, None, :, None]

    def one(cc):
        t = cc.reshape(8, 8, 16, 64).transpose(0, 1, 3, 2)
        return (t[:, :, :, None, :] * eye).reshape(8, 512, 128)

    return jnp.concatenate([one(c_re), -one(c_im)], axis=1).astype(MXU_DTYPE)


def _ssm_out_weights_bwd(dct):
    eye = jnp.eye(8, dtype=F32)[None, :, None, :, None]

    def one(d):
        t = (d.reshape(8, 8, 64, 8, 16) * eye).sum(axis=3)
        return t.transpose(0, 1, 3, 2).reshape(64, 16, 64)

    return one(dct[:, :512]), -one(dct[:, 512:])


def _softmax(s):
    m = jnp.max(s, axis=-1, keepdims=True)
    e = jnp.exp(s - m)
    return e / jnp.sum(e, axis=-1, keepdims=True)


def xattn_fwd(q, kv, bsz, seq):
    tq = _tile(seq, 512)
    nq = seq // tq
    scale = XA_HEAD_DIM ** -0.5

    def body(q_ref, k_ref, v_ref, o_ref):
        s = lax.dot_general(q_ref[...], k_ref[...], _NT, preferred_element_type=F32) * scale
        p = _softmax(s)
        o_ref[...] = _dot(p, v_ref[...], _NN).astype(o_ref.dtype)

    qs = pl.BlockSpec((tq, XA_HEAD_DIM), lambda b, h, i: (b * nq + i, h))
    return pl.pallas_call(
        body, name="xattn_fwd", grid=(bsz, XA_HEADS, nq),
        in_specs=[qs, pl.BlockSpec((MEM_LEN, XA_HEAD_DIM), lambda b, h, i: (b, h)),
                  pl.BlockSpec((MEM_LEN, XA_HEAD_DIM), lambda b, h, i: (b, XA_HEADS + h))],
        out_specs=qs, out_shape=jax.ShapeDtypeStruct((bsz * seq, D_MODEL), BF16),
        compiler_params=_params(("parallel", "parallel", "parallel")),
    )(q, kv, kv)


def xattn_bwd(q, kv, do, bsz, seq):
    tq = _tile(seq, 512)
    nq = seq // tq
    scale = XA_HEAD_DIM ** -0.5

    def body(q_ref, k_ref, v_ref, do_ref, dq_ref, dk_ref, dv_ref):
        @pl.when(pl.program_id(2) == 0)
        def _():
            dk_ref[...] = jnp.zeros_like(dk_ref)
            dv_ref[...] = jnp.zeros_like(dv_ref)

        qv, kk, vv, dov = q_ref[...], k_ref[...], v_ref[...], do_ref[...]
        s = lax.dot_general(qv, kk, _NT, preferred_element_type=F32) * scale
        p = _softmax(s)
        dp = lax.dot_general(dov, vv, _NT, preferred_element_type=F32)
        ds = (p * (dp - jnp.sum(dp * p, axis=-1, keepdims=True)) * scale).astype(MXU_DTYPE)
        dq_ref[...] = lax.dot_general(ds, kk, _NN, preferred_element_type=F32).astype(dq_ref.dtype)
        dk_ref[...] += lax.dot_general(ds, qv, _TN, preferred_element_type=F32)
        dv_ref[...] += lax.dot_general(p.astype(MXU_DTYPE), dov, _TN, preferred_element_type=F32)

    qs = pl.BlockSpec((tq, XA_HEAD_DIM), lambda b, h, i: (b * nq + i, h))
    ks = pl.BlockSpec((MEM_LEN, XA_HEAD_DIM), lambda b, h, i: (b, h))
    vs = pl.BlockSpec((MEM_LEN, XA_HEAD_DIM), lambda b, h, i: (b, XA_HEADS + h))
    dkv = jax.ShapeDtypeStruct((bsz * MEM_LEN, D_MODEL), F32)
    dq, dk, dv = pl.pallas_call(
        body, name="xattn_bwd", grid=(bsz, XA_HEADS, nq),
        in_specs=[qs, ks, vs, qs], out_specs=[qs, ks, ks],
        out_shape=[jax.ShapeDtypeStruct((bsz * seq, D_MODEL), BF16), dkv, dkv],
        compiler_params=_params(("parallel", "parallel", "arbitrary")),
    )(q, kv, kv, do)
    return dq, dk, dv


CONV_HALO = 16


def _shifts_down(x, prev):
    h = prev.shape[0]
    ext = jnp.concatenate([prev, x], axis=0)
    return pltpu.roll(ext, 1, 0)[h:], pltpu.roll(ext, 2, 0)[h:]


def _shifts_up(x, nxt):
    rows = x.shape[0]
    n = rows + nxt.shape[0]
    ext = jnp.concatenate([x, nxt], axis=0)
    return pltpu.roll(ext, n - 1, 0)[:rows], pltpu.roll(ext, n - 2, 0)[:rows]


def _conv_taps(u, u1, u2, w, b):
    return b + w[2:3] * u + w[1:2] * u1 + w[0:1] * u2


def conv_fwd(up, cw, cb, bsz, seq):
    tc = _tile(seq, 512)
    nc = seq // tc
    hb = tc // CONV_HALO
    half = N_DEV // 2

    def body(uv_ref, ug_ref, pv_ref, pg_ref, wv_ref, wg_ref, bv_ref, bg_ref, o_ref):
        c = pl.program_id(2)
        pv = jnp.where(c > 0, pv_ref[...].astype(F32), 0.0)
        pg = jnp.where(c > 0, pg_ref[...].astype(F32), 0.0)
        uv, ug = uv_ref[...].astype(F32), ug_ref[...].astype(F32)
        val = _conv_taps(uv, *_shifts_down(uv, pv), wv_ref[...], bv_ref[...])
        gate = _conv_taps(ug, *_shifts_down(ug, pg), wg_ref[...], bg_ref[...])
        o_ref[...] = (gate * jax.nn.sigmoid(gate) * val).astype(o_ref.dtype)

    def cur(off):
        return pl.BlockSpec((None, tc, FF_SHARD), lambda b, j, c: (j + off, b * nc + c, 0))

    def prv(off):
        return pl.BlockSpec((None, CONV_HALO, FF_SHARD), lambda b, j, c: (j + off, jnp.maximum((b * nc + c) * hb - 1, 0), 0))

    def par(rows, off):
        return pl.BlockSpec((None, rows, FF_SHARD), lambda b, j, c: (j + off, 0, 0))

    return pl.pallas_call(
        body, name="conv_fwd", grid=(bsz, half, nc),
        in_specs=[cur(0), cur(half), prv(0), prv(half), par(3, 0), par(3, half), par(1, 0), par(1, half)],
        out_specs=cur(0), out_shape=jax.ShapeDtypeStruct((half, bsz * seq, FF_SHARD), BF16),
        compiler_params=_params(("parallel", "parallel", "parallel")),
    )(up, up, up, up, cw, cw, cb, cb)


def conv_bwd_taps(up, cw, cb, dact, bsz, seq):
    tc = _tile(seq, 512)
    nc = seq // tc
    hb = tc // CONV_HALO
    half = N_DEV // 2

    def body(uv_ref, ug_ref, pv_ref, pg_ref, wv_ref, wg_ref, bv_ref, bg_ref, da_ref,
             dc_ref, dwv_ref, dwg_ref, dbv_ref, dbg_ref):
        b, c = pl.program_id(1), pl.program_id(2)

        @pl.when((b == 0) & (c == 0))
        def _():
            for r in (dwv_ref, dwg_ref, dbv_ref, dbg_ref):
                r[...] = jnp.zeros_like(r)

        pv = jnp.where(c > 0, pv_ref[...].astype(F32), 0.0)
        pg = jnp.where(c > 0, pg_ref[...].astype(F32), 0.0)
        uv, ug = uv_ref[...].astype(F32), ug_ref[...].astype(F32)
        uv1, uv2 = _shifts_down(uv, pv)
        ug1, ug2 = _shifts_down(ug, pg)
        val = _conv_taps(uv, uv1, uv2, wv_ref[...], bv_ref[...])
        gate = _conv_taps(ug, ug1, ug2, wg_ref[...], bg_ref[...])
        sg = jax.nn.sigmoid(gate)
        da = da_ref[...].astype(F32)
        dsilu = da * sg
        dval = dsilu * gate
        dgate = dsilu * val * (1.0 + gate * (1.0 - sg))
        dc_ref[0] = dval.astype(dc_ref.dtype)
        dc_ref[1] = dgate.astype(dc_ref.dtype)
        for dcv, taps, dw_ref, db_ref in ((dval, (uv2, uv1, uv), dwv_ref, dbv_ref),
                                          (dgate, (ug2, ug1, ug), dwg_ref, dbg_ref)):
            db_ref[...] += jnp.sum(dcv, axis=0, keepdims=True)
            for k, u_k in enumerate(taps):
                dw_ref[k:k + 1, :] += jnp.sum(dcv * u_k, axis=0, keepdims=True)

    def cur(off):
        return pl.BlockSpec((None, tc, FF_SHARD), lambda j, b, c: (j + off, b * nc + c, 0))

    def prv(off):
        return pl.BlockSpec((None, CONV_HALO, FF_SHARD), lambda j, b, c: (j + off, jnp.maximum((b * nc + c) * hb - 1, 0), 0))

    def par(rows, off):
        return pl.BlockSpec((None, rows, FF_SHARD), lambda j, b, c: (j + off, 0, 0))

    t = bsz * seq
    hs = jax.ShapeDtypeStruct((2, half, t, FF_SHARD), BF16)
    ws = jax.ShapeDtypeStruct((half, 3, FF_SHARD), F32)
    bs = jax.ShapeDtypeStruct((half, 1, FF_SHARD), F32)
    dc, dwv, dwg, dbv, dbg = pl.pallas_call(
        body, name="conv_bwd_taps", grid=(half, bsz, nc),
        in_specs=[cur(0), cur(half), prv(0), prv(half), par(3, 0), par(3, half), par(1, 0), par(1, half), cur(0)],
        out_specs=[pl.BlockSpec((2, None, tc, FF_SHARD), lambda j, b, c: (0, j, b * nc + c, 0)),
                   par(3, 0), par(3, 0), par(1, 0), par(1, 0)],
        out_shape=[hs, ws, ws, bs, bs],
        compiler_params=_params(("parallel", "arbitrary", "arbitrary")),
    )(up, up, up, up, cw, cw, cb, cb, dact)
    return (dc.reshape(N_DEV, t, FF_SHARD), jnp.concatenate([dwv, dwg], axis=0),
            jnp.concatenate([dbv, dbg], axis=0))


def conv_bwd_input(dconv, cw, bsz, seq):
    tc = _tile(seq, 1024)
    nc = seq // tc
    hb = tc // CONV_HALO
    nblk = bsz * seq // CONV_HALO

    def body(d_ref, n_ref, w_ref, o_ref):
        c = pl.program_id(2)
        nxt = jnp.where(c < nc - 1, n_ref[...].astype(F32), 0.0)
        d = d_ref[...].astype(F32)
        d1, d2 = _shifts_up(d, nxt)
        w = w_ref[...]
        o_ref[...] = (w[2:3] * d + w[1:2] * d1 + w[0:1] * d2).astype(o_ref.dtype)

    cur = pl.BlockSpec((None, tc, FF_SHARD), lambda j, b, c: (j, b * nc + c, 0))
    return pl.pallas_call(
        body, name="conv_bwd_input", grid=(N_DEV, bsz, nc),
        in_specs=[cur, pl.BlockSpec((None, CONV_HALO, FF_SHARD),
                                    lambda j, b, c: (j, jnp.minimum((b * nc + c + 1) * hb, nblk - 1), 0)),
                  pl.BlockSpec((None, 3, FF_SHARD), lambda j, b, c: (j, 0, 0))],
        out_specs=cur, out_shape=jax.ShapeDtypeStruct(dconv.shape, BF16),
        compiler_params=_params(("parallel", "parallel", "parallel")),
    )(dconv, dconv, cw)


def _my_index():
    return 4 * lax.axis_index("x") + 2 * lax.axis_index("y") + lax.axis_index("c")


def _peer(k):
    return (lax.axis_index("x") ^ ((k >> 2) & 1), lax.axis_index("y") ^ ((k >> 1) & 1),
            lax.axis_index("c") ^ (k & 1))


_HBM = pl.BlockSpec(memory_space=pltpu.HBM)
_SEM = pl.BlockSpec(memory_space=pltpu.SEMAPHORE)
_DATAFLOW = pltpu.SideEffectType.DATAFLOW_SIDE_EFFECTING


def _split_copies(gather, src_ref, land_ref, send_sems, recv_sems, local_sem):
    me = _my_index()

    def part(j):
        return src_ref if gather else src_ref.at[j]

    local = pltpu.make_async_copy(part(me), land_ref.at[me], local_sem)
    sends = [pltpu.make_async_remote_copy(
        src_ref=part(me ^ k), dst_ref=land_ref.at[me], send_sem=send_sems.at[k - 1], recv_sem=recv_sems.at[k - 1],
        device_id=_peer(k), device_id_type=pl.DeviceIdType.MESH) for k in range(1, N_DEV)]
    recvs = [pltpu.make_async_remote_copy(
        src_ref=part(me ^ k), dst_ref=land_ref.at[me ^ k], send_sem=send_sems.at[k - 1], recv_sem=recv_sems.at[k - 1],
        device_id=_peer(k), device_id_type=pl.DeviceIdType.MESH) for k in range(1, N_DEV)]
    return local, sends, recvs


def split_start(name, srcs, gather):
    n = len(srcs)
    lands = [((N_DEV,) + s.shape) if gather else s.shape for s in srcs]

    def body(*refs):
        ins, outs = refs[:2 * n], refs[2 * n:]
        for i in range(n):
            local, sends, _ = _split_copies(gather, ins[i], ins[n + i], *outs[3 * i:3 * i + 3])
            local.start()
            for cp in sends:
                cp.start()
        outs[-1][...] = jnp.zeros_like(outs[-1])

    dma7 = pltpu.SemaphoreType.DMA((N_DEV - 1,))
    out = pl.pallas_call(
        body, name=name,
        out_shape=(dma7, dma7, pltpu.SemaphoreType.DMA(())) * n
                  + tuple(pltpu.HBM(s.shape, s.dtype) for s in srcs)
                  + tuple(pltpu.HBM(shape, s.dtype) for shape, s in zip(lands, srcs))
                  + (jax.ShapeDtypeStruct((8, 128), F32),),
        in_specs=(_HBM,) * (2 * n),
        out_specs=(_SEM,) * (3 * n) + (_HBM,) * (2 * n) + (pl.BlockSpec(memory_space=pltpu.VMEM),),
        input_output_aliases={i: 3 * n + i for i in range(2 * n)},
        compiler_params=pltpu.CompilerParams(has_side_effects=_DATAFLOW),
    )(*[pltpu.with_memory_space_constraint(s, pltpu.HBM) for s in srcs],
      *[pltpu.with_memory_space_constraint(lax.empty(shape, s.dtype), pltpu.HBM) for shape, s in zip(lands, srcs)])
    handles = [tuple(out[3 * i:3 * i + 3]) + (out[3 * n + i], out[4 * n + i]) for i in range(n)]
    return handles, out[-1][0, 0]


def split_wait(name, handles, after, gather):
    send_sems, recv_sems, local_sem, src_thru, land_thru = handles

    def body(src_ref, land_ref, send_sems, recv_sems, local_sem, after_ref, src_dead, got_ref, token):
        local, sends, recvs = _split_copies(gather, src_ref, land_ref, send_sems, recv_sems, local_sem)
        local.wait()
        for cp in recvs:
            cp.wait_send()
            cp.wait_recv()
        token[...] = jnp.zeros_like(token)

    out = pl.pallas_call(
        body, name=name,
        out_shape=(pltpu.HBM(src_thru.shape, src_thru.dtype), pltpu.HBM(land_thru.shape, land_thru.dtype),
                   jax.ShapeDtypeStruct((8, 128), F32)),
        in_specs=(_HBM, _HBM, _SEM, _SEM, _SEM, pl.BlockSpec(memory_space=pl.ANY)),
        out_specs=(_HBM, _HBM, pl.BlockSpec(memory_space=pltpu.VMEM)),
        input_output_aliases={0: 0, 1: 1},
        compiler_params=pltpu.CompilerParams(has_side_effects=_DATAFLOW),
    )(src_thru, land_thru, send_sems, recv_sems, local_sem, after)
    return out[1], out[2][0, 0]


def sum_parts(name, r):
    _, rows, cols = r.shape

    def body(r_ref, o_ref):
        acc = r_ref[0].astype(F32)
        for s in range(1, N_DEV):
            acc = acc + r_ref[s].astype(F32)
        o_ref[...] = acc

    return pl.pallas_call(body, name=name, out_shape=jax.ShapeDtypeStruct((rows, cols), F32),
                          compiler_params=_params())(r)


def adamw(name, w, m, v, parts=None, g=None, layer=0, into=None, order=None):
    _, rows, cols = w.shape
    br = _tile(rows, 256, 16)
    c1 = 1.0 / (1.0 - ADAM_B1 ** ADAM_STEP)
    c2 = 1.0 / (1.0 - ADAM_B2 ** ADAM_STEP)

    def body(g_ref, w_ref, m_ref, v_ref, *rest):
        og_ref, od_ref, om_ref, ov_ref = rest[-4:]
        if parts is None:
            gs = g_ref[...]
        else:
            gs = g_ref[0].astype(F32)
            for s in range(1, N_DEV):
                gs = gs + g_ref[s].astype(F32)
        mn = ADAM_B1 * m_ref[...] + (1.0 - ADAM_B1) * gs
        vn = ADAM_B2 * v_ref[...] + (1.0 - ADAM_B2) * (gs * gs)
        og_ref[...] = gs
        om_ref[...] = mn
        ov_ref[...] = vn
        od_ref[...] = -ADAM_LR * ((mn * c1) / (jnp.sqrt(vn * c2) + ADAM_EPS) + ADAM_WD * w_ref[...])

    blk = pl.BlockSpec((None, br, cols), lambda i: (layer, i, 0))
    if parts is None:
        gspec = pl.BlockSpec((br, cols), lambda i: (i, 0))
    else:
        gspec = pl.BlockSpec((N_DEV, br, cols), lambda i: (0, i, 0))
    earlier = [] if into is None else list(into)
    behind = [] if order is None else [order]
    return pl.pallas_call(
        body, name=name, grid=(rows // br,),
        in_specs=[gspec, blk, blk, blk] + [pl.BlockSpec(memory_space=pl.ANY)] * len(earlier)
                 + [pl.BlockSpec((1, 128), lambda i: (0, 0))] * len(behind),
        out_specs=[blk] * 4, out_shape=[jax.ShapeDtypeStruct(w.shape, F32)] * 4,
        input_output_aliases={4 + k: k for k in range(len(earlier))},
        compiler_params=_params(("parallel",)),
    )(g if parts is None else parts, w, m, v, *earlier, *behind)


SMALL = ("norm_mix", "norm_xattn", "norm_ffn", "norm_mem", "norm_final", "pool_w", "pool_scale",
         "ssm_lam_re", "ssm_lam_im", "ssm_log_dt", "ssm_b_re", "ssm_b_im", "ssm_c_re", "ssm_c_im",
         "ffn_conv_b", "ssm_d", "ffn_conv_w")
SMALL_SHARDED = {"ssm_d": 1, "ffn_conv_w": 2}
BIG = ("ab_w_in", "ab_w_out", "ssm_w_in", "ssm_w_glu", "xa_w_q", "xa_w_kv", "xa_w_o", "ffn_w_up", "ffn_w_down")
WEIGHTS = ("norm_mix", "norm_xattn", "norm_ffn", "norm_mem", "norm_final", "ab_w_in", "pool_w", "pool_scale",
           "ab_w_out", "ssm_w_in", "ssm_lam_re", "ssm_lam_im", "ssm_log_dt", "ssm_b_re", "ssm_b_im", "ssm_c_re",
           "ssm_c_im", "ssm_d", "ssm_w_glu", "xa_w_q", "xa_w_kv", "xa_w_o", "ffn_w_up", "ffn_conv_w", "ffn_conv_b",
           "ffn_w_down")


def _rows8(g):
    return g.reshape(N_DEV, g.size // (N_DEV * D_MODEL), D_MODEL)


def _square(a):
    return a.reshape(D_MODEL, D_MODEL)


_LAYOUT = {"ab_w_out": _square, "ssm_w_in": _square, "xa_w_q": _square, "xa_w_o": _square,
           "ffn_w_down": lambda a: a.reshape(N_DEV // 2, FF_SHARD, D_MODEL)}
GATHER_ORDER = (("ab_w_in", 0), ("ffn_conv_w", None), ("ssm_d", None), ("ab_w_out", 0), ("xa_w_q", 0),
                ("xa_w_kv", 0), ("xa_w_o", 0), ("ffn_w_up", 0), ("ffn_w_down", 0), ("ffn_w_up", 1),
                ("ffn_w_down", 1), ("ssm_w_in", 0), ("ssm_w_glu", 0), ("xa_w_q", 1), ("xa_w_kv", 1), ("xa_w_o", 1))
GATHER_AHEAD = 7
GATHER_BATCHES = (3, 8, 11, 13, 16)


class _Step:
    def __init__(self, master, small):
        self.master, self.small = master, small
        self.pending, self.gathers, self.weights, self.sent, self.queued = [], {}, {}, [], []

    def follow(self, v):
        for z in self.pending:
            v = v + z
        self.pending = []
        return v

    def start_gathers(self, upto, zero):
        upto = min(end for end in GATHER_BATCHES if end >= min(upto, len(GATHER_ORDER)))
        todo = GATHER_ORDER[len(self.gathers):upto]
        if not todo:
            return
        shards = []
        for n, l in todo:
            if l is None:
                shards.append(self.master[n] + zero)
            else:
                shards.append((self.master[n][l] + zero).astype(MXU_DTYPE))
        handles, z = split_start(f"ags_{len(self.gathers)}", shards, gather=True)
        self.gathers.update(zip(todo, handles))
        self.pending.append(z)

    def weight(self, n, l, after):
        if (n, l) not in self.weights:
            full, z = split_wait(f"agw_{n}{'' if l is None else l}", self.gathers[(n, l)], after, gather=True)
            self.weights[(n, l)] = _LAYOUT.get(n, lambda a: a)(full)
            self.start_gathers(GATHER_ORDER.index((n, l)) + 1 + GATHER_AHEAD, z)
        return self.weights[(n, l)]

    def send_grad(self, n, l, part, flush=True):
        self.queued.append((n, l, part))
        if flush:
            handles, z = split_start(f"xs_{n}{l}", [p for _, _, p in self.queued], gather=False)
            self.sent += [(qn, ql, h) for (qn, ql, _), h in zip(self.queued, handles)]
            self.queued = []
            self.pending.append(z)


def _layer_tail(st, l, x_in, hq, mem_n, acts, next_gain=None):
    bsz, seq = acts["bsz"], acts["seq"]
    p = st.small
    q = mm_nn(f"xa_q{l}", hq, st.weight("xa_w_q", l, x_in))
    kv = mm_nn_bs(f"xa_kv{l}", mem_n, st.weight("xa_w_kv", l, x_in))
    o = xattn_fwd(q, kv, bsz, seq)
    x_mid, hf = mm_nn(f"xa_o{l}", o, st.weight("xa_w_o", l, o), res=x_in, out_dtype=F32,
                      norm_gain=st.follow(p["norm_ffn"][l]))
    up = mm_nn_bs(f"ffn_up{l}", hf, st.weight("ffn_w_up", l, x_mid), stacked_out=True)
    conv_w = st.weight("ffn_conv_w", None, x_mid)[:, l]
    act = conv_fwd(up, conv_w, p["ffn_conv_b"][l], bsz, seq)
    w_down = st.weight("ffn_w_down", l, act)
    if next_gain is None:
        x_out, h_next = mm_as_nn(f"ffn_down{l}", act, w_down, res=x_mid), None
    else:
        x_out, h_next = mm_as_nn(f"ffn_down{l}", act, w_down, res=x_mid, norm_gain=st.follow(next_gain))
    acts[l].update(x_in=x_in, hq=hq, q=q, kv=kv, o=o, x_mid=x_mid, hf=hf, up=up, act=act)
    return x_out, h_next


def _layer_tail_bwd(st, l, dx, mem_n, acts, grads):
    a = acts[l]
    bsz, seq = acts["bsz"], acts["seq"]
    p = st.small
    dact = mm_nt_os(f"d_act{l}", dx, st.weight("ffn_w_down", l, dx))
    st.send_grad("ffn_w_down", l, _rows8(mm_tn(f"g_ffn_down{l}", a["act"], dx, a_stacked=True)), flush=False)
    conv_w = st.weight("ffn_conv_w", None, dx)[:, l]
    dconv, dcw, dcb = conv_bwd_taps(a["up"], conv_w, p["ffn_conv_b"][l], dact, bsz, seq)
    grads["ffn_conv_w"][l] = dcw
    grads["ffn_conv_b"][l] = dcb
    dup = conv_bwd_input(dconv, conv_w, bsz, seq)
    dx_mid, grads["norm_ffn"][l] = mm_nt_bs(f"d_hf{l}", dup, st.weight("ffn_w_up", l, dx), dc_stacked=True,
                                            rms=(a["x_mid"], st.follow(p["norm_ffn"][l]), dx))
    st.send_grad("ffn_w_up", l, mm_tn(f"g_ffn_up{l}", a["hf"], dup, dc_stacked=True))
    do = mm_nt(f"d_o{l}", dx_mid, st.weight("xa_w_o", l, dx))
    st.send_grad("xa_w_o", l, _rows8(mm_tn(f"g_xa_o{l}", a["o"], dx_mid)), flush=False)
    dq, dk, dv = xattn_bwd(a["q"], a["kv"], do, bsz, seq)
    dkv = jnp.concatenate([dk, dv], axis=1).astype(BF16)
    dx_in, grads["norm_xattn"][l] = mm_nt(f"d_hq{l}", dq, st.weight("xa_w_q", l, dx),
                                          rms=(a["x_in"], st.follow(p["norm_xattn"][l]), dx_mid))
    st.send_grad("xa_w_q", l, _rows8(mm_tn(f"g_xa_q{l}", a["hq"], dq)), flush=False)
    dmem_n = mm_nt_bs(f"d_memn{l}", dkv, st.weight("xa_w_kv", l, dx), out_dtype=F32)
    st.send_grad("xa_w_kv", l, mm_tn(f"g_xa_kv{l}", mem_n, dkv, dc_cols=2 * D_MODEL // N_DEV))
    return dx_in, dmem_n


def kernel(x, mem, norm_mix, norm_xattn, norm_ffn, norm_mem, norm_final, ab_w_in, pool_w, pool_scale, ab_w_out, ssm_w_in, ssm_lam_re, ssm_lam_im, ssm_log_dt, ssm_b_re, ssm_b_im, ssm_c_re, ssm_c_im, ssm_d, ssm_w_glu, xa_w_q, xa_w_kv, xa_w_o, ffn_w_up, ffn_conv_w, ffn_conv_b, ffn_w_down, loss_target, m_norm_mix, m_norm_xattn, m_norm_ffn, m_norm_mem, m_norm_final, m_ab_w_in, m_pool_w, m_pool_scale, m_ab_w_out, m_ssm_w_in, m_ssm_lam_re, m_ssm_lam_im, m_ssm_log_dt, m_ssm_b_re, m_ssm_b_im, m_ssm_c_re, m_ssm_c_im, m_ssm_d, m_ssm_w_glu, m_xa_w_q, m_xa_w_kv, m_xa_w_o, m_ffn_w_up, m_ffn_conv_w, m_ffn_conv_b, m_ffn_w_down, v_norm_mix, v_norm_xattn, v_norm_ffn, v_norm_mem, v_norm_final, v_ab_w_in, v_pool_w, v_pool_scale, v_ab_w_out, v_ssm_w_in, v_ssm_lam_re, v_ssm_lam_im, v_ssm_log_dt, v_ssm_b_re, v_ssm_b_im, v_ssm_c_re, v_ssm_c_im, v_ssm_d, v_ssm_w_glu, v_xa_w_q, v_xa_w_kv, v_xa_w_o, v_ffn_w_up, v_ffn_conv_w, v_ffn_conv_b, v_ffn_w_down):
    given = dict(locals())
    master = {n: given[n] for n in WEIGHTS}
    mom1 = {n: given["m_" + n] for n in WEIGHTS}
    mom2 = {n: given["v_" + n] for n in WEIGHTS}
    bsz, seq, d = x.shape
    t = bsz * seq
    me = _my_index()

    st = _Step(master, {"norm_xattn": norm_xattn, "norm_ffn": norm_ffn,
                        "ffn_conv_b": [ffn_conv_b[l].reshape(N_DEV, 1, FF_SHARD) for l in range(2)]})
    st.start_gathers(1, 0.0)
    zero = st.follow(jnp.zeros((), F32))

    acts = {"bsz": bsz, "seq": seq, 0: {}, 1: {}}
    x0 = x.reshape(t, d)
    mem2 = mem.reshape(bsz * MEM_LEN, d)
    mem_n = rms_fwd("rms_mem", mem2, norm_mem + zero)
    pscale = pool_scale.reshape(1, SB_WIDTH)

    h0 = rms_fwd("rms_mix0", x0, norm_mix[0] + zero)
    w_in = st.weight("ab_w_in", 0, h0)
    proj = mm_nn_bs("ab_in", h0, w_in, out_dtype=F32)
    a_out, rsum = sb_attn_fwd(proj, st.follow(jnp.zeros((1, 128), F32)), bsz, seq)
    p_out = pool_fwd(proj, pool_w[0], pscale, bsz, seq)
    w_out = st.weight("ab_w_out", 0, a_out)
    x1 = mm_nn("ab_out_a", a_out, w_out, res=x0, out_dtype=F32)
    x1, hq0 = mm_nn("ab_out_p", p_out, w_out, res=x1, koff=SB_WIDTH, out_dtype=F32,
                    norm_gain=st.follow(norm_xattn[0]))
    x3, h1 = _layer_tail(st, 0, x1, hq0, mem_n, acts, next_gain=norm_mix[1])

    b_re2 = ssm_b_re.reshape(64, 1024)
    b_im2 = ssm_b_im.reshape(64, 1024)
    log_dt = ssm_log_dt.reshape(64, 1)
    lb_re, lb_im, bb_re2, bb_im2 = ssm_prep(ssm_lam_re[0], ssm_lam_im[0], log_dt, b_re2, b_im2)
    wt = _ssm_in_weights(bb_re2, bb_im2)
    ct = _ssm_out_weights(ssm_c_re[0], ssm_c_im[0])
    a_re = lb_re.reshape(1, SSM_STATES)
    a_im = lb_im.reshape(1, SSM_STATES)
    u = mm_nn("ssm_in", h1, st.weight("ssm_w_in", 0, x3), out_dtype=F32)
    dskip = st.weight("ssm_d", None, x3).reshape(1, D_MODEL)
    y, gl, h_re, h_im = ssm_fwd(u, wt, ct, a_re, a_im, dskip, bsz, seq)
    glu = mm_nn_bs("ssm_glu", gl, st.weight("ssm_w_glu", 0, gl), out_dtype=F32)
    x4, hq1 = glu_fwd(glu, x3, st.follow(norm_xattn[1]))
    x6, _ = _layer_tail(st, 1, x4, hq1, mem_n, acts)

    loss_row, dx, g_norm_final = loss_head(x6, norm_final, loss_target.reshape(t, d))
    loss = lax.psum(loss_row[0, 0], MESH_AXES)

    grads = {n: [None, None] for n in ("ffn_conv_w", "ffn_conv_b", "norm_ffn", "norm_xattn", "norm_mix")}
    dx4, dmem_1 = _layer_tail_bwd(st, 1, dx, mem_n, acts, grads)
    dglu = glu_bwd(glu, dx4)
    dgl = mm_nt_bs("d_gl", dglu, st.weight("ssm_w_glu", 0, dx))
    st.send_grad("ssm_w_glu", 0, mm_tn("g_ssm_glu", gl, dglu, dc_cols=2 * D_MODEL // N_DEV), flush=False)
    du, dwt, dct, g_dskip, da_re, da_im = ssm_bwd(dgl, y, u, h_re, h_im, wt, ct, a_re, a_im, dskip, bsz, seq)
    dbb_re, dbb_im = _ssm_in_weights_bwd(dwt)
    g_c_re, g_c_im = _ssm_out_weights_bwd(dct)
    g_lam_re, g_lam_im, g_log_dt, g_b_re, g_b_im = ssm_prep_bwd(
        ssm_lam_re[0], ssm_lam_im[0], log_dt, b_re2, b_im2, da_re.reshape(64, 64), da_im.reshape(64, 64),
        dbb_re, dbb_im)
    dx3, grads["norm_mix"][1] = mm_nt("d_h1", du, st.weight("ssm_w_in", 0, dx),
                                      rms=(x3, st.follow(norm_mix[1]), dx4))
    st.send_grad("ssm_w_in", 0, _rows8(mm_tn("g_ssm_in", h1, du)))

    dx1, dmem_0 = _layer_tail_bwd(st, 0, dx3, mem_n, acts, grads)
    dcat = mm_nt("d_cat", dx1, st.weight("ab_w_out", 0, dx))
    st.send_grad("ab_w_out", 0, _rows8(jnp.concatenate(
        [mm_tn("g_ab_out_a", a_out, dx1), mm_tn("g_ab_out_p", p_out, dx1)], axis=0)), flush=False)
    dq, dk, dv = sb_attn_bwd(proj, rsum, dcat, bsz, seq)
    dpu, g_pool_w, g_pool_scale = pool_bwd(proj, pool_w[0], st.follow(pscale), dcat, bsz, seq)
    dproj = jnp.concatenate([dq, dk, dv, dpu], axis=1).astype(BF16)
    st.send_grad("ab_w_in", 0, mm_tn("g_ab_in", h0, dproj, dc_cols=2 * D_MODEL // N_DEV))
    dx0, grads["norm_mix"][0] = mm_nt_bs("d_h0", dproj, st.weight("ab_w_in", 0, dx),
                                         rms=(x0, st.follow(norm_mix[0]), dx1))
    _, g_norm_mem = rms_bwd("rms_mem_bwd", mem2, norm_mem, dmem_0 + dmem_1, need_dx=False)

    small_g = {
        "norm_mix": jnp.stack([g[0] for g in grads["norm_mix"]]),
        "norm_xattn": jnp.stack([g[0] for g in grads["norm_xattn"]]),
        "norm_ffn": jnp.stack([g[0] for g in grads["norm_ffn"]]),
        "norm_mem": g_norm_mem[0], "norm_final": g_norm_final[0],
        "pool_w": g_pool_w[None], "pool_scale": g_pool_scale,
        "ssm_lam_re": g_lam_re[None], "ssm_lam_im": g_lam_im[None], "ssm_log_dt": g_log_dt.reshape(1, 64),
        "ssm_b_re": g_b_re.reshape(1, 64, 64, 16), "ssm_b_im": g_b_im.reshape(1, 64, 64, 16),
        "ssm_c_re": g_c_re[None], "ssm_c_im": g_c_im[None],
        "ffn_conv_b": jnp.stack([g.reshape(2 * D_FF) for g in grads["ffn_conv_b"]]),
        "ssm_d": g_dskip,
        "ffn_conv_w": jnp.stack([g.transpose(1, 0, 2).reshape(3, 2 * D_FF) for g in grads["ffn_conv_w"]]),
    }
    sizes = [int(small_g[n].size) for n in SMALL]
    total = sum(sizes)
    rows8 = -(-total // (N_DEV * 128 * 8)) * 8
    flat = jnp.concatenate([small_g[n].reshape(-1).astype(F32) for n in SMALL]
                           + [jnp.zeros((N_DEV * rows8 * 128 - total,), F32)])
    (in_flight,), z = split_start("xs_small", [flat.reshape(N_DEV, rows8, 128)], gather=False)
    st.pending.append(z)
    stepped, last = {}, dx0
    for i, (n, l, handles) in enumerate(st.sent):
        if i == len(st.sent) // 2:
            recv, _ = split_wait("xw_small", in_flight, last, gather=False)
            (in_flight,), z = split_start("ags_small", [sum_parts("sum_small", recv)], gather=True)
            st.pending.append(z)
        recv, _ = split_wait(f"xw_{n}{l}", handles, dx0, gather=False)
        shape3 = (master[n].shape[0],) + recv.shape[1:]
        stepped[n] = adamw(f"adamw_{n}{l}", master[n].reshape(shape3), mom1[n].reshape(shape3),
                           mom2[n].reshape(shape3), parts=recv, layer=l, into=stepped.get(n),
                           order=st.follow(jnp.zeros((1, 128), F32)))
        last = stepped[n][0]
    out_g, out_d, out_m, out_v = ({n: stepped[n][k].reshape(master[n].shape) for n in BIG} for k in range(4))
    summed = split_wait("agw_small", in_flight, last, gather=True)[0].reshape(-1)

    def local_part(name, a):
        ax = SMALL_SHARDED.get(name)
        if ax is None:
            return a
        n_loc = a.shape[ax] // N_DEV
        return lax.dynamic_slice_in_dim(a, me * n_loc, n_loc, axis=ax)

    off = 0
    for n, sz in zip(SMALL, sizes):
        g_n = local_part(n, summed[off:off + sz].reshape(small_g[n].shape))
        off += sz
        cols = g_n.shape[-1] if g_n.shape[-1] >= 128 or g_n.ndim < 3 else g_n.shape[-1] * g_n.shape[-2]
        shape3 = (1, g_n.size // cols, cols)
        res = adamw("adamw_" + n, master[n].reshape(shape3), mom1[n].reshape(shape3), mom2[n].reshape(shape3),
                    g=g_n.reshape(shape3[1:]))
        for dst, r in zip((out_g, out_d, out_m, out_v), res):
            dst[n] = r.reshape(master[n].shape)

    return (loss, dx0.reshape(bsz, seq, d), *[out_g[n] for n in WEIGHTS], *[out_d[n] for n in WEIGHTS],
            *[out_m[n] for n in WEIGHTS], *[out_v[n] for n in WEIGHTS])
```

```python
import math

import jax
import jax.numpy as jnp
from jax import lax
from jax.experimental import pallas as pl
from jax.experimental.pallas import tpu as pltpu

F32 = jnp.float32
BF16 = jnp.bfloat16
MXU_DTYPE = jnp.bfloat16
N_DEV = 8
MESH_AXES = ("x", "y", "c")

D_MODEL = 1024
SB_HEAD_DIM = 64
SB_WIDTH = 512
SB_BLOCK = 256
POOL_WINDOWS = (2, 4, 8, 16)
POOL_GROUP = 128
POOL_HALO = 16
SSM_TILES = 8
SSM_TILE_STATES = 512
SSM_STATES = 4096
SSM_LANES = 1024
MEM_LEN = 256
XA_HEADS = 4
XA_HEAD_DIM = 256
D_FF = 2816
FF_SHARD = 704
EPS = 1e-6
ADAM_LR = 0.001
ADAM_B1 = 0.9
ADAM_B2 = 0.999
ADAM_EPS = 1e-08
ADAM_WD = 0.01
ADAM_STEP = 10
VMEM_LIMIT = 56 * 1024 * 1024

_NN = (((1,), (0,)), ((), ()))
_NT = (((1,), (1,)), ((), ()))
_TN = (((0,), (0,)), ((), ()))


def _params(sem=None):
    if sem is None:
        return pltpu.CompilerParams(vmem_limit_bytes=VMEM_LIMIT)
    return pltpu.CompilerParams(dimension_semantics=sem, vmem_limit_bytes=VMEM_LIMIT)


def _tile(n, pref, mult=8):
    if n <= pref:
        return n
    for t in range(pref, 0, -1):
        if n % t == 0 and t % mult == 0:
            return t
    return n


def _dot(a, b, dims):
    return lax.dot_general(a.astype(MXU_DTYPE), b.astype(MXU_DTYPE), dims, preferred_element_type=F32)


def _dot_exact01(x, m01, dims=_NN):
    x1 = x.astype(BF16)
    r1 = x - x1.astype(F32)
    x2 = r1.astype(BF16)
    x3 = (r1 - x2.astype(F32)).astype(BF16)
    m = m01.astype(BF16)
    out = lax.dot_general(x1, m, dims, preferred_element_type=F32)
    out = out + lax.dot_general(x2, m, dims, preferred_element_type=F32)
    return out + lax.dot_general(x3, m, dims, preferred_element_type=F32)


def _mm(name, a, b, dims, grid, a_spec, b_spec, o_spec, out_shape, out_dtype, acc_shape, res=None, r_spec=None,
        group=1, n=None, a_sel="full", b_sel="full", o_sel="full", norm_gain=None, rms=None):
    nk = grid[2]
    if out_dtype is None:
        out_dtype = BF16
    n_out = out_shape[-1]
    vec = pl.BlockSpec((1, n_out), lambda i, j, kk: (0, 0))

    def at(sel, s):
        if sel == "lead":
            return (s,)
        if sel == "lanes":
            return (slice(None), slice(s * n, (s + 1) * n))
        return (Ellipsis,)

    extra = [] if res is None else [(res, r_spec)]
    if norm_gain is not None:
        extra.append((norm_gain.reshape(1, n_out), vec))
    if rms is not None:
        extra += [(rms[0], o_spec), (rms[1].reshape(1, n_out), vec), (rms[2], o_spec)]
    n_in = 2 + len(extra)
    if rms is not None:
        out_specs = [o_spec, vec]
        out_shapes = [jax.ShapeDtypeStruct(out_shape, F32), jax.ShapeDtypeStruct((1, n_out), F32)]
    elif norm_gain is not None:
        out_specs = [o_spec, o_spec]
        out_shapes = [jax.ShapeDtypeStruct(out_shape, out_dtype), jax.ShapeDtypeStruct(out_shape, BF16)]
    else:
        out_specs, out_shapes = o_spec, jax.ShapeDtypeStruct(out_shape, out_dtype)

    def body(*refs):
        a_ref, b_ref = refs[0], refs[1]
        ins = list(refs[2:n_in])
        r_ref = ins.pop(0) if res is not None else None
        outs = refs[n_in:]
        o_ref = outs[0]
        acc = refs[-1] if nk > 1 else None
        k = pl.program_id(2)

        def finish(val):
            if r_ref is not None:
                val = val + r_ref[...].astype(F32)
            if rms is not None:
                x_ref, g_ref, d_ref = ins
                xf = x_ref[...]
                r = lax.rsqrt(jnp.mean(xf * xf, axis=-1, keepdims=True) + EPS)
                xh = xf * r
                part = jnp.sum(val * xh, axis=0, keepdims=True)
                first = pl.program_id(0) == 0

                @pl.when(first)
                def _():
                    outs[1][...] = part

                @pl.when(jnp.logical_not(first))
                def _():
                    outs[1][...] += part

                dxh = val * g_ref[...]
                o_ref[...] = d_ref[...] + r * (dxh - xh * jnp.mean(dxh * xh, axis=-1, keepdims=True))
                return
            o_ref[...] = val.astype(out_dtype)
            if norm_gain is not None:
                r = lax.rsqrt(jnp.mean(val * val, axis=-1, keepdims=True) + EPS)
                outs[1][...] = (val * r * ins[0][...]).astype(BF16)

        def emit(s, val):
            if nk == 1:
                if o_sel == "full":
                    finish(val)
                else:
                    o_ref[at(o_sel, s)] = val.astype(out_dtype)
                return

            @pl.when(k == 0)
            def _():
                acc[at(o_sel, s)] = val

            @pl.when(k > 0)
            def _():
                acc[at(o_sel, s)] += val

        total = None
        if a_sel == "full" and b_sel == "lanes":
            wide = _dot(a_ref[...], b_ref[...], dims)
            for s in range(group):
                emit(s, wide[:, s * n:(s + 1) * n])
        else:
            for s in range(group):
                val = _dot(a_ref[at(a_sel, s)], b_ref[at(b_sel, s)], dims)
                if o_sel == "full":
                    total = val if total is None else total + val
                else:
                    emit(s, val)
        if o_sel == "full":
            emit(0, total)
        if nk > 1:
            @pl.when(k == nk - 1)
            def _():
                if o_sel == "full":
                    finish(acc[...])
                else:
                    o_ref[...] = acc[...].astype(out_dtype)

    rows_sem = "arbitrary" if rms is not None else "parallel"
    return pl.pallas_call(
        body, name=name, grid=grid, in_specs=[a_spec, b_spec] + [s for _, s in extra], out_specs=out_specs,
        out_shape=out_shapes, scratch_shapes=[pltpu.VMEM(acc_shape, F32)] if nk > 1 else [],
        compiler_params=_params((rows_sem, rows_sem, "arbitrary")),
    )(a, b, *[x for x, _ in extra])


def _row_tile(m, epi):
    return _tile(m, 512 if epi.get("rms") is not None else 1024)


def mm_nn(name, a, b, res=None, koff=0, out_dtype=None, **epi):
    m, k = a.shape
    n = b.shape[1]
    tm, tn, tk = _row_tile(m, epi), _tile(n, 1024, 128), _tile(k, 1024, 128)
    kb = koff // tk
    spec = pl.BlockSpec((tm, tn), lambda i, j, kk: (i, j))
    return _mm(name, a, b, _NN, (m // tm, n // tn, k // tk),
               pl.BlockSpec((tm, tk), lambda i, j, kk: (i, kk)),
               pl.BlockSpec((tk, tn), lambda i, j, kk: (kk + kb, j)),
               spec, (m, n), out_dtype, (tm, tn), res, spec, **epi)


def mm_nn_bs(name, a, bs, stacked_out=False, out_dtype=None):
    m, k = a.shape
    s, _, n = bs.shape
    tm, tk = _tile(m, 2048 if stacked_out else 1024), _tile(k, 1024, 128)
    a_spec = pl.BlockSpec((tm, tk), lambda i, j, kk: (i, kk))
    if stacked_out:
        return _mm(name, a, bs, _NN, (m // tm, s, k // tk), a_spec,
                   pl.BlockSpec((None, tk, n), lambda i, j, kk: (j, kk, 0)),
                   pl.BlockSpec((None, tm, n), lambda i, j, kk: (j, i, 0)), (s, m, n), out_dtype, (tm, n))
    g = _tile(s, max(1, 1024 // n), 1)
    return _mm(name, a, bs, _NN, (m // tm, s // g, k // tk), a_spec,
               pl.BlockSpec((g, tk, n), lambda i, j, kk: (j, kk, 0)),
               pl.BlockSpec((tm, g * n), lambda i, j, kk: (i, j)), (m, s * n), out_dtype, (tm, g * n),
               group=g, n=n, b_sel="lead", o_sel="lanes")


def mm_as_nn(name, a_st, b3, res, out_dtype=F32, **epi):
    s, m, kp = a_st.shape
    n = b3.shape[2]
    tm, tn = _row_tile(m, epi), _tile(n, 1024, 128)
    spec = pl.BlockSpec((tm, tn), lambda i, j, kk: (i, j))
    g = _tile(s, 2, 1)
    return _mm(name, a_st, b3, _NN, (m // tm, n // tn, s // g),
               pl.BlockSpec((g, tm, kp), lambda i, j, kk: (kk, i, 0)),
               pl.BlockSpec((g, kp, tn), lambda i, j, kk: (kk, 0, j)),
               spec, (m, n), out_dtype, (tm, tn), res, spec, group=g, a_sel="lead", b_sel="lead", **epi)


def mm_nt(name, dc, b, out_dtype=None, **epi):
    m, n = dc.shape
    k = b.shape[0]
    tm, tko, tnr = _row_tile(m, epi), _tile(k, 1024, 128), _tile(n, 1024, 128)
    return _mm(name, dc, b, _NT, (m // tm, k // tko, n // tnr),
               pl.BlockSpec((tm, tnr), lambda i, j, kk: (i, kk)),
               pl.BlockSpec((tko, tnr), lambda i, j, kk: (j, kk)),
               pl.BlockSpec((tm, tko), lambda i, j, kk: (i, j)), (m, k), out_dtype, (tm, tko), **epi)


def mm_nt_bs(name, dc, bs, dc_stacked=False, out_dtype=None, **epi):
    s, k, n = bs.shape
    m = dc.shape[1] if dc_stacked else dc.shape[0]
    tm, tko = (_tile(m, 1024) if dc_stacked else _row_tile(m, epi)), _tile(k, 1024, 128)
    o_spec = pl.BlockSpec((tm, tko), lambda i, j, kk: (i, j))
    if dc_stacked:
        g = _tile(s, 2, 1)
        return _mm(name, dc, bs, _NT, (m // tm, k // tko, s // g),
                   pl.BlockSpec((g, tm, n), lambda i, j, kk: (kk, i, 0)),
                   pl.BlockSpec((g, tko, n), lambda i, j, kk: (kk, j, 0)), o_spec, (m, k), out_dtype, (tm, tko),
                   group=g, a_sel="lead", b_sel="lead", **epi)
    g = _tile(s, max(1, 2048 // n), 1)
    return _mm(name, dc, bs, _NT, (m // tm, k // tko, s // g),
               pl.BlockSpec((tm, g * n), lambda i, j, kk: (i, kk)),
               pl.BlockSpec((g, tko, n), lambda i, j, kk: (kk, j, 0)), o_spec, (m, k), out_dtype, (tm, tko),
               group=g, n=n, a_sel="lanes", b_sel="lead", **epi)


def mm_nt_os(name, dc, b3, out_dtype=None):
    m, n = dc.shape
    s, kp, _ = b3.shape
    tm, tnr = _tile(m, 2048), _tile(n, 1024, 128)
    return _mm(name, dc, b3, _NT, (m // tm, s, n // tnr),
               pl.BlockSpec((tm, tnr), lambda i, j, kk: (i, kk)),
               pl.BlockSpec((None, kp, tnr), lambda i, j, kk: (j, 0, kk)),
               pl.BlockSpec((None, tm, kp), lambda i, j, kk: (j, i, 0)), (s, m, kp), out_dtype, (tm, kp))


def mm_tn(name, a, dc, a_stacked=False, dc_cols=None, dc_stacked=False, out_dtype=None):
    if a_stacked:
        s, m, kp = a.shape
        n = dc.shape[1]
        tno, tmr = _tile(n, 1024, 128), _tile(m, 2048)
        return _mm(name, a, dc, _TN, (s, n // tno, m // tmr),
                   pl.BlockSpec((None, tmr, kp), lambda i, j, kk: (i, kk, 0)),
                   pl.BlockSpec((tmr, tno), lambda i, j, kk: (kk, j)),
                   pl.BlockSpec((None, kp, tno), lambda i, j, kk: (i, 0, j)), (s, kp, n), out_dtype, (kp, tno))
    m, k = a.shape
    tko, tmr = _tile(k, 1024, 128), _tile(m, 2048)
    a_spec = pl.BlockSpec((tmr, tko), lambda i, j, kk: (kk, i))
    if dc_stacked:
        s, _, n = dc.shape
        return _mm(name, a, dc, _TN, (k // tko, s, m // tmr), a_spec,
                   pl.BlockSpec((None, tmr, n), lambda i, j, kk: (j, kk, 0)),
                   pl.BlockSpec((None, tko, n), lambda i, j, kk: (j, i, 0)), (s, k, n), out_dtype, (tko, n))
    if dc_cols is not None:
        n = dc_cols
        s = dc.shape[1] // n
        g = _tile(s, max(1, 1024 // n), 1)
        return _mm(name, a, dc, _TN, (k // tko, s // g, m // tmr), a_spec,
                   pl.BlockSpec((tmr, g * n), lambda i, j, kk: (kk, j)),
                   pl.BlockSpec((g, tko, n), lambda i, j, kk: (j, i, 0)), (s, k, n), out_dtype, (g, tko, n),
                   group=g, n=n, b_sel="lanes", o_sel="lead")
    n = dc.shape[1]
    tno = _tile(n, 1024, 128)
    return _mm(name, a, dc, _TN, (k // tko, n // tno, m // tmr), a_spec,
               pl.BlockSpec((tmr, tno), lambda i, j, kk: (kk, j)),
               pl.BlockSpec((tko, tno), lambda i, j, kk: (i, j)), (k, n), out_dtype, (tko, tno))


def rms_fwd(name, x, g):
    t, d = x.shape
    tr = _tile(t, 512)

    def body(x_ref, g_ref, o_ref):
        xf = x_ref[...]
        r = lax.rsqrt(jnp.mean(xf * xf, axis=-1, keepdims=True) + EPS)
        o_ref[...] = (xf * r * g_ref[...]).astype(o_ref.dtype)

    return pl.pallas_call(
        body, name=name, grid=(t // tr,),
        in_specs=[pl.BlockSpec((tr, d), lambda i: (i, 0)), pl.BlockSpec((1, d), lambda i: (0, 0))],
        out_specs=pl.BlockSpec((tr, d), lambda i: (i, 0)),
        out_shape=jax.ShapeDtypeStruct((t, d), BF16), compiler_params=_params(("parallel",)),
    )(x, g.reshape(1, d))


def rms_bwd(name, x, g, dh, dres=None, need_dx=True):
    t, d = x.shape
    tr = _tile(t, 512)

    def body(*refs):
        refs = list(refs)
        x_ref, g_ref, dh_ref = refs[:3]
        r_ref = refs[3] if dres is not None else None
        outs = refs[4:] if dres is not None else refs[3:]
        dx_ref, dg_ref = (outs[0], outs[1]) if need_dx else (None, outs[0])
        i = pl.program_id(0)

        @pl.when(i == 0)
        def _():
            dg_ref[...] = jnp.zeros_like(dg_ref)

        xf = x_ref[...]
        dhf = dh_ref[...].astype(F32)
        r = lax.rsqrt(jnp.mean(xf * xf, axis=-1, keepdims=True) + EPS)
        xh = xf * r
        dg_ref[...] += jnp.sum(dhf * xh, axis=0, keepdims=True)
        if need_dx:
            dxh = dhf * g_ref[...]
            dx = r * (dxh - xh * jnp.mean(dxh * xh, axis=-1, keepdims=True))
            if r_ref is not None:
                dx = dx + r_ref[...]
            dx_ref[...] = dx

    row = pl.BlockSpec((tr, d), lambda i: (i, 0))
    vec = pl.BlockSpec((1, d), lambda i: (0, 0))
    in_specs = [row, vec, row] + ([row] if dres is not None else [])
    args = (x, g.reshape(1, d), dh) + ((dres,) if dres is not None else ())
    out_specs = ([row] if need_dx else []) + [vec]
    out_shape = ([jax.ShapeDtypeStruct((t, d), F32)] if need_dx else []) + [jax.ShapeDtypeStruct((1, d), F32)]
    res = pl.pallas_call(
        body, name=name, grid=(t // tr,), in_specs=in_specs, out_specs=out_specs, out_shape=out_shape,
        compiler_params=_params(("arbitrary",)),
    )(*args)
    return res if need_dx else (None, res[0])


def loss_head(x, g, tgt):
    t, d = x.shape
    tr = _tile(t, 512)

    def body(x_ref, g_ref, t_ref, l_ref, dx_ref, dg_ref):
        i = pl.program_id(0)

        @pl.when(i == 0)
        def _():
            l_ref[...] = jnp.zeros_like(l_ref)
            dg_ref[...] = jnp.zeros_like(dg_ref)

        xf = x_ref[...]
        r = lax.rsqrt(jnp.mean(xf * xf, axis=-1, keepdims=True) + EPS)
        xh = xf * r
        diff = xh * g_ref[...] - t_ref[...]
        l_ref[...] += 0.5 * jnp.sum(jnp.mean(diff * diff, axis=-1, keepdims=True))
        dy = diff * (1.0 / d)
        dg_ref[...] += jnp.sum(dy * xh, axis=0, keepdims=True)
        dxh = dy * g_ref[...]
        dx_ref[...] = r * (dxh - xh * jnp.mean(dxh * xh, axis=-1, keepdims=True))

    row = pl.BlockSpec((tr, d), lambda i: (i, 0))
    vec = pl.BlockSpec((1, d), lambda i: (0, 0))
    return pl.pallas_call(
        body, name="loss_head", grid=(t // tr,), in_specs=[row, vec, row],
        out_specs=[pl.BlockSpec((1, 128), lambda i: (0, 0)), row, vec],
        out_shape=[jax.ShapeDtypeStruct((1, 128), F32), jax.ShapeDtypeStruct((t, d), F32),
                   jax.ShapeDtypeStruct((1, d), F32)],
        compiler_params=_params(("arbitrary",)),
    )(x, g.reshape(1, d), tgt)


def glu_fwd(glu, x, gain):
    t, d = x.shape
    tr = _tile(t, 512)

    def body(v_ref, g_ref, x_ref, n_ref, o_ref, h_ref):
        y = x_ref[...] + v_ref[...] * jax.nn.sigmoid(g_ref[...])
        o_ref[...] = y
        r = lax.rsqrt(jnp.mean(y * y, axis=-1, keepdims=True) + EPS)
        h_ref[...] = (y * r * n_ref[...]).astype(h_ref.dtype)

    row = pl.BlockSpec((tr, d), lambda i: (i, 0))
    return pl.pallas_call(
        body, name="glu_fwd", grid=(t // tr,),
        in_specs=[row, pl.BlockSpec((tr, d), lambda i: (i, 1)), row, pl.BlockSpec((1, d), lambda i: (0, 0))],
        out_specs=[row, row],
        out_shape=[jax.ShapeDtypeStruct((t, d), F32), jax.ShapeDtypeStruct((t, d), BF16)],
        compiler_params=_params(("parallel",)),
    )(glu, glu, x, gain.reshape(1, d))


def glu_bwd(glu, dmix):
    t, d = dmix.shape
    tr = _tile(t, 512)

    def body(v_ref, g_ref, d_ref, o_ref):
        sg = jax.nn.sigmoid(g_ref[...])
        dm = d_ref[...]
        o_ref[:, :d] = (dm * sg).astype(o_ref.dtype)
        o_ref[:, d:] = (dm * v_ref[...] * sg * (1.0 - sg)).astype(o_ref.dtype)

    return pl.pallas_call(
        body, name="glu_bwd", grid=(t // tr,),
        in_specs=[pl.BlockSpec((tr, d), lambda i: (i, 0)), pl.BlockSpec((tr, d), lambda i: (i, 1)),
                  pl.BlockSpec((tr, d), lambda i: (i, 0))],
        out_specs=pl.BlockSpec((tr, 2 * d), lambda i: (i, 0)),
        out_shape=jax.ShapeDtypeStruct((t, 2 * d), BF16), compiler_params=_params(("parallel",)),
    )(glu, glu, dmix)


def _head_masks(shape):
    lane = lax.broadcasted_iota(jnp.int32, shape, 1)
    return lane < SB_HEAD_DIM


def _stack_heads(xf, is_a):
    return jnp.concatenate([jnp.where(is_a, xf, 0.0), jnp.where(is_a, 0.0, xf)], axis=0).astype(MXU_DTYPE)


def _diag_mask(qb, row0, rows):
    row = (lax.broadcasted_iota(jnp.int32, (rows, qb), 0) + row0) & (qb - 1)
    col = lax.broadcasted_iota(jnp.int32, (rows, qb), 1)
    return col < row


def _tri01(qb, pred):
    j = lax.broadcasted_iota(jnp.int32, (qb, qb), 0)
    s = lax.broadcasted_iota(jnp.int32, (qb, qb), 1)
    m = pred(j, s).astype(BF16)
    return jnp.concatenate([m, m], axis=0)


def _split_cat(x):
    hi = x.astype(BF16)
    lo = (x - hi.astype(F32)).astype(BF16)
    return jnp.concatenate([hi, lo], axis=1)


def sb_attn_fwd(proj, order, bsz, seq):
    qb = SB_BLOCK
    nq = seq // qb
    npair = SB_WIDTH // 128
    scale = SB_HEAD_DIM ** -0.5

    def body(q_ref, k_ref, v_ref, order_ref, o_ref, r_ref):
        qi = pl.program_id(2)
        is_a = _head_masks((qb, 128))
        q2 = _stack_heads(q_ref[...] * scale, is_a)
        diag = _diag_mask(qb, 0, 2 * qb)
        upper = _tri01(qb, lambda j, s: j > s)

        def blocks(kbs, acc, run, masked):
            sl = [pl.ds(pl.multiple_of(kb * qb, qb), qb) for kb in kbs]
            zs = [lax.dot_general(q2, k_ref[s, :].astype(MXU_DTYPE), _NT, preferred_element_type=F32) for s in sl]
            lks = [-jnp.maximum(z, 0.0) - jnp.log(1.0 + jnp.exp(-jnp.abs(z))) for z in zs]
            lbs = [lk + z for lk, z in zip(lks, zs)]
            if masked:
                lks = [jnp.where(diag, lk, 0.0) for lk in lks]
            cs = [lax.dot_general(_split_cat(lk), upper, _NN, preferred_element_type=F32) for lk in lks]
            for lk, lb, c, s in zip(lks, lbs, cs, sl):
                w = jnp.exp(lb + (run + c))
                if masked:
                    w = jnp.where(diag, w, 0.0)
                acc = acc + lax.dot_general(w.astype(MXU_DTYPE), v_ref[s, :].astype(MXU_DTYPE), _NN,
                                            preferred_element_type=F32)
                run = run + jnp.sum(lk, axis=1, keepdims=True)
            return acc, run

        carry = blocks([qi], jnp.zeros((2 * qb, 128), F32), jnp.zeros((2 * qb, 1), F32), True)
        carry = lax.cond(qi % 2 == 1, lambda c: blocks([qi - 1], c[0], c[1], False), lambda c: c, carry)
        top = qi - qi % 2
        acc, run = lax.fori_loop(
            0, qi // 2, lambda i, c: blocks([top - 1 - 2 * i, top - 2 - 2 * i], c[0], c[1], False), carry)
        o_ref[...] = jnp.where(is_a, acc[:qb], acc[qb:]).astype(o_ref.dtype)
        r_ref[...] = jnp.where(is_a, run[:qb], run[qb:])

    return pl.pallas_call(
        body, name="sb_attn_fwd", grid=(bsz, npair, nq),
        in_specs=[pl.BlockSpec((qb, 128), lambda b, p, i: (b * nq + i, p)),
                  pl.BlockSpec((seq, 128), lambda b, p, i: (b, npair + p)),
                  pl.BlockSpec((seq, 128), lambda b, p, i: (b, 2 * npair + p)),
                  pl.BlockSpec((1, 128), lambda b, p, i: (0, 0))],
        out_specs=[pl.BlockSpec((qb, 128), lambda b, p, i: (b * nq + i, p)),
                   pl.BlockSpec((qb, 128), lambda b, p, i: (b * nq + i, p))],
        out_shape=[jax.ShapeDtypeStruct((bsz * seq, SB_WIDTH), BF16),
                   jax.ShapeDtypeStruct((bsz * seq, SB_WIDTH), F32)],
        compiler_params=_params(("parallel", "parallel", "arbitrary")),
    )(proj, proj, proj, order)


def sb_attn_bwd(proj, rsum, dcat, bsz, seq):
    qb = SB_BLOCK
    nq = seq // qb
    npair = SB_WIDTH // 128
    scale = SB_HEAD_DIM ** -0.5

    def body(q_ref, k_ref, v_ref, r_ref, do_ref, dq_ref, dk_ref, dv_ref):
        qi = pl.program_id(2)

        @pl.when(qi == 0)
        def _():
            dk_ref[...] = jnp.zeros_like(dk_ref)
            dv_ref[...] = jnp.zeros_like(dv_ref)

        is_a = _head_masks((qb, 128))
        q2 = _stack_heads(q_ref[...] * scale, is_a)
        do2 = _stack_heads(do_ref[...].astype(F32), is_a)
        rf = r_ref[...]
        rtot = jnp.concatenate([rf[:, 0:1], rf[:, SB_HEAD_DIM:SB_HEAD_DIM + 1]], axis=0)
        diag = _diag_mask(qb, 0, 2 * qb)
        incl = _tri01(qb, lambda j, s: j <= s)
        strict = _tri01(qb, lambda j, s: j < s)

        def blocks(kbs, dq, pre, epre, masked):
            sl = [pl.ds(pl.multiple_of(kb * qb, qb), qb) for kb in kbs]
            ks = [k_ref[s, :].astype(MXU_DTYPE) for s in sl]
            vs = [v_ref[s, :].astype(MXU_DTYPE) for s in sl]
            zs = [lax.dot_general(q2, kblk, _NT, preferred_element_type=F32) for kblk in ks]
            dws = [lax.dot_general(do2, vblk, _NT, preferred_element_type=F32) for vblk in vs]
            lks = [-jnp.maximum(z, 0.0) - jnp.log(1.0 + jnp.exp(-jnp.abs(z))) for z in zs]
            lbs = [lk + z for lk, z in zip(lks, zs)]
            if masked:
                lks = [jnp.where(diag, lk, 0.0) for lk in lks]
            ps = [lax.dot_general(_split_cat(lk), incl, _NN, preferred_element_type=F32) for lk in lks]
            ws, es = [], []
            for lk, lb, p, dw in zip(lks, lbs, ps, dws):
                w = jnp.exp(lb + (rtot - (pre + p)))
                if masked:
                    w = jnp.where(diag, w, 0.0)
                ws.append(w)
                es.append(dw * w)
                pre = pre + jnp.sum(lk, axis=1, keepdims=True)
            cs = [lax.dot_general(_split_cat(e), strict, _NN, preferred_element_type=F32) for e in es]
            for e, lb, c, w, kblk, s in zip(es, lbs, cs, ws, ks, sl):
                dz = e - jnp.exp(lb) * (e + (epre + c))
                if masked:
                    dz = jnp.where(diag, dz, 0.0)
                dz = dz.astype(MXU_DTYPE)
                dq = dq + lax.dot_general(dz, kblk, _NN, preferred_element_type=F32)
                dk_ref[s, :] += lax.dot_general(dz, q2, _TN, preferred_element_type=F32)
                dv_ref[s, :] += lax.dot_general(w.astype(MXU_DTYPE), do2, _TN, preferred_element_type=F32)
                epre = epre + jnp.sum(e, axis=1, keepdims=True)
            return dq, pre, epre

        zc = jnp.zeros((2 * qb, 1), F32)
        carry = lax.fori_loop(0, qi // 2, lambda i, c: blocks([2 * i, 2 * i + 1], c[0], c[1], c[2], False),
                              (jnp.zeros((2 * qb, 128), F32), zc, zc))
        carry = lax.cond(qi % 2 == 1, lambda c: blocks([qi - 1], c[0], c[1], c[2], False), lambda c: c, carry)
        dq = blocks([qi], carry[0], carry[1], carry[2], True)[0]
        dq_ref[...] = jnp.where(is_a, dq[:qb], dq[qb:]) * scale

    full = jax.ShapeDtypeStruct((bsz * seq, SB_WIDTH), F32)
    qspec = pl.BlockSpec((qb, 128), lambda b, p, i: (b * nq + i, p))
    return pl.pallas_call(
        body, name="sb_attn_bwd", grid=(bsz, npair, nq),
        in_specs=[qspec,
                  pl.BlockSpec((seq, 128), lambda b, p, i: (b, npair + p)),
                  pl.BlockSpec((seq, 128), lambda b, p, i: (b, 2 * npair + p)),
                  qspec, qspec],
        out_specs=[qspec, pl.BlockSpec((seq, 128), lambda b, p, i: (b, p)),
                   pl.BlockSpec((seq, 128), lambda b, p, i: (b, p))],
        out_shape=[full, full, full],
        compiler_params=_params(("parallel", "parallel", "arbitrary")),
    )(proj, proj, proj, rsum, dcat)


def _window_sums(x, forward):
    n = x.shape[0]
    out = []
    s = x
    for sh in (1, 2, 4, 8):
        s = s + pltpu.roll(s, (n - sh) if forward else sh, 0)
        out.append(s)
    return out


def _pool_counts(tc, c, w):
    t = lax.broadcasted_iota(jnp.int32, (tc, 1), 0) + c * tc
    return jnp.minimum(t + 1, w).astype(F32)


def pool_fwd(proj, pool_w, pool_scale, bsz, seq):
    tc = _tile(seq, 512)
    nc = seq // tc
    hb = tc // POOL_HALO
    ucol = 3

    def body(u_ref, prev_ref, w_ref, s_ref, o_ref):
        c = pl.program_id(1)
        prev = jnp.where(c > 0, prev_ref[...], 0.0)
        x = jnp.concatenate([prev, u_ref[...]], axis=0)
        sums = _window_sums(x, forward=False)
        for g, win in enumerate(POOL_WINDOWS):
            ls = slice(g * POOL_GROUP, (g + 1) * POOL_GROUP)
            pooled = sums[g][POOL_HALO:, ls] / _pool_counts(tc, c, win) - x[POOL_HALO:, ls]
            y = _dot(pooled, w_ref[g], _NN)
            o_ref[:, ls] = (y * s_ref[:, ls]).astype(o_ref.dtype)

    return pl.pallas_call(
        body, name="pool_fwd", grid=(bsz, nc),
        in_specs=[pl.BlockSpec((tc, SB_WIDTH), lambda b, c: (b * nc + c, ucol)),
                  pl.BlockSpec((POOL_HALO, SB_WIDTH), lambda b, c: (jnp.maximum((b * nc + c) * hb - 1, 0), ucol)),
                  pl.BlockSpec((4, POOL_GROUP, POOL_GROUP), lambda b, c: (0, 0, 0)),
                  pl.BlockSpec((1, SB_WIDTH), lambda b, c: (0, 0))],
        out_specs=pl.BlockSpec((tc, SB_WIDTH), lambda b, c: (b * nc + c, 0)),
        out_shape=jax.ShapeDtypeStruct((bsz * seq, SB_WIDTH), BF16),
        compiler_params=_params(("parallel", "parallel")),
    )(proj, proj, pool_w, pool_scale)


def pool_bwd(proj, pool_w, pool_scale, dcat, bsz, seq):
    tc = _tile(seq, 512)
    nc = seq // tc
    hb = tc // POOL_HALO
    nblk = bsz * seq // POOL_HALO
    ucol = 3

    def body(u_ref, prev_ref, dy_ref, nxt_ref, w_ref, s_ref, du_ref, dw_ref, ds_ref):
        b, c = pl.program_id(0), pl.program_id(1)

        @pl.when((b == 0) & (c == 0))
        def _():
            dw_ref[...] = jnp.zeros_like(dw_ref)
            ds_ref[...] = jnp.zeros_like(ds_ref)

        prev = jnp.where(c > 0, prev_ref[...], 0.0)
        x = jnp.concatenate([prev, u_ref[...]], axis=0)
        sums = _window_sums(x, forward=False)
        nxt = jnp.where(c < nc - 1, nxt_ref[...].astype(F32), 0.0)
        dy = jnp.concatenate([dy_ref[...].astype(F32), nxt], axis=0)
        tq = lax.broadcasted_iota(jnp.int32, (tc + POOL_HALO, 1), 0) + c * tc
        for g, win in enumerate(POOL_WINDOWS):
            ls = slice(g * POOL_GROUP, (g + 1) * POOL_GROUP)
            pooled = sums[g][POOL_HALO:, ls] / _pool_counts(tc, c, win) - x[POOL_HALO:, ls]
            y = _dot(pooled, w_ref[g], _NN)
            ds_ref[:, ls] += jnp.sum(dy[:tc, ls] * y, axis=0, keepdims=True)
            dz = dy[:, ls] * s_ref[:, ls]
            dw_ref[g] += _dot(pooled, dz[:tc], _TN)
            dpool = _dot(dz, w_ref[g], _NT)
            dmean = dpool / jnp.minimum(tq + 1, win).astype(F32)
            fsum = _window_sums(dmean, forward=True)[g]
            du_ref[:, ls] = fsum[:tc] - dpool[:tc]

    return pl.pallas_call(
        body, name="pool_bwd", grid=(bsz, nc),
        in_specs=[pl.BlockSpec((tc, SB_WIDTH), lambda b, c: (b * nc + c, ucol)),
                  pl.BlockSpec((POOL_HALO, SB_WIDTH), lambda b, c: (jnp.maximum((b * nc + c) * hb - 1, 0), ucol)),
                  pl.BlockSpec((tc, SB_WIDTH), lambda b, c: (b * nc + c, 1)),
                  pl.BlockSpec((POOL_HALO, SB_WIDTH), lambda b, c: (jnp.minimum((b * nc + c + 1) * hb, nblk - 1), 1)),
                  pl.BlockSpec((4, POOL_GROUP, POOL_GROUP), lambda b, c: (0, 0, 0)),
                  pl.BlockSpec((1, SB_WIDTH), lambda b, c: (0, 0))],
        out_specs=[pl.BlockSpec((tc, SB_WIDTH), lambda b, c: (b * nc + c, 0)),
                   pl.BlockSpec((4, POOL_GROUP, POOL_GROUP), lambda b, c: (0, 0, 0)),
                   pl.BlockSpec((1, SB_WIDTH), lambda b, c: (0, 0))],
        out_shape=[jax.ShapeDtypeStruct((bsz * seq, SB_WIDTH), F32),
                   jax.ShapeDtypeStruct((4, POOL_GROUP, POOL_GROUP), F32),
                   jax.ShapeDtypeStruct((1, SB_WIDTH), F32)],
        compiler_params=_params(("arbitrary", "arbitrary")),
    )(proj, proj, dcat, dcat, pool_w, pool_scale)


def _lbar(lam_re, lam_im, log_dt):
    dt = jnp.exp(log_dt)
    mag = jnp.exp(lam_re * dt)
    ang = lam_im * dt
    return mag * jnp.cos(ang), mag * jnp.sin(ang)


def _bbar(lam_re, lam_im, log_dt, b_re, b_im):
    lb_re, lb_im = _lbar(lam_re, lam_im, log_dt)
    n_re = lb_re - 1.0
    den = lam_re * lam_re + lam_im * lam_im
    coef_re = (n_re * lam_re + lb_im * lam_im) / den
    coef_im = (lb_im * lam_re - n_re * lam_im) / den
    return coef_re * b_re - coef_im * b_im, coef_re * b_im + coef_im * b_re


def _expand01():
    p = lax.broadcasted_iota(jnp.int32, (64, 1024), 0)
    q = lax.broadcasted_iota(jnp.int32, (64, 1024), 1)
    return (lax.shift_right_logical(q, 4) == p).astype(BF16)


def ssm_prep(lam_re, lam_im, log_dt, b_re2, b_im2):
    def body(lr_ref, li_ref, dt_ref, br_ref, bi_ref, ar_ref, ai_ref, bbr_ref, bbi_ref):
        e = _expand01()
        lr, li, dt = lr_ref[...], li_ref[...], dt_ref[...]
        ar_ref[...], ai_ref[...] = _lbar(lr, li, dt)
        bbr_ref[...], bbi_ref[...] = _bbar(_dot_exact01(lr, e), _dot_exact01(li, e), dt, br_ref[...], bi_ref[...])

    s64 = jax.ShapeDtypeStruct((64, 64), F32)
    s1k = jax.ShapeDtypeStruct((64, 1024), F32)
    return pl.pallas_call(body, name="ssm_prep", out_shape=[s64, s64, s1k, s1k], compiler_params=_params())(
        lam_re, lam_im, log_dt, b_re2, b_im2)


def ssm_prep_bwd(lam_re, lam_im, log_dt, b_re2, b_im2, da_re, da_im, dbb_re, dbb_im):
    def body(lr_ref, li_ref, dt_ref, br_ref, bi_ref, dar_ref, dai_ref, dbr_ref, dbi_ref,
             olr_ref, oli_ref, odt_ref, obr_ref, obi_ref):
        e = _expand01()
        lr, li, dt = lr_ref[...], li_ref[...], dt_ref[...]
        _, vjp_a = jax.vjp(_lbar, lr, li, dt)
        g_lr, g_li, g_dt = vjp_a((dar_ref[...], dai_ref[...]))
        _, vjp_b = jax.vjp(_bbar, _dot_exact01(lr, e), _dot_exact01(li, e), dt, br_ref[...], bi_ref[...])
        x_lr, x_li, x_dt, g_br, g_bi = vjp_b((dbr_ref[...], dbi_ref[...]))
        olr_ref[...] = g_lr + _dot_exact01(x_lr, e, _NT)
        oli_ref[...] = g_li + _dot_exact01(x_li, e, _NT)
        odt_ref[...] = g_dt + x_dt
        obr_ref[...] = g_br
        obi_ref[...] = g_bi

    s64 = jax.ShapeDtypeStruct((64, 64), F32)
    s1k = jax.ShapeDtypeStruct((64, 1024), F32)
    return pl.pallas_call(body, name="ssm_prep_bwd",
                          out_shape=[s64, s64, jax.ShapeDtypeStruct((64, 1), F32), s1k, s1k],
                          compiler_params=_params())(
        lam_re, lam_im, log_dt, b_re2, b_im2, da_re, da_im, dbb_re, dbb_im)


def _gelu(y):
    c = math.sqrt(2.0 / math.pi)
    return 0.5 * y * (1.0 + jnp.tanh(c * (y + 0.044715 * y * y * y)))


def _gelu_grad(y):
    c = math.sqrt(2.0 / math.pi)
    th = jnp.tanh(c * (y + 0.044715 * y * y * y))
    return 0.5 * (1.0 + th) + 0.5 * y * (1.0 - th * th) * c * (1.0 + 3.0 * 0.044715 * y * y)


def _cmul(ar, ai, br, bi):
    return ar * br - ai * bi, ar * bi + ai * br


def _scan_tables(ar, ai, reverse, tabs):
    row = lax.broadcasted_iota(jnp.int32, (8, SSM_STATES), 0)
    a1 = (ar, ai)
    a2 = _cmul(*a1, *a1)
    a4 = _cmul(*a2, *a2)
    powers = [a1, a2, _cmul(*a2, *a1), a4]
    powers += [_cmul(*a4, *p) for p in powers]
    for k, (val, sh) in enumerate(((a1, 1), (a2, 2), (a4, 4))):
        keep = (row < 8 - sh) if reverse else (row >= sh)
        tabs[2 * k][...] = jnp.where(keep, val[0], 0.0)
        tabs[2 * k + 1][...] = jnp.where(keep, val[1], 0.0)
    pr = jnp.zeros((8, SSM_STATES), F32)
    pi = jnp.zeros((8, SSM_STATES), F32)
    for r in range(8):
        val = powers[7 - r] if reverse else powers[r]
        pr = jnp.where(row == r, val[0], pr)
        pi = jnp.where(row == r, val[1], pi)
    tabs[6][...] = pr
    tabs[7][...] = pi


def _scan8(xr, xi, tabs, ls, cr, ci, reverse):
    for k, sh in enumerate((1, 2, 4)):
        amt = (8 - sh) if reverse else sh
        sr, si = pltpu.roll(xr, amt, 0), pltpu.roll(xi, amt, 0)
        lr, li = tabs[2 * k][:, ls], tabs[2 * k + 1][:, ls]
        xr, xi = xr + lr * sr - li * si, xi + lr * si + li * sr
    pr, pi = tabs[6][:, ls], tabs[7][:, ls]
    return xr + pr * cr - pi * ci, xi + pr * ci + pi * cr


def _block8(b):
    return pl.ds(pl.multiple_of(b * 8, 8), 8)


def ssm_fwd(u, wt, ct, a_re, a_im, dskip, bsz, seq):
    tc = _tile(seq, 256)
    nc = seq // tc
    ns = SSM_TILE_STATES
    nl = SSM_STATES // SSM_LANES

    def body(u_ref, wt_ref, ct_ref, ar_ref, ai_ref, d_ref, y_ref, gl_ref, hr_ref, hi_ref, sr_ref, si_ref, *tabs):
        b, c = pl.program_id(0), pl.program_id(1)

        @pl.when((b == 0) & (c == 0))
        def _():
            _scan_tables(ar_ref[...], ai_ref[...], False, tabs)

        @pl.when(c == 0)
        def _():
            sr_ref[...] = jnp.zeros_like(sr_ref)
            si_ref[...] = jnp.zeros_like(si_ref)

        uf = u_ref[...]
        for i in range(SSM_TILES):
            bu = _dot(uf[:, i * 128:(i + 1) * 128], wt_ref[i], _NN)
            hr_ref[:, i * ns:(i + 1) * ns] = bu[:, :ns]
            hi_ref[:, i * ns:(i + 1) * ns] = bu[:, ns:]

        def step(blk, carry):
            rows = _block8(blk)
            new = []
            for j in range(nl):
                ls = slice(j * SSM_LANES, (j + 1) * SSM_LANES)
                xr, xi = _scan8(hr_ref[rows, ls], hi_ref[rows, ls], tabs, ls, carry[2 * j], carry[2 * j + 1], False)
                hr_ref[rows, ls] = xr
                hi_ref[rows, ls] = xi
                new += [xr[7:8], xi[7:8]]
            return tuple(new)

        init = []
        for j in range(nl):
            ls = slice(j * SSM_LANES, (j + 1) * SSM_LANES)
            init += [sr_ref[:, ls], si_ref[:, ls]]
        last = lax.fori_loop(0, tc // 8, step, tuple(init), unroll=2)
        for j in range(nl):
            ls = slice(j * SSM_LANES, (j + 1) * SSM_LANES)
            sr_ref[:, ls] = last[2 * j]
            si_ref[:, ls] = last[2 * j + 1]
        for i in range(SSM_TILES):
            hcat = jnp.concatenate([hr_ref[:, i * ns:(i + 1) * ns], hi_ref[:, i * ns:(i + 1) * ns]], axis=1)
            ls = slice(i * 128, (i + 1) * 128)
            y = _dot(hcat, ct_ref[i], _NN) + d_ref[:, ls] * uf[:, ls]
            y_ref[:, ls] = y
            gl_ref[:, ls] = _gelu(y).astype(gl_ref.dtype)

    t = bsz * seq
    row = pl.BlockSpec((tc, D_MODEL), lambda b, c: (b * nc + c, 0))
    st = pl.BlockSpec((tc, SSM_STATES), lambda b, c: (b * nc + c, 0))
    diag = pl.BlockSpec((1, SSM_STATES), lambda b, c: (0, 0))
    return pl.pallas_call(
        body, name="ssm_fwd", grid=(bsz, nc),
        in_specs=[row, pl.BlockSpec((SSM_TILES, 128, 2 * ns), lambda b, c: (0, 0, 0)),
                  pl.BlockSpec((SSM_TILES, 2 * ns, 128), lambda b, c: (0, 0, 0)), diag, diag,
                  pl.BlockSpec((1, D_MODEL), lambda b, c: (0, 0))],
        out_specs=[row, row, st, st],
        out_shape=[jax.ShapeDtypeStruct((t, D_MODEL), F32), jax.ShapeDtypeStruct((t, D_MODEL), BF16),
                   jax.ShapeDtypeStruct((t, SSM_STATES), F32), jax.ShapeDtypeStruct((t, SSM_STATES), F32)],
        scratch_shapes=[pltpu.VMEM((1, SSM_STATES), F32)] * 2 + [pltpu.VMEM((8, SSM_STATES), F32)] * 8,
        compiler_params=_params(("arbitrary", "arbitrary")),
    )(u, wt, ct, a_re, a_im, dskip)


def ssm_bwd(dgl, y, u, h_re, h_im, wt, ct, a_re, a_im, dskip, bsz, seq):
    tc = _tile(seq, 256)
    nc = seq // tc
    nb = tc // 8
    ns = SSM_TILE_STATES
    nl = SSM_STATES // SSM_LANES

    def body(dgl_ref, y_ref, u_ref, hr_ref, hi_ref, pr_ref, pi_ref, wt_ref, ct_ref, ar_ref, ai_ref, d_ref,
             du_ref, dwt_ref, dct_ref, dd_ref, dar_ref, dai_ref, gr_ref, gi_ref, sr_ref, si_ref, ar8_ref, ai8_ref,
             *tabs):
        b, c = pl.program_id(0), pl.program_id(1)

        @pl.when((b == 0) & (c == 0))
        def _():
            for r in (dwt_ref, dct_ref, dd_ref, ar8_ref, ai8_ref):
                r[...] = jnp.zeros_like(r)
            _scan_tables(ar_ref[...], -ai_ref[...], True, tabs)

        @pl.when(c == 0)
        def _():
            sr_ref[...] = jnp.zeros_like(sr_ref)
            si_ref[...] = jnp.zeros_like(si_ref)

        uf = u_ref[...]
        dy = dgl_ref[...].astype(F32) * _gelu_grad(y_ref[...])
        dd_ref[...] += jnp.sum(dy * uf, axis=0, keepdims=True)
        for i in range(SSM_TILES):
            dyi = dy[:, i * 128:(i + 1) * 128]
            dh = _dot(dyi, ct_ref[i], _NT)
            gr_ref[:, i * ns:(i + 1) * ns] = dh[:, :ns]
            gi_ref[:, i * ns:(i + 1) * ns] = dh[:, ns:]
            hcat = jnp.concatenate([hr_ref[:, i * ns:(i + 1) * ns], hi_ref[:, i * ns:(i + 1) * ns]], axis=1)
            dct_ref[i] += _dot(hcat, dyi, _TN)
        row0 = lax.broadcasted_iota(jnp.int32, (8, SSM_LANES), 0) == 0

        def block(blk, carry, before):
            rows = _block8(blk)
            new = []
            for j in range(nl):
                ls = slice(j * SSM_LANES, (j + 1) * SSM_LANES)
                gr, gi = _scan8(gr_ref[rows, ls], gi_ref[rows, ls], tabs, ls, carry[2 * j], carry[2 * j + 1], True)
                gr_ref[rows, ls] = gr
                gi_ref[rows, ls] = gi
                bpr, bpi = before(j)
                hpr = jnp.where(row0, bpr, pltpu.roll(hr_ref[rows, ls], 1, 0))
                hpi = jnp.where(row0, bpi, pltpu.roll(hi_ref[rows, ls], 1, 0))
                ar8_ref[:, ls] += gr * hpr + gi * hpi
                ai8_ref[:, ls] += gi * hpr - gr * hpi
                new += [gr[0:1], gi[0:1]]
            return tuple(new)

        def step(jj, carry):
            blk = nb - 1 - jj
            prev_rows = _block8(blk - 1)

            def before(j):
                ls = slice(j * SSM_LANES, (j + 1) * SSM_LANES)
                return hr_ref[prev_rows, ls][7:8], hi_ref[prev_rows, ls][7:8]

            return block(blk, carry, before)

        init = []
        for j in range(nl):
            ls = slice(j * SSM_LANES, (j + 1) * SSM_LANES)
            init += [sr_ref[:, ls], si_ref[:, ls]]
        carry = lax.fori_loop(0, nb - 1, step, tuple(init))
        first = c == nc - 1

        def before_chunk(j):
            ls = slice(j * SSM_LANES, (j + 1) * SSM_LANES)
            return (jnp.where(first, 0.0, pr_ref[:, ls][7:8]), jnp.where(first, 0.0, pi_ref[:, ls][7:8]))

        last = block(0, carry, before_chunk)
        for j in range(nl):
            ls = slice(j * SSM_LANES, (j + 1) * SSM_LANES)
            sr_ref[:, ls] = last[2 * j]
            si_ref[:, ls] = last[2 * j + 1]
        for i in range(SSM_TILES):
            ls = slice(i * 128, (i + 1) * 128)
            gcat = jnp.concatenate([gr_ref[:, i * ns:(i + 1) * ns], gi_ref[:, i * ns:(i + 1) * ns]], axis=1)
            du_ref[:, ls] = (_dot(gcat, wt_ref[i], _NT) + d_ref[:, ls] * dy[:, ls]).astype(du_ref.dtype)
            dwt_ref[i] += _dot(uf[:, ls], gcat, _TN)

        @pl.when((b == bsz - 1) & (c == nc - 1))
        def _():
            dar_ref[...] = jnp.sum(ar8_ref[...], axis=0, keepdims=True)
            dai_ref[...] = jnp.sum(ai8_ref[...], axis=0, keepdims=True)

    t = bsz * seq
    rev = lambda b, c: (b * nc + (nc - 1 - c), 0)
    row = pl.BlockSpec((tc, D_MODEL), rev)
    st = pl.BlockSpec((tc, SSM_STATES), rev)
    prev = pl.BlockSpec((8, SSM_STATES), lambda b, c: (jnp.maximum((b * nc + (nc - 1 - c)) * nb - 1, 0), 0))
    diag = pl.BlockSpec((1, SSM_STATES), lambda b, c: (0, 0))
    wts = pl.BlockSpec((SSM_TILES, 128, 2 * ns), lambda b, c: (0, 0, 0))
    cts = pl.BlockSpec((SSM_TILES, 2 * ns, 128), lambda b, c: (0, 0, 0))
    vec = pl.BlockSpec((1, D_MODEL), lambda b, c: (0, 0))
    return pl.pallas_call(
        body, name="ssm_bwd", grid=(bsz, nc),
        in_specs=[row, row, row, st, st, prev, prev, wts, cts, diag, diag, vec],
        out_specs=[row, wts, cts, vec, diag, diag],
        out_shape=[jax.ShapeDtypeStruct((t, D_MODEL), BF16),
                   jax.ShapeDtypeStruct((SSM_TILES, 128, 2 * ns), F32),
                   jax.ShapeDtypeStruct((SSM_TILES, 2 * ns, 128), F32),
                   jax.ShapeDtypeStruct((1, D_MODEL), F32),
                   jax.ShapeDtypeStruct((1, SSM_STATES), F32), jax.ShapeDtypeStruct((1, SSM_STATES), F32)],
        scratch_shapes=[pltpu.VMEM((tc, SSM_STATES), F32)] * 2 + [pltpu.VMEM((1, SSM_STATES), F32)] * 2
                       + [pltpu.VMEM((8, SSM_STATES), F32)] * 10,
        compiler_params=_params(("arbitrary", "arbitrary")),
    )(dgl, y, u, h_re, h_im, h_re, h_im, wt, ct, a_re, a_im, dskip)


def _ssm_in_weights(bb_re2, bb_im2):
    eye = jnp.eye(8, dtype=F32)[None, :, None, :, None]

    def one(bb):
        t = bb.reshape(8, 8, 64, 16).transpose(0, 1, 3, 2)
        return (t[:, :, :, None, :] * eye).reshape(8, 128, 512)

    return jnp.concatenate([one(bb_re2), one(bb_im2)], axis=-1).astype(MXU_DTYPE)


def _ssm_in_weights_bwd(dwt):
    eye = jnp.eye(8, dtype=F32)[None, :, None, :, None]

    def one(d):
        t = (d.reshape(8, 8, 16, 8, 64) * eye).sum(axis=3)
        return t.transpose(0, 1, 3, 2).reshape(64, 1024)

    return one(dwt[..., :512]), one(dwt[..., 512:])


def _ssm_out_weights(c_re, c_im):
    eye = jnp.eye(8, dtype=F32)[None, :, None, :, None]

    def one(cc):
        t = cc.reshape(8, 8, 16, 64).transpose(0, 1, 3, 2)
        return (t[:, :, :, None, :] * eye).reshape(8, 512, 128)

    return jnp.concatenate([one(c_re), -one(c_im)], axis=1).astype(MXU_DTYPE)


def _ssm_out_weights_bwd(dct):
    eye = jnp.eye(8, dtype=F32)[None, :, None, :, None]

    def one(d):
        t = (d.reshape(8, 8, 64, 8, 16) * eye).sum(axis=3)
        return t.transpose(0, 1, 3, 2).reshape(64, 16, 64)

    return one(dct[:, :512]), -one(dct[:, 512:])


def _softmax(s):
    m = jnp.max(s, axis=-1, keepdims=True)
    e = jnp.exp(s - m)
    return e / jnp.sum(e, axis=-1, keepdims=True)


def xattn_fwd(q, kv, bsz, seq):
    tq = _tile(seq, 512)
    nq = seq // tq
    scale = XA_HEAD_DIM ** -0.5

    def body(q_ref, k_ref, v_ref, o_ref):
        s = lax.dot_general(q_ref[...], k_ref[...], _NT, preferred_element_type=F32) * scale
        p = _softmax(s)
        o_ref[...] = _dot(p, v_ref[...], _NN).astype(o_ref.dtype)

    qs = pl.BlockSpec((tq, XA_HEAD_DIM), lambda b, h, i: (b * nq + i, h))
    return pl.pallas_call(
        body, name="xattn_fwd", grid=(bsz, XA_HEADS, nq),
        in_specs=[qs, pl.BlockSpec((MEM_LEN, XA_HEAD_DIM), lambda b, h, i: (b, h)),
                  pl.BlockSpec((MEM_LEN, XA_HEAD_DIM), lambda b, h, i: (b, XA_HEADS + h))],
        out_specs=qs, out_shape=jax.ShapeDtypeStruct((bsz * seq, D_MODEL), BF16),
        compiler_params=_params(("parallel", "parallel", "parallel")),
    )(q, kv, kv)


def xattn_bwd(q, kv, do, bsz, seq):
    tq = _tile(seq, 512)
    nq = seq // tq
    scale = XA_HEAD_DIM ** -0.5

    def body(q_ref, k_ref, v_ref, do_ref, dq_ref, dk_ref, dv_ref):
        @pl.when(pl.program_id(2) == 0)
        def _():
            dk_ref[...] = jnp.zeros_like(dk_ref)
            dv_ref[...] = jnp.zeros_like(dv_ref)

        qv, kk, vv, dov = q_ref[...], k_ref[...], v_ref[...], do_ref[...]
        s = lax.dot_general(qv, kk, _NT, preferred_element_type=F32) * scale
        p = _softmax(s)
        dp = lax.dot_general(dov, vv, _NT, preferred_element_type=F32)
        ds = (p * (dp - jnp.sum(dp * p, axis=-1, keepdims=True)) * scale).astype(MXU_DTYPE)
        dq_ref[...] = lax.dot_general(ds, kk, _NN, preferred_element_type=F32).astype(dq_ref.dtype)
        dk_ref[...] += lax.dot_general(ds, qv, _TN, preferred_element_type=F32)
        dv_ref[...] += lax.dot_general(p.astype(MXU_DTYPE), dov, _TN, preferred_element_type=F32)

    qs = pl.BlockSpec((tq, XA_HEAD_DIM), lambda b, h, i: (b * nq + i, h))
    ks = pl.BlockSpec((MEM_LEN, XA_HEAD_DIM), lambda b, h, i: (b, h))
    vs = pl.BlockSpec((MEM_LEN, XA_HEAD_DIM), lambda b, h, i: (b, XA_HEADS + h))
    dkv = jax.ShapeDtypeStruct((bsz * MEM_LEN, D_MODEL), F32)
    dq, dk, dv = pl.pallas_call(
        body, name="xattn_bwd", grid=(bsz, XA_HEADS, nq),
        in_specs=[qs, ks, vs, qs], out_specs=[qs, ks, ks],
        out_shape=[jax.ShapeDtypeStruct((bsz * seq, D_MODEL), BF16), dkv, dkv],
        compiler_params=_params(("parallel", "parallel", "arbitrary")),
    )(q, kv, kv, do)
    return dq, dk, dv


CONV_HALO = 16


def _shifts_down(x, prev):
    h = prev.shape[0]
    ext = jnp.concatenate([prev, x], axis=0)
    return pltpu.roll(ext, 1, 0)[h:], pltpu.roll(ext, 2, 0)[h:]


def _shifts_up(x, nxt):
    rows = x.shape[0]
    n = rows + nxt.shape[0]
    ext = jnp.concatenate([x, nxt], axis=0)
    return pltpu.roll(ext, n - 1, 0)[:rows], pltpu.roll(ext, n - 2, 0)[:rows]


def _conv_taps(u, u1, u2, w, b):
    return b + w[2:3] * u + w[1:2] * u1 + w[0:1] * u2


def conv_fwd(up, cw, cb, bsz, seq):
    tc = _tile(seq, 512)
    nc = seq // tc
    hb = tc // CONV_HALO
    half = N_DEV // 2

    def body(uv_ref, ug_ref, pv_ref, pg_ref, wv_ref, wg_ref, bv_ref, bg_ref, o_ref):
        c = pl.program_id(2)
        pv = jnp.where(c > 0, pv_ref[...].astype(F32), 0.0)
        pg = jnp.where(c > 0, pg_ref[...].astype(F32), 0.0)
        uv, ug = uv_ref[...].astype(F32), ug_ref[...].astype(F32)
        val = _conv_taps(uv, *_shifts_down(uv, pv), wv_ref[...], bv_ref[...])
        gate = _conv_taps(ug, *_shifts_down(ug, pg), wg_ref[...], bg_ref[...])
        o_ref[...] = (gate * jax.nn.sigmoid(gate) * val).astype(o_ref.dtype)

    def cur(off):
        return pl.BlockSpec((None, tc, FF_SHARD), lambda b, j, c: (j + off, b * nc + c, 0))

    def prv(off):
        return pl.BlockSpec((None, CONV_HALO, FF_SHARD), lambda b, j, c: (j + off, jnp.maximum((b * nc + c) * hb - 1, 0), 0))

    def par(rows, off):
        return pl.BlockSpec((None, rows, FF_SHARD), lambda b, j, c: (j + off, 0, 0))

    return pl.pallas_call(
        body, name="conv_fwd", grid=(bsz, half, nc),
        in_specs=[cur(0), cur(half), prv(0), prv(half), par(3, 0), par(3, half), par(1, 0), par(1, half)],
        out_specs=cur(0), out_shape=jax.ShapeDtypeStruct((half, bsz * seq, FF_SHARD), BF16),
        compiler_params=_params(("parallel", "parallel", "parallel")),
    )(up, up, up, up, cw, cw, cb, cb)


def conv_bwd_taps(up, cw, cb, dact, bsz, seq):
    tc = _tile(seq, 512)
    nc = seq // tc
    hb = tc // CONV_HALO
    half = N_DEV // 2

    def body(uv_ref, ug_ref, pv_ref, pg_ref, wv_ref, wg_ref, bv_ref, bg_ref, da_ref,
             dc_ref, dwv_ref, dwg_ref, dbv_ref, dbg_ref):
        b, c = pl.program_id(1), pl.program_id(2)

        @pl.when((b == 0) & (c == 0))
        def _():
            for r in (dwv_ref, dwg_ref, dbv_ref, dbg_ref):
                r[...] = jnp.zeros_like(r)

        pv = jnp.where(c > 0, pv_ref[...].astype(F32), 0.0)
        pg = jnp.where(c > 0, pg_ref[...].astype(F32), 0.0)
        uv, ug = uv_ref[...].astype(F32), ug_ref[...].astype(F32)
        uv1, uv2 = _shifts_down(uv, pv)
        ug1, ug2 = _shifts_down(ug, pg)
        val = _conv_taps(uv, uv1, uv2, wv_ref[...], bv_ref[...])
        gate = _conv_taps(ug, ug1, ug2, wg_ref[...], bg_ref[...])
        sg = jax.nn.sigmoid(gate)
        da = da_ref[...].astype(F32)
        dsilu = da * sg
        dval = dsilu * gate
        dgate = dsilu * val * (1.0 + gate * (1.0 - sg))
        dc_ref[0] = dval.astype(dc_ref.dtype)
        dc_ref[1] = dgate.astype(dc_ref.dtype)
        for dcv, taps, dw_ref, db_ref in ((dval, (uv2, uv1, uv), dwv_ref, dbv_ref),
                                          (dgate, (ug2, ug1, ug), dwg_ref, dbg_ref)):
            db_ref[...] += jnp.sum(dcv, axis=0, keepdims=True)
            for k, u_k in enumerate(taps):
                dw_ref[k:k + 1, :] += jnp.sum(dcv * u_k, axis=0, keepdims=True)

    def cur(off):
        return pl.BlockSpec((None, tc, FF_SHARD), lambda j, b, c: (j + off, b * nc + c, 0))

    def prv(off):
        return pl.BlockSpec((None, CONV_HALO, FF_SHARD), lambda j, b, c: (j + off, jnp.maximum((b * nc + c) * hb - 1, 0), 0))

    def par(rows, off):
        return pl.BlockSpec((None, rows, FF_SHARD), lambda j, b, c: (j + off, 0, 0))

    t = bsz * seq
    hs = jax.ShapeDtypeStruct((2, half, t, FF_SHARD), BF16)
    ws = jax.ShapeDtypeStruct((half, 3, FF_SHARD), F32)
    bs = jax.ShapeDtypeStruct((half, 1, FF_SHARD), F32)
    dc, dwv, dwg, dbv, dbg = pl.pallas_call(
        body, name="conv_bwd_taps", grid=(half, bsz, nc),
        in_specs=[cur(0), cur(half), prv(0), prv(half), par(3, 0), par(3, half), par(1, 0), par(1, half), cur(0)],
        out_specs=[pl.BlockSpec((2, None, tc, FF_SHARD), lambda j, b, c: (0, j, b * nc + c, 0)),
                   par(3, 0), par(3, 0), par(1, 0), par(1, 0)],
        out_shape=[hs, ws, ws, bs, bs],
        compiler_params=_params(("parallel", "arbitrary", "arbitrary")),
    )(up, up, up, up, cw, cw, cb, cb, dact)
    return (dc.reshape(N_DEV, t, FF_SHARD), jnp.concatenate([dwv, dwg], axis=0),
            jnp.concatenate([dbv, dbg], axis=0))


def conv_bwd_input(dconv, cw, bsz, seq):
    tc = _tile(seq, 1024)
    nc = seq // tc
    hb = tc // CONV_HALO
    nblk = bsz * seq // CONV_HALO

    def body(d_ref, n_ref, w_ref, o_ref):
        c = pl.program_id(2)
        nxt = jnp.where(c < nc - 1, n_ref[...].astype(F32), 0.0)
        d = d_ref[...].astype(F32)
        d1, d2 = _shifts_up(d, nxt)
        w = w_ref[...]
        o_ref[...] = (w[2:3] * d + w[1:2] * d1 + w[0:1] * d2).astype(o_ref.dtype)

    cur = pl.BlockSpec((None, tc, FF_SHARD), lambda j, b, c: (j, b * nc + c, 0))
    return pl.pallas_call(
        body, name="conv_bwd_input", grid=(N_DEV, bsz, nc),
        in_specs=[cur, pl.BlockSpec((None, CONV_HALO, FF_SHARD),
                                    lambda j, b, c: (j, jnp.minimum((b * nc + c + 1) * hb, nblk - 1), 0)),
                  pl.BlockSpec((None, 3, FF_SHARD), lambda j, b, c: (j, 0, 0))],
        out_specs=cur, out_shape=jax.ShapeDtypeStruct(dconv.shape, BF16),
        compiler_params=_params(("parallel", "parallel", "parallel")),
    )(dconv, dconv, cw)


def _my_index():
    return 4 * lax.axis_index("x") + 2 * lax.axis_index("y") + lax.axis_index("c")


def _peer(k):
    return (lax.axis_index("x") ^ ((k >> 2) & 1), lax.axis_index("y") ^ ((k >> 1) & 1),
            lax.axis_index("c") ^ (k & 1))


_HBM = pl.BlockSpec(memory_space=pltpu.HBM)
_SEM = pl.BlockSpec(memory_space=pltpu.SEMAPHORE)
_DATAFLOW = pltpu.SideEffectType.DATAFLOW_SIDE_EFFECTING


def _split_copies(gather, src_ref, land_ref, send_sems, recv_sems, local_sem):
    me = _my_index()

    def part(j):
        return src_ref if gather else src_ref.at[j]

    local = pltpu.make_async_copy(part(me), land_ref.at[me], local_sem)
    sends = [pltpu.make_async_remote_copy(
        src_ref=part(me ^ k), dst_ref=land_ref.at[me], send_sem=send_sems.at[k - 1], recv_sem=recv_sems.at[k - 1],
        device_id=_peer(k), device_id_type=pl.DeviceIdType.MESH) for k in range(1, N_DEV)]
    recvs = [pltpu.make_async_remote_copy(
        src_ref=part(me ^ k), dst_ref=land_ref.at[me ^ k], send_sem=send_sems.at[k - 1], recv_sem=recv_sems.at[k - 1],
        device_id=_peer(k), device_id_type=pl.DeviceIdType.MESH) for k in range(1, N_DEV)]
    return local, sends, recvs


def split_start(name, srcs, gather):
    n = len(srcs)
    lands = [((N_DEV,) + s.shape) if gather else s.shape for s in srcs]

    def body(*refs):
        ins, outs = refs[:2 * n], refs[2 * n:]
        for i in range(n):
            local, sends, _ = _split_copies(gather, ins[i], ins[n + i], *outs[3 * i:3 * i + 3])
            local.start()
            for cp in sends:
                cp.start()
        outs[-1][...] = jnp.zeros_like(outs[-1])

    dma7 = pltpu.SemaphoreType.DMA((N_DEV - 1,))
    out = pl.pallas_call(
        body, name=name,
        out_shape=(dma7, dma7, pltpu.SemaphoreType.DMA(())) * n
                  + tuple(pltpu.HBM(s.shape, s.dtype) for s in srcs)
                  + tuple(pltpu.HBM(shape, s.dtype) for shape, s in zip(lands, srcs))
                  + (jax.ShapeDtypeStruct((8, 128), F32),),
        in_specs=(_HBM,) * (2 * n),
        out_specs=(_SEM,) * (3 * n) + (_HBM,) * (2 * n) + (pl.BlockSpec(memory_space=pltpu.VMEM),),
        input_output_aliases={i: 3 * n + i for i in range(2 * n)},
        compiler_params=pltpu.CompilerParams(has_side_effects=_DATAFLOW),
    )(*[pltpu.with_memory_space_constraint(s, pltpu.HBM) for s in srcs],
      *[pltpu.with_memory_space_constraint(lax.empty(shape, s.dtype), pltpu.HBM) for shape, s in zip(lands, srcs)])
    handles = [tuple(out[3 * i:3 * i + 3]) + (out[3 * n + i], out[4 * n + i]) for i in range(n)]
    return handles, out[-1][0, 0]


def split_wait(name, handles, after, gather):
    send_sems, recv_sems, local_sem, src_thru, land_thru = handles

    def body(src_ref, land_ref, send_sems, recv_sems, local_sem, after_ref, src_dead, got_ref, token):
        local, sends, recvs = _split_copies(gather, src_ref, land_ref, send_sems, recv_sems, local_sem)
        local.wait()
        for cp in recvs:
            cp.wait_send()
            cp.wait_recv()
        token[...] = jnp.zeros_like(token)

    out = pl.pallas_call(
        body, name=name,
        out_shape=(pltpu.HBM(src_thru.shape, src_thru.dtype), pltpu.HBM(land_thru.shape, land_thru.dtype),
                   jax.ShapeDtypeStruct((8, 128), F32)),
        in_specs=(_HBM, _HBM, _SEM, _SEM, _SEM, pl.BlockSpec(memory_space=pl.ANY)),
        out_specs=(_HBM, _HBM, pl.BlockSpec(memory_space=pltpu.VMEM)),
        input_output_aliases={0: 0, 1: 1},
        compiler_params=pltpu.CompilerParams(has_side_effects=_DATAFLOW),
    )(src_thru, land_thru, send_sems, recv_sems, local_sem, after)
    return out[1], out[2][0, 0]


def sum_parts(name, r):
    _, rows, cols = r.shape

    def body(r_ref, o_ref):
        acc = r_ref[0].astype(F32)
        for s in range(1, N_DEV):
            acc = acc + r_ref[s].astype(F32)
        o_ref[...] = acc

    return pl.pallas_call(body, name=name, out_shape=jax.ShapeDtypeStruct((rows, cols), F32),
                          compiler_params=_params())(r)


def adamw(name, w, m, v, parts=None, g=None, layer=0, into=None, order=None):
    _, rows, cols = w.shape
    br = _tile(rows, 256, 16)
    c1 = 1.0 / (1.0 - ADAM_B1 ** ADAM_STEP)
    c2 = 1.0 / (1.0 - ADAM_B2 ** ADAM_STEP)

    def body(g_ref, w_ref, m_ref, v_ref, *rest):
        og_ref, od_ref, om_ref, ov_ref = rest[-4:]
        if parts is None:
            gs = g_ref[...]
        else:
            gs = g_ref[0].astype(F32)
            for s in range(1, N_DEV):
                gs = gs + g_ref[s].astype(F32)
        mn = ADAM_B1 * m_ref[...] + (1.0 - ADAM_B1) * gs
        vn = ADAM_B2 * v_ref[...] + (1.0 - ADAM_B2) * (gs * gs)
        og_ref[...] = gs
        om_ref[...] = mn
        ov_ref[...] = vn
        od_ref[...] = -ADAM_LR * ((mn * c1) / (jnp.sqrt(vn * c2) + ADAM_EPS) + ADAM_WD * w_ref[...])

    blk = pl.BlockSpec((None, br, cols), lambda i: (layer, i, 0))
    if parts is None:
        gspec = pl.BlockSpec((br, cols), lambda i: (i, 0))
    else:
        gspec = pl.BlockSpec((N_DEV, br, cols), lambda i: (0, i, 0))
    earlier = [] if into is None else list(into)
    behind = [] if order is None else [order]
    return pl.pallas_call(
        body, name=name, grid=(rows // br,),
        in_specs=[gspec, blk, blk, blk] + [pl.BlockSpec(memory_space=pl.ANY)] * len(earlier)
                 + [pl.BlockSpec((1, 128), lambda i: (0, 0))] * len(behind),
        out_specs=[blk] * 4, out_shape=[jax.ShapeDtypeStruct(w.shape, F32)] * 4,
        input_output_aliases={4 + k: k for k in range(len(earlier))},
        compiler_params=_params(("parallel",)),
    )(g if parts is None else parts, w, m, v, *earlier, *behind)


SMALL = ("norm_mix", "norm_xattn", "norm_ffn", "norm_mem", "norm_final", "pool_w", "pool_scale",
         "ssm_lam_re", "ssm_lam_im", "ssm_log_dt", "ssm_b_re", "ssm_b_im", "ssm_c_re", "ssm_c_im",
         "ffn_conv_b", "ssm_d", "ffn_conv_w")
SMALL_SHARDED = {"ssm_d": 1, "ffn_conv_w": 2}
BIG = ("ab_w_in", "ab_w_out", "ssm_w_in", "ssm_w_glu", "xa_w_q", "xa_w_kv", "xa_w_o", "ffn_w_up", "ffn_w_down")
WEIGHTS = ("norm_mix", "norm_xattn", "norm_ffn", "norm_mem", "norm_final", "ab_w_in", "pool_w", "pool_scale",
           "ab_w_out", "ssm_w_in", "ssm_lam_re", "ssm_lam_im", "ssm_log_dt", "ssm_b_re", "ssm_b_im", "ssm_c_re",
           "ssm_c_im", "ssm_d", "ssm_w_glu", "xa_w_q", "xa_w_kv", "xa_w_o", "ffn_w_up", "ffn_conv_w", "ffn_conv_b",
           "ffn_w_down")


def _rows8(g):
    return g.reshape(N_DEV, g.size // (N_DEV * D_MODEL), D_MODEL)


def _square(a):
    return a.reshape(D_MODEL, D_MODEL)


_LAYOUT = {"ab_w_out": _square, "ssm_w_in": _square, "xa_w_q": _square, "xa_w_o": _square,
           "ffn_w_down": lambda a: a.reshape(N_DEV // 2, FF_SHARD, D_MODEL)}
GATHER_ORDER = (("ab_w_in", 0), ("ffn_conv_w", None), ("ssm_d", None), ("ab_w_out", 0), ("xa_w_q", 0),
                ("xa_w_kv", 0), ("xa_w_o", 0), ("ffn_w_up", 0), ("ffn_w_down", 0), ("ffn_w_up", 1),
                ("ffn_w_down", 1), ("ssm_w_in", 0), ("ssm_w_glu", 0), ("xa_w_q", 1), ("xa_w_kv", 1), ("xa_w_o", 1))
GATHER_AHEAD = 7
GATHER_BATCHES = (3, 8, 11, 13, 16)


class _Step:
    def __init__(self, master, small):
        self.master, self.small = master, small
        self.pending, self.gathers, self.weights, self.sent, self.queued = [], {}, {}, [], []

    def follow(self, v):
        for z in self.pending:
            v = v + z
        self.pending = []
        return v

    def start_gathers(self, upto, zero):
        upto = min(end for end in GATHER_BATCHES if end >= min(upto, len(GATHER_ORDER)))
        todo = GATHER_ORDER[len(self.gathers):upto]
        if not todo:
            return
        shards = []
        for n, l in todo:
            if l is None:
                shards.append(self.master[n] + zero)
            else:
                shards.append((self.master[n][l] + zero).astype(MXU_DTYPE))
        handles, z = split_start(f"ags_{len(self.gathers)}", shards, gather=True)
        self.gathers.update(zip(todo, handles))
        self.pending.append(z)

    def weight(self, n, l, after):
        if (n, l) not in self.weights:
            full, z = split_wait(f"agw_{n}{'' if l is None else l}", self.gathers[(n, l)], after, gather=True)
            self.weights[(n, l)] = _LAYOUT.get(n, lambda a: a)(full)
            self.start_gathers(GATHER_ORDER.index((n, l)) + 1 + GATHER_AHEAD, z)
        return self.weights[(n, l)]

    def send_grad(self, n, l, part, flush=True):
        self.queued.append((n, l, part))
        if flush:
            handles, z = split_start(f"xs_{n}{l}", [p for _, _, p in self.queued], gather=False)
            self.sent += [(qn, ql, h) for (qn, ql, _), h in zip(self.queued, handles)]
            self.queued = []
            self.pending.append(z)


def _layer_tail(st, l, x_in, hq, mem_n, acts, next_gain=None):
    bsz, seq = acts["bsz"], acts["seq"]
    p = st.small
    q = mm_nn(f"xa_q{l}", hq, st.weight("xa_w_q", l, x_in))
    kv = mm_nn_bs(f"xa_kv{l}", mem_n, st.weight("xa_w_kv", l, x_in))
    o = xattn_fwd(q, kv, bsz, seq)
    x_mid, hf = mm_nn(f"xa_o{l}", o, st.weight("xa_w_o", l, o), res=x_in, out_dtype=F32,
                      norm_gain=st.follow(p["norm_ffn"][l]))
    up = mm_nn_bs(f"ffn_up{l}", hf, st.weight("ffn_w_up", l, x_mid), stacked_out=True)
    conv_w = st.weight("ffn_conv_w", None, x_mid)[:, l]
    act = conv_fwd(up, conv_w, p["ffn_conv_b"][l], bsz, seq)
    w_down = st.weight("ffn_w_down", l, act)
    if next_gain is None:
        x_out, h_next = mm_as_nn(f"ffn_down{l}", act, w_down, res=x_mid), None
    else:
        x_out, h_next = mm_as_nn(f"ffn_down{l}", act, w_down, res=x_mid, norm_gain=st.follow(next_gain))
    acts[l].update(x_in=x_in, hq=hq, q=q, kv=kv, o=o, x_mid=x_mid, hf=hf, up=up, act=act)
    return x_out, h_next


def _layer_tail_bwd(st, l, dx, mem_n, acts, grads):
    a = acts[l]
    bsz, seq = acts["bsz"], acts["seq"]
    p = st.small
    dact = mm_nt_os(f"d_act{l}", dx, st.weight("ffn_w_down", l, dx))
    st.send_grad("ffn_w_down", l, _rows8(mm_tn(f"g_ffn_down{l}", a["act"], dx, a_stacked=True)), flush=False)
    conv_w = st.weight("ffn_conv_w", None, dx)[:, l]
    dconv, dcw, dcb = conv_bwd_taps(a["up"], conv_w, p["ffn_conv_b"][l], dact, bsz, seq)
    grads["ffn_conv_w"][l] = dcw
    grads["ffn_conv_b"][l] = dcb
    dup = conv_bwd_input(dconv, conv_w, bsz, seq)
    dx_mid, grads["norm_ffn"][l] = mm_nt_bs(f"d_hf{l}", dup, st.weight("ffn_w_up", l, dx), dc_stacked=True,
                                            rms=(a["x_mid"], st.follow(p["norm_ffn"][l]), dx))
    st.send_grad("ffn_w_up", l, mm_tn(f"g_ffn_up{l}", a["hf"], dup, dc_stacked=True))
    do = mm_nt(f"d_o{l}", dx_mid, st.weight("xa_w_o", l, dx))
    st.send_grad("xa_w_o", l, _rows8(mm_tn(f"g_xa_o{l}", a["o"], dx_mid)), flush=False)
    dq, dk, dv = xattn_bwd(a["q"], a["kv"], do, bsz, seq)
    dkv = jnp.concatenate([dk, dv], axis=1).astype(BF16)
    dx_in, grads["norm_xattn"][l] = mm_nt(f"d_hq{l}", dq, st.weight("xa_w_q", l, dx),
                                          rms=(a["x_in"], st.follow(p["norm_xattn"][l]), dx_mid))
    st.send_grad("xa_w_q", l, _rows8(mm_tn(f"g_xa_q{l}", a["hq"], dq)), flush=False)
    dmem_n = mm_nt_bs(f"d_memn{l}", dkv, st.weight("xa_w_kv", l, dx), out_dtype=F32)
    st.send_grad("xa_w_kv", l, mm_tn(f"g_xa_kv{l}", mem_n, dkv, dc_cols=2 * D_MODEL // N_DEV))
    return dx_in, dmem_n


def kernel(x, mem, norm_mix, norm_xattn, norm_ffn, norm_mem, norm_final, ab_w_in, pool_w, pool_scale, ab_w_out, ssm_w_in, ssm_lam_re, ssm_lam_im, ssm_log_dt, ssm_b_re, ssm_b_im, ssm_c_re, ssm_c_im, ssm_d, ssm_w_glu, xa_w_q, xa_w_kv, xa_w_o, ffn_w_up, ffn_conv_w, ffn_conv_b, ffn_w_down, loss_target, m_norm_mix, m_norm_xattn, m_norm_ffn, m_norm_mem, m_norm_final, m_ab_w_in, m_pool_w, m_pool_scale, m_ab_w_out, m_ssm_w_in, m_ssm_lam_re, m_ssm_lam_im, m_ssm_log_dt, m_ssm_b_re, m_ssm_b_im, m_ssm_c_re, m_ssm_c_im, m_ssm_d, m_ssm_w_glu, m_xa_w_q, m_xa_w_kv, m_xa_w_o, m_ffn_w_up, m_ffn_conv_w, m_ffn_conv_b, m_ffn_w_down, v_norm_mix, v_norm_xattn, v_norm_ffn, v_norm_mem, v_norm_final, v_ab_w_in, v_pool_w, v_pool_scale, v_ab_w_out, v_ssm_w_in, v_ssm_lam_re, v_ssm_lam_im, v_ssm_log_dt, v_ssm_b_re, v_ssm_b_im, v_ssm_c_re, v_ssm_c_im, v_ssm_d, v_ssm_w_glu, v_xa_w_q, v_xa_w_kv, v_xa_w_o, v_ffn_w_up, v_ffn_conv_w, v_ffn_conv_b, v_ffn_w_down):
    given = dict(locals())
    master = {n: given[n] for n in WEIGHTS}
    mom1 = {n: given["m_" + n] for n in WEIGHTS}
    mom2 = {n: given["v_" + n] for n in WEIGHTS}
    bsz, seq, d = x.shape
    t = bsz * seq
    me = _my_index()

    st = _Step(master, {"norm_xattn": norm_xattn, "norm_ffn": norm_ffn,
                        "ffn_conv_b": [ffn_conv_b[l].reshape(N_DEV, 1, FF_SHARD) for l in range(2)]})
    st.start_gathers(1, 0.0)
    zero = st.follow(jnp.zeros((), F32))

    acts = {"bsz": bsz, "seq": seq, 0: {}, 1: {}}
    x0 = x.reshape(t, d)
    mem2 = mem.reshape(bsz * MEM_LEN, d)
    mem_n = rms_fwd("rms_mem", mem2, norm_mem + zero)
    pscale = pool_scale.reshape(1, SB_WIDTH)

    h0 = rms_fwd("rms_mix0", x0, norm_mix[0] + zero)
    w_in = st.weight("ab_w_in", 0, h0)
    proj = mm_nn_bs("ab_in", h0, w_in, out_dtype=F32)
    a_out, rsum = sb_attn_fwd(proj, st.follow(jnp.zeros((1, 128), F32)), bsz, seq)
    p_out = pool_fwd(proj, pool_w[0], pscale, bsz, seq)
    w_out = st.weight("ab_w_out", 0, a_out)
    x1 = mm_nn("ab_out_a", a_out, w_out, res=x0, out_dtype=F32)
    x1, hq0 = mm_nn("ab_out_p", p_out, w_out, res=x1, koff=SB_WIDTH, out_dtype=F32,
                    norm_gain=st.follow(norm_xattn[0]))
    x3, h1 = _layer_tail(st, 0, x1, hq0, mem_n, acts, next_gain=norm_mix[1])

    b_re2 = ssm_b_re.reshape(64, 1024)
    b_im2 = ssm_b_im.reshape(64, 1024)
    log_dt = ssm_log_dt.reshape(64, 1)
    lb_re, lb_im, bb_re2, bb_im2 = ssm_prep(ssm_lam_re[0], ssm_lam_im[0], log_dt, b_re2, b_im2)
    wt = _ssm_in_weights(bb_re2, bb_im2)
    ct = _ssm_out_weights(ssm_c_re[0], ssm_c_im[0])
    a_re = lb_re.reshape(1, SSM_STATES)
    a_im = lb_im.reshape(1, SSM_STATES)
    u = mm_nn("ssm_in", h1, st.weight("ssm_w_in", 0, x3), out_dtype=F32)
    dskip = st.weight("ssm_d", None, x3).reshape(1, D_MODEL)
    y, gl, h_re, h_im = ssm_fwd(u, wt, ct, a_re, a_im, dskip, bsz, seq)
    glu = mm_nn_bs("ssm_glu", gl, st.weight("ssm_w_glu", 0, gl), out_dtype=F32)
    x4, hq1 = glu_fwd(glu, x3, st.follow(norm_xattn[1]))
    x6, _ = _layer_tail(st, 1, x4, hq1, mem_n, acts)

    loss_row, dx, g_norm_final = loss_head(x6, norm_final, loss_target.reshape(t, d))
    loss = lax.psum(loss_row[0, 0], MESH_AXES)

    grads = {n: [None, None] for n in ("ffn_conv_w", "ffn_conv_b", "norm_ffn", "norm_xattn", "norm_mix")}
    dx4, dmem_1 = _layer_tail_bwd(st, 1, dx, mem_n, acts, grads)
    dglu = glu_bwd(glu, dx4)
    dgl = mm_nt_bs("d_gl", dglu, st.weight("ssm_w_glu", 0, dx))
    st.send_grad("ssm_w_glu", 0, mm_tn("g_ssm_glu", gl, dglu, dc_cols=2 * D_MODEL // N_DEV), flush=False)
    du, dwt, dct, g_dskip, da_re, da_im = ssm_bwd(dgl, y, u, h_re, h_im, wt, ct, a_re, a_im, dskip, bsz, seq)
    dbb_re, dbb_im = _ssm_in_weights_bwd(dwt)
    g_c_re, g_c_im = _ssm_out_weights_bwd(dct)
    g_lam_re, g_lam_im, g_log_dt, g_b_re, g_b_im = ssm_prep_bwd(
        ssm_lam_re[0], ssm_lam_im[0], log_dt, b_re2, b_im2, da_re.reshape(64, 64), da_im.reshape(64, 64),
        dbb_re, dbb_im)
    dx3, grads["norm_mix"][1] = mm_nt("d_h1", du, st.weight("ssm_w_in", 0, dx),
                                      rms=(x3, st.follow(norm_mix[1]), dx4))
    st.send_grad("ssm_w_in", 0, _rows8(mm_tn("g_ssm_in", h1, du)))

    dx1, dmem_0 = _layer_tail_bwd(st, 0, dx3, mem_n, acts, grads)
    dcat = mm_nt("d_cat", dx1, st.weight("ab_w_out", 0, dx))
    st.send_grad("ab_w_out", 0, _rows8(jnp.concatenate(
        [mm_tn("g_ab_out_a", a_out, dx1), mm_tn("g_ab_out_p", p_out, dx1)], axis=0)), flush=False)
    dq, dk, dv = sb_attn_bwd(proj, rsum, dcat, bsz, seq)
    dpu, g_pool_w, g_pool_scale = pool_bwd(proj, pool_w[0], st.follow(pscale), dcat, bsz, seq)
    dproj = jnp.concatenate([dq, dk, dv, dpu], axis=1).astype(BF16)
    st.send_grad("ab_w_in", 0, mm_tn("g_ab_in", h0, dproj, dc_cols=2 * D_MODEL // N_DEV))
    dx0, grads["norm_mix"][0] = mm_nt_bs("d_h0", dproj, st.weight("ab_w_in", 0, dx),
                                         rms=(x0, st.follow(norm_mix[0]), dx1))
    _, g_norm_mem = rms_bwd("rms_mem_bwd", mem2, norm_mem, dmem_0 + dmem_1, need_dx=False)

    small_g = {
        "norm_mix": jnp.stack([g[0] for g in grads["norm_mix"]]),
        "norm_xattn": jnp.stack([g[0] for g in grads["norm_xattn"]]),
        "norm_ffn": jnp.stack([g[0] for g in grads["norm_ffn"]]),
        "norm_mem": g_norm_mem[0], "norm_final": g_norm_final[0],
        "pool_w": g_pool_w[None], "pool_scale": g_pool_scale,
        "ssm_lam_re": g_lam_re[None], "ssm_lam_im": g_lam_im[None], "ssm_log_dt": g_log_dt.reshape(1, 64),
        "ssm_b_re": g_b_re.reshape(1, 64, 64, 16), "ssm_b_im": g_b_im.reshape(1, 64, 64, 16),
        "ssm_c_re": g_c_re[None], "ssm_c_im": g_c_im[None],
        "ffn_conv_b": jnp.stack([g.reshape(2 * D_FF) for g in grads["ffn_conv_b"]]),
        "ssm_d": g_dskip,
        "ffn_conv_w": jnp.stack([g.transpose(1, 0, 2).reshape(3, 2 * D_FF) for g in grads["ffn_conv_w"]]),
    }
    sizes = [int(small_g[n].size) for n in SMALL]
    total = sum(sizes)
    rows8 = -(-total // (N_DEV * 128 * 8)) * 8
    flat = jnp.concatenate([small_g[n].reshape(-1).astype(F32) for n in SMALL]
                           + [jnp.zeros((N_DEV * rows8 * 128 - total,), F32)])
    (in_flight,), z = split_start("xs_small", [flat.reshape(N_DEV, rows8, 128)], gather=False)
    st.pending.append(z)
    stepped, last = {}, dx0
    for i, (n, l, handles) in enumerate(st.sent):
        if i == len(st.sent) // 2:
            recv, _ = split_wait("xw_small", in_flight, last, gather=False)
            (in_flight,), z = split_start("ags_small", [sum_parts("sum_small", recv)], gather=True)
            st.pending.append(z)
        recv, _ = split_wait(f"xw_{n}{l}", handles, dx0, gather=False)
        shape3 = (master[n].shape[0],) + recv.shape[1:]
        stepped[n] = adamw(f"adamw_{n}{l}", master[n].reshape(shape3), mom1[n].reshape(shape3),
                           mom2[n].reshape(shape3), parts=recv, layer=l, into=stepped.get(n),
                           order=st.follow(jnp.zeros((1, 128), F32)))
        last = stepped[n][0]
    out_g, out_d, out_m, out_v = ({n: stepped[n][k].reshape(master[n].shape) for n in BIG} for k in range(4))
    summed = split_wait("agw_small", in_flight, last, gather=True)[0].reshape(-1)

    def local_part(name, a):
        ax = SMALL_SHARDED.get(name)
        if ax is None:
            return a
        n_loc = a.shape[ax] // N_DEV
        return lax.dynamic_slice_in_dim(a, me * n_loc, n_loc, axis=ax)

    off = 0
    for n, sz in zip(SMALL, sizes):
        g_n = local_part(n, summed[off:off + sz].reshape(small_g[n].shape))
        off += sz
        cols = g_n.shape[-1] if g_n.shape[-1] >= 128 or g_n.ndim < 3 else g_n.shape[-1] * g_n.shape[-2]
        shape3 = (1, g_n.size // cols, cols)
        res = adamw("adamw_" + n, master[n].reshape(shape3), mom1[n].reshape(shape3), mom2[n].reshape(shape3),
                    g=g_n.reshape(shape3[1:]))
        for dst, r in zip((out_g, out_d, out_m, out_v), res):
            dst[n] = r.reshape(master[n].shape)

    return (loss, dx0.reshape(bsz, seq, d), *[out_g[n] for n in WEIGHTS], *[out_d[n] for n in WEIGHTS],
            *[out_m[n] for n in WEIGHTS], *[out_v[n] for n in WEIGHTS])
```

```python
import math

import jax
import jax.numpy as jnp
from jax import lax
from jax.experimental import pallas as pl
from jax.experimental.pallas import tpu as pltpu

F32 = jnp.float32
BF16 = jnp.bfloat16
MXU_DTYPE = jnp.bfloat16
N_DEV = 8
MESH_AXES = ("x", "y", "c")

D_MODEL = 1024
SB_HEAD_DIM = 64
SB_WIDTH = 512
SB_BLOCK = 256
POOL_WINDOWS = (2, 4, 8, 16)
POOL_GROUP = 128
POOL_HALO = 16
SSM_TILES = 8
SSM_TILE_STATES = 512
SSM_STATES = 4096
SSM_LANES = 1024
MEM_LEN = 256
XA_HEADS = 4
XA_HEAD_DIM = 256
D_FF = 2816
FF_SHARD = 704
EPS = 1e-6
ADAM_LR = 0.001
ADAM_B1 = 0.9
ADAM_B2 = 0.999
ADAM_EPS = 1e-08
ADAM_WD = 0.01
ADAM_STEP = 10
VMEM_LIMIT = 56 * 1024 * 1024

_NN = (((1,), (0,)), ((), ()))
_NT = (((1,), (1,)), ((), ()))
_TN = (((0,), (0,)), ((), ()))


def _params(sem=None):
    if sem is None:
        return pltpu.CompilerParams(vmem_limit_bytes=VMEM_LIMIT)
    return pltpu.CompilerParams(dimension_semantics=sem, vmem_limit_bytes=VMEM_LIMIT)


def _tile(n, pref, mult=8):
    if n <= pref:
        return n
    for t in range(pref, 0, -1):
        if n % t == 0 and t % mult == 0:
            return t
    return n


def _dot(a, b, dims):
    return lax.dot_general(a.astype(MXU_DTYPE), b.astype(MXU_DTYPE), dims, preferred_element_type=F32)


def _dot_exact01(x, m01, dims=_NN):
    x1 = x.astype(BF16)
    r1 = x - x1.astype(F32)
    x2 = r1.astype(BF16)
    x3 = (r1 - x2.astype(F32)).astype(BF16)
    m = m01.astype(BF16)
    out = lax.dot_general(x1, m, dims, preferred_element_type=F32)
    out = out + lax.dot_general(x2, m, dims, preferred_element_type=F32)
    return out + lax.dot_general(x3, m, dims, preferred_element_type=F32)


def _mm(name, a, b, dims, grid, a_spec, b_spec, o_spec, out_shape, out_dtype, acc_shape, res=None, r_spec=None,
        group=1, n=None, a_sel="full", b_sel="full", o_sel="full", norm_gain=None, rms=None):
    nk = grid[2]
    if out_dtype is None:
        out_dtype = BF16
    n_out = out_shape[-1]
    vec = pl.BlockSpec((1, n_out), lambda i, j, kk: (0, 0))

    def at(sel, s):
        if sel == "lead":
            return (s,)
        if sel == "lanes":
            return (slice(None), slice(s * n, (s + 1) * n))
        return (Ellipsis,)

    extra = [] if res is None else [(res, r_spec)]
    if norm_gain is not None:
        extra.append((norm_gain.reshape(1, n_out), vec))
    if rms is not None:
        extra += [(rms[0], o_spec), (rms[1].reshape(1, n_out), vec), (rms[2], o_spec)]
    n_in = 2 + len(extra)
    if rms is not None:
        out_specs = [o_spec, vec]
        out_shapes = [jax.ShapeDtypeStruct(out_shape, F32), jax.ShapeDtypeStruct((1, n_out), F32)]
    elif norm_gain is not None:
        out_specs = [o_spec, o_spec]
        out_shapes = [jax.ShapeDtypeStruct(out_shape, out_dtype), jax.ShapeDtypeStruct(out_shape, BF16)]
    else:
        out_specs, out_shapes = o_spec, jax.ShapeDtypeStruct(out_shape, out_dtype)

    def body(*refs):
        a_ref, b_ref = refs[0], refs[1]
        ins = list(refs[2:n_in])
        r_ref = ins.pop(0) if res is not None else None
        outs = refs[n_in:]
        o_ref = outs[0]
        acc = refs[-1] if nk > 1 else None
        k = pl.program_id(2)

        def finish(val):
            if r_ref is not None:
                val = val + r_ref[...].astype(F32)
            if rms is not None:
                x_ref, g_ref, d_ref = ins
                xf = x_ref[...]
                r = lax.rsqrt(jnp.mean(xf * xf, axis=-1, keepdims=True) + EPS)
                xh = xf * r
                part = jnp.sum(val * xh, axis=0, keepdims=True)
                first = pl.program_id(0) == 0

                @pl.when(first)
                def _():
                    outs[1][...] = part

                @pl.when(jnp.logical_not(first))
                def _():
                    outs[1][...] += part

                dxh = val * g_ref[...]
                o_ref[...] = d_ref[...] + r * (dxh - xh * jnp.mean(dxh * xh, axis=-1, keepdims=True))
                return
            o_ref[...] = val.astype(out_dtype)
            if norm_gain is not None:
                r = lax.rsqrt(jnp.mean(val * val, axis=-1, keepdims=True) + EPS)
                outs[1][...] = (val * r * ins[0][...]).astype(BF16)

        def emit(s, val):
            if nk == 1:
                if o_sel == "full":
                    finish(val)
                else:
                    o_ref[at(o_sel, s)] = val.astype(out_dtype)
                return

            @pl.when(k == 0)
            def _():
                acc[at(o_sel, s)] = val

            @pl.when(k > 0)
            def _():
                acc[at(o_sel, s)] += val

        total = None
        if a_sel == "full" and b_sel == "lanes":
            wide = _dot(a_ref[...], b_ref[...], dims)
            for s in range(group):
                emit(s, wide[:, s * n:(s + 1) * n])
        else:
            for s in range(group):
                val = _dot(a_ref[at(a_sel, s)], b_ref[at(b_sel, s)], dims)
                if o_sel == "full":
                    total = val if total is None else total + val
                else:
                    emit(s, val)
        if o_sel == "full":
            emit(0, total)
        if nk > 1:
            @pl.when(k == nk - 1)
            def _():
                if o_sel == "full":
                    finish(acc[...])
                else:
                    o_ref[...] = acc[...].astype(out_dtype)

    rows_sem = "arbitrary" if rms is not None else "parallel"
    return pl.pallas_call(
        body, name=name, grid=grid, in_specs=[a_spec, b_spec] + [s for _, s in extra], out_specs=out_specs,
        out_shape=out_shapes, scratch_shapes=[pltpu.VMEM(acc_shape, F32)] if nk > 1 else [],
        compiler_params=_params((rows_sem, rows_sem, "arbitrary")),
    )(a, b, *[x for x, _ in extra])


def _row_tile(m, epi):
    return _tile(m, 512 if epi.get("rms") is not None else 1024)


def mm_nn(name, a, b, res=None, koff=0, out_dtype=None, **epi):
    m, k = a.shape
    n = b.shape[1]
    tm, tn, tk = _row_tile(m, epi), _tile(n, 1024, 128), _tile(k, 1024, 128)
    kb = koff // tk
    spec = pl.BlockSpec((tm, tn), lambda i, j, kk: (i, j))
    return _mm(name, a, b, _NN, (m // tm, n // tn, k // tk),
               pl.BlockSpec((tm, tk), lambda i, j, kk: (i, kk)),
               pl.BlockSpec((tk, tn), lambda i, j, kk: (kk + kb, j)),
               spec, (m, n), out_dtype, (tm, tn), res, spec, **epi)


def mm_nn_bs(name, a, bs, stacked_out=False, out_dtype=None):
    m, k = a.shape
    s, _, n = bs.shape
    tm, tk = _tile(m, 2048 if stacked_out else 1024), _tile(k, 1024, 128)
    a_spec = pl.BlockSpec((tm, tk), lambda i, j, kk: (i, kk))
    if stacked_out:
        return _mm(name, a, bs, _NN, (m // tm, s, k // tk), a_spec,
                   pl.BlockSpec((None, tk, n), lambda i, j, kk: (j, kk, 0)),
                   pl.BlockSpec((None, tm, n), lambda i, j, kk: (j, i, 0)), (s, m, n), out_dtype, (tm, n))
    g = _tile(s, max(1, 1024 // n), 1)
    return _mm(name, a, bs, _NN, (m // tm, s // g, k // tk), a_spec,
               pl.BlockSpec((g, tk, n), lambda i, j, kk: (j, kk, 0)),
               pl.BlockSpec((tm, g * n), lambda i, j, kk: (i, j)), (m, s * n), out_dtype, (tm, g * n),
               group=g, n=n, b_sel="lead", o_sel="lanes")


def mm_as_nn(name, a_st, b3, res, out_dtype=F32, **epi):
    s, m, kp = a_st.shape
    n = b3.shape[2]
    tm, tn = _row_tile(m, epi), _tile(n, 1024, 128)
    spec = pl.BlockSpec((tm, tn), lambda i, j, kk: (i, j))
    g = _tile(s, 2, 1)
    return _mm(name, a_st, b3, _NN, (m // tm, n // tn, s // g),
               pl.BlockSpec((g, tm, kp), lambda i, j, kk: (kk, i, 0)),
               pl.BlockSpec((g, kp, tn), lambda i, j, kk: (kk, 0, j)),
               spec, (m, n), out_dtype, (tm, tn), res, spec, group=g, a_sel="lead", b_sel="lead", **epi)


def mm_nt(name, dc, b, out_dtype=None, **epi):
    m, n = dc.shape
    k = b.shape[0]
    tm, tko, tnr = _row_tile(m, epi), _tile(k, 1024, 128), _tile(n, 1024, 128)
    return _mm(name, dc, b, _NT, (m // tm, k // tko, n // tnr),
               pl.BlockSpec((tm, tnr), lambda i, j, kk: (i, kk)),
               pl.BlockSpec((tko, tnr), lambda i, j, kk: (j, kk)),
               pl.BlockSpec((tm, tko), lambda i, j, kk: (i, j)), (m, k), out_dtype, (tm, tko), **epi)


def mm_nt_bs(name, dc, bs, dc_stacked=False, out_dtype=None, **epi):
    s, k, n = bs.shape
    m = dc.shape[1] if dc_stacked else dc.shape[0]
    tm, tko = (_tile(m, 1024) if dc_stacked else _row_tile(m, epi)), _tile(k, 1024, 128)
    o_spec = pl.BlockSpec((tm, tko), lambda i, j, kk: (i, j))
    if dc_stacked:
        g = _tile(s, 2, 1)
        return _mm(name, dc, bs, _NT, (m // tm, k // tko, s // g),
                   pl.BlockSpec((g, tm, n), lambda i, j, kk: (kk, i, 0)),
                   pl.BlockSpec((g, tko, n), lambda i, j, kk: (kk, j, 0)), o_spec, (m, k), out_dtype, (tm, tko),
                   group=g, a_sel="lead", b_sel="lead", **epi)
    g = _tile(s, max(1, 2048 // n), 1)
    return _mm(name, dc, bs, _NT, (m // tm, k // tko, s // g),
               pl.BlockSpec((tm, g * n), lambda i, j, kk: (i, kk)),
               pl.BlockSpec((g, tko, n), lambda i, j, kk: (kk, j, 0)), o_spec, (m, k), out_dtype, (tm, tko),
               group=g, n=n, a_sel="lanes", b_sel="lead", **epi)


def mm_nt_os(name, dc, b3, out_dtype=None):
    m, n = dc.shape
    s, kp, _ = b3.shape
    tm, tnr = _tile(m, 2048), _tile(n, 1024, 128)
    return _mm(name, dc, b3, _NT, (m // tm, s, n // tnr),
               pl.BlockSpec((tm, tnr), lambda i, j, kk: (i, kk)),
               pl.BlockSpec((None, kp, tnr), lambda i, j, kk: (j, 0, kk)),
               pl.BlockSpec((None, tm, kp), lambda i, j, kk: (j, i, 0)), (s, m, kp), out_dtype, (tm, kp))


def mm_tn(name, a, dc, a_stacked=False, dc_cols=None, dc_stacked=False, out_dtype=None):
    if a_stacked:
        s, m, kp = a.shape
        n = dc.shape[1]
        tno, tmr = _tile(n, 1024, 128), _tile(m, 2048)
        return _mm(name, a, dc, _TN, (s, n // tno, m // tmr),
                   pl.BlockSpec((None, tmr, kp), lambda i, j, kk: (i, kk, 0)),
                   pl.BlockSpec((tmr, tno), lambda i, j, kk: (kk, j)),
                   pl.BlockSpec((None, kp, tno), lambda i, j, kk: (i, 0, j)), (s, kp, n), out_dtype, (kp, tno))
    m, k = a.shape
    tko, tmr = _tile(k, 1024, 128), _tile(m, 2048)
    a_spec = pl.BlockSpec((tmr, tko), lambda i, j, kk: (kk, i))
    if dc_stacked:
        s, _, n = dc.shape
        return _mm(name, a, dc, _TN, (k // tko, s, m // tmr), a_spec,
                   pl.BlockSpec((None, tmr, n), lambda i, j, kk: (j, kk, 0)),
                   pl.BlockSpec((None, tko, n), lambda i, j, kk: (j, i, 0)), (s, k, n), out_dtype, (tko, n))
    if dc_cols is not None:
        n = dc_cols
        s = dc.shape[1] // n
        g = _tile(s, max(1, 1024 // n), 1)
        return _mm(name, a, dc, _TN, (k // tko, s // g, m // tmr), a_spec,
                   pl.BlockSpec((tmr, g * n), lambda i, j, kk: (kk, j)),
                   pl.BlockSpec((g, tko, n), lambda i, j, kk: (j, i, 0)), (s, k, n), out_dtype, (g, tko, n),
                   group=g, n=n, b_sel="lanes", o_sel="lead")
    n = dc.shape[1]
    tno = _tile(n, 1024, 128)
    return _mm(name, a, dc, _TN, (k // tko, n // tno, m // tmr), a_spec,
               pl.BlockSpec((tmr, tno), lambda i, j, kk: (kk, j)),
               pl.BlockSpec((tko, tno), lambda i, j, kk: (i, j)), (k, n), out_dtype, (tko, tno))


def rms_fwd(name, x, g):
    t, d = x.shape
    tr = _tile(t, 512)

    def body(x_ref, g_ref, o_ref):
        xf = x_ref[...]
        r = lax.rsqrt(jnp.mean(xf * xf, axis=-1, keepdims=True) + EPS)
        o_ref[...] = (xf * r * g_ref[...]).astype(o_ref.dtype)

    return pl.pallas_call(
        body, name=name, grid=(t // tr,),
        in_specs=[pl.BlockSpec((tr, d), lambda i: (i, 0)), pl.BlockSpec((1, d), lambda i: (0, 0))],
        out_specs=pl.BlockSpec((tr, d), lambda i: (i, 0)),
        out_shape=jax.ShapeDtypeStruct((t, d), BF16), compiler_params=_params(("parallel",)),
    )(x, g.reshape(1, d))


def rms_bwd(name, x, g, dh, dres=None, need_dx=True):
    t, d = x.shape
    tr = _tile(t, 512)

    def body(*refs):
        refs = list(refs)
        x_ref, g_ref, dh_ref = refs[:3]
        r_ref = refs[3] if dres is not None else None
        outs = refs[4:] if dres is not None else refs[3:]
        dx_ref, dg_ref = (outs[0], outs[1]) if need_dx else (None, outs[0])
        i = pl.program_id(0)

        @pl.when(i == 0)
        def _():
            dg_ref[...] = jnp.zeros_like(dg_ref)

        xf = x_ref[...]
        dhf = dh_ref[...].astype(F32)
        r = lax.rsqrt(jnp.mean(xf * xf, axis=-1, keepdims=True) + EPS)
        xh = xf * r
        dg_ref[...] += jnp.sum(dhf * xh, axis=0, keepdims=True)
        if need_dx:
            dxh = dhf * g_ref[...]
            dx = r * (dxh - xh * jnp.mean(dxh * xh, axis=-1, keepdims=True))
            if r_ref is not None:
                dx = dx + r_ref[...]
            dx_ref[...] = dx

    row = pl.BlockSpec((tr, d), lambda i: (i, 0))
    vec = pl.BlockSpec((1, d), lambda i: (0, 0))
    in_specs = [row, vec, row] + ([row] if dres is not None else [])
    args = (x, g.reshape(1, d), dh) + ((dres,) if dres is not None else ())
    out_specs = ([row] if need_dx else []) + [vec]
    out_shape = ([jax.ShapeDtypeStruct((t, d), F32)] if need_dx else []) + [jax.ShapeDtypeStruct((1, d), F32)]
    res = pl.pallas_call(
        body, name=name, grid=(t // tr,), in_specs=in_specs, out_specs=out_specs, out_shape=out_shape,
        compiler_params=_params(("arbitrary",)),
    )(*args)
    return res if need_dx else (None, res[0])


def loss_head(x, g, tgt):
    t, d = x.shape
    tr = _tile(t, 512)

    def body(x_ref, g_ref, t_ref, l_ref, dx_ref, dg_ref):
        i = pl.program_id(0)

        @pl.when(i == 0)
        def _():
            l_ref[...] = jnp.zeros_like(l_ref)
            dg_ref[...] = jnp.zeros_like(dg_ref)

        xf = x_ref[...]
        r = lax.rsqrt(jnp.mean(xf * xf, axis=-1, keepdims=True) + EPS)
        xh = xf * r
        diff = xh * g_ref[...] - t_ref[...]
        l_ref[...] += 0.5 * jnp.sum(jnp.mean(diff * diff, axis=-1, keepdims=True))
        dy = diff * (1.0 / d)
        dg_ref[...] += jnp.sum(dy * xh, axis=0, keepdims=True)
        dxh = dy * g_ref[...]
        dx_ref[...] = r * (dxh - xh * jnp.mean(dxh * xh, axis=-1, keepdims=True))

    row = pl.BlockSpec((tr, d), lambda i: (i, 0))
    vec = pl.BlockSpec((1, d), lambda i: (0, 0))
    return pl.pallas_call(
        body, name="loss_head", grid=(t // tr,), in_specs=[row, vec, row],
        out_specs=[pl.BlockSpec((1, 128), lambda i: (0, 0)), row, vec],
        out_shape=[jax.ShapeDtypeStruct((1, 128), F32), jax.ShapeDtypeStruct((t, d), F32),
                   jax.ShapeDtypeStruct((1, d), F32)],
        compiler_params=_params(("arbitrary",)),
    )(x, g.reshape(1, d), tgt)


def glu_fwd(glu, x, gain):
    t, d = x.shape
    tr = _tile(t, 512)

    def body(v_ref, g_ref, x_ref, n_ref, o_ref, h_ref):
        y = x_ref[...] + v_ref[...] * jax.nn.sigmoid(g_ref[...])
        o_ref[...] = y
        r = lax.rsqrt(jnp.mean(y * y, axis=-1, keepdims=True) + EPS)
        h_ref[...] = (y * r * n_ref[...]).astype(h_ref.dtype)

    row = pl.BlockSpec((tr, d), lambda i: (i, 0))
    return pl.pallas_call(
        body, name="glu_fwd", grid=(t // tr,),
        in_specs=[row, pl.BlockSpec((tr, d), lambda i: (i, 1)), row, pl.BlockSpec((1, d), lambda i: (0, 0))],
        out_specs=[row, row],
        out_shape=[jax.ShapeDtypeStruct((t, d), F32), jax.ShapeDtypeStruct((t, d), BF16)],
        compiler_params=_params(("parallel",)),
    )(glu, glu, x, gain.reshape(1, d))


def glu_bwd(glu, dmix):
    t, d = dmix.shape
    tr = _tile(t, 512)

    def body(v_ref, g_ref, d_ref, o_ref):
        sg = jax.nn.sigmoid(g_ref[...])
        dm = d_ref[...]
        o_ref[:, :d] = (dm * sg).astype(o_ref.dtype)
        o_ref[:, d:] = (dm * v_ref[...] * sg * (1.0 - sg)).astype(o_ref.dtype)

    return pl.pallas_call(
        body, name="glu_bwd", grid=(t // tr,),
        in_specs=[pl.BlockSpec((tr, d), lambda i: (i, 0)), pl.BlockSpec((tr, d), lambda i: (i, 1)),
                  pl.BlockSpec((tr, d), lambda i: (i, 0))],
        out_specs=pl.BlockSpec((tr, 2 * d), lambda i: (i, 0)),
        out_shape=jax.ShapeDtypeStruct((t, 2 * d), BF16), compiler_params=_params(("parallel",)),
    )(glu, glu, dmix)


def _head_masks(shape):
    lane = lax.broadcasted_iota(jnp.int32, shape, 1)
    return lane < SB_HEAD_DIM


def _stack_heads(xf, is_a):
    return jnp.concatenate([jnp.where(is_a, xf, 0.0), jnp.where(is_a, 0.0, xf)], axis=0).astype(MXU_DTYPE)


def _diag_mask(qb, row0, rows):
    row = (lax.broadcasted_iota(jnp.int32, (rows, qb), 0) + row0) & (qb - 1)
    col = lax.broadcasted_iota(jnp.int32, (rows, qb), 1)
    return col < row


def _tri01(qb, pred):
    j = lax.broadcasted_iota(jnp.int32, (qb, qb), 0)
    s = lax.broadcasted_iota(jnp.int32, (qb, qb), 1)
    m = pred(j, s).astype(BF16)
    return jnp.concatenate([m, m], axis=0)


def _split_cat(x):
    hi = x.astype(BF16)
    lo = (x - hi.astype(F32)).astype(BF16)
    return jnp.concatenate([hi, lo], axis=1)


def sb_attn_fwd(proj, order, bsz, seq):
    qb = SB_BLOCK
    nq = seq // qb
    npair = SB_WIDTH // 128
    scale = SB_HEAD_DIM ** -0.5

    def body(q_ref, k_ref, v_ref, order_ref, o_ref, r_ref):
        qi = pl.program_id(2)
        is_a = _head_masks((qb, 128))
        q2 = _stack_heads(q_ref[...] * scale, is_a)
        diag = _diag_mask(qb, 0, 2 * qb)
        upper = _tri01(qb, lambda j, s: j > s)

        def blocks(kbs, acc, run, masked):
            sl = [pl.ds(pl.multiple_of(kb * qb, qb), qb) for kb in kbs]
            zs = [lax.dot_general(q2, k_ref[s, :].astype(MXU_DTYPE), _NT, preferred_element_type=F32) for s in sl]
            lks = [-jnp.maximum(z, 0.0) - jnp.log(1.0 + jnp.exp(-jnp.abs(z))) for z in zs]
            lbs = [lk + z for lk, z in zip(lks, zs)]
            if masked:
                lks = [jnp.where(diag, lk, 0.0) for lk in lks]
            cs = [lax.dot_general(_split_cat(lk), upper, _NN, preferred_element_type=F32) for lk in lks]
            for lk, lb, c, s in zip(lks, lbs, cs, sl):
                w = jnp.exp(lb + (run + c))
                if masked:
                    w = jnp.where(diag, w, 0.0)
                acc = acc + lax.dot_general(w.astype(MXU_DTYPE), v_ref[s, :].astype(MXU_DTYPE), _NN,
                                            preferred_element_type=F32)
                run = run + jnp.sum(lk, axis=1, keepdims=True)
            return acc, run

        carry = blocks([qi], jnp.zeros((2 * qb, 128), F32), jnp.zeros((2 * qb, 1), F32), True)
        carry = lax.cond(qi % 2 == 1, lambda c: blocks([qi - 1], c[0], c[1], False), lambda c: c, carry)
        top = qi - qi % 2
        acc, run = lax.fori_loop(
            0, qi // 2, lambda i, c: blocks([top - 1 - 2 * i, top - 2 - 2 * i], c[0], c[1], False), carry)
        o_ref[...] = jnp.where(is_a, acc[:qb], acc[qb:]).astype(o_ref.dtype)
        r_ref[...] = jnp.where(is_a, run[:qb], run[qb:])

    return pl.pallas_call(
        body, name="sb_attn_fwd", grid=(bsz, npair, nq),
        in_specs=[pl.BlockSpec((qb, 128), lambda b, p, i: (b * nq + i, p)),
                  pl.BlockSpec((seq, 128), lambda b, p, i: (b, npair + p)),
                  pl.BlockSpec((seq, 128), lambda b, p, i: (b, 2 * npair + p)),
                  pl.BlockSpec((1, 128), lambda b, p, i: (0, 0))],
        out_specs=[pl.BlockSpec((qb, 128), lambda b, p, i: (b * nq + i, p)),
                   pl.BlockSpec((qb, 128), lambda b, p, i: (b * nq + i, p))],
        out_shape=[jax.ShapeDtypeStruct((bsz * seq, SB_WIDTH), BF16),
                   jax.ShapeDtypeStruct((bsz * seq, SB_WIDTH), F32)],
        compiler_params=_params(("parallel", "parallel", "arbitrary")),
    )(proj, proj, proj, order)


def sb_attn_bwd(proj, rsum, dcat, bsz, seq):
    qb = SB_BLOCK
    nq = seq // qb
    npair = SB_WIDTH // 128
    scale = SB_HEAD_DIM ** -0.5

    def body(q_ref, k_ref, v_ref, r_ref, do_ref, dq_ref, dk_ref, dv_ref):
        qi = pl.program_id(2)

        @pl.when(qi == 0)
        def _():
            dk_ref[...] = jnp.zeros_like(dk_ref)
            dv_ref[...] = jnp.zeros_like(dv_ref)

        is_a = _head_masks((qb, 128))
        q2 = _stack_heads(q_ref[...] * scale, is_a)
        do2 = _stack_heads(do_ref[...].astype(F32), is_a)
        rf = r_ref[...]
        rtot = jnp.concatenate([rf[:, 0:1], rf[:, SB_HEAD_DIM:SB_HEAD_DIM + 1]], axis=0)
        diag = _diag_mask(qb, 0, 2 * qb)
        incl = _tri01(qb, lambda j, s: j <= s)
        strict = _tri01(qb, lambda j, s: j < s)

        def blocks(kbs, dq, pre, epre, masked):
            sl = [pl.ds(pl.multiple_of(kb * qb, qb), qb) for kb in kbs]
            ks = [k_ref[s, :].astype(MXU_DTYPE) for s in sl]
            vs = [v_ref[s, :].astype(MXU_DTYPE) for s in sl]
            zs = [lax.dot_general(q2, kblk, _NT, preferred_element_type=F32) for kblk in ks]
            dws = [lax.dot_general(do2, vblk, _NT, preferred_element_type=F32) for vblk in vs]
            lks = [-jnp.maximum(z, 0.0) - jnp.log(1.0 + jnp.exp(-jnp.abs(z))) for z in zs]
            lbs = [lk + z for lk, z in zip(lks, zs)]
            if masked:
                lks = [jnp.where(diag, lk, 0.0) for lk in lks]
            ps = [lax.dot_general(_split_cat(lk), incl, _NN, preferred_element_type=F32) for lk in lks]
            ws, es = [], []
            for lk, lb, p, dw in zip(lks, lbs, ps, dws):
                w = jnp.exp(lb + (rtot - (pre + p)))
                if masked:
                    w = jnp.where(diag, w, 0.0)
                ws.append(w)
                es.append(dw * w)
                pre = pre + jnp.sum(lk, axis=1, keepdims=True)
            cs = [lax.dot_general(_split_cat(e), strict, _NN, preferred_element_type=F32) for e in es]
            for e, lb, c, w, kblk, s in zip(es, lbs, cs, ws, ks, sl):
                dz = e - jnp.exp(lb) * (e + (epre + c))
                if masked:
                    dz = jnp.where(diag, dz, 0.0)
                dz = dz.astype(MXU_DTYPE)
                dq = dq + lax.dot_general(dz, kblk, _NN, preferred_element_type=F32)
                dk_ref[s, :] += lax.dot_general(dz, q2, _TN, preferred_element_type=F32)
                dv_ref[s, :] += lax.dot_general(w.astype(MXU_DTYPE), do2, _TN, preferred_element_type=F32)
                epre = epre + jnp.sum(e, axis=1, keepdims=True)
            return dq, pre, epre

        zc = jnp.zeros((2 * qb, 1), F32)
        carry = lax.fori_loop(0, qi // 2, lambda i, c: blocks([2 * i, 2 * i + 1], c[0], c[1], c[2], False),
                              (jnp.zeros((2 * qb, 128), F32), zc, zc))
        carry = lax.cond(qi % 2 == 1, lambda c: blocks([qi - 1], c[0], c[1], c[2], False), lambda c: c, carry)
        dq = blocks([qi], carry[0], carry[1], carry[2], True)[0]
        dq_ref[...] = jnp.where(is_a, dq[:qb], dq[qb:]) * scale

    full = jax.ShapeDtypeStruct((bsz * seq, SB_WIDTH), F32)
    qspec = pl.BlockSpec((qb, 128), lambda b, p, i: (b * nq + i, p))
    return pl.pallas_call(
        body, name="sb_attn_bwd", grid=(bsz, npair, nq),
        in_specs=[qspec,
                  pl.BlockSpec((seq, 128), lambda b, p, i: (b, npair + p)),
                  pl.BlockSpec((seq, 128), lambda b, p, i: (b, 2 * npair + p)),
                  qspec, qspec],
        out_specs=[qspec, pl.BlockSpec((seq, 128), lambda b, p, i: (b, p)),
                   pl.BlockSpec((seq, 128), lambda b, p, i: (b, p))],
        out_shape=[full, full, full],
        compiler_params=_params(("parallel", "parallel", "arbitrary")),
    )(proj, proj, proj, rsum, dcat)


def _window_sums(x, forward):
    n = x.shape[0]
    out = []
    s = x
    for sh in (1, 2, 4, 8):
        s = s + pltpu.roll(s, (n - sh) if forward else sh, 0)
        out.append(s)
    return out


def _pool_counts(tc, c, w):
    t = lax.broadcasted_iota(jnp.int32, (tc, 1), 0) + c * tc
    return jnp.minimum(t + 1, w).astype(F32)


def pool_fwd(proj, pool_w, pool_scale, bsz, seq):
    tc = _tile(seq, 512)
    nc = seq // tc
    hb = tc // POOL_HALO
    ucol = 3

    def body(u_ref, prev_ref, w_ref, s_ref, o_ref):
        c = pl.program_id(1)
        prev = jnp.where(c > 0, prev_ref[...], 0.0)
        x = jnp.concatenate([prev, u_ref[...]], axis=0)
        sums = _window_sums(x, forward=False)
        for g, win in enumerate(POOL_WINDOWS):
            ls = slice(g * POOL_GROUP, (g + 1) * POOL_GROUP)
            pooled = sums[g][POOL_HALO:, ls] / _pool_counts(tc, c, win) - x[POOL_HALO:, ls]
            y = _dot(pooled, w_ref[g], _NN)
            o_ref[:, ls] = (y * s_ref[:, ls]).astype(o_ref.dtype)

    return pl.pallas_call(
        body, name="pool_fwd", grid=(bsz, nc),
        in_specs=[pl.BlockSpec((tc, SB_WIDTH), lambda b, c: (b * nc + c, ucol)),
                  pl.BlockSpec((POOL_HALO, SB_WIDTH), lambda b, c: (jnp.maximum((b * nc + c) * hb - 1, 0), ucol)),
                  pl.BlockSpec((4, POOL_GROUP, POOL_GROUP), lambda b, c: (0, 0, 0)),
                  pl.BlockSpec((1, SB_WIDTH), lambda b, c: (0, 0))],
        out_specs=pl.BlockSpec((tc, SB_WIDTH), lambda b, c: (b * nc + c, 0)),
        out_shape=jax.ShapeDtypeStruct((bsz * seq, SB_WIDTH), BF16),
        compiler_params=_params(("parallel", "parallel")),
    )(proj, proj, pool_w, pool_scale)


def pool_bwd(proj, pool_w, pool_scale, dcat, bsz, seq):
    tc = _tile(seq, 512)
    nc = seq // tc
    hb = tc // POOL_HALO
    nblk = bsz * seq // POOL_HALO
    ucol = 3

    def body(u_ref, prev_ref, dy_ref, nxt_ref, w_ref, s_ref, du_ref, dw_ref, ds_ref):
        b, c = pl.program_id(0), pl.program_id(1)

        @pl.when((b == 0) & (c == 0))
        def _():
            dw_ref[...] = jnp.zeros_like(dw_ref)
            ds_ref[...] = jnp.zeros_like(ds_ref)

        prev = jnp.where(c > 0, prev_ref[...], 0.0)
        x = jnp.concatenate([prev, u_ref[...]], axis=0)
        sums = _window_sums(x, forward=False)
        nxt = jnp.where(c < nc - 1, nxt_ref[...].astype(F32), 0.0)
        dy = jnp.concatenate([dy_ref[...].astype(F32), nxt], axis=0)
        tq = lax.broadcasted_iota(jnp.int32, (tc + POOL_HALO, 1), 0) + c * tc
        for g, win in enumerate(POOL_WINDOWS):
            ls = slice(g * POOL_GROUP, (g + 1) * POOL_GROUP)
            pooled = sums[g][POOL_HALO:, ls] / _pool_counts(tc, c, win) - x[POOL_HALO:, ls]
            y = _dot(pooled, w_ref[g], _NN)
            ds_ref[:, ls] += jnp.sum(dy[:tc, ls] * y, axis=0, keepdims=True)
            dz = dy[:, ls] * s_ref[:, ls]
            dw_ref[g] += _dot(pooled, dz[:tc], _TN)
            dpool = _dot(dz, w_ref[g], _NT)
            dmean = dpool / jnp.minimum(tq + 1, win).astype(F32)
            fsum = _window_sums(dmean, forward=True)[g]
            du_ref[:, ls] = fsum[:tc] - dpool[:tc]

    return pl.pallas_call(
        body, name="pool_bwd", grid=(bsz, nc),
        in_specs=[pl.BlockSpec((tc, SB_WIDTH), lambda b, c: (b * nc + c, ucol)),
                  pl.BlockSpec((POOL_HALO, SB_WIDTH), lambda b, c: (jnp.maximum((b * nc + c) * hb - 1, 0), ucol)),
                  pl.BlockSpec((tc, SB_WIDTH), lambda b, c: (b * nc + c, 1)),
                  pl.BlockSpec((POOL_HALO, SB_WIDTH), lambda b, c: (jnp.minimum((b * nc + c + 1) * hb, nblk - 1), 1)),
                  pl.BlockSpec((4, POOL_GROUP, POOL_GROUP), lambda b, c: (0, 0, 0)),
                  pl.BlockSpec((1, SB_WIDTH), lambda b, c: (0, 0))],
        out_specs=[pl.BlockSpec((tc, SB_WIDTH), lambda b, c: (b * nc + c, 0)),
                   pl.BlockSpec((4, POOL_GROUP, POOL_GROUP), lambda b, c: (0, 0, 0)),
                   pl.BlockSpec((1, SB_WIDTH), lambda b, c: (0, 0))],
        out_shape=[jax.ShapeDtypeStruct((bsz * seq, SB_WIDTH), F32),
                   jax.ShapeDtypeStruct((4, POOL_GROUP, POOL_GROUP), F32),
                   jax.ShapeDtypeStruct((1, SB_WIDTH), F32)],
        compiler_params=_params(("arbitrary", "arbitrary")),
    )(proj, proj, dcat, dcat, pool_w, pool_scale)


def _lbar(lam_re, lam_im, log_dt):
    dt = jnp.exp(log_dt)
    mag = jnp.exp(lam_re * dt)
    ang = lam_im * dt
    return mag * jnp.cos(ang), mag * jnp.sin(ang)


def _bbar(lam_re, lam_im, log_dt, b_re, b_im):
    lb_re, lb_im = _lbar(lam_re, lam_im, log_dt)
    n_re = lb_re - 1.0
    den = lam_re * lam_re + lam_im * lam_im
    coef_re = (n_re * lam_re + lb_im * lam_im) / den
    coef_im = (lb_im * lam_re - n_re * lam_im) / den
    return coef_re * b_re - coef_im * b_im, coef_re * b_im + coef_im * b_re


def _expand01():
    p = lax.broadcasted_iota(jnp.int32, (64, 1024), 0)
    q = lax.broadcasted_iota(jnp.int32, (64, 1024), 1)
    return (lax.shift_right_logical(q, 4) == p).astype(BF16)


def ssm_prep(lam_re, lam_im, log_dt, b_re2, b_im2):
    def body(lr_ref, li_ref, dt_ref, br_ref, bi_ref, ar_ref, ai_ref, bbr_ref, bbi_ref):
        e = _expand01()
        lr, li, dt = lr_ref[...], li_ref[...], dt_ref[...]
        ar_ref[...], ai_ref[...] = _lbar(lr, li, dt)
        bbr_ref[...], bbi_ref[...] = _bbar(_dot_exact01(lr, e), _dot_exact01(li, e), dt, br_ref[...], bi_ref[...])

    s64 = jax.ShapeDtypeStruct((64, 64), F32)
    s1k = jax.ShapeDtypeStruct((64, 1024), F32)
    return pl.pallas_call(body, name="ssm_prep", out_shape=[s64, s64, s1k, s1k], compiler_params=_params())(
        lam_re, lam_im, log_dt, b_re2, b_im2)


def ssm_prep_bwd(lam_re, lam_im, log_dt, b_re2, b_im2, da_re, da_im, dbb_re, dbb_im):
    def body(lr_ref, li_ref, dt_ref, br_ref, bi_ref, dar_ref, dai_ref, dbr_ref, dbi_ref,
             olr_ref, oli_ref, odt_ref, obr_ref, obi_ref):
        e = _expand01()
        lr, li, dt = lr_ref[...], li_ref[...], dt_ref[...]
        _, vjp_a = jax.vjp(_lbar, lr, li, dt)
        g_lr, g_li, g_dt = vjp_a((dar_ref[...], dai_ref[...]))
        _, vjp_b = jax.vjp(_bbar, _dot_exact01(lr, e), _dot_exact01(li, e), dt, br_ref[...], bi_ref[...])
        x_lr, x_li, x_dt, g_br, g_bi = vjp_b((dbr_ref[...], dbi_ref[...]))
        olr_ref[...] = g_lr + _dot_exact01(x_lr, e, _NT)
        oli_ref[...] = g_li + _dot_exact01(x_li, e, _NT)
        odt_ref[...] = g_dt + x_dt
        obr_ref[...] = g_br
        obi_ref[...] = g_bi

    s64 = jax.ShapeDtypeStruct((64, 64), F32)
    s1k = jax.ShapeDtypeStruct((64, 1024), F32)
    return pl.pallas_call(body, name="ssm_prep_bwd",
                          out_shape=[s64, s64, jax.ShapeDtypeStruct((64, 1), F32), s1k, s1k],
                          compiler_params=_params())(
        lam_re, lam_im, log_dt, b_re2, b_im2, da_re, da_im, dbb_re, dbb_im)


def _gelu(y):
    c = math.sqrt(2.0 / math.pi)
    return 0.5 * y * (1.0 + jnp.tanh(c * (y + 0.044715 * y * y * y)))


def _gelu_grad(y):
    c = math.sqrt(2.0 / math.pi)
    th = jnp.tanh(c * (y + 0.044715 * y * y * y))
    return 0.5 * (1.0 + th) + 0.5 * y * (1.0 - th * th) * c * (1.0 + 3.0 * 0.044715 * y * y)


def _cmul(ar, ai, br, bi):
    return ar * br - ai * bi, ar * bi + ai * br


def _scan_tables(ar, ai, reverse, tabs):
    row = lax.broadcasted_iota(jnp.int32, (8, SSM_STATES), 0)
    a1 = (ar, ai)
    a2 = _cmul(*a1, *a1)
    a4 = _cmul(*a2, *a2)
    powers = [a1, a2, _cmul(*a2, *a1), a4]
    powers += [_cmul(*a4, *p) for p in powers]
    for k, (val, sh) in enumerate(((a1, 1), (a2, 2), (a4, 4))):
        keep = (row < 8 - sh) if reverse else (row >= sh)
        tabs[2 * k][...] = jnp.where(keep, val[0], 0.0)
        tabs[2 * k + 1][...] = jnp.where(keep, val[1], 0.0)
    pr = jnp.zeros((8, SSM_STATES), F32)
    pi = jnp.zeros((8, SSM_STATES), F32)
    for r in range(8):
        val = powers[7 - r] if reverse else powers[r]
        pr = jnp.where(row == r, val[0], pr)
        pi = jnp.where(row == r, val[1], pi)
    tabs[6][...] = pr
    tabs[7][...] = pi


def _scan8(xr, xi, tabs, ls, cr, ci, reverse):
    for k, sh in enumerate((1, 2, 4)):
        amt = (8 - sh) if reverse else sh
        sr, si = pltpu.roll(xr, amt, 0), pltpu.roll(xi, amt, 0)
        lr, li = tabs[2 * k][:, ls], tabs[2 * k + 1][:, ls]
        xr, xi = xr + lr * sr - li * si, xi + lr * si + li * sr
    pr, pi = tabs[6][:, ls], tabs[7][:, ls]
    return xr + pr * cr - pi * ci, xi + pr * ci + pi * cr


def _block8(b):
    return pl.ds(pl.multiple_of(b * 8, 8), 8)


def ssm_fwd(u, wt, ct, a_re, a_im, dskip, bsz, seq):
    tc = _tile(seq, 256)
    nc = seq // tc
    ns = SSM_TILE_STATES
    nl = SSM_STATES // SSM_LANES

    def body(u_ref, wt_ref, ct_ref, ar_ref, ai_ref, d_ref, y_ref, gl_ref, hr_ref, hi_ref, sr_ref, si_ref, *tabs):
        b, c = pl.program_id(0), pl.program_id(1)

        @pl.when((b == 0) & (c == 0))
        def _():
            _scan_tables(ar_ref[...], ai_ref[...], False, tabs)

        @pl.when(c == 0)
        def _():
            sr_ref[...] = jnp.zeros_like(sr_ref)
            si_ref[...] = jnp.zeros_like(si_ref)

        uf = u_ref[...]
        for i in range(SSM_TILES):
            bu = _dot(uf[:, i * 128:(i + 1) * 128], wt_ref[i], _NN)
            hr_ref[:, i * ns:(i + 1) * ns] = bu[:, :ns]
            hi_ref[:, i * ns:(i + 1) * ns] = bu[:, ns:]

        def step(blk, carry):
            rows = _block8(blk)
            new = []
            for j in range(nl):
                ls = slice(j * SSM_LANES, (j + 1) * SSM_LANES)
                xr, xi = _scan8(hr_ref[rows, ls], hi_ref[rows, ls], tabs, ls, carry[2 * j], carry[2 * j + 1], False)
                hr_ref[rows, ls] = xr
                hi_ref[rows, ls] = xi
                new += [xr[7:8], xi[7:8]]
            return tuple(new)

        init = []
        for j in range(nl):
            ls = slice(j * SSM_LANES, (j + 1) * SSM_LANES)
            init += [sr_ref[:, ls], si_ref[:, ls]]
        last = lax.fori_loop(0, tc // 8, step, tuple(init), unroll=2)
        for j in range(nl):
            ls = slice(j * SSM_LANES, (j + 1) * SSM_LANES)
            sr_ref[:, ls] = last[2 * j]
            si_ref[:, ls] = last[2 * j + 1]
        for i in range(SSM_TILES):
            hcat = jnp.concatenate([hr_ref[:, i * ns:(i + 1) * ns], hi_ref[:, i * ns:(i + 1) * ns]], axis=1)
            ls = slice(i * 128, (i + 1) * 128)
            y = _dot(hcat, ct_ref[i], _NN) + d_ref[:, ls] * uf[:, ls]
            y_ref[:, ls] = y
            gl_ref[:, ls] = _gelu(y).astype(gl_ref.dtype)

    t = bsz * seq
    row = pl.BlockSpec((tc, D_MODEL), lambda b, c: (b * nc + c, 0))
    st = pl.BlockSpec((tc, SSM_STATES), lambda b, c: (b * nc + c, 0))
    diag = pl.BlockSpec((1, SSM_STATES), lambda b, c: (0, 0))
    return pl.pallas_call(
        body, name="ssm_fwd", grid=(bsz, nc),
        in_specs=[row, pl.BlockSpec((SSM_TILES, 128, 2 * ns), lambda b, c: (0, 0, 0)),
                  pl.BlockSpec((SSM_TILES, 2 * ns, 128), lambda b, c: (0, 0, 0)), diag, diag,
                  pl.BlockSpec((1, D_MODEL), lambda b, c: (0, 0))],
        out_specs=[row, row, st, st],
        out_shape=[jax.ShapeDtypeStruct((t, D_MODEL), F32), jax.ShapeDtypeStruct((t, D_MODEL), BF16),
                   jax.ShapeDtypeStruct((t, SSM_STATES), F32), jax.ShapeDtypeStruct((t, SSM_STATES), F32)],
        scratch_shapes=[pltpu.VMEM((1, SSM_STATES), F32)] * 2 + [pltpu.VMEM((8, SSM_STATES), F32)] * 8,
        compiler_params=_params(("arbitrary", "arbitrary")),
    )(u, wt, ct, a_re, a_im, dskip)


def ssm_bwd(dgl, y, u, h_re, h_im, wt, ct, a_re, a_im, dskip, bsz, seq):
    tc = _tile(seq, 256)
    nc = seq // tc
    nb = tc // 8
    ns = SSM_TILE_STATES
    nl = SSM_STATES // SSM_LANES

    def body(dgl_ref, y_ref, u_ref, hr_ref, hi_ref, pr_ref, pi_ref, wt_ref, ct_ref, ar_ref, ai_ref, d_ref,
             du_ref, dwt_ref, dct_ref, dd_ref, dar_ref, dai_ref, gr_ref, gi_ref, sr_ref, si_ref, ar8_ref, ai8_ref,
             *tabs):
        b, c = pl.program_id(0), pl.program_id(1)

        @pl.when((b == 0) & (c == 0))
        def _():
            for r in (dwt_ref, dct_ref, dd_ref, ar8_ref, ai8_ref):
                r[...] = jnp.zeros_like(r)
            _scan_tables(ar_ref[...], -ai_ref[...], True, tabs)

        @pl.when(c == 0)
        def _():
            sr_ref[...] = jnp.zeros_like(sr_ref)
            si_ref[...] = jnp.zeros_like(si_ref)

        uf = u_ref[...]
        dy = dgl_ref[...].astype(F32) * _gelu_grad(y_ref[...])
        dd_ref[...] += jnp.sum(dy * uf, axis=0, keepdims=True)
        for i in range(SSM_TILES):
            dyi = dy[:, i * 128:(i + 1) * 128]
            dh = _dot(dyi, ct_ref[i], _NT)
            gr_ref[:, i * ns:(i + 1) * ns] = dh[:, :ns]
            gi_ref[:, i * ns:(i + 1) * ns] = dh[:, ns:]
            hcat = jnp.concatenate([hr_ref[:, i * ns:(i + 1) * ns], hi_ref[:, i * ns:(i + 1) * ns]], axis=1)
            dct_ref[i] += _dot(hcat, dyi, _TN)
        row0 = lax.broadcasted_iota(jnp.int32, (8, SSM_LANES), 0) == 0

        def block(blk, carry, before):
            rows = _block8(blk)
            new = []
            for j in range(nl):
                ls = slice(j * SSM_LANES, (j + 1) * SSM_LANES)
                gr, gi = _scan8(gr_ref[rows, ls], gi_ref[rows, ls], tabs, ls, carry[2 * j], carry[2 * j + 1], True)
                gr_ref[rows, ls] = gr
                gi_ref[rows, ls] = gi
                bpr, bpi = before(j)
                hpr = jnp.where(row0, bpr, pltpu.roll(hr_ref[rows, ls], 1, 0))
                hpi = jnp.where(row0, bpi, pltpu.roll(hi_ref[rows, ls], 1, 0))
                ar8_ref[:, ls] += gr * hpr + gi * hpi
                ai8_ref[:, ls] += gi * hpr - gr * hpi
                new += [gr[0:1], gi[0:1]]
            return tuple(new)

        def step(jj, carry):
            blk = nb - 1 - jj
            prev_rows = _block8(blk - 1)

            def before(j):
                ls = slice(j * SSM_LANES, (j + 1) * SSM_LANES)
                return hr_ref[prev_rows, ls][7:8], hi_ref[prev_rows, ls][7:8]

            return block(blk, carry, before)

        init = []
        for j in range(nl):
            ls = slice(j * SSM_LANES, (j + 1) * SSM_LANES)
            init += [sr_ref[:, ls], si_ref[:, ls]]
        carry = lax.fori_loop(0, nb - 1, step, tuple(init))
        first = c == nc - 1

        def before_chunk(j):
            ls = slice(j * SSM_LANES, (j + 1) * SSM_LANES)
            return (jnp.where(first, 0.0, pr_ref[:, ls][7:8]), jnp.where(first, 0.0, pi_ref[:, ls][7:8]))

        last = block(0, carry, before_chunk)
        for j in range(nl):
            ls = slice(j * SSM_LANES, (j + 1) * SSM_LANES)
            sr_ref[:, ls] = last[2 * j]
            si_ref[:, ls] = last[2 * j + 1]
        for i in range(SSM_TILES):
            ls = slice(i * 128, (i + 1) * 128)
            gcat = jnp.concatenate([gr_ref[:, i * ns:(i + 1) * ns], gi_ref[:, i * ns:(i + 1) * ns]], axis=1)
            du_ref[:, ls] = (_dot(gcat, wt_ref[i], _NT) + d_ref[:, ls] * dy[:, ls]).astype(du_ref.dtype)
            dwt_ref[i] += _dot(uf[:, ls], gcat, _TN)

        @pl.when((b == bsz - 1) & (c == nc - 1))
        def _():
            dar_ref[...] = jnp.sum(ar8_ref[...], axis=0, keepdims=True)
            dai_ref[...] = jnp.sum(ai8_ref[...], axis=0, keepdims=True)

    t = bsz * seq
    rev = lambda b, c: (b * nc + (nc - 1 - c), 0)
    row = pl.BlockSpec((tc, D_MODEL), rev)
    st = pl.BlockSpec((tc, SSM_STATES), rev)
    prev = pl.BlockSpec((8, SSM_STATES), lambda b, c: (jnp.maximum((b * nc + (nc - 1 - c)) * nb - 1, 0), 0))
    diag = pl.BlockSpec((1, SSM_STATES), lambda b, c: (0, 0))
    wts = pl.BlockSpec((SSM_TILES, 128, 2 * ns), lambda b, c: (0, 0, 0))
    cts = pl.BlockSpec((SSM_TILES, 2 * ns, 128), lambda b, c: (0, 0, 0))
    vec = pl.BlockSpec((1, D_MODEL), lambda b, c: (0, 0))
    return pl.pallas_call(
        body, name="ssm_bwd", grid=(bsz, nc),
        in_specs=[row, row, row, st, st, prev, prev, wts, cts, diag, diag, vec],
        out_specs=[row, wts, cts, vec, diag, diag],
        out_shape=[jax.ShapeDtypeStruct((t, D_MODEL), BF16),
                   jax.ShapeDtypeStruct((SSM_TILES, 128, 2 * ns), F32),
                   jax.ShapeDtypeStruct((SSM_TILES, 2 * ns, 128), F32),
                   jax.ShapeDtypeStruct((1, D_MODEL), F32),
                   jax.ShapeDtypeStruct((1, SSM_STATES), F32), jax.ShapeDtypeStruct((1, SSM_STATES), F32)],
        scratch_shapes=[pltpu.VMEM((tc, SSM_STATES), F32)] * 2 + [pltpu.VMEM((1, SSM_STATES), F32)] * 2
                       + [pltpu.VMEM((8, SSM_STATES), F32)] * 10,
        compiler_params=_params(("arbitrary", "arbitrary")),
    )(dgl, y, u, h_re, h_im, h_re, h_im, wt, ct, a_re, a_im, dskip)


def _ssm_in_weights(bb_re2, bb_im2):
    eye = jnp.eye(8, dtype=F32)[None, :, None, :, None]

    def one(bb):
        t = bb.reshape(8, 8, 64, 16).transpose(0, 1, 3, 2)
        return (t[:, :, :, None, :] * eye).reshape(8, 128, 512)

    return jnp.concatenate([one(bb_re2), one(bb_im2)], axis=-1).astype(MXU_DTYPE)


def _ssm_in_weights_bwd(dwt):
    eye = jnp.eye(8, dtype=F32)[None, :, None, :, None]

    def one(d):
        t = (d.reshape(8, 8, 16, 8, 64) * eye).sum(axis=3)
        return t.transpose(0, 1, 3, 2).reshape(64, 1024)

    return one(dwt[..., :512]), one(dwt[..., 512:])


def _ssm_out_weights(c_re, c_im):
    eye = jnp.eye(8, dtype=F32)[None, :, None, :, None]

    def one(cc):
        t = cc.reshape(8, 8, 16, 64).transpose(0, 1, 3, 2)
        return (t[:, :, :, None, :] * eye).reshape(8, 512, 128)

    return jnp.concatenate([one(c_re), -one(c_im)], axis=1).astype(MXU_DTYPE)


def _ssm_out_weights_bwd(dct):
    eye = jnp.eye(8, dtype=F32)[None, :, None, :, None]

    def one(d):
        t = (d.reshape(8, 8, 64, 8, 16) * eye).sum(axis=3)
        return t.transpose(0, 1, 3, 2).reshape(64, 16, 64)

    return one(dct[:, :512]), -one(dct[:, 512:])


def _softmax(s):
    m = jnp.max(s, axis=-1, keepdims=True)
    e = jnp.exp(s - m)
    return e / jnp.sum(e, axis=-1, keepdims=True)


def xattn_fwd(q, kv, bsz, seq):
    tq = _tile(seq, 512)
    nq = seq // tq
    scale = XA_HEAD_DIM ** -0.5

    def body(q_ref, k_ref, v_ref, o_ref):
        s = lax.dot_general(q_ref[...], k_ref[...], _NT, preferred_element_type=F32) * scale
        p = _softmax(s)
        o_ref[...] = _dot(p, v_ref[...], _NN).astype(o_ref.dtype)

    qs = pl.BlockSpec((tq, XA_HEAD_DIM), lambda b, h, i: (b * nq + i, h))
    return pl.pallas_call(
        body, name="xattn_fwd", grid=(bsz, XA_HEADS, nq),
        in_specs=[qs, pl.BlockSpec((MEM_LEN, XA_HEAD_DIM), lambda b, h, i: (b, h)),
                  pl.BlockSpec((MEM_LEN, XA_HEAD_DIM), lambda b, h, i: (b, XA_HEADS + h))],
        out_specs=qs, out_shape=jax.ShapeDtypeStruct((bsz * seq, D_MODEL), BF16),
        compiler_params=_params(("parallel", "parallel", "parallel")),
    )(q, kv, kv)


def xattn_bwd(q, kv, do, bsz, seq):
    tq = _tile(seq, 512)
    nq = seq // tq
    scale = XA_HEAD_DIM ** -0.5

    def body(q_ref, k_ref, v_ref, do_ref, dq_ref, dk_ref, dv_ref):
        @pl.when(pl.program_id(2) == 0)
        def _():
            dk_ref[...] = jnp.zeros_like(dk_ref)
            dv_ref[...] = jnp.zeros_like(dv_ref)

        qv, kk, vv, dov = q_ref[...], k_ref[...], v_ref[...], do_ref[...]
        s = lax.dot_general(qv, kk, _NT, preferred_element_type=F32) * scale
        p = _softmax(s)
        dp = lax.dot_general(dov, vv, _NT, preferred_element_type=F32)
        ds = (p * (dp - jnp.sum(dp * p, axis=-1, keepdims=True)) * scale).astype(MXU_DTYPE)
        dq_ref[...] = lax.dot_general(ds, kk, _NN, preferred_element_type=F32).astype(dq_ref.dtype)
        dk_ref[...] += lax.dot_general(ds, qv, _TN, preferred_element_type=F32)
        dv_ref[...] += lax.dot_general(p.astype(MXU_DTYPE), dov, _TN, preferred_element_type=F32)

    qs = pl.BlockSpec((tq, XA_HEAD_DIM), lambda b, h, i: (b * nq + i, h))
    ks = pl.BlockSpec((MEM_LEN, XA_HEAD_DIM), lambda b, h, i: (b, h))
    vs = pl.BlockSpec((MEM_LEN, XA_HEAD_DIM), lambda b, h, i: (b, XA_HEADS + h))
    dkv = jax.ShapeDtypeStruct((bsz * MEM_LEN, D_MODEL), F32)
    dq, dk, dv = pl.pallas_call(
        body, name="xattn_bwd", grid=(bsz, XA_HEADS, nq),
        in_specs=[qs, ks, vs, qs], out_specs=[qs, ks, ks],
        out_shape=[jax.ShapeDtypeStruct((bsz * seq, D_MODEL), BF16), dkv, dkv],
        compiler_params=_params(("parallel", "parallel", "arbitrary")),
    )(q, kv, kv, do)
    return dq, dk, dv


CONV_HALO = 16


def _shifts_down(x, prev):
    h = prev.shape[0]
    ext = jnp.concatenate([prev, x], axis=0)
    return pltpu.roll(ext, 1, 0)[h:], pltpu.roll(ext, 2, 0)[h:]


def _shifts_up(x, nxt):
    rows = x.shape[0]
    n = rows + nxt.shape[0]
    ext = jnp.concatenate([x, nxt], axis=0)
    return pltpu.roll(ext, n - 1, 0)[:rows], pltpu.roll(ext, n - 2, 0)[:rows]


def _conv_taps(u, u1, u2, w, b):
    return b + w[2:3] * u + w[1:2] * u1 + w[0:1] * u2


def conv_fwd(up, cw, cb, bsz, seq):
    tc = _tile(seq, 512)
    nc = seq // tc
    hb = tc // CONV_HALO
    half = N_DEV // 2

    def body(uv_ref, ug_ref, pv_ref, pg_ref, wv_ref, wg_ref, bv_ref, bg_ref, o_ref):
        c = pl.program_id(2)
        pv = jnp.where(c > 0, pv_ref[...].astype(F32), 0.0)
        pg = jnp.where(c > 0, pg_ref[...].astype(F32), 0.0)
        uv, ug = uv_ref[...].astype(F32), ug_ref[...].astype(F32)
        val = _conv_taps(uv, *_shifts_down(uv, pv), wv_ref[...], bv_ref[...])
        gate = _conv_taps(ug, *_shifts_down(ug, pg), wg_ref[...], bg_ref[...])
        o_ref[...] = (gate * jax.nn.sigmoid(gate) * val).astype(o_ref.dtype)

    def cur(off):
        return pl.BlockSpec((None, tc, FF_SHARD), lambda b, j, c: (j + off, b * nc + c, 0))

    def prv(off):
        return pl.BlockSpec((None, CONV_HALO, FF_SHARD), lambda b, j, c: (j + off, jnp.maximum((b * nc + c) * hb - 1, 0), 0))

    def par(rows, off):
        return pl.BlockSpec((None, rows, FF_SHARD), lambda b, j, c: (j + off, 0, 0))

    return pl.pallas_call(
        body, name="conv_fwd", grid=(bsz, half, nc),
        in_specs=[cur(0), cur(half), prv(0), prv(half), par(3, 0), par(3, half), par(1, 0), par(1, half)],
        out_specs=cur(0), out_shape=jax.ShapeDtypeStruct((half, bsz * seq, FF_SHARD), BF16),
        compiler_params=_params(("parallel", "parallel", "parallel")),
    )(up, up, up, up, cw, cw, cb, cb)


def conv_bwd_taps(up, cw, cb, dact, bsz, seq):
    tc = _tile(seq, 512)
    nc = seq // tc
    hb = tc // CONV_HALO
    half = N_DEV // 2

    def body(uv_ref, ug_ref, pv_ref, pg_ref, wv_ref, wg_ref, bv_ref, bg_ref, da_ref,
             dc_ref, dwv_ref, dwg_ref, dbv_ref, dbg_ref):
        b, c = pl.program_id(1), pl.program_id(2)

        @pl.when((b == 0) & (c == 0))
        def _():
            for r in (dwv_ref, dwg_ref, dbv_ref, dbg_ref):
                r[...] = jnp.zeros_like(r)

        pv = jnp.where(c > 0, pv_ref[...].astype(F32), 0.0)
        pg = jnp.where(c > 0, pg_ref[...].astype(F32), 0.0)
        uv, ug = uv_ref[...].astype(F32), ug_ref[...].astype(F32)
        uv1, uv2 = _shifts_down(uv, pv)
        ug1, ug2 = _shifts_down(ug, pg)
        val = _conv_taps(uv, uv1, uv2, wv_ref[...], bv_ref[...])
        gate = _conv_taps(ug, ug1, ug2, wg_ref[...], bg_ref[...])
        sg = jax.nn.sigmoid(gate)
        da = da_ref[...].astype(F32)
        dsilu = da * sg
        dval = dsilu * gate
        dgate = dsilu * val * (1.0 + gate * (1.0 - sg))
        dc_ref[0] = dval.astype(dc_ref.dtype)
        dc_ref[1] = dgate.astype(dc_ref.dtype)
        for dcv, taps, dw_ref, db_ref in ((dval, (uv2, uv1, uv), dwv_ref, dbv_ref),
                                          (dgate, (ug2, ug1, ug), dwg_ref, dbg_ref)):
            db_ref[...] += jnp.sum(dcv, axis=0, keepdims=True)
            for k, u_k in enumerate(taps):
                dw_ref[k:k + 1, :] += jnp.sum(dcv * u_k, axis=0, keepdims=True)

    def cur(off):
        return pl.BlockSpec((None, tc, FF_SHARD), lambda j, b, c: (j + off, b * nc + c, 0))

    def prv(off):
        return pl.BlockSpec((None, CONV_HALO, FF_SHARD), lambda j, b, c: (j + off, jnp.maximum((b * nc + c) * hb - 1, 0), 0))

    def par(rows, off):
        return pl.BlockSpec((None, rows, FF_SHARD), lambda j, b, c: (j + off, 0, 0))

    t = bsz * seq
    hs = jax.ShapeDtypeStruct((2, half, t, FF_SHARD), BF16)
    ws = jax.ShapeDtypeStruct((half, 3, FF_SHARD), F32)
    bs = jax.ShapeDtypeStruct((half, 1, FF_SHARD), F32)
    dc, dwv, dwg, dbv, dbg = pl.pallas_call(
        body, name="conv_bwd_taps", grid=(half, bsz, nc),
        in_specs=[cur(0), cur(half), prv(0), prv(half), par(3, 0), par(3, half), par(1, 0), par(1, half), cur(0)],
        out_specs=[pl.BlockSpec((2, None, tc, FF_SHARD), lambda j, b, c: (0, j, b * nc + c, 0)),
                   par(3, 0), par(3, 0), par(1, 0), par(1, 0)],
        out_shape=[hs, ws, ws, bs, bs],
        compiler_params=_params(("parallel", "arbitrary", "arbitrary")),
    )(up, up, up, up, cw, cw, cb, cb, dact)
    return (dc.reshape(N_DEV, t, FF_SHARD), jnp.concatenate([dwv, dwg], axis=0),
            jnp.concatenate([dbv, dbg], axis=0))


def conv_bwd_input(dconv, cw, bsz, seq):
    tc = _tile(seq, 1024)
    nc = seq // tc
    hb = tc // CONV_HALO
    nblk = bsz * seq // CONV_HALO

    def body(d_ref, n_ref, w_ref, o_ref):
        c = pl.program_id(2)
        nxt = jnp.where(c < nc - 1, n_ref[...].astype(F32), 0.0)
        d = d_ref[...].astype(F32)
        d1, d2 = _shifts_up(d, nxt)
        w = w_ref[...]
        o_ref[...] = (w[2:3] * d + w[1:2] * d1 + w[0:1] * d2).astype(o_ref.dtype)

    cur = pl.BlockSpec((None, tc, FF_SHARD), lambda j, b, c: (j, b * nc + c, 0))
    return pl.pallas_call(
        body, name="conv_bwd_input", grid=(N_DEV, bsz, nc),
        in_specs=[cur, pl.BlockSpec((None, CONV_HALO, FF_SHARD),
                                    lambda j, b, c: (j, jnp.minimum((b * nc + c + 1) * hb, nblk - 1), 0)),
                  pl.BlockSpec((None, 3, FF_SHARD), lambda j, b, c: (j, 0, 0))],
        out_specs=cur, out_shape=jax.ShapeDtypeStruct(dconv.shape, BF16),
        compiler_params=_params(("parallel", "parallel", "parallel")),
    )(dconv, dconv, cw)


def _my_index():
    return 4 * lax.axis_index("x") + 2 * lax.axis_index("y") + lax.axis_index("c")


def _peer(k):
    return (lax.axis_index("x") ^ ((k >> 2) & 1), lax.axis_index("y") ^ ((k >> 1) & 1),
            lax.axis_index("c") ^ (k & 1))


_HBM = pl.BlockSpec(memory_space=pltpu.HBM)
_SEM = pl.BlockSpec(memory_space=pltpu.SEMAPHORE)
_DATAFLOW = pltpu.SideEffectType.DATAFLOW_SIDE_EFFECTING


def _split_copies(gather, src_ref, land_ref, send_sems, recv_sems, local_sem):
    me = _my_index()

    def part(j):
        return src_ref if gather else src_ref.at[j]

    local = pltpu.make_async_copy(part(me), land_ref.at[me], local_sem)
    sends = [pltpu.make_async_remote_copy(
        src_ref=part(me ^ k), dst_ref=land_ref.at[me], send_sem=send_sems.at[k - 1], recv_sem=recv_sems.at[k - 1],
        device_id=_peer(k), device_id_type=pl.DeviceIdType.MESH) for k in range(1, N_DEV)]
    recvs = [pltpu.make_async_remote_copy(
        src_ref=part(me ^ k), dst_ref=land_ref.at[me ^ k], send_sem=send_sems.at[k - 1], recv_sem=recv_sems.at[k - 1],
        device_id=_peer(k), device_id_type=pl.DeviceIdType.MESH) for k in range(1, N_DEV)]
    return local, sends, recvs


def split_start(name, srcs, gather):
    n = len(srcs)
    lands = [((N_DEV,) + s.shape) if gather else s.shape for s in srcs]

    def body(*refs):
        ins, outs = refs[:2 * n], refs[2 * n:]
        for i in range(n):
            local, sends, _ = _split_copies(gather, ins[i], ins[n + i], *outs[3 * i:3 * i + 3])
            local.start()
            for cp in sends:
                cp.start()
        outs[-1][...] = jnp.zeros_like(outs[-1])

    dma7 = pltpu.SemaphoreType.DMA((N_DEV - 1,))
    out = pl.pallas_call(
        body, name=name,
        out_shape=(dma7, dma7, pltpu.SemaphoreType.DMA(())) * n
                  + tuple(pltpu.HBM(s.shape, s.dtype) for s in srcs)
                  + tuple(pltpu.HBM(shape, s.dtype) for shape, s in zip(lands, srcs))
                  + (jax.ShapeDtypeStruct((8, 128), F32),),
        in_specs=(_HBM,) * (2 * n),
        out_specs=(_SEM,) * (3 * n) + (_HBM,) * (2 * n) + (pl.BlockSpec(memory_space=pltpu.VMEM),),
        input_output_aliases={i: 3 * n + i for i in range(2 * n)},
        compiler_params=pltpu.CompilerParams(has_side_effects=_DATAFLOW),
    )(*[pltpu.with_memory_space_constraint(s, pltpu.HBM) for s in srcs],
      *[pltpu.with_memory_space_constraint(lax.empty(shape, s.dtype), pltpu.HBM) for shape, s in zip(lands, srcs)])
    handles = [tuple(out[3 * i:3 * i + 3]) + (out[3 * n + i], out[4 * n + i]) for i in range(n)]
    return handles, out[-1][0, 0]


def split_wait(name, handles, after, gather):
    send_sems, recv_sems, local_sem, src_thru, land_thru = handles

    def body(src_ref, land_ref, send_sems, recv_sems, local_sem, after_ref, src_dead, got_ref, token):
        local, sends, recvs = _split_copies(gather, src_ref, land_ref, send_sems, recv_sems, local_sem)
        local.wait()
        for cp in recvs:
            cp.wait_send()
            cp.wait_recv()
        token[...] = jnp.zeros_like(token)

    out = pl.pallas_call(
        body, name=name,
        out_shape=(pltpu.HBM(src_thru.shape, src_thru.dtype), pltpu.HBM(land_thru.shape, land_thru.dtype),
                   jax.ShapeDtypeStruct((8, 128), F32)),
        in_specs=(_HBM, _HBM, _SEM, _SEM, _SEM, pl.BlockSpec(memory_space=pl.ANY)),
        out_specs=(_HBM, _HBM, pl.BlockSpec(memory_space=pltpu.VMEM)),
        input_output_aliases={0: 0, 1: 1},
        compiler_params=pltpu.CompilerParams(has_side_effects=_DATAFLOW),
    )(src_thru, land_thru, send_sems, recv_sems, local_sem, after)
    return out[1], out[2][0, 0]


def sum_parts(name, r):
    _, rows, cols = r.shape

    def body(r_ref, o_ref):
        acc = r_ref[0].astype(F32)
        for s in range(1, N_DEV):
            acc = acc + r_ref[s].astype(F32)
        o_ref[...] = acc

    return pl.pallas_call(body, name=name, out_shape=jax.ShapeDtypeStruct((rows, cols), F32),
                          compiler_params=_params())(r)


def adamw(name, w, m, v, parts=None, g=None, layer=0, into=None, order=None):
    _, rows, cols = w.shape
    br = _tile(rows, 256, 16)
    c1 = 1.0 / (1.0 - ADAM_B1 ** ADAM_STEP)
    c2 = 1.0 / (1.0 - ADAM_B2 ** ADAM_STEP)

    def body(g_ref, w_ref, m_ref, v_ref, *rest):
        og_ref, od_ref, om_ref, ov_ref = rest[-4:]
        if parts is None:
            gs = g_ref[...]
        else:
            gs = g_ref[0].astype(F32)
            for s in range(1, N_DEV):
                gs = gs + g_ref[s].astype(F32)
        mn = ADAM_B1 * m_ref[...] + (1.0 - ADAM_B1) * gs
        vn = ADAM_B2 * v_ref[...] + (1.0 - ADAM_B2) * (gs * gs)
        og_ref[...] = gs
        om_ref[...] = mn
        ov_ref[...] = vn
        od_ref[...] = -ADAM_LR * ((mn * c1) / (jnp.sqrt(vn * c2) + ADAM_EPS) + ADAM_WD * w_ref[...])

    blk = pl.BlockSpec((None, br, cols), lambda i: (layer, i, 0))
    if parts is None:
        gspec = pl.BlockSpec((br, cols), lambda i: (i, 0))
    else:
        gspec = pl.BlockSpec((N_DEV, br, cols), lambda i: (0, i, 0))
    earlier = [] if into is None else list(into)
    behind = [] if order is None else [order]
    return pl.pallas_call(
        body, name=name, grid=(rows // br,),
        in_specs=[gspec, blk, blk, blk] + [pl.BlockSpec(memory_space=pl.ANY)] * len(earlier)
                 + [pl.BlockSpec((1, 128), lambda i: (0, 0))] * len(behind),
        out_specs=[blk] * 4, out_shape=[jax.ShapeDtypeStruct(w.shape, F32)] * 4,
        input_output_aliases={4 + k: k for k in range(len(earlier))},
        compiler_params=_params(("parallel",)),
    )(g if parts is None else parts, w, m, v, *earlier, *behind)


SMALL = ("norm_mix", "norm_xattn", "norm_ffn", "norm_mem", "norm_final", "pool_w", "pool_scale",
         "ssm_lam_re", "ssm_lam_im", "ssm_log_dt", "ssm_b_re", "ssm_b_im", "ssm_c_re", "ssm_c_im",
         "ffn_conv_b", "ssm_d", "ffn_conv_w")
SMALL_SHARDED = {"ssm_d": 1, "ffn_conv_w": 2}
BIG = ("ab_w_in", "ab_w_out", "ssm_w_in", "ssm_w_glu", "xa_w_q", "xa_w_kv", "xa_w_o", "ffn_w_up", "ffn_w_down")
WEIGHTS = ("norm_mix", "norm_xattn", "norm_ffn", "norm_mem", "norm_final", "ab_w_in", "pool_w", "pool_scale",
           "ab_w_out", "ssm_w_in", "ssm_lam_re", "ssm_lam_im", "ssm_log_dt", "ssm_b_re", "ssm_b_im", "ssm_c_re",
           "ssm_c_im", "ssm_d", "ssm_w_glu", "xa_w_q", "xa_w_kv", "xa_w_o", "ffn_w_up", "ffn_conv_w", "ffn_conv_b",
           "ffn_w_down")


def _rows8(g):
    return g.reshape(N_DEV, g.size // (N_DEV * D_MODEL), D_MODEL)


def _square(a):
    return a.reshape(D_MODEL, D_MODEL)


_LAYOUT = {"ab_w_out": _square, "ssm_w_in": _square, "xa_w_q": _square, "xa_w_o": _square,
           "ffn_w_down": lambda a: a.reshape(N_DEV // 2, FF_SHARD, D_MODEL)}
GATHER_ORDER = (("ab_w_in", 0), ("ffn_conv_w", None), ("ssm_d", None), ("ab_w_out", 0), ("xa_w_q", 0),
                ("xa_w_kv", 0), ("xa_w_o", 0), ("ffn_w_up", 0), ("ffn_w_down", 0), ("ffn_w_up", 1),
                ("ffn_w_down", 1), ("ssm_w_in", 0), ("ssm_w_glu", 0), ("xa_w_q", 1), ("xa_w_kv", 1), ("xa_w_o", 1))
GATHER_AHEAD = 7
GATHER_BATCHES = (3, 8, 11, 16)


class _Step:
    def __init__(self, master, small):
        self.master, self.small = master, small
        self.pending, self.gathers, self.weights, self.sent, self.queued = [], {}, {}, [], []

    def follow(self, v):
        for z in self.pending:
            v = v + z
        self.pending = []
        return v

    def start_gathers(self, upto, zero):
        upto = min(end for end in GATHER_BATCHES if end >= min(upto, len(GATHER_ORDER)))
        todo = GATHER_ORDER[len(self.gathers):upto]
        if not todo:
            return
        shards = []
        for n, l in todo:
            if l is None:
                shards.append(self.master[n] + zero)
            else:
                shards.append((self.master[n][l] + zero).astype(MXU_DTYPE))
        handles, z = split_start(f"ags_{len(self.gathers)}", shards, gather=True)
        self.gathers.update(zip(todo, handles))
        self.pending.append(z)

    def weight(self, n, l, after):
        if (n, l) not in self.weights:
            full, z = split_wait(f"agw_{n}{'' if l is None else l}", self.gathers[(n, l)], after, gather=True)
            self.weights[(n, l)] = _LAYOUT.get(n, lambda a: a)(full)
            self.start_gathers(GATHER_ORDER.index((n, l)) + 1 + GATHER_AHEAD, z)
        return self.weights[(n, l)]

    def send_grad(self, n, l, part, flush=True):
        self.queued.append((n, l, part))
        if flush:
            handles, z = split_start(f"xs_{n}{l}", [p for _, _, p in self.queued], gather=False)
            self.sent += [(qn, ql, h) for (qn, ql, _), h in zip(self.queued, handles)]
            self.queued = []
            self.pending.append(z)


def _layer_tail(st, l, x_in, hq, mem_n, acts, next_gain=None):
    bsz, seq = acts["bsz"], acts["seq"]
    p = st.small
    q = mm_nn(f"xa_q{l}", hq, st.weight("xa_w_q", l, x_in))
    kv = mm_nn_bs(f"xa_kv{l}", mem_n, st.weight("xa_w_kv", l, x_in))
    o = xattn_fwd(q, kv, bsz, seq)
    x_mid, hf = mm_nn(f"xa_o{l}", o, st.weight("xa_w_o", l, o), res=x_in, out_dtype=F32,
                      norm_gain=st.follow(p["norm_ffn"][l]))
    up = mm_nn_bs(f"ffn_up{l}", hf, st.weight("ffn_w_up", l, x_mid), stacked_out=True)
    conv_w = st.weight("ffn_conv_w", None, x_mid)[:, l]
    act = conv_fwd(up, conv_w, p["ffn_conv_b"][l], bsz, seq)
    w_down = st.weight("ffn_w_down", l, act)
    if next_gain is None:
        x_out, h_next = mm_as_nn(f"ffn_down{l}", act, w_down, res=x_mid), None
    else:
        x_out, h_next = mm_as_nn(f"ffn_down{l}", act, w_down, res=x_mid, norm_gain=st.follow(next_gain))
    acts[l].update(x_in=x_in, hq=hq, q=q, kv=kv, o=o, x_mid=x_mid, hf=hf, up=up, act=act)
    return x_out, h_next


def _layer_tail_bwd(st, l, dx, mem_n, acts, grads):
    a = acts[l]
    bsz, seq = acts["bsz"], acts["seq"]
    p = st.small
    dact = mm_nt_os(f"d_act{l}", dx, st.weight("ffn_w_down", l, dx))
    st.send_grad("ffn_w_down", l, _rows8(mm_tn(f"g_ffn_down{l}", a["act"], dx, a_stacked=True)), flush=False)
    conv_w = st.weight("ffn_conv_w", None, dx)[:, l]
    dconv, dcw, dcb = conv_bwd_taps(a["up"], conv_w, p["ffn_conv_b"][l], dact, bsz, seq)
    grads["ffn_conv_w"][l] = dcw
    grads["ffn_conv_b"][l] = dcb
    dup = conv_bwd_input(dconv, conv_w, bsz, seq)
    dx_mid, grads["norm_ffn"][l] = mm_nt_bs(f"d_hf{l}", dup, st.weight("ffn_w_up", l, dx), dc_stacked=True,
                                            rms=(a["x_mid"], st.follow(p["norm_ffn"][l]), dx))
    st.send_grad("ffn_w_up", l, mm_tn(f"g_ffn_up{l}", a["hf"], dup, dc_stacked=True))
    do = mm_nt(f"d_o{l}", dx_mid, st.weight("xa_w_o", l, dx))
    st.send_grad("xa_w_o", l, _rows8(mm_tn(f"g_xa_o{l}", a["o"], dx_mid)), flush=False)
    dq, dk, dv = xattn_bwd(a["q"], a["kv"], do, bsz, seq)
    dkv = jnp.concatenate([dk, dv], axis=1).astype(BF16)
    dx_in, grads["norm_xattn"][l] = mm_nt(f"d_hq{l}", dq, st.weight("xa_w_q", l, dx),
                                          rms=(a["x_in"], st.follow(p["norm_xattn"][l]), dx_mid))
    st.send_grad("xa_w_q", l, _rows8(mm_tn(f"g_xa_q{l}", a["hq"], dq)), flush=False)
    dmem_n = mm_nt_bs(f"d_memn{l}", dkv, st.weight("xa_w_kv", l, dx), out_dtype=F32)
    st.send_grad("xa_w_kv", l, mm_tn(f"g_xa_kv{l}", mem_n, dkv, dc_cols=2 * D_MODEL // N_DEV))
    return dx_in, dmem_n


def kernel(x, mem, norm_mix, norm_xattn, norm_ffn, norm_mem, norm_final, ab_w_in, pool_w, pool_scale, ab_w_out, ssm_w_in, ssm_lam_re, ssm_lam_im, ssm_log_dt, ssm_b_re, ssm_b_im, ssm_c_re, ssm_c_im, ssm_d, ssm_w_glu, xa_w_q, xa_w_kv, xa_w_o, ffn_w_up, ffn_conv_w, ffn_conv_b, ffn_w_down, loss_target, m_norm_mix, m_norm_xattn, m_norm_ffn, m_norm_mem, m_norm_final, m_ab_w_in, m_pool_w, m_pool_scale, m_ab_w_out, m_ssm_w_in, m_ssm_lam_re, m_ssm_lam_im, m_ssm_log_dt, m_ssm_b_re, m_ssm_b_im, m_ssm_c_re, m_ssm_c_im, m_ssm_d, m_ssm_w_glu, m_xa_w_q, m_xa_w_kv, m_xa_w_o, m_ffn_w_up, m_ffn_conv_w, m_ffn_conv_b, m_ffn_w_down, v_norm_mix, v_norm_xattn, v_norm_ffn, v_norm_mem, v_norm_final, v_ab_w_in, v_pool_w, v_pool_scale, v_ab_w_out, v_ssm_w_in, v_ssm_lam_re, v_ssm_lam_im, v_ssm_log_dt, v_ssm_b_re, v_ssm_b_im, v_ssm_c_re, v_ssm_c_im, v_ssm_d, v_ssm_w_glu, v_xa_w_q, v_xa_w_kv, v_xa_w_o, v_ffn_w_up, v_ffn_conv_w, v_ffn_conv_b, v_ffn_w_down):
    given = dict(locals())
    master = {n: given[n] for n in WEIGHTS}
    mom1 = {n: given["m_" + n] for n in WEIGHTS}
    mom2 = {n: given["v_" + n] for n in WEIGHTS}
    bsz, seq, d = x.shape
    t = bsz * seq
    me = _my_index()

    st = _Step(master, {"norm_xattn": norm_xattn, "norm_ffn": norm_ffn,
                        "ffn_conv_b": [ffn_conv_b[l].reshape(N_DEV, 1, FF_SHARD) for l in range(2)]})
    st.start_gathers(1, 0.0)
    zero = st.follow(jnp.zeros((), F32))

    acts = {"bsz": bsz, "seq": seq, 0: {}, 1: {}}
    x0 = x.reshape(t, d)
    mem2 = mem.reshape(bsz * MEM_LEN, d)
    mem_n = rms_fwd("rms_mem", mem2, norm_mem + zero)
    pscale = pool_scale.reshape(1, SB_WIDTH)

    h0 = rms_fwd("rms_mix0", x0, norm_mix[0] + zero)
    w_in = st.weight("ab_w_in", 0, h0)
    proj = mm_nn_bs("ab_in", h0, w_in, out_dtype=F32)
    a_out, rsum = sb_attn_fwd(proj, st.follow(jnp.zeros((1, 128), F32)), bsz, seq)
    p_out = pool_fwd(proj, pool_w[0], pscale, bsz, seq)
    w_out = st.weight("ab_w_out", 0, a_out)
    x1 = mm_nn("ab_out_a", a_out, w_out, res=x0, out_dtype=F32)
    x1, hq0 = mm_nn("ab_out_p", p_out, w_out, res=x1, koff=SB_WIDTH, out_dtype=F32,
                    norm_gain=st.follow(norm_xattn[0]))
    x3, h1 = _layer_tail(st, 0, x1, hq0, mem_n, acts, next_gain=norm_mix[1])

    b_re2 = ssm_b_re.reshape(64, 1024)
    b_im2 = ssm_b_im.reshape(64, 1024)
    log_dt = ssm_log_dt.reshape(64, 1)
    lb_re, lb_im, bb_re2, bb_im2 = ssm_prep(ssm_lam_re[0], ssm_lam_im[0], log_dt, b_re2, b_im2)
    wt = _ssm_in_weights(bb_re2, bb_im2)
    ct = _ssm_out_weights(ssm_c_re[0], ssm_c_im[0])
    a_re = lb_re.reshape(1, SSM_STATES)
    a_im = lb_im.reshape(1, SSM_STATES)
    u = mm_nn("ssm_in", h1, st.weight("ssm_w_in", 0, x3), out_dtype=F32)
    dskip = st.weight("ssm_d", None, x3).reshape(1, D_MODEL)
    y, gl, h_re, h_im = ssm_fwd(u, wt, ct, a_re, a_im, dskip, bsz, seq)
    glu = mm_nn_bs("ssm_glu", gl, st.weight("ssm_w_glu", 0, gl), out_dtype=F32)
    x4, hq1 = glu_fwd(glu, x3, st.follow(norm_xattn[1]))
    x6, _ = _layer_tail(st, 1, x4, hq1, mem_n, acts)

    loss_row, dx, g_norm_final = loss_head(x6, norm_final, loss_target.reshape(t, d))
    loss = lax.psum(loss_row[0, 0], MESH_AXES)

    grads = {n: [None, None] for n in ("ffn_conv_w", "ffn_conv_b", "norm_ffn", "norm_xattn", "norm_mix")}
    dx4, dmem_1 = _layer_tail_bwd(st, 1, dx, mem_n, acts, grads)
    dglu = glu_bwd(glu, dx4)
    dgl = mm_nt_bs("d_gl", dglu, st.weight("ssm_w_glu", 0, dx))
    st.send_grad("ssm_w_glu", 0, mm_tn("g_ssm_glu", gl, dglu, dc_cols=2 * D_MODEL // N_DEV), flush=False)
    du, dwt, dct, g_dskip, da_re, da_im = ssm_bwd(dgl, y, u, h_re, h_im, wt, ct, a_re, a_im, dskip, bsz, seq)
    dbb_re, dbb_im = _ssm_in_weights_bwd(dwt)
    g_c_re, g_c_im = _ssm_out_weights_bwd(dct)
    g_lam_re, g_lam_im, g_log_dt, g_b_re, g_b_im = ssm_prep_bwd(
        ssm_lam_re[0], ssm_lam_im[0], log_dt, b_re2, b_im2, da_re.reshape(64, 64), da_im.reshape(64, 64),
        dbb_re, dbb_im)
    dx3, grads["norm_mix"][1] = mm_nt("d_h1", du, st.weight("ssm_w_in", 0, dx),
                                      rms=(x3, st.follow(norm_mix[1]), dx4))
    st.send_grad("ssm_w_in", 0, _rows8(mm_tn("g_ssm_in", h1, du)))

    dx1, dmem_0 = _layer_tail_bwd(st, 0, dx3, mem_n, acts, grads)
    dcat = mm_nt("d_cat", dx1, st.weight("ab_w_out", 0, dx))
    st.send_grad("ab_w_out", 0, _rows8(jnp.concatenate(
        [mm_tn("g_ab_out_a", a_out, dx1), mm_tn("g_ab_out_p", p_out, dx1)], axis=0)), flush=False)
    dq, dk, dv = sb_attn_bwd(proj, rsum, dcat, bsz, seq)
    dpu, g_pool_w, g_pool_scale = pool_bwd(proj, pool_w[0], st.follow(pscale), dcat, bsz, seq)
    dproj = jnp.concatenate([dq, dk, dv, dpu], axis=1).astype(BF16)
    st.send_grad("ab_w_in", 0, mm_tn("g_ab_in", h0, dproj, dc_cols=2 * D_MODEL // N_DEV))
    dx0, grads["norm_mix"][0] = mm_nt_bs("d_h0", dproj, st.weight("ab_w_in", 0, dx),
                                         rms=(x0, st.follow(norm_mix[0]), dx1))
    _, g_norm_mem = rms_bwd("rms_mem_bwd", mem2, norm_mem, dmem_0 + dmem_1, need_dx=False)

    small_g = {
        "norm_mix": jnp.stack([g[0] for g in grads["norm_mix"]]),
        "norm_xattn": jnp.stack([g[0] for g in grads["norm_xattn"]]),
        "norm_ffn": jnp.stack([g[0] for g in grads["norm_ffn"]]),
        "norm_mem": g_norm_mem[0], "norm_final": g_norm_final[0],
        "pool_w": g_pool_w[None], "pool_scale": g_pool_scale,
        "ssm_lam_re": g_lam_re[None], "ssm_lam_im": g_lam_im[None], "ssm_log_dt": g_log_dt.reshape(1, 64),
        "ssm_b_re": g_b_re.reshape(1, 64, 64, 16), "ssm_b_im": g_b_im.reshape(1, 64, 64, 16),
        "ssm_c_re": g_c_re[None], "ssm_c_im": g_c_im[None],
        "ffn_conv_b": jnp.stack([g.reshape(2 * D_FF) for g in grads["ffn_conv_b"]]),
        "ssm_d": g_dskip,
        "ffn_conv_w": jnp.stack([g.transpose(1, 0, 2).reshape(3, 2 * D_FF) for g in grads["ffn_conv_w"]]),
    }
    sizes = [int(small_g[n].size) for n in SMALL]
    total = sum(sizes)
    rows8 = -(-total // (N_DEV * 128 * 8)) * 8
    flat = jnp.concatenate([small_g[n].reshape(-1).astype(F32) for n in SMALL]
                           + [jnp.zeros((N_DEV * rows8 * 128 - total,), F32)])
    (in_flight,), z = split_start("xs_small", [flat.reshape(N_DEV, rows8, 128)], gather=False)
    st.pending.append(z)
    stepped, last = {}, dx0
    for i, (n, l, handles) in enumerate(st.sent):
        if i == len(st.sent) // 2:
            recv, _ = split_wait("xw_small", in_flight, last, gather=False)
            (in_flight,), z = split_start("ags_small", [sum_parts("sum_small", recv)], gather=True)
            st.pending.append(z)
        recv, _ = split_wait(f"xw_{n}{l}", handles, dx0, gather=False)
        shape3 = (master[n].shape[0],) + recv.shape[1:]
        stepped[n] = adamw(f"adamw_{n}{l}", master[n].reshape(shape3), mom1[n].reshape(shape3),
                           mom2[n].reshape(shape3), parts=recv, layer=l, into=stepped.get(n),
                           order=st.follow(jnp.zeros((1, 128), F32)))
        last = stepped[n][0]
    out_g, out_d, out_m, out_v = ({n: stepped[n][k].reshape(master[n].shape) for n in BIG} for k in range(4))
    summed = split_wait("agw_small", in_flight, last, gather=True)[0].reshape(-1)

    def local_part(name, a):
        ax = SMALL_SHARDED.get(name)
        if ax is None:
            return a
        n_loc = a.shape[ax] // N_DEV
        return lax.dynamic_slice_in_dim(a, me * n_loc, n_loc, axis=ax)

    off = 0
    for n, sz in zip(SMALL, sizes):
        g_n = local_part(n, summed[off:off + sz].reshape(small_g[n].shape))
        off += sz
        cols = g_n.shape[-1] if g_n.shape[-1] >= 128 or g_n.ndim < 3 else g_n.shape[-1] * g_n.shape[-2]
        shape3 = (1, g_n.size // cols, cols)
        res = adamw("adamw_" + n, master[n].reshape(shape3), mom1[n].reshape(shape3), mom2[n].reshape(shape3),
                    g=g_n.reshape(shape3[1:]))
        for dst, r in zip((out_g, out_d, out_m, out_v), res):
            dst[n] = r.reshape(master[n].shape)

    return (loss, dx0.reshape(bsz, seq, d), *[out_g[n] for n in WEIGHTS], *[out_d[n] for n in WEIGHTS],
            *[out_m[n] for n in WEIGHTS], *[out_v[n] for n in WEIGHTS])
```

```python
import math

import jax
import jax.numpy as jnp
from jax import lax
from jax.experimental import pallas as pl
from jax.experimental.pallas import tpu as pltpu

F32 = jnp.float32
BF16 = jnp.bfloat16
MXU_DTYPE = jnp.bfloat16
N_DEV = 8
MESH_AXES = ("x", "y", "c")

D_MODEL = 1024
SB_HEAD_DIM = 64
SB_WIDTH = 512
SB_BLOCK = 256
POOL_WINDOWS = (2, 4, 8, 16)
POOL_GROUP = 128
POOL_HALO = 16
SSM_TILES = 8
SSM_TILE_STATES = 512
SSM_STATES = 4096
SSM_LANES = 1024
MEM_LEN = 256
XA_HEADS = 4
XA_HEAD_DIM = 256
D_FF = 2816
FF_SHARD = 704
EPS = 1e-6
ADAM_LR = 0.001
ADAM_B1 = 0.9
ADAM_B2 = 0.999
ADAM_EPS = 1e-08
ADAM_WD = 0.01
ADAM_STEP = 10
VMEM_LIMIT = 56 * 1024 * 1024

_NN = (((1,), (0,)), ((), ()))
_NT = (((1,), (1,)), ((), ()))
_TN = (((0,), (0,)), ((), ()))


def _params(sem=None):
    if sem is None:
        return pltpu.CompilerParams(vmem_limit_bytes=VMEM_LIMIT)
    return pltpu.CompilerParams(dimension_semantics=sem, vmem_limit_bytes=VMEM_LIMIT)


def _tile(n, pref, mult=8):
    if n <= pref:
        return n
    for t in range(pref, 0, -1):
        if n % t == 0 and t % mult == 0:
            return t
    return n


def _dot(a, b, dims):
    return lax.dot_general(a.astype(MXU_DTYPE), b.astype(MXU_DTYPE), dims, preferred_element_type=F32)


def _dot_exact01(x, m01, dims=_NN):
    x1 = x.astype(BF16)
    r1 = x - x1.astype(F32)
    x2 = r1.astype(BF16)
    x3 = (r1 - x2.astype(F32)).astype(BF16)
    m = m01.astype(BF16)
    out = lax.dot_general(x1, m, dims, preferred_element_type=F32)
    out = out + lax.dot_general(x2, m, dims, preferred_element_type=F32)
    return out + lax.dot_general(x3, m, dims, preferred_element_type=F32)


def _mm(name, a, b, dims, grid, a_spec, b_spec, o_spec, out_shape, out_dtype, acc_shape, res=None, r_spec=None,
        group=1, n=None, a_sel="full", b_sel="full", o_sel="full", norm_gain=None, rms=None):
    nk = grid[2]
    if out_dtype is None:
        out_dtype = BF16
    n_out = out_shape[-1]
    vec = pl.BlockSpec((1, n_out), lambda i, j, kk: (0, 0))

    def at(sel, s):
        if sel == "lead":
            return (s,)
        if sel == "lanes":
            return (slice(None), slice(s * n, (s + 1) * n))
        return (Ellipsis,)

    extra = [] if res is None else [(res, r_spec)]
    if norm_gain is not None:
        extra.append((norm_gain.reshape(1, n_out), vec))
    if rms is not None:
        extra += [(rms[0], o_spec), (rms[1].reshape(1, n_out), vec), (rms[2], o_spec)]
    n_in = 2 + len(extra)
    if rms is not None:
        out_specs = [o_spec, vec]
        out_shapes = [jax.ShapeDtypeStruct(out_shape, F32), jax.ShapeDtypeStruct((1, n_out), F32)]
    elif norm_gain is not None:
        out_specs = [o_spec, o_spec]
        out_shapes = [jax.ShapeDtypeStruct(out_shape, out_dtype), jax.ShapeDtypeStruct(out_shape, BF16)]
    else:
        out_specs, out_shapes = o_spec, jax.ShapeDtypeStruct(out_shape, out_dtype)

    def body(*refs):
        a_ref, b_ref = refs[0], refs[1]
        ins = list(refs[2:n_in])
        r_ref = ins.pop(0) if res is not None else None
        outs = refs[n_in:]
        o_ref = outs[0]
        acc = refs[-1] if nk > 1 else None
        k = pl.program_id(2)

        def finish(val):
            if r_ref is not None:
                val = val + r_ref[...].astype(F32)
            if rms is not None:
                x_ref, g_ref, d_ref = ins
                xf = x_ref[...]
                r = lax.rsqrt(jnp.mean(xf * xf, axis=-1, keepdims=True) + EPS)
                xh = xf * r
                part = jnp.sum(val * xh, axis=0, keepdims=True)
                first = pl.program_id(0) == 0

                @pl.when(first)
                def _():
                    outs[1][...] = part

                @pl.when(jnp.logical_not(first))
                def _():
                    outs[1][...] += part

                dxh = val * g_ref[...]
                o_ref[...] = d_ref[...] + r * (dxh - xh * jnp.mean(dxh * xh, axis=-1, keepdims=True))
                return
            o_ref[...] = val.astype(out_dtype)
            if norm_gain is not None:
                r = lax.rsqrt(jnp.mean(val * val, axis=-1, keepdims=True) + EPS)
                outs[1][...] = (val * r * ins[0][...]).astype(BF16)

        def emit(s, val):
            if nk == 1:
                if o_sel == "full":
                    finish(val)
                else:
                    o_ref[at(o_sel, s)] = val.astype(out_dtype)
                return

            @pl.when(k == 0)
            def _():
                acc[at(o_sel, s)] = val

            @pl.when(k > 0)
            def _():
                acc[at(o_sel, s)] += val

        total = None
        if a_sel == "full" and b_sel == "lanes":
            wide = _dot(a_ref[...], b_ref[...], dims)
            for s in range(group):
                emit(s, wide[:, s * n:(s + 1) * n])
        else:
            for s in range(group):
                val = _dot(a_ref[at(a_sel, s)], b_ref[at(b_sel, s)], dims)
                if o_sel == "full":
                    total = val if total is None else total + val
                else:
                    emit(s, val)
        if o_sel == "full":
            emit(0, total)
        if nk > 1:
            @pl.when(k == nk - 1)
            def _():
                if o_sel == "full":
                    finish(acc[...])
                else:
                    o_ref[...] = acc[...].astype(out_dtype)

    rows_sem = "arbitrary" if rms is not None else "parallel"
    return pl.pallas_call(
        body, name=name, grid=grid, in_specs=[a_spec, b_spec] + [s for _, s in extra], out_specs=out_specs,
        out_shape=out_shapes, scratch_shapes=[pltpu.VMEM(acc_shape, F32)] if nk > 1 else [],
        compiler_params=_params((rows_sem, rows_sem, "arbitrary")),
    )(a, b, *[x for x, _ in extra])


def _row_tile(m, epi):
    return _tile(m, 512 if epi.get("rms") is not None else 1024)


def mm_nn(name, a, b, res=None, koff=0, out_dtype=None, **epi):
    m, k = a.shape
    n = b.shape[1]
    tm, tn, tk = _row_tile(m, epi), _tile(n, 1024, 128), _tile(k, 1024, 128)
    kb = koff // tk
    spec = pl.BlockSpec((tm, tn), lambda i, j, kk: (i, j))
    return _mm(name, a, b, _NN, (m // tm, n // tn, k // tk),
               pl.BlockSpec((tm, tk), lambda i, j, kk: (i, kk)),
               pl.BlockSpec((tk, tn), lambda i, j, kk: (kk + kb, j)),
               spec, (m, n), out_dtype, (tm, tn), res, spec, **epi)


def mm_nn_bs(name, a, bs, stacked_out=False, out_dtype=None):
    m, k = a.shape
    s, _, n = bs.shape
    tm, tk = _tile(m, 2048 if stacked_out else 1024), _tile(k, 1024, 128)
    a_spec = pl.BlockSpec((tm, tk), lambda i, j, kk: (i, kk))
    if stacked_out:
        return _mm(name, a, bs, _NN, (m // tm, s, k // tk), a_spec,
                   pl.BlockSpec((None, tk, n), lambda i, j, kk: (j, kk, 0)),
                   pl.BlockSpec((None, tm, n), lambda i, j, kk: (j, i, 0)), (s, m, n), out_dtype, (tm, n))
    g = _tile(s, max(1, 1024 // n), 1)
    return _mm(name, a, bs, _NN, (m // tm, s // g, k // tk), a_spec,
               pl.BlockSpec((g, tk, n), lambda i, j, kk: (j, kk, 0)),
               pl.BlockSpec((tm, g * n), lambda i, j, kk: (i, j)), (m, s * n), out_dtype, (tm, g * n),
               group=g, n=n, b_sel="lead", o_sel="lanes")


def mm_as_nn(name, a_st, b3, res, out_dtype=F32, **epi):
    s, m, kp = a_st.shape
    n = b3.shape[2]
    tm, tn = _row_tile(m, epi), _tile(n, 1024, 128)
    spec = pl.BlockSpec((tm, tn), lambda i, j, kk: (i, j))
    g = _tile(s, 2, 1)
    return _mm(name, a_st, b3, _NN, (m // tm, n // tn, s // g),
               pl.BlockSpec((g, tm, kp), lambda i, j, kk: (kk, i, 0)),
               pl.BlockSpec((g, kp, tn), lambda i, j, kk: (kk, 0, j)),
               spec, (m, n), out_dtype, (tm, tn), res, spec, group=g, a_sel="lead", b_sel="lead", **epi)


def mm_nt(name, dc, b, out_dtype=None, **epi):
    m, n = dc.shape
    k = b.shape[0]
    tm, tko, tnr = _row_tile(m, epi), _tile(k, 1024, 128), _tile(n, 1024, 128)
    return _mm(name, dc, b, _NT, (m // tm, k // tko, n // tnr),
               pl.BlockSpec((tm, tnr), lambda i, j, kk: (i, kk)),
               pl.BlockSpec((tko, tnr), lambda i, j, kk: (j, kk)),
               pl.BlockSpec((tm, tko), lambda i, j, kk: (i, j)), (m, k), out_dtype, (tm, tko), **epi)


def mm_nt_bs(name, dc, bs, dc_stacked=False, out_dtype=None, **epi):
    s, k, n = bs.shape
    m = dc.shape[1] if dc_stacked else dc.shape[0]
    tm, tko = (_tile(m, 1024) if dc_stacked else _row_tile(m, epi)), _tile(k, 1024, 128)
    o_spec = pl.BlockSpec((tm, tko), lambda i, j, kk: (i, j))
    if dc_stacked:
        g = _tile(s, 2, 1)
        return _mm(name, dc, bs, _NT, (m // tm, k // tko, s // g),
                   pl.BlockSpec((g, tm, n), lambda i, j, kk: (kk, i, 0)),
                   pl.BlockSpec((g, tko, n), lambda i, j, kk: (kk, j, 0)), o_spec, (m, k), out_dtype, (tm, tko),
                   group=g, a_sel="lead", b_sel="lead", **epi)
    g = _tile(s, max(1, 2048 // n), 1)
    return _mm(name, dc, bs, _NT, (m // tm, k // tko, s // g),
               pl.BlockSpec((tm, g * n), lambda i, j, kk: (i, kk)),
               pl.BlockSpec((g, tko, n), lambda i, j, kk: (kk, j, 0)), o_spec, (m, k), out_dtype, (tm, tko),
               group=g, n=n, a_sel="lanes", b_sel="lead", **epi)


def mm_nt_os(name, dc, b3, out_dtype=None):
    m, n = dc.shape
    s, kp, _ = b3.shape
    tm, tnr = _tile(m, 2048), _tile(n, 1024, 128)
    return _mm(name, dc, b3, _NT, (m // tm, s, n // tnr),
               pl.BlockSpec((tm, tnr), lambda i, j, kk: (i, kk)),
               pl.BlockSpec((None, kp, tnr), lambda i, j, kk: (j, 0, kk)),
               pl.BlockSpec((None, tm, kp), lambda i, j, kk: (j, i, 0)), (s, m, kp), out_dtype, (tm, kp))


def mm_tn(name, a, dc, a_stacked=False, dc_cols=None, dc_stacked=False, out_dtype=None):
    if a_stacked:
        s, m, kp = a.shape
        n = dc.shape[1]
        tno, tmr = _tile(n, 1024, 128), _tile(m, 2048)
        return _mm(name, a, dc, _TN, (s, n // tno, m // tmr),
                   pl.BlockSpec((None, tmr, kp), lambda i, j, kk: (i, kk, 0)),
                   pl.BlockSpec((tmr, tno), lambda i, j, kk: (kk, j)),
                   pl.BlockSpec((None, kp, tno), lambda i, j, kk: (i, 0, j)), (s, kp, n), out_dtype, (kp, tno))
    m, k = a.shape
    tko, tmr = _tile(k, 1024, 128), _tile(m, 2048)
    a_spec = pl.BlockSpec((tmr, tko), lambda i, j, kk: (kk, i))
    if dc_stacked:
        s, _, n = dc.shape
        return _mm(name, a, dc, _TN, (k // tko, s, m // tmr), a_spec,
                   pl.BlockSpec((None, tmr, n), lambda i, j, kk: (j, kk, 0)),
                   pl.BlockSpec((None, tko, n), lambda i, j, kk: (j, i, 0)), (s, k, n), out_dtype, (tko, n))
    if dc_cols is not None:
        n = dc_cols
        s = dc.shape[1] // n
        g = _tile(s, max(1, 1024 // n), 1)
        return _mm(name, a, dc, _TN, (k // tko, s // g, m // tmr), a_spec,
                   pl.BlockSpec((tmr, g * n), lambda i, j, kk: (kk, j)),
                   pl.BlockSpec((g, tko, n), lambda i, j, kk: (j, i, 0)), (s, k, n), out_dtype, (g, tko, n),
                   group=g, n=n, b_sel="lanes", o_sel="lead")
    n = dc.shape[1]
    tno = _tile(n, 1024, 128)
    return _mm(name, a, dc, _TN, (k // tko, n // tno, m // tmr), a_spec,
               pl.BlockSpec((tmr, tno), lambda i, j, kk: (kk, j)),
               pl.BlockSpec((tko, tno), lambda i, j, kk: (i, j)), (k, n), out_dtype, (tko, tno))


def rms_fwd(name, x, g):
    t, d = x.shape
    tr = _tile(t, 512)

    def body(x_ref, g_ref, o_ref):
        xf = x_ref[...]
        r = lax.rsqrt(jnp.mean(xf * xf, axis=-1, keepdims=True) + EPS)
        o_ref[...] = (xf * r * g_ref[...]).astype(o_ref.dtype)

    return pl.pallas_call(
        body, name=name, grid=(t // tr,),
        in_specs=[pl.BlockSpec((tr, d), lambda i: (i, 0)), pl.BlockSpec((1, d), lambda i: (0, 0))],
        out_specs=pl.BlockSpec((tr, d), lambda i: (i, 0)),
        out_shape=jax.ShapeDtypeStruct((t, d), BF16), compiler_params=_params(("parallel",)),
    )(x, g.reshape(1, d))


def rms_bwd(name, x, g, dh, dres=None, need_dx=True):
    t, d = x.shape
    tr = _tile(t, 512)

    def body(*refs):
        refs = list(refs)
        x_ref, g_ref, dh_ref = refs[:3]
        r_ref = refs[3] if dres is not None else None
        outs = refs[4:] if dres is not None else refs[3:]
        dx_ref, dg_ref = (outs[0], outs[1]) if need_dx else (None, outs[0])
        i = pl.program_id(0)

        @pl.when(i == 0)
        def _():
            dg_ref[...] = jnp.zeros_like(dg_ref)

        xf = x_ref[...]
        dhf = dh_ref[...].astype(F32)
        r = lax.rsqrt(jnp.mean(xf * xf, axis=-1, keepdims=True) + EPS)
        xh = xf * r
        dg_ref[...] += jnp.sum(dhf * xh, axis=0, keepdims=True)
        if need_dx:
            dxh = dhf * g_ref[...]
            dx = r * (dxh - xh * jnp.mean(dxh * xh, axis=-1, keepdims=True))
            if r_ref is not None:
                dx = dx + r_ref[...]
            dx_ref[...] = dx

    row = pl.BlockSpec((tr, d), lambda i: (i, 0))
    vec = pl.BlockSpec((1, d), lambda i: (0, 0))
    in_specs = [row, vec, row] + ([row] if dres is not None else [])
    args = (x, g.reshape(1, d), dh) + ((dres,) if dres is not None else ())
    out_specs = ([row] if need_dx else []) + [vec]
    out_shape = ([jax.ShapeDtypeStruct((t, d), F32)] if need_dx else []) + [jax.ShapeDtypeStruct((1, d), F32)]
    res = pl.pallas_call(
        body, name=name, grid=(t // tr,), in_specs=in_specs, out_specs=out_specs, out_shape=out_shape,
        compiler_params=_params(("arbitrary",)),
    )(*args)
    return res if need_dx else (None, res[0])


def loss_head(x, g, tgt):
    t, d = x.shape
    tr = _tile(t, 512)

    def body(x_ref, g_ref, t_ref, l_ref, dx_ref, dg_ref):
        i = pl.program_id(0)

        @pl.when(i == 0)
        def _():
            l_ref[...] = jnp.zeros_like(l_ref)
            dg_ref[...] = jnp.zeros_like(dg_ref)

        xf = x_ref[...]
        r = lax.rsqrt(jnp.mean(xf * xf, axis=-1, keepdims=True) + EPS)
        xh = xf * r
        diff = xh * g_ref[...] - t_ref[...]
        l_ref[...] += 0.5 * jnp.sum(jnp.mean(diff * diff, axis=-1, keepdims=True))
        dy = diff * (1.0 / d)
        dg_ref[...] += jnp.sum(dy * xh, axis=0, keepdims=True)
        dxh = dy * g_ref[...]
        dx_ref[...] = r * (dxh - xh * jnp.mean(dxh * xh, axis=-1, keepdims=True))

    row = pl.BlockSpec((tr, d), lambda i: (i, 0))
    vec = pl.BlockSpec((1, d), lambda i: (0, 0))
    return pl.pallas_call(
        body, name="loss_head", grid=(t // tr,), in_specs=[row, vec, row],
        out_specs=[pl.BlockSpec((1, 128), lambda i: (0, 0)), row, vec],
        out_shape=[jax.ShapeDtypeStruct((1, 128), F32), jax.ShapeDtypeStruct((t, d), F32),
                   jax.ShapeDtypeStruct((1, d), F32)],
        compiler_params=_params(("arbitrary",)),
    )(x, g.reshape(1, d), tgt)


def glu_fwd(glu, x, gain):
    t, d = x.shape
    tr = _tile(t, 512)

    def body(v_ref, g_ref, x_ref, n_ref, o_ref, h_ref):
        y = x_ref[...] + v_ref[...] * jax.nn.sigmoid(g_ref[...])
        o_ref[...] = y
        r = lax.rsqrt(jnp.mean(y * y, axis=-1, keepdims=True) + EPS)
        h_ref[...] = (y * r * n_ref[...]).astype(h_ref.dtype)

    row = pl.BlockSpec((tr, d), lambda i: (i, 0))
    return pl.pallas_call(
        body, name="glu_fwd", grid=(t // tr,),
        in_specs=[row, pl.BlockSpec((tr, d), lambda i: (i, 1)), row, pl.BlockSpec((1, d), lambda i: (0, 0))],
        out_specs=[row, row],
        out_shape=[jax.ShapeDtypeStruct((t, d), F32), jax.ShapeDtypeStruct((t, d), BF16)],
        compiler_params=_params(("parallel",)),
    )(glu, glu, x, gain.reshape(1, d))


def glu_bwd(glu, dmix):
    t, d = dmix.shape
    tr = _tile(t, 512)

    def body(v_ref, g_ref, d_ref, o_ref):
        sg = jax.nn.sigmoid(g_ref[...])
        dm = d_ref[...]
        o_ref[:, :d] = (dm * sg).astype(o_ref.dtype)
        o_ref[:, d:] = (dm * v_ref[...] * sg * (1.0 - sg)).astype(o_ref.dtype)

    return pl.pallas_call(
        body, name="glu_bwd", grid=(t // tr,),
        in_specs=[pl.BlockSpec((tr, d), lambda i: (i, 0)), pl.BlockSpec((tr, d), lambda i: (i, 1)),
                  pl.BlockSpec((tr, d), lambda i: (i, 0))],
        out_specs=pl.BlockSpec((tr, 2 * d), lambda i: (i, 0)),
        out_shape=jax.ShapeDtypeStruct((t, 2 * d), BF16), compiler_params=_params(("parallel",)),
    )(glu, glu, dmix)


def _head_masks(shape):
    lane = lax.broadcasted_iota(jnp.int32, shape, 1)
    return lane < SB_HEAD_DIM


def _stack_heads(xf, is_a):
    return jnp.concatenate([jnp.where(is_a, xf, 0.0), jnp.where(is_a, 0.0, xf)], axis=0).astype(MXU_DTYPE)


def _diag_mask(qb, row0, rows):
    row = (lax.broadcasted_iota(jnp.int32, (rows, qb), 0) + row0) & (qb - 1)
    col = lax.broadcasted_iota(jnp.int32, (rows, qb), 1)
    return col < row


def _tri01(qb, pred):
    j = lax.broadcasted_iota(jnp.int32, (qb, qb), 0)
    s = lax.broadcasted_iota(jnp.int32, (qb, qb), 1)
    m = pred(j, s).astype(BF16)
    return jnp.concatenate([m, m], axis=0)


def _split_cat(x):
    hi = x.astype(BF16)
    lo = (x - hi.astype(F32)).astype(BF16)
    return jnp.concatenate([hi, lo], axis=1)


def sb_attn_fwd(proj, order, bsz, seq):
    qb = SB_BLOCK
    nq = seq // qb
    npair = SB_WIDTH // 128
    scale = SB_HEAD_DIM ** -0.5

    def body(q_ref, k_ref, v_ref, order_ref, o_ref, r_ref):
        qi = pl.program_id(2)
        is_a = _head_masks((qb, 128))
        q2 = _stack_heads(q_ref[...] * scale, is_a)
        diag = _diag_mask(qb, 0, 2 * qb)
        upper = _tri01(qb, lambda j, s: j > s)

        def blocks(kbs, acc, run, masked):
            sl = [pl.ds(pl.multiple_of(kb * qb, qb), qb) for kb in kbs]
            zs = [lax.dot_general(q2, k_ref[s, :].astype(MXU_DTYPE), _NT, preferred_element_type=F32) for s in sl]
            lks = [-jnp.maximum(z, 0.0) - jnp.log(1.0 + jnp.exp(-jnp.abs(z))) for z in zs]
            lbs = [lk + z for lk, z in zip(lks, zs)]
            if masked:
                lks = [jnp.where(diag, lk, 0.0) for lk in lks]
            cs = [lax.dot_general(_split_cat(lk), upper, _NN, preferred_element_type=F32) for lk in lks]
            for lk, lb, c, s in zip(lks, lbs, cs, sl):
                w = jnp.exp(lb + (run + c))
                if masked:
                    w = jnp.where(diag, w, 0.0)
                acc = acc + lax.dot_general(w.astype(MXU_DTYPE), v_ref[s, :].astype(MXU_DTYPE), _NN,
                                            preferred_element_type=F32)
                run = run + jnp.sum(lk, axis=1, keepdims=True)
            return acc, run

        carry = blocks([qi], jnp.zeros((2 * qb, 128), F32), jnp.zeros((2 * qb, 1), F32), True)
        carry = lax.cond(qi % 2 == 1, lambda c: blocks([qi - 1], c[0], c[1], False), lambda c: c, carry)
        top = qi - qi % 2
        acc, run = lax.fori_loop(
            0, qi // 2, lambda i, c: blocks([top - 1 - 2 * i, top - 2 - 2 * i], c[0], c[1], False), carry)
        o_ref[...] = jnp.where(is_a, acc[:qb], acc[qb:]).astype(o_ref.dtype)
        r_ref[...] = jnp.where(is_a, run[:qb], run[qb:])

    return pl.pallas_call(
        body, name="sb_attn_fwd", grid=(bsz, npair, nq),
        in_specs=[pl.BlockSpec((qb, 128), lambda b, p, i: (b * nq + i, p)),
                  pl.BlockSpec((seq, 128), lambda b, p, i: (b, npair + p)),
                  pl.BlockSpec((seq, 128), lambda b, p, i: (b, 2 * npair + p)),
                  pl.BlockSpec((1, 128), lambda b, p, i: (0, 0))],
        out_specs=[pl.BlockSpec((qb, 128), lambda b, p, i: (b * nq + i, p)),
                   pl.BlockSpec((qb, 128), lambda b, p, i: (b * nq + i, p))],
        out_shape=[jax.ShapeDtypeStruct((bsz * seq, SB_WIDTH), BF16),
                   jax.ShapeDtypeStruct((bsz * seq, SB_WIDTH), F32)],
        compiler_params=_params(("parallel", "parallel", "arbitrary")),
    )(proj, proj, proj, order)


def sb_attn_bwd(proj, rsum, dcat, bsz, seq):
    qb = SB_BLOCK
    nq = seq // qb
    npair = SB_WIDTH // 128
    scale = SB_HEAD_DIM ** -0.5

    def body(q_ref, k_ref, v_ref, r_ref, do_ref, dq_ref, dk_ref, dv_ref):
        qi = pl.program_id(2)

        @pl.when(qi == 0)
        def _():
            dk_ref[...] = jnp.zeros_like(dk_ref)
            dv_ref[...] = jnp.zeros_like(dv_ref)

        is_a = _head_masks((qb, 128))
        q2 = _stack_heads(q_ref[...] * scale, is_a)
        do2 = _stack_heads(do_ref[...].astype(F32), is_a)
        rf = r_ref[...]
        rtot = jnp.concatenate([rf[:, 0:1], rf[:, SB_HEAD_DIM:SB_HEAD_DIM + 1]], axis=0)
        diag = _diag_mask(qb, 0, 2 * qb)
        incl = _tri01(qb, lambda j, s: j <= s)
        strict = _tri01(qb, lambda j, s: j < s)

        def blocks(kbs, dq, pre, epre, masked):
            sl = [pl.ds(pl.multiple_of(kb * qb, qb), qb) for kb in kbs]
            ks = [k_ref[s, :].astype(MXU_DTYPE) for s in sl]
            vs = [v_ref[s, :].astype(MXU_DTYPE) for s in sl]
            zs = [lax.dot_general(q2, kblk, _NT, preferred_element_type=F32) for kblk in ks]
            dws = [lax.dot_general(do2, vblk, _NT, preferred_element_type=F32) for vblk in vs]
            lks = [-jnp.maximum(z, 0.0) - jnp.log(1.0 + jnp.exp(-jnp.abs(z))) for z in zs]
            lbs = [lk + z for lk, z in zip(lks, zs)]
            if masked:
                lks = [jnp.where(diag, lk, 0.0) for lk in lks]
            ps = [lax.dot_general(_split_cat(lk), incl, _NN, preferred_element_type=F32) for lk in lks]
            ws, es = [], []
            for lk, lb, p, dw in zip(lks, lbs, ps, dws):
                w = jnp.exp(lb + (rtot - (pre + p)))
                if masked:
                    w = jnp.where(diag, w, 0.0)
                ws.append(w)
                es.append(dw * w)
                pre = pre + jnp.sum(lk, axis=1, keepdims=True)
            cs = [lax.dot_general(_split_cat(e), strict, _NN, preferred_element_type=F32) for e in es]
            for e, lb, c, w, kblk, s in zip(es, lbs, cs, ws, ks, sl):
                dz = e - jnp.exp(lb) * (e + (epre + c))
                if masked:
                    dz = jnp.where(diag, dz, 0.0)
                dz = dz.astype(MXU_DTYPE)
                dq = dq + lax.dot_general(dz, kblk, _NN, preferred_element_type=F32)
                dk_ref[s, :] += lax.dot_general(dz, q2, _TN, preferred_element_type=F32)
                dv_ref[s, :] += lax.dot_general(w.astype(MXU_DTYPE), do2, _TN, preferred_element_type=F32)
                epre = epre + jnp.sum(e, axis=1, keepdims=True)
            return dq, pre, epre

        zc = jnp.zeros((2 * qb, 1), F32)
        carry = lax.fori_loop(0, qi // 2, lambda i, c: blocks([2 * i, 2 * i + 1], c[0], c[1], c[2], False),
                              (jnp.zeros((2 * qb, 128), F32), zc, zc))
        carry = lax.cond(qi % 2 == 1, lambda c: blocks([qi - 1], c[0], c[1], c[2], False), lambda c: c, carry)
        dq = blocks([qi], carry[0], carry[1], carry[2], True)[0]
        dq_ref[...] = jnp.where(is_a, dq[:qb], dq[qb:]) * scale

    full = jax.ShapeDtypeStruct((bsz * seq, SB_WIDTH), F32)
    qspec = pl.BlockSpec((qb, 128), lambda b, p, i: (b * nq + i, p))
    return pl.pallas_call(
        body, name="sb_attn_bwd", grid=(bsz, npair, nq),
        in_specs=[qspec,
                  pl.BlockSpec((seq, 128), lambda b, p, i: (b, npair + p)),
                  pl.BlockSpec((seq, 128), lambda b, p, i: (b, 2 * npair + p)),
                  qspec, qspec],
        out_specs=[qspec, pl.BlockSpec((seq, 128), lambda b, p, i: (b, p)),
                   pl.BlockSpec((seq, 128), lambda b, p, i: (b, p))],
        out_shape=[full, full, full],
        compiler_params=_params(("parallel", "parallel", "arbitrary")),
    )(proj, proj, proj, rsum, dcat)


def _window_sums(x, forward):
    n = x.shape[0]
    out = []
    s = x
    for sh in (1, 2, 4, 8):
        s = s + pltpu.roll(s, (n - sh) if forward else sh, 0)
        out.append(s)
    return out


def _pool_counts(tc, c, w):
    t = lax.broadcasted_iota(jnp.int32, (tc, 1), 0) + c * tc
    return jnp.minimum(t + 1, w).astype(F32)


def pool_fwd(proj, pool_w, pool_scale, bsz, seq):
    tc = _tile(seq, 512)
    nc = seq // tc
    hb = tc // POOL_HALO
    ucol = 3

    def body(u_ref, prev_ref, w_ref, s_ref, o_ref):
        c = pl.program_id(1)
        prev = jnp.where(c > 0, prev_ref[...], 0.0)
        x = jnp.concatenate([prev, u_ref[...]], axis=0)
        sums = _window_sums(x, forward=False)
        for g, win in enumerate(POOL_WINDOWS):
            ls = slice(g * POOL_GROUP, (g + 1) * POOL_GROUP)
            pooled = sums[g][POOL_HALO:, ls] / _pool_counts(tc, c, win) - x[POOL_HALO:, ls]
            y = _dot(pooled, w_ref[g], _NN)
            o_ref[:, ls] = (y * s_ref[:, ls]).astype(o_ref.dtype)

    return pl.pallas_call(
        body, name="pool_fwd", grid=(bsz, nc),
        in_specs=[pl.BlockSpec((tc, SB_WIDTH), lambda b, c: (b * nc + c, ucol)),
                  pl.BlockSpec((POOL_HALO, SB_WIDTH), lambda b, c: (jnp.maximum((b * nc + c) * hb - 1, 0), ucol)),
                  pl.BlockSpec((4, POOL_GROUP, POOL_GROUP), lambda b, c: (0, 0, 0)),
                  pl.BlockSpec((1, SB_WIDTH), lambda b, c: (0, 0))],
        out_specs=pl.BlockSpec((tc, SB_WIDTH), lambda b, c: (b * nc + c, 0)),
        out_shape=jax.ShapeDtypeStruct((bsz * seq, SB_WIDTH), BF16),
        compiler_params=_params(("parallel", "parallel")),
    )(proj, proj, pool_w, pool_scale)


def pool_bwd(proj, pool_w, pool_scale, dcat, bsz, seq):
    tc = _tile(seq, 512)
    nc = seq // tc
    hb = tc // POOL_HALO
    nblk = bsz * seq // POOL_HALO
    ucol = 3

    def body(u_ref, prev_ref, dy_ref, nxt_ref, w_ref, s_ref, du_ref, dw_ref, ds_ref):
        b, c = pl.program_id(0), pl.program_id(1)

        @pl.when((b == 0) & (c == 0))
        def _():
            dw_ref[...] = jnp.zeros_like(dw_ref)
            ds_ref[...] = jnp.zeros_like(ds_ref)

        prev = jnp.where(c > 0, prev_ref[...], 0.0)
        x = jnp.concatenate([prev, u_ref[...]], axis=0)
        sums = _window_sums(x, forward=False)
        nxt = jnp.where(c < nc - 1, nxt_ref[...].astype(F32), 0.0)
        dy = jnp.concatenate([dy_ref[...].astype(F32), nxt], axis=0)
        tq = lax.broadcasted_iota(jnp.int32, (tc + POOL_HALO, 1), 0) + c * tc
        for g, win in enumerate(POOL_WINDOWS):
            ls = slice(g * POOL_GROUP, (g + 1) * POOL_GROUP)
            pooled = sums[g][POOL_HALO:, ls] / _pool_counts(tc, c, win) - x[POOL_HALO:, ls]
            y = _dot(pooled, w_ref[g], _NN)
            ds_ref[:, ls] += jnp.sum(dy[:tc, ls] * y, axis=0, keepdims=True)
            dz = dy[:, ls] * s_ref[:, ls]
            dw_ref[g] += _dot(pooled, dz[:tc], _TN)
            dpool = _dot(dz, w_ref[g], _NT)
            dmean = dpool / jnp.minimum(tq + 1, win).astype(F32)
            fsum = _window_sums(dmean, forward=True)[g]
            du_ref[:, ls] = fsum[:tc] - dpool[:tc]

    return pl.pallas_call(
        body, name="pool_bwd", grid=(bsz, nc),
        in_specs=[pl.BlockSpec((tc, SB_WIDTH), lambda b, c: (b * nc + c, ucol)),
                  pl.BlockSpec((POOL_HALO, SB_WIDTH), lambda b, c: (jnp.maximum((b * nc + c) * hb - 1, 0), ucol)),
                  pl.BlockSpec((tc, SB_WIDTH), lambda b, c: (b * nc + c, 1)),
                  pl.BlockSpec((POOL_HALO, SB_WIDTH), lambda b, c: (jnp.minimum((b * nc + c + 1) * hb, nblk - 1), 1)),
                  pl.BlockSpec((4, POOL_GROUP, POOL_GROUP), lambda b, c: (0, 0, 0)),
                  pl.BlockSpec((1, SB_WIDTH), lambda b, c: (0, 0))],
        out_specs=[pl.BlockSpec((tc, SB_WIDTH), lambda b, c: (b * nc + c, 0)),
                   pl.BlockSpec((4, POOL_GROUP, POOL_GROUP), lambda b, c: (0, 0, 0)),
                   pl.BlockSpec((1, SB_WIDTH), lambda b, c: (0, 0))],
        out_shape=[jax.ShapeDtypeStruct((bsz * seq, SB_WIDTH), F32),
                   jax.ShapeDtypeStruct((4, POOL_GROUP, POOL_GROUP), F32),
                   jax.ShapeDtypeStruct((1, SB_WIDTH), F32)],
        compiler_params=_params(("arbitrary", "arbitrary")),
    )(proj, proj, dcat, dcat, pool_w, pool_scale)


def _lbar(lam_re, lam_im, log_dt):
    dt = jnp.exp(log_dt)
    mag = jnp.exp(lam_re * dt)
    ang = lam_im * dt
    return mag * jnp.cos(ang), mag * jnp.sin(ang)


def _bbar(lam_re, lam_im, log_dt, b_re, b_im):
    lb_re, lb_im = _lbar(lam_re, lam_im, log_dt)
    n_re = lb_re - 1.0
    den = lam_re * lam_re + lam_im * lam_im
    coef_re = (n_re * lam_re + lb_im * lam_im) / den
    coef_im = (lb_im * lam_re - n_re * lam_im) / den
    return coef_re * b_re - coef_im * b_im, coef_re * b_im + coef_im * b_re


def _expand01():
    p = lax.broadcasted_iota(jnp.int32, (64, 1024), 0)
    q = lax.broadcasted_iota(jnp.int32, (64, 1024), 1)
    return (lax.shift_right_logical(q, 4) == p).astype(BF16)


def ssm_prep(lam_re, lam_im, log_dt, b_re2, b_im2):
    def body(lr_ref, li_ref, dt_ref, br_ref, bi_ref, ar_ref, ai_ref, bbr_ref, bbi_ref):
        e = _expand01()
        lr, li, dt = lr_ref[...], li_ref[...], dt_ref[...]
        ar_ref[...], ai_ref[...] = _lbar(lr, li, dt)
        bbr_ref[...], bbi_ref[...] = _bbar(_dot_exact01(lr, e), _dot_exact01(li, e), dt, br_ref[...], bi_ref[...])

    s64 = jax.ShapeDtypeStruct((64, 64), F32)
    s1k = jax.ShapeDtypeStruct((64, 1024), F32)
    return pl.pallas_call(body, name="ssm_prep", out_shape=[s64, s64, s1k, s1k], compiler_params=_params())(
        lam_re, lam_im, log_dt, b_re2, b_im2)


def ssm_prep_bwd(lam_re, lam_im, log_dt, b_re2, b_im2, da_re, da_im, dbb_re, dbb_im):
    def body(lr_ref, li_ref, dt_ref, br_ref, bi_ref, dar_ref, dai_ref, dbr_ref, dbi_ref,
             olr_ref, oli_ref, odt_ref, obr_ref, obi_ref):
        e = _expand01()
        lr, li, dt = lr_ref[...], li_ref[...], dt_ref[...]
        _, vjp_a = jax.vjp(_lbar, lr, li, dt)
        g_lr, g_li, g_dt = vjp_a((dar_ref[...], dai_ref[...]))
        _, vjp_b = jax.vjp(_bbar, _dot_exact01(lr, e), _dot_exact01(li, e), dt, br_ref[...], bi_ref[...])
        x_lr, x_li, x_dt, g_br, g_bi = vjp_b((dbr_ref[...], dbi_ref[...]))
        olr_ref[...] = g_lr + _dot_exact01(x_lr, e, _NT)
        oli_ref[...] = g_li + _dot_exact01(x_li, e, _NT)
        odt_ref[...] = g_dt + x_dt
        obr_ref[...] = g_br
        obi_ref[...] = g_bi

    s64 = jax.ShapeDtypeStruct((64, 64), F32)
    s1k = jax.ShapeDtypeStruct((64, 1024), F32)
    return pl.pallas_call(body, name="ssm_prep_bwd",
                          out_shape=[s64, s64, jax.ShapeDtypeStruct((64, 1), F32), s1k, s1k],
                          compiler_params=_params())(
        lam_re, lam_im, log_dt, b_re2, b_im2, da_re, da_im, dbb_re, dbb_im)


def _gelu(y):
    c = math.sqrt(2.0 / math.pi)
    return 0.5 * y * (1.0 + jnp.tanh(c * (y + 0.044715 * y * y * y)))


def _gelu_grad(y):
    c = math.sqrt(2.0 / math.pi)
    th = jnp.tanh(c * (y + 0.044715 * y * y * y))
    return 0.5 * (1.0 + th) + 0.5 * y * (1.0 - th * th) * c * (1.0 + 3.0 * 0.044715 * y * y)


def _cmul(ar, ai, br, bi):
    return ar * br - ai * bi, ar * bi + ai * br


def _scan_tables(ar, ai, reverse, tabs):
    row = lax.broadcasted_iota(jnp.int32, (8, SSM_STATES), 0)
    a1 = (ar, ai)
    a2 = _cmul(*a1, *a1)
    a4 = _cmul(*a2, *a2)
    powers = [a1, a2, _cmul(*a2, *a1), a4]
    powers += [_cmul(*a4, *p) for p in powers]
    for k, (val, sh) in enumerate(((a1, 1), (a2, 2), (a4, 4))):
        keep = (row < 8 - sh) if reverse else (row >= sh)
        tabs[2 * k][...] = jnp.where(keep, val[0], 0.0)
        tabs[2 * k + 1][...] = jnp.where(keep, val[1], 0.0)
    pr = jnp.zeros((8, SSM_STATES), F32)
    pi = jnp.zeros((8, SSM_STATES), F32)
    for r in range(8):
        val = powers[7 - r] if reverse else powers[r]
        pr = jnp.where(row == r, val[0], pr)
        pi = jnp.where(row == r, val[1], pi)
    tabs[6][...] = pr
    tabs[7][...] = pi


def _scan8(xr, xi, tabs, ls, cr, ci, reverse):
    for k, sh in enumerate((1, 2, 4)):
        amt = (8 - sh) if reverse else sh
        sr, si = pltpu.roll(xr, amt, 0), pltpu.roll(xi, amt, 0)
        lr, li = tabs[2 * k][:, ls], tabs[2 * k + 1][:, ls]
        xr, xi = xr + lr * sr - li * si, xi + lr * si + li * sr
    pr, pi = tabs[6][:, ls], tabs[7][:, ls]
    return xr + pr * cr - pi * ci, xi + pr * ci + pi * cr


def _block8(b):
    return pl.ds(pl.multiple_of(b * 8, 8), 8)


def ssm_fwd(u, wt, ct, a_re, a_im, dskip, bsz, seq):
    tc = _tile(seq, 256)
    nc = seq // tc
    ns = SSM_TILE_STATES
    nl = SSM_STATES // SSM_LANES

    def body(u_ref, wt_ref, ct_ref, ar_ref, ai_ref, d_ref, y_ref, gl_ref, hr_ref, hi_ref, sr_ref, si_ref, *tabs):
        b, c = pl.program_id(0), pl.program_id(1)

        @pl.when((b == 0) & (c == 0))
        def _():
            _scan_tables(ar_ref[...], ai_ref[...], False, tabs)

        @pl.when(c == 0)
        def _():
            sr_ref[...] = jnp.zeros_like(sr_ref)
            si_ref[...] = jnp.zeros_like(si_ref)

        uf = u_ref[...]
        for i in range(SSM_TILES):
            bu = _dot(uf[:, i * 128:(i + 1) * 128], wt_ref[i], _NN)
            hr_ref[:, i * ns:(i + 1) * ns] = bu[:, :ns]
            hi_ref[:, i * ns:(i + 1) * ns] = bu[:, ns:]

        def step(blk, carry):
            rows = _block8(blk)
            new = []
            for j in range(nl):
                ls = slice(j * SSM_LANES, (j + 1) * SSM_LANES)
                xr, xi = _scan8(hr_ref[rows, ls], hi_ref[rows, ls], tabs, ls, carry[2 * j], carry[2 * j + 1], False)
                hr_ref[rows, ls] = xr
                hi_ref[rows, ls] = xi
                new += [xr[7:8], xi[7:8]]
            return tuple(new)

        init = []
        for j in range(nl):
            ls = slice(j * SSM_LANES, (j + 1) * SSM_LANES)
            init += [sr_ref[:, ls], si_ref[:, ls]]
        last = lax.fori_loop(0, tc // 8, step, tuple(init), unroll=2)
        for j in range(nl):
            ls = slice(j * SSM_LANES, (j + 1) * SSM_LANES)
            sr_ref[:, ls] = last[2 * j]
            si_ref[:, ls] = last[2 * j + 1]
        for i in range(SSM_TILES):
            hcat = jnp.concatenate([hr_ref[:, i * ns:(i + 1) * ns], hi_ref[:, i * ns:(i + 1) * ns]], axis=1)
            ls = slice(i * 128, (i + 1) * 128)
            y = _dot(hcat, ct_ref[i], _NN) + d_ref[:, ls] * uf[:, ls]
            y_ref[:, ls] = y
            gl_ref[:, ls] = _gelu(y).astype(gl_ref.dtype)

    t = bsz * seq
    row = pl.BlockSpec((tc, D_MODEL), lambda b, c: (b * nc + c, 0))
    st = pl.BlockSpec((tc, SSM_STATES), lambda b, c: (b * nc + c, 0))
    diag = pl.BlockSpec((1, SSM_STATES), lambda b, c: (0, 0))
    return pl.pallas_call(
        body, name="ssm_fwd", grid=(bsz, nc),
        in_specs=[row, pl.BlockSpec((SSM_TILES, 128, 2 * ns), lambda b, c: (0, 0, 0)),
                  pl.BlockSpec((SSM_TILES, 2 * ns, 128), lambda b, c: (0, 0, 0)), diag, diag,
                  pl.BlockSpec((1, D_MODEL), lambda b, c: (0, 0))],
        out_specs=[row, row, st, st],
        out_shape=[jax.ShapeDtypeStruct((t, D_MODEL), F32), jax.ShapeDtypeStruct((t, D_MODEL), BF16),
                   jax.ShapeDtypeStruct((t, SSM_STATES), F32), jax.ShapeDtypeStruct((t, SSM_STATES), F32)],
        scratch_shapes=[pltpu.VMEM((1, SSM_STATES), F32)] * 2 + [pltpu.VMEM((8, SSM_STATES), F32)] * 8,
        compiler_params=_params(("arbitrary", "arbitrary")),
    )(u, wt, ct, a_re, a_im, dskip)


def ssm_bwd(dgl, y, u, h_re, h_im, wt, ct, a_re, a_im, dskip, bsz, seq):
    tc = _tile(seq, 256)
    nc = seq // tc
    nb = tc // 8
    ns = SSM_TILE_STATES
    nl = SSM_STATES // SSM_LANES

    def body(dgl_ref, y_ref, u_ref, hr_ref, hi_ref, pr_ref, pi_ref, wt_ref, ct_ref, ar_ref, ai_ref, d_ref,
             du_ref, dwt_ref, dct_ref, dd_ref, dar_ref, dai_ref, gr_ref, gi_ref, sr_ref, si_ref, ar8_ref, ai8_ref,
             *tabs):
        b, c = pl.program_id(0), pl.program_id(1)

        @pl.when((b == 0) & (c == 0))
        def _():
            for r in (dwt_ref, dct_ref, dd_ref, ar8_ref, ai8_ref):
                r[...] = jnp.zeros_like(r)
            _scan_tables(ar_ref[...], -ai_ref[...], True, tabs)

        @pl.when(c == 0)
        def _():
            sr_ref[...] = jnp.zeros_like(sr_ref)
            si_ref[...] = jnp.zeros_like(si_ref)

        uf = u_ref[...]
        dy = dgl_ref[...].astype(F32) * _gelu_grad(y_ref[...])
        dd_ref[...] += jnp.sum(dy * uf, axis=0, keepdims=True)
        for i in range(SSM_TILES):
            dyi = dy[:, i * 128:(i + 1) * 128]
            dh = _dot(dyi, ct_ref[i], _NT)
            gr_ref[:, i * ns:(i + 1) * ns] = dh[:, :ns]
            gi_ref[:, i * ns:(i + 1) * ns] = dh[:, ns:]
            hcat = jnp.concatenate([hr_ref[:, i * ns:(i + 1) * ns], hi_ref[:, i * ns:(i + 1) * ns]], axis=1)
            dct_ref[i] += _dot(hcat, dyi, _TN)
        row0 = lax.broadcasted_iota(jnp.int32, (8, SSM_LANES), 0) == 0

        def block(blk, carry, before):
            rows = _block8(blk)
            new = []
            for j in range(nl):
                ls = slice(j * SSM_LANES, (j + 1) * SSM_LANES)
                gr, gi = _scan8(gr_ref[rows, ls], gi_ref[rows, ls], tabs, ls, carry[2 * j], carry[2 * j + 1], True)
                gr_ref[rows, ls] = gr
                gi_ref[rows, ls] = gi
                bpr, bpi = before(j)
                hpr = jnp.where(row0, bpr, pltpu.roll(hr_ref[rows, ls], 1, 0))
                hpi = jnp.where(row0, bpi, pltpu.roll(hi_ref[rows, ls], 1, 0))
                ar8_ref[:, ls] += gr * hpr + gi * hpi
                ai8_ref[:, ls] += gi * hpr - gr * hpi
                new += [gr[0:1], gi[0:1]]
            return tuple(new)

        def step(jj, carry):
            blk = nb - 1 - jj
            prev_rows = _block8(blk - 1)

            def before(j):
                ls = slice(j * SSM_LANES, (j + 1) * SSM_LANES)
                return hr_ref[prev_rows, ls][7:8], hi_ref[prev_rows, ls][7:8]

            return block(blk, carry, before)

        init = []
        for j in range(nl):
            ls = slice(j * SSM_LANES, (j + 1) * SSM_LANES)
            init += [sr_ref[:, ls], si_ref[:, ls]]
        carry = lax.fori_loop(0, nb - 1, step, tuple(init))
        first = c == nc - 1

        def before_chunk(j):
            ls = slice(j * SSM_LANES, (j + 1) * SSM_LANES)
            return (jnp.where(first, 0.0, pr_ref[:, ls][7:8]), jnp.where(first, 0.0, pi_ref[:, ls][7:8]))

        last = block(0, carry, before_chunk)
        for j in range(nl):
            ls = slice(j * SSM_LANES, (j + 1) * SSM_LANES)
            sr_ref[:, ls] = last[2 * j]
            si_ref[:, ls] = last[2 * j + 1]
        for i in range(SSM_TILES):
            ls = slice(i * 128, (i + 1) * 128)
            gcat = jnp.concatenate([gr_ref[:, i * ns:(i + 1) * ns], gi_ref[:, i * ns:(i + 1) * ns]], axis=1)
            du_ref[:, ls] = (_dot(gcat, wt_ref[i], _NT) + d_ref[:, ls] * dy[:, ls]).astype(du_ref.dtype)
            dwt_ref[i] += _dot(uf[:, ls], gcat, _TN)

        @pl.when((b == bsz - 1) & (c == nc - 1))
        def _():
            dar_ref[...] = jnp.sum(ar8_ref[...], axis=0, keepdims=True)
            dai_ref[...] = jnp.sum(ai8_ref[...], axis=0, keepdims=True)

    t = bsz * seq
    rev = lambda b, c: (b * nc + (nc - 1 - c), 0)
    row = pl.BlockSpec((tc, D_MODEL), rev)
    st = pl.BlockSpec((tc, SSM_STATES), rev)
    prev = pl.BlockSpec((8, SSM_STATES), lambda b, c: (jnp.maximum((b * nc + (nc - 1 - c)) * nb - 1, 0), 0))
    diag = pl.BlockSpec((1, SSM_STATES), lambda b, c: (0, 0))
    wts = pl.BlockSpec((SSM_TILES, 128, 2 * ns), lambda b, c: (0, 0, 0))
    cts = pl.BlockSpec((SSM_TILES, 2 * ns, 128), lambda b, c: (0, 0, 0))
    vec = pl.BlockSpec((1, D_MODEL), lambda b, c: (0, 0))
    return pl.pallas_call(
        body, name="ssm_bwd", grid=(bsz, nc),
        in_specs=[row, row, row, st, st, prev, prev, wts, cts, diag, diag, vec],
        out_specs=[row, wts, cts, vec, diag, diag],
        out_shape=[jax.ShapeDtypeStruct((t, D_MODEL), BF16),
                   jax.ShapeDtypeStruct((SSM_TILES, 128, 2 * ns), F32),
                   jax.ShapeDtypeStruct((SSM_TILES, 2 * ns, 128), F32),
                   jax.ShapeDtypeStruct((1, D_MODEL), F32),
                   jax.ShapeDtypeStruct((1, SSM_STATES), F32), jax.ShapeDtypeStruct((1, SSM_STATES), F32)],
        scratch_shapes=[pltpu.VMEM((tc, SSM_STATES), F32)] * 2 + [pltpu.VMEM((1, SSM_STATES), F32)] * 2
                       + [pltpu.VMEM((8, SSM_STATES), F32)] * 10,
        compiler_params=_params(("arbitrary", "arbitrary")),
    )(dgl, y, u, h_re, h_im, h_re, h_im, wt, ct, a_re, a_im, dskip)


def _ssm_in_weights(bb_re2, bb_im2):
    eye = jnp.eye(8, dtype=F32)[None, :, None, :, None]

    def one(bb):
        t = bb.reshape(8, 8, 64, 16).transpose(0, 1, 3, 2)
        return (t[:, :, :, None, :] * eye).reshape(8, 128, 512)

    return jnp.concatenate([one(bb_re2), one(bb_im2)], axis=-1).astype(MXU_DTYPE)


def _ssm_in_weights_bwd(dwt):
    eye = jnp.eye(8, dtype=F32)[None, :, None, :, None]

    def one(d):
        t = (d.reshape(8, 8, 16, 8, 64) * eye).sum(axis=3)
        return t.transpose(0, 1, 3, 2).reshape(64, 1024)

    return one(dwt[..., :512]), one(dwt[..., 512:])


def _ssm_out_weights(c_re, c_im):
    eye = jnp.eye(8, dtype=F32)[None, :, None, :, None]

    def one(cc):
        t = cc.reshape(8, 8, 16, 64).transpose(0, 1, 3, 2)
        return (t[:, :, :, None, :] * eye).reshape(8, 512, 128)

    return jnp.concatenate([one(c_re), -one(c_im)], axis=1).astype(MXU_DTYPE)


def _ssm_out_weights_bwd(dct):
    eye = jnp.eye(8, dtype=F32)[None, :, None, :, None]

    def one(d):
        t = (d.reshape(8, 8, 64, 8, 16) * eye).sum(axis=3)
        return t.transpose(0, 1, 3, 2).reshape(64, 16, 64)

    return one(dct[:, :512]), -one(dct[:, 512:])


def _softmax(s):
    m = jnp.max(s, axis=-1, keepdims=True)
    e = jnp.exp(s - m)
    return e / jnp.sum(e, axis=-1, keepdims=True)


def xattn_fwd(q, kv, bsz, seq):
    tq = _tile(seq, 512)
    nq = seq // tq
    scale = XA_HEAD_DIM ** -0.5

    def body(q_ref, k_ref, v_ref, o_ref):
        s = lax.dot_general(q_ref[...], k_ref[...], _NT, preferred_element_type=F32) * scale
        p = _softmax(s)
        o_ref[...] = _dot(p, v_ref[...], _NN).astype(o_ref.dtype)

    qs = pl.BlockSpec((tq, XA_HEAD_DIM), lambda b, h, i: (b * nq + i, h))
    return pl.pallas_call(
        body, name="xattn_fwd", grid=(bsz, XA_HEADS, nq),
        in_specs=[qs, pl.BlockSpec((MEM_LEN, XA_HEAD_DIM), lambda b, h, i: (b, h)),
                  pl.BlockSpec((MEM_LEN, XA_HEAD_DIM), lambda b, h, i: (b, XA_HEADS + h))],
        out_specs=qs, out_shape=jax.ShapeDtypeStruct((bsz * seq, D_MODEL), BF16),
        compiler_params=_params(("parallel", "parallel", "parallel")),
    )(q, kv, kv)


def xattn_bwd(q, kv, do, bsz, seq):
    tq = _tile(seq, 512)
    nq = seq // tq
    scale = XA_HEAD_DIM ** -0.5

    def body(q_ref, k_ref, v_ref, do_ref, dq_ref, dk_ref, dv_ref):
        @pl.when(pl.program_id(2) == 0)
        def _():
            dk_ref[...] = jnp.zeros_like(dk_ref)
            dv_ref[...] = jnp.zeros_like(dv_ref)

        qv, kk, vv, dov = q_ref[...], k_ref[...], v_ref[...], do_ref[...]
        s = lax.dot_general(qv, kk, _NT, preferred_element_type=F32) * scale
        p = _softmax(s)
        dp = lax.dot_general(dov, vv, _NT, preferred_element_type=F32)
        ds = (p * (dp - jnp.sum(dp * p, axis=-1, keepdims=True)) * scale).astype(MXU_DTYPE)
        dq_ref[...] = lax.dot_general(ds, kk, _NN, preferred_element_type=F32).astype(dq_ref.dtype)
        dk_ref[...] += lax.dot_general(ds, qv, _TN, preferred_element_type=F32)
        dv_ref[...] += lax.dot_general(p.astype(MXU_DTYPE), dov, _TN, preferred_element_type=F32)

    qs = pl.BlockSpec((tq, XA_HEAD_DIM), lambda b, h, i: (b * nq + i, h))
    ks = pl.BlockSpec((MEM_LEN, XA_HEAD_DIM), lambda b, h, i: (b, h))
    vs = pl.BlockSpec((MEM_LEN, XA_HEAD_DIM), lambda b, h, i: (b, XA_HEADS + h))
    dkv = jax.ShapeDtypeStruct((bsz * MEM_LEN, D_MODEL), F32)
    dq, dk, dv = pl.pallas_call(
        body, name="xattn_bwd", grid=(bsz, XA_HEADS, nq),
        in_specs=[qs, ks, vs, qs], out_specs=[qs, ks, ks],
        out_shape=[jax.ShapeDtypeStruct((bsz * seq, D_MODEL), BF16), dkv, dkv],
        compiler_params=_params(("parallel", "parallel", "arbitrary")),
    )(q, kv, kv, do)
    return dq, dk, dv


CONV_HALO = 16


def _shifts_down(x, prev):
    h = prev.shape[0]
    ext = jnp.concatenate([prev, x], axis=0)
    return pltpu.roll(ext, 1, 0)[h:], pltpu.roll(ext, 2, 0)[h:]


def _shifts_up(x, nxt):
    rows = x.shape[0]
    n = rows + nxt.shape[0]
    ext = jnp.concatenate([x, nxt], axis=0)
    return pltpu.roll(ext, n - 1, 0)[:rows], pltpu.roll(ext, n - 2, 0)[:rows]


def _conv_taps(u, u1, u2, w, b):
    return b + w[2:3] * u + w[1:2] * u1 + w[0:1] * u2


def conv_fwd(up, cw, cb, bsz, seq):
    tc = _tile(seq, 512)
    nc = seq // tc
    hb = tc // CONV_HALO
    half = N_DEV // 2

    def body(uv_ref, ug_ref, pv_ref, pg_ref, wv_ref, wg_ref, bv_ref, bg_ref, o_ref):
        c = pl.program_id(2)
        pv = jnp.where(c > 0, pv_ref[...].astype(F32), 0.0)
        pg = jnp.where(c > 0, pg_ref[...].astype(F32), 0.0)
        uv, ug = uv_ref[...].astype(F32), ug_ref[...].astype(F32)
        val = _conv_taps(uv, *_shifts_down(uv, pv), wv_ref[...], bv_ref[...])
        gate = _conv_taps(ug, *_shifts_down(ug, pg), wg_ref[...], bg_ref[...])
        o_ref[...] = (gate * jax.nn.sigmoid(gate) * val).astype(o_ref.dtype)

    def cur(off):
        return pl.BlockSpec((None, tc, FF_SHARD), lambda b, j, c: (j + off, b * nc + c, 0))

    def prv(off):
        return pl.BlockSpec((None, CONV_HALO, FF_SHARD), lambda b, j, c: (j + off, jnp.maximum((b * nc + c) * hb - 1, 0), 0))

    def par(rows, off):
        return pl.BlockSpec((None, rows, FF_SHARD), lambda b, j, c: (j + off, 0, 0))

    return pl.pallas_call(
        body, name="conv_fwd", grid=(bsz, half, nc),
        in_specs=[cur(0), cur(half), prv(0), prv(half), par(3, 0), par(3, half), par(1, 0), par(1, half)],
        out_specs=cur(0), out_shape=jax.ShapeDtypeStruct((half, bsz * seq, FF_SHARD), BF16),
        compiler_params=_params(("parallel", "parallel", "parallel")),
    )(up, up, up, up, cw, cw, cb, cb)


def conv_bwd_taps(up, cw, cb, dact, bsz, seq):
    tc = _tile(seq, 512)
    nc = seq // tc
    hb = tc // CONV_HALO
    half = N_DEV // 2

    def body(uv_ref, ug_ref, pv_ref, pg_ref, wv_ref, wg_ref, bv_ref, bg_ref, da_ref,
             dc_ref, dwv_ref, dwg_ref, dbv_ref, dbg_ref):
        b, c = pl.program_id(1), pl.program_id(2)

        @pl.when((b == 0) & (c == 0))
        def _():
            for r in (dwv_ref, dwg_ref, dbv_ref, dbg_ref):
                r[...] = jnp.zeros_like(r)

        pv = jnp.where(c > 0, pv_ref[...].astype(F32), 0.0)
        pg = jnp.where(c > 0, pg_ref[...].astype(F32), 0.0)
        uv, ug = uv_ref[...].astype(F32), ug_ref[...].astype(F32)
        uv1, uv2 = _shifts_down(uv, pv)
        ug1, ug2 = _shifts_down(ug, pg)
        val = _conv_taps(uv, uv1, uv2, wv_ref[...], bv_ref[...])
        gate = _conv_taps(ug, ug1, ug2, wg_ref[...], bg_ref[...])
        sg = jax.nn.sigmoid(gate)
        da = da_ref[...].astype(F32)
        dsilu = da * sg
        dval = dsilu * gate
        dgate = dsilu * val * (1.0 + gate * (1.0 - sg))
        dc_ref[0] = dval.astype(dc_ref.dtype)
        dc_ref[1] = dgate.astype(dc_ref.dtype)
        for dcv, taps, dw_ref, db_ref in ((dval, (uv2, uv1, uv), dwv_ref, dbv_ref),
                                          (dgate, (ug2, ug1, ug), dwg_ref, dbg_ref)):
            db_ref[...] += jnp.sum(dcv, axis=0, keepdims=True)
            for k, u_k in enumerate(taps):
                dw_ref[k:k + 1, :] += jnp.sum(dcv * u_k, axis=0, keepdims=True)

    def cur(off):
        return pl.BlockSpec((None, tc, FF_SHARD), lambda j, b, c: (j + off, b * nc + c, 0))

    def prv(off):
        return pl.BlockSpec((None, CONV_HALO, FF_SHARD), lambda j, b, c: (j + off, jnp.maximum((b * nc + c) * hb - 1, 0), 0))

    def par(rows, off):
        return pl.BlockSpec((None, rows, FF_SHARD), lambda j, b, c: (j + off, 0, 0))

    t = bsz * seq
    hs = jax.ShapeDtypeStruct((2, half, t, FF_SHARD), BF16)
    ws = jax.ShapeDtypeStruct((half, 3, FF_SHARD), F32)
    bs = jax.ShapeDtypeStruct((half, 1, FF_SHARD), F32)
    dc, dwv, dwg, dbv, dbg = pl.pallas_call(
        body, name="conv_bwd_taps", grid=(half, bsz, nc),
        in_specs=[cur(0), cur(half), prv(0), prv(half), par(3, 0), par(3, half), par(1, 0), par(1, half), cur(0)],
        out_specs=[pl.BlockSpec((2, None, tc, FF_SHARD), lambda j, b, c: (0, j, b * nc + c, 0)),
                   par(3, 0), par(3, 0), par(1, 0), par(1, 0)],
        out_shape=[hs, ws, ws, bs, bs],
        compiler_params=_params(("parallel", "arbitrary", "arbitrary")),
    )(up, up, up, up, cw, cw, cb, cb, dact)
    return (dc.reshape(N_DEV, t, FF_SHARD), jnp.concatenate([dwv, dwg], axis=0),
            jnp.concatenate([dbv, dbg], axis=0))


def conv_bwd_input(dconv, cw, bsz, seq):
    tc = _tile(seq, 1024)
    nc = seq // tc
    hb = tc // CONV_HALO
    nblk = bsz * seq // CONV_HALO

    def body(d_ref, n_ref, w_ref, o_ref):
        c = pl.program_id(2)
        nxt = jnp.where(c < nc - 1, n_ref[...].astype(F32), 0.0)
        d = d_ref[...].astype(F32)
        d1, d2 = _shifts_up(d, nxt)
        w = w_ref[...]
        o_ref[...] = (w[2:3] * d + w[1:2] * d1 + w[0:1] * d2).astype(o_ref.dtype)

    cur = pl.BlockSpec((None, tc, FF_SHARD), lambda j, b, c: (j, b * nc + c, 0))
    return pl.pallas_call(
        body, name="conv_bwd_input", grid=(N_DEV, bsz, nc),
        in_specs=[cur, pl.BlockSpec((None, CONV_HALO, FF_SHARD),
                                    lambda j, b, c: (j, jnp.minimum((b * nc + c + 1) * hb, nblk - 1), 0)),
                  pl.BlockSpec((None, 3, FF_SHARD), lambda j, b, c: (j, 0, 0))],
        out_specs=cur, out_shape=jax.ShapeDtypeStruct(dconv.shape, BF16),
        compiler_params=_params(("parallel", "parallel", "parallel")),
    )(dconv, dconv, cw)


def _my_index():
    return 4 * lax.axis_index("x") + 2 * lax.axis_index("y") + lax.axis_index("c")


def _peer(k):
    return (lax.axis_index("x") ^ ((k >> 2) & 1), lax.axis_index("y") ^ ((k >> 1) & 1),
            lax.axis_index("c") ^ (k & 1))


_HBM = pl.BlockSpec(memory_space=pltpu.HBM)
_SEM = pl.BlockSpec(memory_space=pltpu.SEMAPHORE)
_DATAFLOW = pltpu.SideEffectType.DATAFLOW_SIDE_EFFECTING


def _split_copies(gather, src_ref, land_ref, send_sems, recv_sems, local_sem):
    me = _my_index()

    def part(j):
        return src_ref if gather else src_ref.at[j]

    local = pltpu.make_async_copy(part(me), land_ref.at[me], local_sem)
    sends = [pltpu.make_async_remote_copy(
        src_ref=part(me ^ k), dst_ref=land_ref.at[me], send_sem=send_sems.at[k - 1], recv_sem=recv_sems.at[k - 1],
        device_id=_peer(k), device_id_type=pl.DeviceIdType.MESH) for k in range(1, N_DEV)]
    recvs = [pltpu.make_async_remote_copy(
        src_ref=part(me ^ k), dst_ref=land_ref.at[me ^ k], send_sem=send_sems.at[k - 1], recv_sem=recv_sems.at[k - 1],
        device_id=_peer(k), device_id_type=pl.DeviceIdType.MESH) for k in range(1, N_DEV)]
    return local, sends, recvs


def split_start(name, srcs, gather):
    n = len(srcs)
    lands = [((N_DEV,) + s.shape) if gather else s.shape for s in srcs]

    def body(*refs):
        ins, outs = refs[:2 * n], refs[2 * n:]
        for i in range(n):
            local, sends, _ = _split_copies(gather, ins[i], ins[n + i], *outs[3 * i:3 * i + 3])
            local.start()
            for cp in sends:
                cp.start()
        outs[-1][...] = jnp.zeros_like(outs[-1])

    dma7 = pltpu.SemaphoreType.DMA((N_DEV - 1,))
    out = pl.pallas_call(
        body, name=name,
        out_shape=(dma7, dma7, pltpu.SemaphoreType.DMA(())) * n
                  + tuple(pltpu.HBM(s.shape, s.dtype) for s in srcs)
                  + tuple(pltpu.HBM(shape, s.dtype) for shape, s in zip(lands, srcs))
                  + (jax.ShapeDtypeStruct((8, 128), F32),),
        in_specs=(_HBM,) * (2 * n),
        out_specs=(_SEM,) * (3 * n) + (_HBM,) * (2 * n) + (pl.BlockSpec(memory_space=pltpu.VMEM),),
        input_output_aliases={i: 3 * n + i for i in range(2 * n)},
        compiler_params=pltpu.CompilerParams(has_side_effects=_DATAFLOW),
    )(*[pltpu.with_memory_space_constraint(s, pltpu.HBM) for s in srcs],
      *[pltpu.with_memory_space_constraint(lax.empty(shape, s.dtype), pltpu.HBM) for shape, s in zip(lands, srcs)])
    handles = [tuple(out[3 * i:3 * i + 3]) + (out[3 * n + i], out[4 * n + i]) for i in range(n)]
    return handles, out[-1][0, 0]


def split_wait(name, handles, after, gather):
    send_sems, recv_sems, local_sem, src_thru, land_thru = handles

    def body(src_ref, land_ref, send_sems, recv_sems, local_sem, after_ref, src_dead, got_ref, token):
        local, sends, recvs = _split_copies(gather, src_ref, land_ref, send_sems, recv_sems, local_sem)
        local.wait()
        for cp in recvs:
            cp.wait_send()
            cp.wait_recv()
        token[...] = jnp.zeros_like(token)

    out = pl.pallas_call(
        body, name=name,
        out_shape=(pltpu.HBM(src_thru.shape, src_thru.dtype), pltpu.HBM(land_thru.shape, land_thru.dtype),
                   jax.ShapeDtypeStruct((8, 128), F32)),
        in_specs=(_HBM, _HBM, _SEM, _SEM, _SEM, pl.BlockSpec(memory_space=pl.ANY)),
        out_specs=(_HBM, _HBM, pl.BlockSpec(memory_space=pltpu.VMEM)),
        input_output_aliases={0: 0, 1: 1},
        compiler_params=pltpu.CompilerParams(has_side_effects=_DATAFLOW),
    )(src_thru, land_thru, send_sems, recv_sems, local_sem, after)
    return out[1], out[2][0, 0]


def sum_parts(name, r):
    _, rows, cols = r.shape

    def body(r_ref, o_ref):
        acc = r_ref[0].astype(F32)
        for s in range(1, N_DEV):
            acc = acc + r_ref[s].astype(F32)
        o_ref[...] = acc

    return pl.pallas_call(body, name=name, out_shape=jax.ShapeDtypeStruct((rows, cols), F32),
                          compiler_params=_params())(r)


def adamw(name, w, m, v, parts=None, g=None, layer=0, into=None, order=None):
    _, rows, cols = w.shape
    br = _tile(rows, 256, 16)
    c1 = 1.0 / (1.0 - ADAM_B1 ** ADAM_STEP)
    c2 = 1.0 / (1.0 - ADAM_B2 ** ADAM_STEP)

    def body(g_ref, w_ref, m_ref, v_ref, *rest):
        og_ref, od_ref, om_ref, ov_ref = rest[-4:]
        if parts is None:
            gs = g_ref[...]
        else:
            gs = g_ref[0].astype(F32)
            for s in range(1, N_DEV):
                gs = gs + g_ref[s].astype(F32)
        mn = ADAM_B1 * m_ref[...] + (1.0 - ADAM_B1) * gs
        vn = ADAM_B2 * v_ref[...] + (1.0 - ADAM_B2) * (gs * gs)
        og_ref[...] = gs
        om_ref[...] = mn
        ov_ref[...] = vn
        od_ref[...] = -ADAM_LR * ((mn * c1) / (jnp.sqrt(vn * c2) + ADAM_EPS) + ADAM_WD * w_ref[...])

    blk = pl.BlockSpec((None, br, cols), lambda i: (layer, i, 0))
    if parts is None:
        gspec = pl.BlockSpec((br, cols), lambda i: (i, 0))
    else:
        gspec = pl.BlockSpec((N_DEV, br, cols), lambda i: (0, i, 0))
    earlier = [] if into is None else list(into)
    behind = [] if order is None else [order]
    return pl.pallas_call(
        body, name=name, grid=(rows // br,),
        in_specs=[gspec, blk, blk, blk] + [pl.BlockSpec(memory_space=pl.ANY)] * len(earlier)
                 + [pl.BlockSpec((1, 128), lambda i: (0, 0))] * len(behind),
        out_specs=[blk] * 4, out_shape=[jax.ShapeDtypeStruct(w.shape, F32)] * 4,
        input_output_aliases={4 + k: k for k in range(len(earlier))},
        compiler_params=_params(("parallel",)),
    )(g if parts is None else parts, w, m, v, *earlier, *behind)


SMALL = ("norm_mix", "norm_xattn", "norm_ffn", "norm_mem", "norm_final", "pool_w", "pool_scale",
         "ssm_lam_re", "ssm_lam_im", "ssm_log_dt", "ssm_b_re", "ssm_b_im", "ssm_c_re", "ssm_c_im",
         "ffn_conv_b", "ssm_d", "ffn_conv_w")
SMALL_SHARDED = {"ssm_d": 1, "ffn_conv_w": 2}
BIG = ("ab_w_in", "ab_w_out", "ssm_w_in", "ssm_w_glu", "xa_w_q", "xa_w_kv", "xa_w_o", "ffn_w_up", "ffn_w_down")
WEIGHTS = ("norm_mix", "norm_xattn", "norm_ffn", "norm_mem", "norm_final", "ab_w_in", "pool_w", "pool_scale",
           "ab_w_out", "ssm_w_in", "ssm_lam_re", "ssm_lam_im", "ssm_log_dt", "ssm_b_re", "ssm_b_im", "ssm_c_re",
           "ssm_c_im", "ssm_d", "ssm_w_glu", "xa_w_q", "xa_w_kv", "xa_w_o", "ffn_w_up", "ffn_conv_w", "ffn_conv_b",
           "ffn_w_down")


def _rows8(g):
    return g.reshape(N_DEV, g.size // (N_DEV * D_MODEL), D_MODEL)


def _square(a):
    return a.reshape(D_MODEL, D_MODEL)


_LAYOUT = {"ab_w_out": _square, "ssm_w_in": _square, "xa_w_q": _square, "xa_w_o": _square,
           "ffn_w_down": lambda a: a.reshape(N_DEV // 2, FF_SHARD, D_MODEL)}
GATHER_ORDER = (("ab_w_in", 0), ("ffn_conv_w", None), ("ssm_d", None), ("ab_w_out", 0), ("xa_w_q", 0),
                ("xa_w_kv", 0), ("xa_w_o", 0), ("ffn_w_up", 0), ("ffn_w_down", 0), ("ffn_w_up", 1),
                ("ffn_w_down", 1), ("ssm_w_in", 0), ("ssm_w_glu", 0), ("xa_w_q", 1), ("xa_w_kv", 1), ("xa_w_o", 1))
GATHER_AHEAD = 7
GATHER_BATCHES = (3, 8, 16)


class _Step:
    def __init__(self, master, small):
        self.master, self.small = master, small
        self.pending, self.gathers, self.weights, self.sent, self.queued = [], {}, {}, [], []

    def follow(self, v):
        for z in self.pending:
            v = v + z
        self.pending = []
        return v

    def start_gathers(self, upto, zero):
        upto = min(end for end in GATHER_BATCHES if end >= min(upto, len(GATHER_ORDER)))
        todo = GATHER_ORDER[len(self.gathers):upto]
        if not todo:
            return
        shards = []
        for n, l in todo:
            if l is None:
                shards.append(self.master[n] + zero)
            else:
                shards.append((self.master[n][l] + zero).astype(MXU_DTYPE))
        handles, z = split_start(f"ags_{len(self.gathers)}", shards, gather=True)
        self.gathers.update(zip(todo, handles))
        self.pending.append(z)

    def weight(self, n, l, after):
        if (n, l) not in self.weights:
            full, z = split_wait(f"agw_{n}{'' if l is None else l}", self.gathers[(n, l)], after, gather=True)
            self.weights[(n, l)] = _LAYOUT.get(n, lambda a: a)(full)
            self.start_gathers(GATHER_ORDER.index((n, l)) + 1 + GATHER_AHEAD, z)
        return self.weights[(n, l)]

    def send_grad(self, n, l, part, flush=True):
        self.queued.append((n, l, part))
        if flush:
            handles, z = split_start(f"xs_{n}{l}", [p for _, _, p in self.queued], gather=False)
            self.sent += [(qn, ql, h) for (qn, ql, _), h in zip(self.queued, handles)]
            self.queued = []
            self.pending.append(z)


def _layer_tail(st, l, x_in, hq, mem_n, acts, next_gain=None):
    bsz, seq = acts["bsz"], acts["seq"]
    p = st.small
    q = mm_nn(f"xa_q{l}", hq, st.weight("xa_w_q", l, x_in))
    kv = mm_nn_bs(f"xa_kv{l}", mem_n, st.weight("xa_w_kv", l, x_in))
    o = xattn_fwd(q, kv, bsz, seq)
    x_mid, hf = mm_nn(f"xa_o{l}", o, st.weight("xa_w_o", l, o), res=x_in, out_dtype=F32,
                      norm_gain=st.follow(p["norm_ffn"][l]))
    up = mm_nn_bs(f"ffn_up{l}", hf, st.weight("ffn_w_up", l, x_mid), stacked_out=True)
    conv_w = st.weight("ffn_conv_w", None, x_mid)[:, l]
    act = conv_fwd(up, conv_w, p["ffn_conv_b"][l], bsz, seq)
    w_down = st.weight("ffn_w_down", l, act)
    if next_gain is None:
        x_out, h_next = mm_as_nn(f"ffn_down{l}", act, w_down, res=x_mid), None
    else:
        x_out, h_next = mm_as_nn(f"ffn_down{l}", act, w_down, res=x_mid, norm_gain=st.follow(next_gain))
    acts[l].update(x_in=x_in, hq=hq, q=q, kv=kv, o=o, x_mid=x_mid, hf=hf, up=up, act=act)
    return x_out, h_next


def _layer_tail_bwd(st, l, dx, mem_n, acts, grads):
    a = acts[l]
    bsz, seq = acts["bsz"], acts["seq"]
    p = st.small
    dact = mm_nt_os(f"d_act{l}", dx, st.weight("ffn_w_down", l, dx))
    st.send_grad("ffn_w_down", l, _rows8(mm_tn(f"g_ffn_down{l}", a["act"], dx, a_stacked=True)), flush=False)
    conv_w = st.weight("ffn_conv_w", None, dx)[:, l]
    dconv, dcw, dcb = conv_bwd_taps(a["up"], conv_w, p["ffn_conv_b"][l], dact, bsz, seq)
    grads["ffn_conv_w"][l] = dcw
    grads["ffn_conv_b"][l] = dcb
    dup = conv_bwd_input(dconv, conv_w, bsz, seq)
    dx_mid, grads["norm_ffn"][l] = mm_nt_bs(f"d_hf{l}", dup, st.weight("ffn_w_up", l, dx), dc_stacked=True,
                                            rms=(a["x_mid"], st.follow(p["norm_ffn"][l]), dx))
    st.send_grad("ffn_w_up", l, mm_tn(f"g_ffn_up{l}", a["hf"], dup, dc_stacked=True))
    do = mm_nt(f"d_o{l}", dx_mid, st.weight("xa_w_o", l, dx))
    st.send_grad("xa_w_o", l, _rows8(mm_tn(f"g_xa_o{l}", a["o"], dx_mid)), flush=False)
    dq, dk, dv = xattn_bwd(a["q"], a["kv"], do, bsz, seq)
    dkv = jnp.concatenate([dk, dv], axis=1).astype(BF16)
    dx_in, grads["norm_xattn"][l] = mm_nt(f"d_hq{l}", dq, st.weight("xa_w_q", l, dx),
                                          rms=(a["x_in"], st.follow(p["norm_xattn"][l]), dx_mid))
    st.send_grad("xa_w_q", l, _rows8(mm_tn(f"g_xa_q{l}", a["hq"], dq)), flush=False)
    dmem_n = mm_nt_bs(f"d_memn{l}", dkv, st.weight("xa_w_kv", l, dx), out_dtype=F32)
    st.send_grad("xa_w_kv", l, mm_tn(f"g_xa_kv{l}", mem_n, dkv, dc_cols=2 * D_MODEL // N_DEV))
    return dx_in, dmem_n


def kernel(x, mem, norm_mix, norm_xattn, norm_ffn, norm_mem, norm_final, ab_w_in, pool_w, pool_scale, ab_w_out, ssm_w_in, ssm_lam_re, ssm_lam_im, ssm_log_dt, ssm_b_re, ssm_b_im, ssm_c_re, ssm_c_im, ssm_d, ssm_w_glu, xa_w_q, xa_w_kv, xa_w_o, ffn_w_up, ffn_conv_w, ffn_conv_b, ffn_w_down, loss_target, m_norm_mix, m_norm_xattn, m_norm_ffn, m_norm_mem, m_norm_final, m_ab_w_in, m_pool_w, m_pool_scale, m_ab_w_out, m_ssm_w_in, m_ssm_lam_re, m_ssm_lam_im, m_ssm_log_dt, m_ssm_b_re, m_ssm_b_im, m_ssm_c_re, m_ssm_c_im, m_ssm_d, m_ssm_w_glu, m_xa_w_q, m_xa_w_kv, m_xa_w_o, m_ffn_w_up, m_ffn_conv_w, m_ffn_conv_b, m_ffn_w_down, v_norm_mix, v_norm_xattn, v_norm_ffn, v_norm_mem, v_norm_final, v_ab_w_in, v_pool_w, v_pool_scale, v_ab_w_out, v_ssm_w_in, v_ssm_lam_re, v_ssm_lam_im, v_ssm_log_dt, v_ssm_b_re, v_ssm_b_im, v_ssm_c_re, v_ssm_c_im, v_ssm_d, v_ssm_w_glu, v_xa_w_q, v_xa_w_kv, v_xa_w_o, v_ffn_w_up, v_ffn_conv_w, v_ffn_conv_b, v_ffn_w_down):
    given = dict(locals())
    master = {n: given[n] for n in WEIGHTS}
    mom1 = {n: given["m_" + n] for n in WEIGHTS}
    mom2 = {n: given["v_" + n] for n in WEIGHTS}
    bsz, seq, d = x.shape
    t = bsz * seq
    me = _my_index()

    st = _Step(master, {"norm_xattn": norm_xattn, "norm_ffn": norm_ffn,
                        "ffn_conv_b": [ffn_conv_b[l].reshape(N_DEV, 1, FF_SHARD) for l in range(2)]})
    st.start_gathers(1, 0.0)
    zero = st.follow(jnp.zeros((), F32))

    acts = {"bsz": bsz, "seq": seq, 0: {}, 1: {}}
    x0 = x.reshape(t, d)
    mem2 = mem.reshape(bsz * MEM_LEN, d)
    mem_n = rms_fwd("rms_mem", mem2, norm_mem + zero)
    pscale = pool_scale.reshape(1, SB_WIDTH)

    h0 = rms_fwd("rms_mix0", x0, norm_mix[0] + zero)
    w_in = st.weight("ab_w_in", 0, h0)
    proj = mm_nn_bs("ab_in", h0, w_in, out_dtype=F32)
    a_out, rsum = sb_attn_fwd(proj, st.follow(jnp.zeros((1, 128), F32)), bsz, seq)
    p_out = pool_fwd(proj, pool_w[0], pscale, bsz, seq)
    w_out = st.weight("ab_w_out", 0, a_out)
    x1 = mm_nn("ab_out_a", a_out, w_out, res=x0, out_dtype=F32)
    x1, hq0 = mm_nn("ab_out_p", p_out, w_out, res=x1, koff=SB_WIDTH, out_dtype=F32,
                    norm_gain=st.follow(norm_xattn[0]))
    x3, h1 = _layer_tail(st, 0, x1, hq0, mem_n, acts, next_gain=norm_mix[1])

    b_re2 = ssm_b_re.reshape(64, 1024)
    b_im2 = ssm_b_im.reshape(64, 1024)
    log_dt = ssm_log_dt.reshape(64, 1)
    lb_re, lb_im, bb_re2, bb_im2 = ssm_prep(ssm_lam_re[0], ssm_lam_im[0], log_dt, b_re2, b_im2)
    wt = _ssm_in_weights(bb_re2, bb_im2)
    ct = _ssm_out_weights(ssm_c_re[0], ssm_c_im[0])
    a_re = lb_re.reshape(1, SSM_STATES)
    a_im = lb_im.reshape(1, SSM_STATES)
    u = mm_nn("ssm_in", h1, st.weight("ssm_w_in", 0, x3), out_dtype=F32)
    dskip = st.weight("ssm_d", None, x3).reshape(1, D_MODEL)
    y, gl, h_re, h_im = ssm_fwd(u, wt, ct, a_re, a_im, dskip, bsz, seq)
    glu = mm_nn_bs("ssm_glu", gl, st.weight("ssm_w_glu", 0, gl), out_dtype=F32)
    x4, hq1 = glu_fwd(glu, x3, st.follow(norm_xattn[1]))
    x6, _ = _layer_tail(st, 1, x4, hq1, mem_n, acts)

    loss_row, dx, g_norm_final = loss_head(x6, norm_final, loss_target.reshape(t, d))
    loss = lax.psum(loss_row[0, 0], MESH_AXES)

    grads = {n: [None, None] for n in ("ffn_conv_w", "ffn_conv_b", "norm_ffn", "norm_xattn", "norm_mix")}
    dx4, dmem_1 = _layer_tail_bwd(st, 1, dx, mem_n, acts, grads)
    dglu = glu_bwd(glu, dx4)
    dgl = mm_nt_bs("d_gl", dglu, st.weight("ssm_w_glu", 0, dx))
    st.send_grad("ssm_w_glu", 0, mm_tn("g_ssm_glu", gl, dglu, dc_cols=2 * D_MODEL // N_DEV), flush=False)
    du, dwt, dct, g_dskip, da_re, da_im = ssm_bwd(dgl, y, u, h_re, h_im, wt, ct, a_re, a_im, dskip, bsz, seq)
    dbb_re, dbb_im = _ssm_in_weights_bwd(dwt)
    g_c_re, g_c_im = _ssm_out_weights_bwd(dct)
    g_lam_re, g_lam_im, g_log_dt, g_b_re, g_b_im = ssm_prep_bwd(
        ssm_lam_re[0], ssm_lam_im[0], log_dt, b_re2, b_im2, da_re.reshape(64, 64), da_im.reshape(64, 64),
        dbb_re, dbb_im)
    dx3, grads["norm_mix"][1] = mm_nt("d_h1", du, st.weight("ssm_w_in", 0, dx),
                                      rms=(x3, st.follow(norm_mix[1]), dx4))
    st.send_grad("ssm_w_in", 0, _rows8(mm_tn("g_ssm_in", h1, du)))

    dx1, dmem_0 = _layer_tail_bwd(st, 0, dx3, mem_n, acts, grads)
    dcat = mm_nt("d_cat", dx1, st.weight("ab_w_out", 0, dx))
    st.send_grad("ab_w_out", 0, _rows8(jnp.concatenate(
        [mm_tn("g_ab_out_a", a_out, dx1), mm_tn("g_ab_out_p", p_out, dx1)], axis=0)), flush=False)
    dq, dk, dv = sb_attn_bwd(proj, rsum, dcat, bsz, seq)
    dpu, g_pool_w, g_pool_scale = pool_bwd(proj, pool_w[0], st.follow(pscale), dcat, bsz, seq)
    dproj = jnp.concatenate([dq, dk, dv, dpu], axis=1).astype(BF16)
    st.send_grad("ab_w_in", 0, mm_tn("g_ab_in", h0, dproj, dc_cols=2 * D_MODEL // N_DEV))
    dx0, grads["norm_mix"][0] = mm_nt_bs("d_h0", dproj, st.weight("ab_w_in", 0, dx),
                                         rms=(x0, st.follow(norm_mix[0]), dx1))
    _, g_norm_mem = rms_bwd("rms_mem_bwd", mem2, norm_mem, dmem_0 + dmem_1, need_dx=False)

    small_g = {
        "norm_mix": jnp.stack([g[0] for g in grads["norm_mix"]]),
        "norm_xattn": jnp.stack([g[0] for g in grads["norm_xattn"]]),
        "norm_ffn": jnp.stack([g[0] for g in grads["norm_ffn"]]),
        "norm_mem": g_norm_mem[0], "norm_final": g_norm_final[0],
        "pool_w": g_pool_w[None], "pool_scale": g_pool_scale,
        "ssm_lam_re": g_lam_re[None], "ssm_lam_im": g_lam_im[None], "ssm_log_dt": g_log_dt.reshape(1, 64),
        "ssm_b_re": g_b_re.reshape(1, 64, 64, 16), "ssm_b_im": g_b_im.reshape(1, 64, 64, 16),
        "ssm_c_re": g_c_re[None], "ssm_c_im": g_c_im[None],
        "ffn_conv_b": jnp.stack([g.reshape(2 * D_FF) for g in grads["ffn_conv_b"]]),
        "ssm_d": g_dskip,
        "ffn_conv_w": jnp.stack([g.transpose(1, 0, 2).reshape(3, 2 * D_FF) for g in grads["ffn_conv_w"]]),
    }
    sizes = [int(small_g[n].size) for n in SMALL]
    total = sum(sizes)
    rows8 = -(-total // (N_DEV * 128 * 8)) * 8
    flat = jnp.concatenate([small_g[n].reshape(-1).astype(F32) for n in SMALL]
                           + [jnp.zeros((N_DEV * rows8 * 128 - total,), F32)])
    (in_flight,), z = split_start("xs_small", [flat.reshape(N_DEV, rows8, 128)], gather=False)
    st.pending.append(z)
    stepped, last = {}, dx0
    for i, (n, l, handles) in enumerate(st.sent):
        if i == len(st.sent) // 2:
            recv, _ = split_wait("xw_small", in_flight, last, gather=False)
            (in_flight,), z = split_start("ags_small", [sum_parts("sum_small", recv)], gather=True)
            st.pending.append(z)
        recv, _ = split_wait(f"xw_{n}{l}", handles, dx0, gather=False)
        shape3 = (master[n].shape[0],) + recv.shape[1:]
        stepped[n] = adamw(f"adamw_{n}{l}", master[n].reshape(shape3), mom1[n].reshape(shape3),
                           mom2[n].reshape(shape3), parts=recv, layer=l, into=stepped.get(n),
                           order=st.follow(jnp.zeros((1, 128), F32)))
        last = stepped[n][0]
    out_g, out_d, out_m, out_v = ({n: stepped[n][k].reshape(master[n].shape) for n in BIG} for k in range(4))
    summed = split_wait("agw_small", in_flight, last, gather=True)[0].reshape(-1)

    def local_part(name, a):
        ax = SMALL_SHARDED.get(name)
        if ax is None:
            return a
        n_loc = a.shape[ax] // N_DEV
        return lax.dynamic_slice_in_dim(a, me * n_loc, n_loc, axis=ax)

    off = 0
    for n, sz in zip(SMALL, sizes):
        g_n = local_part(n, summed[off:off + sz].reshape(small_g[n].shape))
        off += sz
        cols = g_n.shape[-1] if g_n.shape[-1] >= 128 or g_n.ndim < 3 else g_n.shape[-1] * g_n.shape[-2]
        shape3 = (1, g_n.size // cols, cols)
        res = adamw("adamw_" + n, master[n].reshape(shape3), mom1[n].reshape(shape3), mom2[n].reshape(shape3),
                    g=g_n.reshape(shape3[1:]))
        for dst, r in zip((out_g, out_d, out_m, out_v), res):
            dst[n] = r.reshape(master[n].shape)

    return (loss, dx0.reshape(bsz, seq, d), *[out_g[n] for n in WEIGHTS], *[out_d[n] for n in WEIGHTS],
            *[out_m[n] for n in WEIGHTS], *[out_v[n] for n in WEIGHTS])
```

```python
import math

import jax
import jax.numpy as jnp
from jax import lax
from jax.experimental import pallas as pl
from jax.experimental.pallas import tpu as pltpu

F32 = jnp.float32
BF16 = jnp.bfloat16
MXU_DTYPE = jnp.bfloat16
N_DEV = 8
MESH_AXES = ("x", "y", "c")

D_MODEL = 1024
SB_HEAD_DIM = 64
SB_WIDTH = 512
SB_BLOCK = 256
POOL_WINDOWS = (2, 4, 8, 16)
POOL_GROUP = 128
POOL_HALO = 16
SSM_TILES = 8
SSM_TILE_STATES = 512
SSM_STATES = 4096
SSM_LANES = 1024
MEM_LEN = 256
XA_HEADS = 4
XA_HEAD_DIM = 256
D_FF = 2816
FF_SHARD = 704
EPS = 1e-6
ADAM_LR = 0.001
ADAM_B1 = 0.9
ADAM_B2 = 0.999
ADAM_EPS = 1e-08
ADAM_WD = 0.01
ADAM_STEP = 10
VMEM_LIMIT = 56 * 1024 * 1024

_NN = (((1,), (0,)), ((), ()))
_NT = (((1,), (1,)), ((), ()))
_TN = (((0,), (0,)), ((), ()))


def _params(sem=None):
    if sem is None:
        return pltpu.CompilerParams(vmem_limit_bytes=VMEM_LIMIT)
    return pltpu.CompilerParams(dimension_semantics=sem, vmem_limit_bytes=VMEM_LIMIT)


def _tile(n, pref, mult=8):
    if n <= pref:
        return n
    for t in range(pref, 0, -1):
        if n % t == 0 and t % mult == 0:
            return t
    return n


def _dot(a, b, dims):
    return lax.dot_general(a.astype(MXU_DTYPE), b.astype(MXU_DTYPE), dims, preferred_element_type=F32)


def _dot_exact01(x, m01, dims=_NN):
    x1 = x.astype(BF16)
    r1 = x - x1.astype(F32)
    x2 = r1.astype(BF16)
    x3 = (r1 - x2.astype(F32)).astype(BF16)
    m = m01.astype(BF16)
    out = lax.dot_general(x1, m, dims, preferred_element_type=F32)
    out = out + lax.dot_general(x2, m, dims, preferred_element_type=F32)
    return out + lax.dot_general(x3, m, dims, preferred_element_type=F32)


def _mm(name, a, b, dims, grid, a_spec, b_spec, o_spec, out_shape, out_dtype, acc_shape, res=None, r_spec=None,
        group=1, n=None, a_sel="full", b_sel="full", o_sel="full", norm_gain=None, rms=None):
    nk = grid[2]
    if out_dtype is None:
        out_dtype = BF16
    n_out = out_shape[-1]
    vec = pl.BlockSpec((1, n_out), lambda i, j, kk: (0, 0))

    def at(sel, s):
        if sel == "lead":
            return (s,)
        if sel == "lanes":
            return (slice(None), slice(s * n, (s + 1) * n))
        return (Ellipsis,)

    extra = [] if res is None else [(res, r_spec)]
    if norm_gain is not None:
        extra.append((norm_gain.reshape(1, n_out), vec))
    if rms is not None:
        extra += [(rms[0], o_spec), (rms[1].reshape(1, n_out), vec), (rms[2], o_spec)]
    n_in = 2 + len(extra)
    if rms is not None:
        out_specs = [o_spec, vec]
        out_shapes = [jax.ShapeDtypeStruct(out_shape, F32), jax.ShapeDtypeStruct((1, n_out), F32)]
    elif norm_gain is not None:
        out_specs = [o_spec, o_spec]
        out_shapes = [jax.ShapeDtypeStruct(out_shape, out_dtype), jax.ShapeDtypeStruct(out_shape, BF16)]
    else:
        out_specs, out_shapes = o_spec, jax.ShapeDtypeStruct(out_shape, out_dtype)

    def body(*refs):
        a_ref, b_ref = refs[0], refs[1]
        ins = list(refs[2:n_in])
        r_ref = ins.pop(0) if res is not None else None
        outs = refs[n_in:]
        o_ref = outs[0]
        acc = refs[-1] if nk > 1 else None
        k = pl.program_id(2)

        def finish(val):
            if r_ref is not None:
                val = val + r_ref[...].astype(F32)
            if rms is not None:
                x_ref, g_ref, d_ref = ins
                xf = x_ref[...]
                r = lax.rsqrt(jnp.mean(xf * xf, axis=-1, keepdims=True) + EPS)
                xh = xf * r
                part = jnp.sum(val * xh, axis=0, keepdims=True)
                first = pl.program_id(0) == 0

                @pl.when(first)
                def _():
                    outs[1][...] = part

                @pl.when(jnp.logical_not(first))
                def _():
                    outs[1][...] += part

                dxh = val * g_ref[...]
                o_ref[...] = d_ref[...] + r * (dxh - xh * jnp.mean(dxh * xh, axis=-1, keepdims=True))
                return
            o_ref[...] = val.astype(out_dtype)
            if norm_gain is not None:
                r = lax.rsqrt(jnp.mean(val * val, axis=-1, keepdims=True) + EPS)
                outs[1][...] = (val * r * ins[0][...]).astype(BF16)

        def emit(s, val):
            if nk == 1:
                if o_sel == "full":
                    finish(val)
                else:
                    o_ref[at(o_sel, s)] = val.astype(out_dtype)
                return

            @pl.when(k == 0)
            def _():
                acc[at(o_sel, s)] = val

            @pl.when(k > 0)
            def _():
                acc[at(o_sel, s)] += val

        total = None
        if a_sel == "full" and b_sel == "lanes":
            wide = _dot(a_ref[...], b_ref[...], dims)
            for s in range(group):
                emit(s, wide[:, s * n:(s + 1) * n])
        else:
            for s in range(group):
                val = _dot(a_ref[at(a_sel, s)], b_ref[at(b_sel, s)], dims)
                if o_sel == "full":
                    total = val if total is None else total + val
                else:
                    emit(s, val)
        if o_sel == "full":
            emit(0, total)
        if nk > 1:
            @pl.when(k == nk - 1)
            def _():
                if o_sel == "full":
                    finish(acc[...])
                else:
                    o_ref[...] = acc[...].astype(out_dtype)

    rows_sem = "arbitrary" if rms is not None else "parallel"
    return pl.pallas_call(
        body, name=name, grid=grid, in_specs=[a_spec, b_spec] + [s for _, s in extra], out_specs=out_specs,
        out_shape=out_shapes, scratch_shapes=[pltpu.VMEM(acc_shape, F32)] if nk > 1 else [],
        compiler_params=_params((rows_sem, rows_sem, "arbitrary")),
    )(a, b, *[x for x, _ in extra])


def _row_tile(m, epi):
    return _tile(m, 512 if epi.get("rms") is not None else 1024)


def mm_nn(name, a, b, res=None, koff=0, out_dtype=None, **epi):
    m, k = a.shape
    n = b.shape[1]
    tm, tn, tk = _row_tile(m, epi), _tile(n, 1024, 128), _tile(k, 1024, 128)
    kb = koff // tk
    spec = pl.BlockSpec((tm, tn), lambda i, j, kk: (i, j))
    return _mm(name, a, b, _NN, (m // tm, n // tn, k // tk),
               pl.BlockSpec((tm, tk), lambda i, j, kk: (i, kk)),
               pl.BlockSpec((tk, tn), lambda i, j, kk: (kk + kb, j)),
               spec, (m, n), out_dtype, (tm, tn), res, spec, **epi)


def mm_nn_bs(name, a, bs, stacked_out=False, out_dtype=None):
    m, k = a.shape
    s, _, n = bs.shape
    tm, tk = _tile(m, 2048 if stacked_out else 1024), _tile(k, 1024, 128)
    a_spec = pl.BlockSpec((tm, tk), lambda i, j, kk: (i, kk))
    if stacked_out:
        return _mm(name, a, bs, _NN, (m // tm, s, k // tk), a_spec,
                   pl.BlockSpec((None, tk, n), lambda i, j, kk: (j, kk, 0)),
                   pl.BlockSpec((None, tm, n), lambda i, j, kk: (j, i, 0)), (s, m, n), out_dtype, (tm, n))
    g = _tile(s, max(1, 1024 // n), 1)
    return _mm(name, a, bs, _NN, (m // tm, s // g, k // tk), a_spec,
               pl.BlockSpec((g, tk, n), lambda i, j, kk: (j, kk, 0)),
               pl.BlockSpec((tm, g * n), lambda i, j, kk: (i, j)), (m, s * n), out_dtype, (tm, g * n),
               group=g, n=n, b_sel="lead", o_sel="lanes")


def mm_as_nn(name, a_st, b3, res, out_dtype=F32, **epi):
    s, m, kp = a_st.shape
    n = b3.shape[2]
    tm, tn = _row_tile(m, epi), _tile(n, 1024, 128)
    spec = pl.BlockSpec((tm, tn), lambda i, j, kk: (i, j))
    g = _tile(s, 2, 1)
    return _mm(name, a_st, b3, _NN, (m // tm, n // tn, s // g),
               pl.BlockSpec((g, tm, kp), lambda i, j, kk: (kk, i, 0)),
               pl.BlockSpec((g, kp, tn), lambda i, j, kk: (kk, 0, j)),
               spec, (m, n), out_dtype, (tm, tn), res, spec, group=g, a_sel="lead", b_sel="lead", **epi)


def mm_nt(name, dc, b, out_dtype=None, **epi):
    m, n = dc.shape
    k = b.shape[0]
    tm, tko, tnr = _tile(m, 1024), _tile(k, 1024, 128), _tile(n, 1024, 128)
    return _mm(name, dc, b, _NT, (m // tm, k // tko, n // tnr),
               pl.BlockSpec((tm, tnr), lambda i, j, kk: (i, kk)),
               pl.BlockSpec((tko, tnr), lambda i, j, kk: (j, kk)),
               pl.BlockSpec((tm, tko), lambda i, j, kk: (i, j)), (m, k), out_dtype, (tm, tko), **epi)


def mm_nt_bs(name, dc, bs, dc_stacked=False, out_dtype=None, **epi):
    s, k, n = bs.shape
    m = dc.shape[1] if dc_stacked else dc.shape[0]
    tm, tko = (_tile(m, 1024) if dc_stacked else _row_tile(m, epi)), _tile(k, 1024, 128)
    o_spec = pl.BlockSpec((tm, tko), lambda i, j, kk: (i, j))
    if dc_stacked:
        g = _tile(s, 2, 1)
        return _mm(name, dc, bs, _NT, (m // tm, k // tko, s // g),
                   pl.BlockSpec((g, tm, n), lambda i, j, kk: (kk, i, 0)),
                   pl.BlockSpec((g, tko, n), lambda i, j, kk: (kk, j, 0)), o_spec, (m, k), out_dtype, (tm, tko),
                   group=g, a_sel="lead", b_sel="lead", **epi)
    g = _tile(s, max(1, 2048 // n), 1)
    return _mm(name, dc, bs, _NT, (m // tm, k // tko, s // g),
               pl.BlockSpec((tm, g * n), lambda i, j, kk: (i, kk)),
               pl.BlockSpec((g, tko, n), lambda i, j, kk: (kk, j, 0)), o_spec, (m, k), out_dtype, (tm, tko),
               group=g, n=n, a_sel="lanes", b_sel="lead", **epi)


def mm_nt_os(name, dc, b3, out_dtype=None):
    m, n = dc.shape
    s, kp, _ = b3.shape
    tm, tnr = _tile(m, 2048), _tile(n, 1024, 128)
    return _mm(name, dc, b3, _NT, (m // tm, s, n // tnr),
               pl.BlockSpec((tm, tnr), lambda i, j, kk: (i, kk)),
               pl.BlockSpec((None, kp, tnr), lambda i, j, kk: (j, 0, kk)),
               pl.BlockSpec((None, tm, kp), lambda i, j, kk: (j, i, 0)), (s, m, kp), out_dtype, (tm, kp))


def mm_tn(name, a, dc, a_stacked=False, dc_cols=None, dc_stacked=False, out_dtype=None):
    if a_stacked:
        s, m, kp = a.shape
        n = dc.shape[1]
        tno, tmr = _tile(n, 1024, 128), _tile(m, 2048)
        return _mm(name, a, dc, _TN, (s, n // tno, m // tmr),
                   pl.BlockSpec((None, tmr, kp), lambda i, j, kk: (i, kk, 0)),
                   pl.BlockSpec((tmr, tno), lambda i, j, kk: (kk, j)),
                   pl.BlockSpec((None, kp, tno), lambda i, j, kk: (i, 0, j)), (s, kp, n), out_dtype, (kp, tno))
    m, k = a.shape
    tko, tmr = _tile(k, 1024, 128), _tile(m, 2048)
    a_spec = pl.BlockSpec((tmr, tko), lambda i, j, kk: (kk, i))
    if dc_stacked:
        s, _, n = dc.shape
        return _mm(name, a, dc, _TN, (k // tko, s, m // tmr), a_spec,
                   pl.BlockSpec((None, tmr, n), lambda i, j, kk: (j, kk, 0)),
                   pl.BlockSpec((None, tko, n), lambda i, j, kk: (j, i, 0)), (s, k, n), out_dtype, (tko, n))
    if dc_cols is not None:
        n = dc_cols
        s = dc.shape[1] // n
        g = _tile(s, max(1, 1024 // n), 1)
        return _mm(name, a, dc, _TN, (k // tko, s // g, m // tmr), a_spec,
                   pl.BlockSpec((tmr, g * n), lambda i, j, kk: (kk, j)),
                   pl.BlockSpec((g, tko, n), lambda i, j, kk: (j, i, 0)), (s, k, n), out_dtype, (g, tko, n),
                   group=g, n=n, b_sel="lanes", o_sel="lead")
    n = dc.shape[1]
    tno = _tile(n, 1024, 128)
    return _mm(name, a, dc, _TN, (k // tko, n // tno, m // tmr), a_spec,
               pl.BlockSpec((tmr, tno), lambda i, j, kk: (kk, j)),
               pl.BlockSpec((tko, tno), lambda i, j, kk: (i, j)), (k, n), out_dtype, (tko, tno))


def rms_fwd(name, x, g):
    t, d = x.shape
    tr = _tile(t, 512)

    def body(x_ref, g_ref, o_ref):
        xf = x_ref[...]
        r = lax.rsqrt(jnp.mean(xf * xf, axis=-1, keepdims=True) + EPS)
        o_ref[...] = (xf * r * g_ref[...]).astype(o_ref.dtype)

    return pl.pallas_call(
        body, name=name, grid=(t // tr,),
        in_specs=[pl.BlockSpec((tr, d), lambda i: (i, 0)), pl.BlockSpec((1, d), lambda i: (0, 0))],
        out_specs=pl.BlockSpec((tr, d), lambda i: (i, 0)),
        out_shape=jax.ShapeDtypeStruct((t, d), BF16), compiler_params=_params(("parallel",)),
    )(x, g.reshape(1, d))


def rms_bwd(name, x, g, dh, dres=None, need_dx=True):
    t, d = x.shape
    tr = _tile(t, 512)

    def body(*refs):
        refs = list(refs)
        x_ref, g_ref, dh_ref = refs[:3]
        r_ref = refs[3] if dres is not None else None
        outs = refs[4:] if dres is not None else refs[3:]
        dx_ref, dg_ref = (outs[0], outs[1]) if need_dx else (None, outs[0])
        i = pl.program_id(0)

        @pl.when(i == 0)
        def _():
            dg_ref[...] = jnp.zeros_like(dg_ref)

        xf = x_ref[...]
        dhf = dh_ref[...].astype(F32)
        r = lax.rsqrt(jnp.mean(xf * xf, axis=-1, keepdims=True) + EPS)
        xh = xf * r
        dg_ref[...] += jnp.sum(dhf * xh, axis=0, keepdims=True)
        if need_dx:
            dxh = dhf * g_ref[...]
            dx = r * (dxh - xh * jnp.mean(dxh * xh, axis=-1, keepdims=True))
            if r_ref is not None:
                dx = dx + r_ref[...]
            dx_ref[...] = dx

    row = pl.BlockSpec((tr, d), lambda i: (i, 0))
    vec = pl.BlockSpec((1, d), lambda i: (0, 0))
    in_specs = [row, vec, row] + ([row] if dres is not None else [])
    args = (x, g.reshape(1, d), dh) + ((dres,) if dres is not None else ())
    out_specs = ([row] if need_dx else []) + [vec]
    out_shape = ([jax.ShapeDtypeStruct((t, d), F32)] if need_dx else []) + [jax.ShapeDtypeStruct((1, d), F32)]
    res = pl.pallas_call(
        body, name=name, grid=(t // tr,), in_specs=in_specs, out_specs=out_specs, out_shape=out_shape,
        compiler_params=_params(("arbitrary",)),
    )(*args)
    return res if need_dx else (None, res[0])


def loss_head(x, g, tgt):
    t, d = x.shape
    tr = _tile(t, 512)

    def body(x_ref, g_ref, t_ref, l_ref, dx_ref, dg_ref):
        i = pl.program_id(0)

        @pl.when(i == 0)
        def _():
            l_ref[...] = jnp.zeros_like(l_ref)
            dg_ref[...] = jnp.zeros_like(dg_ref)

        xf = x_ref[...]
        r = lax.rsqrt(jnp.mean(xf * xf, axis=-1, keepdims=True) + EPS)
        xh = xf * r
        diff = xh * g_ref[...] - t_ref[...]
        l_ref[...] += 0.5 * jnp.sum(jnp.mean(diff * diff, axis=-1, keepdims=True))
        dy = diff * (1.0 / d)
        dg_ref[...] += jnp.sum(dy * xh, axis=0, keepdims=True)
        dxh = dy * g_ref[...]
        dx_ref[...] = r * (dxh - xh * jnp.mean(dxh * xh, axis=-1, keepdims=True))

    row = pl.BlockSpec((tr, d), lambda i: (i, 0))
    vec = pl.BlockSpec((1, d), lambda i: (0, 0))
    return pl.pallas_call(
        body, name="loss_head", grid=(t // tr,), in_specs=[row, vec, row],
        out_specs=[pl.BlockSpec((1, 128), lambda i: (0, 0)), row, vec],
        out_shape=[jax.ShapeDtypeStruct((1, 128), F32), jax.ShapeDtypeStruct((t, d), F32),
                   jax.ShapeDtypeStruct((1, d), F32)],
        compiler_params=_params(("arbitrary",)),
    )(x, g.reshape(1, d), tgt)


def glu_fwd(glu, x, gain):
    t, d = x.shape
    tr = _tile(t, 512)

    def body(v_ref, g_ref, x_ref, n_ref, o_ref, h_ref):
        y = x_ref[...] + v_ref[...] * jax.nn.sigmoid(g_ref[...])
        o_ref[...] = y
        r = lax.rsqrt(jnp.mean(y * y, axis=-1, keepdims=True) + EPS)
        h_ref[...] = (y * r * n_ref[...]).astype(h_ref.dtype)

    row = pl.BlockSpec((tr, d), lambda i: (i, 0))
    return pl.pallas_call(
        body, name="glu_fwd", grid=(t // tr,),
        in_specs=[row, pl.BlockSpec((tr, d), lambda i: (i, 1)), row, pl.BlockSpec((1, d), lambda i: (0, 0))],
        out_specs=[row, row],
        out_shape=[jax.ShapeDtypeStruct((t, d), F32), jax.ShapeDtypeStruct((t, d), BF16)],
        compiler_params=_params(("parallel",)),
    )(glu, glu, x, gain.reshape(1, d))


def glu_bwd(glu, dmix):
    t, d = dmix.shape
    tr = _tile(t, 512)

    def body(v_ref, g_ref, d_ref, o_ref):
        sg = jax.nn.sigmoid(g_ref[...])
        dm = d_ref[...]
        o_ref[:, :d] = (dm * sg).astype(o_ref.dtype)
        o_ref[:, d:] = (dm * v_ref[...] * sg * (1.0 - sg)).astype(o_ref.dtype)

    return pl.pallas_call(
        body, name="glu_bwd", grid=(t // tr,),
        in_specs=[pl.BlockSpec((tr, d), lambda i: (i, 0)), pl.BlockSpec((tr, d), lambda i: (i, 1)),
                  pl.BlockSpec((tr, d), lambda i: (i, 0))],
        out_specs=pl.BlockSpec((tr, 2 * d), lambda i: (i, 0)),
        out_shape=jax.ShapeDtypeStruct((t, 2 * d), BF16), compiler_params=_params(("parallel",)),
    )(glu, glu, dmix)


def _head_masks(shape):
    lane = lax.broadcasted_iota(jnp.int32, shape, 1)
    return lane < SB_HEAD_DIM


def _stack_heads(xf, is_a):
    return jnp.concatenate([jnp.where(is_a, xf, 0.0), jnp.where(is_a, 0.0, xf)], axis=0).astype(MXU_DTYPE)


def _diag_mask(qb, row0, rows):
    row = (lax.broadcasted_iota(jnp.int32, (rows, qb), 0) + row0) & (qb - 1)
    col = lax.broadcasted_iota(jnp.int32, (rows, qb), 1)
    return col < row


def _tri01(qb, pred):
    j = lax.broadcasted_iota(jnp.int32, (qb, qb), 0)
    s = lax.broadcasted_iota(jnp.int32, (qb, qb), 1)
    m = pred(j, s).astype(BF16)
    return jnp.concatenate([m, m], axis=0)


def _split_cat(x):
    hi = x.astype(BF16)
    lo = (x - hi.astype(F32)).astype(BF16)
    return jnp.concatenate([hi, lo], axis=1)


def sb_attn_fwd(proj, order, bsz, seq):
    qb = SB_BLOCK
    nq = seq // qb
    npair = SB_WIDTH // 128
    scale = SB_HEAD_DIM ** -0.5

    def body(q_ref, k_ref, v_ref, order_ref, o_ref, r_ref):
        qi = pl.program_id(2)
        is_a = _head_masks((qb, 128))
        q2 = _stack_heads(q_ref[...] * scale, is_a)
        diag = _diag_mask(qb, 0, 2 * qb)
        upper = _tri01(qb, lambda j, s: j > s)

        def blocks(kbs, acc, run, masked):
            sl = [pl.ds(pl.multiple_of(kb * qb, qb), qb) for kb in kbs]
            zs = [lax.dot_general(q2, k_ref[s, :].astype(MXU_DTYPE), _NT, preferred_element_type=F32) for s in sl]
            lks = [-jnp.maximum(z, 0.0) - jnp.log(1.0 + jnp.exp(-jnp.abs(z))) for z in zs]
            lbs = [lk + z for lk, z in zip(lks, zs)]
            if masked:
                lks = [jnp.where(diag, lk, 0.0) for lk in lks]
            cs = [lax.dot_general(_split_cat(lk), upper, _NN, preferred_element_type=F32) for lk in lks]
            for lk, lb, c, s in zip(lks, lbs, cs, sl):
                w = jnp.exp(lb + (run + c))
                if masked:
                    w = jnp.where(diag, w, 0.0)
                acc = acc + lax.dot_general(w.astype(MXU_DTYPE), v_ref[s, :].astype(MXU_DTYPE), _NN,
                                            preferred_element_type=F32)
                run = run + jnp.sum(lk, axis=1, keepdims=True)
            return acc, run

        carry = blocks([qi], jnp.zeros((2 * qb, 128), F32), jnp.zeros((2 * qb, 1), F32), True)
        carry = lax.cond(qi % 2 == 1, lambda c: blocks([qi - 1], c[0], c[1], False), lambda c: c, carry)
        top = qi - qi % 2
        acc, run = lax.fori_loop(
            0, qi // 2, lambda i, c: blocks([top - 1 - 2 * i, top - 2 - 2 * i], c[0], c[1], False), carry)
        o_ref[...] = jnp.where(is_a, acc[:qb], acc[qb:]).astype(o_ref.dtype)
        r_ref[...] = jnp.where(is_a, run[:qb], run[qb:])

    return pl.pallas_call(
        body, name="sb_attn_fwd", grid=(bsz, npair, nq),
        in_specs=[pl.BlockSpec((qb, 128), lambda b, p, i: (b * nq + i, p)),
                  pl.BlockSpec((seq, 128), lambda b, p, i: (b, npair + p)),
                  pl.BlockSpec((seq, 128), lambda b, p, i: (b, 2 * npair + p)),
                  pl.BlockSpec((1, 128), lambda b, p, i: (0, 0))],
        out_specs=[pl.BlockSpec((qb, 128), lambda b, p, i: (b * nq + i, p)),
                   pl.BlockSpec((qb, 128), lambda b, p, i: (b * nq + i, p))],
        out_shape=[jax.ShapeDtypeStruct((bsz * seq, SB_WIDTH), BF16),
                   jax.ShapeDtypeStruct((bsz * seq, SB_WIDTH), F32)],
        compiler_params=_params(("parallel", "parallel", "arbitrary")),
    )(proj, proj, proj, order)


def sb_attn_bwd(proj, rsum, dcat, bsz, seq):
    qb = SB_BLOCK
    nq = seq // qb
    npair = SB_WIDTH // 128
    scale = SB_HEAD_DIM ** -0.5

    def body(q_ref, k_ref, v_ref, r_ref, do_ref, dq_ref, dk_ref, dv_ref):
        qi = pl.program_id(2)

        @pl.when(qi == 0)
        def _():
            dk_ref[...] = jnp.zeros_like(dk_ref)
            dv_ref[...] = jnp.zeros_like(dv_ref)

        is_a = _head_masks((qb, 128))
        q2 = _stack_heads(q_ref[...] * scale, is_a)
        do2 = _stack_heads(do_ref[...].astype(F32), is_a)
        rf = r_ref[...]
        rtot = jnp.concatenate([rf[:, 0:1], rf[:, SB_HEAD_DIM:SB_HEAD_DIM + 1]], axis=0)
        diag = _diag_mask(qb, 0, 2 * qb)
        incl = _tri01(qb, lambda j, s: j <= s)
        strict = _tri01(qb, lambda j, s: j < s)

        def blocks(kbs, dq, pre, epre, masked):
            sl = [pl.ds(pl.multiple_of(kb * qb, qb), qb) for kb in kbs]
            ks = [k_ref[s, :].astype(MXU_DTYPE) for s in sl]
            vs = [v_ref[s, :].astype(MXU_DTYPE) for s in sl]
            zs = [lax.dot_general(q2, kblk, _NT, preferred_element_type=F32) for kblk in ks]
            dws = [lax.dot_general(do2, vblk, _NT, preferred_element_type=F32) for vblk in vs]
            lks = [-jnp.maximum(z, 0.0) - jnp.log(1.0 + jnp.exp(-jnp.abs(z))) for z in zs]
            lbs = [lk + z for lk, z in zip(lks, zs)]
            if masked:
                lks = [jnp.where(diag, lk, 0.0) for lk in lks]
            ps = [lax.dot_general(_split_cat(lk), incl, _NN, preferred_element_type=F32) for lk in lks]
            ws, es = [], []
            for lk, lb, p, dw in zip(lks, lbs, ps, dws):
                w = jnp.exp(lb + (rtot - (pre + p)))
                if masked:
                    w = jnp.where(diag, w, 0.0)
                ws.append(w)
                es.append(dw * w)
                pre = pre + jnp.sum(lk, axis=1, keepdims=True)
            cs = [lax.dot_general(_split_cat(e), strict, _NN, preferred_element_type=F32) for e in es]
            for e, lb, c, w, kblk, s in zip(es, lbs, cs, ws, ks, sl):
                dz = e - jnp.exp(lb) * (e + (epre + c))
                if masked:
                    dz = jnp.where(diag, dz, 0.0)
                dz = dz.astype(MXU_DTYPE)
                dq = dq + lax.dot_general(dz, kblk, _NN, preferred_element_type=F32)
                dk_ref[s, :] += lax.dot_general(dz, q2, _TN, preferred_element_type=F32)
                dv_ref[s, :] += lax.dot_general(w.astype(MXU_DTYPE), do2, _TN, preferred_element_type=F32)
                epre = epre + jnp.sum(e, axis=1, keepdims=True)
            return dq, pre, epre

        zc = jnp.zeros((2 * qb, 1), F32)
        carry = lax.fori_loop(0, qi // 2, lambda i, c: blocks([2 * i, 2 * i + 1], c[0], c[1], c[2], False),
                              (jnp.zeros((2 * qb, 128), F32), zc, zc))
        carry = lax.cond(qi % 2 == 1, lambda c: blocks([qi - 1], c[0], c[1], c[2], False), lambda c: c, carry)
        dq = blocks([qi], carry[0], carry[1], carry[2], True)[0]
        dq_ref[...] = jnp.where(is_a, dq[:qb], dq[qb:]) * scale

    full = jax.ShapeDtypeStruct((bsz * seq, SB_WIDTH), F32)
    qspec = pl.BlockSpec((qb, 128), lambda b, p, i: (b * nq + i, p))
    return pl.pallas_call(
        body, name="sb_attn_bwd", grid=(bsz, npair, nq),
        in_specs=[qspec,
                  pl.BlockSpec((seq, 128), lambda b, p, i: (b, npair + p)),
                  pl.BlockSpec((seq, 128), lambda b, p, i: (b, 2 * npair + p)),
                  qspec, qspec],
        out_specs=[qspec, pl.BlockSpec((seq, 128), lambda b, p, i: (b, p)),
                   pl.BlockSpec((seq, 128), lambda b, p, i: (b, p))],
        out_shape=[full, full, full],
        compiler_params=_params(("parallel", "parallel", "arbitrary")),
    )(proj, proj, proj, rsum, dcat)


def _window_sums(x, forward):
    n = x.shape[0]
    out = []
    s = x
    for sh in (1, 2, 4, 8):
        s = s + pltpu.roll(s, (n - sh) if forward else sh, 0)
        out.append(s)
    return out


def _pool_counts(tc, c, w):
    t = lax.broadcasted_iota(jnp.int32, (tc, 1), 0) + c * tc
    return jnp.minimum(t + 1, w).astype(F32)


def pool_fwd(proj, pool_w, pool_scale, bsz, seq):
    tc = _tile(seq, 512)
    nc = seq // tc
    hb = tc // POOL_HALO
    ucol = 3

    def body(u_ref, prev_ref, w_ref, s_ref, o_ref):
        c = pl.program_id(1)
        prev = jnp.where(c > 0, prev_ref[...], 0.0)
        x = jnp.concatenate([prev, u_ref[...]], axis=0)
        sums = _window_sums(x, forward=False)
        for g, win in enumerate(POOL_WINDOWS):
            ls = slice(g * POOL_GROUP, (g + 1) * POOL_GROUP)
            pooled = sums[g][POOL_HALO:, ls] / _pool_counts(tc, c, win) - x[POOL_HALO:, ls]
            y = _dot(pooled, w_ref[g], _NN)
            o_ref[:, ls] = (y * s_ref[:, ls]).astype(o_ref.dtype)

    return pl.pallas_call(
        body, name="pool_fwd", grid=(bsz, nc),
        in_specs=[pl.BlockSpec((tc, SB_WIDTH), lambda b, c: (b * nc + c, ucol)),
                  pl.BlockSpec((POOL_HALO, SB_WIDTH), lambda b, c: (jnp.maximum((b * nc + c) * hb - 1, 0), ucol)),
                  pl.BlockSpec((4, POOL_GROUP, POOL_GROUP), lambda b, c: (0, 0, 0)),
                  pl.BlockSpec((1, SB_WIDTH), lambda b, c: (0, 0))],
        out_specs=pl.BlockSpec((tc, SB_WIDTH), lambda b, c: (b * nc + c, 0)),
        out_shape=jax.ShapeDtypeStruct((bsz * seq, SB_WIDTH), BF16),
        compiler_params=_params(("parallel", "parallel")),
    )(proj, proj, pool_w, pool_scale)


def pool_bwd(proj, pool_w, pool_scale, dcat, bsz, seq):
    tc = _tile(seq, 512)
    nc = seq // tc
    hb = tc // POOL_HALO
    nblk = bsz * seq // POOL_HALO
    ucol = 3

    def body(u_ref, prev_ref, dy_ref, nxt_ref, w_ref, s_ref, du_ref, dw_ref, ds_ref):
        b, c = pl.program_id(0), pl.program_id(1)

        @pl.when((b == 0) & (c == 0))
        def _():
            dw_ref[...] = jnp.zeros_like(dw_ref)
            ds_ref[...] = jnp.zeros_like(ds_ref)

        prev = jnp.where(c > 0, prev_ref[...], 0.0)
        x = jnp.concatenate([prev, u_ref[...]], axis=0)
        sums = _window_sums(x, forward=False)
        nxt = jnp.where(c < nc - 1, nxt_ref[...].astype(F32), 0.0)
        dy = jnp.concatenate([dy_ref[...].astype(F32), nxt], axis=0)
        tq = lax.broadcasted_iota(jnp.int32, (tc + POOL_HALO, 1), 0) + c * tc
        for g, win in enumerate(POOL_WINDOWS):
            ls = slice(g * POOL_GROUP, (g + 1) * POOL_GROUP)
            pooled = sums[g][POOL_HALO:, ls] / _pool_counts(tc, c, win) - x[POOL_HALO:, ls]
            y = _dot(pooled, w_ref[g], _NN)
            ds_ref[:, ls] += jnp.sum(dy[:tc, ls] * y, axis=0, keepdims=True)
            dz = dy[:, ls] * s_ref[:, ls]
            dw_ref[g] += _dot(pooled, dz[:tc], _TN)
            dpool = _dot(dz, w_ref[g], _NT)
            dmean = dpool / jnp.minimum(tq + 1, win).astype(F32)
            fsum = _window_sums(dmean, forward=True)[g]
            du_ref[:, ls] = fsum[:tc] - dpool[:tc]

    return pl.pallas_call(
        body, name="pool_bwd", grid=(bsz, nc),
        in_specs=[pl.BlockSpec((tc, SB_WIDTH), lambda b, c: (b * nc + c, ucol)),
                  pl.BlockSpec((POOL_HALO, SB_WIDTH), lambda b, c: (jnp.maximum((b * nc + c) * hb - 1, 0), ucol)),
                  pl.BlockSpec((tc, SB_WIDTH), lambda b, c: (b * nc + c, 1)),
                  pl.BlockSpec((POOL_HALO, SB_WIDTH), lambda b, c: (jnp.minimum((b * nc + c + 1) * hb, nblk - 1), 1)),
                  pl.BlockSpec((4, POOL_GROUP, POOL_GROUP), lambda b, c: (0, 0, 0)),
                  pl.BlockSpec((1, SB_WIDTH), lambda b, c: (0, 0))],
        out_specs=[pl.BlockSpec((tc, SB_WIDTH), lambda b, c: (b * nc + c, 0)),
                   pl.BlockSpec((4, POOL_GROUP, POOL_GROUP), lambda b, c: (0, 0, 0)),
                   pl.BlockSpec((1, SB_WIDTH), lambda b, c: (0, 0))],
        out_shape=[jax.ShapeDtypeStruct((bsz * seq, SB_WIDTH), F32),
                   jax.ShapeDtypeStruct((4, POOL_GROUP, POOL_GROUP), F32),
                   jax.ShapeDtypeStruct((1, SB_WIDTH), F32)],
        compiler_params=_params(("arbitrary", "arbitrary")),
    )(proj, proj, dcat, dcat, pool_w, pool_scale)


def _lbar(lam_re, lam_im, log_dt):
    dt = jnp.exp(log_dt)
    mag = jnp.exp(lam_re * dt)
    ang = lam_im * dt
    return mag * jnp.cos(ang), mag * jnp.sin(ang)


def _bbar(lam_re, lam_im, log_dt, b_re, b_im):
    lb_re, lb_im = _lbar(lam_re, lam_im, log_dt)
    n_re = lb_re - 1.0
    den = lam_re * lam_re + lam_im * lam_im
    coef_re = (n_re * lam_re + lb_im * lam_im) / den
    coef_im = (lb_im * lam_re - n_re * lam_im) / den
    return coef_re * b_re - coef_im * b_im, coef_re * b_im + coef_im * b_re


def _expand01():
    p = lax.broadcasted_iota(jnp.int32, (64, 1024), 0)
    q = lax.broadcasted_iota(jnp.int32, (64, 1024), 1)
    return (lax.shift_right_logical(q, 4) == p).astype(BF16)


def ssm_prep(lam_re, lam_im, log_dt, b_re2, b_im2):
    def body(lr_ref, li_ref, dt_ref, br_ref, bi_ref, ar_ref, ai_ref, bbr_ref, bbi_ref):
        e = _expand01()
        lr, li, dt = lr_ref[...], li_ref[...], dt_ref[...]
        ar_ref[...], ai_ref[...] = _lbar(lr, li, dt)
        bbr_ref[...], bbi_ref[...] = _bbar(_dot_exact01(lr, e), _dot_exact01(li, e), dt, br_ref[...], bi_ref[...])

    s64 = jax.ShapeDtypeStruct((64, 64), F32)
    s1k = jax.ShapeDtypeStruct((64, 1024), F32)
    return pl.pallas_call(body, name="ssm_prep", out_shape=[s64, s64, s1k, s1k], compiler_params=_params())(
        lam_re, lam_im, log_dt, b_re2, b_im2)


def ssm_prep_bwd(lam_re, lam_im, log_dt, b_re2, b_im2, da_re, da_im, dbb_re, dbb_im):
    def body(lr_ref, li_ref, dt_ref, br_ref, bi_ref, dar_ref, dai_ref, dbr_ref, dbi_ref,
             olr_ref, oli_ref, odt_ref, obr_ref, obi_ref):
        e = _expand01()
        lr, li, dt = lr_ref[...], li_ref[...], dt_ref[...]
        _, vjp_a = jax.vjp(_lbar, lr, li, dt)
        g_lr, g_li, g_dt = vjp_a((dar_ref[...], dai_ref[...]))
        _, vjp_b = jax.vjp(_bbar, _dot_exact01(lr, e), _dot_exact01(li, e), dt, br_ref[...], bi_ref[...])
        x_lr, x_li, x_dt, g_br, g_bi = vjp_b((dbr_ref[...], dbi_ref[...]))
        olr_ref[...] = g_lr + _dot_exact01(x_lr, e, _NT)
        oli_ref[...] = g_li + _dot_exact01(x_li, e, _NT)
        odt_ref[...] = g_dt + x_dt
        obr_ref[...] = g_br
        obi_ref[...] = g_bi

    s64 = jax.ShapeDtypeStruct((64, 64), F32)
    s1k = jax.ShapeDtypeStruct((64, 1024), F32)
    return pl.pallas_call(body, name="ssm_prep_bwd",
                          out_shape=[s64, s64, jax.ShapeDtypeStruct((64, 1), F32), s1k, s1k],
                          compiler_params=_params())(
        lam_re, lam_im, log_dt, b_re2, b_im2, da_re, da_im, dbb_re, dbb_im)


def _gelu(y):
    c = math.sqrt(2.0 / math.pi)
    return 0.5 * y * (1.0 + jnp.tanh(c * (y + 0.044715 * y * y * y)))


def _gelu_grad(y):
    c = math.sqrt(2.0 / math.pi)
    th = jnp.tanh(c * (y + 0.044715 * y * y * y))
    return 0.5 * (1.0 + th) + 0.5 * y * (1.0 - th * th) * c * (1.0 + 3.0 * 0.044715 * y * y)


def _cmul(ar, ai, br, bi):
    return ar * br - ai * bi, ar * bi + ai * br


def _scan_tables(ar, ai, reverse, tabs):
    row = lax.broadcasted_iota(jnp.int32, (8, SSM_STATES), 0)
    a1 = (ar, ai)
    a2 = _cmul(*a1, *a1)
    a4 = _cmul(*a2, *a2)
    powers = [a1, a2, _cmul(*a2, *a1), a4]
    powers += [_cmul(*a4, *p) for p in powers]
    for k, (val, sh) in enumerate(((a1, 1), (a2, 2), (a4, 4))):
        keep = (row < 8 - sh) if reverse else (row >= sh)
        tabs[2 * k][...] = jnp.where(keep, val[0], 0.0)
        tabs[2 * k + 1][...] = jnp.where(keep, val[1], 0.0)
    pr = jnp.zeros((8, SSM_STATES), F32)
    pi = jnp.zeros((8, SSM_STATES), F32)
    for r in range(8):
        val = powers[7 - r] if reverse else powers[r]
        pr = jnp.where(row == r, val[0], pr)
        pi = jnp.where(row == r, val[1], pi)
    tabs[6][...] = pr
    tabs[7][...] = pi


def _scan8(xr, xi, tabs, ls, cr, ci, reverse):
    for k, sh in enumerate((1, 2, 4)):
        amt = (8 - sh) if reverse else sh
        sr, si = pltpu.roll(xr, amt, 0), pltpu.roll(xi, amt, 0)
        lr, li = tabs[2 * k][:, ls], tabs[2 * k + 1][:, ls]
        xr, xi = xr + lr * sr - li * si, xi + lr * si + li * sr
    pr, pi = tabs[6][:, ls], tabs[7][:, ls]
    return xr + pr * cr - pi * ci, xi + pr * ci + pi * cr


def _block8(b):
    return pl.ds(pl.multiple_of(b * 8, 8), 8)


def ssm_fwd(u, wt, ct, a_re, a_im, dskip, bsz, seq):
    tc = _tile(seq, 256)
    nc = seq // tc
    ns = SSM_TILE_STATES
    nl = SSM_STATES // SSM_LANES

    def body(u_ref, wt_ref, ct_ref, ar_ref, ai_ref, d_ref, y_ref, gl_ref, hr_ref, hi_ref, sr_ref, si_ref, *tabs):
        b, c = pl.program_id(0), pl.program_id(1)

        @pl.when((b == 0) & (c == 0))
        def _():
            _scan_tables(ar_ref[...], ai_ref[...], False, tabs)

        @pl.when(c == 0)
        def _():
            sr_ref[...] = jnp.zeros_like(sr_ref)
            si_ref[...] = jnp.zeros_like(si_ref)

        uf = u_ref[...]
        for i in range(SSM_TILES):
            bu = _dot(uf[:, i * 128:(i + 1) * 128], wt_ref[i], _NN)
            hr_ref[:, i * ns:(i + 1) * ns] = bu[:, :ns]
            hi_ref[:, i * ns:(i + 1) * ns] = bu[:, ns:]

        def step(blk, carry):
            rows = _block8(blk)
            new = []
            for j in range(nl):
                ls = slice(j * SSM_LANES, (j + 1) * SSM_LANES)
                xr, xi = _scan8(hr_ref[rows, ls], hi_ref[rows, ls], tabs, ls, carry[2 * j], carry[2 * j + 1], False)
                hr_ref[rows, ls] = xr
                hi_ref[rows, ls] = xi
                new += [xr[7:8], xi[7:8]]
            return tuple(new)

        init = []
        for j in range(nl):
            ls = slice(j * SSM_LANES, (j + 1) * SSM_LANES)
            init += [sr_ref[:, ls], si_ref[:, ls]]
        last = lax.fori_loop(0, tc // 8, step, tuple(init), unroll=2)
        for j in range(nl):
            ls = slice(j * SSM_LANES, (j + 1) * SSM_LANES)
            sr_ref[:, ls] = last[2 * j]
            si_ref[:, ls] = last[2 * j + 1]
        for i in range(SSM_TILES):
            hcat = jnp.concatenate([hr_ref[:, i * ns:(i + 1) * ns], hi_ref[:, i * ns:(i + 1) * ns]], axis=1)
            ls = slice(i * 128, (i + 1) * 128)
            y = _dot(hcat, ct_ref[i], _NN) + d_ref[:, ls] * uf[:, ls]
            y_ref[:, ls] = y
            gl_ref[:, ls] = _gelu(y).astype(gl_ref.dtype)

    t = bsz * seq
    row = pl.BlockSpec((tc, D_MODEL), lambda b, c: (b * nc + c, 0))
    st = pl.BlockSpec((tc, SSM_STATES), lambda b, c: (b * nc + c, 0))
    diag = pl.BlockSpec((1, SSM_STATES), lambda b, c: (0, 0))
    return pl.pallas_call(
        body, name="ssm_fwd", grid=(bsz, nc),
        in_specs=[row, pl.BlockSpec((SSM_TILES, 128, 2 * ns), lambda b, c: (0, 0, 0)),
                  pl.BlockSpec((SSM_TILES, 2 * ns, 128), lambda b, c: (0, 0, 0)), diag, diag,
                  pl.BlockSpec((1, D_MODEL), lambda b, c: (0, 0))],
        out_specs=[row, row, st, st],
        out_shape=[jax.ShapeDtypeStruct((t, D_MODEL), F32), jax.ShapeDtypeStruct((t, D_MODEL), BF16),
                   jax.ShapeDtypeStruct((t, SSM_STATES), F32), jax.ShapeDtypeStruct((t, SSM_STATES), F32)],
        scratch_shapes=[pltpu.VMEM((1, SSM_STATES), F32)] * 2 + [pltpu.VMEM((8, SSM_STATES), F32)] * 8,
        compiler_params=_params(("arbitrary", "arbitrary")),
    )(u, wt, ct, a_re, a_im, dskip)


def ssm_bwd(dgl, y, u, h_re, h_im, wt, ct, a_re, a_im, dskip, bsz, seq):
    tc = _tile(seq, 256)
    nc = seq // tc
    nb = tc // 8
    ns = SSM_TILE_STATES
    nl = SSM_STATES // SSM_LANES

    def body(dgl_ref, y_ref, u_ref, hr_ref, hi_ref, pr_ref, pi_ref, wt_ref, ct_ref, ar_ref, ai_ref, d_ref,
             du_ref, dwt_ref, dct_ref, dd_ref, dar_ref, dai_ref, gr_ref, gi_ref, sr_ref, si_ref, ar8_ref, ai8_ref,
             *tabs):
        b, c = pl.program_id(0), pl.program_id(1)

        @pl.when((b == 0) & (c == 0))
        def _():
            for r in (dwt_ref, dct_ref, dd_ref, ar8_ref, ai8_ref):
                r[...] = jnp.zeros_like(r)
            _scan_tables(ar_ref[...], -ai_ref[...], True, tabs)

        @pl.when(c == 0)
        def _():
            sr_ref[...] = jnp.zeros_like(sr_ref)
            si_ref[...] = jnp.zeros_like(si_ref)

        uf = u_ref[...]
        dy = dgl_ref[...].astype(F32) * _gelu_grad(y_ref[...])
        dd_ref[...] += jnp.sum(dy * uf, axis=0, keepdims=True)
        for i in range(SSM_TILES):
            dyi = dy[:, i * 128:(i + 1) * 128]
            dh = _dot(dyi, ct_ref[i], _NT)
            gr_ref[:, i * ns:(i + 1) * ns] = dh[:, :ns]
            gi_ref[:, i * ns:(i + 1) * ns] = dh[:, ns:]
            hcat = jnp.concatenate([hr_ref[:, i * ns:(i + 1) * ns], hi_ref[:, i * ns:(i + 1) * ns]], axis=1)
            dct_ref[i] += _dot(hcat, dyi, _TN)
        row0 = lax.broadcasted_iota(jnp.int32, (8, SSM_LANES), 0) == 0

        def block(blk, carry, before):
            rows = _block8(blk)
            new = []
            for j in range(nl):
                ls = slice(j * SSM_LANES, (j + 1) * SSM_LANES)
                gr, gi = _scan8(gr_ref[rows, ls], gi_ref[rows, ls], tabs, ls, carry[2 * j], carry[2 * j + 1], True)
                gr_ref[rows, ls] = gr
                gi_ref[rows, ls] = gi
                bpr, bpi = before(j)
                hpr = jnp.where(row0, bpr, pltpu.roll(hr_ref[rows, ls], 1, 0))
                hpi = jnp.where(row0, bpi, pltpu.roll(hi_ref[rows, ls], 1, 0))
                ar8_ref[:, ls] += gr * hpr + gi * hpi
                ai8_ref[:, ls] += gi * hpr - gr * hpi
                new += [gr[0:1], gi[0:1]]
            return tuple(new)

        def step(jj, carry):
            blk = nb - 1 - jj
            prev_rows = _block8(blk - 1)

            def before(j):
                ls = slice(j * SSM_LANES, (j + 1) * SSM_LANES)
                return hr_ref[prev_rows, ls][7:8], hi_ref[prev_rows, ls][7:8]

            return block(blk, carry, before)

        init = []
        for j in range(nl):
            ls = slice(j * SSM_LANES, (j + 1) * SSM_LANES)
            init += [sr_ref[:, ls], si_ref[:, ls]]
        carry = lax.fori_loop(0, nb - 1, step, tuple(init))
        first = c == nc - 1

        def before_chunk(j):
            ls = slice(j * SSM_LANES, (j + 1) * SSM_LANES)
            return (jnp.where(first, 0.0, pr_ref[:, ls][7:8]), jnp.where(first, 0.0, pi_ref[:, ls][7:8]))

        last = block(0, carry, before_chunk)
        for j in range(nl):
            ls = slice(j * SSM_LANES, (j + 1) * SSM_LANES)
            sr_ref[:, ls] = last[2 * j]
            si_ref[:, ls] = last[2 * j + 1]
        for i in range(SSM_TILES):
            ls = slice(i * 128, (i + 1) * 128)
            gcat = jnp.concatenate([gr_ref[:, i * ns:(i + 1) * ns], gi_ref[:, i * ns:(i + 1) * ns]], axis=1)
            du_ref[:, ls] = (_dot(gcat, wt_ref[i], _NT) + d_ref[:, ls] * dy[:, ls]).astype(du_ref.dtype)
            dwt_ref[i] += _dot(uf[:, ls], gcat, _TN)

        @pl.when((b == bsz - 1) & (c == nc - 1))
        def _():
            dar_ref[...] = jnp.sum(ar8_ref[...], axis=0, keepdims=True)
            dai_ref[...] = jnp.sum(ai8_ref[...], axis=0, keepdims=True)

    t = bsz * seq
    rev = lambda b, c: (b * nc + (nc - 1 - c), 0)
    row = pl.BlockSpec((tc, D_MODEL), rev)
    st = pl.BlockSpec((tc, SSM_STATES), rev)
    prev = pl.BlockSpec((8, SSM_STATES), lambda b, c: (jnp.maximum((b * nc + (nc - 1 - c)) * nb - 1, 0), 0))
    diag = pl.BlockSpec((1, SSM_STATES), lambda b, c: (0, 0))
    wts = pl.BlockSpec((SSM_TILES, 128, 2 * ns), lambda b, c: (0, 0, 0))
    cts = pl.BlockSpec((SSM_TILES, 2 * ns, 128), lambda b, c: (0, 0, 0))
    vec = pl.BlockSpec((1, D_MODEL), lambda b, c: (0, 0))
    return pl.pallas_call(
        body, name="ssm_bwd", grid=(bsz, nc),
        in_specs=[row, row, row, st, st, prev, prev, wts, cts, diag, diag, vec],
        out_specs=[row, wts, cts, vec, diag, diag],
        out_shape=[jax.ShapeDtypeStruct((t, D_MODEL), BF16),
                   jax.ShapeDtypeStruct((SSM_TILES, 128, 2 * ns), F32),
                   jax.ShapeDtypeStruct((SSM_TILES, 2 * ns, 128), F32),
                   jax.ShapeDtypeStruct((1, D_MODEL), F32),
                   jax.ShapeDtypeStruct((1, SSM_STATES), F32), jax.ShapeDtypeStruct((1, SSM_STATES), F32)],
        scratch_shapes=[pltpu.VMEM((tc, SSM_STATES), F32)] * 2 + [pltpu.VMEM((1, SSM_STATES), F32)] * 2
                       + [pltpu.VMEM((8, SSM_STATES), F32)] * 10,
        compiler_params=_params(("arbitrary", "arbitrary")),
    )(dgl, y, u, h_re, h_im, h_re, h_im, wt, ct, a_re, a_im, dskip)


def _ssm_in_weights(bb_re2, bb_im2):
    eye = jnp.eye(8, dtype=F32)[None, :, None, :, None]

    def one(bb):
        t = bb.reshape(8, 8, 64, 16).transpose(0, 1, 3, 2)
        return (t[:, :, :, None, :] * eye).reshape(8, 128, 512)

    return jnp.concatenate([one(bb_re2), one(bb_im2)], axis=-1).astype(MXU_DTYPE)


def _ssm_in_weights_bwd(dwt):
    eye = jnp.eye(8, dtype=F32)[None, :, None, :, None]

    def one(d):
        t = (d.reshape(8, 8, 16, 8, 64) * eye).sum(axis=3)
        return t.transpose(0, 1, 3, 2).reshape(64, 1024)

    return one(dwt[..., :512]), one(dwt[..., 512:])


def _ssm_out_weights(c_re, c_im):
    eye = jnp.eye(8, dtype=F32)[None, :, None, :, None]

    def one(cc):
        t = cc.reshape(8, 8, 16, 64).transpose(0, 1, 3, 2)
        return (t[:, :, :, None, :] * eye).reshape(8, 512, 128)

    return jnp.concatenate([one(c_re), -one(c_im)], axis=1).astype(MXU_DTYPE)


def _ssm_out_weights_bwd(dct):
    eye = jnp.eye(8, dtype=F32)[None, :, None, :, None]

    def one(d):
        t = (d.reshape(8, 8, 64, 8, 16) * eye).sum(axis=3)
        return t.transpose(0, 1, 3, 2).reshape(64, 16, 64)

    return one(dct[:, :512]), -one(dct[:, 512:])


def _softmax(s):
    m = jnp.max(s, axis=-1, keepdims=True)
    e = jnp.exp(s - m)
    return e / jnp.sum(e, axis=-1, keepdims=True)


def xattn_fwd(q, kv, bsz, seq):
    tq = _tile(seq, 512)
    nq = seq // tq
    scale = XA_HEAD_DIM ** -0.5

    def body(q_ref, k_ref, v_ref, o_ref):
        s = lax.dot_general(q_ref[...], k_ref[...], _NT, preferred_element_type=F32) * scale
        p = _softmax(s)
        o_ref[...] = _dot(p, v_ref[...], _NN).astype(o_ref.dtype)

    qs = pl.BlockSpec((tq, XA_HEAD_DIM), lambda b, h, i: (b * nq + i, h))
    return pl.pallas_call(
        body, name="xattn_fwd", grid=(bsz, XA_HEADS, nq),
        in_specs=[qs, pl.BlockSpec((MEM_LEN, XA_HEAD_DIM), lambda b, h, i: (b, h)),
                  pl.BlockSpec((MEM_LEN, XA_HEAD_DIM), lambda b, h, i: (b, XA_HEADS + h))],
        out_specs=qs, out_shape=jax.ShapeDtypeStruct((bsz * seq, D_MODEL), BF16),
        compiler_params=_params(("parallel", "parallel", "parallel")),
    )(q, kv, kv)


def xattn_bwd(q, kv, do, bsz, seq):
    tq = _tile(seq, 512)
    nq = seq // tq
    scale = XA_HEAD_DIM ** -0.5

    def body(q_ref, k_ref, v_ref, do_ref, dq_ref, dk_ref, dv_ref):
        @pl.when(pl.program_id(2) == 0)
        def _():
            dk_ref[...] = jnp.zeros_like(dk_ref)
            dv_ref[...] = jnp.zeros_like(dv_ref)

        qv, kk, vv, dov = q_ref[...], k_ref[...], v_ref[...], do_ref[...]
        s = lax.dot_general(qv, kk, _NT, preferred_element_type=F32) * scale
        p = _softmax(s)
        dp = lax.dot_general(dov, vv, _NT, preferred_element_type=F32)
        ds = (p * (dp - jnp.sum(dp * p, axis=-1, keepdims=True)) * scale).astype(MXU_DTYPE)
        dq_ref[...] = lax.dot_general(ds, kk, _NN, preferred_element_type=F32).astype(dq_ref.dtype)
        dk_ref[...] += lax.dot_general(ds, qv, _TN, preferred_element_type=F32)
        dv_ref[...] += lax.dot_general(p.astype(MXU_DTYPE), dov, _TN, preferred_element_type=F32)

    qs = pl.BlockSpec((tq, XA_HEAD_DIM), lambda b, h, i: (b * nq + i, h))
    ks = pl.BlockSpec((MEM_LEN, XA_HEAD_DIM), lambda b, h, i: (b, h))
    vs = pl.BlockSpec((MEM_LEN, XA_HEAD_DIM), lambda b, h, i: (b, XA_HEADS + h))
    dkv = jax.ShapeDtypeStruct((bsz * MEM_LEN, D_MODEL), F32)
    dq, dk, dv = pl.pallas_call(
        body, name="xattn_bwd", grid=(bsz, XA_HEADS, nq),
        in_specs=[qs, ks, vs, qs], out_specs=[qs, ks, ks],
        out_shape=[jax.ShapeDtypeStruct((bsz * seq, D_MODEL), BF16), dkv, dkv],
        compiler_params=_params(("parallel", "parallel", "arbitrary")),
    )(q, kv, kv, do)
    return dq, dk, dv


CONV_HALO = 16


def _shifts_down(x, prev):
    h = prev.shape[0]
    ext = jnp.concatenate([prev, x], axis=0)
    return pltpu.roll(ext, 1, 0)[h:], pltpu.roll(ext, 2, 0)[h:]


def _shifts_up(x, nxt):
    rows = x.shape[0]
    n = rows + nxt.shape[0]
    ext = jnp.concatenate([x, nxt], axis=0)
    return pltpu.roll(ext, n - 1, 0)[:rows], pltpu.roll(ext, n - 2, 0)[:rows]


def _conv_taps(u, u1, u2, w, b):
    return b + w[2:3] * u + w[1:2] * u1 + w[0:1] * u2


def conv_fwd(up, cw, cb, bsz, seq):
    tc = _tile(seq, 512)
    nc = seq // tc
    hb = tc // CONV_HALO
    half = N_DEV // 2

    def body(uv_ref, ug_ref, pv_ref, pg_ref, wv_ref, wg_ref, bv_ref, bg_ref, o_ref):
        c = pl.program_id(2)
        pv = jnp.where(c > 0, pv_ref[...].astype(F32), 0.0)
        pg = jnp.where(c > 0, pg_ref[...].astype(F32), 0.0)
        uv, ug = uv_ref[...].astype(F32), ug_ref[...].astype(F32)
        val = _conv_taps(uv, *_shifts_down(uv, pv), wv_ref[...], bv_ref[...])
        gate = _conv_taps(ug, *_shifts_down(ug, pg), wg_ref[...], bg_ref[...])
        o_ref[...] = (gate * jax.nn.sigmoid(gate) * val).astype(o_ref.dtype)

    def cur(off):
        return pl.BlockSpec((None, tc, FF_SHARD), lambda b, j, c: (j + off, b * nc + c, 0))

    def prv(off):
        return pl.BlockSpec((None, CONV_HALO, FF_SHARD), lambda b, j, c: (j + off, jnp.maximum((b * nc + c) * hb - 1, 0), 0))

    def par(rows, off):
        return pl.BlockSpec((None, rows, FF_SHARD), lambda b, j, c: (j + off, 0, 0))

    return pl.pallas_call(
        body, name="conv_fwd", grid=(bsz, half, nc),
        in_specs=[cur(0), cur(half), prv(0), prv(half), par(3, 0), par(3, half), par(1, 0), par(1, half)],
        out_specs=cur(0), out_shape=jax.ShapeDtypeStruct((half, bsz * seq, FF_SHARD), BF16),
        compiler_params=_params(("parallel", "parallel", "parallel")),
    )(up, up, up, up, cw, cw, cb, cb)


def conv_bwd_taps(up, cw, cb, dact, bsz, seq):
    tc = _tile(seq, 512)
    nc = seq // tc
    hb = tc // CONV_HALO
    half = N_DEV // 2

    def body(uv_ref, ug_ref, pv_ref, pg_ref, wv_ref, wg_ref, bv_ref, bg_ref, da_ref,
             dc_ref, dwv_ref, dwg_ref, dbv_ref, dbg_ref):
        b, c = pl.program_id(1), pl.program_id(2)

        @pl.when((b == 0) & (c == 0))
        def _():
            for r in (dwv_ref, dwg_ref, dbv_ref, dbg_ref):
                r[...] = jnp.zeros_like(r)

        pv = jnp.where(c > 0, pv_ref[...].astype(F32), 0.0)
        pg = jnp.where(c > 0, pg_ref[...].astype(F32), 0.0)
        uv, ug = uv_ref[...].astype(F32), ug_ref[...].astype(F32)
        uv1, uv2 = _shifts_down(uv, pv)
        ug1, ug2 = _shifts_down(ug, pg)
        val = _conv_taps(uv, uv1, uv2, wv_ref[...], bv_ref[...])
        gate = _conv_taps(ug, ug1, ug2, wg_ref[...], bg_ref[...])
        sg = jax.nn.sigmoid(gate)
        da = da_ref[...].astype(F32)
        dsilu = da * sg
        dval = dsilu * gate
        dgate = dsilu * val * (1.0 + gate * (1.0 - sg))
        dc_ref[0] = dval.astype(dc_ref.dtype)
        dc_ref[1] = dgate.astype(dc_ref.dtype)
        for dcv, taps, dw_ref, db_ref in ((dval, (uv2, uv1, uv), dwv_ref, dbv_ref),
                                          (dgate, (ug2, ug1, ug), dwg_ref, dbg_ref)):
            db_ref[...] += jnp.sum(dcv, axis=0, keepdims=True)
            for k, u_k in enumerate(taps):
                dw_ref[k:k + 1, :] += jnp.sum(dcv * u_k, axis=0, keepdims=True)

    def cur(off):
        return pl.BlockSpec((None, tc, FF_SHARD), lambda j, b, c: (j + off, b * nc + c, 0))

    def prv(off):
        return pl.BlockSpec((None, CONV_HALO, FF_SHARD), lambda j, b, c: (j + off, jnp.maximum((b * nc + c) * hb - 1, 0), 0))

    def par(rows, off):
        return pl.BlockSpec((None, rows, FF_SHARD), lambda j, b, c: (j + off, 0, 0))

    t = bsz * seq
    hs = jax.ShapeDtypeStruct((2, half, t, FF_SHARD), BF16)
    ws = jax.ShapeDtypeStruct((half, 3, FF_SHARD), F32)
    bs = jax.ShapeDtypeStruct((half, 1, FF_SHARD), F32)
    dc, dwv, dwg, dbv, dbg = pl.pallas_call(
        body, name="conv_bwd_taps", grid=(half, bsz, nc),
        in_specs=[cur(0), cur(half), prv(0), prv(half), par(3, 0), par(3, half), par(1, 0), par(1, half), cur(0)],
        out_specs=[pl.BlockSpec((2, None, tc, FF_SHARD), lambda j, b, c: (0, j, b * nc + c, 0)),
                   par(3, 0), par(3, 0), par(1, 0), par(1, 0)],
        out_shape=[hs, ws, ws, bs, bs],
        compiler_params=_params(("parallel", "arbitrary", "arbitrary")),
    )(up, up, up, up, cw, cw, cb, cb, dact)
    return (dc.reshape(N_DEV, t, FF_SHARD), jnp.concatenate([dwv, dwg], axis=0),
            jnp.concatenate([dbv, dbg], axis=0))


def conv_bwd_input(dconv, cw, bsz, seq):
    tc = _tile(seq, 1024)
    nc = seq // tc
    hb = tc // CONV_HALO
    nblk = bsz * seq // CONV_HALO

    def body(d_ref, n_ref, w_ref, o_ref):
        c = pl.program_id(2)
        nxt = jnp.where(c < nc - 1, n_ref[...].astype(F32), 0.0)
        d = d_ref[...].astype(F32)
        d1, d2 = _shifts_up(d, nxt)
        w = w_ref[...]
        o_ref[...] = (w[2:3] * d + w[1:2] * d1 + w[0:1] * d2).astype(o_ref.dtype)

    cur = pl.BlockSpec((None, tc, FF_SHARD), lambda j, b, c: (j, b * nc + c, 0))
    return pl.pallas_call(
        body, name="conv_bwd_input", grid=(N_DEV, bsz, nc),
        in_specs=[cur, pl.BlockSpec((None, CONV_HALO, FF_SHARD),
                                    lambda j, b, c: (j, jnp.minimum((b * nc + c + 1) * hb, nblk - 1), 0)),
                  pl.BlockSpec((None, 3, FF_SHARD), lambda j, b, c: (j, 0, 0))],
        out_specs=cur, out_shape=jax.ShapeDtypeStruct(dconv.shape, BF16),
        compiler_params=_params(("parallel", "parallel", "parallel")),
    )(dconv, dconv, cw)


def _my_index():
    return 4 * lax.axis_index("x") + 2 * lax.axis_index("y") + lax.axis_index("c")


def _peer(k):
    return (lax.axis_index("x") ^ ((k >> 2) & 1), lax.axis_index("y") ^ ((k >> 1) & 1),
            lax.axis_index("c") ^ (k & 1))


_HBM = pl.BlockSpec(memory_space=pltpu.HBM)
_SEM = pl.BlockSpec(memory_space=pltpu.SEMAPHORE)
_DATAFLOW = pltpu.SideEffectType.DATAFLOW_SIDE_EFFECTING


def _split_copies(gather, src_ref, land_ref, send_sems, recv_sems, local_sem):
    me = _my_index()

    def part(j):
        return src_ref if gather else src_ref.at[j]

    local = pltpu.make_async_copy(part(me), land_ref.at[me], local_sem)
    sends = [pltpu.make_async_remote_copy(
        src_ref=part(me ^ k), dst_ref=land_ref.at[me], send_sem=send_sems.at[k - 1], recv_sem=recv_sems.at[k - 1],
        device_id=_peer(k), device_id_type=pl.DeviceIdType.MESH) for k in range(1, N_DEV)]
    recvs = [pltpu.make_async_remote_copy(
        src_ref=part(me ^ k), dst_ref=land_ref.at[me ^ k], send_sem=send_sems.at[k - 1], recv_sem=recv_sems.at[k - 1],
        device_id=_peer(k), device_id_type=pl.DeviceIdType.MESH) for k in range(1, N_DEV)]
    return local, sends, recvs


def split_start(name, srcs, gather):
    n = len(srcs)
    lands = [((N_DEV,) + s.shape) if gather else s.shape for s in srcs]

    def body(*refs):
        ins, outs = refs[:2 * n], refs[2 * n:]
        for i in range(n):
            local, sends, _ = _split_copies(gather, ins[i], ins[n + i], *outs[3 * i:3 * i + 3])
            local.start()
            for cp in sends:
                cp.start()
        outs[-1][...] = jnp.zeros_like(outs[-1])

    dma7 = pltpu.SemaphoreType.DMA((N_DEV - 1,))
    out = pl.pallas_call(
        body, name=name,
        out_shape=(dma7, dma7, pltpu.SemaphoreType.DMA(())) * n
                  + tuple(pltpu.HBM(s.shape, s.dtype) for s in srcs)
                  + tuple(pltpu.HBM(shape, s.dtype) for shape, s in zip(lands, srcs))
                  + (jax.ShapeDtypeStruct((8, 128), F32),),
        in_specs=(_HBM,) * (2 * n),
        out_specs=(_SEM,) * (3 * n) + (_HBM,) * (2 * n) + (pl.BlockSpec(memory_space=pltpu.VMEM),),
        input_output_aliases={i: 3 * n + i for i in range(2 * n)},
        compiler_params=pltpu.CompilerParams(has_side_effects=_DATAFLOW),
    )(*[pltpu.with_memory_space_constraint(s, pltpu.HBM) for s in srcs],
      *[pltpu.with_memory_space_constraint(lax.empty(shape, s.dtype), pltpu.HBM) for shape, s in zip(lands, srcs)])
    handles = [tuple(out[3 * i:3 * i + 3]) + (out[3 * n + i], out[4 * n + i]) for i in range(n)]
    return handles, out[-1][0, 0]


def split_wait(name, handles, after, gather):
    send_sems, recv_sems, local_sem, src_thru, land_thru = handles

    def body(src_ref, land_ref, send_sems, recv_sems, local_sem, after_ref, src_dead, got_ref, token):
        local, sends, recvs = _split_copies(gather, src_ref, land_ref, send_sems, recv_sems, local_sem)
        local.wait()
        for cp in recvs:
            cp.wait_send()
            cp.wait_recv()
        token[...] = jnp.zeros_like(token)

    out = pl.pallas_call(
        body, name=name,
        out_shape=(pltpu.HBM(src_thru.shape, src_thru.dtype), pltpu.HBM(land_thru.shape, land_thru.dtype),
                   jax.ShapeDtypeStruct((8, 128), F32)),
        in_specs=(_HBM, _HBM, _SEM, _SEM, _SEM, pl.BlockSpec(memory_space=pl.ANY)),
        out_specs=(_HBM, _HBM, pl.BlockSpec(memory_space=pltpu.VMEM)),
        input_output_aliases={0: 0, 1: 1},
        compiler_params=pltpu.CompilerParams(has_side_effects=_DATAFLOW),
    )(src_thru, land_thru, send_sems, recv_sems, local_sem, after)
    return out[1], out[2][0, 0]


def sum_parts(name, r):
    _, rows, cols = r.shape

    def body(r_ref, o_ref):
        acc = r_ref[0].astype(F32)
        for s in range(1, N_DEV):
            acc = acc + r_ref[s].astype(F32)
        o_ref[...] = acc

    return pl.pallas_call(body, name=name, out_shape=jax.ShapeDtypeStruct((rows, cols), F32),
                          compiler_params=_params())(r)


def adamw(name, w, m, v, parts=None, g=None, layer=0, into=None, order=None):
    _, rows, cols = w.shape
    br = _tile(rows, 256, 16)
    c1 = 1.0 / (1.0 - ADAM_B1 ** ADAM_STEP)
    c2 = 1.0 / (1.0 - ADAM_B2 ** ADAM_STEP)

    def body(g_ref, w_ref, m_ref, v_ref, *rest):
        og_ref, od_ref, om_ref, ov_ref = rest[-4:]
        if parts is None:
            gs = g_ref[...]
        else:
            gs = g_ref[0].astype(F32)
            for s in range(1, N_DEV):
                gs = gs + g_ref[s].astype(F32)
        mn = ADAM_B1 * m_ref[...] + (1.0 - ADAM_B1) * gs
        vn = ADAM_B2 * v_ref[...] + (1.0 - ADAM_B2) * (gs * gs)
        og_ref[...] = gs
        om_ref[...] = mn
        ov_ref[...] = vn
        od_ref[...] = -ADAM_LR * ((mn * c1) / (jnp.sqrt(vn * c2) + ADAM_EPS) + ADAM_WD * w_ref[...])

    blk = pl.BlockSpec((None, br, cols), lambda i: (layer, i, 0))
    if parts is None:
        gspec = pl.BlockSpec((br, cols), lambda i: (i, 0))
    else:
        gspec = pl.BlockSpec((N_DEV, br, cols), lambda i: (0, i, 0))
    earlier = [] if into is None else list(into)
    behind = [] if order is None else [order]
    return pl.pallas_call(
        body, name=name, grid=(rows // br,),
        in_specs=[gspec, blk, blk, blk] + [pl.BlockSpec(memory_space=pl.ANY)] * len(earlier)
                 + [pl.BlockSpec((1, 128), lambda i: (0, 0))] * len(behind),
        out_specs=[blk] * 4, out_shape=[jax.ShapeDtypeStruct(w.shape, F32)] * 4,
        input_output_aliases={4 + k: k for k in range(len(earlier))},
        compiler_params=_params(("parallel",)),
    )(g if parts is None else parts, w, m, v, *earlier, *behind)


SMALL = ("norm_mix", "norm_xattn", "norm_ffn", "norm_mem", "norm_final", "pool_w", "pool_scale",
         "ssm_lam_re", "ssm_lam_im", "ssm_log_dt", "ssm_b_re", "ssm_b_im", "ssm_c_re", "ssm_c_im",
         "ffn_conv_b", "ssm_d", "ffn_conv_w")
SMALL_SHARDED = {"ssm_d": 1, "ffn_conv_w": 2}
BIG = ("ab_w_in", "ab_w_out", "ssm_w_in", "ssm_w_glu", "xa_w_q", "xa_w_kv", "xa_w_o", "ffn_w_up", "ffn_w_down")
WEIGHTS = ("norm_mix", "norm_xattn", "norm_ffn", "norm_mem", "norm_final", "ab_w_in", "pool_w", "pool_scale",
           "ab_w_out", "ssm_w_in", "ssm_lam_re", "ssm_lam_im", "ssm_log_dt", "ssm_b_re", "ssm_b_im", "ssm_c_re",
           "ssm_c_im", "ssm_d", "ssm_w_glu", "xa_w_q", "xa_w_kv", "xa_w_o", "ffn_w_up", "ffn_conv_w", "ffn_conv_b",
           "ffn_w_down")


def _rows8(g):
    return g.reshape(N_DEV, g.size // (N_DEV * D_MODEL), D_MODEL)


def _square(a):
    return a.reshape(D_MODEL, D_MODEL)


_LAYOUT = {"ab_w_out": _square, "ssm_w_in": _square, "xa_w_q": _square, "xa_w_o": _square,
           "ffn_w_down": lambda a: a.reshape(N_DEV // 2, FF_SHARD, D_MODEL)}
GATHER_ORDER = (("ab_w_in", 0), ("ffn_conv_w", None), ("ssm_d", None), ("ab_w_out", 0), ("xa_w_q", 0),
                ("xa_w_kv", 0), ("xa_w_o", 0), ("ffn_w_up", 0), ("ffn_w_down", 0), ("ffn_w_up", 1),
                ("ffn_w_down", 1), ("ssm_w_in", 0), ("ssm_w_glu", 0), ("xa_w_q", 1), ("xa_w_kv", 1), ("xa_w_o", 1))
GATHER_AHEAD = 7
GATHER_BATCHES = (3, 8, 16)


class _Step:
    def __init__(self, master, small):
        self.master, self.small = master, small
        self.pending, self.gathers, self.weights, self.sent, self.queued = [], {}, {}, [], []

    def follow(self, v):
        for z in self.pending:
            v = v + z
        self.pending = []
        return v

    def start_gathers(self, upto, zero):
        upto = min(end for end in GATHER_BATCHES if end >= min(upto, len(GATHER_ORDER)))
        todo = GATHER_ORDER[len(self.gathers):upto]
        if not todo:
            return
        shards = []
        for n, l in todo:
            if l is None:
                shards.append(self.master[n] + zero)
            else:
                shards.append((self.master[n][l] + zero).astype(MXU_DTYPE))
        handles, z = split_start(f"ags_{len(self.gathers)}", shards, gather=True)
        self.gathers.update(zip(todo, handles))
        self.pending.append(z)

    def weight(self, n, l, after):
        if (n, l) not in self.weights:
            full, z = split_wait(f"agw_{n}{'' if l is None else l}", self.gathers[(n, l)], after, gather=True)
            self.weights[(n, l)] = _LAYOUT.get(n, lambda a: a)(full)
            self.start_gathers(GATHER_ORDER.index((n, l)) + 1 + GATHER_AHEAD, z)
        return self.weights[(n, l)]

    def send_grad(self, n, l, part, flush=True):
        self.queued.append((n, l, part))
        if flush:
            handles, z = split_start(f"xs_{n}{l}", [p for _, _, p in self.queued], gather=False)
            self.sent += [(qn, ql, h) for (qn, ql, _), h in zip(self.queued, handles)]
            self.queued = []
            self.pending.append(z)


def _layer_tail(st, l, x_in, hq, mem_n, acts, next_gain=None):
    bsz, seq = acts["bsz"], acts["seq"]
    p = st.small
    q = mm_nn(f"xa_q{l}", hq, st.weight("xa_w_q", l, x_in))
    kv = mm_nn_bs(f"xa_kv{l}", mem_n, st.weight("xa_w_kv", l, x_in))
    o = xattn_fwd(q, kv, bsz, seq)
    x_mid, hf = mm_nn(f"xa_o{l}", o, st.weight("xa_w_o", l, o), res=x_in, out_dtype=F32,
                      norm_gain=st.follow(p["norm_ffn"][l]))
    up = mm_nn_bs(f"ffn_up{l}", hf, st.weight("ffn_w_up", l, x_mid), stacked_out=True)
    conv_w = st.weight("ffn_conv_w", None, x_mid)[:, l]
    act = conv_fwd(up, conv_w, p["ffn_conv_b"][l], bsz, seq)
    w_down = st.weight("ffn_w_down", l, act)
    if next_gain is None:
        x_out, h_next = mm_as_nn(f"ffn_down{l}", act, w_down, res=x_mid), None
    else:
        x_out, h_next = mm_as_nn(f"ffn_down{l}", act, w_down, res=x_mid, norm_gain=st.follow(next_gain))
    acts[l].update(x_in=x_in, hq=hq, q=q, kv=kv, o=o, x_mid=x_mid, hf=hf, up=up, act=act)
    return x_out, h_next


def _layer_tail_bwd(st, l, dx, mem_n, acts, grads):
    a = acts[l]
    bsz, seq = acts["bsz"], acts["seq"]
    p = st.small
    dact = mm_nt_os(f"d_act{l}", dx, st.weight("ffn_w_down", l, dx))
    st.send_grad("ffn_w_down", l, _rows8(mm_tn(f"g_ffn_down{l}", a["act"], dx, a_stacked=True)), flush=False)
    conv_w = st.weight("ffn_conv_w", None, dx)[:, l]
    dconv, dcw, dcb = conv_bwd_taps(a["up"], conv_w, p["ffn_conv_b"][l], dact, bsz, seq)
    grads["ffn_conv_w"][l] = dcw
    grads["ffn_conv_b"][l] = dcb
    dup = conv_bwd_input(dconv, conv_w, bsz, seq)
    dx_mid, grads["norm_ffn"][l] = mm_nt_bs(f"d_hf{l}", dup, st.weight("ffn_w_up", l, dx), dc_stacked=True,
                                            rms=(a["x_mid"], st.follow(p["norm_ffn"][l]), dx))
    st.send_grad("ffn_w_up", l, mm_tn(f"g_ffn_up{l}", a["hf"], dup, dc_stacked=True))
    do = mm_nt(f"d_o{l}", dx_mid, st.weight("xa_w_o", l, dx))
    st.send_grad("xa_w_o", l, _rows8(mm_tn(f"g_xa_o{l}", a["o"], dx_mid)), flush=False)
    dq, dk, dv = xattn_bwd(a["q"], a["kv"], do, bsz, seq)
    dkv = jnp.concatenate([dk, dv], axis=1).astype(BF16)
    dx_in, grads["norm_xattn"][l] = mm_nt(f"d_hq{l}", dq, st.weight("xa_w_q", l, dx),
                                          rms=(a["x_in"], st.follow(p["norm_xattn"][l]), dx_mid))
    st.send_grad("xa_w_q", l, _rows8(mm_tn(f"g_xa_q{l}", a["hq"], dq)), flush=False)
    dmem_n = mm_nt_bs(f"d_memn{l}", dkv, st.weight("xa_w_kv", l, dx), out_dtype=F32)
    st.send_grad("xa_w_kv", l, mm_tn(f"g_xa_kv{l}", mem_n, dkv, dc_cols=2 * D_MODEL // N_DEV))
    return dx_in, dmem_n


def kernel(x, mem, norm_mix, norm_xattn, norm_ffn, norm_mem, norm_final, ab_w_in, pool_w, pool_scale, ab_w_out, ssm_w_in, ssm_lam_re, ssm_lam_im, ssm_log_dt, ssm_b_re, ssm_b_im, ssm_c_re, ssm_c_im, ssm_d, ssm_w_glu, xa_w_q, xa_w_kv, xa_w_o, ffn_w_up, ffn_conv_w, ffn_conv_b, ffn_w_down, loss_target, m_norm_mix, m_norm_xattn, m_norm_ffn, m_norm_mem, m_norm_final, m_ab_w_in, m_pool_w, m_pool_scale, m_ab_w_out, m_ssm_w_in, m_ssm_lam_re, m_ssm_lam_im, m_ssm_log_dt, m_ssm_b_re, m_ssm_b_im, m_ssm_c_re, m_ssm_c_im, m_ssm_d, m_ssm_w_glu, m_xa_w_q, m_xa_w_kv, m_xa_w_o, m_ffn_w_up, m_ffn_conv_w, m_ffn_conv_b, m_ffn_w_down, v_norm_mix, v_norm_xattn, v_norm_ffn, v_norm_mem, v_norm_final, v_ab_w_in, v_pool_w, v_pool_scale, v_ab_w_out, v_ssm_w_in, v_ssm_lam_re, v_ssm_lam_im, v_ssm_log_dt, v_ssm_b_re, v_ssm_b_im, v_ssm_c_re, v_ssm_c_im, v_ssm_d, v_ssm_w_glu, v_xa_w_q, v_xa_w_kv, v_xa_w_o, v_ffn_w_up, v_ffn_conv_w, v_ffn_conv_b, v_ffn_w_down):
    given = dict(locals())
    master = {n: given[n] for n in WEIGHTS}
    mom1 = {n: given["m_" + n] for n in WEIGHTS}
    mom2 = {n: given["v_" + n] for n in WEIGHTS}
    bsz, seq, d = x.shape
    t = bsz * seq
    me = _my_index()

    st = _Step(master, {"norm_xattn": norm_xattn, "norm_ffn": norm_ffn,
                        "ffn_conv_b": [ffn_conv_b[l].reshape(N_DEV, 1, FF_SHARD) for l in range(2)]})
    st.start_gathers(1, 0.0)
    zero = st.follow(jnp.zeros((), F32))

    acts = {"bsz": bsz, "seq": seq, 0: {}, 1: {}}
    x0 = x.reshape(t, d)
    mem2 = mem.reshape(bsz * MEM_LEN, d)
    mem_n = rms_fwd("rms_mem", mem2, norm_mem + zero)
    pscale = pool_scale.reshape(1, SB_WIDTH)

    h0 = rms_fwd("rms_mix0", x0, norm_mix[0] + zero)
    w_in = st.weight("ab_w_in", 0, h0)
    proj = mm_nn_bs("ab_in", h0, w_in, out_dtype=F32)
    a_out, rsum = sb_attn_fwd(proj, st.follow(jnp.zeros((1, 128), F32)), bsz, seq)
    p_out = pool_fwd(proj, pool_w[0], pscale, bsz, seq)
    w_out = st.weight("ab_w_out", 0, a_out)
    x1 = mm_nn("ab_out_a", a_out, w_out, res=x0, out_dtype=F32)
    x1, hq0 = mm_nn("ab_out_p", p_out, w_out, res=x1, koff=SB_WIDTH, out_dtype=F32,
                    norm_gain=st.follow(norm_xattn[0]))
    x3, h1 = _layer_tail(st, 0, x1, hq0, mem_n, acts, next_gain=norm_mix[1])

    b_re2 = ssm_b_re.reshape(64, 1024)
    b_im2 = ssm_b_im.reshape(64, 1024)
    log_dt = ssm_log_dt.reshape(64, 1)
    lb_re, lb_im, bb_re2, bb_im2 = ssm_prep(ssm_lam_re[0], ssm_lam_im[0], log_dt, b_re2, b_im2)
    wt = _ssm_in_weights(bb_re2, bb_im2)
    ct = _ssm_out_weights(ssm_c_re[0], ssm_c_im[0])
    a_re = lb_re.reshape(1, SSM_STATES)
    a_im = lb_im.reshape(1, SSM_STATES)
    u = mm_nn("ssm_in", h1, st.weight("ssm_w_in", 0, x3), out_dtype=F32)
    dskip = st.weight("ssm_d", None, x3).reshape(1, D_MODEL)
    y, gl, h_re, h_im = ssm_fwd(u, wt, ct, a_re, a_im, dskip, bsz, seq)
    glu = mm_nn_bs("ssm_glu", gl, st.weight("ssm_w_glu", 0, gl), out_dtype=F32)
    x4, hq1 = glu_fwd(glu, x3, st.follow(norm_xattn[1]))
    x6, _ = _layer_tail(st, 1, x4, hq1, mem_n, acts)

    loss_row, dx, g_norm_final = loss_head(x6, norm_final, loss_target.reshape(t, d))
    loss = lax.psum(loss_row[0, 0], MESH_AXES)

    grads = {n: [None, None] for n in ("ffn_conv_w", "ffn_conv_b", "norm_ffn", "norm_xattn", "norm_mix")}
    dx4, dmem_1 = _layer_tail_bwd(st, 1, dx, mem_n, acts, grads)
    dglu = glu_bwd(glu, dx4)
    dgl = mm_nt_bs("d_gl", dglu, st.weight("ssm_w_glu", 0, dx))
    st.send_grad("ssm_w_glu", 0, mm_tn("g_ssm_glu", gl, dglu, dc_cols=2 * D_MODEL // N_DEV), flush=False)
    du, dwt, dct, g_dskip, da_re, da_im = ssm_bwd(dgl, y, u, h_re, h_im, wt, ct, a_re, a_im, dskip, bsz, seq)
    dbb_re, dbb_im = _ssm_in_weights_bwd(dwt)
    g_c_re, g_c_im = _ssm_out_weights_bwd(dct)
    g_lam_re, g_lam_im, g_log_dt, g_b_re, g_b_im = ssm_prep_bwd(
        ssm_lam_re[0], ssm_lam_im[0], log_dt, b_re2, b_im2, da_re.reshape(64, 64), da_im.reshape(64, 64),
        dbb_re, dbb_im)
    dx3, grads["norm_mix"][1] = mm_nt("d_h1", du, st.weight("ssm_w_in", 0, dx),
                                      rms=(x3, st.follow(norm_mix[1]), dx4))
    st.send_grad("ssm_w_in", 0, _rows8(mm_tn("g_ssm_in", h1, du)))

    dx1, dmem_0 = _layer_tail_bwd(st, 0, dx3, mem_n, acts, grads)
    dcat = mm_nt("d_cat", dx1, st.weight("ab_w_out", 0, dx))
    st.send_grad("ab_w_out", 0, _rows8(jnp.concatenate(
        [mm_tn("g_ab_out_a", a_out, dx1), mm_tn("g_ab_out_p", p_out, dx1)], axis=0)), flush=False)
    dq, dk, dv = sb_attn_bwd(proj, rsum, dcat, bsz, seq)
    dpu, g_pool_w, g_pool_scale = pool_bwd(proj, pool_w[0], st.follow(pscale), dcat, bsz, seq)
    dproj = jnp.concatenate([dq, dk, dv, dpu], axis=1).astype(BF16)
    st.send_grad("ab_w_in", 0, mm_tn("g_ab_in", h0, dproj, dc_cols=2 * D_MODEL // N_DEV))
    dx0, grads["norm_mix"][0] = mm_nt_bs("d_h0", dproj, st.weight("ab_w_in", 0, dx),
                                         rms=(x0, st.follow(norm_mix[0]), dx1))
    _, g_norm_mem = rms_bwd("rms_mem_bwd", mem2, norm_mem, dmem_0 + dmem_1, need_dx=False)

    small_g = {
        "norm_mix": jnp.stack([g[0] for g in grads["norm_mix"]]),
        "norm_xattn": jnp.stack([g[0] for g in grads["norm_xattn"]]),
        "norm_ffn": jnp.stack([g[0] for g in grads["norm_ffn"]]),
        "norm_mem": g_norm_mem[0], "norm_final": g_norm_final[0],
        "pool_w": g_pool_w[None], "pool_scale": g_pool_scale,
        "ssm_lam_re": g_lam_re[None], "ssm_lam_im": g_lam_im[None], "ssm_log_dt": g_log_dt.reshape(1, 64),
        "ssm_b_re": g_b_re.reshape(1, 64, 64, 16), "ssm_b_im": g_b_im.reshape(1, 64, 64, 16),
        "ssm_c_re": g_c_re[None], "ssm_c_im": g_c_im[None],
        "ffn_conv_b": jnp.stack([g.reshape(2 * D_FF) for g in grads["ffn_conv_b"]]),
        "ssm_d": g_dskip,
        "ffn_conv_w": jnp.stack([g.transpose(1, 0, 2).reshape(3, 2 * D_FF) for g in grads["ffn_conv_w"]]),
    }
    sizes = [int(small_g[n].size) for n in SMALL]
    total = sum(sizes)
    rows8 = -(-total // (N_DEV * 128 * 8)) * 8
    flat = jnp.concatenate([small_g[n].reshape(-1).astype(F32) for n in SMALL]
                           + [jnp.zeros((N_DEV * rows8 * 128 - total,), F32)])
    (in_flight,), z = split_start("xs_small", [flat.reshape(N_DEV, rows8, 128)], gather=False)
    st.pending.append(z)
    stepped, last = {}, dx0
    for i, (n, l, handles) in enumerate(st.sent):
        if i == len(st.sent) // 2:
            recv, _ = split_wait("xw_small", in_flight, last, gather=False)
            (in_flight,), z = split_start("ags_small", [sum_parts("sum_small", recv)], gather=True)
            st.pending.append(z)
        recv, _ = split_wait(f"xw_{n}{l}", handles, dx0, gather=False)
        shape3 = (master[n].shape[0],) + recv.shape[1:]
        stepped[n] = adamw(f"adamw_{n}{l}", master[n].reshape(shape3), mom1[n].reshape(shape3),
                           mom2[n].reshape(shape3), parts=recv, layer=l, into=stepped.get(n),
                           order=st.follow(jnp.zeros((1, 128), F32)))
        last = stepped[n][0]
    out_g, out_d, out_m, out_v = ({n: stepped[n][k].reshape(master[n].shape) for n in BIG} for k in range(4))
    summed = split_wait("agw_small", in_flight, last, gather=True)[0].reshape(-1)

    def local_part(name, a):
        ax = SMALL_SHARDED.get(name)
        if ax is None:
            return a
        n_loc = a.shape[ax] // N_DEV
        return lax.dynamic_slice_in_dim(a, me * n_loc, n_loc, axis=ax)

    off = 0
    for n, sz in zip(SMALL, sizes):
        g_n = local_part(n, summed[off:off + sz].reshape(small_g[n].shape))
        off += sz
        cols = g_n.shape[-1] if g_n.shape[-1] >= 128 or g_n.ndim < 3 else g_n.shape[-1] * g_n.shape[-2]
        shape3 = (1, g_n.size // cols, cols)
        res = adamw("adamw_" + n, master[n].reshape(shape3), mom1[n].reshape(shape3), mom2[n].reshape(shape3),
                    g=g_n.reshape(shape3[1:]))
        for dst, r in zip((out_g, out_d, out_m, out_v), res):
            dst[n] = r.reshape(master[n].shape)

    return (loss, dx0.reshape(bsz, seq, d), *[out_g[n] for n in WEIGHTS], *[out_d[n] for n in WEIGHTS],
            *[out_m[n] for n in WEIGHTS], *[out_v[n] for n in WEIGHTS])
```

```python
import math

import jax
import jax.numpy as jnp
from jax import lax
from jax.experimental import pallas as pl
from jax.experimental.pallas import tpu as pltpu

F32 = jnp.float32
BF16 = jnp.bfloat16
MXU_DTYPE = jnp.bfloat16
N_DEV = 8
MESH_AXES = ("x", "y", "c")

D_MODEL = 1024
SB_HEAD_DIM = 64
SB_WIDTH = 512
SB_BLOCK = 256
POOL_WINDOWS = (2, 4, 8, 16)
POOL_GROUP = 128
POOL_HALO = 16
SSM_TILES = 8
SSM_TILE_STATES = 512
SSM_STATES = 4096
SSM_LANES = 1024
MEM_LEN = 256
XA_HEADS = 4
XA_HEAD_DIM = 256
D_FF = 2816
FF_SHARD = 704
EPS = 1e-6
ADAM_LR = 0.001
ADAM_B1 = 0.9
ADAM_B2 = 0.999
ADAM_EPS = 1e-08
ADAM_WD = 0.01
ADAM_STEP = 10
VMEM_LIMIT = 56 * 1024 * 1024

_NN = (((1,), (0,)), ((), ()))
_NT = (((1,), (1,)), ((), ()))
_TN = (((0,), (0,)), ((), ()))


def _params(sem=None):
    if sem is None:
        return pltpu.CompilerParams(vmem_limit_bytes=VMEM_LIMIT)
    return pltpu.CompilerParams(dimension_semantics=sem, vmem_limit_bytes=VMEM_LIMIT)


def _tile(n, pref, mult=8):
    if n <= pref:
        return n
    for t in range(pref, 0, -1):
        if n % t == 0 and t % mult == 0:
            return t
    return n


def _dot(a, b, dims):
    return lax.dot_general(a.astype(MXU_DTYPE), b.astype(MXU_DTYPE), dims, preferred_element_type=F32)


def _dot_exact01(x, m01, dims=_NN):
    x1 = x.astype(BF16)
    r1 = x - x1.astype(F32)
    x2 = r1.astype(BF16)
    x3 = (r1 - x2.astype(F32)).astype(BF16)
    m = m01.astype(BF16)
    out = lax.dot_general(x1, m, dims, preferred_element_type=F32)
    out = out + lax.dot_general(x2, m, dims, preferred_element_type=F32)
    return out + lax.dot_general(x3, m, dims, preferred_element_type=F32)


def _mm(name, a, b, dims, grid, a_spec, b_spec, o_spec, out_shape, out_dtype, acc_shape, res=None, r_spec=None,
        group=1, n=None, a_sel="full", b_sel="full", o_sel="full", norm_gain=None, rms=None):
    nk = grid[2]
    if out_dtype is None:
        out_dtype = BF16
    n_out = out_shape[-1]
    vec = pl.BlockSpec((1, n_out), lambda i, j, kk: (0, 0))

    def at(sel, s):
        if sel == "lead":
            return (s,)
        if sel == "lanes":
            return (slice(None), slice(s * n, (s + 1) * n))
        return (Ellipsis,)

    extra = [] if res is None else [(res, r_spec)]
    if norm_gain is not None:
        extra.append((norm_gain.reshape(1, n_out), vec))
    if rms is not None:
        extra += [(rms[0], o_spec), (rms[1].reshape(1, n_out), vec), (rms[2], o_spec)]
    n_in = 2 + len(extra)
    if rms is not None:
        out_specs = [o_spec, vec]
        out_shapes = [jax.ShapeDtypeStruct(out_shape, F32), jax.ShapeDtypeStruct((1, n_out), F32)]
    elif norm_gain is not None:
        out_specs = [o_spec, o_spec]
        out_shapes = [jax.ShapeDtypeStruct(out_shape, out_dtype), jax.ShapeDtypeStruct(out_shape, BF16)]
    else:
        out_specs, out_shapes = o_spec, jax.ShapeDtypeStruct(out_shape, out_dtype)

    def body(*refs):
        a_ref, b_ref = refs[0], refs[1]
        ins = list(refs[2:n_in])
        r_ref = ins.pop(0) if res is not None else None
        outs = refs[n_in:]
        o_ref = outs[0]
        acc = refs[-1] if nk > 1 else None
        k = pl.program_id(2)

        def finish(val):
            if r_ref is not None:
                val = val + r_ref[...].astype(F32)
            if rms is not None:
                x_ref, g_ref, d_ref = ins
                xf = x_ref[...]
                r = lax.rsqrt(jnp.mean(xf * xf, axis=-1, keepdims=True) + EPS)
                xh = xf * r
                part = jnp.sum(val * xh, axis=0, keepdims=True)
                first = pl.program_id(0) == 0

                @pl.when(first)
                def _():
                    outs[1][...] = part

                @pl.when(jnp.logical_not(first))
                def _():
                    outs[1][...] += part

                dxh = val * g_ref[...]
                o_ref[...] = d_ref[...] + r * (dxh - xh * jnp.mean(dxh * xh, axis=-1, keepdims=True))
                return
            o_ref[...] = val.astype(out_dtype)
            if norm_gain is not None:
                r = lax.rsqrt(jnp.mean(val * val, axis=-1, keepdims=True) + EPS)
                outs[1][...] = (val * r * ins[0][...]).astype(BF16)

        def emit(s, val):
            if nk == 1:
                if o_sel == "full":
                    finish(val)
                else:
                    o_ref[at(o_sel, s)] = val.astype(out_dtype)
                return

            @pl.when(k == 0)
            def _():
                acc[at(o_sel, s)] = val

            @pl.when(k > 0)
            def _():
                acc[at(o_sel, s)] += val

        total = None
        if a_sel == "full" and b_sel == "lanes":
            wide = _dot(a_ref[...], b_ref[...], dims)
            for s in range(group):
                emit(s, wide[:, s * n:(s + 1) * n])
        else:
            for s in range(group):
                val = _dot(a_ref[at(a_sel, s)], b_ref[at(b_sel, s)], dims)
                if o_sel == "full":
                    total = val if total is None else total + val
                else:
                    emit(s, val)
        if o_sel == "full":
            emit(0, total)
        if nk > 1:
            @pl.when(k == nk - 1)
            def _():
                if o_sel == "full":
                    finish(acc[...])
                else:
                    o_ref[...] = acc[...].astype(out_dtype)

    rows_sem = "arbitrary" if rms is not None else "parallel"
    return pl.pallas_call(
        body, name=name, grid=grid, in_specs=[a_spec, b_spec] + [s for _, s in extra], out_specs=out_specs,
        out_shape=out_shapes, scratch_shapes=[pltpu.VMEM(acc_shape, F32)] if nk > 1 else [],
        compiler_params=_params((rows_sem, rows_sem, "arbitrary")),
    )(a, b, *[x for x, _ in extra])


def _row_tile(m, epi):
    return _tile(m, 512 if epi.get("rms") is not None else 1024)


def mm_nn(name, a, b, res=None, koff=0, out_dtype=None, **epi):
    m, k = a.shape
    n = b.shape[1]
    tm, tn, tk = _row_tile(m, epi), _tile(n, 1024, 128), _tile(k, 1024, 128)
    kb = koff // tk
    spec = pl.BlockSpec((tm, tn), lambda i, j, kk: (i, j))
    return _mm(name, a, b, _NN, (m // tm, n // tn, k // tk),
               pl.BlockSpec((tm, tk), lambda i, j, kk: (i, kk)),
               pl.BlockSpec((tk, tn), lambda i, j, kk: (kk + kb, j)),
               spec, (m, n), out_dtype, (tm, tn), res, spec, **epi)


def mm_nn_bs(name, a, bs, stacked_out=False, out_dtype=None):
    m, k = a.shape
    s, _, n = bs.shape
    tm, tk = _tile(m, 2048 if stacked_out else 1024), _tile(k, 1024, 128)
    a_spec = pl.BlockSpec((tm, tk), lambda i, j, kk: (i, kk))
    if stacked_out:
        return _mm(name, a, bs, _NN, (m // tm, s, k // tk), a_spec,
                   pl.BlockSpec((None, tk, n), lambda i, j, kk: (j, kk, 0)),
                   pl.BlockSpec((None, tm, n), lambda i, j, kk: (j, i, 0)), (s, m, n), out_dtype, (tm, n))
    g = _tile(s, max(1, 1024 // n), 1)
    return _mm(name, a, bs, _NN, (m // tm, s // g, k // tk), a_spec,
               pl.BlockSpec((g, tk, n), lambda i, j, kk: (j, kk, 0)),
               pl.BlockSpec((tm, g * n), lambda i, j, kk: (i, j)), (m, s * n), out_dtype, (tm, g * n),
               group=g, n=n, b_sel="lead", o_sel="lanes")


def mm_as_nn(name, a_st, b3, res, out_dtype=F32, **epi):
    s, m, kp = a_st.shape
    n = b3.shape[2]
    tm, tn = _row_tile(m, epi), _tile(n, 1024, 128)
    spec = pl.BlockSpec((tm, tn), lambda i, j, kk: (i, j))
    g = _tile(s, 2, 1)
    return _mm(name, a_st, b3, _NN, (m // tm, n // tn, s // g),
               pl.BlockSpec((g, tm, kp), lambda i, j, kk: (kk, i, 0)),
               pl.BlockSpec((g, kp, tn), lambda i, j, kk: (kk, 0, j)),
               spec, (m, n), out_dtype, (tm, tn), res, spec, group=g, a_sel="lead", b_sel="lead", **epi)


def mm_nt(name, dc, b, out_dtype=None, **epi):
    m, n = dc.shape
    k = b.shape[0]
    tm, tko, tnr = _tile(m, 1024), _tile(k, 1024, 128), _tile(n, 1024, 128)
    return _mm(name, dc, b, _NT, (m // tm, k // tko, n // tnr),
               pl.BlockSpec((tm, tnr), lambda i, j, kk: (i, kk)),
               pl.BlockSpec((tko, tnr), lambda i, j, kk: (j, kk)),
               pl.BlockSpec((tm, tko), lambda i, j, kk: (i, j)), (m, k), out_dtype, (tm, tko), **epi)


def mm_nt_bs(name, dc, bs, dc_stacked=False, out_dtype=None, **epi):
    s, k, n = bs.shape
    m = dc.shape[1] if dc_stacked else dc.shape[0]
    tm, tko = (_tile(m, 1024) if dc_stacked else _row_tile(m, epi)), _tile(k, 1024, 128)
    o_spec = pl.BlockSpec((tm, tko), lambda i, j, kk: (i, j))
    if dc_stacked:
        g = _tile(s, 2, 1)
        return _mm(name, dc, bs, _NT, (m // tm, k // tko, s // g),
                   pl.BlockSpec((g, tm, n), lambda i, j, kk: (kk, i, 0)),
                   pl.BlockSpec((g, tko, n), lambda i, j, kk: (kk, j, 0)), o_spec, (m, k), out_dtype, (tm, tko),
                   group=g, a_sel="lead", b_sel="lead", **epi)
    g = _tile(s, max(1, 2048 // n), 1)
    return _mm(name, dc, bs, _NT, (m // tm, k // tko, s // g),
               pl.BlockSpec((tm, g * n), lambda i, j, kk: (i, kk)),
               pl.BlockSpec((g, tko, n), lambda i, j, kk: (kk, j, 0)), o_spec, (m, k), out_dtype, (tm, tko),
               group=g, n=n, a_sel="lanes", b_sel="lead", **epi)


def mm_nt_os(name, dc, b3, out_dtype=None):
    m, n = dc.shape
    s, kp, _ = b3.shape
    tm, tnr = _tile(m, 2048), _tile(n, 1024, 128)
    return _mm(name, dc, b3, _NT, (m // tm, s, n // tnr),
               pl.BlockSpec((tm, tnr), lambda i, j, kk: (i, kk)),
               pl.BlockSpec((None, kp, tnr), lambda i, j, kk: (j, 0, kk)),
               pl.BlockSpec((None, tm, kp), lambda i, j, kk: (j, i, 0)), (s, m, kp), out_dtype, (tm, kp))


def mm_tn(name, a, dc, a_stacked=False, dc_cols=None, dc_stacked=False, out_dtype=None):
    if a_stacked:
        s, m, kp = a.shape
        n = dc.shape[1]
        tno, tmr = _tile(n, 1024, 128), _tile(m, 2048)
        return _mm(name, a, dc, _TN, (s, n // tno, m // tmr),
                   pl.BlockSpec((None, tmr, kp), lambda i, j, kk: (i, kk, 0)),
                   pl.BlockSpec((tmr, tno), lambda i, j, kk: (kk, j)),
                   pl.BlockSpec((None, kp, tno), lambda i, j, kk: (i, 0, j)), (s, kp, n), out_dtype, (kp, tno))
    m, k = a.shape
    tko, tmr = _tile(k, 1024, 128), _tile(m, 2048)
    a_spec = pl.BlockSpec((tmr, tko), lambda i, j, kk: (kk, i))
    if dc_stacked:
        s, _, n = dc.shape
        return _mm(name, a, dc, _TN, (k // tko, s, m // tmr), a_spec,
                   pl.BlockSpec((None, tmr, n), lambda i, j, kk: (j, kk, 0)),
                   pl.BlockSpec((None, tko, n), lambda i, j, kk: (j, i, 0)), (s, k, n), out_dtype, (tko, n))
    if dc_cols is not None:
        n = dc_cols
        s = dc.shape[1] // n
        g = _tile(s, max(1, 1024 // n), 1)
        return _mm(name, a, dc, _TN, (k // tko, s // g, m // tmr), a_spec,
                   pl.BlockSpec((tmr, g * n), lambda i, j, kk: (kk, j)),
                   pl.BlockSpec((g, tko, n), lambda i, j, kk: (j, i, 0)), (s, k, n), out_dtype, (g, tko, n),
                   group=g, n=n, b_sel="lanes", o_sel="lead")
    n = dc.shape[1]
    tno = _tile(n, 1024, 128)
    return _mm(name, a, dc, _TN, (k // tko, n // tno, m // tmr), a_spec,
               pl.BlockSpec((tmr, tno), lambda i, j, kk: (kk, j)),
               pl.BlockSpec((tko, tno), lambda i, j, kk: (i, j)), (k, n), out_dtype, (tko, tno))


def rms_fwd(name, x, g):
    t, d = x.shape
    tr = _tile(t, 512)

    def body(x_ref, g_ref, o_ref):
        xf = x_ref[...]
        r = lax.rsqrt(jnp.mean(xf * xf, axis=-1, keepdims=True) + EPS)
        o_ref[...] = (xf * r * g_ref[...]).astype(o_ref.dtype)

    return pl.pallas_call(
        body, name=name, grid=(t // tr,),
        in_specs=[pl.BlockSpec((tr, d), lambda i: (i, 0)), pl.BlockSpec((1, d), lambda i: (0, 0))],
        out_specs=pl.BlockSpec((tr, d), lambda i: (i, 0)),
        out_shape=jax.ShapeDtypeStruct((t, d), BF16), compiler_params=_params(("parallel",)),
    )(x, g.reshape(1, d))


def rms_bwd(name, x, g, dh, dres=None, need_dx=True):
    t, d = x.shape
    tr = _tile(t, 512)

    def body(*refs):
        refs = list(refs)
        x_ref, g_ref, dh_ref = refs[:3]
        r_ref = refs[3] if dres is not None else None
        outs = refs[4:] if dres is not None else refs[3:]
        dx_ref, dg_ref = (outs[0], outs[1]) if need_dx else (None, outs[0])
        i = pl.program_id(0)

        @pl.when(i == 0)
        def _():
            dg_ref[...] = jnp.zeros_like(dg_ref)

        xf = x_ref[...]
        dhf = dh_ref[...].astype(F32)
        r = lax.rsqrt(jnp.mean(xf * xf, axis=-1, keepdims=True) + EPS)
        xh = xf * r
        dg_ref[...] += jnp.sum(dhf * xh, axis=0, keepdims=True)
        if need_dx:
            dxh = dhf * g_ref[...]
            dx = r * (dxh - xh * jnp.mean(dxh * xh, axis=-1, keepdims=True))
            if r_ref is not None:
                dx = dx + r_ref[...]
            dx_ref[...] = dx

    row = pl.BlockSpec((tr, d), lambda i: (i, 0))
    vec = pl.BlockSpec((1, d), lambda i: (0, 0))
    in_specs = [row, vec, row] + ([row] if dres is not None else [])
    args = (x, g.reshape(1, d), dh) + ((dres,) if dres is not None else ())
    out_specs = ([row] if need_dx else []) + [vec]
    out_shape = ([jax.ShapeDtypeStruct((t, d), F32)] if need_dx else []) + [jax.ShapeDtypeStruct((1, d), F32)]
    res = pl.pallas_call(
        body, name=name, grid=(t // tr,), in_specs=in_specs, out_specs=out_specs, out_shape=out_shape,
        compiler_params=_params(("arbitrary",)),
    )(*args)
    return res if need_dx else (None, res[0])


def loss_head(x, g, tgt):
    t, d = x.shape
    tr = _tile(t, 512)

    def body(x_ref, g_ref, t_ref, l_ref, dx_ref, dg_ref):
        i = pl.program_id(0)

        @pl.when(i == 0)
        def _():
            l_ref[...] = jnp.zeros_like(l_ref)
            dg_ref[...] = jnp.zeros_like(dg_ref)

        xf = x_ref[...]
        r = lax.rsqrt(jnp.mean(xf * xf, axis=-1, keepdims=True) + EPS)
        xh = xf * r
        diff = xh * g_ref[...] - t_ref[...]
        l_ref[...] += 0.5 * jnp.sum(jnp.mean(diff * diff, axis=-1, keepdims=True))
        dy = diff * (1.0 / d)
        dg_ref[...] += jnp.sum(dy * xh, axis=0, keepdims=True)
        dxh = dy * g_ref[...]
        dx_ref[...] = r * (dxh - xh * jnp.mean(dxh * xh, axis=-1, keepdims=True))

    row = pl.BlockSpec((tr, d), lambda i: (i, 0))
    vec = pl.BlockSpec((1, d), lambda i: (0, 0))
    return pl.pallas_call(
        body, name="loss_head", grid=(t // tr,), in_specs=[row, vec, row],
        out_specs=[pl.BlockSpec((1, 128), lambda i: (0, 0)), row, vec],
        out_shape=[jax.ShapeDtypeStruct((1, 128), F32), jax.ShapeDtypeStruct((t, d), F32),
                   jax.ShapeDtypeStruct((1, d), F32)],
        compiler_params=_params(("arbitrary",)),
    )(x, g.reshape(1, d), tgt)


def glu_fwd(glu, x, gain):
    t, d = x.shape
    tr = _tile(t, 512)

    def body(v_ref, g_ref, x_ref, n_ref, o_ref, h_ref):
        y = x_ref[...] + v_ref[...] * jax.nn.sigmoid(g_ref[...])
        o_ref[...] = y
        r = lax.rsqrt(jnp.mean(y * y, axis=-1, keepdims=True) + EPS)
        h_ref[...] = (y * r * n_ref[...]).astype(h_ref.dtype)

    row = pl.BlockSpec((tr, d), lambda i: (i, 0))
    return pl.pallas_call(
        body, name="glu_fwd", grid=(t // tr,),
        in_specs=[row, pl.BlockSpec((tr, d), lambda i: (i, 1)), row, pl.BlockSpec((1, d), lambda i: (0, 0))],
        out_specs=[row, row],
        out_shape=[jax.ShapeDtypeStruct((t, d), F32), jax.ShapeDtypeStruct((t, d), BF16)],
        compiler_params=_params(("parallel",)),
    )(glu, glu, x, gain.reshape(1, d))


def glu_bwd(glu, dmix):
    t, d = dmix.shape
    tr = _tile(t, 512)

    def body(v_ref, g_ref, d_ref, o_ref):
        sg = jax.nn.sigmoid(g_ref[...])
        dm = d_ref[...]
        o_ref[:, :d] = (dm * sg).astype(o_ref.dtype)
        o_ref[:, d:] = (dm * v_ref[...] * sg * (1.0 - sg)).astype(o_ref.dtype)

    return pl.pallas_call(
        body, name="glu_bwd", grid=(t // tr,),
        in_specs=[pl.BlockSpec((tr, d), lambda i: (i, 0)), pl.BlockSpec((tr, d), lambda i: (i, 1)),
                  pl.BlockSpec((tr, d), lambda i: (i, 0))],
        out_specs=pl.BlockSpec((tr, 2 * d), lambda i: (i, 0)),
        out_shape=jax.ShapeDtypeStruct((t, 2 * d), BF16), compiler_params=_params(("parallel",)),
    )(glu, glu, dmix)


def _head_masks(shape):
    lane = lax.broadcasted_iota(jnp.int32, shape, 1)
    return lane < SB_HEAD_DIM


def _stack_heads(xf, is_a):
    return jnp.concatenate([jnp.where(is_a, xf, 0.0), jnp.where(is_a, 0.0, xf)], axis=0).astype(MXU_DTYPE)


def _diag_mask(qb, row0, rows):
    row = (lax.broadcasted_iota(jnp.int32, (rows, qb), 0) + row0) & (qb - 1)
    col = lax.broadcasted_iota(jnp.int32, (rows, qb), 1)
    return col < row


def _tri01(qb, pred):
    j = lax.broadcasted_iota(jnp.int32, (qb, qb), 0)
    s = lax.broadcasted_iota(jnp.int32, (qb, qb), 1)
    m = pred(j, s).astype(BF16)
    return jnp.concatenate([m, m], axis=0)


def _split_cat(x):
    hi = x.astype(BF16)
    lo = (x - hi.astype(F32)).astype(BF16)
    return jnp.concatenate([hi, lo], axis=1)


def sb_attn_fwd(proj, order, bsz, seq):
    qb = SB_BLOCK
    nq = seq // qb
    npair = SB_WIDTH // 128
    scale = SB_HEAD_DIM ** -0.5

    def body(q_ref, k_ref, v_ref, order_ref, o_ref, r_ref):
        qi = pl.program_id(2)
        is_a = _head_masks((qb, 128))
        q2 = _stack_heads(q_ref[...] * scale, is_a)
        diag = _diag_mask(qb, 0, 2 * qb)
        upper = _tri01(qb, lambda j, s: j > s)

        def blocks(kbs, acc, run, masked):
            sl = [pl.ds(pl.multiple_of(kb * qb, qb), qb) for kb in kbs]
            zs = [lax.dot_general(q2, k_ref[s, :].astype(MXU_DTYPE), _NT, preferred_element_type=F32) for s in sl]
            lks = [-jnp.maximum(z, 0.0) - jnp.log(1.0 + jnp.exp(-jnp.abs(z))) for z in zs]
            lbs = [lk + z for lk, z in zip(lks, zs)]
            if masked:
                lks = [jnp.where(diag, lk, 0.0) for lk in lks]
            cs = [lax.dot_general(_split_cat(lk), upper, _NN, preferred_element_type=F32) for lk in lks]
            for lk, lb, c, s in zip(lks, lbs, cs, sl):
                w = jnp.exp(lb + (run + c))
                if masked:
                    w = jnp.where(diag, w, 0.0)
                acc = acc + lax.dot_general(w.astype(MXU_DTYPE), v_ref[s, :].astype(MXU_DTYPE), _NN,
                                            preferred_element_type=F32)
                run = run + (c[:, 0:1] + lk[:, 0:1])
            return acc, run

        carry = blocks([qi], jnp.zeros((2 * qb, 128), F32), jnp.zeros((2 * qb, 1), F32), True)
        carry = lax.cond(qi % 2 == 1, lambda c: blocks([qi - 1], c[0], c[1], False), lambda c: c, carry)
        top = qi - qi % 2
        acc, run = lax.fori_loop(
            0, qi // 2, lambda i, c: blocks([top - 1 - 2 * i, top - 2 - 2 * i], c[0], c[1], False), carry)
        o_ref[...] = jnp.where(is_a, acc[:qb], acc[qb:]).astype(o_ref.dtype)
        r_ref[...] = jnp.where(is_a, run[:qb], run[qb:])

    return pl.pallas_call(
        body, name="sb_attn_fwd", grid=(bsz, npair, nq),
        in_specs=[pl.BlockSpec((qb, 128), lambda b, p, i: (b * nq + i, p)),
                  pl.BlockSpec((seq, 128), lambda b, p, i: (b, npair + p)),
                  pl.BlockSpec((seq, 128), lambda b, p, i: (b, 2 * npair + p)),
                  pl.BlockSpec((1, 128), lambda b, p, i: (0, 0))],
        out_specs=[pl.BlockSpec((qb, 128), lambda b, p, i: (b * nq + i, p)),
                   pl.BlockSpec((qb, 128), lambda b, p, i: (b * nq + i, p))],
        out_shape=[jax.ShapeDtypeStruct((bsz * seq, SB_WIDTH), BF16),
                   jax.ShapeDtypeStruct((bsz * seq, SB_WIDTH), F32)],
        compiler_params=_params(("parallel", "parallel", "arbitrary")),
    )(proj, proj, proj, order)


def sb_attn_bwd(proj, rsum, dcat, bsz, seq):
    qb = SB_BLOCK
    nq = seq // qb
    npair = SB_WIDTH // 128
    scale = SB_HEAD_DIM ** -0.5

    def body(q_ref, k_ref, v_ref, r_ref, do_ref, dq_ref, dk_ref, dv_ref):
        qi = pl.program_id(2)

        @pl.when(qi == 0)
        def _():
            dk_ref[...] = jnp.zeros_like(dk_ref)
            dv_ref[...] = jnp.zeros_like(dv_ref)

        is_a = _head_masks((qb, 128))
        q2 = _stack_heads(q_ref[...] * scale, is_a)
        do2 = _stack_heads(do_ref[...].astype(F32), is_a)
        rf = r_ref[...]
        rtot = jnp.concatenate([rf[:, 0:1], rf[:, SB_HEAD_DIM:SB_HEAD_DIM + 1]], axis=0)
        diag = _diag_mask(qb, 0, 2 * qb)
        incl = _tri01(qb, lambda j, s: j <= s)
        strict = _tri01(qb, lambda j, s: j < s)

        def blocks(kbs, dq, pre, epre, masked):
            sl = [pl.ds(pl.multiple_of(kb * qb, qb), qb) for kb in kbs]
            ks = [k_ref[s, :].astype(MXU_DTYPE) for s in sl]
            vs = [v_ref[s, :].astype(MXU_DTYPE) for s in sl]
            zs = [lax.dot_general(q2, kblk, _NT, preferred_element_type=F32) for kblk in ks]
            dws = [lax.dot_general(do2, vblk, _NT, preferred_element_type=F32) for vblk in vs]
            lks = [-jnp.maximum(z, 0.0) - jnp.log(1.0 + jnp.exp(-jnp.abs(z))) for z in zs]
            lbs = [lk + z for lk, z in zip(lks, zs)]
            if masked:
                lks = [jnp.where(diag, lk, 0.0) for lk in lks]
            ps = [lax.dot_general(_split_cat(lk), incl, _NN, preferred_element_type=F32) for lk in lks]
            ws, es = [], []
            for lk, lb, p, dw in zip(lks, lbs, ps, dws):
                w = jnp.exp(lb + (rtot - (pre + p)))
                if masked:
                    w = jnp.where(diag, w, 0.0)
                ws.append(w)
                es.append(dw * w)
                pre = pre + p[:, qb - 1:qb]
            cs = [lax.dot_general(_split_cat(e), strict, _NN, preferred_element_type=F32) for e in es]
            for e, lb, c, w, kblk, s in zip(es, lbs, cs, ws, ks, sl):
                dz = e - jnp.exp(lb) * (e + (epre + c))
                if masked:
                    dz = jnp.where(diag, dz, 0.0)
                dz = dz.astype(MXU_DTYPE)
                dq = dq + lax.dot_general(dz, kblk, _NN, preferred_element_type=F32)
                dk_ref[s, :] += lax.dot_general(dz, q2, _TN, preferred_element_type=F32)
                dv_ref[s, :] += lax.dot_general(w.astype(MXU_DTYPE), do2, _TN, preferred_element_type=F32)
                epre = epre + (c[:, qb - 1:qb] + e[:, qb - 1:qb])
            return dq, pre, epre

        zc = jnp.zeros((2 * qb, 1), F32)
        carry = lax.fori_loop(0, qi // 2, lambda i, c: blocks([2 * i, 2 * i + 1], c[0], c[1], c[2], False),
                              (jnp.zeros((2 * qb, 128), F32), zc, zc))
        carry = lax.cond(qi % 2 == 1, lambda c: blocks([qi - 1], c[0], c[1], c[2], False), lambda c: c, carry)
        dq = blocks([qi], carry[0], carry[1], carry[2], True)[0]
        dq_ref[...] = jnp.where(is_a, dq[:qb], dq[qb:]) * scale

    full = jax.ShapeDtypeStruct((bsz * seq, SB_WIDTH), F32)
    qspec = pl.BlockSpec((qb, 128), lambda b, p, i: (b * nq + i, p))
    return pl.pallas_call(
        body, name="sb_attn_bwd", grid=(bsz, npair, nq),
        in_specs=[qspec,
                  pl.BlockSpec((seq, 128), lambda b, p, i: (b, npair + p)),
                  pl.BlockSpec((seq, 128), lambda b, p, i: (b, 2 * npair + p)),
                  qspec, qspec],
        out_specs=[qspec, pl.BlockSpec((seq, 128), lambda b, p, i: (b, p)),
                   pl.BlockSpec((seq, 128), lambda b, p, i: (b, p))],
        out_shape=[full, full, full],
        compiler_params=_params(("parallel", "parallel", "arbitrary")),
    )(proj, proj, proj, rsum, dcat)


def _window_sums(x, forward):
    n = x.shape[0]
    out = []
    s = x
    for sh in (1, 2, 4, 8):
        s = s + pltpu.roll(s, (n - sh) if forward else sh, 0)
        out.append(s)
    return out


def _pool_counts(tc, c, w):
    t = lax.broadcasted_iota(jnp.int32, (tc, 1), 0) + c * tc
    return jnp.minimum(t + 1, w).astype(F32)


def pool_fwd(proj, pool_w, pool_scale, bsz, seq):
    tc = _tile(seq, 512)
    nc = seq // tc
    hb = tc // POOL_HALO
    ucol = 3

    def body(u_ref, prev_ref, w_ref, s_ref, o_ref):
        c = pl.program_id(1)
        prev = jnp.where(c > 0, prev_ref[...], 0.0)
        x = jnp.concatenate([prev, u_ref[...]], axis=0)
        sums = _window_sums(x, forward=False)
        for g, win in enumerate(POOL_WINDOWS):
            ls = slice(g * POOL_GROUP, (g + 1) * POOL_GROUP)
            pooled = sums[g][POOL_HALO:, ls] / _pool_counts(tc, c, win) - x[POOL_HALO:, ls]
            y = _dot(pooled, w_ref[g], _NN)
            o_ref[:, ls] = (y * s_ref[:, ls]).astype(o_ref.dtype)

    return pl.pallas_call(
        body, name="pool_fwd", grid=(bsz, nc),
        in_specs=[pl.BlockSpec((tc, SB_WIDTH), lambda b, c: (b * nc + c, ucol)),
                  pl.BlockSpec((POOL_HALO, SB_WIDTH), lambda b, c: (jnp.maximum((b * nc + c) * hb - 1, 0), ucol)),
                  pl.BlockSpec((4, POOL_GROUP, POOL_GROUP), lambda b, c: (0, 0, 0)),
                  pl.BlockSpec((1, SB_WIDTH), lambda b, c: (0, 0))],
        out_specs=pl.BlockSpec((tc, SB_WIDTH), lambda b, c: (b * nc + c, 0)),
        out_shape=jax.ShapeDtypeStruct((bsz * seq, SB_WIDTH), BF16),
        compiler_params=_params(("parallel", "parallel")),
    )(proj, proj, pool_w, pool_scale)


def pool_bwd(proj, pool_w, pool_scale, dcat, bsz, seq):
    tc = _tile(seq, 512)
    nc = seq // tc
    hb = tc // POOL_HALO
    nblk = bsz * seq // POOL_HALO
    ucol = 3

    def body(u_ref, prev_ref, dy_ref, nxt_ref, w_ref, s_ref, du_ref, dw_ref, ds_ref):
        b, c = pl.program_id(0), pl.program_id(1)

        @pl.when((b == 0) & (c == 0))
        def _():
            dw_ref[...] = jnp.zeros_like(dw_ref)
            ds_ref[...] = jnp.zeros_like(ds_ref)

        prev = jnp.where(c > 0, prev_ref[...], 0.0)
        x = jnp.concatenate([prev, u_ref[...]], axis=0)
        sums = _window_sums(x, forward=False)
        nxt = jnp.where(c < nc - 1, nxt_ref[...].astype(F32), 0.0)
        dy = jnp.concatenate([dy_ref[...].astype(F32), nxt], axis=0)
        tq = lax.broadcasted_iota(jnp.int32, (tc + POOL_HALO, 1), 0) + c * tc
        for g, win in enumerate(POOL_WINDOWS):
            ls = slice(g * POOL_GROUP, (g + 1) * POOL_GROUP)
            pooled = sums[g][POOL_HALO:, ls] / _pool_counts(tc, c, win) - x[POOL_HALO:, ls]
            y = _dot(pooled, w_ref[g], _NN)
            ds_ref[:, ls] += jnp.sum(dy[:tc, ls] * y, axis=0, keepdims=True)
            dz = dy[:, ls] * s_ref[:, ls]
            dw_ref[g] += _dot(pooled, dz[:tc], _TN)
            dpool = _dot(dz, w_ref[g], _NT)
            dmean = dpool / jnp.minimum(tq + 1, win).astype(F32)
            fsum = _window_sums(dmean, forward=True)[g]
            du_ref[:, ls] = fsum[:tc] - dpool[:tc]

    return pl.pallas_call(
        body, name="pool_bwd", grid=(bsz, nc),
        in_specs=[pl.BlockSpec((tc, SB_WIDTH), lambda b, c: (b * nc + c, ucol)),
                  pl.BlockSpec((POOL_HALO, SB_WIDTH), lambda b, c: (jnp.maximum((b * nc + c) * hb - 1, 0), ucol)),
                  pl.BlockSpec((tc, SB_WIDTH), lambda b, c: (b * nc + c, 1)),
                  pl.BlockSpec((POOL_HALO, SB_WIDTH), lambda b, c: (jnp.minimum((b * nc + c + 1) * hb, nblk - 1), 1)),
                  pl.BlockSpec((4, POOL_GROUP, POOL_GROUP), lambda b, c: (0, 0, 0)),
                  pl.BlockSpec((1, SB_WIDTH), lambda b, c: (0, 0))],
        out_specs=[pl.BlockSpec((tc, SB_WIDTH), lambda b, c: (b * nc + c, 0)),
                   pl.BlockSpec((4, POOL_GROUP, POOL_GROUP), lambda b, c: (0, 0, 0)),
                   pl.BlockSpec((1, SB_WIDTH), lambda b, c: (0, 0))],
        out_shape=[jax.ShapeDtypeStruct((bsz * seq, SB_WIDTH), F32),
                   jax.ShapeDtypeStruct((4, POOL_GROUP, POOL_GROUP), F32),
                   jax.ShapeDtypeStruct((1, SB_WIDTH), F32)],
        compiler_params=_params(("arbitrary", "arbitrary")),
    )(proj, proj, dcat, dcat, pool_w, pool_scale)


def _lbar(lam_re, lam_im, log_dt):
    dt = jnp.exp(log_dt)
    mag = jnp.exp(lam_re * dt)
    ang = lam_im * dt
    return mag * jnp.cos(ang), mag * jnp.sin(ang)


def _bbar(lam_re, lam_im, log_dt, b_re, b_im):
    lb_re, lb_im = _lbar(lam_re, lam_im, log_dt)
    n_re = lb_re - 1.0
    den = lam_re * lam_re + lam_im * lam_im
    coef_re = (n_re * lam_re + lb_im * lam_im) / den
    coef_im = (lb_im * lam_re - n_re * lam_im) / den
    return coef_re * b_re - coef_im * b_im, coef_re * b_im + coef_im * b_re


def _expand01():
    p = lax.broadcasted_iota(jnp.int32, (64, 1024), 0)
    q = lax.broadcasted_iota(jnp.int32, (64, 1024), 1)
    return (lax.shift_right_logical(q, 4) == p).astype(BF16)


def ssm_prep(lam_re, lam_im, log_dt, b_re2, b_im2):
    def body(lr_ref, li_ref, dt_ref, br_ref, bi_ref, ar_ref, ai_ref, bbr_ref, bbi_ref):
        e = _expand01()
        lr, li, dt = lr_ref[...], li_ref[...], dt_ref[...]
        ar_ref[...], ai_ref[...] = _lbar(lr, li, dt)
        bbr_ref[...], bbi_ref[...] = _bbar(_dot_exact01(lr, e), _dot_exact01(li, e), dt, br_ref[...], bi_ref[...])

    s64 = jax.ShapeDtypeStruct((64, 64), F32)
    s1k = jax.ShapeDtypeStruct((64, 1024), F32)
    return pl.pallas_call(body, name="ssm_prep", out_shape=[s64, s64, s1k, s1k], compiler_params=_params())(
        lam_re, lam_im, log_dt, b_re2, b_im2)


def ssm_prep_bwd(lam_re, lam_im, log_dt, b_re2, b_im2, da_re, da_im, dbb_re, dbb_im):
    def body(lr_ref, li_ref, dt_ref, br_ref, bi_ref, dar_ref, dai_ref, dbr_ref, dbi_ref,
             olr_ref, oli_ref, odt_ref, obr_ref, obi_ref):
        e = _expand01()
        lr, li, dt = lr_ref[...], li_ref[...], dt_ref[...]
        _, vjp_a = jax.vjp(_lbar, lr, li, dt)
        g_lr, g_li, g_dt = vjp_a((dar_ref[...], dai_ref[...]))
        _, vjp_b = jax.vjp(_bbar, _dot_exact01(lr, e), _dot_exact01(li, e), dt, br_ref[...], bi_ref[...])
        x_lr, x_li, x_dt, g_br, g_bi = vjp_b((dbr_ref[...], dbi_ref[...]))
        olr_ref[...] = g_lr + _dot_exact01(x_lr, e, _NT)
        oli_ref[...] = g_li + _dot_exact01(x_li, e, _NT)
        odt_ref[...] = g_dt + x_dt
        obr_ref[...] = g_br
        obi_ref[...] = g_bi

    s64 = jax.ShapeDtypeStruct((64, 64), F32)
    s1k = jax.ShapeDtypeStruct((64, 1024), F32)
    return pl.pallas_call(body, name="ssm_prep_bwd",
                          out_shape=[s64, s64, jax.ShapeDtypeStruct((64, 1), F32), s1k, s1k],
                          compiler_params=_params())(
        lam_re, lam_im, log_dt, b_re2, b_im2, da_re, da_im, dbb_re, dbb_im)


def _gelu(y):
    c = math.sqrt(2.0 / math.pi)
    return 0.5 * y * (1.0 + jnp.tanh(c * (y + 0.044715 * y * y * y)))


def _gelu_grad(y):
    c = math.sqrt(2.0 / math.pi)
    th = jnp.tanh(c * (y + 0.044715 * y * y * y))
    return 0.5 * (1.0 + th) + 0.5 * y * (1.0 - th * th) * c * (1.0 + 3.0 * 0.044715 * y * y)


def _cmul(ar, ai, br, bi):
    return ar * br - ai * bi, ar * bi + ai * br


def _scan_tables(ar, ai, reverse, tabs):
    row = lax.broadcasted_iota(jnp.int32, (8, SSM_STATES), 0)
    a1 = (ar, ai)
    a2 = _cmul(*a1, *a1)
    a4 = _cmul(*a2, *a2)
    powers = [a1, a2, _cmul(*a2, *a1), a4]
    powers += [_cmul(*a4, *p) for p in powers]
    for k, (val, sh) in enumerate(((a1, 1), (a2, 2), (a4, 4))):
        keep = (row < 8 - sh) if reverse else (row >= sh)
        tabs[2 * k][...] = jnp.where(keep, val[0], 0.0)
        tabs[2 * k + 1][...] = jnp.where(keep, val[1], 0.0)
    pr = jnp.zeros((8, SSM_STATES), F32)
    pi = jnp.zeros((8, SSM_STATES), F32)
    for r in range(8):
        val = powers[7 - r] if reverse else powers[r]
        pr = jnp.where(row == r, val[0], pr)
        pi = jnp.where(row == r, val[1], pi)
    tabs[6][...] = pr
    tabs[7][...] = pi


def _scan8(xr, xi, tabs, ls, cr, ci, reverse):
    for k, sh in enumerate((1, 2, 4)):
        amt = (8 - sh) if reverse else sh
        sr, si = pltpu.roll(xr, amt, 0), pltpu.roll(xi, amt, 0)
        lr, li = tabs[2 * k][:, ls], tabs[2 * k + 1][:, ls]
        xr, xi = xr + lr * sr - li * si, xi + lr * si + li * sr
    pr, pi = tabs[6][:, ls], tabs[7][:, ls]
    return xr + pr * cr - pi * ci, xi + pr * ci + pi * cr


def _block8(b):
    return pl.ds(pl.multiple_of(b * 8, 8), 8)


def ssm_fwd(u, wt, ct, a_re, a_im, dskip, bsz, seq):
    tc = _tile(seq, 256)
    nc = seq // tc
    ns = SSM_TILE_STATES
    nl = SSM_STATES // SSM_LANES

    def body(u_ref, wt_ref, ct_ref, ar_ref, ai_ref, d_ref, y_ref, gl_ref, hr_ref, hi_ref, sr_ref, si_ref, *tabs):
        b, c = pl.program_id(0), pl.program_id(1)

        @pl.when((b == 0) & (c == 0))
        def _():
            _scan_tables(ar_ref[...], ai_ref[...], False, tabs)

        @pl.when(c == 0)
        def _():
            sr_ref[...] = jnp.zeros_like(sr_ref)
            si_ref[...] = jnp.zeros_like(si_ref)

        uf = u_ref[...]
        for i in range(SSM_TILES):
            bu = _dot(uf[:, i * 128:(i + 1) * 128], wt_ref[i], _NN)
            hr_ref[:, i * ns:(i + 1) * ns] = bu[:, :ns]
            hi_ref[:, i * ns:(i + 1) * ns] = bu[:, ns:]

        def step(blk, carry):
            rows = _block8(blk)
            new = []
            for j in range(nl):
                ls = slice(j * SSM_LANES, (j + 1) * SSM_LANES)
                xr, xi = _scan8(hr_ref[rows, ls], hi_ref[rows, ls], tabs, ls, carry[2 * j], carry[2 * j + 1], False)
                hr_ref[rows, ls] = xr
                hi_ref[rows, ls] = xi
                new += [xr[7:8], xi[7:8]]
            return tuple(new)

        init = []
        for j in range(nl):
            ls = slice(j * SSM_LANES, (j + 1) * SSM_LANES)
            init += [sr_ref[:, ls], si_ref[:, ls]]
        last = lax.fori_loop(0, tc // 8, step, tuple(init), unroll=2)
        for j in range(nl):
            ls = slice(j * SSM_LANES, (j + 1) * SSM_LANES)
            sr_ref[:, ls] = last[2 * j]
            si_ref[:, ls] = last[2 * j + 1]
        for i in range(SSM_TILES):
            hcat = jnp.concatenate([hr_ref[:, i * ns:(i + 1) * ns], hi_ref[:, i * ns:(i + 1) * ns]], axis=1)
            ls = slice(i * 128, (i + 1) * 128)
            y = _dot(hcat, ct_ref[i], _NN) + d_ref[:, ls] * uf[:, ls]
            y_ref[:, ls] = y
            gl_ref[:, ls] = _gelu(y).astype(gl_ref.dtype)

    t = bsz * seq
    row = pl.BlockSpec((tc, D_MODEL), lambda b, c: (b * nc + c, 0))
    st = pl.BlockSpec((tc, SSM_STATES), lambda b, c: (b * nc + c, 0))
    diag = pl.BlockSpec((1, SSM_STATES), lambda b, c: (0, 0))
    return pl.pallas_call(
        body, name="ssm_fwd", grid=(bsz, nc),
        in_specs=[row, pl.BlockSpec((SSM_TILES, 128, 2 * ns), lambda b, c: (0, 0, 0)),
                  pl.BlockSpec((SSM_TILES, 2 * ns, 128), lambda b, c: (0, 0, 0)), diag, diag,
                  pl.BlockSpec((1, D_MODEL), lambda b, c: (0, 0))],
        out_specs=[row, row, st, st],
        out_shape=[jax.ShapeDtypeStruct((t, D_MODEL), F32), jax.ShapeDtypeStruct((t, D_MODEL), BF16),
                   jax.ShapeDtypeStruct((t, SSM_STATES), F32), jax.ShapeDtypeStruct((t, SSM_STATES), F32)],
        scratch_shapes=[pltpu.VMEM((1, SSM_STATES), F32)] * 2 + [pltpu.VMEM((8, SSM_STATES), F32)] * 8,
        compiler_params=_params(("arbitrary", "arbitrary")),
    )(u, wt, ct, a_re, a_im, dskip)


def ssm_bwd(dgl, y, u, h_re, h_im, wt, ct, a_re, a_im, dskip, bsz, seq):
    tc = _tile(seq, 256)
    nc = seq // tc
    nb = tc // 8
    ns = SSM_TILE_STATES
    nl = SSM_STATES // SSM_LANES

    def body(dgl_ref, y_ref, u_ref, hr_ref, hi_ref, pr_ref, pi_ref, wt_ref, ct_ref, ar_ref, ai_ref, d_ref,
             du_ref, dwt_ref, dct_ref, dd_ref, dar_ref, dai_ref, gr_ref, gi_ref, sr_ref, si_ref, ar8_ref, ai8_ref,
             *tabs):
        b, c = pl.program_id(0), pl.program_id(1)

        @pl.when((b == 0) & (c == 0))
        def _():
            for r in (dwt_ref, dct_ref, dd_ref, ar8_ref, ai8_ref):
                r[...] = jnp.zeros_like(r)
            _scan_tables(ar_ref[...], -ai_ref[...], True, tabs)

        @pl.when(c == 0)
        def _():
            sr_ref[...] = jnp.zeros_like(sr_ref)
            si_ref[...] = jnp.zeros_like(si_ref)

        uf = u_ref[...]
        dy = dgl_ref[...].astype(F32) * _gelu_grad(y_ref[...])
        dd_ref[...] += jnp.sum(dy * uf, axis=0, keepdims=True)
        for i in range(SSM_TILES):
            dyi = dy[:, i * 128:(i + 1) * 128]
            dh = _dot(dyi, ct_ref[i], _NT)
            gr_ref[:, i * ns:(i + 1) * ns] = dh[:, :ns]
            gi_ref[:, i * ns:(i + 1) * ns] = dh[:, ns:]
            hcat = jnp.concatenate([hr_ref[:, i * ns:(i + 1) * ns], hi_ref[:, i * ns:(i + 1) * ns]], axis=1)
            dct_ref[i] += _dot(hcat, dyi, _TN)
        row0 = lax.broadcasted_iota(jnp.int32, (8, SSM_LANES), 0) == 0

        def block(blk, carry, before):
            rows = _block8(blk)
            new = []
            for j in range(nl):
                ls = slice(j * SSM_LANES, (j + 1) * SSM_LANES)
                gr, gi = _scan8(gr_ref[rows, ls], gi_ref[rows, ls], tabs, ls, carry[2 * j], carry[2 * j + 1], True)
                gr_ref[rows, ls] = gr
                gi_ref[rows, ls] = gi
                bpr, bpi = before(j)
                hpr = jnp.where(row0, bpr, pltpu.roll(hr_ref[rows, ls], 1, 0))
                hpi = jnp.where(row0, bpi, pltpu.roll(hi_ref[rows, ls], 1, 0))
                ar8_ref[:, ls] += gr * hpr + gi * hpi
                ai8_ref[:, ls] += gi * hpr - gr * hpi
                new += [gr[0:1], gi[0:1]]
            return tuple(new)

        def step(jj, carry):
            blk = nb - 1 - jj
            prev_rows = _block8(blk - 1)

            def before(j):
                ls = slice(j * SSM_LANES, (j + 1) * SSM_LANES)
                return hr_ref[prev_rows, ls][7:8], hi_ref[prev_rows, ls][7:8]

            return block(blk, carry, before)

        init = []
        for j in range(nl):
            ls = slice(j * SSM_LANES, (j + 1) * SSM_LANES)
            init += [sr_ref[:, ls], si_ref[:, ls]]
        carry = lax.fori_loop(0, nb - 1, step, tuple(init))
        first = c == nc - 1

        def before_chunk(j):
            ls = slice(j * SSM_LANES, (j + 1) * SSM_LANES)
            return (jnp.where(first, 0.0, pr_ref[:, ls][7:8]), jnp.where(first, 0.0, pi_ref[:, ls][7:8]))

        last = block(0, carry, before_chunk)
        for j in range(nl):
            ls = slice(j * SSM_LANES, (j + 1) * SSM_LANES)
            sr_ref[:, ls] = last[2 * j]
            si_ref[:, ls] = last[2 * j + 1]
        for i in range(SSM_TILES):
            ls = slice(i * 128, (i + 1) * 128)
            gcat = jnp.concatenate([gr_ref[:, i * ns:(i + 1) * ns], gi_ref[:, i * ns:(i + 1) * ns]], axis=1)
            du_ref[:, ls] = (_dot(gcat, wt_ref[i], _NT) + d_ref[:, ls] * dy[:, ls]).astype(du_ref.dtype)
            dwt_ref[i] += _dot(uf[:, ls], gcat, _TN)

        @pl.when((b == bsz - 1) & (c == nc - 1))
        def _():
            dar_ref[...] = jnp.sum(ar8_ref[...], axis=0, keepdims=True)
            dai_ref[...] = jnp.sum(ai8_ref[...], axis=0, keepdims=True)

    t = bsz * seq
    rev = lambda b, c: (b * nc + (nc - 1 - c), 0)
    row = pl.BlockSpec((tc, D_MODEL), rev)
    st = pl.BlockSpec((tc, SSM_STATES), rev)
    prev = pl.BlockSpec((8, SSM_STATES), lambda b, c: (jnp.maximum((b * nc + (nc - 1 - c)) * nb - 1, 0), 0))
    diag = pl.BlockSpec((1, SSM_STATES), lambda b, c: (0, 0))
    wts = pl.BlockSpec((SSM_TILES, 128, 2 * ns), lambda b, c: (0, 0, 0))
    cts = pl.BlockSpec((SSM_TILES, 2 * ns, 128), lambda b, c: (0, 0, 0))
    vec = pl.BlockSpec((1, D_MODEL), lambda b, c: (0, 0))
    return pl.pallas_call(
        body, name="ssm_bwd", grid=(bsz, nc),
        in_specs=[row, row, row, st, st, prev, prev, wts, cts, diag, diag, vec],
        out_specs=[row, wts, cts, vec, diag, diag],
        out_shape=[jax.ShapeDtypeStruct((t, D_MODEL), BF16),
                   jax.ShapeDtypeStruct((SSM_TILES, 128, 2 * ns), F32),
                   jax.ShapeDtypeStruct((SSM_TILES, 2 * ns, 128), F32),
                   jax.ShapeDtypeStruct((1, D_MODEL), F32),
                   jax.ShapeDtypeStruct((1, SSM_STATES), F32), jax.ShapeDtypeStruct((1, SSM_STATES), F32)],
        scratch_shapes=[pltpu.VMEM((tc, SSM_STATES), F32)] * 2 + [pltpu.VMEM((1, SSM_STATES), F32)] * 2
                       + [pltpu.VMEM((8, SSM_STATES), F32)] * 10,
        compiler_params=_params(("arbitrary", "arbitrary")),
    )(dgl, y, u, h_re, h_im, h_re, h_im, wt, ct, a_re, a_im, dskip)


def _ssm_in_weights(bb_re2, bb_im2):
    eye = jnp.eye(8, dtype=F32)[None, :, None, :, None]

    def one(bb):
        t = bb.reshape(8, 8, 64, 16).transpose(0, 1, 3, 2)
        return (t[:, :, :, None, :] * eye).reshape(8, 128, 512)

    return jnp.concatenate([one(bb_re2), one(bb_im2)], axis=-1).astype(MXU_DTYPE)


def _ssm_in_weights_bwd(dwt):
    eye = jnp.eye(8, dtype=F32)[None, :, None, :, None]

    def one(d):
        t = (d.reshape(8, 8, 16, 8, 64) * eye).sum(axis=3)
        return t.transpose(0, 1, 3, 2).reshape(64, 1024)

    return one(dwt[..., :512]), one(dwt[..., 512:])


def _ssm_out_weights(c_re, c_im):
    eye = jnp.eye(8, dtype=F32)[None, :, None, :, None]

    def one(cc):
        t = cc.reshape(8, 8, 16, 64).transpose(0, 1, 3, 2)
        return (t[:, :, :, None, :] * eye).reshape(8, 512, 128)

    return jnp.concatenate([one(c_re), -one(c_im)], axis=1).astype(MXU_DTYPE)


def _ssm_out_weights_bwd(dct):
    eye = jnp.eye(8, dtype=F32)[None, :, None, :, None]

    def one(d):
        t = (d.reshape(8, 8, 64, 8, 16) * eye).sum(axis=3)
        return t.transpose(0, 1, 3, 2).reshape(64, 16, 64)

    return one(dct[:, :512]), -one(dct[:, 512:])


def _softmax(s):
    m = jnp.max(s, axis=-1, keepdims=True)
    e = jnp.exp(s - m)
    return e / jnp.sum(e, axis=-1, keepdims=True)


def xattn_fwd(q, kv, bsz, seq):
    tq = _tile(seq, 512)
    nq = seq // tq
    scale = XA_HEAD_DIM ** -0.5

    def body(q_ref, k_ref, v_ref, o_ref):
        s = lax.dot_general(q_ref[...], k_ref[...], _NT, preferred_element_type=F32) * scale
        p = _softmax(s)
        o_ref[...] = _dot(p, v_ref[...], _NN).astype(o_ref.dtype)

    qs = pl.BlockSpec((tq, XA_HEAD_DIM), lambda b, h, i: (b * nq + i, h))
    return pl.pallas_call(
        body, name="xattn_fwd", grid=(bsz, XA_HEADS, nq),
        in_specs=[qs, pl.BlockSpec((MEM_LEN, XA_HEAD_DIM), lambda b, h, i: (b, h)),
                  pl.BlockSpec((MEM_LEN, XA_HEAD_DIM), lambda b, h, i: (b, XA_HEADS + h))],
        out_specs=qs, out_shape=jax.ShapeDtypeStruct((bsz * seq, D_MODEL), BF16),
        compiler_params=_params(("parallel", "parallel", "parallel")),
    )(q, kv, kv)


def xattn_bwd(q, kv, do, bsz, seq):
    tq = _tile(seq, 512)
    nq = seq // tq
    scale = XA_HEAD_DIM ** -0.5

    def body(q_ref, k_ref, v_ref, do_ref, dq_ref, dk_ref, dv_ref):
        @pl.when(pl.program_id(2) == 0)
        def _():
            dk_ref[...] = jnp.zeros_like(dk_ref)
            dv_ref[...] = jnp.zeros_like(dv_ref)

        qv, kk, vv, dov = q_ref[...], k_ref[...], v_ref[...], do_ref[...]
        s = lax.dot_general(qv, kk, _NT, preferred_element_type=F32) * scale
        p = _softmax(s)
        dp = lax.dot_general(dov, vv, _NT, preferred_element_type=F32)
        ds = (p * (dp - jnp.sum(dp * p, axis=-1, keepdims=True)) * scale).astype(MXU_DTYPE)
        dq_ref[...] = lax.dot_general(ds, kk, _NN, preferred_element_type=F32).astype(dq_ref.dtype)
        dk_ref[...] += lax.dot_general(ds, qv, _TN, preferred_element_type=F32)
        dv_ref[...] += lax.dot_general(p.astype(MXU_DTYPE), dov, _TN, preferred_element_type=F32)

    qs = pl.BlockSpec((tq, XA_HEAD_DIM), lambda b, h, i: (b * nq + i, h))
    ks = pl.BlockSpec((MEM_LEN, XA_HEAD_DIM), lambda b, h, i: (b, h))
    vs = pl.BlockSpec((MEM_LEN, XA_HEAD_DIM), lambda b, h, i: (b, XA_HEADS + h))
    dkv = jax.ShapeDtypeStruct((bsz * MEM_LEN, D_MODEL), F32)
    dq, dk, dv = pl.pallas_call(
        body, name="xattn_bwd", grid=(bsz, XA_HEADS, nq),
        in_specs=[qs, ks, vs, qs], out_specs=[qs, ks, ks],
        out_shape=[jax.ShapeDtypeStruct((bsz * seq, D_MODEL), BF16), dkv, dkv],
        compiler_params=_params(("parallel", "parallel", "arbitrary")),
    )(q, kv, kv, do)
    return dq, dk, dv


CONV_HALO = 16


def _shifts_down(x, prev):
    h = prev.shape[0]
    ext = jnp.concatenate([prev, x], axis=0)
    return pltpu.roll(ext, 1, 0)[h:], pltpu.roll(ext, 2, 0)[h:]


def _shifts_up(x, nxt):
    rows = x.shape[0]
    n = rows + nxt.shape[0]
    ext = jnp.concatenate([x, nxt], axis=0)
    return pltpu.roll(ext, n - 1, 0)[:rows], pltpu.roll(ext, n - 2, 0)[:rows]


def _conv_taps(u, u1, u2, w, b):
    return b + w[2:3] * u + w[1:2] * u1 + w[0:1] * u2


def conv_fwd(up, cw, cb, bsz, seq):
    tc = _tile(seq, 512)
    nc = seq // tc
    hb = tc // CONV_HALO
    half = N_DEV // 2

    def body(uv_ref, ug_ref, pv_ref, pg_ref, wv_ref, wg_ref, bv_ref, bg_ref, o_ref):
        c = pl.program_id(2)
        pv = jnp.where(c > 0, pv_ref[...].astype(F32), 0.0)
        pg = jnp.where(c > 0, pg_ref[...].astype(F32), 0.0)
        uv, ug = uv_ref[...].astype(F32), ug_ref[...].astype(F32)
        val = _conv_taps(uv, *_shifts_down(uv, pv), wv_ref[...], bv_ref[...])
        gate = _conv_taps(ug, *_shifts_down(ug, pg), wg_ref[...], bg_ref[...])
        o_ref[...] = (gate * jax.nn.sigmoid(gate) * val).astype(o_ref.dtype)

    def cur(off):
        return pl.BlockSpec((None, tc, FF_SHARD), lambda b, j, c: (j + off, b * nc + c, 0))

    def prv(off):
        return pl.BlockSpec((None, CONV_HALO, FF_SHARD), lambda b, j, c: (j + off, jnp.maximum((b * nc + c) * hb - 1, 0), 0))

    def par(rows, off):
        return pl.BlockSpec((None, rows, FF_SHARD), lambda b, j, c: (j + off, 0, 0))

    return pl.pallas_call(
        body, name="conv_fwd", grid=(bsz, half, nc),
        in_specs=[cur(0), cur(half), prv(0), prv(half), par(3, 0), par(3, half), par(1, 0), par(1, half)],
        out_specs=cur(0), out_shape=jax.ShapeDtypeStruct((half, bsz * seq, FF_SHARD), BF16),
        compiler_params=_params(("parallel", "parallel", "parallel")),
    )(up, up, up, up, cw, cw, cb, cb)


def conv_bwd_taps(up, cw, cb, dact, bsz, seq):
    tc = _tile(seq, 512)
    nc = seq // tc
    hb = tc // CONV_HALO
    half = N_DEV // 2

    def body(uv_ref, ug_ref, pv_ref, pg_ref, wv_ref, wg_ref, bv_ref, bg_ref, da_ref,
             dc_ref, dwv_ref, dwg_ref, dbv_ref, dbg_ref):
        b, c = pl.program_id(1), pl.program_id(2)

        @pl.when((b == 0) & (c == 0))
        def _():
            for r in (dwv_ref, dwg_ref, dbv_ref, dbg_ref):
                r[...] = jnp.zeros_like(r)

        pv = jnp.where(c > 0, pv_ref[...].astype(F32), 0.0)
        pg = jnp.where(c > 0, pg_ref[...].astype(F32), 0.0)
        uv, ug = uv_ref[...].astype(F32), ug_ref[...].astype(F32)
        uv1, uv2 = _shifts_down(uv, pv)
        ug1, ug2 = _shifts_down(ug, pg)
        val = _conv_taps(uv, uv1, uv2, wv_ref[...], bv_ref[...])
        gate = _conv_taps(ug, ug1, ug2, wg_ref[...], bg_ref[...])
        sg = jax.nn.sigmoid(gate)
        da = da_ref[...].astype(F32)
        dsilu = da * sg
        dval = dsilu * gate
        dgate = dsilu * val * (1.0 + gate * (1.0 - sg))
        dc_ref[0] = dval.astype(dc_ref.dtype)
        dc_ref[1] = dgate.astype(dc_ref.dtype)
        for dcv, taps, dw_ref, db_ref in ((dval, (uv2, uv1, uv), dwv_ref, dbv_ref),
                                          (dgate, (ug2, ug1, ug), dwg_ref, dbg_ref)):
            db_ref[...] += jnp.sum(dcv, axis=0, keepdims=True)
            for k, u_k in enumerate(taps):
                dw_ref[k:k + 1, :] += jnp.sum(dcv * u_k, axis=0, keepdims=True)

    def cur(off):
        return pl.BlockSpec((None, tc, FF_SHARD), lambda j, b, c: (j + off, b * nc + c, 0))

    def prv(off):
        return pl.BlockSpec((None, CONV_HALO, FF_SHARD), lambda j, b, c: (j + off, jnp.maximum((b * nc + c) * hb - 1, 0), 0))

    def par(rows, off):
        return pl.BlockSpec((None, rows, FF_SHARD), lambda j, b, c: (j + off, 0, 0))

    t = bsz * seq
    hs = jax.ShapeDtypeStruct((2, half, t, FF_SHARD), BF16)
    ws = jax.ShapeDtypeStruct((half, 3, FF_SHARD), F32)
    bs = jax.ShapeDtypeStruct((half, 1, FF_SHARD), F32)
    dc, dwv, dwg, dbv, dbg = pl.pallas_call(
        body, name="conv_bwd_taps", grid=(half, bsz, nc),
        in_specs=[cur(0), cur(half), prv(0), prv(half), par(3, 0), par(3, half), par(1, 0), par(1, half), cur(0)],
        out_specs=[pl.BlockSpec((2, None, tc, FF_SHARD), lambda j, b, c: (0, j, b * nc + c, 0)),
                   par(3, 0), par(3, 0), par(1, 0), par(1, 0)],
        out_shape=[hs, ws, ws, bs, bs],
        compiler_params=_params(("parallel", "arbitrary", "arbitrary")),
    )(up, up, up, up, cw, cw, cb, cb, dact)
    return (dc.reshape(N_DEV, t, FF_SHARD), jnp.concatenate([dwv, dwg], axis=0),
            jnp.concatenate([dbv, dbg], axis=0))


def conv_bwd_input(dconv, cw, bsz, seq):
    tc = _tile(seq, 1024)
    nc = seq // tc
    hb = tc // CONV_HALO
    nblk = bsz * seq // CONV_HALO

    def body(d_ref, n_ref, w_ref, o_ref):
        c = pl.program_id(2)
        nxt = jnp.where(c < nc - 1, n_ref[...].astype(F32), 0.0)
        d = d_ref[...].astype(F32)
        d1, d2 = _shifts_up(d, nxt)
        w = w_ref[...]
        o_ref[...] = (w[2:3] * d + w[1:2] * d1 + w[0:1] * d2).astype(o_ref.dtype)

    cur = pl.BlockSpec((None, tc, FF_SHARD), lambda j, b, c: (j, b * nc + c, 0))
    return pl.pallas_call(
        body, name="conv_bwd_input", grid=(N_DEV, bsz, nc),
        in_specs=[cur, pl.BlockSpec((None, CONV_HALO, FF_SHARD),
                                    lambda j, b, c: (j, jnp.minimum((b * nc + c + 1) * hb, nblk - 1), 0)),
                  pl.BlockSpec((None, 3, FF_SHARD), lambda j, b, c: (j, 0, 0))],
        out_specs=cur, out_shape=jax.ShapeDtypeStruct(dconv.shape, BF16),
        compiler_params=_params(("parallel", "parallel", "parallel")),
    )(dconv, dconv, cw)


def _my_index():
    return 4 * lax.axis_index("x") + 2 * lax.axis_index("y") + lax.axis_index("c")


def _peer(k):
    return (lax.axis_index("x") ^ ((k >> 2) & 1), lax.axis_index("y") ^ ((k >> 1) & 1),
            lax.axis_index("c") ^ (k & 1))


_HBM = pl.BlockSpec(memory_space=pltpu.HBM)
_SEM = pl.BlockSpec(memory_space=pltpu.SEMAPHORE)
_DATAFLOW = pltpu.SideEffectType.DATAFLOW_SIDE_EFFECTING


def _split_copies(gather, src_ref, land_ref, send_sems, recv_sems, local_sem):
    me = _my_index()

    def part(j):
        return src_ref if gather else src_ref.at[j]

    local = pltpu.make_async_copy(part(me), land_ref.at[me], local_sem)
    sends = [pltpu.make_async_remote_copy(
        src_ref=part(me ^ k), dst_ref=land_ref.at[me], send_sem=send_sems.at[k - 1], recv_sem=recv_sems.at[k - 1],
        device_id=_peer(k), device_id_type=pl.DeviceIdType.MESH) for k in range(1, N_DEV)]
    recvs = [pltpu.make_async_remote_copy(
        src_ref=part(me ^ k), dst_ref=land_ref.at[me ^ k], send_sem=send_sems.at[k - 1], recv_sem=recv_sems.at[k - 1],
        device_id=_peer(k), device_id_type=pl.DeviceIdType.MESH) for k in range(1, N_DEV)]
    return local, sends, recvs


def split_start(name, srcs, gather):
    n = len(srcs)
    lands = [((N_DEV,) + s.shape) if gather else s.shape for s in srcs]

    def body(*refs):
        ins, outs = refs[:2 * n], refs[2 * n:]
        for i in range(n):
            local, sends, _ = _split_copies(gather, ins[i], ins[n + i], *outs[3 * i:3 * i + 3])
            local.start()
            for cp in sends:
                cp.start()
        outs[-1][...] = jnp.zeros_like(outs[-1])

    dma7 = pltpu.SemaphoreType.DMA((N_DEV - 1,))
    out = pl.pallas_call(
        body, name=name,
        out_shape=(dma7, dma7, pltpu.SemaphoreType.DMA(())) * n
                  + tuple(pltpu.HBM(s.shape, s.dtype) for s in srcs)
                  + tuple(pltpu.HBM(shape, s.dtype) for shape, s in zip(lands, srcs))
                  + (jax.ShapeDtypeStruct((8, 128), F32),),
        in_specs=(_HBM,) * (2 * n),
        out_specs=(_SEM,) * (3 * n) + (_HBM,) * (2 * n) + (pl.BlockSpec(memory_space=pltpu.VMEM),),
        input_output_aliases={i: 3 * n + i for i in range(2 * n)},
        compiler_params=pltpu.CompilerParams(has_side_effects=_DATAFLOW),
    )(*[pltpu.with_memory_space_constraint(s, pltpu.HBM) for s in srcs],
      *[pltpu.with_memory_space_constraint(lax.empty(shape, s.dtype), pltpu.HBM) for shape, s in zip(lands, srcs)])
    handles = [tuple(out[3 * i:3 * i + 3]) + (out[3 * n + i], out[4 * n + i]) for i in range(n)]
    return handles, out[-1][0, 0]


def split_wait(name, handles, after, gather):
    send_sems, recv_sems, local_sem, src_thru, land_thru = handles

    def body(src_ref, land_ref, send_sems, recv_sems, local_sem, after_ref, src_dead, got_ref, token):
        local, sends, recvs = _split_copies(gather, src_ref, land_ref, send_sems, recv_sems, local_sem)
        local.wait()
        for cp in recvs:
            cp.wait_send()
            cp.wait_recv()
        token[...] = jnp.zeros_like(token)

    out = pl.pallas_call(
        body, name=name,
        out_shape=(pltpu.HBM(src_thru.shape, src_thru.dtype), pltpu.HBM(land_thru.shape, land_thru.dtype),
                   jax.ShapeDtypeStruct((8, 128), F32)),
        in_specs=(_HBM, _HBM, _SEM, _SEM, _SEM, pl.BlockSpec(memory_space=pl.ANY)),
        out_specs=(_HBM, _HBM, pl.BlockSpec(memory_space=pltpu.VMEM)),
        input_output_aliases={0: 0, 1: 1},
        compiler_params=pltpu.CompilerParams(has_side_effects=_DATAFLOW),
    )(src_thru, land_thru, send_sems, recv_sems, local_sem, after)
    return out[1], out[2][0, 0]


def sum_parts(name, r):
    _, rows, cols = r.shape

    def body(r_ref, o_ref):
        acc = r_ref[0].astype(F32)
        for s in range(1, N_DEV):
            acc = acc + r_ref[s].astype(F32)
        o_ref[...] = acc

    return pl.pallas_call(body, name=name, out_shape=jax.ShapeDtypeStruct((rows, cols), F32),
                          compiler_params=_params())(r)


def adamw(name, w, m, v, parts=None, g=None, layer=0, into=None, order=None):
    _, rows, cols = w.shape
    br = _tile(rows, 256, 16)
    c1 = 1.0 / (1.0 - ADAM_B1 ** ADAM_STEP)
    c2 = 1.0 / (1.0 - ADAM_B2 ** ADAM_STEP)

    def body(g_ref, w_ref, m_ref, v_ref, *rest):
        og_ref, od_ref, om_ref, ov_ref = rest[-4:]
        if parts is None:
            gs = g_ref[...]
        else:
            gs = g_ref[0].astype(F32)
            for s in range(1, N_DEV):
                gs = gs + g_ref[s].astype(F32)
        mn = ADAM_B1 * m_ref[...] + (1.0 - ADAM_B1) * gs
        vn = ADAM_B2 * v_ref[...] + (1.0 - ADAM_B2) * (gs * gs)
        og_ref[...] = gs
        om_ref[...] = mn
        ov_ref[...] = vn
        od_ref[...] = -ADAM_LR * ((mn * c1) / (jnp.sqrt(vn * c2) + ADAM_EPS) + ADAM_WD * w_ref[...])

    blk = pl.BlockSpec((None, br, cols), lambda i: (layer, i, 0))
    if parts is None:
        gspec = pl.BlockSpec((br, cols), lambda i: (i, 0))
    else:
        gspec = pl.BlockSpec((N_DEV, br, cols), lambda i: (0, i, 0))
    earlier = [] if into is None else list(into)
    behind = [] if order is None else [order]
    return pl.pallas_call(
        body, name=name, grid=(rows // br,),
        in_specs=[gspec, blk, blk, blk] + [pl.BlockSpec(memory_space=pl.ANY)] * len(earlier)
                 + [pl.BlockSpec((1, 128), lambda i: (0, 0))] * len(behind),
        out_specs=[blk] * 4, out_shape=[jax.ShapeDtypeStruct(w.shape, F32)] * 4,
        input_output_aliases={4 + k: k for k in range(len(earlier))},
        compiler_params=_params(("parallel",)),
    )(g if parts is None else parts, w, m, v, *earlier, *behind)


SMALL = ("norm_mix", "norm_xattn", "norm_ffn", "norm_mem", "norm_final", "pool_w", "pool_scale",
         "ssm_lam_re", "ssm_lam_im", "ssm_log_dt", "ssm_b_re", "ssm_b_im", "ssm_c_re", "ssm_c_im",
         "ffn_conv_b", "ssm_d", "ffn_conv_w")
SMALL_SHARDED = {"ssm_d": 1, "ffn_conv_w": 2}
BIG = ("ab_w_in", "ab_w_out", "ssm_w_in", "ssm_w_glu", "xa_w_q", "xa_w_kv", "xa_w_o", "ffn_w_up", "ffn_w_down")
WEIGHTS = ("norm_mix", "norm_xattn", "norm_ffn", "norm_mem", "norm_final", "ab_w_in", "pool_w", "pool_scale",
           "ab_w_out", "ssm_w_in", "ssm_lam_re", "ssm_lam_im", "ssm_log_dt", "ssm_b_re", "ssm_b_im", "ssm_c_re",
           "ssm_c_im", "ssm_d", "ssm_w_glu", "xa_w_q", "xa_w_kv", "xa_w_o", "ffn_w_up", "ffn_conv_w", "ffn_conv_b",
           "ffn_w_down")


def _rows8(g):
    return g.reshape(N_DEV, g.size // (N_DEV * D_MODEL), D_MODEL)


def _square(a):
    return a.reshape(D_MODEL, D_MODEL)


_LAYOUT = {"ab_w_out": _square, "ssm_w_in": _square, "xa_w_q": _square, "xa_w_o": _square,
           "ffn_w_down": lambda a: a.reshape(N_DEV // 2, FF_SHARD, D_MODEL)}
GATHER_ORDER = (("ab_w_in", 0), ("ffn_conv_w", None), ("ssm_d", None), ("ab_w_out", 0), ("xa_w_q", 0),
                ("xa_w_kv", 0), ("xa_w_o", 0), ("ffn_w_up", 0), ("ffn_w_down", 0), ("ffn_w_up", 1),
                ("ffn_w_down", 1), ("ssm_w_in", 0), ("ssm_w_glu", 0), ("xa_w_q", 1), ("xa_w_kv", 1), ("xa_w_o", 1))
GATHER_AHEAD = 7
GATHER_BATCHES = (3, 8, 16)


class _Step:
    def __init__(self, master, small):
        self.master, self.small = master, small
        self.pending, self.gathers, self.weights, self.sent, self.queued = [], {}, {}, [], []

    def follow(self, v):
        for z in self.pending:
            v = v + z
        self.pending = []
        return v

    def start_gathers(self, upto, zero):
        upto = min(end for end in GATHER_BATCHES if end >= min(upto, len(GATHER_ORDER)))
        todo = GATHER_ORDER[len(self.gathers):upto]
        if not todo:
            return
        shards = []
        for n, l in todo:
            if l is None:
                shards.append(self.master[n] + zero)
            else:
                shards.append((self.master[n][l] + zero).astype(MXU_DTYPE))
        handles, z = split_start(f"ags_{len(self.gathers)}", shards, gather=True)
        self.gathers.update(zip(todo, handles))
        self.pending.append(z)

    def weight(self, n, l, after):
        if (n, l) not in self.weights:
            full, z = split_wait(f"agw_{n}{'' if l is None else l}", self.gathers[(n, l)], after, gather=True)
            self.weights[(n, l)] = _LAYOUT.get(n, lambda a: a)(full)
            self.start_gathers(GATHER_ORDER.index((n, l)) + 1 + GATHER_AHEAD, z)
        return self.weights[(n, l)]

    def send_grad(self, n, l, part, flush=True):
        self.queued.append((n, l, part))
        if flush:
            handles, z = split_start(f"xs_{n}{l}", [p for _, _, p in self.queued], gather=False)
            self.sent += [(qn, ql, h) for (qn, ql, _), h in zip(self.queued, handles)]
            self.queued = []
            self.pending.append(z)


def _layer_tail(st, l, x_in, hq, mem_n, acts, next_gain=None):
    bsz, seq = acts["bsz"], acts["seq"]
    p = st.small
    q = mm_nn(f"xa_q{l}", hq, st.weight("xa_w_q", l, x_in))
    kv = mm_nn_bs(f"xa_kv{l}", mem_n, st.weight("xa_w_kv", l, x_in))
    o = xattn_fwd(q, kv, bsz, seq)
    x_mid, hf = mm_nn(f"xa_o{l}", o, st.weight("xa_w_o", l, o), res=x_in, out_dtype=F32,
                      norm_gain=st.follow(p["norm_ffn"][l]))
    up = mm_nn_bs(f"ffn_up{l}", hf, st.weight("ffn_w_up", l, x_mid), stacked_out=True)
    conv_w = st.weight("ffn_conv_w", None, x_mid)[:, l]
    act = conv_fwd(up, conv_w, p["ffn_conv_b"][l], bsz, seq)
    w_down = st.weight("ffn_w_down", l, act)
    if next_gain is None:
        x_out, h_next = mm_as_nn(f"ffn_down{l}", act, w_down, res=x_mid), None
    else:
        x_out, h_next = mm_as_nn(f"ffn_down{l}", act, w_down, res=x_mid, norm_gain=st.follow(next_gain))
    acts[l].update(x_in=x_in, hq=hq, q=q, kv=kv, o=o, x_mid=x_mid, hf=hf, up=up, act=act)
    return x_out, h_next


def _layer_tail_bwd(st, l, dx, mem_n, acts, grads):
    a = acts[l]
    bsz, seq = acts["bsz"], acts["seq"]
    p = st.small
    dact = mm_nt_os(f"d_act{l}", dx, st.weight("ffn_w_down", l, dx))
    st.send_grad("ffn_w_down", l, _rows8(mm_tn(f"g_ffn_down{l}", a["act"], dx, a_stacked=True)), flush=False)
    conv_w = st.weight("ffn_conv_w", None, dx)[:, l]
    dconv, dcw, dcb = conv_bwd_taps(a["up"], conv_w, p["ffn_conv_b"][l], dact, bsz, seq)
    grads["ffn_conv_w"][l] = dcw
    grads["ffn_conv_b"][l] = dcb
    dup = conv_bwd_input(dconv, conv_w, bsz, seq)
    dx_mid, grads["norm_ffn"][l] = mm_nt_bs(f"d_hf{l}", dup, st.weight("ffn_w_up", l, dx), dc_stacked=True,
                                            rms=(a["x_mid"], st.follow(p["norm_ffn"][l]), dx))
    st.send_grad("ffn_w_up", l, mm_tn(f"g_ffn_up{l}", a["hf"], dup, dc_stacked=True))
    do = mm_nt(f"d_o{l}", dx_mid, st.weight("xa_w_o", l, dx))
    st.send_grad("xa_w_o", l, _rows8(mm_tn(f"g_xa_o{l}", a["o"], dx_mid)), flush=False)
    dq, dk, dv = xattn_bwd(a["q"], a["kv"], do, bsz, seq)
    dkv = jnp.concatenate([dk, dv], axis=1).astype(BF16)
    dx_in, grads["norm_xattn"][l] = mm_nt(f"d_hq{l}", dq, st.weight("xa_w_q", l, dx),
                                          rms=(a["x_in"], st.follow(p["norm_xattn"][l]), dx_mid))
    st.send_grad("xa_w_q", l, _rows8(mm_tn(f"g_xa_q{l}", a["hq"], dq)), flush=False)
    dmem_n = mm_nt_bs(f"d_memn{l}", dkv, st.weight("xa_w_kv", l, dx), out_dtype=F32)
    st.send_grad("xa_w_kv", l, mm_tn(f"g_xa_kv{l}", mem_n, dkv, dc_cols=2 * D_MODEL // N_DEV))
    return dx_in, dmem_n


def kernel(x, mem, norm_mix, norm_xattn, norm_ffn, norm_mem, norm_final, ab_w_in, pool_w, pool_scale, ab_w_out, ssm_w_in, ssm_lam_re, ssm_lam_im, ssm_log_dt, ssm_b_re, ssm_b_im, ssm_c_re, ssm_c_im, ssm_d, ssm_w_glu, xa_w_q, xa_w_kv, xa_w_o, ffn_w_up, ffn_conv_w, ffn_conv_b, ffn_w_down, loss_target, m_norm_mix, m_norm_xattn, m_norm_ffn, m_norm_mem, m_norm_final, m_ab_w_in, m_pool_w, m_pool_scale, m_ab_w_out, m_ssm_w_in, m_ssm_lam_re, m_ssm_lam_im, m_ssm_log_dt, m_ssm_b_re, m_ssm_b_im, m_ssm_c_re, m_ssm_c_im, m_ssm_d, m_ssm_w_glu, m_xa_w_q, m_xa_w_kv, m_xa_w_o, m_ffn_w_up, m_ffn_conv_w, m_ffn_conv_b, m_ffn_w_down, v_norm_mix, v_norm_xattn, v_norm_ffn, v_norm_mem, v_norm_final, v_ab_w_in, v_pool_w, v_pool_scale, v_ab_w_out, v_ssm_w_in, v_ssm_lam_re, v_ssm_lam_im, v_ssm_log_dt, v_ssm_b_re, v_ssm_b_im, v_ssm_c_re, v_ssm_c_im, v_ssm_d, v_ssm_w_glu, v_xa_w_q, v_xa_w_kv, v_xa_w_o, v_ffn_w_up, v_ffn_conv_w, v_ffn_conv_b, v_ffn_w_down):
    given = dict(locals())
    master = {n: given[n] for n in WEIGHTS}
    mom1 = {n: given["m_" + n] for n in WEIGHTS}
    mom2 = {n: given["v_" + n] for n in WEIGHTS}
    bsz, seq, d = x.shape
    t = bsz * seq
    me = _my_index()

    st = _Step(master, {"norm_xattn": norm_xattn, "norm_ffn": norm_ffn,
                        "ffn_conv_b": [ffn_conv_b[l].reshape(N_DEV, 1, FF_SHARD) for l in range(2)]})
    st.start_gathers(1, 0.0)
    zero = st.follow(jnp.zeros((), F32))

    acts = {"bsz": bsz, "seq": seq, 0: {}, 1: {}}
    x0 = x.reshape(t, d)
    mem2 = mem.reshape(bsz * MEM_LEN, d)
    mem_n = rms_fwd("rms_mem", mem2, norm_mem + zero)
    pscale = pool_scale.reshape(1, SB_WIDTH)

    h0 = rms_fwd("rms_mix0", x0, norm_mix[0] + zero)
    w_in = st.weight("ab_w_in", 0, h0)
    proj = mm_nn_bs("ab_in", h0, w_in, out_dtype=F32)
    a_out, rsum = sb_attn_fwd(proj, st.follow(jnp.zeros((1, 128), F32)), bsz, seq)
    p_out = pool_fwd(proj, pool_w[0], pscale, bsz, seq)
    w_out = st.weight("ab_w_out", 0, a_out)
    x1 = mm_nn("ab_out_a", a_out, w_out, res=x0, out_dtype=F32)
    x1, hq0 = mm_nn("ab_out_p", p_out, w_out, res=x1, koff=SB_WIDTH, out_dtype=F32,
                    norm_gain=st.follow(norm_xattn[0]))
    x3, h1 = _layer_tail(st, 0, x1, hq0, mem_n, acts, next_gain=norm_mix[1])

    b_re2 = ssm_b_re.reshape(64, 1024)
    b_im2 = ssm_b_im.reshape(64, 1024)
    log_dt = ssm_log_dt.reshape(64, 1)
    lb_re, lb_im, bb_re2, bb_im2 = ssm_prep(ssm_lam_re[0], ssm_lam_im[0], log_dt, b_re2, b_im2)
    wt = _ssm_in_weights(bb_re2, bb_im2)
    ct = _ssm_out_weights(ssm_c_re[0], ssm_c_im[0])
    a_re = lb_re.reshape(1, SSM_STATES)
    a_im = lb_im.reshape(1, SSM_STATES)
    u = mm_nn("ssm_in", h1, st.weight("ssm_w_in", 0, x3), out_dtype=F32)
    dskip = st.weight("ssm_d", None, x3).reshape(1, D_MODEL)
    y, gl, h_re, h_im = ssm_fwd(u, wt, ct, a_re, a_im, dskip, bsz, seq)
    glu = mm_nn_bs("ssm_glu", gl, st.weight("ssm_w_glu", 0, gl), out_dtype=F32)
    x4, hq1 = glu_fwd(glu, x3, st.follow(norm_xattn[1]))
    x6, _ = _layer_tail(st, 1, x4, hq1, mem_n, acts)

    loss_row, dx, g_norm_final = loss_head(x6, norm_final, loss_target.reshape(t, d))
    loss = lax.psum(loss_row[0, 0], MESH_AXES)

    grads = {n: [None, None] for n in ("ffn_conv_w", "ffn_conv_b", "norm_ffn", "norm_xattn", "norm_mix")}
    dx4, dmem_1 = _layer_tail_bwd(st, 1, dx, mem_n, acts, grads)
    dglu = glu_bwd(glu, dx4)
    dgl = mm_nt_bs("d_gl", dglu, st.weight("ssm_w_glu", 0, dx))
    st.send_grad("ssm_w_glu", 0, mm_tn("g_ssm_glu", gl, dglu, dc_cols=2 * D_MODEL // N_DEV), flush=False)
    du, dwt, dct, g_dskip, da_re, da_im = ssm_bwd(dgl, y, u, h_re, h_im, wt, ct, a_re, a_im, dskip, bsz, seq)
    dbb_re, dbb_im = _ssm_in_weights_bwd(dwt)
    g_c_re, g_c_im = _ssm_out_weights_bwd(dct)
    g_lam_re, g_lam_im, g_log_dt, g_b_re, g_b_im = ssm_prep_bwd(
        ssm_lam_re[0], ssm_lam_im[0], log_dt, b_re2, b_im2, da_re.reshape(64, 64), da_im.reshape(64, 64),
        dbb_re, dbb_im)
    dx3, grads["norm_mix"][1] = mm_nt("d_h1", du, st.weight("ssm_w_in", 0, dx),
                                      rms=(x3, st.follow(norm_mix[1]), dx4))
    st.send_grad("ssm_w_in", 0, _rows8(mm_tn("g_ssm_in", h1, du)))

    dx1, dmem_0 = _layer_tail_bwd(st, 0, dx3, mem_n, acts, grads)
    dcat = mm_nt("d_cat", dx1, st.weight("ab_w_out", 0, dx))
    st.send_grad("ab_w_out", 0, _rows8(jnp.concatenate(
        [mm_tn("g_ab_out_a", a_out, dx1), mm_tn("g_ab_out_p", p_out, dx1)], axis=0)), flush=False)
    dq, dk, dv = sb_attn_bwd(proj, rsum, dcat, bsz, seq)
    dpu, g_pool_w, g_pool_scale = pool_bwd(proj, pool_w[0], st.follow(pscale), dcat, bsz, seq)
    dproj = jnp.concatenate([dq, dk, dv, dpu], axis=1).astype(BF16)
    st.send_grad("ab_w_in", 0, mm_tn("g_ab_in", h0, dproj, dc_cols=2 * D_MODEL // N_DEV))
    dx0, grads["norm_mix"][0] = mm_nt_bs("d_h0", dproj, st.weight("ab_w_in", 0, dx),
                                         rms=(x0, st.follow(norm_mix[0]), dx1))
    _, g_norm_mem = rms_bwd("rms_mem_bwd", mem2, norm_mem, dmem_0 + dmem_1, need_dx=False)

    small_g = {
        "norm_mix": jnp.stack([g[0] for g in grads["norm_mix"]]),
        "norm_xattn": jnp.stack([g[0] for g in grads["norm_xattn"]]),
        "norm_ffn": jnp.stack([g[0] for g in grads["norm_ffn"]]),
        "norm_mem": g_norm_mem[0], "norm_final": g_norm_final[0],
        "pool_w": g_pool_w[None], "pool_scale": g_pool_scale,
        "ssm_lam_re": g_lam_re[None], "ssm_lam_im": g_lam_im[None], "ssm_log_dt": g_log_dt.reshape(1, 64),
        "ssm_b_re": g_b_re.reshape(1, 64, 64, 16), "ssm_b_im": g_b_im.reshape(1, 64, 64, 16),
        "ssm_c_re": g_c_re[None], "ssm_c_im": g_c_im[None],
        "ffn_conv_b": jnp.stack([g.reshape(2 * D_FF) for g in grads["ffn_conv_b"]]),
        "ssm_d": g_dskip,
        "ffn_conv_w": jnp.stack([g.transpose(1, 0, 2).reshape(3, 2 * D_FF) for g in grads["ffn_conv_w"]]),
    }
    sizes = [int(small_g[n].size) for n in SMALL]
    total = sum(sizes)
    rows8 = -(-total // (N_DEV * 128 * 8)) * 8
    flat = jnp.concatenate([small_g[n].reshape(-1).astype(F32) for n in SMALL]
                           + [jnp.zeros((N_DEV * rows8 * 128 - total,), F32)])
    (in_flight,), z = split_start("xs_small", [flat.reshape(N_DEV, rows8, 128)], gather=False)
    st.pending.append(z)
    stepped, last = {}, dx0
    for i, (n, l, handles) in enumerate(st.sent):
        if i == len(st.sent) // 2:
            recv, _ = split_wait("xw_small", in_flight, last, gather=False)
            (in_flight,), z = split_start("ags_small", [sum_parts("sum_small", recv)], gather=True)
            st.pending.append(z)
        recv, _ = split_wait(f"xw_{n}{l}", handles, dx0, gather=False)
        shape3 = (master[n].shape[0],) + recv.shape[1:]
        stepped[n] = adamw(f"adamw_{n}{l}", master[n].reshape(shape3), mom1[n].reshape(shape3),
                           mom2[n].reshape(shape3), parts=recv, layer=l, into=stepped.get(n),
                           order=st.follow(jnp.zeros((1, 128), F32)))
        last = stepped[n][0]
    out_g, out_d, out_m, out_v = ({n: stepped[n][k].reshape(master[n].shape) for n in BIG} for k in range(4))
    summed = split_wait("agw_small", in_flight, last, gather=True)[0].reshape(-1)

    def local_part(name, a):
        ax = SMALL_SHARDED.get(name)
        if ax is None:
            return a
        n_loc = a.shape[ax] // N_DEV
        return lax.dynamic_slice_in_dim(a, me * n_loc, n_loc, axis=ax)

    off = 0
    for n, sz in zip(SMALL, sizes):
        g_n = local_part(n, summed[off:off + sz].reshape(small_g[n].shape))
        off += sz
        cols = g_n.shape[-1] if g_n.shape[-1] >= 128 or g_n.ndim < 3 else g_n.shape[-1] * g_n.shape[-2]
        shape3 = (1, g_n.size // cols, cols)
        res = adamw("adamw_" + n, master[n].reshape(shape3), mom1[n].reshape(shape3), mom2[n].reshape(shape3),
                    g=g_n.reshape(shape3[1:]))
        for dst, r in zip((out_g, out_d, out_m, out_v), res):
            dst[n] = r.reshape(master[n].shape)

    return (loss, dx0.reshape(bsz, seq, d), *[out_g[n] for n in WEIGHTS], *[out_d[n] for n in WEIGHTS],
            *[out_m[n] for n in WEIGHTS], *[out_v[n] for n in WEIGHTS])
```
